```python
import jax, jax.numpy as jnp
from jax import lax
import numpy as np

D_MODEL = 1024
BATCH = 8
SEQ = 2048
DEPTH = 1

MLA_HEADS = 8
MLA_NOPE = 64
MLA_ROPE = 32
MLA_V = 64
MLA_Q_LORA = 512
MLA_KV_LORA = 256
DIL_HEADS = 8
DIL_HEAD_DIM = 64
DIL_PATTERNS = ((128, 1), (512, 4), (2048, 16))
DIL_WIDTH = DIL_HEADS * DIL_HEAD_DIM
MIX_WIDTH = MLA_HEADS * MLA_V + DIL_WIDTH
IN_COLS = MLA_Q_LORA + MLA_KV_LORA + MLA_ROPE + 3 * DIL_WIDTH
SPLITS = (MLA_Q_LORA,
          MLA_Q_LORA + MLA_KV_LORA,
          MLA_Q_LORA + MLA_KV_LORA + MLA_ROPE,
          MLA_Q_LORA + MLA_KV_LORA + MLA_ROPE + DIL_WIDTH,
          MLA_Q_LORA + MLA_KV_LORA + MLA_ROPE + 2 * DIL_WIDTH)
D_FF = 2816
CONV_WIDTH = 3
ROPE_THETA = 10000.0
EPS = 1e-6
Q_BLOCK = 128
NEG_INF = -1e30

kernel_name = "hybrid_mla_dilated_convffn_adaln"


def rms_norm(x, g):
    xf = x.astype(jnp.float32)
    y = xf * lax.rsqrt(jnp.mean(xf * xf, axis=-1, keepdims=True) + EPS)
    return (y * g.astype(jnp.float32)).astype(x.dtype)


def rope(x, positions):
    d = x.shape[-1]
    half = d // 2
    inv_freq = jnp.power(ROPE_THETA, -2.0 * jnp.arange(half, dtype=jnp.float32) / d)
    ang = positions.astype(jnp.float32)[:, :, None, None] * inv_freq
    cos, sin = jnp.cos(ang), jnp.sin(ang)
    xf = x.astype(jnp.float32)
    x1, x2 = xf[..., :half], xf[..., half:]
    return jnp.concatenate([x1 * cos - x2 * sin, x1 * sin + x2 * cos], axis=-1).astype(x.dtype)


def causal_dense_attention(q, k, v, scale):
    B, S, H, Dk = q.shape
    Dv = v.shape[-1]
    nb = S // Q_BLOCK
    qb = q.reshape(B, nb, Q_BLOCK, H, Dk).transpose(1, 0, 2, 3, 4)
    kpos = jnp.arange(S)

    def one_block(args):
        qi, i = args
        s = jnp.einsum('bqhd,bkhd->bhqk', qi, k).astype(jnp.float32) * scale
        qpos = i * Q_BLOCK + jnp.arange(Q_BLOCK)
        mask = kpos[None, :] <= qpos[:, None]
        s = jnp.where(mask, s, NEG_INF)
        p = jax.nn.softmax(s, axis=-1).astype(v.dtype)
        return jnp.einsum('bhqk,bkhd->bqhd', p, v)

    out = lax.map(one_block, (qb, jnp.arange(nb)))
    return out.transpose(1, 0, 2, 3, 4).reshape(B, S, H, Dv)


def banded_causal_attention(q, k, v, span):
    N, L, H, D = q.shape
    blk = span
    nb = -(-L // blk)
    Lp = nb * blk
    pad = ((0, 0), (0, Lp - L), (0, 0), (0, 0))
    q, k, v = jnp.pad(q, pad), jnp.pad(k, pad), jnp.pad(v, pad)
    qb = q.reshape(N, nb, blk, H, D)

    def two_blocks(t):
        tb = jnp.pad(t, ((0, 0), (blk, 0), (0, 0), (0, 0))).reshape(N, nb + 1, blk, H, D)
        return jnp.concatenate([tb[:, :-1], tb[:, 1:]], axis=2)

    kb, vb = two_blocks(k), two_blocks(v)
    s = jnp.einsum('nbqhd,nbkhd->nbhqk', qb, kb).astype(jnp.float32) * (D ** -0.5)
    blk_idx = jnp.arange(nb)[:, None, None]
    qry_pos = blk_idx * blk + jnp.arange(blk)[None, :, None]
    key_pos = (blk_idx - 1) * blk + jnp.arange(2 * blk)[None, None, :]
    dist = qry_pos - key_pos
    mask = (dist >= 0) & (dist <= span) & (key_pos >= 0)
    s = jnp.where(mask[None, :, None], s, NEG_INF)
    m = jnp.max(s, axis=-1, keepdims=True)
    e = jnp.exp(s - m)
    denom = jnp.sum(e, axis=-1, keepdims=True)
    p = (e / denom).astype(v.dtype)
    o = jnp.einsum('nbhqk,nbkhd->nbqhd', p, vb).reshape(N, Lp, H, D)[:, :L]
    lse = (m + jnp.log(denom))[..., 0]
    lse = lse.transpose(0, 1, 3, 2).reshape(N, Lp, H)[:, :L]
    return o, lse


def to_strided(t, dil):
    B, S, H, D = t.shape
    return t.reshape(B, S // dil, dil, H, D).transpose(0, 2, 1, 3, 4).reshape(B * dil, S // dil, H, D)


def dilated_attention(q, k, v):
    B, S, H, D = q.shape
    outs, lses = [], []
    for window, dil in DIL_PATTERNS:
        L = S // dil
        o, lse = banded_causal_attention(to_strided(q, dil), to_strided(k, dil),
                                         to_strided(v, dil), window // dil)
        outs.append(o.reshape(B, dil, L, H, D).transpose(0, 2, 1, 3, 4).reshape(B, S, H, D))
        lses.append(lse.reshape(B, dil, L, H).transpose(0, 2, 1, 3).reshape(B, S, H))
    w = jax.nn.softmax(jnp.stack(lses, axis=0), axis=0)
    out = jnp.sum(w[..., None] * jnp.stack(outs, axis=0).astype(jnp.float32), axis=0)
    return out.astype(q.dtype)


def causal_depthwise_conv(u, w, b):
    K = w.shape[0]
    S = u.shape[1]
    up = jnp.pad(u, ((0, 0), (K - 1, 0), (0, 0)))
    y = b
    for kk in range(K):
        y = y + up[:, kk:kk + S] * w[kk]
    return y


def _fwd_setup_inputs(seed: int = 0) -> dict:
    key = jax.random.key(seed)
    ks = jax.random.split(key, 24)
    nrm = jax.random.normal
    L = DEPTH

    def gain(k, n):
        return 1.0 + 0.05 * nrm(k, (L, n), jnp.float32)

    x = nrm(ks[0], (BATCH, SEQ, D_MODEL), jnp.float32)
    c = nrm(ks[1], (BATCH, D_MODEL), jnp.float32)
    positions = (jnp.arange(SEQ, dtype=jnp.int32)[None, :]
                 + jax.random.randint(ks[2], (BATCH, 1), 0, 4096, dtype=jnp.int32))
    return {
        "x": x,
        "c": c,
        "positions": positions,
        "w_ada": nrm(ks[3], (L, D_MODEL, 6 * D_MODEL), jnp.float32) * (0.5 * D_MODEL ** -0.5),
        "b_ada": 0.02 * nrm(ks[4], (L, 6 * D_MODEL), jnp.float32),
        "g_mix_norm": gain(ks[5], D_MODEL),
        "w_in": nrm(ks[6], (L, D_MODEL, IN_COLS), jnp.float32) * D_MODEL ** -0.5,
        "g_q_lat": gain(ks[7], MLA_Q_LORA),
        "w_q_b": nrm(ks[8], (L, MLA_Q_LORA, MLA_HEADS * (MLA_NOPE + MLA_ROPE)), jnp.float32) * MLA_Q_LORA ** -0.5,
        "g_kv_lat": gain(ks[9], MLA_KV_LORA),
        "w_kv_b": nrm(ks[10], (L, MLA_KV_LORA, MLA_HEADS * (MLA_NOPE + MLA_V)), jnp.float32) * MLA_KV_LORA ** -0.5,
        "g_mla_q_nope": gain(ks[11], MLA_NOPE),
        "g_mla_q_pe": gain(ks[12], MLA_ROPE),
        "g_mla_k_nope": gain(ks[13], MLA_NOPE),
        "g_mla_k_pe": gain(ks[14], MLA_ROPE),
        "g_dil_q": gain(ks[15], DIL_HEAD_DIM),
        "g_dil_k": gain(ks[16], DIL_HEAD_DIM),
        "w_o": nrm(ks[17], (L, MIX_WIDTH, D_MODEL), jnp.float32) * MIX_WIDTH ** -0.5,
        "g_ffn_norm": gain(ks[18], D_MODEL),
        "w_up": nrm(ks[19], (L, D_MODEL, 2 * D_FF), jnp.float32) * D_MODEL ** -0.5,
        "w_conv": nrm(ks[20], (L, CONV_WIDTH, 2 * D_FF), jnp.float32) * CONV_WIDTH ** -0.5,
        "b_conv": 0.02 * nrm(ks[21], (L, 2 * D_FF), jnp.float32),
        "w_down": nrm(ks[22], (L, D_FF, D_MODEL), jnp.float32) * D_FF ** -0.5,
    }


def _fwd_reference(x, c, positions, w_ada, b_ada, g_mix_norm, w_in, g_q_lat, w_q_b, g_kv_lat, w_kv_b,
              g_mla_q_nope, g_mla_q_pe, g_mla_k_nope, g_mla_k_pe, g_dil_q, g_dil_k, w_o,
              g_ffn_norm, w_up, w_conv, b_conv, w_down):
    B, S, _ = x.shape
    for l in range(DEPTH):
        mod = jax.nn.silu(c) @ w_ada[l] + b_ada[l]
        sh1, sc1, g1, sh2, sc2, g2 = jnp.split(mod, 6, axis=-1)

        h = rms_norm(x, g_mix_norm[l]) * (1.0 + sc1[:, None]) + sh1[:, None]
        proj = h @ w_in[l]
        q_lat, kv_lat, k_pe, qd, kd, vd = jnp.split(proj, SPLITS, axis=-1)

        q = (rms_norm(q_lat, g_q_lat[l]) @ w_q_b[l]).reshape(B, S, MLA_HEADS, MLA_NOPE + MLA_ROPE)
        q_nope = rms_norm(q[..., :MLA_NOPE], g_mla_q_nope[l])
        q_pe = rope(rms_norm(q[..., MLA_NOPE:], g_mla_q_pe[l]), positions)
        kv = (rms_norm(kv_lat, g_kv_lat[l]) @ w_kv_b[l]).reshape(B, S, MLA_HEADS, MLA_NOPE + MLA_V)
        k_nope = rms_norm(kv[..., :MLA_NOPE], g_mla_k_nope[l])
        v_mla = kv[..., MLA_NOPE:]
        k_pe = rope(rms_norm(k_pe, g_mla_k_pe[l])[:, :, None, :], positions)
        k_mla = jnp.concatenate([k_nope, jnp.broadcast_to(k_pe, (B, S, MLA_HEADS, MLA_ROPE))], axis=-1)
        q_mla = jnp.concatenate([q_nope, q_pe], axis=-1)
        o_mla = causal_dense_attention(q_mla, k_mla, v_mla, (MLA_NOPE + MLA_ROPE) ** -0.5)

        qd = rope(rms_norm(qd.reshape(B, S, DIL_HEADS, DIL_HEAD_DIM), g_dil_q[l]), positions)
        kd = rope(rms_norm(kd.reshape(B, S, DIL_HEADS, DIL_HEAD_DIM), g_dil_k[l]), positions)
        vd = vd.reshape(B, S, DIL_HEADS, DIL_HEAD_DIM)
        o_dil = dilated_attention(qd, kd, vd)

        mix = jnp.concatenate([o_mla.reshape(B, S, MLA_HEADS * MLA_V),
                               o_dil.reshape(B, S, DIL_WIDTH)], axis=-1) @ w_o[l]
        x = x + g1[:, None] * mix

        h2 = rms_norm(x, g_ffn_norm[l]) * (1.0 + sc2[:, None]) + sh2[:, None]
        u = causal_depthwise_conv(h2 @ w_up[l], w_conv[l], b_conv[l])
        gate, val = jnp.split(u, 2, axis=-1)
        x = x + g2[:, None] * ((jax.nn.silu(gate) * val) @ w_down[l])
    return x


import jax as _jax
import jax.numpy as _jnp

TWIN_FORMAT = 'train_step'
FWD_PARAMS = ['x', 'c', 'positions', 'w_ada', 'b_ada', 'g_mix_norm', 'w_in', 'g_q_lat', 'w_q_b', 'g_kv_lat', 'w_kv_b', 'g_mla_q_nope', 'g_mla_q_pe', 'g_mla_k_nope', 'g_mla_k_pe', 'g_dil_q', 'g_dil_k', 'w_o', 'g_ffn_norm', 'w_up', 'w_conv', 'b_conv', 'w_down']
TWIN_WEIGHTS = ['w_ada', 'b_ada', 'g_mix_norm', 'w_in', 'g_q_lat', 'w_q_b', 'g_kv_lat', 'w_kv_b', 'g_mla_q_nope', 'g_mla_q_pe', 'g_mla_k_nope', 'g_mla_k_pe', 'g_dil_q', 'g_dil_k', 'w_o', 'g_ffn_norm', 'w_up', 'w_conv', 'b_conv', 'w_down']
TWIN_DIFF_INPUT = 'x'
TWIN_INPUTS = ['x', 'c', 'positions', 'w_ada', 'b_ada', 'g_mix_norm', 'w_in', 'g_q_lat', 'w_q_b', 'g_kv_lat', 'w_kv_b', 'g_mla_q_nope', 'g_mla_q_pe', 'g_mla_k_nope', 'g_mla_k_pe', 'g_dil_q', 'g_dil_k', 'w_o', 'g_ffn_norm', 'w_up', 'w_conv', 'b_conv', 'w_down', 'loss_target', 'm_w_ada', 'm_b_ada', 'm_g_mix_norm', 'm_w_in', 'm_g_q_lat', 'm_w_q_b', 'm_g_kv_lat', 'm_w_kv_b', 'm_g_mla_q_nope', 'm_g_mla_q_pe', 'm_g_mla_k_nope', 'm_g_mla_k_pe', 'm_g_dil_q', 'm_g_dil_k', 'm_w_o', 'm_g_ffn_norm', 'm_w_up', 'm_w_conv', 'm_b_conv', 'm_w_down', 'v_w_ada', 'v_b_ada', 'v_g_mix_norm', 'v_w_in', 'v_g_q_lat', 'v_w_q_b', 'v_g_kv_lat', 'v_w_kv_b', 'v_g_mla_q_nope', 'v_g_mla_q_pe', 'v_g_mla_k_nope', 'v_g_mla_k_pe', 'v_g_dil_q', 'v_g_dil_k', 'v_w_o', 'v_g_ffn_norm', 'v_w_up', 'v_w_conv', 'v_b_conv', 'v_w_down']
TWIN_OUTPUTS = ['loss', 'grad_x', 'grad_w_ada', 'grad_b_ada', 'grad_g_mix_norm', 'grad_w_in', 'grad_g_q_lat', 'grad_w_q_b', 'grad_g_kv_lat', 'grad_w_kv_b', 'grad_g_mla_q_nope', 'grad_g_mla_q_pe', 'grad_g_mla_k_nope', 'grad_g_mla_k_pe', 'grad_g_dil_q', 'grad_g_dil_k', 'grad_w_o', 'grad_g_ffn_norm', 'grad_w_up', 'grad_w_conv', 'grad_b_conv', 'grad_w_down', 'delta_w_ada', 'delta_b_ada', 'delta_g_mix_norm', 'delta_w_in', 'delta_g_q_lat', 'delta_w_q_b', 'delta_g_kv_lat', 'delta_w_kv_b', 'delta_g_mla_q_nope', 'delta_g_mla_q_pe', 'delta_g_mla_k_nope', 'delta_g_mla_k_pe', 'delta_g_dil_q', 'delta_g_dil_k', 'delta_w_o', 'delta_g_ffn_norm', 'delta_w_up', 'delta_w_conv', 'delta_b_conv', 'delta_w_down', 'new_m_w_ada', 'new_m_b_ada', 'new_m_g_mix_norm', 'new_m_w_in', 'new_m_g_q_lat', 'new_m_w_q_b', 'new_m_g_kv_lat', 'new_m_w_kv_b', 'new_m_g_mla_q_nope', 'new_m_g_mla_q_pe', 'new_m_g_mla_k_nope', 'new_m_g_mla_k_pe', 'new_m_g_dil_q', 'new_m_g_dil_k', 'new_m_w_o', 'new_m_g_ffn_norm', 'new_m_w_up', 'new_m_w_conv', 'new_m_b_conv', 'new_m_w_down', 'new_v_w_ada', 'new_v_b_ada', 'new_v_g_mix_norm', 'new_v_w_in', 'new_v_g_q_lat', 'new_v_w_q_b', 'new_v_g_kv_lat', 'new_v_w_kv_b', 'new_v_g_mla_q_nope', 'new_v_g_mla_q_pe', 'new_v_g_mla_k_nope', 'new_v_g_mla_k_pe', 'new_v_g_dil_q', 'new_v_g_dil_k', 'new_v_w_o', 'new_v_g_ffn_norm', 'new_v_w_up', 'new_v_w_conv', 'new_v_b_conv', 'new_v_w_down']
TWIN_LEAF_KINDS = {'loss': 'loss', 'grad_x': 'grad_x', 'grad_w_ada': 'grad_w', 'grad_b_ada': 'grad_w', 'grad_g_mix_norm': 'grad_w', 'grad_w_in': 'grad_w', 'grad_g_q_lat': 'grad_w', 'grad_w_q_b': 'grad_w', 'grad_g_kv_lat': 'grad_w', 'grad_w_kv_b': 'grad_w', 'grad_g_mla_q_nope': 'grad_w', 'grad_g_mla_q_pe': 'grad_w', 'grad_g_mla_k_nope': 'grad_w', 'grad_g_mla_k_pe': 'grad_w', 'grad_g_dil_q': 'grad_w', 'grad_g_dil_k': 'grad_w', 'grad_w_o': 'grad_w', 'grad_g_ffn_norm': 'grad_w', 'grad_w_up': 'grad_w', 'grad_w_conv': 'grad_w', 'grad_b_conv': 'grad_w', 'grad_w_down': 'grad_w', 'delta_w_ada': 'delta_w', 'delta_b_ada': 'delta_w', 'delta_g_mix_norm': 'delta_w', 'delta_w_in': 'delta_w', 'delta_g_q_lat': 'delta_w', 'delta_w_q_b': 'delta_w', 'delta_g_kv_lat': 'delta_w', 'delta_w_kv_b': 'delta_w', 'delta_g_mla_q_nope': 'delta_w', 'delta_g_mla_q_pe': 'delta_w', 'delta_g_mla_k_nope': 'delta_w', 'delta_g_mla_k_pe': 'delta_w', 'delta_g_dil_q': 'delta_w', 'delta_g_dil_k': 'delta_w', 'delta_w_o': 'delta_w', 'delta_g_ffn_norm': 'delta_w', 'delta_w_up': 'delta_w', 'delta_w_conv': 'delta_w', 'delta_b_conv': 'delta_w', 'delta_w_down': 'delta_w', 'new_m_w_ada': 'new_m', 'new_m_b_ada': 'new_m', 'new_m_g_mix_norm': 'new_m', 'new_m_w_in': 'new_m', 'new_m_g_q_lat': 'new_m', 'new_m_w_q_b': 'new_m', 'new_m_g_kv_lat': 'new_m', 'new_m_w_kv_b': 'new_m', 'new_m_g_mla_q_nope': 'new_m', 'new_m_g_mla_q_pe': 'new_m', 'new_m_g_mla_k_nope': 'new_m', 'new_m_g_mla_k_pe': 'new_m', 'new_m_g_dil_q': 'new_m', 'new_m_g_dil_k': 'new_m', 'new_m_w_o': 'new_m', 'new_m_g_ffn_norm': 'new_m', 'new_m_w_up': 'new_m', 'new_m_w_conv': 'new_m', 'new_m_b_conv': 'new_m', 'new_m_w_down': 'new_m', 'new_v_w_ada': 'new_v', 'new_v_b_ada': 'new_v', 'new_v_g_mix_norm': 'new_v', 'new_v_w_in': 'new_v', 'new_v_g_q_lat': 'new_v', 'new_v_w_q_b': 'new_v', 'new_v_g_kv_lat': 'new_v', 'new_v_w_kv_b': 'new_v', 'new_v_g_mla_q_nope': 'new_v', 'new_v_g_mla_q_pe': 'new_v', 'new_v_g_mla_k_nope': 'new_v', 'new_v_g_mla_k_pe': 'new_v', 'new_v_g_dil_q': 'new_v', 'new_v_g_dil_k': 'new_v', 'new_v_w_o': 'new_v', 'new_v_g_ffn_norm': 'new_v', 'new_v_w_up': 'new_v', 'new_v_w_conv': 'new_v', 'new_v_b_conv': 'new_v', 'new_v_w_down': 'new_v'}


def _forward(args):
    return _fwd_reference(*[args[k] for k in FWD_PARAMS])


def _output_shape():
    out = _jax.eval_shape(lambda: _forward(_fwd_setup_inputs(0)))
    return out.shape, out.dtype

N_MICROBATCH = 1
ADAM_LR = 0.001
ADAM_B1 = 0.9
ADAM_B2 = 0.999
ADAM_EPS = 1e-08
ADAM_WD = 0.01
ADAM_STEP = 10
PER_EXAMPLE_BATCH_AXIS = {'x': 0, 'c': 0, 'positions': 0, 'loss_target': 0}
SHARED_INPUTS = []
_WEIGHT_DTYPES = {'w_ada': _jnp.float32, 'b_ada': _jnp.float32, 'g_mix_norm': _jnp.float32, 'w_in': _jnp.float32, 'g_q_lat': _jnp.float32, 'w_q_b': _jnp.float32, 'g_kv_lat': _jnp.float32, 'w_kv_b': _jnp.float32, 'g_mla_q_nope': _jnp.float32, 'g_mla_q_pe': _jnp.float32, 'g_mla_k_nope': _jnp.float32, 'g_mla_k_pe': _jnp.float32, 'g_dil_q': _jnp.float32, 'g_dil_k': _jnp.float32, 'w_o': _jnp.float32, 'g_ffn_norm': _jnp.float32, 'w_up': _jnp.float32, 'w_conv': _jnp.float32, 'b_conv': _jnp.float32, 'w_down': _jnp.float32}
MOMENT_SCALE = {'w_ada': 4.830523e-01, 'b_ada': 1.037903e+00, 'g_mix_norm': 2.874699e-02, 'w_in': 8.977231e-02, 'g_q_lat': 1.050115e-02, 'w_q_b': 8.729583e-03, 'g_kv_lat': 3.638748e-01, 'w_kv_b': 9.285677e-02, 'g_mla_q_nope': 5.496283e-02, 'g_mla_q_pe': 4.802929e-02, 'g_mla_k_nope': 5.371666e-02, 'g_mla_k_pe': 4.704255e-02, 'g_dil_q': 1.088611e-01, 'g_dil_k': 1.091401e-01, 'w_o': 1.231715e-01, 'g_ffn_norm': 1.788759e+00, 'w_up': 7.202317e-02, 'w_conv': 2.797798e-01, 'b_conv': 2.275088e-01, 'w_down': 5.787006e-02}


def _to_microbatches(a, axis):
    t = _jnp.moveaxis(a, axis, 0)
    t = t.reshape((N_MICROBATCH, t.shape[0] // N_MICROBATCH) + t.shape[1:])
    return _jnp.moveaxis(t, 1, axis + 1)


def setup_inputs(seed: int = 0) -> dict:
    inp = _fwd_setup_inputs(seed)
    key = _jax.random.fold_in(_jax.random.key(seed), 7919)
    shape, _ = _output_shape()
    out = dict(inp)
    out["loss_target"] = _jax.random.normal(_jax.random.fold_in(key, 0), shape, _jnp.float32)
    for i, name in enumerate(TWIN_WEIGHTS):
        w = inp[name].astype(_jnp.float32)
        if MOMENT_SCALE is None:
            s = _jnp.sqrt(_jnp.mean(_jnp.square(w)) + 1e-30)
        else:
            s = MOMENT_SCALE[name]
        km, kv = _jax.random.split(_jax.random.fold_in(key, i + 1))
        out[name] = w
        out["m_" + name] = s * _jax.random.normal(km, w.shape, _jnp.float32)
        out["v_" + name] = (s * s) * _jax.random.uniform(kv, w.shape, _jnp.float32, 0.5, 1.5)
    if N_MICROBATCH > 1:
        for name, axis in PER_EXAMPLE_BATCH_AXIS.items():
            out[name] = _to_microbatches(out[name], axis)
    return {'x': out['x'], 'c': out['c'], 'positions': out['positions'], 'w_ada': out['w_ada'], 'b_ada': out['b_ada'], 'g_mix_norm': out['g_mix_norm'], 'w_in': out['w_in'], 'g_q_lat': out['g_q_lat'], 'w_q_b': out['w_q_b'], 'g_kv_lat': out['g_kv_lat'], 'w_kv_b': out['w_kv_b'], 'g_mla_q_nope': out['g_mla_q_nope'], 'g_mla_q_pe': out['g_mla_q_pe'], 'g_mla_k_nope': out['g_mla_k_nope'], 'g_mla_k_pe': out['g_mla_k_pe'], 'g_dil_q': out['g_dil_q'], 'g_dil_k': out['g_dil_k'], 'w_o': out['w_o'], 'g_ffn_norm': out['g_ffn_norm'], 'w_up': out['w_up'], 'w_conv': out['w_conv'], 'b_conv': out['b_conv'], 'w_down': out['w_down'], 'loss_target': out['loss_target'], 'm_w_ada': out['m_w_ada'], 'm_b_ada': out['m_b_ada'], 'm_g_mix_norm': out['m_g_mix_norm'], 'm_w_in': out['m_w_in'], 'm_g_q_lat': out['m_g_q_lat'], 'm_w_q_b': out['m_w_q_b'], 'm_g_kv_lat': out['m_g_kv_lat'], 'm_w_kv_b': out['m_w_kv_b'], 'm_g_mla_q_nope': out['m_g_mla_q_nope'], 'm_g_mla_q_pe': out['m_g_mla_q_pe'], 'm_g_mla_k_nope': out['m_g_mla_k_nope'], 'm_g_mla_k_pe': out['m_g_mla_k_pe'], 'm_g_dil_q': out['m_g_dil_q'], 'm_g_dil_k': out['m_g_dil_k'], 'm_w_o': out['m_w_o'], 'm_g_ffn_norm': out['m_g_ffn_norm'], 'm_w_up': out['m_w_up'], 'm_w_conv': out['m_w_conv'], 'm_b_conv': out['m_b_conv'], 'm_w_down': out['m_w_down'], 'v_w_ada': out['v_w_ada'], 'v_b_ada': out['v_b_ada'], 'v_g_mix_norm': out['v_g_mix_norm'], 'v_w_in': out['v_w_in'], 'v_g_q_lat': out['v_g_q_lat'], 'v_w_q_b': out['v_w_q_b'], 'v_g_kv_lat': out['v_g_kv_lat'], 'v_w_kv_b': out['v_w_kv_b'], 'v_g_mla_q_nope': out['v_g_mla_q_nope'], 'v_g_mla_q_pe': out['v_g_mla_q_pe'], 'v_g_mla_k_nope': out['v_g_mla_k_nope'], 'v_g_mla_k_pe': out['v_g_mla_k_pe'], 'v_g_dil_q': out['v_g_dil_q'], 'v_g_dil_k': out['v_g_dil_k'], 'v_w_o': out['v_w_o'], 'v_g_ffn_norm': out['v_g_ffn_norm'], 'v_w_up': out['v_w_up'], 'v_w_conv': out['v_w_conv'], 'v_b_conv': out['v_b_conv'], 'v_w_down': out['v_w_down']}


def _loss(weights, diff, rest, loss_target):
    with _jax.named_scope("forward"):
        args = {**rest, TWIN_DIFF_INPUT: diff, **{k: w.astype(_WEIGHT_DTYPES[k]) for k, w in weights.items()}}
        y = _forward(args)
    with _jax.named_scope("loss_head"):
        err = _jnp.square(y.astype(_jnp.float32) - loss_target)
        return 0.5 * _jnp.sum(_jnp.mean(err, axis=-1)) if err.ndim else 0.5 * err


def _adamw(w, g, m, v):
    m = ADAM_B1 * m + (1.0 - ADAM_B1) * g
    v = ADAM_B2 * v + (1.0 - ADAM_B2) * _jnp.square(g)
    m_hat = m / (1.0 - ADAM_B1 ** ADAM_STEP)
    v_hat = v / (1.0 - ADAM_B2 ** ADAM_STEP)
    delta = -ADAM_LR * (m_hat / (_jnp.sqrt(v_hat) + ADAM_EPS) + ADAM_WD * w)
    return delta, m, v


def reference(x, c, positions, w_ada, b_ada, g_mix_norm, w_in, g_q_lat, w_q_b, g_kv_lat, w_kv_b, g_mla_q_nope, g_mla_q_pe, g_mla_k_nope, g_mla_k_pe, g_dil_q, g_dil_k, w_o, g_ffn_norm, w_up, w_conv, b_conv, w_down, loss_target, m_w_ada, m_b_ada, m_g_mix_norm, m_w_in, m_g_q_lat, m_w_q_b, m_g_kv_lat, m_w_kv_b, m_g_mla_q_nope, m_g_mla_q_pe, m_g_mla_k_nope, m_g_mla_k_pe, m_g_dil_q, m_g_dil_k, m_w_o, m_g_ffn_norm, m_w_up, m_w_conv, m_b_conv, m_w_down, v_w_ada, v_b_ada, v_g_mix_norm, v_w_in, v_g_q_lat, v_w_q_b, v_g_kv_lat, v_w_kv_b, v_g_mla_q_nope, v_g_mla_q_pe, v_g_mla_k_nope, v_g_mla_k_pe, v_g_dil_q, v_g_dil_k, v_w_o, v_g_ffn_norm, v_w_up, v_w_conv, v_b_conv, v_w_down):
    given = dict(x=x, c=c, positions=positions, w_ada=w_ada, b_ada=b_ada, g_mix_norm=g_mix_norm, w_in=w_in, g_q_lat=g_q_lat, w_q_b=w_q_b, g_kv_lat=g_kv_lat, w_kv_b=w_kv_b, g_mla_q_nope=g_mla_q_nope, g_mla_q_pe=g_mla_q_pe, g_mla_k_nope=g_mla_k_nope, g_mla_k_pe=g_mla_k_pe, g_dil_q=g_dil_q, g_dil_k=g_dil_k, w_o=w_o, g_ffn_norm=g_ffn_norm, w_up=w_up, w_conv=w_conv, b_conv=b_conv, w_down=w_down, loss_target=loss_target, m_w_ada=m_w_ada, m_b_ada=m_b_ada, m_g_mix_norm=m_g_mix_norm, m_w_in=m_w_in, m_g_q_lat=m_g_q_lat, m_w_q_b=m_w_q_b, m_g_kv_lat=m_g_kv_lat, m_w_kv_b=m_w_kv_b, m_g_mla_q_nope=m_g_mla_q_nope, m_g_mla_q_pe=m_g_mla_q_pe, m_g_mla_k_nope=m_g_mla_k_nope, m_g_mla_k_pe=m_g_mla_k_pe, m_g_dil_q=m_g_dil_q, m_g_dil_k=m_g_dil_k, m_w_o=m_w_o, m_g_ffn_norm=m_g_ffn_norm, m_w_up=m_w_up, m_w_conv=m_w_conv, m_b_conv=m_b_conv, m_w_down=m_w_down, v_w_ada=v_w_ada, v_b_ada=v_b_ada, v_g_mix_norm=v_g_mix_norm, v_w_in=v_w_in, v_g_q_lat=v_g_q_lat, v_w_q_b=v_w_q_b, v_g_kv_lat=v_g_kv_lat, v_w_kv_b=v_w_kv_b, v_g_mla_q_nope=v_g_mla_q_nope, v_g_mla_q_pe=v_g_mla_q_pe, v_g_mla_k_nope=v_g_mla_k_nope, v_g_mla_k_pe=v_g_mla_k_pe, v_g_dil_q=v_g_dil_q, v_g_dil_k=v_g_dil_k, v_w_o=v_w_o, v_g_ffn_norm=v_g_ffn_norm, v_w_up=v_w_up, v_w_conv=v_w_conv, v_b_conv=v_b_conv, v_w_down=v_w_down)
    weights = {n: given[n] for n in TWIN_WEIGHTS}
    shared = {n: given[n] for n in SHARED_INPUTS}
    per_example = {n: given[n] for n in ['x', 'c', 'positions']}
    grad_fn = _jax.value_and_grad(_loss, argnums=(0, 1))

    def one_microbatch(ex, loss_target):
        ex = dict(ex)
        diff = ex.pop(TWIN_DIFF_INPUT)
        return grad_fn(weights, diff, {**shared, **ex}, loss_target)

    if N_MICROBATCH == 1:
        loss, (grad_w, grad_x) = one_microbatch(per_example, given["loss_target"])
    else:
        def body(carry, xs):
            loss_sum, grad_sum = carry
            l_k, (gw_k, gx_k) = one_microbatch(xs[0], xs[1])
            with _jax.named_scope("update"):
                return (loss_sum + l_k, _jax.tree.map(_jnp.add, grad_sum, gw_k)), gx_k

        init = (_jnp.zeros((), _jnp.float32), _jax.tree.map(_jnp.zeros_like, weights))
        (loss, grad_w), grad_x = _jax.lax.scan(body, init, (per_example, given["loss_target"]))
    with _jax.named_scope("update"):
        delta_w, new_m, new_v = {}, {}, {}
        for n in TWIN_WEIGHTS:
            delta_w[n], new_m[n], new_v[n] = _adamw(weights[n], grad_w[n], given["m_" + n], given["v_" + n])
    return (loss, grad_x, *[grad_w[n] for n in TWIN_WEIGHTS], *[delta_w[n] for n in TWIN_WEIGHTS],
            *[new_m[n] for n in TWIN_WEIGHTS], *[new_v[n] for n in TWIN_WEIGHTS])
```

```python
import functools

import numpy as np
import jax
import jax.numpy as jnp
from jax import lax
from jax.experimental import pallas as pl
from jax.experimental.pallas import tpu as pltpu

F32 = jnp.float32
BF16 = jnp.bfloat16
I32 = jnp.int32

D_MODEL = 1024
HEADS = 8
NOPE = 64
ROPE = 32
Q_LORA = 512
KV_LORA = 256
DIL_DIM = 64
DIL_W = HEADS * DIL_DIM
D_FF = 2816
UP_W = 2 * D_FF
IN_COLS = Q_LORA + KV_LORA + ROPE + 3 * DIL_W
ROPE_THETA = 10000.0
EPS = 1e-6
NEG_INF = -1e30
N_DEV = 8
N_CHIP = 4

ADAM_LR = 0.001
ADAM_B1 = 0.9
ADAM_B2 = 0.999
ADAM_EPS = 1e-08
ADAM_WD = 0.01
ADAM_STEP = 10

LANE = 128
ROW_TILE = 256
ATT_TILE = 256
VMEM_CAP = 56 * 1024 * 1024
VMEM_FLOOR = 32 * 1024 * 1024

P_QLAT, P_QD, P_KD, P_VD, P_KVLAT, P_KPE = 0, 512, 1024, 1536, 2048, 2304
P_COLS = 2432
KPE_OFF = 64

NN = (((1,), (0,)), ((), ()))
NT = (((1,), (1,)), ((), ()))
TN = (((0,), (0,)), ((), ()))
HIGHEST = lax.Precision.HIGHEST
MESH = pl.DeviceIdType.MESH


def _params(sem=None, est_bytes=0):
    limit = int(min(max(2 * est_bytes + (4 << 20), VMEM_FLOOR), VMEM_CAP))
    if sem is None:
        return pltpu.CompilerParams(vmem_limit_bytes=limit)
    return pltpu.CompilerParams(dimension_semantics=sem, vmem_limit_bytes=limit)


def _nbytes(shape, dtype):
    return int(np.prod(shape)) * jnp.dtype(dtype).itemsize


def _mm(a, b, dims, out_dtype, tm, tn, name):
    if dims == "nn":
        (m, k), (k2, n) = a.shape, b.shape
        a_spec = pl.BlockSpec((tm, k), lambda i, j: (i, 0))
        b_spec = pl.BlockSpec((k, tn), lambda i, j: (0, j))
        dn = NN
    elif dims == "nt":
        (m, k), (n, k2) = a.shape, b.shape
        a_spec = pl.BlockSpec((tm, k), lambda i, j: (i, 0))
        b_spec = pl.BlockSpec((tn, k), lambda i, j: (j, 0))
        dn = NT
    else:
        (k, m), (k2, n) = a.shape, b.shape
        a_spec = pl.BlockSpec((k, tm), lambda i, j: (0, i))
        b_spec = pl.BlockSpec((k, tn), lambda i, j: (0, j))
        dn = TN
    assert k == k2 and m % tm == 0 and n % tn == 0, (name, a.shape, b.shape, tm, tn)

    def body(a_ref, b_ref, o_ref):
        o_ref[...] = lax.dot_general(a_ref[...], b_ref[...], dn, preferred_element_type=F32).astype(o_ref.dtype)

    est = _nbytes((tm, k), a.dtype) + _nbytes((tn, k), b.dtype) + _nbytes((tm, tn), F32) + _nbytes((tm, tn), out_dtype)
    return pl.pallas_call(
        body, name=name,
        grid=(m // tm, n // tn),
        in_specs=[a_spec, b_spec],
        out_specs=pl.BlockSpec((tm, tn), lambda i, j: (i, j)),
        out_shape=jax.ShapeDtypeStruct((m, n), out_dtype),
        compiler_params=_params(("parallel", "parallel"), est),
    )(a, b)


def _seg_consts():
    seg_q = np.zeros((HEADS * LANE, LANE), np.float32)
    inv_q = np.zeros((1, LANE), np.float32)
    seg_k = np.zeros((HEADS * LANE, LANE), np.float32)
    inv_k = np.zeros((1, LANE), np.float32)
    seg_d = np.zeros((DIL_W, LANE), np.float32)
    inv_d = np.zeros((1, LANE), np.float32)
    for h in range(HEADS):
        seg_q[h * LANE:h * LANE + NOPE, 2 * h] = 1.0
        seg_q[h * LANE + NOPE:h * LANE + NOPE + ROPE, 2 * h + 1] = 1.0
        inv_q[0, 2 * h], inv_q[0, 2 * h + 1] = 1.0 / NOPE, 1.0 / ROPE
        seg_k[h * LANE:h * LANE + NOPE, h] = 1.0
        inv_k[0, h] = 1.0 / NOPE
        seg_d[h * DIL_DIM:(h + 1) * DIL_DIM, h] = 1.0
        inv_d[0, h] = 1.0 / DIL_DIM
    fold_q = np.tile(np.eye(LANE, dtype=np.float32), (HEADS, 1))
    fold_d = np.zeros((DIL_W, LANE), np.float32)
    fold_d[np.arange(DIL_W), np.arange(DIL_W) % DIL_DIM] = 1.0
    j = lambda v: jnp.asarray(v)
    return dict(seg_q=j(seg_q), exp_q=j(seg_q.T.copy()), inv_q=j(inv_q), seg_k=j(seg_k), exp_k=j(seg_k.T.copy()),
                inv_k=j(inv_k), seg_d=j(seg_d), exp_d=j(seg_d.T.copy()), inv_d=j(inv_d), fold_q=j(fold_q), fold_d=j(fold_d))


def _rope_consts():
    inv_d = jnp.power(ROPE_THETA, -2.0 * jnp.arange(DIL_DIM // 2, dtype=F32) / DIL_DIM)
    inv_q = jnp.power(ROPE_THETA, -2.0 * jnp.arange(ROPE // 2, dtype=F32) / ROPE)
    lanes = np.arange(LANE)
    freq_d = inv_d[lanes % (DIL_DIM // 2)]
    in_pe = (lanes >= KPE_OFF) & (lanes < KPE_OFF + ROPE)
    freq_q = jnp.where(jnp.asarray(in_pe), inv_q[(lanes - KPE_OFF) % (ROPE // 2)], 0.0)
    sign_d = np.where(lanes % DIL_DIM < DIL_DIM // 2, -1.0, 1.0).astype(np.float32)
    sign_q = np.where(in_pe, np.where((lanes - KPE_OFF) < ROPE // 2, -1.0, 1.0), 0.0).astype(np.float32)
    zeros, ones = np.zeros(LANE, np.float32), np.ones(LANE, np.float32)
    freq = jnp.concatenate([freq_d, freq_d, freq_q, freq_q])[None, :]
    csel = jnp.asarray(np.concatenate([ones, zeros, ones, zeros]))[None, :]
    ssel = jnp.asarray(np.concatenate([zeros, sign_d, zeros, sign_q]))[None, :]
    return freq, csel, ssel


def _full(shape):
    return pl.BlockSpec(shape, lambda *_: (0,) * len(shape))


def _tile_lanes(x, n):
    return jnp.concatenate([x] * n, axis=1)


def _rope_tables(pos_col, freq, csel, ssel):
    s = pos_col.shape[0]

    def body(p_ref, f_ref, c_ref, s_ref, o_ref):
        ang = p_ref[...].astype(F32) * f_ref[...]
        o_ref[...] = c_ref[...] * jnp.cos(ang) + s_ref[...] * jnp.sin(ang)

    return pl.pallas_call(
        body, name="rope_tables", grid=(s // ROW_TILE,),
        in_specs=[pl.BlockSpec((ROW_TILE, 1), lambda i: (i, 0)), _full((1, 4 * LANE)), _full((1, 4 * LANE)), _full((1, 4 * LANE))],
        out_specs=pl.BlockSpec((ROW_TILE, 4 * LANE), lambda i: (i, 0)),
        out_shape=jax.ShapeDtypeStruct((s, 4 * LANE), F32),
        compiler_params=_params(("parallel",)),
    )(pos_col, freq, csel, ssel)


def _rms(x):
    return lax.rsqrt(jnp.mean(x * x, axis=-1, keepdims=True) + EPS)


def _prenorm(x, gain, scale, shift, name):
    s, d = x.shape

    def body(x_ref, g_ref, sc_ref, sh_ref, h_ref):
        xv = x_ref[...]
        h = (xv * _rms(xv)) * g_ref[...] * (1.0 + sc_ref[...]) + sh_ref[...]
        h_ref[...] = h.astype(BF16)

    row = pl.BlockSpec((ROW_TILE, d), lambda i: (i, 0))
    return pl.pallas_call(
        body, name=name, grid=(s // ROW_TILE,),
        in_specs=[row, _full((1, d)), _full((1, d)), _full((1, d))],
        out_specs=row, out_shape=jax.ShapeDtypeStruct((s, d), BF16),
        compiler_params=_params(("parallel",)),
    )(x, gain, scale, shift)


def _latnorm(proj, g_q, g_kv):
    s = proj.shape[0]

    def body(q_ref, kv_ref, gq_ref, gkv_ref, ql_ref, kvl_ref):
        q, kv = q_ref[...], kv_ref[...]
        ql_ref[...] = ((q * _rms(q)) * gq_ref[...]).astype(BF16)
        kvl_ref[...] = ((kv * _rms(kv)) * gkv_ref[...]).astype(BF16)

    return pl.pallas_call(
        body, name="latnorm", grid=(s // ROW_TILE,),
        in_specs=[pl.BlockSpec((ROW_TILE, Q_LORA), lambda i: (i, P_QLAT // Q_LORA)),
                  pl.BlockSpec((ROW_TILE, KV_LORA), lambda i: (i, P_KVLAT // KV_LORA)),
                  _full((1, Q_LORA)), _full((1, KV_LORA))],
        out_specs=[pl.BlockSpec((ROW_TILE, Q_LORA), lambda i: (i, 0)), pl.BlockSpec((ROW_TILE, KV_LORA), lambda i: (i, 0))],
        out_shape=[jax.ShapeDtypeStruct((s, Q_LORA), BF16), jax.ShapeDtypeStruct((s, KV_LORA), BF16)],
        compiler_params=_params(("parallel",)),
    )(proj, proj, g_q, g_kv)


def _seg_rinv(x, seg, exp, inv):
    ssq = jnp.dot(x * x, seg, precision=HIGHEST, preferred_element_type=F32)
    r = lax.rsqrt(ssq * inv + EPS)
    return jnp.dot(r, exp, precision=HIGHEST, preferred_element_type=F32)


def _seg_mean(v, seg, exp, inv):
    return jnp.dot(jnp.dot(v, seg, precision=HIGHEST, preferred_element_type=F32) * inv, exp,
                   precision=HIGHEST, preferred_element_type=F32)


def _swap_halves(x, half):
    n = x.shape[1]
    lane = lax.broadcasted_iota(I32, (1, n), 1)
    first = (lane & (2 * half - 1)) < half
    return jnp.where(first, pltpu.roll(x, n - half, 1), pltpu.roll(x, half, 1))


def _rope(x, cos, sin_signed, half):
    return x * cos + _swap_halves(x, half) * sin_signed


def _rope_bwd(dy, cos, sin_signed, half):
    return dy * cos + _swap_halves(dy * sin_signed, half)


def _pe_lane_mask(n):
    lane = lax.broadcasted_iota(I32, (1, n), 1) & (LANE - 1)
    return (lane >= KPE_OFF) & (lane < KPE_OFF + ROPE)


def _attn_prep(q_raw, kv_raw, proj, tab, gains, consts):
    s = q_raw.shape[0]
    hw = HEADS * LANE

    def body(q_ref, kv_ref, kpe_ref, qd_ref, kd_ref, vd_ref, tab_ref,
             gq_ref, gk_ref, gkpe_ref, gdq_ref, gdk_ref,
             segq_ref, expq_ref, invq_ref, segk_ref, expk_ref, invk_ref, segd_ref, expd_ref, invd_ref,
             qm_ref, km_ref, vm_ref, qdo_ref, kdo_ref, vdo_ref):
        tab_v = tab_ref[...]
        cos_d, sin_d = _tile_lanes(tab_v[:, 0:LANE], DIL_W // LANE), _tile_lanes(tab_v[:, LANE:2 * LANE], DIL_W // LANE)
        cos_q1, sin_q1 = tab_v[:, 2 * LANE:3 * LANE], tab_v[:, 3 * LANE:4 * LANE]
        cos_q, sin_q = _tile_lanes(cos_q1, HEADS), _tile_lanes(sin_q1, HEADS)

        q = q_ref[...]
        qn = q * _seg_rinv(q, segq_ref[...], expq_ref[...], invq_ref[...]) * gq_ref[...]
        qm_ref[...] = _rope(qn, cos_q, sin_q, ROPE // 2).astype(BF16)

        kv = kv_ref[...]
        kp = kv[:, :hw]
        kn = kp * _seg_rinv(kp, segk_ref[...], expk_ref[...], invk_ref[...]) * gk_ref[...]
        kpe = kpe_ref[...]
        r_pe = lax.rsqrt(jnp.sum(kpe * kpe, axis=-1, keepdims=True) * (1.0 / ROPE) + EPS)
        kpe_r = _rope(kpe * r_pe * gkpe_ref[...], cos_q1, sin_q1, ROPE // 2)
        km_ref[...] = (kn + _tile_lanes(kpe_r, HEADS)).astype(BF16)
        vm_ref[...] = kv[:, hw:].astype(BF16)

        qd = qd_ref[...]
        qdn = qd * _seg_rinv(qd, segd_ref[...], expd_ref[...], invd_ref[...]) * gdq_ref[...]
        qdo_ref[...] = _rope(qdn, cos_d, sin_d, DIL_DIM // 2).astype(BF16)
        kd = kd_ref[...]
        kdn = kd * _seg_rinv(kd, segd_ref[...], expd_ref[...], invd_ref[...]) * gdk_ref[...]
        kdo_ref[...] = _rope(kdn, cos_d, sin_d, DIL_DIM // 2).astype(BF16)
        vdo_ref[...] = vd_ref[...].astype(BF16)

    t = ROW_TILE
    row = lambda w, cb=0: pl.BlockSpec((t, w), lambda i: (i, cb))
    c = consts
    return pl.pallas_call(
        body, name="attn_prep", grid=(s // t,),
        in_specs=[row(hw), row(hw + DIL_W), row(LANE, P_KPE // LANE), row(DIL_W, P_QD // DIL_W), row(DIL_W, P_KD // DIL_W),
                  row(DIL_W, P_VD // DIL_W), row(4 * LANE),
                  _full((1, hw)), _full((1, hw)), _full((1, LANE)), _full((1, DIL_W)), _full((1, DIL_W)),
                  _full((hw, LANE)), _full((LANE, hw)), _full((1, LANE)), _full((hw, LANE)), _full((LANE, hw)), _full((1, LANE)),
                  _full((DIL_W, LANE)), _full((LANE, DIL_W)), _full((1, LANE))],
        out_specs=[row(hw), row(hw), row(DIL_W), row(DIL_W), row(DIL_W), row(DIL_W)],
        out_shape=[jax.ShapeDtypeStruct((s, hw), BF16), jax.ShapeDtypeStruct((s, hw), BF16)]
        + [jax.ShapeDtypeStruct((s, DIL_W), BF16)] * 4,
        compiler_params=_params(("parallel",), 24 << 20),
    )(q_raw, kv_raw, proj, proj, proj, proj, tab, gains["q"], gains["k"], gains["kpe"], gains["dq"], gains["dk"],
      c["seg_q"], c["exp_q"], c["inv_q"], c["seg_k"], c["exp_k"], c["inv_k"], c["seg_d"], c["exp_d"], c["inv_d"])


def _attn_prep_bwd(dqm, dkm, dvm, dqd, dkd, dvd, q_raw, kv_raw, proj, tab, gains, consts):
    s = q_raw.shape[0]
    hw = HEADS * LANE
    n_steps = s // ROW_TILE

    def body(dqm_ref, dkm_ref, dvm_ref, dqd_ref, dkd_ref, dvd_ref, q_ref, kv_ref, kpe_ref, qd_ref, kd_ref, tab_ref,
             gq_ref, gk_ref, gkpe_ref, gdq_ref, gdk_ref,
             segq_ref, expq_ref, invq_ref, segk_ref, expk_ref, invk_ref, segd_ref, expd_ref, invd_ref, foldq_ref, foldd_ref,
             dq_ref, dkv_ref, dkpe_ref, dqdo_ref, dkdo_ref, dvdo_ref, dg_ref, acc_ref):
        i = pl.program_id(0)

        @pl.when(i == 0)
        def _():
            acc_ref[...] = jnp.zeros_like(acc_ref)

        tab_v = tab_ref[...]
        cos_d, sin_d = _tile_lanes(tab_v[:, 0:LANE], DIL_W // LANE), _tile_lanes(tab_v[:, LANE:2 * LANE], DIL_W // LANE)
        cos_q1, sin_q1 = tab_v[:, 2 * LANE:3 * LANE], tab_v[:, 3 * LANE:4 * LANE]
        cos_q, sin_q = _tile_lanes(cos_q1, HEADS), _tile_lanes(sin_q1, HEADS)

        def norm_bwd(x, dyg, gain, seg, exp, inv):
            rinv = _seg_rinv(x, seg, exp, inv)
            xn = x * rinv
            dxn = dyg * gain
            dx = rinv * (dxn - xn * _seg_mean(dxn * xn, seg, exp, inv))
            return dx, jnp.sum(dyg * xn, axis=0, keepdims=True)

        dq, gq_l = norm_bwd(q_ref[...], _rope_bwd(dqm_ref[...], cos_q, sin_q, ROPE // 2), gq_ref[...],
                            segq_ref[...], expq_ref[...], invq_ref[...])
        dq_ref[...] = dq.astype(BF16)

        dkm = dkm_ref[...]
        kv = kv_ref[...]
        dkp, gk_l = norm_bwd(kv[:, :hw], dkm, gk_ref[...], segk_ref[...], expk_ref[...], invk_ref[...])
        dkv_ref[:, :hw] = dkp.astype(BF16)
        dkv_ref[:, hw:] = dvm_ref[...].astype(BF16)

        dkpe_r = dkm[:, 0:LANE]
        for h in range(1, HEADS):
            dkpe_r = dkpe_r + dkm[:, h * LANE:(h + 1) * LANE]
        dkpe_r = jnp.where(_pe_lane_mask(LANE), dkpe_r, 0.0)
        dyg = _rope_bwd(dkpe_r, cos_q1, sin_q1, ROPE // 2)
        kpe = kpe_ref[...]
        r_pe = lax.rsqrt(jnp.sum(kpe * kpe, axis=-1, keepdims=True) * (1.0 / ROPE) + EPS)
        xn = kpe * r_pe
        dxn = dyg * gkpe_ref[...]
        dkpe = r_pe * (dxn - xn * (jnp.sum(dxn * xn, axis=-1, keepdims=True) * (1.0 / ROPE)))
        dkpe_ref[...] = dkpe.astype(BF16)
        gkpe_l = jnp.sum(dyg * xn, axis=0, keepdims=True)

        dqd_v, gdq_l = norm_bwd(qd_ref[...], _rope_bwd(dqd_ref[...], cos_d, sin_d, DIL_DIM // 2), gdq_ref[...],
                                segd_ref[...], expd_ref[...], invd_ref[...])
        dqdo_ref[...] = dqd_v.astype(BF16)
        dkd_v, gdk_l = norm_bwd(kd_ref[...], _rope_bwd(dkd_ref[...], cos_d, sin_d, DIL_DIM // 2), gdk_ref[...],
                                segd_ref[...], expd_ref[...], invd_ref[...])
        dkdo_ref[...] = dkd_v.astype(BF16)
        dvdo_ref[...] = dvd_ref[...].astype(BF16)

        acc_ref[0:1, :] += gq_l
        acc_ref[1:2, :] += gk_l
        acc_ref[2:3, 0:LANE] += gkpe_l
        acc_ref[3:4, 0:DIL_W] += gdq_l
        acc_ref[4:5, 0:DIL_W] += gdk_l

        @pl.when(i == n_steps - 1)
        def _():
            acc = acc_ref[...]
            fq = jnp.dot(acc, foldq_ref[...], precision=HIGHEST, preferred_element_type=F32)
            fd = jnp.dot(acc[:, 0:DIL_W], foldd_ref[...], precision=HIGHEST, preferred_element_type=F32)
            rows = lax.broadcasted_iota(I32, (8, LANE), 0)
            dg_ref[...] = jnp.where(rows < 2, fq, jnp.where(rows == 2, acc[:, 0:LANE], fd))

    t = ROW_TILE
    row = lambda w, cb=0: pl.BlockSpec((t, w), lambda i: (i, cb))
    c = consts
    return pl.pallas_call(
        body, name="attn_prep_bwd", grid=(n_steps,),
        in_specs=[row(hw), row(hw), row(DIL_W), row(DIL_W), row(DIL_W), row(DIL_W),
                  row(hw), row(hw + DIL_W), row(LANE, P_KPE // LANE), row(DIL_W, P_QD // DIL_W), row(DIL_W, P_KD // DIL_W),
                  row(4 * LANE),
                  _full((1, hw)), _full((1, hw)), _full((1, LANE)), _full((1, DIL_W)), _full((1, DIL_W)),
                  _full((hw, LANE)), _full((LANE, hw)), _full((1, LANE)), _full((hw, LANE)), _full((LANE, hw)), _full((1, LANE)),
                  _full((DIL_W, LANE)), _full((LANE, DIL_W)), _full((1, LANE)), _full((hw, LANE)), _full((DIL_W, LANE))],
        out_specs=[row(hw), row(hw + DIL_W), row(LANE), row(DIL_W), row(DIL_W), row(DIL_W), _full((8, LANE))],
        out_shape=[jax.ShapeDtypeStruct((s, hw), BF16), jax.ShapeDtypeStruct((s, hw + DIL_W), BF16),
                   jax.ShapeDtypeStruct((s, LANE), BF16)] + [jax.ShapeDtypeStruct((s, DIL_W), BF16)] * 3
        + [jax.ShapeDtypeStruct((8, LANE), F32)],
        scratch_shapes=[pltpu.VMEM((8, hw), F32)],
        compiler_params=_params(("arbitrary",), 28 << 20),
    )(dqm, dkm, dvm, dqd, dkd, dvd, q_raw, kv_raw, proj, proj, proj, tab,
      gains["q"], gains["k"], gains["kpe"], gains["dq"], gains["dk"],
      c["seg_q"], c["exp_q"], c["inv_q"], c["seg_k"], c["exp_k"], c["inv_k"], c["seg_d"], c["exp_d"], c["inv_d"],
      c["fold_q"], c["fold_d"])


def _latnorm_bwd(dql, dkvl, proj, g_q, g_kv):
    s = proj.shape[0]
    n_steps = s // ROW_TILE

    def body(dql_ref, dkvl_ref, q_ref, kv_ref, gq_ref, gkv_ref, dq_ref, dkv_ref, dg_ref):
        i = pl.program_id(0)

        @pl.when(i == 0)
        def _():
            dg_ref[...] = jnp.zeros_like(dg_ref)

        def one(x, dyg, gain):
            r = _rms(x)
            xn = x * r
            dxn = dyg * gain
            dx = r * (dxn - xn * jnp.mean(dxn * xn, axis=-1, keepdims=True))
            return dx, jnp.sum(dyg * xn, axis=0, keepdims=True)

        dq, gq_l = one(q_ref[...], dql_ref[...], gq_ref[...])
        dkv, gkv_l = one(kv_ref[...], dkvl_ref[...], gkv_ref[...])
        dq_ref[...] = dq.astype(BF16)
        dkv_ref[...] = dkv.astype(BF16)
        dg_ref[0:1, :] += gq_l
        dg_ref[1:2, 0:KV_LORA] += gkv_l

    t = ROW_TILE
    return pl.pallas_call(
        body, name="latnorm_bwd", grid=(n_steps,),
        in_specs=[pl.BlockSpec((t, Q_LORA), lambda i: (i, 0)), pl.BlockSpec((t, KV_LORA), lambda i: (i, 0)),
                  pl.BlockSpec((t, Q_LORA), lambda i: (i, P_QLAT // Q_LORA)),
                  pl.BlockSpec((t, KV_LORA), lambda i: (i, P_KVLAT // KV_LORA)),
                  _full((1, Q_LORA)), _full((1, KV_LORA))],
        out_specs=[pl.BlockSpec((t, Q_LORA), lambda i: (i, 0)), pl.BlockSpec((t, KV_LORA), lambda i: (i, 0)), _full((8, Q_LORA))],
        out_shape=[jax.ShapeDtypeStruct((s, Q_LORA), BF16), jax.ShapeDtypeStruct((s, KV_LORA), BF16),
                   jax.ShapeDtypeStruct((8, Q_LORA), F32)],
        compiler_params=_params(("arbitrary",)),
    )(dql, dkvl, proj, proj, g_q, g_kv)


def _resid_prenorm(x, mix, g1, gain, scale, shift):
    s, d = x.shape

    def body(x_ref, mix_ref, g1_ref, g_ref, sc_ref, sh_ref, x1_ref, h_ref):
        x1 = x_ref[...] + g1_ref[...] * mix_ref[...]
        x1_ref[...] = x1
        h_ref[...] = ((x1 * _rms(x1)) * g_ref[...] * (1.0 + sc_ref[...]) + sh_ref[...]).astype(BF16)

    row = pl.BlockSpec((ROW_TILE, d), lambda i: (i, 0))
    vec = _full((1, d))
    return pl.pallas_call(
        body, name="resid_prenorm", grid=(s // ROW_TILE,),
        in_specs=[row, row, vec, vec, vec, vec], out_specs=[row, row],
        out_shape=[jax.ShapeDtypeStruct((s, d), F32), jax.ShapeDtypeStruct((s, d), BF16)],
        compiler_params=_params(("parallel",)),
    )(x, mix, g1, gain, scale, shift)


CONV_TILE = 1408
HALO = 8


def _shift_down(x, halo, k):
    t = x.shape[0]
    row = lax.broadcasted_iota(I32, (t, 1), 0)
    out = pltpu.roll(x, k, 0)
    for r in range(k):
        out = jnp.where(row == r, halo[HALO - k + r:HALO - k + r + 1, :], out)
    return out


def _shift_up(x, halo, k):
    t = x.shape[0]
    row = lax.broadcasted_iota(I32, (t, 1), 0)
    out = pltpu.roll(x, t - k, 0)
    for r in range(k):
        out = jnp.where(row == t - k + r, halo[r:r + 1, :], out)
    return out


def _conv_fwd(x, halo, w, b):
    p1, p2 = _shift_down(x, halo, 1), _shift_down(x, halo, 2)
    u = b + p2 * w[0:1, :]
    u = u + p1 * w[1:2, :]
    u = u + x * w[2:3, :]
    return u, p1, p2


def _sigmoid(x):
    return 1.0 / (1.0 + jnp.exp(-x))


def _conv_gate(up, w_conv, b_conv):
    s = up.shape[0]
    t = ROW_TILE
    nj = D_FF // CONV_TILE
    hb = t // HALO

    def body(g_ref, v_ref, gh_ref, vh_ref, wg_ref, wv_ref, bg_ref, bv_ref, a_ref):
        live = (pl.program_id(0) > 0).astype(F32)
        ug, _, _ = _conv_fwd(g_ref[...], gh_ref[...] * live, wg_ref[...], bg_ref[...])
        uv, _, _ = _conv_fwd(v_ref[...], vh_ref[...] * live, wv_ref[...], bv_ref[...])
        a_ref[...] = (ug * _sigmoid(ug) * uv).astype(BF16)

    main = lambda off: pl.BlockSpec((t, CONV_TILE), lambda i, j: (i, j + off))
    halo = lambda off: pl.BlockSpec((HALO, CONV_TILE), lambda i, j: (jnp.maximum(i * hb - 1, 0), j + off))
    wsp = lambda off: pl.BlockSpec((3, CONV_TILE), lambda i, j: (0, j + off))
    bsp = lambda off: pl.BlockSpec((1, CONV_TILE), lambda i, j: (0, j + off))
    return pl.pallas_call(
        body, name="conv_gate", grid=(s // t, nj),
        in_specs=[main(0), main(nj), halo(0), halo(nj), wsp(0), wsp(nj), bsp(0), bsp(nj)],
        out_specs=pl.BlockSpec((t, CONV_TILE), lambda i, j: (i, j)),
        out_shape=jax.ShapeDtypeStruct((s, D_FF), BF16),
        compiler_params=_params(("parallel", "parallel"), 12 << 20),
    )(up, up, up, up, w_conv, w_conv, b_conv, b_conv)


def _gate_bwd(up, da, w_conv, b_conv):
    s = up.shape[0]
    t = ROW_TILE
    nj = D_FF // CONV_TILE
    hb = t // HALO

    def body(g_ref, v_ref, gh_ref, vh_ref, da_ref, wg_ref, wv_ref, bg_ref, bv_ref,
             dug_ref, duv_ref, dbg_ref, dbv_ref, dwg_ref, dwv_ref):
        i = pl.program_id(1)

        @pl.when(i == 0)
        def _():
            for r in (dbg_ref, dbv_ref, dwg_ref, dwv_ref):
                r[...] = jnp.zeros_like(r)

        live = (i > 0).astype(F32)
        xg, xv = g_ref[...], v_ref[...]
        ug, g1, g2 = _conv_fwd(xg, gh_ref[...] * live, wg_ref[...], bg_ref[...])
        uv, v1, v2 = _conv_fwd(xv, vh_ref[...] * live, wv_ref[...], bv_ref[...])
        sg = _sigmoid(ug)
        da_v = da_ref[...]
        dug = da_v * uv * (sg * (1.0 + ug * (1.0 - sg)))
        duv = da_v * (ug * sg)
        dug_ref[...] = dug
        duv_ref[...] = duv
        csum = lambda z: jnp.sum(z, axis=0, keepdims=True)
        dbg_ref[...] += csum(dug)
        dbv_ref[...] += csum(duv)
        dwg_ref[0:1, :] += csum(dug * g2)
        dwg_ref[1:2, :] += csum(dug * g1)
        dwg_ref[2:3, :] += csum(dug * xg)
        dwv_ref[0:1, :] += csum(duv * v2)
        dwv_ref[1:2, :] += csum(duv * v1)
        dwv_ref[2:3, :] += csum(duv * xv)

    main = lambda off: pl.BlockSpec((t, CONV_TILE), lambda j, i: (i, j + off))
    halo = lambda off: pl.BlockSpec((HALO, CONV_TILE), lambda j, i: (jnp.maximum(i * hb - 1, 0), j + off))
    wsp = lambda off: pl.BlockSpec((3, CONV_TILE), lambda j, i: (0, j + off))
    bsp = lambda off: pl.BlockSpec((1, CONV_TILE), lambda j, i: (0, j + off))
    outs = pl.pallas_call(
        body, name="gate_bwd", grid=(nj, s // t),
        in_specs=[main(0), main(nj), halo(0), halo(nj), pl.BlockSpec((t, CONV_TILE), lambda j, i: (i, j)),
                  wsp(0), wsp(nj), bsp(0), bsp(nj)],
        out_specs=[pl.BlockSpec((t, CONV_TILE), lambda j, i: (i, j)), pl.BlockSpec((t, CONV_TILE), lambda j, i: (i, j)),
                   pl.BlockSpec((1, CONV_TILE), lambda j, i: (0, j)), pl.BlockSpec((1, CONV_TILE), lambda j, i: (0, j)),
                   pl.BlockSpec((3, CONV_TILE), lambda j, i: (0, j)), pl.BlockSpec((3, CONV_TILE), lambda j, i: (0, j))],
        out_shape=[jax.ShapeDtypeStruct((s, D_FF), F32), jax.ShapeDtypeStruct((s, D_FF), F32),
                   jax.ShapeDtypeStruct((1, D_FF), F32), jax.ShapeDtypeStruct((1, D_FF), F32),
                   jax.ShapeDtypeStruct((3, D_FF), F32), jax.ShapeDtypeStruct((3, D_FF), F32)],
        compiler_params=_params(("parallel", "arbitrary"), 20 << 20),
    )(up, up, up, up, da, w_conv, w_conv, b_conv, b_conv)
    return outs


def _conv_bwd(du, w_half, name):
    s = du.shape[0]
    t = ROW_TILE
    nj = D_FF // CONV_TILE
    hb = t // HALO
    n_i = s // t

    def body(d_ref, h_ref, w_ref, o_ref):
        live = (pl.program_id(0) < n_i - 1).astype(F32)
        x = d_ref[...]
        halo = h_ref[...] * live
        w = w_ref[...]
        o = x * w[2:3, :] + _shift_up(x, halo, 1) * w[1:2, :] + _shift_up(x, halo, 2) * w[0:1, :]
        o_ref[...] = o.astype(BF16)

    return pl.pallas_call(
        body, name=name, grid=(n_i, nj),
        in_specs=[pl.BlockSpec((t, CONV_TILE), lambda i, j: (i, j)),
                  pl.BlockSpec((HALO, CONV_TILE), lambda i, j: (jnp.minimum((i + 1) * hb, s // HALO - 1), j)),
                  pl.BlockSpec((3, CONV_TILE), lambda i, j: (0, j))],
        out_specs=pl.BlockSpec((t, CONV_TILE), lambda i, j: (i, j)),
        out_shape=jax.ShapeDtypeStruct((s, D_FF), BF16),
        compiler_params=_params(("parallel", "parallel"), 8 << 20),
    )(du, du, w_half)


def _final(x1, ffn, tgt, g2):
    s, d = x1.shape
    n_steps = s // ROW_TILE

    def body(x1_ref, f_ref, t_ref, g2_ref, dy_ref, df_ref, dg2_ref, loss_ref, lacc_ref):
        i = pl.program_id(0)

        @pl.when(i == 0)
        def _():
            dg2_ref[...] = jnp.zeros_like(dg2_ref)
            lacc_ref[...] = jnp.zeros_like(lacc_ref)

        f = f_ref[...]
        e = x1_ref[...] + g2_ref[...] * f - t_ref[...]
        dy = e * (1.0 / d)
        dy_ref[...] = dy
        df_ref[...] = (dy * g2_ref[...]).astype(BF16)
        dg2_ref[...] += jnp.sum(dy * f, axis=0, keepdims=True)
        lacc_ref[...] += jnp.sum(e * e, axis=0, keepdims=True)

        @pl.when(i == n_steps - 1)
        def _():
            loss_ref[...] = jnp.sum(lacc_ref[...], axis=1, keepdims=True) * (0.5 / d)

    row = pl.BlockSpec((ROW_TILE, d), lambda i: (i, 0))
    return pl.pallas_call(
        body, name="final", grid=(n_steps,),
        in_specs=[row, row, row, _full((1, d))],
        out_specs=[row, row, _full((1, d)), _full((1, 1))],
        out_shape=[jax.ShapeDtypeStruct((s, d), F32), jax.ShapeDtypeStruct((s, d), BF16),
                   jax.ShapeDtypeStruct((1, d), F32), jax.ShapeDtypeStruct((1, 1), F32)],
        scratch_shapes=[pltpu.VMEM((1, d), F32)],
        compiler_params=_params(("arbitrary",)),
    )(x1, ffn, tgt, g2)


def _ffnnorm_bwd(dh2, x1, dy, mix, gain, scale, g1):
    s, d = x1.shape
    n_steps = s // ROW_TILE

    def body(dh_ref, x_ref, dy_ref, mix_ref, g_ref, sc_ref, g1_ref, dx_ref, dm_ref, acc_ref):
        i = pl.program_id(0)

        @pl.when(i == 0)
        def _():
            acc_ref[...] = jnp.zeros_like(acc_ref)

        dh, x = dh_ref[...], x_ref[...]
        r = _rms(x)
        xn = x * r
        dn = dh * (1.0 + sc_ref[...])
        dxn = dn * g_ref[...]
        dx = dy_ref[...] + r * (dxn - xn * jnp.mean(dxn * xn, axis=-1, keepdims=True))
        dx_ref[...] = dx
        dm_ref[...] = (dx * g1_ref[...]).astype(BF16)
        csum = lambda z: jnp.sum(z, axis=0, keepdims=True)
        acc_ref[0:1, :] += csum(dh)
        acc_ref[1:2, :] += csum(dh * (xn * g_ref[...]))
        acc_ref[2:3, :] += csum(dn * xn)
        acc_ref[3:4, :] += csum(dx * mix_ref[...])

    row = pl.BlockSpec((ROW_TILE, d), lambda i: (i, 0))
    vec = _full((1, d))
    return pl.pallas_call(
        body, name="ffnnorm_bwd", grid=(n_steps,),
        in_specs=[row, row, row, row, vec, vec, vec],
        out_specs=[row, row, _full((8, d))],
        out_shape=[jax.ShapeDtypeStruct((s, d), F32), jax.ShapeDtypeStruct((s, d), BF16), jax.ShapeDtypeStruct((8, d), F32)],
        compiler_params=_params(("arbitrary",)),
    )(dh2, x1, dy, mix, gain, scale, g1)


def _mixnorm_bwd(dh, x, dx1, gain, scale):
    s, d = x.shape
    n_steps = s // ROW_TILE

    def body(dh_ref, x_ref, dx1_ref, g_ref, sc_ref, gx_ref, acc_ref):
        i = pl.program_id(0)

        @pl.when(i == 0)
        def _():
            acc_ref[...] = jnp.zeros_like(acc_ref)

        dh, x = dh_ref[...], x_ref[...]
        r = _rms(x)
        xn = x * r
        dn = dh * (1.0 + sc_ref[...])
        dxn = dn * g_ref[...]
        gx_ref[...] = dx1_ref[...] + r * (dxn - xn * jnp.mean(dxn * xn, axis=-1, keepdims=True))
        csum = lambda z: jnp.sum(z, axis=0, keepdims=True)
        acc_ref[0:1, :] += csum(dh)
        acc_ref[1:2, :] += csum(dh * (xn * g_ref[...]))
        acc_ref[2:3, :] += csum(dn * xn)

    row = pl.BlockSpec((ROW_TILE, d), lambda i: (i, 0))
    vec = _full((1, d))
    return pl.pallas_call(
        body, name="mixnorm_bwd", grid=(n_steps,),
        in_specs=[row, row, row, vec, vec],
        out_specs=[row, _full((8, d))],
        out_shape=[jax.ShapeDtypeStruct((s, d), F32), jax.ShapeDtypeStruct((8, d), F32)],
        compiler_params=_params(("arbitrary",)),
    )(dh, x, dx1, gain, scale)


def _key_count(d, dilated):
    if not dilated:
        return jnp.where(d >= 0, 1.0, 0.0)
    one = lambda cond: jnp.where(cond, 1.0, 0.0)
    cnt = one(d <= 128) + one(((d & 3) == 0) & (d <= 512)) + one((d & 15) == 0)
    return jnp.where(d >= 0, cnt, 0.0)


def _attn_fwd(q, k, v, mla, scale, name):
    s = q.shape[0]
    qw = 2 * LANE if mla else LANE
    t = ATT_TILE
    nq = s // t

    def body(q_ref, k_ref, v_ref, o_ref, lse_ref):
        lane = lax.broadcasted_iota(I32, (1, LANE), 1)
        rel = lax.broadcasted_iota(I32, (t, t), 0) - lax.broadcasted_iota(I32, (t, t), 1)

        def q_block(qi, carry):
            r0 = pl.multiple_of(qi * t, t)
            res = []
            for a in range(2):
                sel = (lane < DIL_DIM) if a == 0 else (lane >= DIL_DIM)
                if mla:
                    qa = q_ref[pl.ds(r0, t), a * LANE:(a + 1) * LANE]
                else:
                    qa = q_ref[pl.ds(r0, t), :]
                    qa = jnp.where(sel, qa, jnp.zeros_like(qa))

                def k_block(kj, c, a=a, qa=qa):
                    m, l, acc = c
                    c0 = pl.multiple_of(kj * t, t)
                    ka = k_ref[pl.ds(c0, t), a * LANE:(a + 1) * LANE] if mla else k_ref[pl.ds(c0, t), :]
                    sc = lax.dot_general(qa, ka, NT, preferred_element_type=F32) * scale
                    cnt = _key_count(rel + (r0 - c0), not mla)
                    sc = jnp.where(cnt > 0.0, sc, NEG_INF)
                    m_new = jnp.maximum(m, jnp.max(sc, axis=1, keepdims=True))
                    alpha = jnp.exp(m - m_new)
                    p = jnp.exp(sc - m_new) * cnt
                    l = alpha * l + jnp.sum(p, axis=1, keepdims=True)
                    acc = alpha * acc + jnp.dot(p.astype(BF16), v_ref[pl.ds(c0, t), :], preferred_element_type=F32)
                    return m_new, l, acc

                init = (jnp.full((t, 1), NEG_INF, F32), jnp.zeros((t, 1), F32), jnp.zeros((t, LANE), F32))
                m, l, acc = lax.fori_loop(0, qi + 1, k_block, init)
                res.append((acc / l, m + jnp.log(l)))
            o_ref[pl.ds(r0, t), :] = jnp.where(lane < DIL_DIM, res[0][0], res[1][0]).astype(BF16)
            lse_ref[0, pl.ds(r0, t), :] = res[0][1]
            lse_ref[1, pl.ds(r0, t), :] = res[1][1]
            return carry

        lax.fori_loop(0, nq, q_block, 0)

    return pl.pallas_call(
        body, name=name, grid=(HEADS // 2,),
        in_specs=[pl.BlockSpec((s, qw), lambda h: (0, h)), pl.BlockSpec((s, qw), lambda h: (0, h)),
                  pl.BlockSpec((s, LANE), lambda h: (0, h))],
        out_specs=[pl.BlockSpec((s, LANE), lambda h: (0, h)), pl.BlockSpec((2, s, 1), lambda h: (h, 0, 0))],
        out_shape=[jax.ShapeDtypeStruct((s, DIL_W), BF16), jax.ShapeDtypeStruct((HEADS, s, 1), F32)],
        compiler_params=_params(("parallel",), 12 << 20),
    )(q, k, v)


def _attn_bwd(q, k, v, o, do, do_block0, lse, mla, scale, name):
    s = q.shape[0]
    qw = 2 * LANE if mla else LANE
    t = ATT_TILE
    nq = s // t

    def body(q_ref, k_ref, v_ref, o_ref, do_ref, lse_ref, dq_ref, dk_ref, dv_ref, delta_ref):
        lane = lax.broadcasted_iota(I32, (1, LANE), 1)
        rel = lax.broadcasted_iota(I32, (t, t), 0) - lax.broadcasted_iota(I32, (t, t), 1)

        def delta_block(qi, carry):
            r0 = pl.multiple_of(qi * t, t)
            prod = do_ref[pl.ds(r0, t), :] * o_ref[pl.ds(r0, t), :].astype(F32)
            delta_ref[0, pl.ds(r0, t), :] = jnp.sum(jnp.where(lane < DIL_DIM, prod, 0.0), axis=1, keepdims=True)
            delta_ref[1, pl.ds(r0, t), :] = jnp.sum(jnp.where(lane >= DIL_DIM, prod, 0.0), axis=1, keepdims=True)
            return carry

        lax.fori_loop(0, nq, delta_block, 0)
        dq_ref[...] = jnp.zeros_like(dq_ref)

        for a in range(2):
            sel = (lane < DIL_DIM) if a == 0 else (lane >= DIL_DIM)
            cols = slice(a * LANE, (a + 1) * LANE) if mla else slice(0, LANE)

            def k_block(kj, carry, a=a, sel=sel, cols=cols):
                c0 = pl.multiple_of(kj * t, t)
                ka = k_ref[pl.ds(c0, t), cols]
                if not mla:
                    ka = jnp.where(sel, ka, jnp.zeros_like(ka))
                vb = v_ref[pl.ds(c0, t), :]

                def q_block(qi, c):
                    dk_acc, dv_acc = c
                    r0 = pl.multiple_of(qi * t, t)
                    qa = q_ref[pl.ds(r0, t), cols]
                    if not mla:
                        qa = jnp.where(sel, qa, jnp.zeros_like(qa))
                    doa = jnp.where(sel, do_ref[pl.ds(r0, t), :], 0.0).astype(BF16)
                    sc = lax.dot_general(qa, ka, NT, preferred_element_type=F32) * scale
                    cnt = _key_count(rel + (r0 - c0), not mla)
                    sc = jnp.where(cnt > 0.0, sc, NEG_INF)
                    p = jnp.exp(sc - lse_ref[a, pl.ds(r0, t), :]) * cnt
                    dp = lax.dot_general(doa, vb, NT, preferred_element_type=F32)
                    ds = (p * (dp - delta_ref[a, pl.ds(r0, t), :]) * scale).astype(BF16)
                    dv_acc = dv_acc + lax.dot_general(p.astype(BF16), doa, TN, preferred_element_type=F32)
                    dk_acc = dk_acc + lax.dot_general(ds, qa, TN, preferred_element_type=F32)
                    dq_ref[pl.ds(r0, t), cols] += jnp.dot(ds, ka, preferred_element_type=F32)
                    return dk_acc, dv_acc

                zero = jnp.zeros((t, LANE), F32)
                dk_acc, dv_acc = lax.fori_loop(kj, nq, q_block, (zero, zero))
                if mla or a == 0:
                    dk_ref[pl.ds(c0, t), cols] = dk_acc
                else:
                    dk_ref[pl.ds(c0, t), cols] += dk_acc
                if a == 0:
                    dv_ref[pl.ds(c0, t), :] = dv_acc
                else:
                    dv_ref[pl.ds(c0, t), :] += dv_acc
                return carry

            lax.fori_loop(0, nq, k_block, 0)

    b0 = do_block0
    return pl.pallas_call(
        body, name=name, grid=(HEADS // 2,),
        in_specs=[pl.BlockSpec((s, qw), lambda h: (0, h)), pl.BlockSpec((s, qw), lambda h: (0, h)),
                  pl.BlockSpec((s, LANE), lambda h: (0, h)), pl.BlockSpec((s, LANE), lambda h: (0, h)),
                  pl.BlockSpec((s, LANE), lambda h: (0, h + b0)), pl.BlockSpec((2, s, 1), lambda h: (h, 0, 0))],
        out_specs=[pl.BlockSpec((s, qw), lambda h: (0, h)), pl.BlockSpec((s, qw), lambda h: (0, h)),
                   pl.BlockSpec((s, LANE), lambda h: (0, h))],
        out_shape=[jax.ShapeDtypeStruct(q.shape, F32), jax.ShapeDtypeStruct(k.shape, F32), jax.ShapeDtypeStruct((s, DIL_W), F32)],
        scratch_shapes=[pltpu.VMEM((2, s, 1), F32)],
        compiler_params=_params(("parallel",), 24 << 20),
    )(q, k, v, o, do, lse)


def _ada_fwd(c_all, w_shard, b_shard):
    n, d = c_all.shape
    cols = w_shard.shape[1]

    def body(c_ref, w_ref, b_ref, o_ref):
        cv = c_ref[...]
        sc = (cv * _sigmoid(cv)).astype(BF16)
        o_ref[...] = jnp.dot(sc, w_ref[...].astype(BF16), preferred_element_type=F32) + b_ref[...]

    return pl.pallas_call(
        body, name="ada_fwd", out_shape=jax.ShapeDtypeStruct((n, cols), F32),
        compiler_params=_params(None, 16 << 20),
    )(c_all, w_shard, b_shard)


def _ada_bwd(c_all, dmod_shard):
    n, d = c_all.shape
    cols = dmod_shard.shape[1]

    def body(c_ref, g_ref, o_ref):
        cv = c_ref[...]
        o_ref[...] = lax.dot_general(cv * _sigmoid(cv), g_ref[...], TN, precision=HIGHEST, preferred_element_type=F32)

    return pl.pallas_call(
        body, name="ada_bwd", out_shape=jax.ShapeDtypeStruct((d, cols), F32),
        compiler_params=_params(None, 16 << 20),
    )(c_all, dmod_shard)


def _sum_devices(g):
    n, r, w = g.shape

    def body(g_ref, o_ref):
        acc = g_ref[0]
        for k in range(1, n):
            acc = acc + g_ref[k]
        o_ref[...] = acc

    return pl.pallas_call(
        body, name="sum_devices", out_shape=jax.ShapeDtypeStruct((r, w), F32),
        compiler_params=_params(None, 4 << 20),
    )(g)


def _adamw(w, g, m, v, name):
    r, c = w.shape
    tr = r
    for cand in (256, 128, 64, 32, 16, 8):
        if r % cand == 0 and r > cand:
            tr = cand
            break

    def body(w_ref, g_ref, m_ref, v_ref, d_ref, mo_ref, vo_ref):
        gv = g_ref[...]
        mn = ADAM_B1 * m_ref[...] + (1.0 - ADAM_B1) * gv
        vn = ADAM_B2 * v_ref[...] + (1.0 - ADAM_B2) * (gv * gv)
        m_hat = mn / (1.0 - ADAM_B1 ** ADAM_STEP)
        v_hat = vn / (1.0 - ADAM_B2 ** ADAM_STEP)
        d_ref[...] = -ADAM_LR * (m_hat / (jnp.sqrt(v_hat) + ADAM_EPS) + ADAM_WD * w_ref[...])
        mo_ref[...] = mn
        vo_ref[...] = vn

    blk = pl.BlockSpec((tr, c), lambda i: (i, 0))
    return pl.pallas_call(
        body, name=name, grid=(r // tr,), in_specs=[blk] * 4, out_specs=[blk] * 3,
        out_shape=[jax.ShapeDtypeStruct((r, c), F32)] * 3,
        compiler_params=_params(("parallel",), 7 * _nbytes((tr, c), F32)),
    )(w, g, m, v)


def _position():
    return lax.axis_index("x"), lax.axis_index("y"), lax.axis_index("c")


def _other_chips(x, y):
    return [(1 - x, y, 2 * (1 - x) + y), (x, 1 - y, 2 * x + (1 - y)), (1 - x, 1 - y, 2 * (1 - x) + (1 - y))]


def _ag_small(v, name):
    r, w = v.shape

    def body(v_ref, out_ref, send_sems, recv_sems, local_sem):
        x, y, c = _position()
        me = 4 * x + 2 * y + c
        mine = pltpu.make_async_copy(v_ref, out_ref.at[me], local_sem)
        mine.start()
        peers = []
        for k in range(1, N_DEV):
            fx, fy, fc = (k >> 2) & 1, (k >> 1) & 1, k & 1
            px = 1 - x if fx else x
            py = 1 - y if fy else y
            pc = 1 - c if fc else c
            peers.append((px, py, pc))
        sends = []
        for k, peer in enumerate(peers):
            cp = pltpu.make_async_remote_copy(src_ref=v_ref, dst_ref=out_ref.at[me], send_sem=send_sems.at[k],
                                              recv_sem=recv_sems.at[k], device_id=peer, device_id_type=MESH)
            cp.start()
            sends.append(cp)
        for k, (px, py, pc) in enumerate(peers):
            pltpu.make_async_remote_copy(src_ref=v_ref, dst_ref=out_ref.at[4 * px + 2 * py + pc], send_sem=send_sems.at[k],
                                         recv_sem=recv_sems.at[k], device_id=(px, py, pc), device_id_type=MESH).wait_recv()
        for cp in sends:
            cp.wait_send()
        mine.wait()

    return pl.pallas_call(
        body, name=name,
        out_shape=jax.ShapeDtypeStruct((N_DEV, r, w), F32),
        in_specs=[pl.BlockSpec(memory_space=pltpu.VMEM)],
        out_specs=pl.BlockSpec(memory_space=pltpu.VMEM),
        scratch_shapes=[pltpu.SemaphoreType.DMA((N_DEV - 1,)), pltpu.SemaphoreType.DMA((N_DEV - 1,)), pltpu.SemaphoreType.DMA],
        compiler_params=_params(None, 10 * _nbytes((r, w), F32)),
    )(v)


ANY = pl.BlockSpec(memory_space=pl.ANY)


def _ag_weights(shards):
    n = len(shards)

    def body(*refs):
        w_refs, out_refs = refs[:n], refs[n:2 * n]
        send_sems, recv_sems, local_sems = refs[2 * n:]
        x, y, c = _position()
        q0 = 2 * x + y
        sibling = (x, y, 1 - c)
        chips = _other_chips(x, y)
        local_copies, sends = [], []
        for k in range(n):
            w_ref, out_ref = w_refs[k], out_refs[k]
            half = w_ref.shape[0] // 2

            def blk(q, e, out_ref=out_ref, half=half):
                return out_ref.at[q, pl.ds(pl.multiple_of(e * half, 16), half), :]

            local = pltpu.make_async_copy(w_ref, out_ref.at[q0], local_sems.at[k])
            local.start()
            local_copies.append(local)
            src = w_ref.at[pl.ds(pl.multiple_of(c * half, 16), half), :]
            for j, (cx, cy, _) in enumerate(chips):
                cp = pltpu.make_async_remote_copy(src_ref=src, dst_ref=blk(q0, c), send_sem=send_sems.at[6 * k + j],
                                                  recv_sem=recv_sems.at[6 * k + j], device_id=(cx, cy, c), device_id_type=MESH)
                cp.start()
                sends.append(cp)
        for k in range(n):
            out_ref = out_refs[k]
            half = w_refs[k].shape[0] // 2

            def blk(q, e, out_ref=out_ref, half=half):
                return out_ref.at[q, pl.ds(pl.multiple_of(e * half, 16), half), :]

            for j, (cx, cy, qj) in enumerate(chips):
                pltpu.make_async_remote_copy(src_ref=blk(qj, c), dst_ref=blk(qj, c), send_sem=send_sems.at[6 * k + j],
                                             recv_sem=recv_sems.at[6 * k + j], device_id=(cx, cy, c),
                                             device_id_type=MESH).wait_recv()
                fw = pltpu.make_async_remote_copy(src_ref=blk(qj, c), dst_ref=blk(qj, c), send_sem=send_sems.at[6 * k + 3 + j],
                                                  recv_sem=recv_sems.at[6 * k + 3 + j], device_id=sibling, device_id_type=MESH)
                fw.start()
                sends.append(fw)
        for k in range(n):
            out_ref = out_refs[k]
            half = w_refs[k].shape[0] // 2
            for j, (cx, cy, qj) in enumerate(chips):
                dst = out_ref.at[qj, pl.ds(pl.multiple_of((1 - c) * half, 16), half), :]
                pltpu.make_async_remote_copy(src_ref=dst, dst_ref=dst, send_sem=send_sems.at[6 * k + 3 + j],
                                             recv_sem=recv_sems.at[6 * k + 3 + j], device_id=sibling,
                                             device_id_type=MESH).wait_recv()
        for cp in sends:
            cp.wait_send()
        for cp in local_copies:
            cp.wait()

    return pl.pallas_call(
        body, name="ag_weights",
        out_shape=[jax.ShapeDtypeStruct((N_CHIP,) + s.shape, s.dtype) for s in shards],
        in_specs=[ANY] * n, out_specs=[ANY] * n,
        scratch_shapes=[pltpu.SemaphoreType.DMA((6 * n,)), pltpu.SemaphoreType.DMA((6 * n,)), pltpu.SemaphoreType.DMA((n,))],
    )(*shards)


def _swap_halves_d2d(grads):
    n = len(grads)

    def body(*refs):
        g_refs, out_refs = refs[:n], refs[n:2 * n]
        send_sems, recv_sems = refs[2 * n:]
        x, y, c = _position()
        sibling = (x, y, 1 - c)
        cps = []
        for k in range(n):
            cp = pltpu.make_async_remote_copy(src_ref=g_refs[k].at[:, 1 - c], dst_ref=out_refs[k], send_sem=send_sems.at[k],
                                              recv_sem=recv_sems.at[k], device_id=sibling, device_id_type=MESH)
            cp.start()
            cps.append(cp)
        for cp in cps:
            cp.wait_recv()
        for cp in cps:
            cp.wait_send()

    return pl.pallas_call(
        body, name="rs_pair_swap",
        out_shape=[jax.ShapeDtypeStruct((N_CHIP,) + g.shape[2:], g.dtype) for g in grads],
        in_specs=[ANY] * n, out_specs=[ANY] * n,
        scratch_shapes=[pltpu.SemaphoreType.DMA((n,)), pltpu.SemaphoreType.DMA((n,))],
    )(*grads)


def _pair_sum(g, a, c_idx, name):
    _, _, rh, cols = g.shape
    tr = rh
    for cand in (256, 128, 64, 32, 16):
        if rh % cand == 0 and rh > cand:
            tr = cand
            break

    def body(c_ref, g_ref, a_ref, o_ref):
        o_ref[...] = (g_ref[...] + a_ref[...]).astype(BF16)

    return pl.pallas_call(
        body, name=name,
        grid_spec=pltpu.PrefetchScalarGridSpec(
            num_scalar_prefetch=1, grid=(N_CHIP, rh // tr),
            in_specs=[pl.BlockSpec((None, None, tr, cols), lambda q, i, c_ref: (q, c_ref[0], i, 0)),
                      pl.BlockSpec((None, tr, cols), lambda q, i, c_ref: (q, i, 0))],
            out_specs=pl.BlockSpec((None, tr, cols), lambda q, i, c_ref: (q, i, 0))),
        out_shape=jax.ShapeDtypeStruct((N_CHIP, rh, cols), BF16),
        compiler_params=_params(("parallel", "parallel"), 10 * _nbytes((tr, cols), F32)),
    )(c_idx, g, a)


def _scatter_partials(parts):
    n = len(parts)

    def body(*refs):
        p_refs, out_refs = refs[:n], refs[n:2 * n]
        send_sems, recv_sems = refs[2 * n:]
        x, y, c = _position()
        chips = _other_chips(x, y)
        cps = []
        for k in range(n):
            for j, (cx, cy, qj) in enumerate(chips):
                cp = pltpu.make_async_remote_copy(src_ref=p_refs[k].at[qj], dst_ref=out_refs[k].at[j],
                                                  send_sem=send_sems.at[3 * k + j], recv_sem=recv_sems.at[3 * k + j],
                                                  device_id=(cx, cy, c), device_id_type=MESH)
                cp.start()
                cps.append(cp)
        for cp in cps:
            cp.wait_recv()
        for cp in cps:
            cp.wait_send()

    return pl.pallas_call(
        body, name="rs_scatter",
        out_shape=[jax.ShapeDtypeStruct((3,) + p.shape[1:], p.dtype) for p in parts],
        in_specs=[ANY] * n, out_specs=[ANY] * n,
        scratch_shapes=[pltpu.SemaphoreType.DMA((3 * n,)), pltpu.SemaphoreType.DMA((3 * n,))],
    )(*parts)


def _shard_sum(p, b, q_idx, name):
    _, rh, cols = p.shape
    tr = rh
    for cand in (256, 128, 64, 32, 16):
        if rh % cand == 0 and rh > cand:
            tr = cand
            break

    def body(q_ref, p_ref, b_ref, o_ref):
        acc = p_ref[...].astype(F32)
        for j in range(3):
            acc = acc + b_ref[j].astype(F32)
        o_ref[...] = acc

    return pl.pallas_call(
        body, name=name,
        grid_spec=pltpu.PrefetchScalarGridSpec(
            num_scalar_prefetch=1, grid=(rh // tr,),
            in_specs=[pl.BlockSpec((None, tr, cols), lambda i, q_ref: (q_ref[0], i, 0)),
                      pl.BlockSpec((3, tr, cols), lambda i, q_ref: (0, i, 0))],
            out_specs=pl.BlockSpec((tr, cols), lambda i, q_ref: (i, 0))),
        out_shape=jax.ShapeDtypeStruct((rh, cols), F32),
        compiler_params=_params(("parallel",), 8 * _nbytes((tr, cols), F32)),
    )(q_idx, p, b)


def _join_halves(halves):
    n = len(halves)

    def body(*refs):
        h_refs, out_refs = refs[:n], refs[n:2 * n]
        send_sems, recv_sems, local_sems = refs[2 * n:]
        x, y, c = _position()
        sibling = (x, y, 1 - c)
        cps, locs = [], []
        for k in range(n):
            loc = pltpu.make_async_copy(h_refs[k], out_refs[k].at[c], local_sems.at[k])
            loc.start()
            locs.append(loc)
            cp = pltpu.make_async_remote_copy(src_ref=h_refs[k], dst_ref=out_refs[k].at[c], send_sem=send_sems.at[k],
                                              recv_sem=recv_sems.at[k], device_id=sibling, device_id_type=MESH)
            cp.start()
            cps.append(cp)
        for k in range(n):
            dst = out_refs[k].at[1 - c]
            pltpu.make_async_remote_copy(src_ref=h_refs[k], dst_ref=dst, send_sem=send_sems.at[k], recv_sem=recv_sems.at[k],
                                         device_id=sibling, device_id_type=MESH).wait_recv()
        for cp in cps:
            cp.wait_send()
        for loc in locs:
            loc.wait()

    return pl.pallas_call(
        body, name="rs_join",
        out_shape=[jax.ShapeDtypeStruct((2,) + h.shape, h.dtype) for h in halves],
        in_specs=[ANY] * n, out_specs=[ANY] * n,
        scratch_shapes=[pltpu.SemaphoreType.DMA((n,)), pltpu.SemaphoreType.DMA((n,)), pltpu.SemaphoreType.DMA((n,))],
    )(*halves)


def _cols_from_shards(g):
    q, r, cs = g.shape
    return jnp.transpose(g, (1, 0, 2)).reshape(r, q * cs)


def _cols_to_shards(w):
    r, cfull = w.shape
    return jnp.transpose(w.reshape(r, N_CHIP, cfull // N_CHIP), (1, 0, 2))


def _pad_w_in(w):
    z = lambda n: jnp.zeros((w.shape[0], n), w.dtype)
    q_lat, kv_lat, kpe = w[:, 0:512], w[:, 512:768], w[:, 768:800]
    qd, kd, vd = w[:, 800:1312], w[:, 1312:1824], w[:, 1824:2336]
    return jnp.concatenate([q_lat, qd, kd, vd, kv_lat, z(KPE_OFF), kpe, z(LANE - KPE_OFF - ROPE)], axis=1)


def _unpad_w_in(g):
    return jnp.concatenate([g[:, P_QLAT:P_QLAT + Q_LORA], g[:, P_KVLAT:P_KVLAT + KV_LORA],
                            g[:, P_KPE + KPE_OFF:P_KPE + KPE_OFF + ROPE], g[:, P_QD:P_QD + 3 * DIL_W]], axis=1)


def _pad_w_qb(w):
    w3 = w.reshape(Q_LORA, HEADS, NOPE + ROPE)
    return jnp.pad(w3, ((0, 0), (0, 0), (0, LANE - NOPE - ROPE))).reshape(Q_LORA, HEADS * LANE)


def _unpad_w_qb(g):
    return g.reshape(Q_LORA, HEADS, LANE)[:, :, :NOPE + ROPE].reshape(Q_LORA, HEADS * (NOPE + ROPE))


def _pad_w_kvb(w):
    w3 = w.reshape(KV_LORA, HEADS, 2 * NOPE)
    kp = jnp.pad(w3[:, :, :NOPE], ((0, 0), (0, 0), (0, LANE - NOPE))).reshape(KV_LORA, HEADS * LANE)
    return jnp.concatenate([kp, w3[:, :, NOPE:].reshape(KV_LORA, DIL_W)], axis=1)


def _unpad_w_kvb(g):
    gk = g[:, :HEADS * LANE].reshape(KV_LORA, HEADS, LANE)[:, :, :NOPE]
    gv = g[:, HEADS * LANE:].reshape(KV_LORA, HEADS, NOPE)
    return jnp.concatenate([gk, gv], axis=2).reshape(KV_LORA, HEADS * 2 * NOPE)


def _head_gains(g_q_nope, g_q_pe, g_k_nope, g_k_pe, g_dq, g_dk):
    z = lambda n: jnp.zeros((1, n), F32)
    q1 = jnp.concatenate([g_q_nope, g_q_pe, z(LANE - NOPE - ROPE)], axis=1)
    k1 = jnp.concatenate([g_k_nope, z(LANE - NOPE)], axis=1)
    kpe = jnp.concatenate([z(KPE_OFF), g_k_pe, z(LANE - KPE_OFF - ROPE)], axis=1)
    return dict(q=jnp.tile(q1, (1, HEADS)), k=jnp.tile(k1, (1, HEADS)), kpe=kpe,
                dq=jnp.tile(g_dq, (1, HEADS)), dk=jnp.tile(g_dk, (1, HEADS)))


SMALL_NAMES = ("g_mix_norm", "g_q_lat", "g_kv_lat", "g_mla_q_nope", "g_mla_q_pe", "g_mla_k_nope", "g_mla_k_pe",
               "g_dil_q", "g_dil_k", "g_ffn_norm", "b_conv")


def _pack(vs):
    parts, spans, off = [], [], 0
    for v in vs:
        n = v.shape[1]
        npad = -(-n // LANE) * LANE
        parts.append(jnp.pad(v, ((0, 0), (0, npad - n))))
        spans.append((off, n))
        off += npad
    return jnp.concatenate(parts, axis=1), spans


def kernel(x, c, positions, w_ada, b_ada, g_mix_norm, w_in, g_q_lat, w_q_b, g_kv_lat, w_kv_b, g_mla_q_nope, g_mla_q_pe, g_mla_k_nope, g_mla_k_pe, g_dil_q, g_dil_k, w_o, g_ffn_norm, w_up, w_conv, b_conv, w_down, loss_target, m_w_ada, m_b_ada, m_g_mix_norm, m_w_in, m_g_q_lat, m_w_q_b, m_g_kv_lat, m_w_kv_b, m_g_mla_q_nope, m_g_mla_q_pe, m_g_mla_k_nope, m_g_mla_k_pe, m_g_dil_q, m_g_dil_k, m_w_o, m_g_ffn_norm, m_w_up, m_w_conv, m_b_conv, m_w_down, v_w_ada, v_b_ada, v_g_mix_norm, v_w_in, v_g_q_lat, v_w_q_b, v_g_kv_lat, v_w_kv_b, v_g_mla_q_nope, v_g_mla_q_pe, v_g_mla_k_nope, v_g_mla_k_pe, v_g_dil_q, v_g_dil_k, v_w_o, v_g_ffn_norm, v_w_up, v_w_conv, v_b_conv, v_w_down):
    args = dict(locals())
    weights = {n: args[n][0] for n in ("w_ada", "w_in", "w_q_b", "w_kv_b", "w_o", "w_up", "w_conv", "w_down")}
    small_w = {n: args[n] for n in SMALL_NAMES + ("b_ada",)}
    mom_m = {n[2:]: (args[n][0] if args[n].ndim == 3 else args[n]) for n in args if n.startswith("m_")}
    mom_v = {n[2:]: (args[n][0] if args[n].ndim == 3 else args[n]) for n in args if n.startswith("v_")}

    xi, yi, ci = _position()
    q0 = 2 * xi + yi
    me = 4 * xi + 2 * yi + ci
    xs, tgt = x[0], loss_target[0]
    s = xs.shape[0]
    consts = _seg_consts()

    c_all = _ag_small(c, "ag_c")[:, 0, :]
    ada_cols = w_ada.shape[2]
    b_shard = lax.dynamic_slice_in_dim(b_ada, q0 * ada_cols, ada_cols, axis=1)
    mod_blk = _ada_fwd(c_all, weights["w_ada"], b_shard)
    mod_all = _ag_small(mod_blk, "ag_mod").reshape(N_CHIP, 2, N_DEV, ada_cols)
    mod = lax.dynamic_index_in_dim(lax.dynamic_index_in_dim(mod_all, ci, 1, False), me, 1, False)
    mod = mod.reshape(1, N_CHIP * ada_cols)
    sh1, sc1, g1, sh2, sc2, g2 = [mod[:, k * D_MODEL:(k + 1) * D_MODEL] for k in range(6)]

    gathered = _ag_weights([weights[n].astype(BF16) for n in ("w_in", "w_q_b", "w_kv_b", "w_o", "w_up", "w_down")])
    w_in_p = _pad_w_in(_cols_from_shards(gathered[0]))
    w_qb_p = _pad_w_qb(_cols_from_shards(gathered[1]))
    w_kvb_p = _pad_w_kvb(_cols_from_shards(gathered[2]))
    w_o_f = gathered[3].reshape(D_MODEL, D_MODEL)
    w_up_f = _cols_from_shards(gathered[4])
    w_down_f = gathered[5].reshape(D_FF, D_MODEL)
    w_conv_f = _ag_small(weights["w_conv"], "ag_wconv")
    w_conv_f = jnp.transpose(w_conv_f.reshape(N_CHIP, 2, 3, -1)[:, 0], (1, 0, 2)).reshape(3, UP_W)

    gains = _head_gains(g_mla_q_nope, g_mla_q_pe, g_mla_k_nope, g_mla_k_pe, g_dil_q, g_dil_k)
    tab = _rope_tables(positions.reshape(s, 1), *_rope_consts())

    h = _prenorm(xs, g_mix_norm, sc1, sh1, "prenorm")
    proj = _mm(h, w_in_p, "nn", F32, 512, P_COLS, "mm_in")
    ql, kvl = _latnorm(proj, g_q_lat, g_kv_lat)
    q_raw = _mm(ql, w_qb_p, "nn", F32, 512, HEADS * LANE, "mm_qb")
    kv_raw = _mm(kvl, w_kvb_p, "nn", F32, 512, HEADS * LANE + DIL_W, "mm_kvb")
    qm, km, vm, qd, kd, vd = _attn_prep(q_raw, kv_raw, proj, tab, gains, consts)
    scale_m, scale_d = (NOPE + ROPE) ** -0.5, DIL_DIM ** -0.5
    o_m, lse_m = _attn_fwd(qm, km, vm, True, scale_m, "attn_mla")
    o_d, lse_d = _attn_fwd(qd, kd, vd, False, scale_d, "attn_dil")
    mix_in = jnp.concatenate([o_m, o_d], axis=1)
    mix = _mm(mix_in, w_o_f, "nn", F32, 512, D_MODEL, "mm_o")
    x1, h2 = _resid_prenorm(xs, mix, g1, g_ffn_norm, sc2, sh2)
    up = _mm(h2, w_up_f, "nn", F32, 512, CONV_TILE, "mm_up")
    act = _conv_gate(up, w_conv_f, b_conv)
    ffn = _mm(act, w_down_f, "nn", F32, 256, D_MODEL, "mm_down")
    dy, dffn, dg2, loss_part = _final(x1, ffn, tgt, g2)

    da = _mm(dffn, w_down_f, "nt", F32, 512, CONV_TILE, "mm_down_dx")
    gw_down = _mm(act, dffn, "tn", F32, 256, D_MODEL, "mm_down_dw")
    dug, duv, dbg, dbv, dwg, dwv = _gate_bwd(up, da, w_conv_f, b_conv)
    dup = jnp.concatenate([_conv_bwd(dug, w_conv_f[:, :D_FF], "conv_bwd_gate"),
                           _conv_bwd(duv, w_conv_f[:, D_FF:], "conv_bwd_val")], axis=1)
    dh2 = _mm(dup, w_up_f, "nt", F32, 256, 512, "mm_up_dx")
    gw_up = _mm(h2, dup, "tn", F32, 512, CONV_TILE, "mm_up_dw")
    dx1, dmix, acc2 = _ffnnorm_bwd(dh2, x1, dy, mix, g_ffn_norm, sc2, g1)
    dmix_in = _mm(dmix, w_o_f, "nt", F32, 512, D_MODEL, "mm_o_dx")
    gw_o = _mm(mix_in, dmix, "tn", F32, 512, D_MODEL, "mm_o_dw")
    dqm, dkm, dvm = _attn_bwd(qm, km, vm, o_m, dmix_in, 0, lse_m, True, scale_m, "attn_mla_bwd")
    dqd, dkd, dvd = _attn_bwd(qd, kd, vd, o_d, dmix_in, DIL_W // LANE, lse_d, False, scale_d, "attn_dil_bwd")
    dq_raw, dkv_raw, dkpe_b, dqd_b, dkd_b, dvd_b, dgains = _attn_prep_bwd(
        dqm, dkm, dvm, dqd, dkd, dvd, q_raw, kv_raw, proj, tab, gains, consts)
    dql = _mm(dq_raw, w_qb_p, "nt", F32, 512, Q_LORA, "mm_qb_dx")
    gw_qb = _unpad_w_qb(_mm(ql, dq_raw, "tn", F32, Q_LORA, HEADS * LANE, "mm_qb_dw"))
    dkvl = _mm(dkv_raw, w_kvb_p, "nt", F32, 512, KV_LORA, "mm_kvb_dx")
    gw_kvb = _unpad_w_kvb(_mm(kvl, dkv_raw, "tn", F32, KV_LORA, HEADS * LANE + DIL_W, "mm_kvb_dw"))
    dqlat_b, dkvlat_b, dglat = _latnorm_bwd(dql, dkvl, proj, g_q_lat, g_kv_lat)
    dproj = jnp.concatenate([dqlat_b, dqd_b, dkd_b, dvd_b, dkvlat_b, dkpe_b], axis=1)
    dh = _mm(dproj, w_in_p, "nt", F32, 512, D_MODEL, "mm_in_dx")
    gw_in = _unpad_w_in(_mm(h, dproj, "tn", F32, 512, P_COLS, "mm_in_dw"))
    grad_x, acc1 = _mixnorm_bwd(dh, xs, dx1, g_mix_norm, sc1)

    dmod = jnp.concatenate([acc1[0:1], acc1[1:2], acc2[3:4], acc2[0:1], acc2[1:2], dg2], axis=1)
    small_g = {"g_mix_norm": acc1[2:3], "g_q_lat": dglat[0:1], "g_kv_lat": dglat[1:2, :KV_LORA],
               "g_mla_q_nope": dgains[0:1, :NOPE], "g_mla_q_pe": dgains[0:1, NOPE:NOPE + ROPE],
               "g_mla_k_nope": dgains[1:2, :NOPE], "g_mla_k_pe": dgains[2:3, KPE_OFF:KPE_OFF + ROPE],
               "g_dil_q": dgains[3:4, :DIL_DIM], "g_dil_k": dgains[4:5, :DIL_DIM], "g_ffn_norm": acc2[2:3],
               "b_conv": jnp.concatenate([dbg, dbv], axis=1)}
    dw_conv = jnp.concatenate([dwg, dwv], axis=1)
    packed, spans = _pack([dmod] + [small_g[n] for n in SMALL_NAMES] + [dw_conv[k:k + 1] for k in range(3)])
    gathered_small = _ag_small(packed, "ag_small")
    summed = _sum_devices(gathered_small)
    take = lambda k: summed[:, spans[k][0]:spans[k][0] + spans[k][1]]
    grads = {"b_ada": take(0)}
    for k, n in enumerate(SMALL_NAMES):
        grads[n] = take(1 + k)
    shard_cols = UP_W // N_CHIP
    gconv_full = jnp.concatenate([take(1 + len(SMALL_NAMES) + k) for k in range(3)], axis=0)
    grads["w_conv"] = lax.dynamic_slice_in_dim(gconv_full, q0 * shard_cols, shard_cols, axis=1)
    dmod_all = gathered_small[:, 0, :6 * D_MODEL]
    grads["w_ada"] = _ada_bwd(c_all, lax.dynamic_slice_in_dim(dmod_all, q0 * ada_cols, ada_cols, axis=1))

    def halves(g4):
        q, r, cc = g4.shape
        return g4.reshape(q, 2, r // 2, cc)

    big = [halves(_cols_to_shards(gw_in)), halves(_cols_to_shards(gw_qb)), halves(_cols_to_shards(gw_kvb)),
           halves(gw_o.reshape(N_CHIP, D_MODEL // N_CHIP, D_MODEL)), halves(_cols_to_shards(gw_up)),
           halves(gw_down.reshape(N_CHIP, D_FF // N_CHIP, D_MODEL))]
    big_names = ("w_in", "w_q_b", "w_kv_b", "w_o", "w_up", "w_down")
    from_sibling = _swap_halves_d2d(big)
    c_idx, q_idx = jnp.reshape(ci, (1,)).astype(I32), jnp.reshape(q0, (1,)).astype(I32)
    chip_sums = [_pair_sum(g, a, c_idx, "pair_sum_" + n) for g, a, n in zip(big, from_sibling, big_names)]
    received = _scatter_partials(chip_sums)
    half_sums = [_shard_sum(p, b, q_idx, "shard_sum_" + n) for p, b, n in zip(chip_sums, received, big_names)]
    joined = _join_halves(half_sums)
    for n, j in zip(big_names, joined):
        grads[n] = j.reshape(2 * j.shape[1], j.shape[2])

    delta, new_m, new_v = {}, {}, {}
    for n in ("w_ada", "w_in", "w_q_b", "w_kv_b", "w_o", "w_up", "w_conv", "w_down"):
        delta[n], new_m[n], new_v[n] = _adamw(weights[n], grads[n], mom_m[n], mom_v[n], "adamw_" + n)
    vec_names = ("b_ada",) + SMALL_NAMES
    pw, vspans = _pack([small_w[n] for n in vec_names])
    pg, _ = _pack([grads[n] for n in vec_names])
    pm, _ = _pack([mom_m[n] for n in vec_names])
    pv, _ = _pack([mom_v[n] for n in vec_names])
    rows8 = lambda z: z.reshape(8, z.shape[1] // 8)
    pad_mask, _ = _pack([jnp.ones_like(small_w[n]) for n in vec_names])
    pv = jnp.where(pad_mask > 0, pv, 1.0)
    sd, sm, sv = _adamw(rows8(pw), rows8(pg), rows8(pm), rows8(pv), "adamw_small")
    for k, n in enumerate(vec_names):
        o, ln = vspans[k]
        delta[n], new_m[n], new_v[n] = (z.reshape(1, -1)[:, o:o + ln] for z in (sd, sm, sv))

    loss = lax.psum(loss_part[0, 0], ("x", "y", "c"))
    order = ("w_ada", "b_ada", "g_mix_norm", "w_in", "g_q_lat", "w_q_b", "g_kv_lat", "w_kv_b", "g_mla_q_nope", "g_mla_q_pe",
             "g_mla_k_nope", "g_mla_k_pe", "g_dil_q", "g_dil_k", "w_o", "g_ffn_norm", "w_up", "w_conv", "b_conv", "w_down")
    lead = lambda n, z: z[None] if n.startswith("w_") else z
    outs = [loss, grad_x[None]]
    for d_ in (grads, delta, new_m, new_v):
        outs += [lead(n, d_[n]) for n in order]
    return tuple(outs)
```

```python
import functools

import numpy as np
import jax
import jax.numpy as jnp
from jax import lax
from jax.experimental import pallas as pl
from jax.experimental.pallas import tpu as pltpu

F32 = jnp.float32
BF16 = jnp.bfloat16
I32 = jnp.int32

D_MODEL = 1024
HEADS = 8
NOPE = 64
ROPE = 32
Q_LORA = 512
KV_LORA = 256
DIL_DIM = 64
DIL_W = HEADS * DIL_DIM
D_FF = 2816
UP_W = 2 * D_FF
IN_COLS = Q_LORA + KV_LORA + ROPE + 3 * DIL_W
ROPE_THETA = 10000.0
EPS = 1e-6
NEG_INF = -1e30
N_DEV = 8
N_CHIP = 4

ADAM_LR = 0.001
ADAM_B1 = 0.9
ADAM_B2 = 0.999
ADAM_EPS = 1e-08
ADAM_WD = 0.01
ADAM_STEP = 10

LANE = 128
ROW_TILE = 256
ATT_TILE = 256
VMEM_CAP = 56 * 1024 * 1024
VMEM_FLOOR = 32 * 1024 * 1024

P_QLAT, P_QD, P_KD, P_VD, P_KVLAT, P_KPE = 0, 512, 1024, 1536, 2048, 2304
P_COLS = 2432
KPE_OFF = 64

NN = (((1,), (0,)), ((), ()))
NT = (((1,), (1,)), ((), ()))
TN = (((0,), (0,)), ((), ()))
HIGHEST = lax.Precision.HIGHEST
MESH = pl.DeviceIdType.MESH


def _params(sem=None, est_bytes=0):
    limit = int(min(max(2 * est_bytes + (4 << 20), VMEM_FLOOR), VMEM_CAP))
    if sem is None:
        return pltpu.CompilerParams(vmem_limit_bytes=limit)
    return pltpu.CompilerParams(dimension_semantics=sem, vmem_limit_bytes=limit)


def _nbytes(shape, dtype):
    return int(np.prod(shape)) * jnp.dtype(dtype).itemsize


def _mm(a, b, dims, out_dtype, tm, tn, name):
    if dims == "nn":
        (m, k), (k2, n) = a.shape, b.shape
        a_spec = pl.BlockSpec((tm, k), lambda i, j: (i, 0))
        b_spec = pl.BlockSpec((k, tn), lambda i, j: (0, j))
        dn = NN
    elif dims == "nt":
        (m, k), (n, k2) = a.shape, b.shape
        a_spec = pl.BlockSpec((tm, k), lambda i, j: (i, 0))
        b_spec = pl.BlockSpec((tn, k), lambda i, j: (j, 0))
        dn = NT
    else:
        (k, m), (k2, n) = a.shape, b.shape
        a_spec = pl.BlockSpec((k, tm), lambda i, j: (0, i))
        b_spec = pl.BlockSpec((k, tn), lambda i, j: (0, j))
        dn = TN
    assert k == k2 and m % tm == 0 and n % tn == 0, (name, a.shape, b.shape, tm, tn)

    def body(a_ref, b_ref, o_ref):
        o_ref[...] = lax.dot_general(a_ref[...], b_ref[...], dn, preferred_element_type=F32).astype(o_ref.dtype)

    est = _nbytes((tm, k), a.dtype) + _nbytes((tn, k), b.dtype) + _nbytes((tm, tn), F32) + _nbytes((tm, tn), out_dtype)
    return pl.pallas_call(
        body, name=name,
        grid=(m // tm, n // tn),
        in_specs=[a_spec, b_spec],
        out_specs=pl.BlockSpec((tm, tn), lambda i, j: (i, j)),
        out_shape=jax.ShapeDtypeStruct((m, n), out_dtype),
        compiler_params=_params(("parallel", "parallel"), est),
    )(a, b)


def _seg_consts():
    seg_q = np.zeros((HEADS * LANE, LANE), np.float32)
    inv_q = np.zeros((1, LANE), np.float32)
    seg_k = np.zeros((HEADS * LANE, LANE), np.float32)
    inv_k = np.zeros((1, LANE), np.float32)
    seg_d = np.zeros((DIL_W, LANE), np.float32)
    inv_d = np.zeros((1, LANE), np.float32)
    for h in range(HEADS):
        seg_q[h * LANE:h * LANE + NOPE, 2 * h] = 1.0
        seg_q[h * LANE + NOPE:h * LANE + NOPE + ROPE, 2 * h + 1] = 1.0
        inv_q[0, 2 * h], inv_q[0, 2 * h + 1] = 1.0 / NOPE, 1.0 / ROPE
        seg_k[h * LANE:h * LANE + NOPE, h] = 1.0
        inv_k[0, h] = 1.0 / NOPE
        seg_d[h * DIL_DIM:(h + 1) * DIL_DIM, h] = 1.0
        inv_d[0, h] = 1.0 / DIL_DIM
    fold_q = np.tile(np.eye(LANE, dtype=np.float32), (HEADS, 1))
    fold_d = np.zeros((DIL_W, LANE), np.float32)
    fold_d[np.arange(DIL_W), np.arange(DIL_W) % DIL_DIM] = 1.0
    j = lambda v: jnp.asarray(v)
    b = lambda v: jnp.asarray(v, dtype=BF16)
    return dict(seg_q=b(seg_q), exp_q=b(seg_q.T.copy()), inv_q=j(inv_q), seg_k=b(seg_k), exp_k=b(seg_k.T.copy()),
                inv_k=j(inv_k), seg_d=b(seg_d), exp_d=b(seg_d.T.copy()), inv_d=j(inv_d), fold_q=j(fold_q), fold_d=j(fold_d))


def _rope_consts():
    inv_d = jnp.power(ROPE_THETA, -2.0 * jnp.arange(DIL_DIM // 2, dtype=F32) / DIL_DIM)
    inv_q = jnp.power(ROPE_THETA, -2.0 * jnp.arange(ROPE // 2, dtype=F32) / ROPE)
    lanes = np.arange(LANE)
    freq_d = inv_d[lanes % (DIL_DIM // 2)]
    in_pe = (lanes >= KPE_OFF) & (lanes < KPE_OFF + ROPE)
    freq_q = jnp.where(jnp.asarray(in_pe), inv_q[(lanes - KPE_OFF) % (ROPE // 2)], 0.0)
    sign_d = np.where(lanes % DIL_DIM < DIL_DIM // 2, -1.0, 1.0).astype(np.float32)
    sign_q = np.where(in_pe, np.where((lanes - KPE_OFF) < ROPE // 2, -1.0, 1.0), 0.0).astype(np.float32)
    zeros, ones = np.zeros(LANE, np.float32), np.ones(LANE, np.float32)
    freq = jnp.concatenate([freq_d, freq_d, freq_q, freq_q])[None, :]
    csel = jnp.asarray(np.concatenate([ones, zeros, ones, zeros]))[None, :]
    ssel = jnp.asarray(np.concatenate([zeros, sign_d, zeros, sign_q]))[None, :]
    return freq, csel, ssel


def _full(shape):
    return pl.BlockSpec(shape, lambda *_: (0,) * len(shape))


def _tile_lanes(x, n):
    return jnp.concatenate([x] * n, axis=1)


def _rope_tables(pos_col, freq, csel, ssel):
    s = pos_col.shape[0]

    def body(p_ref, f_ref, c_ref, s_ref, o_ref):
        ang = p_ref[...].astype(F32) * f_ref[...]
        o_ref[...] = c_ref[...] * jnp.cos(ang) + s_ref[...] * jnp.sin(ang)

    return pl.pallas_call(
        body, name="rope_tables", grid=(s // ROW_TILE,),
        in_specs=[pl.BlockSpec((ROW_TILE, 1), lambda i: (i, 0)), _full((1, 4 * LANE)), _full((1, 4 * LANE)), _full((1, 4 * LANE))],
        out_specs=pl.BlockSpec((ROW_TILE, 4 * LANE), lambda i: (i, 0)),
        out_shape=jax.ShapeDtypeStruct((s, 4 * LANE), F32),
        compiler_params=_params(("parallel",)),
    )(pos_col, freq, csel, ssel)


def _rms(x):
    return lax.rsqrt(jnp.mean(x * x, axis=-1, keepdims=True) + EPS)


def _prenorm(x, gain, scale, shift, name):
    s, d = x.shape

    def body(x_ref, g_ref, sc_ref, sh_ref, h_ref):
        xv = x_ref[...]
        h = (xv * _rms(xv)) * g_ref[...] * (1.0 + sc_ref[...]) + sh_ref[...]
        h_ref[...] = h.astype(BF16)

    row = pl.BlockSpec((ROW_TILE, d), lambda i: (i, 0))
    return pl.pallas_call(
        body, name=name, grid=(s // ROW_TILE,),
        in_specs=[row, _full((1, d)), _full((1, d)), _full((1, d))],
        out_specs=row, out_shape=jax.ShapeDtypeStruct((s, d), BF16),
        compiler_params=_params(("parallel",)),
    )(x, gain, scale, shift)


def _latnorm(proj, g_q, g_kv):
    s = proj.shape[0]

    def body(q_ref, kv_ref, gq_ref, gkv_ref, ql_ref, kvl_ref):
        q, kv = q_ref[...], kv_ref[...]
        ql_ref[...] = ((q * _rms(q)) * gq_ref[...]).astype(BF16)
        kvl_ref[...] = ((kv * _rms(kv)) * gkv_ref[...]).astype(BF16)

    return pl.pallas_call(
        body, name="latnorm", grid=(s // ROW_TILE,),
        in_specs=[pl.BlockSpec((ROW_TILE, Q_LORA), lambda i: (i, P_QLAT // Q_LORA)),
                  pl.BlockSpec((ROW_TILE, KV_LORA), lambda i: (i, P_KVLAT // KV_LORA)),
                  _full((1, Q_LORA)), _full((1, KV_LORA))],
        out_specs=[pl.BlockSpec((ROW_TILE, Q_LORA), lambda i: (i, 0)), pl.BlockSpec((ROW_TILE, KV_LORA), lambda i: (i, 0))],
        out_shape=[jax.ShapeDtypeStruct((s, Q_LORA), BF16), jax.ShapeDtypeStruct((s, KV_LORA), BF16)],
        compiler_params=_params(("parallel",)),
    )(proj, proj, g_q, g_kv)


def _dot01(v, mat01):
    hi = v.astype(BF16)
    lo = (v - hi.astype(F32)).astype(BF16)
    return jnp.dot(hi, mat01, preferred_element_type=F32) + jnp.dot(lo, mat01, preferred_element_type=F32)


def _seg_rinv(x, seg, exp, inv):
    r = lax.rsqrt(_dot01(x * x, seg) * inv + EPS)
    return _dot01(r, exp)


def _seg_mean(v, seg, exp, inv):
    return _dot01(_dot01(v, seg) * inv, exp)


def _swap_halves(x, half):
    n = x.shape[1]
    lane = lax.broadcasted_iota(I32, (1, n), 1)
    first = (lane & (2 * half - 1)) < half
    return jnp.where(first, pltpu.roll(x, n - half, 1), pltpu.roll(x, half, 1))


def _rope(x, cos, sin_signed, half):
    return x * cos + _swap_halves(x, half) * sin_signed


def _rope_bwd(dy, cos, sin_signed, half):
    return dy * cos + _swap_halves(dy * sin_signed, half)


def _pe_lane_mask(n):
    lane = lax.broadcasted_iota(I32, (1, n), 1) & (LANE - 1)
    return (lane >= KPE_OFF) & (lane < KPE_OFF + ROPE)


def _attn_prep(q_raw, kv_raw, proj, tab, gains, consts):
    s = q_raw.shape[0]
    hw = HEADS * LANE

    def body(q_ref, kv_ref, kpe_ref, qd_ref, kd_ref, vd_ref, tab_ref,
             gq_ref, gk_ref, gkpe_ref, gdq_ref, gdk_ref,
             segq_ref, expq_ref, invq_ref, segk_ref, expk_ref, invk_ref, segd_ref, expd_ref, invd_ref,
             qm_ref, km_ref, vm_ref, qdo_ref, kdo_ref, vdo_ref):
        tab_v = tab_ref[...]
        cos_d, sin_d = _tile_lanes(tab_v[:, 0:LANE], DIL_W // LANE), _tile_lanes(tab_v[:, LANE:2 * LANE], DIL_W // LANE)
        cos_q1, sin_q1 = tab_v[:, 2 * LANE:3 * LANE], tab_v[:, 3 * LANE:4 * LANE]
        cos_q, sin_q = _tile_lanes(cos_q1, HEADS), _tile_lanes(sin_q1, HEADS)

        q = q_ref[...]
        qn = q * _seg_rinv(q, segq_ref[...], expq_ref[...], invq_ref[...]) * gq_ref[...]
        qm_ref[...] = _rope(qn, cos_q, sin_q, ROPE // 2).astype(BF16)

        kv = kv_ref[...]
        kp = kv[:, :hw]
        kn = kp * _seg_rinv(kp, segk_ref[...], expk_ref[...], invk_ref[...]) * gk_ref[...]
        kpe = kpe_ref[...]
        r_pe = lax.rsqrt(jnp.sum(kpe * kpe, axis=-1, keepdims=True) * (1.0 / ROPE) + EPS)
        kpe_r = _rope(kpe * r_pe * gkpe_ref[...], cos_q1, sin_q1, ROPE // 2)
        km_ref[...] = (kn + _tile_lanes(kpe_r, HEADS)).astype(BF16)
        vm_ref[...] = kv[:, hw:].astype(BF16)

        qd = qd_ref[...]
        qdn = qd * _seg_rinv(qd, segd_ref[...], expd_ref[...], invd_ref[...]) * gdq_ref[...]
        qdo_ref[...] = _rope(qdn, cos_d, sin_d, DIL_DIM // 2).astype(BF16)
        kd = kd_ref[...]
        kdn = kd * _seg_rinv(kd, segd_ref[...], expd_ref[...], invd_ref[...]) * gdk_ref[...]
        kdo_ref[...] = _rope(kdn, cos_d, sin_d, DIL_DIM // 2).astype(BF16)
        vdo_ref[...] = vd_ref[...].astype(BF16)

    t = ROW_TILE
    row = lambda w, cb=0: pl.BlockSpec((t, w), lambda i: (i, cb))
    c = consts
    return pl.pallas_call(
        body, name="attn_prep", grid=(s // t,),
        in_specs=[row(hw), row(hw + DIL_W), row(LANE, P_KPE // LANE), row(DIL_W, P_QD // DIL_W), row(DIL_W, P_KD // DIL_W),
                  row(DIL_W, P_VD // DIL_W), row(4 * LANE),
                  _full((1, hw)), _full((1, hw)), _full((1, LANE)), _full((1, DIL_W)), _full((1, DIL_W)),
                  _full((hw, LANE)), _full((LANE, hw)), _full((1, LANE)), _full((hw, LANE)), _full((LANE, hw)), _full((1, LANE)),
                  _full((DIL_W, LANE)), _full((LANE, DIL_W)), _full((1, LANE))],
        out_specs=[row(hw), row(hw), row(DIL_W), row(DIL_W), row(DIL_W), row(DIL_W)],
        out_shape=[jax.ShapeDtypeStruct((s, hw), BF16), jax.ShapeDtypeStruct((s, hw), BF16)]
        + [jax.ShapeDtypeStruct((s, DIL_W), BF16)] * 4,
        compiler_params=_params(("parallel",), 24 << 20),
    )(q_raw, kv_raw, proj, proj, proj, proj, tab, gains["q"], gains["k"], gains["kpe"], gains["dq"], gains["dk"],
      c["seg_q"], c["exp_q"], c["inv_q"], c["seg_k"], c["exp_k"], c["inv_k"], c["seg_d"], c["exp_d"], c["inv_d"])


def _attn_prep_bwd(dqm, dkm, dvm, dqd, dkd, dvd, q_raw, kv_raw, proj, tab, gains, consts):
    s = q_raw.shape[0]
    hw = HEADS * LANE
    n_steps = s // ROW_TILE

    def body(dqm_ref, dkm_ref, dvm_ref, dqd_ref, dkd_ref, dvd_ref, q_ref, kv_ref, kpe_ref, qd_ref, kd_ref, tab_ref,
             gq_ref, gk_ref, gkpe_ref, gdq_ref, gdk_ref,
             segq_ref, expq_ref, invq_ref, segk_ref, expk_ref, invk_ref, segd_ref, expd_ref, invd_ref, foldq_ref, foldd_ref,
             dq_ref, dkv_ref, dkpe_ref, dqdo_ref, dkdo_ref, dvdo_ref, dg_ref, acc_ref):
        i = pl.program_id(0)

        @pl.when(i == 0)
        def _():
            acc_ref[...] = jnp.zeros_like(acc_ref)

        tab_v = tab_ref[...]
        cos_d, sin_d = _tile_lanes(tab_v[:, 0:LANE], DIL_W // LANE), _tile_lanes(tab_v[:, LANE:2 * LANE], DIL_W // LANE)
        cos_q1, sin_q1 = tab_v[:, 2 * LANE:3 * LANE], tab_v[:, 3 * LANE:4 * LANE]
        cos_q, sin_q = _tile_lanes(cos_q1, HEADS), _tile_lanes(sin_q1, HEADS)

        def norm_bwd(x, dyg, gain, seg, exp, inv):
            rinv = _seg_rinv(x, seg, exp, inv)
            xn = x * rinv
            dxn = dyg * gain
            dx = rinv * (dxn - xn * _seg_mean(dxn * xn, seg, exp, inv))
            return dx, jnp.sum(dyg * xn, axis=0, keepdims=True)

        dq, gq_l = norm_bwd(q_ref[...], _rope_bwd(dqm_ref[...], cos_q, sin_q, ROPE // 2), gq_ref[...],
                            segq_ref[...], expq_ref[...], invq_ref[...])
        dq_ref[...] = dq.astype(BF16)

        dkm = dkm_ref[...]
        kv = kv_ref[...]
        dkp, gk_l = norm_bwd(kv[:, :hw], dkm, gk_ref[...], segk_ref[...], expk_ref[...], invk_ref[...])
        dkv_ref[:, :hw] = dkp.astype(BF16)
        dkv_ref[:, hw:] = dvm_ref[...].astype(BF16)

        dkpe_r = dkm[:, 0:LANE]
        for h in range(1, HEADS):
            dkpe_r = dkpe_r + dkm[:, h * LANE:(h + 1) * LANE]
        dkpe_r = jnp.where(_pe_lane_mask(LANE), dkpe_r, 0.0)
        dyg = _rope_bwd(dkpe_r, cos_q1, sin_q1, ROPE // 2)
        kpe = kpe_ref[...]
        r_pe = lax.rsqrt(jnp.sum(kpe * kpe, axis=-1, keepdims=True) * (1.0 / ROPE) + EPS)
        xn = kpe * r_pe
        dxn = dyg * gkpe_ref[...]
        dkpe = r_pe * (dxn - xn * (jnp.sum(dxn * xn, axis=-1, keepdims=True) * (1.0 / ROPE)))
        dkpe_ref[...] = dkpe.astype(BF16)
        gkpe_l = jnp.sum(dyg * xn, axis=0, keepdims=True)

        dqd_v, gdq_l = norm_bwd(qd_ref[...], _rope_bwd(dqd_ref[...], cos_d, sin_d, DIL_DIM // 2), gdq_ref[...],
                                segd_ref[...], expd_ref[...], invd_ref[...])
        dqdo_ref[...] = dqd_v.astype(BF16)
        dkd_v, gdk_l = norm_bwd(kd_ref[...], _rope_bwd(dkd_ref[...], cos_d, sin_d, DIL_DIM // 2), gdk_ref[...],
                                segd_ref[...], expd_ref[...], invd_ref[...])
        dkdo_ref[...] = dkd_v.astype(BF16)
        dvdo_ref[...] = dvd_ref[...].astype(BF16)

        acc_ref[0:1, :] += gq_l
        acc_ref[1:2, :] += gk_l
        acc_ref[2:3, 0:LANE] += gkpe_l
        acc_ref[3:4, 0:DIL_W] += gdq_l
        acc_ref[4:5, 0:DIL_W] += gdk_l

        @pl.when(i == n_steps - 1)
        def _():
            acc = acc_ref[...]
            fq = jnp.dot(acc, foldq_ref[...], precision=HIGHEST, preferred_element_type=F32)
            fd = jnp.dot(acc[:, 0:DIL_W], foldd_ref[...], precision=HIGHEST, preferred_element_type=F32)
            rows = lax.broadcasted_iota(I32, (8, LANE), 0)
            dg_ref[...] = jnp.where(rows < 2, fq, jnp.where(rows == 2, acc[:, 0:LANE], fd))

    t = ROW_TILE
    row = lambda w, cb=0: pl.BlockSpec((t, w), lambda i: (i, cb))
    c = consts
    return pl.pallas_call(
        body, name="attn_prep_bwd", grid=(n_steps,),
        in_specs=[row(hw), row(hw), row(DIL_W), row(DIL_W), row(DIL_W), row(DIL_W),
                  row(hw), row(hw + DIL_W), row(LANE, P_KPE // LANE), row(DIL_W, P_QD // DIL_W), row(DIL_W, P_KD // DIL_W),
                  row(4 * LANE),
                  _full((1, hw)), _full((1, hw)), _full((1, LANE)), _full((1, DIL_W)), _full((1, DIL_W)),
                  _full((hw, LANE)), _full((LANE, hw)), _full((1, LANE)), _full((hw, LANE)), _full((LANE, hw)), _full((1, LANE)),
                  _full((DIL_W, LANE)), _full((LANE, DIL_W)), _full((1, LANE)), _full((hw, LANE)), _full((DIL_W, LANE))],
        out_specs=[row(hw), row(hw + DIL_W), row(LANE), row(DIL_W), row(DIL_W), row(DIL_W), _full((8, LANE))],
        out_shape=[jax.ShapeDtypeStruct((s, hw), BF16), jax.ShapeDtypeStruct((s, hw + DIL_W), BF16),
                   jax.ShapeDtypeStruct((s, LANE), BF16)] + [jax.ShapeDtypeStruct((s, DIL_W), BF16)] * 3
        + [jax.ShapeDtypeStruct((8, LANE), F32)],
        scratch_shapes=[pltpu.VMEM((8, hw), F32)],
        compiler_params=_params(("arbitrary",), 28 << 20),
    )(dqm, dkm, dvm, dqd, dkd, dvd, q_raw, kv_raw, proj, proj, proj, tab,
      gains["q"], gains["k"], gains["kpe"], gains["dq"], gains["dk"],
      c["seg_q"], c["exp_q"], c["inv_q"], c["seg_k"], c["exp_k"], c["inv_k"], c["seg_d"], c["exp_d"], c["inv_d"],
      c["fold_q"], c["fold_d"])


def _latnorm_bwd(dql, dkvl, proj, g_q, g_kv):
    s = proj.shape[0]
    n_steps = s // ROW_TILE

    def body(dql_ref, dkvl_ref, q_ref, kv_ref, gq_ref, gkv_ref, dq_ref, dkv_ref, dg_ref):
        i = pl.program_id(0)

        @pl.when(i == 0)
        def _():
            dg_ref[...] = jnp.zeros_like(dg_ref)

        def one(x, dyg, gain):
            r = _rms(x)
            xn = x * r
            dxn = dyg * gain
            dx = r * (dxn - xn * jnp.mean(dxn * xn, axis=-1, keepdims=True))
            return dx, jnp.sum(dyg * xn, axis=0, keepdims=True)

        dq, gq_l = one(q_ref[...], dql_ref[...], gq_ref[...])
        dkv, gkv_l = one(kv_ref[...], dkvl_ref[...], gkv_ref[...])
        dq_ref[...] = dq.astype(BF16)
        dkv_ref[...] = dkv.astype(BF16)
        dg_ref[0:1, :] += gq_l
        dg_ref[1:2, 0:KV_LORA] += gkv_l

    t = ROW_TILE
    return pl.pallas_call(
        body, name="latnorm_bwd", grid=(n_steps,),
        in_specs=[pl.BlockSpec((t, Q_LORA), lambda i: (i, 0)), pl.BlockSpec((t, KV_LORA), lambda i: (i, 0)),
                  pl.BlockSpec((t, Q_LORA), lambda i: (i, P_QLAT // Q_LORA)),
                  pl.BlockSpec((t, KV_LORA), lambda i: (i, P_KVLAT // KV_LORA)),
                  _full((1, Q_LORA)), _full((1, KV_LORA))],
        out_specs=[pl.BlockSpec((t, Q_LORA), lambda i: (i, 0)), pl.BlockSpec((t, KV_LORA), lambda i: (i, 0)), _full((8, Q_LORA))],
        out_shape=[jax.ShapeDtypeStruct((s, Q_LORA), BF16), jax.ShapeDtypeStruct((s, KV_LORA), BF16),
                   jax.ShapeDtypeStruct((8, Q_LORA), F32)],
        compiler_params=_params(("arbitrary",)),
    )(dql, dkvl, proj, proj, g_q, g_kv)


def _resid_prenorm(x, mix, g1, gain, scale, shift):
    s, d = x.shape

    def body(x_ref, mix_ref, g1_ref, g_ref, sc_ref, sh_ref, x1_ref, h_ref):
        x1 = x_ref[...] + g1_ref[...] * mix_ref[...]
        x1_ref[...] = x1
        h_ref[...] = ((x1 * _rms(x1)) * g_ref[...] * (1.0 + sc_ref[...]) + sh_ref[...]).astype(BF16)

    row = pl.BlockSpec((ROW_TILE, d), lambda i: (i, 0))
    vec = _full((1, d))
    return pl.pallas_call(
        body, name="resid_prenorm", grid=(s // ROW_TILE,),
        in_specs=[row, row, vec, vec, vec, vec], out_specs=[row, row],
        out_shape=[jax.ShapeDtypeStruct((s, d), F32), jax.ShapeDtypeStruct((s, d), BF16)],
        compiler_params=_params(("parallel",)),
    )(x, mix, g1, gain, scale, shift)


CONV_TILE = 1408
HALO = 8


def _shift_down(x, halo, k):
    t = x.shape[0]
    row = lax.broadcasted_iota(I32, (t, 1), 0)
    out = pltpu.roll(x, k, 0)
    for r in range(k):
        out = jnp.where(row == r, halo[HALO - k + r:HALO - k + r + 1, :], out)
    return out


def _shift_up(x, halo, k):
    t = x.shape[0]
    row = lax.broadcasted_iota(I32, (t, 1), 0)
    out = pltpu.roll(x, t - k, 0)
    for r in range(k):
        out = jnp.where(row == t - k + r, halo[r:r + 1, :], out)
    return out


def _conv_fwd(x, halo, w, b):
    p1, p2 = _shift_down(x, halo, 1), _shift_down(x, halo, 2)
    u = b + p2 * w[0:1, :]
    u = u + p1 * w[1:2, :]
    u = u + x * w[2:3, :]
    return u, p1, p2


def _sigmoid(x):
    return 1.0 / (1.0 + jnp.exp(-x))


def _conv_gate(up, w_conv, b_conv):
    s = up.shape[0]
    t = ROW_TILE
    nj = D_FF // CONV_TILE
    hb = t // HALO

    def body(g_ref, v_ref, gh_ref, vh_ref, wg_ref, wv_ref, bg_ref, bv_ref, a_ref):
        live = (pl.program_id(0) > 0).astype(F32)
        ug, _, _ = _conv_fwd(g_ref[...], gh_ref[...] * live, wg_ref[...], bg_ref[...])
        uv, _, _ = _conv_fwd(v_ref[...], vh_ref[...] * live, wv_ref[...], bv_ref[...])
        a_ref[...] = (ug * _sigmoid(ug) * uv).astype(BF16)

    main = lambda off: pl.BlockSpec((t, CONV_TILE), lambda i, j: (i, j + off))
    halo = lambda off: pl.BlockSpec((HALO, CONV_TILE), lambda i, j: (jnp.maximum(i * hb - 1, 0), j + off))
    wsp = lambda off: pl.BlockSpec((3, CONV_TILE), lambda i, j: (0, j + off))
    bsp = lambda off: pl.BlockSpec((1, CONV_TILE), lambda i, j: (0, j + off))
    return pl.pallas_call(
        body, name="conv_gate", grid=(s // t, nj),
        in_specs=[main(0), main(nj), halo(0), halo(nj), wsp(0), wsp(nj), bsp(0), bsp(nj)],
        out_specs=pl.BlockSpec((t, CONV_TILE), lambda i, j: (i, j)),
        out_shape=jax.ShapeDtypeStruct((s, D_FF), BF16),
        compiler_params=_params(("parallel", "parallel"), 12 << 20),
    )(up, up, up, up, w_conv, w_conv, b_conv, b_conv)


def _gate_bwd(up, da, w_conv, b_conv):
    s = up.shape[0]
    t = ROW_TILE
    nj = D_FF // CONV_TILE
    hb = t // HALO

    def body(g_ref, v_ref, gh_ref, vh_ref, da_ref, wg_ref, wv_ref, bg_ref, bv_ref,
             dug_ref, duv_ref, dbg_ref, dbv_ref, dwg_ref, dwv_ref):
        i = pl.program_id(1)

        @pl.when(i == 0)
        def _():
            for r in (dbg_ref, dbv_ref, dwg_ref, dwv_ref):
                r[...] = jnp.zeros_like(r)

        live = (i > 0).astype(F32)
        xg, xv = g_ref[...], v_ref[...]
        ug, g1, g2 = _conv_fwd(xg, gh_ref[...] * live, wg_ref[...], bg_ref[...])
        uv, v1, v2 = _conv_fwd(xv, vh_ref[...] * live, wv_ref[...], bv_ref[...])
        sg = _sigmoid(ug)
        da_v = da_ref[...]
        dug = da_v * uv * (sg * (1.0 + ug * (1.0 - sg)))
        duv = da_v * (ug * sg)
        dug_ref[...] = dug
        duv_ref[...] = duv
        csum = lambda z: jnp.sum(z, axis=0, keepdims=True)
        dbg_ref[...] += csum(dug)
        dbv_ref[...] += csum(duv)
        dwg_ref[0:1, :] += csum(dug * g2)
        dwg_ref[1:2, :] += csum(dug * g1)
        dwg_ref[2:3, :] += csum(dug * xg)
        dwv_ref[0:1, :] += csum(duv * v2)
        dwv_ref[1:2, :] += csum(duv * v1)
        dwv_ref[2:3, :] += csum(duv * xv)

    main = lambda off: pl.BlockSpec((t, CONV_TILE), lambda j, i: (i, j + off))
    halo = lambda off: pl.BlockSpec((HALO, CONV_TILE), lambda j, i: (jnp.maximum(i * hb - 1, 0), j + off))
    wsp = lambda off: pl.BlockSpec((3, CONV_TILE), lambda j, i: (0, j + off))
    bsp = lambda off: pl.BlockSpec((1, CONV_TILE), lambda j, i: (0, j + off))
    outs = pl.pallas_call(
        body, name="gate_bwd", grid=(nj, s // t),
        in_specs=[main(0), main(nj), halo(0), halo(nj), pl.BlockSpec((t, CONV_TILE), lambda j, i: (i, j)),
                  wsp(0), wsp(nj), bsp(0), bsp(nj)],
        out_specs=[pl.BlockSpec((t, CONV_TILE), lambda j, i: (i, j)), pl.BlockSpec((t, CONV_TILE), lambda j, i: (i, j)),
                   pl.BlockSpec((1, CONV_TILE), lambda j, i: (0, j)), pl.BlockSpec((1, CONV_TILE), lambda j, i: (0, j)),
                   pl.BlockSpec((3, CONV_TILE), lambda j, i: (0, j)), pl.BlockSpec((3, CONV_TILE), lambda j, i: (0, j))],
        out_shape=[jax.ShapeDtypeStruct((s, D_FF), F32), jax.ShapeDtypeStruct((s, D_FF), F32),
                   jax.ShapeDtypeStruct((1, D_FF), F32), jax.ShapeDtypeStruct((1, D_FF), F32),
                   jax.ShapeDtypeStruct((3, D_FF), F32), jax.ShapeDtypeStruct((3, D_FF), F32)],
        compiler_params=_params(("parallel", "arbitrary"), 20 << 20),
    )(up, up, up, up, da, w_conv, w_conv, b_conv, b_conv)
    return outs


def _conv_bwd(du, w_half, name):
    s = du.shape[0]
    t = ROW_TILE
    nj = D_FF // CONV_TILE
    hb = t // HALO
    n_i = s // t

    def body(d_ref, h_ref, w_ref, o_ref):
        live = (pl.program_id(0) < n_i - 1).astype(F32)
        x = d_ref[...]
        halo = h_ref[...] * live
        w = w_ref[...]
        o = x * w[2:3, :] + _shift_up(x, halo, 1) * w[1:2, :] + _shift_up(x, halo, 2) * w[0:1, :]
        o_ref[...] = o.astype(BF16)

    return pl.pallas_call(
        body, name=name, grid=(n_i, nj),
        in_specs=[pl.BlockSpec((t, CONV_TILE), lambda i, j: (i, j)),
                  pl.BlockSpec((HALO, CONV_TILE), lambda i, j: (jnp.minimum((i + 1) * hb, s // HALO - 1), j)),
                  pl.BlockSpec((3, CONV_TILE), lambda i, j: (0, j))],
        out_specs=pl.BlockSpec((t, CONV_TILE), lambda i, j: (i, j)),
        out_shape=jax.ShapeDtypeStruct((s, D_FF), BF16),
        compiler_params=_params(("parallel", "parallel"), 8 << 20),
    )(du, du, w_half)


def _final(x1, ffn, tgt, g2):
    s, d = x1.shape
    n_steps = s // ROW_TILE

    def body(x1_ref, f_ref, t_ref, g2_ref, dy_ref, df_ref, dg2_ref, loss_ref, lacc_ref):
        i = pl.program_id(0)

        @pl.when(i == 0)
        def _():
            dg2_ref[...] = jnp.zeros_like(dg2_ref)
            lacc_ref[...] = jnp.zeros_like(lacc_ref)

        f = f_ref[...]
        e = x1_ref[...] + g2_ref[...] * f - t_ref[...]
        dy = e * (1.0 / d)
        dy_ref[...] = dy
        df_ref[...] = (dy * g2_ref[...]).astype(BF16)
        dg2_ref[...] += jnp.sum(dy * f, axis=0, keepdims=True)
        lacc_ref[...] += jnp.sum(e * e, axis=0, keepdims=True)

        @pl.when(i == n_steps - 1)
        def _():
            loss_ref[...] = jnp.sum(lacc_ref[...], axis=1, keepdims=True) * (0.5 / d)

    row = pl.BlockSpec((ROW_TILE, d), lambda i: (i, 0))
    return pl.pallas_call(
        body, name="final", grid=(n_steps,),
        in_specs=[row, row, row, _full((1, d))],
        out_specs=[row, row, _full((1, d)), _full((1, 1))],
        out_shape=[jax.ShapeDtypeStruct((s, d), F32), jax.ShapeDtypeStruct((s, d), BF16),
                   jax.ShapeDtypeStruct((1, d), F32), jax.ShapeDtypeStruct((1, 1), F32)],
        scratch_shapes=[pltpu.VMEM((1, d), F32)],
        compiler_params=_params(("arbitrary",)),
    )(x1, ffn, tgt, g2)


def _ffnnorm_bwd(dh2, x1, dy, mix, gain, scale, g1):
    s, d = x1.shape
    n_steps = s // ROW_TILE

    def body(dh_ref, x_ref, dy_ref, mix_ref, g_ref, sc_ref, g1_ref, dx_ref, dm_ref, acc_ref):
        i = pl.program_id(0)

        @pl.when(i == 0)
        def _():
            acc_ref[...] = jnp.zeros_like(acc_ref)

        dh, x = dh_ref[...], x_ref[...]
        r = _rms(x)
        xn = x * r
        dn = dh * (1.0 + sc_ref[...])
        dxn = dn * g_ref[...]
        dx = dy_ref[...] + r * (dxn - xn * jnp.mean(dxn * xn, axis=-1, keepdims=True))
        dx_ref[...] = dx
        dm_ref[...] = (dx * g1_ref[...]).astype(BF16)
        csum = lambda z: jnp.sum(z, axis=0, keepdims=True)
        acc_ref[0:1, :] += csum(dh)
        acc_ref[1:2, :] += csum(dh * (xn * g_ref[...]))
        acc_ref[2:3, :] += csum(dn * xn)
        acc_ref[3:4, :] += csum(dx * mix_ref[...])

    row = pl.BlockSpec((ROW_TILE, d), lambda i: (i, 0))
    vec = _full((1, d))
    return pl.pallas_call(
        body, name="ffnnorm_bwd", grid=(n_steps,),
        in_specs=[row, row, row, row, vec, vec, vec],
        out_specs=[row, row, _full((8, d))],
        out_shape=[jax.ShapeDtypeStruct((s, d), F32), jax.ShapeDtypeStruct((s, d), BF16), jax.ShapeDtypeStruct((8, d), F32)],
        compiler_params=_params(("arbitrary",)),
    )(dh2, x1, dy, mix, gain, scale, g1)


def _mixnorm_bwd(dh, x, dx1, gain, scale):
    s, d = x.shape
    n_steps = s // ROW_TILE

    def body(dh_ref, x_ref, dx1_ref, g_ref, sc_ref, gx_ref, acc_ref):
        i = pl.program_id(0)

        @pl.when(i == 0)
        def _():
            acc_ref[...] = jnp.zeros_like(acc_ref)

        dh, x = dh_ref[...], x_ref[...]
        r = _rms(x)
        xn = x * r
        dn = dh * (1.0 + sc_ref[...])
        dxn = dn * g_ref[...]
        gx_ref[...] = dx1_ref[...] + r * (dxn - xn * jnp.mean(dxn * xn, axis=-1, keepdims=True))
        csum = lambda z: jnp.sum(z, axis=0, keepdims=True)
        acc_ref[0:1, :] += csum(dh)
        acc_ref[1:2, :] += csum(dh * (xn * g_ref[...]))
        acc_ref[2:3, :] += csum(dn * xn)

    row = pl.BlockSpec((ROW_TILE, d), lambda i: (i, 0))
    vec = _full((1, d))
    return pl.pallas_call(
        body, name="mixnorm_bwd", grid=(n_steps,),
        in_specs=[row, row, row, vec, vec],
        out_specs=[row, _full((8, d))],
        out_shape=[jax.ShapeDtypeStruct((s, d), F32), jax.ShapeDtypeStruct((8, d), F32)],
        compiler_params=_params(("arbitrary",)),
    )(dh, x, dx1, gain, scale)


def _key_count(d, dilated):
    if not dilated:
        return jnp.where(d >= 0, 1.0, 0.0)
    one = lambda cond: jnp.where(cond, 1.0, 0.0)
    cnt = one(d <= 128) + one(((d & 3) == 0) & (d <= 512)) + one((d & 15) == 0)
    return jnp.where(d >= 0, cnt, 0.0)


def _block_kinds(mla):
    if mla:
        return 1, "diag", "none"
    near = -(-(512 + ATT_TILE) // ATT_TILE)
    return near, "near", "far"


def _scores_t(ka, qa, scale, kind, rel_t, offset):
    st = lax.dot_general(ka, qa, NT, preferred_element_type=F32) * scale
    cnt = None
    if kind == "diag":
        st = jnp.where(rel_t >= 0, st, NEG_INF)
    elif kind == "far":
        st = jnp.where((rel_t & 15) == 0, st, NEG_INF)
    elif kind == "near":
        cnt = _key_count(rel_t + offset, True)
        st = jnp.where(cnt > 0.0, st, NEG_INF)
    return st, cnt


def _attn_fwd(q, k, v, mla, scale, name):
    s = q.shape[0]
    qw = 2 * LANE if mla else LANE
    t = ATT_TILE
    nq = s // t
    n_near, kind_near, kind_far = _block_kinds(mla)

    def body(q_ref, k_ref, v_ref, o_ref, lse_ref, vt_ref):
        lane = lax.broadcasted_iota(I32, (1, LANE), 1)
        rel_t = lax.broadcasted_iota(I32, (t, t), 1) - lax.broadcasted_iota(I32, (t, t), 0)

        def transpose_v(j, carry):
            c0 = pl.multiple_of(j * t, t)
            vt_ref[:, pl.ds(c0, t)] = v_ref[pl.ds(c0, t), :].astype(F32).T.astype(BF16)
            return carry

        lax.fori_loop(0, nq, transpose_v, 0)

        def q_block(qi, carry):
            r0 = pl.multiple_of(qi * t, t)
            res = []
            for a in range(2):
                kcols = slice(a * LANE, (a + 1) * LANE) if mla else slice(0, LANE)
                qa = q_ref[pl.ds(r0, t), kcols]
                if not mla:
                    sel = (lane < DIL_DIM) if a == 0 else (lane >= DIL_DIM)
                    qa = jnp.where(sel, qa, jnp.zeros_like(qa))

                def k_block(kj, c, kind, a=a, qa=qa, kcols=kcols):
                    m, l, acc = c
                    c0 = pl.multiple_of(kj * t, t)
                    st, cnt = _scores_t(k_ref[pl.ds(c0, t), kcols], qa, scale, kind, rel_t, r0 - c0)
                    m_new = jnp.maximum(m, jnp.max(st, axis=0, keepdims=True))
                    alpha = jnp.exp(m - m_new)
                    p = jnp.exp(st - m_new)
                    if cnt is not None:
                        p = p * cnt
                    l = alpha * l + jnp.sum(p, axis=0, keepdims=True)
                    vt = vt_ref[a * DIL_DIM:(a + 1) * DIL_DIM, pl.ds(c0, t)]
                    acc = alpha * acc + jnp.dot(vt, p.astype(BF16), preferred_element_type=F32)
                    return m_new, l, acc

                init = (jnp.full((1, t), NEG_INF, F32), jnp.zeros((1, t), F32), jnp.zeros((DIL_DIM, t), F32))
                first_near = jnp.maximum(qi + 1 - n_near, 0)
                c = lax.fori_loop(0, first_near, functools.partial(k_block, kind=kind_far), init)
                m, l, acc = lax.fori_loop(first_near, qi + 1, functools.partial(k_block, kind=kind_near), c)
                res.append((acc / l, m + jnp.log(l)))
            o_t = jnp.concatenate([res[0][0], res[1][0]], axis=0)
            o_ref[pl.ds(r0, t), :] = o_t.T.astype(BF16)
            lse_ref[0, :, pl.ds(r0, t)] = res[0][1]
            lse_ref[1, :, pl.ds(r0, t)] = res[1][1]
            return carry

        lax.fori_loop(0, nq, q_block, 0)

    return pl.pallas_call(
        body, name=name, grid=(HEADS // 2,),
        in_specs=[pl.BlockSpec((s, qw), lambda h: (0, h)), pl.BlockSpec((s, qw), lambda h: (0, h)),
                  pl.BlockSpec((s, LANE), lambda h: (0, h))],
        out_specs=[pl.BlockSpec((s, LANE), lambda h: (0, h)), pl.BlockSpec((2, 1, s), lambda h: (h, 0, 0))],
        out_shape=[jax.ShapeDtypeStruct((s, DIL_W), BF16), jax.ShapeDtypeStruct((HEADS, 1, s), F32)],
        scratch_shapes=[pltpu.VMEM((LANE, s), BF16)],
        compiler_params=_params(("parallel",), 12 << 20),
    )(q, k, v)


def _attn_bwd(q, k, v, o, do, do_block0, lse, mla, scale, name):
    s = q.shape[0]
    qw = 2 * LANE if mla else LANE
    t = ATT_TILE
    nq = s // t
    n_near, kind_near, kind_far = _block_kinds(mla)

    def body(q_ref, k_ref, v_ref, o_ref, do_ref, lse_ref, dq_ref, dk_ref, dv_ref, kt_ref, dot_ref, dob_ref, dqt_ref, delta_ref):
        lane = lax.broadcasted_iota(I32, (1, LANE), 1)
        row = lax.broadcasted_iota(I32, (LANE, 1), 0)
        rel_t = lax.broadcasted_iota(I32, (t, t), 1) - lax.broadcasted_iota(I32, (t, t), 0)

        def prepare(j, carry):
            c0 = pl.multiple_of(j * t, t)
            do_blk = do_ref[pl.ds(c0, t), :]
            dob_ref[pl.ds(c0, t), :] = do_blk.astype(BF16)
            do_t = do_blk.T
            dot_ref[:, pl.ds(c0, t)] = do_t.astype(BF16)
            prod = do_t * o_ref[pl.ds(c0, t), :].astype(F32).T
            delta_ref[0, :, pl.ds(c0, t)] = jnp.sum(prod[0:DIL_DIM], axis=0, keepdims=True)
            delta_ref[1, :, pl.ds(c0, t)] = jnp.sum(prod[DIL_DIM:LANE], axis=0, keepdims=True)
            for w in range(qw // LANE):
                kt_ref[w * LANE:(w + 1) * LANE, pl.ds(c0, t)] = (
                    k_ref[pl.ds(c0, t), w * LANE:(w + 1) * LANE].astype(F32).T.astype(BF16))
            return carry

        lax.fori_loop(0, nq, prepare, 0)
        dqt_ref[...] = jnp.zeros_like(dqt_ref)

        for a in range(2):
            sel = (lane < DIL_DIM) if a == 0 else (lane >= DIL_DIM)
            rsel = (row < DIL_DIM) if a == 0 else (row >= DIL_DIM)
            cols = slice(a * LANE, (a + 1) * LANE) if mla else slice(0, LANE)

            def k_block(kj, carry, a=a, sel=sel, rsel=rsel, cols=cols):
                c0 = pl.multiple_of(kj * t, t)
                ka = k_ref[pl.ds(c0, t), cols]
                kt = kt_ref[cols, pl.ds(c0, t)]
                if not mla:
                    ka = jnp.where(sel, ka, jnp.zeros_like(ka))
                    kt = jnp.where(rsel, kt, jnp.zeros_like(kt))
                vb = v_ref[pl.ds(c0, t), :]
                vb = jnp.where(sel, vb, jnp.zeros_like(vb))

                def q_block(qi, c, kind):
                    dk_acc, dv_acc = c
                    r0 = pl.multiple_of(qi * t, t)
                    qa = q_ref[pl.ds(r0, t), cols]
                    st, cnt = _scores_t(ka, qa, scale, kind, rel_t, r0 - c0)
                    p = jnp.exp(st - lse_ref[a, :, pl.ds(r0, t)])
                    if cnt is not None:
                        p = p * cnt
                    dp = jnp.dot(vb, dot_ref[:, pl.ds(r0, t)], preferred_element_type=F32)
                    ds = (p * (dp - delta_ref[a, :, pl.ds(r0, t)]) * scale).astype(BF16)
                    dv_acc = dv_acc + jnp.dot(p.astype(BF16), dob_ref[pl.ds(r0, t), :], preferred_element_type=F32)
                    dk_acc = dk_acc + jnp.dot(ds, qa, preferred_element_type=F32)
                    dqt_ref[cols, pl.ds(r0, t)] += jnp.dot(kt, ds, preferred_element_type=F32)
                    return dk_acc, dv_acc

                zero = jnp.zeros((t, LANE), F32)
                last_near = jnp.minimum(kj + n_near, nq)
                c = lax.fori_loop(kj, last_near, functools.partial(q_block, kind=kind_near), (zero, zero))
                dk_acc, dv_acc = lax.fori_loop(last_near, nq, functools.partial(q_block, kind=kind_far), c)
                if mla:
                    dk_ref[pl.ds(c0, t), cols] = dk_acc
                elif a == 0:
                    dk_ref[pl.ds(c0, t), :] = jnp.where(sel, dk_acc, 0.0)
                else:
                    dk_ref[pl.ds(c0, t), :] += jnp.where(sel, dk_acc, 0.0)
                if a == 0:
                    dv_ref[pl.ds(c0, t), :] = jnp.where(sel, dv_acc, 0.0)
                else:
                    dv_ref[pl.ds(c0, t), :] += jnp.where(sel, dv_acc, 0.0)
                return carry

            lax.fori_loop(0, nq, k_block, 0)

        def write_dq(j, carry):
            c0 = pl.multiple_of(j * t, t)
            for w in range(qw // LANE):
                dq_ref[pl.ds(c0, t), w * LANE:(w + 1) * LANE] = dqt_ref[w * LANE:(w + 1) * LANE, pl.ds(c0, t)].T
            return carry

        lax.fori_loop(0, nq, write_dq, 0)

    b0 = do_block0
    return pl.pallas_call(
        body, name=name, grid=(HEADS // 2,),
        in_specs=[pl.BlockSpec((s, qw), lambda h: (0, h)), pl.BlockSpec((s, qw), lambda h: (0, h)),
                  pl.BlockSpec((s, LANE), lambda h: (0, h)), pl.BlockSpec((s, LANE), lambda h: (0, h)),
                  pl.BlockSpec((s, LANE), lambda h: (0, h + b0)), pl.BlockSpec((2, 1, s), lambda h: (h, 0, 0))],
        out_specs=[pl.BlockSpec((s, qw), lambda h: (0, h)), pl.BlockSpec((s, qw), lambda h: (0, h)),
                   pl.BlockSpec((s, LANE), lambda h: (0, h))],
        out_shape=[jax.ShapeDtypeStruct(q.shape, F32), jax.ShapeDtypeStruct(k.shape, F32), jax.ShapeDtypeStruct((s, DIL_W), F32)],
        scratch_shapes=[pltpu.VMEM((qw, s), BF16), pltpu.VMEM((LANE, s), BF16), pltpu.VMEM((s, LANE), BF16),
                        pltpu.VMEM((qw, s), F32), pltpu.VMEM((2, 1, s), F32)],
        compiler_params=_params(("parallel",), 24 << 20),
    )(q, k, v, o, do, lse)


def _ada_fwd(c_all, w_shard, b_shard):
    n, d = c_all.shape
    cols = w_shard.shape[1]

    def body(c_ref, w_ref, b_ref, o_ref):
        cv = c_ref[...]
        sc = (cv * _sigmoid(cv)).astype(BF16)
        o_ref[...] = jnp.dot(sc, w_ref[...].astype(BF16), preferred_element_type=F32) + b_ref[...]

    return pl.pallas_call(
        body, name="ada_fwd", out_shape=jax.ShapeDtypeStruct((n, cols), F32),
        compiler_params=_params(None, 16 << 20),
    )(c_all, w_shard, b_shard)


def _ada_bwd(c_all, dmod_shard):
    n, d = c_all.shape
    cols = dmod_shard.shape[1]

    def body(c_ref, g_ref, o_ref):
        cv = c_ref[...]
        o_ref[...] = lax.dot_general(cv * _sigmoid(cv), g_ref[...], TN, precision=HIGHEST, preferred_element_type=F32)

    return pl.pallas_call(
        body, name="ada_bwd", out_shape=jax.ShapeDtypeStruct((d, cols), F32),
        compiler_params=_params(None, 16 << 20),
    )(c_all, dmod_shard)


def _sum_devices(g):
    n, r, w = g.shape

    def body(g_ref, o_ref):
        acc = g_ref[0]
        for k in range(1, n):
            acc = acc + g_ref[k]
        o_ref[...] = acc

    return pl.pallas_call(
        body, name="sum_devices", out_shape=jax.ShapeDtypeStruct((r, w), F32),
        compiler_params=_params(None, 4 << 20),
    )(g)


def _adamw(w, g, m, v, name):
    r, c = w.shape
    tr = r
    for cand in (256, 128, 64, 32, 16, 8):
        if r % cand == 0 and r > cand:
            tr = cand
            break

    def body(w_ref, g_ref, m_ref, v_ref, d_ref, mo_ref, vo_ref):
        gv = g_ref[...]
        mn = ADAM_B1 * m_ref[...] + (1.0 - ADAM_B1) * gv
        vn = ADAM_B2 * v_ref[...] + (1.0 - ADAM_B2) * (gv * gv)
        m_hat = mn / (1.0 - ADAM_B1 ** ADAM_STEP)
        v_hat = vn / (1.0 - ADAM_B2 ** ADAM_STEP)
        d_ref[...] = -ADAM_LR * (m_hat / (jnp.sqrt(v_hat) + ADAM_EPS) + ADAM_WD * w_ref[...])
        mo_ref[...] = mn
        vo_ref[...] = vn

    blk = pl.BlockSpec((tr, c), lambda i: (i, 0))
    return pl.pallas_call(
        body, name=name, grid=(r // tr,), in_specs=[blk] * 4, out_specs=[blk] * 3,
        out_shape=[jax.ShapeDtypeStruct((r, c), F32)] * 3,
        compiler_params=_params(("parallel",), 7 * _nbytes((tr, c), F32)),
    )(w, g, m, v)


def _position():
    return lax.axis_index("x"), lax.axis_index("y"), lax.axis_index("c")


def _other_chips(x, y):
    return [(1 - x, y, 2 * (1 - x) + y), (x, 1 - y, 2 * x + (1 - y)), (1 - x, 1 - y, 2 * (1 - x) + (1 - y))]


def _ag_small(v, name):
    r, w = v.shape

    def body(v_ref, out_ref, send_sems, recv_sems, local_sem):
        x, y, c = _position()
        me = 4 * x + 2 * y + c
        mine = pltpu.make_async_copy(v_ref, out_ref.at[me], local_sem)
        mine.start()
        peers = []
        for k in range(1, N_DEV):
            fx, fy, fc = (k >> 2) & 1, (k >> 1) & 1, k & 1
            px = 1 - x if fx else x
            py = 1 - y if fy else y
            pc = 1 - c if fc else c
            peers.append((px, py, pc))
        sends = []
        for k, peer in enumerate(peers):
            cp = pltpu.make_async_remote_copy(src_ref=v_ref, dst_ref=out_ref.at[me], send_sem=send_sems.at[k],
                                              recv_sem=recv_sems.at[k], device_id=peer, device_id_type=MESH)
            cp.start()
            sends.append(cp)
        for k, (px, py, pc) in enumerate(peers):
            pltpu.make_async_remote_copy(src_ref=v_ref, dst_ref=out_ref.at[4 * px + 2 * py + pc], send_sem=send_sems.at[k],
                                         recv_sem=recv_sems.at[k], device_id=(px, py, pc), device_id_type=MESH).wait_recv()
        for cp in sends:
            cp.wait_send()
        mine.wait()

    return pl.pallas_call(
        body, name=name,
        out_shape=jax.ShapeDtypeStruct((N_DEV, r, w), F32),
        in_specs=[pl.BlockSpec(memory_space=pltpu.VMEM)],
        out_specs=pl.BlockSpec(memory_space=pltpu.VMEM),
        scratch_shapes=[pltpu.SemaphoreType.DMA((N_DEV - 1,)), pltpu.SemaphoreType.DMA((N_DEV - 1,)), pltpu.SemaphoreType.DMA],
        compiler_params=_params(None, 10 * _nbytes((r, w), F32)),
    )(v)


ANY = pl.BlockSpec(memory_space=pl.ANY)


def _ag_weights(shards):
    n = len(shards)

    def body(*refs):
        w_refs, out_refs = refs[:n], refs[n:2 * n]
        send_sems, recv_sems = refs[2 * n:]
        x, y, c = _position()
        q0 = 2 * x + y
        sibling = (x, y, 1 - c)
        chips = _other_chips(x, y)
        sends = []
        for k in range(n):
            w_ref, out_ref = w_refs[k], out_refs[k]
            half = w_ref.shape[0] // 2

            def blk(q, e, out_ref=out_ref, half=half):
                return out_ref.at[q, pl.ds(pl.multiple_of(e * half, 16), half), :]

            src = w_ref.at[pl.ds(pl.multiple_of(c * half, 16), half), :]
            for j, (cx, cy, _) in enumerate(chips):
                cp = pltpu.make_async_remote_copy(src_ref=src, dst_ref=blk(q0, c), send_sem=send_sems.at[6 * k + j],
                                                  recv_sem=recv_sems.at[6 * k + j], device_id=(cx, cy, c), device_id_type=MESH)
                cp.start()
                sends.append(cp)
        for k in range(n):
            out_ref = out_refs[k]
            half = w_refs[k].shape[0] // 2

            def blk(q, e, out_ref=out_ref, half=half):
                return out_ref.at[q, pl.ds(pl.multiple_of(e * half, 16), half), :]

            for j, (cx, cy, qj) in enumerate(chips):
                pltpu.make_async_remote_copy(src_ref=blk(qj, c), dst_ref=blk(qj, c), send_sem=send_sems.at[6 * k + j],
                                             recv_sem=recv_sems.at[6 * k + j], device_id=(cx, cy, c),
                                             device_id_type=MESH).wait_recv()
                fw = pltpu.make_async_remote_copy(src_ref=blk(qj, c), dst_ref=blk(qj, c), send_sem=send_sems.at[6 * k + 3 + j],
                                                  recv_sem=recv_sems.at[6 * k + 3 + j], device_id=sibling, device_id_type=MESH)
                fw.start()
                sends.append(fw)
        for k in range(n):
            out_ref = out_refs[k]
            half = w_refs[k].shape[0] // 2
            for j, (cx, cy, qj) in enumerate(chips):
                dst = out_ref.at[qj, pl.ds(pl.multiple_of((1 - c) * half, 16), half), :]
                pltpu.make_async_remote_copy(src_ref=dst, dst_ref=dst, send_sem=send_sems.at[6 * k + 3 + j],
                                             recv_sem=recv_sems.at[6 * k + 3 + j], device_id=sibling,
                                             device_id_type=MESH).wait_recv()
        for cp in sends:
            cp.wait_send()

    return pl.pallas_call(
        body, name="ag_weights",
        out_shape=[jax.ShapeDtypeStruct((N_CHIP,) + s.shape, s.dtype) for s in shards],
        in_specs=[ANY] * n, out_specs=[ANY] * n,
        scratch_shapes=[pltpu.SemaphoreType.DMA((6 * n,)), pltpu.SemaphoreType.DMA((6 * n,))],
    )(*shards)


def _swap_halves_d2d(grads):
    n = len(grads)

    def body(*refs):
        g_refs, out_refs = refs[:n], refs[n:2 * n]
        send_sems, recv_sems = refs[2 * n:]
        x, y, c = _position()
        sibling = (x, y, 1 - c)
        cps = []
        for k in range(n):
            cp = pltpu.make_async_remote_copy(src_ref=g_refs[k].at[:, 1 - c], dst_ref=out_refs[k], send_sem=send_sems.at[k],
                                              recv_sem=recv_sems.at[k], device_id=sibling, device_id_type=MESH)
            cp.start()
            cps.append(cp)
        for cp in cps:
            cp.wait_recv()
        for cp in cps:
            cp.wait_send()

    return pl.pallas_call(
        body, name="rs_pair_swap",
        out_shape=[jax.ShapeDtypeStruct((N_CHIP,) + g.shape[2:], g.dtype) for g in grads],
        in_specs=[ANY] * n, out_specs=[ANY] * n,
        scratch_shapes=[pltpu.SemaphoreType.DMA((n,)), pltpu.SemaphoreType.DMA((n,))],
    )(*grads)


def _pair_sum(g, a, c_idx, name):
    _, _, rh, cols = g.shape
    tr = rh
    for cand in (256, 128, 64, 32, 16):
        if rh % cand == 0 and rh > cand:
            tr = cand
            break

    def body(c_ref, g_ref, a_ref, o_ref):
        o_ref[...] = (g_ref[...] + a_ref[...]).astype(BF16)

    return pl.pallas_call(
        body, name=name,
        grid_spec=pltpu.PrefetchScalarGridSpec(
            num_scalar_prefetch=1, grid=(N_CHIP, rh // tr),
            in_specs=[pl.BlockSpec((None, None, tr, cols), lambda q, i, c_ref: (q, c_ref[0], i, 0)),
                      pl.BlockSpec((None, tr, cols), lambda q, i, c_ref: (q, i, 0))],
            out_specs=pl.BlockSpec((None, tr, cols), lambda q, i, c_ref: (q, i, 0))),
        out_shape=jax.ShapeDtypeStruct((N_CHIP, rh, cols), BF16),
        compiler_params=_params(("parallel", "parallel"), 10 * _nbytes((tr, cols), F32)),
    )(c_idx, g, a)


def _scatter_partials(parts):
    n = len(parts)

    def body(*refs):
        p_refs, out_refs = refs[:n], refs[n:2 * n]
        send_sems, recv_sems = refs[2 * n:]
        x, y, c = _position()
        chips = _other_chips(x, y)
        cps = []
        for k in range(n):
            for j, (cx, cy, qj) in enumerate(chips):
                cp = pltpu.make_async_remote_copy(src_ref=p_refs[k].at[qj], dst_ref=out_refs[k].at[j],
                                                  send_sem=send_sems.at[3 * k + j], recv_sem=recv_sems.at[3 * k + j],
                                                  device_id=(cx, cy, c), device_id_type=MESH)
                cp.start()
                cps.append(cp)
        for cp in cps:
            cp.wait_recv()
        for cp in cps:
            cp.wait_send()

    return pl.pallas_call(
        body, name="rs_scatter",
        out_shape=[jax.ShapeDtypeStruct((3,) + p.shape[1:], p.dtype) for p in parts],
        in_specs=[ANY] * n, out_specs=[ANY] * n,
        scratch_shapes=[pltpu.SemaphoreType.DMA((3 * n,)), pltpu.SemaphoreType.DMA((3 * n,))],
    )(*parts)


def _shard_sum(p, b, q_idx, name):
    _, rh, cols = p.shape
    tr = rh
    for cand in (256, 128, 64, 32, 16):
        if rh % cand == 0 and rh > cand:
            tr = cand
            break

    def body(q_ref, p_ref, b_ref, o_ref):
        acc = p_ref[...].astype(F32)
        for j in range(3):
            acc = acc + b_ref[j].astype(F32)
        o_ref[...] = acc

    return pl.pallas_call(
        body, name=name,
        grid_spec=pltpu.PrefetchScalarGridSpec(
            num_scalar_prefetch=1, grid=(rh // tr,),
            in_specs=[pl.BlockSpec((None, tr, cols), lambda i, q_ref: (q_ref[0], i, 0)),
                      pl.BlockSpec((3, tr, cols), lambda i, q_ref: (0, i, 0))],
            out_specs=pl.BlockSpec((tr, cols), lambda i, q_ref: (i, 0))),
        out_shape=jax.ShapeDtypeStruct((rh, cols), F32),
        compiler_params=_params(("parallel",), 8 * _nbytes((tr, cols), F32)),
    )(q_idx, p, b)


def _join_halves(halves):
    n = len(halves)

    def body(*refs):
        h_refs, out_refs = refs[:n], refs[n:2 * n]
        send_sems, recv_sems = refs[2 * n:]
        x, y, c = _position()
        sibling = (x, y, 1 - c)
        cps = []
        for k in range(n):
            cp = pltpu.make_async_remote_copy(src_ref=h_refs[k], dst_ref=out_refs[k], send_sem=send_sems.at[k],
                                              recv_sem=recv_sems.at[k], device_id=sibling, device_id_type=MESH)
            cp.start()
            cps.append(cp)
        for cp in cps:
            cp.wait_recv()
        for cp in cps:
            cp.wait_send()

    return pl.pallas_call(
        body, name="rs_join",
        out_shape=[jax.ShapeDtypeStruct(h.shape, h.dtype) for h in halves],
        in_specs=[ANY] * n, out_specs=[ANY] * n,
        scratch_shapes=[pltpu.SemaphoreType.DMA((n,)), pltpu.SemaphoreType.DMA((n,))],
    )(*halves)


def _cols_from_shards(g):
    q, r, cs = g.shape
    return jnp.transpose(g, (1, 0, 2)).reshape(r, q * cs)


def _cols_to_shards(w):
    r, cfull = w.shape
    return jnp.transpose(w.reshape(r, N_CHIP, cfull // N_CHIP), (1, 0, 2))


def _pad_w_in(w):
    z = lambda n: jnp.zeros((w.shape[0], n), w.dtype)
    q_lat, kv_lat, kpe = w[:, 0:512], w[:, 512:768], w[:, 768:800]
    qd, kd, vd = w[:, 800:1312], w[:, 1312:1824], w[:, 1824:2336]
    return jnp.concatenate([q_lat, qd, kd, vd, kv_lat, z(KPE_OFF), kpe, z(LANE - KPE_OFF - ROPE)], axis=1)


def _unpad_w_in(g):
    return jnp.concatenate([g[:, P_QLAT:P_QLAT + Q_LORA], g[:, P_KVLAT:P_KVLAT + KV_LORA],
                            g[:, P_KPE + KPE_OFF:P_KPE + KPE_OFF + ROPE], g[:, P_QD:P_QD + 3 * DIL_W]], axis=1)


def _pad_w_qb(w):
    w3 = w.reshape(Q_LORA, HEADS, NOPE + ROPE)
    return jnp.pad(w3, ((0, 0), (0, 0), (0, LANE - NOPE - ROPE))).reshape(Q_LORA, HEADS * LANE)


def _unpad_w_qb(g):
    return g.reshape(Q_LORA, HEADS, LANE)[:, :, :NOPE + ROPE].reshape(Q_LORA, HEADS * (NOPE + ROPE))


def _pad_w_kvb(w):
    w3 = w.reshape(KV_LORA, HEADS, 2 * NOPE)
    kp = jnp.pad(w3[:, :, :NOPE], ((0, 0), (0, 0), (0, LANE - NOPE))).reshape(KV_LORA, HEADS * LANE)
    return jnp.concatenate([kp, w3[:, :, NOPE:].reshape(KV_LORA, DIL_W)], axis=1)


def _unpad_w_kvb(g):
    gk = g[:, :HEADS * LANE].reshape(KV_LORA, HEADS, LANE)[:, :, :NOPE]
    gv = g[:, HEADS * LANE:].reshape(KV_LORA, HEADS, NOPE)
    return jnp.concatenate([gk, gv], axis=2).reshape(KV_LORA, HEADS * 2 * NOPE)


def _head_gains(g_q_nope, g_q_pe, g_k_nope, g_k_pe, g_dq, g_dk):
    z = lambda n: jnp.zeros((1, n), F32)
    q1 = jnp.concatenate([g_q_nope, g_q_pe, z(LANE - NOPE - ROPE)], axis=1)
    k1 = jnp.concatenate([g_k_nope, z(LANE - NOPE)], axis=1)
    kpe = jnp.concatenate([z(KPE_OFF), g_k_pe, z(LANE - KPE_OFF - ROPE)], axis=1)
    return dict(q=jnp.tile(q1, (1, HEADS)), k=jnp.tile(k1, (1, HEADS)), kpe=kpe,
                dq=jnp.tile(g_dq, (1, HEADS)), dk=jnp.tile(g_dk, (1, HEADS)))


SMALL_NAMES = ("g_mix_norm", "g_q_lat", "g_kv_lat", "g_mla_q_nope", "g_mla_q_pe", "g_mla_k_nope", "g_mla_k_pe",
               "g_dil_q", "g_dil_k", "g_ffn_norm", "b_conv")


def _pack(vs):
    parts, spans, off = [], [], 0
    for v in vs:
        n = v.shape[1]
        npad = -(-n // LANE) * LANE
        parts.append(jnp.pad(v, ((0, 0), (0, npad - n))))
        spans.append((off, n))
        off += npad
    return jnp.concatenate(parts, axis=1), spans


def kernel(x, c, positions, w_ada, b_ada, g_mix_norm, w_in, g_q_lat, w_q_b, g_kv_lat, w_kv_b, g_mla_q_nope, g_mla_q_pe, g_mla_k_nope, g_mla_k_pe, g_dil_q, g_dil_k, w_o, g_ffn_norm, w_up, w_conv, b_conv, w_down, loss_target, m_w_ada, m_b_ada, m_g_mix_norm, m_w_in, m_g_q_lat, m_w_q_b, m_g_kv_lat, m_w_kv_b, m_g_mla_q_nope, m_g_mla_q_pe, m_g_mla_k_nope, m_g_mla_k_pe, m_g_dil_q, m_g_dil_k, m_w_o, m_g_ffn_norm, m_w_up, m_w_conv, m_b_conv, m_w_down, v_w_ada, v_b_ada, v_g_mix_norm, v_w_in, v_g_q_lat, v_w_q_b, v_g_kv_lat, v_w_kv_b, v_g_mla_q_nope, v_g_mla_q_pe, v_g_mla_k_nope, v_g_mla_k_pe, v_g_dil_q, v_g_dil_k, v_w_o, v_g_ffn_norm, v_w_up, v_w_conv, v_b_conv, v_w_down):
    args = dict(locals())
    weights = {n: args[n][0] for n in ("w_ada", "w_in", "w_q_b", "w_kv_b", "w_o", "w_up", "w_conv", "w_down")}
    small_w = {n: args[n] for n in SMALL_NAMES + ("b_ada",)}
    mom_m = {n[2:]: (args[n][0] if args[n].ndim == 3 else args[n]) for n in args if n.startswith("m_")}
    mom_v = {n[2:]: (args[n][0] if args[n].ndim == 3 else args[n]) for n in args if n.startswith("v_")}

    xi, yi, ci = _position()
    q0 = 2 * xi + yi
    me = 4 * xi + 2 * yi + ci
    xs, tgt = x[0], loss_target[0]
    s = xs.shape[0]
    consts = _seg_consts()

    c_all = _ag_small(c, "ag_c")[:, 0, :]
    ada_cols = w_ada.shape[2]
    b_shard = lax.dynamic_slice_in_dim(b_ada, q0 * ada_cols, ada_cols, axis=1)
    mod_blk = _ada_fwd(c_all, weights["w_ada"], b_shard)
    mod_all = _ag_small(mod_blk, "ag_mod").reshape(N_CHIP, 2, N_DEV, ada_cols)
    mod = lax.dynamic_index_in_dim(lax.dynamic_index_in_dim(mod_all, ci, 1, False), me, 1, False)
    mod = mod.reshape(1, N_CHIP * ada_cols)
    sh1, sc1, g1, sh2, sc2, g2 = [mod[:, k * D_MODEL:(k + 1) * D_MODEL] for k in range(6)]

    own = [weights[n].astype(BF16) for n in ("w_in", "w_q_b", "w_kv_b", "w_o", "w_up", "w_down")]
    gathered = [lax.dynamic_update_slice_in_dim(g, w[None], q0, axis=0) for g, w in zip(_ag_weights(own), own)]
    w_in_p = _pad_w_in(_cols_from_shards(gathered[0]))
    w_qb_p = _pad_w_qb(_cols_from_shards(gathered[1]))
    w_kvb_p = _pad_w_kvb(_cols_from_shards(gathered[2]))
    w_o_f = gathered[3].reshape(D_MODEL, D_MODEL)
    w_up_f = _cols_from_shards(gathered[4])
    w_down_f = gathered[5].reshape(D_FF, D_MODEL)
    w_conv_f = _ag_small(weights["w_conv"], "ag_wconv")
    w_conv_f = jnp.transpose(w_conv_f.reshape(N_CHIP, 2, 3, -1)[:, 0], (1, 0, 2)).reshape(3, UP_W)

    gains = _head_gains(g_mla_q_nope, g_mla_q_pe, g_mla_k_nope, g_mla_k_pe, g_dil_q, g_dil_k)
    tab = _rope_tables(positions.reshape(s, 1), *_rope_consts())

    h = _prenorm(xs, g_mix_norm, sc1, sh1, "prenorm")
    proj = _mm(h, w_in_p, "nn", F32, 512, P_COLS, "mm_in")
    ql, kvl = _latnorm(proj, g_q_lat, g_kv_lat)
    q_raw = _mm(ql, w_qb_p, "nn", F32, 512, HEADS * LANE, "mm_qb")
    kv_raw = _mm(kvl, w_kvb_p, "nn", F32, 512, HEADS * LANE + DIL_W, "mm_kvb")
    qm, km, vm, qd, kd, vd = _attn_prep(q_raw, kv_raw, proj, tab, gains, consts)
    scale_m, scale_d = (NOPE + ROPE) ** -0.5, DIL_DIM ** -0.5
    o_m, lse_m = _attn_fwd(qm, km, vm, True, scale_m, "attn_mla")
    o_d, lse_d = _attn_fwd(qd, kd, vd, False, scale_d, "attn_dil")
    mix_in = jnp.concatenate([o_m, o_d], axis=1)
    mix = _mm(mix_in, w_o_f, "nn", F32, 512, D_MODEL, "mm_o")
    x1, h2 = _resid_prenorm(xs, mix, g1, g_ffn_norm, sc2, sh2)
    up = _mm(h2, w_up_f, "nn", F32, 512, CONV_TILE, "mm_up")
    act = _conv_gate(up, w_conv_f, b_conv)
    ffn = _mm(act, w_down_f, "nn", F32, 256, D_MODEL, "mm_down")
    dy, dffn, dg2, loss_part = _final(x1, ffn, tgt, g2)

    da = _mm(dffn, w_down_f, "nt", F32, 512, CONV_TILE, "mm_down_dx")
    gw_down = _mm(act, dffn, "tn", F32, 256, D_MODEL, "mm_down_dw")
    dug, duv, dbg, dbv, dwg, dwv = _gate_bwd(up, da, w_conv_f, b_conv)
    dup = jnp.concatenate([_conv_bwd(dug, w_conv_f[:, :D_FF], "conv_bwd_gate"),
                           _conv_bwd(duv, w_conv_f[:, D_FF:], "conv_bwd_val")], axis=1)
    dh2 = _mm(dup, w_up_f, "nt", F32, 256, 512, "mm_up_dx")
    gw_up = _mm(h2, dup, "tn", F32, 512, CONV_TILE, "mm_up_dw")
    dx1, dmix, acc2 = _ffnnorm_bwd(dh2, x1, dy, mix, g_ffn_norm, sc2, g1)
    dmix_in = _mm(dmix, w_o_f, "nt", F32, 512, D_MODEL, "mm_o_dx")
    gw_o = _mm(mix_in, dmix, "tn", F32, 512, D_MODEL, "mm_o_dw")
    dqm, dkm, dvm = _attn_bwd(qm, km, vm, o_m, dmix_in, 0, lse_m, True, scale_m, "attn_mla_bwd")
    dqd, dkd, dvd = _attn_bwd(qd, kd, vd, o_d, dmix_in, DIL_W // LANE, lse_d, False, scale_d, "attn_dil_bwd")
    dq_raw, dkv_raw, dkpe_b, dqd_b, dkd_b, dvd_b, dgains = _attn_prep_bwd(
        dqm, dkm, dvm, dqd, dkd, dvd, q_raw, kv_raw, proj, tab, gains, consts)
    dql = _mm(dq_raw, w_qb_p, "nt", F32, 512, Q_LORA, "mm_qb_dx")
    gw_qb = _unpad_w_qb(_mm(ql, dq_raw, "tn", F32, Q_LORA, HEADS * LANE, "mm_qb_dw"))
    dkvl = _mm(dkv_raw, w_kvb_p, "nt", F32, 512, KV_LORA, "mm_kvb_dx")
    gw_kvb = _unpad_w_kvb(_mm(kvl, dkv_raw, "tn", F32, KV_LORA, HEADS * LANE + DIL_W, "mm_kvb_dw"))
    dqlat_b, dkvlat_b, dglat = _latnorm_bwd(dql, dkvl, proj, g_q_lat, g_kv_lat)
    dproj = jnp.concatenate([dqlat_b, dqd_b, dkd_b, dvd_b, dkvlat_b, dkpe_b], axis=1)
    dh = _mm(dproj, w_in_p, "nt", F32, 512, D_MODEL, "mm_in_dx")
    gw_in = _unpad_w_in(_mm(h, dproj, "tn", F32, 512, P_COLS, "mm_in_dw"))
    grad_x, acc1 = _mixnorm_bwd(dh, xs, dx1, g_mix_norm, sc1)

    dmod = jnp.concatenate([acc1[0:1], acc1[1:2], acc2[3:4], acc2[0:1], acc2[1:2], dg2], axis=1)
    small_g = {"g_mix_norm": acc1[2:3], "g_q_lat": dglat[0:1], "g_kv_lat": dglat[1:2, :KV_LORA],
               "g_mla_q_nope": dgains[0:1, :NOPE], "g_mla_q_pe": dgains[0:1, NOPE:NOPE + ROPE],
               "g_mla_k_nope": dgains[1:2, :NOPE], "g_mla_k_pe": dgains[2:3, KPE_OFF:KPE_OFF + ROPE],
               "g_dil_q": dgains[3:4, :DIL_DIM], "g_dil_k": dgains[4:5, :DIL_DIM], "g_ffn_norm": acc2[2:3],
               "b_conv": jnp.concatenate([dbg, dbv], axis=1)}
    dw_conv = jnp.concatenate([dwg, dwv], axis=1)
    packed, spans = _pack([dmod] + [small_g[n] for n in SMALL_NAMES] + [dw_conv[k:k + 1] for k in range(3)])
    gathered_small = _ag_small(packed, "ag_small")
    summed = _sum_devices(gathered_small)
    take = lambda k: summed[:, spans[k][0]:spans[k][0] + spans[k][1]]
    grads = {"b_ada": take(0)}
    for k, n in enumerate(SMALL_NAMES):
        grads[n] = take(1 + k)
    shard_cols = UP_W // N_CHIP
    gconv_full = jnp.concatenate([take(1 + len(SMALL_NAMES) + k) for k in range(3)], axis=0)
    grads["w_conv"] = lax.dynamic_slice_in_dim(gconv_full, q0 * shard_cols, shard_cols, axis=1)
    dmod_all = gathered_small[:, 0, :6 * D_MODEL]
    grads["w_ada"] = _ada_bwd(c_all, lax.dynamic_slice_in_dim(dmod_all, q0 * ada_cols, ada_cols, axis=1))

    def halves(g4):
        q, r, cc = g4.shape
        return g4.reshape(q, 2, r // 2, cc)

    big = [halves(_cols_to_shards(gw_in)), halves(_cols_to_shards(gw_qb)), halves(_cols_to_shards(gw_kvb)),
           halves(gw_o.reshape(N_CHIP, D_MODEL // N_CHIP, D_MODEL)), halves(_cols_to_shards(gw_up)),
           halves(gw_down.reshape(N_CHIP, D_FF // N_CHIP, D_MODEL))]
    big_names = ("w_in", "w_q_b", "w_kv_b", "w_o", "w_up", "w_down")
    from_sibling = _swap_halves_d2d(big)
    c_idx, q_idx = jnp.reshape(ci, (1,)).astype(I32), jnp.reshape(q0, (1,)).astype(I32)
    chip_sums = [_pair_sum(g, a, c_idx, "pair_sum_" + n) for g, a, n in zip(big, from_sibling, big_names)]
    received = _scatter_partials(chip_sums)
    half_sums = [_shard_sum(p, b, q_idx, "shard_sum_" + n) for p, b, n in zip(chip_sums, received, big_names)]
    from_sib = _join_halves(half_sums)
    south = ci == 0
    for n, mine, theirs in zip(big_names, half_sums, from_sib):
        grads[n] = jnp.concatenate([jnp.where(south, mine, theirs), jnp.where(south, theirs, mine)], axis=0)

    delta, new_m, new_v = {}, {}, {}
    for n in ("w_ada", "w_in", "w_q_b", "w_kv_b", "w_o", "w_up", "w_conv", "w_down"):
        delta[n], new_m[n], new_v[n] = _adamw(weights[n], grads[n], mom_m[n], mom_v[n], "adamw_" + n)
    vec_names = ("b_ada",) + SMALL_NAMES
    pw, vspans = _pack([small_w[n] for n in vec_names])
    pg, _ = _pack([grads[n] for n in vec_names])
    pm, _ = _pack([mom_m[n] for n in vec_names])
    pv, _ = _pack([mom_v[n] for n in vec_names])
    rows8 = lambda z: z.reshape(8, z.shape[1] // 8)
    pad_mask, _ = _pack([jnp.ones_like(small_w[n]) for n in vec_names])
    pv = jnp.where(pad_mask > 0, pv, 1.0)
    sd, sm, sv = _adamw(rows8(pw), rows8(pg), rows8(pm), rows8(pv), "adamw_small")
    for k, n in enumerate(vec_names):
        o, ln = vspans[k]
        delta[n], new_m[n], new_v[n] = (z.reshape(1, -1)[:, o:o + ln] for z in (sd, sm, sv))

    loss = lax.psum(loss_part[0, 0], ("x", "y", "c"))
    order = ("w_ada", "b_ada", "g_mix_norm", "w_in", "g_q_lat", "w_q_b", "g_kv_lat", "w_kv_b", "g_mla_q_nope", "g_mla_q_pe",
             "g_mla_k_nope", "g_mla_k_pe", "g_dil_q", "g_dil_k", "w_o", "g_ffn_norm", "w_up", "w_conv", "b_conv", "w_down")
    lead = lambda n, z: z[None] if n.startswith("w_") else z
    outs = [loss, grad_x[None]]
    for d_ in (grads, delta, new_m, new_v):
        outs += [lead(n, d_[n]) for n in order]
    return tuple(outs)
```

```python
import functools

import numpy as np
import jax
import jax.numpy as jnp
from jax import lax
from jax.experimental import pallas as pl
from jax.experimental.pallas import tpu as pltpu

F32 = jnp.float32
BF16 = jnp.bfloat16
I32 = jnp.int32

D_MODEL = 1024
HEADS = 8
NOPE = 64
ROPE = 32
Q_LORA = 512
KV_LORA = 256
DIL_DIM = 64
DIL_W = HEADS * DIL_DIM
D_FF = 2816
UP_W = 2 * D_FF
IN_COLS = Q_LORA + KV_LORA + ROPE + 3 * DIL_W
ROPE_THETA = 10000.0
EPS = 1e-6
NEG_INF = -1e30
N_DEV = 8
N_CHIP = 4

ADAM_LR = 0.001
ADAM_B1 = 0.9
ADAM_B2 = 0.999
ADAM_EPS = 1e-08
ADAM_WD = 0.01
ADAM_STEP = 10

LANE = 128
ROW_TILE = 256
ATT_TILE = 256
VMEM_CAP = 56 * 1024 * 1024
VMEM_FLOOR = 32 * 1024 * 1024

P_QLAT, P_QD, P_KD, P_VD, P_KVLAT, P_KPE = 0, 512, 1024, 1536, 2048, 2304
P_COLS = 2432
KPE_OFF = 64

NN = (((1,), (0,)), ((), ()))
NT = (((1,), (1,)), ((), ()))
TN = (((0,), (0,)), ((), ()))
HIGHEST = lax.Precision.HIGHEST
MESH = pl.DeviceIdType.MESH


def _params(sem=None, est_bytes=0):
    limit = int(min(max(2 * est_bytes + (4 << 20), VMEM_FLOOR), VMEM_CAP))
    if sem is None:
        return pltpu.CompilerParams(vmem_limit_bytes=limit)
    return pltpu.CompilerParams(dimension_semantics=sem, vmem_limit_bytes=limit)


def _nbytes(shape, dtype):
    return int(np.prod(shape)) * jnp.dtype(dtype).itemsize


def _mm(a, b, dims, out_dtype, tm, tn, name):
    if dims == "nn":
        (m, k), (k2, n) = a.shape, b.shape
        a_spec = pl.BlockSpec((tm, k), lambda i, j: (i, 0))
        b_spec = pl.BlockSpec((k, tn), lambda i, j: (0, j))
        dn = NN
    elif dims == "nt":
        (m, k), (n, k2) = a.shape, b.shape
        a_spec = pl.BlockSpec((tm, k), lambda i, j: (i, 0))
        b_spec = pl.BlockSpec((tn, k), lambda i, j: (j, 0))
        dn = NT
    else:
        (k, m), (k2, n) = a.shape, b.shape
        a_spec = pl.BlockSpec((k, tm), lambda i, j: (0, i))
        b_spec = pl.BlockSpec((k, tn), lambda i, j: (0, j))
        dn = TN
    assert k == k2 and m % tm == 0 and n % tn == 0, (name, a.shape, b.shape, tm, tn)

    def body(a_ref, b_ref, o_ref):
        o_ref[...] = lax.dot_general(a_ref[...], b_ref[...], dn, preferred_element_type=F32).astype(o_ref.dtype)

    est = _nbytes((tm, k), a.dtype) + _nbytes((tn, k), b.dtype) + _nbytes((tm, tn), F32) + _nbytes((tm, tn), out_dtype)
    return pl.pallas_call(
        body, name=name,
        grid=(m // tm, n // tn),
        in_specs=[a_spec, b_spec],
        out_specs=pl.BlockSpec((tm, tn), lambda i, j: (i, j)),
        out_shape=jax.ShapeDtypeStruct((m, n), out_dtype),
        compiler_params=_params(("parallel", "parallel"), est),
    )(a, b)


def _seg_consts():
    seg_q = np.zeros((HEADS * LANE, LANE), np.float32)
    inv_q = np.zeros((1, LANE), np.float32)
    seg_k = np.zeros((HEADS * LANE, LANE), np.float32)
    inv_k = np.zeros((1, LANE), np.float32)
    seg_d = np.zeros((DIL_W, LANE), np.float32)
    inv_d = np.zeros((1, LANE), np.float32)
    for h in range(HEADS):
        seg_q[h * LANE:h * LANE + NOPE, 2 * h] = 1.0
        seg_q[h * LANE + NOPE:h * LANE + NOPE + ROPE, 2 * h + 1] = 1.0
        inv_q[0, 2 * h], inv_q[0, 2 * h + 1] = 1.0 / NOPE, 1.0 / ROPE
        seg_k[h * LANE:h * LANE + NOPE, h] = 1.0
        inv_k[0, h] = 1.0 / NOPE
        seg_d[h * DIL_DIM:(h + 1) * DIL_DIM, h] = 1.0
        inv_d[0, h] = 1.0 / DIL_DIM
    fold_q = np.tile(np.eye(LANE, dtype=np.float32), (HEADS, 1))
    fold_d = np.zeros((DIL_W, LANE), np.float32)
    fold_d[np.arange(DIL_W), np.arange(DIL_W) % DIL_DIM] = 1.0
    j = lambda v: jnp.asarray(v)
    b = lambda v: jnp.asarray(v, dtype=BF16)
    return dict(seg_q=b(seg_q), exp_q=b(seg_q.T.copy()), inv_q=j(inv_q), seg_k=b(seg_k), exp_k=b(seg_k.T.copy()),
                inv_k=j(inv_k), seg_d=b(seg_d), exp_d=b(seg_d.T.copy()), inv_d=j(inv_d), fold_q=j(fold_q), fold_d=j(fold_d))


def _rope_consts():
    inv_d = jnp.power(ROPE_THETA, -2.0 * jnp.arange(DIL_DIM // 2, dtype=F32) / DIL_DIM)
    inv_q = jnp.power(ROPE_THETA, -2.0 * jnp.arange(ROPE // 2, dtype=F32) / ROPE)
    lanes = np.arange(LANE)
    freq_d = inv_d[lanes % (DIL_DIM // 2)]
    in_pe = (lanes >= KPE_OFF) & (lanes < KPE_OFF + ROPE)
    freq_q = jnp.where(jnp.asarray(in_pe), inv_q[(lanes - KPE_OFF) % (ROPE // 2)], 0.0)
    sign_d = np.where(lanes % DIL_DIM < DIL_DIM // 2, -1.0, 1.0).astype(np.float32)
    sign_q = np.where(in_pe, np.where((lanes - KPE_OFF) < ROPE // 2, -1.0, 1.0), 0.0).astype(np.float32)
    zeros, ones = np.zeros(LANE, np.float32), np.ones(LANE, np.float32)
    freq = jnp.concatenate([freq_d, freq_d, freq_q, freq_q])[None, :]
    csel = jnp.asarray(np.concatenate([ones, zeros, ones, zeros]))[None, :]
    ssel = jnp.asarray(np.concatenate([zeros, sign_d, zeros, sign_q]))[None, :]
    return freq, csel, ssel


def _full(shape):
    return pl.BlockSpec(shape, lambda *_: (0,) * len(shape))


def _tile_lanes(x, n):
    return jnp.concatenate([x] * n, axis=1)


def _rope_tables(pos_col, freq, csel, ssel):
    s = pos_col.shape[0]

    def body(p_ref, f_ref, c_ref, s_ref, o_ref):
        ang = p_ref[...].astype(F32) * f_ref[...]
        o_ref[...] = c_ref[...] * jnp.cos(ang) + s_ref[...] * jnp.sin(ang)

    return pl.pallas_call(
        body, name="rope_tables", grid=(s // ROW_TILE,),
        in_specs=[pl.BlockSpec((ROW_TILE, 1), lambda i: (i, 0)), _full((1, 4 * LANE)), _full((1, 4 * LANE)), _full((1, 4 * LANE))],
        out_specs=pl.BlockSpec((ROW_TILE, 4 * LANE), lambda i: (i, 0)),
        out_shape=jax.ShapeDtypeStruct((s, 4 * LANE), F32),
        compiler_params=_params(("parallel",)),
    )(pos_col, freq, csel, ssel)


def _rms(x):
    return lax.rsqrt(jnp.mean(x * x, axis=-1, keepdims=True) + EPS)


def _prenorm(x, gain, scale, shift, name):
    s, d = x.shape

    def body(x_ref, g_ref, sc_ref, sh_ref, h_ref):
        xv = x_ref[...]
        h = (xv * _rms(xv)) * g_ref[...] * (1.0 + sc_ref[...]) + sh_ref[...]
        h_ref[...] = h.astype(BF16)

    row = pl.BlockSpec((ROW_TILE, d), lambda i: (i, 0))
    return pl.pallas_call(
        body, name=name, grid=(s // ROW_TILE,),
        in_specs=[row, _full((1, d)), _full((1, d)), _full((1, d))],
        out_specs=row, out_shape=jax.ShapeDtypeStruct((s, d), BF16),
        compiler_params=_params(("parallel",)),
    )(x, gain, scale, shift)


def _latnorm(proj, g_q, g_kv):
    s = proj.shape[0]

    def body(q_ref, kv_ref, gq_ref, gkv_ref, ql_ref, kvl_ref):
        q, kv = q_ref[...], kv_ref[...]
        ql_ref[...] = ((q * _rms(q)) * gq_ref[...]).astype(BF16)
        kvl_ref[...] = ((kv * _rms(kv)) * gkv_ref[...]).astype(BF16)

    return pl.pallas_call(
        body, name="latnorm", grid=(s // ROW_TILE,),
        in_specs=[pl.BlockSpec((ROW_TILE, Q_LORA), lambda i: (i, P_QLAT // Q_LORA)),
                  pl.BlockSpec((ROW_TILE, KV_LORA), lambda i: (i, P_KVLAT // KV_LORA)),
                  _full((1, Q_LORA)), _full((1, KV_LORA))],
        out_specs=[pl.BlockSpec((ROW_TILE, Q_LORA), lambda i: (i, 0)), pl.BlockSpec((ROW_TILE, KV_LORA), lambda i: (i, 0))],
        out_shape=[jax.ShapeDtypeStruct((s, Q_LORA), BF16), jax.ShapeDtypeStruct((s, KV_LORA), BF16)],
        compiler_params=_params(("parallel",)),
    )(proj, proj, g_q, g_kv)


def _dot01(v, mat01):
    hi = v.astype(BF16)
    lo = (v - hi.astype(F32)).astype(BF16)
    return jnp.dot(hi, mat01, preferred_element_type=F32) + jnp.dot(lo, mat01, preferred_element_type=F32)


def _seg_rinv(x, seg, exp, inv):
    r = lax.rsqrt(_dot01(x * x, seg) * inv + EPS)
    return _dot01(r, exp)


def _seg_mean(v, seg, exp, inv):
    return _dot01(_dot01(v, seg) * inv, exp)


def _swap_halves(x, half):
    n = x.shape[1]
    lane = lax.broadcasted_iota(I32, (1, n), 1)
    first = (lane & (2 * half - 1)) < half
    return jnp.where(first, pltpu.roll(x, n - half, 1), pltpu.roll(x, half, 1))


def _rope(x, cos, sin_signed, half):
    return x * cos + _swap_halves(x, half) * sin_signed


def _rope_bwd(dy, cos, sin_signed, half):
    return dy * cos + _swap_halves(dy * sin_signed, half)


def _pe_lane_mask(n):
    lane = lax.broadcasted_iota(I32, (1, n), 1) & (LANE - 1)
    return (lane >= KPE_OFF) & (lane < KPE_OFF + ROPE)


def _attn_prep(q_raw, kv_raw, proj, tab, gains, consts):
    s = q_raw.shape[0]
    hw = HEADS * LANE

    def body(q_ref, kv_ref, kpe_ref, qd_ref, kd_ref, vd_ref, tab_ref,
             gq_ref, gk_ref, gkpe_ref, gdq_ref, gdk_ref,
             segq_ref, expq_ref, invq_ref, segk_ref, expk_ref, invk_ref, segd_ref, expd_ref, invd_ref,
             qm_ref, km_ref, vm_ref, qdo_ref, kdo_ref, vdo_ref):
        tab_v = tab_ref[...]
        cos_d, sin_d = _tile_lanes(tab_v[:, 0:LANE], DIL_W // LANE), _tile_lanes(tab_v[:, LANE:2 * LANE], DIL_W // LANE)
        cos_q1, sin_q1 = tab_v[:, 2 * LANE:3 * LANE], tab_v[:, 3 * LANE:4 * LANE]
        cos_q, sin_q = _tile_lanes(cos_q1, HEADS), _tile_lanes(sin_q1, HEADS)

        q = q_ref[...]
        qn = q * _seg_rinv(q, segq_ref[...], expq_ref[...], invq_ref[...]) * gq_ref[...]
        qm_ref[...] = _rope(qn, cos_q, sin_q, ROPE // 2).astype(BF16)

        kv = kv_ref[...]
        kp = kv[:, :hw]
        kn = kp * _seg_rinv(kp, segk_ref[...], expk_ref[...], invk_ref[...]) * gk_ref[...]
        kpe = kpe_ref[...]
        r_pe = lax.rsqrt(jnp.sum(kpe * kpe, axis=-1, keepdims=True) * (1.0 / ROPE) + EPS)
        kpe_r = _rope(kpe * r_pe * gkpe_ref[...], cos_q1, sin_q1, ROPE // 2)
        km_ref[...] = (kn + _tile_lanes(kpe_r, HEADS)).astype(BF16)
        vm_ref[...] = kv[:, hw:].astype(BF16)

        qd = qd_ref[...]
        qdn = qd * _seg_rinv(qd, segd_ref[...], expd_ref[...], invd_ref[...]) * gdq_ref[...]
        qdo_ref[...] = _rope(qdn, cos_d, sin_d, DIL_DIM // 2).astype(BF16)
        kd = kd_ref[...]
        kdn = kd * _seg_rinv(kd, segd_ref[...], expd_ref[...], invd_ref[...]) * gdk_ref[...]
        kdo_ref[...] = _rope(kdn, cos_d, sin_d, DIL_DIM // 2).astype(BF16)
        vdo_ref[...] = vd_ref[...].astype(BF16)

    t = ROW_TILE
    row = lambda w, cb=0: pl.BlockSpec((t, w), lambda i: (i, cb))
    c = consts
    return pl.pallas_call(
        body, name="attn_prep", grid=(s // t,),
        in_specs=[row(hw), row(hw + DIL_W), row(LANE, P_KPE // LANE), row(DIL_W, P_QD // DIL_W), row(DIL_W, P_KD // DIL_W),
                  row(DIL_W, P_VD // DIL_W), row(4 * LANE),
                  _full((1, hw)), _full((1, hw)), _full((1, LANE)), _full((1, DIL_W)), _full((1, DIL_W)),
                  _full((hw, LANE)), _full((LANE, hw)), _full((1, LANE)), _full((hw, LANE)), _full((LANE, hw)), _full((1, LANE)),
                  _full((DIL_W, LANE)), _full((LANE, DIL_W)), _full((1, LANE))],
        out_specs=[row(hw), row(hw), row(DIL_W), row(DIL_W), row(DIL_W), row(DIL_W)],
        out_shape=[jax.ShapeDtypeStruct((s, hw), BF16), jax.ShapeDtypeStruct((s, hw), BF16)]
        + [jax.ShapeDtypeStruct((s, DIL_W), BF16)] * 4,
        compiler_params=_params(("parallel",), 24 << 20),
    )(q_raw, kv_raw, proj, proj, proj, proj, tab, gains["q"], gains["k"], gains["kpe"], gains["dq"], gains["dk"],
      c["seg_q"], c["exp_q"], c["inv_q"], c["seg_k"], c["exp_k"], c["inv_k"], c["seg_d"], c["exp_d"], c["inv_d"])


def _attn_prep_bwd(dqm, dkm, dvm, dqd, dkd, dvd, q_raw, kv_raw, proj, tab, gains, consts):
    s = q_raw.shape[0]
    hw = HEADS * LANE
    n_steps = s // ROW_TILE

    def body(dqm_ref, dkm_ref, dvm_ref, dqd_ref, dkd_ref, dvd_ref, q_ref, kv_ref, kpe_ref, qd_ref, kd_ref, tab_ref,
             gq_ref, gk_ref, gkpe_ref, gdq_ref, gdk_ref,
             segq_ref, expq_ref, invq_ref, segk_ref, expk_ref, invk_ref, segd_ref, expd_ref, invd_ref, foldq_ref, foldd_ref,
             dq_ref, dkv_ref, dkpe_ref, dqdo_ref, dkdo_ref, dvdo_ref, dg_ref, acc_ref):
        i = pl.program_id(0)

        @pl.when(i == 0)
        def _():
            acc_ref[...] = jnp.zeros_like(acc_ref)

        tab_v = tab_ref[...]
        cos_d, sin_d = _tile_lanes(tab_v[:, 0:LANE], DIL_W // LANE), _tile_lanes(tab_v[:, LANE:2 * LANE], DIL_W // LANE)
        cos_q1, sin_q1 = tab_v[:, 2 * LANE:3 * LANE], tab_v[:, 3 * LANE:4 * LANE]
        cos_q, sin_q = _tile_lanes(cos_q1, HEADS), _tile_lanes(sin_q1, HEADS)

        def norm_bwd(x, dyg, gain, seg, exp, inv):
            rinv = _seg_rinv(x, seg, exp, inv)
            xn = x * rinv
            dxn = dyg * gain
            dx = rinv * (dxn - xn * _seg_mean(dxn * xn, seg, exp, inv))
            return dx, jnp.sum(dyg * xn, axis=0, keepdims=True)

        dq, gq_l = norm_bwd(q_ref[...], _rope_bwd(dqm_ref[...], cos_q, sin_q, ROPE // 2), gq_ref[...],
                            segq_ref[...], expq_ref[...], invq_ref[...])
        dq_ref[...] = dq.astype(BF16)

        dkm = dkm_ref[...]
        kv = kv_ref[...]
        dkp, gk_l = norm_bwd(kv[:, :hw], dkm, gk_ref[...], segk_ref[...], expk_ref[...], invk_ref[...])
        dkv_ref[:, :hw] = dkp.astype(BF16)
        dkv_ref[:, hw:] = dvm_ref[...].astype(BF16)

        dkpe_r = dkm[:, 0:LANE]
        for h in range(1, HEADS):
            dkpe_r = dkpe_r + dkm[:, h * LANE:(h + 1) * LANE]
        dkpe_r = jnp.where(_pe_lane_mask(LANE), dkpe_r, 0.0)
        dyg = _rope_bwd(dkpe_r, cos_q1, sin_q1, ROPE // 2)
        kpe = kpe_ref[...]
        r_pe = lax.rsqrt(jnp.sum(kpe * kpe, axis=-1, keepdims=True) * (1.0 / ROPE) + EPS)
        xn = kpe * r_pe
        dxn = dyg * gkpe_ref[...]
        dkpe = r_pe * (dxn - xn * (jnp.sum(dxn * xn, axis=-1, keepdims=True) * (1.0 / ROPE)))
        dkpe_ref[...] = dkpe.astype(BF16)
        gkpe_l = jnp.sum(dyg * xn, axis=0, keepdims=True)

        dqd_v, gdq_l = norm_bwd(qd_ref[...], _rope_bwd(dqd_ref[...], cos_d, sin_d, DIL_DIM // 2), gdq_ref[...],
                                segd_ref[...], expd_ref[...], invd_ref[...])
        dqdo_ref[...] = dqd_v.astype(BF16)
        dkd_v, gdk_l = norm_bwd(kd_ref[...], _rope_bwd(dkd_ref[...], cos_d, sin_d, DIL_DIM // 2), gdk_ref[...],
                                segd_ref[...], expd_ref[...], invd_ref[...])
        dkdo_ref[...] = dkd_v.astype(BF16)
        dvdo_ref[...] = dvd_ref[...].astype(BF16)

        acc_ref[0:1, :] += gq_l
        acc_ref[1:2, :] += gk_l
        acc_ref[2:3, 0:LANE] += gkpe_l
        acc_ref[3:4, 0:DIL_W] += gdq_l
        acc_ref[4:5, 0:DIL_W] += gdk_l

        @pl.when(i == n_steps - 1)
        def _():
            acc = acc_ref[...]
            fq = jnp.dot(acc, foldq_ref[...], precision=HIGHEST, preferred_element_type=F32)
            fd = jnp.dot(acc[:, 0:DIL_W], foldd_ref[...], precision=HIGHEST, preferred_element_type=F32)
            rows = lax.broadcasted_iota(I32, (8, LANE), 0)
            dg_ref[...] = jnp.where(rows < 2, fq, jnp.where(rows == 2, acc[:, 0:LANE], fd))

    t = ROW_TILE
    row = lambda w, cb=0: pl.BlockSpec((t, w), lambda i: (i, cb))
    c = consts
    return pl.pallas_call(
        body, name="attn_prep_bwd", grid=(n_steps,),
        in_specs=[row(hw), row(hw), row(DIL_W), row(DIL_W), row(DIL_W), row(DIL_W),
                  row(hw), row(hw + DIL_W), row(LANE, P_KPE // LANE), row(DIL_W, P_QD // DIL_W), row(DIL_W, P_KD // DIL_W),
                  row(4 * LANE),
                  _full((1, hw)), _full((1, hw)), _full((1, LANE)), _full((1, DIL_W)), _full((1, DIL_W)),
                  _full((hw, LANE)), _full((LANE, hw)), _full((1, LANE)), _full((hw, LANE)), _full((LANE, hw)), _full((1, LANE)),
                  _full((DIL_W, LANE)), _full((LANE, DIL_W)), _full((1, LANE)), _full((hw, LANE)), _full((DIL_W, LANE))],
        out_specs=[row(hw), row(hw + DIL_W), row(LANE), row(DIL_W), row(DIL_W), row(DIL_W), _full((8, LANE))],
        out_shape=[jax.ShapeDtypeStruct((s, hw), BF16), jax.ShapeDtypeStruct((s, hw + DIL_W), BF16),
                   jax.ShapeDtypeStruct((s, LANE), BF16)] + [jax.ShapeDtypeStruct((s, DIL_W), BF16)] * 3
        + [jax.ShapeDtypeStruct((8, LANE), F32)],
        scratch_shapes=[pltpu.VMEM((8, hw), F32)],
        compiler_params=_params(("arbitrary",), 28 << 20),
    )(dqm, dkm, dvm, dqd, dkd, dvd, q_raw, kv_raw, proj, proj, proj, tab,
      gains["q"], gains["k"], gains["kpe"], gains["dq"], gains["dk"],
      c["seg_q"], c["exp_q"], c["inv_q"], c["seg_k"], c["exp_k"], c["inv_k"], c["seg_d"], c["exp_d"], c["inv_d"],
      c["fold_q"], c["fold_d"])


def _latnorm_bwd(dql, dkvl, proj, g_q, g_kv):
    s = proj.shape[0]
    n_steps = s // ROW_TILE

    def body(dql_ref, dkvl_ref, q_ref, kv_ref, gq_ref, gkv_ref, dq_ref, dkv_ref, dg_ref):
        i = pl.program_id(0)

        @pl.when(i == 0)
        def _():
            dg_ref[...] = jnp.zeros_like(dg_ref)

        def one(x, dyg, gain):
            r = _rms(x)
            xn = x * r
            dxn = dyg * gain
            dx = r * (dxn - xn * jnp.mean(dxn * xn, axis=-1, keepdims=True))
            return dx, jnp.sum(dyg * xn, axis=0, keepdims=True)

        dq, gq_l = one(q_ref[...], dql_ref[...], gq_ref[...])
        dkv, gkv_l = one(kv_ref[...], dkvl_ref[...], gkv_ref[...])
        dq_ref[...] = dq.astype(BF16)
        dkv_ref[...] = dkv.astype(BF16)
        dg_ref[0:1, :] += gq_l
        dg_ref[1:2, 0:KV_LORA] += gkv_l

    t = ROW_TILE
    return pl.pallas_call(
        body, name="latnorm_bwd", grid=(n_steps,),
        in_specs=[pl.BlockSpec((t, Q_LORA), lambda i: (i, 0)), pl.BlockSpec((t, KV_LORA), lambda i: (i, 0)),
                  pl.BlockSpec((t, Q_LORA), lambda i: (i, P_QLAT // Q_LORA)),
                  pl.BlockSpec((t, KV_LORA), lambda i: (i, P_KVLAT // KV_LORA)),
                  _full((1, Q_LORA)), _full((1, KV_LORA))],
        out_specs=[pl.BlockSpec((t, Q_LORA), lambda i: (i, 0)), pl.BlockSpec((t, KV_LORA), lambda i: (i, 0)), _full((8, Q_LORA))],
        out_shape=[jax.ShapeDtypeStruct((s, Q_LORA), BF16), jax.ShapeDtypeStruct((s, KV_LORA), BF16),
                   jax.ShapeDtypeStruct((8, Q_LORA), F32)],
        compiler_params=_params(("arbitrary",)),
    )(dql, dkvl, proj, proj, g_q, g_kv)


def _resid_prenorm(x, mix, g1, gain, scale, shift):
    s, d = x.shape

    def body(x_ref, mix_ref, g1_ref, g_ref, sc_ref, sh_ref, x1_ref, h_ref):
        x1 = x_ref[...] + g1_ref[...] * mix_ref[...]
        x1_ref[...] = x1
        h_ref[...] = ((x1 * _rms(x1)) * g_ref[...] * (1.0 + sc_ref[...]) + sh_ref[...]).astype(BF16)

    row = pl.BlockSpec((ROW_TILE, d), lambda i: (i, 0))
    vec = _full((1, d))
    return pl.pallas_call(
        body, name="resid_prenorm", grid=(s // ROW_TILE,),
        in_specs=[row, row, vec, vec, vec, vec], out_specs=[row, row],
        out_shape=[jax.ShapeDtypeStruct((s, d), F32), jax.ShapeDtypeStruct((s, d), BF16)],
        compiler_params=_params(("parallel",)),
    )(x, mix, g1, gain, scale, shift)


CONV_TILE = 1408
HALO = 8


def _shift_down(x, halo, k):
    t = x.shape[0]
    row = lax.broadcasted_iota(I32, (t, 1), 0)
    out = pltpu.roll(x, k, 0)
    for r in range(k):
        out = jnp.where(row == r, halo[HALO - k + r:HALO - k + r + 1, :], out)
    return out


def _shift_up(x, halo, k):
    t = x.shape[0]
    row = lax.broadcasted_iota(I32, (t, 1), 0)
    out = pltpu.roll(x, t - k, 0)
    for r in range(k):
        out = jnp.where(row == t - k + r, halo[r:r + 1, :], out)
    return out


def _conv_fwd(x, halo, w, b):
    p1, p2 = _shift_down(x, halo, 1), _shift_down(x, halo, 2)
    u = b + p2 * w[0:1, :]
    u = u + p1 * w[1:2, :]
    u = u + x * w[2:3, :]
    return u, p1, p2


def _sigmoid(x):
    return 1.0 / (1.0 + jnp.exp(-x))


def _conv_gate(up, w_conv, b_conv):
    s = up.shape[0]
    t = ROW_TILE
    nj = D_FF // CONV_TILE
    hb = t // HALO

    def body(g_ref, v_ref, gh_ref, vh_ref, wg_ref, wv_ref, bg_ref, bv_ref, a_ref):
        live = (pl.program_id(0) > 0).astype(F32)
        ug, _, _ = _conv_fwd(g_ref[...], gh_ref[...] * live, wg_ref[...], bg_ref[...])
        uv, _, _ = _conv_fwd(v_ref[...], vh_ref[...] * live, wv_ref[...], bv_ref[...])
        a_ref[...] = (ug * _sigmoid(ug) * uv).astype(BF16)

    main = lambda off: pl.BlockSpec((t, CONV_TILE), lambda i, j: (i, j + off))
    halo = lambda off: pl.BlockSpec((HALO, CONV_TILE), lambda i, j: (jnp.maximum(i * hb - 1, 0), j + off))
    wsp = lambda off: pl.BlockSpec((3, CONV_TILE), lambda i, j: (0, j + off))
    bsp = lambda off: pl.BlockSpec((1, CONV_TILE), lambda i, j: (0, j + off))
    return pl.pallas_call(
        body, name="conv_gate", grid=(s // t, nj),
        in_specs=[main(0), main(nj), halo(0), halo(nj), wsp(0), wsp(nj), bsp(0), bsp(nj)],
        out_specs=pl.BlockSpec((t, CONV_TILE), lambda i, j: (i, j)),
        out_shape=jax.ShapeDtypeStruct((s, D_FF), BF16),
        compiler_params=_params(("parallel", "parallel"), 12 << 20),
    )(up, up, up, up, w_conv, w_conv, b_conv, b_conv)


def _gate_bwd(up, da, w_conv, b_conv):
    s = up.shape[0]
    t = ROW_TILE
    nj = D_FF // CONV_TILE
    hb = t // HALO

    def body(g_ref, v_ref, gh_ref, vh_ref, da_ref, wg_ref, wv_ref, bg_ref, bv_ref,
             dug_ref, duv_ref, dbg_ref, dbv_ref, dwg_ref, dwv_ref):
        i = pl.program_id(1)

        @pl.when(i == 0)
        def _():
            for r in (dbg_ref, dbv_ref, dwg_ref, dwv_ref):
                r[...] = jnp.zeros_like(r)

        live = (i > 0).astype(F32)
        xg, xv = g_ref[...], v_ref[...]
        ug, g1, g2 = _conv_fwd(xg, gh_ref[...] * live, wg_ref[...], bg_ref[...])
        uv, v1, v2 = _conv_fwd(xv, vh_ref[...] * live, wv_ref[...], bv_ref[...])
        sg = _sigmoid(ug)
        da_v = da_ref[...]
        dug = da_v * uv * (sg * (1.0 + ug * (1.0 - sg)))
        duv = da_v * (ug * sg)
        dug_ref[...] = dug
        duv_ref[...] = duv
        csum = lambda z: jnp.sum(z, axis=0, keepdims=True)
        dbg_ref[...] += csum(dug)
        dbv_ref[...] += csum(duv)
        dwg_ref[0:1, :] += csum(dug * g2)
        dwg_ref[1:2, :] += csum(dug * g1)
        dwg_ref[2:3, :] += csum(dug * xg)
        dwv_ref[0:1, :] += csum(duv * v2)
        dwv_ref[1:2, :] += csum(duv * v1)
        dwv_ref[2:3, :] += csum(duv * xv)

    main = lambda off: pl.BlockSpec((t, CONV_TILE), lambda j, i: (i, j + off))
    halo = lambda off: pl.BlockSpec((HALO, CONV_TILE), lambda j, i: (jnp.maximum(i * hb - 1, 0), j + off))
    wsp = lambda off: pl.BlockSpec((3, CONV_TILE), lambda j, i: (0, j + off))
    bsp = lambda off: pl.BlockSpec((1, CONV_TILE), lambda j, i: (0, j + off))
    outs = pl.pallas_call(
        body, name="gate_bwd", grid=(nj, s // t),
        in_specs=[main(0), main(nj), halo(0), halo(nj), pl.BlockSpec((t, CONV_TILE), lambda j, i: (i, j)),
                  wsp(0), wsp(nj), bsp(0), bsp(nj)],
        out_specs=[pl.BlockSpec((t, CONV_TILE), lambda j, i: (i, j)), pl.BlockSpec((t, CONV_TILE), lambda j, i: (i, j)),
                   pl.BlockSpec((1, CONV_TILE), lambda j, i: (0, j)), pl.BlockSpec((1, CONV_TILE), lambda j, i: (0, j)),
                   pl.BlockSpec((3, CONV_TILE), lambda j, i: (0, j)), pl.BlockSpec((3, CONV_TILE), lambda j, i: (0, j))],
        out_shape=[jax.ShapeDtypeStruct((s, D_FF), F32), jax.ShapeDtypeStruct((s, D_FF), F32),
                   jax.ShapeDtypeStruct((1, D_FF), F32), jax.ShapeDtypeStruct((1, D_FF), F32),
                   jax.ShapeDtypeStruct((3, D_FF), F32), jax.ShapeDtypeStruct((3, D_FF), F32)],
        compiler_params=_params(("parallel", "arbitrary"), 20 << 20),
    )(up, up, up, up, da, w_conv, w_conv, b_conv, b_conv)
    return outs


def _conv_bwd(du, w_half, name):
    s = du.shape[0]
    t = ROW_TILE
    nj = D_FF // CONV_TILE
    hb = t // HALO
    n_i = s // t

    def body(d_ref, h_ref, w_ref, o_ref):
        live = (pl.program_id(0) < n_i - 1).astype(F32)
        x = d_ref[...]
        halo = h_ref[...] * live
        w = w_ref[...]
        o = x * w[2:3, :] + _shift_up(x, halo, 1) * w[1:2, :] + _shift_up(x, halo, 2) * w[0:1, :]
        o_ref[...] = o.astype(BF16)

    return pl.pallas_call(
        body, name=name, grid=(n_i, nj),
        in_specs=[pl.BlockSpec((t, CONV_TILE), lambda i, j: (i, j)),
                  pl.BlockSpec((HALO, CONV_TILE), lambda i, j: (jnp.minimum((i + 1) * hb, s // HALO - 1), j)),
                  pl.BlockSpec((3, CONV_TILE), lambda i, j: (0, j))],
        out_specs=pl.BlockSpec((t, CONV_TILE), lambda i, j: (i, j)),
        out_shape=jax.ShapeDtypeStruct((s, D_FF), BF16),
        compiler_params=_params(("parallel", "parallel"), 8 << 20),
    )(du, du, w_half)


def _final(x1, ffn, tgt, g2):
    s, d = x1.shape
    n_steps = s // ROW_TILE

    def body(x1_ref, f_ref, t_ref, g2_ref, dy_ref, df_ref, dg2_ref, loss_ref, lacc_ref):
        i = pl.program_id(0)

        @pl.when(i == 0)
        def _():
            dg2_ref[...] = jnp.zeros_like(dg2_ref)
            lacc_ref[...] = jnp.zeros_like(lacc_ref)

        f = f_ref[...]
        e = x1_ref[...] + g2_ref[...] * f - t_ref[...]
        dy = e * (1.0 / d)
        dy_ref[...] = dy
        df_ref[...] = (dy * g2_ref[...]).astype(BF16)
        dg2_ref[...] += jnp.sum(dy * f, axis=0, keepdims=True)
        lacc_ref[...] += jnp.sum(e * e, axis=0, keepdims=True)

        @pl.when(i == n_steps - 1)
        def _():
            loss_ref[...] = jnp.sum(lacc_ref[...], axis=1, keepdims=True) * (0.5 / d)

    row = pl.BlockSpec((ROW_TILE, d), lambda i: (i, 0))
    return pl.pallas_call(
        body, name="final", grid=(n_steps,),
        in_specs=[row, row, row, _full((1, d))],
        out_specs=[row, row, _full((1, d)), _full((1, 1))],
        out_shape=[jax.ShapeDtypeStruct((s, d), F32), jax.ShapeDtypeStruct((s, d), BF16),
                   jax.ShapeDtypeStruct((1, d), F32), jax.ShapeDtypeStruct((1, 1), F32)],
        scratch_shapes=[pltpu.VMEM((1, d), F32)],
        compiler_params=_params(("arbitrary",)),
    )(x1, ffn, tgt, g2)


def _ffnnorm_bwd(dh2, x1, dy, mix, gain, scale, g1):
    s, d = x1.shape
    n_steps = s // ROW_TILE

    def body(dh_ref, x_ref, dy_ref, mix_ref, g_ref, sc_ref, g1_ref, dx_ref, dm_ref, acc_ref):
        i = pl.program_id(0)

        @pl.when(i == 0)
        def _():
            acc_ref[...] = jnp.zeros_like(acc_ref)

        dh, x = dh_ref[...], x_ref[...]
        r = _rms(x)
        xn = x * r
        dn = dh * (1.0 + sc_ref[...])
        dxn = dn * g_ref[...]
        dx = dy_ref[...] + r * (dxn - xn * jnp.mean(dxn * xn, axis=-1, keepdims=True))
        dx_ref[...] = dx
        dm_ref[...] = (dx * g1_ref[...]).astype(BF16)
        csum = lambda z: jnp.sum(z, axis=0, keepdims=True)
        acc_ref[0:1, :] += csum(dh)
        acc_ref[1:2, :] += csum(dh * (xn * g_ref[...]))
        acc_ref[2:3, :] += csum(dn * xn)
        acc_ref[3:4, :] += csum(dx * mix_ref[...])

    row = pl.BlockSpec((ROW_TILE, d), lambda i: (i, 0))
    vec = _full((1, d))
    return pl.pallas_call(
        body, name="ffnnorm_bwd", grid=(n_steps,),
        in_specs=[row, row, row, row, vec, vec, vec],
        out_specs=[row, row, _full((8, d))],
        out_shape=[jax.ShapeDtypeStruct((s, d), F32), jax.ShapeDtypeStruct((s, d), BF16), jax.ShapeDtypeStruct((8, d), F32)],
        compiler_params=_params(("arbitrary",)),
    )(dh2, x1, dy, mix, gain, scale, g1)


def _mixnorm_bwd(dh, x, dx1, gain, scale):
    s, d = x.shape
    n_steps = s // ROW_TILE

    def body(dh_ref, x_ref, dx1_ref, g_ref, sc_ref, gx_ref, acc_ref):
        i = pl.program_id(0)

        @pl.when(i == 0)
        def _():
            acc_ref[...] = jnp.zeros_like(acc_ref)

        dh, x = dh_ref[...], x_ref[...]
        r = _rms(x)
        xn = x * r
        dn = dh * (1.0 + sc_ref[...])
        dxn = dn * g_ref[...]
        gx_ref[...] = dx1_ref[...] + r * (dxn - xn * jnp.mean(dxn * xn, axis=-1, keepdims=True))
        csum = lambda z: jnp.sum(z, axis=0, keepdims=True)
        acc_ref[0:1, :] += csum(dh)
        acc_ref[1:2, :] += csum(dh * (xn * g_ref[...]))
        acc_ref[2:3, :] += csum(dn * xn)

    row = pl.BlockSpec((ROW_TILE, d), lambda i: (i, 0))
    vec = _full((1, d))
    return pl.pallas_call(
        body, name="mixnorm_bwd", grid=(n_steps,),
        in_specs=[row, row, row, vec, vec],
        out_specs=[row, _full((8, d))],
        out_shape=[jax.ShapeDtypeStruct((s, d), F32), jax.ShapeDtypeStruct((8, d), F32)],
        compiler_params=_params(("arbitrary",)),
    )(dh, x, dx1, gain, scale)


def _key_count(d, dilated):
    if not dilated:
        return jnp.where(d >= 0, 1.0, 0.0)
    one = lambda cond: jnp.where(cond, 1.0, 0.0)
    cnt = one(d <= 128) + one(((d & 3) == 0) & (d <= 512)) + one((d & 15) == 0)
    return jnp.where(d >= 0, cnt, 0.0)


def _block_kinds(mla):
    if mla:
        return 1, "diag", "none"
    near = -(-(512 + ATT_TILE) // ATT_TILE)
    return near, "near", "far"


def _scores_t(ka, qa, scale, kind, rel_t, offset):
    st = lax.dot_general(ka, qa, NT, preferred_element_type=F32) * scale
    cnt = None
    if kind == "diag":
        st = jnp.where(rel_t >= 0, st, NEG_INF)
    elif kind == "far":
        st = jnp.where((rel_t & 15) == 0, st, NEG_INF)
    elif kind == "near":
        cnt = _key_count(rel_t + offset, True)
        st = jnp.where(cnt > 0.0, st, NEG_INF)
    return st, cnt


def _attn_fwd(q, k, v, mla, scale, name, gather=()):
    s = q.shape[0]
    qw = 2 * LANE if mla else LANE
    t = ATT_TILE
    nq = s // t
    n_near, kind_near, kind_far = _block_kinds(mla)
    ng = len(gather)
    last_step = HEADS // 2 - 1

    def body(*refs):
        q_ref, k_ref, v_ref = refs[:3]
        o_ref, lse_ref = refs[3 + ng:5 + ng]
        vt_ref = refs[5 + 2 * ng]
        comm = (refs[3:3 + ng], refs[5 + ng:5 + 2 * ng]) + tuple(refs[6 + 2 * ng:])
        if ng:
            @pl.when(pl.program_id(0) == 0)
            def _():
                _Gather(*comm).start()

            @pl.when(pl.program_id(0) == last_step)
            def _():
                _Gather(*comm).forward()

        lane = lax.broadcasted_iota(I32, (1, LANE), 1)
        rel_t = lax.broadcasted_iota(I32, (t, t), 1) - lax.broadcasted_iota(I32, (t, t), 0)

        def transpose_v(j, carry):
            c0 = pl.multiple_of(j * t, t)
            vt_ref[:, pl.ds(c0, t)] = v_ref[pl.ds(c0, t), :].astype(F32).T.astype(BF16)
            return carry

        lax.fori_loop(0, nq, transpose_v, 0)

        def q_block(qi, carry):
            r0 = pl.multiple_of(qi * t, t)
            kcols = [slice(a * LANE, (a + 1) * LANE) if mla else slice(0, LANE) for a in range(2)]
            qas = [q_ref[pl.ds(r0, t), kcols[a]] for a in range(2)]
            if not mla:
                qas = [jnp.where(lane < DIL_DIM, qas[0], jnp.zeros_like(qas[0])),
                       jnp.where(lane >= DIL_DIM, qas[1], jnp.zeros_like(qas[1]))]

            def k_block(kj, c, kind):
                c0 = pl.multiple_of(kj * t, t)
                out = []
                for a in range(2):
                    m, l, acc = c[a]
                    st, cnt = _scores_t(k_ref[pl.ds(c0, t), kcols[a]], qas[a], scale, kind, rel_t, r0 - c0)
                    m_new = jnp.maximum(m, jnp.max(st, axis=0, keepdims=True))
                    alpha = jnp.exp(m - m_new)
                    p = jnp.exp(st - m_new)
                    if cnt is not None:
                        p = p * cnt
                    l = alpha * l + jnp.sum(p, axis=0, keepdims=True)
                    vt = vt_ref[a * DIL_DIM:(a + 1) * DIL_DIM, pl.ds(c0, t)]
                    acc = alpha * acc + jnp.dot(vt, p.astype(BF16), preferred_element_type=F32)
                    out.append((m_new, l, acc))
                return tuple(out)

            one = (jnp.full((1, t), NEG_INF, F32), jnp.zeros((1, t), F32), jnp.zeros((DIL_DIM, t), F32))
            first_near = jnp.maximum(qi + 1 - n_near, 0)
            c = lax.fori_loop(0, first_near, functools.partial(k_block, kind=kind_far), (one, one))
            res = lax.fori_loop(first_near, qi + 1, functools.partial(k_block, kind=kind_near), c)
            o_t = jnp.concatenate([res[a][2] / res[a][1] for a in range(2)], axis=0)
            o_ref[pl.ds(r0, t), :] = o_t.T.astype(BF16)
            for a in range(2):
                lse_ref[a, :, pl.ds(r0, t)] = res[a][0] + jnp.log(res[a][1])
            return carry

        lax.fori_loop(0, nq, q_block, 0)

        if ng:
            @pl.when(pl.program_id(0) == last_step)
            def _():
                _Gather(*comm).finish()

    return pl.pallas_call(
        body, name=name, grid=(HEADS // 2,),
        in_specs=[pl.BlockSpec((s, qw), lambda h: (0, h)), pl.BlockSpec((s, qw), lambda h: (0, h)),
                  pl.BlockSpec((s, LANE), lambda h: (0, h))] + [ANY] * ng,
        out_specs=[pl.BlockSpec((s, LANE), lambda h: (0, h)), pl.BlockSpec((2, 1, s), lambda h: (h, 0, 0))] + [ANY] * ng,
        out_shape=[jax.ShapeDtypeStruct((s, DIL_W), BF16), jax.ShapeDtypeStruct((HEADS, 1, s), F32)] + _Gather.out_shapes(gather),
        scratch_shapes=[pltpu.VMEM((LANE, s), BF16)] + (_Gather.semaphores(ng) if ng else []),
        compiler_params=_params(("arbitrary",) if ng else ("parallel",), 12 << 20),
    )(q, k, v, *gather)


def _attn_bwd(q, k, v, o, do, do_block0, lse, mla, scale, name, scatter=()):
    s = q.shape[0]
    qw = 2 * LANE if mla else LANE
    t = ATT_TILE
    nq = s // t
    n_near, kind_near, kind_far = _block_kinds(mla)
    ns = len(scatter)
    last_step = HEADS // 2 - 1

    def body(*refs):
        q_ref, k_ref, v_ref, o_ref, do_ref, lse_ref = refs[:6]
        dq_ref, dk_ref, dv_ref = refs[6 + ns:9 + ns]
        kt_ref, dot_ref, dob_ref, dqt_ref, delta_ref = refs[9 + 2 * ns:14 + 2 * ns]
        comm = (refs[6:6 + ns], refs[9 + ns:9 + 2 * ns]) + tuple(refs[14 + 2 * ns:])
        if ns:
            @pl.when(pl.program_id(0) == 0)
            def _():
                _Scatter(*comm).start()

        lane = lax.broadcasted_iota(I32, (1, LANE), 1)
        row = lax.broadcasted_iota(I32, (LANE, 1), 0)
        rel_t = lax.broadcasted_iota(I32, (t, t), 1) - lax.broadcasted_iota(I32, (t, t), 0)

        def prepare(j, carry):
            c0 = pl.multiple_of(j * t, t)
            do_blk = do_ref[pl.ds(c0, t), :]
            dob_ref[pl.ds(c0, t), :] = do_blk.astype(BF16)
            do_t = do_blk.T
            dot_ref[:, pl.ds(c0, t)] = do_t.astype(BF16)
            prod = do_t * o_ref[pl.ds(c0, t), :].astype(F32).T
            delta_ref[0, :, pl.ds(c0, t)] = jnp.sum(prod[0:DIL_DIM], axis=0, keepdims=True)
            delta_ref[1, :, pl.ds(c0, t)] = jnp.sum(prod[DIL_DIM:LANE], axis=0, keepdims=True)
            for w in range(qw // LANE):
                kt_ref[w * LANE:(w + 1) * LANE, pl.ds(c0, t)] = (
                    k_ref[pl.ds(c0, t), w * LANE:(w + 1) * LANE].astype(F32).T.astype(BF16))
            return carry

        lax.fori_loop(0, nq, prepare, 0)
        dqt_ref[...] = jnp.zeros_like(dqt_ref)

        sels = [lane < DIL_DIM, lane >= DIL_DIM]
        rsels = [row < DIL_DIM, row >= DIL_DIM]
        cols = [slice(a * LANE, (a + 1) * LANE) if mla else slice(0, LANE) for a in range(2)]

        def k_block(kj, carry):
            c0 = pl.multiple_of(kj * t, t)
            kas = [k_ref[pl.ds(c0, t), cols[a]] for a in range(2)]
            kts = [kt_ref[cols[a], pl.ds(c0, t)] for a in range(2)]
            if not mla:
                kas = [jnp.where(sels[a], kas[a], jnp.zeros_like(kas[a])) for a in range(2)]
                kts = [jnp.where(rsels[a], kts[a], jnp.zeros_like(kts[a])) for a in range(2)]
            vb = v_ref[pl.ds(c0, t), :]
            vbs = [jnp.where(sels[a], vb, jnp.zeros_like(vb)) for a in range(2)]

            def q_block(qi, c, kind):
                r0 = pl.multiple_of(qi * t, t)
                out, dq_parts = [], []
                for a in range(2):
                    dk_acc, dv_acc = c[a]
                    qa = q_ref[pl.ds(r0, t), cols[a]]
                    st, cnt = _scores_t(kas[a], qa, scale, kind, rel_t, r0 - c0)
                    p = jnp.exp(st - lse_ref[a, :, pl.ds(r0, t)])
                    if cnt is not None:
                        p = p * cnt
                    dp = jnp.dot(vbs[a], dot_ref[:, pl.ds(r0, t)], preferred_element_type=F32)
                    ds = (p * (dp - delta_ref[a, :, pl.ds(r0, t)]) * scale).astype(BF16)
                    dv_acc = dv_acc + jnp.dot(p.astype(BF16), dob_ref[pl.ds(r0, t), :], preferred_element_type=F32)
                    dk_acc = dk_acc + jnp.dot(ds, qa, preferred_element_type=F32)
                    dq_parts.append(jnp.dot(kts[a], ds, preferred_element_type=F32))
                    out.append((dk_acc, dv_acc))
                if mla:
                    for a in range(2):
                        dqt_ref[cols[a], pl.ds(r0, t)] += dq_parts[a]
                else:
                    dqt_ref[:, pl.ds(r0, t)] += dq_parts[0] + dq_parts[1]
                return tuple(out)

            zero = jnp.zeros((t, LANE), F32)
            last_near = jnp.minimum(kj + n_near, nq)
            c = lax.fori_loop(kj, last_near, functools.partial(q_block, kind=kind_near), ((zero, zero), (zero, zero)))
            (dk0, dv0), (dk1, dv1) = lax.fori_loop(last_near, nq, functools.partial(q_block, kind=kind_far), c)
            if mla:
                dk_ref[pl.ds(c0, t), cols[0]] = dk0
                dk_ref[pl.ds(c0, t), cols[1]] = dk1
            else:
                dk_ref[pl.ds(c0, t), :] = jnp.where(sels[0], dk0, dk1)
            dv_ref[pl.ds(c0, t), :] = jnp.where(sels[0], dv0, dv1)
            return carry

        lax.fori_loop(0, nq, k_block, 0)

        def write_dq(j, carry):
            c0 = pl.multiple_of(j * t, t)
            for w in range(qw // LANE):
                dq_ref[pl.ds(c0, t), w * LANE:(w + 1) * LANE] = dqt_ref[w * LANE:(w + 1) * LANE, pl.ds(c0, t)].T
            return carry

        lax.fori_loop(0, nq, write_dq, 0)

        if ns:
            @pl.when(pl.program_id(0) == last_step)
            def _():
                _Scatter(*comm).finish()

    b0 = do_block0
    return pl.pallas_call(
        body, name=name, grid=(HEADS // 2,),
        in_specs=[pl.BlockSpec((s, qw), lambda h: (0, h)), pl.BlockSpec((s, qw), lambda h: (0, h)),
                  pl.BlockSpec((s, LANE), lambda h: (0, h)), pl.BlockSpec((s, LANE), lambda h: (0, h)),
                  pl.BlockSpec((s, LANE), lambda h: (0, h + b0)), pl.BlockSpec((2, 1, s), lambda h: (h, 0, 0))] + [ANY] * ns,
        out_specs=[pl.BlockSpec((s, qw), lambda h: (0, h)), pl.BlockSpec((s, qw), lambda h: (0, h)),
                   pl.BlockSpec((s, LANE), lambda h: (0, h))] + [ANY] * ns,
        out_shape=[jax.ShapeDtypeStruct(q.shape, F32), jax.ShapeDtypeStruct(k.shape, F32), jax.ShapeDtypeStruct((s, DIL_W), F32)]
        + _Scatter.out_shapes(scatter),
        scratch_shapes=[pltpu.VMEM((qw, s), BF16), pltpu.VMEM((LANE, s), BF16), pltpu.VMEM((s, LANE), BF16),
                        pltpu.VMEM((qw, s), F32), pltpu.VMEM((2, 1, s), F32)] + (_Scatter.semaphores(ns) if ns else []),
        compiler_params=_params(("arbitrary",) if ns else ("parallel",), 24 << 20),
    )(q, k, v, o, do, lse, *scatter)


def _ada_fwd(c_all, w_shard, b_shard):
    n, d = c_all.shape
    cols = w_shard.shape[1]

    def body(c_ref, w_ref, b_ref, o_ref):
        cv = c_ref[...]
        sc = (cv * _sigmoid(cv)).astype(BF16)
        o_ref[...] = jnp.dot(sc, w_ref[...].astype(BF16), preferred_element_type=F32) + b_ref[...]

    return pl.pallas_call(
        body, name="ada_fwd", out_shape=jax.ShapeDtypeStruct((n, cols), F32),
        compiler_params=_params(None, 16 << 20),
    )(c_all, w_shard, b_shard)


def _ada_bwd(c_all, dmod_shard):
    n, d = c_all.shape
    cols = dmod_shard.shape[1]

    def body(c_ref, g_ref, o_ref):
        cv = c_ref[...]
        o_ref[...] = lax.dot_general(cv * _sigmoid(cv), g_ref[...], TN, precision=HIGHEST, preferred_element_type=F32)

    return pl.pallas_call(
        body, name="ada_bwd", out_shape=jax.ShapeDtypeStruct((d, cols), F32),
        compiler_params=_params(None, 16 << 20),
    )(c_all, dmod_shard)


def _sum_devices(g):
    n, r, w = g.shape

    def body(g_ref, o_ref):
        acc = g_ref[0]
        for k in range(1, n):
            acc = acc + g_ref[k]
        o_ref[...] = acc

    return pl.pallas_call(
        body, name="sum_devices", out_shape=jax.ShapeDtypeStruct((r, w), F32),
        compiler_params=_params(None, 4 << 20),
    )(g)


def _adamw(w, g, m, v, name):
    r, c = w.shape
    tr = r
    for cand in (256, 128, 64, 32, 16, 8):
        if r % cand == 0 and r > cand:
            tr = cand
            break

    def body(w_ref, g_ref, m_ref, v_ref, d_ref, mo_ref, vo_ref):
        gv = g_ref[...]
        mn = ADAM_B1 * m_ref[...] + (1.0 - ADAM_B1) * gv
        vn = ADAM_B2 * v_ref[...] + (1.0 - ADAM_B2) * (gv * gv)
        m_hat = mn / (1.0 - ADAM_B1 ** ADAM_STEP)
        v_hat = vn / (1.0 - ADAM_B2 ** ADAM_STEP)
        d_ref[...] = -ADAM_LR * (m_hat / (jnp.sqrt(v_hat) + ADAM_EPS) + ADAM_WD * w_ref[...])
        mo_ref[...] = mn
        vo_ref[...] = vn

    blk = pl.BlockSpec((tr, c), lambda i: (i, 0))
    return pl.pallas_call(
        body, name=name, grid=(r // tr,), in_specs=[blk] * 4, out_specs=[blk] * 3,
        out_shape=[jax.ShapeDtypeStruct((r, c), F32)] * 3,
        compiler_params=_params(("parallel",), 7 * _nbytes((tr, c), F32)),
    )(w, g, m, v)


def _position():
    return lax.axis_index("x"), lax.axis_index("y"), lax.axis_index("c")


def _other_chips(x, y):
    return [(1 - x, y, 2 * (1 - x) + y), (x, 1 - y, 2 * x + (1 - y)), (1 - x, 1 - y, 2 * (1 - x) + (1 - y))]


def _ag_small(v, name):
    r, w = v.shape

    def body(v_ref, out_ref, send_sems, recv_sems, local_sem):
        x, y, c = _position()
        me = 4 * x + 2 * y + c
        mine = pltpu.make_async_copy(v_ref, out_ref.at[me], local_sem)
        mine.start()
        peers = []
        for k in range(1, N_DEV):
            fx, fy, fc = (k >> 2) & 1, (k >> 1) & 1, k & 1
            px = 1 - x if fx else x
            py = 1 - y if fy else y
            pc = 1 - c if fc else c
            peers.append((px, py, pc))
        sends = []
        for k, peer in enumerate(peers):
            cp = pltpu.make_async_remote_copy(src_ref=v_ref, dst_ref=out_ref.at[me], send_sem=send_sems.at[k],
                                              recv_sem=recv_sems.at[k], device_id=peer, device_id_type=MESH)
            cp.start()
            sends.append(cp)
        for k, (px, py, pc) in enumerate(peers):
            pltpu.make_async_remote_copy(src_ref=v_ref, dst_ref=out_ref.at[4 * px + 2 * py + pc], send_sem=send_sems.at[k],
                                         recv_sem=recv_sems.at[k], device_id=(px, py, pc), device_id_type=MESH).wait_recv()
        for cp in sends:
            cp.wait_send()
        mine.wait()

    return pl.pallas_call(
        body, name=name,
        out_shape=jax.ShapeDtypeStruct((N_DEV, r, w), F32),
        in_specs=[pl.BlockSpec(memory_space=pltpu.VMEM)],
        out_specs=pl.BlockSpec(memory_space=pltpu.VMEM),
        scratch_shapes=[pltpu.SemaphoreType.DMA((N_DEV - 1,)), pltpu.SemaphoreType.DMA((N_DEV - 1,)), pltpu.SemaphoreType.DMA],
        compiler_params=_params(None, 10 * _nbytes((r, w), F32)),
    )(v)


ANY = pl.BlockSpec(memory_space=pl.ANY)


def _ag_weights(shards, name):
    n = len(shards)

    def body(*refs):
        gather = _Gather(refs[:n], refs[n:2 * n], *refs[2 * n:])
        gather.start()
        gather.forward()
        gather.finish()

    return pl.pallas_call(
        body, name=name,
        out_shape=_Gather.out_shapes(shards), in_specs=[ANY] * n, out_specs=[ANY] * n,
        scratch_shapes=_Gather.semaphores(n),
    )(*shards)


class _Gather:
    def __init__(self, w_refs, out_refs, send_sems, recv_sems):
        x, y, c = _position()
        q0 = 2 * x + y
        sibling = (x, y, 1 - c)
        self.ici, self.ici_in, self.fwd, self.fwd_in = [], [], [], []
        for k, (w_ref, out_ref) in enumerate(zip(w_refs, out_refs)):
            half = w_ref.shape[0] // 2

            def blk(q, e, out_ref=out_ref, half=half):
                return out_ref.at[q, pl.ds(pl.multiple_of(e * half, 16), half), :]

            def copy(src, dst, i, to):
                return pltpu.make_async_remote_copy(src_ref=src, dst_ref=dst, send_sem=send_sems.at[i], recv_sem=recv_sems.at[i],
                                                    device_id=to, device_id_type=MESH)

            src = w_ref.at[pl.ds(pl.multiple_of(c * half, 16), half), :]
            for j, (cx, cy, qj) in enumerate(_other_chips(x, y)):
                self.ici.append(copy(src, blk(q0, c), 6 * k + j, (cx, cy, c)))
                self.ici_in.append(copy(blk(qj, c), blk(qj, c), 6 * k + j, (cx, cy, c)))
                self.fwd.append(copy(blk(qj, c), blk(qj, c), 6 * k + 3 + j, sibling))
                self.fwd_in.append(copy(blk(qj, 1 - c), blk(qj, 1 - c), 6 * k + 3 + j, sibling))

    @staticmethod
    def out_shapes(shards):
        return [jax.ShapeDtypeStruct((N_CHIP,) + s.shape, s.dtype) for s in shards]

    @staticmethod
    def semaphores(n):
        return [pltpu.SemaphoreType.DMA((6 * n,)), pltpu.SemaphoreType.DMA((6 * n,))]

    def start(self):
        for cp in self.ici:
            cp.start()

    def forward(self):
        for arrived, onward in zip(self.ici_in, self.fwd):
            arrived.wait_recv()
            onward.start()

    def finish(self):
        for cp in self.fwd_in:
            cp.wait_recv()
        for cp in self.ici + self.fwd:
            cp.wait_send()


def _swap_halves_d2d(grads, name):
    n = len(grads)

    def body(*refs):
        g_refs, out_refs = refs[:n], refs[n:2 * n]
        send_sems, recv_sems = refs[2 * n:]
        x, y, c = _position()
        sibling = (x, y, 1 - c)
        cps = []
        for k in range(n):
            cp = pltpu.make_async_remote_copy(src_ref=g_refs[k].at[:, 1 - c], dst_ref=out_refs[k], send_sem=send_sems.at[k],
                                              recv_sem=recv_sems.at[k], device_id=sibling, device_id_type=MESH)
            cp.start()
            cps.append(cp)
        for cp in cps:
            cp.wait_recv()
        for cp in cps:
            cp.wait_send()

    return pl.pallas_call(
        body, name=name,
        out_shape=[jax.ShapeDtypeStruct((N_CHIP,) + g.shape[2:], g.dtype) for g in grads],
        in_specs=[ANY] * n, out_specs=[ANY] * n,
        scratch_shapes=[pltpu.SemaphoreType.DMA((n,)), pltpu.SemaphoreType.DMA((n,))],
    )(*grads)


def _pair_sum(g, a, c_idx, name):
    _, _, rh, cols = g.shape
    tr = rh
    for cand in (256, 128, 64, 32, 16):
        if rh % cand == 0 and rh > cand:
            tr = cand
            break

    def body(c_ref, g_ref, a_ref, o_ref):
        o_ref[...] = (g_ref[...] + a_ref[...]).astype(BF16)

    return pl.pallas_call(
        body, name=name,
        grid_spec=pltpu.PrefetchScalarGridSpec(
            num_scalar_prefetch=1, grid=(N_CHIP, rh // tr),
            in_specs=[pl.BlockSpec((None, None, tr, cols), lambda q, i, c_ref: (q, c_ref[0], i, 0)),
                      pl.BlockSpec((None, tr, cols), lambda q, i, c_ref: (q, i, 0))],
            out_specs=pl.BlockSpec((None, tr, cols), lambda q, i, c_ref: (q, i, 0))),
        out_shape=jax.ShapeDtypeStruct((N_CHIP, rh, cols), BF16),
        compiler_params=_params(("parallel", "parallel"), 10 * _nbytes((tr, cols), F32)),
    )(c_idx, g, a)


def _scatter_partials(parts, name):
    n = len(parts)

    def body(*refs):
        scatter = _Scatter(refs[:n], refs[n:2 * n], *refs[2 * n:])
        scatter.start()
        scatter.finish()

    return pl.pallas_call(
        body, name=name,
        out_shape=_Scatter.out_shapes(parts), in_specs=[ANY] * n, out_specs=[ANY] * n,
        scratch_shapes=_Scatter.semaphores(n),
    )(*parts)


class _Scatter:
    def __init__(self, p_refs, out_refs, send_sems, recv_sems):
        x, y, c = _position()
        self.copies = []
        for k, (p_ref, out_ref) in enumerate(zip(p_refs, out_refs)):
            for j, (cx, cy, qj) in enumerate(_other_chips(x, y)):
                self.copies.append(pltpu.make_async_remote_copy(
                    src_ref=p_ref.at[qj], dst_ref=out_ref.at[j], send_sem=send_sems.at[3 * k + j],
                    recv_sem=recv_sems.at[3 * k + j], device_id=(cx, cy, c), device_id_type=MESH))

    @staticmethod
    def out_shapes(parts):
        return [jax.ShapeDtypeStruct((3,) + p.shape[1:], p.dtype) for p in parts]

    @staticmethod
    def semaphores(n):
        return [pltpu.SemaphoreType.DMA((3 * n,)), pltpu.SemaphoreType.DMA((3 * n,))]

    def start(self):
        for cp in self.copies:
            cp.start()

    def finish(self):
        for cp in self.copies:
            cp.wait_recv()
        for cp in self.copies:
            cp.wait_send()


def _shard_sum(p, b, q_idx, name):
    _, rh, cols = p.shape
    tr = rh
    for cand in (256, 128, 64, 32, 16):
        if rh % cand == 0 and rh > cand:
            tr = cand
            break

    def body(q_ref, p_ref, b_ref, o_ref):
        acc = p_ref[...].astype(F32)
        for j in range(3):
            acc = acc + b_ref[j].astype(F32)
        o_ref[...] = acc

    return pl.pallas_call(
        body, name=name,
        grid_spec=pltpu.PrefetchScalarGridSpec(
            num_scalar_prefetch=1, grid=(rh // tr,),
            in_specs=[pl.BlockSpec((None, tr, cols), lambda i, q_ref: (q_ref[0], i, 0)),
                      pl.BlockSpec((3, tr, cols), lambda i, q_ref: (0, i, 0))],
            out_specs=pl.BlockSpec((tr, cols), lambda i, q_ref: (i, 0))),
        out_shape=jax.ShapeDtypeStruct((rh, cols), F32),
        compiler_params=_params(("parallel",), 8 * _nbytes((tr, cols), F32)),
    )(q_idx, p, b)


def _join_halves(halves):
    n = len(halves)

    def body(*refs):
        h_refs, out_refs = refs[:n], refs[n:2 * n]
        send_sems, recv_sems = refs[2 * n:]
        x, y, c = _position()
        sibling = (x, y, 1 - c)
        cps = []
        for k in range(n):
            cp = pltpu.make_async_remote_copy(src_ref=h_refs[k], dst_ref=out_refs[k], send_sem=send_sems.at[k],
                                              recv_sem=recv_sems.at[k], device_id=sibling, device_id_type=MESH)
            cp.start()
            cps.append(cp)
        for cp in cps:
            cp.wait_recv()
        for cp in cps:
            cp.wait_send()

    return pl.pallas_call(
        body, name="rs_join",
        out_shape=[jax.ShapeDtypeStruct(h.shape, h.dtype) for h in halves],
        in_specs=[ANY] * n, out_specs=[ANY] * n,
        scratch_shapes=[pltpu.SemaphoreType.DMA((n,)), pltpu.SemaphoreType.DMA((n,))],
    )(*halves)


def _cols_from_shards(g):
    q, r, cs = g.shape
    return jnp.transpose(g, (1, 0, 2)).reshape(r, q * cs)


def _cols_to_shards(w):
    r, cfull = w.shape
    return jnp.transpose(w.reshape(r, N_CHIP, cfull // N_CHIP), (1, 0, 2))


def _pad_w_in(w):
    z = lambda n: jnp.zeros((w.shape[0], n), w.dtype)
    q_lat, kv_lat, kpe = w[:, 0:512], w[:, 512:768], w[:, 768:800]
    qd, kd, vd = w[:, 800:1312], w[:, 1312:1824], w[:, 1824:2336]
    return jnp.concatenate([q_lat, qd, kd, vd, kv_lat, z(KPE_OFF), kpe, z(LANE - KPE_OFF - ROPE)], axis=1)


def _unpad_w_in(g):
    return jnp.concatenate([g[:, P_QLAT:P_QLAT + Q_LORA], g[:, P_KVLAT:P_KVLAT + KV_LORA],
                            g[:, P_KPE + KPE_OFF:P_KPE + KPE_OFF + ROPE], g[:, P_QD:P_QD + 3 * DIL_W]], axis=1)


def _pad_w_qb(w):
    w3 = w.reshape(Q_LORA, HEADS, NOPE + ROPE)
    return jnp.pad(w3, ((0, 0), (0, 0), (0, LANE - NOPE - ROPE))).reshape(Q_LORA, HEADS * LANE)


def _unpad_w_qb(g):
    return g.reshape(Q_LORA, HEADS, LANE)[:, :, :NOPE + ROPE].reshape(Q_LORA, HEADS * (NOPE + ROPE))


def _pad_w_kvb(w):
    w3 = w.reshape(KV_LORA, HEADS, 2 * NOPE)
    kp = jnp.pad(w3[:, :, :NOPE], ((0, 0), (0, 0), (0, LANE - NOPE))).reshape(KV_LORA, HEADS * LANE)
    return jnp.concatenate([kp, w3[:, :, NOPE:].reshape(KV_LORA, DIL_W)], axis=1)


def _unpad_w_kvb(g):
    gk = g[:, :HEADS * LANE].reshape(KV_LORA, HEADS, LANE)[:, :, :NOPE]
    gv = g[:, HEADS * LANE:].reshape(KV_LORA, HEADS, NOPE)
    return jnp.concatenate([gk, gv], axis=2).reshape(KV_LORA, HEADS * 2 * NOPE)


def _head_gains(g_q_nope, g_q_pe, g_k_nope, g_k_pe, g_dq, g_dk):
    z = lambda n: jnp.zeros((1, n), F32)
    q1 = jnp.concatenate([g_q_nope, g_q_pe, z(LANE - NOPE - ROPE)], axis=1)
    k1 = jnp.concatenate([g_k_nope, z(LANE - NOPE)], axis=1)
    kpe = jnp.concatenate([z(KPE_OFF), g_k_pe, z(LANE - KPE_OFF - ROPE)], axis=1)
    return dict(q=jnp.tile(q1, (1, HEADS)), k=jnp.tile(k1, (1, HEADS)), kpe=kpe,
                dq=jnp.tile(g_dq, (1, HEADS)), dk=jnp.tile(g_dk, (1, HEADS)))


SMALL_NAMES = ("g_mix_norm", "g_q_lat", "g_kv_lat", "g_mla_q_nope", "g_mla_q_pe", "g_mla_k_nope", "g_mla_k_pe",
               "g_dil_q", "g_dil_k", "g_ffn_norm", "b_conv")


def _pack(vs):
    parts, spans, off = [], [], 0
    for v in vs:
        n = v.shape[1]
        npad = -(-n // LANE) * LANE
        parts.append(jnp.pad(v, ((0, 0), (0, npad - n))))
        spans.append((off, n))
        off += npad
    return jnp.concatenate(parts, axis=1), spans


def kernel(x, c, positions, w_ada, b_ada, g_mix_norm, w_in, g_q_lat, w_q_b, g_kv_lat, w_kv_b, g_mla_q_nope, g_mla_q_pe, g_mla_k_nope, g_mla_k_pe, g_dil_q, g_dil_k, w_o, g_ffn_norm, w_up, w_conv, b_conv, w_down, loss_target, m_w_ada, m_b_ada, m_g_mix_norm, m_w_in, m_g_q_lat, m_w_q_b, m_g_kv_lat, m_w_kv_b, m_g_mla_q_nope, m_g_mla_q_pe, m_g_mla_k_nope, m_g_mla_k_pe, m_g_dil_q, m_g_dil_k, m_w_o, m_g_ffn_norm, m_w_up, m_w_conv, m_b_conv, m_w_down, v_w_ada, v_b_ada, v_g_mix_norm, v_w_in, v_g_q_lat, v_w_q_b, v_g_kv_lat, v_w_kv_b, v_g_mla_q_nope, v_g_mla_q_pe, v_g_mla_k_nope, v_g_mla_k_pe, v_g_dil_q, v_g_dil_k, v_w_o, v_g_ffn_norm, v_w_up, v_w_conv, v_b_conv, v_w_down):
    args = dict(locals())
    weights = {n: args[n][0] for n in ("w_ada", "w_in", "w_q_b", "w_kv_b", "w_o", "w_up", "w_conv", "w_down")}
    small_w = {n: args[n] for n in SMALL_NAMES + ("b_ada",)}
    mom_m = {n[2:]: (args[n][0] if args[n].ndim == 3 else args[n]) for n in args if n.startswith("m_")}
    mom_v = {n[2:]: (args[n][0] if args[n].ndim == 3 else args[n]) for n in args if n.startswith("v_")}

    xi, yi, ci = _position()
    q0 = 2 * xi + yi
    me = 4 * xi + 2 * yi + ci
    xs, tgt = x[0], loss_target[0]
    s = xs.shape[0]
    consts = _seg_consts()
    c_idx, q_idx = jnp.reshape(ci, (1,)).astype(I32), jnp.reshape(q0, (1,)).astype(I32)

    def halves(g4):
        q, r, cc = g4.shape
        return g4.reshape(q, 2, r // 2, cc)

    c_all = _ag_small(c, "ag_c")[:, 0, :]
    ada_cols = w_ada.shape[2]
    b_shard = lax.dynamic_slice_in_dim(b_ada, q0 * ada_cols, ada_cols, axis=1)
    mod_blk = _ada_fwd(c_all, weights["w_ada"], b_shard)
    mod_all = _ag_small(mod_blk, "ag_mod").reshape(N_CHIP, 2, N_DEV, ada_cols)
    mod = lax.dynamic_index_in_dim(lax.dynamic_index_in_dim(mod_all, ci, 1, False), me, 1, False)
    mod = mod.reshape(1, N_CHIP * ada_cols)
    sh1, sc1, g1, sh2, sc2, g2 = [mod[:, k * D_MODEL:(k + 1) * D_MODEL] for k in range(6)]

    place_own = lambda gs, ws: [lax.dynamic_update_slice_in_dim(g, w[None], q0, axis=0) for g, w in zip(gs, ws)]
    own_first = [weights[n].astype(BF16) for n in ("w_in", "w_q_b", "w_kv_b")]
    own_later = [weights[n].astype(BF16) for n in ("w_o", "w_up", "w_down")]
    gathered = place_own(_ag_weights(own_first, "ag_weights"), own_first)
    w_in_p = _pad_w_in(_cols_from_shards(gathered[0]))
    w_qb_p = _pad_w_qb(_cols_from_shards(gathered[1]))
    w_kvb_p = _pad_w_kvb(_cols_from_shards(gathered[2]))
    w_conv_f = _ag_small(weights["w_conv"], "ag_wconv")
    w_conv_f = jnp.transpose(w_conv_f.reshape(N_CHIP, 2, 3, -1)[:, 0], (1, 0, 2)).reshape(3, UP_W)

    gains = _head_gains(g_mla_q_nope, g_mla_q_pe, g_mla_k_nope, g_mla_k_pe, g_dil_q, g_dil_k)
    tab = _rope_tables(positions.reshape(s, 1), *_rope_consts())

    h = _prenorm(xs, g_mix_norm, sc1, sh1, "prenorm")
    proj = _mm(h, w_in_p, "nn", F32, 512, P_COLS, "mm_in")
    ql, kvl = _latnorm(proj, g_q_lat, g_kv_lat)
    q_raw = _mm(ql, w_qb_p, "nn", F32, 512, HEADS * LANE, "mm_qb")
    kv_raw = _mm(kvl, w_kvb_p, "nn", F32, 512, HEADS * LANE + DIL_W, "mm_kvb")
    qm, km, vm, qd, kd, vd = _attn_prep(q_raw, kv_raw, proj, tab, gains, consts)
    scale_m, scale_d = (NOPE + ROPE) ** -0.5, DIL_DIM ** -0.5
    o_m, lse_m, *gathered = _attn_fwd(qm, km, vm, True, scale_m, "attn_mla", gather=own_later)
    gathered = place_own(gathered, own_later)
    w_o_f = gathered[0].reshape(D_MODEL, D_MODEL)
    w_up_f = _cols_from_shards(gathered[1])
    w_down_f = gathered[2].reshape(D_FF, D_MODEL)
    o_d, lse_d = _attn_fwd(qd, kd, vd, False, scale_d, "attn_dil")
    mix_in = jnp.concatenate([o_m, o_d], axis=1)
    mix = _mm(mix_in, w_o_f, "nn", F32, 512, D_MODEL, "mm_o")
    x1, h2 = _resid_prenorm(xs, mix, g1, g_ffn_norm, sc2, sh2)
    up = _mm(h2, w_up_f, "nn", F32, 512, CONV_TILE, "mm_up")
    act = _conv_gate(up, w_conv_f, b_conv)
    ffn = _mm(act, w_down_f, "nn", F32, 256, D_MODEL, "mm_down")
    dy, dffn, dg2, loss_part = _final(x1, ffn, tgt, g2)

    da = _mm(dffn, w_down_f, "nt", F32, 512, CONV_TILE, "mm_down_dx")
    gw_down = _mm(act, dffn, "tn", F32, 256, D_MODEL, "mm_down_dw")
    dug, duv, dbg, dbv, dwg, dwv = _gate_bwd(up, da, w_conv_f, b_conv)
    dup = jnp.concatenate([_conv_bwd(dug, w_conv_f[:, :D_FF], "conv_bwd_gate"),
                           _conv_bwd(duv, w_conv_f[:, D_FF:], "conv_bwd_val")], axis=1)
    dh2 = _mm(dup, w_up_f, "nt", F32, 256, 512, "mm_up_dx")
    gw_up = _mm(h2, dup, "tn", F32, 512, CONV_TILE, "mm_up_dw")
    dx1, dmix, acc2 = _ffnnorm_bwd(dh2, x1, dy, mix, g_ffn_norm, sc2, g1)
    dmix_in = _mm(dmix, w_o_f, "nt", F32, 512, D_MODEL, "mm_o_dx")
    gw_o = _mm(mix_in, dmix, "tn", F32, 512, D_MODEL, "mm_o_dw")
    early_names = ("w_up", "w_down")
    early = [halves(_cols_to_shards(gw_up)), halves(gw_down.reshape(N_CHIP, D_FF // N_CHIP, D_MODEL))]
    early_sib = _swap_halves_d2d(early, "rs_pair_swap_early")
    early_sums = [_pair_sum(g, a, c_idx, "pair_sum_" + n) for g, a, n in zip(early, early_sib, early_names)]
    dqm, dkm, dvm, *early_recv = _attn_bwd(qm, km, vm, o_m, dmix_in, 0, lse_m, True, scale_m, "attn_mla_bwd",
                                           scatter=early_sums)
    dqd, dkd, dvd = _attn_bwd(qd, kd, vd, o_d, dmix_in, DIL_W // LANE, lse_d, False, scale_d, "attn_dil_bwd")
    dq_raw, dkv_raw, dkpe_b, dqd_b, dkd_b, dvd_b, dgains = _attn_prep_bwd(
        dqm, dkm, dvm, dqd, dkd, dvd, q_raw, kv_raw, proj, tab, gains, consts)
    dql = _mm(dq_raw, w_qb_p, "nt", F32, 512, Q_LORA, "mm_qb_dx")
    gw_qb = _unpad_w_qb(_mm(ql, dq_raw, "tn", F32, Q_LORA, HEADS * LANE, "mm_qb_dw"))
    dkvl = _mm(dkv_raw, w_kvb_p, "nt", F32, 512, KV_LORA, "mm_kvb_dx")
    gw_kvb = _unpad_w_kvb(_mm(kvl, dkv_raw, "tn", F32, KV_LORA, HEADS * LANE + DIL_W, "mm_kvb_dw"))
    dqlat_b, dkvlat_b, dglat = _latnorm_bwd(dql, dkvl, proj, g_q_lat, g_kv_lat)
    dproj = jnp.concatenate([dqlat_b, dqd_b, dkd_b, dvd_b, dkvlat_b, dkpe_b], axis=1)
    dh = _mm(dproj, w_in_p, "nt", F32, 512, D_MODEL, "mm_in_dx")
    gw_in = _unpad_w_in(_mm(h, dproj, "tn", F32, 512, P_COLS, "mm_in_dw"))
    grad_x, acc1 = _mixnorm_bwd(dh, xs, dx1, g_mix_norm, sc1)

    dmod = jnp.concatenate([acc1[0:1], acc1[1:2], acc2[3:4], acc2[0:1], acc2[1:2], dg2], axis=1)
    small_g = {"g_mix_norm": acc1[2:3], "g_q_lat": dglat[0:1], "g_kv_lat": dglat[1:2, :KV_LORA],
               "g_mla_q_nope": dgains[0:1, :NOPE], "g_mla_q_pe": dgains[0:1, NOPE:NOPE + ROPE],
               "g_mla_k_nope": dgains[1:2, :NOPE], "g_mla_k_pe": dgains[2:3, KPE_OFF:KPE_OFF + ROPE],
               "g_dil_q": dgains[3:4, :DIL_DIM], "g_dil_k": dgains[4:5, :DIL_DIM], "g_ffn_norm": acc2[2:3],
               "b_conv": jnp.concatenate([dbg, dbv], axis=1)}
    dw_conv = jnp.concatenate([dwg, dwv], axis=1)
    packed, spans = _pack([dmod] + [small_g[n] for n in SMALL_NAMES] + [dw_conv[k:k + 1] for k in range(3)])
    gathered_small = _ag_small(packed, "ag_small")
    summed = _sum_devices(gathered_small)
    take = lambda k: summed[:, spans[k][0]:spans[k][0] + spans[k][1]]
    grads = {"b_ada": take(0)}
    for k, n in enumerate(SMALL_NAMES):
        grads[n] = take(1 + k)
    shard_cols = UP_W // N_CHIP
    gconv_full = jnp.concatenate([take(1 + len(SMALL_NAMES) + k) for k in range(3)], axis=0)
    grads["w_conv"] = lax.dynamic_slice_in_dim(gconv_full, q0 * shard_cols, shard_cols, axis=1)
    dmod_all = gathered_small[:, 0, :6 * D_MODEL]
    grads["w_ada"] = _ada_bwd(c_all, lax.dynamic_slice_in_dim(dmod_all, q0 * ada_cols, ada_cols, axis=1))

    late_names = ("w_in", "w_q_b", "w_kv_b", "w_o")
    late = [halves(_cols_to_shards(gw_in)), halves(_cols_to_shards(gw_qb)), halves(_cols_to_shards(gw_kvb)),
            halves(gw_o.reshape(N_CHIP, D_MODEL // N_CHIP, D_MODEL))]
    late_sib = _swap_halves_d2d(late, "rs_pair_swap_late")
    late_sums = [_pair_sum(g, a, c_idx, "pair_sum_" + n) for g, a, n in zip(late, late_sib, late_names)]
    late_recv = _scatter_partials(late_sums, "rs_scatter_late")
    big_names = late_names + early_names
    half_sums = [_shard_sum(p, b, q_idx, "shard_sum_" + n)
                 for p, b, n in zip(late_sums + early_sums, list(late_recv) + list(early_recv), big_names)]
    from_sib = _join_halves(half_sums)
    south = ci == 0
    for n, mine, theirs in zip(big_names, half_sums, from_sib):
        grads[n] = jnp.concatenate([jnp.where(south, mine, theirs), jnp.where(south, theirs, mine)], axis=0)

    delta, new_m, new_v = {}, {}, {}
    for n in ("w_ada", "w_in", "w_q_b", "w_kv_b", "w_o", "w_up", "w_conv", "w_down"):
        delta[n], new_m[n], new_v[n] = _adamw(weights[n], grads[n], mom_m[n], mom_v[n], "adamw_" + n)
    vec_names = ("b_ada",) + SMALL_NAMES
    pw, vspans = _pack([small_w[n] for n in vec_names])
    pg, _ = _pack([grads[n] for n in vec_names])
    pm, _ = _pack([mom_m[n] for n in vec_names])
    pv, _ = _pack([mom_v[n] for n in vec_names])
    rows8 = lambda z: z.reshape(8, z.shape[1] // 8)
    pad_mask, _ = _pack([jnp.ones_like(small_w[n]) for n in vec_names])
    pv = jnp.where(pad_mask > 0, pv, 1.0)
    sd, sm, sv = _adamw(rows8(pw), rows8(pg), rows8(pm), rows8(pv), "adamw_small")
    for k, n in enumerate(vec_names):
        o, ln = vspans[k]
        delta[n], new_m[n], new_v[n] = (z.reshape(1, -1)[:, o:o + ln] for z in (sd, sm, sv))

    loss = lax.psum(loss_part[0, 0], ("x", "y", "c"))
    order = ("w_ada", "b_ada", "g_mix_norm", "w_in", "g_q_lat", "w_q_b", "g_kv_lat", "w_kv_b", "g_mla_q_nope", "g_mla_q_pe",
             "g_mla_k_nope", "g_mla_k_pe", "g_dil_q", "g_dil_k", "w_o", "g_ffn_norm", "w_up", "w_conv", "b_conv", "w_down")
    lead = lambda n, z: z[None] if n.startswith("w_") else z
    outs = [loss, grad_x[None]]
    for d_ in (grads, delta, new_m, new_v):
        outs += [lead(n, d_[n]) for n in order]
    return tuple(outs)
```

```python
import functools

import numpy as np
import jax
import jax.numpy as jnp
from jax import lax
from jax.experimental import pallas as pl
from jax.experimental.pallas import tpu as pltpu

F32 = jnp.float32
BF16 = jnp.bfloat16
I32 = jnp.int32

D_MODEL = 1024
HEADS = 8
NOPE = 64
ROPE = 32
Q_LORA = 512
KV_LORA = 256
DIL_DIM = 64
DIL_W = HEADS * DIL_DIM
D_FF = 2816
UP_W = 2 * D_FF
IN_COLS = Q_LORA + KV_LORA + ROPE + 3 * DIL_W
ROPE_THETA = 10000.0
EPS = 1e-6
NEG_INF = -1e30
N_DEV = 8
N_CHIP = 4

ADAM_LR = 0.001
ADAM_B1 = 0.9
ADAM_B2 = 0.999
ADAM_EPS = 1e-08
ADAM_WD = 0.01
ADAM_STEP = 10

LANE = 128
ROW_TILE = 256
ATT_TILE = 256
VMEM_CAP = 56 * 1024 * 1024
VMEM_FLOOR = 32 * 1024 * 1024

P_QLAT, P_QD, P_KD, P_VD, P_KVLAT, P_KPE = 0, 512, 1024, 1536, 2048, 2304
P_COLS = 2432
KPE_OFF = 64

NN = (((1,), (0,)), ((), ()))
NT = (((1,), (1,)), ((), ()))
TN = (((0,), (0,)), ((), ()))
HIGHEST = lax.Precision.HIGHEST
MESH = pl.DeviceIdType.MESH


def _params(sem=None, est_bytes=0):
    limit = int(min(max(2 * est_bytes + (4 << 20), VMEM_FLOOR), VMEM_CAP))
    if sem is None:
        return pltpu.CompilerParams(vmem_limit_bytes=limit)
    return pltpu.CompilerParams(dimension_semantics=sem, vmem_limit_bytes=limit)


def _nbytes(shape, dtype):
    return int(np.prod(shape)) * jnp.dtype(dtype).itemsize


def _hbm(*xs):
    return [pltpu.with_memory_space_constraint(x, pltpu.HBM) for x in xs]


def _hbm_out(shape, dtype):
    return pltpu.HBM(tuple(shape), dtype)


def _mm(a, b, dims, out_dtype, tm, tn, name):
    if dims == "nn":
        (m, k), (k2, n) = a.shape, b.shape
        a_spec = pl.BlockSpec((tm, k), lambda i, j: (i, 0))
        b_spec = pl.BlockSpec((k, tn), lambda i, j: (0, j))
        dn = NN
    elif dims == "nt":
        (m, k), (n, k2) = a.shape, b.shape
        a_spec = pl.BlockSpec((tm, k), lambda i, j: (i, 0))
        b_spec = pl.BlockSpec((tn, k), lambda i, j: (j, 0))
        dn = NT
    else:
        (k, m), (k2, n) = a.shape, b.shape
        a_spec = pl.BlockSpec((k, tm), lambda i, j: (0, i))
        b_spec = pl.BlockSpec((k, tn), lambda i, j: (0, j))
        dn = TN
    assert k == k2 and m % tm == 0 and n % tn == 0, (name, a.shape, b.shape, tm, tn)

    def body(a_ref, b_ref, o_ref):
        o_ref[...] = lax.dot_general(a_ref[...], b_ref[...], dn, preferred_element_type=F32).astype(o_ref.dtype)

    est = _nbytes((tm, k), a.dtype) + _nbytes((tn, k), b.dtype) + _nbytes((tm, tn), F32) + _nbytes((tm, tn), out_dtype)
    return pl.pallas_call(
        body, name=name,
        grid=(m // tm, n // tn),
        in_specs=[a_spec, b_spec],
        out_specs=pl.BlockSpec((tm, tn), lambda i, j: (i, j)),
        out_shape=_hbm_out((m, n), out_dtype),
        compiler_params=_params(("parallel", "parallel"), est),
    )(*_hbm(a, b))


def _seg_consts():
    seg_q = np.zeros((HEADS * LANE, LANE), np.float32)
    inv_q = np.zeros((1, LANE), np.float32)
    seg_k = np.zeros((HEADS * LANE, LANE), np.float32)
    inv_k = np.zeros((1, LANE), np.float32)
    seg_d = np.zeros((DIL_W, LANE), np.float32)
    inv_d = np.zeros((1, LANE), np.float32)
    for h in range(HEADS):
        seg_q[h * LANE:h * LANE + NOPE, 2 * h] = 1.0
        seg_q[h * LANE + NOPE:h * LANE + NOPE + ROPE, 2 * h + 1] = 1.0
        inv_q[0, 2 * h], inv_q[0, 2 * h + 1] = 1.0 / NOPE, 1.0 / ROPE
        seg_k[h * LANE:h * LANE + NOPE, h] = 1.0
        inv_k[0, h] = 1.0 / NOPE
        seg_d[h * DIL_DIM:(h + 1) * DIL_DIM, h] = 1.0
        inv_d[0, h] = 1.0 / DIL_DIM
    fold_q = np.tile(np.eye(LANE, dtype=np.float32), (HEADS, 1))
    fold_d = np.zeros((DIL_W, LANE), np.float32)
    fold_d[np.arange(DIL_W), np.arange(DIL_W) % DIL_DIM] = 1.0
    j = lambda v: jnp.asarray(v)
    b = lambda v: jnp.asarray(v, dtype=BF16)
    return dict(seg_q=b(seg_q), exp_q=b(seg_q.T.copy()), inv_q=j(inv_q), seg_k=b(seg_k), exp_k=b(seg_k.T.copy()),
                inv_k=j(inv_k), seg_d=b(seg_d), exp_d=b(seg_d.T.copy()), inv_d=j(inv_d), fold_q=j(fold_q), fold_d=j(fold_d))


def _rope_consts():
    inv_d = jnp.power(ROPE_THETA, -2.0 * jnp.arange(DIL_DIM // 2, dtype=F32) / DIL_DIM)
    inv_q = jnp.power(ROPE_THETA, -2.0 * jnp.arange(ROPE // 2, dtype=F32) / ROPE)
    lanes = np.arange(LANE)
    freq_d = inv_d[lanes % (DIL_DIM // 2)]
    in_pe = (lanes >= KPE_OFF) & (lanes < KPE_OFF + ROPE)
    freq_q = jnp.where(jnp.asarray(in_pe), inv_q[(lanes - KPE_OFF) % (ROPE // 2)], 0.0)
    sign_d = np.where(lanes % DIL_DIM < DIL_DIM // 2, -1.0, 1.0).astype(np.float32)
    sign_q = np.where(in_pe, np.where((lanes - KPE_OFF) < ROPE // 2, -1.0, 1.0), 0.0).astype(np.float32)
    zeros, ones = np.zeros(LANE, np.float32), np.ones(LANE, np.float32)
    freq = jnp.concatenate([freq_d, freq_d, freq_q, freq_q])[None, :]
    csel = jnp.asarray(np.concatenate([ones, zeros, ones, zeros]))[None, :]
    ssel = jnp.asarray(np.concatenate([zeros, sign_d, zeros, sign_q]))[None, :]
    return freq, csel, ssel


def _full(shape):
    return pl.BlockSpec(shape, lambda *_: (0,) * len(shape))


def _tile_lanes(x, n):
    return jnp.concatenate([x] * n, axis=1)


def _rope_tables(pos_col, freq, csel, ssel):
    s = pos_col.shape[0]

    def body(p_ref, f_ref, c_ref, s_ref, o_ref):
        ang = p_ref[...].astype(F32) * f_ref[...]
        o_ref[...] = c_ref[...] * jnp.cos(ang) + s_ref[...] * jnp.sin(ang)

    return pl.pallas_call(
        body, name="rope_tables", grid=(s // ROW_TILE,),
        in_specs=[pl.BlockSpec((ROW_TILE, 1), lambda i: (i, 0)), _full((1, 4 * LANE)), _full((1, 4 * LANE)), _full((1, 4 * LANE))],
        out_specs=pl.BlockSpec((ROW_TILE, 4 * LANE), lambda i: (i, 0)),
        out_shape=_hbm_out((s, 4 * LANE), F32),
        compiler_params=_params(("parallel",)),
    )(pos_col, freq, csel, ssel)


def _rms(x):
    return lax.rsqrt(jnp.mean(x * x, axis=-1, keepdims=True) + EPS)


def _prenorm(x, gain, scale, shift, name):
    s, d = x.shape

    def body(x_ref, g_ref, sc_ref, sh_ref, h_ref):
        xv = x_ref[...]
        h = (xv * _rms(xv)) * g_ref[...] * (1.0 + sc_ref[...]) + sh_ref[...]
        h_ref[...] = h.astype(BF16)

    row = pl.BlockSpec((ROW_TILE, d), lambda i: (i, 0))
    return pl.pallas_call(
        body, name=name, grid=(s // ROW_TILE,),
        in_specs=[row, _full((1, d)), _full((1, d)), _full((1, d))],
        out_specs=row, out_shape=_hbm_out((s, d), BF16),
        compiler_params=_params(("parallel",)),
    )(*_hbm(x), gain, scale, shift)


def _latnorm(proj, g_q, g_kv):
    s = proj.shape[0]

    def body(q_ref, kv_ref, gq_ref, gkv_ref, ql_ref, kvl_ref):
        q, kv = q_ref[...], kv_ref[...]
        ql_ref[...] = ((q * _rms(q)) * gq_ref[...]).astype(BF16)
        kvl_ref[...] = ((kv * _rms(kv)) * gkv_ref[...]).astype(BF16)

    return pl.pallas_call(
        body, name="latnorm", grid=(s // ROW_TILE,),
        in_specs=[pl.BlockSpec((ROW_TILE, Q_LORA), lambda i: (i, P_QLAT // Q_LORA)),
                  pl.BlockSpec((ROW_TILE, KV_LORA), lambda i: (i, P_KVLAT // KV_LORA)),
                  _full((1, Q_LORA)), _full((1, KV_LORA))],
        out_specs=[pl.BlockSpec((ROW_TILE, Q_LORA), lambda i: (i, 0)), pl.BlockSpec((ROW_TILE, KV_LORA), lambda i: (i, 0))],
        out_shape=[_hbm_out((s, Q_LORA), BF16), _hbm_out((s, KV_LORA), BF16)],
        compiler_params=_params(("parallel",)),
    )(*_hbm(proj, proj), g_q, g_kv)


def _dot01(v, mat01):
    hi = v.astype(BF16)
    lo = (v - hi.astype(F32)).astype(BF16)
    return jnp.dot(hi, mat01, preferred_element_type=F32) + jnp.dot(lo, mat01, preferred_element_type=F32)


def _seg_rinv(x, seg, exp, inv):
    r = lax.rsqrt(_dot01(x * x, seg) * inv + EPS)
    return _dot01(r, exp)


def _seg_mean(v, seg, exp, inv):
    return _dot01(_dot01(v, seg) * inv, exp)


def _swap_halves(x, half):
    n = x.shape[1]
    lane = lax.broadcasted_iota(I32, (1, n), 1)
    first = (lane & (2 * half - 1)) < half
    return jnp.where(first, pltpu.roll(x, n - half, 1), pltpu.roll(x, half, 1))


def _rope(x, cos, sin_signed, half):
    return x * cos + _swap_halves(x, half) * sin_signed


def _rope_bwd(dy, cos, sin_signed, half):
    return dy * cos + _swap_halves(dy * sin_signed, half)


def _pe_lane_mask(n):
    lane = lax.broadcasted_iota(I32, (1, n), 1) & (LANE - 1)
    return (lane >= KPE_OFF) & (lane < KPE_OFF + ROPE)


def _attn_prep(q_raw, kv_raw, proj, tab, gains, consts):
    s = q_raw.shape[0]
    hw = HEADS * LANE

    def body(q_ref, kv_ref, kpe_ref, qd_ref, kd_ref, vd_ref, tab_ref,
             gq_ref, gk_ref, gkpe_ref, gdq_ref, gdk_ref,
             segq_ref, expq_ref, invq_ref, segk_ref, expk_ref, invk_ref, segd_ref, expd_ref, invd_ref,
             qm_ref, km_ref, vm_ref, qdo_ref, kdo_ref, vdo_ref):
        tab_v = tab_ref[...]
        cos_d, sin_d = _tile_lanes(tab_v[:, 0:LANE], DIL_W // LANE), _tile_lanes(tab_v[:, LANE:2 * LANE], DIL_W // LANE)
        cos_q1, sin_q1 = tab_v[:, 2 * LANE:3 * LANE], tab_v[:, 3 * LANE:4 * LANE]
        cos_q, sin_q = _tile_lanes(cos_q1, HEADS), _tile_lanes(sin_q1, HEADS)

        q = q_ref[...]
        qn = q * _seg_rinv(q, segq_ref[...], expq_ref[...], invq_ref[...]) * gq_ref[...]
        qm_ref[...] = _rope(qn, cos_q, sin_q, ROPE // 2).astype(BF16)

        kv = kv_ref[...]
        kp = kv[:, :hw]
        kn = kp * _seg_rinv(kp, segk_ref[...], expk_ref[...], invk_ref[...]) * gk_ref[...]
        kpe = kpe_ref[...]
        r_pe = lax.rsqrt(jnp.sum(kpe * kpe, axis=-1, keepdims=True) * (1.0 / ROPE) + EPS)
        kpe_r = _rope(kpe * r_pe * gkpe_ref[...], cos_q1, sin_q1, ROPE // 2)
        km_ref[...] = (kn + _tile_lanes(kpe_r, HEADS)).astype(BF16)
        vm_ref[...] = kv[:, hw:].astype(BF16)

        qd = qd_ref[...]
        qdn = qd * _seg_rinv(qd, segd_ref[...], expd_ref[...], invd_ref[...]) * gdq_ref[...]
        qdo_ref[...] = _rope(qdn, cos_d, sin_d, DIL_DIM // 2).astype(BF16)
        kd = kd_ref[...]
        kdn = kd * _seg_rinv(kd, segd_ref[...], expd_ref[...], invd_ref[...]) * gdk_ref[...]
        kdo_ref[...] = _rope(kdn, cos_d, sin_d, DIL_DIM // 2).astype(BF16)
        vdo_ref[...] = vd_ref[...].astype(BF16)

    t = ROW_TILE
    row = lambda w, cb=0: pl.BlockSpec((t, w), lambda i: (i, cb))
    c = consts
    return pl.pallas_call(
        body, name="attn_prep", grid=(s // t,),
        in_specs=[row(hw), row(hw + DIL_W), row(LANE, P_KPE // LANE), row(DIL_W, P_QD // DIL_W), row(DIL_W, P_KD // DIL_W),
                  row(DIL_W, P_VD // DIL_W), row(4 * LANE),
                  _full((1, hw)), _full((1, hw)), _full((1, LANE)), _full((1, DIL_W)), _full((1, DIL_W)),
                  _full((hw, LANE)), _full((LANE, hw)), _full((1, LANE)), _full((hw, LANE)), _full((LANE, hw)), _full((1, LANE)),
                  _full((DIL_W, LANE)), _full((LANE, DIL_W)), _full((1, LANE))],
        out_specs=[row(hw), row(hw), row(DIL_W), row(DIL_W), row(DIL_W), row(DIL_W)],
        out_shape=[_hbm_out((s, hw), BF16), _hbm_out((s, hw), BF16)]
        + [_hbm_out((s, DIL_W), BF16)] * 4,
        compiler_params=_params(("parallel",), 24 << 20),
    )(*_hbm(q_raw, kv_raw, proj, proj, proj, proj, tab), gains["q"], gains["k"], gains["kpe"], gains["dq"], gains["dk"],
      c["seg_q"], c["exp_q"], c["inv_q"], c["seg_k"], c["exp_k"], c["inv_k"], c["seg_d"], c["exp_d"], c["inv_d"])


def _attn_prep_bwd(dqm, dkm, dvm, dqd, dkd, dvd, q_raw, kv_raw, proj, tab, gains, consts):
    s = q_raw.shape[0]
    hw = HEADS * LANE
    n_steps = s // ROW_TILE

    def body(dqm_ref, dkm_ref, dvm_ref, dqd_ref, dkd_ref, dvd_ref, q_ref, kv_ref, kpe_ref, qd_ref, kd_ref, tab_ref,
             gq_ref, gk_ref, gkpe_ref, gdq_ref, gdk_ref,
             segq_ref, expq_ref, invq_ref, segk_ref, expk_ref, invk_ref, segd_ref, expd_ref, invd_ref, foldq_ref, foldd_ref,
             dq_ref, dkv_ref, dkpe_ref, dqdo_ref, dkdo_ref, dvdo_ref, dg_ref, acc_ref):
        i = pl.program_id(0)

        @pl.when(i == 0)
        def _():
            acc_ref[...] = jnp.zeros_like(acc_ref)

        tab_v = tab_ref[...]
        cos_d, sin_d = _tile_lanes(tab_v[:, 0:LANE], DIL_W // LANE), _tile_lanes(tab_v[:, LANE:2 * LANE], DIL_W // LANE)
        cos_q1, sin_q1 = tab_v[:, 2 * LANE:3 * LANE], tab_v[:, 3 * LANE:4 * LANE]
        cos_q, sin_q = _tile_lanes(cos_q1, HEADS), _tile_lanes(sin_q1, HEADS)

        def norm_bwd(x, dyg, gain, seg, exp, inv):
            rinv = _seg_rinv(x, seg, exp, inv)
            xn = x * rinv
            dxn = dyg * gain
            dx = rinv * (dxn - xn * _seg_mean(dxn * xn, seg, exp, inv))
            return dx, jnp.sum(dyg * xn, axis=0, keepdims=True)

        dq, gq_l = norm_bwd(q_ref[...], _rope_bwd(dqm_ref[...], cos_q, sin_q, ROPE // 2), gq_ref[...],
                            segq_ref[...], expq_ref[...], invq_ref[...])
        dq_ref[...] = dq.astype(BF16)

        dkm = dkm_ref[...]
        kv = kv_ref[...]
        dkp, gk_l = norm_bwd(kv[:, :hw], dkm, gk_ref[...], segk_ref[...], expk_ref[...], invk_ref[...])
        dkv_ref[:, :hw] = dkp.astype(BF16)
        dkv_ref[:, hw:] = dvm_ref[...].astype(BF16)

        dkpe_r = dkm[:, 0:LANE]
        for h in range(1, HEADS):
            dkpe_r = dkpe_r + dkm[:, h * LANE:(h + 1) * LANE]
        dkpe_r = jnp.where(_pe_lane_mask(LANE), dkpe_r, 0.0)
        dyg = _rope_bwd(dkpe_r, cos_q1, sin_q1, ROPE // 2)
        kpe = kpe_ref[...]
        r_pe = lax.rsqrt(jnp.sum(kpe * kpe, axis=-1, keepdims=True) * (1.0 / ROPE) + EPS)
        xn = kpe * r_pe
        dxn = dyg * gkpe_ref[...]
        dkpe = r_pe * (dxn - xn * (jnp.sum(dxn * xn, axis=-1, keepdims=True) * (1.0 / ROPE)))
        dkpe_ref[...] = dkpe.astype(BF16)
        gkpe_l = jnp.sum(dyg * xn, axis=0, keepdims=True)

        dqd_v, gdq_l = norm_bwd(qd_ref[...], _rope_bwd(dqd_ref[...], cos_d, sin_d, DIL_DIM // 2), gdq_ref[...],
                                segd_ref[...], expd_ref[...], invd_ref[...])
        dqdo_ref[...] = dqd_v.astype(BF16)
        dkd_v, gdk_l = norm_bwd(kd_ref[...], _rope_bwd(dkd_ref[...], cos_d, sin_d, DIL_DIM // 2), gdk_ref[...],
                                segd_ref[...], expd_ref[...], invd_ref[...])
        dkdo_ref[...] = dkd_v.astype(BF16)
        dvdo_ref[...] = dvd_ref[...].astype(BF16)

        acc_ref[0:1, :] += gq_l
        acc_ref[1:2, :] += gk_l
        acc_ref[2:3, 0:LANE] += gkpe_l
        acc_ref[3:4, 0:DIL_W] += gdq_l
        acc_ref[4:5, 0:DIL_W] += gdk_l

        @pl.when(i == n_steps - 1)
        def _():
            acc = acc_ref[...]
            fq = jnp.dot(acc, foldq_ref[...], precision=HIGHEST, preferred_element_type=F32)
            fd = jnp.dot(acc[:, 0:DIL_W], foldd_ref[...], precision=HIGHEST, preferred_element_type=F32)
            rows = lax.broadcasted_iota(I32, (8, LANE), 0)
            dg_ref[...] = jnp.where(rows < 2, fq, jnp.where(rows == 2, acc[:, 0:LANE], fd))

    t = ROW_TILE
    row = lambda w, cb=0: pl.BlockSpec((t, w), lambda i: (i, cb))
    c = consts
    return pl.pallas_call(
        body, name="attn_prep_bwd", grid=(n_steps,),
        in_specs=[row(hw), row(hw), row(DIL_W), row(DIL_W), row(DIL_W), row(DIL_W),
                  row(hw), row(hw + DIL_W), row(LANE, P_KPE // LANE), row(DIL_W, P_QD // DIL_W), row(DIL_W, P_KD // DIL_W),
                  row(4 * LANE),
                  _full((1, hw)), _full((1, hw)), _full((1, LANE)), _full((1, DIL_W)), _full((1, DIL_W)),
                  _full((hw, LANE)), _full((LANE, hw)), _full((1, LANE)), _full((hw, LANE)), _full((LANE, hw)), _full((1, LANE)),
                  _full((DIL_W, LANE)), _full((LANE, DIL_W)), _full((1, LANE)), _full((hw, LANE)), _full((DIL_W, LANE))],
        out_specs=[row(hw), row(hw + DIL_W), row(LANE), row(DIL_W), row(DIL_W), row(DIL_W), _full((8, LANE))],
        out_shape=[_hbm_out((s, hw), BF16), _hbm_out((s, hw + DIL_W), BF16),
                   _hbm_out((s, LANE), BF16)] + [_hbm_out((s, DIL_W), BF16)] * 3
        + [_hbm_out((8, LANE), F32)],
        scratch_shapes=[pltpu.VMEM((8, hw), F32)],
        compiler_params=_params(("arbitrary",), 28 << 20),
    )(*_hbm(dqm, dkm, dvm, dqd, dkd, dvd, q_raw, kv_raw, proj, proj, proj, tab),
      gains["q"], gains["k"], gains["kpe"], gains["dq"], gains["dk"],
      c["seg_q"], c["exp_q"], c["inv_q"], c["seg_k"], c["exp_k"], c["inv_k"], c["seg_d"], c["exp_d"], c["inv_d"],
      c["fold_q"], c["fold_d"])


def _latnorm_bwd(dql, dkvl, proj, g_q, g_kv):
    s = proj.shape[0]
    n_steps = s // ROW_TILE

    def body(dql_ref, dkvl_ref, q_ref, kv_ref, gq_ref, gkv_ref, dq_ref, dkv_ref, dg_ref):
        i = pl.program_id(0)

        @pl.when(i == 0)
        def _():
            dg_ref[...] = jnp.zeros_like(dg_ref)

        def one(x, dyg, gain):
            r = _rms(x)
            xn = x * r
            dxn = dyg * gain
            dx = r * (dxn - xn * jnp.mean(dxn * xn, axis=-1, keepdims=True))
            return dx, jnp.sum(dyg * xn, axis=0, keepdims=True)

        dq, gq_l = one(q_ref[...], dql_ref[...], gq_ref[...])
        dkv, gkv_l = one(kv_ref[...], dkvl_ref[...], gkv_ref[...])
        dq_ref[...] = dq.astype(BF16)
        dkv_ref[...] = dkv.astype(BF16)
        dg_ref[0:1, :] += gq_l
        dg_ref[1:2, 0:KV_LORA] += gkv_l

    t = ROW_TILE
    return pl.pallas_call(
        body, name="latnorm_bwd", grid=(n_steps,),
        in_specs=[pl.BlockSpec((t, Q_LORA), lambda i: (i, 0)), pl.BlockSpec((t, KV_LORA), lambda i: (i, 0)),
                  pl.BlockSpec((t, Q_LORA), lambda i: (i, P_QLAT // Q_LORA)),
                  pl.BlockSpec((t, KV_LORA), lambda i: (i, P_KVLAT // KV_LORA)),
                  _full((1, Q_LORA)), _full((1, KV_LORA))],
        out_specs=[pl.BlockSpec((t, Q_LORA), lambda i: (i, 0)), pl.BlockSpec((t, KV_LORA), lambda i: (i, 0)), _full((8, Q_LORA))],
        out_shape=[_hbm_out((s, Q_LORA), BF16), _hbm_out((s, KV_LORA), BF16),
                   _hbm_out((8, Q_LORA), F32)],
        compiler_params=_params(("arbitrary",)),
    )(*_hbm(dql, dkvl, proj, proj), g_q, g_kv)


def _resid_prenorm(x, mix, g1, gain, scale, shift):
    s, d = x.shape

    def body(x_ref, mix_ref, g1_ref, g_ref, sc_ref, sh_ref, x1_ref, h_ref):
        x1 = x_ref[...] + g1_ref[...] * mix_ref[...]
        x1_ref[...] = x1
        h_ref[...] = ((x1 * _rms(x1)) * g_ref[...] * (1.0 + sc_ref[...]) + sh_ref[...]).astype(BF16)

    row = pl.BlockSpec((ROW_TILE, d), lambda i: (i, 0))
    vec = _full((1, d))
    return pl.pallas_call(
        body, name="resid_prenorm", grid=(s // ROW_TILE,),
        in_specs=[row, row, vec, vec, vec, vec], out_specs=[row, row],
        out_shape=[_hbm_out((s, d), F32), _hbm_out((s, d), BF16)],
        compiler_params=_params(("parallel",)),
    )(*_hbm(x, mix), g1, gain, scale, shift)


CONV_TILE = 1408
HALO = 8


def _shift_down(x, halo, k):
    t = x.shape[0]
    row = lax.broadcasted_iota(I32, (t, 1), 0)
    out = pltpu.roll(x, k, 0)
    for r in range(k):
        out = jnp.where(row == r, halo[HALO - k + r:HALO - k + r + 1, :], out)
    return out


def _shift_up(x, halo, k):
    t = x.shape[0]
    row = lax.broadcasted_iota(I32, (t, 1), 0)
    out = pltpu.roll(x, t - k, 0)
    for r in range(k):
        out = jnp.where(row == t - k + r, halo[r:r + 1, :], out)
    return out


def _conv_fwd(x, halo, w, b):
    p1, p2 = _shift_down(x, halo, 1), _shift_down(x, halo, 2)
    u = b + p2 * w[0:1, :]
    u = u + p1 * w[1:2, :]
    u = u + x * w[2:3, :]
    return u, p1, p2


def _sigmoid(x):
    return 1.0 / (1.0 + jnp.exp(-x))


def _conv_gate(up, w_conv, b_conv):
    s = up.shape[0]
    t = ROW_TILE
    nj = D_FF // CONV_TILE
    hb = t // HALO

    def body(g_ref, v_ref, gh_ref, vh_ref, wg_ref, wv_ref, bg_ref, bv_ref, a_ref):
        live = (pl.program_id(0) > 0).astype(F32)
        ug, _, _ = _conv_fwd(g_ref[...], gh_ref[...] * live, wg_ref[...], bg_ref[...])
        uv, _, _ = _conv_fwd(v_ref[...], vh_ref[...] * live, wv_ref[...], bv_ref[...])
        a_ref[...] = (ug * _sigmoid(ug) * uv).astype(BF16)

    main = lambda off: pl.BlockSpec((t, CONV_TILE), lambda i, j: (i, j + off))
    halo = lambda off: pl.BlockSpec((HALO, CONV_TILE), lambda i, j: (jnp.maximum(i * hb - 1, 0), j + off))
    wsp = lambda off: pl.BlockSpec((3, CONV_TILE), lambda i, j: (0, j + off))
    bsp = lambda off: pl.BlockSpec((1, CONV_TILE), lambda i, j: (0, j + off))
    return pl.pallas_call(
        body, name="conv_gate", grid=(s // t, nj),
        in_specs=[main(0), main(nj), halo(0), halo(nj), wsp(0), wsp(nj), bsp(0), bsp(nj)],
        out_specs=pl.BlockSpec((t, CONV_TILE), lambda i, j: (i, j)),
        out_shape=_hbm_out((s, D_FF), BF16),
        compiler_params=_params(("parallel", "parallel"), 12 << 20),
    )(*_hbm(up, up, up, up), w_conv, w_conv, b_conv, b_conv)


def _gate_bwd(up, da, w_conv, b_conv):
    s = up.shape[0]
    t = ROW_TILE
    nj = D_FF // CONV_TILE
    hb = t // HALO

    def body(g_ref, v_ref, gh_ref, vh_ref, da_ref, wg_ref, wv_ref, bg_ref, bv_ref,
             dug_ref, duv_ref, dbg_ref, dbv_ref, dwg_ref, dwv_ref):
        i = pl.program_id(1)

        @pl.when(i == 0)
        def _():
            for r in (dbg_ref, dbv_ref, dwg_ref, dwv_ref):
                r[...] = jnp.zeros_like(r)

        live = (i > 0).astype(F32)
        xg, xv = g_ref[...], v_ref[...]
        ug, g1, g2 = _conv_fwd(xg, gh_ref[...] * live, wg_ref[...], bg_ref[...])
        uv, v1, v2 = _conv_fwd(xv, vh_ref[...] * live, wv_ref[...], bv_ref[...])
        sg = _sigmoid(ug)
        da_v = da_ref[...]
        dug = da_v * uv * (sg * (1.0 + ug * (1.0 - sg)))
        duv = da_v * (ug * sg)
        dug_ref[...] = dug
        duv_ref[...] = duv
        csum = lambda z: jnp.sum(z, axis=0, keepdims=True)
        dbg_ref[...] += csum(dug)
        dbv_ref[...] += csum(duv)
        dwg_ref[0:1, :] += csum(dug * g2)
        dwg_ref[1:2, :] += csum(dug * g1)
        dwg_ref[2:3, :] += csum(dug * xg)
        dwv_ref[0:1, :] += csum(duv * v2)
        dwv_ref[1:2, :] += csum(duv * v1)
        dwv_ref[2:3, :] += csum(duv * xv)

    main = lambda off: pl.BlockSpec((t, CONV_TILE), lambda j, i: (i, j + off))
    halo = lambda off: pl.BlockSpec((HALO, CONV_TILE), lambda j, i: (jnp.maximum(i * hb - 1, 0), j + off))
    wsp = lambda off: pl.BlockSpec((3, CONV_TILE), lambda j, i: (0, j + off))
    bsp = lambda off: pl.BlockSpec((1, CONV_TILE), lambda j, i: (0, j + off))
    outs = pl.pallas_call(
        body, name="gate_bwd", grid=(nj, s // t),
        in_specs=[main(0), main(nj), halo(0), halo(nj), pl.BlockSpec((t, CONV_TILE), lambda j, i: (i, j)),
                  wsp(0), wsp(nj), bsp(0), bsp(nj)],
        out_specs=[pl.BlockSpec((t, CONV_TILE), lambda j, i: (i, j)), pl.BlockSpec((t, CONV_TILE), lambda j, i: (i, j)),
                   pl.BlockSpec((1, CONV_TILE), lambda j, i: (0, j)), pl.BlockSpec((1, CONV_TILE), lambda j, i: (0, j)),
                   pl.BlockSpec((3, CONV_TILE), lambda j, i: (0, j)), pl.BlockSpec((3, CONV_TILE), lambda j, i: (0, j))],
        out_shape=[_hbm_out((s, D_FF), F32), _hbm_out((s, D_FF), F32),
                   _hbm_out((1, D_FF), F32), _hbm_out((1, D_FF), F32),
                   _hbm_out((3, D_FF), F32), _hbm_out((3, D_FF), F32)],
        compiler_params=_params(("parallel", "arbitrary"), 20 << 20),
    )(*_hbm(up, up, up, up, da), w_conv, w_conv, b_conv, b_conv)
    return outs


def _conv_bwd(du, w_half, name):
    s = du.shape[0]
    t = ROW_TILE
    nj = D_FF // CONV_TILE
    hb = t // HALO
    n_i = s // t

    def body(d_ref, h_ref, w_ref, o_ref):
        live = (pl.program_id(0) < n_i - 1).astype(F32)
        x = d_ref[...]
        halo = h_ref[...] * live
        w = w_ref[...]
        o = x * w[2:3, :] + _shift_up(x, halo, 1) * w[1:2, :] + _shift_up(x, halo, 2) * w[0:1, :]
        o_ref[...] = o.astype(BF16)

    return pl.pallas_call(
        body, name=name, grid=(n_i, nj),
        in_specs=[pl.BlockSpec((t, CONV_TILE), lambda i, j: (i, j)),
                  pl.BlockSpec((HALO, CONV_TILE), lambda i, j: (jnp.minimum((i + 1) * hb, s // HALO - 1), j)),
                  pl.BlockSpec((3, CONV_TILE), lambda i, j: (0, j))],
        out_specs=pl.BlockSpec((t, CONV_TILE), lambda i, j: (i, j)),
        out_shape=_hbm_out((s, D_FF), BF16),
        compiler_params=_params(("parallel", "parallel"), 8 << 20),
    )(*_hbm(du, du), w_half)


def _final(x1, ffn, tgt, g2):
    s, d = x1.shape
    n_steps = s // ROW_TILE

    def body(x1_ref, f_ref, t_ref, g2_ref, dy_ref, df_ref, dg2_ref, loss_ref, lacc_ref):
        i = pl.program_id(0)

        @pl.when(i == 0)
        def _():
            dg2_ref[...] = jnp.zeros_like(dg2_ref)
            lacc_ref[...] = jnp.zeros_like(lacc_ref)

        f = f_ref[...]
        e = x1_ref[...] + g2_ref[...] * f - t_ref[...]
        dy = e * (1.0 / d)
        dy_ref[...] = dy
        df_ref[...] = (dy * g2_ref[...]).astype(BF16)
        dg2_ref[...] += jnp.sum(dy * f, axis=0, keepdims=True)
        lacc_ref[...] += jnp.sum(e * e, axis=0, keepdims=True)

        @pl.when(i == n_steps - 1)
        def _():
            loss_ref[...] = jnp.sum(lacc_ref[...], axis=1, keepdims=True) * (0.5 / d)

    row = pl.BlockSpec((ROW_TILE, d), lambda i: (i, 0))
    return pl.pallas_call(
        body, name="final", grid=(n_steps,),
        in_specs=[row, row, row, _full((1, d))],
        out_specs=[row, row, _full((1, d)), _full((1, 1))],
        out_shape=[_hbm_out((s, d), F32), _hbm_out((s, d), BF16),
                   _hbm_out((1, d), F32), _hbm_out((1, 1), F32)],
        scratch_shapes=[pltpu.VMEM((1, d), F32)],
        compiler_params=_params(("arbitrary",)),
    )(*_hbm(x1, ffn, tgt), g2)


def _ffnnorm_bwd(dh2, x1, dy, mix, gain, scale, g1):
    s, d = x1.shape
    n_steps = s // ROW_TILE

    def body(dh_ref, x_ref, dy_ref, mix_ref, g_ref, sc_ref, g1_ref, dx_ref, dm_ref, acc_ref):
        i = pl.program_id(0)

        @pl.when(i == 0)
        def _():
            acc_ref[...] = jnp.zeros_like(acc_ref)

        dh, x = dh_ref[...], x_ref[...]
        r = _rms(x)
        xn = x * r
        dn = dh * (1.0 + sc_ref[...])
        dxn = dn * g_ref[...]
        dx = dy_ref[...] + r * (dxn - xn * jnp.mean(dxn * xn, axis=-1, keepdims=True))
        dx_ref[...] = dx
        dm_ref[...] = (dx * g1_ref[...]).astype(BF16)
        csum = lambda z: jnp.sum(z, axis=0, keepdims=True)
        acc_ref[0:1, :] += csum(dh)
        acc_ref[1:2, :] += csum(dh * (xn * g_ref[...]))
        acc_ref[2:3, :] += csum(dn * xn)
        acc_ref[3:4, :] += csum(dx * mix_ref[...])

    row = pl.BlockSpec((ROW_TILE, d), lambda i: (i, 0))
    vec = _full((1, d))
    return pl.pallas_call(
        body, name="ffnnorm_bwd", grid=(n_steps,),
        in_specs=[row, row, row, row, vec, vec, vec],
        out_specs=[row, row, _full((8, d))],
        out_shape=[_hbm_out((s, d), F32), _hbm_out((s, d), BF16), _hbm_out((8, d), F32)],
        compiler_params=_params(("arbitrary",)),
    )(*_hbm(dh2, x1, dy, mix), gain, scale, g1)


def _mixnorm_bwd(dh, x, dx1, gain, scale):
    s, d = x.shape
    n_steps = s // ROW_TILE

    def body(dh_ref, x_ref, dx1_ref, g_ref, sc_ref, gx_ref, acc_ref):
        i = pl.program_id(0)

        @pl.when(i == 0)
        def _():
            acc_ref[...] = jnp.zeros_like(acc_ref)

        dh, x = dh_ref[...], x_ref[...]
        r = _rms(x)
        xn = x * r
        dn = dh * (1.0 + sc_ref[...])
        dxn = dn * g_ref[...]
        gx_ref[...] = dx1_ref[...] + r * (dxn - xn * jnp.mean(dxn * xn, axis=-1, keepdims=True))
        csum = lambda z: jnp.sum(z, axis=0, keepdims=True)
        acc_ref[0:1, :] += csum(dh)
        acc_ref[1:2, :] += csum(dh * (xn * g_ref[...]))
        acc_ref[2:3, :] += csum(dn * xn)

    row = pl.BlockSpec((ROW_TILE, d), lambda i: (i, 0))
    vec = _full((1, d))
    return pl.pallas_call(
        body, name="mixnorm_bwd", grid=(n_steps,),
        in_specs=[row, row, row, vec, vec],
        out_specs=[row, _full((8, d))],
        out_shape=[_hbm_out((s, d), F32), _hbm_out((8, d), F32)],
        compiler_params=_params(("arbitrary",)),
    )(*_hbm(dh, x, dx1), gain, scale)


def _key_count(d, dilated):
    if not dilated:
        return jnp.where(d >= 0, 1.0, 0.0)
    one = lambda cond: jnp.where(cond, 1.0, 0.0)
    cnt = one(d <= 128) + one(((d & 3) == 0) & (d <= 512)) + one((d & 15) == 0)
    return jnp.where(d >= 0, cnt, 0.0)


def _block_kinds(mla):
    if mla:
        return 1, "diag", "none"
    near = -(-(512 + ATT_TILE) // ATT_TILE)
    return near, "near", "far"


def _scores_t(ka, qa, scale, kind, rel_t, offset):
    st = lax.dot_general(ka, qa, NT, preferred_element_type=F32) * scale
    cnt = None
    if kind == "diag":
        st = jnp.where(rel_t >= 0, st, NEG_INF)
    elif kind == "far":
        st = jnp.where((rel_t & 15) == 0, st, NEG_INF)
    elif kind == "near":
        cnt = _key_count(rel_t + offset, True)
        st = jnp.where(cnt > 0.0, st, NEG_INF)
    return st, cnt


def _attn_fwd(q, k, v, mla, scale, name, gather=()):
    s = q.shape[0]
    qw = 2 * LANE if mla else LANE
    t = ATT_TILE
    nq = s // t
    n_near, kind_near, kind_far = _block_kinds(mla)
    ng = len(gather)
    last_step = HEADS // 2 - 1

    def body(*refs):
        q_ref, k_ref, v_ref = refs[:3]
        o_ref, lse_ref = refs[3 + ng:5 + ng]
        vt_ref = refs[5 + 2 * ng]
        comm = (refs[3:3 + ng], refs[5 + ng:5 + 2 * ng]) + tuple(refs[6 + 2 * ng:])
        if ng:
            @pl.when(pl.program_id(0) == 0)
            def _():
                _Gather(*comm).start()

            @pl.when(pl.program_id(0) == last_step)
            def _():
                _Gather(*comm).forward()

        lane = lax.broadcasted_iota(I32, (1, LANE), 1)
        rel_t = lax.broadcasted_iota(I32, (t, t), 1) - lax.broadcasted_iota(I32, (t, t), 0)

        def transpose_v(j, carry):
            c0 = pl.multiple_of(j * t, t)
            vt_ref[:, pl.ds(c0, t)] = v_ref[pl.ds(c0, t), :].astype(F32).T.astype(BF16)
            return carry

        lax.fori_loop(0, nq, transpose_v, 0)

        def q_block(qi, carry):
            r0 = pl.multiple_of(qi * t, t)
            kcols = [slice(a * LANE, (a + 1) * LANE) if mla else slice(0, LANE) for a in range(2)]
            qas = [q_ref[pl.ds(r0, t), kcols[a]] for a in range(2)]
            if not mla:
                qas = [jnp.where(lane < DIL_DIM, qas[0], jnp.zeros_like(qas[0])),
                       jnp.where(lane >= DIL_DIM, qas[1], jnp.zeros_like(qas[1]))]

            def k_block(kj, c, kind):
                c0 = pl.multiple_of(kj * t, t)
                out = []
                for a in range(2):
                    m, l, acc = c[a]
                    st, cnt = _scores_t(k_ref[pl.ds(c0, t), kcols[a]], qas[a], scale, kind, rel_t, r0 - c0)
                    m_new = jnp.maximum(m, jnp.max(st, axis=0, keepdims=True))
                    alpha = jnp.exp(m - m_new)
                    p = jnp.exp(st - m_new)
                    if cnt is not None:
                        p = p * cnt
                    l = alpha * l + jnp.sum(p, axis=0, keepdims=True)
                    vt = vt_ref[a * DIL_DIM:(a + 1) * DIL_DIM, pl.ds(c0, t)]
                    acc = alpha * acc + jnp.dot(vt, p.astype(BF16), preferred_element_type=F32)
                    out.append((m_new, l, acc))
                return tuple(out)

            one = (jnp.full((1, t), NEG_INF, F32), jnp.zeros((1, t), F32), jnp.zeros((DIL_DIM, t), F32))
            first_near = jnp.maximum(qi + 1 - n_near, 0)
            c = lax.fori_loop(0, first_near, functools.partial(k_block, kind=kind_far), (one, one))
            res = lax.fori_loop(first_near, qi + 1, functools.partial(k_block, kind=kind_near), c)
            o_t = jnp.concatenate([res[a][2] / res[a][1] for a in range(2)], axis=0)
            o_ref[pl.ds(r0, t), :] = o_t.T.astype(BF16)
            for a in range(2):
                lse_ref[a, :, pl.ds(r0, t)] = res[a][0] + jnp.log(res[a][1])
            return carry

        lax.fori_loop(0, nq, q_block, 0)

        if ng:
            @pl.when(pl.program_id(0) == last_step)
            def _():
                _Gather(*comm).finish()

    return pl.pallas_call(
        body, name=name, grid=(HEADS // 2,),
        in_specs=[pl.BlockSpec((s, qw), lambda h: (0, h)), pl.BlockSpec((s, qw), lambda h: (0, h)),
                  pl.BlockSpec((s, LANE), lambda h: (0, h))] + [ANY] * ng,
        out_specs=[pl.BlockSpec((s, LANE), lambda h: (0, h)), pl.BlockSpec((2, 1, s), lambda h: (h, 0, 0))] + [ANY] * ng,
        out_shape=[_hbm_out((s, DIL_W), BF16), _hbm_out((HEADS, 1, s), F32)] + _Gather.out_shapes(gather),
        scratch_shapes=[pltpu.VMEM((LANE, s), BF16)] + (_Gather.semaphores(ng) if ng else []),
        compiler_params=_params(("arbitrary",) if ng else ("parallel",), 12 << 20),
    )(*_hbm(q, k, v, *gather))


def _attn_bwd(q, k, v, o, do, do_block0, lse, mla, scale, name, scatter=()):
    s = q.shape[0]
    qw = 2 * LANE if mla else LANE
    t = ATT_TILE
    nq = s // t
    n_near, kind_near, kind_far = _block_kinds(mla)
    ns = len(scatter)
    last_step = HEADS // 2 - 1

    def body(*refs):
        q_ref, k_ref, v_ref, o_ref, do_ref, lse_ref = refs[:6]
        dq_ref, dk_ref, dv_ref = refs[6 + ns:9 + ns]
        kt_ref, dot_ref, dob_ref, dqt_ref, delta_ref = refs[9 + 2 * ns:14 + 2 * ns]
        comm = (refs[6:6 + ns], refs[9 + ns:9 + 2 * ns]) + tuple(refs[14 + 2 * ns:])
        if ns:
            @pl.when(pl.program_id(0) == 0)
            def _():
                _Scatter(*comm).start()

        lane = lax.broadcasted_iota(I32, (1, LANE), 1)
        row = lax.broadcasted_iota(I32, (LANE, 1), 0)
        rel_t = lax.broadcasted_iota(I32, (t, t), 1) - lax.broadcasted_iota(I32, (t, t), 0)

        def prepare(j, carry):
            c0 = pl.multiple_of(j * t, t)
            do_blk = do_ref[pl.ds(c0, t), :]
            dob_ref[pl.ds(c0, t), :] = do_blk.astype(BF16)
            do_t = do_blk.T
            dot_ref[:, pl.ds(c0, t)] = do_t.astype(BF16)
            prod = do_t * o_ref[pl.ds(c0, t), :].astype(F32).T
            delta_ref[0, :, pl.ds(c0, t)] = jnp.sum(prod[0:DIL_DIM], axis=0, keepdims=True)
            delta_ref[1, :, pl.ds(c0, t)] = jnp.sum(prod[DIL_DIM:LANE], axis=0, keepdims=True)
            for w in range(qw // LANE):
                kt_ref[w * LANE:(w + 1) * LANE, pl.ds(c0, t)] = (
                    k_ref[pl.ds(c0, t), w * LANE:(w + 1) * LANE].astype(F32).T.astype(BF16))
            return carry

        lax.fori_loop(0, nq, prepare, 0)
        dqt_ref[...] = jnp.zeros_like(dqt_ref)

        sels = [lane < DIL_DIM, lane >= DIL_DIM]
        rsels = [row < DIL_DIM, row >= DIL_DIM]
        cols = [slice(a * LANE, (a + 1) * LANE) if mla else slice(0, LANE) for a in range(2)]

        def k_block(kj, carry):
            c0 = pl.multiple_of(kj * t, t)
            kas = [k_ref[pl.ds(c0, t), cols[a]] for a in range(2)]
            kts = [kt_ref[cols[a], pl.ds(c0, t)] for a in range(2)]
            if not mla:
                kas = [jnp.where(sels[a], kas[a], jnp.zeros_like(kas[a])) for a in range(2)]
                kts = [jnp.where(rsels[a], kts[a], jnp.zeros_like(kts[a])) for a in range(2)]
            vb = v_ref[pl.ds(c0, t), :]
            vbs = [jnp.where(sels[a], vb, jnp.zeros_like(vb)) for a in range(2)]

            def q_block(qi, c, kind):
                r0 = pl.multiple_of(qi * t, t)
                out, dq_parts = [], []
                for a in range(2):
                    dk_acc, dv_acc = c[a]
                    qa = q_ref[pl.ds(r0, t), cols[a]]
                    st, cnt = _scores_t(kas[a], qa, scale, kind, rel_t, r0 - c0)
                    p = jnp.exp(st - lse_ref[a, :, pl.ds(r0, t)])
                    if cnt is not None:
                        p = p * cnt
                    dp = jnp.dot(vbs[a], dot_ref[:, pl.ds(r0, t)], preferred_element_type=F32)
                    ds = (p * (dp - delta_ref[a, :, pl.ds(r0, t)]) * scale).astype(BF16)
                    dv_acc = dv_acc + jnp.dot(p.astype(BF16), dob_ref[pl.ds(r0, t), :], preferred_element_type=F32)
                    dk_acc = dk_acc + jnp.dot(ds, qa, preferred_element_type=F32)
                    dq_parts.append(jnp.dot(kts[a], ds, preferred_element_type=F32))
                    out.append((dk_acc, dv_acc))
                if mla:
                    for a in range(2):
                        dqt_ref[cols[a], pl.ds(r0, t)] += dq_parts[a]
                else:
                    dqt_ref[:, pl.ds(r0, t)] += dq_parts[0] + dq_parts[1]
                return tuple(out)

            zero = jnp.zeros((t, LANE), F32)
            last_near = jnp.minimum(kj + n_near, nq)
            c = lax.fori_loop(kj, last_near, functools.partial(q_block, kind=kind_near), ((zero, zero), (zero, zero)))
            (dk0, dv0), (dk1, dv1) = lax.fori_loop(last_near, nq, functools.partial(q_block, kind=kind_far), c)
            if mla:
                dk_ref[pl.ds(c0, t), cols[0]] = dk0
                dk_ref[pl.ds(c0, t), cols[1]] = dk1
            else:
                dk_ref[pl.ds(c0, t), :] = jnp.where(sels[0], dk0, dk1)
            dv_ref[pl.ds(c0, t), :] = jnp.where(sels[0], dv0, dv1)
            return carry

        lax.fori_loop(0, nq, k_block, 0)

        def write_dq(j, carry):
            c0 = pl.multiple_of(j * t, t)
            for w in range(qw // LANE):
                dq_ref[pl.ds(c0, t), w * LANE:(w + 1) * LANE] = dqt_ref[w * LANE:(w + 1) * LANE, pl.ds(c0, t)].T
            return carry

        lax.fori_loop(0, nq, write_dq, 0)

        if ns:
            @pl.when(pl.program_id(0) == last_step)
            def _():
                _Scatter(*comm).finish()

    b0 = do_block0
    return pl.pallas_call(
        body, name=name, grid=(HEADS // 2,),
        in_specs=[pl.BlockSpec((s, qw), lambda h: (0, h)), pl.BlockSpec((s, qw), lambda h: (0, h)),
                  pl.BlockSpec((s, LANE), lambda h: (0, h)), pl.BlockSpec((s, LANE), lambda h: (0, h)),
                  pl.BlockSpec((s, LANE), lambda h: (0, h + b0)), pl.BlockSpec((2, 1, s), lambda h: (h, 0, 0))] + [ANY] * ns,
        out_specs=[pl.BlockSpec((s, qw), lambda h: (0, h)), pl.BlockSpec((s, qw), lambda h: (0, h)),
                   pl.BlockSpec((s, LANE), lambda h: (0, h))] + [ANY] * ns,
        out_shape=[_hbm_out(q.shape, F32), _hbm_out(k.shape, F32), _hbm_out((s, DIL_W), F32)]
        + _Scatter.out_shapes(scatter),
        scratch_shapes=[pltpu.VMEM((qw, s), BF16), pltpu.VMEM((LANE, s), BF16), pltpu.VMEM((s, LANE), BF16),
                        pltpu.VMEM((qw, s), F32), pltpu.VMEM((2, 1, s), F32)] + (_Scatter.semaphores(ns) if ns else []),
        compiler_params=_params(("arbitrary",) if ns else ("parallel",), 24 << 20),
    )(*_hbm(q, k, v, o, do, lse, *scatter))


def _ada_fwd(c_all, w_shard, b_shard):
    n, d = c_all.shape
    cols = w_shard.shape[1]

    def body(c_ref, w_ref, b_ref, o_ref):
        cv = c_ref[...]
        sc = (cv * _sigmoid(cv)).astype(BF16)
        o_ref[...] = jnp.dot(sc, w_ref[...].astype(BF16), preferred_element_type=F32) + b_ref[...]

    return pl.pallas_call(
        body, name="ada_fwd", out_shape=jax.ShapeDtypeStruct((n, cols), F32),
        compiler_params=_params(None, 16 << 20),
    )(c_all, w_shard, b_shard)


def _ada_bwd(c_all, dmod_shard):
    n, d = c_all.shape
    cols = dmod_shard.shape[1]

    def body(c_ref, g_ref, o_ref):
        cv = c_ref[...]
        o_ref[...] = lax.dot_general(cv * _sigmoid(cv), g_ref[...], TN, precision=HIGHEST, preferred_element_type=F32)

    return pl.pallas_call(
        body, name="ada_bwd", out_shape=jax.ShapeDtypeStruct((d, cols), F32),
        compiler_params=_params(None, 16 << 20),
    )(c_all, dmod_shard)


def _sum_devices(g):
    n, r, w = g.shape

    def body(g_ref, o_ref):
        acc = g_ref[0]
        for k in range(1, n):
            acc = acc + g_ref[k]
        o_ref[...] = acc

    return pl.pallas_call(
        body, name="sum_devices", out_shape=jax.ShapeDtypeStruct((r, w), F32),
        compiler_params=_params(None, 4 << 20),
    )(g)


def _adamw(w, g, m, v, name):
    r, c = w.shape
    tr = r
    for cand in (256, 128, 64, 32, 16, 8):
        if r % cand == 0 and r > cand:
            tr = cand
            break

    def body(w_ref, g_ref, m_ref, v_ref, d_ref, mo_ref, vo_ref):
        gv = g_ref[...]
        mn = ADAM_B1 * m_ref[...] + (1.0 - ADAM_B1) * gv
        vn = ADAM_B2 * v_ref[...] + (1.0 - ADAM_B2) * (gv * gv)
        m_hat = mn / (1.0 - ADAM_B1 ** ADAM_STEP)
        v_hat = vn / (1.0 - ADAM_B2 ** ADAM_STEP)
        d_ref[...] = -ADAM_LR * (m_hat / (jnp.sqrt(v_hat) + ADAM_EPS) + ADAM_WD * w_ref[...])
        mo_ref[...] = mn
        vo_ref[...] = vn

    blk = pl.BlockSpec((tr, c), lambda i: (i, 0))
    return pl.pallas_call(
        body, name=name, grid=(r // tr,), in_specs=[blk] * 4, out_specs=[blk] * 3,
        out_shape=[_hbm_out((r, c), F32)] * 3,
        compiler_params=_params(("parallel",), 7 * _nbytes((tr, c), F32)),
    )(*_hbm(w, g, m, v))


def _position():
    return lax.axis_index("x"), lax.axis_index("y"), lax.axis_index("c")


def _other_chips(x, y):
    return [(1 - x, y, 2 * (1 - x) + y), (x, 1 - y, 2 * x + (1 - y)), (1 - x, 1 - y, 2 * (1 - x) + (1 - y))]


def _ag_small(v, name):
    r, w = v.shape

    def body(v_ref, out_ref, send_sems, recv_sems, local_sem):
        x, y, c = _position()
        me = 4 * x + 2 * y + c
        mine = pltpu.make_async_copy(v_ref, out_ref.at[me], local_sem)
        mine.start()
        peers = []
        for k in range(1, N_DEV):
            fx, fy, fc = (k >> 2) & 1, (k >> 1) & 1, k & 1
            px = 1 - x if fx else x
            py = 1 - y if fy else y
            pc = 1 - c if fc else c
            peers.append((px, py, pc))
        sends = []
        for k, peer in enumerate(peers):
            cp = pltpu.make_async_remote_copy(src_ref=v_ref, dst_ref=out_ref.at[me], send_sem=send_sems.at[k],
                                              recv_sem=recv_sems.at[k], device_id=peer, device_id_type=MESH)
            cp.start()
            sends.append(cp)
        for k, (px, py, pc) in enumerate(peers):
            pltpu.make_async_remote_copy(src_ref=v_ref, dst_ref=out_ref.at[4 * px + 2 * py + pc], send_sem=send_sems.at[k],
                                         recv_sem=recv_sems.at[k], device_id=(px, py, pc), device_id_type=MESH).wait_recv()
        for cp in sends:
            cp.wait_send()
        mine.wait()

    return pl.pallas_call(
        body, name=name,
        out_shape=jax.ShapeDtypeStruct((N_DEV, r, w), F32),
        in_specs=[pl.BlockSpec(memory_space=pltpu.VMEM)],
        out_specs=pl.BlockSpec(memory_space=pltpu.VMEM),
        scratch_shapes=[pltpu.SemaphoreType.DMA((N_DEV - 1,)), pltpu.SemaphoreType.DMA((N_DEV - 1,)), pltpu.SemaphoreType.DMA],
        compiler_params=_params(None, 10 * _nbytes((r, w), F32)),
    )(v)


ANY = pl.BlockSpec(memory_space=pl.ANY)


def _ag_weights(shards, name):
    n = len(shards)

    def body(*refs):
        gather = _Gather(refs[:n], refs[n:2 * n], *refs[2 * n:])
        gather.start()
        gather.forward()
        gather.finish()

    return pl.pallas_call(
        body, name=name,
        out_shape=_Gather.out_shapes(shards), in_specs=[ANY] * n, out_specs=[ANY] * n,
        scratch_shapes=_Gather.semaphores(n),
    )(*shards)


class _Gather:
    def __init__(self, w_refs, out_refs, send_sems, recv_sems):
        x, y, c = _position()
        q0 = 2 * x + y
        sibling = (x, y, 1 - c)
        self.ici, self.ici_in, self.fwd, self.fwd_in = [], [], [], []
        for k, (w_ref, out_ref) in enumerate(zip(w_refs, out_refs)):
            half = w_ref.shape[0] // 2

            def blk(q, e, out_ref=out_ref, half=half):
                return out_ref.at[q, pl.ds(pl.multiple_of(e * half, 16), half), :]

            def copy(src, dst, i, to):
                return pltpu.make_async_remote_copy(src_ref=src, dst_ref=dst, send_sem=send_sems.at[i], recv_sem=recv_sems.at[i],
                                                    device_id=to, device_id_type=MESH)

            src = w_ref.at[pl.ds(pl.multiple_of(c * half, 16), half), :]
            for j, (cx, cy, qj) in enumerate(_other_chips(x, y)):
                self.ici.append(copy(src, blk(q0, c), 6 * k + j, (cx, cy, c)))
                self.ici_in.append(copy(blk(qj, c), blk(qj, c), 6 * k + j, (cx, cy, c)))
                self.fwd.append(copy(blk(qj, c), blk(qj, c), 6 * k + 3 + j, sibling))
                self.fwd_in.append(copy(blk(qj, 1 - c), blk(qj, 1 - c), 6 * k + 3 + j, sibling))

    @staticmethod
    def out_shapes(shards):
        return [_hbm_out((N_CHIP,) + s.shape, s.dtype) for s in shards]

    @staticmethod
    def semaphores(n):
        return [pltpu.SemaphoreType.DMA((6 * n,)), pltpu.SemaphoreType.DMA((6 * n,))]

    def start(self):
        for cp in self.ici:
            cp.start()

    def forward(self):
        for arrived, onward in zip(self.ici_in, self.fwd):
            arrived.wait_recv()
            onward.start()

    def finish(self):
        for cp in self.fwd_in:
            cp.wait_recv()
        for cp in self.ici + self.fwd:
            cp.wait_send()


def _swap_halves_d2d(grads, name):
    n = len(grads)

    def body(*refs):
        g_refs, out_refs = refs[:n], refs[n:2 * n]
        send_sems, recv_sems = refs[2 * n:]
        x, y, c = _position()
        sibling = (x, y, 1 - c)
        cps = []
        for k in range(n):
            cp = pltpu.make_async_remote_copy(src_ref=g_refs[k].at[:, 1 - c], dst_ref=out_refs[k], send_sem=send_sems.at[k],
                                              recv_sem=recv_sems.at[k], device_id=sibling, device_id_type=MESH)
            cp.start()
            cps.append(cp)
        for cp in cps:
            cp.wait_recv()
        for cp in cps:
            cp.wait_send()

    return pl.pallas_call(
        body, name=name,
        out_shape=[_hbm_out((N_CHIP,) + g.shape[2:], g.dtype) for g in grads],
        in_specs=[ANY] * n, out_specs=[ANY] * n,
        scratch_shapes=[pltpu.SemaphoreType.DMA((n,)), pltpu.SemaphoreType.DMA((n,))],
    )(*grads)


def _pair_sum(g, a, c_idx, name):
    _, _, rh, cols = g.shape
    tr = rh
    for cand in (256, 128, 64, 32, 16):
        if rh % cand == 0 and rh > cand:
            tr = cand
            break

    def body(c_ref, g_ref, a_ref, o_ref):
        o_ref[...] = (g_ref[...] + a_ref[...]).astype(BF16)

    return pl.pallas_call(
        body, name=name,
        grid_spec=pltpu.PrefetchScalarGridSpec(
            num_scalar_prefetch=1, grid=(N_CHIP, rh // tr),
            in_specs=[pl.BlockSpec((None, None, tr, cols), lambda q, i, c_ref: (q, c_ref[0], i, 0)),
                      pl.BlockSpec((None, tr, cols), lambda q, i, c_ref: (q, i, 0))],
            out_specs=pl.BlockSpec((None, tr, cols), lambda q, i, c_ref: (q, i, 0))),
        out_shape=_hbm_out((N_CHIP, rh, cols), BF16),
        compiler_params=_params(("parallel", "parallel"), 10 * _nbytes((tr, cols), F32)),
    )(c_idx, *_hbm(g, a))


def _scatter_partials(parts, name):
    n = len(parts)

    def body(*refs):
        scatter = _Scatter(refs[:n], refs[n:2 * n], *refs[2 * n:])
        scatter.start()
        scatter.finish()

    return pl.pallas_call(
        body, name=name,
        out_shape=_Scatter.out_shapes(parts), in_specs=[ANY] * n, out_specs=[ANY] * n,
        scratch_shapes=_Scatter.semaphores(n),
    )(*parts)


class _Scatter:
    def __init__(self, p_refs, out_refs, send_sems, recv_sems):
        x, y, c = _position()
        self.copies = []
        for k, (p_ref, out_ref) in enumerate(zip(p_refs, out_refs)):
            for j, (cx, cy, qj) in enumerate(_other_chips(x, y)):
                self.copies.append(pltpu.make_async_remote_copy(
                    src_ref=p_ref.at[qj], dst_ref=out_ref.at[j], send_sem=send_sems.at[3 * k + j],
                    recv_sem=recv_sems.at[3 * k + j], device_id=(cx, cy, c), device_id_type=MESH))

    @staticmethod
    def out_shapes(parts):
        return [_hbm_out((3,) + p.shape[1:], p.dtype) for p in parts]

    @staticmethod
    def semaphores(n):
        return [pltpu.SemaphoreType.DMA((3 * n,)), pltpu.SemaphoreType.DMA((3 * n,))]

    def start(self):
        for cp in self.copies:
            cp.start()

    def finish(self):
        for cp in self.copies:
            cp.wait_recv()
        for cp in self.copies:
            cp.wait_send()


def _shard_sum(p, b, q_idx, name):
    _, rh, cols = p.shape
    tr = rh
    for cand in (256, 128, 64, 32, 16):
        if rh % cand == 0 and rh > cand:
            tr = cand
            break

    def body(q_ref, p_ref, b_ref, o_ref):
        acc = p_ref[...].astype(F32)
        for j in range(3):
            acc = acc + b_ref[j].astype(F32)
        o_ref[...] = acc

    return pl.pallas_call(
        body, name=name,
        grid_spec=pltpu.PrefetchScalarGridSpec(
            num_scalar_prefetch=1, grid=(rh // tr,),
            in_specs=[pl.BlockSpec((None, tr, cols), lambda i, q_ref: (q_ref[0], i, 0)),
                      pl.BlockSpec((3, tr, cols), lambda i, q_ref: (0, i, 0))],
            out_specs=pl.BlockSpec((tr, cols), lambda i, q_ref: (i, 0))),
        out_shape=_hbm_out((rh, cols), F32),
        compiler_params=_params(("parallel",), 8 * _nbytes((tr, cols), F32)),
    )(q_idx, *_hbm(p, b))


def _join_halves(halves):
    n = len(halves)

    def body(*refs):
        h_refs, out_refs = refs[:n], refs[n:2 * n]
        send_sems, recv_sems = refs[2 * n:]
        x, y, c = _position()
        sibling = (x, y, 1 - c)
        cps = []
        for k in range(n):
            cp = pltpu.make_async_remote_copy(src_ref=h_refs[k], dst_ref=out_refs[k], send_sem=send_sems.at[k],
                                              recv_sem=recv_sems.at[k], device_id=sibling, device_id_type=MESH)
            cp.start()
            cps.append(cp)
        for cp in cps:
            cp.wait_recv()
        for cp in cps:
            cp.wait_send()

    return pl.pallas_call(
        body, name="rs_join",
        out_shape=[_hbm_out(h.shape, h.dtype) for h in halves],
        in_specs=[ANY] * n, out_specs=[ANY] * n,
        scratch_shapes=[pltpu.SemaphoreType.DMA((n,)), pltpu.SemaphoreType.DMA((n,))],
    )(*halves)


def _cols_from_shards(g):
    q, r, cs = g.shape
    return jnp.transpose(g, (1, 0, 2)).reshape(r, q * cs)


def _cols_to_shards(w):
    r, cfull = w.shape
    return jnp.transpose(w.reshape(r, N_CHIP, cfull // N_CHIP), (1, 0, 2))


def _pad_w_in(w):
    z = lambda n: jnp.zeros((w.shape[0], n), w.dtype)
    q_lat, kv_lat, kpe = w[:, 0:512], w[:, 512:768], w[:, 768:800]
    qd, kd, vd = w[:, 800:1312], w[:, 1312:1824], w[:, 1824:2336]
    return jnp.concatenate([q_lat, qd, kd, vd, kv_lat, z(KPE_OFF), kpe, z(LANE - KPE_OFF - ROPE)], axis=1)


def _unpad_w_in(g):
    return jnp.concatenate([g[:, P_QLAT:P_QLAT + Q_LORA], g[:, P_KVLAT:P_KVLAT + KV_LORA],
                            g[:, P_KPE + KPE_OFF:P_KPE + KPE_OFF + ROPE], g[:, P_QD:P_QD + 3 * DIL_W]], axis=1)


def _pad_w_qb(w):
    w3 = w.reshape(Q_LORA, HEADS, NOPE + ROPE)
    return jnp.pad(w3, ((0, 0), (0, 0), (0, LANE - NOPE - ROPE))).reshape(Q_LORA, HEADS * LANE)


def _unpad_w_qb(g):
    return g.reshape(Q_LORA, HEADS, LANE)[:, :, :NOPE + ROPE].reshape(Q_LORA, HEADS * (NOPE + ROPE))


def _pad_w_kvb(w):
    w3 = w.reshape(KV_LORA, HEADS, 2 * NOPE)
    kp = jnp.pad(w3[:, :, :NOPE], ((0, 0), (0, 0), (0, LANE - NOPE))).reshape(KV_LORA, HEADS * LANE)
    return jnp.concatenate([kp, w3[:, :, NOPE:].reshape(KV_LORA, DIL_W)], axis=1)


def _unpad_w_kvb(g):
    gk = g[:, :HEADS * LANE].reshape(KV_LORA, HEADS, LANE)[:, :, :NOPE]
    gv = g[:, HEADS * LANE:].reshape(KV_LORA, HEADS, NOPE)
    return jnp.concatenate([gk, gv], axis=2).reshape(KV_LORA, HEADS * 2 * NOPE)


def _head_gains(g_q_nope, g_q_pe, g_k_nope, g_k_pe, g_dq, g_dk):
    z = lambda n: jnp.zeros((1, n), F32)
    q1 = jnp.concatenate([g_q_nope, g_q_pe, z(LANE - NOPE - ROPE)], axis=1)
    k1 = jnp.concatenate([g_k_nope, z(LANE - NOPE)], axis=1)
    kpe = jnp.concatenate([z(KPE_OFF), g_k_pe, z(LANE - KPE_OFF - ROPE)], axis=1)
    return dict(q=jnp.tile(q1, (1, HEADS)), k=jnp.tile(k1, (1, HEADS)), kpe=kpe,
                dq=jnp.tile(g_dq, (1, HEADS)), dk=jnp.tile(g_dk, (1, HEADS)))


SMALL_NAMES = ("g_mix_norm", "g_q_lat", "g_kv_lat", "g_mla_q_nope", "g_mla_q_pe", "g_mla_k_nope", "g_mla_k_pe",
               "g_dil_q", "g_dil_k", "g_ffn_norm", "b_conv")


def _pack(vs):
    parts, spans, off = [], [], 0
    for v in vs:
        n = v.shape[1]
        npad = -(-n // LANE) * LANE
        parts.append(jnp.pad(v, ((0, 0), (0, npad - n))))
        spans.append((off, n))
        off += npad
    return jnp.concatenate(parts, axis=1), spans


def kernel(x, c, positions, w_ada, b_ada, g_mix_norm, w_in, g_q_lat, w_q_b, g_kv_lat, w_kv_b, g_mla_q_nope, g_mla_q_pe, g_mla_k_nope, g_mla_k_pe, g_dil_q, g_dil_k, w_o, g_ffn_norm, w_up, w_conv, b_conv, w_down, loss_target, m_w_ada, m_b_ada, m_g_mix_norm, m_w_in, m_g_q_lat, m_w_q_b, m_g_kv_lat, m_w_kv_b, m_g_mla_q_nope, m_g_mla_q_pe, m_g_mla_k_nope, m_g_mla_k_pe, m_g_dil_q, m_g_dil_k, m_w_o, m_g_ffn_norm, m_w_up, m_w_conv, m_b_conv, m_w_down, v_w_ada, v_b_ada, v_g_mix_norm, v_w_in, v_g_q_lat, v_w_q_b, v_g_kv_lat, v_w_kv_b, v_g_mla_q_nope, v_g_mla_q_pe, v_g_mla_k_nope, v_g_mla_k_pe, v_g_dil_q, v_g_dil_k, v_w_o, v_g_ffn_norm, v_w_up, v_w_conv, v_b_conv, v_w_down):
    args = dict(locals())
    weights = {n: args[n][0] for n in ("w_ada", "w_in", "w_q_b", "w_kv_b", "w_o", "w_up", "w_conv", "w_down")}
    small_w = {n: args[n] for n in SMALL_NAMES + ("b_ada",)}
    mom_m = {n[2:]: (args[n][0] if args[n].ndim == 3 else args[n]) for n in args if n.startswith("m_")}
    mom_v = {n[2:]: (args[n][0] if args[n].ndim == 3 else args[n]) for n in args if n.startswith("v_")}

    xi, yi, ci = _position()
    q0 = 2 * xi + yi
    me = 4 * xi + 2 * yi + ci
    xs, tgt = x[0], loss_target[0]
    s = xs.shape[0]
    consts = _seg_consts()
    c_idx, q_idx = jnp.reshape(ci, (1,)).astype(I32), jnp.reshape(q0, (1,)).astype(I32)

    def halves(g4):
        q, r, cc = g4.shape
        return g4.reshape(q, 2, r // 2, cc)

    c_all = _ag_small(c, "ag_c")[:, 0, :]
    ada_cols = w_ada.shape[2]
    b_shard = lax.dynamic_slice_in_dim(b_ada, q0 * ada_cols, ada_cols, axis=1)
    mod_blk = _ada_fwd(c_all, weights["w_ada"], b_shard)
    mod_all = _ag_small(mod_blk, "ag_mod").reshape(N_CHIP, 2, N_DEV, ada_cols)
    mod = lax.dynamic_index_in_dim(lax.dynamic_index_in_dim(mod_all, ci, 1, False), me, 1, False)
    mod = mod.reshape(1, N_CHIP * ada_cols)
    sh1, sc1, g1, sh2, sc2, g2 = [mod[:, k * D_MODEL:(k + 1) * D_MODEL] for k in range(6)]

    place_own = lambda gs, ws: [lax.dynamic_update_slice_in_dim(g, w[None], q0, axis=0) for g, w in zip(gs, ws)]
    own_first = [weights[n].astype(BF16) for n in ("w_in", "w_q_b", "w_kv_b")]
    own_later = [weights[n].astype(BF16) for n in ("w_o", "w_up", "w_down")]
    gathered = place_own(_ag_weights(own_first, "ag_weights"), own_first)
    w_in_p = _pad_w_in(_cols_from_shards(gathered[0]))
    w_qb_p = _pad_w_qb(_cols_from_shards(gathered[1]))
    w_kvb_p = _pad_w_kvb(_cols_from_shards(gathered[2]))
    w_conv_f = _ag_small(weights["w_conv"], "ag_wconv")
    w_conv_f = jnp.transpose(w_conv_f.reshape(N_CHIP, 2, 3, -1)[:, 0], (1, 0, 2)).reshape(3, UP_W)

    gains = _head_gains(g_mla_q_nope, g_mla_q_pe, g_mla_k_nope, g_mla_k_pe, g_dil_q, g_dil_k)
    tab = _rope_tables(positions.reshape(s, 1), *_rope_consts())

    h = _prenorm(xs, g_mix_norm, sc1, sh1, "prenorm")
    proj = _mm(h, w_in_p, "nn", F32, 512, P_COLS, "mm_in")
    ql, kvl = _latnorm(proj, g_q_lat, g_kv_lat)
    q_raw = _mm(ql, w_qb_p, "nn", F32, 512, HEADS * LANE, "mm_qb")
    kv_raw = _mm(kvl, w_kvb_p, "nn", F32, 512, HEADS * LANE + DIL_W, "mm_kvb")
    qm, km, vm, qd, kd, vd = _attn_prep(q_raw, kv_raw, proj, tab, gains, consts)
    scale_m, scale_d = (NOPE + ROPE) ** -0.5, DIL_DIM ** -0.5
    o_m, lse_m, *gathered = _attn_fwd(qm, km, vm, True, scale_m, "attn_mla", gather=own_later)
    gathered = place_own(gathered, own_later)
    w_o_f = gathered[0].reshape(D_MODEL, D_MODEL)
    w_up_f = _cols_from_shards(gathered[1])
    w_down_f = gathered[2].reshape(D_FF, D_MODEL)
    o_d, lse_d = _attn_fwd(qd, kd, vd, False, scale_d, "attn_dil")
    mix_in = jnp.concatenate([o_m, o_d], axis=1)
    mix = _mm(mix_in, w_o_f, "nn", F32, 512, D_MODEL, "mm_o")
    x1, h2 = _resid_prenorm(xs, mix, g1, g_ffn_norm, sc2, sh2)
    up = _mm(h2, w_up_f, "nn", F32, 512, CONV_TILE, "mm_up")
    act = _conv_gate(up, w_conv_f, b_conv)
    ffn = _mm(act, w_down_f, "nn", F32, 256, D_MODEL, "mm_down")
    dy, dffn, dg2, loss_part = _final(x1, ffn, tgt, g2)

    da = _mm(dffn, w_down_f, "nt", F32, 512, CONV_TILE, "mm_down_dx")
    gw_down = _mm(act, dffn, "tn", F32, 256, D_MODEL, "mm_down_dw")
    dug, duv, dbg, dbv, dwg, dwv = _gate_bwd(up, da, w_conv_f, b_conv)
    dup = jnp.concatenate([_conv_bwd(dug, w_conv_f[:, :D_FF], "conv_bwd_gate"),
                           _conv_bwd(duv, w_conv_f[:, D_FF:], "conv_bwd_val")], axis=1)
    dh2 = _mm(dup, w_up_f, "nt", F32, 256, 512, "mm_up_dx")
    gw_up = _mm(h2, dup, "tn", F32, 512, CONV_TILE, "mm_up_dw")
    dx1, dmix, acc2 = _ffnnorm_bwd(dh2, x1, dy, mix, g_ffn_norm, sc2, g1)
    dmix_in = _mm(dmix, w_o_f, "nt", F32, 512, D_MODEL, "mm_o_dx")
    gw_o = _mm(mix_in, dmix, "tn", F32, 512, D_MODEL, "mm_o_dw")
    early_names = ("w_up", "w_down")
    early = [halves(_cols_to_shards(gw_up)), halves(gw_down.reshape(N_CHIP, D_FF // N_CHIP, D_MODEL))]
    early_sib = _swap_halves_d2d(early, "rs_pair_swap_early")
    early_sums = [_pair_sum(g, a, c_idx, "pair_sum_" + n) for g, a, n in zip(early, early_sib, early_names)]
    dqm, dkm, dvm, *early_recv = _attn_bwd(qm, km, vm, o_m, dmix_in, 0, lse_m, True, scale_m, "attn_mla_bwd",
                                           scatter=early_sums)
    dqd, dkd, dvd = _attn_bwd(qd, kd, vd, o_d, dmix_in, DIL_W // LANE, lse_d, False, scale_d, "attn_dil_bwd")
    dq_raw, dkv_raw, dkpe_b, dqd_b, dkd_b, dvd_b, dgains = _attn_prep_bwd(
        dqm, dkm, dvm, dqd, dkd, dvd, q_raw, kv_raw, proj, tab, gains, consts)
    dql = _mm(dq_raw, w_qb_p, "nt", F32, 512, Q_LORA, "mm_qb_dx")
    gw_qb = _unpad_w_qb(_mm(ql, dq_raw, "tn", F32, Q_LORA, HEADS * LANE, "mm_qb_dw"))
    dkvl = _mm(dkv_raw, w_kvb_p, "nt", F32, 512, KV_LORA, "mm_kvb_dx")
    gw_kvb = _unpad_w_kvb(_mm(kvl, dkv_raw, "tn", F32, KV_LORA, HEADS * LANE + DIL_W, "mm_kvb_dw"))
    dqlat_b, dkvlat_b, dglat = _latnorm_bwd(dql, dkvl, proj, g_q_lat, g_kv_lat)
    dproj = jnp.concatenate([dqlat_b, dqd_b, dkd_b, dvd_b, dkvlat_b, dkpe_b], axis=1)
    dh = _mm(dproj, w_in_p, "nt", F32, 512, D_MODEL, "mm_in_dx")
    gw_in = _unpad_w_in(_mm(h, dproj, "tn", F32, 512, P_COLS, "mm_in_dw"))
    grad_x, acc1 = _mixnorm_bwd(dh, xs, dx1, g_mix_norm, sc1)

    dmod = jnp.concatenate([acc1[0:1], acc1[1:2], acc2[3:4], acc2[0:1], acc2[1:2], dg2], axis=1)
    small_g = {"g_mix_norm": acc1[2:3], "g_q_lat": dglat[0:1], "g_kv_lat": dglat[1:2, :KV_LORA],
               "g_mla_q_nope": dgains[0:1, :NOPE], "g_mla_q_pe": dgains[0:1, NOPE:NOPE + ROPE],
               "g_mla_k_nope": dgains[1:2, :NOPE], "g_mla_k_pe": dgains[2:3, KPE_OFF:KPE_OFF + ROPE],
               "g_dil_q": dgains[3:4, :DIL_DIM], "g_dil_k": dgains[4:5, :DIL_DIM], "g_ffn_norm": acc2[2:3],
               "b_conv": jnp.concatenate([dbg, dbv], axis=1)}
    dw_conv = jnp.concatenate([dwg, dwv], axis=1)
    packed, spans = _pack([dmod] + [small_g[n] for n in SMALL_NAMES] + [dw_conv[k:k + 1] for k in range(3)])
    gathered_small = _ag_small(packed, "ag_small")
    summed = _sum_devices(gathered_small)
    take = lambda k: summed[:, spans[k][0]:spans[k][0] + spans[k][1]]
    grads = {"b_ada": take(0)}
    for k, n in enumerate(SMALL_NAMES):
        grads[n] = take(1 + k)
    shard_cols = UP_W // N_CHIP
    gconv_full = jnp.concatenate([take(1 + len(SMALL_NAMES) + k) for k in range(3)], axis=0)
    grads["w_conv"] = lax.dynamic_slice_in_dim(gconv_full, q0 * shard_cols, shard_cols, axis=1)
    dmod_all = gathered_small[:, 0, :6 * D_MODEL]
    grads["w_ada"] = _ada_bwd(c_all, lax.dynamic_slice_in_dim(dmod_all, q0 * ada_cols, ada_cols, axis=1))

    late_names = ("w_in", "w_q_b", "w_kv_b", "w_o")
    late = [halves(_cols_to_shards(gw_in)), halves(_cols_to_shards(gw_qb)), halves(_cols_to_shards(gw_kvb)),
            halves(gw_o.reshape(N_CHIP, D_MODEL // N_CHIP, D_MODEL))]
    late_sib = _swap_halves_d2d(late, "rs_pair_swap_late")
    late_sums = [_pair_sum(g, a, c_idx, "pair_sum_" + n) for g, a, n in zip(late, late_sib, late_names)]
    late_recv = _scatter_partials(late_sums, "rs_scatter_late")
    big_names = late_names + early_names
    half_sums = [_shard_sum(p, b, q_idx, "shard_sum_" + n)
                 for p, b, n in zip(late_sums + early_sums, list(late_recv) + list(early_recv), big_names)]
    from_sib = _join_halves(half_sums)
    south = ci == 0
    for n, mine, theirs in zip(big_names, half_sums, from_sib):
        grads[n] = jnp.concatenate([jnp.where(south, mine, theirs), jnp.where(south, theirs, mine)], axis=0)

    delta, new_m, new_v = {}, {}, {}
    for n in ("w_ada", "w_in", "w_q_b", "w_kv_b", "w_o", "w_up", "w_conv", "w_down"):
        delta[n], new_m[n], new_v[n] = _adamw(weights[n], grads[n], mom_m[n], mom_v[n], "adamw_" + n)
    vec_names = ("b_ada",) + SMALL_NAMES
    pw, vspans = _pack([small_w[n] for n in vec_names])
    pg, _ = _pack([grads[n] for n in vec_names])
    pm, _ = _pack([mom_m[n] for n in vec_names])
    pv, _ = _pack([mom_v[n] for n in vec_names])
    rows8 = lambda z: z.reshape(8, z.shape[1] // 8)
    pad_mask, _ = _pack([jnp.ones_like(small_w[n]) for n in vec_names])
    pv = jnp.where(pad_mask > 0, pv, 1.0)
    sd, sm, sv = _adamw(rows8(pw), rows8(pg), rows8(pm), rows8(pv), "adamw_small")
    for k, n in enumerate(vec_names):
        o, ln = vspans[k]
        delta[n], new_m[n], new_v[n] = (z.reshape(1, -1)[:, o:o + ln] for z in (sd, sm, sv))

    loss = lax.psum(loss_part[0, 0], ("x", "y", "c"))
    order = ("w_ada", "b_ada", "g_mix_norm", "w_in", "g_q_lat", "w_q_b", "g_kv_lat", "w_kv_b", "g_mla_q_nope", "g_mla_q_pe",
             "g_mla_k_nope", "g_mla_k_pe", "g_dil_q", "g_dil_k", "w_o", "g_ffn_norm", "w_up", "w_conv", "b_conv", "w_down")
    lead = lambda n, z: z[None] if n.startswith("w_") else z
    outs = [loss, grad_x[None]]
    for d_ in (grads, delta, new_m, new_v):
        outs += [lead(n, d_[n]) for n in order]
    return tuple(outs)
```

```python
import functools

import numpy as np
import jax
import jax.numpy as jnp
from jax import lax
from jax.experimental import pallas as pl
from jax.experimental.pallas import tpu as pltpu

F32 = jnp.float32
BF16 = jnp.bfloat16
I32 = jnp.int32

D_MODEL = 1024
HEADS = 8
NOPE = 64
ROPE = 32
Q_LORA = 512
KV_LORA = 256
DIL_DIM = 64
DIL_W = HEADS * DIL_DIM
D_FF = 2816
UP_W = 2 * D_FF
IN_COLS = Q_LORA + KV_LORA + ROPE + 3 * DIL_W
ROPE_THETA = 10000.0
EPS = 1e-6
NEG_INF = -1e30
N_DEV = 8
N_CHIP = 4

ADAM_LR = 0.001
ADAM_B1 = 0.9
ADAM_B2 = 0.999
ADAM_EPS = 1e-08
ADAM_WD = 0.01
ADAM_STEP = 10

LANE = 128
ROW_TILE = 256
ATT_TQ = 512
ATT_TK = 256
LOG2E = 1.4426950408889634
LN2 = 0.6931471805599453
VMEM_CAP = 56 * 1024 * 1024
VMEM_FLOOR = 32 * 1024 * 1024

P_QLAT, P_QD, P_KD, P_VD, P_KVLAT, P_KPE = 0, 512, 1024, 1536, 2048, 2304
P_COLS = 2432
KPE_OFF = 64

NN = (((1,), (0,)), ((), ()))
NT = (((1,), (1,)), ((), ()))
TN = (((0,), (0,)), ((), ()))
HIGHEST = lax.Precision.HIGHEST
MESH = pl.DeviceIdType.MESH


def _params(sem=None, est_bytes=0):
    limit = int(min(max(2 * est_bytes + (4 << 20), VMEM_FLOOR), VMEM_CAP))
    if sem is None:
        return pltpu.CompilerParams(vmem_limit_bytes=limit)
    return pltpu.CompilerParams(dimension_semantics=sem, vmem_limit_bytes=limit)


def _nbytes(shape, dtype):
    return int(np.prod(shape)) * jnp.dtype(dtype).itemsize


def _mm(a, b, dims, out_dtype, tm, tn, name):
    if dims == "nn":
        (m, k), (k2, n) = a.shape, b.shape
        a_spec = pl.BlockSpec((tm, k), lambda i, j: (i, 0))
        b_spec = pl.BlockSpec((k, tn), lambda i, j: (0, j))
        dn = NN
    elif dims == "nt":
        (m, k), (n, k2) = a.shape, b.shape
        a_spec = pl.BlockSpec((tm, k), lambda i, j: (i, 0))
        b_spec = pl.BlockSpec((tn, k), lambda i, j: (j, 0))
        dn = NT
    else:
        (k, m), (k2, n) = a.shape, b.shape
        a_spec = pl.BlockSpec((k, tm), lambda i, j: (0, i))
        b_spec = pl.BlockSpec((k, tn), lambda i, j: (0, j))
        dn = TN
    assert k == k2 and m % tm == 0 and n % tn == 0, (name, a.shape, b.shape, tm, tn)

    def body(a_ref, b_ref, o_ref):
        o_ref[...] = lax.dot_general(a_ref[...], b_ref[...], dn, preferred_element_type=F32).astype(o_ref.dtype)

    est = _nbytes((tm, k), a.dtype) + _nbytes((tn, k), b.dtype) + _nbytes((tm, tn), F32) + _nbytes((tm, tn), out_dtype)
    return pl.pallas_call(
        body, name=name,
        grid=(m // tm, n // tn),
        in_specs=[a_spec, b_spec],
        out_specs=pl.BlockSpec((tm, tn), lambda i, j: (i, j)),
        out_shape=jax.ShapeDtypeStruct((m, n), out_dtype),
        compiler_params=_params(("parallel", "parallel"), est),
    )(a, b)


def _seg_consts():
    seg_q = np.zeros((HEADS * LANE, LANE), np.float32)
    inv_q = np.zeros((1, LANE), np.float32)
    seg_k = np.zeros((HEADS * LANE, LANE), np.float32)
    inv_k = np.zeros((1, LANE), np.float32)
    seg_d = np.zeros((DIL_W, LANE), np.float32)
    inv_d = np.zeros((1, LANE), np.float32)
    for h in range(HEADS):
        seg_q[h * LANE:h * LANE + NOPE, 2 * h] = 1.0
        seg_q[h * LANE + NOPE:h * LANE + NOPE + ROPE, 2 * h + 1] = 1.0
        inv_q[0, 2 * h], inv_q[0, 2 * h + 1] = 1.0 / NOPE, 1.0 / ROPE
        seg_k[h * LANE:h * LANE + NOPE, h] = 1.0
        inv_k[0, h] = 1.0 / NOPE
        seg_d[h * DIL_DIM:(h + 1) * DIL_DIM, h] = 1.0
        inv_d[0, h] = 1.0 / DIL_DIM
    fold_q = np.tile(np.eye(LANE, dtype=np.float32), (HEADS, 1))
    fold_d = np.zeros((DIL_W, LANE), np.float32)
    fold_d[np.arange(DIL_W), np.arange(DIL_W) % DIL_DIM] = 1.0
    j = lambda v: jnp.asarray(v)
    b = lambda v: jnp.asarray(v, dtype=BF16)
    return dict(seg_q=b(seg_q), exp_q=b(seg_q.T.copy()), inv_q=j(inv_q), seg_k=b(seg_k), exp_k=b(seg_k.T.copy()),
                inv_k=j(inv_k), seg_d=b(seg_d), exp_d=b(seg_d.T.copy()), inv_d=j(inv_d), fold_q=j(fold_q), fold_d=j(fold_d))


def _rope_consts():
    inv_d = jnp.power(ROPE_THETA, -2.0 * jnp.arange(DIL_DIM // 2, dtype=F32) / DIL_DIM)
    inv_q = jnp.power(ROPE_THETA, -2.0 * jnp.arange(ROPE // 2, dtype=F32) / ROPE)
    lanes = np.arange(LANE)
    freq_d = inv_d[lanes % (DIL_DIM // 2)]
    in_pe = (lanes >= KPE_OFF) & (lanes < KPE_OFF + ROPE)
    freq_q = jnp.where(jnp.asarray(in_pe), inv_q[(lanes - KPE_OFF) % (ROPE // 2)], 0.0)
    sign_d = np.where(lanes % DIL_DIM < DIL_DIM // 2, -1.0, 1.0).astype(np.float32)
    sign_q = np.where(in_pe, np.where((lanes - KPE_OFF) < ROPE // 2, -1.0, 1.0), 0.0).astype(np.float32)
    zeros, ones = np.zeros(LANE, np.float32), np.ones(LANE, np.float32)
    freq = jnp.concatenate([freq_d, freq_d, freq_q, freq_q])[None, :]
    csel = jnp.asarray(np.concatenate([ones, zeros, ones, zeros]))[None, :]
    ssel = jnp.asarray(np.concatenate([zeros, sign_d, zeros, sign_q]))[None, :]
    return freq, csel, ssel


def _full(shape):
    return pl.BlockSpec(shape, lambda *_: (0,) * len(shape))


def _tile_lanes(x, n):
    return jnp.concatenate([x] * n, axis=1)


def _rope_tables(pos_col, freq, csel, ssel):
    s = pos_col.shape[0]

    def body(p_ref, f_ref, c_ref, s_ref, o_ref):
        ang = p_ref[...].astype(F32) * f_ref[...]
        o_ref[...] = c_ref[...] * jnp.cos(ang) + s_ref[...] * jnp.sin(ang)

    return pl.pallas_call(
        body, name="rope_tables", grid=(s // ROW_TILE,),
        in_specs=[pl.BlockSpec((ROW_TILE, 1), lambda i: (i, 0)), _full((1, 4 * LANE)), _full((1, 4 * LANE)), _full((1, 4 * LANE))],
        out_specs=pl.BlockSpec((ROW_TILE, 4 * LANE), lambda i: (i, 0)),
        out_shape=jax.ShapeDtypeStruct((s, 4 * LANE), F32),
        compiler_params=_params(("parallel",)),
    )(pos_col, freq, csel, ssel)


def _rms(x):
    return lax.rsqrt(jnp.mean(x * x, axis=-1, keepdims=True) + EPS)


def _prenorm(x, gain, scale, shift, name):
    s, d = x.shape

    def body(x_ref, g_ref, sc_ref, sh_ref, h_ref):
        xv = x_ref[...]
        h = (xv * _rms(xv)) * g_ref[...] * (1.0 + sc_ref[...]) + sh_ref[...]
        h_ref[...] = h.astype(BF16)

    row = pl.BlockSpec((ROW_TILE, d), lambda i: (i, 0))
    return pl.pallas_call(
        body, name=name, grid=(s // ROW_TILE,),
        in_specs=[row, _full((1, d)), _full((1, d)), _full((1, d))],
        out_specs=row, out_shape=jax.ShapeDtypeStruct((s, d), BF16),
        compiler_params=_params(("parallel",)),
    )(x, gain, scale, shift)


def _latnorm(proj, g_q, g_kv):
    s = proj.shape[0]

    def body(q_ref, kv_ref, gq_ref, gkv_ref, ql_ref, kvl_ref):
        q, kv = q_ref[...], kv_ref[...]
        ql_ref[...] = ((q * _rms(q)) * gq_ref[...]).astype(BF16)
        kvl_ref[...] = ((kv * _rms(kv)) * gkv_ref[...]).astype(BF16)

    return pl.pallas_call(
        body, name="latnorm", grid=(s // ROW_TILE,),
        in_specs=[pl.BlockSpec((ROW_TILE, Q_LORA), lambda i: (i, P_QLAT // Q_LORA)),
                  pl.BlockSpec((ROW_TILE, KV_LORA), lambda i: (i, P_KVLAT // KV_LORA)),
                  _full((1, Q_LORA)), _full((1, KV_LORA))],
        out_specs=[pl.BlockSpec((ROW_TILE, Q_LORA), lambda i: (i, 0)), pl.BlockSpec((ROW_TILE, KV_LORA), lambda i: (i, 0))],
        out_shape=[jax.ShapeDtypeStruct((s, Q_LORA), BF16), jax.ShapeDtypeStruct((s, KV_LORA), BF16)],
        compiler_params=_params(("parallel",)),
    )(proj, proj, g_q, g_kv)


def _dot01(v, mat01):
    hi = v.astype(BF16)
    lo = (v - hi.astype(F32)).astype(BF16)
    return jnp.dot(hi, mat01, preferred_element_type=F32) + jnp.dot(lo, mat01, preferred_element_type=F32)


def _seg_rinv(x, seg, exp, inv):
    r = lax.rsqrt(_dot01(x * x, seg) * inv + EPS)
    return _dot01(r, exp)


def _seg_mean(v, seg, exp, inv):
    return _dot01(_dot01(v, seg) * inv, exp)


def _swap_halves(x, half):
    n = x.shape[1]
    lane = lax.broadcasted_iota(I32, (1, n), 1)
    first = (lane & (2 * half - 1)) < half
    return jnp.where(first, pltpu.roll(x, n - half, 1), pltpu.roll(x, half, 1))


def _rope(x, cos, sin_signed, half):
    return x * cos + _swap_halves(x, half) * sin_signed


def _rope_bwd(dy, cos, sin_signed, half):
    return dy * cos + _swap_halves(dy * sin_signed, half)


def _pe_lane_mask(n):
    lane = lax.broadcasted_iota(I32, (1, n), 1) & (LANE - 1)
    return (lane >= KPE_OFF) & (lane < KPE_OFF + ROPE)


def _attn_prep(q_raw, kv_raw, proj, tab, gains, consts):
    s = q_raw.shape[0]
    hw = HEADS * LANE

    def body(q_ref, kv_ref, kpe_ref, qd_ref, kd_ref, vd_ref, tab_ref,
             gq_ref, gk_ref, gkpe_ref, gdq_ref, gdk_ref,
             segq_ref, expq_ref, invq_ref, segk_ref, expk_ref, invk_ref, segd_ref, expd_ref, invd_ref,
             qm_ref, km_ref, vm_ref, qdo_ref, kdo_ref, vdo_ref):
        tab_v = tab_ref[...]
        cos_d, sin_d = _tile_lanes(tab_v[:, 0:LANE], DIL_W // LANE), _tile_lanes(tab_v[:, LANE:2 * LANE], DIL_W // LANE)
        cos_q1, sin_q1 = tab_v[:, 2 * LANE:3 * LANE], tab_v[:, 3 * LANE:4 * LANE]
        cos_q, sin_q = _tile_lanes(cos_q1, HEADS), _tile_lanes(sin_q1, HEADS)

        q = q_ref[...]
        qn = q * _seg_rinv(q, segq_ref[...], expq_ref[...], invq_ref[...]) * gq_ref[...]
        qm_ref[...] = _rope(qn, cos_q, sin_q, ROPE // 2).astype(BF16)

        kv = kv_ref[...]
        kp = kv[:, :hw]
        kn = kp * _seg_rinv(kp, segk_ref[...], expk_ref[...], invk_ref[...]) * gk_ref[...]
        kpe = kpe_ref[...]
        r_pe = lax.rsqrt(jnp.sum(kpe * kpe, axis=-1, keepdims=True) * (1.0 / ROPE) + EPS)
        kpe_r = _rope(kpe * r_pe * gkpe_ref[...], cos_q1, sin_q1, ROPE // 2)
        km_ref[...] = (kn + _tile_lanes(kpe_r, HEADS)).astype(BF16)
        vm_ref[...] = kv[:, hw:].astype(BF16)

        qd = qd_ref[...]
        qdn = qd * _seg_rinv(qd, segd_ref[...], expd_ref[...], invd_ref[...]) * gdq_ref[...]
        qdo_ref[...] = _rope(qdn, cos_d, sin_d, DIL_DIM // 2).astype(BF16)
        kd = kd_ref[...]
        kdn = kd * _seg_rinv(kd, segd_ref[...], expd_ref[...], invd_ref[...]) * gdk_ref[...]
        kdo_ref[...] = _rope(kdn, cos_d, sin_d, DIL_DIM // 2).astype(BF16)
        vdo_ref[...] = vd_ref[...].astype(BF16)

    t = ROW_TILE
    row = lambda w, cb=0: pl.BlockSpec((t, w), lambda i: (i, cb))
    c = consts
    return pl.pallas_call(
        body, name="attn_prep", grid=(s // t,),
        in_specs=[row(hw), row(hw + DIL_W), row(LANE, P_KPE // LANE), row(DIL_W, P_QD // DIL_W), row(DIL_W, P_KD // DIL_W),
                  row(DIL_W, P_VD // DIL_W), row(4 * LANE),
                  _full((1, hw)), _full((1, hw)), _full((1, LANE)), _full((1, DIL_W)), _full((1, DIL_W)),
                  _full((hw, LANE)), _full((LANE, hw)), _full((1, LANE)), _full((hw, LANE)), _full((LANE, hw)), _full((1, LANE)),
                  _full((DIL_W, LANE)), _full((LANE, DIL_W)), _full((1, LANE))],
        out_specs=[row(hw), row(hw), row(DIL_W), row(DIL_W), row(DIL_W), row(DIL_W)],
        out_shape=[jax.ShapeDtypeStruct((s, hw), BF16), jax.ShapeDtypeStruct((s, hw), BF16)]
        + [jax.ShapeDtypeStruct((s, DIL_W), BF16)] * 4,
        compiler_params=_params(("parallel",), 24 << 20),
    )(q_raw, kv_raw, proj, proj, proj, proj, tab, gains["q"], gains["k"], gains["kpe"], gains["dq"], gains["dk"],
      c["seg_q"], c["exp_q"], c["inv_q"], c["seg_k"], c["exp_k"], c["inv_k"], c["seg_d"], c["exp_d"], c["inv_d"])


def _attn_prep_bwd(dqm, dkm, dvm, dqd, dkd, dvd, q_raw, kv_raw, proj, tab, gains, consts):
    s = q_raw.shape[0]
    hw = HEADS * LANE
    n_steps = s // ROW_TILE

    def body(dqm_ref, dkm_ref, dvm_ref, dqd_ref, dkd_ref, dvd_ref, q_ref, kv_ref, kpe_ref, qd_ref, kd_ref, tab_ref,
             gq_ref, gk_ref, gkpe_ref, gdq_ref, gdk_ref,
             segq_ref, expq_ref, invq_ref, segk_ref, expk_ref, invk_ref, segd_ref, expd_ref, invd_ref, foldq_ref, foldd_ref,
             dq_ref, dkv_ref, dkpe_ref, dqdo_ref, dkdo_ref, dvdo_ref, dg_ref, acc_ref):
        i = pl.program_id(0)

        @pl.when(i == 0)
        def _():
            acc_ref[...] = jnp.zeros_like(acc_ref)

        tab_v = tab_ref[...]
        cos_d, sin_d = _tile_lanes(tab_v[:, 0:LANE], DIL_W // LANE), _tile_lanes(tab_v[:, LANE:2 * LANE], DIL_W // LANE)
        cos_q1, sin_q1 = tab_v[:, 2 * LANE:3 * LANE], tab_v[:, 3 * LANE:4 * LANE]
        cos_q, sin_q = _tile_lanes(cos_q1, HEADS), _tile_lanes(sin_q1, HEADS)

        def norm_bwd(x, dyg, gain, seg, exp, inv):
            rinv = _seg_rinv(x, seg, exp, inv)
            xn = x * rinv
            dxn = dyg * gain
            dx = rinv * (dxn - xn * _seg_mean(dxn * xn, seg, exp, inv))
            return dx, jnp.sum(dyg * xn, axis=0, keepdims=True)

        dq, gq_l = norm_bwd(q_ref[...], _rope_bwd(dqm_ref[...], cos_q, sin_q, ROPE // 2), gq_ref[...],
                            segq_ref[...], expq_ref[...], invq_ref[...])
        dq_ref[...] = dq.astype(BF16)

        dkm = dkm_ref[...]
        kv = kv_ref[...]
        dkp, gk_l = norm_bwd(kv[:, :hw], dkm, gk_ref[...], segk_ref[...], expk_ref[...], invk_ref[...])
        dkv_ref[:, :hw] = dkp.astype(BF16)
        dkv_ref[:, hw:] = dvm_ref[...].astype(BF16)

        dkpe_r = dkm[:, 0:LANE]
        for h in range(1, HEADS):
            dkpe_r = dkpe_r + dkm[:, h * LANE:(h + 1) * LANE]
        dkpe_r = jnp.where(_pe_lane_mask(LANE), dkpe_r, 0.0)
        dyg = _rope_bwd(dkpe_r, cos_q1, sin_q1, ROPE // 2)
        kpe = kpe_ref[...]
        r_pe = lax.rsqrt(jnp.sum(kpe * kpe, axis=-1, keepdims=True) * (1.0 / ROPE) + EPS)
        xn = kpe * r_pe
        dxn = dyg * gkpe_ref[...]
        dkpe = r_pe * (dxn - xn * (jnp.sum(dxn * xn, axis=-1, keepdims=True) * (1.0 / ROPE)))
        dkpe_ref[...] = dkpe.astype(BF16)
        gkpe_l = jnp.sum(dyg * xn, axis=0, keepdims=True)

        dqd_v, gdq_l = norm_bwd(qd_ref[...], _rope_bwd(dqd_ref[...], cos_d, sin_d, DIL_DIM // 2), gdq_ref[...],
                                segd_ref[...], expd_ref[...], invd_ref[...])
        dqdo_ref[...] = dqd_v.astype(BF16)
        dkd_v, gdk_l = norm_bwd(kd_ref[...], _rope_bwd(dkd_ref[...], cos_d, sin_d, DIL_DIM // 2), gdk_ref[...],
                                segd_ref[...], expd_ref[...], invd_ref[...])
        dkdo_ref[...] = dkd_v.astype(BF16)
        dvdo_ref[...] = dvd_ref[...].astype(BF16)

        acc_ref[0:1, :] += gq_l
        acc_ref[1:2, :] += gk_l
        acc_ref[2:3, 0:LANE] += gkpe_l
        acc_ref[3:4, 0:DIL_W] += gdq_l
        acc_ref[4:5, 0:DIL_W] += gdk_l

        @pl.when(i == n_steps - 1)
        def _():
            acc = acc_ref[...]
            fq = jnp.dot(acc, foldq_ref[...], precision=HIGHEST, preferred_element_type=F32)
            fd = jnp.dot(acc[:, 0:DIL_W], foldd_ref[...], precision=HIGHEST, preferred_element_type=F32)
            rows = lax.broadcasted_iota(I32, (8, LANE), 0)
            dg_ref[...] = jnp.where(rows < 2, fq, jnp.where(rows == 2, acc[:, 0:LANE], fd))

    t = ROW_TILE
    row = lambda w, cb=0: pl.BlockSpec((t, w), lambda i: (i, cb))
    c = consts
    return pl.pallas_call(
        body, name="attn_prep_bwd", grid=(n_steps,),
        in_specs=[row(hw), row(hw), row(DIL_W), row(DIL_W), row(DIL_W), row(DIL_W),
                  row(hw), row(hw + DIL_W), row(LANE, P_KPE // LANE), row(DIL_W, P_QD // DIL_W), row(DIL_W, P_KD // DIL_W),
                  row(4 * LANE),
                  _full((1, hw)), _full((1, hw)), _full((1, LANE)), _full((1, DIL_W)), _full((1, DIL_W)),
                  _full((hw, LANE)), _full((LANE, hw)), _full((1, LANE)), _full((hw, LANE)), _full((LANE, hw)), _full((1, LANE)),
                  _full((DIL_W, LANE)), _full((LANE, DIL_W)), _full((1, LANE)), _full((hw, LANE)), _full((DIL_W, LANE))],
        out_specs=[row(hw), row(hw + DIL_W), row(LANE), row(DIL_W), row(DIL_W), row(DIL_W), _full((8, LANE))],
        out_shape=[jax.ShapeDtypeStruct((s, hw), BF16), jax.ShapeDtypeStruct((s, hw + DIL_W), BF16),
                   jax.ShapeDtypeStruct((s, LANE), BF16)] + [jax.ShapeDtypeStruct((s, DIL_W), BF16)] * 3
        + [jax.ShapeDtypeStruct((8, LANE), F32)],
        scratch_shapes=[pltpu.VMEM((8, hw), F32)],
        compiler_params=_params(("arbitrary",), 28 << 20),
    )(dqm, dkm, dvm, dqd, dkd, dvd, q_raw, kv_raw, proj, proj, proj, tab,
      gains["q"], gains["k"], gains["kpe"], gains["dq"], gains["dk"],
      c["seg_q"], c["exp_q"], c["inv_q"], c["seg_k"], c["exp_k"], c["inv_k"], c["seg_d"], c["exp_d"], c["inv_d"],
      c["fold_q"], c["fold_d"])


def _latnorm_bwd(dql, dkvl, proj, g_q, g_kv):
    s = proj.shape[0]
    n_steps = s // ROW_TILE

    def body(dql_ref, dkvl_ref, q_ref, kv_ref, gq_ref, gkv_ref, dq_ref, dkv_ref, dg_ref):
        i = pl.program_id(0)

        @pl.when(i == 0)
        def _():
            dg_ref[...] = jnp.zeros_like(dg_ref)

        def one(x, dyg, gain):
            r = _rms(x)
            xn = x * r
            dxn = dyg * gain
            dx = r * (dxn - xn * jnp.mean(dxn * xn, axis=-1, keepdims=True))
            return dx, jnp.sum(dyg * xn, axis=0, keepdims=True)

        dq, gq_l = one(q_ref[...], dql_ref[...], gq_ref[...])
        dkv, gkv_l = one(kv_ref[...], dkvl_ref[...], gkv_ref[...])
        dq_ref[...] = dq.astype(BF16)
        dkv_ref[...] = dkv.astype(BF16)
        dg_ref[0:1, :] += gq_l
        dg_ref[1:2, 0:KV_LORA] += gkv_l

    t = ROW_TILE
    return pl.pallas_call(
        body, name="latnorm_bwd", grid=(n_steps,),
        in_specs=[pl.BlockSpec((t, Q_LORA), lambda i: (i, 0)), pl.BlockSpec((t, KV_LORA), lambda i: (i, 0)),
                  pl.BlockSpec((t, Q_LORA), lambda i: (i, P_QLAT // Q_LORA)),
                  pl.BlockSpec((t, KV_LORA), lambda i: (i, P_KVLAT // KV_LORA)),
                  _full((1, Q_LORA)), _full((1, KV_LORA))],
        out_specs=[pl.BlockSpec((t, Q_LORA), lambda i: (i, 0)), pl.BlockSpec((t, KV_LORA), lambda i: (i, 0)), _full((8, Q_LORA))],
        out_shape=[jax.ShapeDtypeStruct((s, Q_LORA), BF16), jax.ShapeDtypeStruct((s, KV_LORA), BF16),
                   jax.ShapeDtypeStruct((8, Q_LORA), F32)],
        compiler_params=_params(("arbitrary",)),
    )(dql, dkvl, proj, proj, g_q, g_kv)


def _resid_prenorm(x, mix, g1, gain, scale, shift):
    s, d = x.shape

    def body(x_ref, mix_ref, g1_ref, g_ref, sc_ref, sh_ref, x1_ref, h_ref):
        x1 = x_ref[...] + g1_ref[...] * mix_ref[...]
        x1_ref[...] = x1
        h_ref[...] = ((x1 * _rms(x1)) * g_ref[...] * (1.0 + sc_ref[...]) + sh_ref[...]).astype(BF16)

    row = pl.BlockSpec((ROW_TILE, d), lambda i: (i, 0))
    vec = _full((1, d))
    return pl.pallas_call(
        body, name="resid_prenorm", grid=(s // ROW_TILE,),
        in_specs=[row, row, vec, vec, vec, vec], out_specs=[row, row],
        out_shape=[jax.ShapeDtypeStruct((s, d), F32), jax.ShapeDtypeStruct((s, d), BF16)],
        compiler_params=_params(("parallel",)),
    )(x, mix, g1, gain, scale, shift)


CONV_TILE = 1408
HALO = 8


def _shift_down(x, halo, k):
    t = x.shape[0]
    row = lax.broadcasted_iota(I32, (t, 1), 0)
    out = pltpu.roll(x, k, 0)
    for r in range(k):
        out = jnp.where(row == r, halo[HALO - k + r:HALO - k + r + 1, :], out)
    return out


def _shift_up(x, halo, k):
    t = x.shape[0]
    row = lax.broadcasted_iota(I32, (t, 1), 0)
    out = pltpu.roll(x, t - k, 0)
    for r in range(k):
        out = jnp.where(row == t - k + r, halo[r:r + 1, :], out)
    return out


def _conv_fwd(x, halo, w, b):
    p1, p2 = _shift_down(x, halo, 1), _shift_down(x, halo, 2)
    u = b + p2 * w[0:1, :]
    u = u + p1 * w[1:2, :]
    u = u + x * w[2:3, :]
    return u, p1, p2


def _sigmoid(x):
    return 1.0 / (1.0 + jnp.exp(-x))


def _conv_gate(up, w_conv, b_conv):
    s = up.shape[0]
    t = ROW_TILE
    nj = D_FF // CONV_TILE
    hb = t // HALO

    def body(g_ref, v_ref, gh_ref, vh_ref, wg_ref, wv_ref, bg_ref, bv_ref, a_ref):
        live = (pl.program_id(0) > 0).astype(F32)
        ug, _, _ = _conv_fwd(g_ref[...], gh_ref[...] * live, wg_ref[...], bg_ref[...])
        uv, _, _ = _conv_fwd(v_ref[...], vh_ref[...] * live, wv_ref[...], bv_ref[...])
        a_ref[...] = (ug * _sigmoid(ug) * uv).astype(BF16)

    main = lambda off: pl.BlockSpec((t, CONV_TILE), lambda i, j: (i, j + off))
    halo = lambda off: pl.BlockSpec((HALO, CONV_TILE), lambda i, j: (jnp.maximum(i * hb - 1, 0), j + off))
    wsp = lambda off: pl.BlockSpec((3, CONV_TILE), lambda i, j: (0, j + off))
    bsp = lambda off: pl.BlockSpec((1, CONV_TILE), lambda i, j: (0, j + off))
    return pl.pallas_call(
        body, name="conv_gate", grid=(s // t, nj),
        in_specs=[main(0), main(nj), halo(0), halo(nj), wsp(0), wsp(nj), bsp(0), bsp(nj)],
        out_specs=pl.BlockSpec((t, CONV_TILE), lambda i, j: (i, j)),
        out_shape=jax.ShapeDtypeStruct((s, D_FF), BF16),
        compiler_params=_params(("parallel", "parallel"), 12 << 20),
    )(up, up, up, up, w_conv, w_conv, b_conv, b_conv)


def _gate_bwd(up, da, w_conv, b_conv):
    s = up.shape[0]
    t = ROW_TILE
    nj = D_FF // CONV_TILE
    hb = t // HALO

    def body(g_ref, v_ref, gh_ref, vh_ref, da_ref, wg_ref, wv_ref, bg_ref, bv_ref,
             dug_ref, duv_ref, dbg_ref, dbv_ref, dwg_ref, dwv_ref):
        i = pl.program_id(1)

        @pl.when(i == 0)
        def _():
            for r in (dbg_ref, dbv_ref, dwg_ref, dwv_ref):
                r[...] = jnp.zeros_like(r)

        live = (i > 0).astype(F32)
        xg, xv = g_ref[...], v_ref[...]
        ug, g1, g2 = _conv_fwd(xg, gh_ref[...] * live, wg_ref[...], bg_ref[...])
        uv, v1, v2 = _conv_fwd(xv, vh_ref[...] * live, wv_ref[...], bv_ref[...])
        sg = _sigmoid(ug)
        da_v = da_ref[...]
        dug = da_v * uv * (sg * (1.0 + ug * (1.0 - sg)))
        duv = da_v * (ug * sg)
        dug_ref[...] = dug
        duv_ref[...] = duv
        csum = lambda z: jnp.sum(z, axis=0, keepdims=True)
        dbg_ref[...] += csum(dug)
        dbv_ref[...] += csum(duv)
        dwg_ref[0:1, :] += csum(dug * g2)
        dwg_ref[1:2, :] += csum(dug * g1)
        dwg_ref[2:3, :] += csum(dug * xg)
        dwv_ref[0:1, :] += csum(duv * v2)
        dwv_ref[1:2, :] += csum(duv * v1)
        dwv_ref[2:3, :] += csum(duv * xv)

    main = lambda off: pl.BlockSpec((t, CONV_TILE), lambda j, i: (i, j + off))
    halo = lambda off: pl.BlockSpec((HALO, CONV_TILE), lambda j, i: (jnp.maximum(i * hb - 1, 0), j + off))
    wsp = lambda off: pl.BlockSpec((3, CONV_TILE), lambda j, i: (0, j + off))
    bsp = lambda off: pl.BlockSpec((1, CONV_TILE), lambda j, i: (0, j + off))
    outs = pl.pallas_call(
        body, name="gate_bwd", grid=(nj, s // t),
        in_specs=[main(0), main(nj), halo(0), halo(nj), pl.BlockSpec((t, CONV_TILE), lambda j, i: (i, j)),
                  wsp(0), wsp(nj), bsp(0), bsp(nj)],
        out_specs=[pl.BlockSpec((t, CONV_TILE), lambda j, i: (i, j)), pl.BlockSpec((t, CONV_TILE), lambda j, i: (i, j)),
                   pl.BlockSpec((1, CONV_TILE), lambda j, i: (0, j)), pl.BlockSpec((1, CONV_TILE), lambda j, i: (0, j)),
                   pl.BlockSpec((3, CONV_TILE), lambda j, i: (0, j)), pl.BlockSpec((3, CONV_TILE), lambda j, i: (0, j))],
        out_shape=[jax.ShapeDtypeStruct((s, D_FF), F32), jax.ShapeDtypeStruct((s, D_FF), F32),
                   jax.ShapeDtypeStruct((1, D_FF), F32), jax.ShapeDtypeStruct((1, D_FF), F32),
                   jax.ShapeDtypeStruct((3, D_FF), F32), jax.ShapeDtypeStruct((3, D_FF), F32)],
        compiler_params=_params(("parallel", "arbitrary"), 20 << 20),
    )(up, up, up, up, da, w_conv, w_conv, b_conv, b_conv)
    return outs


def _conv_bwd(du, w_half, name):
    s = du.shape[0]
    t = ROW_TILE
    nj = D_FF // CONV_TILE
    hb = t // HALO
    n_i = s // t

    def body(d_ref, h_ref, w_ref, o_ref):
        live = (pl.program_id(0) < n_i - 1).astype(F32)
        x = d_ref[...]
        halo = h_ref[...] * live
        w = w_ref[...]
        o = x * w[2:3, :] + _shift_up(x, halo, 1) * w[1:2, :] + _shift_up(x, halo, 2) * w[0:1, :]
        o_ref[...] = o.astype(BF16)

    return pl.pallas_call(
        body, name=name, grid=(n_i, nj),
        in_specs=[pl.BlockSpec((t, CONV_TILE), lambda i, j: (i, j)),
                  pl.BlockSpec((HALO, CONV_TILE), lambda i, j: (jnp.minimum((i + 1) * hb, s // HALO - 1), j)),
                  pl.BlockSpec((3, CONV_TILE), lambda i, j: (0, j))],
        out_specs=pl.BlockSpec((t, CONV_TILE), lambda i, j: (i, j)),
        out_shape=jax.ShapeDtypeStruct((s, D_FF), BF16),
        compiler_params=_params(("parallel", "parallel"), 8 << 20),
    )(du, du, w_half)


def _final(x1, ffn, tgt, g2):
    s, d = x1.shape
    n_steps = s // ROW_TILE

    def body(x1_ref, f_ref, t_ref, g2_ref, dy_ref, df_ref, dg2_ref, loss_ref, lacc_ref):
        i = pl.program_id(0)

        @pl.when(i == 0)
        def _():
            dg2_ref[...] = jnp.zeros_like(dg2_ref)
            lacc_ref[...] = jnp.zeros_like(lacc_ref)

        f = f_ref[...]
        e = x1_ref[...] + g2_ref[...] * f - t_ref[...]
        dy = e * (1.0 / d)
        dy_ref[...] = dy
        df_ref[...] = (dy * g2_ref[...]).astype(BF16)
        dg2_ref[...] += jnp.sum(dy * f, axis=0, keepdims=True)
        lacc_ref[...] += jnp.sum(e * e, axis=0, keepdims=True)

        @pl.when(i == n_steps - 1)
        def _():
            loss_ref[...] = jnp.sum(lacc_ref[...], axis=1, keepdims=True) * (0.5 / d)

    row = pl.BlockSpec((ROW_TILE, d), lambda i: (i, 0))
    return pl.pallas_call(
        body, name="final", grid=(n_steps,),
        in_specs=[row, row, row, _full((1, d))],
        out_specs=[row, row, _full((1, d)), _full((1, 1))],
        out_shape=[jax.ShapeDtypeStruct((s, d), F32), jax.ShapeDtypeStruct((s, d), BF16),
                   jax.ShapeDtypeStruct((1, d), F32), jax.ShapeDtypeStruct((1, 1), F32)],
        scratch_shapes=[pltpu.VMEM((1, d), F32)],
        compiler_params=_params(("arbitrary",)),
    )(x1, ffn, tgt, g2)


def _ffnnorm_bwd(dh2, x1, dy, mix, gain, scale, g1):
    s, d = x1.shape
    n_steps = s // ROW_TILE

    def body(dh_ref, x_ref, dy_ref, mix_ref, g_ref, sc_ref, g1_ref, dx_ref, dm_ref, acc_ref):
        i = pl.program_id(0)

        @pl.when(i == 0)
        def _():
            acc_ref[...] = jnp.zeros_like(acc_ref)

        dh, x = dh_ref[...], x_ref[...]
        r = _rms(x)
        xn = x * r
        dn = dh * (1.0 + sc_ref[...])
        dxn = dn * g_ref[...]
        dx = dy_ref[...] + r * (dxn - xn * jnp.mean(dxn * xn, axis=-1, keepdims=True))
        dx_ref[...] = dx
        dm_ref[...] = (dx * g1_ref[...]).astype(BF16)
        csum = lambda z: jnp.sum(z, axis=0, keepdims=True)
        acc_ref[0:1, :] += csum(dh)
        acc_ref[1:2, :] += csum(dh * (xn * g_ref[...]))
        acc_ref[2:3, :] += csum(dn * xn)
        acc_ref[3:4, :] += csum(dx * mix_ref[...])

    row = pl.BlockSpec((ROW_TILE, d), lambda i: (i, 0))
    vec = _full((1, d))
    return pl.pallas_call(
        body, name="ffnnorm_bwd", grid=(n_steps,),
        in_specs=[row, row, row, row, vec, vec, vec],
        out_specs=[row, row, _full((8, d))],
        out_shape=[jax.ShapeDtypeStruct((s, d), F32), jax.ShapeDtypeStruct((s, d), BF16), jax.ShapeDtypeStruct((8, d), F32)],
        compiler_params=_params(("arbitrary",)),
    )(dh2, x1, dy, mix, gain, scale, g1)


def _mixnorm_bwd(dh, x, dx1, gain, scale):
    s, d = x.shape
    n_steps = s // ROW_TILE

    def body(dh_ref, x_ref, dx1_ref, g_ref, sc_ref, gx_ref, acc_ref):
        i = pl.program_id(0)

        @pl.when(i == 0)
        def _():
            acc_ref[...] = jnp.zeros_like(acc_ref)

        dh, x = dh_ref[...], x_ref[...]
        r = _rms(x)
        xn = x * r
        dn = dh * (1.0 + sc_ref[...])
        dxn = dn * g_ref[...]
        gx_ref[...] = dx1_ref[...] + r * (dxn - xn * jnp.mean(dxn * xn, axis=-1, keepdims=True))
        csum = lambda z: jnp.sum(z, axis=0, keepdims=True)
        acc_ref[0:1, :] += csum(dh)
        acc_ref[1:2, :] += csum(dh * (xn * g_ref[...]))
        acc_ref[2:3, :] += csum(dn * xn)

    row = pl.BlockSpec((ROW_TILE, d), lambda i: (i, 0))
    vec = _full((1, d))
    return pl.pallas_call(
        body, name="mixnorm_bwd", grid=(n_steps,),
        in_specs=[row, row, row, vec, vec],
        out_specs=[row, _full((8, d))],
        out_shape=[jax.ShapeDtypeStruct((s, d), F32), jax.ShapeDtypeStruct((8, d), F32)],
        compiler_params=_params(("arbitrary",)),
    )(dh, x, dx1, gain, scale)


def _key_count(d, dilated):
    if not dilated:
        return jnp.where(d >= 0, 1.0, 0.0)
    one = lambda cond: jnp.where(cond, 1.0, 0.0)
    cnt = one(d <= 128) + one(((d & 3) == 0) & (d <= 512)) + one((d & 15) == 0)
    return jnp.where(d >= 0, cnt, 0.0)


def _block_kinds(mla):
    return (0, "diag", "none") if mla else (512, "near", "far")


def _scores_t(ka, qa, scale, kind, rel_t, offset):
    st = lax.dot_general(ka, qa, NT, preferred_element_type=F32) * (scale * LOG2E)
    cnt = None
    if kind == "diag":
        st = jnp.where(rel_t + offset >= 0, st, NEG_INF)
    elif kind == "far":
        st = jnp.where((rel_t & 15) == 0, st, NEG_INF)
    elif kind == "near":
        cnt = _key_count(rel_t + offset, True)
        st = jnp.where(cnt > 0.0, st, NEG_INF)
    return st, cnt


def _attn_fwd(q, k, v, mla, scale, name, gather=()):
    s = q.shape[0]
    qw = 2 * LANE if mla else LANE
    tq, tk = ATT_TQ, ATT_TK
    reach, kind_near, kind_far = _block_kinds(mla)
    assert s % tq == 0 and tq % tk == 0 and reach % tk == 0
    ng = len(gather)
    last_step = HEADS // 2 - 1

    def body(*refs):
        q_ref, k_ref, v_ref = refs[:3]
        o_ref, lse_ref = refs[3 + ng:5 + ng]
        vt_ref = refs[5 + 2 * ng]
        comm = (refs[3:3 + ng], refs[5 + ng:5 + 2 * ng]) + tuple(refs[6 + 2 * ng:])
        if ng:
            @pl.when(pl.program_id(0) == 0)
            def _():
                _Gather(*comm).start()

            @pl.when(pl.program_id(0) == last_step)
            def _():
                _Gather(*comm).forward()

        lane = lax.broadcasted_iota(I32, (1, LANE), 1)
        rel_t = lax.broadcasted_iota(I32, (tk, tq), 1) - lax.broadcasted_iota(I32, (tk, tq), 0)

        def transpose_v(j, carry):
            c0 = pl.multiple_of(j * tk, tk)
            vt_ref[:, pl.ds(c0, tk)] = v_ref[pl.ds(c0, tk), :].astype(F32).T.astype(BF16)
            return carry

        lax.fori_loop(0, s // tk, transpose_v, 0)

        def q_block(qi, carry):
            r0 = pl.multiple_of(qi * tq, tq)
            kcols = [slice(a * LANE, (a + 1) * LANE) if mla else slice(0, LANE) for a in range(2)]
            qas = [q_ref[pl.ds(r0, tq), kcols[a]] for a in range(2)]
            if not mla:
                qas = [jnp.where(lane < DIL_DIM, qas[0], jnp.zeros_like(qas[0])),
                       jnp.where(lane >= DIL_DIM, qas[1], jnp.zeros_like(qas[1]))]

            def k_block(kj, c, kind):
                c0 = pl.multiple_of(kj * tk, tk)
                out = []
                for a in range(2):
                    m, l, acc = c[a]
                    st, cnt = _scores_t(k_ref[pl.ds(c0, tk), kcols[a]], qas[a], scale, kind, rel_t, r0 - c0)
                    m_new = jnp.maximum(m, jnp.max(st, axis=0, keepdims=True))
                    alpha = jnp.exp2(m - m_new)
                    p = jnp.exp2(st - m_new)
                    if cnt is not None:
                        p = p * cnt
                    l = alpha * l + jnp.sum(p, axis=0, keepdims=True)
                    vt = vt_ref[a * DIL_DIM:(a + 1) * DIL_DIM, pl.ds(c0, tk)]
                    acc = alpha * acc + jnp.dot(vt, p.astype(BF16), preferred_element_type=F32)
                    out.append((m_new, l, acc))
                return tuple(out)

            one = (jnp.full((1, tq), NEG_INF, F32), jnp.zeros((1, tq), F32), jnp.zeros((DIL_DIM, tq), F32))
            first_near = jnp.maximum((r0 - reach) // tk, 0)
            c = lax.fori_loop(0, first_near, functools.partial(k_block, kind=kind_far), (one, one))
            res = lax.fori_loop(first_near, (r0 + tq) // tk, functools.partial(k_block, kind=kind_near), c)
            o_t = jnp.concatenate([res[a][2] / res[a][1] for a in range(2)], axis=0)
            o_ref[pl.ds(r0, tq), :] = o_t.T.astype(BF16)
            for a in range(2):
                lse_ref[a, :, pl.ds(r0, tq)] = res[a][0] * LN2 + jnp.log(res[a][1])
            return carry

        lax.fori_loop(0, s // tq, q_block, 0)

        if ng:
            @pl.when(pl.program_id(0) == last_step)
            def _():
                _Gather(*comm).finish()

    return pl.pallas_call(
        body, name=name, grid=(HEADS // 2,),
        in_specs=[pl.BlockSpec((s, qw), lambda h: (0, h)), pl.BlockSpec((s, qw), lambda h: (0, h)),
                  pl.BlockSpec((s, LANE), lambda h: (0, h))] + [ANY] * ng,
        out_specs=[pl.BlockSpec((s, LANE), lambda h: (0, h)), pl.BlockSpec((2, 1, s), lambda h: (h, 0, 0))] + [ANY] * ng,
        out_shape=[jax.ShapeDtypeStruct((s, DIL_W), BF16), jax.ShapeDtypeStruct((HEADS, 1, s), F32)] + _Gather.out_shapes(gather),
        scratch_shapes=[pltpu.VMEM((LANE, s), BF16)] + (_Gather.semaphores(ng) if ng else []),
        compiler_params=_params(("arbitrary",) if ng else ("parallel",), 12 << 20),
    )(q, k, v, *gather)


def _attn_bwd(q, k, v, o, do, do_block0, lse, mla, scale, name, scatter=()):
    s = q.shape[0]
    qw = 2 * LANE if mla else LANE
    tq, tk = ATT_TQ, ATT_TK
    nq = s // tq
    reach, kind_near, kind_far = _block_kinds(mla)
    assert s % tq == 0 and tq % tk == 0
    ns = len(scatter)
    last_step = HEADS // 2 - 1

    def body(*refs):
        q_ref, k_ref, v_ref, o_ref, do_ref, lse_ref = refs[:6]
        dq_ref, dk_ref, dv_ref = refs[6 + ns:9 + ns]
        kt_ref, dot_ref, dob_ref, dqt_ref, delta_ref, lse2_ref = refs[9 + 2 * ns:15 + 2 * ns]
        comm = (refs[6:6 + ns], refs[9 + ns:9 + 2 * ns]) + tuple(refs[15 + 2 * ns:])
        if ns:
            @pl.when(pl.program_id(0) == 0)
            def _():
                _Scatter(*comm).start()

        lane = lax.broadcasted_iota(I32, (1, LANE), 1)
        row = lax.broadcasted_iota(I32, (LANE, 1), 0)
        rel_t = lax.broadcasted_iota(I32, (tk, tq), 1) - lax.broadcasted_iota(I32, (tk, tq), 0)

        def prepare(j, carry):
            c0 = pl.multiple_of(j * tk, tk)
            do_blk = do_ref[pl.ds(c0, tk), :]
            dob_ref[pl.ds(c0, tk), :] = do_blk.astype(BF16)
            do_t = do_blk.T
            dot_ref[:, pl.ds(c0, tk)] = do_t.astype(BF16)
            prod = do_t * o_ref[pl.ds(c0, tk), :].astype(F32).T
            delta_ref[0, :, pl.ds(c0, tk)] = jnp.sum(prod[0:DIL_DIM], axis=0, keepdims=True)
            delta_ref[1, :, pl.ds(c0, tk)] = jnp.sum(prod[DIL_DIM:LANE], axis=0, keepdims=True)
            for w in range(qw // LANE):
                kt_ref[w * LANE:(w + 1) * LANE, pl.ds(c0, tk)] = (
                    k_ref[pl.ds(c0, tk), w * LANE:(w + 1) * LANE].astype(F32).T.astype(BF16))
            return carry

        lax.fori_loop(0, s // tk, prepare, 0)
        dqt_ref[...] = jnp.zeros_like(dqt_ref)
        lse2_ref[...] = lse_ref[...] * LOG2E

        sels = [lane < DIL_DIM, lane >= DIL_DIM]
        rsels = [row < DIL_DIM, row >= DIL_DIM]
        cols = [slice(a * LANE, (a + 1) * LANE) if mla else slice(0, LANE) for a in range(2)]

        def k_block(kj, carry):
            c0 = pl.multiple_of(kj * tk, tk)
            kas = [k_ref[pl.ds(c0, tk), cols[a]] for a in range(2)]
            kts = [kt_ref[cols[a], pl.ds(c0, tk)] for a in range(2)]
            if not mla:
                kas = [jnp.where(sels[a], kas[a], jnp.zeros_like(kas[a])) for a in range(2)]
                kts = [jnp.where(rsels[a], kts[a], jnp.zeros_like(kts[a])) for a in range(2)]
            vb = v_ref[pl.ds(c0, tk), :]
            vbs = [jnp.where(sels[a], vb, jnp.zeros_like(vb)) for a in range(2)]

            def q_block(qi, c, kind):
                r0 = pl.multiple_of(qi * tq, tq)
                out, dq_parts = [], []
                for a in range(2):
                    dk_acc, dv_acc = c[a]
                    qa = q_ref[pl.ds(r0, tq), cols[a]]
                    st, cnt = _scores_t(kas[a], qa, scale, kind, rel_t, r0 - c0)
                    p = jnp.exp2(st - lse2_ref[a, :, pl.ds(r0, tq)])
                    if cnt is not None:
                        p = p * cnt
                    dp = jnp.dot(vbs[a], dot_ref[:, pl.ds(r0, tq)], preferred_element_type=F32)
                    ds = (p * (dp - delta_ref[a, :, pl.ds(r0, tq)]) * scale).astype(BF16)
                    dv_acc = dv_acc + jnp.dot(p.astype(BF16), dob_ref[pl.ds(r0, tq), :], preferred_element_type=F32)
                    dk_acc = dk_acc + jnp.dot(ds, qa, preferred_element_type=F32)
                    dq_parts.append(jnp.dot(kts[a], ds, preferred_element_type=F32))
                    out.append((dk_acc, dv_acc))
                if mla:
                    for a in range(2):
                        dqt_ref[cols[a], pl.ds(r0, tq)] += dq_parts[a]
                else:
                    dqt_ref[:, pl.ds(r0, tq)] += dq_parts[0] + dq_parts[1]
                return tuple(out)

            zero = jnp.zeros((tk, LANE), F32)
            last_near = jnp.minimum((c0 + tk - 1 + reach) // tq + 1, nq)
            c = lax.fori_loop(c0 // tq, last_near, functools.partial(q_block, kind=kind_near), ((zero, zero), (zero, zero)))
            (dk0, dv0), (dk1, dv1) = lax.fori_loop(last_near, nq, functools.partial(q_block, kind=kind_far), c)
            if mla:
                dk_ref[pl.ds(c0, tk), cols[0]] = dk0
                dk_ref[pl.ds(c0, tk), cols[1]] = dk1
            else:
                dk_ref[pl.ds(c0, tk), :] = jnp.where(sels[0], dk0, dk1)
            dv_ref[pl.ds(c0, tk), :] = jnp.where(sels[0], dv0, dv1)
            return carry

        lax.fori_loop(0, s // tk, k_block, 0)

        def write_dq(j, carry):
            c0 = pl.multiple_of(j * tk, tk)
            for w in range(qw // LANE):
                dq_ref[pl.ds(c0, tk), w * LANE:(w + 1) * LANE] = dqt_ref[w * LANE:(w + 1) * LANE, pl.ds(c0, tk)].T
            return carry

        lax.fori_loop(0, s // tk, write_dq, 0)

        if ns:
            @pl.when(pl.program_id(0) == last_step)
            def _():
                _Scatter(*comm).finish()

    b0 = do_block0
    return pl.pallas_call(
        body, name=name, grid=(HEADS // 2,),
        in_specs=[pl.BlockSpec((s, qw), lambda h: (0, h)), pl.BlockSpec((s, qw), lambda h: (0, h)),
                  pl.BlockSpec((s, LANE), lambda h: (0, h)), pl.BlockSpec((s, LANE), lambda h: (0, h)),
                  pl.BlockSpec((s, LANE), lambda h: (0, h + b0)), pl.BlockSpec((2, 1, s), lambda h: (h, 0, 0))] + [ANY] * ns,
        out_specs=[pl.BlockSpec((s, qw), lambda h: (0, h)), pl.BlockSpec((s, qw), lambda h: (0, h)),
                   pl.BlockSpec((s, LANE), lambda h: (0, h))] + [ANY] * ns,
        out_shape=[jax.ShapeDtypeStruct(q.shape, F32), jax.ShapeDtypeStruct(k.shape, F32), jax.ShapeDtypeStruct((s, DIL_W), F32)]
        + _Scatter.out_shapes(scatter),
        scratch_shapes=[pltpu.VMEM((qw, s), BF16), pltpu.VMEM((LANE, s), BF16), pltpu.VMEM((s, LANE), BF16),
                        pltpu.VMEM((qw, s), F32), pltpu.VMEM((2, 1, s), F32), pltpu.VMEM((2, 1, s), F32)]
        + (_Scatter.semaphores(ns) if ns else []),
        compiler_params=_params(("arbitrary",) if ns else ("parallel",), 24 << 20),
    )(q, k, v, o, do, lse, *scatter)


def _ada_fwd(c_all, w_shard, b_shard):
    n, d = c_all.shape
    cols = w_shard.shape[1]

    def body(c_ref, w_ref, b_ref, o_ref):
        cv = c_ref[...]
        sc = (cv * _sigmoid(cv)).astype(BF16)
        o_ref[...] = jnp.dot(sc, w_ref[...].astype(BF16), preferred_element_type=F32) + b_ref[...]

    return pl.pallas_call(
        body, name="ada_fwd", out_shape=jax.ShapeDtypeStruct((n, cols), F32),
        compiler_params=_params(None, 16 << 20),
    )(c_all, w_shard, b_shard)


def _ada_bwd(c_all, dmod_shard):
    n, d = c_all.shape
    cols = dmod_shard.shape[1]

    def body(c_ref, g_ref, o_ref):
        cv = c_ref[...]
        o_ref[...] = lax.dot_general(cv * _sigmoid(cv), g_ref[...], TN, precision=HIGHEST, preferred_element_type=F32)

    return pl.pallas_call(
        body, name="ada_bwd", out_shape=jax.ShapeDtypeStruct((d, cols), F32),
        compiler_params=_params(None, 16 << 20),
    )(c_all, dmod_shard)


def _sum_devices(g):
    n, r, w = g.shape

    def body(g_ref, o_ref):
        acc = g_ref[0]
        for k in range(1, n):
            acc = acc + g_ref[k]
        o_ref[...] = acc

    return pl.pallas_call(
        body, name="sum_devices", out_shape=jax.ShapeDtypeStruct((r, w), F32),
        compiler_params=_params(None, 4 << 20),
    )(g)


def _adamw(w, g, m, v, name):
    r, c = w.shape
    tr = r
    for cand in (256, 128, 64, 32, 16, 8):
        if r % cand == 0 and r > cand:
            tr = cand
            break

    def body(w_ref, g_ref, m_ref, v_ref, d_ref, mo_ref, vo_ref):
        gv = g_ref[...]
        mn = ADAM_B1 * m_ref[...] + (1.0 - ADAM_B1) * gv
        vn = ADAM_B2 * v_ref[...] + (1.0 - ADAM_B2) * (gv * gv)
        m_hat = mn / (1.0 - ADAM_B1 ** ADAM_STEP)
        v_hat = vn / (1.0 - ADAM_B2 ** ADAM_STEP)
        d_ref[...] = -ADAM_LR * (m_hat / (jnp.sqrt(v_hat) + ADAM_EPS) + ADAM_WD * w_ref[...])
        mo_ref[...] = mn
        vo_ref[...] = vn

    blk = pl.BlockSpec((tr, c), lambda i: (i, 0))
    return pl.pallas_call(
        body, name=name, grid=(r // tr,), in_specs=[blk] * 4, out_specs=[blk] * 3,
        out_shape=[jax.ShapeDtypeStruct((r, c), F32)] * 3,
        compiler_params=_params(("parallel",), 7 * _nbytes((tr, c), F32)),
    )(w, g, m, v)


def _position():
    return lax.axis_index("x"), lax.axis_index("y"), lax.axis_index("c")


def _other_chips(x, y):
    return [(1 - x, y, 2 * (1 - x) + y), (x, 1 - y, 2 * x + (1 - y)), (1 - x, 1 - y, 2 * (1 - x) + (1 - y))]


def _ag_small(v, name):
    r, w = v.shape

    def body(v_ref, out_ref, send_sems, recv_sems, local_sem):
        x, y, c = _position()
        me = 4 * x + 2 * y + c
        mine = pltpu.make_async_copy(v_ref, out_ref.at[me], local_sem)
        mine.start()
        peers = []
        for k in range(1, N_DEV):
            fx, fy, fc = (k >> 2) & 1, (k >> 1) & 1, k & 1
            px = 1 - x if fx else x
            py = 1 - y if fy else y
            pc = 1 - c if fc else c
            peers.append((px, py, pc))
        sends = []
        for k, peer in enumerate(peers):
            cp = pltpu.make_async_remote_copy(src_ref=v_ref, dst_ref=out_ref.at[me], send_sem=send_sems.at[k],
                                              recv_sem=recv_sems.at[k], device_id=peer, device_id_type=MESH)
            cp.start()
            sends.append(cp)
        for k, (px, py, pc) in enumerate(peers):
            pltpu.make_async_remote_copy(src_ref=v_ref, dst_ref=out_ref.at[4 * px + 2 * py + pc], send_sem=send_sems.at[k],
                                         recv_sem=recv_sems.at[k], device_id=(px, py, pc), device_id_type=MESH).wait_recv()
        for cp in sends:
            cp.wait_send()
        mine.wait()

    return pl.pallas_call(
        body, name=name,
        out_shape=jax.ShapeDtypeStruct((N_DEV, r, w), F32),
        in_specs=[pl.BlockSpec(memory_space=pltpu.VMEM)],
        out_specs=pl.BlockSpec(memory_space=pltpu.VMEM),
        scratch_shapes=[pltpu.SemaphoreType.DMA((N_DEV - 1,)), pltpu.SemaphoreType.DMA((N_DEV - 1,)), pltpu.SemaphoreType.DMA],
        compiler_params=_params(None, 10 * _nbytes((r, w), F32)),
    )(v)


ANY = pl.BlockSpec(memory_space=pl.ANY)


def _ag_weights(shards, name):
    n = len(shards)

    def body(*refs):
        gather = _Gather(refs[:n], refs[n:2 * n], *refs[2 * n:])
        gather.start()
        gather.forward()
        gather.finish()

    return pl.pallas_call(
        body, name=name,
        out_shape=_Gather.out_shapes(shards), in_specs=[ANY] * n, out_specs=[ANY] * n,
        scratch_shapes=_Gather.semaphores(n),
    )(*shards)


class _Gather:
    def __init__(self, w_refs, out_refs, send_sems, recv_sems):
        x, y, c = _position()
        q0 = 2 * x + y
        sibling = (x, y, 1 - c)
        self.ici, self.ici_in, self.fwd, self.fwd_in = [], [], [], []
        for k, (w_ref, out_ref) in enumerate(zip(w_refs, out_refs)):
            half = w_ref.shape[0] // 2

            def blk(q, e, out_ref=out_ref, half=half):
                return out_ref.at[q, pl.ds(pl.multiple_of(e * half, 16), half), :]

            def copy(src, dst, i, to):
                return pltpu.make_async_remote_copy(src_ref=src, dst_ref=dst, send_sem=send_sems.at[i], recv_sem=recv_sems.at[i],
                                                    device_id=to, device_id_type=MESH)

            src = w_ref.at[pl.ds(pl.multiple_of(c * half, 16), half), :]
            for j, (cx, cy, qj) in enumerate(_other_chips(x, y)):
                self.ici.append(copy(src, blk(q0, c), 6 * k + j, (cx, cy, c)))
                self.ici_in.append(copy(blk(qj, c), blk(qj, c), 6 * k + j, (cx, cy, c)))
                self.fwd.append(copy(blk(qj, c), blk(qj, c), 6 * k + 3 + j, sibling))
                self.fwd_in.append(copy(blk(qj, 1 - c), blk(qj, 1 - c), 6 * k + 3 + j, sibling))

    @staticmethod
    def out_shapes(shards):
        return [jax.ShapeDtypeStruct((N_CHIP,) + s.shape, s.dtype) for s in shards]

    @staticmethod
    def semaphores(n):
        return [pltpu.SemaphoreType.DMA((6 * n,)), pltpu.SemaphoreType.DMA((6 * n,))]

    def start(self):
        for cp in self.ici:
            cp.start()

    def forward(self):
        for arrived, onward in zip(self.ici_in, self.fwd):
            arrived.wait_recv()
            onward.start()

    def finish(self):
        for cp in self.fwd_in:
            cp.wait_recv()
        for cp in self.ici + self.fwd:
            cp.wait_send()


def _swap_halves_d2d(grads, name):
    n = len(grads)

    def body(*refs):
        g_refs, out_refs = refs[:n], refs[n:2 * n]
        send_sems, recv_sems = refs[2 * n:]
        x, y, c = _position()
        sibling = (x, y, 1 - c)
        cps = []
        for k in range(n):
            cp = pltpu.make_async_remote_copy(src_ref=g_refs[k].at[:, 1 - c], dst_ref=out_refs[k], send_sem=send_sems.at[k],
                                              recv_sem=recv_sems.at[k], device_id=sibling, device_id_type=MESH)
            cp.start()
            cps.append(cp)
        for cp in cps:
            cp.wait_recv()
        for cp in cps:
            cp.wait_send()

    return pl.pallas_call(
        body, name=name,
        out_shape=[jax.ShapeDtypeStruct((N_CHIP,) + g.shape[2:], g.dtype) for g in grads],
        in_specs=[ANY] * n, out_specs=[ANY] * n,
        scratch_shapes=[pltpu.SemaphoreType.DMA((n,)), pltpu.SemaphoreType.DMA((n,))],
    )(*grads)


def _pair_sum(g, a, c_idx, name):
    _, _, rh, cols = g.shape
    tr = rh
    for cand in (256, 128, 64, 32, 16):
        if rh % cand == 0 and rh > cand:
            tr = cand
            break

    def body(c_ref, g_ref, a_ref, o_ref):
        o_ref[...] = (g_ref[...] + a_ref[...]).astype(BF16)

    return pl.pallas_call(
        body, name=name,
        grid_spec=pltpu.PrefetchScalarGridSpec(
            num_scalar_prefetch=1, grid=(N_CHIP, rh // tr),
            in_specs=[pl.BlockSpec((None, None, tr, cols), lambda q, i, c_ref: (q, c_ref[0], i, 0)),
                      pl.BlockSpec((None, tr, cols), lambda q, i, c_ref: (q, i, 0))],
            out_specs=pl.BlockSpec((None, tr, cols), lambda q, i, c_ref: (q, i, 0))),
        out_shape=jax.ShapeDtypeStruct((N_CHIP, rh, cols), BF16),
        compiler_params=_params(("parallel", "parallel"), 10 * _nbytes((tr, cols), F32)),
    )(c_idx, g, a)


def _scatter_partials(parts, name):
    n = len(parts)

    def body(*refs):
        scatter = _Scatter(refs[:n], refs[n:2 * n], *refs[2 * n:])
        scatter.start()
        scatter.finish()

    return pl.pallas_call(
        body, name=name,
        out_shape=_Scatter.out_shapes(parts), in_specs=[ANY] * n, out_specs=[ANY] * n,
        scratch_shapes=_Scatter.semaphores(n),
    )(*parts)


class _Scatter:
    def __init__(self, p_refs, out_refs, send_sems, recv_sems):
        x, y, c = _position()
        self.copies = []
        for k, (p_ref, out_ref) in enumerate(zip(p_refs, out_refs)):
            for j, (cx, cy, qj) in enumerate(_other_chips(x, y)):
                self.copies.append(pltpu.make_async_remote_copy(
                    src_ref=p_ref.at[qj], dst_ref=out_ref.at[j], send_sem=send_sems.at[3 * k + j],
                    recv_sem=recv_sems.at[3 * k + j], device_id=(cx, cy, c), device_id_type=MESH))

    @staticmethod
    def out_shapes(parts):
        return [jax.ShapeDtypeStruct((3,) + p.shape[1:], p.dtype) for p in parts]

    @staticmethod
    def semaphores(n):
        return [pltpu.SemaphoreType.DMA((3 * n,)), pltpu.SemaphoreType.DMA((3 * n,))]

    def start(self):
        for cp in self.copies:
            cp.start()

    def finish(self):
        for cp in self.copies:
            cp.wait_recv()
        for cp in self.copies:
            cp.wait_send()


def _shard_sum(p, b, q_idx, name):
    _, rh, cols = p.shape
    tr = rh
    for cand in (256, 128, 64, 32, 16):
        if rh % cand == 0 and rh > cand:
            tr = cand
            break

    def body(q_ref, p_ref, b_ref, o_ref):
        acc = p_ref[...].astype(F32)
        for j in range(3):
            acc = acc + b_ref[j].astype(F32)
        o_ref[...] = acc

    return pl.pallas_call(
        body, name=name,
        grid_spec=pltpu.PrefetchScalarGridSpec(
            num_scalar_prefetch=1, grid=(rh // tr,),
            in_specs=[pl.BlockSpec((None, tr, cols), lambda i, q_ref: (q_ref[0], i, 0)),
                      pl.BlockSpec((3, tr, cols), lambda i, q_ref: (0, i, 0))],
            out_specs=pl.BlockSpec((tr, cols), lambda i, q_ref: (i, 0))),
        out_shape=jax.ShapeDtypeStruct((rh, cols), F32),
        compiler_params=_params(("parallel",), 8 * _nbytes((tr, cols), F32)),
    )(q_idx, p, b)


def _join_halves(halves):
    n = len(halves)

    def body(*refs):
        h_refs, out_refs = refs[:n], refs[n:2 * n]
        send_sems, recv_sems = refs[2 * n:]
        x, y, c = _position()
        sibling = (x, y, 1 - c)
        cps = []
        for k in range(n):
            cp = pltpu.make_async_remote_copy(src_ref=h_refs[k], dst_ref=out_refs[k], send_sem=send_sems.at[k],
                                              recv_sem=recv_sems.at[k], device_id=sibling, device_id_type=MESH)
            cp.start()
            cps.append(cp)
        for cp in cps:
            cp.wait_recv()
        for cp in cps:
            cp.wait_send()

    return pl.pallas_call(
        body, name="rs_join",
        out_shape=[jax.ShapeDtypeStruct(h.shape, h.dtype) for h in halves],
        in_specs=[ANY] * n, out_specs=[ANY] * n,
        scratch_shapes=[pltpu.SemaphoreType.DMA((n,)), pltpu.SemaphoreType.DMA((n,))],
    )(*halves)


def _cols_from_shards(g):
    q, r, cs = g.shape
    return jnp.transpose(g, (1, 0, 2)).reshape(r, q * cs)


def _cols_to_shards(w):
    r, cfull = w.shape
    return jnp.transpose(w.reshape(r, N_CHIP, cfull // N_CHIP), (1, 0, 2))


def _pad_w_in(w):
    z = lambda n: jnp.zeros((w.shape[0], n), w.dtype)
    q_lat, kv_lat, kpe = w[:, 0:512], w[:, 512:768], w[:, 768:800]
    qd, kd, vd = w[:, 800:1312], w[:, 1312:1824], w[:, 1824:2336]
    return jnp.concatenate([q_lat, qd, kd, vd, kv_lat, z(KPE_OFF), kpe, z(LANE - KPE_OFF - ROPE)], axis=1)


def _unpad_w_in(g):
    return jnp.concatenate([g[:, P_QLAT:P_QLAT + Q_LORA], g[:, P_KVLAT:P_KVLAT + KV_LORA],
                            g[:, P_KPE + KPE_OFF:P_KPE + KPE_OFF + ROPE], g[:, P_QD:P_QD + 3 * DIL_W]], axis=1)


def _pad_w_qb(w):
    w3 = w.reshape(Q_LORA, HEADS, NOPE + ROPE)
    return jnp.pad(w3, ((0, 0), (0, 0), (0, LANE - NOPE - ROPE))).reshape(Q_LORA, HEADS * LANE)


def _unpad_w_qb(g):
    return g.reshape(Q_LORA, HEADS, LANE)[:, :, :NOPE + ROPE].reshape(Q_LORA, HEADS * (NOPE + ROPE))


def _pad_w_kvb(w):
    w3 = w.reshape(KV_LORA, HEADS, 2 * NOPE)
    kp = jnp.pad(w3[:, :, :NOPE], ((0, 0), (0, 0), (0, LANE - NOPE))).reshape(KV_LORA, HEADS * LANE)
    return jnp.concatenate([kp, w3[:, :, NOPE:].reshape(KV_LORA, DIL_W)], axis=1)


def _unpad_w_kvb(g):
    gk = g[:, :HEADS * LANE].reshape(KV_LORA, HEADS, LANE)[:, :, :NOPE]
    gv = g[:, HEADS * LANE:].reshape(KV_LORA, HEADS, NOPE)
    return jnp.concatenate([gk, gv], axis=2).reshape(KV_LORA, HEADS * 2 * NOPE)


def _head_gains(g_q_nope, g_q_pe, g_k_nope, g_k_pe, g_dq, g_dk):
    z = lambda n: jnp.zeros((1, n), F32)
    q1 = jnp.concatenate([g_q_nope, g_q_pe, z(LANE - NOPE - ROPE)], axis=1)
    k1 = jnp.concatenate([g_k_nope, z(LANE - NOPE)], axis=1)
    kpe = jnp.concatenate([z(KPE_OFF), g_k_pe, z(LANE - KPE_OFF - ROPE)], axis=1)
    return dict(q=jnp.tile(q1, (1, HEADS)), k=jnp.tile(k1, (1, HEADS)), kpe=kpe,
                dq=jnp.tile(g_dq, (1, HEADS)), dk=jnp.tile(g_dk, (1, HEADS)))


SMALL_NAMES = ("g_mix_norm", "g_q_lat", "g_kv_lat", "g_mla_q_nope", "g_mla_q_pe", "g_mla_k_nope", "g_mla_k_pe",
               "g_dil_q", "g_dil_k", "g_ffn_norm", "b_conv")


def _pack(vs):
    parts, spans, off = [], [], 0
    for v in vs:
        n = v.shape[1]
        npad = -(-n // LANE) * LANE
        parts.append(jnp.pad(v, ((0, 0), (0, npad - n))))
        spans.append((off, n))
        off += npad
    return jnp.concatenate(parts, axis=1), spans


def kernel(x, c, positions, w_ada, b_ada, g_mix_norm, w_in, g_q_lat, w_q_b, g_kv_lat, w_kv_b, g_mla_q_nope, g_mla_q_pe, g_mla_k_nope, g_mla_k_pe, g_dil_q, g_dil_k, w_o, g_ffn_norm, w_up, w_conv, b_conv, w_down, loss_target, m_w_ada, m_b_ada, m_g_mix_norm, m_w_in, m_g_q_lat, m_w_q_b, m_g_kv_lat, m_w_kv_b, m_g_mla_q_nope, m_g_mla_q_pe, m_g_mla_k_nope, m_g_mla_k_pe, m_g_dil_q, m_g_dil_k, m_w_o, m_g_ffn_norm, m_w_up, m_w_conv, m_b_conv, m_w_down, v_w_ada, v_b_ada, v_g_mix_norm, v_w_in, v_g_q_lat, v_w_q_b, v_g_kv_lat, v_w_kv_b, v_g_mla_q_nope, v_g_mla_q_pe, v_g_mla_k_nope, v_g_mla_k_pe, v_g_dil_q, v_g_dil_k, v_w_o, v_g_ffn_norm, v_w_up, v_w_conv, v_b_conv, v_w_down):
    args = dict(locals())
    weights = {n: args[n][0] for n in ("w_ada", "w_in", "w_q_b", "w_kv_b", "w_o", "w_up", "w_conv", "w_down")}
    small_w = {n: args[n] for n in SMALL_NAMES + ("b_ada",)}
    mom_m = {n[2:]: (args[n][0] if args[n].ndim == 3 else args[n]) for n in args if n.startswith("m_")}
    mom_v = {n[2:]: (args[n][0] if args[n].ndim == 3 else args[n]) for n in args if n.startswith("v_")}

    xi, yi, ci = _position()
    q0 = 2 * xi + yi
    me = 4 * xi + 2 * yi + ci
    xs, tgt = x[0], loss_target[0]
    s = xs.shape[0]
    consts = _seg_consts()
    c_idx, q_idx = jnp.reshape(ci, (1,)).astype(I32), jnp.reshape(q0, (1,)).astype(I32)

    def halves(g4):
        q, r, cc = g4.shape
        return g4.reshape(q, 2, r // 2, cc)

    c_all = _ag_small(c, "ag_c")[:, 0, :]
    ada_cols = w_ada.shape[2]
    b_shard = lax.dynamic_slice_in_dim(b_ada, q0 * ada_cols, ada_cols, axis=1)
    mod_blk = _ada_fwd(c_all, weights["w_ada"], b_shard)
    mod_all = _ag_small(mod_blk, "ag_mod").reshape(N_CHIP, 2, N_DEV, ada_cols)
    mod = lax.dynamic_index_in_dim(lax.dynamic_index_in_dim(mod_all, ci, 1, False), me, 1, False)
    mod = mod.reshape(1, N_CHIP * ada_cols)
    sh1, sc1, g1, sh2, sc2, g2 = [mod[:, k * D_MODEL:(k + 1) * D_MODEL] for k in range(6)]

    place_own = lambda gs, ws: [lax.dynamic_update_slice_in_dim(g, w[None], q0, axis=0) for g, w in zip(gs, ws)]
    own_first = [weights[n].astype(BF16) for n in ("w_in", "w_q_b", "w_kv_b")]
    own_later = [weights[n].astype(BF16) for n in ("w_o", "w_up", "w_down")]
    gathered = place_own(_ag_weights(own_first, "ag_weights"), own_first)
    w_in_p = _pad_w_in(_cols_from_shards(gathered[0]))
    w_qb_p = _pad_w_qb(_cols_from_shards(gathered[1]))
    w_kvb_p = _pad_w_kvb(_cols_from_shards(gathered[2]))
    w_conv_f = _ag_small(weights["w_conv"], "ag_wconv")
    w_conv_f = jnp.transpose(w_conv_f.reshape(N_CHIP, 2, 3, -1)[:, 0], (1, 0, 2)).reshape(3, UP_W)

    gains = _head_gains(g_mla_q_nope, g_mla_q_pe, g_mla_k_nope, g_mla_k_pe, g_dil_q, g_dil_k)
    tab = _rope_tables(positions.reshape(s, 1), *_rope_consts())

    h = _prenorm(xs, g_mix_norm, sc1, sh1, "prenorm")
    proj = _mm(h, w_in_p, "nn", F32, 512, P_COLS, "mm_in")
    ql, kvl = _latnorm(proj, g_q_lat, g_kv_lat)
    q_raw = _mm(ql, w_qb_p, "nn", F32, 512, HEADS * LANE, "mm_qb")
    kv_raw = _mm(kvl, w_kvb_p, "nn", F32, 512, HEADS * LANE + DIL_W, "mm_kvb")
    qm, km, vm, qd, kd, vd = _attn_prep(q_raw, kv_raw, proj, tab, gains, consts)
    scale_m, scale_d = (NOPE + ROPE) ** -0.5, DIL_DIM ** -0.5
    o_m, lse_m, *gathered = _attn_fwd(qm, km, vm, True, scale_m, "attn_mla", gather=own_later)
    gathered = place_own(gathered, own_later)
    w_o_f = gathered[0].reshape(D_MODEL, D_MODEL)
    w_up_f = _cols_from_shards(gathered[1])
    w_down_f = gathered[2].reshape(D_FF, D_MODEL)
    o_d, lse_d = _attn_fwd(qd, kd, vd, False, scale_d, "attn_dil")
    mix_in = jnp.concatenate([o_m, o_d], axis=1)
    mix = _mm(mix_in, w_o_f, "nn", F32, 512, D_MODEL, "mm_o")
    x1, h2 = _resid_prenorm(xs, mix, g1, g_ffn_norm, sc2, sh2)
    up = _mm(h2, w_up_f, "nn", F32, 512, CONV_TILE, "mm_up")
    act = _conv_gate(up, w_conv_f, b_conv)
    ffn = _mm(act, w_down_f, "nn", F32, 256, D_MODEL, "mm_down")
    dy, dffn, dg2, loss_part = _final(x1, ffn, tgt, g2)

    da = _mm(dffn, w_down_f, "nt", F32, 512, CONV_TILE, "mm_down_dx")
    gw_down = _mm(act, dffn, "tn", F32, 256, D_MODEL, "mm_down_dw")
    dug, duv, dbg, dbv, dwg, dwv = _gate_bwd(up, da, w_conv_f, b_conv)
    dup = jnp.concatenate([_conv_bwd(dug, w_conv_f[:, :D_FF], "conv_bwd_gate"),
                           _conv_bwd(duv, w_conv_f[:, D_FF:], "conv_bwd_val")], axis=1)
    dh2 = _mm(dup, w_up_f, "nt", F32, 256, 512, "mm_up_dx")
    gw_up = _mm(h2, dup, "tn", F32, 512, CONV_TILE, "mm_up_dw")
    dx1, dmix, acc2 = _ffnnorm_bwd(dh2, x1, dy, mix, g_ffn_norm, sc2, g1)
    dmix_in = _mm(dmix, w_o_f, "nt", F32, 512, D_MODEL, "mm_o_dx")
    gw_o = _mm(mix_in, dmix, "tn", F32, 512, D_MODEL, "mm_o_dw")
    early_names = ("w_up", "w_down")
    early = [halves(_cols_to_shards(gw_up)), halves(gw_down.reshape(N_CHIP, D_FF // N_CHIP, D_MODEL))]
    early_sib = _swap_halves_d2d(early, "rs_pair_swap_early")
    early_sums = [_pair_sum(g, a, c_idx, "pair_sum_" + n) for g, a, n in zip(early, early_sib, early_names)]
    dqm, dkm, dvm, *early_recv = _attn_bwd(qm, km, vm, o_m, dmix_in, 0, lse_m, True, scale_m, "attn_mla_bwd",
                                           scatter=early_sums)
    dqd, dkd, dvd = _attn_bwd(qd, kd, vd, o_d, dmix_in, DIL_W // LANE, lse_d, False, scale_d, "attn_dil_bwd")
    dq_raw, dkv_raw, dkpe_b, dqd_b, dkd_b, dvd_b, dgains = _attn_prep_bwd(
        dqm, dkm, dvm, dqd, dkd, dvd, q_raw, kv_raw, proj, tab, gains, consts)
    dql = _mm(dq_raw, w_qb_p, "nt", F32, 512, Q_LORA, "mm_qb_dx")
    gw_qb = _unpad_w_qb(_mm(ql, dq_raw, "tn", F32, Q_LORA, HEADS * LANE, "mm_qb_dw"))
    dkvl = _mm(dkv_raw, w_kvb_p, "nt", F32, 512, KV_LORA, "mm_kvb_dx")
    gw_kvb = _unpad_w_kvb(_mm(kvl, dkv_raw, "tn", F32, KV_LORA, HEADS * LANE + DIL_W, "mm_kvb_dw"))
    dqlat_b, dkvlat_b, dglat = _latnorm_bwd(dql, dkvl, proj, g_q_lat, g_kv_lat)
    dproj = jnp.concatenate([dqlat_b, dqd_b, dkd_b, dvd_b, dkvlat_b, dkpe_b], axis=1)
    dh = _mm(dproj, w_in_p, "nt", F32, 512, D_MODEL, "mm_in_dx")
    gw_in = _unpad_w_in(_mm(h, dproj, "tn", F32, 512, P_COLS, "mm_in_dw"))
    grad_x, acc1 = _mixnorm_bwd(dh, xs, dx1, g_mix_norm, sc1)

    dmod = jnp.concatenate([acc1[0:1], acc1[1:2], acc2[3:4], acc2[0:1], acc2[1:2], dg2], axis=1)
    small_g = {"g_mix_norm": acc1[2:3], "g_q_lat": dglat[0:1], "g_kv_lat": dglat[1:2, :KV_LORA],
               "g_mla_q_nope": dgains[0:1, :NOPE], "g_mla_q_pe": dgains[0:1, NOPE:NOPE + ROPE],
               "g_mla_k_nope": dgains[1:2, :NOPE], "g_mla_k_pe": dgains[2:3, KPE_OFF:KPE_OFF + ROPE],
               "g_dil_q": dgains[3:4, :DIL_DIM], "g_dil_k": dgains[4:5, :DIL_DIM], "g_ffn_norm": acc2[2:3],
               "b_conv": jnp.concatenate([dbg, dbv], axis=1)}
    dw_conv = jnp.concatenate([dwg, dwv], axis=1)
    packed, spans = _pack([dmod] + [small_g[n] for n in SMALL_NAMES] + [dw_conv[k:k + 1] for k in range(3)])
    gathered_small = _ag_small(packed, "ag_small")
    summed = _sum_devices(gathered_small)
    take = lambda k: summed[:, spans[k][0]:spans[k][0] + spans[k][1]]
    grads = {"b_ada": take(0)}
    for k, n in enumerate(SMALL_NAMES):
        grads[n] = take(1 + k)
    shard_cols = UP_W // N_CHIP
    gconv_full = jnp.concatenate([take(1 + len(SMALL_NAMES) + k) for k in range(3)], axis=0)
    grads["w_conv"] = lax.dynamic_slice_in_dim(gconv_full, q0 * shard_cols, shard_cols, axis=1)
    dmod_all = gathered_small[:, 0, :6 * D_MODEL]
    grads["w_ada"] = _ada_bwd(c_all, lax.dynamic_slice_in_dim(dmod_all, q0 * ada_cols, ada_cols, axis=1))

    late_names = ("w_in", "w_q_b", "w_kv_b", "w_o")
    late = [halves(_cols_to_shards(gw_in)), halves(_cols_to_shards(gw_qb)), halves(_cols_to_shards(gw_kvb)),
            halves(gw_o.reshape(N_CHIP, D_MODEL // N_CHIP, D_MODEL))]
    late_sib = _swap_halves_d2d(late, "rs_pair_swap_late")
    late_sums = [_pair_sum(g, a, c_idx, "pair_sum_" + n) for g, a, n in zip(late, late_sib, late_names)]
    late_recv = _scatter_partials(late_sums, "rs_scatter_late")
    big_names = late_names + early_names
    half_sums = [_shard_sum(p, b, q_idx, "shard_sum_" + n)
                 for p, b, n in zip(late_sums + early_sums, list(late_recv) + list(early_recv), big_names)]
    from_sib = _join_halves(half_sums)
    south = ci == 0
    for n, mine, theirs in zip(big_names, half_sums, from_sib):
        grads[n] = jnp.concatenate([jnp.where(south, mine, theirs), jnp.where(south, theirs, mine)], axis=0)

    delta, new_m, new_v = {}, {}, {}
    for n in ("w_ada", "w_in", "w_q_b", "w_kv_b", "w_o", "w_up", "w_conv", "w_down"):
        delta[n], new_m[n], new_v[n] = _adamw(weights[n], grads[n], mom_m[n], mom_v[n], "adamw_" + n)
    vec_names = ("b_ada",) + SMALL_NAMES
    pw, vspans = _pack([small_w[n] for n in vec_names])
    pg, _ = _pack([grads[n] for n in vec_names])
    pm, _ = _pack([mom_m[n] for n in vec_names])
    pv, _ = _pack([mom_v[n] for n in vec_names])
    rows8 = lambda z: z.reshape(8, z.shape[1] // 8)
    pad_mask, _ = _pack([jnp.ones_like(small_w[n]) for n in vec_names])
    pv = jnp.where(pad_mask > 0, pv, 1.0)
    sd, sm, sv = _adamw(rows8(pw), rows8(pg), rows8(pm), rows8(pv), "adamw_small")
    for k, n in enumerate(vec_names):
        o, ln = vspans[k]
        delta[n], new_m[n], new_v[n] = (z.reshape(1, -1)[:, o:o + ln] for z in (sd, sm, sv))

    loss = lax.psum(loss_part[0, 0], ("x", "y", "c"))
    order = ("w_ada", "b_ada", "g_mix_norm", "w_in", "g_q_lat", "w_q_b", "g_kv_lat", "w_kv_b", "g_mla_q_nope", "g_mla_q_pe",
             "g_mla_k_nope", "g_mla_k_pe", "g_dil_q", "g_dil_k", "w_o", "g_ffn_norm", "w_up", "w_conv", "b_conv", "w_down")
    lead = lambda n, z: z[None] if n.startswith("w_") else z
    outs = [loss, grad_x[None]]
    for d_ in (grads, delta, new_m, new_v):
        outs += [lead(n, d_[n]) for n in order]
    return tuple(outs)
```

```python
import functools

import numpy as np
import jax
import jax.numpy as jnp
from jax import lax
from jax.experimental import pallas as pl
from jax.experimental.pallas import tpu as pltpu

F32 = jnp.float32
BF16 = jnp.bfloat16
I32 = jnp.int32

D_MODEL = 1024
HEADS = 8
NOPE = 64
ROPE = 32
Q_LORA = 512
KV_LORA = 256
DIL_DIM = 64
DIL_W = HEADS * DIL_DIM
D_FF = 2816
UP_W = 2 * D_FF
IN_COLS = Q_LORA + KV_LORA + ROPE + 3 * DIL_W
ROPE_THETA = 10000.0
EPS = 1e-6
NEG_INF = -1e30
N_DEV = 8
N_CHIP = 4

ADAM_LR = 0.001
ADAM_B1 = 0.9
ADAM_B2 = 0.999
ADAM_EPS = 1e-08
ADAM_WD = 0.01
ADAM_STEP = 10

LANE = 128
ROW_TILE = 256
ATT_TQ = 512
ATT_TK = 256
LOG2E = 1.4426950408889634
LN2 = 0.6931471805599453
VMEM_CAP = 56 * 1024 * 1024
VMEM_FLOOR = 32 * 1024 * 1024

P_QLAT, P_QD, P_KD, P_VD, P_KVLAT, P_KPE = 0, 512, 1024, 1536, 2048, 2304
P_COLS = 2432
KPE_OFF = 64

NN = (((1,), (0,)), ((), ()))
NT = (((1,), (1,)), ((), ()))
TN = (((0,), (0,)), ((), ()))
HIGHEST = lax.Precision.HIGHEST
MESH = pl.DeviceIdType.MESH


def _params(sem=None, est_bytes=0):
    limit = int(min(max(2 * est_bytes + (4 << 20), VMEM_FLOOR), VMEM_CAP))
    if sem is None:
        return pltpu.CompilerParams(vmem_limit_bytes=limit)
    return pltpu.CompilerParams(dimension_semantics=sem, vmem_limit_bytes=limit)


def _nbytes(shape, dtype):
    return int(np.prod(shape)) * jnp.dtype(dtype).itemsize


def _mm(a, b, dims, out_dtype, tm, tn, name):
    if dims == "nn":
        (m, k), (k2, n) = a.shape, b.shape
        a_spec = pl.BlockSpec((tm, k), lambda i, j: (i, 0))
        b_spec = pl.BlockSpec((k, tn), lambda i, j: (0, j))
        dn = NN
    elif dims == "nt":
        (m, k), (n, k2) = a.shape, b.shape
        a_spec = pl.BlockSpec((tm, k), lambda i, j: (i, 0))
        b_spec = pl.BlockSpec((tn, k), lambda i, j: (j, 0))
        dn = NT
    else:
        (k, m), (k2, n) = a.shape, b.shape
        a_spec = pl.BlockSpec((k, tm), lambda i, j: (0, i))
        b_spec = pl.BlockSpec((k, tn), lambda i, j: (0, j))
        dn = TN
    assert k == k2 and m % tm == 0 and n % tn == 0, (name, a.shape, b.shape, tm, tn)

    def body(a_ref, b_ref, o_ref):
        o_ref[...] = lax.dot_general(a_ref[...], b_ref[...], dn, preferred_element_type=F32).astype(o_ref.dtype)

    est = _nbytes((tm, k), a.dtype) + _nbytes((tn, k), b.dtype) + _nbytes((tm, tn), F32) + _nbytes((tm, tn), out_dtype)
    return pl.pallas_call(
        body, name=name,
        grid=(m // tm, n // tn),
        in_specs=[a_spec, b_spec],
        out_specs=pl.BlockSpec((tm, tn), lambda i, j: (i, j)),
        out_shape=jax.ShapeDtypeStruct((m, n), out_dtype),
        compiler_params=_params(("parallel", "parallel"), est),
    )(a, b)


def _seg_consts():
    seg_q = np.zeros((HEADS * LANE, LANE), np.float32)
    inv_q = np.zeros((1, LANE), np.float32)
    seg_k = np.zeros((HEADS * LANE, LANE), np.float32)
    inv_k = np.zeros((1, LANE), np.float32)
    seg_d = np.zeros((DIL_W, LANE), np.float32)
    inv_d = np.zeros((1, LANE), np.float32)
    for h in range(HEADS):
        seg_q[h * LANE:h * LANE + NOPE, 2 * h] = 1.0
        seg_q[h * LANE + NOPE:h * LANE + NOPE + ROPE, 2 * h + 1] = 1.0
        inv_q[0, 2 * h], inv_q[0, 2 * h + 1] = 1.0 / NOPE, 1.0 / ROPE
        seg_k[h * LANE:h * LANE + NOPE, h] = 1.0
        inv_k[0, h] = 1.0 / NOPE
        seg_d[h * DIL_DIM:(h + 1) * DIL_DIM, h] = 1.0
        inv_d[0, h] = 1.0 / DIL_DIM
    fold_q = np.tile(np.eye(LANE, dtype=np.float32), (HEADS, 1))
    fold_d = np.zeros((DIL_W, LANE), np.float32)
    fold_d[np.arange(DIL_W), np.arange(DIL_W) % DIL_DIM] = 1.0
    j = lambda v: jnp.asarray(v)
    b = lambda v: jnp.asarray(v, dtype=BF16)
    return dict(seg_q=b(seg_q), exp_q=b(seg_q.T.copy()), inv_q=j(inv_q), seg_k=b(seg_k), exp_k=b(seg_k.T.copy()),
                inv_k=j(inv_k), seg_d=b(seg_d), exp_d=b(seg_d.T.copy()), inv_d=j(inv_d), fold_q=j(fold_q), fold_d=j(fold_d))


def _rope_consts():
    inv_d = jnp.power(ROPE_THETA, -2.0 * jnp.arange(DIL_DIM // 2, dtype=F32) / DIL_DIM)
    inv_q = jnp.power(ROPE_THETA, -2.0 * jnp.arange(ROPE // 2, dtype=F32) / ROPE)
    lanes = np.arange(LANE)
    freq_d = inv_d[lanes % (DIL_DIM // 2)]
    in_pe = (lanes >= KPE_OFF) & (lanes < KPE_OFF + ROPE)
    freq_q = jnp.where(jnp.asarray(in_pe), inv_q[(lanes - KPE_OFF) % (ROPE // 2)], 0.0)
    sign_d = np.where(lanes % DIL_DIM < DIL_DIM // 2, -1.0, 1.0).astype(np.float32)
    sign_q = np.where(in_pe, np.where((lanes - KPE_OFF) < ROPE // 2, -1.0, 1.0), 0.0).astype(np.float32)
    zeros, ones = np.zeros(LANE, np.float32), np.ones(LANE, np.float32)
    freq = jnp.concatenate([freq_d, freq_d, freq_q, freq_q])[None, :]
    csel = jnp.asarray(np.concatenate([ones, zeros, ones, zeros]))[None, :]
    ssel = jnp.asarray(np.concatenate([zeros, sign_d, zeros, sign_q]))[None, :]
    return freq, csel, ssel


def _full(shape):
    return pl.BlockSpec(shape, lambda *_: (0,) * len(shape))


def _tile_lanes(x, n):
    return jnp.concatenate([x] * n, axis=1)


def _rope_tables(pos_col, freq, csel, ssel):
    s = pos_col.shape[0]

    def body(p_ref, f_ref, c_ref, s_ref, o_ref):
        ang = p_ref[...].astype(F32) * f_ref[...]
        o_ref[...] = c_ref[...] * jnp.cos(ang) + s_ref[...] * jnp.sin(ang)

    return pl.pallas_call(
        body, name="rope_tables", grid=(s // ROW_TILE,),
        in_specs=[pl.BlockSpec((ROW_TILE, 1), lambda i: (i, 0)), _full((1, 4 * LANE)), _full((1, 4 * LANE)), _full((1, 4 * LANE))],
        out_specs=pl.BlockSpec((ROW_TILE, 4 * LANE), lambda i: (i, 0)),
        out_shape=jax.ShapeDtypeStruct((s, 4 * LANE), F32),
        compiler_params=_params(("parallel",)),
    )(pos_col, freq, csel, ssel)


def _rms(x):
    return lax.rsqrt(jnp.mean(x * x, axis=-1, keepdims=True) + EPS)


def _prenorm(x, gain, scale, shift, name):
    s, d = x.shape

    def body(x_ref, g_ref, sc_ref, sh_ref, h_ref):
        xv = x_ref[...]
        h = (xv * _rms(xv)) * g_ref[...] * (1.0 + sc_ref[...]) + sh_ref[...]
        h_ref[...] = h.astype(BF16)

    row = pl.BlockSpec((ROW_TILE, d), lambda i: (i, 0))
    return pl.pallas_call(
        body, name=name, grid=(s // ROW_TILE,),
        in_specs=[row, _full((1, d)), _full((1, d)), _full((1, d))],
        out_specs=row, out_shape=jax.ShapeDtypeStruct((s, d), BF16),
        compiler_params=_params(("parallel",)),
    )(x, gain, scale, shift)


def _latnorm(proj, g_q, g_kv):
    s = proj.shape[0]

    def body(q_ref, kv_ref, gq_ref, gkv_ref, ql_ref, kvl_ref):
        q, kv = q_ref[...], kv_ref[...]
        ql_ref[...] = ((q * _rms(q)) * gq_ref[...]).astype(BF16)
        kvl_ref[...] = ((kv * _rms(kv)) * gkv_ref[...]).astype(BF16)

    return pl.pallas_call(
        body, name="latnorm", grid=(s // ROW_TILE,),
        in_specs=[pl.BlockSpec((ROW_TILE, Q_LORA), lambda i: (i, P_QLAT // Q_LORA)),
                  pl.BlockSpec((ROW_TILE, KV_LORA), lambda i: (i, P_KVLAT // KV_LORA)),
                  _full((1, Q_LORA)), _full((1, KV_LORA))],
        out_specs=[pl.BlockSpec((ROW_TILE, Q_LORA), lambda i: (i, 0)), pl.BlockSpec((ROW_TILE, KV_LORA), lambda i: (i, 0))],
        out_shape=[jax.ShapeDtypeStruct((s, Q_LORA), BF16), jax.ShapeDtypeStruct((s, KV_LORA), BF16)],
        compiler_params=_params(("parallel",)),
    )(proj, proj, g_q, g_kv)


def _dot01(v, mat01):
    hi = v.astype(BF16)
    lo = (v - hi.astype(F32)).astype(BF16)
    return jnp.dot(hi, mat01, preferred_element_type=F32) + jnp.dot(lo, mat01, preferred_element_type=F32)


def _seg_rinv(x, seg, exp, inv):
    r = lax.rsqrt(_dot01(x * x, seg) * inv + EPS)
    return _dot01(r, exp)


def _seg_mean(v, seg, exp, inv):
    return _dot01(_dot01(v, seg) * inv, exp)


def _swap_halves(x, half):
    n = x.shape[1]
    lane = lax.broadcasted_iota(I32, (1, n), 1)
    first = (lane & (2 * half - 1)) < half
    return jnp.where(first, pltpu.roll(x, n - half, 1), pltpu.roll(x, half, 1))


def _rope(x, cos, sin_signed, half):
    return x * cos + _swap_halves(x, half) * sin_signed


def _rope_bwd(dy, cos, sin_signed, half):
    return dy * cos + _swap_halves(dy * sin_signed, half)


def _pe_lane_mask(n):
    lane = lax.broadcasted_iota(I32, (1, n), 1) & (LANE - 1)
    return (lane >= KPE_OFF) & (lane < KPE_OFF + ROPE)


def _attn_prep(q_raw, kv_raw, proj, tab, gains, consts):
    s = q_raw.shape[0]
    hw = HEADS * LANE

    def body(q_ref, kv_ref, kpe_ref, qd_ref, kd_ref, vd_ref, tab_ref,
             gq_ref, gk_ref, gkpe_ref, gdq_ref, gdk_ref,
             segq_ref, expq_ref, invq_ref, segk_ref, expk_ref, invk_ref, segd_ref, expd_ref, invd_ref,
             qm_ref, km_ref, vm_ref, qdo_ref, kdo_ref, vdo_ref):
        tab_v = tab_ref[...]
        cos_d, sin_d = _tile_lanes(tab_v[:, 0:LANE], DIL_W // LANE), _tile_lanes(tab_v[:, LANE:2 * LANE], DIL_W // LANE)
        cos_q1, sin_q1 = tab_v[:, 2 * LANE:3 * LANE], tab_v[:, 3 * LANE:4 * LANE]
        cos_q, sin_q = _tile_lanes(cos_q1, HEADS), _tile_lanes(sin_q1, HEADS)

        q = q_ref[...]
        qn = q * _seg_rinv(q, segq_ref[...], expq_ref[...], invq_ref[...]) * gq_ref[...]
        qm_ref[...] = _rope(qn, cos_q, sin_q, ROPE // 2).astype(BF16)

        kv = kv_ref[...]
        kp = kv[:, :hw]
        kn = kp * _seg_rinv(kp, segk_ref[...], expk_ref[...], invk_ref[...]) * gk_ref[...]
        kpe = kpe_ref[...]
        r_pe = lax.rsqrt(jnp.sum(kpe * kpe, axis=-1, keepdims=True) * (1.0 / ROPE) + EPS)
        kpe_r = _rope(kpe * r_pe * gkpe_ref[...], cos_q1, sin_q1, ROPE // 2)
        km_ref[...] = (kn + _tile_lanes(kpe_r, HEADS)).astype(BF16)
        vm_ref[...] = kv[:, hw:].astype(BF16)

        qd = qd_ref[...]
        qdn = qd * _seg_rinv(qd, segd_ref[...], expd_ref[...], invd_ref[...]) * gdq_ref[...]
        qdo_ref[...] = _rope(qdn, cos_d, sin_d, DIL_DIM // 2).astype(BF16)
        kd = kd_ref[...]
        kdn = kd * _seg_rinv(kd, segd_ref[...], expd_ref[...], invd_ref[...]) * gdk_ref[...]
        kdo_ref[...] = _rope(kdn, cos_d, sin_d, DIL_DIM // 2).astype(BF16)
        vdo_ref[...] = vd_ref[...].astype(BF16)

    t = ROW_TILE
    row = lambda w, cb=0: pl.BlockSpec((t, w), lambda i: (i, cb))
    c = consts
    return pl.pallas_call(
        body, name="attn_prep", grid=(s // t,),
        in_specs=[row(hw), row(hw + DIL_W), row(LANE, P_KPE // LANE), row(DIL_W, P_QD // DIL_W), row(DIL_W, P_KD // DIL_W),
                  row(DIL_W, P_VD // DIL_W), row(4 * LANE),
                  _full((1, hw)), _full((1, hw)), _full((1, LANE)), _full((1, DIL_W)), _full((1, DIL_W)),
                  _full((hw, LANE)), _full((LANE, hw)), _full((1, LANE)), _full((hw, LANE)), _full((LANE, hw)), _full((1, LANE)),
                  _full((DIL_W, LANE)), _full((LANE, DIL_W)), _full((1, LANE))],
        out_specs=[row(hw), row(hw), row(DIL_W), row(DIL_W), row(DIL_W), row(DIL_W)],
        out_shape=[jax.ShapeDtypeStruct((s, hw), BF16), jax.ShapeDtypeStruct((s, hw), BF16)]
        + [jax.ShapeDtypeStruct((s, DIL_W), BF16)] * 4,
        compiler_params=_params(("parallel",), 24 << 20),
    )(q_raw, kv_raw, proj, proj, proj, proj, tab, gains["q"], gains["k"], gains["kpe"], gains["dq"], gains["dk"],
      c["seg_q"], c["exp_q"], c["inv_q"], c["seg_k"], c["exp_k"], c["inv_k"], c["seg_d"], c["exp_d"], c["inv_d"])


def _attn_prep_bwd(dqm, dkm, dvm, dqd, dkd, dvd, q_raw, kv_raw, proj, tab, gains, consts):
    s = q_raw.shape[0]
    hw = HEADS * LANE
    n_steps = s // ROW_TILE

    def body(dqm_ref, dkm_ref, dvm_ref, dqd_ref, dkd_ref, dvd_ref, q_ref, kv_ref, kpe_ref, qd_ref, kd_ref, tab_ref,
             gq_ref, gk_ref, gkpe_ref, gdq_ref, gdk_ref,
             segq_ref, expq_ref, invq_ref, segk_ref, expk_ref, invk_ref, segd_ref, expd_ref, invd_ref, foldq_ref, foldd_ref,
             dq_ref, dkv_ref, dkpe_ref, dqdo_ref, dkdo_ref, dvdo_ref, dg_ref, acc_ref):
        i = pl.program_id(0)

        @pl.when(i == 0)
        def _():
            acc_ref[...] = jnp.zeros_like(acc_ref)

        tab_v = tab_ref[...]
        cos_d, sin_d = _tile_lanes(tab_v[:, 0:LANE], DIL_W // LANE), _tile_lanes(tab_v[:, LANE:2 * LANE], DIL_W // LANE)
        cos_q1, sin_q1 = tab_v[:, 2 * LANE:3 * LANE], tab_v[:, 3 * LANE:4 * LANE]
        cos_q, sin_q = _tile_lanes(cos_q1, HEADS), _tile_lanes(sin_q1, HEADS)

        def norm_bwd(x, dyg, gain, seg, exp, inv):
            rinv = _seg_rinv(x, seg, exp, inv)
            xn = x * rinv
            dxn = dyg * gain
            dx = rinv * (dxn - xn * _seg_mean(dxn * xn, seg, exp, inv))
            return dx, jnp.sum(dyg * xn, axis=0, keepdims=True)

        dq, gq_l = norm_bwd(q_ref[...], _rope_bwd(dqm_ref[...], cos_q, sin_q, ROPE // 2), gq_ref[...],
                            segq_ref[...], expq_ref[...], invq_ref[...])
        dq_ref[...] = dq.astype(BF16)

        dkm = dkm_ref[...]
        kv = kv_ref[...]
        dkp, gk_l = norm_bwd(kv[:, :hw], dkm, gk_ref[...], segk_ref[...], expk_ref[...], invk_ref[...])
        dkv_ref[:, :hw] = dkp.astype(BF16)
        dkv_ref[:, hw:] = dvm_ref[...].astype(BF16)

        dkpe_r = dkm[:, 0:LANE]
        for h in range(1, HEADS):
            dkpe_r = dkpe_r + dkm[:, h * LANE:(h + 1) * LANE]
        dkpe_r = jnp.where(_pe_lane_mask(LANE), dkpe_r, 0.0)
        dyg = _rope_bwd(dkpe_r, cos_q1, sin_q1, ROPE // 2)
        kpe = kpe_ref[...]
        r_pe = lax.rsqrt(jnp.sum(kpe * kpe, axis=-1, keepdims=True) * (1.0 / ROPE) + EPS)
        xn = kpe * r_pe
        dxn = dyg * gkpe_ref[...]
        dkpe = r_pe * (dxn - xn * (jnp.sum(dxn * xn, axis=-1, keepdims=True) * (1.0 / ROPE)))
        dkpe_ref[...] = dkpe.astype(BF16)
        gkpe_l = jnp.sum(dyg * xn, axis=0, keepdims=True)

        dqd_v, gdq_l = norm_bwd(qd_ref[...], _rope_bwd(dqd_ref[...], cos_d, sin_d, DIL_DIM // 2), gdq_ref[...],
                                segd_ref[...], expd_ref[...], invd_ref[...])
        dqdo_ref[...] = dqd_v.astype(BF16)
        dkd_v, gdk_l = norm_bwd(kd_ref[...], _rope_bwd(dkd_ref[...], cos_d, sin_d, DIL_DIM // 2), gdk_ref[...],
                                segd_ref[...], expd_ref[...], invd_ref[...])
        dkdo_ref[...] = dkd_v.astype(BF16)
        dvdo_ref[...] = dvd_ref[...].astype(BF16)

        acc_ref[0:1, :] += gq_l
        acc_ref[1:2, :] += gk_l
        acc_ref[2:3, 0:LANE] += gkpe_l
        acc_ref[3:4, 0:DIL_W] += gdq_l
        acc_ref[4:5, 0:DIL_W] += gdk_l

        @pl.when(i == n_steps - 1)
        def _():
            acc = acc_ref[...]
            fq = jnp.dot(acc, foldq_ref[...], precision=HIGHEST, preferred_element_type=F32)
            fd = jnp.dot(acc[:, 0:DIL_W], foldd_ref[...], precision=HIGHEST, preferred_element_type=F32)
            rows = lax.broadcasted_iota(I32, (8, LANE), 0)
            base = jnp.where(rows < 2, fq, jnp.where(rows == 2, acc[:, 0:LANE], fd))
            at0 = pltpu.roll(base, LANE - KPE_OFF, 1)
            dg_ref[...] = jnp.where(rows == 5, pltpu.roll(at0, 5, 0), jnp.where(rows == 2, at0, base))

    t = ROW_TILE
    row = lambda w, cb=0: pl.BlockSpec((t, w), lambda i: (i, cb))
    c = consts
    return pl.pallas_call(
        body, name="attn_prep_bwd", grid=(n_steps,),
        in_specs=[row(hw), row(hw), row(DIL_W), row(DIL_W), row(DIL_W), row(DIL_W),
                  row(hw), row(hw + DIL_W), row(LANE, P_KPE // LANE), row(DIL_W, P_QD // DIL_W), row(DIL_W, P_KD // DIL_W),
                  row(4 * LANE),
                  _full((1, hw)), _full((1, hw)), _full((1, LANE)), _full((1, DIL_W)), _full((1, DIL_W)),
                  _full((hw, LANE)), _full((LANE, hw)), _full((1, LANE)), _full((hw, LANE)), _full((LANE, hw)), _full((1, LANE)),
                  _full((DIL_W, LANE)), _full((LANE, DIL_W)), _full((1, LANE)), _full((hw, LANE)), _full((DIL_W, LANE))],
        out_specs=[row(hw), row(hw + DIL_W), row(LANE), row(DIL_W), row(DIL_W), row(DIL_W), _full((8, LANE))],
        out_shape=[jax.ShapeDtypeStruct((s, hw), BF16), jax.ShapeDtypeStruct((s, hw + DIL_W), BF16),
                   jax.ShapeDtypeStruct((s, LANE), BF16)] + [jax.ShapeDtypeStruct((s, DIL_W), BF16)] * 3
        + [jax.ShapeDtypeStruct((8, LANE), F32)],
        scratch_shapes=[pltpu.VMEM((8, hw), F32)],
        compiler_params=_params(("arbitrary",), 28 << 20),
    )(dqm, dkm, dvm, dqd, dkd, dvd, q_raw, kv_raw, proj, proj, proj, tab,
      gains["q"], gains["k"], gains["kpe"], gains["dq"], gains["dk"],
      c["seg_q"], c["exp_q"], c["inv_q"], c["seg_k"], c["exp_k"], c["inv_k"], c["seg_d"], c["exp_d"], c["inv_d"],
      c["fold_q"], c["fold_d"])


def _latnorm_bwd(dql, dkvl, proj, g_q, g_kv):
    s = proj.shape[0]
    n_steps = s // ROW_TILE

    def body(dql_ref, dkvl_ref, q_ref, kv_ref, gq_ref, gkv_ref, dq_ref, dkv_ref, dg_ref):
        i = pl.program_id(0)

        @pl.when(i == 0)
        def _():
            dg_ref[...] = jnp.zeros_like(dg_ref)

        def one(x, dyg, gain):
            r = _rms(x)
            xn = x * r
            dxn = dyg * gain
            dx = r * (dxn - xn * jnp.mean(dxn * xn, axis=-1, keepdims=True))
            return dx, jnp.sum(dyg * xn, axis=0, keepdims=True)

        dq, gq_l = one(q_ref[...], dql_ref[...], gq_ref[...])
        dkv, gkv_l = one(kv_ref[...], dkvl_ref[...], gkv_ref[...])
        dq_ref[...] = dq.astype(BF16)
        dkv_ref[...] = dkv.astype(BF16)
        dg_ref[0:1, :] += gq_l
        dg_ref[1:2, 0:KV_LORA] += gkv_l

    t = ROW_TILE
    return pl.pallas_call(
        body, name="latnorm_bwd", grid=(n_steps,),
        in_specs=[pl.BlockSpec((t, Q_LORA), lambda i: (i, 0)), pl.BlockSpec((t, KV_LORA), lambda i: (i, 0)),
                  pl.BlockSpec((t, Q_LORA), lambda i: (i, P_QLAT // Q_LORA)),
                  pl.BlockSpec((t, KV_LORA), lambda i: (i, P_KVLAT // KV_LORA)),
                  _full((1, Q_LORA)), _full((1, KV_LORA))],
        out_specs=[pl.BlockSpec((t, Q_LORA), lambda i: (i, 0)), pl.BlockSpec((t, KV_LORA), lambda i: (i, 0)), _full((8, Q_LORA))],
        out_shape=[jax.ShapeDtypeStruct((s, Q_LORA), BF16), jax.ShapeDtypeStruct((s, KV_LORA), BF16),
                   jax.ShapeDtypeStruct((8, Q_LORA), F32)],
        compiler_params=_params(("arbitrary",)),
    )(dql, dkvl, proj, proj, g_q, g_kv)


def _resid_prenorm(x, mix, g1, gain, scale, shift):
    s, d = x.shape

    def body(x_ref, mix_ref, g1_ref, g_ref, sc_ref, sh_ref, x1_ref, h_ref):
        x1 = x_ref[...] + g1_ref[...] * mix_ref[...]
        x1_ref[...] = x1
        h_ref[...] = ((x1 * _rms(x1)) * g_ref[...] * (1.0 + sc_ref[...]) + sh_ref[...]).astype(BF16)

    row = pl.BlockSpec((ROW_TILE, d), lambda i: (i, 0))
    vec = _full((1, d))
    return pl.pallas_call(
        body, name="resid_prenorm", grid=(s // ROW_TILE,),
        in_specs=[row, row, vec, vec, vec, vec], out_specs=[row, row],
        out_shape=[jax.ShapeDtypeStruct((s, d), F32), jax.ShapeDtypeStruct((s, d), BF16)],
        compiler_params=_params(("parallel",)),
    )(x, mix, g1, gain, scale, shift)


CONV_TILE = 1408
HALO = 8


def _shift_down(x, halo, k):
    t = x.shape[0]
    row = lax.broadcasted_iota(I32, (t, 1), 0)
    out = pltpu.roll(x, k, 0)
    for r in range(k):
        out = jnp.where(row == r, halo[HALO - k + r:HALO - k + r + 1, :], out)
    return out


def _shift_up(x, halo, k):
    t = x.shape[0]
    row = lax.broadcasted_iota(I32, (t, 1), 0)
    out = pltpu.roll(x, t - k, 0)
    for r in range(k):
        out = jnp.where(row == t - k + r, halo[r:r + 1, :], out)
    return out


def _conv_fwd(x, halo, w, b):
    p1, p2 = _shift_down(x, halo, 1), _shift_down(x, halo, 2)
    u = b + p2 * w[0:1, :]
    u = u + p1 * w[1:2, :]
    u = u + x * w[2:3, :]
    return u, p1, p2


def _sigmoid(x):
    return 1.0 / (1.0 + jnp.exp(-x))


def _conv_gate(up, w_conv, b_conv):
    s = up.shape[0]
    t = ROW_TILE
    nj = D_FF // CONV_TILE
    hb = t // HALO

    def body(g_ref, v_ref, gh_ref, vh_ref, wg_ref, wv_ref, bg_ref, bv_ref, a_ref):
        live = (pl.program_id(0) > 0).astype(F32)
        ug, _, _ = _conv_fwd(g_ref[...], gh_ref[...] * live, wg_ref[...], bg_ref[...])
        uv, _, _ = _conv_fwd(v_ref[...], vh_ref[...] * live, wv_ref[...], bv_ref[...])
        a_ref[...] = (ug * _sigmoid(ug) * uv).astype(BF16)

    main = lambda off: pl.BlockSpec((t, CONV_TILE), lambda i, j: (i, j + off))
    halo = lambda off: pl.BlockSpec((HALO, CONV_TILE), lambda i, j: (jnp.maximum(i * hb - 1, 0), j + off))
    wsp = lambda off: pl.BlockSpec((3, CONV_TILE), lambda i, j: (0, j + off))
    bsp = lambda off: pl.BlockSpec((1, CONV_TILE), lambda i, j: (0, j + off))
    return pl.pallas_call(
        body, name="conv_gate", grid=(s // t, nj),
        in_specs=[main(0), main(nj), halo(0), halo(nj), wsp(0), wsp(nj), bsp(0), bsp(nj)],
        out_specs=pl.BlockSpec((t, CONV_TILE), lambda i, j: (i, j)),
        out_shape=jax.ShapeDtypeStruct((s, D_FF), BF16),
        compiler_params=_params(("parallel", "parallel"), 12 << 20),
    )(up, up, up, up, w_conv, w_conv, b_conv, b_conv)


def _gate_bwd(up, da, w_conv, b_conv):
    s = up.shape[0]
    t = ROW_TILE
    nj = D_FF // CONV_TILE
    hb = t // HALO

    def body(g_ref, v_ref, gh_ref, vh_ref, da_ref, wg_ref, wv_ref, bg_ref, bv_ref,
             dug_ref, duv_ref, dbg_ref, dbv_ref, dwg_ref, dwv_ref):
        i = pl.program_id(1)

        @pl.when(i == 0)
        def _():
            for r in (dbg_ref, dbv_ref, dwg_ref, dwv_ref):
                r[...] = jnp.zeros_like(r)

        live = (i > 0).astype(F32)
        xg, xv = g_ref[...], v_ref[...]
        ug, g1, g2 = _conv_fwd(xg, gh_ref[...] * live, wg_ref[...], bg_ref[...])
        uv, v1, v2 = _conv_fwd(xv, vh_ref[...] * live, wv_ref[...], bv_ref[...])
        sg = _sigmoid(ug)
        da_v = da_ref[...]
        dug = da_v * uv * (sg * (1.0 + ug * (1.0 - sg)))
        duv = da_v * (ug * sg)
        dug_ref[...] = dug
        duv_ref[...] = duv
        csum = lambda z: jnp.sum(z, axis=0, keepdims=True)
        dbg_ref[...] += csum(dug)
        dbv_ref[...] += csum(duv)
        dwg_ref[0:1, :] += csum(dug * g2)
        dwg_ref[1:2, :] += csum(dug * g1)
        dwg_ref[2:3, :] += csum(dug * xg)
        dwv_ref[0:1, :] += csum(duv * v2)
        dwv_ref[1:2, :] += csum(duv * v1)
        dwv_ref[2:3, :] += csum(duv * xv)

    main = lambda off: pl.BlockSpec((t, CONV_TILE), lambda j, i: (i, j + off))
    halo = lambda off: pl.BlockSpec((HALO, CONV_TILE), lambda j, i: (jnp.maximum(i * hb - 1, 0), j + off))
    wsp = lambda off: pl.BlockSpec((3, CONV_TILE), lambda j, i: (0, j + off))
    bsp = lambda off: pl.BlockSpec((1, CONV_TILE), lambda j, i: (0, j + off))
    outs = pl.pallas_call(
        body, name="gate_bwd", grid=(nj, s // t),
        in_specs=[main(0), main(nj), halo(0), halo(nj), pl.BlockSpec((t, CONV_TILE), lambda j, i: (i, j)),
                  wsp(0), wsp(nj), bsp(0), bsp(nj)],
        out_specs=[pl.BlockSpec((t, CONV_TILE), lambda j, i: (i, j)), pl.BlockSpec((t, CONV_TILE), lambda j, i: (i, j)),
                   pl.BlockSpec((1, CONV_TILE), lambda j, i: (0, j)), pl.BlockSpec((1, CONV_TILE), lambda j, i: (0, j)),
                   pl.BlockSpec((3, CONV_TILE), lambda j, i: (0, j)), pl.BlockSpec((3, CONV_TILE), lambda j, i: (0, j))],
        out_shape=[jax.ShapeDtypeStruct((s, D_FF), F32), jax.ShapeDtypeStruct((s, D_FF), F32),
                   jax.ShapeDtypeStruct((1, D_FF), F32), jax.ShapeDtypeStruct((1, D_FF), F32),
                   jax.ShapeDtypeStruct((3, D_FF), F32), jax.ShapeDtypeStruct((3, D_FF), F32)],
        compiler_params=_params(("parallel", "arbitrary"), 20 << 20),
    )(up, up, up, up, da, w_conv, w_conv, b_conv, b_conv)
    return outs


def _conv_bwd(du, w_half, name):
    s = du.shape[0]
    t = ROW_TILE
    nj = D_FF // CONV_TILE
    hb = t // HALO
    n_i = s // t

    def body(d_ref, h_ref, w_ref, o_ref):
        live = (pl.program_id(0) < n_i - 1).astype(F32)
        x = d_ref[...]
        halo = h_ref[...] * live
        w = w_ref[...]
        o = x * w[2:3, :] + _shift_up(x, halo, 1) * w[1:2, :] + _shift_up(x, halo, 2) * w[0:1, :]
        o_ref[...] = o.astype(BF16)

    return pl.pallas_call(
        body, name=name, grid=(n_i, nj),
        in_specs=[pl.BlockSpec((t, CONV_TILE), lambda i, j: (i, j)),
                  pl.BlockSpec((HALO, CONV_TILE), lambda i, j: (jnp.minimum((i + 1) * hb, s // HALO - 1), j)),
                  pl.BlockSpec((3, CONV_TILE), lambda i, j: (0, j))],
        out_specs=pl.BlockSpec((t, CONV_TILE), lambda i, j: (i, j)),
        out_shape=jax.ShapeDtypeStruct((s, D_FF), BF16),
        compiler_params=_params(("parallel", "parallel"), 8 << 20),
    )(du, du, w_half)


def _final(x1, ffn, tgt, g2):
    s, d = x1.shape
    n_steps = s // ROW_TILE

    def body(x1_ref, f_ref, t_ref, g2_ref, dy_ref, df_ref, dg2_ref, loss_ref, lacc_ref):
        i = pl.program_id(0)

        @pl.when(i == 0)
        def _():
            dg2_ref[...] = jnp.zeros_like(dg2_ref)
            lacc_ref[...] = jnp.zeros_like(lacc_ref)

        f = f_ref[...]
        e = x1_ref[...] + g2_ref[...] * f - t_ref[...]
        dy = e * (1.0 / d)
        dy_ref[...] = dy
        df_ref[...] = (dy * g2_ref[...]).astype(BF16)
        dg2_ref[...] += jnp.sum(dy * f, axis=0, keepdims=True)
        lacc_ref[...] += jnp.sum(e * e, axis=0, keepdims=True)

        @pl.when(i == n_steps - 1)
        def _():
            loss_ref[...] = jnp.sum(lacc_ref[...], axis=1, keepdims=True) * (0.5 / d)

    row = pl.BlockSpec((ROW_TILE, d), lambda i: (i, 0))
    return pl.pallas_call(
        body, name="final", grid=(n_steps,),
        in_specs=[row, row, row, _full((1, d))],
        out_specs=[row, row, _full((1, d)), _full((1, 1))],
        out_shape=[jax.ShapeDtypeStruct((s, d), F32), jax.ShapeDtypeStruct((s, d), BF16),
                   jax.ShapeDtypeStruct((1, d), F32), jax.ShapeDtypeStruct((1, 1), F32)],
        scratch_shapes=[pltpu.VMEM((1, d), F32)],
        compiler_params=_params(("arbitrary",)),
    )(x1, ffn, tgt, g2)


def _ffnnorm_bwd(dh2, x1, dy, mix, gain, scale, g1):
    s, d = x1.shape
    n_steps = s // ROW_TILE

    def body(dh_ref, x_ref, dy_ref, mix_ref, g_ref, sc_ref, g1_ref, dx_ref, dm_ref, acc_ref):
        i = pl.program_id(0)

        @pl.when(i == 0)
        def _():
            acc_ref[...] = jnp.zeros_like(acc_ref)

        dh, x = dh_ref[...], x_ref[...]
        r = _rms(x)
        xn = x * r
        dn = dh * (1.0 + sc_ref[...])
        dxn = dn * g_ref[...]
        dx = dy_ref[...] + r * (dxn - xn * jnp.mean(dxn * xn, axis=-1, keepdims=True))
        dx_ref[...] = dx
        dm_ref[...] = (dx * g1_ref[...]).astype(BF16)
        csum = lambda z: jnp.sum(z, axis=0, keepdims=True)
        acc_ref[0:1, :] += csum(dh)
        acc_ref[1:2, :] += csum(dh * (xn * g_ref[...]))
        acc_ref[2:3, :] += csum(dn * xn)
        acc_ref[3:4, :] += csum(dx * mix_ref[...])

    row = pl.BlockSpec((ROW_TILE, d), lambda i: (i, 0))
    vec = _full((1, d))
    return pl.pallas_call(
        body, name="ffnnorm_bwd", grid=(n_steps,),
        in_specs=[row, row, row, row, vec, vec, vec],
        out_specs=[row, row, _full((8, d))],
        out_shape=[jax.ShapeDtypeStruct((s, d), F32), jax.ShapeDtypeStruct((s, d), BF16), jax.ShapeDtypeStruct((8, d), F32)],
        compiler_params=_params(("arbitrary",)),
    )(dh2, x1, dy, mix, gain, scale, g1)


def _mixnorm_bwd(dh, x, dx1, gain, scale):
    s, d = x.shape
    n_steps = s // ROW_TILE

    def body(dh_ref, x_ref, dx1_ref, g_ref, sc_ref, gx_ref, acc_ref):
        i = pl.program_id(0)

        @pl.when(i == 0)
        def _():
            acc_ref[...] = jnp.zeros_like(acc_ref)

        dh, x = dh_ref[...], x_ref[...]
        r = _rms(x)
        xn = x * r
        dn = dh * (1.0 + sc_ref[...])
        dxn = dn * g_ref[...]
        gx_ref[...] = dx1_ref[...] + r * (dxn - xn * jnp.mean(dxn * xn, axis=-1, keepdims=True))
        csum = lambda z: jnp.sum(z, axis=0, keepdims=True)
        acc_ref[0:1, :] += csum(dh)
        acc_ref[1:2, :] += csum(dh * (xn * g_ref[...]))
        acc_ref[2:3, :] += csum(dn * xn)

    row = pl.BlockSpec((ROW_TILE, d), lambda i: (i, 0))
    vec = _full((1, d))
    return pl.pallas_call(
        body, name="mixnorm_bwd", grid=(n_steps,),
        in_specs=[row, row, row, vec, vec],
        out_specs=[row, _full((8, d))],
        out_shape=[jax.ShapeDtypeStruct((s, d), F32), jax.ShapeDtypeStruct((8, d), F32)],
        compiler_params=_params(("arbitrary",)),
    )(dh, x, dx1, gain, scale)


def _key_count(d, dilated):
    if not dilated:
        return jnp.where(d >= 0, 1.0, 0.0)
    one = lambda cond: jnp.where(cond, 1.0, 0.0)
    cnt = one(d <= 128) + one(((d & 3) == 0) & (d <= 512)) + one((d & 15) == 0)
    return jnp.where(d >= 0, cnt, 0.0)


def _block_kinds(mla):
    return (0, "diag", "none") if mla else (512, "near", "far")


def _scores_t(ka, qa, scale, kind, rel_t, offset):
    st = lax.dot_general(ka, qa, NT, preferred_element_type=F32) * (scale * LOG2E)
    cnt = None
    if kind == "diag":
        st = jnp.where(rel_t + offset >= 0, st, NEG_INF)
    elif kind == "far":
        st = jnp.where((rel_t & 15) == 0, st, NEG_INF)
    elif kind == "near":
        cnt = _key_count(rel_t + offset, True)
        st = jnp.where(cnt > 0.0, st, NEG_INF)
    return st, cnt


def _attn_fwd(q, k, v, mla, scale, name, gather=()):
    s = q.shape[0]
    qw = 2 * LANE if mla else LANE
    tq, tk = ATT_TQ, ATT_TK
    reach, kind_near, kind_far = _block_kinds(mla)
    assert s % tq == 0 and tq % tk == 0 and reach % tk == 0
    ng = len(gather)
    last_step = HEADS // 2 - 1

    def body(*refs):
        q_ref, k_ref, v_ref = refs[:3]
        o_ref, lse_ref = refs[3 + ng:5 + ng]
        vt_ref = refs[5 + 2 * ng]
        comm = (refs[3:3 + ng], refs[5 + ng:5 + 2 * ng]) + tuple(refs[6 + 2 * ng:])
        if ng:
            @pl.when(pl.program_id(0) == 0)
            def _():
                _Gather(*comm).start()

            @pl.when(pl.program_id(0) == last_step)
            def _():
                _Gather(*comm).forward()

        lane = lax.broadcasted_iota(I32, (1, LANE), 1)
        rel_t = lax.broadcasted_iota(I32, (tk, tq), 1) - lax.broadcasted_iota(I32, (tk, tq), 0)

        def transpose_v(j, carry):
            c0 = pl.multiple_of(j * tk, tk)
            vt_ref[:, pl.ds(c0, tk)] = v_ref[pl.ds(c0, tk), :].astype(F32).T.astype(BF16)
            return carry

        lax.fori_loop(0, s // tk, transpose_v, 0)

        def q_block(qi, carry):
            r0 = pl.multiple_of(qi * tq, tq)
            kcols = [slice(a * LANE, (a + 1) * LANE) if mla else slice(0, LANE) for a in range(2)]
            qas = [q_ref[pl.ds(r0, tq), kcols[a]] for a in range(2)]
            if not mla:
                qas = [jnp.where(lane < DIL_DIM, qas[0], jnp.zeros_like(qas[0])),
                       jnp.where(lane >= DIL_DIM, qas[1], jnp.zeros_like(qas[1]))]

            def k_block(kj, c, kind):
                c0 = pl.multiple_of(kj * tk, tk)
                out = []
                for a in range(2):
                    m, l, acc = c[a]
                    st, cnt = _scores_t(k_ref[pl.ds(c0, tk), kcols[a]], qas[a], scale, kind, rel_t, r0 - c0)
                    m_new = jnp.maximum(m, jnp.max(st, axis=0, keepdims=True))
                    alpha = jnp.exp2(m - m_new)
                    p = jnp.exp2(st - m_new)
                    if cnt is not None:
                        p = p * cnt
                    l = alpha * l + jnp.sum(p, axis=0, keepdims=True)
                    vt = vt_ref[a * DIL_DIM:(a + 1) * DIL_DIM, pl.ds(c0, tk)]
                    acc = alpha * acc + jnp.dot(vt, p.astype(BF16), preferred_element_type=F32)
                    out.append((m_new, l, acc))
                return tuple(out)

            one = (jnp.full((1, tq), NEG_INF, F32), jnp.zeros((1, tq), F32), jnp.zeros((DIL_DIM, tq), F32))
            first_near = jnp.maximum((r0 - reach) // tk, 0)
            c = lax.fori_loop(0, first_near, functools.partial(k_block, kind=kind_far), (one, one))
            res = lax.fori_loop(first_near, (r0 + tq) // tk, functools.partial(k_block, kind=kind_near), c)
            o_t = jnp.concatenate([res[a][2] / res[a][1] for a in range(2)], axis=0)
            o_ref[pl.ds(r0, tq), :] = o_t.T.astype(BF16)
            for a in range(2):
                lse_ref[a, :, pl.ds(r0, tq)] = res[a][0] * LN2 + jnp.log(res[a][1])
            return carry

        lax.fori_loop(0, s // tq, q_block, 0)

        if ng:
            @pl.when(pl.program_id(0) == last_step)
            def _():
                _Gather(*comm).finish()

    return pl.pallas_call(
        body, name=name, grid=(HEADS // 2,),
        in_specs=[pl.BlockSpec((s, qw), lambda h: (0, h)), pl.BlockSpec((s, qw), lambda h: (0, h)),
                  pl.BlockSpec((s, LANE), lambda h: (0, h))] + [ANY] * ng,
        out_specs=[pl.BlockSpec((s, LANE), lambda h: (0, h)), pl.BlockSpec((2, 1, s), lambda h: (h, 0, 0))] + [ANY] * ng,
        out_shape=[jax.ShapeDtypeStruct((s, DIL_W), BF16), jax.ShapeDtypeStruct((HEADS, 1, s), F32)] + _Gather.out_shapes(gather),
        scratch_shapes=[pltpu.VMEM((LANE, s), BF16)] + (_Gather.semaphores(ng) if ng else []),
        compiler_params=_params(("arbitrary",) if ng else ("parallel",), 12 << 20),
    )(q, k, v, *gather)


def _attn_bwd(q, k, v, o, do, do_block0, lse, mla, scale, name, scatter=()):
    s = q.shape[0]
    qw = 2 * LANE if mla else LANE
    tq, tk = ATT_TQ, ATT_TK
    nq = s // tq
    reach, kind_near, kind_far = _block_kinds(mla)
    assert s % tq == 0 and tq % tk == 0
    ns = len(scatter)
    last_step = HEADS // 2 - 1

    def body(*refs):
        q_ref, k_ref, v_ref, o_ref, do_ref, lse_ref = refs[:6]
        dq_ref, dk_ref, dv_ref = refs[6 + ns:9 + ns]
        kt_ref, dot_ref, dob_ref, dqt_ref, delta_ref, lse2_ref = refs[9 + 2 * ns:15 + 2 * ns]
        comm = (refs[6:6 + ns], refs[9 + ns:9 + 2 * ns]) + tuple(refs[15 + 2 * ns:])
        if ns:
            @pl.when(pl.program_id(0) == 0)
            def _():
                _Scatter(*comm).start()

        lane = lax.broadcasted_iota(I32, (1, LANE), 1)
        row = lax.broadcasted_iota(I32, (LANE, 1), 0)
        rel_t = lax.broadcasted_iota(I32, (tk, tq), 1) - lax.broadcasted_iota(I32, (tk, tq), 0)

        def prepare(j, carry):
            c0 = pl.multiple_of(j * tk, tk)
            do_blk = do_ref[pl.ds(c0, tk), :]
            dob_ref[pl.ds(c0, tk), :] = do_blk.astype(BF16)
            do_t = do_blk.T
            dot_ref[:, pl.ds(c0, tk)] = do_t.astype(BF16)
            prod = do_t * o_ref[pl.ds(c0, tk), :].astype(F32).T
            delta_ref[0, :, pl.ds(c0, tk)] = jnp.sum(prod[0:DIL_DIM], axis=0, keepdims=True)
            delta_ref[1, :, pl.ds(c0, tk)] = jnp.sum(prod[DIL_DIM:LANE], axis=0, keepdims=True)
            for w in range(qw // LANE):
                kt_ref[w * LANE:(w + 1) * LANE, pl.ds(c0, tk)] = (
                    k_ref[pl.ds(c0, tk), w * LANE:(w + 1) * LANE].astype(F32).T.astype(BF16))
            return carry

        lax.fori_loop(0, s // tk, prepare, 0)
        dqt_ref[...] = jnp.zeros_like(dqt_ref)
        lse2_ref[...] = lse_ref[...] * LOG2E

        sels = [lane < DIL_DIM, lane >= DIL_DIM]
        rsels = [row < DIL_DIM, row >= DIL_DIM]
        cols = [slice(a * LANE, (a + 1) * LANE) if mla else slice(0, LANE) for a in range(2)]

        def k_block(kj, carry):
            c0 = pl.multiple_of(kj * tk, tk)
            kas = [k_ref[pl.ds(c0, tk), cols[a]] for a in range(2)]
            kts = [kt_ref[cols[a], pl.ds(c0, tk)] for a in range(2)]
            if not mla:
                kas = [jnp.where(sels[a], kas[a], jnp.zeros_like(kas[a])) for a in range(2)]
                kts = [jnp.where(rsels[a], kts[a], jnp.zeros_like(kts[a])) for a in range(2)]
            vb = v_ref[pl.ds(c0, tk), :]
            vbs = [jnp.where(sels[a], vb, jnp.zeros_like(vb)) for a in range(2)]

            def q_block(qi, c, kind):
                r0 = pl.multiple_of(qi * tq, tq)
                out, dq_parts = [], []
                for a in range(2):
                    dk_acc, dv_acc = c[a]
                    qa = q_ref[pl.ds(r0, tq), cols[a]]
                    st, cnt = _scores_t(kas[a], qa, scale, kind, rel_t, r0 - c0)
                    p = jnp.exp2(st - lse2_ref[a, :, pl.ds(r0, tq)])
                    if cnt is not None:
                        p = p * cnt
                    dp = jnp.dot(vbs[a], dot_ref[:, pl.ds(r0, tq)], preferred_element_type=F32)
                    ds = (p * (dp - delta_ref[a, :, pl.ds(r0, tq)]) * scale).astype(BF16)
                    dv_acc = dv_acc + jnp.dot(p.astype(BF16), dob_ref[pl.ds(r0, tq), :], preferred_element_type=F32)
                    dk_acc = dk_acc + jnp.dot(ds, qa, preferred_element_type=F32)
                    dq_parts.append(jnp.dot(kts[a], ds, preferred_element_type=F32))
                    out.append((dk_acc, dv_acc))
                if mla:
                    for a in range(2):
                        dqt_ref[cols[a], pl.ds(r0, tq)] += dq_parts[a]
                else:
                    dqt_ref[:, pl.ds(r0, tq)] += dq_parts[0] + dq_parts[1]
                return tuple(out)

            zero = jnp.zeros((tk, LANE), F32)
            last_near = jnp.minimum((c0 + tk - 1 + reach) // tq + 1, nq)
            c = lax.fori_loop(c0 // tq, last_near, functools.partial(q_block, kind=kind_near), ((zero, zero), (zero, zero)))
            (dk0, dv0), (dk1, dv1) = lax.fori_loop(last_near, nq, functools.partial(q_block, kind=kind_far), c)
            if mla:
                dk_ref[pl.ds(c0, tk), cols[0]] = dk0
                dk_ref[pl.ds(c0, tk), cols[1]] = dk1
            else:
                dk_ref[pl.ds(c0, tk), :] = jnp.where(sels[0], dk0, dk1)
            dv_ref[pl.ds(c0, tk), :] = jnp.where(sels[0], dv0, dv1)
            return carry

        lax.fori_loop(0, s // tk, k_block, 0)

        def write_dq(j, carry):
            c0 = pl.multiple_of(j * tk, tk)
            for w in range(qw // LANE):
                dq_ref[pl.ds(c0, tk), w * LANE:(w + 1) * LANE] = dqt_ref[w * LANE:(w + 1) * LANE, pl.ds(c0, tk)].T
            return carry

        lax.fori_loop(0, s // tk, write_dq, 0)

        if ns:
            @pl.when(pl.program_id(0) == last_step)
            def _():
                _Scatter(*comm).finish()

    b0 = do_block0
    return pl.pallas_call(
        body, name=name, grid=(HEADS // 2,),
        in_specs=[pl.BlockSpec((s, qw), lambda h: (0, h)), pl.BlockSpec((s, qw), lambda h: (0, h)),
                  pl.BlockSpec((s, LANE), lambda h: (0, h)), pl.BlockSpec((s, LANE), lambda h: (0, h)),
                  pl.BlockSpec((s, LANE), lambda h: (0, h + b0)), pl.BlockSpec((2, 1, s), lambda h: (h, 0, 0))] + [ANY] * ns,
        out_specs=[pl.BlockSpec((s, qw), lambda h: (0, h)), pl.BlockSpec((s, qw), lambda h: (0, h)),
                   pl.BlockSpec((s, LANE), lambda h: (0, h))] + [ANY] * ns,
        out_shape=[jax.ShapeDtypeStruct(q.shape, F32), jax.ShapeDtypeStruct(k.shape, F32), jax.ShapeDtypeStruct((s, DIL_W), F32)]
        + _Scatter.out_shapes(scatter),
        scratch_shapes=[pltpu.VMEM((qw, s), BF16), pltpu.VMEM((LANE, s), BF16), pltpu.VMEM((s, LANE), BF16),
                        pltpu.VMEM((qw, s), F32), pltpu.VMEM((2, 1, s), F32), pltpu.VMEM((2, 1, s), F32)]
        + (_Scatter.semaphores(ns) if ns else []),
        compiler_params=_params(("arbitrary",) if ns else ("parallel",), 24 << 20),
    )(q, k, v, o, do, lse, *scatter)


def _ada_fwd(c_all, w_shard, b_shard):
    n, d = c_all.shape
    cols = w_shard.shape[1]

    def body(c_ref, w_ref, b_ref, o_ref):
        cv = c_ref[...]
        sc = (cv * _sigmoid(cv)).astype(BF16)
        o_ref[...] = jnp.dot(sc, w_ref[...].astype(BF16), preferred_element_type=F32) + b_ref[...]

    return pl.pallas_call(
        body, name="ada_fwd", out_shape=jax.ShapeDtypeStruct((n, cols), F32),
        compiler_params=_params(None, 16 << 20),
    )(c_all, w_shard, b_shard)


def _ada_bwd(c_all, dmod_shard):
    n, d = c_all.shape
    cols = dmod_shard.shape[1]

    def body(c_ref, g_ref, o_ref):
        cv = c_ref[...]
        o_ref[...] = lax.dot_general(cv * _sigmoid(cv), g_ref[...], TN, precision=HIGHEST, preferred_element_type=F32)

    return pl.pallas_call(
        body, name="ada_bwd", out_shape=jax.ShapeDtypeStruct((d, cols), F32),
        compiler_params=_params(None, 16 << 20),
    )(c_all, dmod_shard)


SMALL_WIDTHS = (("g_mix_norm", D_MODEL), ("g_q_lat", Q_LORA), ("g_kv_lat", KV_LORA), ("g_mla_q_nope", NOPE),
                ("g_mla_q_pe", ROPE), ("g_mla_k_nope", NOPE), ("g_mla_k_pe", ROPE), ("g_dil_q", DIL_DIM),
                ("g_dil_k", DIL_DIM), ("g_ffn_norm", D_MODEL), ("b_conv", UP_W))


def _small_layout():
    pieces = (("dmod", 6 * D_MODEL),) + SMALL_WIDTHS + tuple(("w_conv%d" % k, UP_W) for k in range(3))
    layout, off = {}, 0
    for name, width in pieces:
        layout[name] = (width, off)
        off += -(-width // LANE) * LANE
    return layout, off


def _pack_small(acc1, acc2, dg2, dglat, dgains, dbg, dbv, dwg, dwv):
    layout, total = _small_layout()

    def body(a1, a2, g2, gl, gg, bg, bv, wg, wv, o_ref):
        o_ref[...] = jnp.zeros_like(o_ref)

        def put(name, src, shift=0):
            start = layout[name][1] + shift
            o_ref[:, start:start + src.shape[1]] = src

        for k, src in enumerate((a1[0:1, :], a1[1:2, :], a2[3:4, :], a2[0:1, :], a2[1:2, :], g2[...])):
            put("dmod", src, k * D_MODEL)
        put("g_mix_norm", a1[2:3, :])
        put("g_q_lat", gl[0:1, :])
        put("g_kv_lat", gl[1:2, 0:KV_LORA])
        put("g_mla_q_nope", gg[0:1, 0:NOPE])
        put("g_mla_q_pe", gg[5:6, 0:ROPE])
        put("g_mla_k_nope", gg[1:2, 0:NOPE])
        put("g_mla_k_pe", gg[2:3, 0:ROPE])
        put("g_dil_q", gg[3:4, 0:DIL_DIM])
        put("g_dil_k", gg[4:5, 0:DIL_DIM])
        put("g_ffn_norm", a2[2:3, :])
        put("b_conv", bg[...])
        put("b_conv", bv[...], D_FF)
        for k in range(3):
            put("w_conv%d" % k, wg[k:k + 1, :])
            put("w_conv%d" % k, wv[k:k + 1, :], D_FF)

    return pl.pallas_call(
        body, name="pack_small", out_shape=jax.ShapeDtypeStruct((1, total), F32),
        compiler_params=_params(None, 2 << 20),
    )(acc1, acc2, dg2, dglat, dgains, dbg, dbv, dwg, dwv)


def _sum_unpack(g):
    n_dev, _, total = g.shape
    layout, _ = _small_layout()

    def body(g_ref, *refs):
        o_refs, s_ref = refs[:-1], refs[-1]
        acc = g_ref[0]
        for k in range(1, n_dev):
            acc = acc + g_ref[k]
        s_ref[...] = acc
        take = lambda name: s_ref[:, layout[name][1]:layout[name][1] + layout[name][0]]
        o_refs[0][...] = take("dmod")
        for i, (name, _) in enumerate(SMALL_WIDTHS):
            o_refs[1 + i][...] = take(name)
        for k in range(3):
            o_refs[-1][k:k + 1, :] = take("w_conv%d" % k)

    shapes = [(1, 6 * D_MODEL)] + [(1, w) for _, w in SMALL_WIDTHS] + [(3, UP_W)]
    return pl.pallas_call(
        body, name="sum_unpack", out_shape=[jax.ShapeDtypeStruct(sh, F32) for sh in shapes],
        scratch_shapes=[pltpu.VMEM((1, total), F32)],
        compiler_params=_params(None, 4 << 20),
    )(g)


def _adamw_math(w, g, m, v):
    mn = ADAM_B1 * m + (1.0 - ADAM_B1) * g
    vn = ADAM_B2 * v + (1.0 - ADAM_B2) * (g * g)
    m_hat = mn / (1.0 - ADAM_B1 ** ADAM_STEP)
    v_hat = vn / (1.0 - ADAM_B2 ** ADAM_STEP)
    return -ADAM_LR * (m_hat / (jnp.sqrt(v_hat) + ADAM_EPS) + ADAM_WD * w), mn, vn


def _adamw_vectors(ws, gs, ms, vs):
    k = len(ws)

    def body(*refs):
        for i in range(k):
            d, mn, vn = _adamw_math(refs[i][...], refs[k + i][...], refs[2 * k + i][...], refs[3 * k + i][...])
            refs[4 * k + i][...] = d
            refs[5 * k + i][...] = mn
            refs[6 * k + i][...] = vn

    outs = pl.pallas_call(
        body, name="adamw_vectors", out_shape=[jax.ShapeDtypeStruct(w.shape, F32) for w in ws] * 3,
        compiler_params=_params(None, 2 << 20),
    )(*ws, *gs, *ms, *vs)
    return outs[:k], outs[k:2 * k], outs[2 * k:]


def _adamw(w, g, m, v, name):
    r, c = w.shape
    tr = r
    for cand in (256, 128, 64, 32, 16, 8):
        if r % cand == 0 and r > cand:
            tr = cand
            break

    def body(w_ref, g_ref, m_ref, v_ref, d_ref, mo_ref, vo_ref):
        d_ref[...], mo_ref[...], vo_ref[...] = _adamw_math(w_ref[...], g_ref[...], m_ref[...], v_ref[...])

    blk = pl.BlockSpec((tr, c), lambda i: (i, 0))
    return pl.pallas_call(
        body, name=name, grid=(r // tr,), in_specs=[blk] * 4, out_specs=[blk] * 3,
        out_shape=[jax.ShapeDtypeStruct((r, c), F32)] * 3,
        compiler_params=_params(("parallel",), 7 * _nbytes((tr, c), F32)),
    )(w, g, m, v)


def _position():
    return lax.axis_index("x"), lax.axis_index("y"), lax.axis_index("c")


def _other_chips(x, y):
    return [(1 - x, y, 2 * (1 - x) + y), (x, 1 - y, 2 * x + (1 - y)), (1 - x, 1 - y, 2 * (1 - x) + (1 - y))]


def _ag_small(v, name):
    r, w = v.shape

    def body(v_ref, out_ref, send_sems, recv_sems, local_sem):
        x, y, c = _position()
        me = 4 * x + 2 * y + c
        mine = pltpu.make_async_copy(v_ref, out_ref.at[me], local_sem)
        mine.start()
        peers = []
        for k in range(1, N_DEV):
            fx, fy, fc = (k >> 2) & 1, (k >> 1) & 1, k & 1
            px = 1 - x if fx else x
            py = 1 - y if fy else y
            pc = 1 - c if fc else c
            peers.append((px, py, pc))
        sends = []
        for k, peer in enumerate(peers):
            cp = pltpu.make_async_remote_copy(src_ref=v_ref, dst_ref=out_ref.at[me], send_sem=send_sems.at[k],
                                              recv_sem=recv_sems.at[k], device_id=peer, device_id_type=MESH)
            cp.start()
            sends.append(cp)
        for k, (px, py, pc) in enumerate(peers):
            pltpu.make_async_remote_copy(src_ref=v_ref, dst_ref=out_ref.at[4 * px + 2 * py + pc], send_sem=send_sems.at[k],
                                         recv_sem=recv_sems.at[k], device_id=(px, py, pc), device_id_type=MESH).wait_recv()
        for cp in sends:
            cp.wait_send()
        mine.wait()

    return pl.pallas_call(
        body, name=name,
        out_shape=jax.ShapeDtypeStruct((N_DEV, r, w), F32),
        in_specs=[pl.BlockSpec(memory_space=pltpu.VMEM)],
        out_specs=pl.BlockSpec(memory_space=pltpu.VMEM),
        scratch_shapes=[pltpu.SemaphoreType.DMA((N_DEV - 1,)), pltpu.SemaphoreType.DMA((N_DEV - 1,)), pltpu.SemaphoreType.DMA],
        compiler_params=_params(None, 10 * _nbytes((r, w), F32)),
    )(v)


ANY = pl.BlockSpec(memory_space=pl.ANY)


def _ag_weights(shards, name):
    n = len(shards)

    def body(*refs):
        gather = _Gather(refs[:n], refs[n:2 * n], *refs[2 * n:])
        gather.start()
        gather.forward()
        gather.finish()

    return pl.pallas_call(
        body, name=name,
        out_shape=_Gather.out_shapes(shards), in_specs=[ANY] * n, out_specs=[ANY] * n,
        scratch_shapes=_Gather.semaphores(n),
    )(*shards)


class _Gather:
    def __init__(self, w_refs, out_refs, send_sems, recv_sems):
        x, y, c = _position()
        q0 = 2 * x + y
        sibling = (x, y, 1 - c)
        self.ici, self.ici_in, self.fwd, self.fwd_in = [], [], [], []
        for k, (w_ref, out_ref) in enumerate(zip(w_refs, out_refs)):
            half = w_ref.shape[0] // 2

            def blk(q, e, out_ref=out_ref, half=half):
                return out_ref.at[q, pl.ds(pl.multiple_of(e * half, 16), half), :]

            def copy(src, dst, i, to):
                return pltpu.make_async_remote_copy(src_ref=src, dst_ref=dst, send_sem=send_sems.at[i], recv_sem=recv_sems.at[i],
                                                    device_id=to, device_id_type=MESH)

            src = w_ref.at[pl.ds(pl.multiple_of(c * half, 16), half), :]
            for j, (cx, cy, qj) in enumerate(_other_chips(x, y)):
                self.ici.append(copy(src, blk(q0, c), 6 * k + j, (cx, cy, c)))
                self.ici_in.append(copy(blk(qj, c), blk(qj, c), 6 * k + j, (cx, cy, c)))
                self.fwd.append(copy(blk(qj, c), blk(qj, c), 6 * k + 3 + j, sibling))
                self.fwd_in.append(copy(blk(qj, 1 - c), blk(qj, 1 - c), 6 * k + 3 + j, sibling))

    @staticmethod
    def out_shapes(shards):
        return [jax.ShapeDtypeStruct((N_CHIP,) + s.shape, s.dtype) for s in shards]

    @staticmethod
    def semaphores(n):
        return [pltpu.SemaphoreType.DMA((6 * n,)), pltpu.SemaphoreType.DMA((6 * n,))]

    def start(self):
        for cp in self.ici:
            cp.start()

    def forward(self):
        for arrived, onward in zip(self.ici_in, self.fwd):
            arrived.wait_recv()
            onward.start()

    def finish(self):
        for cp in self.fwd_in:
            cp.wait_recv()
        for cp in self.ici + self.fwd:
            cp.wait_send()


def _swap_halves_d2d(grads, name):
    n = len(grads)

    def body(*refs):
        g_refs, out_refs = refs[:n], refs[n:2 * n]
        send_sems, recv_sems = refs[2 * n:]
        x, y, c = _position()
        sibling = (x, y, 1 - c)
        cps = []
        for k in range(n):
            cp = pltpu.make_async_remote_copy(src_ref=g_refs[k].at[:, 1 - c], dst_ref=out_refs[k], send_sem=send_sems.at[k],
                                              recv_sem=recv_sems.at[k], device_id=sibling, device_id_type=MESH)
            cp.start()
            cps.append(cp)
        for cp in cps:
            cp.wait_recv()
        for cp in cps:
            cp.wait_send()

    return pl.pallas_call(
        body, name=name,
        out_shape=[jax.ShapeDtypeStruct((N_CHIP,) + g.shape[2:], g.dtype) for g in grads],
        in_specs=[ANY] * n, out_specs=[ANY] * n,
        scratch_shapes=[pltpu.SemaphoreType.DMA((n,)), pltpu.SemaphoreType.DMA((n,))],
    )(*grads)


def _pair_sum(g, a, c_idx, name):
    _, _, rh, cols = g.shape
    tr = rh
    for cand in (256, 128, 64, 32, 16):
        if rh % cand == 0 and rh > cand:
            tr = cand
            break

    def body(c_ref, g_ref, a_ref, o_ref):
        o_ref[...] = (g_ref[...] + a_ref[...]).astype(BF16)

    return pl.pallas_call(
        body, name=name,
        grid_spec=pltpu.PrefetchScalarGridSpec(
            num_scalar_prefetch=1, grid=(N_CHIP, rh // tr),
            in_specs=[pl.BlockSpec((None, None, tr, cols), lambda q, i, c_ref: (q, c_ref[0], i, 0)),
                      pl.BlockSpec((None, tr, cols), lambda q, i, c_ref: (q, i, 0))],
            out_specs=pl.BlockSpec((None, tr, cols), lambda q, i, c_ref: (q, i, 0))),
        out_shape=jax.ShapeDtypeStruct((N_CHIP, rh, cols), BF16),
        compiler_params=_params(("parallel", "parallel"), 10 * _nbytes((tr, cols), F32)),
    )(c_idx, g, a)


def _scatter_partials(parts, name):
    n = len(parts)

    def body(*refs):
        scatter = _Scatter(refs[:n], refs[n:2 * n], *refs[2 * n:])
        scatter.start()
        scatter.finish()

    return pl.pallas_call(
        body, name=name,
        out_shape=_Scatter.out_shapes(parts), in_specs=[ANY] * n, out_specs=[ANY] * n,
        scratch_shapes=_Scatter.semaphores(n),
    )(*parts)


class _Scatter:
    def __init__(self, p_refs, out_refs, send_sems, recv_sems):
        x, y, c = _position()
        self.copies = []
        for k, (p_ref, out_ref) in enumerate(zip(p_refs, out_refs)):
            for j, (cx, cy, qj) in enumerate(_other_chips(x, y)):
                self.copies.append(pltpu.make_async_remote_copy(
                    src_ref=p_ref.at[qj], dst_ref=out_ref.at[j], send_sem=send_sems.at[3 * k + j],
                    recv_sem=recv_sems.at[3 * k + j], device_id=(cx, cy, c), device_id_type=MESH))

    @staticmethod
    def out_shapes(parts):
        return [jax.ShapeDtypeStruct((3,) + p.shape[1:], p.dtype) for p in parts]

    @staticmethod
    def semaphores(n):
        return [pltpu.SemaphoreType.DMA((3 * n,)), pltpu.SemaphoreType.DMA((3 * n,))]

    def start(self):
        for cp in self.copies:
            cp.start()

    def finish(self):
        for cp in self.copies:
            cp.wait_recv()
        for cp in self.copies:
            cp.wait_send()


def _shard_sum(p, b, q_idx, name):
    _, rh, cols = p.shape
    tr = rh
    for cand in (256, 128, 64, 32, 16):
        if rh % cand == 0 and rh > cand:
            tr = cand
            break

    def body(q_ref, p_ref, b_ref, o_ref):
        acc = p_ref[...].astype(F32)
        for j in range(3):
            acc = acc + b_ref[j].astype(F32)
        o_ref[...] = acc

    return pl.pallas_call(
        body, name=name,
        grid_spec=pltpu.PrefetchScalarGridSpec(
            num_scalar_prefetch=1, grid=(rh // tr,),
            in_specs=[pl.BlockSpec((None, tr, cols), lambda i, q_ref: (q_ref[0], i, 0)),
                      pl.BlockSpec((3, tr, cols), lambda i, q_ref: (0, i, 0))],
            out_specs=pl.BlockSpec((tr, cols), lambda i, q_ref: (i, 0))),
        out_shape=jax.ShapeDtypeStruct((rh, cols), F32),
        compiler_params=_params(("parallel",), 8 * _nbytes((tr, cols), F32)),
    )(q_idx, p, b)


def _join_halves(halves):
    n = len(halves)

    def body(*refs):
        h_refs, out_refs = refs[:n], refs[n:2 * n]
        send_sems, recv_sems = refs[2 * n:]
        x, y, c = _position()
        sibling = (x, y, 1 - c)
        cps = []
        for k in range(n):
            cp = pltpu.make_async_remote_copy(src_ref=h_refs[k], dst_ref=out_refs[k], send_sem=send_sems.at[k],
                                              recv_sem=recv_sems.at[k], device_id=sibling, device_id_type=MESH)
            cp.start()
            cps.append(cp)
        for cp in cps:
            cp.wait_recv()
        for cp in cps:
            cp.wait_send()

    return pl.pallas_call(
        body, name="rs_join",
        out_shape=[jax.ShapeDtypeStruct(h.shape, h.dtype) for h in halves],
        in_specs=[ANY] * n, out_specs=[ANY] * n,
        scratch_shapes=[pltpu.SemaphoreType.DMA((n,)), pltpu.SemaphoreType.DMA((n,))],
    )(*halves)


def _cols_from_shards(g):
    q, r, cs = g.shape
    return jnp.transpose(g, (1, 0, 2)).reshape(r, q * cs)


def _cols_to_shards(w):
    r, cfull = w.shape
    return jnp.transpose(w.reshape(r, N_CHIP, cfull // N_CHIP), (1, 0, 2))


def _pad_w_in(w):
    z = lambda n: jnp.zeros((w.shape[0], n), w.dtype)
    q_lat, kv_lat, kpe = w[:, 0:512], w[:, 512:768], w[:, 768:800]
    qd, kd, vd = w[:, 800:1312], w[:, 1312:1824], w[:, 1824:2336]
    return jnp.concatenate([q_lat, qd, kd, vd, kv_lat, z(KPE_OFF), kpe, z(LANE - KPE_OFF - ROPE)], axis=1)


def _unpad_w_in(g):
    return jnp.concatenate([g[:, P_QLAT:P_QLAT + Q_LORA], g[:, P_KVLAT:P_KVLAT + KV_LORA],
                            g[:, P_KPE + KPE_OFF:P_KPE + KPE_OFF + ROPE], g[:, P_QD:P_QD + 3 * DIL_W]], axis=1)


def _pad_w_qb(w):
    w3 = w.reshape(Q_LORA, HEADS, NOPE + ROPE)
    return jnp.pad(w3, ((0, 0), (0, 0), (0, LANE - NOPE - ROPE))).reshape(Q_LORA, HEADS * LANE)


def _unpad_w_qb(g):
    return g.reshape(Q_LORA, HEADS, LANE)[:, :, :NOPE + ROPE].reshape(Q_LORA, HEADS * (NOPE + ROPE))


def _pad_w_kvb(w):
    w3 = w.reshape(KV_LORA, HEADS, 2 * NOPE)
    kp = jnp.pad(w3[:, :, :NOPE], ((0, 0), (0, 0), (0, LANE - NOPE))).reshape(KV_LORA, HEADS * LANE)
    return jnp.concatenate([kp, w3[:, :, NOPE:].reshape(KV_LORA, DIL_W)], axis=1)


def _unpad_w_kvb(g):
    gk = g[:, :HEADS * LANE].reshape(KV_LORA, HEADS, LANE)[:, :, :NOPE]
    gv = g[:, HEADS * LANE:].reshape(KV_LORA, HEADS, NOPE)
    return jnp.concatenate([gk, gv], axis=2).reshape(KV_LORA, HEADS * 2 * NOPE)


def _head_gains(g_q_nope, g_q_pe, g_k_nope, g_k_pe, g_dq, g_dk):
    z = lambda n: jnp.zeros((1, n), F32)
    q1 = jnp.concatenate([g_q_nope, g_q_pe, z(LANE - NOPE - ROPE)], axis=1)
    k1 = jnp.concatenate([g_k_nope, z(LANE - NOPE)], axis=1)
    kpe = jnp.concatenate([z(KPE_OFF), g_k_pe, z(LANE - KPE_OFF - ROPE)], axis=1)
    return dict(q=jnp.tile(q1, (1, HEADS)), k=jnp.tile(k1, (1, HEADS)), kpe=kpe,
                dq=jnp.tile(g_dq, (1, HEADS)), dk=jnp.tile(g_dk, (1, HEADS)))


def kernel(x, c, positions, w_ada, b_ada, g_mix_norm, w_in, g_q_lat, w_q_b, g_kv_lat, w_kv_b, g_mla_q_nope, g_mla_q_pe, g_mla_k_nope, g_mla_k_pe, g_dil_q, g_dil_k, w_o, g_ffn_norm, w_up, w_conv, b_conv, w_down, loss_target, m_w_ada, m_b_ada, m_g_mix_norm, m_w_in, m_g_q_lat, m_w_q_b, m_g_kv_lat, m_w_kv_b, m_g_mla_q_nope, m_g_mla_q_pe, m_g_mla_k_nope, m_g_mla_k_pe, m_g_dil_q, m_g_dil_k, m_w_o, m_g_ffn_norm, m_w_up, m_w_conv, m_b_conv, m_w_down, v_w_ada, v_b_ada, v_g_mix_norm, v_w_in, v_g_q_lat, v_w_q_b, v_g_kv_lat, v_w_kv_b, v_g_mla_q_nope, v_g_mla_q_pe, v_g_mla_k_nope, v_g_mla_k_pe, v_g_dil_q, v_g_dil_k, v_w_o, v_g_ffn_norm, v_w_up, v_w_conv, v_b_conv, v_w_down):
    args = dict(locals())
    weights = {n: args[n][0] for n in ("w_ada", "w_in", "w_q_b", "w_kv_b", "w_o", "w_up", "w_conv", "w_down")}
    small_w = {n: args[n] for n in ("b_ada",) + tuple(n for n, _ in SMALL_WIDTHS)}
    mom_m = {n[2:]: (args[n][0] if args[n].ndim == 3 else args[n]) for n in args if n.startswith("m_")}
    mom_v = {n[2:]: (args[n][0] if args[n].ndim == 3 else args[n]) for n in args if n.startswith("v_")}

    xi, yi, ci = _position()
    q0 = 2 * xi + yi
    me = 4 * xi + 2 * yi + ci
    xs, tgt = x[0], loss_target[0]
    s = xs.shape[0]
    consts = _seg_consts()
    c_idx, q_idx = jnp.reshape(ci, (1,)).astype(I32), jnp.reshape(q0, (1,)).astype(I32)

    def halves(g4):
        q, r, cc = g4.shape
        return g4.reshape(q, 2, r // 2, cc)

    c_all = _ag_small(c, "ag_c")[:, 0, :]
    ada_cols = w_ada.shape[2]
    b_shard = lax.dynamic_slice_in_dim(b_ada, q0 * ada_cols, ada_cols, axis=1)
    mod_blk = _ada_fwd(c_all, weights["w_ada"], b_shard)
    mod_all = _ag_small(mod_blk, "ag_mod").reshape(N_CHIP, 2, N_DEV, ada_cols)
    mod = lax.dynamic_index_in_dim(lax.dynamic_index_in_dim(mod_all, ci, 1, False), me, 1, False)
    mod = mod.reshape(1, N_CHIP * ada_cols)
    sh1, sc1, g1, sh2, sc2, g2 = [mod[:, k * D_MODEL:(k + 1) * D_MODEL] for k in range(6)]

    place_own = lambda gs, ws: [lax.dynamic_update_slice_in_dim(g, w[None], q0, axis=0) for g, w in zip(gs, ws)]
    own_first = [weights[n].astype(BF16) for n in ("w_in", "w_q_b", "w_kv_b")]
    own_later = [weights[n].astype(BF16) for n in ("w_o", "w_up", "w_down")]
    gathered = place_own(_ag_weights(own_first, "ag_weights"), own_first)
    w_in_p = _pad_w_in(_cols_from_shards(gathered[0]))
    w_qb_p = _pad_w_qb(_cols_from_shards(gathered[1]))
    w_kvb_p = _pad_w_kvb(_cols_from_shards(gathered[2]))
    w_conv_f = _ag_small(weights["w_conv"], "ag_wconv")
    w_conv_f = jnp.transpose(w_conv_f.reshape(N_CHIP, 2, 3, -1)[:, 0], (1, 0, 2)).reshape(3, UP_W)

    gains = _head_gains(g_mla_q_nope, g_mla_q_pe, g_mla_k_nope, g_mla_k_pe, g_dil_q, g_dil_k)
    tab = _rope_tables(positions.reshape(s, 1), *_rope_consts())

    h = _prenorm(xs, g_mix_norm, sc1, sh1, "prenorm")
    proj = _mm(h, w_in_p, "nn", F32, 512, P_COLS, "mm_in")
    ql, kvl = _latnorm(proj, g_q_lat, g_kv_lat)
    q_raw = _mm(ql, w_qb_p, "nn", F32, 512, HEADS * LANE, "mm_qb")
    kv_raw = _mm(kvl, w_kvb_p, "nn", F32, 512, HEADS * LANE + DIL_W, "mm_kvb")
    qm, km, vm, qd, kd, vd = _attn_prep(q_raw, kv_raw, proj, tab, gains, consts)
    scale_m, scale_d = (NOPE + ROPE) ** -0.5, DIL_DIM ** -0.5
    o_m, lse_m, *gathered = _attn_fwd(qm, km, vm, True, scale_m, "attn_mla", gather=own_later[:2])
    o_d, lse_d, *gathered_d = _attn_fwd(qd, kd, vd, False, scale_d, "attn_dil", gather=own_later[2:])
    gathered = place_own(gathered + gathered_d, own_later)
    w_o_f = gathered[0].reshape(D_MODEL, D_MODEL)
    w_up_f = _cols_from_shards(gathered[1])
    w_down_f = gathered[2].reshape(D_FF, D_MODEL)
    mix_in = jnp.concatenate([o_m, o_d], axis=1)
    mix = _mm(mix_in, w_o_f, "nn", F32, 512, D_MODEL, "mm_o")
    x1, h2 = _resid_prenorm(xs, mix, g1, g_ffn_norm, sc2, sh2)
    up = _mm(h2, w_up_f, "nn", F32, 512, CONV_TILE, "mm_up")
    act = _conv_gate(up, w_conv_f, b_conv)
    ffn = _mm(act, w_down_f, "nn", F32, 256, D_MODEL, "mm_down")
    dy, dffn, dg2, loss_part = _final(x1, ffn, tgt, g2)

    da = _mm(dffn, w_down_f, "nt", F32, 512, CONV_TILE, "mm_down_dx")
    gw_down = _mm(act, dffn, "tn", F32, 256, D_MODEL, "mm_down_dw")
    dug, duv, dbg, dbv, dwg, dwv = _gate_bwd(up, da, w_conv_f, b_conv)
    dup = jnp.concatenate([_conv_bwd(dug, w_conv_f[:, :D_FF], "conv_bwd_gate"),
                           _conv_bwd(duv, w_conv_f[:, D_FF:], "conv_bwd_val")], axis=1)
    dh2 = _mm(dup, w_up_f, "nt", F32, 256, 512, "mm_up_dx")
    gw_up = _mm(h2, dup, "tn", F32, 512, CONV_TILE, "mm_up_dw")
    dx1, dmix, acc2 = _ffnnorm_bwd(dh2, x1, dy, mix, g_ffn_norm, sc2, g1)
    dmix_in = _mm(dmix, w_o_f, "nt", F32, 512, D_MODEL, "mm_o_dx")
    gw_o = _mm(mix_in, dmix, "tn", F32, 512, D_MODEL, "mm_o_dw")
    early_names = ("w_up", "w_down", "w_o")
    early = [halves(_cols_to_shards(gw_up)), halves(gw_down.reshape(N_CHIP, D_FF // N_CHIP, D_MODEL)),
             halves(gw_o.reshape(N_CHIP, D_MODEL // N_CHIP, D_MODEL))]
    early_sib = _swap_halves_d2d(early, "rs_pair_swap_early")
    early_sums = [_pair_sum(g, a, c_idx, "pair_sum_" + n) for g, a, n in zip(early, early_sib, early_names)]
    dqm, dkm, dvm, *early_recv = _attn_bwd(qm, km, vm, o_m, dmix_in, 0, lse_m, True, scale_m, "attn_mla_bwd",
                                           scatter=early_sums[:1])
    dqd, dkd, dvd, *early_recv_d = _attn_bwd(qd, kd, vd, o_d, dmix_in, DIL_W // LANE, lse_d, False, scale_d,
                                             "attn_dil_bwd", scatter=early_sums[1:])
    early_recv = early_recv + early_recv_d
    dq_raw, dkv_raw, dkpe_b, dqd_b, dkd_b, dvd_b, dgains = _attn_prep_bwd(
        dqm, dkm, dvm, dqd, dkd, dvd, q_raw, kv_raw, proj, tab, gains, consts)
    dql = _mm(dq_raw, w_qb_p, "nt", F32, 512, Q_LORA, "mm_qb_dx")
    gw_qb = _unpad_w_qb(_mm(ql, dq_raw, "tn", F32, Q_LORA, HEADS * LANE, "mm_qb_dw"))
    dkvl = _mm(dkv_raw, w_kvb_p, "nt", F32, 512, KV_LORA, "mm_kvb_dx")
    gw_kvb = _unpad_w_kvb(_mm(kvl, dkv_raw, "tn", F32, KV_LORA, HEADS * LANE + DIL_W, "mm_kvb_dw"))
    dqlat_b, dkvlat_b, dglat = _latnorm_bwd(dql, dkvl, proj, g_q_lat, g_kv_lat)
    dproj = jnp.concatenate([dqlat_b, dqd_b, dkd_b, dvd_b, dkvlat_b, dkpe_b], axis=1)
    dh = _mm(dproj, w_in_p, "nt", F32, 512, D_MODEL, "mm_in_dx")
    gw_in = _unpad_w_in(_mm(h, dproj, "tn", F32, 512, P_COLS, "mm_in_dw"))
    grad_x, acc1 = _mixnorm_bwd(dh, xs, dx1, g_mix_norm, sc1)

    packed = _pack_small(acc1, acc2, dg2, dglat, dgains, dbg, dbv, dwg, dwv)
    gathered_small = _ag_small(packed, "ag_small")
    grad_b_ada, *small_grads, gconv_full = _sum_unpack(gathered_small)
    grads = {"b_ada": grad_b_ada}
    grads.update({n: g for (n, _), g in zip(SMALL_WIDTHS, small_grads)})
    shard_cols = UP_W // N_CHIP
    grads["w_conv"] = lax.dynamic_slice_in_dim(gconv_full, q0 * shard_cols, shard_cols, axis=1)
    dmod_all = gathered_small[:, 0, :6 * D_MODEL]
    grads["w_ada"] = _ada_bwd(c_all, lax.dynamic_slice_in_dim(dmod_all, q0 * ada_cols, ada_cols, axis=1))

    late_names = ("w_in", "w_q_b", "w_kv_b")
    late = [halves(_cols_to_shards(gw_in)), halves(_cols_to_shards(gw_qb)), halves(_cols_to_shards(gw_kvb))]
    late_sib = _swap_halves_d2d(late, "rs_pair_swap_late")
    late_sums = [_pair_sum(g, a, c_idx, "pair_sum_" + n) for g, a, n in zip(late, late_sib, late_names)]
    late_recv = _scatter_partials(late_sums, "rs_scatter_late")
    big_names = late_names + early_names
    half_sums = [_shard_sum(p, b, q_idx, "shard_sum_" + n)
                 for p, b, n in zip(late_sums + early_sums, list(late_recv) + list(early_recv), big_names)]
    from_sib = _join_halves(half_sums)
    south = ci == 0
    for n, mine, theirs in zip(big_names, half_sums, from_sib):
        grads[n] = jnp.concatenate([jnp.where(south, mine, theirs), jnp.where(south, theirs, mine)], axis=0)

    delta, new_m, new_v = {}, {}, {}
    for n in ("w_ada", "w_in", "w_q_b", "w_kv_b", "w_o", "w_up", "w_conv", "w_down"):
        delta[n], new_m[n], new_v[n] = _adamw(weights[n], grads[n], mom_m[n], mom_v[n], "adamw_" + n)
    vec_names = ("b_ada",) + tuple(n for n, _ in SMALL_WIDTHS)
    sd, sm, sv = _adamw_vectors(*[[d_[n] for n in vec_names] for d_ in (small_w, grads, mom_m, mom_v)])
    for k, n in enumerate(vec_names):
        delta[n], new_m[n], new_v[n] = sd[k], sm[k], sv[k]

    loss = lax.psum(loss_part[0, 0], ("x", "y", "c"))
    order = ("w_ada", "b_ada", "g_mix_norm", "w_in", "g_q_lat", "w_q_b", "g_kv_lat", "w_kv_b", "g_mla_q_nope", "g_mla_q_pe",
             "g_mla_k_nope", "g_mla_k_pe", "g_dil_q", "g_dil_k", "w_o", "g_ffn_norm", "w_up", "w_conv", "b_conv", "w_down")
    lead = lambda n, z: z[None] if n.startswith("w_") else z
    outs = [loss, grad_x[None]]
    for d_ in (grads, delta, new_m, new_v):
        outs += [lead(n, d_[n]) for n in order]
    return tuple(outs)
```

```python
import functools

import numpy as np
import jax
import jax.numpy as jnp
from jax import lax
from jax.experimental import pallas as pl
from jax.experimental.pallas import tpu as pltpu

F32 = jnp.float32
BF16 = jnp.bfloat16
I32 = jnp.int32

D_MODEL = 1024
HEADS = 8
NOPE = 64
ROPE = 32
Q_LORA = 512
KV_LORA = 256
DIL_DIM = 64
DIL_W = HEADS * DIL_DIM
D_FF = 2816
UP_W = 2 * D_FF
IN_COLS = Q_LORA + KV_LORA + ROPE + 3 * DIL_W
ROPE_THETA = 10000.0
EPS = 1e-6
NEG_INF = -1e30
N_DEV = 8
N_CHIP = 4

ADAM_LR = 0.001
ADAM_B1 = 0.9
ADAM_B2 = 0.999
ADAM_EPS = 1e-08
ADAM_WD = 0.01
ADAM_STEP = 10

LANE = 128
ROW_TILE = 256
ATT_TQ = 512
ATT_TK = 256
LOG2E = 1.4426950408889634
LN2 = 0.6931471805599453
VMEM_CAP = 56 * 1024 * 1024
VMEM_FLOOR = 32 * 1024 * 1024

P_QLAT, P_QD, P_KD, P_VD, P_KVLAT, P_KPE = 0, 512, 1024, 1536, 2048, 2304
P_COLS = 2432
KPE_OFF = 64

NN = (((1,), (0,)), ((), ()))
NT = (((1,), (1,)), ((), ()))
TN = (((0,), (0,)), ((), ()))
HIGHEST = lax.Precision.HIGHEST
MESH = pl.DeviceIdType.MESH


def _params(sem=None, est_bytes=0):
    limit = int(min(max(2 * est_bytes + (4 << 20), VMEM_FLOOR), VMEM_CAP))
    if sem is None:
        return pltpu.CompilerParams(vmem_limit_bytes=limit)
    return pltpu.CompilerParams(dimension_semantics=sem, vmem_limit_bytes=limit)


def _nbytes(shape, dtype):
    return int(np.prod(shape)) * jnp.dtype(dtype).itemsize


def _in_hbm(*xs):
    return [pltpu.with_memory_space_constraint(x, pltpu.HBM) for x in xs]


def _mm(a, b, dims, out_dtype, tm, tn, name):
    if dims == "nn":
        (m, k), (k2, n) = a.shape, b.shape
        a_spec = pl.BlockSpec((tm, k), lambda i, j: (i, 0))
        b_spec = pl.BlockSpec((k, tn), lambda i, j: (0, j))
        dn = NN
    elif dims == "nt":
        (m, k), (n, k2) = a.shape, b.shape
        a_spec = pl.BlockSpec((tm, k), lambda i, j: (i, 0))
        b_spec = pl.BlockSpec((tn, k), lambda i, j: (j, 0))
        dn = NT
    else:
        (k, m), (k2, n) = a.shape, b.shape
        a_spec = pl.BlockSpec((k, tm), lambda i, j: (0, i))
        b_spec = pl.BlockSpec((k, tn), lambda i, j: (0, j))
        dn = TN
    assert k == k2 and m % tm == 0 and n % tn == 0, (name, a.shape, b.shape, tm, tn)

    def body(a_ref, b_ref, o_ref):
        o_ref[...] = lax.dot_general(a_ref[...], b_ref[...], dn, preferred_element_type=F32).astype(o_ref.dtype)

    est = _nbytes((tm, k), a.dtype) + _nbytes((tn, k), b.dtype) + _nbytes((tm, tn), F32) + _nbytes((tm, tn), out_dtype)
    return pl.pallas_call(
        body, name=name,
        grid=(m // tm, n // tn),
        in_specs=[a_spec, b_spec],
        out_specs=pl.BlockSpec((tm, tn), lambda i, j: (i, j)),
        out_shape=jax.ShapeDtypeStruct((m, n), out_dtype),
        compiler_params=_params(("parallel", "parallel"), est),
    )(a, b)


def _seg_consts():
    seg_q = np.zeros((HEADS * LANE, LANE), np.float32)
    inv_q = np.zeros((1, LANE), np.float32)
    seg_k = np.zeros((HEADS * LANE, LANE), np.float32)
    inv_k = np.zeros((1, LANE), np.float32)
    seg_d = np.zeros((DIL_W, LANE), np.float32)
    inv_d = np.zeros((1, LANE), np.float32)
    for h in range(HEADS):
        seg_q[h * LANE:h * LANE + NOPE, 2 * h] = 1.0
        seg_q[h * LANE + NOPE:h * LANE + NOPE + ROPE, 2 * h + 1] = 1.0
        inv_q[0, 2 * h], inv_q[0, 2 * h + 1] = 1.0 / NOPE, 1.0 / ROPE
        seg_k[h * LANE:h * LANE + NOPE, h] = 1.0
        inv_k[0, h] = 1.0 / NOPE
        seg_d[h * DIL_DIM:(h + 1) * DIL_DIM, h] = 1.0
        inv_d[0, h] = 1.0 / DIL_DIM
    fold_q = np.tile(np.eye(LANE, dtype=np.float32), (HEADS, 1))
    fold_d = np.zeros((DIL_W, LANE), np.float32)
    fold_d[np.arange(DIL_W), np.arange(DIL_W) % DIL_DIM] = 1.0
    j = lambda v: jnp.asarray(v)
    b = lambda v: jnp.asarray(v, dtype=BF16)
    return dict(seg_q=b(seg_q), exp_q=b(seg_q.T.copy()), inv_q=j(inv_q), seg_k=b(seg_k), exp_k=b(seg_k.T.copy()),
                inv_k=j(inv_k), seg_d=b(seg_d), exp_d=b(seg_d.T.copy()), inv_d=j(inv_d), fold_q=j(fold_q), fold_d=j(fold_d))


def _rope_consts():
    inv_d = jnp.power(ROPE_THETA, -2.0 * jnp.arange(DIL_DIM // 2, dtype=F32) / DIL_DIM)
    inv_q = jnp.power(ROPE_THETA, -2.0 * jnp.arange(ROPE // 2, dtype=F32) / ROPE)
    lanes = np.arange(LANE)
    freq_d = inv_d[lanes % (DIL_DIM // 2)]
    in_pe = (lanes >= KPE_OFF) & (lanes < KPE_OFF + ROPE)
    freq_q = jnp.where(jnp.asarray(in_pe), inv_q[(lanes - KPE_OFF) % (ROPE // 2)], 0.0)
    sign_d = np.where(lanes % DIL_DIM < DIL_DIM // 2, -1.0, 1.0).astype(np.float32)
    sign_q = np.where(in_pe, np.where((lanes - KPE_OFF) < ROPE // 2, -1.0, 1.0), 0.0).astype(np.float32)
    zeros, ones = np.zeros(LANE, np.float32), np.ones(LANE, np.float32)
    freq = jnp.concatenate([freq_d, freq_d, freq_q, freq_q])[None, :]
    csel = jnp.asarray(np.concatenate([ones, zeros, ones, zeros]))[None, :]
    ssel = jnp.asarray(np.concatenate([zeros, sign_d, zeros, sign_q]))[None, :]
    return freq, csel, ssel


def _full(shape):
    return pl.BlockSpec(shape, lambda *_: (0,) * len(shape))


def _tile_lanes(x, n):
    return jnp.concatenate([x] * n, axis=1)


def _rope_tables(pos_col, freq, csel, ssel):
    s = pos_col.shape[0]

    def body(p_ref, f_ref, c_ref, s_ref, o_ref):
        ang = p_ref[...].astype(F32) * f_ref[...]
        o_ref[...] = c_ref[...] * jnp.cos(ang) + s_ref[...] * jnp.sin(ang)

    return pl.pallas_call(
        body, name="rope_tables", grid=(s // ROW_TILE,),
        in_specs=[pl.BlockSpec((ROW_TILE, 1), lambda i: (i, 0)), _full((1, 4 * LANE)), _full((1, 4 * LANE)), _full((1, 4 * LANE))],
        out_specs=pl.BlockSpec((ROW_TILE, 4 * LANE), lambda i: (i, 0)),
        out_shape=jax.ShapeDtypeStruct((s, 4 * LANE), F32),
        compiler_params=_params(("parallel",)),
    )(pos_col, freq, csel, ssel)


def _rms(x):
    return lax.rsqrt(jnp.mean(x * x, axis=-1, keepdims=True) + EPS)


def _prenorm(x, gain, scale, shift, name):
    s, d = x.shape

    def body(x_ref, g_ref, sc_ref, sh_ref, h_ref):
        xv = x_ref[...]
        h = (xv * _rms(xv)) * g_ref[...] * (1.0 + sc_ref[...]) + sh_ref[...]
        h_ref[...] = h.astype(BF16)

    row = pl.BlockSpec((ROW_TILE, d), lambda i: (i, 0))
    return pl.pallas_call(
        body, name=name, grid=(s // ROW_TILE,),
        in_specs=[row, _full((1, d)), _full((1, d)), _full((1, d))],
        out_specs=row, out_shape=jax.ShapeDtypeStruct((s, d), BF16),
        compiler_params=_params(("parallel",)),
    )(x, gain, scale, shift)


def _latnorm(proj, g_q, g_kv):
    s = proj.shape[0]

    def body(q_ref, kv_ref, gq_ref, gkv_ref, ql_ref, kvl_ref):
        q, kv = q_ref[...], kv_ref[...]
        ql_ref[...] = ((q * _rms(q)) * gq_ref[...]).astype(BF16)
        kvl_ref[...] = ((kv * _rms(kv)) * gkv_ref[...]).astype(BF16)

    return pl.pallas_call(
        body, name="latnorm", grid=(s // ROW_TILE,),
        in_specs=[pl.BlockSpec((ROW_TILE, Q_LORA), lambda i: (i, P_QLAT // Q_LORA)),
                  pl.BlockSpec((ROW_TILE, KV_LORA), lambda i: (i, P_KVLAT // KV_LORA)),
                  _full((1, Q_LORA)), _full((1, KV_LORA))],
        out_specs=[pl.BlockSpec((ROW_TILE, Q_LORA), lambda i: (i, 0)), pl.BlockSpec((ROW_TILE, KV_LORA), lambda i: (i, 0))],
        out_shape=[jax.ShapeDtypeStruct((s, Q_LORA), BF16), jax.ShapeDtypeStruct((s, KV_LORA), BF16)],
        compiler_params=_params(("parallel",)),
    )(proj, proj, g_q, g_kv)


def _dot01(v, mat01):
    hi = v.astype(BF16)
    lo = (v - hi.astype(F32)).astype(BF16)
    return jnp.dot(hi, mat01, preferred_element_type=F32) + jnp.dot(lo, mat01, preferred_element_type=F32)


def _seg_rinv(x, seg, exp, inv):
    r = lax.rsqrt(_dot01(x * x, seg) * inv + EPS)
    return _dot01(r, exp)


def _seg_mean(v, seg, exp, inv):
    return _dot01(_dot01(v, seg) * inv, exp)


def _swap_halves(x, half):
    n = x.shape[1]
    lane = lax.broadcasted_iota(I32, (1, n), 1)
    first = (lane & (2 * half - 1)) < half
    return jnp.where(first, pltpu.roll(x, n - half, 1), pltpu.roll(x, half, 1))


def _rope(x, cos, sin_signed, half):
    return x * cos + _swap_halves(x, half) * sin_signed


def _rope_bwd(dy, cos, sin_signed, half):
    return dy * cos + _swap_halves(dy * sin_signed, half)


def _pe_lane_mask(n):
    lane = lax.broadcasted_iota(I32, (1, n), 1) & (LANE - 1)
    return (lane >= KPE_OFF) & (lane < KPE_OFF + ROPE)


def _attn_prep(q_raw, kv_raw, proj, tab, gains, consts):
    s = q_raw.shape[0]
    hw = HEADS * LANE

    def body(q_ref, kv_ref, kpe_ref, qd_ref, kd_ref, vd_ref, tab_ref,
             gq_ref, gk_ref, gkpe_ref, gdq_ref, gdk_ref,
             segq_ref, expq_ref, invq_ref, segk_ref, expk_ref, invk_ref, segd_ref, expd_ref, invd_ref,
             qm_ref, km_ref, vm_ref, qdo_ref, kdo_ref, vdo_ref):
        tab_v = tab_ref[...]
        cos_d, sin_d = _tile_lanes(tab_v[:, 0:LANE], DIL_W // LANE), _tile_lanes(tab_v[:, LANE:2 * LANE], DIL_W // LANE)
        cos_q1, sin_q1 = tab_v[:, 2 * LANE:3 * LANE], tab_v[:, 3 * LANE:4 * LANE]
        cos_q, sin_q = _tile_lanes(cos_q1, HEADS), _tile_lanes(sin_q1, HEADS)

        q = q_ref[...]
        qn = q * _seg_rinv(q, segq_ref[...], expq_ref[...], invq_ref[...]) * gq_ref[...]
        qm_ref[...] = _rope(qn, cos_q, sin_q, ROPE // 2).astype(BF16)

        kv = kv_ref[...]
        kp = kv[:, :hw]
        kn = kp * _seg_rinv(kp, segk_ref[...], expk_ref[...], invk_ref[...]) * gk_ref[...]
        kpe = kpe_ref[...]
        r_pe = lax.rsqrt(jnp.sum(kpe * kpe, axis=-1, keepdims=True) * (1.0 / ROPE) + EPS)
        kpe_r = _rope(kpe * r_pe * gkpe_ref[...], cos_q1, sin_q1, ROPE // 2)
        km_ref[...] = (kn + _tile_lanes(kpe_r, HEADS)).astype(BF16)
        vm_ref[...] = kv[:, hw:].astype(BF16)

        qd = qd_ref[...]
        qdn = qd * _seg_rinv(qd, segd_ref[...], expd_ref[...], invd_ref[...]) * gdq_ref[...]
        qdo_ref[...] = _rope(qdn, cos_d, sin_d, DIL_DIM // 2).astype(BF16)
        kd = kd_ref[...]
        kdn = kd * _seg_rinv(kd, segd_ref[...], expd_ref[...], invd_ref[...]) * gdk_ref[...]
        kdo_ref[...] = _rope(kdn, cos_d, sin_d, DIL_DIM // 2).astype(BF16)
        vdo_ref[...] = vd_ref[...].astype(BF16)

    t = ROW_TILE
    row = lambda w, cb=0: pl.BlockSpec((t, w), lambda i: (i, cb))
    c = consts
    return pl.pallas_call(
        body, name="attn_prep", grid=(s // t,),
        in_specs=[row(hw), row(hw + DIL_W), row(LANE, P_KPE // LANE), row(DIL_W, P_QD // DIL_W), row(DIL_W, P_KD // DIL_W),
                  row(DIL_W, P_VD // DIL_W), row(4 * LANE),
                  _full((1, hw)), _full((1, hw)), _full((1, LANE)), _full((1, DIL_W)), _full((1, DIL_W)),
                  _full((hw, LANE)), _full((LANE, hw)), _full((1, LANE)), _full((hw, LANE)), _full((LANE, hw)), _full((1, LANE)),
                  _full((DIL_W, LANE)), _full((LANE, DIL_W)), _full((1, LANE))],
        out_specs=[row(hw), row(hw), row(DIL_W), row(DIL_W), row(DIL_W), row(DIL_W)],
        out_shape=[jax.ShapeDtypeStruct((s, hw), BF16), jax.ShapeDtypeStruct((s, hw), BF16)]
        + [jax.ShapeDtypeStruct((s, DIL_W), BF16)] * 4,
        compiler_params=_params(("parallel",), 24 << 20),
    )(q_raw, kv_raw, proj, proj, proj, proj, tab, gains["q"], gains["k"], gains["kpe"], gains["dq"], gains["dk"],
      c["seg_q"], c["exp_q"], c["inv_q"], c["seg_k"], c["exp_k"], c["inv_k"], c["seg_d"], c["exp_d"], c["inv_d"])


def _attn_prep_bwd(dqm, dkm, dvm, dqd, dkd, dvd, q_raw, kv_raw, proj, tab, gains, consts):
    s = q_raw.shape[0]
    hw = HEADS * LANE
    n_steps = s // ROW_TILE

    def body(dqm_ref, dkm_ref, dvm_ref, dqd_ref, dkd_ref, dvd_ref, q_ref, kv_ref, kpe_ref, qd_ref, kd_ref, tab_ref,
             gq_ref, gk_ref, gkpe_ref, gdq_ref, gdk_ref,
             segq_ref, expq_ref, invq_ref, segk_ref, expk_ref, invk_ref, segd_ref, expd_ref, invd_ref, foldq_ref, foldd_ref,
             dq_ref, dkv_ref, dkpe_ref, dqdo_ref, dkdo_ref, dvdo_ref, dg_ref, acc_ref):
        i = pl.program_id(0)

        @pl.when(i == 0)
        def _():
            acc_ref[...] = jnp.zeros_like(acc_ref)

        tab_v = tab_ref[...]
        cos_d, sin_d = _tile_lanes(tab_v[:, 0:LANE], DIL_W // LANE), _tile_lanes(tab_v[:, LANE:2 * LANE], DIL_W // LANE)
        cos_q1, sin_q1 = tab_v[:, 2 * LANE:3 * LANE], tab_v[:, 3 * LANE:4 * LANE]
        cos_q, sin_q = _tile_lanes(cos_q1, HEADS), _tile_lanes(sin_q1, HEADS)

        def norm_bwd(x, dyg, gain, seg, exp, inv):
            rinv = _seg_rinv(x, seg, exp, inv)
            xn = x * rinv
            dxn = dyg * gain
            dx = rinv * (dxn - xn * _seg_mean(dxn * xn, seg, exp, inv))
            return dx, jnp.sum(dyg * xn, axis=0, keepdims=True)

        dq, gq_l = norm_bwd(q_ref[...], _rope_bwd(dqm_ref[...], cos_q, sin_q, ROPE // 2), gq_ref[...],
                            segq_ref[...], expq_ref[...], invq_ref[...])
        dq_ref[...] = dq.astype(BF16)

        dkm = dkm_ref[...]
        kv = kv_ref[...]
        dkp, gk_l = norm_bwd(kv[:, :hw], dkm, gk_ref[...], segk_ref[...], expk_ref[...], invk_ref[...])
        dkv_ref[:, :hw] = dkp.astype(BF16)
        dkv_ref[:, hw:] = dvm_ref[...].astype(BF16)

        dkpe_r = dkm[:, 0:LANE]
        for h in range(1, HEADS):
            dkpe_r = dkpe_r + dkm[:, h * LANE:(h + 1) * LANE]
        dkpe_r = jnp.where(_pe_lane_mask(LANE), dkpe_r, 0.0)
        dyg = _rope_bwd(dkpe_r, cos_q1, sin_q1, ROPE // 2)
        kpe = kpe_ref[...]
        r_pe = lax.rsqrt(jnp.sum(kpe * kpe, axis=-1, keepdims=True) * (1.0 / ROPE) + EPS)
        xn = kpe * r_pe
        dxn = dyg * gkpe_ref[...]
        dkpe = r_pe * (dxn - xn * (jnp.sum(dxn * xn, axis=-1, keepdims=True) * (1.0 / ROPE)))
        dkpe_ref[...] = dkpe.astype(BF16)
        gkpe_l = jnp.sum(dyg * xn, axis=0, keepdims=True)

        dqd_v, gdq_l = norm_bwd(qd_ref[...], _rope_bwd(dqd_ref[...], cos_d, sin_d, DIL_DIM // 2), gdq_ref[...],
                                segd_ref[...], expd_ref[...], invd_ref[...])
        dqdo_ref[...] = dqd_v.astype(BF16)
        dkd_v, gdk_l = norm_bwd(kd_ref[...], _rope_bwd(dkd_ref[...], cos_d, sin_d, DIL_DIM // 2), gdk_ref[...],
                                segd_ref[...], expd_ref[...], invd_ref[...])
        dkdo_ref[...] = dkd_v.astype(BF16)
        dvdo_ref[...] = dvd_ref[...].astype(BF16)

        acc_ref[0:1, :] += gq_l
        acc_ref[1:2, :] += gk_l
        acc_ref[2:3, 0:LANE] += gkpe_l
        acc_ref[3:4, 0:DIL_W] += gdq_l
        acc_ref[4:5, 0:DIL_W] += gdk_l

        @pl.when(i == n_steps - 1)
        def _():
            acc = acc_ref[...]
            fq = jnp.dot(acc, foldq_ref[...], precision=HIGHEST, preferred_element_type=F32)
            fd = jnp.dot(acc[:, 0:DIL_W], foldd_ref[...], precision=HIGHEST, preferred_element_type=F32)
            rows = lax.broadcasted_iota(I32, (8, LANE), 0)
            base = jnp.where(rows < 2, fq, jnp.where(rows == 2, acc[:, 0:LANE], fd))
            at0 = pltpu.roll(base, LANE - KPE_OFF, 1)
            dg_ref[...] = jnp.where(rows == 5, pltpu.roll(at0, 5, 0), jnp.where(rows == 2, at0, base))

    t = ROW_TILE
    row = lambda w, cb=0: pl.BlockSpec((t, w), lambda i: (i, cb))
    c = consts
    return pl.pallas_call(
        body, name="attn_prep_bwd", grid=(n_steps,),
        in_specs=[row(hw), row(hw), row(DIL_W), row(DIL_W), row(DIL_W), row(DIL_W),
                  row(hw), row(hw + DIL_W), row(LANE, P_KPE // LANE), row(DIL_W, P_QD // DIL_W), row(DIL_W, P_KD // DIL_W),
                  row(4 * LANE),
                  _full((1, hw)), _full((1, hw)), _full((1, LANE)), _full((1, DIL_W)), _full((1, DIL_W)),
                  _full((hw, LANE)), _full((LANE, hw)), _full((1, LANE)), _full((hw, LANE)), _full((LANE, hw)), _full((1, LANE)),
                  _full((DIL_W, LANE)), _full((LANE, DIL_W)), _full((1, LANE)), _full((hw, LANE)), _full((DIL_W, LANE))],
        out_specs=[row(hw), row(hw + DIL_W), row(LANE), row(DIL_W), row(DIL_W), row(DIL_W), _full((8, LANE))],
        out_shape=[jax.ShapeDtypeStruct((s, hw), BF16), jax.ShapeDtypeStruct((s, hw + DIL_W), BF16),
                   jax.ShapeDtypeStruct((s, LANE), BF16)] + [jax.ShapeDtypeStruct((s, DIL_W), BF16)] * 3
        + [jax.ShapeDtypeStruct((8, LANE), F32)],
        scratch_shapes=[pltpu.VMEM((8, hw), F32)],
        compiler_params=_params(("arbitrary",), 28 << 20),
    )(dqm, dkm, dvm, dqd, dkd, dvd, q_raw, kv_raw, proj, proj, proj, tab,
      gains["q"], gains["k"], gains["kpe"], gains["dq"], gains["dk"],
      c["seg_q"], c["exp_q"], c["inv_q"], c["seg_k"], c["exp_k"], c["inv_k"], c["seg_d"], c["exp_d"], c["inv_d"],
      c["fold_q"], c["fold_d"])


def _latnorm_bwd(dql, dkvl, proj, g_q, g_kv):
    s = proj.shape[0]
    n_steps = s // ROW_TILE

    def body(dql_ref, dkvl_ref, q_ref, kv_ref, gq_ref, gkv_ref, dq_ref, dkv_ref, dg_ref):
        i = pl.program_id(0)

        @pl.when(i == 0)
        def _():
            dg_ref[...] = jnp.zeros_like(dg_ref)

        def one(x, dyg, gain):
            r = _rms(x)
            xn = x * r
            dxn = dyg * gain
            dx = r * (dxn - xn * jnp.mean(dxn * xn, axis=-1, keepdims=True))
            return dx, jnp.sum(dyg * xn, axis=0, keepdims=True)

        dq, gq_l = one(q_ref[...], dql_ref[...], gq_ref[...])
        dkv, gkv_l = one(kv_ref[...], dkvl_ref[...], gkv_ref[...])
        dq_ref[...] = dq.astype(BF16)
        dkv_ref[...] = dkv.astype(BF16)
        dg_ref[0:1, :] += gq_l
        dg_ref[1:2, 0:KV_LORA] += gkv_l

    t = ROW_TILE
    return pl.pallas_call(
        body, name="latnorm_bwd", grid=(n_steps,),
        in_specs=[pl.BlockSpec((t, Q_LORA), lambda i: (i, 0)), pl.BlockSpec((t, KV_LORA), lambda i: (i, 0)),
                  pl.BlockSpec((t, Q_LORA), lambda i: (i, P_QLAT // Q_LORA)),
                  pl.BlockSpec((t, KV_LORA), lambda i: (i, P_KVLAT // KV_LORA)),
                  _full((1, Q_LORA)), _full((1, KV_LORA))],
        out_specs=[pl.BlockSpec((t, Q_LORA), lambda i: (i, 0)), pl.BlockSpec((t, KV_LORA), lambda i: (i, 0)), _full((8, Q_LORA))],
        out_shape=[jax.ShapeDtypeStruct((s, Q_LORA), BF16), jax.ShapeDtypeStruct((s, KV_LORA), BF16),
                   jax.ShapeDtypeStruct((8, Q_LORA), F32)],
        compiler_params=_params(("arbitrary",)),
    )(dql, dkvl, proj, proj, g_q, g_kv)


def _resid_prenorm(x, mix, g1, gain, scale, shift):
    s, d = x.shape

    def body(x_ref, mix_ref, g1_ref, g_ref, sc_ref, sh_ref, x1_ref, h_ref):
        x1 = x_ref[...] + g1_ref[...] * mix_ref[...]
        x1_ref[...] = x1
        h_ref[...] = ((x1 * _rms(x1)) * g_ref[...] * (1.0 + sc_ref[...]) + sh_ref[...]).astype(BF16)

    row = pl.BlockSpec((ROW_TILE, d), lambda i: (i, 0))
    vec = _full((1, d))
    return pl.pallas_call(
        body, name="resid_prenorm", grid=(s // ROW_TILE,),
        in_specs=[row, row, vec, vec, vec, vec], out_specs=[row, row],
        out_shape=[jax.ShapeDtypeStruct((s, d), F32), jax.ShapeDtypeStruct((s, d), BF16)],
        compiler_params=_params(("parallel",)),
    )(x, mix, g1, gain, scale, shift)


CONV_TILE = 1408
HALO = 8


def _shift_down(x, halo, k):
    t = x.shape[0]
    row = lax.broadcasted_iota(I32, (t, 1), 0)
    out = pltpu.roll(x, k, 0)
    for r in range(k):
        out = jnp.where(row == r, halo[HALO - k + r:HALO - k + r + 1, :], out)
    return out


def _shift_up(x, halo, k):
    t = x.shape[0]
    row = lax.broadcasted_iota(I32, (t, 1), 0)
    out = pltpu.roll(x, t - k, 0)
    for r in range(k):
        out = jnp.where(row == t - k + r, halo[r:r + 1, :], out)
    return out


def _conv_fwd(x, halo, w, b):
    p1, p2 = _shift_down(x, halo, 1), _shift_down(x, halo, 2)
    u = b + p2 * w[0:1, :]
    u = u + p1 * w[1:2, :]
    u = u + x * w[2:3, :]
    return u, p1, p2


def _sigmoid(x):
    return 1.0 / (1.0 + jnp.exp(-x))


def _conv_gate(up, w_conv, b_conv):
    s = up.shape[0]
    t = ROW_TILE
    nj = D_FF // CONV_TILE
    hb = t // HALO

    def body(g_ref, v_ref, gh_ref, vh_ref, wg_ref, wv_ref, bg_ref, bv_ref, a_ref):
        live = (pl.program_id(0) > 0).astype(F32)
        ug, _, _ = _conv_fwd(g_ref[...], gh_ref[...] * live, wg_ref[...], bg_ref[...])
        uv, _, _ = _conv_fwd(v_ref[...], vh_ref[...] * live, wv_ref[...], bv_ref[...])
        a_ref[...] = (ug * _sigmoid(ug) * uv).astype(BF16)

    main = lambda off: pl.BlockSpec((t, CONV_TILE), lambda i, j: (i, j + off))
    halo = lambda off: pl.BlockSpec((HALO, CONV_TILE), lambda i, j: (jnp.maximum(i * hb - 1, 0), j + off))
    wsp = lambda off: pl.BlockSpec((3, CONV_TILE), lambda i, j: (0, j + off))
    bsp = lambda off: pl.BlockSpec((1, CONV_TILE), lambda i, j: (0, j + off))
    return pl.pallas_call(
        body, name="conv_gate", grid=(s // t, nj),
        in_specs=[main(0), main(nj), halo(0), halo(nj), wsp(0), wsp(nj), bsp(0), bsp(nj)],
        out_specs=pl.BlockSpec((t, CONV_TILE), lambda i, j: (i, j)),
        out_shape=jax.ShapeDtypeStruct((s, D_FF), BF16),
        compiler_params=_params(("parallel", "parallel"), 12 << 20),
    )(up, up, up, up, w_conv, w_conv, b_conv, b_conv)


def _gate_bwd(up, da, w_conv, b_conv):
    s = up.shape[0]
    t = ROW_TILE
    nj = D_FF // CONV_TILE
    hb = t // HALO
    n_i = s // t

    def body(g_ref, v_ref, gh_ref, vh_ref, gn_ref, vn_ref, da_ref, dan_ref, wg_ref, wv_ref, bg_ref, bv_ref,
             dupg_ref, dupv_ref, dbg_ref, dbv_ref, dwg_ref, dwv_ref):
        i = pl.program_id(1)

        @pl.when(i == 0)
        def _():
            for r in (dbg_ref, dbv_ref, dwg_ref, dwv_ref):
                r[...] = jnp.zeros_like(r)

        def d_gate(ug, uv, da_v):
            sg = _sigmoid(ug)
            return da_v * uv * (sg * (1.0 + ug * (1.0 - sg))), da_v * (ug * sg)

        live = (i > 0).astype(F32)
        xg, xv = g_ref[...], v_ref[...]
        wg, wv = wg_ref[...], wv_ref[...]
        ug, g1, g2 = _conv_fwd(xg, gh_ref[...] * live, wg, bg_ref[...])
        uv, v1, v2 = _conv_fwd(xv, vh_ref[...] * live, wv, bv_ref[...])
        dug, duv = d_gate(ug, uv, da_ref[...])

        more = (i < n_i - 1).astype(F32)
        ug_n, _, _ = _conv_fwd(gn_ref[...], xg[t - HALO:, :], wg, bg_ref[...])
        uv_n, _, _ = _conv_fwd(vn_ref[...], xv[t - HALO:, :], wv, bv_ref[...])
        dug_n, duv_n = d_gate(ug_n, uv_n, dan_ref[...] * more)

        def conv_t(du, du_n, w):
            return du * w[2:3, :] + _shift_up(du, du_n, 1) * w[1:2, :] + _shift_up(du, du_n, 2) * w[0:1, :]

        dupg_ref[...] = conv_t(dug, dug_n, wg).astype(BF16)
        dupv_ref[...] = conv_t(duv, duv_n, wv).astype(BF16)
        csum = lambda z: jnp.sum(z, axis=0, keepdims=True)
        dbg_ref[...] += csum(dug)
        dbv_ref[...] += csum(duv)
        dwg_ref[0:1, :] += csum(dug * g2)
        dwg_ref[1:2, :] += csum(dug * g1)
        dwg_ref[2:3, :] += csum(dug * xg)
        dwv_ref[0:1, :] += csum(duv * v2)
        dwv_ref[1:2, :] += csum(duv * v1)
        dwv_ref[2:3, :] += csum(duv * xv)

    last_halo = s // HALO - 1
    main = lambda off: pl.BlockSpec((t, CONV_TILE), lambda j, i: (i, j + off))
    halo = lambda off: pl.BlockSpec((HALO, CONV_TILE), lambda j, i: (jnp.maximum(i * hb - 1, 0), j + off))
    nxt = lambda off: pl.BlockSpec((HALO, CONV_TILE), lambda j, i: (jnp.minimum((i + 1) * hb, last_halo), j + off))
    wsp = lambda off: pl.BlockSpec((3, CONV_TILE), lambda j, i: (0, j + off))
    bsp = lambda off: pl.BlockSpec((1, CONV_TILE), lambda j, i: (0, j + off))
    outs = pl.pallas_call(
        body, name="gate_bwd", grid=(nj, n_i),
        in_specs=[main(0), main(nj), halo(0), halo(nj), nxt(0), nxt(nj), main(0), nxt(0),
                  wsp(0), wsp(nj), bsp(0), bsp(nj)],
        out_specs=[main(0), main(0),
                   pl.BlockSpec((1, CONV_TILE), lambda j, i: (0, j)), pl.BlockSpec((1, CONV_TILE), lambda j, i: (0, j)),
                   pl.BlockSpec((3, CONV_TILE), lambda j, i: (0, j)), pl.BlockSpec((3, CONV_TILE), lambda j, i: (0, j))],
        out_shape=[jax.ShapeDtypeStruct((s, D_FF), BF16), jax.ShapeDtypeStruct((s, D_FF), BF16),
                   jax.ShapeDtypeStruct((1, D_FF), F32), jax.ShapeDtypeStruct((1, D_FF), F32),
                   jax.ShapeDtypeStruct((3, D_FF), F32), jax.ShapeDtypeStruct((3, D_FF), F32)],
        compiler_params=_params(("parallel", "arbitrary"), 24 << 20),
    )(up, up, up, up, up, up, da, da, w_conv, w_conv, b_conv, b_conv)
    return outs


def _final(x1, ffn, tgt, g2):
    s, d = x1.shape
    n_steps = s // ROW_TILE

    def body(x1_ref, f_ref, t_ref, g2_ref, dy_ref, df_ref, dg2_ref, loss_ref, lacc_ref):
        i = pl.program_id(0)

        @pl.when(i == 0)
        def _():
            dg2_ref[...] = jnp.zeros_like(dg2_ref)
            lacc_ref[...] = jnp.zeros_like(lacc_ref)

        f = f_ref[...]
        e = x1_ref[...] + g2_ref[...] * f - t_ref[...]
        dy = e * (1.0 / d)
        dy_ref[...] = dy
        df_ref[...] = (dy * g2_ref[...]).astype(BF16)
        dg2_ref[...] += jnp.sum(dy * f, axis=0, keepdims=True)
        lacc_ref[...] += jnp.sum(e * e, axis=0, keepdims=True)

        @pl.when(i == n_steps - 1)
        def _():
            loss_ref[...] = jnp.sum(lacc_ref[...], axis=1, keepdims=True) * (0.5 / d)

    row = pl.BlockSpec((ROW_TILE, d), lambda i: (i, 0))
    return pl.pallas_call(
        body, name="final", grid=(n_steps,),
        in_specs=[row, row, row, _full((1, d))],
        out_specs=[row, row, _full((1, d)), _full((1, 1))],
        out_shape=[jax.ShapeDtypeStruct((s, d), F32), jax.ShapeDtypeStruct((s, d), BF16),
                   jax.ShapeDtypeStruct((1, d), F32), jax.ShapeDtypeStruct((1, 1), F32)],
        scratch_shapes=[pltpu.VMEM((1, d), F32)],
        compiler_params=_params(("arbitrary",)),
    )(x1, ffn, tgt, g2)


def _ffnnorm_bwd(dh2, x1, dy, mix, gain, scale, g1):
    s, d = x1.shape
    n_steps = s // ROW_TILE

    def body(dh_ref, x_ref, dy_ref, mix_ref, g_ref, sc_ref, g1_ref, dx_ref, dm_ref, acc_ref):
        i = pl.program_id(0)

        @pl.when(i == 0)
        def _():
            acc_ref[...] = jnp.zeros_like(acc_ref)

        dh, x = dh_ref[...], x_ref[...]
        r = _rms(x)
        xn = x * r
        dn = dh * (1.0 + sc_ref[...])
        dxn = dn * g_ref[...]
        dx = dy_ref[...] + r * (dxn - xn * jnp.mean(dxn * xn, axis=-1, keepdims=True))
        dx_ref[...] = dx
        dm_ref[...] = (dx * g1_ref[...]).astype(BF16)
        csum = lambda z: jnp.sum(z, axis=0, keepdims=True)
        acc_ref[0:1, :] += csum(dh)
        acc_ref[1:2, :] += csum(dh * (xn * g_ref[...]))
        acc_ref[2:3, :] += csum(dn * xn)
        acc_ref[3:4, :] += csum(dx * mix_ref[...])

    row = pl.BlockSpec((ROW_TILE, d), lambda i: (i, 0))
    vec = _full((1, d))
    return pl.pallas_call(
        body, name="ffnnorm_bwd", grid=(n_steps,),
        in_specs=[row, row, row, row, vec, vec, vec],
        out_specs=[row, row, _full((8, d))],
        out_shape=[jax.ShapeDtypeStruct((s, d), F32), jax.ShapeDtypeStruct((s, d), BF16), jax.ShapeDtypeStruct((8, d), F32)],
        compiler_params=_params(("arbitrary",)),
    )(dh2, x1, dy, mix, gain, scale, g1)


def _mixnorm_bwd(dh, x, dx1, gain, scale):
    s, d = x.shape
    n_steps = s // ROW_TILE

    def body(dh_ref, x_ref, dx1_ref, g_ref, sc_ref, gx_ref, acc_ref):
        i = pl.program_id(0)

        @pl.when(i == 0)
        def _():
            acc_ref[...] = jnp.zeros_like(acc_ref)

        dh, x = dh_ref[...], x_ref[...]
        r = _rms(x)
        xn = x * r
        dn = dh * (1.0 + sc_ref[...])
        dxn = dn * g_ref[...]
        gx_ref[...] = dx1_ref[...] + r * (dxn - xn * jnp.mean(dxn * xn, axis=-1, keepdims=True))
        csum = lambda z: jnp.sum(z, axis=0, keepdims=True)
        acc_ref[0:1, :] += csum(dh)
        acc_ref[1:2, :] += csum(dh * (xn * g_ref[...]))
        acc_ref[2:3, :] += csum(dn * xn)

    row = pl.BlockSpec((ROW_TILE, d), lambda i: (i, 0))
    vec = _full((1, d))
    return pl.pallas_call(
        body, name="mixnorm_bwd", grid=(n_steps,),
        in_specs=[row, row, row, vec, vec],
        out_specs=[row, _full((8, d))],
        out_shape=[jax.ShapeDtypeStruct((s, d), F32), jax.ShapeDtypeStruct((8, d), F32)],
        compiler_params=_params(("arbitrary",)),
    )(dh, x, dx1, gain, scale)


def _key_count(d, dilated):
    if not dilated:
        return jnp.where(d >= 0, 1.0, 0.0)
    one = lambda cond: jnp.where(cond, 1.0, 0.0)
    cnt = one(d <= 128) + one(((d & 3) == 0) & (d <= 512)) + one((d & 15) == 0)
    return jnp.where(d >= 0, cnt, 0.0)


def _block_kinds(mla):
    return (0, "diag", "none") if mla else (512, "near", "far")


def _scores_t(ka, qa, scale, kind, rel_t, offset):
    st = lax.dot_general(ka, qa, NT, preferred_element_type=F32) * (scale * LOG2E)
    cnt = None
    if kind == "diag":
        st = jnp.where(rel_t + offset >= 0, st, NEG_INF)
    elif kind == "far":
        st = jnp.where((rel_t & 15) == 0, st, NEG_INF)
    elif kind == "near":
        cnt = _key_count(rel_t + offset, True)
        st = jnp.where(cnt > 0.0, st, NEG_INF)
    return st, cnt


def _attn_fwd(q, k, v, mla, scale, name, gather=()):
    s = q.shape[0]
    qw = 2 * LANE if mla else LANE
    tq, tk = ATT_TQ, ATT_TK
    reach, kind_near, kind_far = _block_kinds(mla)
    assert s % tq == 0 and tq % tk == 0 and reach % tk == 0
    ng = len(gather)
    last_step = HEADS // 2 - 1

    def body(*refs):
        q_ref, k_ref, v_ref = refs[:3]
        o_ref, lse_ref = refs[3 + ng:5 + ng]
        vt_ref = refs[5 + 2 * ng]
        comm = (refs[3:3 + ng], refs[5 + ng:5 + 2 * ng]) + tuple(refs[6 + 2 * ng:])
        if ng:
            @pl.when(pl.program_id(0) == 0)
            def _():
                _Gather(*comm).start()

            @pl.when(pl.program_id(0) == last_step)
            def _():
                _Gather(*comm).forward()

        lane = lax.broadcasted_iota(I32, (1, LANE), 1)
        rel_t = lax.broadcasted_iota(I32, (tk, tq), 1) - lax.broadcasted_iota(I32, (tk, tq), 0)

        def transpose_v(j, carry):
            c0 = pl.multiple_of(j * tk, tk)
            vt_ref[:, pl.ds(c0, tk)] = v_ref[pl.ds(c0, tk), :].astype(F32).T.astype(BF16)
            return carry

        lax.fori_loop(0, s // tk, transpose_v, 0)

        def q_block(qi, carry):
            r0 = pl.multiple_of(qi * tq, tq)
            kcols = [slice(a * LANE, (a + 1) * LANE) if mla else slice(0, LANE) for a in range(2)]
            qas = [q_ref[pl.ds(r0, tq), kcols[a]] for a in range(2)]
            if not mla:
                qas = [jnp.where(lane < DIL_DIM, qas[0], jnp.zeros_like(qas[0])),
                       jnp.where(lane >= DIL_DIM, qas[1], jnp.zeros_like(qas[1]))]

            def k_block(kj, c, kind):
                c0 = pl.multiple_of(kj * tk, tk)
                out = []
                for a in range(2):
                    m, l, acc = c[a]
                    st, cnt = _scores_t(k_ref[pl.ds(c0, tk), kcols[a]], qas[a], scale, kind, rel_t, r0 - c0)
                    m_new = jnp.maximum(m, jnp.max(st, axis=0, keepdims=True))
                    alpha = jnp.exp2(m - m_new)
                    p = jnp.exp2(st - m_new)
                    if cnt is not None:
                        p = p * cnt
                    l = alpha * l + jnp.sum(p, axis=0, keepdims=True)
                    vt = vt_ref[a * DIL_DIM:(a + 1) * DIL_DIM, pl.ds(c0, tk)]
                    acc = alpha * acc + jnp.dot(vt, p.astype(BF16), preferred_element_type=F32)
                    out.append((m_new, l, acc))
                return tuple(out)

            one = (jnp.full((1, tq), NEG_INF, F32), jnp.zeros((1, tq), F32), jnp.zeros((DIL_DIM, tq), F32))
            first_near = jnp.maximum((r0 - reach) // tk, 0)
            c = lax.fori_loop(0, first_near, functools.partial(k_block, kind=kind_far), (one, one))
            res = lax.fori_loop(first_near, (r0 + tq) // tk, functools.partial(k_block, kind=kind_near), c)
            o_t = jnp.concatenate([res[a][2] / res[a][1] for a in range(2)], axis=0)
            o_ref[pl.ds(r0, tq), :] = o_t.T.astype(BF16)
            for a in range(2):
                lse_ref[a, :, pl.ds(r0, tq)] = res[a][0] * LN2 + jnp.log(res[a][1])
            return carry

        lax.fori_loop(0, s // tq, q_block, 0)

        if ng:
            @pl.when(pl.program_id(0) == last_step)
            def _():
                _Gather(*comm).finish()

    return pl.pallas_call(
        body, name=name, grid=(HEADS // 2,),
        in_specs=[pl.BlockSpec((s, qw), lambda h: (0, h)), pl.BlockSpec((s, qw), lambda h: (0, h)),
                  pl.BlockSpec((s, LANE), lambda h: (0, h))] + [ANY] * ng,
        out_specs=[pl.BlockSpec((s, LANE), lambda h: (0, h)), pl.BlockSpec((2, 1, s), lambda h: (h, 0, 0))] + [ANY] * ng,
        out_shape=[jax.ShapeDtypeStruct((s, DIL_W), BF16), jax.ShapeDtypeStruct((HEADS, 1, s), F32)] + _Gather.out_shapes(gather),
        scratch_shapes=[pltpu.VMEM((LANE, s), BF16)] + (_Gather.semaphores(ng) if ng else []),
        compiler_params=_params(("arbitrary",) if ng else ("parallel",), 12 << 20),
    )(q, k, v, *gather)


def _attn_bwd(q, k, v, o, do, do_block0, lse, mla, scale, name, scatter=()):
    s = q.shape[0]
    qw = 2 * LANE if mla else LANE
    tq, tk = ATT_TQ, ATT_TK
    nq = s // tq
    reach, kind_near, kind_far = _block_kinds(mla)
    assert s % tq == 0 and tq % tk == 0
    ns = len(scatter)
    last_step = HEADS // 2 - 1

    def body(*refs):
        q_ref, k_ref, v_ref, o_ref, do_ref, lse_ref = refs[:6]
        dq_ref, dk_ref, dv_ref = refs[6 + ns:9 + ns]
        kt_ref, dot_ref, dob_ref, dqt_ref, delta_ref, lse2_ref = refs[9 + 2 * ns:15 + 2 * ns]
        comm = (refs[6:6 + ns], refs[9 + ns:9 + 2 * ns]) + tuple(refs[15 + 2 * ns:])
        if ns:
            @pl.when(pl.program_id(0) == 0)
            def _():
                _Scatter(*comm).start()

        lane = lax.broadcasted_iota(I32, (1, LANE), 1)
        row = lax.broadcasted_iota(I32, (LANE, 1), 0)
        rel_t = lax.broadcasted_iota(I32, (tk, tq), 1) - lax.broadcasted_iota(I32, (tk, tq), 0)

        def prepare(j, carry):
            c0 = pl.multiple_of(j * tk, tk)
            do_blk = do_ref[pl.ds(c0, tk), :]
            dob_ref[pl.ds(c0, tk), :] = do_blk.astype(BF16)
            do_t = do_blk.T
            dot_ref[:, pl.ds(c0, tk)] = do_t.astype(BF16)
            prod = do_t * o_ref[pl.ds(c0, tk), :].astype(F32).T
            delta_ref[0, :, pl.ds(c0, tk)] = jnp.sum(prod[0:DIL_DIM], axis=0, keepdims=True)
            delta_ref[1, :, pl.ds(c0, tk)] = jnp.sum(prod[DIL_DIM:LANE], axis=0, keepdims=True)
            for w in range(qw // LANE):
                kt_ref[w * LANE:(w + 1) * LANE, pl.ds(c0, tk)] = (
                    k_ref[pl.ds(c0, tk), w * LANE:(w + 1) * LANE].astype(F32).T.astype(BF16))
            return carry

        lax.fori_loop(0, s // tk, prepare, 0)
        dqt_ref[...] = jnp.zeros_like(dqt_ref)
        lse2_ref[...] = lse_ref[...] * LOG2E

        sels = [lane < DIL_DIM, lane >= DIL_DIM]
        rsels = [row < DIL_DIM, row >= DIL_DIM]
        cols = [slice(a * LANE, (a + 1) * LANE) if mla else slice(0, LANE) for a in range(2)]

        def k_block(kj, carry):
            c0 = pl.multiple_of(kj * tk, tk)
            kas = [k_ref[pl.ds(c0, tk), cols[a]] for a in range(2)]
            kts = [kt_ref[cols[a], pl.ds(c0, tk)] for a in range(2)]
            if not mla:
                kas = [jnp.where(sels[a], kas[a], jnp.zeros_like(kas[a])) for a in range(2)]
                kts = [jnp.where(rsels[a], kts[a], jnp.zeros_like(kts[a])) for a in range(2)]
            vb = v_ref[pl.ds(c0, tk), :]
            vbs = [jnp.where(sels[a], vb, jnp.zeros_like(vb)) for a in range(2)]

            def q_block(qi, c, kind):
                r0 = pl.multiple_of(qi * tq, tq)
                out, dq_parts = [], []
                for a in range(2):
                    dk_acc, dv_acc = c[a]
                    qa = q_ref[pl.ds(r0, tq), cols[a]]
                    st, cnt = _scores_t(kas[a], qa, scale, kind, rel_t, r0 - c0)
                    p = jnp.exp2(st - lse2_ref[a, :, pl.ds(r0, tq)])
                    if cnt is not None:
                        p = p * cnt
                    dp = jnp.dot(vbs[a], dot_ref[:, pl.ds(r0, tq)], preferred_element_type=F32)
                    ds = (p * (dp - delta_ref[a, :, pl.ds(r0, tq)]) * scale).astype(BF16)
                    dv_acc = dv_acc + jnp.dot(p.astype(BF16), dob_ref[pl.ds(r0, tq), :], preferred_element_type=F32)
                    dk_acc = dk_acc + jnp.dot(ds, qa, preferred_element_type=F32)
                    dq_parts.append(jnp.dot(kts[a], ds, preferred_element_type=F32))
                    out.append((dk_acc, dv_acc))
                if mla:
                    for a in range(2):
                        dqt_ref[cols[a], pl.ds(r0, tq)] += dq_parts[a]
                else:
                    dqt_ref[:, pl.ds(r0, tq)] += dq_parts[0] + dq_parts[1]
                return tuple(out)

            zero = jnp.zeros((tk, LANE), F32)
            last_near = jnp.minimum((c0 + tk - 1 + reach) // tq + 1, nq)
            c = lax.fori_loop(c0 // tq, last_near, functools.partial(q_block, kind=kind_near), ((zero, zero), (zero, zero)))
            (dk0, dv0), (dk1, dv1) = lax.fori_loop(last_near, nq, functools.partial(q_block, kind=kind_far), c)
            if mla:
                dk_ref[pl.ds(c0, tk), cols[0]] = dk0
                dk_ref[pl.ds(c0, tk), cols[1]] = dk1
            else:
                dk_ref[pl.ds(c0, tk), :] = jnp.where(sels[0], dk0, dk1)
            dv_ref[pl.ds(c0, tk), :] = jnp.where(sels[0], dv0, dv1)
            return carry

        lax.fori_loop(0, s // tk, k_block, 0)

        def write_dq(j, carry):
            c0 = pl.multiple_of(j * tk, tk)
            for w in range(qw // LANE):
                dq_ref[pl.ds(c0, tk), w * LANE:(w + 1) * LANE] = dqt_ref[w * LANE:(w + 1) * LANE, pl.ds(c0, tk)].T
            return carry

        lax.fori_loop(0, s // tk, write_dq, 0)

        if ns:
            @pl.when(pl.program_id(0) == last_step)
            def _():
                _Scatter(*comm).finish()

    b0 = do_block0
    return pl.pallas_call(
        body, name=name, grid=(HEADS // 2,),
        in_specs=[pl.BlockSpec((s, qw), lambda h: (0, h)), pl.BlockSpec((s, qw), lambda h: (0, h)),
                  pl.BlockSpec((s, LANE), lambda h: (0, h)), pl.BlockSpec((s, LANE), lambda h: (0, h)),
                  pl.BlockSpec((s, LANE), lambda h: (0, h + b0)), pl.BlockSpec((2, 1, s), lambda h: (h, 0, 0))] + [ANY] * ns,
        out_specs=[pl.BlockSpec((s, qw), lambda h: (0, h)), pl.BlockSpec((s, qw), lambda h: (0, h)),
                   pl.BlockSpec((s, LANE), lambda h: (0, h))] + [ANY] * ns,
        out_shape=[jax.ShapeDtypeStruct(q.shape, F32), jax.ShapeDtypeStruct(k.shape, F32), jax.ShapeDtypeStruct((s, DIL_W), F32)]
        + _Scatter.out_shapes(scatter),
        scratch_shapes=[pltpu.VMEM((qw, s), BF16), pltpu.VMEM((LANE, s), BF16), pltpu.VMEM((s, LANE), BF16),
                        pltpu.VMEM((qw, s), F32), pltpu.VMEM((2, 1, s), F32), pltpu.VMEM((2, 1, s), F32)]
        + (_Scatter.semaphores(ns) if ns else []),
        compiler_params=_params(("arbitrary",) if ns else ("parallel",), 24 << 20),
    )(q, k, v, o, do, lse, *scatter)


def _ada_fwd(c_all, w_shard, b_shard):
    n, d = c_all.shape
    cols = w_shard.shape[1]

    def body(c_ref, w_ref, b_ref, o_ref):
        cv = c_ref[...]
        sc = (cv * _sigmoid(cv)).astype(BF16)
        o_ref[...] = jnp.dot(sc, w_ref[...].astype(BF16), preferred_element_type=F32) + b_ref[...]

    return pl.pallas_call(
        body, name="ada_fwd", out_shape=jax.ShapeDtypeStruct((n, cols), F32),
        compiler_params=_params(None, 16 << 20),
    )(c_all, w_shard, b_shard)


def _ada_bwd(c_all, dmod_shard):
    n, d = c_all.shape
    cols = dmod_shard.shape[1]

    def body(c_ref, g_ref, o_ref):
        cv = c_ref[...]
        o_ref[...] = lax.dot_general(cv * _sigmoid(cv), g_ref[...], TN, precision=HIGHEST, preferred_element_type=F32)

    return pl.pallas_call(
        body, name="ada_bwd", out_shape=jax.ShapeDtypeStruct((d, cols), F32),
        compiler_params=_params(None, 16 << 20),
    )(c_all, dmod_shard)


SMALL_WIDTHS = (("g_mix_norm", D_MODEL), ("g_q_lat", Q_LORA), ("g_kv_lat", KV_LORA), ("g_mla_q_nope", NOPE),
                ("g_mla_q_pe", ROPE), ("g_mla_k_nope", NOPE), ("g_mla_k_pe", ROPE), ("g_dil_q", DIL_DIM),
                ("g_dil_k", DIL_DIM), ("g_ffn_norm", D_MODEL), ("b_conv", UP_W))


def _small_layout():
    pieces = (("dmod", 6 * D_MODEL),) + SMALL_WIDTHS + tuple(("w_conv%d" % k, UP_W) for k in range(3))
    layout, off = {}, 0
    for name, width in pieces:
        layout[name] = (width, off)
        off += -(-width // LANE) * LANE
    return layout, off


def _pack_small(acc1, acc2, dg2, dglat, dgains, dbg, dbv, dwg, dwv):
    layout, total = _small_layout()

    def body(a1, a2, g2, gl, gg, bg, bv, wg, wv, o_ref):
        o_ref[...] = jnp.zeros_like(o_ref)

        def put(name, src, shift=0):
            start = layout[name][1] + shift
            o_ref[:, start:start + src.shape[1]] = src

        for k, src in enumerate((a1[0:1, :], a1[1:2, :], a2[3:4, :], a2[0:1, :], a2[1:2, :], g2[...])):
            put("dmod", src, k * D_MODEL)
        put("g_mix_norm", a1[2:3, :])
        put("g_q_lat", gl[0:1, :])
        put("g_kv_lat", gl[1:2, 0:KV_LORA])
        put("g_mla_q_nope", gg[0:1, 0:NOPE])
        put("g_mla_q_pe", gg[5:6, 0:ROPE])
        put("g_mla_k_nope", gg[1:2, 0:NOPE])
        put("g_mla_k_pe", gg[2:3, 0:ROPE])
        put("g_dil_q", gg[3:4, 0:DIL_DIM])
        put("g_dil_k", gg[4:5, 0:DIL_DIM])
        put("g_ffn_norm", a2[2:3, :])
        put("b_conv", bg[...])
        put("b_conv", bv[...], D_FF)
        for k in range(3):
            put("w_conv%d" % k, wg[k:k + 1, :])
            put("w_conv%d" % k, wv[k:k + 1, :], D_FF)

    ins = (acc1, acc2, dg2, dglat, dgains, dbg, dbv, dwg, dwv)
    return pl.pallas_call(
        body, name="pack_small", grid=(1,), in_specs=[_full(a.shape) for a in ins], out_specs=_full((1, total)),
        out_shape=jax.ShapeDtypeStruct((1, total), F32),
        compiler_params=_params(("arbitrary",), 2 << 20),
    )(*_in_hbm(*ins))


def _sum_unpack(g):
    n_dev, _, total = g.shape
    layout, _ = _small_layout()

    def body(g_ref, *refs):
        o_refs, s_ref = refs[:-1], refs[-1]
        acc = g_ref[0]
        for k in range(1, n_dev):
            acc = acc + g_ref[k]
        s_ref[...] = acc
        take = lambda name: s_ref[:, layout[name][1]:layout[name][1] + layout[name][0]]
        o_refs[0][...] = take("dmod")
        for i, (name, _) in enumerate(SMALL_WIDTHS):
            o_refs[1 + i][...] = take(name)
        for k in range(3):
            o_refs[-1][k:k + 1, :] = take("w_conv%d" % k)

    shapes = [(1, 6 * D_MODEL)] + [(1, w) for _, w in SMALL_WIDTHS] + [(3, UP_W)]
    return pl.pallas_call(
        body, name="sum_unpack", out_shape=[jax.ShapeDtypeStruct(sh, F32) for sh in shapes],
        scratch_shapes=[pltpu.VMEM((1, total), F32)],
        compiler_params=_params(None, 4 << 20),
    )(g)


def _adamw_math(w, g, m, v):
    mn = ADAM_B1 * m + (1.0 - ADAM_B1) * g
    vn = ADAM_B2 * v + (1.0 - ADAM_B2) * (g * g)
    m_hat = mn / (1.0 - ADAM_B1 ** ADAM_STEP)
    v_hat = vn / (1.0 - ADAM_B2 ** ADAM_STEP)
    return -ADAM_LR * (m_hat / (jnp.sqrt(v_hat) + ADAM_EPS) + ADAM_WD * w), mn, vn


def _adamw_vectors(ws, gs, ms, vs):
    k = len(ws)

    def body(*refs):
        for i in range(k):
            d, mn, vn = _adamw_math(refs[i][...], refs[k + i][...], refs[2 * k + i][...], refs[3 * k + i][...])
            refs[4 * k + i][...] = d
            refs[5 * k + i][...] = mn
            refs[6 * k + i][...] = vn

    blocks = [_full(w.shape) for w in ws]
    outs = pl.pallas_call(
        body, name="adamw_vectors", grid=(1,), in_specs=blocks * 4, out_specs=blocks * 3,
        out_shape=[jax.ShapeDtypeStruct(w.shape, F32) for w in ws] * 3,
        compiler_params=_params(("arbitrary",), 2 << 20),
    )(*_in_hbm(*ws, *gs, *ms, *vs))
    return outs[:k], outs[k:2 * k], outs[2 * k:]


def _adamw(w, g, m, v, name):
    r, c = w.shape
    tr = r
    for cand in (256, 128, 64, 32, 16, 8):
        if r % cand == 0 and r > cand:
            tr = cand
            break

    def body(w_ref, g_ref, m_ref, v_ref, d_ref, mo_ref, vo_ref):
        d_ref[...], mo_ref[...], vo_ref[...] = _adamw_math(w_ref[...], g_ref[...], m_ref[...], v_ref[...])

    blk = pl.BlockSpec((tr, c), lambda i: (i, 0))
    return pl.pallas_call(
        body, name=name, grid=(r // tr,), in_specs=[blk] * 4, out_specs=[blk] * 3,
        out_shape=[jax.ShapeDtypeStruct((r, c), F32)] * 3,
        compiler_params=_params(("parallel",), 7 * _nbytes((tr, c), F32)),
    )(w, g, m, v)


def _position():
    return lax.axis_index("x"), lax.axis_index("y"), lax.axis_index("c")


def _other_chips(x, y):
    return [(1 - x, y, 2 * (1 - x) + y), (x, 1 - y, 2 * x + (1 - y)), (1 - x, 1 - y, 2 * (1 - x) + (1 - y))]


def _ag_small(v, name):
    r, w = v.shape

    def body(v_ref, out_ref, send_sems, recv_sems, local_sem):
        x, y, c = _position()
        me = 4 * x + 2 * y + c
        mine = pltpu.make_async_copy(v_ref, out_ref.at[me], local_sem)
        mine.start()
        peers = []
        for k in range(1, N_DEV):
            fx, fy, fc = (k >> 2) & 1, (k >> 1) & 1, k & 1
            px = 1 - x if fx else x
            py = 1 - y if fy else y
            pc = 1 - c if fc else c
            peers.append((px, py, pc))
        sends = []
        for k, peer in enumerate(peers):
            cp = pltpu.make_async_remote_copy(src_ref=v_ref, dst_ref=out_ref.at[me], send_sem=send_sems.at[k],
                                              recv_sem=recv_sems.at[k], device_id=peer, device_id_type=MESH)
            cp.start()
            sends.append(cp)
        for k, (px, py, pc) in enumerate(peers):
            pltpu.make_async_remote_copy(src_ref=v_ref, dst_ref=out_ref.at[4 * px + 2 * py + pc], send_sem=send_sems.at[k],
                                         recv_sem=recv_sems.at[k], device_id=(px, py, pc), device_id_type=MESH).wait_recv()
        for cp in sends:
            cp.wait_send()
        mine.wait()

    return pl.pallas_call(
        body, name=name,
        out_shape=jax.ShapeDtypeStruct((N_DEV, r, w), F32),
        in_specs=[pl.BlockSpec(memory_space=pltpu.VMEM)],
        out_specs=pl.BlockSpec(memory_space=pltpu.VMEM),
        scratch_shapes=[pltpu.SemaphoreType.DMA((N_DEV - 1,)), pltpu.SemaphoreType.DMA((N_DEV - 1,)), pltpu.SemaphoreType.DMA],
        compiler_params=_params(None, 10 * _nbytes((r, w), F32)),
    )(v)


ANY = pl.BlockSpec(memory_space=pl.ANY)


def _ag_weights(shards, name):
    n = len(shards)

    def body(*refs):
        gather = _Gather(refs[:n], refs[n:2 * n], *refs[2 * n:])
        gather.start()
        gather.forward()
        gather.finish()

    return pl.pallas_call(
        body, name=name,
        out_shape=_Gather.out_shapes(shards), in_specs=[ANY] * n, out_specs=[ANY] * n,
        scratch_shapes=_Gather.semaphores(n),
    )(*shards)


class _Gather:
    def __init__(self, w_refs, out_refs, send_sems, recv_sems):
        x, y, c = _position()
        q0 = 2 * x + y
        sibling = (x, y, 1 - c)
        self.ici, self.ici_in, self.fwd, self.fwd_in = [], [], [], []
        for k, (w_ref, out_ref) in enumerate(zip(w_refs, out_refs)):
            half = w_ref.shape[0] // 2

            def blk(q, e, out_ref=out_ref, half=half):
                return out_ref.at[q, pl.ds(pl.multiple_of(e * half, 16), half), :]

            def copy(src, dst, i, to):
                return pltpu.make_async_remote_copy(src_ref=src, dst_ref=dst, send_sem=send_sems.at[i], recv_sem=recv_sems.at[i],
                                                    device_id=to, device_id_type=MESH)

            src = w_ref.at[pl.ds(pl.multiple_of(c * half, 16), half), :]
            for j, (cx, cy, qj) in enumerate(_other_chips(x, y)):
                self.ici.append(copy(src, blk(q0, c), 6 * k + j, (cx, cy, c)))
                self.ici_in.append(copy(blk(qj, c), blk(qj, c), 6 * k + j, (cx, cy, c)))
                self.fwd.append(copy(blk(qj, c), blk(qj, c), 6 * k + 3 + j, sibling))
                self.fwd_in.append(copy(blk(qj, 1 - c), blk(qj, 1 - c), 6 * k + 3 + j, sibling))

    @staticmethod
    def out_shapes(shards):
        return [jax.ShapeDtypeStruct((N_CHIP,) + s.shape, s.dtype) for s in shards]

    @staticmethod
    def semaphores(n):
        return [pltpu.SemaphoreType.DMA((6 * n,)), pltpu.SemaphoreType.DMA((6 * n,))]

    def start(self):
        for cp in self.ici:
            cp.start()

    def forward(self):
        for arrived, onward in zip(self.ici_in, self.fwd):
            arrived.wait_recv()
            onward.start()

    def finish(self):
        for cp in self.fwd_in:
            cp.wait_recv()
        for cp in self.ici + self.fwd:
            cp.wait_send()


def _swap_halves_d2d(grads, name):
    n = len(grads)

    def body(*refs):
        g_refs, out_refs = refs[:n], refs[n:2 * n]
        send_sems, recv_sems = refs[2 * n:]
        x, y, c = _position()
        sibling = (x, y, 1 - c)
        cps = []
        for k in range(n):
            cp = pltpu.make_async_remote_copy(src_ref=g_refs[k].at[:, 1 - c], dst_ref=out_refs[k], send_sem=send_sems.at[k],
                                              recv_sem=recv_sems.at[k], device_id=sibling, device_id_type=MESH)
            cp.start()
            cps.append(cp)
        for cp in cps:
            cp.wait_recv()
        for cp in cps:
            cp.wait_send()

    return pl.pallas_call(
        body, name=name,
        out_shape=[jax.ShapeDtypeStruct((N_CHIP,) + g.shape[2:], g.dtype) for g in grads],
        in_specs=[ANY] * n, out_specs=[ANY] * n,
        scratch_shapes=[pltpu.SemaphoreType.DMA((n,)), pltpu.SemaphoreType.DMA((n,))],
    )(*grads)


def _pair_sum(g, a, c_idx, name):
    _, _, rh, cols = g.shape
    tr = rh
    for cand in (256, 128, 64, 32, 16):
        if rh % cand == 0 and rh > cand:
            tr = cand
            break

    def body(c_ref, g_ref, a_ref, o_ref):
        o_ref[...] = (g_ref[...] + a_ref[...]).astype(BF16)

    return pl.pallas_call(
        body, name=name,
        grid_spec=pltpu.PrefetchScalarGridSpec(
            num_scalar_prefetch=1, grid=(N_CHIP, rh // tr),
            in_specs=[pl.BlockSpec((None, None, tr, cols), lambda q, i, c_ref: (q, c_ref[0], i, 0)),
                      pl.BlockSpec((None, tr, cols), lambda q, i, c_ref: (q, i, 0))],
            out_specs=pl.BlockSpec((None, tr, cols), lambda q, i, c_ref: (q, i, 0))),
        out_shape=jax.ShapeDtypeStruct((N_CHIP, rh, cols), BF16),
        compiler_params=_params(("parallel", "parallel"), 10 * _nbytes((tr, cols), F32)),
    )(c_idx, g, a)


def _scatter_partials(parts, name):
    n = len(parts)

    def body(*refs):
        scatter = _Scatter(refs[:n], refs[n:2 * n], *refs[2 * n:])
        scatter.start()
        scatter.finish()

    return pl.pallas_call(
        body, name=name,
        out_shape=_Scatter.out_shapes(parts), in_specs=[ANY] * n, out_specs=[ANY] * n,
        scratch_shapes=_Scatter.semaphores(n),
    )(*parts)


class _Scatter:
    def __init__(self, p_refs, out_refs, send_sems, recv_sems):
        x, y, c = _position()
        self.copies = []
        for k, (p_ref, out_ref) in enumerate(zip(p_refs, out_refs)):
            for j, (cx, cy, qj) in enumerate(_other_chips(x, y)):
                self.copies.append(pltpu.make_async_remote_copy(
                    src_ref=p_ref.at[qj], dst_ref=out_ref.at[j], send_sem=send_sems.at[3 * k + j],
                    recv_sem=recv_sems.at[3 * k + j], device_id=(cx, cy, c), device_id_type=MESH))

    @staticmethod
    def out_shapes(parts):
        return [jax.ShapeDtypeStruct((3,) + p.shape[1:], p.dtype) for p in parts]

    @staticmethod
    def semaphores(n):
        return [pltpu.SemaphoreType.DMA((3 * n,)), pltpu.SemaphoreType.DMA((3 * n,))]

    def start(self):
        for cp in self.copies:
            cp.start()

    def finish(self):
        for cp in self.copies:
            cp.wait_recv()
        for cp in self.copies:
            cp.wait_send()


def _shard_sum(p, b, q_idx, name):
    _, rh, cols = p.shape
    tr = rh
    for cand in (256, 128, 64, 32, 16):
        if rh % cand == 0 and rh > cand:
            tr = cand
            break

    def body(q_ref, p_ref, b_ref, o_ref):
        acc = p_ref[...].astype(F32)
        for j in range(3):
            acc = acc + b_ref[j].astype(F32)
        o_ref[...] = acc

    return pl.pallas_call(
        body, name=name,
        grid_spec=pltpu.PrefetchScalarGridSpec(
            num_scalar_prefetch=1, grid=(rh // tr,),
            in_specs=[pl.BlockSpec((None, tr, cols), lambda i, q_ref: (q_ref[0], i, 0)),
                      pl.BlockSpec((3, tr, cols), lambda i, q_ref: (0, i, 0))],
            out_specs=pl.BlockSpec((tr, cols), lambda i, q_ref: (i, 0))),
        out_shape=jax.ShapeDtypeStruct((rh, cols), F32),
        compiler_params=_params(("parallel",), 8 * _nbytes((tr, cols), F32)),
    )(q_idx, p, b)


def _join_halves(halves):
    n = len(halves)

    def body(*refs):
        h_refs, out_refs = refs[:n], refs[n:2 * n]
        send_sems, recv_sems = refs[2 * n:]
        x, y, c = _position()
        sibling = (x, y, 1 - c)
        cps = []
        for k in range(n):
            cp = pltpu.make_async_remote_copy(src_ref=h_refs[k], dst_ref=out_refs[k], send_sem=send_sems.at[k],
                                              recv_sem=recv_sems.at[k], device_id=sibling, device_id_type=MESH)
            cp.start()
            cps.append(cp)
        for cp in cps:
            cp.wait_recv()
        for cp in cps:
            cp.wait_send()

    return pl.pallas_call(
        body, name="rs_join",
        out_shape=[jax.ShapeDtypeStruct(h.shape, h.dtype) for h in halves],
        in_specs=[ANY] * n, out_specs=[ANY] * n,
        scratch_shapes=[pltpu.SemaphoreType.DMA((n,)), pltpu.SemaphoreType.DMA((n,))],
    )(*halves)


def _cols_from_shards(g):
    q, r, cs = g.shape
    return jnp.transpose(g, (1, 0, 2)).reshape(r, q * cs)


def _cols_to_shards(w):
    r, cfull = w.shape
    return jnp.transpose(w.reshape(r, N_CHIP, cfull // N_CHIP), (1, 0, 2))


def _pad_w_in(w):
    z = lambda n: jnp.zeros((w.shape[0], n), w.dtype)
    q_lat, kv_lat, kpe = w[:, 0:512], w[:, 512:768], w[:, 768:800]
    qd, kd, vd = w[:, 800:1312], w[:, 1312:1824], w[:, 1824:2336]
    return jnp.concatenate([q_lat, qd, kd, vd, kv_lat, z(KPE_OFF), kpe, z(LANE - KPE_OFF - ROPE)], axis=1)


def _unpad_w_in(g):
    return jnp.concatenate([g[:, P_QLAT:P_QLAT + Q_LORA], g[:, P_KVLAT:P_KVLAT + KV_LORA],
                            g[:, P_KPE + KPE_OFF:P_KPE + KPE_OFF + ROPE], g[:, P_QD:P_QD + 3 * DIL_W]], axis=1)


def _pad_w_qb(w):
    w3 = w.reshape(Q_LORA, HEADS, NOPE + ROPE)
    return jnp.pad(w3, ((0, 0), (0, 0), (0, LANE - NOPE - ROPE))).reshape(Q_LORA, HEADS * LANE)


def _unpad_w_qb(g):
    return g.reshape(Q_LORA, HEADS, LANE)[:, :, :NOPE + ROPE].reshape(Q_LORA, HEADS * (NOPE + ROPE))


def _pad_w_kvb(w):
    w3 = w.reshape(KV_LORA, HEADS, 2 * NOPE)
    kp = jnp.pad(w3[:, :, :NOPE], ((0, 0), (0, 0), (0, LANE - NOPE))).reshape(KV_LORA, HEADS * LANE)
    return jnp.concatenate([kp, w3[:, :, NOPE:].reshape(KV_LORA, DIL_W)], axis=1)


def _unpad_w_kvb(g):
    gk = g[:, :HEADS * LANE].reshape(KV_LORA, HEADS, LANE)[:, :, :NOPE]
    gv = g[:, HEADS * LANE:].reshape(KV_LORA, HEADS, NOPE)
    return jnp.concatenate([gk, gv], axis=2).reshape(KV_LORA, HEADS * 2 * NOPE)


def _head_gains(g_q_nope, g_q_pe, g_k_nope, g_k_pe, g_dq, g_dk):
    z = lambda n: jnp.zeros((1, n), F32)
    q1 = jnp.concatenate([g_q_nope, g_q_pe, z(LANE - NOPE - ROPE)], axis=1)
    k1 = jnp.concatenate([g_k_nope, z(LANE - NOPE)], axis=1)
    kpe = jnp.concatenate([z(KPE_OFF), g_k_pe, z(LANE - KPE_OFF - ROPE)], axis=1)
    return dict(q=jnp.tile(q1, (1, HEADS)), k=jnp.tile(k1, (1, HEADS)), kpe=kpe,
                dq=jnp.tile(g_dq, (1, HEADS)), dk=jnp.tile(g_dk, (1, HEADS)))


def kernel(x, c, positions, w_ada, b_ada, g_mix_norm, w_in, g_q_lat, w_q_b, g_kv_lat, w_kv_b, g_mla_q_nope, g_mla_q_pe, g_mla_k_nope, g_mla_k_pe, g_dil_q, g_dil_k, w_o, g_ffn_norm, w_up, w_conv, b_conv, w_down, loss_target, m_w_ada, m_b_ada, m_g_mix_norm, m_w_in, m_g_q_lat, m_w_q_b, m_g_kv_lat, m_w_kv_b, m_g_mla_q_nope, m_g_mla_q_pe, m_g_mla_k_nope, m_g_mla_k_pe, m_g_dil_q, m_g_dil_k, m_w_o, m_g_ffn_norm, m_w_up, m_w_conv, m_b_conv, m_w_down, v_w_ada, v_b_ada, v_g_mix_norm, v_w_in, v_g_q_lat, v_w_q_b, v_g_kv_lat, v_w_kv_b, v_g_mla_q_nope, v_g_mla_q_pe, v_g_mla_k_nope, v_g_mla_k_pe, v_g_dil_q, v_g_dil_k, v_w_o, v_g_ffn_norm, v_w_up, v_w_conv, v_b_conv, v_w_down):
    args = dict(locals())
    weights = {n: args[n][0] for n in ("w_ada", "w_in", "w_q_b", "w_kv_b", "w_o", "w_up", "w_conv", "w_down")}
    small_w = {n: args[n] for n in ("b_ada",) + tuple(n for n, _ in SMALL_WIDTHS)}
    mom_m = {n[2:]: (args[n][0] if args[n].ndim == 3 else args[n]) for n in args if n.startswith("m_")}
    mom_v = {n[2:]: (args[n][0] if args[n].ndim == 3 else args[n]) for n in args if n.startswith("v_")}

    xi, yi, ci = _position()
    q0 = 2 * xi + yi
    me = 4 * xi + 2 * yi + ci
    xs, tgt = x[0], loss_target[0]
    s = xs.shape[0]
    consts = _seg_consts()
    c_idx, q_idx = jnp.reshape(ci, (1,)).astype(I32), jnp.reshape(q0, (1,)).astype(I32)

    def halves(g4):
        q, r, cc = g4.shape
        return g4.reshape(q, 2, r // 2, cc)

    c_all = _ag_small(c, "ag_c")[:, 0, :]
    ada_cols = w_ada.shape[2]
    b_shard = lax.dynamic_slice_in_dim(b_ada, q0 * ada_cols, ada_cols, axis=1)
    mod_blk = _ada_fwd(c_all, weights["w_ada"], b_shard)
    mod_all = _ag_small(mod_blk, "ag_mod").reshape(N_CHIP, 2, N_DEV, ada_cols)
    mod = lax.dynamic_index_in_dim(lax.dynamic_index_in_dim(mod_all, ci, 1, False), me, 1, False)
    mod = mod.reshape(1, N_CHIP * ada_cols)
    sh1, sc1, g1, sh2, sc2, g2 = [mod[:, k * D_MODEL:(k + 1) * D_MODEL] for k in range(6)]

    place_own = lambda gs, ws: [lax.dynamic_update_slice_in_dim(g, w[None], q0, axis=0) for g, w in zip(gs, ws)]
    own_first = [weights[n].astype(BF16) for n in ("w_in", "w_q_b", "w_kv_b")]
    own_later = [weights[n].astype(BF16) for n in ("w_o", "w_up", "w_down")]
    gathered = place_own(_ag_weights(own_first, "ag_weights"), own_first)
    w_in_p = _pad_w_in(_cols_from_shards(gathered[0]))
    w_qb_p = _pad_w_qb(_cols_from_shards(gathered[1]))
    w_kvb_p = _pad_w_kvb(_cols_from_shards(gathered[2]))
    w_conv_f = _ag_small(weights["w_conv"], "ag_wconv")
    w_conv_f = jnp.transpose(w_conv_f.reshape(N_CHIP, 2, 3, -1)[:, 0], (1, 0, 2)).reshape(3, UP_W)

    gains = _head_gains(g_mla_q_nope, g_mla_q_pe, g_mla_k_nope, g_mla_k_pe, g_dil_q, g_dil_k)
    tab = _rope_tables(positions.reshape(s, 1), *_rope_consts())

    h = _prenorm(xs, g_mix_norm, sc1, sh1, "prenorm")
    proj = _mm(h, w_in_p, "nn", F32, 512, P_COLS, "mm_in")
    ql, kvl = _latnorm(proj, g_q_lat, g_kv_lat)
    q_raw = _mm(ql, w_qb_p, "nn", F32, 512, HEADS * LANE, "mm_qb")
    kv_raw = _mm(kvl, w_kvb_p, "nn", F32, 512, HEADS * LANE + DIL_W, "mm_kvb")
    qm, km, vm, qd, kd, vd = _attn_prep(q_raw, kv_raw, proj, tab, gains, consts)
    scale_m, scale_d = (NOPE + ROPE) ** -0.5, DIL_DIM ** -0.5
    o_m, lse_m, *gathered = _attn_fwd(qm, km, vm, True, scale_m, "attn_mla", gather=own_later[:2])
    o_d, lse_d, *gathered_d = _attn_fwd(qd, kd, vd, False, scale_d, "attn_dil", gather=own_later[2:])
    gathered = place_own(gathered + gathered_d, own_later)
    w_o_f = gathered[0].reshape(D_MODEL, D_MODEL)
    w_up_f = _cols_from_shards(gathered[1])
    w_down_f = gathered[2].reshape(D_FF, D_MODEL)
    mix_in = jnp.concatenate([o_m, o_d], axis=1)
    mix = _mm(mix_in, w_o_f, "nn", F32, 512, D_MODEL, "mm_o")
    x1, h2 = _resid_prenorm(xs, mix, g1, g_ffn_norm, sc2, sh2)
    up = _mm(h2, w_up_f, "nn", F32, 512, CONV_TILE, "mm_up")
    act = _conv_gate(up, w_conv_f, b_conv)
    ffn = _mm(act, w_down_f, "nn", F32, 256, D_MODEL, "mm_down")
    dy, dffn, dg2, loss_part = _final(x1, ffn, tgt, g2)

    da = _mm(dffn, w_down_f, "nt", F32, 512, CONV_TILE, "mm_down_dx")
    gw_down = _mm(act, dffn, "tn", F32, 256, D_MODEL, "mm_down_dw")
    dup_g, dup_v, dbg, dbv, dwg, dwv = _gate_bwd(up, da, w_conv_f, b_conv)
    dup = jnp.concatenate([dup_g, dup_v], axis=1)
    dh2 = _mm(dup, w_up_f, "nt", F32, 256, 512, "mm_up_dx")
    gw_up = _mm(h2, dup, "tn", F32, 512, CONV_TILE, "mm_up_dw")
    dx1, dmix, acc2 = _ffnnorm_bwd(dh2, x1, dy, mix, g_ffn_norm, sc2, g1)
    dmix_in = _mm(dmix, w_o_f, "nt", F32, 512, D_MODEL, "mm_o_dx")
    gw_o = _mm(mix_in, dmix, "tn", F32, 512, D_MODEL, "mm_o_dw")
    early_names = ("w_up", "w_down", "w_o")
    early = [halves(_cols_to_shards(gw_up)), halves(gw_down.reshape(N_CHIP, D_FF // N_CHIP, D_MODEL)),
             halves(gw_o.reshape(N_CHIP, D_MODEL // N_CHIP, D_MODEL))]
    early_sib = _swap_halves_d2d(early, "rs_pair_swap_early")
    early_sums = [_pair_sum(g, a, c_idx, "pair_sum_" + n) for g, a, n in zip(early, early_sib, early_names)]
    dqm, dkm, dvm, *early_recv = _attn_bwd(qm, km, vm, o_m, dmix_in, 0, lse_m, True, scale_m, "attn_mla_bwd",
                                           scatter=early_sums[:1])
    dqd, dkd, dvd, *early_recv_d = _attn_bwd(qd, kd, vd, o_d, dmix_in, DIL_W // LANE, lse_d, False, scale_d,
                                             "attn_dil_bwd", scatter=early_sums[1:])
    early_recv = early_recv + early_recv_d
    dq_raw, dkv_raw, dkpe_b, dqd_b, dkd_b, dvd_b, dgains = _attn_prep_bwd(
        dqm, dkm, dvm, dqd, dkd, dvd, q_raw, kv_raw, proj, tab, gains, consts)
    dql = _mm(dq_raw, w_qb_p, "nt", F32, 512, Q_LORA, "mm_qb_dx")
    gw_qb = _unpad_w_qb(_mm(ql, dq_raw, "tn", F32, Q_LORA, HEADS * LANE, "mm_qb_dw"))
    dkvl = _mm(dkv_raw, w_kvb_p, "nt", F32, 512, KV_LORA, "mm_kvb_dx")
    gw_kvb = _unpad_w_kvb(_mm(kvl, dkv_raw, "tn", F32, KV_LORA, HEADS * LANE + DIL_W, "mm_kvb_dw"))
    dqlat_b, dkvlat_b, dglat = _latnorm_bwd(dql, dkvl, proj, g_q_lat, g_kv_lat)
    dproj = jnp.concatenate([dqlat_b, dqd_b, dkd_b, dvd_b, dkvlat_b, dkpe_b], axis=1)
    dh = _mm(dproj, w_in_p, "nt", F32, 512, D_MODEL, "mm_in_dx")
    gw_in = _unpad_w_in(_mm(h, dproj, "tn", F32, 512, P_COLS, "mm_in_dw"))
    grad_x, acc1 = _mixnorm_bwd(dh, xs, dx1, g_mix_norm, sc1)

    packed = _pack_small(acc1, acc2, dg2, dglat, dgains, dbg, dbv, dwg, dwv)
    gathered_small = _ag_small(packed, "ag_small")
    grad_b_ada, *small_grads, gconv_full = _sum_unpack(gathered_small)
    grads = {"b_ada": grad_b_ada}
    grads.update({n: g for (n, _), g in zip(SMALL_WIDTHS, small_grads)})
    shard_cols = UP_W // N_CHIP
    grads["w_conv"] = lax.dynamic_slice_in_dim(gconv_full, q0 * shard_cols, shard_cols, axis=1)
    dmod_all = gathered_small[:, 0, :6 * D_MODEL]
    grads["w_ada"] = _ada_bwd(c_all, lax.dynamic_slice_in_dim(dmod_all, q0 * ada_cols, ada_cols, axis=1))

    late_names = ("w_in", "w_q_b", "w_kv_b")
    late = [halves(_cols_to_shards(gw_in)), halves(_cols_to_shards(gw_qb)), halves(_cols_to_shards(gw_kvb))]
    late_sib = _swap_halves_d2d(late, "rs_pair_swap_late")
    late_sums = [_pair_sum(g, a, c_idx, "pair_sum_" + n) for g, a, n in zip(late, late_sib, late_names)]
    late_recv = _scatter_partials(late_sums, "rs_scatter_late")
    big_names = late_names + early_names
    half_sums = [_shard_sum(p, b, q_idx, "shard_sum_" + n)
                 for p, b, n in zip(late_sums + early_sums, list(late_recv) + list(early_recv), big_names)]
    from_sib = _join_halves(half_sums)
    south = ci == 0
    for n, mine, theirs in zip(big_names, half_sums, from_sib):
        grads[n] = jnp.concatenate([jnp.where(south, mine, theirs), jnp.where(south, theirs, mine)], axis=0)

    delta, new_m, new_v = {}, {}, {}
    for n in ("w_ada", "w_in", "w_q_b", "w_kv_b", "w_o", "w_up", "w_conv", "w_down"):
        delta[n], new_m[n], new_v[n] = _adamw(weights[n], grads[n], mom_m[n], mom_v[n], "adamw_" + n)
    vec_names = ("b_ada",) + tuple(n for n, _ in SMALL_WIDTHS)
    sd, sm, sv = _adamw_vectors(*[[d_[n] for n in vec_names] for d_ in (small_w, grads, mom_m, mom_v)])
    for k, n in enumerate(vec_names):
        delta[n], new_m[n], new_v[n] = sd[k], sm[k], sv[k]

    loss = lax.psum(loss_part[0, 0], ("x", "y", "c"))
    order = ("w_ada", "b_ada", "g_mix_norm", "w_in", "g_q_lat", "w_q_b", "g_kv_lat", "w_kv_b", "g_mla_q_nope", "g_mla_q_pe",
             "g_mla_k_nope", "g_mla_k_pe", "g_dil_q", "g_dil_k", "w_o", "g_ffn_norm", "w_up", "w_conv", "b_conv", "w_down")
    lead = lambda n, z: z[None] if n.startswith("w_") else z
    outs = [loss, grad_x[None]]
    for d_ in (grads, delta, new_m, new_v):
        outs += [lead(n, d_[n]) for n in order]
    return tuple(outs)
```

```python
import functools

import numpy as np
import jax
import jax.numpy as jnp
from jax import lax
from jax.experimental import pallas as pl
from jax.experimental.pallas import tpu as pltpu

F32 = jnp.float32
BF16 = jnp.bfloat16
I32 = jnp.int32

D_MODEL = 1024
HEADS = 8
NOPE = 64
ROPE = 32
Q_LORA = 512
KV_LORA = 256
DIL_DIM = 64
DIL_W = HEADS * DIL_DIM
D_FF = 2816
UP_W = 2 * D_FF
IN_COLS = Q_LORA + KV_LORA + ROPE + 3 * DIL_W
ROPE_THETA = 10000.0
EPS = 1e-6
NEG_INF = -1e30
N_DEV = 8
N_CHIP = 4

ADAM_LR = 0.001
ADAM_B1 = 0.9
ADAM_B2 = 0.999
ADAM_EPS = 1e-08
ADAM_WD = 0.01
ADAM_STEP = 10

LANE = 128
ROW_TILE = 256
ATT_TQ = 512
ATT_TK = 256
LOG2E = 1.4426950408889634
LN2 = 0.6931471805599453
VMEM_CAP = 56 * 1024 * 1024
VMEM_FLOOR = 32 * 1024 * 1024

P_QLAT, P_QD, P_KD, P_VD, P_KVLAT, P_KPE = 0, 512, 1024, 1536, 2048, 2304
P_COLS = 2432
KPE_OFF = 64

NN = (((1,), (0,)), ((), ()))
NT = (((1,), (1,)), ((), ()))
TN = (((0,), (0,)), ((), ()))
HIGHEST = lax.Precision.HIGHEST
MESH = pl.DeviceIdType.MESH


def _params(sem=None, est_bytes=0):
    limit = int(min(max(2 * est_bytes + (4 << 20), VMEM_FLOOR), VMEM_CAP))
    if sem is None:
        return pltpu.CompilerParams(vmem_limit_bytes=limit)
    return pltpu.CompilerParams(dimension_semantics=sem, vmem_limit_bytes=limit)


def _nbytes(shape, dtype):
    return int(np.prod(shape)) * jnp.dtype(dtype).itemsize


def _in_hbm(*xs):
    return [pltpu.with_memory_space_constraint(x, pltpu.HBM) for x in xs]


def _mm(a, b, dims, out_dtype, tm, tn, name):
    if dims == "nn":
        (m, k), (k2, n) = a.shape, b.shape
        a_spec = pl.BlockSpec((tm, k), lambda i, j: (i, 0))
        b_spec = pl.BlockSpec((k, tn), lambda i, j: (0, j))
        dn = NN
    elif dims == "nt":
        (m, k), (n, k2) = a.shape, b.shape
        a_spec = pl.BlockSpec((tm, k), lambda i, j: (i, 0))
        b_spec = pl.BlockSpec((tn, k), lambda i, j: (j, 0))
        dn = NT
    else:
        (k, m), (k2, n) = a.shape, b.shape
        a_spec = pl.BlockSpec((k, tm), lambda i, j: (0, i))
        b_spec = pl.BlockSpec((k, tn), lambda i, j: (0, j))
        dn = TN
    assert k == k2 and m % tm == 0 and n % tn == 0, (name, a.shape, b.shape, tm, tn)

    def body(a_ref, b_ref, o_ref):
        o_ref[...] = lax.dot_general(a_ref[...], b_ref[...], dn, preferred_element_type=F32).astype(o_ref.dtype)

    est = _nbytes((tm, k), a.dtype) + _nbytes((tn, k), b.dtype) + _nbytes((tm, tn), F32) + _nbytes((tm, tn), out_dtype)
    return pl.pallas_call(
        body, name=name,
        grid=(m // tm, n // tn),
        in_specs=[a_spec, b_spec],
        out_specs=pl.BlockSpec((tm, tn), lambda i, j: (i, j)),
        out_shape=jax.ShapeDtypeStruct((m, n), out_dtype),
        compiler_params=_params(("parallel", "parallel"), est),
    )(a, b)


def _seg_consts():
    seg_q = np.zeros((HEADS * LANE, LANE), np.float32)
    inv_q = np.zeros((1, LANE), np.float32)
    seg_k = np.zeros((HEADS * LANE, LANE), np.float32)
    inv_k = np.zeros((1, LANE), np.float32)
    seg_d = np.zeros((DIL_W, LANE), np.float32)
    inv_d = np.zeros((1, LANE), np.float32)
    for h in range(HEADS):
        seg_q[h * LANE:h * LANE + NOPE, 2 * h] = 1.0
        seg_q[h * LANE + NOPE:h * LANE + NOPE + ROPE, 2 * h + 1] = 1.0
        inv_q[0, 2 * h], inv_q[0, 2 * h + 1] = 1.0 / NOPE, 1.0 / ROPE
        seg_k[h * LANE:h * LANE + NOPE, h] = 1.0
        inv_k[0, h] = 1.0 / NOPE
        seg_d[h * DIL_DIM:(h + 1) * DIL_DIM, h] = 1.0
        inv_d[0, h] = 1.0 / DIL_DIM
    fold_q = np.tile(np.eye(LANE, dtype=np.float32), (HEADS, 1))
    fold_d = np.zeros((DIL_W, LANE), np.float32)
    fold_d[np.arange(DIL_W), np.arange(DIL_W) % DIL_DIM] = 1.0
    j = lambda v: jnp.asarray(v)
    b = lambda v: jnp.asarray(v, dtype=BF16)
    return dict(seg_q=b(seg_q), exp_q=b(seg_q.T.copy()), inv_q=j(inv_q), seg_k=b(seg_k), exp_k=b(seg_k.T.copy()),
                inv_k=j(inv_k), seg_d=b(seg_d), exp_d=b(seg_d.T.copy()), inv_d=j(inv_d), fold_q=j(fold_q), fold_d=j(fold_d))


def _rope_consts():
    inv_d = jnp.power(ROPE_THETA, -2.0 * jnp.arange(DIL_DIM // 2, dtype=F32) / DIL_DIM)
    inv_q = jnp.power(ROPE_THETA, -2.0 * jnp.arange(ROPE // 2, dtype=F32) / ROPE)
    lanes = np.arange(LANE)
    freq_d = inv_d[lanes % (DIL_DIM // 2)]
    in_pe = (lanes >= KPE_OFF) & (lanes < KPE_OFF + ROPE)
    freq_q = jnp.where(jnp.asarray(in_pe), inv_q[(lanes - KPE_OFF) % (ROPE // 2)], 0.0)
    sign_d = np.where(lanes % DIL_DIM < DIL_DIM // 2, -1.0, 1.0).astype(np.float32)
    sign_q = np.where(in_pe, np.where((lanes - KPE_OFF) < ROPE // 2, -1.0, 1.0), 0.0).astype(np.float32)
    zeros, ones = np.zeros(LANE, np.float32), np.ones(LANE, np.float32)
    freq = jnp.concatenate([freq_d, freq_d, freq_q, freq_q])[None, :]
    csel = jnp.asarray(np.concatenate([ones, zeros, ones, zeros]))[None, :]
    ssel = jnp.asarray(np.concatenate([zeros, sign_d, zeros, sign_q]))[None, :]
    return freq, csel, ssel


def _full(shape):
    return pl.BlockSpec(shape, lambda *_: (0,) * len(shape))


def _tile_lanes(x, n):
    return jnp.concatenate([x] * n, axis=1)


def _rope_tables(pos_col, freq, csel, ssel):
    s = pos_col.shape[0]

    def body(p_ref, f_ref, c_ref, s_ref, o_ref):
        ang = p_ref[...].astype(F32) * f_ref[...]
        o_ref[...] = c_ref[...] * jnp.cos(ang) + s_ref[...] * jnp.sin(ang)

    return pl.pallas_call(
        body, name="rope_tables", grid=(s // ROW_TILE,),
        in_specs=[pl.BlockSpec((ROW_TILE, 1), lambda i: (i, 0)), _full((1, 4 * LANE)), _full((1, 4 * LANE)), _full((1, 4 * LANE))],
        out_specs=pl.BlockSpec((ROW_TILE, 4 * LANE), lambda i: (i, 0)),
        out_shape=jax.ShapeDtypeStruct((s, 4 * LANE), F32),
        compiler_params=_params(("parallel",)),
    )(pos_col, freq, csel, ssel)


def _rms(x):
    return lax.rsqrt(jnp.mean(x * x, axis=-1, keepdims=True) + EPS)


def _prenorm(x, gain, scale, shift, name):
    s, d = x.shape

    def body(x_ref, g_ref, sc_ref, sh_ref, h_ref):
        xv = x_ref[...]
        h = (xv * _rms(xv)) * g_ref[...] * (1.0 + sc_ref[...]) + sh_ref[...]
        h_ref[...] = h.astype(BF16)

    row = pl.BlockSpec((ROW_TILE, d), lambda i: (i, 0))
    return pl.pallas_call(
        body, name=name, grid=(s // ROW_TILE,),
        in_specs=[row, _full((1, d)), _full((1, d)), _full((1, d))],
        out_specs=row, out_shape=jax.ShapeDtypeStruct((s, d), BF16),
        compiler_params=_params(("parallel",)),
    )(x, gain, scale, shift)


def _latnorm(proj, g_q, g_kv):
    s = proj.shape[0]

    def body(q_ref, kv_ref, gq_ref, gkv_ref, ql_ref, kvl_ref):
        q, kv = q_ref[...], kv_ref[...]
        ql_ref[...] = ((q * _rms(q)) * gq_ref[...]).astype(BF16)
        kvl_ref[...] = ((kv * _rms(kv)) * gkv_ref[...]).astype(BF16)

    return pl.pallas_call(
        body, name="latnorm", grid=(s // ROW_TILE,),
        in_specs=[pl.BlockSpec((ROW_TILE, Q_LORA), lambda i: (i, P_QLAT // Q_LORA)),
                  pl.BlockSpec((ROW_TILE, KV_LORA), lambda i: (i, P_KVLAT // KV_LORA)),
                  _full((1, Q_LORA)), _full((1, KV_LORA))],
        out_specs=[pl.BlockSpec((ROW_TILE, Q_LORA), lambda i: (i, 0)), pl.BlockSpec((ROW_TILE, KV_LORA), lambda i: (i, 0))],
        out_shape=[jax.ShapeDtypeStruct((s, Q_LORA), BF16), jax.ShapeDtypeStruct((s, KV_LORA), BF16)],
        compiler_params=_params(("parallel",)),
    )(proj, proj, g_q, g_kv)


def _dot01(v, mat01):
    hi = v.astype(BF16)
    lo = (v - hi.astype(F32)).astype(BF16)
    return jnp.dot(hi, mat01, preferred_element_type=F32) + jnp.dot(lo, mat01, preferred_element_type=F32)


def _seg_rinv(x, seg, exp, inv):
    r = lax.rsqrt(_dot01(x * x, seg) * inv + EPS)
    return _dot01(r, exp)


def _seg_mean(v, seg, exp, inv):
    return _dot01(_dot01(v, seg) * inv, exp)


def _swap_halves(x, half):
    n = x.shape[1]
    lane = lax.broadcasted_iota(I32, (1, n), 1)
    first = (lane & (2 * half - 1)) < half
    return jnp.where(first, pltpu.roll(x, n - half, 1), pltpu.roll(x, half, 1))


def _rope(x, cos, sin_signed, half):
    return x * cos + _swap_halves(x, half) * sin_signed


def _rope_bwd(dy, cos, sin_signed, half):
    return dy * cos + _swap_halves(dy * sin_signed, half)


def _pe_lane_mask(n):
    lane = lax.broadcasted_iota(I32, (1, n), 1) & (LANE - 1)
    return (lane >= KPE_OFF) & (lane < KPE_OFF + ROPE)


def _attn_prep(q_raw, kv_raw, proj, tab, gains, consts):
    s = q_raw.shape[0]
    hw = HEADS * LANE

    def body(q_ref, kv_ref, kpe_ref, qd_ref, kd_ref, vd_ref, tab_ref,
             gq_ref, gk_ref, gkpe_ref, gdq_ref, gdk_ref,
             segq_ref, expq_ref, invq_ref, segk_ref, expk_ref, invk_ref, segd_ref, expd_ref, invd_ref,
             qm_ref, km_ref, vm_ref, qdo_ref, kdo_ref, vdo_ref):
        tab_v = tab_ref[...]
        cos_d, sin_d = _tile_lanes(tab_v[:, 0:LANE], DIL_W // LANE), _tile_lanes(tab_v[:, LANE:2 * LANE], DIL_W // LANE)
        cos_q1, sin_q1 = tab_v[:, 2 * LANE:3 * LANE], tab_v[:, 3 * LANE:4 * LANE]
        cos_q, sin_q = _tile_lanes(cos_q1, HEADS), _tile_lanes(sin_q1, HEADS)

        q = q_ref[...]
        qn = q * _seg_rinv(q, segq_ref[...], expq_ref[...], invq_ref[...]) * gq_ref[...]
        qm_ref[...] = _rope(qn, cos_q, sin_q, ROPE // 2).astype(BF16)

        kv = kv_ref[...]
        kp = kv[:, :hw]
        kn = kp * _seg_rinv(kp, segk_ref[...], expk_ref[...], invk_ref[...]) * gk_ref[...]
        kpe = kpe_ref[...]
        r_pe = lax.rsqrt(jnp.sum(kpe * kpe, axis=-1, keepdims=True) * (1.0 / ROPE) + EPS)
        kpe_r = _rope(kpe * r_pe * gkpe_ref[...], cos_q1, sin_q1, ROPE // 2)
        km_ref[...] = (kn + _tile_lanes(kpe_r, HEADS)).astype(BF16)
        vm_ref[...] = kv[:, hw:].astype(BF16)

        qd = qd_ref[...]
        qdn = qd * _seg_rinv(qd, segd_ref[...], expd_ref[...], invd_ref[...]) * gdq_ref[...]
        qdo_ref[...] = _rope(qdn, cos_d, sin_d, DIL_DIM // 2).astype(BF16)
        kd = kd_ref[...]
        kdn = kd * _seg_rinv(kd, segd_ref[...], expd_ref[...], invd_ref[...]) * gdk_ref[...]
        kdo_ref[...] = _rope(kdn, cos_d, sin_d, DIL_DIM // 2).astype(BF16)
        vdo_ref[...] = vd_ref[...].astype(BF16)

    t = ROW_TILE
    row = lambda w, cb=0: pl.BlockSpec((t, w), lambda i: (i, cb))
    c = consts
    return pl.pallas_call(
        body, name="attn_prep", grid=(s // t,),
        in_specs=[row(hw), row(hw + DIL_W), row(LANE, P_KPE // LANE), row(DIL_W, P_QD // DIL_W), row(DIL_W, P_KD // DIL_W),
                  row(DIL_W, P_VD // DIL_W), row(4 * LANE),
                  _full((1, hw)), _full((1, hw)), _full((1, LANE)), _full((1, DIL_W)), _full((1, DIL_W)),
                  _full((hw, LANE)), _full((LANE, hw)), _full((1, LANE)), _full((hw, LANE)), _full((LANE, hw)), _full((1, LANE)),
                  _full((DIL_W, LANE)), _full((LANE, DIL_W)), _full((1, LANE))],
        out_specs=[row(hw), row(hw), row(DIL_W), row(DIL_W), row(DIL_W), row(DIL_W)],
        out_shape=[jax.ShapeDtypeStruct((s, hw), BF16), jax.ShapeDtypeStruct((s, hw), BF16)]
        + [jax.ShapeDtypeStruct((s, DIL_W), BF16)] * 4,
        compiler_params=_params(("parallel",), 24 << 20),
    )(*_in_hbm(q_raw, kv_raw, proj, proj, proj, proj, tab), gains["q"], gains["k"], gains["kpe"], gains["dq"], gains["dk"],
      c["seg_q"], c["exp_q"], c["inv_q"], c["seg_k"], c["exp_k"], c["inv_k"], c["seg_d"], c["exp_d"], c["inv_d"])


def _attn_prep_bwd(dqm, dkm, dvm, dqd, dkd, dvd, q_raw, kv_raw, proj, tab, gains, consts):
    s = q_raw.shape[0]
    hw = HEADS * LANE
    n_steps = s // ROW_TILE

    def body(dqm_ref, dkm_ref, dvm_ref, dqd_ref, dkd_ref, dvd_ref, q_ref, kv_ref, kpe_ref, qd_ref, kd_ref, tab_ref,
             gq_ref, gk_ref, gkpe_ref, gdq_ref, gdk_ref,
             segq_ref, expq_ref, invq_ref, segk_ref, expk_ref, invk_ref, segd_ref, expd_ref, invd_ref, foldq_ref, foldd_ref,
             dq_ref, dkv_ref, dkpe_ref, dqdo_ref, dkdo_ref, dvdo_ref, dg_ref, acc_ref):
        i = pl.program_id(0)

        @pl.when(i == 0)
        def _():
            acc_ref[...] = jnp.zeros_like(acc_ref)

        tab_v = tab_ref[...]
        cos_d, sin_d = _tile_lanes(tab_v[:, 0:LANE], DIL_W // LANE), _tile_lanes(tab_v[:, LANE:2 * LANE], DIL_W // LANE)
        cos_q1, sin_q1 = tab_v[:, 2 * LANE:3 * LANE], tab_v[:, 3 * LANE:4 * LANE]
        cos_q, sin_q = _tile_lanes(cos_q1, HEADS), _tile_lanes(sin_q1, HEADS)

        def norm_bwd(x, dyg, gain, seg, exp, inv):
            rinv = _seg_rinv(x, seg, exp, inv)
            xn = x * rinv
            dxn = dyg * gain
            dx = rinv * (dxn - xn * _seg_mean(dxn * xn, seg, exp, inv))
            return dx, jnp.sum(dyg * xn, axis=0, keepdims=True)

        dq, gq_l = norm_bwd(q_ref[...], _rope_bwd(dqm_ref[...], cos_q, sin_q, ROPE // 2), gq_ref[...],
                            segq_ref[...], expq_ref[...], invq_ref[...])
        dq_ref[...] = dq.astype(BF16)

        dkm = dkm_ref[...]
        kv = kv_ref[...]
        dkp, gk_l = norm_bwd(kv[:, :hw], dkm, gk_ref[...], segk_ref[...], expk_ref[...], invk_ref[...])
        dkv_ref[:, :hw] = dkp.astype(BF16)
        dkv_ref[:, hw:] = dvm_ref[...].astype(BF16)

        dkpe_r = dkm[:, 0:LANE]
        for h in range(1, HEADS):
            dkpe_r = dkpe_r + dkm[:, h * LANE:(h + 1) * LANE]
        dkpe_r = jnp.where(_pe_lane_mask(LANE), dkpe_r, 0.0)
        dyg = _rope_bwd(dkpe_r, cos_q1, sin_q1, ROPE // 2)
        kpe = kpe_ref[...]
        r_pe = lax.rsqrt(jnp.sum(kpe * kpe, axis=-1, keepdims=True) * (1.0 / ROPE) + EPS)
        xn = kpe * r_pe
        dxn = dyg * gkpe_ref[...]
        dkpe = r_pe * (dxn - xn * (jnp.sum(dxn * xn, axis=-1, keepdims=True) * (1.0 / ROPE)))
        dkpe_ref[...] = dkpe.astype(BF16)
        gkpe_l = jnp.sum(dyg * xn, axis=0, keepdims=True)

        dqd_v, gdq_l = norm_bwd(qd_ref[...], _rope_bwd(dqd_ref[...], cos_d, sin_d, DIL_DIM // 2), gdq_ref[...],
                                segd_ref[...], expd_ref[...], invd_ref[...])
        dqdo_ref[...] = dqd_v.astype(BF16)
        dkd_v, gdk_l = norm_bwd(kd_ref[...], _rope_bwd(dkd_ref[...], cos_d, sin_d, DIL_DIM // 2), gdk_ref[...],
                                segd_ref[...], expd_ref[...], invd_ref[...])
        dkdo_ref[...] = dkd_v.astype(BF16)
        dvdo_ref[...] = dvd_ref[...].astype(BF16)

        acc_ref[0:1, :] += gq_l
        acc_ref[1:2, :] += gk_l
        acc_ref[2:3, 0:LANE] += gkpe_l
        acc_ref[3:4, 0:DIL_W] += gdq_l
        acc_ref[4:5, 0:DIL_W] += gdk_l

        @pl.when(i == n_steps - 1)
        def _():
            acc = acc_ref[...]
            fq = jnp.dot(acc, foldq_ref[...], precision=HIGHEST, preferred_element_type=F32)
            fd = jnp.dot(acc[:, 0:DIL_W], foldd_ref[...], precision=HIGHEST, preferred_element_type=F32)
            rows = lax.broadcasted_iota(I32, (8, LANE), 0)
            base = jnp.where(rows < 2, fq, jnp.where(rows == 2, acc[:, 0:LANE], fd))
            at0 = pltpu.roll(base, LANE - KPE_OFF, 1)
            dg_ref[...] = jnp.where(rows == 5, pltpu.roll(at0, 5, 0), jnp.where(rows == 2, at0, base))

    t = ROW_TILE
    row = lambda w, cb=0: pl.BlockSpec((t, w), lambda i: (i, cb))
    c = consts
    return pl.pallas_call(
        body, name="attn_prep_bwd", grid=(n_steps,),
        in_specs=[row(hw), row(hw), row(DIL_W), row(DIL_W), row(DIL_W), row(DIL_W),
                  row(hw), row(hw + DIL_W), row(LANE, P_KPE // LANE), row(DIL_W, P_QD // DIL_W), row(DIL_W, P_KD // DIL_W),
                  row(4 * LANE),
                  _full((1, hw)), _full((1, hw)), _full((1, LANE)), _full((1, DIL_W)), _full((1, DIL_W)),
                  _full((hw, LANE)), _full((LANE, hw)), _full((1, LANE)), _full((hw, LANE)), _full((LANE, hw)), _full((1, LANE)),
                  _full((DIL_W, LANE)), _full((LANE, DIL_W)), _full((1, LANE)), _full((hw, LANE)), _full((DIL_W, LANE))],
        out_specs=[row(hw), row(hw + DIL_W), row(LANE), row(DIL_W), row(DIL_W), row(DIL_W), _full((8, LANE))],
        out_shape=[jax.ShapeDtypeStruct((s, hw), BF16), jax.ShapeDtypeStruct((s, hw + DIL_W), BF16),
                   jax.ShapeDtypeStruct((s, LANE), BF16)] + [jax.ShapeDtypeStruct((s, DIL_W), BF16)] * 3
        + [jax.ShapeDtypeStruct((8, LANE), F32)],
        scratch_shapes=[pltpu.VMEM((8, hw), F32)],
        compiler_params=_params(("arbitrary",), 28 << 20),
    )(*_in_hbm(dqm, dkm, dvm, dqd, dkd, dvd, q_raw, kv_raw, proj, proj, proj, tab),
      gains["q"], gains["k"], gains["kpe"], gains["dq"], gains["dk"],
      c["seg_q"], c["exp_q"], c["inv_q"], c["seg_k"], c["exp_k"], c["inv_k"], c["seg_d"], c["exp_d"], c["inv_d"],
      c["fold_q"], c["fold_d"])


def _latnorm_bwd(dql, dkvl, proj, g_q, g_kv):
    s = proj.shape[0]
    n_steps = s // ROW_TILE

    def body(dql_ref, dkvl_ref, q_ref, kv_ref, gq_ref, gkv_ref, dq_ref, dkv_ref, dg_ref):
        i = pl.program_id(0)

        @pl.when(i == 0)
        def _():
            dg_ref[...] = jnp.zeros_like(dg_ref)

        def one(x, dyg, gain):
            r = _rms(x)
            xn = x * r
            dxn = dyg * gain
            dx = r * (dxn - xn * jnp.mean(dxn * xn, axis=-1, keepdims=True))
            return dx, jnp.sum(dyg * xn, axis=0, keepdims=True)

        dq, gq_l = one(q_ref[...], dql_ref[...], gq_ref[...])
        dkv, gkv_l = one(kv_ref[...], dkvl_ref[...], gkv_ref[...])
        dq_ref[...] = dq.astype(BF16)
        dkv_ref[...] = dkv.astype(BF16)
        dg_ref[0:1, :] += gq_l
        dg_ref[1:2, 0:KV_LORA] += gkv_l

    t = ROW_TILE
    return pl.pallas_call(
        body, name="latnorm_bwd", grid=(n_steps,),
        in_specs=[pl.BlockSpec((t, Q_LORA), lambda i: (i, 0)), pl.BlockSpec((t, KV_LORA), lambda i: (i, 0)),
                  pl.BlockSpec((t, Q_LORA), lambda i: (i, P_QLAT // Q_LORA)),
                  pl.BlockSpec((t, KV_LORA), lambda i: (i, P_KVLAT // KV_LORA)),
                  _full((1, Q_LORA)), _full((1, KV_LORA))],
        out_specs=[pl.BlockSpec((t, Q_LORA), lambda i: (i, 0)), pl.BlockSpec((t, KV_LORA), lambda i: (i, 0)), _full((8, Q_LORA))],
        out_shape=[jax.ShapeDtypeStruct((s, Q_LORA), BF16), jax.ShapeDtypeStruct((s, KV_LORA), BF16),
                   jax.ShapeDtypeStruct((8, Q_LORA), F32)],
        compiler_params=_params(("arbitrary",)),
    )(dql, dkvl, proj, proj, g_q, g_kv)


def _resid_prenorm(x, mix, g1, gain, scale, shift):
    s, d = x.shape

    def body(x_ref, mix_ref, g1_ref, g_ref, sc_ref, sh_ref, x1_ref, h_ref):
        x1 = x_ref[...] + g1_ref[...] * mix_ref[...]
        x1_ref[...] = x1
        h_ref[...] = ((x1 * _rms(x1)) * g_ref[...] * (1.0 + sc_ref[...]) + sh_ref[...]).astype(BF16)

    row = pl.BlockSpec((ROW_TILE, d), lambda i: (i, 0))
    vec = _full((1, d))
    return pl.pallas_call(
        body, name="resid_prenorm", grid=(s // ROW_TILE,),
        in_specs=[row, row, vec, vec, vec, vec], out_specs=[row, row],
        out_shape=[jax.ShapeDtypeStruct((s, d), F32), jax.ShapeDtypeStruct((s, d), BF16)],
        compiler_params=_params(("parallel",)),
    )(x, mix, g1, gain, scale, shift)


CONV_TILE = 1408
HALO = 8


def _shift_down(x, halo, k):
    t = x.shape[0]
    row = lax.broadcasted_iota(I32, (t, 1), 0)
    out = pltpu.roll(x, k, 0)
    for r in range(k):
        out = jnp.where(row == r, halo[HALO - k + r:HALO - k + r + 1, :], out)
    return out


def _shift_up(x, halo, k):
    t = x.shape[0]
    row = lax.broadcasted_iota(I32, (t, 1), 0)
    out = pltpu.roll(x, t - k, 0)
    for r in range(k):
        out = jnp.where(row == t - k + r, halo[r:r + 1, :], out)
    return out


def _conv_fwd(x, halo, w, b):
    p1, p2 = _shift_down(x, halo, 1), _shift_down(x, halo, 2)
    u = b + p2 * w[0:1, :]
    u = u + p1 * w[1:2, :]
    u = u + x * w[2:3, :]
    return u, p1, p2


def _sigmoid(x):
    return 1.0 / (1.0 + jnp.exp(-x))


def _conv_gate(up, w_conv, b_conv):
    s = up.shape[0]
    t = ROW_TILE
    nj = D_FF // CONV_TILE
    hb = t // HALO

    def body(g_ref, v_ref, gh_ref, vh_ref, wg_ref, wv_ref, bg_ref, bv_ref, a_ref):
        live = (pl.program_id(0) > 0).astype(F32)
        ug, _, _ = _conv_fwd(g_ref[...], gh_ref[...] * live, wg_ref[...], bg_ref[...])
        uv, _, _ = _conv_fwd(v_ref[...], vh_ref[...] * live, wv_ref[...], bv_ref[...])
        a_ref[...] = (ug * _sigmoid(ug) * uv).astype(BF16)

    main = lambda off: pl.BlockSpec((t, CONV_TILE), lambda i, j: (i, j + off))
    halo = lambda off: pl.BlockSpec((HALO, CONV_TILE), lambda i, j: (jnp.maximum(i * hb - 1, 0), j + off))
    wsp = lambda off: pl.BlockSpec((3, CONV_TILE), lambda i, j: (0, j + off))
    bsp = lambda off: pl.BlockSpec((1, CONV_TILE), lambda i, j: (0, j + off))
    return pl.pallas_call(
        body, name="conv_gate", grid=(s // t, nj),
        in_specs=[main(0), main(nj), halo(0), halo(nj), wsp(0), wsp(nj), bsp(0), bsp(nj)],
        out_specs=pl.BlockSpec((t, CONV_TILE), lambda i, j: (i, j)),
        out_shape=jax.ShapeDtypeStruct((s, D_FF), BF16),
        compiler_params=_params(("parallel", "parallel"), 12 << 20),
    )(up, up, up, up, w_conv, w_conv, b_conv, b_conv)


def _gate_bwd(up, da, w_conv, b_conv):
    s = up.shape[0]
    t = ROW_TILE
    nj = D_FF // CONV_TILE
    hb = t // HALO
    n_i = s // t

    def body(g_ref, v_ref, gh_ref, vh_ref, gn_ref, vn_ref, da_ref, dan_ref, wg_ref, wv_ref, bg_ref, bv_ref,
             dupg_ref, dupv_ref, dbg_ref, dbv_ref, dwg_ref, dwv_ref):
        i = pl.program_id(1)

        @pl.when(i == 0)
        def _():
            for r in (dbg_ref, dbv_ref, dwg_ref, dwv_ref):
                r[...] = jnp.zeros_like(r)

        def d_gate(ug, uv, da_v):
            sg = _sigmoid(ug)
            return da_v * uv * (sg * (1.0 + ug * (1.0 - sg))), da_v * (ug * sg)

        live = (i > 0).astype(F32)
        xg, xv = g_ref[...], v_ref[...]
        wg, wv = wg_ref[...], wv_ref[...]
        ug, g1, g2 = _conv_fwd(xg, gh_ref[...] * live, wg, bg_ref[...])
        uv, v1, v2 = _conv_fwd(xv, vh_ref[...] * live, wv, bv_ref[...])
        dug, duv = d_gate(ug, uv, da_ref[...])

        more = (i < n_i - 1).astype(F32)
        ug_n, _, _ = _conv_fwd(gn_ref[...], xg[t - HALO:, :], wg, bg_ref[...])
        uv_n, _, _ = _conv_fwd(vn_ref[...], xv[t - HALO:, :], wv, bv_ref[...])
        dug_n, duv_n = d_gate(ug_n, uv_n, dan_ref[...] * more)

        def conv_t(du, du_n, w):
            return du * w[2:3, :] + _shift_up(du, du_n, 1) * w[1:2, :] + _shift_up(du, du_n, 2) * w[0:1, :]

        dupg_ref[...] = conv_t(dug, dug_n, wg).astype(BF16)
        dupv_ref[...] = conv_t(duv, duv_n, wv).astype(BF16)
        csum = lambda z: jnp.sum(z, axis=0, keepdims=True)
        dbg_ref[...] += csum(dug)
        dbv_ref[...] += csum(duv)
        dwg_ref[0:1, :] += csum(dug * g2)
        dwg_ref[1:2, :] += csum(dug * g1)
        dwg_ref[2:3, :] += csum(dug * xg)
        dwv_ref[0:1, :] += csum(duv * v2)
        dwv_ref[1:2, :] += csum(duv * v1)
        dwv_ref[2:3, :] += csum(duv * xv)

    last_halo = s // HALO - 1
    main = lambda off: pl.BlockSpec((t, CONV_TILE), lambda j, i: (i, j + off))
    halo = lambda off: pl.BlockSpec((HALO, CONV_TILE), lambda j, i: (jnp.maximum(i * hb - 1, 0), j + off))
    nxt = lambda off: pl.BlockSpec((HALO, CONV_TILE), lambda j, i: (jnp.minimum((i + 1) * hb, last_halo), j + off))
    wsp = lambda off: pl.BlockSpec((3, CONV_TILE), lambda j, i: (0, j + off))
    bsp = lambda off: pl.BlockSpec((1, CONV_TILE), lambda j, i: (0, j + off))
    outs = pl.pallas_call(
        body, name="gate_bwd", grid=(nj, n_i),
        in_specs=[main(0), main(nj), halo(0), halo(nj), nxt(0), nxt(nj), main(0), nxt(0),
                  wsp(0), wsp(nj), bsp(0), bsp(nj)],
        out_specs=[main(0), main(0),
                   pl.BlockSpec((1, CONV_TILE), lambda j, i: (0, j)), pl.BlockSpec((1, CONV_TILE), lambda j, i: (0, j)),
                   pl.BlockSpec((3, CONV_TILE), lambda j, i: (0, j)), pl.BlockSpec((3, CONV_TILE), lambda j, i: (0, j))],
        out_shape=[jax.ShapeDtypeStruct((s, D_FF), BF16), jax.ShapeDtypeStruct((s, D_FF), BF16),
                   jax.ShapeDtypeStruct((1, D_FF), F32), jax.ShapeDtypeStruct((1, D_FF), F32),
                   jax.ShapeDtypeStruct((3, D_FF), F32), jax.ShapeDtypeStruct((3, D_FF), F32)],
        compiler_params=_params(("parallel", "arbitrary"), 24 << 20),
    )(up, up, up, up, up, up, da, da, w_conv, w_conv, b_conv, b_conv)
    return outs


def _final(x1, ffn, tgt, g2):
    s, d = x1.shape
    n_steps = s // ROW_TILE

    def body(x1_ref, f_ref, t_ref, g2_ref, dy_ref, df_ref, dg2_ref, loss_ref, lacc_ref):
        i = pl.program_id(0)

        @pl.when(i == 0)
        def _():
            dg2_ref[...] = jnp.zeros_like(dg2_ref)
            lacc_ref[...] = jnp.zeros_like(lacc_ref)

        f = f_ref[...]
        e = x1_ref[...] + g2_ref[...] * f - t_ref[...]
        dy = e * (1.0 / d)
        dy_ref[...] = dy
        df_ref[...] = (dy * g2_ref[...]).astype(BF16)
        dg2_ref[...] += jnp.sum(dy * f, axis=0, keepdims=True)
        lacc_ref[...] += jnp.sum(e * e, axis=0, keepdims=True)

        @pl.when(i == n_steps - 1)
        def _():
            loss_ref[...] = jnp.sum(lacc_ref[...], axis=1, keepdims=True) * (0.5 / d)

    row = pl.BlockSpec((ROW_TILE, d), lambda i: (i, 0))
    return pl.pallas_call(
        body, name="final", grid=(n_steps,),
        in_specs=[row, row, row, _full((1, d))],
        out_specs=[row, row, _full((1, d)), _full((1, 1))],
        out_shape=[jax.ShapeDtypeStruct((s, d), F32), jax.ShapeDtypeStruct((s, d), BF16),
                   jax.ShapeDtypeStruct((1, d), F32), jax.ShapeDtypeStruct((1, 1), F32)],
        scratch_shapes=[pltpu.VMEM((1, d), F32)],
        compiler_params=_params(("arbitrary",)),
    )(x1, ffn, tgt, g2)


def _ffnnorm_bwd(dh2, x1, dy, mix, gain, scale, g1):
    s, d = x1.shape
    n_steps = s // ROW_TILE

    def body(dh_ref, x_ref, dy_ref, mix_ref, g_ref, sc_ref, g1_ref, dx_ref, dm_ref, acc_ref):
        i = pl.program_id(0)

        @pl.when(i == 0)
        def _():
            acc_ref[...] = jnp.zeros_like(acc_ref)

        dh, x = dh_ref[...], x_ref[...]
        r = _rms(x)
        xn = x * r
        dn = dh * (1.0 + sc_ref[...])
        dxn = dn * g_ref[...]
        dx = dy_ref[...] + r * (dxn - xn * jnp.mean(dxn * xn, axis=-1, keepdims=True))
        dx_ref[...] = dx
        dm_ref[...] = (dx * g1_ref[...]).astype(BF16)
        csum = lambda z: jnp.sum(z, axis=0, keepdims=True)
        acc_ref[0:1, :] += csum(dh)
        acc_ref[1:2, :] += csum(dh * (xn * g_ref[...]))
        acc_ref[2:3, :] += csum(dn * xn)
        acc_ref[3:4, :] += csum(dx * mix_ref[...])

    row = pl.BlockSpec((ROW_TILE, d), lambda i: (i, 0))
    vec = _full((1, d))
    return pl.pallas_call(
        body, name="ffnnorm_bwd", grid=(n_steps,),
        in_specs=[row, row, row, row, vec, vec, vec],
        out_specs=[row, row, _full((8, d))],
        out_shape=[jax.ShapeDtypeStruct((s, d), F32), jax.ShapeDtypeStruct((s, d), BF16), jax.ShapeDtypeStruct((8, d), F32)],
        compiler_params=_params(("arbitrary",)),
    )(dh2, x1, dy, mix, gain, scale, g1)


def _mixnorm_bwd(dh, x, dx1, gain, scale):
    s, d = x.shape
    n_steps = s // ROW_TILE

    def body(dh_ref, x_ref, dx1_ref, g_ref, sc_ref, gx_ref, acc_ref):
        i = pl.program_id(0)

        @pl.when(i == 0)
        def _():
            acc_ref[...] = jnp.zeros_like(acc_ref)

        dh, x = dh_ref[...], x_ref[...]
        r = _rms(x)
        xn = x * r
        dn = dh * (1.0 + sc_ref[...])
        dxn = dn * g_ref[...]
        gx_ref[...] = dx1_ref[...] + r * (dxn - xn * jnp.mean(dxn * xn, axis=-1, keepdims=True))
        csum = lambda z: jnp.sum(z, axis=0, keepdims=True)
        acc_ref[0:1, :] += csum(dh)
        acc_ref[1:2, :] += csum(dh * (xn * g_ref[...]))
        acc_ref[2:3, :] += csum(dn * xn)

    row = pl.BlockSpec((ROW_TILE, d), lambda i: (i, 0))
    vec = _full((1, d))
    return pl.pallas_call(
        body, name="mixnorm_bwd", grid=(n_steps,),
        in_specs=[row, row, row, vec, vec],
        out_specs=[row, _full((8, d))],
        out_shape=[jax.ShapeDtypeStruct((s, d), F32), jax.ShapeDtypeStruct((8, d), F32)],
        compiler_params=_params(("arbitrary",)),
    )(dh, x, dx1, gain, scale)


def _key_count(d, dilated):
    if not dilated:
        return jnp.where(d >= 0, 1.0, 0.0)
    one = lambda cond: jnp.where(cond, 1.0, 0.0)
    cnt = one(d <= 128) + one(((d & 3) == 0) & (d <= 512)) + one((d & 15) == 0)
    return jnp.where(d >= 0, cnt, 0.0)


def _block_kinds(mla):
    return (0, "diag", "none") if mla else (512, "near", "far")


def _scores_t(ka, qa, scale, kind, rel_t, offset):
    st = lax.dot_general(ka, qa, NT, preferred_element_type=F32) * (scale * LOG2E)
    cnt = None
    if kind == "diag":
        st = jnp.where(rel_t + offset >= 0, st, NEG_INF)
    elif kind == "far":
        st = jnp.where((rel_t & 15) == 0, st, NEG_INF)
    elif kind == "near":
        cnt = _key_count(rel_t + offset, True)
        st = jnp.where(cnt > 0.0, st, NEG_INF)
    return st, cnt


def _attn_fwd(q, k, v, mla, scale, name, gather=()):
    s = q.shape[0]
    qw = 2 * LANE if mla else LANE
    tq, tk = ATT_TQ, ATT_TK
    reach, kind_near, kind_far = _block_kinds(mla)
    assert s % tq == 0 and tq % tk == 0 and reach % tk == 0
    ng = len(gather)
    last_step = HEADS // 2 - 1

    def body(*refs):
        q_ref, k_ref, v_ref = refs[:3]
        o_ref, lse_ref = refs[3 + ng:5 + ng]
        vt_ref = refs[5 + 2 * ng]
        comm = (refs[3:3 + ng], refs[5 + ng:5 + 2 * ng]) + tuple(refs[6 + 2 * ng:])
        if ng:
            @pl.when(pl.program_id(0) == 0)
            def _():
                _Gather(*comm).start()

            @pl.when(pl.program_id(0) == last_step)
            def _():
                _Gather(*comm).forward()

        lane = lax.broadcasted_iota(I32, (1, LANE), 1)
        rel_t = lax.broadcasted_iota(I32, (tk, tq), 1) - lax.broadcasted_iota(I32, (tk, tq), 0)

        def transpose_v(j, carry):
            c0 = pl.multiple_of(j * tk, tk)
            vt_ref[:, pl.ds(c0, tk)] = v_ref[pl.ds(c0, tk), :].astype(F32).T.astype(BF16)
            return carry

        lax.fori_loop(0, s // tk, transpose_v, 0)

        def q_block(qi, carry):
            r0 = pl.multiple_of(qi * tq, tq)
            kcols = [slice(a * LANE, (a + 1) * LANE) if mla else slice(0, LANE) for a in range(2)]
            qas = [q_ref[pl.ds(r0, tq), kcols[a]] for a in range(2)]
            if not mla:
                qas = [jnp.where(lane < DIL_DIM, qas[0], jnp.zeros_like(qas[0])),
                       jnp.where(lane >= DIL_DIM, qas[1], jnp.zeros_like(qas[1]))]

            def k_block(kj, c, kind):
                c0 = pl.multiple_of(kj * tk, tk)
                out = []
                for a in range(2):
                    m, l, acc = c[a]
                    st, cnt = _scores_t(k_ref[pl.ds(c0, tk), kcols[a]], qas[a], scale, kind, rel_t, r0 - c0)
                    m_new = jnp.maximum(m, jnp.max(st, axis=0, keepdims=True))
                    alpha = jnp.exp2(m - m_new)
                    p = jnp.exp2(st - m_new)
                    if cnt is not None:
                        p = p * cnt
                    l = alpha * l + jnp.sum(p, axis=0, keepdims=True)
                    vt = vt_ref[a * DIL_DIM:(a + 1) * DIL_DIM, pl.ds(c0, tk)]
                    acc = alpha * acc + jnp.dot(vt, p.astype(BF16), preferred_element_type=F32)
                    out.append((m_new, l, acc))
                return tuple(out)

            one = (jnp.full((1, tq), NEG_INF, F32), jnp.zeros((1, tq), F32), jnp.zeros((DIL_DIM, tq), F32))
            first_near = jnp.maximum((r0 - reach) // tk, 0)
            c = lax.fori_loop(0, first_near, functools.partial(k_block, kind=kind_far), (one, one))
            res = lax.fori_loop(first_near, (r0 + tq) // tk, functools.partial(k_block, kind=kind_near), c)
            o_t = jnp.concatenate([res[a][2] / res[a][1] for a in range(2)], axis=0)
            o_ref[pl.ds(r0, tq), :] = o_t.T.astype(BF16)
            for a in range(2):
                lse_ref[a, :, pl.ds(r0, tq)] = res[a][0] * LN2 + jnp.log(res[a][1])
            return carry

        lax.fori_loop(0, s // tq, q_block, 0)

        if ng:
            @pl.when(pl.program_id(0) == last_step)
            def _():
                _Gather(*comm).finish()

    return pl.pallas_call(
        body, name=name, grid=(HEADS // 2,),
        in_specs=[pl.BlockSpec((s, qw), lambda h: (0, h)), pl.BlockSpec((s, qw), lambda h: (0, h)),
                  pl.BlockSpec((s, LANE), lambda h: (0, h))] + [ANY] * ng,
        out_specs=[pl.BlockSpec((s, LANE), lambda h: (0, h)), pl.BlockSpec((2, 1, s), lambda h: (h, 0, 0))] + [ANY] * ng,
        out_shape=[jax.ShapeDtypeStruct((s, DIL_W), BF16), jax.ShapeDtypeStruct((HEADS, 1, s), F32)] + _Gather.out_shapes(gather),
        scratch_shapes=[pltpu.VMEM((LANE, s), BF16)] + (_Gather.semaphores(ng) if ng else []),
        compiler_params=_params(("arbitrary",) if ng else ("parallel",), 12 << 20),
    )(*_in_hbm(q, k, v), *gather)


def _attn_bwd(q, k, v, o, do, do_block0, lse, mla, scale, name, scatter=()):
    s = q.shape[0]
    qw = 2 * LANE if mla else LANE
    tq, tk = ATT_TQ, ATT_TK
    nq = s // tq
    reach, kind_near, kind_far = _block_kinds(mla)
    assert s % tq == 0 and tq % tk == 0
    ns = len(scatter)
    last_step = HEADS // 2 - 1

    def body(*refs):
        q_ref, k_ref, v_ref, o_ref, do_ref, lse_ref = refs[:6]
        dq_ref, dk_ref, dv_ref = refs[6 + ns:9 + ns]
        kt_ref, dot_ref, dob_ref, dqt_ref, delta_ref, lse2_ref = refs[9 + 2 * ns:15 + 2 * ns]
        comm = (refs[6:6 + ns], refs[9 + ns:9 + 2 * ns]) + tuple(refs[15 + 2 * ns:])
        if ns:
            @pl.when(pl.program_id(0) == 0)
            def _():
                _Scatter(*comm).start()

        lane = lax.broadcasted_iota(I32, (1, LANE), 1)
        row = lax.broadcasted_iota(I32, (LANE, 1), 0)
        rel_t = lax.broadcasted_iota(I32, (tk, tq), 1) - lax.broadcasted_iota(I32, (tk, tq), 0)

        def prepare(j, carry):
            c0 = pl.multiple_of(j * tk, tk)
            do_blk = do_ref[pl.ds(c0, tk), :]
            dob_ref[pl.ds(c0, tk), :] = do_blk.astype(BF16)
            do_t = do_blk.T
            dot_ref[:, pl.ds(c0, tk)] = do_t.astype(BF16)
            prod = do_t * o_ref[pl.ds(c0, tk), :].astype(F32).T
            delta_ref[0, :, pl.ds(c0, tk)] = jnp.sum(prod[0:DIL_DIM], axis=0, keepdims=True)
            delta_ref[1, :, pl.ds(c0, tk)] = jnp.sum(prod[DIL_DIM:LANE], axis=0, keepdims=True)
            for w in range(qw // LANE):
                kt_ref[w * LANE:(w + 1) * LANE, pl.ds(c0, tk)] = (
                    k_ref[pl.ds(c0, tk), w * LANE:(w + 1) * LANE].astype(F32).T.astype(BF16))
            return carry

        lax.fori_loop(0, s // tk, prepare, 0)
        dqt_ref[...] = jnp.zeros_like(dqt_ref)
        lse2_ref[...] = lse_ref[...] * LOG2E

        sels = [lane < DIL_DIM, lane >= DIL_DIM]
        rsels = [row < DIL_DIM, row >= DIL_DIM]
        cols = [slice(a * LANE, (a + 1) * LANE) if mla else slice(0, LANE) for a in range(2)]

        def k_block(kj, carry):
            c0 = pl.multiple_of(kj * tk, tk)
            kas = [k_ref[pl.ds(c0, tk), cols[a]] for a in range(2)]
            kts = [kt_ref[cols[a], pl.ds(c0, tk)] for a in range(2)]
            if not mla:
                kas = [jnp.where(sels[a], kas[a], jnp.zeros_like(kas[a])) for a in range(2)]
                kts = [jnp.where(rsels[a], kts[a], jnp.zeros_like(kts[a])) for a in range(2)]
            vb = v_ref[pl.ds(c0, tk), :]
            vbs = [jnp.where(sels[a], vb, jnp.zeros_like(vb)) for a in range(2)]

            def q_block(qi, c, kind):
                r0 = pl.multiple_of(qi * tq, tq)
                out, dq_parts = [], []
                for a in range(2):
                    dk_acc, dv_acc = c[a]
                    qa = q_ref[pl.ds(r0, tq), cols[a]]
                    st, cnt = _scores_t(kas[a], qa, scale, kind, rel_t, r0 - c0)
                    p = jnp.exp2(st - lse2_ref[a, :, pl.ds(r0, tq)])
                    if cnt is not None:
                        p = p * cnt
                    dp = jnp.dot(vbs[a], dot_ref[:, pl.ds(r0, tq)], preferred_element_type=F32)
                    ds = (p * (dp - delta_ref[a, :, pl.ds(r0, tq)]) * scale).astype(BF16)
                    dv_acc = dv_acc + jnp.dot(p.astype(BF16), dob_ref[pl.ds(r0, tq), :], preferred_element_type=F32)
                    dk_acc = dk_acc + jnp.dot(ds, qa, preferred_element_type=F32)
                    dq_parts.append(jnp.dot(kts[a], ds, preferred_element_type=F32))
                    out.append((dk_acc, dv_acc))
                if mla:
                    for a in range(2):
                        dqt_ref[cols[a], pl.ds(r0, tq)] += dq_parts[a]
                else:
                    dqt_ref[:, pl.ds(r0, tq)] += dq_parts[0] + dq_parts[1]
                return tuple(out)

            zero = jnp.zeros((tk, LANE), F32)
            last_near = jnp.minimum((c0 + tk - 1 + reach) // tq + 1, nq)
            c = lax.fori_loop(c0 // tq, last_near, functools.partial(q_block, kind=kind_near), ((zero, zero), (zero, zero)))
            (dk0, dv0), (dk1, dv1) = lax.fori_loop(last_near, nq, functools.partial(q_block, kind=kind_far), c)
            if mla:
                dk_ref[pl.ds(c0, tk), cols[0]] = dk0
                dk_ref[pl.ds(c0, tk), cols[1]] = dk1
            else:
                dk_ref[pl.ds(c0, tk), :] = jnp.where(sels[0], dk0, dk1)
            dv_ref[pl.ds(c0, tk), :] = jnp.where(sels[0], dv0, dv1)
            return carry

        lax.fori_loop(0, s // tk, k_block, 0)

        def write_dq(j, carry):
            c0 = pl.multiple_of(j * tk, tk)
            for w in range(qw // LANE):
                dq_ref[pl.ds(c0, tk), w * LANE:(w + 1) * LANE] = dqt_ref[w * LANE:(w + 1) * LANE, pl.ds(c0, tk)].T
            return carry

        lax.fori_loop(0, s // tk, write_dq, 0)

        if ns:
            @pl.when(pl.program_id(0) == last_step)
            def _():
                _Scatter(*comm).finish()

    b0 = do_block0
    return pl.pallas_call(
        body, name=name, grid=(HEADS // 2,),
        in_specs=[pl.BlockSpec((s, qw), lambda h: (0, h)), pl.BlockSpec((s, qw), lambda h: (0, h)),
                  pl.BlockSpec((s, LANE), lambda h: (0, h)), pl.BlockSpec((s, LANE), lambda h: (0, h)),
                  pl.BlockSpec((s, LANE), lambda h: (0, h + b0)), pl.BlockSpec((2, 1, s), lambda h: (h, 0, 0))] + [ANY] * ns,
        out_specs=[pl.BlockSpec((s, qw), lambda h: (0, h)), pl.BlockSpec((s, qw), lambda h: (0, h)),
                   pl.BlockSpec((s, LANE), lambda h: (0, h))] + [ANY] * ns,
        out_shape=[jax.ShapeDtypeStruct(q.shape, F32), jax.ShapeDtypeStruct(k.shape, F32), jax.ShapeDtypeStruct((s, DIL_W), F32)]
        + _Scatter.out_shapes(scatter),
        scratch_shapes=[pltpu.VMEM((qw, s), BF16), pltpu.VMEM((LANE, s), BF16), pltpu.VMEM((s, LANE), BF16),
                        pltpu.VMEM((qw, s), F32), pltpu.VMEM((2, 1, s), F32), pltpu.VMEM((2, 1, s), F32)]
        + (_Scatter.semaphores(ns) if ns else []),
        compiler_params=_params(("arbitrary",) if ns else ("parallel",), 24 << 20),
    )(*_in_hbm(q, k, v, o, do, lse), *scatter)


def _ada_fwd(c_all, w_shard, b_shard):
    n, d = c_all.shape
    cols = w_shard.shape[1]

    def body(c_ref, w_ref, b_ref, o_ref):
        cv = c_ref[...]
        sc = (cv * _sigmoid(cv)).astype(BF16)
        o_ref[...] = jnp.dot(sc, w_ref[...].astype(BF16), preferred_element_type=F32) + b_ref[...]

    return pl.pallas_call(
        body, name="ada_fwd", out_shape=jax.ShapeDtypeStruct((n, cols), F32),
        compiler_params=_params(None, 16 << 20),
    )(c_all, w_shard, b_shard)


def _ada_bwd(c_all, dmod_shard):
    n, d = c_all.shape
    cols = dmod_shard.shape[1]

    def body(c_ref, g_ref, o_ref):
        cv = c_ref[...]
        o_ref[...] = lax.dot_general(cv * _sigmoid(cv), g_ref[...], TN, precision=HIGHEST, preferred_element_type=F32)

    return pl.pallas_call(
        body, name="ada_bwd", out_shape=jax.ShapeDtypeStruct((d, cols), F32),
        compiler_params=_params(None, 16 << 20),
    )(c_all, dmod_shard)


SMALL_WIDTHS = (("g_mix_norm", D_MODEL), ("g_q_lat", Q_LORA), ("g_kv_lat", KV_LORA), ("g_mla_q_nope", NOPE),
                ("g_mla_q_pe", ROPE), ("g_mla_k_nope", NOPE), ("g_mla_k_pe", ROPE), ("g_dil_q", DIL_DIM),
                ("g_dil_k", DIL_DIM), ("g_ffn_norm", D_MODEL), ("b_conv", UP_W))


def _small_layout():
    pieces = (("dmod", 6 * D_MODEL),) + SMALL_WIDTHS + tuple(("w_conv%d" % k, UP_W) for k in range(3)) + (("loss", 1),)
    layout, off = {}, 0
    for name, width in pieces:
        layout[name] = (width, off)
        off += -(-width // LANE) * LANE
    return layout, off


def _pack_small(acc1, acc2, dg2, dglat, dgains, dbg, dbv, dwg, dwv, loss_part):
    layout, total = _small_layout()

    def body(a1, a2, g2, gl, gg, bg, bv, wg, wv, ls, o_ref):
        o_ref[...] = jnp.zeros_like(o_ref)

        def put(name, src, shift=0):
            start = layout[name][1] + shift
            o_ref[:, start:start + src.shape[1]] = src

        for k, src in enumerate((a1[0:1, :], a1[1:2, :], a2[3:4, :], a2[0:1, :], a2[1:2, :], g2[...])):
            put("dmod", src, k * D_MODEL)
        put("g_mix_norm", a1[2:3, :])
        put("g_q_lat", gl[0:1, :])
        put("g_kv_lat", gl[1:2, 0:KV_LORA])
        put("g_mla_q_nope", gg[0:1, 0:NOPE])
        put("g_mla_q_pe", gg[5:6, 0:ROPE])
        put("g_mla_k_nope", gg[1:2, 0:NOPE])
        put("g_mla_k_pe", gg[2:3, 0:ROPE])
        put("g_dil_q", gg[3:4, 0:DIL_DIM])
        put("g_dil_k", gg[4:5, 0:DIL_DIM])
        put("g_ffn_norm", a2[2:3, :])
        put("b_conv", bg[...])
        put("b_conv", bv[...], D_FF)
        for k in range(3):
            put("w_conv%d" % k, wg[k:k + 1, :])
            put("w_conv%d" % k, wv[k:k + 1, :], D_FF)
        put("loss", ls[...])

    ins = (acc1, acc2, dg2, dglat, dgains, dbg, dbv, dwg, dwv, loss_part)
    return pl.pallas_call(
        body, name="pack_small", grid=(1,), in_specs=[_full(a.shape) for a in ins], out_specs=_full((1, total)),
        out_shape=jax.ShapeDtypeStruct((1, total), F32),
        compiler_params=_params(("arbitrary",), 2 << 20),
    )(*_in_hbm(*ins))


def _sum_unpack(g):
    n_dev, _, total = g.shape
    layout, _ = _small_layout()

    def body(g_ref, *refs):
        o_refs, s_ref = refs[:-1], refs[-1]
        acc = g_ref[0]
        for k in range(1, n_dev):
            acc = acc + g_ref[k]
        s_ref[...] = acc
        take = lambda name: s_ref[:, layout[name][1]:layout[name][1] + layout[name][0]]
        o_refs[0][...] = take("dmod")
        for i, (name, _) in enumerate(SMALL_WIDTHS):
            o_refs[1 + i][...] = take(name)
        for k in range(3):
            o_refs[-2][k:k + 1, :] = take("w_conv%d" % k)
        o_refs[-1][...] = take("loss")

    shapes = [(1, 6 * D_MODEL)] + [(1, w) for _, w in SMALL_WIDTHS] + [(3, UP_W), (1, 1)]
    return pl.pallas_call(
        body, name="sum_unpack", out_shape=[jax.ShapeDtypeStruct(sh, F32) for sh in shapes],
        scratch_shapes=[pltpu.VMEM((1, total), F32)],
        compiler_params=_params(None, 4 << 20),
    )(g)


def _adamw_math(w, g, m, v):
    mn = ADAM_B1 * m + (1.0 - ADAM_B1) * g
    vn = ADAM_B2 * v + (1.0 - ADAM_B2) * (g * g)
    m_hat = mn / (1.0 - ADAM_B1 ** ADAM_STEP)
    v_hat = vn / (1.0 - ADAM_B2 ** ADAM_STEP)
    return -ADAM_LR * (m_hat / (jnp.sqrt(v_hat) + ADAM_EPS) + ADAM_WD * w), mn, vn


def _adamw_vectors(ws, gs, ms, vs):
    k = len(ws)

    def body(*refs):
        for i in range(k):
            d, mn, vn = _adamw_math(refs[i][...], refs[k + i][...], refs[2 * k + i][...], refs[3 * k + i][...])
            refs[4 * k + i][...] = d
            refs[5 * k + i][...] = mn
            refs[6 * k + i][...] = vn

    blocks = [_full(w.shape) for w in ws]
    outs = pl.pallas_call(
        body, name="adamw_vectors", grid=(1,), in_specs=blocks * 4, out_specs=blocks * 3,
        out_shape=[jax.ShapeDtypeStruct(w.shape, F32) for w in ws] * 3,
        compiler_params=_params(("arbitrary",), 2 << 20),
    )(*_in_hbm(*ws, *gs, *ms, *vs))
    return outs[:k], outs[k:2 * k], outs[2 * k:]


def _adamw(w, g, m, v, name):
    r, c = w.shape
    tr = r
    for cand in (256, 128, 64, 32, 16, 8):
        if r % cand == 0 and r > cand:
            tr = cand
            break

    def body(w_ref, g_ref, m_ref, v_ref, d_ref, mo_ref, vo_ref):
        d_ref[...], mo_ref[...], vo_ref[...] = _adamw_math(w_ref[...], g_ref[...], m_ref[...], v_ref[...])

    blk = pl.BlockSpec((tr, c), lambda i: (i, 0))
    return pl.pallas_call(
        body, name=name, grid=(r // tr,), in_specs=[blk] * 4, out_specs=[blk] * 3,
        out_shape=[jax.ShapeDtypeStruct((r, c), F32)] * 3,
        compiler_params=_params(("parallel",), 7 * _nbytes((tr, c), F32)),
    )(w, g, m, v)


def _position():
    return lax.axis_index("x"), lax.axis_index("y"), lax.axis_index("c")


def _other_chips(x, y):
    return [(1 - x, y, 2 * (1 - x) + y), (x, 1 - y, 2 * x + (1 - y)), (1 - x, 1 - y, 2 * (1 - x) + (1 - y))]


def _ag_small(v, name):
    r, w = v.shape

    def body(v_ref, out_ref, send_sems, recv_sems, local_sem):
        x, y, c = _position()
        me = 4 * x + 2 * y + c
        mine = pltpu.make_async_copy(v_ref, out_ref.at[me], local_sem)
        mine.start()
        peers = []
        for k in range(1, N_DEV):
            fx, fy, fc = (k >> 2) & 1, (k >> 1) & 1, k & 1
            px = 1 - x if fx else x
            py = 1 - y if fy else y
            pc = 1 - c if fc else c
            peers.append((px, py, pc))
        sends = []
        for k, peer in enumerate(peers):
            cp = pltpu.make_async_remote_copy(src_ref=v_ref, dst_ref=out_ref.at[me], send_sem=send_sems.at[k],
                                              recv_sem=recv_sems.at[k], device_id=peer, device_id_type=MESH)
            cp.start()
            sends.append(cp)
        for k, (px, py, pc) in enumerate(peers):
            pltpu.make_async_remote_copy(src_ref=v_ref, dst_ref=out_ref.at[4 * px + 2 * py + pc], send_sem=send_sems.at[k],
                                         recv_sem=recv_sems.at[k], device_id=(px, py, pc), device_id_type=MESH).wait_recv()
        for cp in sends:
            cp.wait_send()
        mine.wait()

    return pl.pallas_call(
        body, name=name,
        out_shape=jax.ShapeDtypeStruct((N_DEV, r, w), F32),
        in_specs=[pl.BlockSpec(memory_space=pltpu.VMEM)],
        out_specs=pl.BlockSpec(memory_space=pltpu.VMEM),
        scratch_shapes=[pltpu.SemaphoreType.DMA((N_DEV - 1,)), pltpu.SemaphoreType.DMA((N_DEV - 1,)), pltpu.SemaphoreType.DMA],
        compiler_params=_params(None, 10 * _nbytes((r, w), F32)),
    )(v)


ANY = pl.BlockSpec(memory_space=pl.ANY)


def _ag_weights(shards, name):
    n = len(shards)

    def body(*refs):
        gather = _Gather(refs[:n], refs[n:2 * n], *refs[2 * n:])
        gather.start()
        gather.forward()
        gather.finish()

    return pl.pallas_call(
        body, name=name,
        out_shape=_Gather.out_shapes(shards), in_specs=[ANY] * n, out_specs=[ANY] * n,
        scratch_shapes=_Gather.semaphores(n),
    )(*shards)


class _Gather:
    def __init__(self, w_refs, out_refs, send_sems, recv_sems):
        x, y, c = _position()
        q0 = 2 * x + y
        sibling = (x, y, 1 - c)
        self.ici, self.ici_in, self.fwd, self.fwd_in = [], [], [], []
        for k, (w_ref, out_ref) in enumerate(zip(w_refs, out_refs)):
            half = w_ref.shape[0] // 2

            def blk(q, e, out_ref=out_ref, half=half):
                return out_ref.at[q, pl.ds(pl.multiple_of(e * half, 16), half), :]

            def copy(src, dst, i, to):
                return pltpu.make_async_remote_copy(src_ref=src, dst_ref=dst, send_sem=send_sems.at[i], recv_sem=recv_sems.at[i],
                                                    device_id=to, device_id_type=MESH)

            src = w_ref.at[pl.ds(pl.multiple_of(c * half, 16), half), :]
            for j, (cx, cy, qj) in enumerate(_other_chips(x, y)):
                self.ici.append(copy(src, blk(q0, c), 6 * k + j, (cx, cy, c)))
                self.ici_in.append(copy(blk(qj, c), blk(qj, c), 6 * k + j, (cx, cy, c)))
                self.fwd.append(copy(blk(qj, c), blk(qj, c), 6 * k + 3 + j, sibling))
                self.fwd_in.append(copy(blk(qj, 1 - c), blk(qj, 1 - c), 6 * k + 3 + j, sibling))

    @staticmethod
    def out_shapes(shards):
        return [jax.ShapeDtypeStruct((N_CHIP,) + s.shape, s.dtype) for s in shards]

    @staticmethod
    def semaphores(n):
        return [pltpu.SemaphoreType.DMA((6 * n,)), pltpu.SemaphoreType.DMA((6 * n,))]

    def start(self):
        for cp in self.ici:
            cp.start()

    def forward(self):
        for arrived, onward in zip(self.ici_in, self.fwd):
            arrived.wait_recv()
            onward.start()

    def finish(self):
        for cp in self.fwd_in:
            cp.wait_recv()
        for cp in self.ici + self.fwd:
            cp.wait_send()


def _swap_halves_d2d(grads, name):
    n = len(grads)

    def body(*refs):
        g_refs, out_refs = refs[:n], refs[n:2 * n]
        send_sems, recv_sems = refs[2 * n:]
        x, y, c = _position()
        sibling = (x, y, 1 - c)
        cps = []
        for k in range(n):
            cp = pltpu.make_async_remote_copy(src_ref=g_refs[k].at[:, 1 - c], dst_ref=out_refs[k], send_sem=send_sems.at[k],
                                              recv_sem=recv_sems.at[k], device_id=sibling, device_id_type=MESH)
            cp.start()
            cps.append(cp)
        for cp in cps:
            cp.wait_recv()
        for cp in cps:
            cp.wait_send()

    return pl.pallas_call(
        body, name=name,
        out_shape=[jax.ShapeDtypeStruct((N_CHIP,) + g.shape[2:], g.dtype) for g in grads],
        in_specs=[ANY] * n, out_specs=[ANY] * n,
        scratch_shapes=[pltpu.SemaphoreType.DMA((n,)), pltpu.SemaphoreType.DMA((n,))],
    )(*grads)


def _pair_sum(g, a, c_idx, name):
    _, _, rh, cols = g.shape
    tr = rh
    for cand in (256, 128, 64, 32, 16):
        if rh % cand == 0 and rh > cand:
            tr = cand
            break

    def body(c_ref, g_ref, a_ref, o_ref):
        o_ref[...] = (g_ref[...] + a_ref[...]).astype(BF16)

    return pl.pallas_call(
        body, name=name,
        grid_spec=pltpu.PrefetchScalarGridSpec(
            num_scalar_prefetch=1, grid=(N_CHIP, rh // tr),
            in_specs=[pl.BlockSpec((None, None, tr, cols), lambda q, i, c_ref: (q, c_ref[0], i, 0)),
                      pl.BlockSpec((None, tr, cols), lambda q, i, c_ref: (q, i, 0))],
            out_specs=pl.BlockSpec((None, tr, cols), lambda q, i, c_ref: (q, i, 0))),
        out_shape=jax.ShapeDtypeStruct((N_CHIP, rh, cols), BF16),
        compiler_params=_params(("parallel", "parallel"), 10 * _nbytes((tr, cols), F32)),
    )(c_idx, g, a)


def _scatter_partials(parts, name):
    n = len(parts)

    def body(*refs):
        scatter = _Scatter(refs[:n], refs[n:2 * n], *refs[2 * n:])
        scatter.start()
        scatter.finish()

    return pl.pallas_call(
        body, name=name,
        out_shape=_Scatter.out_shapes(parts), in_specs=[ANY] * n, out_specs=[ANY] * n,
        scratch_shapes=_Scatter.semaphores(n),
    )(*parts)


class _Scatter:
    def __init__(self, p_refs, out_refs, send_sems, recv_sems):
        x, y, c = _position()
        self.copies = []
        for k, (p_ref, out_ref) in enumerate(zip(p_refs, out_refs)):
            for j, (cx, cy, qj) in enumerate(_other_chips(x, y)):
                self.copies.append(pltpu.make_async_remote_copy(
                    src_ref=p_ref.at[qj], dst_ref=out_ref.at[j], send_sem=send_sems.at[3 * k + j],
                    recv_sem=recv_sems.at[3 * k + j], device_id=(cx, cy, c), device_id_type=MESH))

    @staticmethod
    def out_shapes(parts):
        return [jax.ShapeDtypeStruct((3,) + p.shape[1:], p.dtype) for p in parts]

    @staticmethod
    def semaphores(n):
        return [pltpu.SemaphoreType.DMA((3 * n,)), pltpu.SemaphoreType.DMA((3 * n,))]

    def start(self):
        for cp in self.copies:
            cp.start()

    def finish(self):
        for cp in self.copies:
            cp.wait_recv()
        for cp in self.copies:
            cp.wait_send()


def _shard_sum(p, b, q_idx, name):
    _, rh, cols = p.shape
    tr = rh
    for cand in (256, 128, 64, 32, 16):
        if rh % cand == 0 and rh > cand:
            tr = cand
            break

    def body(q_ref, p_ref, b_ref, o_ref):
        acc = p_ref[...].astype(F32)
        for j in range(3):
            acc = acc + b_ref[j].astype(F32)
        o_ref[...] = acc

    return pl.pallas_call(
        body, name=name,
        grid_spec=pltpu.PrefetchScalarGridSpec(
            num_scalar_prefetch=1, grid=(rh // tr,),
            in_specs=[pl.BlockSpec((None, tr, cols), lambda i, q_ref: (q_ref[0], i, 0)),
                      pl.BlockSpec((3, tr, cols), lambda i, q_ref: (0, i, 0))],
            out_specs=pl.BlockSpec((tr, cols), lambda i, q_ref: (i, 0))),
        out_shape=jax.ShapeDtypeStruct((rh, cols), F32),
        compiler_params=_params(("parallel",), 8 * _nbytes((tr, cols), F32)),
    )(q_idx, p, b)


def _join_halves(halves):
    n = len(halves)

    def body(*refs):
        h_refs, out_refs = refs[:n], refs[n:2 * n]
        send_sems, recv_sems = refs[2 * n:]
        x, y, c = _position()
        sibling = (x, y, 1 - c)
        cps = []
        for k in range(n):
            cp = pltpu.make_async_remote_copy(src_ref=h_refs[k], dst_ref=out_refs[k], send_sem=send_sems.at[k],
                                              recv_sem=recv_sems.at[k], device_id=sibling, device_id_type=MESH)
            cp.start()
            cps.append(cp)
        for cp in cps:
            cp.wait_recv()
        for cp in cps:
            cp.wait_send()

    return pl.pallas_call(
        body, name="rs_join",
        out_shape=[jax.ShapeDtypeStruct(h.shape, h.dtype) for h in halves],
        in_specs=[ANY] * n, out_specs=[ANY] * n,
        scratch_shapes=[pltpu.SemaphoreType.DMA((n,)), pltpu.SemaphoreType.DMA((n,))],
    )(*halves)


def _cols_from_shards(g):
    q, r, cs = g.shape
    return jnp.transpose(g, (1, 0, 2)).reshape(r, q * cs)


def _cols_to_shards(w):
    r, cfull = w.shape
    return jnp.transpose(w.reshape(r, N_CHIP, cfull // N_CHIP), (1, 0, 2))


def _pad_w_in(w):
    z = lambda n: jnp.zeros((w.shape[0], n), w.dtype)
    q_lat, kv_lat, kpe = w[:, 0:512], w[:, 512:768], w[:, 768:800]
    qd, kd, vd = w[:, 800:1312], w[:, 1312:1824], w[:, 1824:2336]
    return jnp.concatenate([q_lat, qd, kd, vd, kv_lat, z(KPE_OFF), kpe, z(LANE - KPE_OFF - ROPE)], axis=1)


def _unpad_w_in(g):
    return jnp.concatenate([g[:, P_QLAT:P_QLAT + Q_LORA], g[:, P_KVLAT:P_KVLAT + KV_LORA],
                            g[:, P_KPE + KPE_OFF:P_KPE + KPE_OFF + ROPE], g[:, P_QD:P_QD + 3 * DIL_W]], axis=1)


def _pad_w_qb(w):
    w3 = w.reshape(Q_LORA, HEADS, NOPE + ROPE)
    return jnp.pad(w3, ((0, 0), (0, 0), (0, LANE - NOPE - ROPE))).reshape(Q_LORA, HEADS * LANE)


def _unpad_w_qb(g):
    return g.reshape(Q_LORA, HEADS, LANE)[:, :, :NOPE + ROPE].reshape(Q_LORA, HEADS * (NOPE + ROPE))


def _pad_w_kvb(w):
    w3 = w.reshape(KV_LORA, HEADS, 2 * NOPE)
    kp = jnp.pad(w3[:, :, :NOPE], ((0, 0), (0, 0), (0, LANE - NOPE))).reshape(KV_LORA, HEADS * LANE)
    return jnp.concatenate([kp, w3[:, :, NOPE:].reshape(KV_LORA, DIL_W)], axis=1)


def _unpad_w_kvb(g):
    gk = g[:, :HEADS * LANE].reshape(KV_LORA, HEADS, LANE)[:, :, :NOPE]
    gv = g[:, HEADS * LANE:].reshape(KV_LORA, HEADS, NOPE)
    return jnp.concatenate([gk, gv], axis=2).reshape(KV_LORA, HEADS * 2 * NOPE)


def _head_gains(g_q_nope, g_q_pe, g_k_nope, g_k_pe, g_dq, g_dk):
    z = lambda n: jnp.zeros((1, n), F32)
    q1 = jnp.concatenate([g_q_nope, g_q_pe, z(LANE - NOPE - ROPE)], axis=1)
    k1 = jnp.concatenate([g_k_nope, z(LANE - NOPE)], axis=1)
    kpe = jnp.concatenate([z(KPE_OFF), g_k_pe, z(LANE - KPE_OFF - ROPE)], axis=1)
    return dict(q=jnp.tile(q1, (1, HEADS)), k=jnp.tile(k1, (1, HEADS)), kpe=kpe,
                dq=jnp.tile(g_dq, (1, HEADS)), dk=jnp.tile(g_dk, (1, HEADS)))


def kernel(x, c, positions, w_ada, b_ada, g_mix_norm, w_in, g_q_lat, w_q_b, g_kv_lat, w_kv_b, g_mla_q_nope, g_mla_q_pe, g_mla_k_nope, g_mla_k_pe, g_dil_q, g_dil_k, w_o, g_ffn_norm, w_up, w_conv, b_conv, w_down, loss_target, m_w_ada, m_b_ada, m_g_mix_norm, m_w_in, m_g_q_lat, m_w_q_b, m_g_kv_lat, m_w_kv_b, m_g_mla_q_nope, m_g_mla_q_pe, m_g_mla_k_nope, m_g_mla_k_pe, m_g_dil_q, m_g_dil_k, m_w_o, m_g_ffn_norm, m_w_up, m_w_conv, m_b_conv, m_w_down, v_w_ada, v_b_ada, v_g_mix_norm, v_w_in, v_g_q_lat, v_w_q_b, v_g_kv_lat, v_w_kv_b, v_g_mla_q_nope, v_g_mla_q_pe, v_g_mla_k_nope, v_g_mla_k_pe, v_g_dil_q, v_g_dil_k, v_w_o, v_g_ffn_norm, v_w_up, v_w_conv, v_b_conv, v_w_down):
    args = dict(locals())
    weights = {n: args[n][0] for n in ("w_ada", "w_in", "w_q_b", "w_kv_b", "w_o", "w_up", "w_conv", "w_down")}
    small_w = {n: args[n] for n in ("b_ada",) + tuple(n for n, _ in SMALL_WIDTHS)}
    mom_m = {n[2:]: (args[n][0] if args[n].ndim == 3 else args[n]) for n in args if n.startswith("m_")}
    mom_v = {n[2:]: (args[n][0] if args[n].ndim == 3 else args[n]) for n in args if n.startswith("v_")}

    xi, yi, ci = _position()
    q0 = 2 * xi + yi
    me = 4 * xi + 2 * yi + ci
    xs, tgt = x[0], loss_target[0]
    s = xs.shape[0]
    consts = _seg_consts()
    c_idx, q_idx = jnp.reshape(ci, (1,)).astype(I32), jnp.reshape(q0, (1,)).astype(I32)

    def halves(g4):
        q, r, cc = g4.shape
        return g4.reshape(q, 2, r // 2, cc)

    conv_cols = UP_W // N_CHIP
    c_and_taps = _ag_small(jnp.concatenate([c, weights["w_conv"].reshape(1, 3 * conv_cols)], axis=1), "ag_c")[:, 0, :]
    c_all = c_and_taps[:, :D_MODEL]
    w_conv_f = c_and_taps[:, D_MODEL:].reshape(N_CHIP, 2, 3, conv_cols)[:, 0]
    w_conv_f = jnp.transpose(w_conv_f, (1, 0, 2)).reshape(3, UP_W)
    ada_cols = w_ada.shape[2]
    b_shard = lax.dynamic_slice_in_dim(b_ada, q0 * ada_cols, ada_cols, axis=1)
    mod_blk = _ada_fwd(c_all, weights["w_ada"], b_shard)
    mod_all = _ag_small(mod_blk, "ag_mod").reshape(N_CHIP, 2, N_DEV, ada_cols)
    mod = lax.dynamic_index_in_dim(lax.dynamic_index_in_dim(mod_all, ci, 1, False), me, 1, False)
    mod = mod.reshape(1, N_CHIP * ada_cols)
    sh1, sc1, g1, sh2, sc2, g2 = [mod[:, k * D_MODEL:(k + 1) * D_MODEL] for k in range(6)]

    place_own = lambda gs, ws: [lax.dynamic_update_slice_in_dim(g, w[None], q0, axis=0) for g, w in zip(gs, ws)]
    own_first = [weights[n].astype(BF16) for n in ("w_in", "w_q_b", "w_kv_b")]
    own_later = [weights[n].astype(BF16) for n in ("w_o", "w_up", "w_down")]
    gathered = place_own(_ag_weights(own_first, "ag_weights"), own_first)
    w_in_p = _pad_w_in(_cols_from_shards(gathered[0]))
    w_qb_p = _pad_w_qb(_cols_from_shards(gathered[1]))
    w_kvb_p = _pad_w_kvb(_cols_from_shards(gathered[2]))

    gains = _head_gains(g_mla_q_nope, g_mla_q_pe, g_mla_k_nope, g_mla_k_pe, g_dil_q, g_dil_k)
    tab = _rope_tables(positions.reshape(s, 1), *_rope_consts())

    h = _prenorm(xs, g_mix_norm, sc1, sh1, "prenorm")
    proj = _mm(h, w_in_p, "nn", F32, 512, P_COLS, "mm_in")
    ql, kvl = _latnorm(proj, g_q_lat, g_kv_lat)
    q_raw = _mm(ql, w_qb_p, "nn", F32, 512, HEADS * LANE, "mm_qb")
    kv_raw = _mm(kvl, w_kvb_p, "nn", F32, 512, HEADS * LANE + DIL_W, "mm_kvb")
    qm, km, vm, qd, kd, vd = _attn_prep(q_raw, kv_raw, proj, tab, gains, consts)
    scale_m, scale_d = (NOPE + ROPE) ** -0.5, DIL_DIM ** -0.5
    o_m, lse_m, *gathered = _attn_fwd(qm, km, vm, True, scale_m, "attn_mla", gather=own_later[:2])
    o_d, lse_d, *gathered_d = _attn_fwd(qd, kd, vd, False, scale_d, "attn_dil", gather=own_later[2:])
    gathered = place_own(gathered + gathered_d, own_later)
    w_o_f = gathered[0].reshape(D_MODEL, D_MODEL)
    w_up_f = _cols_from_shards(gathered[1])
    w_down_f = gathered[2].reshape(D_FF, D_MODEL)
    mix_in = jnp.concatenate([o_m, o_d], axis=1)
    mix = _mm(mix_in, w_o_f, "nn", F32, 512, D_MODEL, "mm_o")
    x1, h2 = _resid_prenorm(xs, mix, g1, g_ffn_norm, sc2, sh2)
    up = _mm(h2, w_up_f, "nn", F32, 512, CONV_TILE, "mm_up")
    act = _conv_gate(up, w_conv_f, b_conv)
    ffn = _mm(act, w_down_f, "nn", F32, 256, D_MODEL, "mm_down")
    dy, dffn, dg2, loss_part = _final(x1, ffn, tgt, g2)

    da = _mm(dffn, w_down_f, "nt", F32, 512, CONV_TILE, "mm_down_dx")
    gw_down = _mm(act, dffn, "tn", F32, 256, D_MODEL, "mm_down_dw")
    dup_g, dup_v, dbg, dbv, dwg, dwv = _gate_bwd(up, da, w_conv_f, b_conv)
    dup = jnp.concatenate([dup_g, dup_v], axis=1)
    dh2 = _mm(dup, w_up_f, "nt", F32, 256, 512, "mm_up_dx")
    gw_up = _mm(h2, dup, "tn", F32, 512, CONV_TILE, "mm_up_dw")
    dx1, dmix, acc2 = _ffnnorm_bwd(dh2, x1, dy, mix, g_ffn_norm, sc2, g1)
    dmix_in = _mm(dmix, w_o_f, "nt", F32, 512, D_MODEL, "mm_o_dx")
    gw_o = _mm(mix_in, dmix, "tn", F32, 512, D_MODEL, "mm_o_dw")
    early_names = ("w_up", "w_down", "w_o")
    early = [halves(_cols_to_shards(gw_up)), halves(gw_down.reshape(N_CHIP, D_FF // N_CHIP, D_MODEL)),
             halves(gw_o.reshape(N_CHIP, D_MODEL // N_CHIP, D_MODEL))]
    early_sib = _swap_halves_d2d(early, "rs_pair_swap_early")
    early_sums = [_pair_sum(g, a, c_idx, "pair_sum_" + n) for g, a, n in zip(early, early_sib, early_names)]
    dqm, dkm, dvm, *early_recv = _attn_bwd(qm, km, vm, o_m, dmix_in, 0, lse_m, True, scale_m, "attn_mla_bwd",
                                           scatter=early_sums[:1])
    dqd, dkd, dvd, *early_recv_d = _attn_bwd(qd, kd, vd, o_d, dmix_in, DIL_W // LANE, lse_d, False, scale_d,
                                             "attn_dil_bwd", scatter=early_sums[1:])
    early_recv = early_recv + early_recv_d
    dq_raw, dkv_raw, dkpe_b, dqd_b, dkd_b, dvd_b, dgains = _attn_prep_bwd(
        dqm, dkm, dvm, dqd, dkd, dvd, q_raw, kv_raw, proj, tab, gains, consts)
    dql = _mm(dq_raw, w_qb_p, "nt", F32, 512, Q_LORA, "mm_qb_dx")
    gw_qb = _unpad_w_qb(_mm(ql, dq_raw, "tn", F32, Q_LORA, HEADS * LANE, "mm_qb_dw"))
    dkvl = _mm(dkv_raw, w_kvb_p, "nt", F32, 512, KV_LORA, "mm_kvb_dx")
    gw_kvb = _unpad_w_kvb(_mm(kvl, dkv_raw, "tn", F32, KV_LORA, HEADS * LANE + DIL_W, "mm_kvb_dw"))
    dqlat_b, dkvlat_b, dglat = _latnorm_bwd(dql, dkvl, proj, g_q_lat, g_kv_lat)
    dproj = jnp.concatenate([dqlat_b, dqd_b, dkd_b, dvd_b, dkvlat_b, dkpe_b], axis=1)
    dh = _mm(dproj, w_in_p, "nt", F32, 512, D_MODEL, "mm_in_dx")
    gw_in = _unpad_w_in(_mm(h, dproj, "tn", F32, 512, P_COLS, "mm_in_dw"))
    grad_x, acc1 = _mixnorm_bwd(dh, xs, dx1, g_mix_norm, sc1)

    packed = _pack_small(acc1, acc2, dg2, dglat, dgains, dbg, dbv, dwg, dwv, loss_part)
    gathered_small = _ag_small(packed, "ag_small")
    grad_b_ada, *small_grads, gconv_full, loss_sum = _sum_unpack(gathered_small)
    grads = {"b_ada": grad_b_ada}
    grads.update({n: g for (n, _), g in zip(SMALL_WIDTHS, small_grads)})
    shard_cols = UP_W // N_CHIP
    grads["w_conv"] = lax.dynamic_slice_in_dim(gconv_full, q0 * shard_cols, shard_cols, axis=1)
    dmod_all = gathered_small[:, 0, :6 * D_MODEL]
    grads["w_ada"] = _ada_bwd(c_all, lax.dynamic_slice_in_dim(dmod_all, q0 * ada_cols, ada_cols, axis=1))

    late_names = ("w_in", "w_q_b", "w_kv_b")
    late = [halves(_cols_to_shards(gw_in)), halves(_cols_to_shards(gw_qb)), halves(_cols_to_shards(gw_kvb))]
    late_sib = _swap_halves_d2d(late, "rs_pair_swap_late")
    late_sums = [_pair_sum(g, a, c_idx, "pair_sum_" + n) for g, a, n in zip(late, late_sib, late_names)]
    late_recv = _scatter_partials(late_sums, "rs_scatter_late")
    big_names = late_names + early_names
    half_sums = [_shard_sum(p, b, q_idx, "shard_sum_" + n)
                 for p, b, n in zip(late_sums + early_sums, list(late_recv) + list(early_recv), big_names)]
    from_sib = _join_halves(half_sums)
    south = ci == 0
    for n, mine, theirs in zip(big_names, half_sums, from_sib):
        grads[n] = jnp.concatenate([jnp.where(south, mine, theirs), jnp.where(south, theirs, mine)], axis=0)

    delta, new_m, new_v = {}, {}, {}
    for n in ("w_ada", "w_in", "w_q_b", "w_kv_b", "w_o", "w_up", "w_conv", "w_down"):
        operands = (weights[n], grads[n], mom_m[n], mom_v[n])
        if n == "w_ada":
            operands = _in_hbm(*operands)
        delta[n], new_m[n], new_v[n] = _adamw(*operands, "adamw_" + n)
    vec_names = ("b_ada",) + tuple(n for n, _ in SMALL_WIDTHS)
    sd, sm, sv = _adamw_vectors(*[[d_[n] for n in vec_names] for d_ in (small_w, grads, mom_m, mom_v)])
    for k, n in enumerate(vec_names):
        delta[n], new_m[n], new_v[n] = sd[k], sm[k], sv[k]

    loss = loss_sum[0, 0]
    order = ("w_ada", "b_ada", "g_mix_norm", "w_in", "g_q_lat", "w_q_b", "g_kv_lat", "w_kv_b", "g_mla_q_nope", "g_mla_q_pe",
             "g_mla_k_nope", "g_mla_k_pe", "g_dil_q", "g_dil_k", "w_o", "g_ffn_norm", "w_up", "w_conv", "b_conv", "w_down")
    lead = lambda n, z: z[None] if n.startswith("w_") else z
    outs = [loss, grad_x[None]]
    for d_ in (grads, delta, new_m, new_v):
        outs += [lead(n, d_[n]) for n in order]
    return tuple(outs)
```

```python
import functools

import numpy as np
import jax
import jax.numpy as jnp
from jax import lax
from jax.experimental import pallas as pl
from jax.experimental.pallas import tpu as pltpu

F32 = jnp.float32
BF16 = jnp.bfloat16
I32 = jnp.int32

D_MODEL = 1024
HEADS = 8
NOPE = 64
ROPE = 32
Q_LORA = 512
KV_LORA = 256
DIL_DIM = 64
DIL_W = HEADS * DIL_DIM
D_FF = 2816
UP_W = 2 * D_FF
IN_COLS = Q_LORA + KV_LORA + ROPE + 3 * DIL_W
ROPE_THETA = 10000.0
EPS = 1e-6
NEG_INF = -1e30
N_DEV = 8
N_CHIP = 4

ADAM_LR = 0.001
ADAM_B1 = 0.9
ADAM_B2 = 0.999
ADAM_EPS = 1e-08
ADAM_WD = 0.01
ADAM_STEP = 10

LANE = 128
ROW_TILE = 256
ATT_TQ = 512
ATT_TK = 256
LOG2E = 1.4426950408889634
LN2 = 0.6931471805599453
VMEM_CAP = 56 * 1024 * 1024
VMEM_FLOOR = 32 * 1024 * 1024

P_QLAT, P_QD, P_KD, P_VD, P_KVLAT, P_KPE = 0, 512, 1024, 1536, 2048, 2304
P_COLS = 2432
KPE_OFF = 64

NN = (((1,), (0,)), ((), ()))
NT = (((1,), (1,)), ((), ()))
TN = (((0,), (0,)), ((), ()))
HIGHEST = lax.Precision.HIGHEST
MESH = pl.DeviceIdType.MESH


def _params(sem=None, est_bytes=0):
    limit = int(min(max(2 * est_bytes + (4 << 20), VMEM_FLOOR), VMEM_CAP))
    if sem is None:
        return pltpu.CompilerParams(vmem_limit_bytes=limit)
    return pltpu.CompilerParams(dimension_semantics=sem, vmem_limit_bytes=limit)


def _nbytes(shape, dtype):
    return int(np.prod(shape)) * jnp.dtype(dtype).itemsize


def _in_hbm(*xs):
    return [pltpu.with_memory_space_constraint(x, pltpu.HBM) for x in xs]


def _mm(a, b, dims, out_dtype, tm, tn, name):
    if dims == "nn":
        (m, k), (k2, n) = a.shape, b.shape
        a_spec = pl.BlockSpec((tm, k), lambda i, j: (i, 0))
        b_spec = pl.BlockSpec((k, tn), lambda i, j: (0, j))
        dn = NN
    elif dims == "nt":
        (m, k), (n, k2) = a.shape, b.shape
        a_spec = pl.BlockSpec((tm, k), lambda i, j: (i, 0))
        b_spec = pl.BlockSpec((tn, k), lambda i, j: (j, 0))
        dn = NT
    else:
        (k, m), (k2, n) = a.shape, b.shape
        a_spec = pl.BlockSpec((k, tm), lambda i, j: (0, i))
        b_spec = pl.BlockSpec((k, tn), lambda i, j: (0, j))
        dn = TN
    assert k == k2 and m % tm == 0 and n % tn == 0, (name, a.shape, b.shape, tm, tn)

    def body(a_ref, b_ref, o_ref):
        o_ref[...] = lax.dot_general(a_ref[...], b_ref[...], dn, preferred_element_type=F32).astype(o_ref.dtype)

    est = _nbytes((tm, k), a.dtype) + _nbytes((tn, k), b.dtype) + _nbytes((tm, tn), F32) + _nbytes((tm, tn), out_dtype)
    return pl.pallas_call(
        body, name=name,
        grid=(m // tm, n // tn),
        in_specs=[a_spec, b_spec],
        out_specs=pl.BlockSpec((tm, tn), lambda i, j: (i, j)),
        out_shape=jax.ShapeDtypeStruct((m, n), out_dtype),
        compiler_params=_params(("parallel", "parallel"), est),
    )(a, b)


def _seg_consts():
    seg_q = np.zeros((HEADS * LANE, LANE), np.float32)
    inv_q = np.zeros((1, LANE), np.float32)
    seg_k = np.zeros((HEADS * LANE, LANE), np.float32)
    inv_k = np.zeros((1, LANE), np.float32)
    seg_d = np.zeros((DIL_W, LANE), np.float32)
    inv_d = np.zeros((1, LANE), np.float32)
    for h in range(HEADS):
        seg_q[h * LANE:h * LANE + NOPE, 2 * h] = 1.0
        seg_q[h * LANE + NOPE:h * LANE + NOPE + ROPE, 2 * h + 1] = 1.0
        inv_q[0, 2 * h], inv_q[0, 2 * h + 1] = 1.0 / NOPE, 1.0 / ROPE
        seg_k[h * LANE:h * LANE + NOPE, h] = 1.0
        inv_k[0, h] = 1.0 / NOPE
        seg_d[h * DIL_DIM:(h + 1) * DIL_DIM, h] = 1.0
        inv_d[0, h] = 1.0 / DIL_DIM
    fold_q = np.tile(np.eye(LANE, dtype=np.float32), (HEADS, 1))
    fold_d = np.zeros((DIL_W, LANE), np.float32)
    fold_d[np.arange(DIL_W), np.arange(DIL_W) % DIL_DIM] = 1.0
    j = lambda v: jnp.asarray(v)
    b = lambda v: jnp.asarray(v, dtype=BF16)
    return dict(seg_q=b(seg_q), exp_q=b(seg_q.T.copy()), inv_q=j(inv_q), seg_k=b(seg_k), exp_k=b(seg_k.T.copy()),
                inv_k=j(inv_k), seg_d=b(seg_d), exp_d=b(seg_d.T.copy()), inv_d=j(inv_d), fold_q=j(fold_q), fold_d=j(fold_d))


def _rope_consts():
    inv_d = jnp.power(ROPE_THETA, -2.0 * jnp.arange(DIL_DIM // 2, dtype=F32) / DIL_DIM)
    inv_q = jnp.power(ROPE_THETA, -2.0 * jnp.arange(ROPE // 2, dtype=F32) / ROPE)
    lanes = np.arange(LANE)
    freq_d = inv_d[lanes % (DIL_DIM // 2)]
    in_pe = (lanes >= KPE_OFF) & (lanes < KPE_OFF + ROPE)
    freq_q = jnp.where(jnp.asarray(in_pe), inv_q[(lanes - KPE_OFF) % (ROPE // 2)], 0.0)
    sign_d = np.where(lanes % DIL_DIM < DIL_DIM // 2, -1.0, 1.0).astype(np.float32)
    sign_q = np.where(in_pe, np.where((lanes - KPE_OFF) < ROPE // 2, -1.0, 1.0), 0.0).astype(np.float32)
    zeros, ones = np.zeros(LANE, np.float32), np.ones(LANE, np.float32)
    freq = jnp.concatenate([freq_d, freq_d, freq_q, freq_q])[None, :]
    csel = jnp.asarray(np.concatenate([ones, zeros, ones, zeros]))[None, :]
    ssel = jnp.asarray(np.concatenate([zeros, sign_d, zeros, sign_q]))[None, :]
    return freq, csel, ssel


def _full(shape):
    return pl.BlockSpec(shape, lambda *_: (0,) * len(shape))


def _tile_lanes(x, n):
    return jnp.concatenate([x] * n, axis=1)


def _rope_tables(pos_col, freq, csel, ssel):
    s = pos_col.shape[0]

    def body(p_ref, f_ref, c_ref, s_ref, o_ref):
        ang = p_ref[...].astype(F32) * f_ref[...]
        o_ref[...] = c_ref[...] * jnp.cos(ang) + s_ref[...] * jnp.sin(ang)

    return pl.pallas_call(
        body, name="rope_tables", grid=(s // ROW_TILE,),
        in_specs=[pl.BlockSpec((ROW_TILE, 1), lambda i: (i, 0)), _full((1, 4 * LANE)), _full((1, 4 * LANE)), _full((1, 4 * LANE))],
        out_specs=pl.BlockSpec((ROW_TILE, 4 * LANE), lambda i: (i, 0)),
        out_shape=jax.ShapeDtypeStruct((s, 4 * LANE), F32),
        compiler_params=_params(("parallel",)),
    )(pos_col, freq, csel, ssel)


def _rms(x):
    return lax.rsqrt(jnp.mean(x * x, axis=-1, keepdims=True) + EPS)


def _prenorm(x, gain, scale, shift, name):
    s, d = x.shape

    def body(x_ref, g_ref, sc_ref, sh_ref, h_ref):
        xv = x_ref[...]
        h = (xv * _rms(xv)) * g_ref[...] * (1.0 + sc_ref[...]) + sh_ref[...]
        h_ref[...] = h.astype(BF16)

    row = pl.BlockSpec((ROW_TILE, d), lambda i: (i, 0))
    return pl.pallas_call(
        body, name=name, grid=(s // ROW_TILE,),
        in_specs=[row, _full((1, d)), _full((1, d)), _full((1, d))],
        out_specs=row, out_shape=jax.ShapeDtypeStruct((s, d), BF16),
        compiler_params=_params(("parallel",)),
    )(x, gain, scale, shift)


def _latnorm(proj, g_q, g_kv):
    s = proj.shape[0]

    def body(q_ref, kv_ref, gq_ref, gkv_ref, ql_ref, kvl_ref):
        q, kv = q_ref[...], kv_ref[...]
        ql_ref[...] = ((q * _rms(q)) * gq_ref[...]).astype(BF16)
        kvl_ref[...] = ((kv * _rms(kv)) * gkv_ref[...]).astype(BF16)

    return pl.pallas_call(
        body, name="latnorm", grid=(s // ROW_TILE,),
        in_specs=[pl.BlockSpec((ROW_TILE, Q_LORA), lambda i: (i, P_QLAT // Q_LORA)),
                  pl.BlockSpec((ROW_TILE, KV_LORA), lambda i: (i, P_KVLAT // KV_LORA)),
                  _full((1, Q_LORA)), _full((1, KV_LORA))],
        out_specs=[pl.BlockSpec((ROW_TILE, Q_LORA), lambda i: (i, 0)), pl.BlockSpec((ROW_TILE, KV_LORA), lambda i: (i, 0))],
        out_shape=[jax.ShapeDtypeStruct((s, Q_LORA), BF16), jax.ShapeDtypeStruct((s, KV_LORA), BF16)],
        compiler_params=_params(("parallel",)),
    )(proj, proj, g_q, g_kv)


def _dot01(v, mat01):
    hi = v.astype(BF16)
    lo = (v - hi.astype(F32)).astype(BF16)
    return jnp.dot(hi, mat01, preferred_element_type=F32) + jnp.dot(lo, mat01, preferred_element_type=F32)


def _seg_rinv(x, seg, exp, inv):
    r = lax.rsqrt(_dot01(x * x, seg) * inv + EPS)
    return _dot01(r, exp)


def _seg_mean(v, seg, exp, inv):
    return _dot01(_dot01(v, seg) * inv, exp)


def _swap_halves(x, half):
    n = x.shape[1]
    lane = lax.broadcasted_iota(I32, (1, n), 1)
    first = (lane & (2 * half - 1)) < half
    return jnp.where(first, pltpu.roll(x, n - half, 1), pltpu.roll(x, half, 1))


def _rope(x, cos, sin_signed, half):
    return x * cos + _swap_halves(x, half) * sin_signed


def _rope_bwd(dy, cos, sin_signed, half):
    return dy * cos + _swap_halves(dy * sin_signed, half)


def _pe_lane_mask(n):
    lane = lax.broadcasted_iota(I32, (1, n), 1) & (LANE - 1)
    return (lane >= KPE_OFF) & (lane < KPE_OFF + ROPE)


def _attn_prep(q_raw, kv_raw, proj, tab, gains, consts):
    s = q_raw.shape[0]
    hw = HEADS * LANE

    def body(q_ref, kv_ref, kpe_ref, qd_ref, kd_ref, vd_ref, tab_ref,
             gq_ref, gk_ref, gkpe_ref, gdq_ref, gdk_ref,
             segq_ref, expq_ref, invq_ref, segk_ref, expk_ref, invk_ref, segd_ref, expd_ref, invd_ref,
             qm_ref, km_ref, vm_ref, qdo_ref, kdo_ref, vdo_ref):
        tab_v = tab_ref[...]
        cos_d, sin_d = _tile_lanes(tab_v[:, 0:LANE], DIL_W // LANE), _tile_lanes(tab_v[:, LANE:2 * LANE], DIL_W // LANE)
        cos_q1, sin_q1 = tab_v[:, 2 * LANE:3 * LANE], tab_v[:, 3 * LANE:4 * LANE]
        cos_q, sin_q = _tile_lanes(cos_q1, HEADS), _tile_lanes(sin_q1, HEADS)

        q = q_ref[...]
        qn = q * _seg_rinv(q, segq_ref[...], expq_ref[...], invq_ref[...]) * gq_ref[...]
        qm_ref[...] = _rope(qn, cos_q, sin_q, ROPE // 2).astype(BF16)

        kv = kv_ref[...]
        kp = kv[:, :hw]
        kn = kp * _seg_rinv(kp, segk_ref[...], expk_ref[...], invk_ref[...]) * gk_ref[...]
        kpe = kpe_ref[...]
        r_pe = lax.rsqrt(jnp.sum(kpe * kpe, axis=-1, keepdims=True) * (1.0 / ROPE) + EPS)
        kpe_r = _rope(kpe * r_pe * gkpe_ref[...], cos_q1, sin_q1, ROPE // 2)
        km_ref[...] = (kn + _tile_lanes(kpe_r, HEADS)).astype(BF16)
        vm_ref[...] = kv[:, hw:].astype(BF16)

        qd = qd_ref[...]
        qdn = qd * _seg_rinv(qd, segd_ref[...], expd_ref[...], invd_ref[...]) * gdq_ref[...]
        qdo_ref[...] = _rope(qdn, cos_d, sin_d, DIL_DIM // 2).astype(BF16)
        kd = kd_ref[...]
        kdn = kd * _seg_rinv(kd, segd_ref[...], expd_ref[...], invd_ref[...]) * gdk_ref[...]
        kdo_ref[...] = _rope(kdn, cos_d, sin_d, DIL_DIM // 2).astype(BF16)
        vdo_ref[...] = vd_ref[...].astype(BF16)

    t = ROW_TILE
    row = lambda w, cb=0: pl.BlockSpec((t, w), lambda i: (i, cb))
    c = consts
    return pl.pallas_call(
        body, name="attn_prep", grid=(s // t,),
        in_specs=[row(hw), row(hw + DIL_W), row(LANE, P_KPE // LANE), row(DIL_W, P_QD // DIL_W), row(DIL_W, P_KD // DIL_W),
                  row(DIL_W, P_VD // DIL_W), row(4 * LANE),
                  _full((1, hw)), _full((1, hw)), _full((1, LANE)), _full((1, DIL_W)), _full((1, DIL_W)),
                  _full((hw, LANE)), _full((LANE, hw)), _full((1, LANE)), _full((hw, LANE)), _full((LANE, hw)), _full((1, LANE)),
                  _full((DIL_W, LANE)), _full((LANE, DIL_W)), _full((1, LANE))],
        out_specs=[row(hw), row(hw), row(DIL_W), row(DIL_W), row(DIL_W), row(DIL_W)],
        out_shape=[jax.ShapeDtypeStruct((s, hw), BF16), jax.ShapeDtypeStruct((s, hw), BF16)]
        + [jax.ShapeDtypeStruct((s, DIL_W), BF16)] * 4,
        compiler_params=_params(("parallel",), 24 << 20),
    )(*_in_hbm(q_raw, kv_raw, proj, proj, proj, proj, tab), gains["q"], gains["k"], gains["kpe"], gains["dq"], gains["dk"],
      c["seg_q"], c["exp_q"], c["inv_q"], c["seg_k"], c["exp_k"], c["inv_k"], c["seg_d"], c["exp_d"], c["inv_d"])


def _attn_prep_bwd(dqm, dkm, dvm, dqd, dkd, dvd, q_raw, kv_raw, proj, tab, gains, consts):
    s = q_raw.shape[0]
    hw = HEADS * LANE
    n_steps = s // ROW_TILE

    def body(dqm_ref, dkm_ref, dvm_ref, dqd_ref, dkd_ref, dvd_ref, q_ref, kv_ref, kpe_ref, qd_ref, kd_ref, tab_ref,
             gq_ref, gk_ref, gkpe_ref, gdq_ref, gdk_ref,
             segq_ref, expq_ref, invq_ref, segk_ref, expk_ref, invk_ref, segd_ref, expd_ref, invd_ref, foldq_ref, foldd_ref,
             dq_ref, dkv_ref, dkpe_ref, dqdo_ref, dkdo_ref, dvdo_ref, dg_ref, acc_ref):
        i = pl.program_id(0)

        @pl.when(i == 0)
        def _():
            acc_ref[...] = jnp.zeros_like(acc_ref)

        tab_v = tab_ref[...]
        cos_d, sin_d = _tile_lanes(tab_v[:, 0:LANE], DIL_W // LANE), _tile_lanes(tab_v[:, LANE:2 * LANE], DIL_W // LANE)
        cos_q1, sin_q1 = tab_v[:, 2 * LANE:3 * LANE], tab_v[:, 3 * LANE:4 * LANE]
        cos_q, sin_q = _tile_lanes(cos_q1, HEADS), _tile_lanes(sin_q1, HEADS)

        def norm_bwd(x, dyg, gain, seg, exp, inv):
            rinv = _seg_rinv(x, seg, exp, inv)
            xn = x * rinv
            dxn = dyg * gain
            dx = rinv * (dxn - xn * _seg_mean(dxn * xn, seg, exp, inv))
            return dx, jnp.sum(dyg * xn, axis=0, keepdims=True)

        dq, gq_l = norm_bwd(q_ref[...], _rope_bwd(dqm_ref[...], cos_q, sin_q, ROPE // 2), gq_ref[...],
                            segq_ref[...], expq_ref[...], invq_ref[...])
        dq_ref[...] = dq.astype(BF16)

        dkm = dkm_ref[...]
        kv = kv_ref[...]
        dkp, gk_l = norm_bwd(kv[:, :hw], dkm, gk_ref[...], segk_ref[...], expk_ref[...], invk_ref[...])
        dkv_ref[:, :hw] = dkp.astype(BF16)
        dkv_ref[:, hw:] = dvm_ref[...].astype(BF16)

        dkpe_r = dkm[:, 0:LANE]
        for h in range(1, HEADS):
            dkpe_r = dkpe_r + dkm[:, h * LANE:(h + 1) * LANE]
        dkpe_r = jnp.where(_pe_lane_mask(LANE), dkpe_r, 0.0)
        dyg = _rope_bwd(dkpe_r, cos_q1, sin_q1, ROPE // 2)
        kpe = kpe_ref[...]
        r_pe = lax.rsqrt(jnp.sum(kpe * kpe, axis=-1, keepdims=True) * (1.0 / ROPE) + EPS)
        xn = kpe * r_pe
        dxn = dyg * gkpe_ref[...]
        dkpe = r_pe * (dxn - xn * (jnp.sum(dxn * xn, axis=-1, keepdims=True) * (1.0 / ROPE)))
        dkpe_ref[...] = dkpe.astype(BF16)
        gkpe_l = jnp.sum(dyg * xn, axis=0, keepdims=True)

        dqd_v, gdq_l = norm_bwd(qd_ref[...], _rope_bwd(dqd_ref[...], cos_d, sin_d, DIL_DIM // 2), gdq_ref[...],
                                segd_ref[...], expd_ref[...], invd_ref[...])
        dqdo_ref[...] = dqd_v.astype(BF16)
        dkd_v, gdk_l = norm_bwd(kd_ref[...], _rope_bwd(dkd_ref[...], cos_d, sin_d, DIL_DIM // 2), gdk_ref[...],
                                segd_ref[...], expd_ref[...], invd_ref[...])
        dkdo_ref[...] = dkd_v.astype(BF16)
        dvdo_ref[...] = dvd_ref[...].astype(BF16)

        acc_ref[0:1, :] += gq_l
        acc_ref[1:2, :] += gk_l
        acc_ref[2:3, 0:LANE] += gkpe_l
        acc_ref[3:4, 0:DIL_W] += gdq_l
        acc_ref[4:5, 0:DIL_W] += gdk_l

        @pl.when(i == n_steps - 1)
        def _():
            acc = acc_ref[...]
            fq = jnp.dot(acc, foldq_ref[...], precision=HIGHEST, preferred_element_type=F32)
            fd = jnp.dot(acc[:, 0:DIL_W], foldd_ref[...], precision=HIGHEST, preferred_element_type=F32)
            rows = lax.broadcasted_iota(I32, (8, LANE), 0)
            base = jnp.where(rows < 2, fq, jnp.where(rows == 2, acc[:, 0:LANE], fd))
            at0 = pltpu.roll(base, LANE - KPE_OFF, 1)
            dg_ref[...] = jnp.where(rows == 5, pltpu.roll(at0, 5, 0), jnp.where(rows == 2, at0, base))

    t = ROW_TILE
    row = lambda w, cb=0: pl.BlockSpec((t, w), lambda i: (i, cb))
    c = consts
    return pl.pallas_call(
        body, name="attn_prep_bwd", grid=(n_steps,),
        in_specs=[row(hw), row(hw), row(DIL_W), row(DIL_W), row(DIL_W), row(DIL_W),
                  row(hw), row(hw + DIL_W), row(LANE, P_KPE // LANE), row(DIL_W, P_QD // DIL_W), row(DIL_W, P_KD // DIL_W),
                  row(4 * LANE),
                  _full((1, hw)), _full((1, hw)), _full((1, LANE)), _full((1, DIL_W)), _full((1, DIL_W)),
                  _full((hw, LANE)), _full((LANE, hw)), _full((1, LANE)), _full((hw, LANE)), _full((LANE, hw)), _full((1, LANE)),
                  _full((DIL_W, LANE)), _full((LANE, DIL_W)), _full((1, LANE)), _full((hw, LANE)), _full((DIL_W, LANE))],
        out_specs=[row(hw), row(hw + DIL_W), row(LANE), row(DIL_W), row(DIL_W), row(DIL_W), _full((8, LANE))],
        out_shape=[jax.ShapeDtypeStruct((s, hw), BF16), jax.ShapeDtypeStruct((s, hw + DIL_W), BF16),
                   jax.ShapeDtypeStruct((s, LANE), BF16)] + [jax.ShapeDtypeStruct((s, DIL_W), BF16)] * 3
        + [jax.ShapeDtypeStruct((8, LANE), F32)],
        scratch_shapes=[pltpu.VMEM((8, hw), F32)],
        compiler_params=_params(("arbitrary",), 28 << 20),
    )(*_in_hbm(dqm, dkm, dvm, dqd, dkd, dvd, q_raw, kv_raw, proj, proj, proj, tab),
      gains["q"], gains["k"], gains["kpe"], gains["dq"], gains["dk"],
      c["seg_q"], c["exp_q"], c["inv_q"], c["seg_k"], c["exp_k"], c["inv_k"], c["seg_d"], c["exp_d"], c["inv_d"],
      c["fold_q"], c["fold_d"])


def _latnorm_bwd(dql, dkvl, proj, g_q, g_kv):
    s = proj.shape[0]
    n_steps = s // ROW_TILE

    def body(dql_ref, dkvl_ref, q_ref, kv_ref, gq_ref, gkv_ref, dq_ref, dkv_ref, dg_ref):
        i = pl.program_id(0)

        @pl.when(i == 0)
        def _():
            dg_ref[...] = jnp.zeros_like(dg_ref)

        def one(x, dyg, gain):
            r = _rms(x)
            xn = x * r
            dxn = dyg * gain
            dx = r * (dxn - xn * jnp.mean(dxn * xn, axis=-1, keepdims=True))
            return dx, jnp.sum(dyg * xn, axis=0, keepdims=True)

        dq, gq_l = one(q_ref[...], dql_ref[...], gq_ref[...])
        dkv, gkv_l = one(kv_ref[...], dkvl_ref[...], gkv_ref[...])
        dq_ref[...] = dq.astype(BF16)
        dkv_ref[...] = dkv.astype(BF16)
        dg_ref[0:1, :] += gq_l
        dg_ref[1:2, 0:KV_LORA] += gkv_l

    t = ROW_TILE
    return pl.pallas_call(
        body, name="latnorm_bwd", grid=(n_steps,),
        in_specs=[pl.BlockSpec((t, Q_LORA), lambda i: (i, 0)), pl.BlockSpec((t, KV_LORA), lambda i: (i, 0)),
                  pl.BlockSpec((t, Q_LORA), lambda i: (i, P_QLAT // Q_LORA)),
                  pl.BlockSpec((t, KV_LORA), lambda i: (i, P_KVLAT // KV_LORA)),
                  _full((1, Q_LORA)), _full((1, KV_LORA))],
        out_specs=[pl.BlockSpec((t, Q_LORA), lambda i: (i, 0)), pl.BlockSpec((t, KV_LORA), lambda i: (i, 0)), _full((8, Q_LORA))],
        out_shape=[jax.ShapeDtypeStruct((s, Q_LORA), BF16), jax.ShapeDtypeStruct((s, KV_LORA), BF16),
                   jax.ShapeDtypeStruct((8, Q_LORA), F32)],
        compiler_params=_params(("arbitrary",)),
    )(dql, dkvl, proj, proj, g_q, g_kv)


def _resid_prenorm(x, mix, g1, gain, scale, shift):
    s, d = x.shape

    def body(x_ref, mix_ref, g1_ref, g_ref, sc_ref, sh_ref, x1_ref, h_ref):
        x1 = x_ref[...] + g1_ref[...] * mix_ref[...]
        x1_ref[...] = x1
        h_ref[...] = ((x1 * _rms(x1)) * g_ref[...] * (1.0 + sc_ref[...]) + sh_ref[...]).astype(BF16)

    row = pl.BlockSpec((ROW_TILE, d), lambda i: (i, 0))
    vec = _full((1, d))
    return pl.pallas_call(
        body, name="resid_prenorm", grid=(s // ROW_TILE,),
        in_specs=[row, row, vec, vec, vec, vec], out_specs=[row, row],
        out_shape=[jax.ShapeDtypeStruct((s, d), F32), jax.ShapeDtypeStruct((s, d), BF16)],
        compiler_params=_params(("parallel",)),
    )(x, mix, g1, gain, scale, shift)


CONV_TILE = 1408
HALO = 8


def _shift_down(x, halo, k):
    t = x.shape[0]
    row = lax.broadcasted_iota(I32, (t, 1), 0)
    out = pltpu.roll(x, k, 0)
    for r in range(k):
        out = jnp.where(row == r, halo[HALO - k + r:HALO - k + r + 1, :], out)
    return out


def _shift_up(x, halo, k):
    t = x.shape[0]
    row = lax.broadcasted_iota(I32, (t, 1), 0)
    out = pltpu.roll(x, t - k, 0)
    for r in range(k):
        out = jnp.where(row == t - k + r, halo[r:r + 1, :], out)
    return out


def _conv_fwd(x, halo, w, b):
    p1, p2 = _shift_down(x, halo, 1), _shift_down(x, halo, 2)
    u = b + p2 * w[0:1, :]
    u = u + p1 * w[1:2, :]
    u = u + x * w[2:3, :]
    return u, p1, p2


def _sigmoid(x):
    return 1.0 / (1.0 + jnp.exp(-x))


def _conv_gate(up, w_conv, b_conv):
    s = up.shape[0]
    t = ROW_TILE
    nj = D_FF // CONV_TILE
    hb = t // HALO

    def body(g_ref, v_ref, gh_ref, vh_ref, wg_ref, wv_ref, bg_ref, bv_ref, a_ref):
        live = (pl.program_id(0) > 0).astype(F32)
        ug, _, _ = _conv_fwd(g_ref[...], gh_ref[...] * live, wg_ref[...], bg_ref[...])
        uv, _, _ = _conv_fwd(v_ref[...], vh_ref[...] * live, wv_ref[...], bv_ref[...])
        a_ref[...] = (ug * _sigmoid(ug) * uv).astype(BF16)

    main = lambda off: pl.BlockSpec((t, CONV_TILE), lambda i, j: (i, j + off))
    halo = lambda off: pl.BlockSpec((HALO, CONV_TILE), lambda i, j: (jnp.maximum(i * hb - 1, 0), j + off))
    wsp = lambda off: pl.BlockSpec((3, CONV_TILE), lambda i, j: (0, j + off))
    bsp = lambda off: pl.BlockSpec((1, CONV_TILE), lambda i, j: (0, j + off))
    return pl.pallas_call(
        body, name="conv_gate", grid=(s // t, nj),
        in_specs=[main(0), main(nj), halo(0), halo(nj), wsp(0), wsp(nj), bsp(0), bsp(nj)],
        out_specs=pl.BlockSpec((t, CONV_TILE), lambda i, j: (i, j)),
        out_shape=jax.ShapeDtypeStruct((s, D_FF), BF16),
        compiler_params=_params(("parallel", "parallel"), 12 << 20),
    )(up, up, up, up, w_conv, w_conv, b_conv, b_conv)


def _gate_bwd(up, da, w_conv, b_conv):
    s = up.shape[0]
    t = ROW_TILE
    nj = D_FF // CONV_TILE
    hb = t // HALO
    n_i = s // t

    def body(g_ref, v_ref, gh_ref, vh_ref, gn_ref, vn_ref, da_ref, dan_ref, wg_ref, wv_ref, bg_ref, bv_ref,
             dupg_ref, dupv_ref, dbg_ref, dbv_ref, dwg_ref, dwv_ref):
        i = pl.program_id(1)

        @pl.when(i == 0)
        def _():
            for r in (dbg_ref, dbv_ref, dwg_ref, dwv_ref):
                r[...] = jnp.zeros_like(r)

        def d_gate(ug, uv, da_v):
            sg = _sigmoid(ug)
            return da_v * uv * (sg * (1.0 + ug * (1.0 - sg))), da_v * (ug * sg)

        live = (i > 0).astype(F32)
        xg, xv = g_ref[...], v_ref[...]
        wg, wv = wg_ref[...], wv_ref[...]
        ug, g1, g2 = _conv_fwd(xg, gh_ref[...] * live, wg, bg_ref[...])
        uv, v1, v2 = _conv_fwd(xv, vh_ref[...] * live, wv, bv_ref[...])
        dug, duv = d_gate(ug, uv, da_ref[...])

        more = (i < n_i - 1).astype(F32)
        ug_n, _, _ = _conv_fwd(gn_ref[...], xg[t - HALO:, :], wg, bg_ref[...])
        uv_n, _, _ = _conv_fwd(vn_ref[...], xv[t - HALO:, :], wv, bv_ref[...])
        dug_n, duv_n = d_gate(ug_n, uv_n, dan_ref[...] * more)

        def conv_t(du, du_n, w):
            return du * w[2:3, :] + _shift_up(du, du_n, 1) * w[1:2, :] + _shift_up(du, du_n, 2) * w[0:1, :]

        dupg_ref[...] = conv_t(dug, dug_n, wg).astype(BF16)
        dupv_ref[...] = conv_t(duv, duv_n, wv).astype(BF16)
        csum = lambda z: jnp.sum(z, axis=0, keepdims=True)
        dbg_ref[...] += csum(dug)
        dbv_ref[...] += csum(duv)
        dwg_ref[0:1, :] += csum(dug * g2)
        dwg_ref[1:2, :] += csum(dug * g1)
        dwg_ref[2:3, :] += csum(dug * xg)
        dwv_ref[0:1, :] += csum(duv * v2)
        dwv_ref[1:2, :] += csum(duv * v1)
        dwv_ref[2:3, :] += csum(duv * xv)

    last_halo = s // HALO - 1
    main = lambda off: pl.BlockSpec((t, CONV_TILE), lambda j, i: (i, j + off))
    halo = lambda off: pl.BlockSpec((HALO, CONV_TILE), lambda j, i: (jnp.maximum(i * hb - 1, 0), j + off))
    nxt = lambda off: pl.BlockSpec((HALO, CONV_TILE), lambda j, i: (jnp.minimum((i + 1) * hb, last_halo), j + off))
    wsp = lambda off: pl.BlockSpec((3, CONV_TILE), lambda j, i: (0, j + off))
    bsp = lambda off: pl.BlockSpec((1, CONV_TILE), lambda j, i: (0, j + off))
    outs = pl.pallas_call(
        body, name="gate_bwd", grid=(nj, n_i),
        in_specs=[main(0), main(nj), halo(0), halo(nj), nxt(0), nxt(nj), main(0), nxt(0),
                  wsp(0), wsp(nj), bsp(0), bsp(nj)],
        out_specs=[main(0), main(0),
                   pl.BlockSpec((1, CONV_TILE), lambda j, i: (0, j)), pl.BlockSpec((1, CONV_TILE), lambda j, i: (0, j)),
                   pl.BlockSpec((3, CONV_TILE), lambda j, i: (0, j)), pl.BlockSpec((3, CONV_TILE), lambda j, i: (0, j))],
        out_shape=[jax.ShapeDtypeStruct((s, D_FF), BF16), jax.ShapeDtypeStruct((s, D_FF), BF16),
                   jax.ShapeDtypeStruct((1, D_FF), F32), jax.ShapeDtypeStruct((1, D_FF), F32),
                   jax.ShapeDtypeStruct((3, D_FF), F32), jax.ShapeDtypeStruct((3, D_FF), F32)],
        compiler_params=_params(("parallel", "arbitrary"), 24 << 20),
    )(up, up, up, up, up, up, da, da, w_conv, w_conv, b_conv, b_conv)
    return outs


def _final(x1, ffn, tgt, g2):
    s, d = x1.shape
    n_steps = s // ROW_TILE

    def body(x1_ref, f_ref, t_ref, g2_ref, dy_ref, df_ref, dg2_ref, loss_ref, lacc_ref):
        i = pl.program_id(0)

        @pl.when(i == 0)
        def _():
            dg2_ref[...] = jnp.zeros_like(dg2_ref)
            lacc_ref[...] = jnp.zeros_like(lacc_ref)

        f = f_ref[...]
        e = x1_ref[...] + g2_ref[...] * f - t_ref[...]
        dy = e * (1.0 / d)
        dy_ref[...] = dy
        df_ref[...] = (dy * g2_ref[...]).astype(BF16)
        dg2_ref[...] += jnp.sum(dy * f, axis=0, keepdims=True)
        lacc_ref[...] += jnp.sum(e * e, axis=0, keepdims=True)

        @pl.when(i == n_steps - 1)
        def _():
            loss_ref[...] = jnp.sum(lacc_ref[...], axis=1, keepdims=True) * (0.5 / d)

    row = pl.BlockSpec((ROW_TILE, d), lambda i: (i, 0))
    return pl.pallas_call(
        body, name="final", grid=(n_steps,),
        in_specs=[row, row, row, _full((1, d))],
        out_specs=[row, row, _full((1, d)), _full((1, 1))],
        out_shape=[jax.ShapeDtypeStruct((s, d), F32), jax.ShapeDtypeStruct((s, d), BF16),
                   jax.ShapeDtypeStruct((1, d), F32), jax.ShapeDtypeStruct((1, 1), F32)],
        scratch_shapes=[pltpu.VMEM((1, d), F32)],
        compiler_params=_params(("arbitrary",)),
    )(x1, ffn, tgt, g2)


def _ffnnorm_bwd(dh2, x1, dy, mix, gain, scale, g1):
    s, d = x1.shape
    n_steps = s // ROW_TILE

    def body(dh_ref, x_ref, dy_ref, mix_ref, g_ref, sc_ref, g1_ref, dx_ref, dm_ref, acc_ref):
        i = pl.program_id(0)

        @pl.when(i == 0)
        def _():
            acc_ref[...] = jnp.zeros_like(acc_ref)

        dh, x = dh_ref[...], x_ref[...]
        r = _rms(x)
        xn = x * r
        dn = dh * (1.0 + sc_ref[...])
        dxn = dn * g_ref[...]
        dx = dy_ref[...] + r * (dxn - xn * jnp.mean(dxn * xn, axis=-1, keepdims=True))
        dx_ref[...] = dx
        dm_ref[...] = (dx * g1_ref[...]).astype(BF16)
        csum = lambda z: jnp.sum(z, axis=0, keepdims=True)
        acc_ref[0:1, :] += csum(dh)
        acc_ref[1:2, :] += csum(dh * (xn * g_ref[...]))
        acc_ref[2:3, :] += csum(dn * xn)
        acc_ref[3:4, :] += csum(dx * mix_ref[...])

    row = pl.BlockSpec((ROW_TILE, d), lambda i: (i, 0))
    vec = _full((1, d))
    return pl.pallas_call(
        body, name="ffnnorm_bwd", grid=(n_steps,),
        in_specs=[row, row, row, row, vec, vec, vec],
        out_specs=[row, row, _full((8, d))],
        out_shape=[jax.ShapeDtypeStruct((s, d), F32), jax.ShapeDtypeStruct((s, d), BF16), jax.ShapeDtypeStruct((8, d), F32)],
        compiler_params=_params(("arbitrary",)),
    )(dh2, x1, dy, mix, gain, scale, g1)


def _mixnorm_bwd(dh, x, dx1, gain, scale):
    s, d = x.shape
    n_steps = s // ROW_TILE

    def body(dh_ref, x_ref, dx1_ref, g_ref, sc_ref, gx_ref, acc_ref):
        i = pl.program_id(0)

        @pl.when(i == 0)
        def _():
            acc_ref[...] = jnp.zeros_like(acc_ref)

        dh, x = dh_ref[...], x_ref[...]
        r = _rms(x)
        xn = x * r
        dn = dh * (1.0 + sc_ref[...])
        dxn = dn * g_ref[...]
        gx_ref[...] = dx1_ref[...] + r * (dxn - xn * jnp.mean(dxn * xn, axis=-1, keepdims=True))
        csum = lambda z: jnp.sum(z, axis=0, keepdims=True)
        acc_ref[0:1, :] += csum(dh)
        acc_ref[1:2, :] += csum(dh * (xn * g_ref[...]))
        acc_ref[2:3, :] += csum(dn * xn)

    row = pl.BlockSpec((ROW_TILE, d), lambda i: (i, 0))
    vec = _full((1, d))
    return pl.pallas_call(
        body, name="mixnorm_bwd", grid=(n_steps,),
        in_specs=[row, row, row, vec, vec],
        out_specs=[row, _full((8, d))],
        out_shape=[jax.ShapeDtypeStruct((s, d), F32), jax.ShapeDtypeStruct((8, d), F32)],
        compiler_params=_params(("arbitrary",)),
    )(dh, x, dx1, gain, scale)


def _key_count(d, dilated):
    if not dilated:
        return jnp.where(d >= 0, 1.0, 0.0)
    one = lambda cond: jnp.where(cond, 1.0, 0.0)
    cnt = one(d <= 128) + one(((d & 3) == 0) & (d <= 512)) + one((d & 15) == 0)
    return jnp.where(d >= 0, cnt, 0.0)


def _block_kinds(mla):
    return (0, "diag", "none") if mla else (512, "near", "far")


NEAR_OFFSETS = 4


def _scores_t(ka, qa, scale, kind, rel_t, offset, near_tabs=None):
    return _mask_scores(lax.dot_general(ka, qa, NT, preferred_element_type=F32), scale, kind, rel_t, offset, near_tabs)


def _fill_near_tables(bias_ref, cnt_ref, rel_t):
    for idx in range(NEAR_OFFSETS):
        cnt = _key_count(rel_t + (idx - 1) * ATT_TK, True)
        cnt_ref[idx] = cnt
        bias_ref[idx] = jnp.where(cnt > 0.0, 0.0, NEG_INF)


def _mask_scores(products, scale, kind, rel_t, offset, near_tabs=None):
    st = products * (scale * LOG2E)
    cnt = None
    if kind == "diag":
        st = jnp.where(rel_t + offset >= 0, st, NEG_INF)
    elif kind == "far":
        st = jnp.where((rel_t & 15) == 0, st, NEG_INF)
    elif kind == "near":
        bias_ref, cnt_ref = near_tabs
        idx = offset // ATT_TK + 1
        st = st + bias_ref[idx]
        cnt = cnt_ref[idx]
    return st, cnt


def _attn_fwd(q, k, v, mla, scale, name, gather=()):
    s = q.shape[0]
    qw = 2 * LANE if mla else LANE
    tq, tk = ATT_TQ, ATT_TK
    reach, kind_near, kind_far = _block_kinds(mla)
    assert s % tq == 0 and tq % tk == 0 and reach % tk == 0 and (mla or (reach + tq) // tk == NEAR_OFFSETS)
    ng = len(gather)
    last_step = HEADS // 2 - 1

    def body(*refs):
        q_ref, k_ref, v_ref = refs[:3]
        o_ref, lse_ref = refs[3 + ng:5 + ng]
        vt_ref, st_ref = refs[5 + 2 * ng:7 + 2 * ng]
        near_tabs = None if mla else refs[7 + 2 * ng:9 + 2 * ng]
        n_tabs = 0 if mla else 2
        comm = (refs[3:3 + ng], refs[5 + ng:5 + 2 * ng]) + tuple(refs[7 + n_tabs + 2 * ng:])
        if ng:
            @pl.when(pl.program_id(0) == 0)
            def _():
                _Gather(*comm).start()

            @pl.when(pl.program_id(0) == last_step)
            def _():
                _Gather(*comm).forward()

        lane = lax.broadcasted_iota(I32, (1, LANE), 1)
        rel_t = lax.broadcasted_iota(I32, (tk, tq), 1) - lax.broadcasted_iota(I32, (tk, tq), 0)
        if not mla:
            _fill_near_tables(*near_tabs, rel_t)

        def transpose_v(j, carry):
            c0 = pl.multiple_of(j * tk, tk)
            vt_ref[:, pl.ds(c0, tk)] = v_ref[pl.ds(c0, tk), :].astype(F32).T.astype(BF16)
            return carry

        lax.fori_loop(0, s // tk, transpose_v, 0)

        def q_block(qi, carry):
            r0 = pl.multiple_of(qi * tq, tq)
            kcols = [slice(a * LANE, (a + 1) * LANE) if mla else slice(0, LANE) for a in range(2)]
            qas = [q_ref[pl.ds(r0, tq), kcols[a]] for a in range(2)]
            if not mla:
                qas = [jnp.where(lane < DIL_DIM, qas[0], jnp.zeros_like(qas[0])),
                       jnp.where(lane >= DIL_DIM, qas[1], jnp.zeros_like(qas[1]))]

            n_k = (r0 + tq) // tk

            def products(kj):
                c0 = pl.multiple_of(kj * tk, tk)
                return [lax.dot_general(k_ref[pl.ds(c0, tk), kcols[a]], qas[a], NT, preferred_element_type=F32)
                        for a in range(2)]

            for a, pr in enumerate(products(0)):
                st_ref[0, a] = pr

            def k_block(kj, c, kind):
                c0 = pl.multiple_of(kj * tk, tk)
                slot = kj & 1
                ahead = products(jnp.minimum(kj + 1, n_k - 1))
                out = []
                for a in range(2):
                    m, l, acc = c[a]
                    st, cnt = _mask_scores(st_ref[slot, a], scale, kind, rel_t, r0 - c0, near_tabs)
                    st_ref[1 - slot, a] = ahead[a]
                    m_new = jnp.maximum(m, jnp.max(st, axis=0, keepdims=True))
                    alpha = jnp.exp2(m - m_new)
                    p = jnp.exp2(st - m_new)
                    if cnt is not None:
                        p = p * cnt
                    l = alpha * l + jnp.sum(p, axis=0, keepdims=True)
                    vt = vt_ref[a * DIL_DIM:(a + 1) * DIL_DIM, pl.ds(c0, tk)]
                    acc = alpha * acc + jnp.dot(vt, p.astype(BF16), preferred_element_type=F32)
                    out.append((m_new, l, acc))
                return tuple(out)

            one = (jnp.full((1, tq), NEG_INF, F32), jnp.zeros((1, tq), F32), jnp.zeros((DIL_DIM, tq), F32))
            first_near = jnp.maximum((r0 - reach) // tk, 0)
            c = lax.fori_loop(0, first_near, functools.partial(k_block, kind=kind_far), (one, one))
            res = lax.fori_loop(first_near, (r0 + tq) // tk, functools.partial(k_block, kind=kind_near), c)
            o_t = jnp.concatenate([res[a][2] / res[a][1] for a in range(2)], axis=0)
            o_ref[pl.ds(r0, tq), :] = o_t.T.astype(BF16)
            for a in range(2):
                lse_ref[a, :, pl.ds(r0, tq)] = res[a][0] * LN2 + jnp.log(res[a][1])
            return carry

        lax.fori_loop(0, s // tq, q_block, 0)

        if ng:
            @pl.when(pl.program_id(0) == last_step)
            def _():
                _Gather(*comm).finish()

    return pl.pallas_call(
        body, name=name, grid=(HEADS // 2,),
        in_specs=[pl.BlockSpec((s, qw), lambda h: (0, h)), pl.BlockSpec((s, qw), lambda h: (0, h)),
                  pl.BlockSpec((s, LANE), lambda h: (0, h))] + [ANY] * ng,
        out_specs=[pl.BlockSpec((s, LANE), lambda h: (0, h)), pl.BlockSpec((2, 1, s), lambda h: (h, 0, 0))] + [ANY] * ng,
        out_shape=[jax.ShapeDtypeStruct((s, DIL_W), BF16), jax.ShapeDtypeStruct((HEADS, 1, s), F32)] + _Gather.out_shapes(gather),
        scratch_shapes=[pltpu.VMEM((LANE, s), BF16), pltpu.VMEM((2, 2, tk, tq), F32)]
        + ([] if mla else [pltpu.VMEM((NEAR_OFFSETS, tk, tq), F32)] * 2) + (_Gather.semaphores(ng) if ng else []),
        compiler_params=_params(("arbitrary",) if ng else ("parallel",), 12 << 20),
    )(*_in_hbm(q, k, v), *gather)


def _attn_bwd(q, k, v, o, do, do_block0, lse, mla, scale, name, scatter=()):
    s = q.shape[0]
    qw = 2 * LANE if mla else LANE
    tq, tk = ATT_TQ, ATT_TK
    nq = s // tq
    reach, kind_near, kind_far = _block_kinds(mla)
    assert s % tq == 0 and tq % tk == 0
    ns = len(scatter)
    last_step = HEADS // 2 - 1

    def body(*refs):
        q_ref, k_ref, v_ref, o_ref, do_ref, lse_ref = refs[:6]
        dq_ref, dk_ref, dv_ref = refs[6 + ns:9 + ns]
        kt_ref, dot_ref, dob_ref, dqt_ref, delta_ref, lse2_ref = refs[9 + 2 * ns:15 + 2 * ns]
        near_tabs = None if mla else refs[15 + 2 * ns:17 + 2 * ns]
        n_tabs = 0 if mla else 2
        comm = (refs[6:6 + ns], refs[9 + ns:9 + 2 * ns]) + tuple(refs[15 + n_tabs + 2 * ns:])
        if ns:
            @pl.when(pl.program_id(0) == 0)
            def _():
                _Scatter(*comm).start()

        lane = lax.broadcasted_iota(I32, (1, LANE), 1)
        row = lax.broadcasted_iota(I32, (LANE, 1), 0)
        rel_t = lax.broadcasted_iota(I32, (tk, tq), 1) - lax.broadcasted_iota(I32, (tk, tq), 0)
        if not mla:
            _fill_near_tables(*near_tabs, rel_t)

        def prepare(j, carry):
            c0 = pl.multiple_of(j * tk, tk)
            do_blk = do_ref[pl.ds(c0, tk), :]
            dob_ref[pl.ds(c0, tk), :] = do_blk.astype(BF16)
            do_t = do_blk.T
            dot_ref[:, pl.ds(c0, tk)] = do_t.astype(BF16)
            prod = do_t * o_ref[pl.ds(c0, tk), :].astype(F32).T
            delta_ref[0, :, pl.ds(c0, tk)] = jnp.sum(prod[0:DIL_DIM], axis=0, keepdims=True)
            delta_ref[1, :, pl.ds(c0, tk)] = jnp.sum(prod[DIL_DIM:LANE], axis=0, keepdims=True)
            for w in range(qw // LANE):
                kt_ref[w * LANE:(w + 1) * LANE, pl.ds(c0, tk)] = (
                    k_ref[pl.ds(c0, tk), w * LANE:(w + 1) * LANE].astype(F32).T.astype(BF16))
            return carry

        lax.fori_loop(0, s // tk, prepare, 0)
        dqt_ref[...] = jnp.zeros_like(dqt_ref)
        lse2_ref[...] = lse_ref[...] * LOG2E

        sels = [lane < DIL_DIM, lane >= DIL_DIM]
        rsels = [row < DIL_DIM, row >= DIL_DIM]
        cols = [slice(a * LANE, (a + 1) * LANE) if mla else slice(0, LANE) for a in range(2)]

        def k_block(kj, carry):
            c0 = pl.multiple_of(kj * tk, tk)
            kas = [k_ref[pl.ds(c0, tk), cols[a]] for a in range(2)]
            kts = [kt_ref[cols[a], pl.ds(c0, tk)] for a in range(2)]
            if not mla:
                kas = [jnp.where(sels[a], kas[a], jnp.zeros_like(kas[a])) for a in range(2)]
                kts = [jnp.where(rsels[a], kts[a], jnp.zeros_like(kts[a])) for a in range(2)]
            vb = v_ref[pl.ds(c0, tk), :]
            vbs = [jnp.where(sels[a], vb, jnp.zeros_like(vb)) for a in range(2)]

            first = c0 // tq

            def q_block(qi, c, kind):
                r0 = pl.multiple_of(qi * tq, tq)
                out, dq_parts = [], []
                for a in range(2):
                    dk_acc, dv_acc = c[a]
                    qa = q_ref[pl.ds(r0, tq), cols[a]]
                    st, cnt = _scores_t(kas[a], qa, scale, kind, rel_t, r0 - c0, near_tabs)
                    p = jnp.exp2(st - lse2_ref[a, :, pl.ds(r0, tq)])
                    if cnt is not None:
                        p = p * cnt
                    dp = jnp.dot(vbs[a], dot_ref[:, pl.ds(r0, tq)], preferred_element_type=F32)
                    ds = (p * (dp - delta_ref[a, :, pl.ds(r0, tq)]) * scale).astype(BF16)
                    dv_acc = dv_acc + jnp.dot(p.astype(BF16), dob_ref[pl.ds(r0, tq), :], preferred_element_type=F32)
                    dk_acc = dk_acc + jnp.dot(ds, qa, preferred_element_type=F32)
                    dq_parts.append(jnp.dot(kts[a], ds, preferred_element_type=F32))
                    out.append((dk_acc, dv_acc))
                if mla:
                    for a in range(2):
                        dqt_ref[cols[a], pl.ds(r0, tq)] += dq_parts[a]
                else:
                    dqt_ref[:, pl.ds(r0, tq)] += dq_parts[0] + dq_parts[1]
                return tuple(out)

            zero = jnp.zeros((tk, LANE), F32)
            last_near = jnp.minimum((c0 + tk - 1 + reach) // tq + 1, nq)
            c = lax.fori_loop(first, last_near, functools.partial(q_block, kind=kind_near), ((zero, zero), (zero, zero)))
            (dk0, dv0), (dk1, dv1) = lax.fori_loop(last_near, nq, functools.partial(q_block, kind=kind_far), c)
            if mla:
                dk_ref[pl.ds(c0, tk), cols[0]] = dk0
                dk_ref[pl.ds(c0, tk), cols[1]] = dk1
            else:
                dk_ref[pl.ds(c0, tk), :] = jnp.where(sels[0], dk0, dk1)
            dv_ref[pl.ds(c0, tk), :] = jnp.where(sels[0], dv0, dv1)
            return carry

        lax.fori_loop(0, s // tk, k_block, 0)

        def write_dq(j, carry):
            c0 = pl.multiple_of(j * tk, tk)
            for w in range(qw // LANE):
                dq_ref[pl.ds(c0, tk), w * LANE:(w + 1) * LANE] = dqt_ref[w * LANE:(w + 1) * LANE, pl.ds(c0, tk)].T
            return carry

        lax.fori_loop(0, s // tk, write_dq, 0)

        if ns:
            @pl.when(pl.program_id(0) == last_step)
            def _():
                _Scatter(*comm).finish()

    b0 = do_block0
    return pl.pallas_call(
        body, name=name, grid=(HEADS // 2,),
        in_specs=[pl.BlockSpec((s, qw), lambda h: (0, h)), pl.BlockSpec((s, qw), lambda h: (0, h)),
                  pl.BlockSpec((s, LANE), lambda h: (0, h)), pl.BlockSpec((s, LANE), lambda h: (0, h)),
                  pl.BlockSpec((s, LANE), lambda h: (0, h + b0)), pl.BlockSpec((2, 1, s), lambda h: (h, 0, 0))] + [ANY] * ns,
        out_specs=[pl.BlockSpec((s, qw), lambda h: (0, h)), pl.BlockSpec((s, qw), lambda h: (0, h)),
                   pl.BlockSpec((s, LANE), lambda h: (0, h))] + [ANY] * ns,
        out_shape=[jax.ShapeDtypeStruct(q.shape, F32), jax.ShapeDtypeStruct(k.shape, F32), jax.ShapeDtypeStruct((s, DIL_W), F32)]
        + _Scatter.out_shapes(scatter),
        scratch_shapes=[pltpu.VMEM((qw, s), BF16), pltpu.VMEM((LANE, s), BF16), pltpu.VMEM((s, LANE), BF16),
                        pltpu.VMEM((qw, s), F32), pltpu.VMEM((2, 1, s), F32), pltpu.VMEM((2, 1, s), F32)]
        + ([] if mla else [pltpu.VMEM((NEAR_OFFSETS, tk, tq), F32)] * 2) + (_Scatter.semaphores(ns) if ns else []),
        compiler_params=_params(("arbitrary",) if ns else ("parallel",), 24 << 20),
    )(*_in_hbm(q, k, v, o, do, lse), *scatter)


def _ada_fwd(c_all, w_shard, b_shard):
    n, d = c_all.shape
    cols = w_shard.shape[1]

    def body(c_ref, w_ref, b_ref, o_ref):
        cv = c_ref[...]
        sc = (cv * _sigmoid(cv)).astype(BF16)
        o_ref[...] = jnp.dot(sc, w_ref[...].astype(BF16), preferred_element_type=F32) + b_ref[...]

    return pl.pallas_call(
        body, name="ada_fwd", out_shape=jax.ShapeDtypeStruct((n, cols), F32),
        compiler_params=_params(None, 16 << 20),
    )(c_all, w_shard, b_shard)


def _ada_bwd(c_all, dmod_shard):
    n, d = c_all.shape
    cols = dmod_shard.shape[1]

    def body(c_ref, g_ref, o_ref):
        cv = c_ref[...]
        o_ref[...] = lax.dot_general(cv * _sigmoid(cv), g_ref[...], TN, precision=HIGHEST, preferred_element_type=F32)

    return pl.pallas_call(
        body, name="ada_bwd", out_shape=jax.ShapeDtypeStruct((d, cols), F32),
        compiler_params=_params(None, 16 << 20),
    )(c_all, dmod_shard)


SMALL_WIDTHS = (("g_mix_norm", D_MODEL), ("g_q_lat", Q_LORA), ("g_kv_lat", KV_LORA), ("g_mla_q_nope", NOPE),
                ("g_mla_q_pe", ROPE), ("g_mla_k_nope", NOPE), ("g_mla_k_pe", ROPE), ("g_dil_q", DIL_DIM),
                ("g_dil_k", DIL_DIM), ("g_ffn_norm", D_MODEL), ("b_conv", UP_W))


def _small_layout():
    pieces = (("dmod", 6 * D_MODEL),) + SMALL_WIDTHS + tuple(("w_conv%d" % k, UP_W) for k in range(3)) + (("loss", 1),)
    layout, off = {}, 0
    for name, width in pieces:
        layout[name] = (width, off)
        off += -(-width // LANE) * LANE
    return layout, off


def _pack_small(acc1, acc2, dg2, dglat, dgains, dbg, dbv, dwg, dwv, loss_part):
    layout, total = _small_layout()

    def body(a1, a2, g2, gl, gg, bg, bv, wg, wv, ls, o_ref):
        o_ref[...] = jnp.zeros_like(o_ref)

        def put(name, src, shift=0):
            start = layout[name][1] + shift
            o_ref[:, start:start + src.shape[1]] = src

        for k, src in enumerate((a1[0:1, :], a1[1:2, :], a2[3:4, :], a2[0:1, :], a2[1:2, :], g2[...])):
            put("dmod", src, k * D_MODEL)
        put("g_mix_norm", a1[2:3, :])
        put("g_q_lat", gl[0:1, :])
        put("g_kv_lat", gl[1:2, 0:KV_LORA])
        put("g_mla_q_nope", gg[0:1, 0:NOPE])
        put("g_mla_q_pe", gg[5:6, 0:ROPE])
        put("g_mla_k_nope", gg[1:2, 0:NOPE])
        put("g_mla_k_pe", gg[2:3, 0:ROPE])
        put("g_dil_q", gg[3:4, 0:DIL_DIM])
        put("g_dil_k", gg[4:5, 0:DIL_DIM])
        put("g_ffn_norm", a2[2:3, :])
        put("b_conv", bg[...])
        put("b_conv", bv[...], D_FF)
        for k in range(3):
            put("w_conv%d" % k, wg[k:k + 1, :])
            put("w_conv%d" % k, wv[k:k + 1, :], D_FF)
        put("loss", ls[...])

    ins = (acc1, acc2, dg2, dglat, dgains, dbg, dbv, dwg, dwv, loss_part)
    return pl.pallas_call(
        body, name="pack_small", grid=(1,), in_specs=[_full(a.shape) for a in ins], out_specs=_full((1, total)),
        out_shape=jax.ShapeDtypeStruct((1, total), F32),
        compiler_params=_params(("arbitrary",), 2 << 20),
    )(*_in_hbm(*ins))


def _sum_unpack(g):
    n_dev, _, total = g.shape
    layout, _ = _small_layout()

    def body(g_ref, *refs):
        o_refs, s_ref = refs[:-1], refs[-1]
        acc = g_ref[0]
        for k in range(1, n_dev):
            acc = acc + g_ref[k]
        s_ref[...] = acc
        take = lambda name: s_ref[:, layout[name][1]:layout[name][1] + layout[name][0]]
        o_refs[0][...] = take("dmod")
        for i, (name, _) in enumerate(SMALL_WIDTHS):
            o_refs[1 + i][...] = take(name)
        for k in range(3):
            o_refs[-2][k:k + 1, :] = take("w_conv%d" % k)
        o_refs[-1][...] = take("loss")

    shapes = [(1, 6 * D_MODEL)] + [(1, w) for _, w in SMALL_WIDTHS] + [(3, UP_W), (1, 1)]
    return pl.pallas_call(
        body, name="sum_unpack", out_shape=[jax.ShapeDtypeStruct(sh, F32) for sh in shapes],
        scratch_shapes=[pltpu.VMEM((1, total), F32)],
        compiler_params=_params(None, 4 << 20),
    )(g)


def _adamw_math(w, g, m, v):
    mn = ADAM_B1 * m + (1.0 - ADAM_B1) * g
    vn = ADAM_B2 * v + (1.0 - ADAM_B2) * (g * g)
    m_hat = mn / (1.0 - ADAM_B1 ** ADAM_STEP)
    v_hat = vn / (1.0 - ADAM_B2 ** ADAM_STEP)
    return -ADAM_LR * (m_hat / (jnp.sqrt(v_hat) + ADAM_EPS) + ADAM_WD * w), mn, vn


def _adamw_vectors(ws, gs, ms, vs):
    k = len(ws)

    def body(*refs):
        for i in range(k):
            d, mn, vn = _adamw_math(refs[i][...], refs[k + i][...], refs[2 * k + i][...], refs[3 * k + i][...])
            refs[4 * k + i][...] = d
            refs[5 * k + i][...] = mn
            refs[6 * k + i][...] = vn

    blocks = [_full(w.shape) for w in ws]
    outs = pl.pallas_call(
        body, name="adamw_vectors", grid=(1,), in_specs=blocks * 4, out_specs=blocks * 3,
        out_shape=[jax.ShapeDtypeStruct(w.shape, F32) for w in ws] * 3,
        compiler_params=_params(("arbitrary",), 2 << 20),
    )(*_in_hbm(*ws, *gs, *ms, *vs))
    return outs[:k], outs[k:2 * k], outs[2 * k:]


def _adamw(w, g, m, v, name):
    r, c = w.shape
    tr = r
    for cand in (256, 128, 64, 32, 16, 8):
        if r % cand == 0 and r > cand:
            tr = cand
            break

    def body(w_ref, g_ref, m_ref, v_ref, d_ref, mo_ref, vo_ref):
        d_ref[...], mo_ref[...], vo_ref[...] = _adamw_math(w_ref[...], g_ref[...], m_ref[...], v_ref[...])

    blk = pl.BlockSpec((tr, c), lambda i: (i, 0))
    return pl.pallas_call(
        body, name=name, grid=(r // tr,), in_specs=[blk] * 4, out_specs=[blk] * 3,
        out_shape=[jax.ShapeDtypeStruct((r, c), F32)] * 3,
        compiler_params=_params(("parallel",), 7 * _nbytes((tr, c), F32)),
    )(w, g, m, v)


def _position():
    return lax.axis_index("x"), lax.axis_index("y"), lax.axis_index("c")


def _other_chips(x, y):
    return [(1 - x, y, 2 * (1 - x) + y), (x, 1 - y, 2 * x + (1 - y)), (1 - x, 1 - y, 2 * (1 - x) + (1 - y))]


def _ag_small(v, name):
    r, w = v.shape

    def body(v_ref, out_ref, send_sems, recv_sems, local_sem):
        x, y, c = _position()
        me = 4 * x + 2 * y + c
        mine = pltpu.make_async_copy(v_ref, out_ref.at[me], local_sem)
        mine.start()
        peers = []
        for k in range(1, N_DEV):
            fx, fy, fc = (k >> 2) & 1, (k >> 1) & 1, k & 1
            px = 1 - x if fx else x
            py = 1 - y if fy else y
            pc = 1 - c if fc else c
            peers.append((px, py, pc))
        sends = []
        for k, peer in enumerate(peers):
            cp = pltpu.make_async_remote_copy(src_ref=v_ref, dst_ref=out_ref.at[me], send_sem=send_sems.at[k],
                                              recv_sem=recv_sems.at[k], device_id=peer, device_id_type=MESH)
            cp.start()
            sends.append(cp)
        for k, (px, py, pc) in enumerate(peers):
            pltpu.make_async_remote_copy(src_ref=v_ref, dst_ref=out_ref.at[4 * px + 2 * py + pc], send_sem=send_sems.at[k],
                                         recv_sem=recv_sems.at[k], device_id=(px, py, pc), device_id_type=MESH).wait_recv()
        for cp in sends:
            cp.wait_send()
        mine.wait()

    return pl.pallas_call(
        body, name=name,
        out_shape=jax.ShapeDtypeStruct((N_DEV, r, w), F32),
        in_specs=[pl.BlockSpec(memory_space=pltpu.VMEM)],
        out_specs=pl.BlockSpec(memory_space=pltpu.VMEM),
        scratch_shapes=[pltpu.SemaphoreType.DMA((N_DEV - 1,)), pltpu.SemaphoreType.DMA((N_DEV - 1,)), pltpu.SemaphoreType.DMA],
        compiler_params=_params(None, 10 * _nbytes((r, w), F32)),
    )(v)


ANY = pl.BlockSpec(memory_space=pl.ANY)


def _ag_weights(shards, name):
    n = len(shards)

    def body(*refs):
        gather = _Gather(refs[:n], refs[n:2 * n], *refs[2 * n:])
        gather.start()
        gather.forward()
        gather.finish()

    return pl.pallas_call(
        body, name=name,
        out_shape=_Gather.out_shapes(shards), in_specs=[ANY] * n, out_specs=[ANY] * n,
        scratch_shapes=_Gather.semaphores(n),
    )(*shards)


class _Gather:
    def __init__(self, w_refs, out_refs, send_sems, recv_sems):
        x, y, c = _position()
        q0 = 2 * x + y
        sibling = (x, y, 1 - c)
        self.ici, self.ici_in, self.fwd, self.fwd_in = [], [], [], []
        for k, (w_ref, out_ref) in enumerate(zip(w_refs, out_refs)):
            half = w_ref.shape[0] // 2

            def blk(q, e, out_ref=out_ref, half=half):
                return out_ref.at[q, pl.ds(pl.multiple_of(e * half, 16), half), :]

            def copy(src, dst, i, to):
                return pltpu.make_async_remote_copy(src_ref=src, dst_ref=dst, send_sem=send_sems.at[i], recv_sem=recv_sems.at[i],
                                                    device_id=to, device_id_type=MESH)

            src = w_ref.at[pl.ds(pl.multiple_of(c * half, 16), half), :]
            for j, (cx, cy, qj) in enumerate(_other_chips(x, y)):
                self.ici.append(copy(src, blk(q0, c), 6 * k + j, (cx, cy, c)))
                self.ici_in.append(copy(blk(qj, c), blk(qj, c), 6 * k + j, (cx, cy, c)))
                self.fwd.append(copy(blk(qj, c), blk(qj, c), 6 * k + 3 + j, sibling))
                self.fwd_in.append(copy(blk(qj, 1 - c), blk(qj, 1 - c), 6 * k + 3 + j, sibling))

    @staticmethod
    def out_shapes(shards):
        return [jax.ShapeDtypeStruct((N_CHIP,) + s.shape, s.dtype) for s in shards]

    @staticmethod
    def semaphores(n):
        return [pltpu.SemaphoreType.DMA((6 * n,)), pltpu.SemaphoreType.DMA((6 * n,))]

    def start(self):
        for cp in self.ici:
            cp.start()

    def forward(self):
        for arrived, onward in zip(self.ici_in, self.fwd):
            arrived.wait_recv()
            onward.start()

    def finish(self):
        for cp in self.fwd_in:
            cp.wait_recv()
        for cp in self.ici + self.fwd:
            cp.wait_send()


def _swap_halves_d2d(grads, name):
    n = len(grads)

    def body(*refs):
        g_refs, out_refs = refs[:n], refs[n:2 * n]
        send_sems, recv_sems = refs[2 * n:]
        x, y, c = _position()
        sibling = (x, y, 1 - c)
        cps = []
        for k in range(n):
            cp = pltpu.make_async_remote_copy(src_ref=g_refs[k].at[:, 1 - c], dst_ref=out_refs[k], send_sem=send_sems.at[k],
                                              recv_sem=recv_sems.at[k], device_id=sibling, device_id_type=MESH)
            cp.start()
            cps.append(cp)
        for cp in cps:
            cp.wait_recv()
        for cp in cps:
            cp.wait_send()

    return pl.pallas_call(
        body, name=name,
        out_shape=[jax.ShapeDtypeStruct((N_CHIP,) + g.shape[2:], g.dtype) for g in grads],
        in_specs=[ANY] * n, out_specs=[ANY] * n,
        scratch_shapes=[pltpu.SemaphoreType.DMA((n,)), pltpu.SemaphoreType.DMA((n,))],
    )(*grads)


def _pair_sum(g, a, c_idx, name):
    _, _, rh, cols = g.shape
    tr = rh
    for cand in (256, 128, 64, 32, 16):
        if rh % cand == 0 and rh > cand:
            tr = cand
            break

    def body(c_ref, g_ref, a_ref, o_ref):
        o_ref[...] = (g_ref[...] + a_ref[...]).astype(BF16)

    return pl.pallas_call(
        body, name=name,
        grid_spec=pltpu.PrefetchScalarGridSpec(
            num_scalar_prefetch=1, grid=(N_CHIP, rh // tr),
            in_specs=[pl.BlockSpec((None, None, tr, cols), lambda q, i, c_ref: (q, c_ref[0], i, 0)),
                      pl.BlockSpec((None, tr, cols), lambda q, i, c_ref: (q, i, 0))],
            out_specs=pl.BlockSpec((None, tr, cols), lambda q, i, c_ref: (q, i, 0))),
        out_shape=jax.ShapeDtypeStruct((N_CHIP, rh, cols), BF16),
        compiler_params=_params(("parallel", "parallel"), 10 * _nbytes((tr, cols), F32)),
    )(c_idx, g, a)


def _scatter_partials(parts, name):
    n = len(parts)

    def body(*refs):
        scatter = _Scatter(refs[:n], refs[n:2 * n], *refs[2 * n:])
        scatter.start()
        scatter.finish()

    return pl.pallas_call(
        body, name=name,
        out_shape=_Scatter.out_shapes(parts), in_specs=[ANY] * n, out_specs=[ANY] * n,
        scratch_shapes=_Scatter.semaphores(n),
    )(*parts)


class _Scatter:
    def __init__(self, p_refs, out_refs, send_sems, recv_sems):
        x, y, c = _position()
        self.copies = []
        for k, (p_ref, out_ref) in enumerate(zip(p_refs, out_refs)):
            for j, (cx, cy, qj) in enumerate(_other_chips(x, y)):
                self.copies.append(pltpu.make_async_remote_copy(
                    src_ref=p_ref.at[qj], dst_ref=out_ref.at[j], send_sem=send_sems.at[3 * k + j],
                    recv_sem=recv_sems.at[3 * k + j], device_id=(cx, cy, c), device_id_type=MESH))

    @staticmethod
    def out_shapes(parts):
        return [jax.ShapeDtypeStruct((3,) + p.shape[1:], p.dtype) for p in parts]

    @staticmethod
    def semaphores(n):
        return [pltpu.SemaphoreType.DMA((3 * n,)), pltpu.SemaphoreType.DMA((3 * n,))]

    def start(self):
        for cp in self.copies:
            cp.start()

    def finish(self):
        for cp in self.copies:
            cp.wait_recv()
        for cp in self.copies:
            cp.wait_send()


def _shard_sum(p, b, q_idx, name):
    _, rh, cols = p.shape
    tr = rh
    for cand in (256, 128, 64, 32, 16):
        if rh % cand == 0 and rh > cand:
            tr = cand
            break

    def body(q_ref, p_ref, b_ref, o_ref):
        acc = p_ref[...].astype(F32)
        for j in range(3):
            acc = acc + b_ref[j].astype(F32)
        o_ref[...] = acc

    return pl.pallas_call(
        body, name=name,
        grid_spec=pltpu.PrefetchScalarGridSpec(
            num_scalar_prefetch=1, grid=(rh // tr,),
            in_specs=[pl.BlockSpec((None, tr, cols), lambda i, q_ref: (q_ref[0], i, 0)),
                      pl.BlockSpec((3, tr, cols), lambda i, q_ref: (0, i, 0))],
            out_specs=pl.BlockSpec((tr, cols), lambda i, q_ref: (i, 0))),
        out_shape=jax.ShapeDtypeStruct((rh, cols), F32),
        compiler_params=_params(("parallel",), 8 * _nbytes((tr, cols), F32)),
    )(q_idx, p, b)


def _join_halves(halves):
    n = len(halves)

    def body(*refs):
        h_refs, out_refs = refs[:n], refs[n:2 * n]
        send_sems, recv_sems = refs[2 * n:]
        x, y, c = _position()
        sibling = (x, y, 1 - c)
        cps = []
        for k in range(n):
            cp = pltpu.make_async_remote_copy(src_ref=h_refs[k], dst_ref=out_refs[k], send_sem=send_sems.at[k],
                                              recv_sem=recv_sems.at[k], device_id=sibling, device_id_type=MESH)
            cp.start()
            cps.append(cp)
        for cp in cps:
            cp.wait_recv()
        for cp in cps:
            cp.wait_send()

    return pl.pallas_call(
        body, name="rs_join",
        out_shape=[jax.ShapeDtypeStruct(h.shape, h.dtype) for h in halves],
        in_specs=[ANY] * n, out_specs=[ANY] * n,
        scratch_shapes=[pltpu.SemaphoreType.DMA((n,)), pltpu.SemaphoreType.DMA((n,))],
    )(*halves)


def _cols_from_shards(g):
    q, r, cs = g.shape
    return jnp.transpose(g, (1, 0, 2)).reshape(r, q * cs)


def _cols_to_shards(w):
    r, cfull = w.shape
    return jnp.transpose(w.reshape(r, N_CHIP, cfull // N_CHIP), (1, 0, 2))


def _pad_w_in(w):
    z = lambda n: jnp.zeros((w.shape[0], n), w.dtype)
    q_lat, kv_lat, kpe = w[:, 0:512], w[:, 512:768], w[:, 768:800]
    qd, kd, vd = w[:, 800:1312], w[:, 1312:1824], w[:, 1824:2336]
    return jnp.concatenate([q_lat, qd, kd, vd, kv_lat, z(KPE_OFF), kpe, z(LANE - KPE_OFF - ROPE)], axis=1)


def _unpad_w_in(g):
    return jnp.concatenate([g[:, P_QLAT:P_QLAT + Q_LORA], g[:, P_KVLAT:P_KVLAT + KV_LORA],
                            g[:, P_KPE + KPE_OFF:P_KPE + KPE_OFF + ROPE], g[:, P_QD:P_QD + 3 * DIL_W]], axis=1)


def _pad_w_qb(w):
    w3 = w.reshape(Q_LORA, HEADS, NOPE + ROPE)
    return jnp.pad(w3, ((0, 0), (0, 0), (0, LANE - NOPE - ROPE))).reshape(Q_LORA, HEADS * LANE)


def _unpad_w_qb(g):
    return g.reshape(Q_LORA, HEADS, LANE)[:, :, :NOPE + ROPE].reshape(Q_LORA, HEADS * (NOPE + ROPE))


def _pad_w_kvb(w):
    w3 = w.reshape(KV_LORA, HEADS, 2 * NOPE)
    kp = jnp.pad(w3[:, :, :NOPE], ((0, 0), (0, 0), (0, LANE - NOPE))).reshape(KV_LORA, HEADS * LANE)
    return jnp.concatenate([kp, w3[:, :, NOPE:].reshape(KV_LORA, DIL_W)], axis=1)


def _unpad_w_kvb(g):
    gk = g[:, :HEADS * LANE].reshape(KV_LORA, HEADS, LANE)[:, :, :NOPE]
    gv = g[:, HEADS * LANE:].reshape(KV_LORA, HEADS, NOPE)
    return jnp.concatenate([gk, gv], axis=2).reshape(KV_LORA, HEADS * 2 * NOPE)


def _head_gains(g_q_nope, g_q_pe, g_k_nope, g_k_pe, g_dq, g_dk):
    z = lambda n: jnp.zeros((1, n), F32)
    q1 = jnp.concatenate([g_q_nope, g_q_pe, z(LANE - NOPE - ROPE)], axis=1)
    k1 = jnp.concatenate([g_k_nope, z(LANE - NOPE)], axis=1)
    kpe = jnp.concatenate([z(KPE_OFF), g_k_pe, z(LANE - KPE_OFF - ROPE)], axis=1)
    return dict(q=jnp.tile(q1, (1, HEADS)), k=jnp.tile(k1, (1, HEADS)), kpe=kpe,
                dq=jnp.tile(g_dq, (1, HEADS)), dk=jnp.tile(g_dk, (1, HEADS)))


def kernel(x, c, positions, w_ada, b_ada, g_mix_norm, w_in, g_q_lat, w_q_b, g_kv_lat, w_kv_b, g_mla_q_nope, g_mla_q_pe, g_mla_k_nope, g_mla_k_pe, g_dil_q, g_dil_k, w_o, g_ffn_norm, w_up, w_conv, b_conv, w_down, loss_target, m_w_ada, m_b_ada, m_g_mix_norm, m_w_in, m_g_q_lat, m_w_q_b, m_g_kv_lat, m_w_kv_b, m_g_mla_q_nope, m_g_mla_q_pe, m_g_mla_k_nope, m_g_mla_k_pe, m_g_dil_q, m_g_dil_k, m_w_o, m_g_ffn_norm, m_w_up, m_w_conv, m_b_conv, m_w_down, v_w_ada, v_b_ada, v_g_mix_norm, v_w_in, v_g_q_lat, v_w_q_b, v_g_kv_lat, v_w_kv_b, v_g_mla_q_nope, v_g_mla_q_pe, v_g_mla_k_nope, v_g_mla_k_pe, v_g_dil_q, v_g_dil_k, v_w_o, v_g_ffn_norm, v_w_up, v_w_conv, v_b_conv, v_w_down):
    args = dict(locals())
    weights = {n: args[n][0] for n in ("w_ada", "w_in", "w_q_b", "w_kv_b", "w_o", "w_up", "w_conv", "w_down")}
    small_w = {n: args[n] for n in ("b_ada",) + tuple(n for n, _ in SMALL_WIDTHS)}
    mom_m = {n[2:]: (args[n][0] if args[n].ndim == 3 else args[n]) for n in args if n.startswith("m_")}
    mom_v = {n[2:]: (args[n][0] if args[n].ndim == 3 else args[n]) for n in args if n.startswith("v_")}

    xi, yi, ci = _position()
    q0 = 2 * xi + yi
    me = 4 * xi + 2 * yi + ci
    xs, tgt = x[0], loss_target[0]
    s = xs.shape[0]
    consts = _seg_consts()
    c_idx, q_idx = jnp.reshape(ci, (1,)).astype(I32), jnp.reshape(q0, (1,)).astype(I32)

    def halves(g4):
        q, r, cc = g4.shape
        return g4.reshape(q, 2, r // 2, cc)

    conv_cols = UP_W // N_CHIP
    c_and_taps = _ag_small(jnp.concatenate([c, weights["w_conv"].reshape(1, 3 * conv_cols)], axis=1), "ag_c")[:, 0, :]
    c_all = c_and_taps[:, :D_MODEL]
    w_conv_f = c_and_taps[:, D_MODEL:].reshape(N_CHIP, 2, 3, conv_cols)[:, 0]
    w_conv_f = jnp.transpose(w_conv_f, (1, 0, 2)).reshape(3, UP_W)
    ada_cols = w_ada.shape[2]
    b_shard = lax.dynamic_slice_in_dim(b_ada, q0 * ada_cols, ada_cols, axis=1)
    mod_blk = _ada_fwd(c_all, weights["w_ada"], b_shard)
    mod_all = _ag_small(mod_blk, "ag_mod").reshape(N_CHIP, 2, N_DEV, ada_cols)
    mod = lax.dynamic_index_in_dim(lax.dynamic_index_in_dim(mod_all, ci, 1, False), me, 1, False)
    mod = mod.reshape(1, N_CHIP * ada_cols)
    sh1, sc1, g1, sh2, sc2, g2 = [mod[:, k * D_MODEL:(k + 1) * D_MODEL] for k in range(6)]

    place_own = lambda gs, ws: [lax.dynamic_update_slice_in_dim(g, w[None], q0, axis=0) for g, w in zip(gs, ws)]
    own_first = [weights[n].astype(BF16) for n in ("w_in", "w_q_b", "w_kv_b")]
    own_later = [weights[n].astype(BF16) for n in ("w_o", "w_up", "w_down")]
    gathered = place_own(_ag_weights(own_first, "ag_weights"), own_first)
    w_in_p = _pad_w_in(_cols_from_shards(gathered[0]))
    w_qb_p = _pad_w_qb(_cols_from_shards(gathered[1]))
    w_kvb_p = _pad_w_kvb(_cols_from_shards(gathered[2]))

    gains = _head_gains(g_mla_q_nope, g_mla_q_pe, g_mla_k_nope, g_mla_k_pe, g_dil_q, g_dil_k)
    tab = _rope_tables(positions.reshape(s, 1), *_rope_consts())

    h = _prenorm(xs, g_mix_norm, sc1, sh1, "prenorm")
    proj = _mm(h, w_in_p, "nn", F32, 512, P_COLS, "mm_in")
    ql, kvl = _latnorm(proj, g_q_lat, g_kv_lat)
    q_raw = _mm(ql, w_qb_p, "nn", F32, 512, HEADS * LANE, "mm_qb")
    kv_raw = _mm(kvl, w_kvb_p, "nn", F32, 512, HEADS * LANE + DIL_W, "mm_kvb")
    qm, km, vm, qd, kd, vd = _attn_prep(q_raw, kv_raw, proj, tab, gains, consts)
    scale_m, scale_d = (NOPE + ROPE) ** -0.5, DIL_DIM ** -0.5
    o_m, lse_m, *gathered = _attn_fwd(qm, km, vm, True, scale_m, "attn_mla", gather=own_later[:2])
    o_d, lse_d, *gathered_d = _attn_fwd(qd, kd, vd, False, scale_d, "attn_dil", gather=own_later[2:])
    gathered = place_own(gathered + gathered_d, own_later)
    w_o_f = gathered[0].reshape(D_MODEL, D_MODEL)
    w_up_f = _cols_from_shards(gathered[1])
    w_down_f = gathered[2].reshape(D_FF, D_MODEL)
    mix_in = jnp.concatenate([o_m, o_d], axis=1)
    mix = _mm(mix_in, w_o_f, "nn", F32, 512, D_MODEL, "mm_o")
    x1, h2 = _resid_prenorm(xs, mix, g1, g_ffn_norm, sc2, sh2)
    up = _mm(h2, w_up_f, "nn", F32, 512, CONV_TILE, "mm_up")
    act = _conv_gate(up, w_conv_f, b_conv)
    ffn = _mm(act, w_down_f, "nn", F32, 256, D_MODEL, "mm_down")
    dy, dffn, dg2, loss_part = _final(x1, ffn, tgt, g2)

    da = _mm(dffn, w_down_f, "nt", F32, 512, CONV_TILE, "mm_down_dx")
    gw_down = _mm(act, dffn, "tn", F32, 256, D_MODEL, "mm_down_dw")
    dup_g, dup_v, dbg, dbv, dwg, dwv = _gate_bwd(up, da, w_conv_f, b_conv)
    dup = jnp.concatenate([dup_g, dup_v], axis=1)
    dh2 = _mm(dup, w_up_f, "nt", F32, 256, 512, "mm_up_dx")
    gw_up = _mm(h2, dup, "tn", F32, 512, CONV_TILE, "mm_up_dw")
    dx1, dmix, acc2 = _ffnnorm_bwd(dh2, x1, dy, mix, g_ffn_norm, sc2, g1)
    dmix_in = _mm(dmix, w_o_f, "nt", F32, 512, D_MODEL, "mm_o_dx")
    gw_o = _mm(mix_in, dmix, "tn", F32, 512, D_MODEL, "mm_o_dw")
    early_names = ("w_up", "w_down", "w_o")
    early = [halves(_cols_to_shards(gw_up)), halves(gw_down.reshape(N_CHIP, D_FF // N_CHIP, D_MODEL)),
             halves(gw_o.reshape(N_CHIP, D_MODEL // N_CHIP, D_MODEL))]
    early_sib = _swap_halves_d2d(early, "rs_pair_swap_early")
    early_sums = [_pair_sum(g, a, c_idx, "pair_sum_" + n) for g, a, n in zip(early, early_sib, early_names)]
    dqm, dkm, dvm, *early_recv = _attn_bwd(qm, km, vm, o_m, dmix_in, 0, lse_m, True, scale_m, "attn_mla_bwd",
                                           scatter=early_sums[:1])
    dqd, dkd, dvd, *early_recv_d = _attn_bwd(qd, kd, vd, o_d, dmix_in, DIL_W // LANE, lse_d, False, scale_d,
                                             "attn_dil_bwd", scatter=early_sums[1:])
    early_recv = early_recv + early_recv_d
    dq_raw, dkv_raw, dkpe_b, dqd_b, dkd_b, dvd_b, dgains = _attn_prep_bwd(
        dqm, dkm, dvm, dqd, dkd, dvd, q_raw, kv_raw, proj, tab, gains, consts)
    dql = _mm(dq_raw, w_qb_p, "nt", F32, 512, Q_LORA, "mm_qb_dx")
    gw_qb = _unpad_w_qb(_mm(ql, dq_raw, "tn", F32, Q_LORA, HEADS * LANE, "mm_qb_dw"))
    dkvl = _mm(dkv_raw, w_kvb_p, "nt", F32, 512, KV_LORA, "mm_kvb_dx")
    gw_kvb = _unpad_w_kvb(_mm(kvl, dkv_raw, "tn", F32, KV_LORA, HEADS * LANE + DIL_W, "mm_kvb_dw"))
    dqlat_b, dkvlat_b, dglat = _latnorm_bwd(dql, dkvl, proj, g_q_lat, g_kv_lat)
    dproj = jnp.concatenate([dqlat_b, dqd_b, dkd_b, dvd_b, dkvlat_b, dkpe_b], axis=1)
    dh = _mm(dproj, w_in_p, "nt", F32, 512, D_MODEL, "mm_in_dx")
    gw_in = _unpad_w_in(_mm(h, dproj, "tn", F32, 512, P_COLS, "mm_in_dw"))
    grad_x, acc1 = _mixnorm_bwd(dh, xs, dx1, g_mix_norm, sc1)

    packed = _pack_small(acc1, acc2, dg2, dglat, dgains, dbg, dbv, dwg, dwv, loss_part)
    gathered_small = _ag_small(packed, "ag_small")
    grad_b_ada, *small_grads, gconv_full, loss_sum = _sum_unpack(gathered_small)
    grads = {"b_ada": grad_b_ada}
    grads.update({n: g for (n, _), g in zip(SMALL_WIDTHS, small_grads)})
    shard_cols = UP_W // N_CHIP
    grads["w_conv"] = lax.dynamic_slice_in_dim(gconv_full, q0 * shard_cols, shard_cols, axis=1)
    dmod_all = gathered_small[:, 0, :6 * D_MODEL]
    grads["w_ada"] = _ada_bwd(c_all, lax.dynamic_slice_in_dim(dmod_all, q0 * ada_cols, ada_cols, axis=1))

    late_names = ("w_in", "w_q_b", "w_kv_b")
    late = [halves(_cols_to_shards(gw_in)), halves(_cols_to_shards(gw_qb)), halves(_cols_to_shards(gw_kvb))]
    late_sib = _swap_halves_d2d(late, "rs_pair_swap_late")
    late_sums = [_pair_sum(g, a, c_idx, "pair_sum_" + n) for g, a, n in zip(late, late_sib, late_names)]
    late_recv = _scatter_partials(late_sums, "rs_scatter_late")
    big_names = late_names + early_names
    half_sums = [_shard_sum(p, b, q_idx, "shard_sum_" + n)
                 for p, b, n in zip(late_sums + early_sums, list(late_recv) + list(early_recv), big_names)]
    from_sib = _join_halves(half_sums)
    south = ci == 0
    for n, mine, theirs in zip(big_names, half_sums, from_sib):
        grads[n] = jnp.concatenate([jnp.where(south, mine, theirs), jnp.where(south, theirs, mine)], axis=0)

    delta, new_m, new_v = {}, {}, {}
    for n in ("w_ada", "w_in", "w_q_b", "w_kv_b", "w_o", "w_up", "w_conv", "w_down"):
        operands = (weights[n], grads[n], mom_m[n], mom_v[n])
        if n == "w_ada":
            operands = _in_hbm(*operands)
        delta[n], new_m[n], new_v[n] = _adamw(*operands, "adamw_" + n)
    vec_names = ("b_ada",) + tuple(n for n, _ in SMALL_WIDTHS)
    sd, sm, sv = _adamw_vectors(*[[d_[n] for n in vec_names] for d_ in (small_w, grads, mom_m, mom_v)])
    for k, n in enumerate(vec_names):
        delta[n], new_m[n], new_v[n] = sd[k], sm[k], sv[k]

    loss = loss_sum[0, 0]
    order = ("w_ada", "b_ada", "g_mix_norm", "w_in", "g_q_lat", "w_q_b", "g_kv_lat", "w_kv_b", "g_mla_q_nope", "g_mla_q_pe",
             "g_mla_k_nope", "g_mla_k_pe", "g_dil_q", "g_dil_k", "w_o", "g_ffn_norm", "w_up", "w_conv", "b_conv", "w_down")
    lead = lambda n, z: z[None] if n.startswith("w_") else z
    outs = [loss, grad_x[None]]
    for d_ in (grads, delta, new_m, new_v):
        outs += [lead(n, d_[n]) for n in order]
    return tuple(outs)
```

```python
import functools

import numpy as np
import jax
import jax.numpy as jnp
from jax import lax
from jax.experimental import pallas as pl
from jax.experimental.pallas import tpu as pltpu

F32 = jnp.float32
BF16 = jnp.bfloat16
I32 = jnp.int32

D_MODEL = 1024
HEADS = 8
NOPE = 64
ROPE = 32
Q_LORA = 512
KV_LORA = 256
DIL_DIM = 64
DIL_W = HEADS * DIL_DIM
D_FF = 2816
UP_W = 2 * D_FF
IN_COLS = Q_LORA + KV_LORA + ROPE + 3 * DIL_W
ROPE_THETA = 10000.0
EPS = 1e-6
NEG_INF = -1e30
N_DEV = 8
N_CHIP = 4

ADAM_LR = 0.001
ADAM_B1 = 0.9
ADAM_B2 = 0.999
ADAM_EPS = 1e-08
ADAM_WD = 0.01
ADAM_STEP = 10

LANE = 128
ROW_TILE = 256
ATT_TQ = 512
ATT_TK = 256
LOG2E = 1.4426950408889634
LN2 = 0.6931471805599453
VMEM_CAP = 56 * 1024 * 1024
VMEM_FLOOR = 32 * 1024 * 1024

P_QLAT, P_QD, P_KD, P_VD, P_KVLAT, P_KPE = 0, 512, 1024, 1536, 2048, 2304
P_COLS = 2432
KPE_OFF = 64

NN = (((1,), (0,)), ((), ()))
NT = (((1,), (1,)), ((), ()))
TN = (((0,), (0,)), ((), ()))
HIGHEST = lax.Precision.HIGHEST
MESH = pl.DeviceIdType.MESH


def _params(sem=None, est_bytes=0):
    limit = int(min(max(2 * est_bytes + (4 << 20), VMEM_FLOOR), VMEM_CAP))
    if sem is None:
        return pltpu.CompilerParams(vmem_limit_bytes=limit)
    return pltpu.CompilerParams(dimension_semantics=sem, vmem_limit_bytes=limit)


def _nbytes(shape, dtype):
    return int(np.prod(shape)) * jnp.dtype(dtype).itemsize


def _in_hbm(*xs):
    return [pltpu.with_memory_space_constraint(x, pltpu.HBM) for x in xs]


def _mm(a, b, dims, out_dtype, tm, tn, name, col_shards=False, swap=()):
    if dims == "nn":
        (m, k), (k2, n) = a.shape, b.shape
        a_spec = pl.BlockSpec((tm, k), lambda i, j: (i, 0))
        b_spec = pl.BlockSpec((k, tn), lambda i, j: (0, j))
        dn = NN
    elif dims == "nt":
        (m, k), (n, k2) = a.shape, b.shape
        a_spec = pl.BlockSpec((tm, k), lambda i, j: (i, 0))
        b_spec = pl.BlockSpec((tn, k), lambda i, j: (j, 0))
        dn = NT
    else:
        (k, m), (k2, n) = a.shape, b.shape
        a_spec = pl.BlockSpec((k, tm), lambda i, j: (0, i))
        b_spec = pl.BlockSpec((k, tn), lambda i, j: (0, j))
        dn = TN
    assert k == k2 and m % tm == 0 and n % tn == 0, (name, a.shape, b.shape, tm, tn)

    nw = len(swap)
    grid = (m // tm, n // tn)

    def body(*refs):
        a_ref, b_ref, o_ref = refs[0], refs[1], refs[2 + nw]
        comm = (refs[2:2 + nw], refs[3 + nw:3 + 2 * nw]) + tuple(refs[3 + 2 * nw:])
        if nw:
            @pl.when((pl.program_id(0) == 0) & (pl.program_id(1) == 0))
            def _():
                _PairSwap(*comm).start()

        o_ref[...] = lax.dot_general(a_ref[...], b_ref[...], dn, preferred_element_type=F32).astype(o_ref.dtype)

        if nw:
            @pl.when((pl.program_id(0) == grid[0] - 1) & (pl.program_id(1) == grid[1] - 1))
            def _():
                _PairSwap(*comm).finish()

    est = _nbytes((tm, k), a.dtype) + _nbytes((tn, k), b.dtype) + _nbytes((tm, tn), F32) + _nbytes((tm, tn), out_dtype)
    if col_shards:
        out_spec = pl.BlockSpec((None, tm, tn), lambda i, j: (j, i, 0))
        out_shape = jax.ShapeDtypeStruct((n // tn, m, tn), out_dtype)
    else:
        out_spec = pl.BlockSpec((tm, tn), lambda i, j: (i, j))
        out_shape = jax.ShapeDtypeStruct((m, n), out_dtype)
    out = pl.pallas_call(
        body, name=name, grid=grid,
        in_specs=[a_spec, b_spec] + [ANY] * nw,
        out_specs=[out_spec] + [ANY] * nw,
        out_shape=[out_shape] + _PairSwap.out_shapes(swap),
        scratch_shapes=_PairSwap.semaphores(nw) if nw else [],
        compiler_params=_params(("arbitrary", "arbitrary") if nw else ("parallel", "parallel"), est),
    )(a, b, *swap)
    return out if nw else out[0]


def _seg_consts():
    seg_q = np.zeros((HEADS * LANE, LANE), np.float32)
    inv_q = np.zeros((1, LANE), np.float32)
    seg_k = np.zeros((HEADS * LANE, LANE), np.float32)
    inv_k = np.zeros((1, LANE), np.float32)
    seg_d = np.zeros((DIL_W, LANE), np.float32)
    inv_d = np.zeros((1, LANE), np.float32)
    for h in range(HEADS):
        seg_q[h * LANE:h * LANE + NOPE, 2 * h] = 1.0
        seg_q[h * LANE + NOPE:h * LANE + NOPE + ROPE, 2 * h + 1] = 1.0
        inv_q[0, 2 * h], inv_q[0, 2 * h + 1] = 1.0 / NOPE, 1.0 / ROPE
        seg_k[h * LANE:h * LANE + NOPE, h] = 1.0
        inv_k[0, h] = 1.0 / NOPE
        seg_d[h * DIL_DIM:(h + 1) * DIL_DIM, h] = 1.0
        inv_d[0, h] = 1.0 / DIL_DIM
    fold_q = np.tile(np.eye(LANE, dtype=np.float32), (HEADS, 1))
    fold_d = np.zeros((DIL_W, LANE), np.float32)
    fold_d[np.arange(DIL_W), np.arange(DIL_W) % DIL_DIM] = 1.0
    j = lambda v: jnp.asarray(v)
    b = lambda v: jnp.asarray(v, dtype=BF16)
    return dict(seg_q=b(seg_q), exp_q=b(seg_q.T.copy()), inv_q=j(inv_q), seg_k=b(seg_k), exp_k=b(seg_k.T.copy()),
                inv_k=j(inv_k), seg_d=b(seg_d), exp_d=b(seg_d.T.copy()), inv_d=j(inv_d), fold_q=j(fold_q), fold_d=j(fold_d))


def _rope_consts():
    inv_d = jnp.power(ROPE_THETA, -2.0 * jnp.arange(DIL_DIM // 2, dtype=F32) / DIL_DIM)
    inv_q = jnp.power(ROPE_THETA, -2.0 * jnp.arange(ROPE // 2, dtype=F32) / ROPE)
    lanes = np.arange(LANE)
    freq_d = inv_d[lanes % (DIL_DIM // 2)]
    in_pe = (lanes >= KPE_OFF) & (lanes < KPE_OFF + ROPE)
    freq_q = jnp.where(jnp.asarray(in_pe), inv_q[(lanes - KPE_OFF) % (ROPE // 2)], 0.0)
    sign_d = np.where(lanes % DIL_DIM < DIL_DIM // 2, -1.0, 1.0).astype(np.float32)
    sign_q = np.where(in_pe, np.where((lanes - KPE_OFF) < ROPE // 2, -1.0, 1.0), 0.0).astype(np.float32)
    zeros, ones = np.zeros(LANE, np.float32), np.ones(LANE, np.float32)
    freq = jnp.concatenate([freq_d, freq_d, freq_q, freq_q])[None, :]
    csel = jnp.asarray(np.concatenate([ones, zeros, ones, zeros]))[None, :]
    ssel = jnp.asarray(np.concatenate([zeros, sign_d, zeros, sign_q]))[None, :]
    return freq, csel, ssel


def _full(shape):
    return pl.BlockSpec(shape, lambda *_: (0,) * len(shape))


def _tile_lanes(x, n):
    return jnp.concatenate([x] * n, axis=1)


def _rope_tables(pos_col, freq, csel, ssel):
    s = pos_col.shape[0]

    def body(p_ref, f_ref, c_ref, s_ref, o_ref):
        ang = p_ref[...].astype(F32) * f_ref[...]
        o_ref[...] = c_ref[...] * jnp.cos(ang) + s_ref[...] * jnp.sin(ang)

    return pl.pallas_call(
        body, name="rope_tables", grid=(s // ROW_TILE,),
        in_specs=[pl.BlockSpec((ROW_TILE, 1), lambda i: (i, 0)), _full((1, 4 * LANE)), _full((1, 4 * LANE)), _full((1, 4 * LANE))],
        out_specs=pl.BlockSpec((ROW_TILE, 4 * LANE), lambda i: (i, 0)),
        out_shape=jax.ShapeDtypeStruct((s, 4 * LANE), F32),
        compiler_params=_params(("parallel",)),
    )(pos_col, freq, csel, ssel)


def _rms(x):
    return lax.rsqrt(jnp.mean(x * x, axis=-1, keepdims=True) + EPS)


def _prenorm(x, gain, scale, shift, name):
    s, d = x.shape

    def body(x_ref, g_ref, sc_ref, sh_ref, h_ref):
        xv = x_ref[...]
        h = (xv * _rms(xv)) * g_ref[...] * (1.0 + sc_ref[...]) + sh_ref[...]
        h_ref[...] = h.astype(BF16)

    row = pl.BlockSpec((ROW_TILE, d), lambda i: (i, 0))
    return pl.pallas_call(
        body, name=name, grid=(s // ROW_TILE,),
        in_specs=[row, _full((1, d)), _full((1, d)), _full((1, d))],
        out_specs=row, out_shape=jax.ShapeDtypeStruct((s, d), BF16),
        compiler_params=_params(("parallel",)),
    )(x, gain, scale, shift)


def _latnorm(proj, g_q, g_kv):
    s = proj.shape[0]

    def body(q_ref, kv_ref, gq_ref, gkv_ref, ql_ref, kvl_ref):
        q, kv = q_ref[...], kv_ref[...]
        ql_ref[...] = ((q * _rms(q)) * gq_ref[...]).astype(BF16)
        kvl_ref[...] = ((kv * _rms(kv)) * gkv_ref[...]).astype(BF16)

    return pl.pallas_call(
        body, name="latnorm", grid=(s // ROW_TILE,),
        in_specs=[pl.BlockSpec((ROW_TILE, Q_LORA), lambda i: (i, P_QLAT // Q_LORA)),
                  pl.BlockSpec((ROW_TILE, KV_LORA), lambda i: (i, P_KVLAT // KV_LORA)),
                  _full((1, Q_LORA)), _full((1, KV_LORA))],
        out_specs=[pl.BlockSpec((ROW_TILE, Q_LORA), lambda i: (i, 0)), pl.BlockSpec((ROW_TILE, KV_LORA), lambda i: (i, 0))],
        out_shape=[jax.ShapeDtypeStruct((s, Q_LORA), BF16), jax.ShapeDtypeStruct((s, KV_LORA), BF16)],
        compiler_params=_params(("parallel",)),
    )(proj, proj, g_q, g_kv)


def _dot01(v, mat01):
    hi = v.astype(BF16)
    lo = (v - hi.astype(F32)).astype(BF16)
    return jnp.dot(hi, mat01, preferred_element_type=F32) + jnp.dot(lo, mat01, preferred_element_type=F32)


def _seg_rinv(x, seg, exp, inv):
    r = lax.rsqrt(_dot01(x * x, seg) * inv + EPS)
    return _dot01(r, exp)


def _seg_mean(v, seg, exp, inv):
    return _dot01(_dot01(v, seg) * inv, exp)


def _swap_halves(x, half):
    n = x.shape[1]
    lane = lax.broadcasted_iota(I32, (1, n), 1)
    first = (lane & (2 * half - 1)) < half
    return jnp.where(first, pltpu.roll(x, n - half, 1), pltpu.roll(x, half, 1))


def _rope(x, cos, sin_signed, half):
    return x * cos + _swap_halves(x, half) * sin_signed


def _rope_bwd(dy, cos, sin_signed, half):
    return dy * cos + _swap_halves(dy * sin_signed, half)


def _pe_lane_mask(n):
    lane = lax.broadcasted_iota(I32, (1, n), 1) & (LANE - 1)
    return (lane >= KPE_OFF) & (lane < KPE_OFF + ROPE)


def _attn_prep(q_raw, kv_raw, proj, tab, gains, consts):
    s = q_raw.shape[0]
    hw = HEADS * LANE

    def body(q_ref, kv_ref, kpe_ref, qd_ref, kd_ref, vd_ref, tab_ref,
             gq_ref, gk_ref, gkpe_ref, gdq_ref, gdk_ref,
             segq_ref, expq_ref, invq_ref, segk_ref, expk_ref, invk_ref, segd_ref, expd_ref, invd_ref,
             qm_ref, km_ref, vm_ref, qdo_ref, kdo_ref, vdo_ref):
        tab_v = tab_ref[...]
        cos_d, sin_d = _tile_lanes(tab_v[:, 0:LANE], DIL_W // LANE), _tile_lanes(tab_v[:, LANE:2 * LANE], DIL_W // LANE)
        cos_q1, sin_q1 = tab_v[:, 2 * LANE:3 * LANE], tab_v[:, 3 * LANE:4 * LANE]
        cos_q, sin_q = _tile_lanes(cos_q1, HEADS), _tile_lanes(sin_q1, HEADS)

        q = q_ref[...]
        qn = q * _seg_rinv(q, segq_ref[...], expq_ref[...], invq_ref[...]) * gq_ref[...]
        qm_ref[...] = _rope(qn, cos_q, sin_q, ROPE // 2).astype(BF16)

        kv = kv_ref[...]
        kp = kv[:, :hw]
        kn = kp * _seg_rinv(kp, segk_ref[...], expk_ref[...], invk_ref[...]) * gk_ref[...]
        kpe = kpe_ref[...]
        r_pe = lax.rsqrt(jnp.sum(kpe * kpe, axis=-1, keepdims=True) * (1.0 / ROPE) + EPS)
        kpe_r = _rope(kpe * r_pe * gkpe_ref[...], cos_q1, sin_q1, ROPE // 2)
        km_ref[...] = (kn + _tile_lanes(kpe_r, HEADS)).astype(BF16)
        vm_ref[...] = kv[:, hw:].astype(BF16)

        qd = qd_ref[...]
        qdn = qd * _seg_rinv(qd, segd_ref[...], expd_ref[...], invd_ref[...]) * gdq_ref[...]
        qdo_ref[...] = _rope(qdn, cos_d, sin_d, DIL_DIM // 2).astype(BF16)
        kd = kd_ref[...]
        kdn = kd * _seg_rinv(kd, segd_ref[...], expd_ref[...], invd_ref[...]) * gdk_ref[...]
        kdo_ref[...] = _rope(kdn, cos_d, sin_d, DIL_DIM // 2).astype(BF16)
        vdo_ref[...] = vd_ref[...].astype(BF16)

    t = ROW_TILE
    row = lambda w, cb=0: pl.BlockSpec((t, w), lambda i: (i, cb))
    c = consts
    return pl.pallas_call(
        body, name="attn_prep", grid=(s // t,),
        in_specs=[row(hw), row(hw + DIL_W), row(LANE, P_KPE // LANE), row(DIL_W, P_QD // DIL_W), row(DIL_W, P_KD // DIL_W),
                  row(DIL_W, P_VD // DIL_W), row(4 * LANE),
                  _full((1, hw)), _full((1, hw)), _full((1, LANE)), _full((1, DIL_W)), _full((1, DIL_W)),
                  _full((hw, LANE)), _full((LANE, hw)), _full((1, LANE)), _full((hw, LANE)), _full((LANE, hw)), _full((1, LANE)),
                  _full((DIL_W, LANE)), _full((LANE, DIL_W)), _full((1, LANE))],
        out_specs=[row(hw), row(hw), row(DIL_W), row(DIL_W), row(DIL_W), row(DIL_W)],
        out_shape=[jax.ShapeDtypeStruct((s, hw), BF16), jax.ShapeDtypeStruct((s, hw), BF16)]
        + [jax.ShapeDtypeStruct((s, DIL_W), BF16)] * 4,
        compiler_params=_params(("parallel",), 24 << 20),
    )(*_in_hbm(q_raw, kv_raw, proj, proj, proj, proj, tab), gains["q"], gains["k"], gains["kpe"], gains["dq"], gains["dk"],
      c["seg_q"], c["exp_q"], c["inv_q"], c["seg_k"], c["exp_k"], c["inv_k"], c["seg_d"], c["exp_d"], c["inv_d"])


def _attn_prep_bwd(dqm, dkm, dvm, dqd, dkd, dvd, q_raw, kv_raw, proj, tab, gains, consts):
    s = q_raw.shape[0]
    hw = HEADS * LANE
    n_steps = s // ROW_TILE

    def body(dqm_ref, dkm_ref, dvm_ref, dqd_ref, dkd_ref, dvd_ref, q_ref, kv_ref, kpe_ref, qd_ref, kd_ref, tab_ref,
             gq_ref, gk_ref, gkpe_ref, gdq_ref, gdk_ref,
             segq_ref, expq_ref, invq_ref, segk_ref, expk_ref, invk_ref, segd_ref, expd_ref, invd_ref, foldq_ref, foldd_ref,
             dq_ref, dkv_ref, dkpe_ref, dqdo_ref, dkdo_ref, dvdo_ref, dg_ref, acc_ref):
        i = pl.program_id(0)

        @pl.when(i == 0)
        def _():
            acc_ref[...] = jnp.zeros_like(acc_ref)

        tab_v = tab_ref[...]
        cos_d, sin_d = _tile_lanes(tab_v[:, 0:LANE], DIL_W // LANE), _tile_lanes(tab_v[:, LANE:2 * LANE], DIL_W // LANE)
        cos_q1, sin_q1 = tab_v[:, 2 * LANE:3 * LANE], tab_v[:, 3 * LANE:4 * LANE]
        cos_q, sin_q = _tile_lanes(cos_q1, HEADS), _tile_lanes(sin_q1, HEADS)

        def norm_bwd(x, dyg, gain, seg, exp, inv):
            rinv = _seg_rinv(x, seg, exp, inv)
            xn = x * rinv
            dxn = dyg * gain
            dx = rinv * (dxn - xn * _seg_mean(dxn * xn, seg, exp, inv))
            return dx, jnp.sum(dyg * xn, axis=0, keepdims=True)

        dq, gq_l = norm_bwd(q_ref[...], _rope_bwd(dqm_ref[...], cos_q, sin_q, ROPE // 2), gq_ref[...],
                            segq_ref[...], expq_ref[...], invq_ref[...])
        dq_ref[...] = dq.astype(BF16)

        dkm = dkm_ref[...]
        kv = kv_ref[...]
        dkp, gk_l = norm_bwd(kv[:, :hw], dkm, gk_ref[...], segk_ref[...], expk_ref[...], invk_ref[...])
        dkv_ref[:, :hw] = dkp.astype(BF16)
        dkv_ref[:, hw:] = dvm_ref[...].astype(BF16)

        dkpe_r = dkm[:, 0:LANE]
        for h in range(1, HEADS):
            dkpe_r = dkpe_r + dkm[:, h * LANE:(h + 1) * LANE]
        dkpe_r = jnp.where(_pe_lane_mask(LANE), dkpe_r, 0.0)
        dyg = _rope_bwd(dkpe_r, cos_q1, sin_q1, ROPE // 2)
        kpe = kpe_ref[...]
        r_pe = lax.rsqrt(jnp.sum(kpe * kpe, axis=-1, keepdims=True) * (1.0 / ROPE) + EPS)
        xn = kpe * r_pe
        dxn = dyg * gkpe_ref[...]
        dkpe = r_pe * (dxn - xn * (jnp.sum(dxn * xn, axis=-1, keepdims=True) * (1.0 / ROPE)))
        dkpe_ref[...] = dkpe.astype(BF16)
        gkpe_l = jnp.sum(dyg * xn, axis=0, keepdims=True)

        dqd_v, gdq_l = norm_bwd(qd_ref[...], _rope_bwd(dqd_ref[...], cos_d, sin_d, DIL_DIM // 2), gdq_ref[...],
                                segd_ref[...], expd_ref[...], invd_ref[...])
        dqdo_ref[...] = dqd_v.astype(BF16)
        dkd_v, gdk_l = norm_bwd(kd_ref[...], _rope_bwd(dkd_ref[...], cos_d, sin_d, DIL_DIM // 2), gdk_ref[...],
                                segd_ref[...], expd_ref[...], invd_ref[...])
        dkdo_ref[...] = dkd_v.astype(BF16)
        dvdo_ref[...] = dvd_ref[...].astype(BF16)

        acc_ref[0:1, :] += gq_l
        acc_ref[1:2, :] += gk_l
        acc_ref[2:3, 0:LANE] += gkpe_l
        acc_ref[3:4, 0:DIL_W] += gdq_l
        acc_ref[4:5, 0:DIL_W] += gdk_l

        @pl.when(i == n_steps - 1)
        def _():
            acc = acc_ref[...]
            fq = jnp.dot(acc, foldq_ref[...], precision=HIGHEST, preferred_element_type=F32)
            fd = jnp.dot(acc[:, 0:DIL_W], foldd_ref[...], precision=HIGHEST, preferred_element_type=F32)
            rows = lax.broadcasted_iota(I32, (8, LANE), 0)
            base = jnp.where(rows < 2, fq, jnp.where(rows == 2, acc[:, 0:LANE], fd))
            at0 = pltpu.roll(base, LANE - KPE_OFF, 1)
            dg_ref[...] = jnp.where(rows == 5, pltpu.roll(at0, 5, 0), jnp.where(rows == 2, at0, base))

    t = ROW_TILE
    row = lambda w, cb=0: pl.BlockSpec((t, w), lambda i: (i, cb))
    c = consts
    return pl.pallas_call(
        body, name="attn_prep_bwd", grid=(n_steps,),
        in_specs=[row(hw), row(hw), row(DIL_W), row(DIL_W), row(DIL_W), row(DIL_W),
                  row(hw), row(hw + DIL_W), row(LANE, P_KPE // LANE), row(DIL_W, P_QD // DIL_W), row(DIL_W, P_KD // DIL_W),
                  row(4 * LANE),
                  _full((1, hw)), _full((1, hw)), _full((1, LANE)), _full((1, DIL_W)), _full((1, DIL_W)),
                  _full((hw, LANE)), _full((LANE, hw)), _full((1, LANE)), _full((hw, LANE)), _full((LANE, hw)), _full((1, LANE)),
                  _full((DIL_W, LANE)), _full((LANE, DIL_W)), _full((1, LANE)), _full((hw, LANE)), _full((DIL_W, LANE))],
        out_specs=[row(hw), row(hw + DIL_W), row(LANE), row(DIL_W), row(DIL_W), row(DIL_W), _full((8, LANE))],
        out_shape=[jax.ShapeDtypeStruct((s, hw), BF16), jax.ShapeDtypeStruct((s, hw + DIL_W), BF16),
                   jax.ShapeDtypeStruct((s, LANE), BF16)] + [jax.ShapeDtypeStruct((s, DIL_W), BF16)] * 3
        + [jax.ShapeDtypeStruct((8, LANE), F32)],
        scratch_shapes=[pltpu.VMEM((8, hw), F32)],
        compiler_params=_params(("arbitrary",), 28 << 20),
    )(*_in_hbm(dqm, dkm, dvm, dqd, dkd, dvd, q_raw, kv_raw, proj, proj, proj, tab),
      gains["q"], gains["k"], gains["kpe"], gains["dq"], gains["dk"],
      c["seg_q"], c["exp_q"], c["inv_q"], c["seg_k"], c["exp_k"], c["inv_k"], c["seg_d"], c["exp_d"], c["inv_d"],
      c["fold_q"], c["fold_d"])


def _latnorm_bwd(dql, dkvl, proj, g_q, g_kv):
    s = proj.shape[0]
    n_steps = s // ROW_TILE

    def body(dql_ref, dkvl_ref, q_ref, kv_ref, gq_ref, gkv_ref, dq_ref, dkv_ref, dg_ref):
        i = pl.program_id(0)

        @pl.when(i == 0)
        def _():
            dg_ref[...] = jnp.zeros_like(dg_ref)

        def one(x, dyg, gain):
            r = _rms(x)
            xn = x * r
            dxn = dyg * gain
            dx = r * (dxn - xn * jnp.mean(dxn * xn, axis=-1, keepdims=True))
            return dx, jnp.sum(dyg * xn, axis=0, keepdims=True)

        dq, gq_l = one(q_ref[...], dql_ref[...], gq_ref[...])
        dkv, gkv_l = one(kv_ref[...], dkvl_ref[...], gkv_ref[...])
        dq_ref[...] = dq.astype(BF16)
        dkv_ref[...] = dkv.astype(BF16)
        dg_ref[0:1, :] += gq_l
        dg_ref[1:2, 0:KV_LORA] += gkv_l

    t = ROW_TILE
    return pl.pallas_call(
        body, name="latnorm_bwd", grid=(n_steps,),
        in_specs=[pl.BlockSpec((t, Q_LORA), lambda i: (i, 0)), pl.BlockSpec((t, KV_LORA), lambda i: (i, 0)),
                  pl.BlockSpec((t, Q_LORA), lambda i: (i, P_QLAT // Q_LORA)),
                  pl.BlockSpec((t, KV_LORA), lambda i: (i, P_KVLAT // KV_LORA)),
                  _full((1, Q_LORA)), _full((1, KV_LORA))],
        out_specs=[pl.BlockSpec((t, Q_LORA), lambda i: (i, 0)), pl.BlockSpec((t, KV_LORA), lambda i: (i, 0)), _full((8, Q_LORA))],
        out_shape=[jax.ShapeDtypeStruct((s, Q_LORA), BF16), jax.ShapeDtypeStruct((s, KV_LORA), BF16),
                   jax.ShapeDtypeStruct((8, Q_LORA), F32)],
        compiler_params=_params(("arbitrary",)),
    )(dql, dkvl, proj, proj, g_q, g_kv)


def _resid_prenorm(x, mix, g1, gain, scale, shift):
    s, d = x.shape

    def body(x_ref, mix_ref, g1_ref, g_ref, sc_ref, sh_ref, x1_ref, h_ref):
        x1 = x_ref[...] + g1_ref[...] * mix_ref[...]
        x1_ref[...] = x1
        h_ref[...] = ((x1 * _rms(x1)) * g_ref[...] * (1.0 + sc_ref[...]) + sh_ref[...]).astype(BF16)

    row = pl.BlockSpec((ROW_TILE, d), lambda i: (i, 0))
    vec = _full((1, d))
    return pl.pallas_call(
        body, name="resid_prenorm", grid=(s // ROW_TILE,),
        in_specs=[row, row, vec, vec, vec, vec], out_specs=[row, row],
        out_shape=[jax.ShapeDtypeStruct((s, d), F32), jax.ShapeDtypeStruct((s, d), BF16)],
        compiler_params=_params(("parallel",)),
    )(x, mix, g1, gain, scale, shift)


CONV_TILE = 1408
HALO = 8


def _shift_down(x, halo, k):
    t = x.shape[0]
    row = lax.broadcasted_iota(I32, (t, 1), 0)
    out = pltpu.roll(x, k, 0)
    for r in range(k):
        out = jnp.where(row == r, halo[HALO - k + r:HALO - k + r + 1, :], out)
    return out


def _shift_up(x, halo, k):
    t = x.shape[0]
    row = lax.broadcasted_iota(I32, (t, 1), 0)
    out = pltpu.roll(x, t - k, 0)
    for r in range(k):
        out = jnp.where(row == t - k + r, halo[r:r + 1, :], out)
    return out


def _conv_fwd(x, halo, w, b):
    p1, p2 = _shift_down(x, halo, 1), _shift_down(x, halo, 2)
    u = b + p2 * w[0:1, :]
    u = u + p1 * w[1:2, :]
    u = u + x * w[2:3, :]
    return u, p1, p2


def _sigmoid(x):
    return 1.0 / (1.0 + jnp.exp(-x))


def _conv_gate(up, w_conv, b_conv):
    s = up.shape[0]
    t = ROW_TILE
    nj = D_FF // CONV_TILE
    hb = t // HALO

    def body(g_ref, v_ref, gh_ref, vh_ref, wg_ref, wv_ref, bg_ref, bv_ref, a_ref):
        live = (pl.program_id(0) > 0).astype(F32)
        ug, _, _ = _conv_fwd(g_ref[...], gh_ref[...] * live, wg_ref[...], bg_ref[...])
        uv, _, _ = _conv_fwd(v_ref[...], vh_ref[...] * live, wv_ref[...], bv_ref[...])
        a_ref[...] = (ug * _sigmoid(ug) * uv).astype(BF16)

    main = lambda off: pl.BlockSpec((t, CONV_TILE), lambda i, j: (i, j + off))
    halo = lambda off: pl.BlockSpec((HALO, CONV_TILE), lambda i, j: (jnp.maximum(i * hb - 1, 0), j + off))
    wsp = lambda off: pl.BlockSpec((3, CONV_TILE), lambda i, j: (0, j + off))
    bsp = lambda off: pl.BlockSpec((1, CONV_TILE), lambda i, j: (0, j + off))
    return pl.pallas_call(
        body, name="conv_gate", grid=(s // t, nj),
        in_specs=[main(0), main(nj), halo(0), halo(nj), wsp(0), wsp(nj), bsp(0), bsp(nj)],
        out_specs=pl.BlockSpec((t, CONV_TILE), lambda i, j: (i, j)),
        out_shape=jax.ShapeDtypeStruct((s, D_FF), BF16),
        compiler_params=_params(("parallel", "parallel"), 12 << 20),
    )(up, up, up, up, w_conv, w_conv, b_conv, b_conv)


def _gate_bwd(up, da, w_conv, b_conv):
    s = up.shape[0]
    t = ROW_TILE
    nj = D_FF // CONV_TILE
    hb = t // HALO
    n_i = s // t

    def body(g_ref, v_ref, gh_ref, vh_ref, gn_ref, vn_ref, da_ref, dan_ref, wg_ref, wv_ref, bg_ref, bv_ref,
             dupg_ref, dupv_ref, dbg_ref, dbv_ref, dwg_ref, dwv_ref):
        i = pl.program_id(1)

        @pl.when(i == 0)
        def _():
            for r in (dbg_ref, dbv_ref, dwg_ref, dwv_ref):
                r[...] = jnp.zeros_like(r)

        def d_gate(ug, uv, da_v):
            sg = _sigmoid(ug)
            return da_v * uv * (sg * (1.0 + ug * (1.0 - sg))), da_v * (ug * sg)

        live = (i > 0).astype(F32)
        xg, xv = g_ref[...], v_ref[...]
        wg, wv = wg_ref[...], wv_ref[...]
        ug, g1, g2 = _conv_fwd(xg, gh_ref[...] * live, wg, bg_ref[...])
        uv, v1, v2 = _conv_fwd(xv, vh_ref[...] * live, wv, bv_ref[...])
        dug, duv = d_gate(ug, uv, da_ref[...])

        more = (i < n_i - 1).astype(F32)
        ug_n, _, _ = _conv_fwd(gn_ref[...], xg[t - HALO:, :], wg, bg_ref[...])
        uv_n, _, _ = _conv_fwd(vn_ref[...], xv[t - HALO:, :], wv, bv_ref[...])
        dug_n, duv_n = d_gate(ug_n, uv_n, dan_ref[...] * more)

        def conv_t(du, du_n, w):
            return du * w[2:3, :] + _shift_up(du, du_n, 1) * w[1:2, :] + _shift_up(du, du_n, 2) * w[0:1, :]

        dupg_ref[...] = conv_t(dug, dug_n, wg).astype(BF16)
        dupv_ref[...] = conv_t(duv, duv_n, wv).astype(BF16)
        csum = lambda z: jnp.sum(z, axis=0, keepdims=True)
        dbg_ref[...] += csum(dug)
        dbv_ref[...] += csum(duv)
        dwg_ref[0:1, :] += csum(dug * g2)
        dwg_ref[1:2, :] += csum(dug * g1)
        dwg_ref[2:3, :] += csum(dug * xg)
        dwv_ref[0:1, :] += csum(duv * v2)
        dwv_ref[1:2, :] += csum(duv * v1)
        dwv_ref[2:3, :] += csum(duv * xv)

    last_halo = s // HALO - 1
    main = lambda off: pl.BlockSpec((t, CONV_TILE), lambda j, i: (i, j + off))
    halo = lambda off: pl.BlockSpec((HALO, CONV_TILE), lambda j, i: (jnp.maximum(i * hb - 1, 0), j + off))
    nxt = lambda off: pl.BlockSpec((HALO, CONV_TILE), lambda j, i: (jnp.minimum((i + 1) * hb, last_halo), j + off))
    wsp = lambda off: pl.BlockSpec((3, CONV_TILE), lambda j, i: (0, j + off))
    bsp = lambda off: pl.BlockSpec((1, CONV_TILE), lambda j, i: (0, j + off))
    outs = pl.pallas_call(
        body, name="gate_bwd", grid=(nj, n_i),
        in_specs=[main(0), main(nj), halo(0), halo(nj), nxt(0), nxt(nj), main(0), nxt(0),
                  wsp(0), wsp(nj), bsp(0), bsp(nj)],
        out_specs=[main(0), main(0),
                   pl.BlockSpec((1, CONV_TILE), lambda j, i: (0, j)), pl.BlockSpec((1, CONV_TILE), lambda j, i: (0, j)),
                   pl.BlockSpec((3, CONV_TILE), lambda j, i: (0, j)), pl.BlockSpec((3, CONV_TILE), lambda j, i: (0, j))],
        out_shape=[jax.ShapeDtypeStruct((s, D_FF), BF16), jax.ShapeDtypeStruct((s, D_FF), BF16),
                   jax.ShapeDtypeStruct((1, D_FF), F32), jax.ShapeDtypeStruct((1, D_FF), F32),
                   jax.ShapeDtypeStruct((3, D_FF), F32), jax.ShapeDtypeStruct((3, D_FF), F32)],
        compiler_params=_params(("parallel", "arbitrary"), 24 << 20),
    )(up, up, up, up, up, up, da, da, w_conv, w_conv, b_conv, b_conv)
    return outs


def _final(x1, ffn, tgt, g2):
    s, d = x1.shape
    n_steps = s // ROW_TILE

    def body(x1_ref, f_ref, t_ref, g2_ref, dy_ref, df_ref, dg2_ref, loss_ref, lacc_ref):
        i = pl.program_id(0)

        @pl.when(i == 0)
        def _():
            dg2_ref[...] = jnp.zeros_like(dg2_ref)
            lacc_ref[...] = jnp.zeros_like(lacc_ref)

        f = f_ref[...]
        e = x1_ref[...] + g2_ref[...] * f - t_ref[...]
        dy = e * (1.0 / d)
        dy_ref[...] = dy
        df_ref[...] = (dy * g2_ref[...]).astype(BF16)
        dg2_ref[...] += jnp.sum(dy * f, axis=0, keepdims=True)
        lacc_ref[...] += jnp.sum(e * e, axis=0, keepdims=True)

        @pl.when(i == n_steps - 1)
        def _():
            loss_ref[...] = jnp.sum(lacc_ref[...], axis=1, keepdims=True) * (0.5 / d)

    row = pl.BlockSpec((ROW_TILE, d), lambda i: (i, 0))
    return pl.pallas_call(
        body, name="final", grid=(n_steps,),
        in_specs=[row, row, row, _full((1, d))],
        out_specs=[row, row, _full((1, d)), _full((1, 1))],
        out_shape=[jax.ShapeDtypeStruct((s, d), F32), jax.ShapeDtypeStruct((s, d), BF16),
                   jax.ShapeDtypeStruct((1, d), F32), jax.ShapeDtypeStruct((1, 1), F32)],
        scratch_shapes=[pltpu.VMEM((1, d), F32)],
        compiler_params=_params(("arbitrary",)),
    )(x1, ffn, tgt, g2)


def _ffnnorm_bwd(dh2, x1, dy, mix, gain, scale, g1):
    s, d = x1.shape
    n_steps = s // ROW_TILE

    def body(dh_ref, x_ref, dy_ref, mix_ref, g_ref, sc_ref, g1_ref, dx_ref, dm_ref, acc_ref):
        i = pl.program_id(0)

        @pl.when(i == 0)
        def _():
            acc_ref[...] = jnp.zeros_like(acc_ref)

        dh, x = dh_ref[...], x_ref[...]
        r = _rms(x)
        xn = x * r
        dn = dh * (1.0 + sc_ref[...])
        dxn = dn * g_ref[...]
        dx = dy_ref[...] + r * (dxn - xn * jnp.mean(dxn * xn, axis=-1, keepdims=True))
        dx_ref[...] = dx
        dm_ref[...] = (dx * g1_ref[...]).astype(BF16)
        csum = lambda z: jnp.sum(z, axis=0, keepdims=True)
        acc_ref[0:1, :] += csum(dh)
        acc_ref[1:2, :] += csum(dh * (xn * g_ref[...]))
        acc_ref[2:3, :] += csum(dn * xn)
        acc_ref[3:4, :] += csum(dx * mix_ref[...])

    row = pl.BlockSpec((ROW_TILE, d), lambda i: (i, 0))
    vec = _full((1, d))
    return pl.pallas_call(
        body, name="ffnnorm_bwd", grid=(n_steps,),
        in_specs=[row, row, row, row, vec, vec, vec],
        out_specs=[row, row, _full((8, d))],
        out_shape=[jax.ShapeDtypeStruct((s, d), F32), jax.ShapeDtypeStruct((s, d), BF16), jax.ShapeDtypeStruct((8, d), F32)],
        compiler_params=_params(("arbitrary",)),
    )(dh2, x1, dy, mix, gain, scale, g1)


def _mixnorm_bwd(dh, x, dx1, gain, scale):
    s, d = x.shape
    n_steps = s // ROW_TILE

    def body(dh_ref, x_ref, dx1_ref, g_ref, sc_ref, gx_ref, acc_ref):
        i = pl.program_id(0)

        @pl.when(i == 0)
        def _():
            acc_ref[...] = jnp.zeros_like(acc_ref)

        dh, x = dh_ref[...], x_ref[...]
        r = _rms(x)
        xn = x * r
        dn = dh * (1.0 + sc_ref[...])
        dxn = dn * g_ref[...]
        gx_ref[...] = dx1_ref[...] + r * (dxn - xn * jnp.mean(dxn * xn, axis=-1, keepdims=True))
        csum = lambda z: jnp.sum(z, axis=0, keepdims=True)
        acc_ref[0:1, :] += csum(dh)
        acc_ref[1:2, :] += csum(dh * (xn * g_ref[...]))
        acc_ref[2:3, :] += csum(dn * xn)

    row = pl.BlockSpec((ROW_TILE, d), lambda i: (i, 0))
    vec = _full((1, d))
    return pl.pallas_call(
        body, name="mixnorm_bwd", grid=(n_steps,),
        in_specs=[row, row, row, vec, vec],
        out_specs=[row, _full((8, d))],
        out_shape=[jax.ShapeDtypeStruct((s, d), F32), jax.ShapeDtypeStruct((8, d), F32)],
        compiler_params=_params(("arbitrary",)),
    )(dh, x, dx1, gain, scale)


def _key_count(d, dilated):
    if not dilated:
        return jnp.where(d >= 0, 1.0, 0.0)
    one = lambda cond: jnp.where(cond, 1.0, 0.0)
    cnt = one(d <= 128) + one(((d & 3) == 0) & (d <= 512)) + one((d & 15) == 0)
    return jnp.where(d >= 0, cnt, 0.0)


def _block_kinds(mla):
    return (0, "diag", "none") if mla else (512, "near", "far")


NEAR_OFFSETS = 4


def _scores_t(ka, qa, scale, kind, rel_t, offset, near_tabs=None):
    return _mask_scores(lax.dot_general(ka, qa, NT, preferred_element_type=F32), scale, kind, rel_t, offset, near_tabs)


def _fill_near_tables(bias_ref, cnt_ref, rel_t):
    for idx in range(NEAR_OFFSETS):
        cnt = _key_count(rel_t + (idx - 1) * ATT_TK, True)
        cnt_ref[idx] = cnt
        bias_ref[idx] = jnp.where(cnt > 0.0, 0.0, NEG_INF)


def _mask_scores(products, scale, kind, rel_t, offset, near_tabs=None):
    st = products * (scale * LOG2E)
    cnt = None
    if kind == "diag":
        st = jnp.where(rel_t + offset >= 0, st, NEG_INF)
    elif kind == "far":
        st = jnp.where((rel_t & 15) == 0, st, NEG_INF)
    elif kind == "near":
        bias_ref, cnt_ref = near_tabs
        idx = offset // ATT_TK + 1
        st = st + bias_ref[idx]
        cnt = cnt_ref[idx]
    return st, cnt


def _attn_fwd(q, k, v, mla, scale, name, gather=()):
    s = q.shape[0]
    qw = 2 * LANE if mla else LANE
    tq, tk = ATT_TQ, ATT_TK
    reach, kind_near, kind_far = _block_kinds(mla)
    assert s % tq == 0 and tq % tk == 0 and reach % tk == 0 and (mla or (reach + tq) // tk == NEAR_OFFSETS)
    ng = len(gather)
    last_step = HEADS // 2 - 1

    def body(*refs):
        q_ref, k_ref, v_ref = refs[:3]
        o_ref, lse_ref = refs[3 + ng:5 + ng]
        vt_ref, st_ref = refs[5 + 2 * ng:7 + 2 * ng]
        near_tabs = None if mla else refs[7 + 2 * ng:9 + 2 * ng]
        n_tabs = 0 if mla else 2
        comm = (refs[3:3 + ng], refs[5 + ng:5 + 2 * ng]) + tuple(refs[7 + n_tabs + 2 * ng:])
        if ng:
            @pl.when(pl.program_id(0) == 0)
            def _():
                _Gather(*comm).start()

            @pl.when(pl.program_id(0) == last_step)
            def _():
                _Gather(*comm).forward()

        lane = lax.broadcasted_iota(I32, (1, LANE), 1)
        rel_t = lax.broadcasted_iota(I32, (tk, tq), 1) - lax.broadcasted_iota(I32, (tk, tq), 0)
        if not mla:
            _fill_near_tables(*near_tabs, rel_t)

        def transpose_v(j, carry):
            c0 = pl.multiple_of(j * tk, tk)
            vt_ref[:, pl.ds(c0, tk)] = v_ref[pl.ds(c0, tk), :].astype(F32).T.astype(BF16)
            return carry

        lax.fori_loop(0, s // tk, transpose_v, 0)

        def q_block(qi, carry):
            r0 = pl.multiple_of(qi * tq, tq)
            kcols = [slice(a * LANE, (a + 1) * LANE) if mla else slice(0, LANE) for a in range(2)]
            qas = [q_ref[pl.ds(r0, tq), kcols[a]] for a in range(2)]
            if not mla:
                qas = [jnp.where(lane < DIL_DIM, qas[0], jnp.zeros_like(qas[0])),
                       jnp.where(lane >= DIL_DIM, qas[1], jnp.zeros_like(qas[1]))]

            n_k = (r0 + tq) // tk

            def products(kj):
                c0 = pl.multiple_of(kj * tk, tk)
                return [lax.dot_general(k_ref[pl.ds(c0, tk), kcols[a]], qas[a], NT, preferred_element_type=F32)
                        for a in range(2)]

            for a, pr in enumerate(products(0)):
                st_ref[0, a] = pr

            def k_block(kj, c, kind):
                c0 = pl.multiple_of(kj * tk, tk)
                slot = kj & 1
                ahead = products(jnp.minimum(kj + 1, n_k - 1))
                out = []
                for a in range(2):
                    m, l, acc = c[a]
                    st, cnt = _mask_scores(st_ref[slot, a], scale, kind, rel_t, r0 - c0, near_tabs)
                    st_ref[1 - slot, a] = ahead[a]
                    m_new = jnp.maximum(m, jnp.max(st, axis=0, keepdims=True))
                    alpha = jnp.exp2(m - m_new)
                    p = jnp.exp2(st - m_new)
                    if cnt is not None:
                        p = p * cnt
                    l = alpha * l + jnp.sum(p, axis=0, keepdims=True)
                    vt = vt_ref[a * DIL_DIM:(a + 1) * DIL_DIM, pl.ds(c0, tk)]
                    acc = alpha * acc + jnp.dot(vt, p.astype(BF16), preferred_element_type=F32)
                    out.append((m_new, l, acc))
                return tuple(out)

            one = (jnp.full((1, tq), NEG_INF, F32), jnp.zeros((1, tq), F32), jnp.zeros((DIL_DIM, tq), F32))
            first_near = jnp.maximum((r0 - reach) // tk, 0)
            c = lax.fori_loop(0, first_near, functools.partial(k_block, kind=kind_far), (one, one))
            res = lax.fori_loop(first_near, (r0 + tq) // tk, functools.partial(k_block, kind=kind_near), c)
            o_t = jnp.concatenate([res[a][2] / res[a][1] for a in range(2)], axis=0)
            o_ref[pl.ds(r0, tq), :] = o_t.T.astype(BF16)
            for a in range(2):
                lse_ref[a, :, pl.ds(r0, tq)] = res[a][0] * LN2 + jnp.log(res[a][1])
            return carry

        lax.fori_loop(0, s // tq, q_block, 0)

        if ng:
            @pl.when(pl.program_id(0) == last_step)
            def _():
                _Gather(*comm).finish()

    return pl.pallas_call(
        body, name=name, grid=(HEADS // 2,),
        in_specs=[pl.BlockSpec((s, qw), lambda h: (0, h)), pl.BlockSpec((s, qw), lambda h: (0, h)),
                  pl.BlockSpec((s, LANE), lambda h: (0, h))] + [ANY] * ng,
        out_specs=[pl.BlockSpec((s, LANE), lambda h: (0, h)), pl.BlockSpec((2, 1, s), lambda h: (h, 0, 0))] + [ANY] * ng,
        out_shape=[jax.ShapeDtypeStruct((s, DIL_W), BF16), jax.ShapeDtypeStruct((HEADS, 1, s), F32)] + _Gather.out_shapes(gather),
        scratch_shapes=[pltpu.VMEM((LANE, s), BF16), pltpu.VMEM((2, 2, tk, tq), F32)]
        + ([] if mla else [pltpu.VMEM((NEAR_OFFSETS, tk, tq), F32)] * 2) + (_Gather.semaphores(ng) if ng else []),
        compiler_params=_params(("arbitrary",) if ng else ("parallel",), 12 << 20),
    )(*_in_hbm(q, k, v), *gather)


def _attn_bwd(q, k, v, o, do, do_block0, lse, mla, scale, name, scatter=()):
    s = q.shape[0]
    qw = 2 * LANE if mla else LANE
    tq, tk = ATT_TQ, ATT_TK
    nq = s // tq
    reach, kind_near, kind_far = _block_kinds(mla)
    assert s % tq == 0 and tq % tk == 0
    ns = len(scatter)
    last_step = HEADS // 2 - 1

    def body(*refs):
        q_ref, k_ref, v_ref, o_ref, do_ref, lse_ref = refs[:6]
        dq_ref, dk_ref, dv_ref = refs[6 + ns:9 + ns]
        kt_ref, dot_ref, dob_ref, dqt_ref, delta_ref, lse2_ref = refs[9 + 2 * ns:15 + 2 * ns]
        near_tabs = None if mla else refs[15 + 2 * ns:17 + 2 * ns]
        n_tabs = 0 if mla else 2
        comm = (refs[6:6 + ns], refs[9 + ns:9 + 2 * ns]) + tuple(refs[15 + n_tabs + 2 * ns:])
        if ns:
            @pl.when(pl.program_id(0) == 0)
            def _():
                _Scatter(*comm).start()

        lane = lax.broadcasted_iota(I32, (1, LANE), 1)
        row = lax.broadcasted_iota(I32, (LANE, 1), 0)
        rel_t = lax.broadcasted_iota(I32, (tk, tq), 1) - lax.broadcasted_iota(I32, (tk, tq), 0)
        if not mla:
            _fill_near_tables(*near_tabs, rel_t)

        def prepare(j, carry):
            c0 = pl.multiple_of(j * tk, tk)
            do_blk = do_ref[pl.ds(c0, tk), :]
            dob_ref[pl.ds(c0, tk), :] = do_blk.astype(BF16)
            do_t = do_blk.T
            dot_ref[:, pl.ds(c0, tk)] = do_t.astype(BF16)
            prod = do_t * o_ref[pl.ds(c0, tk), :].astype(F32).T
            delta_ref[0, :, pl.ds(c0, tk)] = jnp.sum(prod[0:DIL_DIM], axis=0, keepdims=True)
            delta_ref[1, :, pl.ds(c0, tk)] = jnp.sum(prod[DIL_DIM:LANE], axis=0, keepdims=True)
            for w in range(qw // LANE):
                kt_ref[w * LANE:(w + 1) * LANE, pl.ds(c0, tk)] = (
                    k_ref[pl.ds(c0, tk), w * LANE:(w + 1) * LANE].astype(F32).T.astype(BF16))
            return carry

        lax.fori_loop(0, s // tk, prepare, 0)
        dqt_ref[...] = jnp.zeros_like(dqt_ref)
        lse2_ref[...] = lse_ref[...] * LOG2E

        sels = [lane < DIL_DIM, lane >= DIL_DIM]
        rsels = [row < DIL_DIM, row >= DIL_DIM]
        cols = [slice(a * LANE, (a + 1) * LANE) if mla else slice(0, LANE) for a in range(2)]

        def k_block(kj, carry):
            c0 = pl.multiple_of(kj * tk, tk)
            kas = [k_ref[pl.ds(c0, tk), cols[a]] for a in range(2)]
            kts = [kt_ref[cols[a], pl.ds(c0, tk)] for a in range(2)]
            if not mla:
                kas = [jnp.where(sels[a], kas[a], jnp.zeros_like(kas[a])) for a in range(2)]
                kts = [jnp.where(rsels[a], kts[a], jnp.zeros_like(kts[a])) for a in range(2)]
            vb = v_ref[pl.ds(c0, tk), :]
            vbs = [jnp.where(sels[a], vb, jnp.zeros_like(vb)) for a in range(2)]

            first = c0 // tq

            def q_block(qi, c, kind):
                r0 = pl.multiple_of(qi * tq, tq)
                out, dq_parts = [], []
                for a in range(2):
                    dk_acc, dv_acc = c[a]
                    qa = q_ref[pl.ds(r0, tq), cols[a]]
                    st, cnt = _scores_t(kas[a], qa, scale, kind, rel_t, r0 - c0, near_tabs)
                    p = jnp.exp2(st - lse2_ref[a, :, pl.ds(r0, tq)])
                    if cnt is not None:
                        p = p * cnt
                    dp = jnp.dot(vbs[a], dot_ref[:, pl.ds(r0, tq)], preferred_element_type=F32)
                    ds = (p * (dp - delta_ref[a, :, pl.ds(r0, tq)]) * scale).astype(BF16)
                    dv_acc = dv_acc + jnp.dot(p.astype(BF16), dob_ref[pl.ds(r0, tq), :], preferred_element_type=F32)
                    dk_acc = dk_acc + jnp.dot(ds, qa, preferred_element_type=F32)
                    dq_parts.append(jnp.dot(kts[a], ds, preferred_element_type=F32))
                    out.append((dk_acc, dv_acc))
                if mla:
                    for a in range(2):
                        dqt_ref[cols[a], pl.ds(r0, tq)] += dq_parts[a]
                else:
                    dqt_ref[:, pl.ds(r0, tq)] += dq_parts[0] + dq_parts[1]
                return tuple(out)

            zero = jnp.zeros((tk, LANE), F32)
            last_near = jnp.minimum((c0 + tk - 1 + reach) // tq + 1, nq)
            c = lax.fori_loop(first, last_near, functools.partial(q_block, kind=kind_near), ((zero, zero), (zero, zero)))
            (dk0, dv0), (dk1, dv1) = lax.fori_loop(last_near, nq, functools.partial(q_block, kind=kind_far), c)
            if mla:
                dk_ref[pl.ds(c0, tk), cols[0]] = dk0
                dk_ref[pl.ds(c0, tk), cols[1]] = dk1
            else:
                dk_ref[pl.ds(c0, tk), :] = jnp.where(sels[0], dk0, dk1)
            dv_ref[pl.ds(c0, tk), :] = jnp.where(sels[0], dv0, dv1)
            return carry

        lax.fori_loop(0, s // tk, k_block, 0)

        def write_dq(j, carry):
            c0 = pl.multiple_of(j * tk, tk)
            for w in range(qw // LANE):
                dq_ref[pl.ds(c0, tk), w * LANE:(w + 1) * LANE] = dqt_ref[w * LANE:(w + 1) * LANE, pl.ds(c0, tk)].T
            return carry

        lax.fori_loop(0, s // tk, write_dq, 0)

        if ns:
            @pl.when(pl.program_id(0) == last_step)
            def _():
                _Scatter(*comm).finish()

    b0 = do_block0
    return pl.pallas_call(
        body, name=name, grid=(HEADS // 2,),
        in_specs=[pl.BlockSpec((s, qw), lambda h: (0, h)), pl.BlockSpec((s, qw), lambda h: (0, h)),
                  pl.BlockSpec((s, LANE), lambda h: (0, h)), pl.BlockSpec((s, LANE), lambda h: (0, h)),
                  pl.BlockSpec((s, LANE), lambda h: (0, h + b0)), pl.BlockSpec((2, 1, s), lambda h: (h, 0, 0))] + [ANY] * ns,
        out_specs=[pl.BlockSpec((s, qw), lambda h: (0, h)), pl.BlockSpec((s, qw), lambda h: (0, h)),
                   pl.BlockSpec((s, LANE), lambda h: (0, h))] + [ANY] * ns,
        out_shape=[jax.ShapeDtypeStruct(q.shape, F32), jax.ShapeDtypeStruct(k.shape, F32), jax.ShapeDtypeStruct((s, DIL_W), F32)]
        + _Scatter.out_shapes(scatter),
        scratch_shapes=[pltpu.VMEM((qw, s), BF16), pltpu.VMEM((LANE, s), BF16), pltpu.VMEM((s, LANE), BF16),
                        pltpu.VMEM((qw, s), F32), pltpu.VMEM((2, 1, s), F32), pltpu.VMEM((2, 1, s), F32)]
        + ([] if mla else [pltpu.VMEM((NEAR_OFFSETS, tk, tq), F32)] * 2) + (_Scatter.semaphores(ns) if ns else []),
        compiler_params=_params(("arbitrary",) if ns else ("parallel",), 24 << 20),
    )(*_in_hbm(q, k, v, o, do, lse), *scatter)


def _ada_fwd(c_all, w_shard, b_shard):
    n, d = c_all.shape
    cols = w_shard.shape[1]

    def body(c_ref, w_ref, b_ref, o_ref):
        cv = c_ref[...]
        sc = (cv * _sigmoid(cv)).astype(BF16)
        o_ref[...] = jnp.dot(sc, w_ref[...].astype(BF16), preferred_element_type=F32) + b_ref[...]

    return pl.pallas_call(
        body, name="ada_fwd", out_shape=jax.ShapeDtypeStruct((n, cols), F32),
        compiler_params=_params(None, 16 << 20),
    )(c_all, w_shard, b_shard)


def _ada_bwd(c_all, dmod_shard):
    n, d = c_all.shape
    cols = dmod_shard.shape[1]

    def body(c_ref, g_ref, o_ref):
        cv = c_ref[...]
        o_ref[...] = lax.dot_general(cv * _sigmoid(cv), g_ref[...], TN, precision=HIGHEST, preferred_element_type=F32)

    return pl.pallas_call(
        body, name="ada_bwd", out_shape=jax.ShapeDtypeStruct((d, cols), F32),
        compiler_params=_params(None, 16 << 20),
    )(c_all, dmod_shard)


SMALL_WIDTHS = (("g_mix_norm", D_MODEL), ("g_q_lat", Q_LORA), ("g_kv_lat", KV_LORA), ("g_mla_q_nope", NOPE),
                ("g_mla_q_pe", ROPE), ("g_mla_k_nope", NOPE), ("g_mla_k_pe", ROPE), ("g_dil_q", DIL_DIM),
                ("g_dil_k", DIL_DIM), ("g_ffn_norm", D_MODEL), ("b_conv", UP_W))


def _small_layout():
    pieces = (("dmod", 6 * D_MODEL),) + SMALL_WIDTHS + tuple(("w_conv%d" % k, UP_W) for k in range(3)) + (("loss", 1),)
    layout, off = {}, 0
    for name, width in pieces:
        layout[name] = (width, off)
        off += -(-width // LANE) * LANE
    return layout, off


def _pack_small(acc1, acc2, dg2, dglat, dgains, dbg, dbv, dwg, dwv, loss_part):
    layout, total = _small_layout()

    def body(a1, a2, g2, gl, gg, bg, bv, wg, wv, ls, o_ref):
        o_ref[...] = jnp.zeros_like(o_ref)

        def put(name, src, shift=0):
            start = layout[name][1] + shift
            o_ref[:, start:start + src.shape[1]] = src

        for k, src in enumerate((a1[0:1, :], a1[1:2, :], a2[3:4, :], a2[0:1, :], a2[1:2, :], g2[...])):
            put("dmod", src, k * D_MODEL)
        put("g_mix_norm", a1[2:3, :])
        put("g_q_lat", gl[0:1, :])
        put("g_kv_lat", gl[1:2, 0:KV_LORA])
        put("g_mla_q_nope", gg[0:1, 0:NOPE])
        put("g_mla_q_pe", gg[5:6, 0:ROPE])
        put("g_mla_k_nope", gg[1:2, 0:NOPE])
        put("g_mla_k_pe", gg[2:3, 0:ROPE])
        put("g_dil_q", gg[3:4, 0:DIL_DIM])
        put("g_dil_k", gg[4:5, 0:DIL_DIM])
        put("g_ffn_norm", a2[2:3, :])
        put("b_conv", bg[...])
        put("b_conv", bv[...], D_FF)
        for k in range(3):
            put("w_conv%d" % k, wg[k:k + 1, :])
            put("w_conv%d" % k, wv[k:k + 1, :], D_FF)
        put("loss", ls[...])

    ins = (acc1, acc2, dg2, dglat, dgains, dbg, dbv, dwg, dwv, loss_part)
    return pl.pallas_call(
        body, name="pack_small", grid=(1,), in_specs=[_full(a.shape) for a in ins], out_specs=_full((1, total)),
        out_shape=jax.ShapeDtypeStruct((1, total), F32),
        compiler_params=_params(("arbitrary",), 2 << 20),
    )(*_in_hbm(*ins))


def _sum_unpack(g):
    n_dev, _, total = g.shape
    layout, _ = _small_layout()

    def body(g_ref, *refs):
        o_refs, s_ref = refs[:-1], refs[-1]
        acc = g_ref[0]
        for k in range(1, n_dev):
            acc = acc + g_ref[k]
        s_ref[...] = acc
        take = lambda name: s_ref[:, layout[name][1]:layout[name][1] + layout[name][0]]
        o_refs[0][...] = take("dmod")
        for i, (name, _) in enumerate(SMALL_WIDTHS):
            o_refs[1 + i][...] = take(name)
        for k in range(3):
            o_refs[-2][k:k + 1, :] = take("w_conv%d" % k)
        o_refs[-1][...] = take("loss")

    shapes = [(1, 6 * D_MODEL)] + [(1, w) for _, w in SMALL_WIDTHS] + [(3, UP_W), (1, 1)]
    return pl.pallas_call(
        body, name="sum_unpack", out_shape=[jax.ShapeDtypeStruct(sh, F32) for sh in shapes],
        scratch_shapes=[pltpu.VMEM((1, total), F32)],
        compiler_params=_params(None, 4 << 20),
    )(g)


def _adamw_math(w, g, m, v):
    mn = ADAM_B1 * m + (1.0 - ADAM_B1) * g
    vn = ADAM_B2 * v + (1.0 - ADAM_B2) * (g * g)
    m_hat = mn / (1.0 - ADAM_B1 ** ADAM_STEP)
    v_hat = vn / (1.0 - ADAM_B2 ** ADAM_STEP)
    return -ADAM_LR * (m_hat / (jnp.sqrt(v_hat) + ADAM_EPS) + ADAM_WD * w), mn, vn


def _adamw_vectors(ws, gs, ms, vs):
    k = len(ws)

    def body(*refs):
        for i in range(k):
            d, mn, vn = _adamw_math(refs[i][...], refs[k + i][...], refs[2 * k + i][...], refs[3 * k + i][...])
            refs[4 * k + i][...] = d
            refs[5 * k + i][...] = mn
            refs[6 * k + i][...] = vn

    blocks = [_full(w.shape) for w in ws]
    outs = pl.pallas_call(
        body, name="adamw_vectors", grid=(1,), in_specs=blocks * 4, out_specs=blocks * 3,
        out_shape=[jax.ShapeDtypeStruct(w.shape, F32) for w in ws] * 3,
        compiler_params=_params(("arbitrary",), 2 << 20),
    )(*_in_hbm(*ws, *gs, *ms, *vs))
    return outs[:k], outs[k:2 * k], outs[2 * k:]


def _adamw(w, g, m, v, name):
    r, c = w.shape
    tr = r
    for cand in (256, 128, 64, 32, 16, 8):
        if r % cand == 0 and r > cand:
            tr = cand
            break

    def body(w_ref, g_ref, m_ref, v_ref, d_ref, mo_ref, vo_ref):
        d_ref[...], mo_ref[...], vo_ref[...] = _adamw_math(w_ref[...], g_ref[...], m_ref[...], v_ref[...])

    blk = pl.BlockSpec((tr, c), lambda i: (i, 0))
    return pl.pallas_call(
        body, name=name, grid=(r // tr,), in_specs=[blk] * 4, out_specs=[blk] * 3,
        out_shape=[jax.ShapeDtypeStruct((r, c), F32)] * 3,
        compiler_params=_params(("parallel",), 7 * _nbytes((tr, c), F32)),
    )(w, g, m, v)


def _position():
    return lax.axis_index("x"), lax.axis_index("y"), lax.axis_index("c")


def _other_chips(x, y):
    return [(1 - x, y, 2 * (1 - x) + y), (x, 1 - y, 2 * x + (1 - y)), (1 - x, 1 - y, 2 * (1 - x) + (1 - y))]


def _ag_small(v, name):
    r, w = v.shape

    def body(v_ref, out_ref, send_sems, recv_sems, local_sem):
        x, y, c = _position()
        me = 4 * x + 2 * y + c
        mine = pltpu.make_async_copy(v_ref, out_ref.at[me], local_sem)
        mine.start()
        peers = []
        for k in range(1, N_DEV):
            fx, fy, fc = (k >> 2) & 1, (k >> 1) & 1, k & 1
            px = 1 - x if fx else x
            py = 1 - y if fy else y
            pc = 1 - c if fc else c
            peers.append((px, py, pc))
        sends = []
        for k, peer in enumerate(peers):
            cp = pltpu.make_async_remote_copy(src_ref=v_ref, dst_ref=out_ref.at[me], send_sem=send_sems.at[k],
                                              recv_sem=recv_sems.at[k], device_id=peer, device_id_type=MESH)
            cp.start()
            sends.append(cp)
        for k, (px, py, pc) in enumerate(peers):
            pltpu.make_async_remote_copy(src_ref=v_ref, dst_ref=out_ref.at[4 * px + 2 * py + pc], send_sem=send_sems.at[k],
                                         recv_sem=recv_sems.at[k], device_id=(px, py, pc), device_id_type=MESH).wait_recv()
        for cp in sends:
            cp.wait_send()
        mine.wait()

    return pl.pallas_call(
        body, name=name,
        out_shape=jax.ShapeDtypeStruct((N_DEV, r, w), F32),
        in_specs=[pl.BlockSpec(memory_space=pltpu.VMEM)],
        out_specs=pl.BlockSpec(memory_space=pltpu.VMEM),
        scratch_shapes=[pltpu.SemaphoreType.DMA((N_DEV - 1,)), pltpu.SemaphoreType.DMA((N_DEV - 1,)), pltpu.SemaphoreType.DMA],
        compiler_params=_params(None, 10 * _nbytes((r, w), F32)),
    )(v)


ANY = pl.BlockSpec(memory_space=pl.ANY)


def _ag_weights(shards, name):
    n = len(shards)

    def body(*refs):
        gather = _Gather(refs[:n], refs[n:2 * n], *refs[2 * n:])
        gather.start()
        gather.forward()
        gather.finish()

    return pl.pallas_call(
        body, name=name,
        out_shape=_Gather.out_shapes(shards), in_specs=[ANY] * n, out_specs=[ANY] * n,
        scratch_shapes=_Gather.semaphores(n),
    )(*shards)


class _Gather:
    def __init__(self, w_refs, out_refs, send_sems, recv_sems):
        x, y, c = _position()
        q0 = 2 * x + y
        sibling = (x, y, 1 - c)
        self.ici, self.ici_in, self.fwd, self.fwd_in = [], [], [], []
        for k, (w_ref, out_ref) in enumerate(zip(w_refs, out_refs)):
            half = w_ref.shape[0] // 2

            def blk(q, e, out_ref=out_ref, half=half):
                return out_ref.at[q, pl.ds(pl.multiple_of(e * half, 16), half), :]

            def copy(src, dst, i, to):
                return pltpu.make_async_remote_copy(src_ref=src, dst_ref=dst, send_sem=send_sems.at[i], recv_sem=recv_sems.at[i],
                                                    device_id=to, device_id_type=MESH)

            src = w_ref.at[pl.ds(pl.multiple_of(c * half, 16), half), :]
            for j, (cx, cy, qj) in enumerate(_other_chips(x, y)):
                self.ici.append(copy(src, blk(q0, c), 6 * k + j, (cx, cy, c)))
                self.ici_in.append(copy(blk(qj, c), blk(qj, c), 6 * k + j, (cx, cy, c)))
                self.fwd.append(copy(blk(qj, c), blk(qj, c), 6 * k + 3 + j, sibling))
                self.fwd_in.append(copy(blk(qj, 1 - c), blk(qj, 1 - c), 6 * k + 3 + j, sibling))

    @staticmethod
    def out_shapes(shards):
        return [jax.ShapeDtypeStruct((N_CHIP,) + s.shape, s.dtype) for s in shards]

    @staticmethod
    def semaphores(n):
        return [pltpu.SemaphoreType.DMA((6 * n,)), pltpu.SemaphoreType.DMA((6 * n,))]

    def start(self):
        for cp in self.ici:
            cp.start()

    def forward(self):
        for arrived, onward in zip(self.ici_in, self.fwd):
            arrived.wait_recv()
            onward.start()

    def finish(self):
        for cp in self.fwd_in:
            cp.wait_recv()
        for cp in self.ici + self.fwd:
            cp.wait_send()


def _swap_halves_d2d(grads, name):
    n = len(grads)

    def body(*refs):
        swap = _PairSwap(refs[:n], refs[n:2 * n], *refs[2 * n:])
        swap.start()
        swap.finish()

    return pl.pallas_call(
        body, name=name,
        out_shape=_PairSwap.out_shapes(grads), in_specs=[ANY] * n, out_specs=[ANY] * n,
        scratch_shapes=_PairSwap.semaphores(n),
    )(*grads)


class _PairSwap:
    def __init__(self, g_refs, out_refs, send_sems, recv_sems):
        x, y, c = _position()
        self.copies = [
            pltpu.make_async_remote_copy(src_ref=g_ref.at[:, 1 - c], dst_ref=out_ref, send_sem=send_sems.at[k],
                                         recv_sem=recv_sems.at[k], device_id=(x, y, 1 - c), device_id_type=MESH)
            for k, (g_ref, out_ref) in enumerate(zip(g_refs, out_refs))]

    @staticmethod
    def out_shapes(grads):
        return [jax.ShapeDtypeStruct((N_CHIP,) + g.shape[2:], g.dtype) for g in grads]

    @staticmethod
    def semaphores(n):
        return [pltpu.SemaphoreType.DMA((n,)), pltpu.SemaphoreType.DMA((n,))]

    def start(self):
        for cp in self.copies:
            cp.start()

    def finish(self):
        for cp in self.copies:
            cp.wait_recv()
        for cp in self.copies:
            cp.wait_send()


def _pair_sum(g, a, c_idx, name):
    _, _, rh, cols = g.shape
    tr = rh
    for cand in (256, 128, 64, 32, 16):
        if rh % cand == 0 and rh > cand:
            tr = cand
            break

    def body(c_ref, g_ref, a_ref, o_ref):
        o_ref[...] = (g_ref[...] + a_ref[...]).astype(BF16)

    return pl.pallas_call(
        body, name=name,
        grid_spec=pltpu.PrefetchScalarGridSpec(
            num_scalar_prefetch=1, grid=(N_CHIP, rh // tr),
            in_specs=[pl.BlockSpec((None, None, tr, cols), lambda q, i, c_ref: (q, c_ref[0], i, 0)),
                      pl.BlockSpec((None, tr, cols), lambda q, i, c_ref: (q, i, 0))],
            out_specs=pl.BlockSpec((None, tr, cols), lambda q, i, c_ref: (q, i, 0))),
        out_shape=jax.ShapeDtypeStruct((N_CHIP, rh, cols), BF16),
        compiler_params=_params(("parallel", "parallel"), 10 * _nbytes((tr, cols), F32)),
    )(c_idx, g, a)


def _scatter_partials(parts, name):
    n = len(parts)

    def body(*refs):
        scatter = _Scatter(refs[:n], refs[n:2 * n], *refs[2 * n:])
        scatter.start()
        scatter.finish()

    return pl.pallas_call(
        body, name=name,
        out_shape=_Scatter.out_shapes(parts), in_specs=[ANY] * n, out_specs=[ANY] * n,
        scratch_shapes=_Scatter.semaphores(n),
    )(*parts)


class _Scatter:
    def __init__(self, p_refs, out_refs, send_sems, recv_sems):
        x, y, c = _position()
        self.copies = []
        for k, (p_ref, out_ref) in enumerate(zip(p_refs, out_refs)):
            for j, (cx, cy, qj) in enumerate(_other_chips(x, y)):
                self.copies.append(pltpu.make_async_remote_copy(
                    src_ref=p_ref.at[qj], dst_ref=out_ref.at[j], send_sem=send_sems.at[3 * k + j],
                    recv_sem=recv_sems.at[3 * k + j], device_id=(cx, cy, c), device_id_type=MESH))

    @staticmethod
    def out_shapes(parts):
        return [jax.ShapeDtypeStruct((3,) + p.shape[1:], p.dtype) for p in parts]

    @staticmethod
    def semaphores(n):
        return [pltpu.SemaphoreType.DMA((3 * n,)), pltpu.SemaphoreType.DMA((3 * n,))]

    def start(self):
        for cp in self.copies:
            cp.start()

    def finish(self):
        for cp in self.copies:
            cp.wait_recv()
        for cp in self.copies:
            cp.wait_send()


def _shard_sum(p, b, q_idx, name):
    _, rh, cols = p.shape
    tr = rh
    for cand in (256, 128, 64, 32, 16):
        if rh % cand == 0 and rh > cand:
            tr = cand
            break

    def body(q_ref, p_ref, b_ref, o_ref):
        acc = p_ref[...].astype(F32)
        for j in range(3):
            acc = acc + b_ref[j].astype(F32)
        o_ref[...] = acc

    return pl.pallas_call(
        body, name=name,
        grid_spec=pltpu.PrefetchScalarGridSpec(
            num_scalar_prefetch=1, grid=(rh // tr,),
            in_specs=[pl.BlockSpec((None, tr, cols), lambda i, q_ref: (q_ref[0], i, 0)),
                      pl.BlockSpec((3, tr, cols), lambda i, q_ref: (0, i, 0))],
            out_specs=pl.BlockSpec((tr, cols), lambda i, q_ref: (i, 0))),
        out_shape=jax.ShapeDtypeStruct((rh, cols), F32),
        compiler_params=_params(("parallel",), 8 * _nbytes((tr, cols), F32)),
    )(q_idx, p, b)


def _join_halves(halves):
    n = len(halves)

    def body(*refs):
        h_refs, out_refs = refs[:n], refs[n:2 * n]
        send_sems, recv_sems = refs[2 * n:]
        x, y, c = _position()
        sibling = (x, y, 1 - c)
        cps = []
        for k in range(n):
            cp = pltpu.make_async_remote_copy(src_ref=h_refs[k], dst_ref=out_refs[k], send_sem=send_sems.at[k],
                                              recv_sem=recv_sems.at[k], device_id=sibling, device_id_type=MESH)
            cp.start()
            cps.append(cp)
        for cp in cps:
            cp.wait_recv()
        for cp in cps:
            cp.wait_send()

    return pl.pallas_call(
        body, name="rs_join",
        out_shape=[jax.ShapeDtypeStruct(h.shape, h.dtype) for h in halves],
        in_specs=[ANY] * n, out_specs=[ANY] * n,
        scratch_shapes=[pltpu.SemaphoreType.DMA((n,)), pltpu.SemaphoreType.DMA((n,))],
    )(*halves)


def _cols_from_shards(g):
    q, r, cs = g.shape
    return jnp.transpose(g, (1, 0, 2)).reshape(r, q * cs)


def _cols_to_shards(w):
    r, cfull = w.shape
    return jnp.transpose(w.reshape(r, N_CHIP, cfull // N_CHIP), (1, 0, 2))


def _pad_w_in(w):
    z = lambda n: jnp.zeros((w.shape[0], n), w.dtype)
    q_lat, kv_lat, kpe = w[:, 0:512], w[:, 512:768], w[:, 768:800]
    qd, kd, vd = w[:, 800:1312], w[:, 1312:1824], w[:, 1824:2336]
    return jnp.concatenate([q_lat, qd, kd, vd, kv_lat, z(KPE_OFF), kpe, z(LANE - KPE_OFF - ROPE)], axis=1)


def _unpad_w_in(g):
    return jnp.concatenate([g[:, P_QLAT:P_QLAT + Q_LORA], g[:, P_KVLAT:P_KVLAT + KV_LORA],
                            g[:, P_KPE + KPE_OFF:P_KPE + KPE_OFF + ROPE], g[:, P_QD:P_QD + 3 * DIL_W]], axis=1)


def _pad_w_qb(w):
    w3 = w.reshape(Q_LORA, HEADS, NOPE + ROPE)
    return jnp.pad(w3, ((0, 0), (0, 0), (0, LANE - NOPE - ROPE))).reshape(Q_LORA, HEADS * LANE)


def _unpad_w_qb(g):
    return g.reshape(Q_LORA, HEADS, LANE)[:, :, :NOPE + ROPE].reshape(Q_LORA, HEADS * (NOPE + ROPE))


def _pad_w_kvb(w):
    w3 = w.reshape(KV_LORA, HEADS, 2 * NOPE)
    kp = jnp.pad(w3[:, :, :NOPE], ((0, 0), (0, 0), (0, LANE - NOPE))).reshape(KV_LORA, HEADS * LANE)
    return jnp.concatenate([kp, w3[:, :, NOPE:].reshape(KV_LORA, DIL_W)], axis=1)


def _unpad_w_kvb(g):
    gk = g[:, :HEADS * LANE].reshape(KV_LORA, HEADS, LANE)[:, :, :NOPE]
    gv = g[:, HEADS * LANE:].reshape(KV_LORA, HEADS, NOPE)
    return jnp.concatenate([gk, gv], axis=2).reshape(KV_LORA, HEADS * 2 * NOPE)


def _head_gains(g_q_nope, g_q_pe, g_k_nope, g_k_pe, g_dq, g_dk):
    z = lambda n: jnp.zeros((1, n), F32)
    q1 = jnp.concatenate([g_q_nope, g_q_pe, z(LANE - NOPE - ROPE)], axis=1)
    k1 = jnp.concatenate([g_k_nope, z(LANE - NOPE)], axis=1)
    kpe = jnp.concatenate([z(KPE_OFF), g_k_pe, z(LANE - KPE_OFF - ROPE)], axis=1)
    return dict(q=jnp.tile(q1, (1, HEADS)), k=jnp.tile(k1, (1, HEADS)), kpe=kpe,
                dq=jnp.tile(g_dq, (1, HEADS)), dk=jnp.tile(g_dk, (1, HEADS)))


def kernel(x, c, positions, w_ada, b_ada, g_mix_norm, w_in, g_q_lat, w_q_b, g_kv_lat, w_kv_b, g_mla_q_nope, g_mla_q_pe, g_mla_k_nope, g_mla_k_pe, g_dil_q, g_dil_k, w_o, g_ffn_norm, w_up, w_conv, b_conv, w_down, loss_target, m_w_ada, m_b_ada, m_g_mix_norm, m_w_in, m_g_q_lat, m_w_q_b, m_g_kv_lat, m_w_kv_b, m_g_mla_q_nope, m_g_mla_q_pe, m_g_mla_k_nope, m_g_mla_k_pe, m_g_dil_q, m_g_dil_k, m_w_o, m_g_ffn_norm, m_w_up, m_w_conv, m_b_conv, m_w_down, v_w_ada, v_b_ada, v_g_mix_norm, v_w_in, v_g_q_lat, v_w_q_b, v_g_kv_lat, v_w_kv_b, v_g_mla_q_nope, v_g_mla_q_pe, v_g_mla_k_nope, v_g_mla_k_pe, v_g_dil_q, v_g_dil_k, v_w_o, v_g_ffn_norm, v_w_up, v_w_conv, v_b_conv, v_w_down):
    args = dict(locals())
    weights = {n: args[n][0] for n in ("w_ada", "w_in", "w_q_b", "w_kv_b", "w_o", "w_up", "w_conv", "w_down")}
    small_w = {n: args[n] for n in ("b_ada",) + tuple(n for n, _ in SMALL_WIDTHS)}
    mom_m = {n[2:]: (args[n][0] if args[n].ndim == 3 else args[n]) for n in args if n.startswith("m_")}
    mom_v = {n[2:]: (args[n][0] if args[n].ndim == 3 else args[n]) for n in args if n.startswith("v_")}

    xi, yi, ci = _position()
    q0 = 2 * xi + yi
    me = 4 * xi + 2 * yi + ci
    xs, tgt = x[0], loss_target[0]
    s = xs.shape[0]
    consts = _seg_consts()
    c_idx, q_idx = jnp.reshape(ci, (1,)).astype(I32), jnp.reshape(q0, (1,)).astype(I32)

    def halves(g4):
        q, r, cc = g4.shape
        return g4.reshape(q, 2, r // 2, cc)

    conv_cols = UP_W // N_CHIP
    c_and_taps = _ag_small(jnp.concatenate([c, weights["w_conv"].reshape(1, 3 * conv_cols)], axis=1), "ag_c")[:, 0, :]
    c_all = c_and_taps[:, :D_MODEL]
    w_conv_f = c_and_taps[:, D_MODEL:].reshape(N_CHIP, 2, 3, conv_cols)[:, 0]
    w_conv_f = jnp.transpose(w_conv_f, (1, 0, 2)).reshape(3, UP_W)
    ada_cols = w_ada.shape[2]
    b_shard = lax.dynamic_slice_in_dim(b_ada, q0 * ada_cols, ada_cols, axis=1)
    mod_blk = _ada_fwd(c_all, weights["w_ada"], b_shard)
    mod_all = _ag_small(mod_blk, "ag_mod").reshape(N_CHIP, 2, N_DEV, ada_cols)
    mod = lax.dynamic_index_in_dim(lax.dynamic_index_in_dim(mod_all, ci, 1, False), me, 1, False)
    mod = mod.reshape(1, N_CHIP * ada_cols)
    sh1, sc1, g1, sh2, sc2, g2 = [mod[:, k * D_MODEL:(k + 1) * D_MODEL] for k in range(6)]

    place_own = lambda gs, ws: [lax.dynamic_update_slice_in_dim(g, w[None], q0, axis=0) for g, w in zip(gs, ws)]
    own_first = [weights[n].astype(BF16) for n in ("w_in", "w_q_b", "w_kv_b")]
    own_later = [weights[n].astype(BF16) for n in ("w_o", "w_up", "w_down")]
    gathered = place_own(_ag_weights(own_first, "ag_weights"), own_first)
    w_in_p = _pad_w_in(_cols_from_shards(gathered[0]))
    w_qb_p = _pad_w_qb(_cols_from_shards(gathered[1]))
    w_kvb_p = _pad_w_kvb(_cols_from_shards(gathered[2]))

    gains = _head_gains(g_mla_q_nope, g_mla_q_pe, g_mla_k_nope, g_mla_k_pe, g_dil_q, g_dil_k)
    tab = _rope_tables(positions.reshape(s, 1), *_rope_consts())

    h = _prenorm(xs, g_mix_norm, sc1, sh1, "prenorm")
    proj = _mm(h, w_in_p, "nn", F32, 512, P_COLS, "mm_in")
    ql, kvl = _latnorm(proj, g_q_lat, g_kv_lat)
    q_raw = _mm(ql, w_qb_p, "nn", F32, 512, HEADS * LANE, "mm_qb")
    kv_raw = _mm(kvl, w_kvb_p, "nn", F32, 512, HEADS * LANE + DIL_W, "mm_kvb")
    qm, km, vm, qd, kd, vd = _attn_prep(q_raw, kv_raw, proj, tab, gains, consts)
    scale_m, scale_d = (NOPE + ROPE) ** -0.5, DIL_DIM ** -0.5
    o_m, lse_m, got_up = _attn_fwd(qm, km, vm, True, scale_m, "attn_mla", gather=own_later[1:2])
    o_d, lse_d, got_o, got_down = _attn_fwd(qd, kd, vd, False, scale_d, "attn_dil", gather=[own_later[0], own_later[2]])
    gathered = place_own([got_o, got_up, got_down], own_later)
    w_o_f = gathered[0].reshape(D_MODEL, D_MODEL)
    w_up_f = _cols_from_shards(gathered[1])
    w_down_f = gathered[2].reshape(D_FF, D_MODEL)
    mix_in = jnp.concatenate([o_m, o_d], axis=1)
    mix = _mm(mix_in, w_o_f, "nn", F32, 512, D_MODEL, "mm_o")
    x1, h2 = _resid_prenorm(xs, mix, g1, g_ffn_norm, sc2, sh2)
    up = _mm(h2, w_up_f, "nn", F32, 512, CONV_TILE, "mm_up")
    act = _conv_gate(up, w_conv_f, b_conv)
    ffn = _mm(act, w_down_f, "nn", F32, 256, D_MODEL, "mm_down")
    dy, dffn, dg2, loss_part = _final(x1, ffn, tgt, g2)

    da = _mm(dffn, w_down_f, "nt", F32, 512, CONV_TILE, "mm_down_dx")
    gw_down = _mm(act, dffn, "tn", F32, 256, D_MODEL, "mm_down_dw")
    dup_g, dup_v, dbg, dbv, dwg, dwv = _gate_bwd(up, da, w_conv_f, b_conv)
    dup = jnp.concatenate([dup_g, dup_v], axis=1)
    early_names = ("w_up", "w_down", "w_o")
    gw_up = _mm(h2, dup, "tn", F32, 512, CONV_TILE, "mm_up_dw", col_shards=True)
    early = [halves(gw_up), halves(gw_down.reshape(N_CHIP, D_FF // N_CHIP, D_MODEL))]
    dh2, *early_sib = _mm(dup, w_up_f, "nt", F32, 256, 512, "mm_up_dx", swap=early)
    dx1, dmix, acc2 = _ffnnorm_bwd(dh2, x1, dy, mix, g_ffn_norm, sc2, g1)
    gw_o = _mm(mix_in, dmix, "tn", F32, 512, D_MODEL, "mm_o_dw")
    early.append(halves(gw_o.reshape(N_CHIP, D_MODEL // N_CHIP, D_MODEL)))
    dmix_in, sib_o = _mm(dmix, w_o_f, "nt", F32, 512, D_MODEL, "mm_o_dx", swap=early[2:])
    early_sib.append(sib_o)
    early_sums = [_pair_sum(g, a, c_idx, "pair_sum_" + n) for g, a, n in zip(early, early_sib, early_names)]
    dqm, dkm, dvm, *early_recv = _attn_bwd(qm, km, vm, o_m, dmix_in, 0, lse_m, True, scale_m, "attn_mla_bwd",
                                           scatter=early_sums[:1])
    dqd, dkd, dvd, *early_recv_d = _attn_bwd(qd, kd, vd, o_d, dmix_in, DIL_W // LANE, lse_d, False, scale_d,
                                             "attn_dil_bwd", scatter=early_sums[1:])
    early_recv = early_recv + early_recv_d
    dq_raw, dkv_raw, dkpe_b, dqd_b, dkd_b, dvd_b, dgains = _attn_prep_bwd(
        dqm, dkm, dvm, dqd, dkd, dvd, q_raw, kv_raw, proj, tab, gains, consts)
    dql = _mm(dq_raw, w_qb_p, "nt", F32, 512, Q_LORA, "mm_qb_dx")
    gw_qb = _unpad_w_qb(_mm(ql, dq_raw, "tn", F32, Q_LORA, HEADS * LANE, "mm_qb_dw"))
    dkvl = _mm(dkv_raw, w_kvb_p, "nt", F32, 512, KV_LORA, "mm_kvb_dx")
    gw_kvb = _unpad_w_kvb(_mm(kvl, dkv_raw, "tn", F32, KV_LORA, HEADS * LANE + DIL_W, "mm_kvb_dw"))
    dqlat_b, dkvlat_b, dglat = _latnorm_bwd(dql, dkvl, proj, g_q_lat, g_kv_lat)
    dproj = jnp.concatenate([dqlat_b, dqd_b, dkd_b, dvd_b, dkvlat_b, dkpe_b], axis=1)
    dh = _mm(dproj, w_in_p, "nt", F32, 512, D_MODEL, "mm_in_dx")
    gw_in = _unpad_w_in(_mm(h, dproj, "tn", F32, 512, P_COLS, "mm_in_dw"))
    grad_x, acc1 = _mixnorm_bwd(dh, xs, dx1, g_mix_norm, sc1)

    packed = _pack_small(acc1, acc2, dg2, dglat, dgains, dbg, dbv, dwg, dwv, loss_part)
    gathered_small = _ag_small(packed, "ag_small")
    grad_b_ada, *small_grads, gconv_full, loss_sum = _sum_unpack(gathered_small)
    grads = {"b_ada": grad_b_ada}
    grads.update({n: g for (n, _), g in zip(SMALL_WIDTHS, small_grads)})
    shard_cols = UP_W // N_CHIP
    grads["w_conv"] = lax.dynamic_slice_in_dim(gconv_full, q0 * shard_cols, shard_cols, axis=1)
    dmod_all = gathered_small[:, 0, :6 * D_MODEL]
    grads["w_ada"] = _ada_bwd(c_all, lax.dynamic_slice_in_dim(dmod_all, q0 * ada_cols, ada_cols, axis=1))

    late_names = ("w_in", "w_q_b", "w_kv_b")
    late = [halves(_cols_to_shards(gw_in)), halves(_cols_to_shards(gw_qb)), halves(_cols_to_shards(gw_kvb))]
    late_sib = _swap_halves_d2d(late, "rs_pair_swap_late")
    late_sums = [_pair_sum(g, a, c_idx, "pair_sum_" + n) for g, a, n in zip(late, late_sib, late_names)]
    late_recv = _scatter_partials(late_sums, "rs_scatter_late")
    big_names = late_names + early_names
    half_sums = [_shard_sum(p, b, q_idx, "shard_sum_" + n)
                 for p, b, n in zip(late_sums + early_sums, list(late_recv) + list(early_recv), big_names)]
    from_sib = _join_halves(half_sums)
    south = ci == 0
    for n, mine, theirs in zip(big_names, half_sums, from_sib):
        grads[n] = jnp.concatenate([jnp.where(south, mine, theirs), jnp.where(south, theirs, mine)], axis=0)

    delta, new_m, new_v = {}, {}, {}
    for n in ("w_ada", "w_in", "w_q_b", "w_kv_b", "w_o", "w_up", "w_conv", "w_down"):
        operands = (weights[n], grads[n], mom_m[n], mom_v[n])
        if n == "w_ada":
            operands = _in_hbm(*operands)
        delta[n], new_m[n], new_v[n] = _adamw(*operands, "adamw_" + n)
    vec_names = ("b_ada",) + tuple(n for n, _ in SMALL_WIDTHS)
    sd, sm, sv = _adamw_vectors(*[[d_[n] for n in vec_names] for d_ in (small_w, grads, mom_m, mom_v)])
    for k, n in enumerate(vec_names):
        delta[n], new_m[n], new_v[n] = sd[k], sm[k], sv[k]

    loss = loss_sum[0, 0]
    order = ("w_ada", "b_ada", "g_mix_norm", "w_in", "g_q_lat", "w_q_b", "g_kv_lat", "w_kv_b", "g_mla_q_nope", "g_mla_q_pe",
             "g_mla_k_nope", "g_mla_k_pe", "g_dil_q", "g_dil_k", "w_o", "g_ffn_norm", "w_up", "w_conv", "b_conv", "w_down")
    lead = lambda n, z: z[None] if n.startswith("w_") else z
    outs = [loss, grad_x[None]]
    for d_ in (grads, delta, new_m, new_v):
        outs += [lead(n, d_[n]) for n in order]
    return tuple(outs)
```

```python
import functools

import numpy as np
import jax
import jax.numpy as jnp
from jax import lax
from jax.experimental import pallas as pl
from jax.experimental.pallas import tpu as pltpu

F32 = jnp.float32
BF16 = jnp.bfloat16
I32 = jnp.int32

D_MODEL = 1024
HEADS = 8
NOPE = 64
ROPE = 32
Q_LORA = 512
KV_LORA = 256
DIL_DIM = 64
DIL_W = HEADS * DIL_DIM
D_FF = 2816
UP_W = 2 * D_FF
IN_COLS = Q_LORA + KV_LORA + ROPE + 3 * DIL_W
ROPE_THETA = 10000.0
EPS = 1e-6
NEG_INF = -1e30
N_DEV = 8
N_CHIP = 4

ADAM_LR = 0.001
ADAM_B1 = 0.9
ADAM_B2 = 0.999
ADAM_EPS = 1e-08
ADAM_WD = 0.01
ADAM_STEP = 10

LANE = 128
ROW_TILE = 256
ATT_TQ = 512
ATT_TK = 256
LOG2E = 1.4426950408889634
LN2 = 0.6931471805599453
VMEM_CAP = 56 * 1024 * 1024
VMEM_FLOOR = 32 * 1024 * 1024

P_QLAT, P_QD, P_KD, P_VD, P_KVLAT, P_KPE = 0, 512, 1024, 1536, 2048, 2304
P_COLS = 2432
KPE_OFF = 64

NN = (((1,), (0,)), ((), ()))
NT = (((1,), (1,)), ((), ()))
TN = (((0,), (0,)), ((), ()))
HIGHEST = lax.Precision.HIGHEST
MESH = pl.DeviceIdType.MESH


def _params(sem=None, est_bytes=0):
    limit = int(min(max(2 * est_bytes + (4 << 20), VMEM_FLOOR), VMEM_CAP))
    if sem is None:
        return pltpu.CompilerParams(vmem_limit_bytes=limit)
    return pltpu.CompilerParams(dimension_semantics=sem, vmem_limit_bytes=limit)


def _nbytes(shape, dtype):
    return int(np.prod(shape)) * jnp.dtype(dtype).itemsize


def _in_hbm(*xs):
    return [pltpu.with_memory_space_constraint(x, pltpu.HBM) for x in xs]


def _mm(a, b, dims, out_dtype, tm, tn, name, col_shards=False, swap=(), b_outer=False):
    def spec(block, index):
        if b_outer:
            return pl.BlockSpec(block, lambda g0, g1: index(g1, g0))
        return pl.BlockSpec(block, index)

    if dims == "nn":
        (m, k), (k2, n) = a.shape, b.shape
        a_spec = spec((tm, k), lambda i, j: (i, 0))
        b_spec = spec((k, tn), lambda i, j: (0, j))
        dn = NN
    elif dims == "nt":
        (m, k), (n, k2) = a.shape, b.shape
        a_spec = spec((tm, k), lambda i, j: (i, 0))
        b_spec = spec((tn, k), lambda i, j: (j, 0))
        dn = NT
    else:
        (k, m), (k2, n) = a.shape, b.shape
        a_spec = spec((k, tm), lambda i, j: (0, i))
        b_spec = spec((k, tn), lambda i, j: (0, j))
        dn = TN
    assert k == k2 and m % tm == 0 and n % tn == 0, (name, a.shape, b.shape, tm, tn)

    nw = len(swap)
    grid = (n // tn, m // tm) if b_outer else (m // tm, n // tn)

    def body(*refs):
        a_ref, b_ref, o_ref = refs[0], refs[1], refs[2 + nw]
        comm = (refs[2:2 + nw], refs[3 + nw:3 + 2 * nw]) + tuple(refs[3 + 2 * nw:])
        if nw:
            @pl.when((pl.program_id(0) == 0) & (pl.program_id(1) == 0))
            def _():
                _PairSwap(*comm).start()

        o_ref[...] = lax.dot_general(a_ref[...], b_ref[...], dn, preferred_element_type=F32).astype(o_ref.dtype)

        if nw:
            @pl.when((pl.program_id(0) == grid[0] - 1) & (pl.program_id(1) == grid[1] - 1))
            def _():
                _PairSwap(*comm).finish()

    est = _nbytes((tm, k), a.dtype) + _nbytes((tn, k), b.dtype) + _nbytes((tm, tn), F32) + _nbytes((tm, tn), out_dtype)
    if col_shards:
        out_spec = spec((None, tm, tn), lambda i, j: (j, i, 0))
        out_shape = jax.ShapeDtypeStruct((n // tn, m, tn), out_dtype)
    else:
        out_spec = spec((tm, tn), lambda i, j: (i, j))
        out_shape = jax.ShapeDtypeStruct((m, n), out_dtype)
    out = pl.pallas_call(
        body, name=name, grid=grid,
        in_specs=[a_spec, b_spec] + [ANY] * nw,
        out_specs=[out_spec] + [ANY] * nw,
        out_shape=[out_shape] + _PairSwap.out_shapes(swap),
        scratch_shapes=_PairSwap.semaphores(nw) if nw else [],
        compiler_params=_params(("arbitrary", "arbitrary") if nw else ("parallel", "parallel"), est),
    )(a, b, *swap)
    return out if nw else out[0]


def _seg_consts():
    seg_q = np.zeros((HEADS * LANE, LANE), np.float32)
    inv_q = np.zeros((1, LANE), np.float32)
    seg_k = np.zeros((HEADS * LANE, LANE), np.float32)
    inv_k = np.zeros((1, LANE), np.float32)
    seg_d = np.zeros((DIL_W, LANE), np.float32)
    inv_d = np.zeros((1, LANE), np.float32)
    for h in range(HEADS):
        seg_q[h * LANE:h * LANE + NOPE, 2 * h] = 1.0
        seg_q[h * LANE + NOPE:h * LANE + NOPE + ROPE, 2 * h + 1] = 1.0
        inv_q[0, 2 * h], inv_q[0, 2 * h + 1] = 1.0 / NOPE, 1.0 / ROPE
        seg_k[h * LANE:h * LANE + NOPE, h] = 1.0
        inv_k[0, h] = 1.0 / NOPE
        seg_d[h * DIL_DIM:(h + 1) * DIL_DIM, h] = 1.0
        inv_d[0, h] = 1.0 / DIL_DIM
    fold_q = np.tile(np.eye(LANE, dtype=np.float32), (HEADS, 1))
    fold_d = np.zeros((DIL_W, LANE), np.float32)
    fold_d[np.arange(DIL_W), np.arange(DIL_W) % DIL_DIM] = 1.0
    j = lambda v: jnp.asarray(v)
    b = lambda v: jnp.asarray(v, dtype=BF16)
    return dict(seg_q=b(seg_q), exp_q=b(seg_q.T.copy()), inv_q=j(inv_q), seg_k=b(seg_k), exp_k=b(seg_k.T.copy()),
                inv_k=j(inv_k), seg_d=b(seg_d), exp_d=b(seg_d.T.copy()), inv_d=j(inv_d), fold_q=j(fold_q), fold_d=j(fold_d))


def _rope_consts():
    inv_d = jnp.power(ROPE_THETA, -2.0 * jnp.arange(DIL_DIM // 2, dtype=F32) / DIL_DIM)
    inv_q = jnp.power(ROPE_THETA, -2.0 * jnp.arange(ROPE // 2, dtype=F32) / ROPE)
    lanes = np.arange(LANE)
    freq_d = inv_d[lanes % (DIL_DIM // 2)]
    in_pe = (lanes >= KPE_OFF) & (lanes < KPE_OFF + ROPE)
    freq_q = jnp.where(jnp.asarray(in_pe), inv_q[(lanes - KPE_OFF) % (ROPE // 2)], 0.0)
    sign_d = np.where(lanes % DIL_DIM < DIL_DIM // 2, -1.0, 1.0).astype(np.float32)
    sign_q = np.where(in_pe, np.where((lanes - KPE_OFF) < ROPE // 2, -1.0, 1.0), 0.0).astype(np.float32)
    zeros, ones = np.zeros(LANE, np.float32), np.ones(LANE, np.float32)
    freq = jnp.concatenate([freq_d, freq_d, freq_q, freq_q])[None, :]
    csel = jnp.asarray(np.concatenate([ones, zeros, ones, zeros]))[None, :]
    ssel = jnp.asarray(np.concatenate([zeros, sign_d, zeros, sign_q]))[None, :]
    return freq, csel, ssel


def _full(shape):
    return pl.BlockSpec(shape, lambda *_: (0,) * len(shape))


def _tile_lanes(x, n):
    return jnp.concatenate([x] * n, axis=1)


def _rms(x):
    return lax.rsqrt(jnp.mean(x * x, axis=-1, keepdims=True) + EPS)


def _prenorm(x, gain, scale, shift, name):
    s, d = x.shape

    def body(x_ref, g_ref, sc_ref, sh_ref, h_ref):
        xv = x_ref[...]
        h = (xv * _rms(xv)) * g_ref[...] * (1.0 + sc_ref[...]) + sh_ref[...]
        h_ref[...] = h.astype(BF16)

    row = pl.BlockSpec((ROW_TILE, d), lambda i: (i, 0))
    return pl.pallas_call(
        body, name=name, grid=(s // ROW_TILE,),
        in_specs=[row, _full((1, d)), _full((1, d)), _full((1, d))],
        out_specs=row, out_shape=jax.ShapeDtypeStruct((s, d), BF16),
        compiler_params=_params(("parallel",)),
    )(x, gain, scale, shift)


def _latnorm(proj, g_q, g_kv):
    s = proj.shape[0]

    def body(q_ref, kv_ref, gq_ref, gkv_ref, ql_ref, kvl_ref):
        q, kv = q_ref[...], kv_ref[...]
        ql_ref[...] = ((q * _rms(q)) * gq_ref[...]).astype(BF16)
        kvl_ref[...] = ((kv * _rms(kv)) * gkv_ref[...]).astype(BF16)

    return pl.pallas_call(
        body, name="latnorm", grid=(s // ROW_TILE,),
        in_specs=[pl.BlockSpec((ROW_TILE, Q_LORA), lambda i: (i, P_QLAT // Q_LORA)),
                  pl.BlockSpec((ROW_TILE, KV_LORA), lambda i: (i, P_KVLAT // KV_LORA)),
                  _full((1, Q_LORA)), _full((1, KV_LORA))],
        out_specs=[pl.BlockSpec((ROW_TILE, Q_LORA), lambda i: (i, 0)), pl.BlockSpec((ROW_TILE, KV_LORA), lambda i: (i, 0))],
        out_shape=[jax.ShapeDtypeStruct((s, Q_LORA), BF16), jax.ShapeDtypeStruct((s, KV_LORA), BF16)],
        compiler_params=_params(("parallel",)),
    )(proj, proj, g_q, g_kv)


def _dot01(v, mat01):
    hi = v.astype(BF16)
    lo = (v - hi.astype(F32)).astype(BF16)
    return jnp.dot(hi, mat01, preferred_element_type=F32) + jnp.dot(lo, mat01, preferred_element_type=F32)


def _seg_rinv(x, seg, exp, inv):
    r = lax.rsqrt(_dot01(x * x, seg) * inv + EPS)
    return _dot01(r, exp)


def _seg_mean(v, seg, exp, inv):
    return _dot01(_dot01(v, seg) * inv, exp)


def _swap_halves(x, half):
    n = x.shape[1]
    lane = lax.broadcasted_iota(I32, (1, n), 1)
    first = (lane & (2 * half - 1)) < half
    return jnp.where(first, pltpu.roll(x, n - half, 1), pltpu.roll(x, half, 1))


def _rope(x, cos, sin_signed, half):
    return x * cos + _swap_halves(x, half) * sin_signed


def _rope_bwd(dy, cos, sin_signed, half):
    return dy * cos + _swap_halves(dy * sin_signed, half)


def _pe_lane_mask(n):
    lane = lax.broadcasted_iota(I32, (1, n), 1) & (LANE - 1)
    return (lane >= KPE_OFF) & (lane < KPE_OFF + ROPE)


def _attn_prep(q_raw, kv_raw, proj, tab, gains, consts):
    s = q_raw.shape[0]
    hw = HEADS * LANE

    def body(q_ref, kv_ref, kpe_ref, qd_ref, kd_ref, vd_ref, tab_ref,
             gq_ref, gk_ref, gkpe_ref, gdq_ref, gdk_ref,
             segq_ref, expq_ref, invq_ref, segk_ref, expk_ref, invk_ref, segd_ref, expd_ref, invd_ref,
             qm_ref, km_ref, vm_ref, qdo_ref, kdo_ref, vdo_ref):
        tab_v = tab_ref[...]
        cos_d, sin_d = _tile_lanes(tab_v[:, 0:LANE], DIL_W // LANE), _tile_lanes(tab_v[:, LANE:2 * LANE], DIL_W // LANE)
        cos_q1, sin_q1 = tab_v[:, 2 * LANE:3 * LANE], tab_v[:, 3 * LANE:4 * LANE]
        cos_q, sin_q = _tile_lanes(cos_q1, HEADS), _tile_lanes(sin_q1, HEADS)

        q = q_ref[...]
        qn = q * _seg_rinv(q, segq_ref[...], expq_ref[...], invq_ref[...]) * gq_ref[...]
        qm_ref[...] = _rope(qn, cos_q, sin_q, ROPE // 2).astype(BF16)

        kv = kv_ref[...]
        kp = kv[:, :hw]
        kn = kp * _seg_rinv(kp, segk_ref[...], expk_ref[...], invk_ref[...]) * gk_ref[...]
        kpe = kpe_ref[...]
        r_pe = lax.rsqrt(jnp.sum(kpe * kpe, axis=-1, keepdims=True) * (1.0 / ROPE) + EPS)
        kpe_r = _rope(kpe * r_pe * gkpe_ref[...], cos_q1, sin_q1, ROPE // 2)
        km_ref[...] = (kn + _tile_lanes(kpe_r, HEADS)).astype(BF16)
        vm_ref[...] = kv[:, hw:].astype(BF16)

        qd = qd_ref[...]
        qdn = qd * _seg_rinv(qd, segd_ref[...], expd_ref[...], invd_ref[...]) * gdq_ref[...]
        qdo_ref[...] = _rope(qdn, cos_d, sin_d, DIL_DIM // 2).astype(BF16)
        kd = kd_ref[...]
        kdn = kd * _seg_rinv(kd, segd_ref[...], expd_ref[...], invd_ref[...]) * gdk_ref[...]
        kdo_ref[...] = _rope(kdn, cos_d, sin_d, DIL_DIM // 2).astype(BF16)
        vdo_ref[...] = vd_ref[...].astype(BF16)

    t = ROW_TILE
    row = lambda w, cb=0: pl.BlockSpec((t, w), lambda i: (i, cb))
    c = consts
    return pl.pallas_call(
        body, name="attn_prep", grid=(s // t,),
        in_specs=[row(hw), row(hw + DIL_W), row(LANE, P_KPE // LANE), row(DIL_W, P_QD // DIL_W), row(DIL_W, P_KD // DIL_W),
                  row(DIL_W, P_VD // DIL_W), row(4 * LANE),
                  _full((1, hw)), _full((1, hw)), _full((1, LANE)), _full((1, DIL_W)), _full((1, DIL_W)),
                  _full((hw, LANE)), _full((LANE, hw)), _full((1, LANE)), _full((hw, LANE)), _full((LANE, hw)), _full((1, LANE)),
                  _full((DIL_W, LANE)), _full((LANE, DIL_W)), _full((1, LANE))],
        out_specs=[row(hw), row(hw), row(DIL_W), row(DIL_W), row(DIL_W), row(DIL_W)],
        out_shape=[jax.ShapeDtypeStruct((s, hw), BF16), jax.ShapeDtypeStruct((s, hw), BF16)]
        + [jax.ShapeDtypeStruct((s, DIL_W), BF16)] * 4,
        compiler_params=_params(("parallel",), 24 << 20),
    )(*_in_hbm(q_raw, kv_raw, proj, proj, proj, proj, tab), gains["q"], gains["k"], gains["kpe"], gains["dq"], gains["dk"],
      c["seg_q"], c["exp_q"], c["inv_q"], c["seg_k"], c["exp_k"], c["inv_k"], c["seg_d"], c["exp_d"], c["inv_d"])


def _attn_prep_bwd(dqm, dkm, dvm, dqd, dkd, dvd, q_raw, kv_raw, proj, tab, gains, consts):
    s = q_raw.shape[0]
    hw = HEADS * LANE
    n_steps = s // ROW_TILE

    def body(dqm_ref, dkm_ref, dvm_ref, dqd_ref, dkd_ref, dvd_ref, q_ref, kv_ref, kpe_ref, qd_ref, kd_ref, tab_ref,
             gq_ref, gk_ref, gkpe_ref, gdq_ref, gdk_ref,
             segq_ref, expq_ref, invq_ref, segk_ref, expk_ref, invk_ref, segd_ref, expd_ref, invd_ref, foldq_ref, foldd_ref,
             dq_ref, dkv_ref, dkpe_ref, dqdo_ref, dkdo_ref, dvdo_ref, dg_ref, acc_ref):
        i = pl.program_id(0)

        @pl.when(i == 0)
        def _():
            acc_ref[...] = jnp.zeros_like(acc_ref)

        tab_v = tab_ref[...]
        cos_d, sin_d = _tile_lanes(tab_v[:, 0:LANE], DIL_W // LANE), _tile_lanes(tab_v[:, LANE:2 * LANE], DIL_W // LANE)
        cos_q1, sin_q1 = tab_v[:, 2 * LANE:3 * LANE], tab_v[:, 3 * LANE:4 * LANE]
        cos_q, sin_q = _tile_lanes(cos_q1, HEADS), _tile_lanes(sin_q1, HEADS)

        def norm_bwd(x, dyg, gain, seg, exp, inv):
            rinv = _seg_rinv(x, seg, exp, inv)
            xn = x * rinv
            dxn = dyg * gain
            dx = rinv * (dxn - xn * _seg_mean(dxn * xn, seg, exp, inv))
            return dx, jnp.sum(dyg * xn, axis=0, keepdims=True)

        dq, gq_l = norm_bwd(q_ref[...], _rope_bwd(dqm_ref[...], cos_q, sin_q, ROPE // 2), gq_ref[...],
                            segq_ref[...], expq_ref[...], invq_ref[...])
        dq_ref[...] = dq.astype(BF16)

        dkm = dkm_ref[...]
        kv = kv_ref[...]
        dkp, gk_l = norm_bwd(kv[:, :hw], dkm, gk_ref[...], segk_ref[...], expk_ref[...], invk_ref[...])
        dkv_ref[:, :hw] = dkp.astype(BF16)
        dkv_ref[:, hw:] = dvm_ref[...].astype(BF16)

        dkpe_r = dkm[:, 0:LANE]
        for h in range(1, HEADS):
            dkpe_r = dkpe_r + dkm[:, h * LANE:(h + 1) * LANE]
        dkpe_r = jnp.where(_pe_lane_mask(LANE), dkpe_r, 0.0)
        dyg = _rope_bwd(dkpe_r, cos_q1, sin_q1, ROPE // 2)
        kpe = kpe_ref[...]
        r_pe = lax.rsqrt(jnp.sum(kpe * kpe, axis=-1, keepdims=True) * (1.0 / ROPE) + EPS)
        xn = kpe * r_pe
        dxn = dyg * gkpe_ref[...]
        dkpe = r_pe * (dxn - xn * (jnp.sum(dxn * xn, axis=-1, keepdims=True) * (1.0 / ROPE)))
        dkpe_ref[...] = dkpe.astype(BF16)
        gkpe_l = jnp.sum(dyg * xn, axis=0, keepdims=True)

        dqd_v, gdq_l = norm_bwd(qd_ref[...], _rope_bwd(dqd_ref[...], cos_d, sin_d, DIL_DIM // 2), gdq_ref[...],
                                segd_ref[...], expd_ref[...], invd_ref[...])
        dqdo_ref[...] = dqd_v.astype(BF16)
        dkd_v, gdk_l = norm_bwd(kd_ref[...], _rope_bwd(dkd_ref[...], cos_d, sin_d, DIL_DIM // 2), gdk_ref[...],
                                segd_ref[...], expd_ref[...], invd_ref[...])
        dkdo_ref[...] = dkd_v.astype(BF16)
        dvdo_ref[...] = dvd_ref[...].astype(BF16)

        acc_ref[0:1, :] += gq_l
        acc_ref[1:2, :] += gk_l
        acc_ref[2:3, 0:LANE] += gkpe_l
        acc_ref[3:4, 0:DIL_W] += gdq_l
        acc_ref[4:5, 0:DIL_W] += gdk_l

        @pl.when(i == n_steps - 1)
        def _():
            acc = acc_ref[...]
            fq = jnp.dot(acc, foldq_ref[...], precision=HIGHEST, preferred_element_type=F32)
            fd = jnp.dot(acc[:, 0:DIL_W], foldd_ref[...], precision=HIGHEST, preferred_element_type=F32)
            rows = lax.broadcasted_iota(I32, (8, LANE), 0)
            base = jnp.where(rows < 2, fq, jnp.where(rows == 2, acc[:, 0:LANE], fd))
            at0 = pltpu.roll(base, LANE - KPE_OFF, 1)
            dg_ref[...] = jnp.where(rows == 5, pltpu.roll(at0, 5, 0), jnp.where(rows == 2, at0, base))

    t = ROW_TILE
    row = lambda w, cb=0: pl.BlockSpec((t, w), lambda i: (i, cb))
    c = consts
    return pl.pallas_call(
        body, name="attn_prep_bwd", grid=(n_steps,),
        in_specs=[row(hw), row(hw), row(DIL_W), row(DIL_W), row(DIL_W), row(DIL_W),
                  row(hw), row(hw + DIL_W), row(LANE, P_KPE // LANE), row(DIL_W, P_QD // DIL_W), row(DIL_W, P_KD // DIL_W),
                  row(4 * LANE),
                  _full((1, hw)), _full((1, hw)), _full((1, LANE)), _full((1, DIL_W)), _full((1, DIL_W)),
                  _full((hw, LANE)), _full((LANE, hw)), _full((1, LANE)), _full((hw, LANE)), _full((LANE, hw)), _full((1, LANE)),
                  _full((DIL_W, LANE)), _full((LANE, DIL_W)), _full((1, LANE)), _full((hw, LANE)), _full((DIL_W, LANE))],
        out_specs=[row(hw), row(hw + DIL_W), row(LANE), row(DIL_W), row(DIL_W), row(DIL_W), _full((8, LANE))],
        out_shape=[jax.ShapeDtypeStruct((s, hw), BF16), jax.ShapeDtypeStruct((s, hw + DIL_W), BF16),
                   jax.ShapeDtypeStruct((s, LANE), BF16)] + [jax.ShapeDtypeStruct((s, DIL_W), BF16)] * 3
        + [jax.ShapeDtypeStruct((8, LANE), F32)],
        scratch_shapes=[pltpu.VMEM((8, hw), F32)],
        compiler_params=_params(("arbitrary",), 28 << 20),
    )(*_in_hbm(dqm, dkm, dvm, dqd, dkd, dvd, q_raw, kv_raw, proj, proj, proj, tab),
      gains["q"], gains["k"], gains["kpe"], gains["dq"], gains["dk"],
      c["seg_q"], c["exp_q"], c["inv_q"], c["seg_k"], c["exp_k"], c["inv_k"], c["seg_d"], c["exp_d"], c["inv_d"],
      c["fold_q"], c["fold_d"])


def _latnorm_bwd(dql, dkvl, proj, g_q, g_kv):
    s = proj.shape[0]
    n_steps = s // ROW_TILE

    def body(dql_ref, dkvl_ref, q_ref, kv_ref, gq_ref, gkv_ref, dq_ref, dkv_ref, dg_ref):
        i = pl.program_id(0)

        @pl.when(i == 0)
        def _():
            dg_ref[...] = jnp.zeros_like(dg_ref)

        def one(x, dyg, gain):
            r = _rms(x)
            xn = x * r
            dxn = dyg * gain
            dx = r * (dxn - xn * jnp.mean(dxn * xn, axis=-1, keepdims=True))
            return dx, jnp.sum(dyg * xn, axis=0, keepdims=True)

        dq, gq_l = one(q_ref[...], dql_ref[...], gq_ref[...])
        dkv, gkv_l = one(kv_ref[...], dkvl_ref[...], gkv_ref[...])
        dq_ref[...] = dq.astype(BF16)
        dkv_ref[...] = dkv.astype(BF16)
        dg_ref[0:1, :] += gq_l
        dg_ref[1:2, 0:KV_LORA] += gkv_l

    t = ROW_TILE
    return pl.pallas_call(
        body, name="latnorm_bwd", grid=(n_steps,),
        in_specs=[pl.BlockSpec((t, Q_LORA), lambda i: (i, 0)), pl.BlockSpec((t, KV_LORA), lambda i: (i, 0)),
                  pl.BlockSpec((t, Q_LORA), lambda i: (i, P_QLAT // Q_LORA)),
                  pl.BlockSpec((t, KV_LORA), lambda i: (i, P_KVLAT // KV_LORA)),
                  _full((1, Q_LORA)), _full((1, KV_LORA))],
        out_specs=[pl.BlockSpec((t, Q_LORA), lambda i: (i, 0)), pl.BlockSpec((t, KV_LORA), lambda i: (i, 0)), _full((8, Q_LORA))],
        out_shape=[jax.ShapeDtypeStruct((s, Q_LORA), BF16), jax.ShapeDtypeStruct((s, KV_LORA), BF16),
                   jax.ShapeDtypeStruct((8, Q_LORA), F32)],
        compiler_params=_params(("arbitrary",)),
    )(dql, dkvl, proj, proj, g_q, g_kv)


def _resid_prenorm(x, mix, g1, gain, scale, shift):
    s, d = x.shape

    def body(x_ref, mix_ref, g1_ref, g_ref, sc_ref, sh_ref, x1_ref, h_ref):
        x1 = x_ref[...] + g1_ref[...] * mix_ref[...]
        x1_ref[...] = x1
        h_ref[...] = ((x1 * _rms(x1)) * g_ref[...] * (1.0 + sc_ref[...]) + sh_ref[...]).astype(BF16)

    row = pl.BlockSpec((ROW_TILE, d), lambda i: (i, 0))
    vec = _full((1, d))
    return pl.pallas_call(
        body, name="resid_prenorm", grid=(s // ROW_TILE,),
        in_specs=[row, row, vec, vec, vec, vec], out_specs=[row, row],
        out_shape=[jax.ShapeDtypeStruct((s, d), F32), jax.ShapeDtypeStruct((s, d), BF16)],
        compiler_params=_params(("parallel",)),
    )(x, mix, g1, gain, scale, shift)


CONV_TILE = 1408
HALO = 8


def _shift_down(x, halo, k):
    t = x.shape[0]
    row = lax.broadcasted_iota(I32, (t, 1), 0)
    out = pltpu.roll(x, k, 0)
    for r in range(k):
        out = jnp.where(row == r, halo[HALO - k + r:HALO - k + r + 1, :], out)
    return out


def _shift_up(x, halo, k):
    t = x.shape[0]
    row = lax.broadcasted_iota(I32, (t, 1), 0)
    out = pltpu.roll(x, t - k, 0)
    for r in range(k):
        out = jnp.where(row == t - k + r, halo[r:r + 1, :], out)
    return out


def _conv_fwd(x, halo, w, b):
    p1, p2 = _shift_down(x, halo, 1), _shift_down(x, halo, 2)
    u = b + p2 * w[0:1, :]
    u = u + p1 * w[1:2, :]
    u = u + x * w[2:3, :]
    return u, p1, p2


def _sigmoid(x):
    return 1.0 / (1.0 + jnp.exp(-x))


def _conv_gate(up, w_conv, b_conv):
    s = up.shape[0]
    t = ROW_TILE
    nj = D_FF // CONV_TILE
    hb = t // HALO

    def body(g_ref, v_ref, gh_ref, vh_ref, wg_ref, wv_ref, bg_ref, bv_ref, a_ref):
        live = (pl.program_id(0) > 0).astype(F32)
        ug, _, _ = _conv_fwd(g_ref[...], gh_ref[...] * live, wg_ref[...], bg_ref[...])
        uv, _, _ = _conv_fwd(v_ref[...], vh_ref[...] * live, wv_ref[...], bv_ref[...])
        a_ref[...] = (ug * _sigmoid(ug) * uv).astype(BF16)

    main = lambda off: pl.BlockSpec((t, CONV_TILE), lambda i, j: (i, j + off))
    halo = lambda off: pl.BlockSpec((HALO, CONV_TILE), lambda i, j: (jnp.maximum(i * hb - 1, 0), j + off))
    wsp = lambda off: pl.BlockSpec((3, CONV_TILE), lambda i, j: (0, j + off))
    bsp = lambda off: pl.BlockSpec((1, CONV_TILE), lambda i, j: (0, j + off))
    return pl.pallas_call(
        body, name="conv_gate", grid=(s // t, nj),
        in_specs=[main(0), main(nj), halo(0), halo(nj), wsp(0), wsp(nj), bsp(0), bsp(nj)],
        out_specs=pl.BlockSpec((t, CONV_TILE), lambda i, j: (i, j)),
        out_shape=jax.ShapeDtypeStruct((s, D_FF), BF16),
        compiler_params=_params(("parallel", "parallel"), 12 << 20),
    )(up, up, up, up, w_conv, w_conv, b_conv, b_conv)


def _gate_bwd(up, da, w_conv, b_conv):
    s = up.shape[0]
    t = ROW_TILE
    nj = D_FF // CONV_TILE
    hb = t // HALO
    n_i = s // t

    def body(g_ref, v_ref, gh_ref, vh_ref, gn_ref, vn_ref, da_ref, dan_ref, wg_ref, wv_ref, bg_ref, bv_ref,
             dupg_ref, dupv_ref, dbg_ref, dbv_ref, dwg_ref, dwv_ref):
        i = pl.program_id(1)

        @pl.when(i == 0)
        def _():
            for r in (dbg_ref, dbv_ref, dwg_ref, dwv_ref):
                r[...] = jnp.zeros_like(r)

        def d_gate(ug, uv, da_v):
            sg = _sigmoid(ug)
            return da_v * uv * (sg * (1.0 + ug * (1.0 - sg))), da_v * (ug * sg)

        live = (i > 0).astype(F32)
        xg, xv = g_ref[...], v_ref[...]
        wg, wv = wg_ref[...], wv_ref[...]
        ug, g1, g2 = _conv_fwd(xg, gh_ref[...] * live, wg, bg_ref[...])
        uv, v1, v2 = _conv_fwd(xv, vh_ref[...] * live, wv, bv_ref[...])
        dug, duv = d_gate(ug, uv, da_ref[...])

        more = (i < n_i - 1).astype(F32)
        ug_n, _, _ = _conv_fwd(gn_ref[...], xg[t - HALO:, :], wg, bg_ref[...])
        uv_n, _, _ = _conv_fwd(vn_ref[...], xv[t - HALO:, :], wv, bv_ref[...])
        dug_n, duv_n = d_gate(ug_n, uv_n, dan_ref[...] * more)

        def conv_t(du, du_n, w):
            return du * w[2:3, :] + _shift_up(du, du_n, 1) * w[1:2, :] + _shift_up(du, du_n, 2) * w[0:1, :]

        dupg_ref[...] = conv_t(dug, dug_n, wg).astype(BF16)
        dupv_ref[...] = conv_t(duv, duv_n, wv).astype(BF16)
        csum = lambda z: jnp.sum(z, axis=0, keepdims=True)
        dbg_ref[...] += csum(dug)
        dbv_ref[...] += csum(duv)
        dwg_ref[0:1, :] += csum(dug * g2)
        dwg_ref[1:2, :] += csum(dug * g1)
        dwg_ref[2:3, :] += csum(dug * xg)
        dwv_ref[0:1, :] += csum(duv * v2)
        dwv_ref[1:2, :] += csum(duv * v1)
        dwv_ref[2:3, :] += csum(duv * xv)

    last_halo = s // HALO - 1
    main = lambda off: pl.BlockSpec((t, CONV_TILE), lambda j, i: (i, j + off))
    halo = lambda off: pl.BlockSpec((HALO, CONV_TILE), lambda j, i: (jnp.maximum(i * hb - 1, 0), j + off))
    nxt = lambda off: pl.BlockSpec((HALO, CONV_TILE), lambda j, i: (jnp.minimum((i + 1) * hb, last_halo), j + off))
    wsp = lambda off: pl.BlockSpec((3, CONV_TILE), lambda j, i: (0, j + off))
    bsp = lambda off: pl.BlockSpec((1, CONV_TILE), lambda j, i: (0, j + off))
    outs = pl.pallas_call(
        body, name="gate_bwd", grid=(nj, n_i),
        in_specs=[main(0), main(nj), halo(0), halo(nj), nxt(0), nxt(nj), main(0), nxt(0),
                  wsp(0), wsp(nj), bsp(0), bsp(nj)],
        out_specs=[main(0), main(0),
                   pl.BlockSpec((1, CONV_TILE), lambda j, i: (0, j)), pl.BlockSpec((1, CONV_TILE), lambda j, i: (0, j)),
                   pl.BlockSpec((3, CONV_TILE), lambda j, i: (0, j)), pl.BlockSpec((3, CONV_TILE), lambda j, i: (0, j))],
        out_shape=[jax.ShapeDtypeStruct((s, D_FF), BF16), jax.ShapeDtypeStruct((s, D_FF), BF16),
                   jax.ShapeDtypeStruct((1, D_FF), F32), jax.ShapeDtypeStruct((1, D_FF), F32),
                   jax.ShapeDtypeStruct((3, D_FF), F32), jax.ShapeDtypeStruct((3, D_FF), F32)],
        compiler_params=_params(("parallel", "arbitrary"), 24 << 20),
    )(up, up, up, up, up, up, da, da, w_conv, w_conv, b_conv, b_conv)
    return outs


def _final(x1, ffn, tgt, g2):
    s, d = x1.shape
    n_steps = s // ROW_TILE

    def body(x1_ref, f_ref, t_ref, g2_ref, dy_ref, df_ref, dg2_ref, loss_ref, lacc_ref):
        i = pl.program_id(0)

        @pl.when(i == 0)
        def _():
            dg2_ref[...] = jnp.zeros_like(dg2_ref)
            lacc_ref[...] = jnp.zeros_like(lacc_ref)

        f = f_ref[...]
        e = x1_ref[...] + g2_ref[...] * f - t_ref[...]
        dy = e * (1.0 / d)
        dy_ref[...] = dy
        df_ref[...] = (dy * g2_ref[...]).astype(BF16)
        dg2_ref[...] += jnp.sum(dy * f, axis=0, keepdims=True)
        lacc_ref[...] += jnp.sum(e * e, axis=0, keepdims=True)

        @pl.when(i == n_steps - 1)
        def _():
            loss_ref[...] = jnp.sum(lacc_ref[...], axis=1, keepdims=True) * (0.5 / d)

    row = pl.BlockSpec((ROW_TILE, d), lambda i: (i, 0))
    return pl.pallas_call(
        body, name="final", grid=(n_steps,),
        in_specs=[row, row, row, _full((1, d))],
        out_specs=[row, row, _full((1, d)), _full((1, 1))],
        out_shape=[jax.ShapeDtypeStruct((s, d), F32), jax.ShapeDtypeStruct((s, d), BF16),
                   jax.ShapeDtypeStruct((1, d), F32), jax.ShapeDtypeStruct((1, 1), F32)],
        scratch_shapes=[pltpu.VMEM((1, d), F32)],
        compiler_params=_params(("arbitrary",)),
    )(x1, ffn, tgt, g2)


def _ffnnorm_bwd(dh2, x1, dy, mix, gain, scale, g1):
    s, d = x1.shape
    n_steps = s // ROW_TILE

    def body(dh_ref, x_ref, dy_ref, mix_ref, g_ref, sc_ref, g1_ref, dx_ref, dm_ref, acc_ref):
        i = pl.program_id(0)

        @pl.when(i == 0)
        def _():
            acc_ref[...] = jnp.zeros_like(acc_ref)

        dh, x = dh_ref[...], x_ref[...]
        r = _rms(x)
        xn = x * r
        dn = dh * (1.0 + sc_ref[...])
        dxn = dn * g_ref[...]
        dx = dy_ref[...] + r * (dxn - xn * jnp.mean(dxn * xn, axis=-1, keepdims=True))
        dx_ref[...] = dx
        dm_ref[...] = (dx * g1_ref[...]).astype(BF16)
        csum = lambda z: jnp.sum(z, axis=0, keepdims=True)
        acc_ref[0:1, :] += csum(dh)
        acc_ref[1:2, :] += csum(dh * (xn * g_ref[...]))
        acc_ref[2:3, :] += csum(dn * xn)
        acc_ref[3:4, :] += csum(dx * mix_ref[...])

    row = pl.BlockSpec((ROW_TILE, d), lambda i: (i, 0))
    vec = _full((1, d))
    return pl.pallas_call(
        body, name="ffnnorm_bwd", grid=(n_steps,),
        in_specs=[row, row, row, row, vec, vec, vec],
        out_specs=[row, row, _full((8, d))],
        out_shape=[jax.ShapeDtypeStruct((s, d), F32), jax.ShapeDtypeStruct((s, d), BF16), jax.ShapeDtypeStruct((8, d), F32)],
        compiler_params=_params(("arbitrary",)),
    )(dh2, x1, dy, mix, gain, scale, g1)


def _mixnorm_bwd(dh, x, dx1, gain, scale):
    s, d = x.shape
    n_steps = s // ROW_TILE

    def body(dh_ref, x_ref, dx1_ref, g_ref, sc_ref, gx_ref, acc_ref):
        i = pl.program_id(0)

        @pl.when(i == 0)
        def _():
            acc_ref[...] = jnp.zeros_like(acc_ref)

        dh, x = dh_ref[...], x_ref[...]
        r = _rms(x)
        xn = x * r
        dn = dh * (1.0 + sc_ref[...])
        dxn = dn * g_ref[...]
        gx_ref[...] = dx1_ref[...] + r * (dxn - xn * jnp.mean(dxn * xn, axis=-1, keepdims=True))
        csum = lambda z: jnp.sum(z, axis=0, keepdims=True)
        acc_ref[0:1, :] += csum(dh)
        acc_ref[1:2, :] += csum(dh * (xn * g_ref[...]))
        acc_ref[2:3, :] += csum(dn * xn)

    row = pl.BlockSpec((ROW_TILE, d), lambda i: (i, 0))
    vec = _full((1, d))
    return pl.pallas_call(
        body, name="mixnorm_bwd", grid=(n_steps,),
        in_specs=[row, row, row, vec, vec],
        out_specs=[row, _full((8, d))],
        out_shape=[jax.ShapeDtypeStruct((s, d), F32), jax.ShapeDtypeStruct((8, d), F32)],
        compiler_params=_params(("arbitrary",)),
    )(dh, x, dx1, gain, scale)


def _key_count(d, dilated):
    if not dilated:
        return jnp.where(d >= 0, 1.0, 0.0)
    one = lambda cond: jnp.where(cond, 1.0, 0.0)
    cnt = one(d <= 128) + one(((d & 3) == 0) & (d <= 512)) + one((d & 15) == 0)
    return jnp.where(d >= 0, cnt, 0.0)


def _block_kinds(mla):
    return (0, "diag", "none") if mla else (512, "near", "far")


NEAR_OFFSETS = 4


def _scores_t(ka, qa, scale, kind, rel_t, offset, near_tabs=None):
    return _mask_scores(lax.dot_general(ka, qa, NT, preferred_element_type=F32), scale, kind, rel_t, offset, near_tabs)


def _fill_near_tables(bias_ref, cnt_ref, rel_t):
    for idx in range(NEAR_OFFSETS):
        cnt = _key_count(rel_t + (idx - 1) * ATT_TK, True)
        cnt_ref[idx] = cnt
        bias_ref[idx] = jnp.where(cnt > 0.0, 0.0, NEG_INF)


def _mask_scores(products, scale, kind, rel_t, offset, near_tabs=None):
    st = products * (scale * LOG2E)
    cnt = None
    if kind == "diag":
        st = jnp.where(rel_t + offset >= 0, st, NEG_INF)
    elif kind == "far":
        st = jnp.where((rel_t & 15) == 0, st, NEG_INF)
    elif kind == "near":
        bias_ref, cnt_ref = near_tabs
        idx = offset // ATT_TK + 1
        st = st + bias_ref[idx]
        cnt = cnt_ref[idx]
    return st, cnt


def _attn_fwd(q, k, v, mla, scale, name, gather=()):
    s = q.shape[0]
    qw = 2 * LANE if mla else LANE
    tq, tk = ATT_TQ, ATT_TK
    reach, kind_near, kind_far = _block_kinds(mla)
    assert s % tq == 0 and tq % tk == 0 and reach % tk == 0 and (mla or (reach + tq) // tk == NEAR_OFFSETS)
    ng = len(gather)
    last_step = HEADS // 2 - 1

    def body(*refs):
        q_ref, k_ref, v_ref = refs[:3]
        o_ref, lse_ref = refs[3 + ng:5 + ng]
        vt_ref, st_ref = refs[5 + 2 * ng:7 + 2 * ng]
        near_tabs = None if mla else refs[7 + 2 * ng:9 + 2 * ng]
        n_tabs = 0 if mla else 2
        comm = (refs[3:3 + ng], refs[5 + ng:5 + 2 * ng]) + tuple(refs[7 + n_tabs + 2 * ng:])
        if ng:
            @pl.when(pl.program_id(0) == 0)
            def _():
                _Gather(*comm).start()

            @pl.when(pl.program_id(0) == last_step)
            def _():
                _Gather(*comm).forward()

        lane = lax.broadcasted_iota(I32, (1, LANE), 1)
        rel_t = lax.broadcasted_iota(I32, (tk, tq), 1) - lax.broadcasted_iota(I32, (tk, tq), 0)
        if not mla:
            _fill_near_tables(*near_tabs, rel_t)

        def transpose_v(j, carry):
            c0 = pl.multiple_of(j * tk, tk)
            vt_ref[:, pl.ds(c0, tk)] = v_ref[pl.ds(c0, tk), :].astype(F32).T.astype(BF16)
            return carry

        lax.fori_loop(0, s // tk, transpose_v, 0)

        def q_block(qi, carry):
            r0 = pl.multiple_of(qi * tq, tq)
            kcols = [slice(a * LANE, (a + 1) * LANE) if mla else slice(0, LANE) for a in range(2)]
            qas = [q_ref[pl.ds(r0, tq), kcols[a]] for a in range(2)]
            if not mla:
                qas = [jnp.where(lane < DIL_DIM, qas[0], jnp.zeros_like(qas[0])),
                       jnp.where(lane >= DIL_DIM, qas[1], jnp.zeros_like(qas[1]))]

            n_k = (r0 + tq) // tk

            def products(kj):
                c0 = pl.multiple_of(kj * tk, tk)
                return [lax.dot_general(k_ref[pl.ds(c0, tk), kcols[a]], qas[a], NT, preferred_element_type=F32)
                        for a in range(2)]

            for a, pr in enumerate(products(0)):
                st_ref[0, a] = pr

            def k_block(kj, c, kind):
                c0 = pl.multiple_of(kj * tk, tk)
                slot = kj & 1
                ahead = products(jnp.minimum(kj + 1, n_k - 1))
                out = []
                for a in range(2):
                    m, l, acc = c[a]
                    st, cnt = _mask_scores(st_ref[slot, a], scale, kind, rel_t, r0 - c0, near_tabs)
                    st_ref[1 - slot, a] = ahead[a]
                    m_new = jnp.maximum(m, jnp.max(st, axis=0, keepdims=True))
                    alpha = jnp.exp2(m - m_new)
                    p = jnp.exp2(st - m_new)
                    if cnt is not None:
                        p = p * cnt
                    l = alpha * l + jnp.sum(p, axis=0, keepdims=True)
                    vt = vt_ref[a * DIL_DIM:(a + 1) * DIL_DIM, pl.ds(c0, tk)]
                    acc = alpha * acc + jnp.dot(vt, p.astype(BF16), preferred_element_type=F32)
                    out.append((m_new, l, acc))
                return tuple(out)

            one = (jnp.full((1, tq), NEG_INF, F32), jnp.zeros((1, tq), F32), jnp.zeros((DIL_DIM, tq), F32))
            first_near = jnp.maximum((r0 - reach) // tk, 0)
            c = lax.fori_loop(0, first_near, functools.partial(k_block, kind=kind_far), (one, one))
            res = lax.fori_loop(first_near, (r0 + tq) // tk, functools.partial(k_block, kind=kind_near), c)
            o_t = jnp.concatenate([res[a][2] / res[a][1] for a in range(2)], axis=0)
            o_ref[pl.ds(r0, tq), :] = o_t.T.astype(BF16)
            for a in range(2):
                lse_ref[a, :, pl.ds(r0, tq)] = res[a][0] * LN2 + jnp.log(res[a][1])
            return carry

        lax.fori_loop(0, s // tq, q_block, 0)

        if ng:
            @pl.when(pl.program_id(0) == last_step)
            def _():
                _Gather(*comm).finish()

    return pl.pallas_call(
        body, name=name, grid=(HEADS // 2,),
        in_specs=[pl.BlockSpec((s, qw), lambda h: (0, h)), pl.BlockSpec((s, qw), lambda h: (0, h)),
                  pl.BlockSpec((s, LANE), lambda h: (0, h))] + [ANY] * ng,
        out_specs=[pl.BlockSpec((s, LANE), lambda h: (0, h)), pl.BlockSpec((2, 1, s), lambda h: (h, 0, 0))] + [ANY] * ng,
        out_shape=[jax.ShapeDtypeStruct((s, DIL_W), BF16), jax.ShapeDtypeStruct((HEADS, 1, s), F32)] + _Gather.out_shapes(gather),
        scratch_shapes=[pltpu.VMEM((LANE, s), BF16), pltpu.VMEM((2, 2, tk, tq), F32)]
        + ([] if mla else [pltpu.VMEM((NEAR_OFFSETS, tk, tq), F32)] * 2) + (_Gather.semaphores(ng) if ng else []),
        compiler_params=_params(("arbitrary",) if ng else ("parallel",), 12 << 20),
    )(*_in_hbm(q, k, v), *gather)


def _attn_bwd(q, k, v, o, do, do_block0, lse, mla, scale, name, scatter=()):
    s = q.shape[0]
    qw = 2 * LANE if mla else LANE
    tq, tk = ATT_TQ, ATT_TK
    nq = s // tq
    reach, kind_near, kind_far = _block_kinds(mla)
    assert s % tq == 0 and tq % tk == 0
    ns = len(scatter)
    last_step = HEADS // 2 - 1

    def body(*refs):
        q_ref, k_ref, v_ref, o_ref, do_ref, lse_ref = refs[:6]
        dq_ref, dk_ref, dv_ref = refs[6 + ns:9 + ns]
        kt_ref, dot_ref, dob_ref, dqt_ref, delta_ref, lse2_ref = refs[9 + 2 * ns:15 + 2 * ns]
        near_tabs = None if mla else refs[15 + 2 * ns:17 + 2 * ns]
        n_tabs = 0 if mla else 2
        comm = (refs[6:6 + ns], refs[9 + ns:9 + 2 * ns]) + tuple(refs[15 + n_tabs + 2 * ns:])
        if ns:
            @pl.when(pl.program_id(0) == 0)
            def _():
                _Scatter(*comm).start()

        lane = lax.broadcasted_iota(I32, (1, LANE), 1)
        row = lax.broadcasted_iota(I32, (LANE, 1), 0)
        rel_t = lax.broadcasted_iota(I32, (tk, tq), 1) - lax.broadcasted_iota(I32, (tk, tq), 0)
        if not mla:
            _fill_near_tables(*near_tabs, rel_t)

        def prepare(j, carry):
            c0 = pl.multiple_of(j * tk, tk)
            do_blk = do_ref[pl.ds(c0, tk), :]
            dob_ref[pl.ds(c0, tk), :] = do_blk.astype(BF16)
            do_t = do_blk.T
            dot_ref[:, pl.ds(c0, tk)] = do_t.astype(BF16)
            prod = do_t * o_ref[pl.ds(c0, tk), :].astype(F32).T
            delta_ref[0, :, pl.ds(c0, tk)] = jnp.sum(prod[0:DIL_DIM], axis=0, keepdims=True)
            delta_ref[1, :, pl.ds(c0, tk)] = jnp.sum(prod[DIL_DIM:LANE], axis=0, keepdims=True)
            for w in range(qw // LANE):
                kt_ref[w * LANE:(w + 1) * LANE, pl.ds(c0, tk)] = (
                    k_ref[pl.ds(c0, tk), w * LANE:(w + 1) * LANE].astype(F32).T.astype(BF16))
            return carry

        lax.fori_loop(0, s // tk, prepare, 0)
        dqt_ref[...] = jnp.zeros_like(dqt_ref)
        lse2_ref[...] = lse_ref[...] * LOG2E

        sels = [lane < DIL_DIM, lane >= DIL_DIM]
        rsels = [row < DIL_DIM, row >= DIL_DIM]
        cols = [slice(a * LANE, (a + 1) * LANE) if mla else slice(0, LANE) for a in range(2)]

        def k_block(kj, carry):
            c0 = pl.multiple_of(kj * tk, tk)
            kas = [k_ref[pl.ds(c0, tk), cols[a]] for a in range(2)]
            kts = [kt_ref[cols[a], pl.ds(c0, tk)] for a in range(2)]
            if not mla:
                kas = [jnp.where(sels[a], kas[a], jnp.zeros_like(kas[a])) for a in range(2)]
                kts = [jnp.where(rsels[a], kts[a], jnp.zeros_like(kts[a])) for a in range(2)]
            vb = v_ref[pl.ds(c0, tk), :]
            vbs = [jnp.where(sels[a], vb, jnp.zeros_like(vb)) for a in range(2)]

            first = c0 // tq

            def q_block(qi, c, kind):
                r0 = pl.multiple_of(qi * tq, tq)
                out, dq_parts = [], []
                for a in range(2):
                    dk_acc, dv_acc = c[a]
                    qa = q_ref[pl.ds(r0, tq), cols[a]]
                    st, cnt = _scores_t(kas[a], qa, scale, kind, rel_t, r0 - c0, near_tabs)
                    p = jnp.exp2(st - lse2_ref[a, :, pl.ds(r0, tq)])
                    if cnt is not None:
                        p = p * cnt
                    dp = jnp.dot(vbs[a], dot_ref[:, pl.ds(r0, tq)], preferred_element_type=F32)
                    ds = (p * (dp - delta_ref[a, :, pl.ds(r0, tq)]) * scale).astype(BF16)
                    dv_acc = dv_acc + jnp.dot(p.astype(BF16), dob_ref[pl.ds(r0, tq), :], preferred_element_type=F32)
                    dk_acc = dk_acc + jnp.dot(ds, qa, preferred_element_type=F32)
                    dq_parts.append(jnp.dot(kts[a], ds, preferred_element_type=F32))
                    out.append((dk_acc, dv_acc))
                if mla:
                    for a in range(2):
                        dqt_ref[cols[a], pl.ds(r0, tq)] += dq_parts[a]
                else:
                    dqt_ref[:, pl.ds(r0, tq)] += dq_parts[0] + dq_parts[1]
                return tuple(out)

            zero = jnp.zeros((tk, LANE), F32)
            last_near = jnp.minimum((c0 + tk - 1 + reach) // tq + 1, nq)
            c = lax.fori_loop(first, last_near, functools.partial(q_block, kind=kind_near), ((zero, zero), (zero, zero)))
            (dk0, dv0), (dk1, dv1) = lax.fori_loop(last_near, nq, functools.partial(q_block, kind=kind_far), c)
            if mla:
                dk_ref[pl.ds(c0, tk), cols[0]] = dk0
                dk_ref[pl.ds(c0, tk), cols[1]] = dk1
            else:
                dk_ref[pl.ds(c0, tk), :] = jnp.where(sels[0], dk0, dk1)
            dv_ref[pl.ds(c0, tk), :] = jnp.where(sels[0], dv0, dv1)
            return carry

        lax.fori_loop(0, s // tk, k_block, 0)

        def write_dq(j, carry):
            c0 = pl.multiple_of(j * tk, tk)
            for w in range(qw // LANE):
                dq_ref[pl.ds(c0, tk), w * LANE:(w + 1) * LANE] = dqt_ref[w * LANE:(w + 1) * LANE, pl.ds(c0, tk)].T
            return carry

        lax.fori_loop(0, s // tk, write_dq, 0)

        if ns:
            @pl.when(pl.program_id(0) == last_step)
            def _():
                _Scatter(*comm).finish()

    b0 = do_block0
    return pl.pallas_call(
        body, name=name, grid=(HEADS // 2,),
        in_specs=[pl.BlockSpec((s, qw), lambda h: (0, h)), pl.BlockSpec((s, qw), lambda h: (0, h)),
                  pl.BlockSpec((s, LANE), lambda h: (0, h)), pl.BlockSpec((s, LANE), lambda h: (0, h)),
                  pl.BlockSpec((s, LANE), lambda h: (0, h + b0)), pl.BlockSpec((2, 1, s), lambda h: (h, 0, 0))] + [ANY] * ns,
        out_specs=[pl.BlockSpec((s, qw), lambda h: (0, h)), pl.BlockSpec((s, qw), lambda h: (0, h)),
                   pl.BlockSpec((s, LANE), lambda h: (0, h))] + [ANY] * ns,
        out_shape=[jax.ShapeDtypeStruct(q.shape, F32), jax.ShapeDtypeStruct(k.shape, F32), jax.ShapeDtypeStruct((s, DIL_W), F32)]
        + _Scatter.out_shapes(scatter),
        scratch_shapes=[pltpu.VMEM((qw, s), BF16), pltpu.VMEM((LANE, s), BF16), pltpu.VMEM((s, LANE), BF16),
                        pltpu.VMEM((qw, s), F32), pltpu.VMEM((2, 1, s), F32), pltpu.VMEM((2, 1, s), F32)]
        + ([] if mla else [pltpu.VMEM((NEAR_OFFSETS, tk, tq), F32)] * 2) + (_Scatter.semaphores(ns) if ns else []),
        compiler_params=_params(("arbitrary",) if ns else ("parallel",), 24 << 20),
    )(*_in_hbm(q, k, v, o, do, lse), *scatter)


def _ada_bwd(c_all, dmod_shard):
    n, d = c_all.shape
    cols = dmod_shard.shape[1]

    def body(c_ref, g_ref, o_ref):
        cv = c_ref[...]
        o_ref[...] = lax.dot_general(cv * _sigmoid(cv), g_ref[...], TN, precision=HIGHEST, preferred_element_type=F32)

    return pl.pallas_call(
        body, name="ada_bwd", out_shape=jax.ShapeDtypeStruct((d, cols), F32),
        compiler_params=_params(None, 16 << 20),
    )(c_all, dmod_shard)


SMALL_WIDTHS = (("g_mix_norm", D_MODEL), ("g_q_lat", Q_LORA), ("g_kv_lat", KV_LORA), ("g_mla_q_nope", NOPE),
                ("g_mla_q_pe", ROPE), ("g_mla_k_nope", NOPE), ("g_mla_k_pe", ROPE), ("g_dil_q", DIL_DIM),
                ("g_dil_k", DIL_DIM), ("g_ffn_norm", D_MODEL), ("b_conv", UP_W))


def _small_layout():
    pieces = (("dmod", 6 * D_MODEL),) + SMALL_WIDTHS + tuple(("w_conv%d" % k, UP_W) for k in range(3)) + (("loss", 1),)
    layout, off = {}, 0
    for name, width in pieces:
        layout[name] = (width, off)
        off += -(-width // LANE) * LANE
    return layout, off


def _pack_small(acc1, acc2, dg2, dglat, dgains, dbg, dbv, dwg, dwv, loss_part):
    layout, total = _small_layout()

    def body(a1, a2, g2, gl, gg, bg, bv, wg, wv, ls, o_ref):
        o_ref[...] = jnp.zeros_like(o_ref)

        def put(name, src, shift=0):
            start = layout[name][1] + shift
            o_ref[:, start:start + src.shape[1]] = src

        for k, src in enumerate((a1[0:1, :], a1[1:2, :], a2[3:4, :], a2[0:1, :], a2[1:2, :], g2[...])):
            put("dmod", src, k * D_MODEL)
        put("g_mix_norm", a1[2:3, :])
        put("g_q_lat", gl[0:1, :])
        put("g_kv_lat", gl[1:2, 0:KV_LORA])
        put("g_mla_q_nope", gg[0:1, 0:NOPE])
        put("g_mla_q_pe", gg[5:6, 0:ROPE])
        put("g_mla_k_nope", gg[1:2, 0:NOPE])
        put("g_mla_k_pe", gg[2:3, 0:ROPE])
        put("g_dil_q", gg[3:4, 0:DIL_DIM])
        put("g_dil_k", gg[4:5, 0:DIL_DIM])
        put("g_ffn_norm", a2[2:3, :])
        put("b_conv", bg[...])
        put("b_conv", bv[...], D_FF)
        for k in range(3):
            put("w_conv%d" % k, wg[k:k + 1, :])
            put("w_conv%d" % k, wv[k:k + 1, :], D_FF)
        put("loss", ls[...])

    ins = (acc1, acc2, dg2, dglat, dgains, dbg, dbv, dwg, dwv, loss_part)
    return pl.pallas_call(
        body, name="pack_small", grid=(1,), in_specs=[_full(a.shape) for a in ins], out_specs=_full((1, total)),
        out_shape=jax.ShapeDtypeStruct((1, total), F32),
        compiler_params=_params(("arbitrary",), 2 << 20),
    )(*_in_hbm(*ins))


def _sum_unpack(g):
    n_dev, _, total = g.shape
    layout, _ = _small_layout()

    def body(g_ref, *refs):
        o_refs, s_ref = refs[:-1], refs[-1]
        acc = g_ref[0]
        for k in range(1, n_dev):
            acc = acc + g_ref[k]
        s_ref[...] = acc
        take = lambda name: s_ref[:, layout[name][1]:layout[name][1] + layout[name][0]]
        o_refs[0][...] = take("dmod")
        for i, (name, _) in enumerate(SMALL_WIDTHS):
            o_refs[1 + i][...] = take(name)
        for k in range(3):
            o_refs[-2][k:k + 1, :] = take("w_conv%d" % k)
        o_refs[-1][...] = take("loss")

    shapes = [(1, 6 * D_MODEL)] + [(1, w) for _, w in SMALL_WIDTHS] + [(3, UP_W), (1, 1)]
    return pl.pallas_call(
        body, name="sum_unpack", out_shape=[jax.ShapeDtypeStruct(sh, F32) for sh in shapes],
        scratch_shapes=[pltpu.VMEM((1, total), F32)],
        compiler_params=_params(None, 4 << 20),
    )(g)


def _adamw_math(w, g, m, v):
    mn = ADAM_B1 * m + (1.0 - ADAM_B1) * g
    vn = ADAM_B2 * v + (1.0 - ADAM_B2) * (g * g)
    m_hat = mn / (1.0 - ADAM_B1 ** ADAM_STEP)
    v_hat = vn / (1.0 - ADAM_B2 ** ADAM_STEP)
    return -ADAM_LR * (m_hat / (jnp.sqrt(v_hat) + ADAM_EPS) + ADAM_WD * w), mn, vn


def _adamw_vectors(ws, gs, ms, vs):
    k = len(ws)

    def body(*refs):
        for i in range(k):
            d, mn, vn = _adamw_math(refs[i][...], refs[k + i][...], refs[2 * k + i][...], refs[3 * k + i][...])
            refs[4 * k + i][...] = d
            refs[5 * k + i][...] = mn
            refs[6 * k + i][...] = vn

    blocks = [_full(w.shape) for w in ws]
    outs = pl.pallas_call(
        body, name="adamw_vectors", grid=(1,), in_specs=blocks * 4, out_specs=blocks * 3,
        out_shape=[jax.ShapeDtypeStruct(w.shape, F32) for w in ws] * 3,
        compiler_params=_params(("arbitrary",), 2 << 20),
    )(*_in_hbm(*ws, *gs, *ms, *vs))
    return outs[:k], outs[k:2 * k], outs[2 * k:]


def _adamw(w, g, m, v, name):
    r, c = w.shape
    tr = r
    for cand in (256, 128, 64, 32, 16, 8):
        if r % cand == 0 and r > cand:
            tr = cand
            break

    def body(w_ref, g_ref, m_ref, v_ref, d_ref, mo_ref, vo_ref):
        d_ref[...], mo_ref[...], vo_ref[...] = _adamw_math(w_ref[...], g_ref[...], m_ref[...], v_ref[...])

    blk = pl.BlockSpec((tr, c), lambda i: (i, 0))
    return pl.pallas_call(
        body, name=name, grid=(r // tr,), in_specs=[blk] * 4, out_specs=[blk] * 3,
        out_shape=[jax.ShapeDtypeStruct((r, c), F32)] * 3,
        compiler_params=_params(("parallel",), 7 * _nbytes((tr, c), F32)),
    )(w, g, m, v)


def _position():
    return lax.axis_index("x"), lax.axis_index("y"), lax.axis_index("c")


def _other_chips(x, y):
    return [(1 - x, y, 2 * (1 - x) + y), (x, 1 - y, 2 * x + (1 - y)), (1 - x, 1 - y, 2 * (1 - x) + (1 - y))]


def _ag_small(v, name):
    r, w = v.shape

    def body(v_ref, out_ref, send_sems, recv_sems, local_sem):
        gather = _SmallGather(v_ref, out_ref, send_sems, recv_sems, local_sem)
        gather.start()
        gather.finish()

    return pl.pallas_call(
        body, name=name,
        out_shape=jax.ShapeDtypeStruct((N_DEV, r, w), F32),
        in_specs=[IN_VMEM], out_specs=IN_VMEM,
        scratch_shapes=_SmallGather.semaphores(),
        compiler_params=_params(None, 10 * _nbytes((r, w), F32)),
    )(v)


class _SmallGather:
    def __init__(self, v_ref, out_ref, send_sems, recv_sems, local_sem):
        x, y, c = _position()
        me = 4 * x + 2 * y + c
        self.local = pltpu.make_async_copy(v_ref, out_ref.at[me], local_sem)
        self.sends, self.arrivals = [], []
        for k in range(N_DEV - 1):
            fx, fy, fc = ((k + 1) >> 2) & 1, ((k + 1) >> 1) & 1, (k + 1) & 1
            px, py, pc = (1 - x if fx else x), (1 - y if fy else y), (1 - c if fc else c)

            def copy(dst, k=k, peer=(px, py, pc)):
                return pltpu.make_async_remote_copy(src_ref=v_ref, dst_ref=dst, send_sem=send_sems.at[k],
                                                    recv_sem=recv_sems.at[k], device_id=peer, device_id_type=MESH)

            self.sends.append(copy(out_ref.at[me]))
            self.arrivals.append(copy(out_ref.at[4 * px + 2 * py + pc]))

    @staticmethod
    def semaphores():
        return [pltpu.SemaphoreType.DMA((N_DEV - 1,)), pltpu.SemaphoreType.DMA((N_DEV - 1,)), pltpu.SemaphoreType.DMA]

    def start(self):
        self.local.start()
        for cp in self.sends:
            cp.start()

    def finish(self):
        for cp in self.arrivals:
            cp.wait_recv()
        for cp in self.sends:
            cp.wait_send()
        self.local.wait()


def _prologue(c_taps, w_ada_shard, b_shard, pos_col, rope_consts, shards):
    n = len(shards)
    s = pos_col.shape[0]
    cols = w_ada_shard.shape[1]
    freq, csel, ssel = rope_consts

    def body(*refs):
        ct_ref, w_ref, b_ref, p_ref, f_ref, cs_ref, ss_ref = refs[:7]
        sh_refs = refs[7:7 + n]
        ct_all_ref, mod_all_ref, tab_ref = refs[7 + n:10 + n]
        g_refs = refs[10 + n:10 + 2 * n]
        mod_blk_ref = refs[10 + 2 * n]
        sems = refs[11 + 2 * n:]
        weights = _Gather(sh_refs, g_refs, *sems[6:8])
        weights.start()
        first = _SmallGather(ct_ref, ct_all_ref, *sems[0:3])
        first.start()
        first.finish()
        cv = ct_all_ref[:, 0, 0:D_MODEL]
        sc = (cv * _sigmoid(cv)).astype(BF16)
        mod_blk_ref[...] = jnp.dot(sc, w_ref[...].astype(BF16), preferred_element_type=F32) + b_ref[...]
        second = _SmallGather(mod_blk_ref, mod_all_ref, *sems[3:6])
        second.start()

        def table_rows(i, carry):
            r0 = pl.multiple_of(i * ROW_TILE, ROW_TILE)
            ang = p_ref[pl.ds(r0, ROW_TILE), :].astype(F32) * f_ref[...]
            tab_ref[pl.ds(r0, ROW_TILE), :] = cs_ref[...] * jnp.cos(ang) + ss_ref[...] * jnp.sin(ang)
            return carry

        lax.fori_loop(0, s // ROW_TILE, table_rows, 0)
        second.finish()
        weights.forward()
        weights.finish()

    return pl.pallas_call(
        body, name="prologue",
        out_shape=[jax.ShapeDtypeStruct((N_DEV,) + c_taps.shape, F32), jax.ShapeDtypeStruct((N_DEV, N_DEV, cols), F32),
                   jax.ShapeDtypeStruct((s, 4 * LANE), F32)] + _Gather.out_shapes(shards),
        in_specs=[IN_VMEM] * 7 + [ANY] * n, out_specs=[IN_VMEM] * 3 + [ANY] * n,
        scratch_shapes=[pltpu.VMEM((N_DEV, cols), F32)] + _SmallGather.semaphores() * 2 + _Gather.semaphores(n),
        compiler_params=_params(None, 14 << 20),
    )(c_taps, w_ada_shard, b_shard, pos_col, freq, csel, ssel, *shards)


IN_VMEM = pl.BlockSpec(memory_space=pltpu.VMEM)
ANY = pl.BlockSpec(memory_space=pl.ANY)


class _Gather:
    def __init__(self, w_refs, out_refs, send_sems, recv_sems):
        x, y, c = _position()
        q0 = 2 * x + y
        sibling = (x, y, 1 - c)
        self.ici, self.ici_in, self.fwd, self.fwd_in = [], [], [], []
        for k, (w_ref, out_ref) in enumerate(zip(w_refs, out_refs)):
            half = w_ref.shape[0] // 2

            def blk(q, e, out_ref=out_ref, half=half):
                return out_ref.at[q, pl.ds(pl.multiple_of(e * half, 16), half), :]

            def copy(src, dst, i, to):
                return pltpu.make_async_remote_copy(src_ref=src, dst_ref=dst, send_sem=send_sems.at[i], recv_sem=recv_sems.at[i],
                                                    device_id=to, device_id_type=MESH)

            src = w_ref.at[pl.ds(pl.multiple_of(c * half, 16), half), :]
            for j, (cx, cy, qj) in enumerate(_other_chips(x, y)):
                self.ici.append(copy(src, blk(q0, c), 6 * k + j, (cx, cy, c)))
                self.ici_in.append(copy(blk(qj, c), blk(qj, c), 6 * k + j, (cx, cy, c)))
                self.fwd.append(copy(blk(qj, c), blk(qj, c), 6 * k + 3 + j, sibling))
                self.fwd_in.append(copy(blk(qj, 1 - c), blk(qj, 1 - c), 6 * k + 3 + j, sibling))

    @staticmethod
    def out_shapes(shards):
        return [jax.ShapeDtypeStruct((N_CHIP,) + s.shape, s.dtype) for s in shards]

    @staticmethod
    def semaphores(n):
        return [pltpu.SemaphoreType.DMA((6 * n,)), pltpu.SemaphoreType.DMA((6 * n,))]

    def start(self):
        for cp in self.ici:
            cp.start()

    def forward(self):
        for arrived, onward in zip(self.ici_in, self.fwd):
            arrived.wait_recv()
            onward.start()

    def finish(self):
        for cp in self.fwd_in:
            cp.wait_recv()
        for cp in self.ici + self.fwd:
            cp.wait_send()


def _swap_halves_d2d(grads, name):
    n = len(grads)

    def body(*refs):
        swap = _PairSwap(refs[:n], refs[n:2 * n], *refs[2 * n:])
        swap.start()
        swap.finish()

    return pl.pallas_call(
        body, name=name,
        out_shape=_PairSwap.out_shapes(grads), in_specs=[ANY] * n, out_specs=[ANY] * n,
        scratch_shapes=_PairSwap.semaphores(n),
    )(*grads)


class _PairSwap:
    def __init__(self, g_refs, out_refs, send_sems, recv_sems):
        x, y, c = _position()
        self.copies = [
            pltpu.make_async_remote_copy(src_ref=g_ref.at[:, 1 - c], dst_ref=out_ref, send_sem=send_sems.at[k],
                                         recv_sem=recv_sems.at[k], device_id=(x, y, 1 - c), device_id_type=MESH)
            for k, (g_ref, out_ref) in enumerate(zip(g_refs, out_refs))]

    @staticmethod
    def out_shapes(grads):
        return [jax.ShapeDtypeStruct((N_CHIP,) + g.shape[2:], g.dtype) for g in grads]

    @staticmethod
    def semaphores(n):
        return [pltpu.SemaphoreType.DMA((n,)), pltpu.SemaphoreType.DMA((n,))]

    def start(self):
        for cp in self.copies:
            cp.start()

    def finish(self):
        for cp in self.copies:
            cp.wait_recv()
        for cp in self.copies:
            cp.wait_send()


def _pair_sum(g, a, c_idx, name):
    _, _, rh, cols = g.shape
    tr = rh
    for cand in (256, 128, 64, 32, 16):
        if rh % cand == 0 and rh > cand:
            tr = cand
            break

    def body(c_ref, g_ref, a_ref, o_ref):
        o_ref[...] = (g_ref[...] + a_ref[...]).astype(BF16)

    return pl.pallas_call(
        body, name=name,
        grid_spec=pltpu.PrefetchScalarGridSpec(
            num_scalar_prefetch=1, grid=(N_CHIP, rh // tr),
            in_specs=[pl.BlockSpec((None, None, tr, cols), lambda q, i, c_ref: (q, c_ref[0], i, 0)),
                      pl.BlockSpec((None, tr, cols), lambda q, i, c_ref: (q, i, 0))],
            out_specs=pl.BlockSpec((None, tr, cols), lambda q, i, c_ref: (q, i, 0))),
        out_shape=jax.ShapeDtypeStruct((N_CHIP, rh, cols), BF16),
        compiler_params=_params(("parallel", "parallel"), 10 * _nbytes((tr, cols), F32)),
    )(c_idx, g, a)


def _scatter_partials(parts, name):
    n = len(parts)

    def body(*refs):
        scatter = _Scatter(refs[:n], refs[n:2 * n], *refs[2 * n:])
        scatter.start()
        scatter.finish()

    return pl.pallas_call(
        body, name=name,
        out_shape=_Scatter.out_shapes(parts), in_specs=[ANY] * n, out_specs=[ANY] * n,
        scratch_shapes=_Scatter.semaphores(n),
    )(*parts)


class _Scatter:
    def __init__(self, p_refs, out_refs, send_sems, recv_sems):
        x, y, c = _position()
        self.copies = []
        for k, (p_ref, out_ref) in enumerate(zip(p_refs, out_refs)):
            for j, (cx, cy, qj) in enumerate(_other_chips(x, y)):
                self.copies.append(pltpu.make_async_remote_copy(
                    src_ref=p_ref.at[qj], dst_ref=out_ref.at[j], send_sem=send_sems.at[3 * k + j],
                    recv_sem=recv_sems.at[3 * k + j], device_id=(cx, cy, c), device_id_type=MESH))

    @staticmethod
    def out_shapes(parts):
        return [jax.ShapeDtypeStruct((3,) + p.shape[1:], p.dtype) for p in parts]

    @staticmethod
    def semaphores(n):
        return [pltpu.SemaphoreType.DMA((3 * n,)), pltpu.SemaphoreType.DMA((3 * n,))]

    def start(self):
        for cp in self.copies:
            cp.start()

    def finish(self):
        for cp in self.copies:
            cp.wait_recv()
        for cp in self.copies:
            cp.wait_send()


def _shard_sum(p, b, q_idx, name):
    _, rh, cols = p.shape
    tr = rh
    for cand in (256, 128, 64, 32, 16):
        if rh % cand == 0 and rh > cand:
            tr = cand
            break

    def body(q_ref, p_ref, b_ref, o_ref):
        acc = p_ref[...].astype(F32)
        for j in range(3):
            acc = acc + b_ref[j].astype(F32)
        o_ref[...] = acc

    return pl.pallas_call(
        body, name=name,
        grid_spec=pltpu.PrefetchScalarGridSpec(
            num_scalar_prefetch=1, grid=(rh // tr,),
            in_specs=[pl.BlockSpec((None, tr, cols), lambda i, q_ref: (q_ref[0], i, 0)),
                      pl.BlockSpec((3, tr, cols), lambda i, q_ref: (0, i, 0))],
            out_specs=pl.BlockSpec((tr, cols), lambda i, q_ref: (i, 0))),
        out_shape=jax.ShapeDtypeStruct((rh, cols), F32),
        compiler_params=_params(("parallel",), 8 * _nbytes((tr, cols), F32)),
    )(q_idx, p, b)


def _join_halves(halves):
    n = len(halves)

    def body(*refs):
        h_refs, out_refs = refs[:n], refs[n:2 * n]
        send_sems, recv_sems = refs[2 * n:]
        x, y, c = _position()
        sibling = (x, y, 1 - c)
        cps = []
        for k in range(n):
            cp = pltpu.make_async_remote_copy(src_ref=h_refs[k], dst_ref=out_refs[k], send_sem=send_sems.at[k],
                                              recv_sem=recv_sems.at[k], device_id=sibling, device_id_type=MESH)
            cp.start()
            cps.append(cp)
        for cp in cps:
            cp.wait_recv()
        for cp in cps:
            cp.wait_send()

    return pl.pallas_call(
        body, name="rs_join",
        out_shape=[jax.ShapeDtypeStruct(h.shape, h.dtype) for h in halves],
        in_specs=[ANY] * n, out_specs=[ANY] * n,
        scratch_shapes=[pltpu.SemaphoreType.DMA((n,)), pltpu.SemaphoreType.DMA((n,))],
    )(*halves)


def _cols_from_shards(g):
    q, r, cs = g.shape
    return jnp.transpose(g, (1, 0, 2)).reshape(r, q * cs)


def _cols_to_shards(w):
    r, cfull = w.shape
    return jnp.transpose(w.reshape(r, N_CHIP, cfull // N_CHIP), (1, 0, 2))


def _pad_w_in(w):
    z = lambda n: jnp.zeros((w.shape[0], n), w.dtype)
    q_lat, kv_lat, kpe = w[:, 0:512], w[:, 512:768], w[:, 768:800]
    qd, kd, vd = w[:, 800:1312], w[:, 1312:1824], w[:, 1824:2336]
    return jnp.concatenate([q_lat, qd, kd, vd, kv_lat, z(KPE_OFF), kpe, z(LANE - KPE_OFF - ROPE)], axis=1)


def _unpad_w_in(g):
    return jnp.concatenate([g[:, P_QLAT:P_QLAT + Q_LORA], g[:, P_KVLAT:P_KVLAT + KV_LORA],
                            g[:, P_KPE + KPE_OFF:P_KPE + KPE_OFF + ROPE], g[:, P_QD:P_QD + 3 * DIL_W]], axis=1)


def _pad_w_qb(w):
    w3 = w.reshape(Q_LORA, HEADS, NOPE + ROPE)
    return jnp.pad(w3, ((0, 0), (0, 0), (0, LANE - NOPE - ROPE))).reshape(Q_LORA, HEADS * LANE)


def _unpad_w_qb(g):
    return g.reshape(Q_LORA, HEADS, LANE)[:, :, :NOPE + ROPE].reshape(Q_LORA, HEADS * (NOPE + ROPE))


def _pad_w_kvb(w):
    w3 = w.reshape(KV_LORA, HEADS, 2 * NOPE)
    kp = jnp.pad(w3[:, :, :NOPE], ((0, 0), (0, 0), (0, LANE - NOPE))).reshape(KV_LORA, HEADS * LANE)
    return jnp.concatenate([kp, w3[:, :, NOPE:].reshape(KV_LORA, DIL_W)], axis=1)


def _unpad_w_kvb(g):
    gk = g[:, :HEADS * LANE].reshape(KV_LORA, HEADS, LANE)[:, :, :NOPE]
    gv = g[:, HEADS * LANE:].reshape(KV_LORA, HEADS, NOPE)
    return jnp.concatenate([gk, gv], axis=2).reshape(KV_LORA, HEADS * 2 * NOPE)


def _head_gains(g_q_nope, g_q_pe, g_k_nope, g_k_pe, g_dq, g_dk):
    z = lambda n: jnp.zeros((1, n), F32)
    q1 = jnp.concatenate([g_q_nope, g_q_pe, z(LANE - NOPE - ROPE)], axis=1)
    k1 = jnp.concatenate([g_k_nope, z(LANE - NOPE)], axis=1)
    kpe = jnp.concatenate([z(KPE_OFF), g_k_pe, z(LANE - KPE_OFF - ROPE)], axis=1)
    return dict(q=jnp.tile(q1, (1, HEADS)), k=jnp.tile(k1, (1, HEADS)), kpe=kpe,
                dq=jnp.tile(g_dq, (1, HEADS)), dk=jnp.tile(g_dk, (1, HEADS)))


def kernel(x, c, positions, w_ada, b_ada, g_mix_norm, w_in, g_q_lat, w_q_b, g_kv_lat, w_kv_b, g_mla_q_nope, g_mla_q_pe, g_mla_k_nope, g_mla_k_pe, g_dil_q, g_dil_k, w_o, g_ffn_norm, w_up, w_conv, b_conv, w_down, loss_target, m_w_ada, m_b_ada, m_g_mix_norm, m_w_in, m_g_q_lat, m_w_q_b, m_g_kv_lat, m_w_kv_b, m_g_mla_q_nope, m_g_mla_q_pe, m_g_mla_k_nope, m_g_mla_k_pe, m_g_dil_q, m_g_dil_k, m_w_o, m_g_ffn_norm, m_w_up, m_w_conv, m_b_conv, m_w_down, v_w_ada, v_b_ada, v_g_mix_norm, v_w_in, v_g_q_lat, v_w_q_b, v_g_kv_lat, v_w_kv_b, v_g_mla_q_nope, v_g_mla_q_pe, v_g_mla_k_nope, v_g_mla_k_pe, v_g_dil_q, v_g_dil_k, v_w_o, v_g_ffn_norm, v_w_up, v_w_conv, v_b_conv, v_w_down):
    args = dict(locals())
    weights = {n: args[n][0] for n in ("w_ada", "w_in", "w_q_b", "w_kv_b", "w_o", "w_up", "w_conv", "w_down")}
    small_w = {n: args[n] for n in ("b_ada",) + tuple(n for n, _ in SMALL_WIDTHS)}
    mom_m = {n[2:]: (args[n][0] if args[n].ndim == 3 else args[n]) for n in args if n.startswith("m_")}
    mom_v = {n[2:]: (args[n][0] if args[n].ndim == 3 else args[n]) for n in args if n.startswith("v_")}

    xi, yi, ci = _position()
    q0 = 2 * xi + yi
    me = 4 * xi + 2 * yi + ci
    xs, tgt = x[0], loss_target[0]
    s = xs.shape[0]
    consts = _seg_consts()
    c_idx, q_idx = jnp.reshape(ci, (1,)).astype(I32), jnp.reshape(q0, (1,)).astype(I32)

    def halves(g4):
        q, r, cc = g4.shape
        return g4.reshape(q, 2, r // 2, cc)

    place_own = lambda gs, ws: [lax.dynamic_update_slice_in_dim(g, w[None], q0, axis=0) for g, w in zip(gs, ws)]
    own_first = [weights[n].astype(BF16) for n in ("w_in", "w_q_b", "w_kv_b")]
    own_later = [weights[n].astype(BF16) for n in ("w_o", "w_up", "w_down")]
    conv_cols = UP_W // N_CHIP
    ada_cols = w_ada.shape[2]
    b_shard = lax.dynamic_slice_in_dim(b_ada, q0 * ada_cols, ada_cols, axis=1)
    c_taps = jnp.concatenate([c, weights["w_conv"].reshape(1, 3 * conv_cols)], axis=1)
    c_taps_all, mod_all, tab, *gathered = _prologue(c_taps, weights["w_ada"], b_shard, positions.reshape(s, 1),
                                                    _rope_consts(), own_first)
    c_all = c_taps_all[:, 0, :D_MODEL]
    w_conv_f = c_taps_all[:, 0, D_MODEL:].reshape(N_CHIP, 2, 3, conv_cols)[:, 0]
    w_conv_f = jnp.transpose(w_conv_f, (1, 0, 2)).reshape(3, UP_W)
    mod_all = mod_all.reshape(N_CHIP, 2, N_DEV, ada_cols)
    mod = lax.dynamic_index_in_dim(lax.dynamic_index_in_dim(mod_all, ci, 1, False), me, 1, False)
    mod = mod.reshape(1, N_CHIP * ada_cols)
    sh1, sc1, g1, sh2, sc2, g2 = [mod[:, k * D_MODEL:(k + 1) * D_MODEL] for k in range(6)]
    gathered = place_own(gathered, own_first)
    w_in_p = _pad_w_in(_cols_from_shards(gathered[0]))
    w_qb_p = _pad_w_qb(_cols_from_shards(gathered[1]))
    w_kvb_p = _pad_w_kvb(_cols_from_shards(gathered[2]))
    gains = _head_gains(g_mla_q_nope, g_mla_q_pe, g_mla_k_nope, g_mla_k_pe, g_dil_q, g_dil_k)

    h = _prenorm(xs, g_mix_norm, sc1, sh1, "prenorm")
    proj = _mm(h, w_in_p, "nn", F32, 512, P_COLS, "mm_in")
    ql, kvl = _latnorm(proj, g_q_lat, g_kv_lat)
    q_raw = _mm(ql, w_qb_p, "nn", F32, 512, HEADS * LANE, "mm_qb")
    kv_raw = _mm(kvl, w_kvb_p, "nn", F32, 512, HEADS * LANE + DIL_W, "mm_kvb")
    qm, km, vm, qd, kd, vd = _attn_prep(q_raw, kv_raw, proj, tab, gains, consts)
    scale_m, scale_d = (NOPE + ROPE) ** -0.5, DIL_DIM ** -0.5
    o_m, lse_m, got_up = _attn_fwd(qm, km, vm, True, scale_m, "attn_mla", gather=own_later[1:2])
    o_d, lse_d, got_o, got_down = _attn_fwd(qd, kd, vd, False, scale_d, "attn_dil", gather=[own_later[0], own_later[2]])
    gathered = place_own([got_o, got_up, got_down], own_later)
    w_o_f = gathered[0].reshape(D_MODEL, D_MODEL)
    w_up_f = _cols_from_shards(gathered[1])
    w_down_f = gathered[2].reshape(D_FF, D_MODEL)
    mix_in = jnp.concatenate([o_m, o_d], axis=1)
    mix = _mm(mix_in, w_o_f, "nn", F32, 512, D_MODEL, "mm_o")
    x1, h2 = _resid_prenorm(xs, mix, g1, g_ffn_norm, sc2, sh2)
    up = _mm(h2, w_up_f, "nn", F32, 512, CONV_TILE, "mm_up")
    act = _conv_gate(up, w_conv_f, b_conv)
    ffn = _mm(act, w_down_f, "nn", F32, 256, D_MODEL, "mm_down")
    dy, dffn, dg2, loss_part = _final(x1, ffn, tgt, g2)

    da = _mm(dffn, w_down_f, "nt", F32, 512, CONV_TILE, "mm_down_dx")
    gw_down = _mm(act, dffn, "tn", F32, 256, D_MODEL, "mm_down_dw")
    dup_g, dup_v, dbg, dbv, dwg, dwv = _gate_bwd(up, da, w_conv_f, b_conv)
    dup = jnp.concatenate([dup_g, dup_v], axis=1)
    early_names = ("w_up", "w_down", "w_o")
    gw_up = _mm(h2, dup, "tn", F32, 512, CONV_TILE, "mm_up_dw", col_shards=True)
    early = [halves(gw_up), halves(gw_down.reshape(N_CHIP, D_FF // N_CHIP, D_MODEL))]
    dh2, *early_sib = _mm(dup, w_up_f, "nt", F32, 256, 512, "mm_up_dx", swap=early, b_outer=True)
    dx1, dmix, acc2 = _ffnnorm_bwd(dh2, x1, dy, mix, g_ffn_norm, sc2, g1)
    gw_o = _mm(mix_in, dmix, "tn", F32, 512, D_MODEL, "mm_o_dw")
    early.append(halves(gw_o.reshape(N_CHIP, D_MODEL // N_CHIP, D_MODEL)))
    dmix_in, sib_o = _mm(dmix, w_o_f, "nt", F32, 512, D_MODEL, "mm_o_dx", swap=early[2:])
    early_sib.append(sib_o)
    early_sums = [_pair_sum(g, a, c_idx, "pair_sum_" + n) for g, a, n in zip(early, early_sib, early_names)]
    dqm, dkm, dvm, *early_recv = _attn_bwd(qm, km, vm, o_m, dmix_in, 0, lse_m, True, scale_m, "attn_mla_bwd",
                                           scatter=early_sums[:1])
    dqd, dkd, dvd, *early_recv_d = _attn_bwd(qd, kd, vd, o_d, dmix_in, DIL_W // LANE, lse_d, False, scale_d,
                                             "attn_dil_bwd", scatter=early_sums[1:])
    early_recv = early_recv + early_recv_d
    dq_raw, dkv_raw, dkpe_b, dqd_b, dkd_b, dvd_b, dgains = _attn_prep_bwd(
        dqm, dkm, dvm, dqd, dkd, dvd, q_raw, kv_raw, proj, tab, gains, consts)
    dql = _mm(dq_raw, w_qb_p, "nt", F32, 512, Q_LORA, "mm_qb_dx")
    gw_qb = _unpad_w_qb(_mm(ql, dq_raw, "tn", F32, Q_LORA, HEADS * LANE, "mm_qb_dw"))
    dkvl = _mm(dkv_raw, w_kvb_p, "nt", F32, 512, KV_LORA, "mm_kvb_dx")
    gw_kvb = _unpad_w_kvb(_mm(kvl, dkv_raw, "tn", F32, KV_LORA, HEADS * LANE + DIL_W, "mm_kvb_dw"))
    dqlat_b, dkvlat_b, dglat = _latnorm_bwd(dql, dkvl, proj, g_q_lat, g_kv_lat)
    dproj = jnp.concatenate([dqlat_b, dqd_b, dkd_b, dvd_b, dkvlat_b, dkpe_b], axis=1)
    dh = _mm(dproj, w_in_p, "nt", F32, 512, D_MODEL, "mm_in_dx")
    gw_in = _unpad_w_in(_mm(h, dproj, "tn", F32, 512, P_COLS, "mm_in_dw"))
    grad_x, acc1 = _mixnorm_bwd(dh, xs, dx1, g_mix_norm, sc1)

    packed = _pack_small(acc1, acc2, dg2, dglat, dgains, dbg, dbv, dwg, dwv, loss_part)
    gathered_small = _ag_small(packed, "ag_small")
    grad_b_ada, *small_grads, gconv_full, loss_sum = _sum_unpack(gathered_small)
    grads = {"b_ada": grad_b_ada}
    grads.update({n: g for (n, _), g in zip(SMALL_WIDTHS, small_grads)})
    shard_cols = UP_W // N_CHIP
    grads["w_conv"] = lax.dynamic_slice_in_dim(gconv_full, q0 * shard_cols, shard_cols, axis=1)
    dmod_all = gathered_small[:, 0, :6 * D_MODEL]
    grads["w_ada"] = _ada_bwd(c_all, lax.dynamic_slice_in_dim(dmod_all, q0 * ada_cols, ada_cols, axis=1))

    late_names = ("w_in", "w_q_b", "w_kv_b")
    late = [halves(_cols_to_shards(gw_in)), halves(_cols_to_shards(gw_qb)), halves(_cols_to_shards(gw_kvb))]
    late_sib = _swap_halves_d2d(late, "rs_pair_swap_late")
    late_sums = [_pair_sum(g, a, c_idx, "pair_sum_" + n) for g, a, n in zip(late, late_sib, late_names)]
    late_recv = _scatter_partials(late_sums, "rs_scatter_late")
    big_names = late_names + early_names
    half_sums = [_shard_sum(p, b, q_idx, "shard_sum_" + n)
                 for p, b, n in zip(late_sums + early_sums, list(late_recv) + list(early_recv), big_names)]
    from_sib = _join_halves(half_sums)
    south = ci == 0
    for n, mine, theirs in zip(big_names, half_sums, from_sib):
        grads[n] = jnp.concatenate([jnp.where(south, mine, theirs), jnp.where(south, theirs, mine)], axis=0)

    delta, new_m, new_v = {}, {}, {}
    for n in ("w_ada", "w_in", "w_q_b", "w_kv_b", "w_o", "w_up", "w_conv", "w_down"):
        operands = (weights[n], grads[n], mom_m[n], mom_v[n])
        if n == "w_ada":
            operands = _in_hbm(*operands)
        delta[n], new_m[n], new_v[n] = _adamw(*operands, "adamw_" + n)
    vec_names = ("b_ada",) + tuple(n for n, _ in SMALL_WIDTHS)
    sd, sm, sv = _adamw_vectors(*[[d_[n] for n in vec_names] for d_ in (small_w, grads, mom_m, mom_v)])
    for k, n in enumerate(vec_names):
        delta[n], new_m[n], new_v[n] = sd[k], sm[k], sv[k]

    loss = loss_sum[0, 0]
    order = ("w_ada", "b_ada", "g_mix_norm", "w_in", "g_q_lat", "w_q_b", "g_kv_lat", "w_kv_b", "g_mla_q_nope", "g_mla_q_pe",
             "g_mla_k_nope", "g_mla_k_pe", "g_dil_q", "g_dil_k", "w_o", "g_ffn_norm", "w_up", "w_conv", "b_conv", "w_down")
    lead = lambda n, z: z[None] if n.startswith("w_") else z
    outs = [loss, grad_x[None]]
    for d_ in (grads, delta, new_m, new_v):
        outs += [lead(n, d_[n]) for n in order]
    return tuple(outs)
```

```python
import functools

import numpy as np
import jax
import jax.numpy as jnp
from jax import lax
from jax.experimental import pallas as pl
from jax.experimental.pallas import tpu as pltpu

F32 = jnp.float32
BF16 = jnp.bfloat16
I32 = jnp.int32

D_MODEL = 1024
HEADS = 8
NOPE = 64
ROPE = 32
Q_LORA = 512
KV_LORA = 256
DIL_DIM = 64
DIL_W = HEADS * DIL_DIM
D_FF = 2816
UP_W = 2 * D_FF
IN_COLS = Q_LORA + KV_LORA + ROPE + 3 * DIL_W
ROPE_THETA = 10000.0
EPS = 1e-6
NEG_INF = -1e30
N_DEV = 8
N_CHIP = 4

ADAM_LR = 0.001
ADAM_B1 = 0.9
ADAM_B2 = 0.999
ADAM_EPS = 1e-08
ADAM_WD = 0.01
ADAM_STEP = 10

LANE = 128
ROW_TILE = 256
ATT_TQ = 512
ATT_TK = 256
LOG2E = 1.4426950408889634
LN2 = 0.6931471805599453
VMEM_CAP = 56 * 1024 * 1024
VMEM_FLOOR = 32 * 1024 * 1024

P_QLAT, P_QD, P_KD, P_VD, P_KVLAT, P_KPE = 0, 512, 1024, 1536, 2048, 2304
P_COLS = 2432
KPE_OFF = 64

NN = (((1,), (0,)), ((), ()))
NT = (((1,), (1,)), ((), ()))
TN = (((0,), (0,)), ((), ()))
HIGHEST = lax.Precision.HIGHEST
MESH = pl.DeviceIdType.MESH


def _params(sem=None, est_bytes=0):
    limit = int(min(max(2 * est_bytes + (4 << 20), VMEM_FLOOR), VMEM_CAP))
    if sem is None:
        return pltpu.CompilerParams(vmem_limit_bytes=limit)
    return pltpu.CompilerParams(dimension_semantics=sem, vmem_limit_bytes=limit)


def _nbytes(shape, dtype):
    return int(np.prod(shape)) * jnp.dtype(dtype).itemsize


def _in_hbm(*xs):
    return [pltpu.with_memory_space_constraint(x, pltpu.HBM) for x in xs]


def _mm(a, b, dims, out_dtype, tm, tn, name, col_shards=False, swap=(), b_outer=False):
    def spec(block, index):
        if b_outer:
            return pl.BlockSpec(block, lambda g0, g1: index(g1, g0))
        return pl.BlockSpec(block, index)

    if dims == "nn":
        (m, k), (k2, n) = a.shape, b.shape
        a_spec = spec((tm, k), lambda i, j: (i, 0))
        b_spec = spec((k, tn), lambda i, j: (0, j))
        dn = NN
    elif dims == "nt":
        (m, k), (n, k2) = a.shape, b.shape
        a_spec = spec((tm, k), lambda i, j: (i, 0))
        b_spec = spec((tn, k), lambda i, j: (j, 0))
        dn = NT
    else:
        (k, m), (k2, n) = a.shape, b.shape
        a_spec = spec((k, tm), lambda i, j: (0, i))
        b_spec = spec((k, tn), lambda i, j: (0, j))
        dn = TN
    assert k == k2 and m % tm == 0 and n % tn == 0, (name, a.shape, b.shape, tm, tn)

    nw = len(swap)
    grid = (n // tn, m // tm) if b_outer else (m // tm, n // tn)

    def body(*refs):
        a_ref, b_ref, o_ref = refs[0], refs[1], refs[2 + nw]
        comm = (refs[2:2 + nw], refs[3 + nw:3 + 2 * nw]) + tuple(refs[3 + 2 * nw:])
        if nw:
            @pl.when((pl.program_id(0) == 0) & (pl.program_id(1) == 0))
            def _():
                _PairSwap(*comm).start()

        o_ref[...] = lax.dot_general(a_ref[...], b_ref[...], dn, preferred_element_type=F32).astype(o_ref.dtype)

        if nw:
            @pl.when((pl.program_id(0) == grid[0] - 1) & (pl.program_id(1) == grid[1] - 1))
            def _():
                _PairSwap(*comm).finish()

    est = _nbytes((tm, k), a.dtype) + _nbytes((tn, k), b.dtype) + _nbytes((tm, tn), F32) + _nbytes((tm, tn), out_dtype)
    if col_shards:
        out_spec = spec((None, tm, tn), lambda i, j: (j, i, 0))
        out_shape = jax.ShapeDtypeStruct((n // tn, m, tn), out_dtype)
    else:
        out_spec = spec((tm, tn), lambda i, j: (i, j))
        out_shape = jax.ShapeDtypeStruct((m, n), out_dtype)
    out = pl.pallas_call(
        body, name=name, grid=grid,
        in_specs=[a_spec, b_spec] + [ANY] * nw,
        out_specs=[out_spec] + [ANY] * nw,
        out_shape=[out_shape] + _PairSwap.out_shapes(swap),
        scratch_shapes=_PairSwap.semaphores(nw) if nw else [],
        compiler_params=_params(("arbitrary", "arbitrary") if nw else ("parallel", "parallel"), est),
    )(a, b, *swap)
    return out if nw else out[0]


def _seg_consts():
    seg_q = np.zeros((HEADS * LANE, LANE), np.float32)
    inv_q = np.zeros((1, LANE), np.float32)
    seg_k = np.zeros((HEADS * LANE, LANE), np.float32)
    inv_k = np.zeros((1, LANE), np.float32)
    seg_d = np.zeros((DIL_W, LANE), np.float32)
    inv_d = np.zeros((1, LANE), np.float32)
    for h in range(HEADS):
        seg_q[h * LANE:h * LANE + NOPE, 2 * h] = 1.0
        seg_q[h * LANE + NOPE:h * LANE + NOPE + ROPE, 2 * h + 1] = 1.0
        inv_q[0, 2 * h], inv_q[0, 2 * h + 1] = 1.0 / NOPE, 1.0 / ROPE
        seg_k[h * LANE:h * LANE + NOPE, h] = 1.0
        inv_k[0, h] = 1.0 / NOPE
        seg_d[h * DIL_DIM:(h + 1) * DIL_DIM, h] = 1.0
        inv_d[0, h] = 1.0 / DIL_DIM
    fold_q = np.tile(np.eye(LANE, dtype=np.float32), (HEADS, 1))
    fold_d = np.zeros((DIL_W, LANE), np.float32)
    fold_d[np.arange(DIL_W), np.arange(DIL_W) % DIL_DIM] = 1.0
    j = lambda v: jnp.asarray(v)
    b = lambda v: jnp.asarray(v, dtype=BF16)
    return dict(seg_q=b(seg_q), exp_q=b(seg_q.T.copy()), inv_q=j(inv_q), seg_k=b(seg_k), exp_k=b(seg_k.T.copy()),
                inv_k=j(inv_k), seg_d=b(seg_d), exp_d=b(seg_d.T.copy()), inv_d=j(inv_d), fold_q=j(fold_q), fold_d=j(fold_d))


def _rope_consts():
    inv_d = jnp.power(ROPE_THETA, -2.0 * jnp.arange(DIL_DIM // 2, dtype=F32) / DIL_DIM)
    inv_q = jnp.power(ROPE_THETA, -2.0 * jnp.arange(ROPE // 2, dtype=F32) / ROPE)
    lanes = np.arange(LANE)
    freq_d = inv_d[lanes % (DIL_DIM // 2)]
    in_pe = (lanes >= KPE_OFF) & (lanes < KPE_OFF + ROPE)
    freq_q = jnp.where(jnp.asarray(in_pe), inv_q[(lanes - KPE_OFF) % (ROPE // 2)], 0.0)
    sign_d = np.where(lanes % DIL_DIM < DIL_DIM // 2, -1.0, 1.0).astype(np.float32)
    sign_q = np.where(in_pe, np.where((lanes - KPE_OFF) < ROPE // 2, -1.0, 1.0), 0.0).astype(np.float32)
    zeros, ones = np.zeros(LANE, np.float32), np.ones(LANE, np.float32)
    freq = jnp.concatenate([freq_d, freq_d, freq_q, freq_q])[None, :]
    csel = jnp.asarray(np.concatenate([ones, zeros, ones, zeros]))[None, :]
    ssel = jnp.asarray(np.concatenate([zeros, sign_d, zeros, sign_q]))[None, :]
    return freq, csel, ssel


def _full(shape):
    return pl.BlockSpec(shape, lambda *_: (0,) * len(shape))


def _tile_lanes(x, n):
    return jnp.concatenate([x] * n, axis=1)


def _rms(x):
    return lax.rsqrt(jnp.mean(x * x, axis=-1, keepdims=True) + EPS)


def _prenorm(x, gain, scale, shift, name):
    s, d = x.shape

    def body(x_ref, g_ref, sc_ref, sh_ref, h_ref):
        xv = x_ref[...]
        h = (xv * _rms(xv)) * g_ref[...] * (1.0 + sc_ref[...]) + sh_ref[...]
        h_ref[...] = h.astype(BF16)

    row = pl.BlockSpec((ROW_TILE, d), lambda i: (i, 0))
    return pl.pallas_call(
        body, name=name, grid=(s // ROW_TILE,),
        in_specs=[row, _full((1, d)), _full((1, d)), _full((1, d))],
        out_specs=row, out_shape=jax.ShapeDtypeStruct((s, d), BF16),
        compiler_params=_params(("parallel",)),
    )(x, gain, scale, shift)


def _latnorm(proj, g_q, g_kv):
    s = proj.shape[0]

    def body(q_ref, kv_ref, gq_ref, gkv_ref, ql_ref, kvl_ref):
        q, kv = q_ref[...], kv_ref[...]
        ql_ref[...] = ((q * _rms(q)) * gq_ref[...]).astype(BF16)
        kvl_ref[...] = ((kv * _rms(kv)) * gkv_ref[...]).astype(BF16)

    return pl.pallas_call(
        body, name="latnorm", grid=(s // ROW_TILE,),
        in_specs=[pl.BlockSpec((ROW_TILE, Q_LORA), lambda i: (i, P_QLAT // Q_LORA)),
                  pl.BlockSpec((ROW_TILE, KV_LORA), lambda i: (i, P_KVLAT // KV_LORA)),
                  _full((1, Q_LORA)), _full((1, KV_LORA))],
        out_specs=[pl.BlockSpec((ROW_TILE, Q_LORA), lambda i: (i, 0)), pl.BlockSpec((ROW_TILE, KV_LORA), lambda i: (i, 0))],
        out_shape=[jax.ShapeDtypeStruct((s, Q_LORA), BF16), jax.ShapeDtypeStruct((s, KV_LORA), BF16)],
        compiler_params=_params(("parallel",)),
    )(proj, proj, g_q, g_kv)


def _dot01(v, mat01):
    hi = v.astype(BF16)
    lo = (v - hi.astype(F32)).astype(BF16)
    return jnp.dot(hi, mat01, preferred_element_type=F32) + jnp.dot(lo, mat01, preferred_element_type=F32)


def _seg_rinv(x, seg, exp, inv):
    r = lax.rsqrt(_dot01(x * x, seg) * inv + EPS)
    return _dot01(r, exp)


def _seg_mean(v, seg, exp, inv):
    return _dot01(_dot01(v, seg) * inv, exp)


def _swap_halves(x, half):
    n = x.shape[1]
    lane = lax.broadcasted_iota(I32, (1, n), 1)
    first = (lane & (2 * half - 1)) < half
    return jnp.where(first, pltpu.roll(x, n - half, 1), pltpu.roll(x, half, 1))


def _rope(x, cos, sin_signed, half):
    return x * cos + _swap_halves(x, half) * sin_signed


def _rope_bwd(dy, cos, sin_signed, half):
    return dy * cos + _swap_halves(dy * sin_signed, half)


def _pe_lane_mask(n):
    lane = lax.broadcasted_iota(I32, (1, n), 1) & (LANE - 1)
    return (lane >= KPE_OFF) & (lane < KPE_OFF + ROPE)


def _attn_prep(q_raw, kv_raw, proj, tab, gains, consts):
    s = q_raw.shape[0]
    hw = HEADS * LANE

    def body(q_ref, kv_ref, kpe_ref, qd_ref, kd_ref, vd_ref, tab_ref,
             gq_ref, gk_ref, gkpe_ref, gdq_ref, gdk_ref,
             segq_ref, expq_ref, invq_ref, segk_ref, expk_ref, invk_ref, segd_ref, expd_ref, invd_ref,
             qm_ref, km_ref, vm_ref, qdo_ref, kdo_ref, vdo_ref):
        tab_v = tab_ref[...]
        cos_d, sin_d = _tile_lanes(tab_v[:, 0:LANE], DIL_W // LANE), _tile_lanes(tab_v[:, LANE:2 * LANE], DIL_W // LANE)
        cos_q1, sin_q1 = tab_v[:, 2 * LANE:3 * LANE], tab_v[:, 3 * LANE:4 * LANE]
        cos_q, sin_q = _tile_lanes(cos_q1, HEADS), _tile_lanes(sin_q1, HEADS)

        q = q_ref[...]
        qn = q * _seg_rinv(q, segq_ref[...], expq_ref[...], invq_ref[...]) * gq_ref[...]
        qm_ref[...] = _rope(qn, cos_q, sin_q, ROPE // 2).astype(BF16)

        kv = kv_ref[...]
        kp = kv[:, :hw]
        kn = kp * _seg_rinv(kp, segk_ref[...], expk_ref[...], invk_ref[...]) * gk_ref[...]
        kpe = kpe_ref[...]
        r_pe = lax.rsqrt(jnp.sum(kpe * kpe, axis=-1, keepdims=True) * (1.0 / ROPE) + EPS)
        kpe_r = _rope(kpe * r_pe * gkpe_ref[...], cos_q1, sin_q1, ROPE // 2)
        km_ref[...] = (kn + _tile_lanes(kpe_r, HEADS)).astype(BF16)
        vm_ref[...] = kv[:, hw:].astype(BF16)

        qd = qd_ref[...]
        qdn = qd * _seg_rinv(qd, segd_ref[...], expd_ref[...], invd_ref[...]) * gdq_ref[...]
        qdo_ref[...] = _rope(qdn, cos_d, sin_d, DIL_DIM // 2).astype(BF16)
        kd = kd_ref[...]
        kdn = kd * _seg_rinv(kd, segd_ref[...], expd_ref[...], invd_ref[...]) * gdk_ref[...]
        kdo_ref[...] = _rope(kdn, cos_d, sin_d, DIL_DIM // 2).astype(BF16)
        vdo_ref[...] = vd_ref[...].astype(BF16)

    t = ROW_TILE
    row = lambda w, cb=0: pl.BlockSpec((t, w), lambda i: (i, cb))
    c = consts
    return pl.pallas_call(
        body, name="attn_prep", grid=(s // t,),
        in_specs=[row(hw), row(hw + DIL_W), row(LANE, P_KPE // LANE), row(DIL_W, P_QD // DIL_W), row(DIL_W, P_KD // DIL_W),
                  row(DIL_W, P_VD // DIL_W), row(4 * LANE),
                  _full((1, hw)), _full((1, hw)), _full((1, LANE)), _full((1, DIL_W)), _full((1, DIL_W)),
                  _full((hw, LANE)), _full((LANE, hw)), _full((1, LANE)), _full((hw, LANE)), _full((LANE, hw)), _full((1, LANE)),
                  _full((DIL_W, LANE)), _full((LANE, DIL_W)), _full((1, LANE))],
        out_specs=[row(hw), row(hw), row(DIL_W), row(DIL_W), row(DIL_W), row(DIL_W)],
        out_shape=[jax.ShapeDtypeStruct((s, hw), BF16), jax.ShapeDtypeStruct((s, hw), BF16)]
        + [jax.ShapeDtypeStruct((s, DIL_W), BF16)] * 4,
        compiler_params=_params(("parallel",), 24 << 20),
    )(*_in_hbm(q_raw, kv_raw, proj, proj, proj, proj, tab), gains["q"], gains["k"], gains["kpe"], gains["dq"], gains["dk"],
      c["seg_q"], c["exp_q"], c["inv_q"], c["seg_k"], c["exp_k"], c["inv_k"], c["seg_d"], c["exp_d"], c["inv_d"])


def _attn_prep_bwd(dqm, dkm, dvm, dqd, dkd, dvd, q_raw, kv_raw, proj, tab, gains, consts):
    s = q_raw.shape[0]
    hw = HEADS * LANE
    n_steps = s // ROW_TILE

    def body(dqm_ref, dkm_ref, dvm_ref, dqd_ref, dkd_ref, dvd_ref, q_ref, kv_ref, kpe_ref, qd_ref, kd_ref, tab_ref,
             gq_ref, gk_ref, gkpe_ref, gdq_ref, gdk_ref,
             segq_ref, expq_ref, invq_ref, segk_ref, expk_ref, invk_ref, segd_ref, expd_ref, invd_ref, foldq_ref, foldd_ref,
             dq_ref, dkv_ref, dkpe_ref, dqdo_ref, dkdo_ref, dvdo_ref, dg_ref, acc_ref):
        i = pl.program_id(0)

        @pl.when(i == 0)
        def _():
            acc_ref[...] = jnp.zeros_like(acc_ref)

        tab_v = tab_ref[...]
        cos_d, sin_d = _tile_lanes(tab_v[:, 0:LANE], DIL_W // LANE), _tile_lanes(tab_v[:, LANE:2 * LANE], DIL_W // LANE)
        cos_q1, sin_q1 = tab_v[:, 2 * LANE:3 * LANE], tab_v[:, 3 * LANE:4 * LANE]
        cos_q, sin_q = _tile_lanes(cos_q1, HEADS), _tile_lanes(sin_q1, HEADS)

        def norm_bwd(x, dyg, gain, seg, exp, inv):
            rinv = _seg_rinv(x, seg, exp, inv)
            xn = x * rinv
            dxn = dyg * gain
            dx = rinv * (dxn - xn * _seg_mean(dxn * xn, seg, exp, inv))
            return dx, jnp.sum(dyg * xn, axis=0, keepdims=True)

        dq, gq_l = norm_bwd(q_ref[...], _rope_bwd(dqm_ref[...], cos_q, sin_q, ROPE // 2), gq_ref[...],
                            segq_ref[...], expq_ref[...], invq_ref[...])
        dq_ref[...] = dq.astype(BF16)

        dkm = dkm_ref[...]
        kv = kv_ref[...]
        dkp, gk_l = norm_bwd(kv[:, :hw], dkm, gk_ref[...], segk_ref[...], expk_ref[...], invk_ref[...])
        dkv_ref[:, :hw] = dkp.astype(BF16)
        dkv_ref[:, hw:] = dvm_ref[...].astype(BF16)

        dkpe_r = dkm[:, 0:LANE]
        for h in range(1, HEADS):
            dkpe_r = dkpe_r + dkm[:, h * LANE:(h + 1) * LANE]
        dkpe_r = jnp.where(_pe_lane_mask(LANE), dkpe_r, 0.0)
        dyg = _rope_bwd(dkpe_r, cos_q1, sin_q1, ROPE // 2)
        kpe = kpe_ref[...]
        r_pe = lax.rsqrt(jnp.sum(kpe * kpe, axis=-1, keepdims=True) * (1.0 / ROPE) + EPS)
        xn = kpe * r_pe
        dxn = dyg * gkpe_ref[...]
        dkpe = r_pe * (dxn - xn * (jnp.sum(dxn * xn, axis=-1, keepdims=True) * (1.0 / ROPE)))
        dkpe_ref[...] = dkpe.astype(BF16)
        gkpe_l = jnp.sum(dyg * xn, axis=0, keepdims=True)

        dqd_v, gdq_l = norm_bwd(qd_ref[...], _rope_bwd(dqd_ref[...], cos_d, sin_d, DIL_DIM // 2), gdq_ref[...],
                                segd_ref[...], expd_ref[...], invd_ref[...])
        dqdo_ref[...] = dqd_v.astype(BF16)
        dkd_v, gdk_l = norm_bwd(kd_ref[...], _rope_bwd(dkd_ref[...], cos_d, sin_d, DIL_DIM // 2), gdk_ref[...],
                                segd_ref[...], expd_ref[...], invd_ref[...])
        dkdo_ref[...] = dkd_v.astype(BF16)
        dvdo_ref[...] = dvd_ref[...].astype(BF16)

        acc_ref[0:1, :] += gq_l
        acc_ref[1:2, :] += gk_l
        acc_ref[2:3, 0:LANE] += gkpe_l
        acc_ref[3:4, 0:DIL_W] += gdq_l
        acc_ref[4:5, 0:DIL_W] += gdk_l

        @pl.when(i == n_steps - 1)
        def _():
            acc = acc_ref[...]
            fq = jnp.dot(acc, foldq_ref[...], precision=HIGHEST, preferred_element_type=F32)
            fd = jnp.dot(acc[:, 0:DIL_W], foldd_ref[...], precision=HIGHEST, preferred_element_type=F32)
            rows = lax.broadcasted_iota(I32, (8, LANE), 0)
            base = jnp.where(rows < 2, fq, jnp.where(rows == 2, acc[:, 0:LANE], fd))
            at0 = pltpu.roll(base, LANE - KPE_OFF, 1)
            dg_ref[...] = jnp.where(rows == 5, pltpu.roll(at0, 5, 0), jnp.where(rows == 2, at0, base))

    t = ROW_TILE
    row = lambda w, cb=0: pl.BlockSpec((t, w), lambda i: (i, cb))
    c = consts
    return pl.pallas_call(
        body, name="attn_prep_bwd", grid=(n_steps,),
        in_specs=[row(hw), row(hw), row(DIL_W), row(DIL_W), row(DIL_W), row(DIL_W),
                  row(hw), row(hw + DIL_W), row(LANE, P_KPE // LANE), row(DIL_W, P_QD // DIL_W), row(DIL_W, P_KD // DIL_W),
                  row(4 * LANE),
                  _full((1, hw)), _full((1, hw)), _full((1, LANE)), _full((1, DIL_W)), _full((1, DIL_W)),
                  _full((hw, LANE)), _full((LANE, hw)), _full((1, LANE)), _full((hw, LANE)), _full((LANE, hw)), _full((1, LANE)),
                  _full((DIL_W, LANE)), _full((LANE, DIL_W)), _full((1, LANE)), _full((hw, LANE)), _full((DIL_W, LANE))],
        out_specs=[row(hw), row(hw + DIL_W), row(LANE), row(DIL_W), row(DIL_W), row(DIL_W), _full((8, LANE))],
        out_shape=[jax.ShapeDtypeStruct((s, hw), BF16), jax.ShapeDtypeStruct((s, hw + DIL_W), BF16),
                   jax.ShapeDtypeStruct((s, LANE), BF16)] + [jax.ShapeDtypeStruct((s, DIL_W), BF16)] * 3
        + [jax.ShapeDtypeStruct((8, LANE), F32)],
        scratch_shapes=[pltpu.VMEM((8, hw), F32)],
        compiler_params=_params(("arbitrary",), 28 << 20),
    )(*_in_hbm(dqm, dkm, dvm, dqd, dkd, dvd, q_raw, kv_raw, proj, proj, proj, tab),
      gains["q"], gains["k"], gains["kpe"], gains["dq"], gains["dk"],
      c["seg_q"], c["exp_q"], c["inv_q"], c["seg_k"], c["exp_k"], c["inv_k"], c["seg_d"], c["exp_d"], c["inv_d"],
      c["fold_q"], c["fold_d"])


def _latnorm_bwd(dql, dkvl, proj, g_q, g_kv):
    s = proj.shape[0]
    n_steps = s // ROW_TILE

    def body(dql_ref, dkvl_ref, q_ref, kv_ref, gq_ref, gkv_ref, dq_ref, dkv_ref, dg_ref):
        i = pl.program_id(0)

        @pl.when(i == 0)
        def _():
            dg_ref[...] = jnp.zeros_like(dg_ref)

        def one(x, dyg, gain):
            r = _rms(x)
            xn = x * r
            dxn = dyg * gain
            dx = r * (dxn - xn * jnp.mean(dxn * xn, axis=-1, keepdims=True))
            return dx, jnp.sum(dyg * xn, axis=0, keepdims=True)

        dq, gq_l = one(q_ref[...], dql_ref[...], gq_ref[...])
        dkv, gkv_l = one(kv_ref[...], dkvl_ref[...], gkv_ref[...])
        dq_ref[...] = dq.astype(BF16)
        dkv_ref[...] = dkv.astype(BF16)
        dg_ref[0:1, :] += gq_l
        dg_ref[1:2, 0:KV_LORA] += gkv_l

    t = ROW_TILE
    return pl.pallas_call(
        body, name="latnorm_bwd", grid=(n_steps,),
        in_specs=[pl.BlockSpec((t, Q_LORA), lambda i: (i, 0)), pl.BlockSpec((t, KV_LORA), lambda i: (i, 0)),
                  pl.BlockSpec((t, Q_LORA), lambda i: (i, P_QLAT // Q_LORA)),
                  pl.BlockSpec((t, KV_LORA), lambda i: (i, P_KVLAT // KV_LORA)),
                  _full((1, Q_LORA)), _full((1, KV_LORA))],
        out_specs=[pl.BlockSpec((t, Q_LORA), lambda i: (i, 0)), pl.BlockSpec((t, KV_LORA), lambda i: (i, 0)), _full((8, Q_LORA))],
        out_shape=[jax.ShapeDtypeStruct((s, Q_LORA), BF16), jax.ShapeDtypeStruct((s, KV_LORA), BF16),
                   jax.ShapeDtypeStruct((8, Q_LORA), F32)],
        compiler_params=_params(("arbitrary",)),
    )(dql, dkvl, proj, proj, g_q, g_kv)


def _resid_prenorm(x, mix, g1, gain, scale, shift):
    s, d = x.shape

    def body(x_ref, mix_ref, g1_ref, g_ref, sc_ref, sh_ref, x1_ref, h_ref):
        x1 = x_ref[...] + g1_ref[...] * mix_ref[...]
        x1_ref[...] = x1
        h_ref[...] = ((x1 * _rms(x1)) * g_ref[...] * (1.0 + sc_ref[...]) + sh_ref[...]).astype(BF16)

    row = pl.BlockSpec((ROW_TILE, d), lambda i: (i, 0))
    vec = _full((1, d))
    return pl.pallas_call(
        body, name="resid_prenorm", grid=(s // ROW_TILE,),
        in_specs=[row, row, vec, vec, vec, vec], out_specs=[row, row],
        out_shape=[jax.ShapeDtypeStruct((s, d), F32), jax.ShapeDtypeStruct((s, d), BF16)],
        compiler_params=_params(("parallel",)),
    )(x, mix, g1, gain, scale, shift)


CONV_TILE = 1408
HALO = 8


def _shift_down(x, halo, k):
    t = x.shape[0]
    row = lax.broadcasted_iota(I32, (t, 1), 0)
    out = pltpu.roll(x, k, 0)
    for r in range(k):
        out = jnp.where(row == r, halo[HALO - k + r:HALO - k + r + 1, :], out)
    return out


def _shift_up(x, halo, k):
    t = x.shape[0]
    row = lax.broadcasted_iota(I32, (t, 1), 0)
    out = pltpu.roll(x, t - k, 0)
    for r in range(k):
        out = jnp.where(row == t - k + r, halo[r:r + 1, :], out)
    return out


def _conv_fwd(x, halo, w, b):
    p1, p2 = _shift_down(x, halo, 1), _shift_down(x, halo, 2)
    u = b + p2 * w[0:1, :]
    u = u + p1 * w[1:2, :]
    u = u + x * w[2:3, :]
    return u, p1, p2


def _sigmoid(x):
    return 1.0 / (1.0 + jnp.exp(-x))


def _conv_gate(up, w_conv, b_conv):
    s = up.shape[0]
    t = ROW_TILE
    nj = D_FF // CONV_TILE
    hb = t // HALO

    def body(g_ref, v_ref, gh_ref, vh_ref, wg_ref, wv_ref, bg_ref, bv_ref, a_ref):
        live = (pl.program_id(0) > 0).astype(F32)
        ug, _, _ = _conv_fwd(g_ref[...], gh_ref[...] * live, wg_ref[...], bg_ref[...])
        uv, _, _ = _conv_fwd(v_ref[...], vh_ref[...] * live, wv_ref[...], bv_ref[...])
        a_ref[...] = (ug * _sigmoid(ug) * uv).astype(BF16)

    main = lambda off: pl.BlockSpec((t, CONV_TILE), lambda i, j: (i, j + off))
    halo = lambda off: pl.BlockSpec((HALO, CONV_TILE), lambda i, j: (jnp.maximum(i * hb - 1, 0), j + off))
    wsp = lambda off: pl.BlockSpec((3, CONV_TILE), lambda i, j: (0, j + off))
    bsp = lambda off: pl.BlockSpec((1, CONV_TILE), lambda i, j: (0, j + off))
    return pl.pallas_call(
        body, name="conv_gate", grid=(s // t, nj),
        in_specs=[main(0), main(nj), halo(0), halo(nj), wsp(0), wsp(nj), bsp(0), bsp(nj)],
        out_specs=pl.BlockSpec((t, CONV_TILE), lambda i, j: (i, j)),
        out_shape=jax.ShapeDtypeStruct((s, D_FF), BF16),
        compiler_params=_params(("parallel", "parallel"), 12 << 20),
    )(up, up, up, up, w_conv, w_conv, b_conv, b_conv)


def _gate_bwd(up, da, w_conv, b_conv):
    s = up.shape[0]
    t = ROW_TILE
    nj = D_FF // CONV_TILE
    hb = t // HALO
    n_i = s // t

    def body(g_ref, v_ref, gh_ref, vh_ref, gn_ref, vn_ref, da_ref, dan_ref, wg_ref, wv_ref, bg_ref, bv_ref,
             dupg_ref, dupv_ref, dbg_ref, dbv_ref, dwg_ref, dwv_ref):
        i = pl.program_id(1)

        @pl.when(i == 0)
        def _():
            for r in (dbg_ref, dbv_ref, dwg_ref, dwv_ref):
                r[...] = jnp.zeros_like(r)

        def d_gate(ug, uv, da_v):
            sg = _sigmoid(ug)
            return da_v * uv * (sg * (1.0 + ug * (1.0 - sg))), da_v * (ug * sg)

        live = (i > 0).astype(F32)
        xg, xv = g_ref[...], v_ref[...]
        wg, wv = wg_ref[...], wv_ref[...]
        ug, g1, g2 = _conv_fwd(xg, gh_ref[...] * live, wg, bg_ref[...])
        uv, v1, v2 = _conv_fwd(xv, vh_ref[...] * live, wv, bv_ref[...])
        dug, duv = d_gate(ug, uv, da_ref[...])

        more = (i < n_i - 1).astype(F32)
        ug_n, _, _ = _conv_fwd(gn_ref[...], xg[t - HALO:, :], wg, bg_ref[...])
        uv_n, _, _ = _conv_fwd(vn_ref[...], xv[t - HALO:, :], wv, bv_ref[...])
        dug_n, duv_n = d_gate(ug_n, uv_n, dan_ref[...] * more)

        def conv_t(du, du_n, w):
            return du * w[2:3, :] + _shift_up(du, du_n, 1) * w[1:2, :] + _shift_up(du, du_n, 2) * w[0:1, :]

        dupg_ref[...] = conv_t(dug, dug_n, wg).astype(BF16)
        dupv_ref[...] = conv_t(duv, duv_n, wv).astype(BF16)
        csum = lambda z: jnp.sum(z, axis=0, keepdims=True)
        dbg_ref[...] += csum(dug)
        dbv_ref[...] += csum(duv)
        dwg_ref[0:1, :] += csum(dug * g2)
        dwg_ref[1:2, :] += csum(dug * g1)
        dwg_ref[2:3, :] += csum(dug * xg)
        dwv_ref[0:1, :] += csum(duv * v2)
        dwv_ref[1:2, :] += csum(duv * v1)
        dwv_ref[2:3, :] += csum(duv * xv)

    last_halo = s // HALO - 1
    main = lambda off: pl.BlockSpec((t, CONV_TILE), lambda j, i: (i, j + off))
    halo = lambda off: pl.BlockSpec((HALO, CONV_TILE), lambda j, i: (jnp.maximum(i * hb - 1, 0), j + off))
    nxt = lambda off: pl.BlockSpec((HALO, CONV_TILE), lambda j, i: (jnp.minimum((i + 1) * hb, last_halo), j + off))
    wsp = lambda off: pl.BlockSpec((3, CONV_TILE), lambda j, i: (0, j + off))
    bsp = lambda off: pl.BlockSpec((1, CONV_TILE), lambda j, i: (0, j + off))
    outs = pl.pallas_call(
        body, name="gate_bwd", grid=(nj, n_i),
        in_specs=[main(0), main(nj), halo(0), halo(nj), nxt(0), nxt(nj), main(0), nxt(0),
                  wsp(0), wsp(nj), bsp(0), bsp(nj)],
        out_specs=[main(0), main(0),
                   pl.BlockSpec((1, CONV_TILE), lambda j, i: (0, j)), pl.BlockSpec((1, CONV_TILE), lambda j, i: (0, j)),
                   pl.BlockSpec((3, CONV_TILE), lambda j, i: (0, j)), pl.BlockSpec((3, CONV_TILE), lambda j, i: (0, j))],
        out_shape=[jax.ShapeDtypeStruct((s, D_FF), BF16), jax.ShapeDtypeStruct((s, D_FF), BF16),
                   jax.ShapeDtypeStruct((1, D_FF), F32), jax.ShapeDtypeStruct((1, D_FF), F32),
                   jax.ShapeDtypeStruct((3, D_FF), F32), jax.ShapeDtypeStruct((3, D_FF), F32)],
        compiler_params=_params(("parallel", "arbitrary"), 24 << 20),
    )(up, up, up, up, up, up, da, da, w_conv, w_conv, b_conv, b_conv)
    return outs


def _final(x1, ffn, tgt, g2):
    s, d = x1.shape
    n_steps = s // ROW_TILE

    def body(x1_ref, f_ref, t_ref, g2_ref, dy_ref, df_ref, dg2_ref, loss_ref, lacc_ref):
        i = pl.program_id(0)

        @pl.when(i == 0)
        def _():
            dg2_ref[...] = jnp.zeros_like(dg2_ref)
            lacc_ref[...] = jnp.zeros_like(lacc_ref)

        f = f_ref[...]
        e = x1_ref[...] + g2_ref[...] * f - t_ref[...]
        dy = e * (1.0 / d)
        dy_ref[...] = dy
        df_ref[...] = (dy * g2_ref[...]).astype(BF16)
        dg2_ref[...] += jnp.sum(dy * f, axis=0, keepdims=True)
        lacc_ref[...] += jnp.sum(e * e, axis=0, keepdims=True)

        @pl.when(i == n_steps - 1)
        def _():
            loss_ref[...] = jnp.sum(lacc_ref[...], axis=1, keepdims=True) * (0.5 / d)

    row = pl.BlockSpec((ROW_TILE, d), lambda i: (i, 0))
    return pl.pallas_call(
        body, name="final", grid=(n_steps,),
        in_specs=[row, row, row, _full((1, d))],
        out_specs=[row, row, _full((1, d)), _full((1, 1))],
        out_shape=[jax.ShapeDtypeStruct((s, d), F32), jax.ShapeDtypeStruct((s, d), BF16),
                   jax.ShapeDtypeStruct((1, d), F32), jax.ShapeDtypeStruct((1, 1), F32)],
        scratch_shapes=[pltpu.VMEM((1, d), F32)],
        compiler_params=_params(("arbitrary",)),
    )(x1, ffn, tgt, g2)


def _ffnnorm_bwd(dh2, x1, dy, mix, gain, scale, g1):
    s, d = x1.shape
    n_steps = s // ROW_TILE

    def body(dh_ref, x_ref, dy_ref, mix_ref, g_ref, sc_ref, g1_ref, dx_ref, dm_ref, acc_ref):
        i = pl.program_id(0)

        @pl.when(i == 0)
        def _():
            acc_ref[...] = jnp.zeros_like(acc_ref)

        dh, x = dh_ref[...], x_ref[...]
        r = _rms(x)
        xn = x * r
        dn = dh * (1.0 + sc_ref[...])
        dxn = dn * g_ref[...]
        dx = dy_ref[...] + r * (dxn - xn * jnp.mean(dxn * xn, axis=-1, keepdims=True))
        dx_ref[...] = dx
        dm_ref[...] = (dx * g1_ref[...]).astype(BF16)
        csum = lambda z: jnp.sum(z, axis=0, keepdims=True)
        acc_ref[0:1, :] += csum(dh)
        acc_ref[1:2, :] += csum(dh * (xn * g_ref[...]))
        acc_ref[2:3, :] += csum(dn * xn)
        acc_ref[3:4, :] += csum(dx * mix_ref[...])

    row = pl.BlockSpec((ROW_TILE, d), lambda i: (i, 0))
    vec = _full((1, d))
    return pl.pallas_call(
        body, name="ffnnorm_bwd", grid=(n_steps,),
        in_specs=[row, row, row, row, vec, vec, vec],
        out_specs=[row, row, _full((8, d))],
        out_shape=[jax.ShapeDtypeStruct((s, d), F32), jax.ShapeDtypeStruct((s, d), BF16), jax.ShapeDtypeStruct((8, d), F32)],
        compiler_params=_params(("arbitrary",)),
    )(dh2, x1, dy, mix, gain, scale, g1)


def _mixnorm_bwd(dh, x, dx1, gain, scale):
    s, d = x.shape
    n_steps = s // ROW_TILE

    def body(dh_ref, x_ref, dx1_ref, g_ref, sc_ref, gx_ref, acc_ref):
        i = pl.program_id(0)

        @pl.when(i == 0)
        def _():
            acc_ref[...] = jnp.zeros_like(acc_ref)

        dh, x = dh_ref[...], x_ref[...]
        r = _rms(x)
        xn = x * r
        dn = dh * (1.0 + sc_ref[...])
        dxn = dn * g_ref[...]
        gx_ref[...] = dx1_ref[...] + r * (dxn - xn * jnp.mean(dxn * xn, axis=-1, keepdims=True))
        csum = lambda z: jnp.sum(z, axis=0, keepdims=True)
        acc_ref[0:1, :] += csum(dh)
        acc_ref[1:2, :] += csum(dh * (xn * g_ref[...]))
        acc_ref[2:3, :] += csum(dn * xn)

    row = pl.BlockSpec((ROW_TILE, d), lambda i: (i, 0))
    vec = _full((1, d))
    return pl.pallas_call(
        body, name="mixnorm_bwd", grid=(n_steps,),
        in_specs=[row, row, row, vec, vec],
        out_specs=[row, _full((8, d))],
        out_shape=[jax.ShapeDtypeStruct((s, d), F32), jax.ShapeDtypeStruct((8, d), F32)],
        compiler_params=_params(("arbitrary",)),
    )(dh, x, dx1, gain, scale)


def _key_count(d, dilated):
    if not dilated:
        return jnp.where(d >= 0, 1.0, 0.0)
    one = lambda cond: jnp.where(cond, 1.0, 0.0)
    cnt = one(d <= 128) + one(((d & 3) == 0) & (d <= 512)) + one((d & 15) == 0)
    return jnp.where(d >= 0, cnt, 0.0)


def _block_kinds(mla):
    return (0, "diag", "none") if mla else (512, "near", "far")


NEAR_OFFSETS = 4


def _scores_t(ka, qa, scale, kind, rel_t, offset, near_tabs=None):
    return _mask_scores(lax.dot_general(ka, qa, NT, preferred_element_type=F32), scale, kind, rel_t, offset, near_tabs)


def _fill_near_tables(bias_ref, cnt_ref, rel_t):
    for idx in range(NEAR_OFFSETS):
        cnt = _key_count(rel_t + (idx - 1) * ATT_TK, True)
        cnt_ref[idx] = cnt
        bias_ref[idx] = jnp.where(cnt > 0.0, 0.0, NEG_INF)


def _mask_scores(products, scale, kind, rel_t, offset, near_tabs=None):
    st = products * (scale * LOG2E)
    cnt = None
    if kind == "diag":
        st = jnp.where(rel_t + offset >= 0, st, NEG_INF)
    elif kind == "far":
        st = jnp.where((rel_t & 15) == 0, st, NEG_INF)
    elif kind == "near":
        bias_ref, cnt_ref = near_tabs
        idx = offset // ATT_TK + 1
        st = st + bias_ref[idx]
        cnt = cnt_ref[idx]
    return st, cnt


def _attn_fwd(q, k, v, mla, scale, name, gather=()):
    s = q.shape[0]
    qw = 2 * LANE if mla else LANE
    tq, tk = ATT_TQ, ATT_TK
    reach, kind_near, kind_far = _block_kinds(mla)
    assert s % tq == 0 and tq % tk == 0 and reach % tk == 0 and (mla or (reach + tq) // tk == NEAR_OFFSETS)
    ng = len(gather)
    last_step = HEADS // 2 - 1

    def body(*refs):
        q_ref, k_ref, v_ref = refs[:3]
        o_ref, lse_ref = refs[3 + ng:5 + ng]
        vt_ref, st_ref = refs[5 + 2 * ng:7 + 2 * ng]
        near_tabs = None if mla else refs[7 + 2 * ng:9 + 2 * ng]
        n_tabs = 0 if mla else 2
        comm = (refs[3:3 + ng], refs[5 + ng:5 + 2 * ng]) + tuple(refs[7 + n_tabs + 2 * ng:])
        if ng:
            @pl.when(pl.program_id(0) == 0)
            def _():
                _Gather(*comm).start()

            @pl.when(pl.program_id(0) == last_step)
            def _():
                _Gather(*comm).forward()

        lane = lax.broadcasted_iota(I32, (1, LANE), 1)
        rel_t = lax.broadcasted_iota(I32, (tk, tq), 1) - lax.broadcasted_iota(I32, (tk, tq), 0)
        if not mla:
            _fill_near_tables(*near_tabs, rel_t)

        def transpose_v(j, carry):
            c0 = pl.multiple_of(j * tk, tk)
            vt_ref[:, pl.ds(c0, tk)] = v_ref[pl.ds(c0, tk), :].astype(F32).T.astype(BF16)
            return carry

        lax.fori_loop(0, s // tk, transpose_v, 0)

        def q_block(qi, carry):
            r0 = pl.multiple_of(qi * tq, tq)
            kcols = [slice(a * LANE, (a + 1) * LANE) if mla else slice(0, LANE) for a in range(2)]
            qas = [q_ref[pl.ds(r0, tq), kcols[a]] for a in range(2)]
            if not mla:
                qas = [jnp.where(lane < DIL_DIM, qas[0], jnp.zeros_like(qas[0])),
                       jnp.where(lane >= DIL_DIM, qas[1], jnp.zeros_like(qas[1]))]

            n_k = (r0 + tq) // tk

            def products(kj):
                c0 = pl.multiple_of(kj * tk, tk)
                return [lax.dot_general(k_ref[pl.ds(c0, tk), kcols[a]], qas[a], NT, preferred_element_type=F32)
                        for a in range(2)]

            for a, pr in enumerate(products(0)):
                st_ref[0, a] = pr

            def k_block(kj, c, kind):
                c0 = pl.multiple_of(kj * tk, tk)
                slot = kj & 1
                ahead = products(jnp.minimum(kj + 1, n_k - 1))
                out = []
                for a in range(2):
                    m, l, acc = c[a]
                    st, cnt = _mask_scores(st_ref[slot, a], scale, kind, rel_t, r0 - c0, near_tabs)
                    st_ref[1 - slot, a] = ahead[a]
                    m_new = jnp.maximum(m, jnp.max(st, axis=0, keepdims=True))
                    alpha = jnp.exp2(m - m_new)
                    p = jnp.exp2(st - m_new)
                    if cnt is not None:
                        p = p * cnt
                    l = alpha * l + jnp.sum(p, axis=0, keepdims=True)
                    vt = vt_ref[a * DIL_DIM:(a + 1) * DIL_DIM, pl.ds(c0, tk)]
                    acc = alpha * acc + jnp.dot(vt, p.astype(BF16), preferred_element_type=F32)
                    out.append((m_new, l, acc))
                return tuple(out)

            one = (jnp.full((1, tq), NEG_INF, F32), jnp.zeros((1, tq), F32), jnp.zeros((DIL_DIM, tq), F32))
            first_near = jnp.maximum((r0 - reach) // tk, 0)
            c = lax.fori_loop(0, first_near, functools.partial(k_block, kind=kind_far), (one, one))
            res = lax.fori_loop(first_near, (r0 + tq) // tk, functools.partial(k_block, kind=kind_near), c)
            o_t = jnp.concatenate([res[a][2] / res[a][1] for a in range(2)], axis=0)
            o_ref[pl.ds(r0, tq), :] = o_t.T.astype(BF16)
            for a in range(2):
                lse_ref[a, :, pl.ds(r0, tq)] = res[a][0] * LN2 + jnp.log(res[a][1])
            return carry

        lax.fori_loop(0, s // tq, q_block, 0)

        if ng:
            @pl.when(pl.program_id(0) == last_step)
            def _():
                _Gather(*comm).finish()

    return pl.pallas_call(
        body, name=name, grid=(HEADS // 2,),
        in_specs=[pl.BlockSpec((s, qw), lambda h: (0, h)), pl.BlockSpec((s, qw), lambda h: (0, h)),
                  pl.BlockSpec((s, LANE), lambda h: (0, h))] + [ANY] * ng,
        out_specs=[pl.BlockSpec((s, LANE), lambda h: (0, h)), pl.BlockSpec((2, 1, s), lambda h: (h, 0, 0))] + [ANY] * ng,
        out_shape=[jax.ShapeDtypeStruct((s, DIL_W), BF16), jax.ShapeDtypeStruct((HEADS, 1, s), F32)] + _Gather.out_shapes(gather),
        scratch_shapes=[pltpu.VMEM((LANE, s), BF16), pltpu.VMEM((2, 2, tk, tq), F32)]
        + ([] if mla else [pltpu.VMEM((NEAR_OFFSETS, tk, tq), F32)] * 2) + (_Gather.semaphores(ng) if ng else []),
        compiler_params=_params(("arbitrary",) if ng else ("parallel",), 12 << 20),
    )(*_in_hbm(q, k, v), *gather)


def _attn_bwd(q, k, v, o, do, do_block0, lse, mla, scale, name, scatter=()):
    s = q.shape[0]
    qw = 2 * LANE if mla else LANE
    tq, tk = ATT_TQ, ATT_TK
    nq = s // tq
    reach, kind_near, kind_far = _block_kinds(mla)
    assert s % tq == 0 and tq % tk == 0
    ns = len(scatter)
    last_step = HEADS // 2 - 1

    def body(*refs):
        q_ref, k_ref, v_ref, o_ref, do_ref, lse_ref = refs[:6]
        dq_ref, dk_ref, dv_ref = refs[6 + ns:9 + ns]
        kt_ref, dot_ref, dob_ref, dqt_ref, delta_ref, lse2_ref = refs[9 + 2 * ns:15 + 2 * ns]
        near_tabs = None if mla else refs[15 + 2 * ns:17 + 2 * ns]
        n_tabs = 0 if mla else 2
        comm = (refs[6:6 + ns], refs[9 + ns:9 + 2 * ns]) + tuple(refs[15 + n_tabs + 2 * ns:])
        if ns:
            @pl.when(pl.program_id(0) == 0)
            def _():
                _Scatter(*comm).start()

        lane = lax.broadcasted_iota(I32, (1, LANE), 1)
        row = lax.broadcasted_iota(I32, (LANE, 1), 0)
        rel_t = lax.broadcasted_iota(I32, (tk, tq), 1) - lax.broadcasted_iota(I32, (tk, tq), 0)
        if not mla:
            _fill_near_tables(*near_tabs, rel_t)

        def prepare(j, carry):
            c0 = pl.multiple_of(j * tk, tk)
            do_blk = do_ref[pl.ds(c0, tk), :]
            dob_ref[pl.ds(c0, tk), :] = do_blk.astype(BF16)
            do_t = do_blk.T
            dot_ref[:, pl.ds(c0, tk)] = do_t.astype(BF16)
            prod = do_t * o_ref[pl.ds(c0, tk), :].astype(F32).T
            delta_ref[0, :, pl.ds(c0, tk)] = jnp.sum(prod[0:DIL_DIM], axis=0, keepdims=True)
            delta_ref[1, :, pl.ds(c0, tk)] = jnp.sum(prod[DIL_DIM:LANE], axis=0, keepdims=True)
            for w in range(qw // LANE):
                kt_ref[w * LANE:(w + 1) * LANE, pl.ds(c0, tk)] = (
                    k_ref[pl.ds(c0, tk), w * LANE:(w + 1) * LANE].astype(F32).T.astype(BF16))
            return carry

        lax.fori_loop(0, s // tk, prepare, 0)
        dqt_ref[...] = jnp.zeros_like(dqt_ref)
        lse2_ref[...] = lse_ref[...] * LOG2E

        sels = [lane < DIL_DIM, lane >= DIL_DIM]
        rsels = [row < DIL_DIM, row >= DIL_DIM]
        cols = [slice(a * LANE, (a + 1) * LANE) if mla else slice(0, LANE) for a in range(2)]

        def k_block(kj, carry):
            c0 = pl.multiple_of(kj * tk, tk)
            kas = [k_ref[pl.ds(c0, tk), cols[a]] for a in range(2)]
            kts = [kt_ref[cols[a], pl.ds(c0, tk)] for a in range(2)]
            if not mla:
                kas = [jnp.where(sels[a], kas[a], jnp.zeros_like(kas[a])) for a in range(2)]
                kts = [jnp.where(rsels[a], kts[a], jnp.zeros_like(kts[a])) for a in range(2)]
            vb = v_ref[pl.ds(c0, tk), :]
            vbs = [jnp.where(sels[a], vb, jnp.zeros_like(vb)) for a in range(2)]

            first = c0 // tq

            def q_block(qi, c, kind):
                r0 = pl.multiple_of(qi * tq, tq)
                out, dq_parts = [], []
                for a in range(2):
                    dk_acc, dv_acc = c[a]
                    qa = q_ref[pl.ds(r0, tq), cols[a]]
                    st, cnt = _scores_t(kas[a], qa, scale, kind, rel_t, r0 - c0, near_tabs)
                    p = jnp.exp2(st - lse2_ref[a, :, pl.ds(r0, tq)])
                    if cnt is not None:
                        p = p * cnt
                    dp = jnp.dot(vbs[a], dot_ref[:, pl.ds(r0, tq)], preferred_element_type=F32)
                    ds = (p * (dp - delta_ref[a, :, pl.ds(r0, tq)]) * scale).astype(BF16)
                    dv_acc = dv_acc + jnp.dot(p.astype(BF16), dob_ref[pl.ds(r0, tq), :], preferred_element_type=F32)
                    dk_acc = dk_acc + jnp.dot(ds, qa, preferred_element_type=F32)
                    dq_parts.append(jnp.dot(kts[a], ds, preferred_element_type=F32))
                    out.append((dk_acc, dv_acc))
                if mla:
                    for a in range(2):
                        dqt_ref[cols[a], pl.ds(r0, tq)] += dq_parts[a]
                else:
                    dqt_ref[:, pl.ds(r0, tq)] += dq_parts[0] + dq_parts[1]
                return tuple(out)

            zero = jnp.zeros((tk, LANE), F32)
            last_near = jnp.minimum((c0 + tk - 1 + reach) // tq + 1, nq)
            c = lax.fori_loop(first, last_near, functools.partial(q_block, kind=kind_near), ((zero, zero), (zero, zero)))
            (dk0, dv0), (dk1, dv1) = lax.fori_loop(last_near, nq, functools.partial(q_block, kind=kind_far), c)
            if mla:
                dk_ref[pl.ds(c0, tk), cols[0]] = dk0
                dk_ref[pl.ds(c0, tk), cols[1]] = dk1
            else:
                dk_ref[pl.ds(c0, tk), :] = jnp.where(sels[0], dk0, dk1)
            dv_ref[pl.ds(c0, tk), :] = jnp.where(sels[0], dv0, dv1)
            return carry

        lax.fori_loop(0, s // tk, k_block, 0)

        def write_dq(j, carry):
            c0 = pl.multiple_of(j * tk, tk)
            for w in range(qw // LANE):
                dq_ref[pl.ds(c0, tk), w * LANE:(w + 1) * LANE] = dqt_ref[w * LANE:(w + 1) * LANE, pl.ds(c0, tk)].T
            return carry

        lax.fori_loop(0, s // tk, write_dq, 0)

        if ns:
            @pl.when(pl.program_id(0) == last_step)
            def _():
                _Scatter(*comm).finish()

    b0 = do_block0
    return pl.pallas_call(
        body, name=name, grid=(HEADS // 2,),
        in_specs=[pl.BlockSpec((s, qw), lambda h: (0, h)), pl.BlockSpec((s, qw), lambda h: (0, h)),
                  pl.BlockSpec((s, LANE), lambda h: (0, h)), pl.BlockSpec((s, LANE), lambda h: (0, h)),
                  pl.BlockSpec((s, LANE), lambda h: (0, h + b0)), pl.BlockSpec((2, 1, s), lambda h: (h, 0, 0))] + [ANY] * ns,
        out_specs=[pl.BlockSpec((s, qw), lambda h: (0, h)), pl.BlockSpec((s, qw), lambda h: (0, h)),
                   pl.BlockSpec((s, LANE), lambda h: (0, h))] + [ANY] * ns,
        out_shape=[jax.ShapeDtypeStruct(q.shape, F32), jax.ShapeDtypeStruct(k.shape, F32), jax.ShapeDtypeStruct((s, DIL_W), F32)]
        + _Scatter.out_shapes(scatter),
        scratch_shapes=[pltpu.VMEM((qw, s), BF16), pltpu.VMEM((LANE, s), BF16), pltpu.VMEM((s, LANE), BF16),
                        pltpu.VMEM((qw, s), F32), pltpu.VMEM((2, 1, s), F32), pltpu.VMEM((2, 1, s), F32)]
        + ([] if mla else [pltpu.VMEM((NEAR_OFFSETS, tk, tq), F32)] * 2) + (_Scatter.semaphores(ns) if ns else []),
        compiler_params=_params(("arbitrary",) if ns else ("parallel",), 24 << 20),
    )(*_in_hbm(q, k, v, o, do, lse), *scatter)


def _ada_bwd(c_all, dmod_shard):
    n, d = c_all.shape
    cols = dmod_shard.shape[1]

    def body(c_ref, g_ref, o_ref):
        cv = c_ref[...]
        o_ref[...] = lax.dot_general(cv * _sigmoid(cv), g_ref[...], TN, precision=HIGHEST, preferred_element_type=F32)

    return pl.pallas_call(
        body, name="ada_bwd", out_shape=jax.ShapeDtypeStruct((d, cols), F32),
        compiler_params=_params(None, 16 << 20),
    )(c_all, dmod_shard)


SMALL_WIDTHS = (("g_mix_norm", D_MODEL), ("g_q_lat", Q_LORA), ("g_kv_lat", KV_LORA), ("g_mla_q_nope", NOPE),
                ("g_mla_q_pe", ROPE), ("g_mla_k_nope", NOPE), ("g_mla_k_pe", ROPE), ("g_dil_q", DIL_DIM),
                ("g_dil_k", DIL_DIM), ("g_ffn_norm", D_MODEL), ("b_conv", UP_W))


def _small_layout():
    pieces = (("dmod", 6 * D_MODEL),) + SMALL_WIDTHS + tuple(("w_conv%d" % k, UP_W) for k in range(3)) + (("loss", 1),)
    layout, off = {}, 0
    for name, width in pieces:
        layout[name] = (width, off)
        off += -(-width // LANE) * LANE
    return layout, off


def _pack_small(acc1, acc2, dg2, dglat, dgains, dbg, dbv, dwg, dwv, loss_part):
    layout, total = _small_layout()

    def body(a1, a2, g2, gl, gg, bg, bv, wg, wv, ls, o_ref):
        o_ref[...] = jnp.zeros_like(o_ref)

        def put(name, src, shift=0):
            start = layout[name][1] + shift
            o_ref[:, start:start + src.shape[1]] = src

        for k, src in enumerate((a1[0:1, :], a1[1:2, :], a2[3:4, :], a2[0:1, :], a2[1:2, :], g2[...])):
            put("dmod", src, k * D_MODEL)
        put("g_mix_norm", a1[2:3, :])
        put("g_q_lat", gl[0:1, :])
        put("g_kv_lat", gl[1:2, 0:KV_LORA])
        put("g_mla_q_nope", gg[0:1, 0:NOPE])
        put("g_mla_q_pe", gg[5:6, 0:ROPE])
        put("g_mla_k_nope", gg[1:2, 0:NOPE])
        put("g_mla_k_pe", gg[2:3, 0:ROPE])
        put("g_dil_q", gg[3:4, 0:DIL_DIM])
        put("g_dil_k", gg[4:5, 0:DIL_DIM])
        put("g_ffn_norm", a2[2:3, :])
        put("b_conv", bg[...])
        put("b_conv", bv[...], D_FF)
        for k in range(3):
            put("w_conv%d" % k, wg[k:k + 1, :])
            put("w_conv%d" % k, wv[k:k + 1, :], D_FF)
        put("loss", ls[...])

    ins = (acc1, acc2, dg2, dglat, dgains, dbg, dbv, dwg, dwv, loss_part)
    return pl.pallas_call(
        body, name="pack_small", grid=(1,), in_specs=[_full(a.shape) for a in ins], out_specs=_full((1, total)),
        out_shape=jax.ShapeDtypeStruct((1, total), F32),
        compiler_params=_params(("arbitrary",), 2 << 20),
    )(*_in_hbm(*ins))


def _sum_unpack(g):
    n_dev, _, total = g.shape
    layout, _ = _small_layout()

    def body(g_ref, *refs):
        o_refs, s_ref = refs[:-1], refs[-1]
        acc = g_ref[0]
        for k in range(1, n_dev):
            acc = acc + g_ref[k]
        s_ref[...] = acc
        take = lambda name: s_ref[:, layout[name][1]:layout[name][1] + layout[name][0]]
        o_refs[0][...] = take("dmod")
        for i, (name, _) in enumerate(SMALL_WIDTHS):
            o_refs[1 + i][...] = take(name)
        for k in range(3):
            o_refs[-2][k:k + 1, :] = take("w_conv%d" % k)
        o_refs[-1][...] = take("loss")

    shapes = [(1, 6 * D_MODEL)] + [(1, w) for _, w in SMALL_WIDTHS] + [(3, UP_W), (1, 1)]
    return pl.pallas_call(
        body, name="sum_unpack", out_shape=[jax.ShapeDtypeStruct(sh, F32) for sh in shapes],
        scratch_shapes=[pltpu.VMEM((1, total), F32)],
        compiler_params=_params(None, 4 << 20),
    )(g)


def _adamw_math(w, g, m, v):
    mn = ADAM_B1 * m + (1.0 - ADAM_B1) * g
    vn = ADAM_B2 * v + (1.0 - ADAM_B2) * (g * g)
    m_hat = mn / (1.0 - ADAM_B1 ** ADAM_STEP)
    v_hat = vn / (1.0 - ADAM_B2 ** ADAM_STEP)
    return -ADAM_LR * (m_hat / (jnp.sqrt(v_hat) + ADAM_EPS) + ADAM_WD * w), mn, vn


def _adamw_vectors(ws, gs, ms, vs):
    k = len(ws)

    def body(*refs):
        for i in range(k):
            d, mn, vn = _adamw_math(refs[i][...], refs[k + i][...], refs[2 * k + i][...], refs[3 * k + i][...])
            refs[4 * k + i][...] = d
            refs[5 * k + i][...] = mn
            refs[6 * k + i][...] = vn

    blocks = [_full(w.shape) for w in ws]
    outs = pl.pallas_call(
        body, name="adamw_vectors", grid=(1,), in_specs=blocks * 4, out_specs=blocks * 3,
        out_shape=[jax.ShapeDtypeStruct(w.shape, F32) for w in ws] * 3,
        compiler_params=_params(("arbitrary",), 2 << 20),
    )(*_in_hbm(*ws, *gs, *ms, *vs))
    return outs[:k], outs[k:2 * k], outs[2 * k:]


def _adamw(w, g, m, v, name):
    r, c = w.shape
    tr = r
    for cand in (256, 128, 64, 32, 16, 8):
        if r % cand == 0 and r > cand:
            tr = cand
            break

    def body(w_ref, g_ref, m_ref, v_ref, d_ref, mo_ref, vo_ref):
        d_ref[...], mo_ref[...], vo_ref[...] = _adamw_math(w_ref[...], g_ref[...], m_ref[...], v_ref[...])

    blk = pl.BlockSpec((tr, c), lambda i: (i, 0))
    return pl.pallas_call(
        body, name=name, grid=(r // tr,), in_specs=[blk] * 4, out_specs=[blk] * 3,
        out_shape=[jax.ShapeDtypeStruct((r, c), F32)] * 3,
        compiler_params=_params(("parallel",), 7 * _nbytes((tr, c), F32)),
    )(w, g, m, v)


def _position():
    return lax.axis_index("x"), lax.axis_index("y"), lax.axis_index("c")


def _other_chips(x, y):
    return [(1 - x, y, 2 * (1 - x) + y), (x, 1 - y, 2 * x + (1 - y)), (1 - x, 1 - y, 2 * (1 - x) + (1 - y))]


class _SmallGather:
    def __init__(self, v_ref, out_ref, send_sems, recv_sems, local_sem):
        x, y, c = _position()
        me = 4 * x + 2 * y + c
        self.local = pltpu.make_async_copy(v_ref, out_ref.at[me], local_sem)
        self.sends, self.arrivals = [], []
        for k in range(N_DEV - 1):
            fx, fy, fc = ((k + 1) >> 2) & 1, ((k + 1) >> 1) & 1, (k + 1) & 1
            px, py, pc = (1 - x if fx else x), (1 - y if fy else y), (1 - c if fc else c)

            def copy(dst, k=k, peer=(px, py, pc)):
                return pltpu.make_async_remote_copy(src_ref=v_ref, dst_ref=dst, send_sem=send_sems.at[k],
                                                    recv_sem=recv_sems.at[k], device_id=peer, device_id_type=MESH)

            self.sends.append(copy(out_ref.at[me]))
            self.arrivals.append(copy(out_ref.at[4 * px + 2 * py + pc]))

    @staticmethod
    def semaphores():
        return [pltpu.SemaphoreType.DMA((N_DEV - 1,)), pltpu.SemaphoreType.DMA((N_DEV - 1,)), pltpu.SemaphoreType.DMA]

    def start(self):
        self.local.start()
        for cp in self.sends:
            cp.start()

    def finish(self):
        for cp in self.arrivals:
            cp.wait_recv()
        for cp in self.sends:
            cp.wait_send()
        self.local.wait()


def _prologue(c_taps, w_ada_shard, b_shard, pos_col, rope_consts, shards):
    n = len(shards)
    s = pos_col.shape[0]
    cols = w_ada_shard.shape[1]
    freq, csel, ssel = rope_consts

    def body(*refs):
        ct_ref, w_ref, b_ref, p_ref, f_ref, cs_ref, ss_ref = refs[:7]
        sh_refs = refs[7:7 + n]
        ct_all_ref, mod_all_ref, tab_ref = refs[7 + n:10 + n]
        g_refs = refs[10 + n:10 + 2 * n]
        mod_blk_ref = refs[10 + 2 * n]
        sems = refs[11 + 2 * n:]
        weights = _Gather(sh_refs, g_refs, *sems[6:8])
        weights.start()
        first = _SmallGather(ct_ref, ct_all_ref, *sems[0:3])
        first.start()
        first.finish()
        cv = ct_all_ref[:, 0, 0:D_MODEL]
        sc = (cv * _sigmoid(cv)).astype(BF16)
        mod_blk_ref[...] = jnp.dot(sc, w_ref[...].astype(BF16), preferred_element_type=F32) + b_ref[...]
        second = _SmallGather(mod_blk_ref, mod_all_ref, *sems[3:6])
        second.start()

        def table_rows(i, carry):
            r0 = pl.multiple_of(i * ROW_TILE, ROW_TILE)
            ang = p_ref[pl.ds(r0, ROW_TILE), :].astype(F32) * f_ref[...]
            tab_ref[pl.ds(r0, ROW_TILE), :] = cs_ref[...] * jnp.cos(ang) + ss_ref[...] * jnp.sin(ang)
            return carry

        lax.fori_loop(0, s // ROW_TILE, table_rows, 0)
        second.finish()
        weights.forward()
        weights.finish()

    return pl.pallas_call(
        body, name="prologue",
        out_shape=[jax.ShapeDtypeStruct((N_DEV,) + c_taps.shape, F32), jax.ShapeDtypeStruct((N_DEV, N_DEV, cols), F32),
                   jax.ShapeDtypeStruct((s, 4 * LANE), F32)] + _Gather.out_shapes(shards),
        in_specs=[IN_VMEM] * 7 + [ANY] * n, out_specs=[IN_VMEM] * 3 + [ANY] * n,
        scratch_shapes=[pltpu.VMEM((N_DEV, cols), F32)] + _SmallGather.semaphores() * 2 + _Gather.semaphores(n),
        compiler_params=_params(None, 14 << 20),
    )(c_taps, w_ada_shard, b_shard, pos_col, freq, csel, ssel, *shards)


IN_VMEM = pl.BlockSpec(memory_space=pltpu.VMEM)
ANY = pl.BlockSpec(memory_space=pl.ANY)


class _Gather:
    def __init__(self, w_refs, out_refs, send_sems, recv_sems):
        x, y, c = _position()
        q0 = 2 * x + y
        sibling = (x, y, 1 - c)
        self.ici, self.ici_in, self.fwd, self.fwd_in = [], [], [], []
        for k, (w_ref, out_ref) in enumerate(zip(w_refs, out_refs)):
            half = w_ref.shape[0] // 2

            def blk(q, e, out_ref=out_ref, half=half):
                return out_ref.at[q, pl.ds(pl.multiple_of(e * half, 16), half), :]

            def copy(src, dst, i, to):
                return pltpu.make_async_remote_copy(src_ref=src, dst_ref=dst, send_sem=send_sems.at[i], recv_sem=recv_sems.at[i],
                                                    device_id=to, device_id_type=MESH)

            src = w_ref.at[pl.ds(pl.multiple_of(c * half, 16), half), :]
            for j, (cx, cy, qj) in enumerate(_other_chips(x, y)):
                self.ici.append(copy(src, blk(q0, c), 6 * k + j, (cx, cy, c)))
                self.ici_in.append(copy(blk(qj, c), blk(qj, c), 6 * k + j, (cx, cy, c)))
                self.fwd.append(copy(blk(qj, c), blk(qj, c), 6 * k + 3 + j, sibling))
                self.fwd_in.append(copy(blk(qj, 1 - c), blk(qj, 1 - c), 6 * k + 3 + j, sibling))

    @staticmethod
    def out_shapes(shards):
        return [jax.ShapeDtypeStruct((N_CHIP,) + s.shape, s.dtype) for s in shards]

    @staticmethod
    def semaphores(n):
        return [pltpu.SemaphoreType.DMA((6 * n,)), pltpu.SemaphoreType.DMA((6 * n,))]

    def start(self):
        for cp in self.ici:
            cp.start()

    def forward(self):
        for arrived, onward in zip(self.ici_in, self.fwd):
            arrived.wait_recv()
            onward.start()

    def finish(self):
        for cp in self.fwd_in:
            cp.wait_recv()
        for cp in self.ici + self.fwd:
            cp.wait_send()


def _swap_halves_d2d(grads, name):
    n = len(grads)

    def body(*refs):
        swap = _PairSwap(refs[:n], refs[n:2 * n], *refs[2 * n:])
        swap.start()
        swap.finish()

    return pl.pallas_call(
        body, name=name,
        out_shape=_PairSwap.out_shapes(grads), in_specs=[ANY] * n, out_specs=[ANY] * n,
        scratch_shapes=_PairSwap.semaphores(n),
    )(*grads)


class _PairSwap:
    def __init__(self, g_refs, out_refs, send_sems, recv_sems):
        x, y, c = _position()
        self.copies = [
            pltpu.make_async_remote_copy(src_ref=g_ref.at[:, 1 - c], dst_ref=out_ref, send_sem=send_sems.at[k],
                                         recv_sem=recv_sems.at[k], device_id=(x, y, 1 - c), device_id_type=MESH)
            for k, (g_ref, out_ref) in enumerate(zip(g_refs, out_refs))]

    @staticmethod
    def out_shapes(grads):
        return [jax.ShapeDtypeStruct((N_CHIP,) + g.shape[2:], g.dtype) for g in grads]

    @staticmethod
    def semaphores(n):
        return [pltpu.SemaphoreType.DMA((n,)), pltpu.SemaphoreType.DMA((n,))]

    def start(self):
        for cp in self.copies:
            cp.start()

    def finish(self):
        for cp in self.copies:
            cp.wait_recv()
        for cp in self.copies:
            cp.wait_send()


def _pair_sum(g, a, c_idx, name):
    _, _, rh, cols = g.shape
    tr = rh
    for cand in (256, 128, 64, 32, 16):
        if rh % cand == 0 and rh > cand:
            tr = cand
            break

    def body(c_ref, g_ref, a_ref, o_ref):
        o_ref[...] = (g_ref[...] + a_ref[...]).astype(BF16)

    return pl.pallas_call(
        body, name=name,
        grid_spec=pltpu.PrefetchScalarGridSpec(
            num_scalar_prefetch=1, grid=(N_CHIP, rh // tr),
            in_specs=[pl.BlockSpec((None, None, tr, cols), lambda q, i, c_ref: (q, c_ref[0], i, 0)),
                      pl.BlockSpec((None, tr, cols), lambda q, i, c_ref: (q, i, 0))],
            out_specs=pl.BlockSpec((None, tr, cols), lambda q, i, c_ref: (q, i, 0))),
        out_shape=jax.ShapeDtypeStruct((N_CHIP, rh, cols), BF16),
        compiler_params=_params(("parallel", "parallel"), 10 * _nbytes((tr, cols), F32)),
    )(c_idx, g, a)


def _scatter_and_gather(parts, small, name):
    n = len(parts)

    def body(*refs):
        scatter = _Scatter(refs[:n], refs[n + 1:2 * n + 1], *refs[2 * n + 2:2 * n + 4])
        gather = _SmallGather(refs[n], refs[2 * n + 1], *refs[2 * n + 4:])
        scatter.start()
        gather.start()
        gather.finish()
        scatter.finish()

    return pl.pallas_call(
        body, name=name,
        out_shape=_Scatter.out_shapes(parts) + [jax.ShapeDtypeStruct((N_DEV,) + small.shape, F32)],
        in_specs=[ANY] * n + [IN_VMEM], out_specs=[ANY] * n + [IN_VMEM],
        scratch_shapes=_Scatter.semaphores(n) + _SmallGather.semaphores(),
        compiler_params=_params(None, 10 * _nbytes(small.shape, F32)),
    )(*parts, small)


class _Scatter:
    def __init__(self, p_refs, out_refs, send_sems, recv_sems):
        x, y, c = _position()
        self.copies = []
        for k, (p_ref, out_ref) in enumerate(zip(p_refs, out_refs)):
            for j, (cx, cy, qj) in enumerate(_other_chips(x, y)):
                self.copies.append(pltpu.make_async_remote_copy(
                    src_ref=p_ref.at[qj], dst_ref=out_ref.at[j], send_sem=send_sems.at[3 * k + j],
                    recv_sem=recv_sems.at[3 * k + j], device_id=(cx, cy, c), device_id_type=MESH))

    @staticmethod
    def out_shapes(parts):
        return [jax.ShapeDtypeStruct((3,) + p.shape[1:], p.dtype) for p in parts]

    @staticmethod
    def semaphores(n):
        return [pltpu.SemaphoreType.DMA((3 * n,)), pltpu.SemaphoreType.DMA((3 * n,))]

    def start(self):
        for cp in self.copies:
            cp.start()

    def finish(self):
        for cp in self.copies:
            cp.wait_recv()
        for cp in self.copies:
            cp.wait_send()


def _shard_sum(p, b, q_idx, name):
    _, rh, cols = p.shape
    tr = rh
    for cand in (256, 128, 64, 32, 16):
        if rh % cand == 0 and rh > cand:
            tr = cand
            break

    def body(q_ref, p_ref, b_ref, o_ref):
        acc = p_ref[...].astype(F32)
        for j in range(3):
            acc = acc + b_ref[j].astype(F32)
        o_ref[...] = acc

    return pl.pallas_call(
        body, name=name,
        grid_spec=pltpu.PrefetchScalarGridSpec(
            num_scalar_prefetch=1, grid=(rh // tr,),
            in_specs=[pl.BlockSpec((None, tr, cols), lambda i, q_ref: (q_ref[0], i, 0)),
                      pl.BlockSpec((3, tr, cols), lambda i, q_ref: (0, i, 0))],
            out_specs=pl.BlockSpec((tr, cols), lambda i, q_ref: (i, 0))),
        out_shape=jax.ShapeDtypeStruct((rh, cols), F32),
        compiler_params=_params(("parallel",), 8 * _nbytes((tr, cols), F32)),
    )(q_idx, p, b)


def _join_halves(halves):
    n = len(halves)

    def body(*refs):
        h_refs, out_refs = refs[:n], refs[n:2 * n]
        send_sems, recv_sems = refs[2 * n:]
        x, y, c = _position()
        sibling = (x, y, 1 - c)
        cps = []
        for k in range(n):
            cp = pltpu.make_async_remote_copy(src_ref=h_refs[k], dst_ref=out_refs[k], send_sem=send_sems.at[k],
                                              recv_sem=recv_sems.at[k], device_id=sibling, device_id_type=MESH)
            cp.start()
            cps.append(cp)
        for cp in cps:
            cp.wait_recv()
        for cp in cps:
            cp.wait_send()

    return pl.pallas_call(
        body, name="rs_join",
        out_shape=[jax.ShapeDtypeStruct(h.shape, h.dtype) for h in halves],
        in_specs=[ANY] * n, out_specs=[ANY] * n,
        scratch_shapes=[pltpu.SemaphoreType.DMA((n,)), pltpu.SemaphoreType.DMA((n,))],
    )(*halves)


def _cols_from_shards(g):
    q, r, cs = g.shape
    return jnp.transpose(g, (1, 0, 2)).reshape(r, q * cs)


def _cols_to_shards(w):
    r, cfull = w.shape
    return jnp.transpose(w.reshape(r, N_CHIP, cfull // N_CHIP), (1, 0, 2))


def _pad_w_in(w):
    z = lambda n: jnp.zeros((w.shape[0], n), w.dtype)
    q_lat, kv_lat, kpe = w[:, 0:512], w[:, 512:768], w[:, 768:800]
    qd, kd, vd = w[:, 800:1312], w[:, 1312:1824], w[:, 1824:2336]
    return jnp.concatenate([q_lat, qd, kd, vd, kv_lat, z(KPE_OFF), kpe, z(LANE - KPE_OFF - ROPE)], axis=1)


def _unpad_w_in(g):
    return jnp.concatenate([g[:, P_QLAT:P_QLAT + Q_LORA], g[:, P_KVLAT:P_KVLAT + KV_LORA],
                            g[:, P_KPE + KPE_OFF:P_KPE + KPE_OFF + ROPE], g[:, P_QD:P_QD + 3 * DIL_W]], axis=1)


def _pad_w_qb(w):
    w3 = w.reshape(Q_LORA, HEADS, NOPE + ROPE)
    return jnp.pad(w3, ((0, 0), (0, 0), (0, LANE - NOPE - ROPE))).reshape(Q_LORA, HEADS * LANE)


def _unpad_w_qb(g):
    return g.reshape(Q_LORA, HEADS, LANE)[:, :, :NOPE + ROPE].reshape(Q_LORA, HEADS * (NOPE + ROPE))


def _pad_w_kvb(w):
    w3 = w.reshape(KV_LORA, HEADS, 2 * NOPE)
    kp = jnp.pad(w3[:, :, :NOPE], ((0, 0), (0, 0), (0, LANE - NOPE))).reshape(KV_LORA, HEADS * LANE)
    return jnp.concatenate([kp, w3[:, :, NOPE:].reshape(KV_LORA, DIL_W)], axis=1)


def _unpad_w_kvb(g):
    gk = g[:, :HEADS * LANE].reshape(KV_LORA, HEADS, LANE)[:, :, :NOPE]
    gv = g[:, HEADS * LANE:].reshape(KV_LORA, HEADS, NOPE)
    return jnp.concatenate([gk, gv], axis=2).reshape(KV_LORA, HEADS * 2 * NOPE)


def _head_gains(g_q_nope, g_q_pe, g_k_nope, g_k_pe, g_dq, g_dk):
    z = lambda n: jnp.zeros((1, n), F32)
    q1 = jnp.concatenate([g_q_nope, g_q_pe, z(LANE - NOPE - ROPE)], axis=1)
    k1 = jnp.concatenate([g_k_nope, z(LANE - NOPE)], axis=1)
    kpe = jnp.concatenate([z(KPE_OFF), g_k_pe, z(LANE - KPE_OFF - ROPE)], axis=1)
    return dict(q=jnp.tile(q1, (1, HEADS)), k=jnp.tile(k1, (1, HEADS)), kpe=kpe,
                dq=jnp.tile(g_dq, (1, HEADS)), dk=jnp.tile(g_dk, (1, HEADS)))


def kernel(x, c, positions, w_ada, b_ada, g_mix_norm, w_in, g_q_lat, w_q_b, g_kv_lat, w_kv_b, g_mla_q_nope, g_mla_q_pe, g_mla_k_nope, g_mla_k_pe, g_dil_q, g_dil_k, w_o, g_ffn_norm, w_up, w_conv, b_conv, w_down, loss_target, m_w_ada, m_b_ada, m_g_mix_norm, m_w_in, m_g_q_lat, m_w_q_b, m_g_kv_lat, m_w_kv_b, m_g_mla_q_nope, m_g_mla_q_pe, m_g_mla_k_nope, m_g_mla_k_pe, m_g_dil_q, m_g_dil_k, m_w_o, m_g_ffn_norm, m_w_up, m_w_conv, m_b_conv, m_w_down, v_w_ada, v_b_ada, v_g_mix_norm, v_w_in, v_g_q_lat, v_w_q_b, v_g_kv_lat, v_w_kv_b, v_g_mla_q_nope, v_g_mla_q_pe, v_g_mla_k_nope, v_g_mla_k_pe, v_g_dil_q, v_g_dil_k, v_w_o, v_g_ffn_norm, v_w_up, v_w_conv, v_b_conv, v_w_down):
    args = dict(locals())
    weights = {n: args[n][0] for n in ("w_ada", "w_in", "w_q_b", "w_kv_b", "w_o", "w_up", "w_conv", "w_down")}
    small_w = {n: args[n] for n in ("b_ada",) + tuple(n for n, _ in SMALL_WIDTHS)}
    mom_m = {n[2:]: (args[n][0] if args[n].ndim == 3 else args[n]) for n in args if n.startswith("m_")}
    mom_v = {n[2:]: (args[n][0] if args[n].ndim == 3 else args[n]) for n in args if n.startswith("v_")}

    xi, yi, ci = _position()
    q0 = 2 * xi + yi
    me = 4 * xi + 2 * yi + ci
    xs, tgt = x[0], loss_target[0]
    s = xs.shape[0]
    consts = _seg_consts()
    c_idx, q_idx = jnp.reshape(ci, (1,)).astype(I32), jnp.reshape(q0, (1,)).astype(I32)

    def halves(g4):
        q, r, cc = g4.shape
        return g4.reshape(q, 2, r // 2, cc)

    place_own = lambda gs, ws: [lax.dynamic_update_slice_in_dim(g, w[None], q0, axis=0) for g, w in zip(gs, ws)]
    own_first = [weights[n].astype(BF16) for n in ("w_in", "w_q_b", "w_kv_b")]
    own_later = [weights[n].astype(BF16) for n in ("w_o", "w_up", "w_down")]
    conv_cols = UP_W // N_CHIP
    ada_cols = w_ada.shape[2]
    b_shard = lax.dynamic_slice_in_dim(b_ada, q0 * ada_cols, ada_cols, axis=1)
    c_taps = jnp.concatenate([c, weights["w_conv"].reshape(1, 3 * conv_cols)], axis=1)
    c_taps_all, mod_all, tab, *gathered = _prologue(c_taps, weights["w_ada"], b_shard, positions.reshape(s, 1),
                                                    _rope_consts(), own_first)
    c_all = c_taps_all[:, 0, :D_MODEL]
    w_conv_f = c_taps_all[:, 0, D_MODEL:].reshape(N_CHIP, 2, 3, conv_cols)[:, 0]
    w_conv_f = jnp.transpose(w_conv_f, (1, 0, 2)).reshape(3, UP_W)
    mod_all = mod_all.reshape(N_CHIP, 2, N_DEV, ada_cols)
    mod = lax.dynamic_index_in_dim(lax.dynamic_index_in_dim(mod_all, ci, 1, False), me, 1, False)
    mod = mod.reshape(1, N_CHIP * ada_cols)
    sh1, sc1, g1, sh2, sc2, g2 = [mod[:, k * D_MODEL:(k + 1) * D_MODEL] for k in range(6)]
    gathered = place_own(gathered, own_first)
    w_in_p = _pad_w_in(_cols_from_shards(gathered[0]))
    w_qb_p = _pad_w_qb(_cols_from_shards(gathered[1]))
    w_kvb_p = _pad_w_kvb(_cols_from_shards(gathered[2]))
    gains = _head_gains(g_mla_q_nope, g_mla_q_pe, g_mla_k_nope, g_mla_k_pe, g_dil_q, g_dil_k)

    h = _prenorm(xs, g_mix_norm, sc1, sh1, "prenorm")
    proj = _mm(h, w_in_p, "nn", F32, 512, P_COLS, "mm_in")
    ql, kvl = _latnorm(proj, g_q_lat, g_kv_lat)
    q_raw = _mm(ql, w_qb_p, "nn", F32, 512, HEADS * LANE, "mm_qb")
    kv_raw = _mm(kvl, w_kvb_p, "nn", F32, 512, HEADS * LANE + DIL_W, "mm_kvb")
    qm, km, vm, qd, kd, vd = _attn_prep(q_raw, kv_raw, proj, tab, gains, consts)
    scale_m, scale_d = (NOPE + ROPE) ** -0.5, DIL_DIM ** -0.5
    o_m, lse_m, got_up = _attn_fwd(qm, km, vm, True, scale_m, "attn_mla", gather=own_later[1:2])
    o_d, lse_d, got_o, got_down = _attn_fwd(qd, kd, vd, False, scale_d, "attn_dil", gather=[own_later[0], own_later[2]])
    gathered = place_own([got_o, got_up, got_down], own_later)
    w_o_f = gathered[0].reshape(D_MODEL, D_MODEL)
    w_up_f = _cols_from_shards(gathered[1])
    w_down_f = gathered[2].reshape(D_FF, D_MODEL)
    mix_in = jnp.concatenate([o_m, o_d], axis=1)
    mix = _mm(mix_in, w_o_f, "nn", F32, 512, D_MODEL, "mm_o")
    x1, h2 = _resid_prenorm(xs, mix, g1, g_ffn_norm, sc2, sh2)
    up = _mm(h2, w_up_f, "nn", F32, 512, CONV_TILE, "mm_up")
    act = _conv_gate(up, w_conv_f, b_conv)
    ffn = _mm(act, w_down_f, "nn", F32, 256, D_MODEL, "mm_down")
    dy, dffn, dg2, loss_part = _final(x1, ffn, tgt, g2)

    da = _mm(dffn, w_down_f, "nt", F32, 512, CONV_TILE, "mm_down_dx")
    gw_down = _mm(act, dffn, "tn", F32, 256, D_MODEL, "mm_down_dw")
    dup_g, dup_v, dbg, dbv, dwg, dwv = _gate_bwd(up, da, w_conv_f, b_conv)
    dup = jnp.concatenate([dup_g, dup_v], axis=1)
    early_names = ("w_up", "w_down", "w_o")
    gw_up = _mm(h2, dup, "tn", F32, 512, CONV_TILE, "mm_up_dw", col_shards=True)
    early = [halves(gw_up), halves(gw_down.reshape(N_CHIP, D_FF // N_CHIP, D_MODEL))]
    dh2, *early_sib = _mm(dup, w_up_f, "nt", F32, 256, 512, "mm_up_dx", swap=early, b_outer=True)
    dx1, dmix, acc2 = _ffnnorm_bwd(dh2, x1, dy, mix, g_ffn_norm, sc2, g1)
    gw_o = _mm(mix_in, dmix, "tn", F32, 512, D_MODEL, "mm_o_dw")
    early.append(halves(gw_o.reshape(N_CHIP, D_MODEL // N_CHIP, D_MODEL)))
    dmix_in, sib_o = _mm(dmix, w_o_f, "nt", F32, 512, D_MODEL, "mm_o_dx", swap=early[2:])
    early_sib.append(sib_o)
    early_sums = [_pair_sum(g, a, c_idx, "pair_sum_" + n) for g, a, n in zip(early, early_sib, early_names)]
    dqm, dkm, dvm, *early_recv = _attn_bwd(qm, km, vm, o_m, dmix_in, 0, lse_m, True, scale_m, "attn_mla_bwd",
                                           scatter=early_sums[:1])
    dqd, dkd, dvd, *early_recv_d = _attn_bwd(qd, kd, vd, o_d, dmix_in, DIL_W // LANE, lse_d, False, scale_d,
                                             "attn_dil_bwd", scatter=early_sums[1:])
    early_recv = early_recv + early_recv_d
    dq_raw, dkv_raw, dkpe_b, dqd_b, dkd_b, dvd_b, dgains = _attn_prep_bwd(
        dqm, dkm, dvm, dqd, dkd, dvd, q_raw, kv_raw, proj, tab, gains, consts)
    dql = _mm(dq_raw, w_qb_p, "nt", F32, 512, Q_LORA, "mm_qb_dx")
    gw_qb = _unpad_w_qb(_mm(ql, dq_raw, "tn", F32, Q_LORA, HEADS * LANE, "mm_qb_dw"))
    dkvl = _mm(dkv_raw, w_kvb_p, "nt", F32, 512, KV_LORA, "mm_kvb_dx")
    gw_kvb = _unpad_w_kvb(_mm(kvl, dkv_raw, "tn", F32, KV_LORA, HEADS * LANE + DIL_W, "mm_kvb_dw"))
    dqlat_b, dkvlat_b, dglat = _latnorm_bwd(dql, dkvl, proj, g_q_lat, g_kv_lat)
    dproj = jnp.concatenate([dqlat_b, dqd_b, dkd_b, dvd_b, dkvlat_b, dkpe_b], axis=1)
    dh = _mm(dproj, w_in_p, "nt", F32, 512, D_MODEL, "mm_in_dx")
    gw_in = _unpad_w_in(_mm(h, dproj, "tn", F32, 512, P_COLS, "mm_in_dw"))
    grad_x, acc1 = _mixnorm_bwd(dh, xs, dx1, g_mix_norm, sc1)

    packed = _pack_small(acc1, acc2, dg2, dglat, dgains, dbg, dbv, dwg, dwv, loss_part)
    late_names = ("w_in", "w_q_b", "w_kv_b")
    late = [halves(_cols_to_shards(gw_in)), halves(_cols_to_shards(gw_qb)), halves(_cols_to_shards(gw_kvb))]
    late_sib = _swap_halves_d2d(late, "rs_pair_swap_late")
    late_sums = [_pair_sum(g, a, c_idx, "pair_sum_" + n) for g, a, n in zip(late, late_sib, late_names)]
    *late_recv, gathered_small = _scatter_and_gather(late_sums, packed, "rs_scatter_late")

    grad_b_ada, *small_grads, gconv_full, loss_sum = _sum_unpack(gathered_small)
    grads = {"b_ada": grad_b_ada}
    grads.update({n: g for (n, _), g in zip(SMALL_WIDTHS, small_grads)})
    shard_cols = UP_W // N_CHIP
    grads["w_conv"] = lax.dynamic_slice_in_dim(gconv_full, q0 * shard_cols, shard_cols, axis=1)
    dmod_all = gathered_small[:, 0, :6 * D_MODEL]
    grads["w_ada"] = _ada_bwd(c_all, lax.dynamic_slice_in_dim(dmod_all, q0 * ada_cols, ada_cols, axis=1))

    big_names = late_names + early_names
    half_sums = [_shard_sum(p, b, q_idx, "shard_sum_" + n)
                 for p, b, n in zip(late_sums + early_sums, list(late_recv) + list(early_recv), big_names)]
    from_sib = _join_halves(half_sums)
    south = ci == 0
    for n, mine, theirs in zip(big_names, half_sums, from_sib):
        grads[n] = jnp.concatenate([jnp.where(south, mine, theirs), jnp.where(south, theirs, mine)], axis=0)

    delta, new_m, new_v = {}, {}, {}
    for n in ("w_ada", "w_in", "w_q_b", "w_kv_b", "w_o", "w_up", "w_conv", "w_down"):
        operands = (weights[n], grads[n], mom_m[n], mom_v[n])
        if n == "w_ada":
            operands = _in_hbm(*operands)
        delta[n], new_m[n], new_v[n] = _adamw(*operands, "adamw_" + n)
    vec_names = ("b_ada",) + tuple(n for n, _ in SMALL_WIDTHS)
    sd, sm, sv = _adamw_vectors(*[[d_[n] for n in vec_names] for d_ in (small_w, grads, mom_m, mom_v)])
    for k, n in enumerate(vec_names):
        delta[n], new_m[n], new_v[n] = sd[k], sm[k], sv[k]

    loss = loss_sum[0, 0]
    order = ("w_ada", "b_ada", "g_mix_norm", "w_in", "g_q_lat", "w_q_b", "g_kv_lat", "w_kv_b", "g_mla_q_nope", "g_mla_q_pe",
             "g_mla_k_nope", "g_mla_k_pe", "g_dil_q", "g_dil_k", "w_o", "g_ffn_norm", "w_up", "w_conv", "b_conv", "w_down")
    lead = lambda n, z: z[None] if n.startswith("w_") else z
    outs = [loss, grad_x[None]]
    for d_ in (grads, delta, new_m, new_v):
        outs += [lead(n, d_[n]) for n in order]
    return tuple(outs)
```

```python
import functools

import numpy as np
import jax
import jax.numpy as jnp
from jax import lax
from jax.experimental import pallas as pl
from jax.experimental.pallas import tpu as pltpu

F32 = jnp.float32
BF16 = jnp.bfloat16
I32 = jnp.int32

D_MODEL = 1024
HEADS = 8
NOPE = 64
ROPE = 32
Q_LORA = 512
KV_LORA = 256
DIL_DIM = 64
DIL_W = HEADS * DIL_DIM
D_FF = 2816
UP_W = 2 * D_FF
IN_COLS = Q_LORA + KV_LORA + ROPE + 3 * DIL_W
ROPE_THETA = 10000.0
EPS = 1e-6
NEG_INF = -1e30
N_DEV = 8
N_CHIP = 4

ADAM_LR = 0.001
ADAM_B1 = 0.9
ADAM_B2 = 0.999
ADAM_EPS = 1e-08
ADAM_WD = 0.01
ADAM_STEP = 10

LANE = 128
ROW_TILE = 256
ATT_TQ = 512
ATT_TK = 256
LOG2E = 1.4426950408889634
LN2 = 0.6931471805599453
VMEM_CAP = 56 * 1024 * 1024
VMEM_FLOOR = 32 * 1024 * 1024

P_QLAT, P_QD, P_KD, P_VD, P_KVLAT, P_KPE = 0, 512, 1024, 1536, 2048, 2304
P_COLS = 2432
KPE_OFF = 64

NN = (((1,), (0,)), ((), ()))
NT = (((1,), (1,)), ((), ()))
TN = (((0,), (0,)), ((), ()))
HIGHEST = lax.Precision.HIGHEST
MESH = pl.DeviceIdType.MESH


def _params(sem=None, est_bytes=0):
    limit = int(min(max(2 * est_bytes + (4 << 20), VMEM_FLOOR), VMEM_CAP))
    if sem is None:
        return pltpu.CompilerParams(vmem_limit_bytes=limit)
    return pltpu.CompilerParams(dimension_semantics=sem, vmem_limit_bytes=limit)


def _nbytes(shape, dtype):
    return int(np.prod(shape)) * jnp.dtype(dtype).itemsize


def _in_hbm(*xs):
    return [pltpu.with_memory_space_constraint(x, pltpu.HBM) for x in xs]


def _mm(a, b, dims, out_dtype, tm, tn, name, col_shards=False, swap=(), b_outer=False):
    def spec(block, index):
        if b_outer:
            return pl.BlockSpec(block, lambda g0, g1: index(g1, g0))
        return pl.BlockSpec(block, index)

    if dims == "nn":
        (m, k), (k2, n) = a.shape, b.shape
        a_spec = spec((tm, k), lambda i, j: (i, 0))
        b_spec = spec((k, tn), lambda i, j: (0, j))
        dn = NN
    elif dims == "nt":
        (m, k), (n, k2) = a.shape, b.shape
        a_spec = spec((tm, k), lambda i, j: (i, 0))
        b_spec = spec((tn, k), lambda i, j: (j, 0))
        dn = NT
    else:
        (k, m), (k2, n) = a.shape, b.shape
        a_spec = spec((k, tm), lambda i, j: (0, i))
        b_spec = spec((k, tn), lambda i, j: (0, j))
        dn = TN
    assert k == k2 and m % tm == 0 and n % tn == 0, (name, a.shape, b.shape, tm, tn)

    nw = len(swap)
    grid = (n // tn, m // tm) if b_outer else (m // tm, n // tn)

    def body(*refs):
        a_ref, b_ref, o_ref = refs[0], refs[1], refs[2 + nw]
        comm = (refs[2:2 + nw], refs[3 + nw:3 + 2 * nw]) + tuple(refs[3 + 2 * nw:])
        if nw:
            @pl.when((pl.program_id(0) == 0) & (pl.program_id(1) == 0))
            def _():
                _PairSwap(*comm).start()

        o_ref[...] = lax.dot_general(a_ref[...], b_ref[...], dn, preferred_element_type=F32).astype(o_ref.dtype)

        if nw:
            @pl.when((pl.program_id(0) == grid[0] - 1) & (pl.program_id(1) == grid[1] - 1))
            def _():
                _PairSwap(*comm).finish()

    est = _nbytes((tm, k), a.dtype) + _nbytes((tn, k), b.dtype) + _nbytes((tm, tn), F32) + _nbytes((tm, tn), out_dtype)
    if col_shards:
        out_spec = spec((None, tm, tn), lambda i, j: (j, i, 0))
        out_shape = jax.ShapeDtypeStruct((n // tn, m, tn), out_dtype)
    else:
        out_spec = spec((tm, tn), lambda i, j: (i, j))
        out_shape = jax.ShapeDtypeStruct((m, n), out_dtype)
    out = pl.pallas_call(
        body, name=name, grid=grid,
        in_specs=[a_spec, b_spec] + [ANY] * nw,
        out_specs=[out_spec] + [ANY] * nw,
        out_shape=[out_shape] + _PairSwap.out_shapes(swap),
        scratch_shapes=_PairSwap.semaphores(nw) if nw else [],
        compiler_params=_params(("arbitrary", "arbitrary") if nw else ("parallel", "parallel"), est),
    )(a, b, *swap)
    return out if nw else out[0]


def _seg_consts():
    seg_q = np.zeros((HEADS * LANE, LANE), np.float32)
    inv_q = np.zeros((1, LANE), np.float32)
    seg_k = np.zeros((HEADS * LANE, LANE), np.float32)
    inv_k = np.zeros((1, LANE), np.float32)
    seg_d = np.zeros((DIL_W, LANE), np.float32)
    inv_d = np.zeros((1, LANE), np.float32)
    for h in range(HEADS):
        seg_q[h * LANE:h * LANE + NOPE, 2 * h] = 1.0
        seg_q[h * LANE + NOPE:h * LANE + NOPE + ROPE, 2 * h + 1] = 1.0
        inv_q[0, 2 * h], inv_q[0, 2 * h + 1] = 1.0 / NOPE, 1.0 / ROPE
        seg_k[h * LANE:h * LANE + NOPE, h] = 1.0
        inv_k[0, h] = 1.0 / NOPE
        seg_d[h * DIL_DIM:(h + 1) * DIL_DIM, h] = 1.0
        inv_d[0, h] = 1.0 / DIL_DIM
    fold_q = np.tile(np.eye(LANE, dtype=np.float32), (HEADS, 1))
    fold_d = np.zeros((DIL_W, LANE), np.float32)
    fold_d[np.arange(DIL_W), np.arange(DIL_W) % DIL_DIM] = 1.0
    j = lambda v: jnp.asarray(v)
    b = lambda v: jnp.asarray(v, dtype=BF16)
    return dict(seg_q=b(seg_q), exp_q=b(seg_q.T.copy()), inv_q=j(inv_q), seg_k=b(seg_k), exp_k=b(seg_k.T.copy()),
                inv_k=j(inv_k), seg_d=b(seg_d), exp_d=b(seg_d.T.copy()), inv_d=j(inv_d), fold_q=j(fold_q), fold_d=j(fold_d))


def _rope_consts():
    inv_d = jnp.power(ROPE_THETA, -2.0 * jnp.arange(DIL_DIM // 2, dtype=F32) / DIL_DIM)
    inv_q = jnp.power(ROPE_THETA, -2.0 * jnp.arange(ROPE // 2, dtype=F32) / ROPE)
    lanes = np.arange(LANE)
    freq_d = inv_d[lanes % (DIL_DIM // 2)]
    in_pe = (lanes >= KPE_OFF) & (lanes < KPE_OFF + ROPE)
    freq_q = jnp.where(jnp.asarray(in_pe), inv_q[(lanes - KPE_OFF) % (ROPE // 2)], 0.0)
    sign_d = np.where(lanes % DIL_DIM < DIL_DIM // 2, -1.0, 1.0).astype(np.float32)
    sign_q = np.where(in_pe, np.where((lanes - KPE_OFF) < ROPE // 2, -1.0, 1.0), 0.0).astype(np.float32)
    zeros, ones = np.zeros(LANE, np.float32), np.ones(LANE, np.float32)
    freq = jnp.concatenate([freq_d, freq_d, freq_q, freq_q])[None, :]
    csel = jnp.asarray(np.concatenate([ones, zeros, ones, zeros]))[None, :]
    ssel = jnp.asarray(np.concatenate([zeros, sign_d, zeros, sign_q]))[None, :]
    return freq, csel, ssel


def _full(shape):
    return pl.BlockSpec(shape, lambda *_: (0,) * len(shape))


def _tile_lanes(x, n):
    return jnp.concatenate([x] * n, axis=1)


def _rms(x):
    return lax.rsqrt(jnp.mean(x * x, axis=-1, keepdims=True) + EPS)


def _prenorm(x, gain, scale, shift, name):
    s, d = x.shape

    def body(x_ref, g_ref, sc_ref, sh_ref, h_ref):
        xv = x_ref[...]
        h = (xv * _rms(xv)) * g_ref[...] * (1.0 + sc_ref[...]) + sh_ref[...]
        h_ref[...] = h.astype(BF16)

    row = pl.BlockSpec((ROW_TILE, d), lambda i: (i, 0))
    return pl.pallas_call(
        body, name=name, grid=(s // ROW_TILE,),
        in_specs=[row, _full((1, d)), _full((1, d)), _full((1, d))],
        out_specs=row, out_shape=jax.ShapeDtypeStruct((s, d), BF16),
        compiler_params=_params(("parallel",)),
    )(x, gain, scale, shift)


def _latnorm(proj, g_q, g_kv):
    s = proj.shape[0]

    def body(q_ref, kv_ref, gq_ref, gkv_ref, ql_ref, kvl_ref):
        q, kv = q_ref[...], kv_ref[...]
        ql_ref[...] = ((q * _rms(q)) * gq_ref[...]).astype(BF16)
        kvl_ref[...] = ((kv * _rms(kv)) * gkv_ref[...]).astype(BF16)

    return pl.pallas_call(
        body, name="latnorm", grid=(s // ROW_TILE,),
        in_specs=[pl.BlockSpec((ROW_TILE, Q_LORA), lambda i: (i, P_QLAT // Q_LORA)),
                  pl.BlockSpec((ROW_TILE, KV_LORA), lambda i: (i, P_KVLAT // KV_LORA)),
                  _full((1, Q_LORA)), _full((1, KV_LORA))],
        out_specs=[pl.BlockSpec((ROW_TILE, Q_LORA), lambda i: (i, 0)), pl.BlockSpec((ROW_TILE, KV_LORA), lambda i: (i, 0))],
        out_shape=[jax.ShapeDtypeStruct((s, Q_LORA), BF16), jax.ShapeDtypeStruct((s, KV_LORA), BF16)],
        compiler_params=_params(("parallel",)),
    )(proj, proj, g_q, g_kv)


def _dot01(v, mat01):
    hi = v.astype(BF16)
    lo = (v - hi.astype(F32)).astype(BF16)
    return jnp.dot(hi, mat01, preferred_element_type=F32) + jnp.dot(lo, mat01, preferred_element_type=F32)


def _seg_rinv(x, seg, exp, inv):
    r = lax.rsqrt(_dot01(x * x, seg) * inv + EPS)
    return _dot01(r, exp)


def _seg_mean(v, seg, exp, inv):
    return _dot01(_dot01(v, seg) * inv, exp)


def _swap_halves(x, half):
    n = x.shape[1]
    lane = lax.broadcasted_iota(I32, (1, n), 1)
    first = (lane & (2 * half - 1)) < half
    return jnp.where(first, pltpu.roll(x, n - half, 1), pltpu.roll(x, half, 1))


def _rope(x, cos, sin_signed, half):
    return x * cos + _swap_halves(x, half) * sin_signed


def _rope_bwd(dy, cos, sin_signed, half):
    return dy * cos + _swap_halves(dy * sin_signed, half)


def _pe_lane_mask(n):
    lane = lax.broadcasted_iota(I32, (1, n), 1) & (LANE - 1)
    return (lane >= KPE_OFF) & (lane < KPE_OFF + ROPE)


def _attn_prep(q_raw, kv_raw, proj, tab, gains, consts):
    s = q_raw.shape[0]
    hw = HEADS * LANE

    def body(q_ref, kv_ref, kpe_ref, qd_ref, kd_ref, vd_ref, tab_ref,
             gq_ref, gk_ref, gkpe_ref, gdq_ref, gdk_ref,
             segq_ref, expq_ref, invq_ref, segk_ref, expk_ref, invk_ref, segd_ref, expd_ref, invd_ref,
             qm_ref, km_ref, vm_ref, qdo_ref, kdo_ref, vdo_ref):
        tab_v = tab_ref[...]
        cos_d, sin_d = _tile_lanes(tab_v[:, 0:LANE], DIL_W // LANE), _tile_lanes(tab_v[:, LANE:2 * LANE], DIL_W // LANE)
        cos_q1, sin_q1 = tab_v[:, 2 * LANE:3 * LANE], tab_v[:, 3 * LANE:4 * LANE]
        cos_q, sin_q = _tile_lanes(cos_q1, HEADS), _tile_lanes(sin_q1, HEADS)

        q = q_ref[...]
        qn = q * _seg_rinv(q, segq_ref[...], expq_ref[...], invq_ref[...]) * gq_ref[...]
        qm_ref[...] = _rope(qn, cos_q, sin_q, ROPE // 2).astype(BF16)

        kv = kv_ref[...]
        kp = kv[:, :hw]
        kn = kp * _seg_rinv(kp, segk_ref[...], expk_ref[...], invk_ref[...]) * gk_ref[...]
        kpe = kpe_ref[...]
        r_pe = lax.rsqrt(jnp.sum(kpe * kpe, axis=-1, keepdims=True) * (1.0 / ROPE) + EPS)
        kpe_r = _rope(kpe * r_pe * gkpe_ref[...], cos_q1, sin_q1, ROPE // 2)
        km_ref[...] = (kn + _tile_lanes(kpe_r, HEADS)).astype(BF16)
        vm_ref[...] = kv[:, hw:].astype(BF16)

        qd = qd_ref[...]
        qdn = qd * _seg_rinv(qd, segd_ref[...], expd_ref[...], invd_ref[...]) * gdq_ref[...]
        qdo_ref[...] = _rope(qdn, cos_d, sin_d, DIL_DIM // 2).astype(BF16)
        kd = kd_ref[...]
        kdn = kd * _seg_rinv(kd, segd_ref[...], expd_ref[...], invd_ref[...]) * gdk_ref[...]
        kdo_ref[...] = _rope(kdn, cos_d, sin_d, DIL_DIM // 2).astype(BF16)
        vdo_ref[...] = vd_ref[...].astype(BF16)

    t = ROW_TILE
    row = lambda w, cb=0: pl.BlockSpec((t, w), lambda i: (i, cb))
    c = consts
    return pl.pallas_call(
        body, name="attn_prep", grid=(s // t,),
        in_specs=[row(hw), row(hw + DIL_W), row(LANE, P_KPE // LANE), row(DIL_W, P_QD // DIL_W), row(DIL_W, P_KD // DIL_W),
                  row(DIL_W, P_VD // DIL_W), row(4 * LANE),
                  _full((1, hw)), _full((1, hw)), _full((1, LANE)), _full((1, DIL_W)), _full((1, DIL_W)),
                  _full((hw, LANE)), _full((LANE, hw)), _full((1, LANE)), _full((hw, LANE)), _full((LANE, hw)), _full((1, LANE)),
                  _full((DIL_W, LANE)), _full((LANE, DIL_W)), _full((1, LANE))],
        out_specs=[row(hw), row(hw), row(DIL_W), row(DIL_W), row(DIL_W), row(DIL_W)],
        out_shape=[jax.ShapeDtypeStruct((s, hw), BF16), jax.ShapeDtypeStruct((s, hw), BF16)]
        + [jax.ShapeDtypeStruct((s, DIL_W), BF16)] * 4,
        compiler_params=_params(("parallel",), 24 << 20),
    )(*_in_hbm(q_raw, kv_raw, proj, proj, proj, proj, tab), gains["q"], gains["k"], gains["kpe"], gains["dq"], gains["dk"],
      c["seg_q"], c["exp_q"], c["inv_q"], c["seg_k"], c["exp_k"], c["inv_k"], c["seg_d"], c["exp_d"], c["inv_d"])


def _attn_prep_bwd(dqm, dkm, dvm, dqd, dkd, dvd, q_raw, kv_raw, proj, tab, gains, consts):
    s = q_raw.shape[0]
    hw = HEADS * LANE
    n_steps = s // ROW_TILE

    def body(dqm_ref, dkm_ref, dvm_ref, dqd_ref, dkd_ref, dvd_ref, q_ref, kv_ref, kpe_ref, qd_ref, kd_ref, tab_ref,
             gq_ref, gk_ref, gkpe_ref, gdq_ref, gdk_ref,
             segq_ref, expq_ref, invq_ref, segk_ref, expk_ref, invk_ref, segd_ref, expd_ref, invd_ref, foldq_ref, foldd_ref,
             dq_ref, dkv_ref, dkpe_ref, dqdo_ref, dkdo_ref, dvdo_ref, dg_ref, acc_ref):
        i = pl.program_id(0)

        @pl.when(i == 0)
        def _():
            acc_ref[...] = jnp.zeros_like(acc_ref)

        tab_v = tab_ref[...]
        cos_d, sin_d = _tile_lanes(tab_v[:, 0:LANE], DIL_W // LANE), _tile_lanes(tab_v[:, LANE:2 * LANE], DIL_W // LANE)
        cos_q1, sin_q1 = tab_v[:, 2 * LANE:3 * LANE], tab_v[:, 3 * LANE:4 * LANE]
        cos_q, sin_q = _tile_lanes(cos_q1, HEADS), _tile_lanes(sin_q1, HEADS)

        def norm_bwd(x, dyg, gain, seg, exp, inv):
            rinv = _seg_rinv(x, seg, exp, inv)
            xn = x * rinv
            dxn = dyg * gain
            dx = rinv * (dxn - xn * _seg_mean(dxn * xn, seg, exp, inv))
            return dx, jnp.sum(dyg * xn, axis=0, keepdims=True)

        dq, gq_l = norm_bwd(q_ref[...], _rope_bwd(dqm_ref[...], cos_q, sin_q, ROPE // 2), gq_ref[...],
                            segq_ref[...], expq_ref[...], invq_ref[...])
        dq_ref[...] = dq.astype(BF16)

        dkm = dkm_ref[...]
        kv = kv_ref[...]
        dkp, gk_l = norm_bwd(kv[:, :hw], dkm, gk_ref[...], segk_ref[...], expk_ref[...], invk_ref[...])
        dkv_ref[:, :hw] = dkp.astype(BF16)
        dkv_ref[:, hw:] = dvm_ref[...].astype(BF16)

        dkpe_r = dkm[:, 0:LANE]
        for h in range(1, HEADS):
            dkpe_r = dkpe_r + dkm[:, h * LANE:(h + 1) * LANE]
        dkpe_r = jnp.where(_pe_lane_mask(LANE), dkpe_r, 0.0)
        dyg = _rope_bwd(dkpe_r, cos_q1, sin_q1, ROPE // 2)
        kpe = kpe_ref[...]
        r_pe = lax.rsqrt(jnp.sum(kpe * kpe, axis=-1, keepdims=True) * (1.0 / ROPE) + EPS)
        xn = kpe * r_pe
        dxn = dyg * gkpe_ref[...]
        dkpe = r_pe * (dxn - xn * (jnp.sum(dxn * xn, axis=-1, keepdims=True) * (1.0 / ROPE)))
        dkpe_ref[...] = dkpe.astype(BF16)
        gkpe_l = jnp.sum(dyg * xn, axis=0, keepdims=True)

        dqd_v, gdq_l = norm_bwd(qd_ref[...], _rope_bwd(dqd_ref[...], cos_d, sin_d, DIL_DIM // 2), gdq_ref[...],
                                segd_ref[...], expd_ref[...], invd_ref[...])
        dqdo_ref[...] = dqd_v.astype(BF16)
        dkd_v, gdk_l = norm_bwd(kd_ref[...], _rope_bwd(dkd_ref[...], cos_d, sin_d, DIL_DIM // 2), gdk_ref[...],
                                segd_ref[...], expd_ref[...], invd_ref[...])
        dkdo_ref[...] = dkd_v.astype(BF16)
        dvdo_ref[...] = dvd_ref[...].astype(BF16)

        acc_ref[0:1, :] += gq_l
        acc_ref[1:2, :] += gk_l
        acc_ref[2:3, 0:LANE] += gkpe_l
        acc_ref[3:4, 0:DIL_W] += gdq_l
        acc_ref[4:5, 0:DIL_W] += gdk_l

        @pl.when(i == n_steps - 1)
        def _():
            acc = acc_ref[...]
            fq = jnp.dot(acc, foldq_ref[...], precision=HIGHEST, preferred_element_type=F32)
            fd = jnp.dot(acc[:, 0:DIL_W], foldd_ref[...], precision=HIGHEST, preferred_element_type=F32)
            rows = lax.broadcasted_iota(I32, (8, LANE), 0)
            base = jnp.where(rows < 2, fq, jnp.where(rows == 2, acc[:, 0:LANE], fd))
            at0 = pltpu.roll(base, LANE - KPE_OFF, 1)
            dg_ref[...] = jnp.where(rows == 5, pltpu.roll(at0, 5, 0), jnp.where(rows == 2, at0, base))

    t = ROW_TILE
    row = lambda w, cb=0: pl.BlockSpec((t, w), lambda i: (i, cb))
    c = consts
    return pl.pallas_call(
        body, name="attn_prep_bwd", grid=(n_steps,),
        in_specs=[row(hw), row(hw), row(DIL_W), row(DIL_W), row(DIL_W), row(DIL_W),
                  row(hw), row(hw + DIL_W), row(LANE, P_KPE // LANE), row(DIL_W, P_QD // DIL_W), row(DIL_W, P_KD // DIL_W),
                  row(4 * LANE),
                  _full((1, hw)), _full((1, hw)), _full((1, LANE)), _full((1, DIL_W)), _full((1, DIL_W)),
                  _full((hw, LANE)), _full((LANE, hw)), _full((1, LANE)), _full((hw, LANE)), _full((LANE, hw)), _full((1, LANE)),
                  _full((DIL_W, LANE)), _full((LANE, DIL_W)), _full((1, LANE)), _full((hw, LANE)), _full((DIL_W, LANE))],
        out_specs=[row(hw), row(hw + DIL_W), row(LANE), row(DIL_W), row(DIL_W), row(DIL_W), _full((8, LANE))],
        out_shape=[jax.ShapeDtypeStruct((s, hw), BF16), jax.ShapeDtypeStruct((s, hw + DIL_W), BF16),
                   jax.ShapeDtypeStruct((s, LANE), BF16)] + [jax.ShapeDtypeStruct((s, DIL_W), BF16)] * 3
        + [jax.ShapeDtypeStruct((8, LANE), F32)],
        scratch_shapes=[pltpu.VMEM((8, hw), F32)],
        compiler_params=_params(("arbitrary",), 28 << 20),
    )(*_in_hbm(dqm, dkm, dvm, dqd, dkd, dvd, q_raw, kv_raw, proj, proj, proj, tab),
      gains["q"], gains["k"], gains["kpe"], gains["dq"], gains["dk"],
      c["seg_q"], c["exp_q"], c["inv_q"], c["seg_k"], c["exp_k"], c["inv_k"], c["seg_d"], c["exp_d"], c["inv_d"],
      c["fold_q"], c["fold_d"])


def _latnorm_bwd(dql, dkvl, proj, g_q, g_kv):
    s = proj.shape[0]
    n_steps = s // ROW_TILE

    def body(dql_ref, dkvl_ref, q_ref, kv_ref, gq_ref, gkv_ref, dq_ref, dkv_ref, dg_ref):
        i = pl.program_id(0)

        @pl.when(i == 0)
        def _():
            dg_ref[...] = jnp.zeros_like(dg_ref)

        def one(x, dyg, gain):
            r = _rms(x)
            xn = x * r
            dxn = dyg * gain
            dx = r * (dxn - xn * jnp.mean(dxn * xn, axis=-1, keepdims=True))
            return dx, jnp.sum(dyg * xn, axis=0, keepdims=True)

        dq, gq_l = one(q_ref[...], dql_ref[...], gq_ref[...])
        dkv, gkv_l = one(kv_ref[...], dkvl_ref[...], gkv_ref[...])
        dq_ref[...] = dq.astype(BF16)
        dkv_ref[...] = dkv.astype(BF16)
        dg_ref[0:1, :] += gq_l
        dg_ref[1:2, 0:KV_LORA] += gkv_l

    t = ROW_TILE
    return pl.pallas_call(
        body, name="latnorm_bwd", grid=(n_steps,),
        in_specs=[pl.BlockSpec((t, Q_LORA), lambda i: (i, 0)), pl.BlockSpec((t, KV_LORA), lambda i: (i, 0)),
                  pl.BlockSpec((t, Q_LORA), lambda i: (i, P_QLAT // Q_LORA)),
                  pl.BlockSpec((t, KV_LORA), lambda i: (i, P_KVLAT // KV_LORA)),
                  _full((1, Q_LORA)), _full((1, KV_LORA))],
        out_specs=[pl.BlockSpec((t, Q_LORA), lambda i: (i, 0)), pl.BlockSpec((t, KV_LORA), lambda i: (i, 0)), _full((8, Q_LORA))],
        out_shape=[jax.ShapeDtypeStruct((s, Q_LORA), BF16), jax.ShapeDtypeStruct((s, KV_LORA), BF16),
                   jax.ShapeDtypeStruct((8, Q_LORA), F32)],
        compiler_params=_params(("arbitrary",)),
    )(dql, dkvl, proj, proj, g_q, g_kv)


def _resid_prenorm(x, mix, g1, gain, scale, shift):
    s, d = x.shape

    def body(x_ref, mix_ref, g1_ref, g_ref, sc_ref, sh_ref, x1_ref, h_ref):
        x1 = x_ref[...] + g1_ref[...] * mix_ref[...]
        x1_ref[...] = x1
        h_ref[...] = ((x1 * _rms(x1)) * g_ref[...] * (1.0 + sc_ref[...]) + sh_ref[...]).astype(BF16)

    row = pl.BlockSpec((ROW_TILE, d), lambda i: (i, 0))
    vec = _full((1, d))
    return pl.pallas_call(
        body, name="resid_prenorm", grid=(s // ROW_TILE,),
        in_specs=[row, row, vec, vec, vec, vec], out_specs=[row, row],
        out_shape=[jax.ShapeDtypeStruct((s, d), F32), jax.ShapeDtypeStruct((s, d), BF16)],
        compiler_params=_params(("parallel",)),
    )(x, mix, g1, gain, scale, shift)


CONV_TILE = 1408
HALO = 8


def _shift_down(x, halo, k):
    t = x.shape[0]
    row = lax.broadcasted_iota(I32, (t, 1), 0)
    out = pltpu.roll(x, k, 0)
    for r in range(k):
        out = jnp.where(row == r, halo[HALO - k + r:HALO - k + r + 1, :], out)
    return out


def _shift_up(x, halo, k):
    t = x.shape[0]
    row = lax.broadcasted_iota(I32, (t, 1), 0)
    out = pltpu.roll(x, t - k, 0)
    for r in range(k):
        out = jnp.where(row == t - k + r, halo[r:r + 1, :], out)
    return out


def _conv_fwd(x, halo, w, b):
    p1, p2 = _shift_down(x, halo, 1), _shift_down(x, halo, 2)
    u = b + p2 * w[0:1, :]
    u = u + p1 * w[1:2, :]
    u = u + x * w[2:3, :]
    return u, p1, p2


def _sigmoid(x):
    return 1.0 / (1.0 + jnp.exp(-x))


def _conv_gate(up, w_conv, b_conv):
    s = up.shape[0]
    t = ROW_TILE
    nj = D_FF // CONV_TILE
    hb = t // HALO

    def body(g_ref, v_ref, gh_ref, vh_ref, wg_ref, wv_ref, bg_ref, bv_ref, a_ref):
        live = (pl.program_id(0) > 0).astype(F32)
        ug, _, _ = _conv_fwd(g_ref[...], gh_ref[...] * live, wg_ref[...], bg_ref[...])
        uv, _, _ = _conv_fwd(v_ref[...], vh_ref[...] * live, wv_ref[...], bv_ref[...])
        a_ref[...] = (ug * _sigmoid(ug) * uv).astype(BF16)

    main = lambda off: pl.BlockSpec((t, CONV_TILE), lambda i, j: (i, j + off))
    halo = lambda off: pl.BlockSpec((HALO, CONV_TILE), lambda i, j: (jnp.maximum(i * hb - 1, 0), j + off))
    wsp = lambda off: pl.BlockSpec((3, CONV_TILE), lambda i, j: (0, j + off))
    bsp = lambda off: pl.BlockSpec((1, CONV_TILE), lambda i, j: (0, j + off))
    return pl.pallas_call(
        body, name="conv_gate", grid=(s // t, nj),
        in_specs=[main(0), main(nj), halo(0), halo(nj), wsp(0), wsp(nj), bsp(0), bsp(nj)],
        out_specs=pl.BlockSpec((t, CONV_TILE), lambda i, j: (i, j)),
        out_shape=jax.ShapeDtypeStruct((s, D_FF), BF16),
        compiler_params=_params(("parallel", "parallel"), 12 << 20),
    )(up, up, up, up, w_conv, w_conv, b_conv, b_conv)


def _gate_bwd(up, da, w_conv, b_conv):
    s = up.shape[0]
    t = ROW_TILE
    nj = D_FF // CONV_TILE
    hb = t // HALO
    n_i = s // t

    def body(g_ref, v_ref, gh_ref, vh_ref, gn_ref, vn_ref, da_ref, dan_ref, wg_ref, wv_ref, bg_ref, bv_ref,
             dupg_ref, dupv_ref, dbg_ref, dbv_ref, dwg_ref, dwv_ref):
        i = pl.program_id(1)

        @pl.when(i == 0)
        def _():
            for r in (dbg_ref, dbv_ref, dwg_ref, dwv_ref):
                r[...] = jnp.zeros_like(r)

        def d_gate(ug, uv, da_v):
            sg = _sigmoid(ug)
            return da_v * uv * (sg * (1.0 + ug * (1.0 - sg))), da_v * (ug * sg)

        live = (i > 0).astype(F32)
        xg, xv = g_ref[...], v_ref[...]
        wg, wv = wg_ref[...], wv_ref[...]
        ug, g1, g2 = _conv_fwd(xg, gh_ref[...] * live, wg, bg_ref[...])
        uv, v1, v2 = _conv_fwd(xv, vh_ref[...] * live, wv, bv_ref[...])
        dug, duv = d_gate(ug, uv, da_ref[...])

        more = (i < n_i - 1).astype(F32)
        ug_n, _, _ = _conv_fwd(gn_ref[...], xg[t - HALO:, :], wg, bg_ref[...])
        uv_n, _, _ = _conv_fwd(vn_ref[...], xv[t - HALO:, :], wv, bv_ref[...])
        dug_n, duv_n = d_gate(ug_n, uv_n, dan_ref[...] * more)

        def conv_t(du, du_n, w):
            return du * w[2:3, :] + _shift_up(du, du_n, 1) * w[1:2, :] + _shift_up(du, du_n, 2) * w[0:1, :]

        dupg_ref[...] = conv_t(dug, dug_n, wg).astype(BF16)
        dupv_ref[...] = conv_t(duv, duv_n, wv).astype(BF16)
        csum = lambda z: jnp.sum(z, axis=0, keepdims=True)
        dbg_ref[...] += csum(dug)
        dbv_ref[...] += csum(duv)
        dwg_ref[0:1, :] += csum(dug * g2)
        dwg_ref[1:2, :] += csum(dug * g1)
        dwg_ref[2:3, :] += csum(dug * xg)
        dwv_ref[0:1, :] += csum(duv * v2)
        dwv_ref[1:2, :] += csum(duv * v1)
        dwv_ref[2:3, :] += csum(duv * xv)

    last_halo = s // HALO - 1
    main = lambda off: pl.BlockSpec((t, CONV_TILE), lambda j, i: (i, j + off))
    halo = lambda off: pl.BlockSpec((HALO, CONV_TILE), lambda j, i: (jnp.maximum(i * hb - 1, 0), j + off))
    nxt = lambda off: pl.BlockSpec((HALO, CONV_TILE), lambda j, i: (jnp.minimum((i + 1) * hb, last_halo), j + off))
    wsp = lambda off: pl.BlockSpec((3, CONV_TILE), lambda j, i: (0, j + off))
    bsp = lambda off: pl.BlockSpec((1, CONV_TILE), lambda j, i: (0, j + off))
    outs = pl.pallas_call(
        body, name="gate_bwd", grid=(nj, n_i),
        in_specs=[main(0), main(nj), halo(0), halo(nj), nxt(0), nxt(nj), main(0), nxt(0),
                  wsp(0), wsp(nj), bsp(0), bsp(nj)],
        out_specs=[main(0), main(0),
                   pl.BlockSpec((1, CONV_TILE), lambda j, i: (0, j)), pl.BlockSpec((1, CONV_TILE), lambda j, i: (0, j)),
                   pl.BlockSpec((3, CONV_TILE), lambda j, i: (0, j)), pl.BlockSpec((3, CONV_TILE), lambda j, i: (0, j))],
        out_shape=[jax.ShapeDtypeStruct((s, D_FF), BF16), jax.ShapeDtypeStruct((s, D_FF), BF16),
                   jax.ShapeDtypeStruct((1, D_FF), F32), jax.ShapeDtypeStruct((1, D_FF), F32),
                   jax.ShapeDtypeStruct((3, D_FF), F32), jax.ShapeDtypeStruct((3, D_FF), F32)],
        compiler_params=_params(("parallel", "arbitrary"), 24 << 20),
    )(up, up, up, up, up, up, da, da, w_conv, w_conv, b_conv, b_conv)
    return outs


def _final(x1, ffn, tgt, g2):
    s, d = x1.shape
    n_steps = s // ROW_TILE

    def body(x1_ref, f_ref, t_ref, g2_ref, dy_ref, df_ref, dg2_ref, loss_ref, lacc_ref):
        i = pl.program_id(0)

        @pl.when(i == 0)
        def _():
            dg2_ref[...] = jnp.zeros_like(dg2_ref)
            lacc_ref[...] = jnp.zeros_like(lacc_ref)

        f = f_ref[...]
        e = x1_ref[...] + g2_ref[...] * f - t_ref[...]
        dy = e * (1.0 / d)
        dy_ref[...] = dy
        df_ref[...] = (dy * g2_ref[...]).astype(BF16)
        dg2_ref[...] += jnp.sum(dy * f, axis=0, keepdims=True)
        lacc_ref[...] += jnp.sum(e * e, axis=0, keepdims=True)

        @pl.when(i == n_steps - 1)
        def _():
            loss_ref[...] = jnp.sum(lacc_ref[...], axis=1, keepdims=True) * (0.5 / d)

    row = pl.BlockSpec((ROW_TILE, d), lambda i: (i, 0))
    return pl.pallas_call(
        body, name="final", grid=(n_steps,),
        in_specs=[row, row, row, _full((1, d))],
        out_specs=[row, row, _full((1, d)), _full((1, 1))],
        out_shape=[jax.ShapeDtypeStruct((s, d), F32), jax.ShapeDtypeStruct((s, d), BF16),
                   jax.ShapeDtypeStruct((1, d), F32), jax.ShapeDtypeStruct((1, 1), F32)],
        scratch_shapes=[pltpu.VMEM((1, d), F32)],
        compiler_params=_params(("arbitrary",)),
    )(x1, ffn, tgt, g2)


def _ffnnorm_bwd(dh2, x1, dy, mix, gain, scale, g1):
    s, d = x1.shape
    n_steps = s // ROW_TILE

    def body(dh_ref, x_ref, dy_ref, mix_ref, g_ref, sc_ref, g1_ref, dx_ref, dm_ref, acc_ref):
        i = pl.program_id(0)

        @pl.when(i == 0)
        def _():
            acc_ref[...] = jnp.zeros_like(acc_ref)

        dh, x = dh_ref[...], x_ref[...]
        r = _rms(x)
        xn = x * r
        dn = dh * (1.0 + sc_ref[...])
        dxn = dn * g_ref[...]
        dx = dy_ref[...] + r * (dxn - xn * jnp.mean(dxn * xn, axis=-1, keepdims=True))
        dx_ref[...] = dx
        dm_ref[...] = (dx * g1_ref[...]).astype(BF16)
        csum = lambda z: jnp.sum(z, axis=0, keepdims=True)
        acc_ref[0:1, :] += csum(dh)
        acc_ref[1:2, :] += csum(dh * (xn * g_ref[...]))
        acc_ref[2:3, :] += csum(dn * xn)
        acc_ref[3:4, :] += csum(dx * mix_ref[...])

    row = pl.BlockSpec((ROW_TILE, d), lambda i: (i, 0))
    vec = _full((1, d))
    return pl.pallas_call(
        body, name="ffnnorm_bwd", grid=(n_steps,),
        in_specs=[row, row, row, row, vec, vec, vec],
        out_specs=[row, row, _full((8, d))],
        out_shape=[jax.ShapeDtypeStruct((s, d), F32), jax.ShapeDtypeStruct((s, d), BF16), jax.ShapeDtypeStruct((8, d), F32)],
        compiler_params=_params(("arbitrary",)),
    )(dh2, x1, dy, mix, gain, scale, g1)


def _mixnorm_bwd(dh, x, dx1, gain, scale):
    s, d = x.shape
    n_steps = s // ROW_TILE

    def body(dh_ref, x_ref, dx1_ref, g_ref, sc_ref, gx_ref, acc_ref):
        i = pl.program_id(0)

        @pl.when(i == 0)
        def _():
            acc_ref[...] = jnp.zeros_like(acc_ref)

        dh, x = dh_ref[...], x_ref[...]
        r = _rms(x)
        xn = x * r
        dn = dh * (1.0 + sc_ref[...])
        dxn = dn * g_ref[...]
        gx_ref[...] = dx1_ref[...] + r * (dxn - xn * jnp.mean(dxn * xn, axis=-1, keepdims=True))
        csum = lambda z: jnp.sum(z, axis=0, keepdims=True)
        acc_ref[0:1, :] += csum(dh)
        acc_ref[1:2, :] += csum(dh * (xn * g_ref[...]))
        acc_ref[2:3, :] += csum(dn * xn)

    row = pl.BlockSpec((ROW_TILE, d), lambda i: (i, 0))
    vec = _full((1, d))
    return pl.pallas_call(
        body, name="mixnorm_bwd", grid=(n_steps,),
        in_specs=[row, row, row, vec, vec],
        out_specs=[row, _full((8, d))],
        out_shape=[jax.ShapeDtypeStruct((s, d), F32), jax.ShapeDtypeStruct((8, d), F32)],
        compiler_params=_params(("arbitrary",)),
    )(dh, x, dx1, gain, scale)


def _key_count(d, dilated):
    if not dilated:
        return jnp.where(d >= 0, 1.0, 0.0)
    one = lambda cond: jnp.where(cond, 1.0, 0.0)
    cnt = one(d <= 128) + one(((d & 3) == 0) & (d <= 512)) + one((d & 15) == 0)
    return jnp.where(d >= 0, cnt, 0.0)


def _block_kinds(mla):
    return (0, "diag", "none") if mla else (512, "near", "far")


NEAR_OFFSETS = 4


def _scores_t(ka, qa, scale, kind, rel_t, offset, near_tabs=None):
    return _mask_scores(lax.dot_general(ka, qa, NT, preferred_element_type=F32), scale, kind, rel_t, offset, near_tabs)


def _fill_near_tables(bias_ref, cnt_ref, rel_t):
    for idx in range(NEAR_OFFSETS):
        cnt = _key_count(rel_t + (idx - 1) * ATT_TK, True)
        cnt_ref[idx] = cnt
        bias_ref[idx] = jnp.where(cnt > 0.0, 0.0, NEG_INF)


def _mask_scores(products, scale, kind, rel_t, offset, near_tabs=None):
    st = products * (scale * LOG2E)
    cnt = None
    if kind == "diag":
        st = jnp.where(rel_t + offset >= 0, st, NEG_INF)
    elif kind == "far":
        st = jnp.where((rel_t & 15) == 0, st, NEG_INF)
    elif kind == "near":
        bias_ref, cnt_ref = near_tabs
        idx = offset // ATT_TK + 1
        st = st + bias_ref[idx]
        cnt = cnt_ref[idx]
    return st, cnt


def _attn_fwd(q, k, v, mla, scale, name, gather=()):
    s = q.shape[0]
    qw = 2 * LANE if mla else LANE
    tq, tk = ATT_TQ, ATT_TK
    reach, kind_near, kind_far = _block_kinds(mla)
    assert s % tq == 0 and tq % tk == 0 and reach % tk == 0 and (mla or (reach + tq) // tk == NEAR_OFFSETS)
    ng = len(gather)
    last_step = HEADS // 2 - 1

    def body(*refs):
        q_ref, k_ref, v_ref = refs[:3]
        o_ref, lse_ref = refs[3 + ng:5 + ng]
        vt_ref, st_ref = refs[5 + 2 * ng:7 + 2 * ng]
        near_tabs = None if mla else refs[7 + 2 * ng:9 + 2 * ng]
        n_tabs = 0 if mla else 2
        comm = (refs[3:3 + ng], refs[5 + ng:5 + 2 * ng]) + tuple(refs[7 + n_tabs + 2 * ng:])
        if ng:
            @pl.when(pl.program_id(0) == 0)
            def _():
                _Gather(*comm).start()

            @pl.when(pl.program_id(0) == last_step)
            def _():
                _Gather(*comm).forward()

        lane = lax.broadcasted_iota(I32, (1, LANE), 1)
        rel_t = lax.broadcasted_iota(I32, (tk, tq), 1) - lax.broadcasted_iota(I32, (tk, tq), 0)
        if not mla:
            _fill_near_tables(*near_tabs, rel_t)

        def transpose_v(j, carry):
            c0 = pl.multiple_of(j * tk, tk)
            vt_ref[:, pl.ds(c0, tk)] = v_ref[pl.ds(c0, tk), :].astype(F32).T.astype(BF16)
            return carry

        lax.fori_loop(0, s // tk, transpose_v, 0)

        def q_block(qi, carry):
            r0 = pl.multiple_of(qi * tq, tq)
            kcols = [slice(a * LANE, (a + 1) * LANE) if mla else slice(0, LANE) for a in range(2)]
            qas = [q_ref[pl.ds(r0, tq), kcols[a]] for a in range(2)]
            if not mla:
                qas = [jnp.where(lane < DIL_DIM, qas[0], jnp.zeros_like(qas[0])),
                       jnp.where(lane >= DIL_DIM, qas[1], jnp.zeros_like(qas[1]))]

            n_k = (r0 + tq) // tk

            def products(kj):
                c0 = pl.multiple_of(kj * tk, tk)
                return [lax.dot_general(k_ref[pl.ds(c0, tk), kcols[a]], qas[a], NT, preferred_element_type=F32)
                        for a in range(2)]

            for a, pr in enumerate(products(0)):
                st_ref[0, a] = pr

            def k_block(kj, c, kind):
                c0 = pl.multiple_of(kj * tk, tk)
                slot = kj & 1
                ahead = products(jnp.minimum(kj + 1, n_k - 1))
                out = []
                for a in range(2):
                    m, l, acc = c[a]
                    st, cnt = _mask_scores(st_ref[slot, a], scale, kind, rel_t, r0 - c0, near_tabs)
                    st_ref[1 - slot, a] = ahead[a]
                    m_new = jnp.maximum(m, jnp.max(st, axis=0, keepdims=True))
                    alpha = jnp.exp2(m - m_new)
                    p = jnp.exp2(st - m_new)
                    if cnt is not None:
                        p = p * cnt
                    l = alpha * l + jnp.sum(p, axis=0, keepdims=True)
                    vt = vt_ref[a * DIL_DIM:(a + 1) * DIL_DIM, pl.ds(c0, tk)]
                    acc = alpha * acc + jnp.dot(vt, p.astype(BF16), preferred_element_type=F32)
                    out.append((m_new, l, acc))
                return tuple(out)

            one = (jnp.full((1, tq), NEG_INF, F32), jnp.zeros((1, tq), F32), jnp.zeros((DIL_DIM, tq), F32))
            first_near = jnp.maximum((r0 - reach) // tk, 0)
            c = lax.fori_loop(0, first_near, functools.partial(k_block, kind=kind_far), (one, one))
            res = lax.fori_loop(first_near, (r0 + tq) // tk, functools.partial(k_block, kind=kind_near), c)
            o_t = jnp.concatenate([res[a][2] / res[a][1] for a in range(2)], axis=0)
            o_ref[pl.ds(r0, tq), :] = o_t.T.astype(BF16)
            for a in range(2):
                lse_ref[a, :, pl.ds(r0, tq)] = res[a][0] * LN2 + jnp.log(res[a][1])
            return carry

        lax.fori_loop(0, s // tq, q_block, 0)

        if ng:
            @pl.when(pl.program_id(0) == last_step)
            def _():
                _Gather(*comm).finish()

    return pl.pallas_call(
        body, name=name, grid=(HEADS // 2,),
        in_specs=[pl.BlockSpec((s, qw), lambda h: (0, h)), pl.BlockSpec((s, qw), lambda h: (0, h)),
                  pl.BlockSpec((s, LANE), lambda h: (0, h))] + [ANY] * ng,
        out_specs=[pl.BlockSpec((s, LANE), lambda h: (0, h)), pl.BlockSpec((2, 1, s), lambda h: (h, 0, 0))] + [ANY] * ng,
        out_shape=[jax.ShapeDtypeStruct((s, DIL_W), BF16), jax.ShapeDtypeStruct((HEADS, 1, s), F32)] + _Gather.out_shapes(gather),
        scratch_shapes=[pltpu.VMEM((LANE, s), BF16), pltpu.VMEM((2, 2, tk, tq), F32)]
        + ([] if mla else [pltpu.VMEM((NEAR_OFFSETS, tk, tq), F32)] * 2) + (_Gather.scratch(gather) if ng else []),
        compiler_params=_params(("arbitrary",) if ng else ("parallel",), 12 << 20),
    )(*_in_hbm(q, k, v), *gather)


def _attn_bwd(q, k, v, o, do, do_block0, lse, mla, scale, name, scatter=()):
    s = q.shape[0]
    qw = 2 * LANE if mla else LANE
    tq, tk = ATT_TQ, ATT_TK
    nq = s // tq
    reach, kind_near, kind_far = _block_kinds(mla)
    assert s % tq == 0 and tq % tk == 0
    ns = len(scatter)
    last_step = HEADS // 2 - 1

    def body(*refs):
        q_ref, k_ref, v_ref, o_ref, do_ref, lse_ref = refs[:6]
        dq_ref, dk_ref, dv_ref = refs[6 + ns:9 + ns]
        kt_ref, dot_ref, dob_ref, dqt_ref, delta_ref, lse2_ref = refs[9 + 2 * ns:15 + 2 * ns]
        near_tabs = None if mla else refs[15 + 2 * ns:17 + 2 * ns]
        n_tabs = 0 if mla else 2
        comm = (refs[6:6 + ns], refs[9 + ns:9 + 2 * ns]) + tuple(refs[15 + n_tabs + 2 * ns:])
        if ns:
            @pl.when(pl.program_id(0) == 0)
            def _():
                _Scatter(*comm).start()

        lane = lax.broadcasted_iota(I32, (1, LANE), 1)
        row = lax.broadcasted_iota(I32, (LANE, 1), 0)
        rel_t = lax.broadcasted_iota(I32, (tk, tq), 1) - lax.broadcasted_iota(I32, (tk, tq), 0)
        if not mla:
            _fill_near_tables(*near_tabs, rel_t)

        def prepare(j, carry):
            c0 = pl.multiple_of(j * tk, tk)
            do_blk = do_ref[pl.ds(c0, tk), :]
            dob_ref[pl.ds(c0, tk), :] = do_blk.astype(BF16)
            do_t = do_blk.T
            dot_ref[:, pl.ds(c0, tk)] = do_t.astype(BF16)
            prod = do_t * o_ref[pl.ds(c0, tk), :].astype(F32).T
            delta_ref[0, :, pl.ds(c0, tk)] = jnp.sum(prod[0:DIL_DIM], axis=0, keepdims=True)
            delta_ref[1, :, pl.ds(c0, tk)] = jnp.sum(prod[DIL_DIM:LANE], axis=0, keepdims=True)
            for w in range(qw // LANE):
                kt_ref[w * LANE:(w + 1) * LANE, pl.ds(c0, tk)] = (
                    k_ref[pl.ds(c0, tk), w * LANE:(w + 1) * LANE].astype(F32).T.astype(BF16))
            return carry

        lax.fori_loop(0, s // tk, prepare, 0)
        dqt_ref[...] = jnp.zeros_like(dqt_ref)
        lse2_ref[...] = lse_ref[...] * LOG2E

        sels = [lane < DIL_DIM, lane >= DIL_DIM]
        rsels = [row < DIL_DIM, row >= DIL_DIM]
        cols = [slice(a * LANE, (a + 1) * LANE) if mla else slice(0, LANE) for a in range(2)]

        def k_block(kj, carry):
            c0 = pl.multiple_of(kj * tk, tk)
            kas = [k_ref[pl.ds(c0, tk), cols[a]] for a in range(2)]
            kts = [kt_ref[cols[a], pl.ds(c0, tk)] for a in range(2)]
            if not mla:
                kas = [jnp.where(sels[a], kas[a], jnp.zeros_like(kas[a])) for a in range(2)]
                kts = [jnp.where(rsels[a], kts[a], jnp.zeros_like(kts[a])) for a in range(2)]
            vb = v_ref[pl.ds(c0, tk), :]
            vbs = [jnp.where(sels[a], vb, jnp.zeros_like(vb)) for a in range(2)]

            first = c0 // tq

            def q_block(qi, c, kind):
                r0 = pl.multiple_of(qi * tq, tq)
                out, dq_parts = [], []
                for a in range(2):
                    dk_acc, dv_acc = c[a]
                    qa = q_ref[pl.ds(r0, tq), cols[a]]
                    st, cnt = _scores_t(kas[a], qa, scale, kind, rel_t, r0 - c0, near_tabs)
                    p = jnp.exp2(st - lse2_ref[a, :, pl.ds(r0, tq)])
                    if cnt is not None:
                        p = p * cnt
                    dp = jnp.dot(vbs[a], dot_ref[:, pl.ds(r0, tq)], preferred_element_type=F32)
                    ds = (p * (dp - delta_ref[a, :, pl.ds(r0, tq)]) * scale).astype(BF16)
                    dv_acc = dv_acc + jnp.dot(p.astype(BF16), dob_ref[pl.ds(r0, tq), :], preferred_element_type=F32)
                    dk_acc = dk_acc + jnp.dot(ds, qa, preferred_element_type=F32)
                    dq_parts.append(jnp.dot(kts[a], ds, preferred_element_type=F32))
                    out.append((dk_acc, dv_acc))
                if mla:
                    for a in range(2):
                        dqt_ref[cols[a], pl.ds(r0, tq)] += dq_parts[a]
                else:
                    dqt_ref[:, pl.ds(r0, tq)] += dq_parts[0] + dq_parts[1]
                return tuple(out)

            zero = jnp.zeros((tk, LANE), F32)
            last_near = jnp.minimum((c0 + tk - 1 + reach) // tq + 1, nq)
            c = lax.fori_loop(first, last_near, functools.partial(q_block, kind=kind_near), ((zero, zero), (zero, zero)))
            (dk0, dv0), (dk1, dv1) = lax.fori_loop(last_near, nq, functools.partial(q_block, kind=kind_far), c)
            if mla:
                dk_ref[pl.ds(c0, tk), cols[0]] = dk0
                dk_ref[pl.ds(c0, tk), cols[1]] = dk1
            else:
                dk_ref[pl.ds(c0, tk), :] = jnp.where(sels[0], dk0, dk1)
            dv_ref[pl.ds(c0, tk), :] = jnp.where(sels[0], dv0, dv1)
            return carry

        lax.fori_loop(0, s // tk, k_block, 0)

        def write_dq(j, carry):
            c0 = pl.multiple_of(j * tk, tk)
            for w in range(qw // LANE):
                dq_ref[pl.ds(c0, tk), w * LANE:(w + 1) * LANE] = dqt_ref[w * LANE:(w + 1) * LANE, pl.ds(c0, tk)].T
            return carry

        lax.fori_loop(0, s // tk, write_dq, 0)

        if ns:
            @pl.when(pl.program_id(0) == last_step)
            def _():
                _Scatter(*comm).finish()

    b0 = do_block0
    return pl.pallas_call(
        body, name=name, grid=(HEADS // 2,),
        in_specs=[pl.BlockSpec((s, qw), lambda h: (0, h)), pl.BlockSpec((s, qw), lambda h: (0, h)),
                  pl.BlockSpec((s, LANE), lambda h: (0, h)), pl.BlockSpec((s, LANE), lambda h: (0, h)),
                  pl.BlockSpec((s, LANE), lambda h: (0, h + b0)), pl.BlockSpec((2, 1, s), lambda h: (h, 0, 0))] + [ANY] * ns,
        out_specs=[pl.BlockSpec((s, qw), lambda h: (0, h)), pl.BlockSpec((s, qw), lambda h: (0, h)),
                   pl.BlockSpec((s, LANE), lambda h: (0, h))] + [ANY] * ns,
        out_shape=[jax.ShapeDtypeStruct(q.shape, F32), jax.ShapeDtypeStruct(k.shape, F32), jax.ShapeDtypeStruct((s, DIL_W), F32)]
        + _Scatter.out_shapes(scatter),
        scratch_shapes=[pltpu.VMEM((qw, s), BF16), pltpu.VMEM((LANE, s), BF16), pltpu.VMEM((s, LANE), BF16),
                        pltpu.VMEM((qw, s), F32), pltpu.VMEM((2, 1, s), F32), pltpu.VMEM((2, 1, s), F32)]
        + ([] if mla else [pltpu.VMEM((NEAR_OFFSETS, tk, tq), F32)] * 2) + (_Scatter.semaphores(ns) if ns else []),
        compiler_params=_params(("arbitrary",) if ns else ("parallel",), 24 << 20),
    )(*_in_hbm(q, k, v, o, do, lse), *scatter)


def _ada_bwd(c_all, dmod_shard):
    n, d = c_all.shape
    cols = dmod_shard.shape[1]

    def body(c_ref, g_ref, o_ref):
        cv = c_ref[...]
        o_ref[...] = lax.dot_general(cv * _sigmoid(cv), g_ref[...], TN, precision=HIGHEST, preferred_element_type=F32)

    return pl.pallas_call(
        body, name="ada_bwd", out_shape=jax.ShapeDtypeStruct((d, cols), F32),
        compiler_params=_params(None, 16 << 20),
    )(c_all, dmod_shard)


SMALL_WIDTHS = (("g_mix_norm", D_MODEL), ("g_q_lat", Q_LORA), ("g_kv_lat", KV_LORA), ("g_mla_q_nope", NOPE),
                ("g_mla_q_pe", ROPE), ("g_mla_k_nope", NOPE), ("g_mla_k_pe", ROPE), ("g_dil_q", DIL_DIM),
                ("g_dil_k", DIL_DIM), ("g_ffn_norm", D_MODEL), ("b_conv", UP_W))


def _small_layout():
    pieces = (("dmod", 6 * D_MODEL),) + SMALL_WIDTHS + tuple(("w_conv%d" % k, UP_W) for k in range(3)) + (("loss", 1),)
    layout, off = {}, 0
    for name, width in pieces:
        layout[name] = (width, off)
        off += -(-width // LANE) * LANE
    return layout, off


def _pack_small(acc1, acc2, dg2, dglat, dgains, dbg, dbv, dwg, dwv, loss_part):
    layout, total = _small_layout()

    def body(a1, a2, g2, gl, gg, bg, bv, wg, wv, ls, o_ref):
        o_ref[...] = jnp.zeros_like(o_ref)

        def put(name, src, shift=0):
            start = layout[name][1] + shift
            o_ref[:, start:start + src.shape[1]] = src

        for k, src in enumerate((a1[0:1, :], a1[1:2, :], a2[3:4, :], a2[0:1, :], a2[1:2, :], g2[...])):
            put("dmod", src, k * D_MODEL)
        put("g_mix_norm", a1[2:3, :])
        put("g_q_lat", gl[0:1, :])
        put("g_kv_lat", gl[1:2, 0:KV_LORA])
        put("g_mla_q_nope", gg[0:1, 0:NOPE])
        put("g_mla_q_pe", gg[5:6, 0:ROPE])
        put("g_mla_k_nope", gg[1:2, 0:NOPE])
        put("g_mla_k_pe", gg[2:3, 0:ROPE])
        put("g_dil_q", gg[3:4, 0:DIL_DIM])
        put("g_dil_k", gg[4:5, 0:DIL_DIM])
        put("g_ffn_norm", a2[2:3, :])
        put("b_conv", bg[...])
        put("b_conv", bv[...], D_FF)
        for k in range(3):
            put("w_conv%d" % k, wg[k:k + 1, :])
            put("w_conv%d" % k, wv[k:k + 1, :], D_FF)
        put("loss", ls[...])

    ins = (acc1, acc2, dg2, dglat, dgains, dbg, dbv, dwg, dwv, loss_part)
    return pl.pallas_call(
        body, name="pack_small", grid=(1,), in_specs=[_full(a.shape) for a in ins], out_specs=_full((1, total)),
        out_shape=jax.ShapeDtypeStruct((1, total), F32),
        compiler_params=_params(("arbitrary",), 2 << 20),
    )(*_in_hbm(*ins))


def _sum_unpack(g):
    n_dev, _, total = g.shape
    layout, _ = _small_layout()

    def body(g_ref, *refs):
        o_refs, s_ref = refs[:-1], refs[-1]
        acc = g_ref[0]
        for k in range(1, n_dev):
            acc = acc + g_ref[k]
        s_ref[...] = acc
        take = lambda name: s_ref[:, layout[name][1]:layout[name][1] + layout[name][0]]
        o_refs[0][...] = take("dmod")
        for i, (name, _) in enumerate(SMALL_WIDTHS):
            o_refs[1 + i][...] = take(name)
        for k in range(3):
            o_refs[-2][k:k + 1, :] = take("w_conv%d" % k)
        o_refs[-1][...] = take("loss")

    shapes = [(1, 6 * D_MODEL)] + [(1, w) for _, w in SMALL_WIDTHS] + [(3, UP_W), (1, 1)]
    return pl.pallas_call(
        body, name="sum_unpack", out_shape=[jax.ShapeDtypeStruct(sh, F32) for sh in shapes],
        scratch_shapes=[pltpu.VMEM((1, total), F32)],
        compiler_params=_params(None, 4 << 20),
    )(g)


def _adamw_math(w, g, m, v):
    mn = ADAM_B1 * m + (1.0 - ADAM_B1) * g
    vn = ADAM_B2 * v + (1.0 - ADAM_B2) * (g * g)
    m_hat = mn / (1.0 - ADAM_B1 ** ADAM_STEP)
    v_hat = vn / (1.0 - ADAM_B2 ** ADAM_STEP)
    return -ADAM_LR * (m_hat / (jnp.sqrt(v_hat) + ADAM_EPS) + ADAM_WD * w), mn, vn


def _adamw_vectors(ws, gs, ms, vs):
    k = len(ws)

    def body(*refs):
        for i in range(k):
            d, mn, vn = _adamw_math(refs[i][...], refs[k + i][...], refs[2 * k + i][...], refs[3 * k + i][...])
            refs[4 * k + i][...] = d
            refs[5 * k + i][...] = mn
            refs[6 * k + i][...] = vn

    blocks = [_full(w.shape) for w in ws]
    outs = pl.pallas_call(
        body, name="adamw_vectors", grid=(1,), in_specs=blocks * 4, out_specs=blocks * 3,
        out_shape=[jax.ShapeDtypeStruct(w.shape, F32) for w in ws] * 3,
        compiler_params=_params(("arbitrary",), 2 << 20),
    )(*_in_hbm(*ws, *gs, *ms, *vs))
    return outs[:k], outs[k:2 * k], outs[2 * k:]


def _adamw(w, g, m, v, name):
    r, c = w.shape
    tr = r
    for cand in (256, 128, 64, 32, 16, 8):
        if r % cand == 0 and r > cand:
            tr = cand
            break

    def body(w_ref, g_ref, m_ref, v_ref, d_ref, mo_ref, vo_ref):
        d_ref[...], mo_ref[...], vo_ref[...] = _adamw_math(w_ref[...], g_ref[...], m_ref[...], v_ref[...])

    blk = pl.BlockSpec((tr, c), lambda i: (i, 0))
    return pl.pallas_call(
        body, name=name, grid=(r // tr,), in_specs=[blk] * 4, out_specs=[blk] * 3,
        out_shape=[jax.ShapeDtypeStruct((r, c), F32)] * 3,
        compiler_params=_params(("parallel",), 7 * _nbytes((tr, c), F32)),
    )(w, g, m, v)


def _position():
    return lax.axis_index("x"), lax.axis_index("y"), lax.axis_index("c")


def _other_chips(x, y):
    return [(1 - x, y, 2 * (1 - x) + y), (x, 1 - y, 2 * x + (1 - y)), (1 - x, 1 - y, 2 * (1 - x) + (1 - y))]


class _SmallGather:
    def __init__(self, v_ref, out_ref, send_sems, recv_sems, local_sem):
        x, y, c = _position()
        me = 4 * x + 2 * y + c
        self.local = pltpu.make_async_copy(v_ref, out_ref.at[me], local_sem)
        self.sends, self.arrivals = [], []
        for k in range(N_DEV - 1):
            fx, fy, fc = ((k + 1) >> 2) & 1, ((k + 1) >> 1) & 1, (k + 1) & 1
            px, py, pc = (1 - x if fx else x), (1 - y if fy else y), (1 - c if fc else c)

            def copy(dst, k=k, peer=(px, py, pc)):
                return pltpu.make_async_remote_copy(src_ref=v_ref, dst_ref=dst, send_sem=send_sems.at[k],
                                                    recv_sem=recv_sems.at[k], device_id=peer, device_id_type=MESH)

            self.sends.append(copy(out_ref.at[me]))
            self.arrivals.append(copy(out_ref.at[4 * px + 2 * py + pc]))

    @staticmethod
    def semaphores():
        return [pltpu.SemaphoreType.DMA((N_DEV - 1,)), pltpu.SemaphoreType.DMA((N_DEV - 1,)), pltpu.SemaphoreType.DMA]

    def start(self):
        self.local.start()
        for cp in self.sends:
            cp.start()

    def finish(self):
        for cp in self.arrivals:
            cp.wait_recv()
        for cp in self.sends:
            cp.wait_send()
        self.local.wait()


def _prologue(c_taps, w_ada_shard, b_shard, pos_col, rope_consts, shards):
    n = len(shards)
    s = pos_col.shape[0]
    cols = w_ada_shard.shape[1]
    freq, csel, ssel = rope_consts

    def body(*refs):
        ct_ref, w_ref, b_ref, p_ref, f_ref, cs_ref, ss_ref = refs[:7]
        sh_refs = refs[7:7 + n]
        ct_all_ref, mod_all_ref, tab_ref = refs[7 + n:10 + n]
        g_refs = refs[10 + n:10 + 2 * n]
        mod_blk_ref = refs[10 + 2 * n]
        sems = refs[11 + 2 * n:]
        weights = _Gather(sh_refs, g_refs, *sems[6:])
        weights.start()
        first = _SmallGather(ct_ref, ct_all_ref, *sems[0:3])
        first.start()
        first.finish()
        cv = ct_all_ref[:, 0, 0:D_MODEL]
        sc = (cv * _sigmoid(cv)).astype(BF16)
        mod_blk_ref[...] = jnp.dot(sc, w_ref[...].astype(BF16), preferred_element_type=F32) + b_ref[...]
        second = _SmallGather(mod_blk_ref, mod_all_ref, *sems[3:6])
        second.start()

        def table_rows(i, carry):
            r0 = pl.multiple_of(i * ROW_TILE, ROW_TILE)
            ang = p_ref[pl.ds(r0, ROW_TILE), :].astype(F32) * f_ref[...]
            tab_ref[pl.ds(r0, ROW_TILE), :] = cs_ref[...] * jnp.cos(ang) + ss_ref[...] * jnp.sin(ang)
            return carry

        lax.fori_loop(0, s // ROW_TILE, table_rows, 0)
        second.finish()
        weights.forward()
        weights.finish()

    return pl.pallas_call(
        body, name="prologue",
        out_shape=[jax.ShapeDtypeStruct((N_DEV,) + c_taps.shape, F32), jax.ShapeDtypeStruct((N_DEV, N_DEV, cols), F32),
                   jax.ShapeDtypeStruct((s, 4 * LANE), F32)] + _Gather.out_shapes(shards),
        in_specs=[IN_VMEM] * 7 + [ANY] * n, out_specs=[IN_VMEM] * 3 + [ANY] * n,
        scratch_shapes=[pltpu.VMEM((N_DEV, cols), F32)] + _SmallGather.semaphores() * 2 + _Gather.scratch(shards),
        compiler_params=_params(None, 14 << 20),
    )(c_taps, w_ada_shard, b_shard, pos_col, freq, csel, ssel, *shards)


IN_VMEM = pl.BlockSpec(memory_space=pltpu.VMEM)
ANY = pl.BlockSpec(memory_space=pl.ANY)


class _Gather:
    def __init__(self, w_refs, out_refs, send_sems, recv_sems, own_sems, *bounce_refs):
        x, y, c = _position()
        q0 = 2 * x + y
        sibling = (x, y, 1 - c)
        self.ici, self.ici_in, self.fwd, self.fwd_in, self.own_in, self.own_out = [], [], [], [], [], []
        for k, (w_ref, out_ref) in enumerate(zip(w_refs, out_refs)):
            half = w_ref.shape[0] // 2
            self.own_in.append(pltpu.make_async_copy(w_ref, bounce_refs[k], own_sems.at[2 * k]))
            self.own_out.append(pltpu.make_async_copy(bounce_refs[k], out_ref.at[q0], own_sems.at[2 * k + 1]))

            def blk(q, e, out_ref=out_ref, half=half):
                return out_ref.at[q, pl.ds(pl.multiple_of(e * half, 16), half), :]

            def copy(src, dst, i, to):
                return pltpu.make_async_remote_copy(src_ref=src, dst_ref=dst, send_sem=send_sems.at[i], recv_sem=recv_sems.at[i],
                                                    device_id=to, device_id_type=MESH)

            src = w_ref.at[pl.ds(pl.multiple_of(c * half, 16), half), :]
            for j, (cx, cy, qj) in enumerate(_other_chips(x, y)):
                self.ici.append(copy(src, blk(q0, c), 6 * k + j, (cx, cy, c)))
                self.ici_in.append(copy(blk(qj, c), blk(qj, c), 6 * k + j, (cx, cy, c)))
                self.fwd.append(copy(blk(qj, c), blk(qj, c), 6 * k + 3 + j, sibling))
                self.fwd_in.append(copy(blk(qj, 1 - c), blk(qj, 1 - c), 6 * k + 3 + j, sibling))

    @staticmethod
    def out_shapes(shards):
        return [jax.ShapeDtypeStruct((N_CHIP,) + s.shape, s.dtype) for s in shards]

    @staticmethod
    def scratch(shards):
        n = len(shards)
        return ([pltpu.SemaphoreType.DMA((6 * n,)), pltpu.SemaphoreType.DMA((6 * n,)), pltpu.SemaphoreType.DMA((2 * n,))]
                + [pltpu.VMEM(s.shape, s.dtype) for s in shards])

    def start(self):
        for cp in self.ici + self.own_in:
            cp.start()

    def forward(self):
        for fetched, placed in zip(self.own_in, self.own_out):
            fetched.wait()
            placed.start()
        for arrived, onward in zip(self.ici_in, self.fwd):
            arrived.wait_recv()
            onward.start()

    def finish(self):
        for cp in self.fwd_in:
            cp.wait_recv()
        for cp in self.ici + self.fwd:
            cp.wait_send()
        for cp in self.own_out:
            cp.wait()


def _swap_halves_d2d(grads, name):
    n = len(grads)

    def body(*refs):
        swap = _PairSwap(refs[:n], refs[n:2 * n], *refs[2 * n:])
        swap.start()
        swap.finish()

    return pl.pallas_call(
        body, name=name,
        out_shape=_PairSwap.out_shapes(grads), in_specs=[ANY] * n, out_specs=[ANY] * n,
        scratch_shapes=_PairSwap.semaphores(n),
    )(*grads)


class _PairSwap:
    def __init__(self, g_refs, out_refs, send_sems, recv_sems):
        x, y, c = _position()
        self.copies = [
            pltpu.make_async_remote_copy(src_ref=g_ref.at[:, 1 - c], dst_ref=out_ref, send_sem=send_sems.at[k],
                                         recv_sem=recv_sems.at[k], device_id=(x, y, 1 - c), device_id_type=MESH)
            for k, (g_ref, out_ref) in enumerate(zip(g_refs, out_refs))]

    @staticmethod
    def out_shapes(grads):
        return [jax.ShapeDtypeStruct((N_CHIP,) + g.shape[2:], g.dtype) for g in grads]

    @staticmethod
    def semaphores(n):
        return [pltpu.SemaphoreType.DMA((n,)), pltpu.SemaphoreType.DMA((n,))]

    def start(self):
        for cp in self.copies:
            cp.start()

    def finish(self):
        for cp in self.copies:
            cp.wait_recv()
        for cp in self.copies:
            cp.wait_send()


def _pair_sum(g, a, c_idx, name):
    _, _, rh, cols = g.shape
    tr = rh
    for cand in (256, 128, 64, 32, 16):
        if rh % cand == 0 and rh > cand:
            tr = cand
            break

    def body(c_ref, g_ref, a_ref, o_ref):
        o_ref[...] = (g_ref[...] + a_ref[...]).astype(BF16)

    return pl.pallas_call(
        body, name=name,
        grid_spec=pltpu.PrefetchScalarGridSpec(
            num_scalar_prefetch=1, grid=(N_CHIP, rh // tr),
            in_specs=[pl.BlockSpec((None, None, tr, cols), lambda q, i, c_ref: (q, c_ref[0], i, 0)),
                      pl.BlockSpec((None, tr, cols), lambda q, i, c_ref: (q, i, 0))],
            out_specs=pl.BlockSpec((None, tr, cols), lambda q, i, c_ref: (q, i, 0))),
        out_shape=jax.ShapeDtypeStruct((N_CHIP, rh, cols), BF16),
        compiler_params=_params(("parallel", "parallel"), 10 * _nbytes((tr, cols), F32)),
    )(c_idx, g, a)


def _scatter_and_gather(parts, small, name):
    n = len(parts)

    def body(*refs):
        scatter = _Scatter(refs[:n], refs[n + 1:2 * n + 1], *refs[2 * n + 2:2 * n + 4])
        gather = _SmallGather(refs[n], refs[2 * n + 1], *refs[2 * n + 4:])
        scatter.start()
        gather.start()
        gather.finish()
        scatter.finish()

    return pl.pallas_call(
        body, name=name,
        out_shape=_Scatter.out_shapes(parts) + [jax.ShapeDtypeStruct((N_DEV,) + small.shape, F32)],
        in_specs=[ANY] * n + [IN_VMEM], out_specs=[ANY] * n + [IN_VMEM],
        scratch_shapes=_Scatter.semaphores(n) + _SmallGather.semaphores(),
        compiler_params=_params(None, 10 * _nbytes(small.shape, F32)),
    )(*parts, small)


class _Scatter:
    def __init__(self, p_refs, out_refs, send_sems, recv_sems):
        x, y, c = _position()
        self.copies = []
        for k, (p_ref, out_ref) in enumerate(zip(p_refs, out_refs)):
            for j, (cx, cy, qj) in enumerate(_other_chips(x, y)):
                self.copies.append(pltpu.make_async_remote_copy(
                    src_ref=p_ref.at[qj], dst_ref=out_ref.at[j], send_sem=send_sems.at[3 * k + j],
                    recv_sem=recv_sems.at[3 * k + j], device_id=(cx, cy, c), device_id_type=MESH))

    @staticmethod
    def out_shapes(parts):
        return [jax.ShapeDtypeStruct((3,) + p.shape[1:], p.dtype) for p in parts]

    @staticmethod
    def semaphores(n):
        return [pltpu.SemaphoreType.DMA((3 * n,)), pltpu.SemaphoreType.DMA((3 * n,))]

    def start(self):
        for cp in self.copies:
            cp.start()

    def finish(self):
        for cp in self.copies:
            cp.wait_recv()
        for cp in self.copies:
            cp.wait_send()


def _shard_sum(p, b, qc_idx, name):
    _, rh, cols = p.shape
    tr = rh
    for cand in (256, 128, 64, 32, 16):
        if rh % cand == 0 and rh > cand:
            tr = cand
            break

    def body(qc_ref, p_ref, b_ref, o_ref):
        acc = p_ref[...].astype(F32)
        for j in range(3):
            acc = acc + b_ref[j].astype(F32)
        o_ref[...] = acc

    return pl.pallas_call(
        body, name=name,
        grid_spec=pltpu.PrefetchScalarGridSpec(
            num_scalar_prefetch=1, grid=(rh // tr,),
            in_specs=[pl.BlockSpec((None, tr, cols), lambda i, qc_ref: (qc_ref[0], i, 0)),
                      pl.BlockSpec((3, tr, cols), lambda i, qc_ref: (0, i, 0))],
            out_specs=pl.BlockSpec((None, tr, cols), lambda i, qc_ref: (qc_ref[1], i, 0))),
        out_shape=jax.ShapeDtypeStruct((2, rh, cols), F32),
        compiler_params=_params(("parallel",), 8 * _nbytes((tr, cols), F32)),
    )(qc_idx, p, b)


def _join_halves(shards):
    n = len(shards)

    def body(*refs):
        out_refs = refs[n:2 * n]
        send_sems, recv_sems = refs[2 * n:]
        x, y, c = _position()
        cps = [pltpu.make_async_remote_copy(src_ref=out_refs[k].at[c], dst_ref=out_refs[k].at[c], send_sem=send_sems.at[k],
                                            recv_sem=recv_sems.at[k], device_id=(x, y, 1 - c), device_id_type=MESH)
               for k in range(n)]
        for cp in cps:
            cp.start()
        for k in range(n):
            arriving = out_refs[k].at[1 - c]
            pltpu.make_async_remote_copy(src_ref=arriving, dst_ref=arriving, send_sem=send_sems.at[k], recv_sem=recv_sems.at[k],
                                         device_id=(x, y, 1 - c), device_id_type=MESH).wait_recv()
        for cp in cps:
            cp.wait_send()

    return pl.pallas_call(
        body, name="rs_join",
        out_shape=[jax.ShapeDtypeStruct(a.shape, a.dtype) for a in shards],
        in_specs=[ANY] * n, out_specs=[ANY] * n, input_output_aliases={k: k for k in range(n)},
        scratch_shapes=[pltpu.SemaphoreType.DMA((n,)), pltpu.SemaphoreType.DMA((n,))],
    )(*shards)


def _cols_from_shards(g):
    q, r, cs = g.shape
    return jnp.transpose(g, (1, 0, 2)).reshape(r, q * cs)


def _cols_to_shards(w):
    r, cfull = w.shape
    return jnp.transpose(w.reshape(r, N_CHIP, cfull // N_CHIP), (1, 0, 2))


def _pad_w_in(w):
    z = lambda n: jnp.zeros((w.shape[0], n), w.dtype)
    q_lat, kv_lat, kpe = w[:, 0:512], w[:, 512:768], w[:, 768:800]
    qd, kd, vd = w[:, 800:1312], w[:, 1312:1824], w[:, 1824:2336]
    return jnp.concatenate([q_lat, qd, kd, vd, kv_lat, z(KPE_OFF), kpe, z(LANE - KPE_OFF - ROPE)], axis=1)


def _unpad_w_in(g):
    return jnp.concatenate([g[:, P_QLAT:P_QLAT + Q_LORA], g[:, P_KVLAT:P_KVLAT + KV_LORA],
                            g[:, P_KPE + KPE_OFF:P_KPE + KPE_OFF + ROPE], g[:, P_QD:P_QD + 3 * DIL_W]], axis=1)


def _pad_w_qb(w):
    w3 = w.reshape(Q_LORA, HEADS, NOPE + ROPE)
    return jnp.pad(w3, ((0, 0), (0, 0), (0, LANE - NOPE - ROPE))).reshape(Q_LORA, HEADS * LANE)


def _unpad_w_qb(g):
    return g.reshape(Q_LORA, HEADS, LANE)[:, :, :NOPE + ROPE].reshape(Q_LORA, HEADS * (NOPE + ROPE))


def _pad_w_kvb(w):
    w3 = w.reshape(KV_LORA, HEADS, 2 * NOPE)
    kp = jnp.pad(w3[:, :, :NOPE], ((0, 0), (0, 0), (0, LANE - NOPE))).reshape(KV_LORA, HEADS * LANE)
    return jnp.concatenate([kp, w3[:, :, NOPE:].reshape(KV_LORA, DIL_W)], axis=1)


def _unpad_w_kvb(g):
    gk = g[:, :HEADS * LANE].reshape(KV_LORA, HEADS, LANE)[:, :, :NOPE]
    gv = g[:, HEADS * LANE:].reshape(KV_LORA, HEADS, NOPE)
    return jnp.concatenate([gk, gv], axis=2).reshape(KV_LORA, HEADS * 2 * NOPE)


def _head_gains(g_q_nope, g_q_pe, g_k_nope, g_k_pe, g_dq, g_dk):
    z = lambda n: jnp.zeros((1, n), F32)
    q1 = jnp.concatenate([g_q_nope, g_q_pe, z(LANE - NOPE - ROPE)], axis=1)
    k1 = jnp.concatenate([g_k_nope, z(LANE - NOPE)], axis=1)
    kpe = jnp.concatenate([z(KPE_OFF), g_k_pe, z(LANE - KPE_OFF - ROPE)], axis=1)
    return dict(q=jnp.tile(q1, (1, HEADS)), k=jnp.tile(k1, (1, HEADS)), kpe=kpe,
                dq=jnp.tile(g_dq, (1, HEADS)), dk=jnp.tile(g_dk, (1, HEADS)))


def kernel(x, c, positions, w_ada, b_ada, g_mix_norm, w_in, g_q_lat, w_q_b, g_kv_lat, w_kv_b, g_mla_q_nope, g_mla_q_pe, g_mla_k_nope, g_mla_k_pe, g_dil_q, g_dil_k, w_o, g_ffn_norm, w_up, w_conv, b_conv, w_down, loss_target, m_w_ada, m_b_ada, m_g_mix_norm, m_w_in, m_g_q_lat, m_w_q_b, m_g_kv_lat, m_w_kv_b, m_g_mla_q_nope, m_g_mla_q_pe, m_g_mla_k_nope, m_g_mla_k_pe, m_g_dil_q, m_g_dil_k, m_w_o, m_g_ffn_norm, m_w_up, m_w_conv, m_b_conv, m_w_down, v_w_ada, v_b_ada, v_g_mix_norm, v_w_in, v_g_q_lat, v_w_q_b, v_g_kv_lat, v_w_kv_b, v_g_mla_q_nope, v_g_mla_q_pe, v_g_mla_k_nope, v_g_mla_k_pe, v_g_dil_q, v_g_dil_k, v_w_o, v_g_ffn_norm, v_w_up, v_w_conv, v_b_conv, v_w_down):
    args = dict(locals())
    weights = {n: args[n][0] for n in ("w_ada", "w_in", "w_q_b", "w_kv_b", "w_o", "w_up", "w_conv", "w_down")}
    small_w = {n: args[n] for n in ("b_ada",) + tuple(n for n, _ in SMALL_WIDTHS)}
    mom_m = {n[2:]: (args[n][0] if args[n].ndim == 3 else args[n]) for n in args if n.startswith("m_")}
    mom_v = {n[2:]: (args[n][0] if args[n].ndim == 3 else args[n]) for n in args if n.startswith("v_")}

    xi, yi, ci = _position()
    q0 = 2 * xi + yi
    me = 4 * xi + 2 * yi + ci
    xs, tgt = x[0], loss_target[0]
    s = xs.shape[0]
    consts = _seg_consts()
    c_idx, qc_idx = jnp.reshape(ci, (1,)).astype(I32), jnp.stack([q0, ci]).astype(I32)

    def halves(g4):
        q, r, cc = g4.shape
        return g4.reshape(q, 2, r // 2, cc)

    own_first = [weights[n].astype(BF16) for n in ("w_in", "w_q_b", "w_kv_b")]
    own_later = [weights[n].astype(BF16) for n in ("w_o", "w_up", "w_down")]
    conv_cols = UP_W // N_CHIP
    ada_cols = w_ada.shape[2]
    b_shard = lax.dynamic_slice_in_dim(b_ada, q0 * ada_cols, ada_cols, axis=1)
    c_taps = jnp.concatenate([c, weights["w_conv"].reshape(1, 3 * conv_cols)], axis=1)
    c_taps_all, mod_all, tab, *gathered = _prologue(c_taps, weights["w_ada"], b_shard, positions.reshape(s, 1),
                                                    _rope_consts(), own_first)
    c_all = c_taps_all[:, 0, :D_MODEL]
    w_conv_f = c_taps_all[:, 0, D_MODEL:].reshape(N_CHIP, 2, 3, conv_cols)[:, 0]
    w_conv_f = jnp.transpose(w_conv_f, (1, 0, 2)).reshape(3, UP_W)
    mod_all = mod_all.reshape(N_CHIP, 2, N_DEV, ada_cols)
    mod = lax.dynamic_index_in_dim(lax.dynamic_index_in_dim(mod_all, ci, 1, False), me, 1, False)
    mod = mod.reshape(1, N_CHIP * ada_cols)
    sh1, sc1, g1, sh2, sc2, g2 = [mod[:, k * D_MODEL:(k + 1) * D_MODEL] for k in range(6)]
    w_in_p = _pad_w_in(_cols_from_shards(gathered[0]))
    w_qb_p = _pad_w_qb(_cols_from_shards(gathered[1]))
    w_kvb_p = _pad_w_kvb(_cols_from_shards(gathered[2]))
    gains = _head_gains(g_mla_q_nope, g_mla_q_pe, g_mla_k_nope, g_mla_k_pe, g_dil_q, g_dil_k)

    h = _prenorm(xs, g_mix_norm, sc1, sh1, "prenorm")
    proj = _mm(h, w_in_p, "nn", F32, 512, P_COLS, "mm_in")
    ql, kvl = _latnorm(proj, g_q_lat, g_kv_lat)
    q_raw = _mm(ql, w_qb_p, "nn", F32, 512, HEADS * LANE, "mm_qb")
    kv_raw = _mm(kvl, w_kvb_p, "nn", F32, 512, HEADS * LANE + DIL_W, "mm_kvb")
    qm, km, vm, qd, kd, vd = _attn_prep(q_raw, kv_raw, proj, tab, gains, consts)
    scale_m, scale_d = (NOPE + ROPE) ** -0.5, DIL_DIM ** -0.5
    o_m, lse_m, got_up = _attn_fwd(qm, km, vm, True, scale_m, "attn_mla", gather=own_later[1:2])
    o_d, lse_d, got_o, got_down = _attn_fwd(qd, kd, vd, False, scale_d, "attn_dil", gather=[own_later[0], own_later[2]])
    gathered = [got_o, got_up, got_down]
    w_o_f = gathered[0].reshape(D_MODEL, D_MODEL)
    w_up_f = _cols_from_shards(gathered[1])
    w_down_f = gathered[2].reshape(D_FF, D_MODEL)
    mix_in = jnp.concatenate([o_m, o_d], axis=1)
    mix = _mm(mix_in, w_o_f, "nn", F32, 512, D_MODEL, "mm_o")
    x1, h2 = _resid_prenorm(xs, mix, g1, g_ffn_norm, sc2, sh2)
    up = _mm(h2, w_up_f, "nn", F32, 512, CONV_TILE, "mm_up")
    act = _conv_gate(up, w_conv_f, b_conv)
    ffn = _mm(act, w_down_f, "nn", F32, 256, D_MODEL, "mm_down")
    dy, dffn, dg2, loss_part = _final(x1, ffn, tgt, g2)

    da = _mm(dffn, w_down_f, "nt", F32, 512, CONV_TILE, "mm_down_dx")
    gw_down = _mm(act, dffn, "tn", F32, 256, D_MODEL, "mm_down_dw")
    dup_g, dup_v, dbg, dbv, dwg, dwv = _gate_bwd(up, da, w_conv_f, b_conv)
    dup = jnp.concatenate([dup_g, dup_v], axis=1)
    early_names = ("w_up", "w_down", "w_o")
    gw_up = _mm(h2, dup, "tn", F32, 512, CONV_TILE, "mm_up_dw", col_shards=True)
    early = [halves(gw_up), halves(gw_down.reshape(N_CHIP, D_FF // N_CHIP, D_MODEL))]
    dh2, *early_sib = _mm(dup, w_up_f, "nt", F32, 256, 512, "mm_up_dx", swap=early, b_outer=True)
    dx1, dmix, acc2 = _ffnnorm_bwd(dh2, x1, dy, mix, g_ffn_norm, sc2, g1)
    gw_o = _mm(mix_in, dmix, "tn", F32, 512, D_MODEL, "mm_o_dw")
    early.append(halves(gw_o.reshape(N_CHIP, D_MODEL // N_CHIP, D_MODEL)))
    dmix_in, sib_o = _mm(dmix, w_o_f, "nt", F32, 512, D_MODEL, "mm_o_dx", swap=early[2:])
    early_sib.append(sib_o)
    early_sums = [_pair_sum(g, a, c_idx, "pair_sum_" + n) for g, a, n in zip(early, early_sib, early_names)]
    dqm, dkm, dvm, *early_recv = _attn_bwd(qm, km, vm, o_m, dmix_in, 0, lse_m, True, scale_m, "attn_mla_bwd",
                                           scatter=early_sums[:1])
    dqd, dkd, dvd, *early_recv_d = _attn_bwd(qd, kd, vd, o_d, dmix_in, DIL_W // LANE, lse_d, False, scale_d,
                                             "attn_dil_bwd", scatter=early_sums[1:])
    early_recv = early_recv + early_recv_d
    dq_raw, dkv_raw, dkpe_b, dqd_b, dkd_b, dvd_b, dgains = _attn_prep_bwd(
        dqm, dkm, dvm, dqd, dkd, dvd, q_raw, kv_raw, proj, tab, gains, consts)
    dql = _mm(dq_raw, w_qb_p, "nt", F32, 512, Q_LORA, "mm_qb_dx")
    gw_qb = _unpad_w_qb(_mm(ql, dq_raw, "tn", F32, Q_LORA, HEADS * LANE, "mm_qb_dw"))
    dkvl = _mm(dkv_raw, w_kvb_p, "nt", F32, 512, KV_LORA, "mm_kvb_dx")
    gw_kvb = _unpad_w_kvb(_mm(kvl, dkv_raw, "tn", F32, KV_LORA, HEADS * LANE + DIL_W, "mm_kvb_dw"))
    dqlat_b, dkvlat_b, dglat = _latnorm_bwd(dql, dkvl, proj, g_q_lat, g_kv_lat)
    dproj = jnp.concatenate([dqlat_b, dqd_b, dkd_b, dvd_b, dkvlat_b, dkpe_b], axis=1)
    dh = _mm(dproj, w_in_p, "nt", F32, 512, D_MODEL, "mm_in_dx")
    gw_in = _unpad_w_in(_mm(h, dproj, "tn", F32, 512, P_COLS, "mm_in_dw"))
    grad_x, acc1 = _mixnorm_bwd(dh, xs, dx1, g_mix_norm, sc1)

    packed = _pack_small(acc1, acc2, dg2, dglat, dgains, dbg, dbv, dwg, dwv, loss_part)
    late_names = ("w_in", "w_q_b", "w_kv_b")
    late = [halves(_cols_to_shards(gw_in)), halves(_cols_to_shards(gw_qb)), halves(_cols_to_shards(gw_kvb))]
    late_sib = _swap_halves_d2d(late, "rs_pair_swap_late")
    late_sums = [_pair_sum(g, a, c_idx, "pair_sum_" + n) for g, a, n in zip(late, late_sib, late_names)]
    *late_recv, gathered_small = _scatter_and_gather(late_sums, packed, "rs_scatter_late")

    grad_b_ada, *small_grads, gconv_full, loss_sum = _sum_unpack(gathered_small)
    grads = {"b_ada": grad_b_ada}
    grads.update({n: g for (n, _), g in zip(SMALL_WIDTHS, small_grads)})
    shard_cols = UP_W // N_CHIP
    grads["w_conv"] = lax.dynamic_slice_in_dim(gconv_full, q0 * shard_cols, shard_cols, axis=1)
    dmod_all = gathered_small[:, 0, :6 * D_MODEL]
    grads["w_ada"] = _ada_bwd(c_all, lax.dynamic_slice_in_dim(dmod_all, q0 * ada_cols, ada_cols, axis=1))

    big_names = late_names + early_names
    half_sums = [_shard_sum(p, b, qc_idx, "shard_sum_" + n)
                 for p, b, n in zip(late_sums + early_sums, list(late_recv) + list(early_recv), big_names)]
    for n, full in zip(big_names, _join_halves(half_sums)):
        grads[n] = full.reshape(2 * full.shape[1], full.shape[2])

    delta, new_m, new_v = {}, {}, {}
    for n in ("w_ada", "w_in", "w_q_b", "w_kv_b", "w_o", "w_up", "w_conv", "w_down"):
        operands = (weights[n], grads[n], mom_m[n], mom_v[n])
        if n == "w_ada":
            operands = _in_hbm(*operands)
        delta[n], new_m[n], new_v[n] = _adamw(*operands, "adamw_" + n)
    vec_names = ("b_ada",) + tuple(n for n, _ in SMALL_WIDTHS)
    sd, sm, sv = _adamw_vectors(*[[d_[n] for n in vec_names] for d_ in (small_w, grads, mom_m, mom_v)])
    for k, n in enumerate(vec_names):
        delta[n], new_m[n], new_v[n] = sd[k], sm[k], sv[k]

    loss = loss_sum[0, 0]
    order = ("w_ada", "b_ada", "g_mix_norm", "w_in", "g_q_lat", "w_q_b", "g_kv_lat", "w_kv_b", "g_mla_q_nope", "g_mla_q_pe",
             "g_mla_k_nope", "g_mla_k_pe", "g_dil_q", "g_dil_k", "w_o", "g_ffn_norm", "w_up", "w_conv", "b_conv", "w_down")
    lead = lambda n, z: z[None] if n.startswith("w_") else z
    outs = [loss, grad_x[None]]
    for d_ in (grads, delta, new_m, new_v):
        outs += [lead(n, d_[n]) for n in order]
    return tuple(outs)
```

```python
import functools

import numpy as np
import jax
import jax.numpy as jnp
from jax import lax
from jax.experimental import pallas as pl
from jax.experimental.pallas import tpu as pltpu

F32 = jnp.float32
BF16 = jnp.bfloat16
I32 = jnp.int32

D_MODEL = 1024
HEADS = 8
NOPE = 64
ROPE = 32
Q_LORA = 512
KV_LORA = 256
DIL_DIM = 64
DIL_W = HEADS * DIL_DIM
D_FF = 2816
UP_W = 2 * D_FF
IN_COLS = Q_LORA + KV_LORA + ROPE + 3 * DIL_W
ROPE_THETA = 10000.0
EPS = 1e-6
NEG_INF = -1e30
N_DEV = 8
N_CHIP = 4

ADAM_LR = 0.001
ADAM_B1 = 0.9
ADAM_B2 = 0.999
ADAM_EPS = 1e-08
ADAM_WD = 0.01
ADAM_STEP = 10

LANE = 128
ROW_TILE = 256
ATT_TQ = 512
ATT_TK = 256
LOG2E = 1.4426950408889634
LN2 = 0.6931471805599453
VMEM_CAP = 56 * 1024 * 1024
VMEM_FLOOR = 32 * 1024 * 1024

P_QLAT, P_QD, P_KD, P_VD, P_KVLAT, P_KPE = 0, 512, 1024, 1536, 2048, 2304
P_COLS = 2432
KPE_OFF = 64

NN = (((1,), (0,)), ((), ()))
NT = (((1,), (1,)), ((), ()))
TN = (((0,), (0,)), ((), ()))
HIGHEST = lax.Precision.HIGHEST
MESH = pl.DeviceIdType.MESH


def _params(sem=None, est_bytes=0):
    limit = int(min(max(2 * est_bytes + (4 << 20), VMEM_FLOOR), VMEM_CAP))
    if sem is None:
        return pltpu.CompilerParams(vmem_limit_bytes=limit)
    return pltpu.CompilerParams(dimension_semantics=sem, vmem_limit_bytes=limit)


def _nbytes(shape, dtype):
    return int(np.prod(shape)) * jnp.dtype(dtype).itemsize


def _in_hbm(*xs):
    return [pltpu.with_memory_space_constraint(x, pltpu.HBM) for x in xs]


def _mm(a, b, dims, out_dtype, tm, tn, name, col_shards=False, swap=(), b_outer=False):
    def spec(block, index):
        if b_outer:
            return pl.BlockSpec(block, lambda g0, g1: index(g1, g0))
        return pl.BlockSpec(block, index)

    if dims == "nn":
        (m, k), (k2, n) = a.shape, b.shape
        a_spec = spec((tm, k), lambda i, j: (i, 0))
        b_spec = spec((k, tn), lambda i, j: (0, j))
        dn = NN
    elif dims == "nt":
        (m, k), (n, k2) = a.shape, b.shape
        a_spec = spec((tm, k), lambda i, j: (i, 0))
        b_spec = spec((tn, k), lambda i, j: (j, 0))
        dn = NT
    else:
        (k, m), (k2, n) = a.shape, b.shape
        a_spec = spec((k, tm), lambda i, j: (0, i))
        b_spec = spec((k, tn), lambda i, j: (0, j))
        dn = TN
    assert k == k2 and m % tm == 0 and n % tn == 0, (name, a.shape, b.shape, tm, tn)

    nw = len(swap)
    grid = (n // tn, m // tm) if b_outer else (m // tm, n // tn)

    def body(*refs):
        a_ref, b_ref, o_ref = refs[0], refs[1], refs[2 + nw]
        comm = (refs[2:2 + nw], refs[3 + nw:3 + 2 * nw]) + tuple(refs[3 + 2 * nw:])
        if nw:
            @pl.when((pl.program_id(0) == 0) & (pl.program_id(1) == 0))
            def _():
                _PairSwap(*comm).start()

        o_ref[...] = lax.dot_general(a_ref[...], b_ref[...], dn, preferred_element_type=F32).astype(o_ref.dtype)

        if nw:
            @pl.when((pl.program_id(0) == grid[0] - 1) & (pl.program_id(1) == grid[1] - 1))
            def _():
                _PairSwap(*comm).finish()

    est = _nbytes((tm, k), a.dtype) + _nbytes((tn, k), b.dtype) + _nbytes((tm, tn), F32) + _nbytes((tm, tn), out_dtype)
    if col_shards:
        out_spec = spec((None, tm, tn), lambda i, j: (j, i, 0))
        out_shape = jax.ShapeDtypeStruct((n // tn, m, tn), out_dtype)
    else:
        out_spec = spec((tm, tn), lambda i, j: (i, j))
        out_shape = jax.ShapeDtypeStruct((m, n), out_dtype)
    out = pl.pallas_call(
        body, name=name, grid=grid,
        in_specs=[a_spec, b_spec] + [ANY] * nw,
        out_specs=[out_spec] + [ANY] * nw,
        out_shape=[out_shape] + _PairSwap.out_shapes(swap),
        scratch_shapes=_PairSwap.semaphores(nw) if nw else [],
        compiler_params=_params(("arbitrary", "arbitrary") if nw else ("parallel", "parallel"), est),
    )(a, b, *swap)
    return out if nw else out[0]


def _seg_consts():
    seg_q = np.zeros((HEADS * LANE, LANE), np.float32)
    inv_q = np.zeros((1, LANE), np.float32)
    seg_k = np.zeros((HEADS * LANE, LANE), np.float32)
    inv_k = np.zeros((1, LANE), np.float32)
    seg_d = np.zeros((DIL_W, LANE), np.float32)
    inv_d = np.zeros((1, LANE), np.float32)
    for h in range(HEADS):
        seg_q[h * LANE:h * LANE + NOPE, 2 * h] = 1.0
        seg_q[h * LANE + NOPE:h * LANE + NOPE + ROPE, 2 * h + 1] = 1.0
        inv_q[0, 2 * h], inv_q[0, 2 * h + 1] = 1.0 / NOPE, 1.0 / ROPE
        seg_k[h * LANE:h * LANE + NOPE, h] = 1.0
        inv_k[0, h] = 1.0 / NOPE
        seg_d[h * DIL_DIM:(h + 1) * DIL_DIM, h] = 1.0
        inv_d[0, h] = 1.0 / DIL_DIM
    fold_q = np.tile(np.eye(LANE, dtype=np.float32), (HEADS, 1))
    fold_d = np.zeros((DIL_W, LANE), np.float32)
    fold_d[np.arange(DIL_W), np.arange(DIL_W) % DIL_DIM] = 1.0
    j = lambda v: jnp.asarray(v)
    b = lambda v: jnp.asarray(v, dtype=BF16)
    return dict(seg_q=b(seg_q), exp_q=b(seg_q.T.copy()), inv_q=j(inv_q), seg_k=b(seg_k), exp_k=b(seg_k.T.copy()),
                inv_k=j(inv_k), seg_d=b(seg_d), exp_d=b(seg_d.T.copy()), inv_d=j(inv_d), fold_q=j(fold_q), fold_d=j(fold_d))


def _rope_consts():
    inv_d = jnp.power(ROPE_THETA, -2.0 * jnp.arange(DIL_DIM // 2, dtype=F32) / DIL_DIM)
    inv_q = jnp.power(ROPE_THETA, -2.0 * jnp.arange(ROPE // 2, dtype=F32) / ROPE)
    lanes = np.arange(LANE)
    freq_d = inv_d[lanes % (DIL_DIM // 2)]
    in_pe = (lanes >= KPE_OFF) & (lanes < KPE_OFF + ROPE)
    freq_q = jnp.where(jnp.asarray(in_pe), inv_q[(lanes - KPE_OFF) % (ROPE // 2)], 0.0)
    sign_d = np.where(lanes % DIL_DIM < DIL_DIM // 2, -1.0, 1.0).astype(np.float32)
    sign_q = np.where(in_pe, np.where((lanes - KPE_OFF) < ROPE // 2, -1.0, 1.0), 0.0).astype(np.float32)
    zeros, ones = np.zeros(LANE, np.float32), np.ones(LANE, np.float32)
    freq = jnp.concatenate([freq_d, freq_d, freq_q, freq_q])[None, :]
    csel = jnp.asarray(np.concatenate([ones, zeros, ones, zeros]))[None, :]
    ssel = jnp.asarray(np.concatenate([zeros, sign_d, zeros, sign_q]))[None, :]
    return freq, csel, ssel


def _full(shape):
    return pl.BlockSpec(shape, lambda *_: (0,) * len(shape))


def _tile_lanes(x, n):
    return jnp.concatenate([x] * n, axis=1)


def _rms(x):
    return lax.rsqrt(jnp.mean(x * x, axis=-1, keepdims=True) + EPS)


def _prenorm(x, gain, scale, shift, name):
    s, d = x.shape

    def body(x_ref, g_ref, sc_ref, sh_ref, h_ref):
        xv = x_ref[...]
        h = (xv * _rms(xv)) * g_ref[...] * (1.0 + sc_ref[...]) + sh_ref[...]
        h_ref[...] = h.astype(BF16)

    row = pl.BlockSpec((ROW_TILE, d), lambda i: (i, 0))
    return pl.pallas_call(
        body, name=name, grid=(s // ROW_TILE,),
        in_specs=[row, _full((1, d)), _full((1, d)), _full((1, d))],
        out_specs=row, out_shape=jax.ShapeDtypeStruct((s, d), BF16),
        compiler_params=_params(("parallel",)),
    )(x, gain, scale, shift)


def _latnorm(proj, g_q, g_kv):
    s = proj.shape[0]

    def body(q_ref, kv_ref, gq_ref, gkv_ref, ql_ref, kvl_ref):
        q, kv = q_ref[...], kv_ref[...]
        ql_ref[...] = ((q * _rms(q)) * gq_ref[...]).astype(BF16)
        kvl_ref[...] = ((kv * _rms(kv)) * gkv_ref[...]).astype(BF16)

    return pl.pallas_call(
        body, name="latnorm", grid=(s // ROW_TILE,),
        in_specs=[pl.BlockSpec((ROW_TILE, Q_LORA), lambda i: (i, P_QLAT // Q_LORA)),
                  pl.BlockSpec((ROW_TILE, KV_LORA), lambda i: (i, P_KVLAT // KV_LORA)),
                  _full((1, Q_LORA)), _full((1, KV_LORA))],
        out_specs=[pl.BlockSpec((ROW_TILE, Q_LORA), lambda i: (i, 0)), pl.BlockSpec((ROW_TILE, KV_LORA), lambda i: (i, 0))],
        out_shape=[jax.ShapeDtypeStruct((s, Q_LORA), BF16), jax.ShapeDtypeStruct((s, KV_LORA), BF16)],
        compiler_params=_params(("parallel",)),
    )(proj, proj, g_q, g_kv)


def _dot01(v, mat01):
    hi = v.astype(BF16)
    lo = (v - hi.astype(F32)).astype(BF16)
    return jnp.dot(hi, mat01, preferred_element_type=F32) + jnp.dot(lo, mat01, preferred_element_type=F32)


def _seg_rinv(x, seg, exp, inv):
    r = lax.rsqrt(_dot01(x * x, seg) * inv + EPS)
    return _dot01(r, exp)


def _seg_mean(v, seg, exp, inv):
    return _dot01(_dot01(v, seg) * inv, exp)


def _swap_halves(x, half):
    n = x.shape[1]
    lane = lax.broadcasted_iota(I32, (1, n), 1)
    first = (lane & (2 * half - 1)) < half
    return jnp.where(first, pltpu.roll(x, n - half, 1), pltpu.roll(x, half, 1))


def _rope(x, cos, sin_signed, half):
    return x * cos + _swap_halves(x, half) * sin_signed


def _rope_bwd(dy, cos, sin_signed, half):
    return dy * cos + _swap_halves(dy * sin_signed, half)


def _pe_lane_mask(n):
    lane = lax.broadcasted_iota(I32, (1, n), 1) & (LANE - 1)
    return (lane >= KPE_OFF) & (lane < KPE_OFF + ROPE)


def _attn_prep(q_raw, kv_raw, proj, tab, gains, consts):
    s = q_raw.shape[0]
    hw = HEADS * LANE

    def body(q_ref, kv_ref, kpe_ref, qd_ref, kd_ref, vd_ref, tab_ref,
             gq_ref, gk_ref, gkpe_ref, gdq_ref, gdk_ref,
             segq_ref, expq_ref, invq_ref, segk_ref, expk_ref, invk_ref, segd_ref, expd_ref, invd_ref,
             qm_ref, km_ref, vm_ref, qdo_ref, kdo_ref, vdo_ref):
        tab_v = tab_ref[...]
        cos_d, sin_d = _tile_lanes(tab_v[:, 0:LANE], DIL_W // LANE), _tile_lanes(tab_v[:, LANE:2 * LANE], DIL_W // LANE)
        cos_q1, sin_q1 = tab_v[:, 2 * LANE:3 * LANE], tab_v[:, 3 * LANE:4 * LANE]
        cos_q, sin_q = _tile_lanes(cos_q1, HEADS), _tile_lanes(sin_q1, HEADS)

        q = q_ref[...]
        qn = q * _seg_rinv(q, segq_ref[...], expq_ref[...], invq_ref[...]) * gq_ref[...]
        qm_ref[...] = _rope(qn, cos_q, sin_q, ROPE // 2).astype(BF16)

        kv = kv_ref[...]
        kp = kv[:, :hw]
        kn = kp * _seg_rinv(kp, segk_ref[...], expk_ref[...], invk_ref[...]) * gk_ref[...]
        kpe = kpe_ref[...]
        r_pe = lax.rsqrt(jnp.sum(kpe * kpe, axis=-1, keepdims=True) * (1.0 / ROPE) + EPS)
        kpe_r = _rope(kpe * r_pe * gkpe_ref[...], cos_q1, sin_q1, ROPE // 2)
        km_ref[...] = (kn + _tile_lanes(kpe_r, HEADS)).astype(BF16)
        vm_ref[...] = kv[:, hw:].astype(BF16)

        qd = qd_ref[...]
        qdn = qd * _seg_rinv(qd, segd_ref[...], expd_ref[...], invd_ref[...]) * gdq_ref[...]
        qdo_ref[...] = _rope(qdn, cos_d, sin_d, DIL_DIM // 2).astype(BF16)
        kd = kd_ref[...]
        kdn = kd * _seg_rinv(kd, segd_ref[...], expd_ref[...], invd_ref[...]) * gdk_ref[...]
        kdo_ref[...] = _rope(kdn, cos_d, sin_d, DIL_DIM // 2).astype(BF16)
        vdo_ref[...] = vd_ref[...].astype(BF16)

    t = ROW_TILE
    row = lambda w, cb=0: pl.BlockSpec((t, w), lambda i: (i, cb))
    c = consts
    return pl.pallas_call(
        body, name="attn_prep", grid=(s // t,),
        in_specs=[row(hw), row(hw + DIL_W), row(LANE, P_KPE // LANE), row(DIL_W, P_QD // DIL_W), row(DIL_W, P_KD // DIL_W),
                  row(DIL_W, P_VD // DIL_W), row(4 * LANE),
                  _full((1, hw)), _full((1, hw)), _full((1, LANE)), _full((1, DIL_W)), _full((1, DIL_W)),
                  _full((hw, LANE)), _full((LANE, hw)), _full((1, LANE)), _full((hw, LANE)), _full((LANE, hw)), _full((1, LANE)),
                  _full((DIL_W, LANE)), _full((LANE, DIL_W)), _full((1, LANE))],
        out_specs=[row(hw), row(hw), row(DIL_W), row(DIL_W), row(DIL_W), row(DIL_W)],
        out_shape=[jax.ShapeDtypeStruct((s, hw), BF16), jax.ShapeDtypeStruct((s, hw), BF16)]
        + [jax.ShapeDtypeStruct((s, DIL_W), BF16)] * 4,
        compiler_params=_params(("parallel",), 24 << 20),
    )(*_in_hbm(q_raw, kv_raw, proj, proj, proj, proj, tab), gains["q"], gains["k"], gains["kpe"], gains["dq"], gains["dk"],
      c["seg_q"], c["exp_q"], c["inv_q"], c["seg_k"], c["exp_k"], c["inv_k"], c["seg_d"], c["exp_d"], c["inv_d"])


def _attn_prep_bwd(dqm, dkm, dvm, dqd, dkd, dvd, q_raw, kv_raw, proj, tab, gains, consts):
    s = q_raw.shape[0]
    hw = HEADS * LANE
    n_steps = s // ROW_TILE

    def body(dqm_ref, dkm_ref, dvm_ref, dqd_ref, dkd_ref, dvd_ref, q_ref, kv_ref, kpe_ref, qd_ref, kd_ref, tab_ref,
             gq_ref, gk_ref, gkpe_ref, gdq_ref, gdk_ref,
             segq_ref, expq_ref, invq_ref, segk_ref, expk_ref, invk_ref, segd_ref, expd_ref, invd_ref, foldq_ref, foldd_ref,
             dq_ref, dkv_ref, dkpe_ref, dqdo_ref, dkdo_ref, dvdo_ref, dg_ref, acc_ref):
        i = pl.program_id(0)

        @pl.when(i == 0)
        def _():
            acc_ref[...] = jnp.zeros_like(acc_ref)

        tab_v = tab_ref[...]
        cos_d, sin_d = _tile_lanes(tab_v[:, 0:LANE], DIL_W // LANE), _tile_lanes(tab_v[:, LANE:2 * LANE], DIL_W // LANE)
        cos_q1, sin_q1 = tab_v[:, 2 * LANE:3 * LANE], tab_v[:, 3 * LANE:4 * LANE]
        cos_q, sin_q = _tile_lanes(cos_q1, HEADS), _tile_lanes(sin_q1, HEADS)

        def norm_bwd(x, dyg, gain, seg, exp, inv):
            rinv = _seg_rinv(x, seg, exp, inv)
            xn = x * rinv
            dxn = dyg * gain
            dx = rinv * (dxn - xn * _seg_mean(dxn * xn, seg, exp, inv))
            return dx, jnp.sum(dyg * xn, axis=0, keepdims=True)

        dq, gq_l = norm_bwd(q_ref[...], _rope_bwd(dqm_ref[...], cos_q, sin_q, ROPE // 2), gq_ref[...],
                            segq_ref[...], expq_ref[...], invq_ref[...])
        dq_ref[...] = dq.astype(BF16)

        dkm = dkm_ref[...]
        kv = kv_ref[...]
        dkp, gk_l = norm_bwd(kv[:, :hw], dkm, gk_ref[...], segk_ref[...], expk_ref[...], invk_ref[...])
        dkv_ref[:, :hw] = dkp.astype(BF16)
        dkv_ref[:, hw:] = dvm_ref[...].astype(BF16)

        dkpe_r = dkm[:, 0:LANE]
        for h in range(1, HEADS):
            dkpe_r = dkpe_r + dkm[:, h * LANE:(h + 1) * LANE]
        dkpe_r = jnp.where(_pe_lane_mask(LANE), dkpe_r, 0.0)
        dyg = _rope_bwd(dkpe_r, cos_q1, sin_q1, ROPE // 2)
        kpe = kpe_ref[...]
        r_pe = lax.rsqrt(jnp.sum(kpe * kpe, axis=-1, keepdims=True) * (1.0 / ROPE) + EPS)
        xn = kpe * r_pe
        dxn = dyg * gkpe_ref[...]
        dkpe = r_pe * (dxn - xn * (jnp.sum(dxn * xn, axis=-1, keepdims=True) * (1.0 / ROPE)))
        dkpe_ref[...] = dkpe.astype(BF16)
        gkpe_l = jnp.sum(dyg * xn, axis=0, keepdims=True)

        dqd_v, gdq_l = norm_bwd(qd_ref[...], _rope_bwd(dqd_ref[...], cos_d, sin_d, DIL_DIM // 2), gdq_ref[...],
                                segd_ref[...], expd_ref[...], invd_ref[...])
        dqdo_ref[...] = dqd_v.astype(BF16)
        dkd_v, gdk_l = norm_bwd(kd_ref[...], _rope_bwd(dkd_ref[...], cos_d, sin_d, DIL_DIM // 2), gdk_ref[...],
                                segd_ref[...], expd_ref[...], invd_ref[...])
        dkdo_ref[...] = dkd_v.astype(BF16)
        dvdo_ref[...] = dvd_ref[...].astype(BF16)

        acc_ref[0:1, :] += gq_l
        acc_ref[1:2, :] += gk_l
        acc_ref[2:3, 0:LANE] += gkpe_l
        acc_ref[3:4, 0:DIL_W] += gdq_l
        acc_ref[4:5, 0:DIL_W] += gdk_l

        @pl.when(i == n_steps - 1)
        def _():
            acc = acc_ref[...]
            fq = jnp.dot(acc, foldq_ref[...], precision=HIGHEST, preferred_element_type=F32)
            fd = jnp.dot(acc[:, 0:DIL_W], foldd_ref[...], precision=HIGHEST, preferred_element_type=F32)
            rows = lax.broadcasted_iota(I32, (8, LANE), 0)
            base = jnp.where(rows < 2, fq, jnp.where(rows == 2, acc[:, 0:LANE], fd))
            at0 = pltpu.roll(base, LANE - KPE_OFF, 1)
            dg_ref[...] = jnp.where(rows == 5, pltpu.roll(at0, 5, 0), jnp.where(rows == 2, at0, base))

    t = ROW_TILE
    row = lambda w, cb=0: pl.BlockSpec((t, w), lambda i: (i, cb))
    c = consts
    return pl.pallas_call(
        body, name="attn_prep_bwd", grid=(n_steps,),
        in_specs=[row(hw), row(hw), row(DIL_W), row(DIL_W), row(DIL_W), row(DIL_W),
                  row(hw), row(hw + DIL_W), row(LANE, P_KPE // LANE), row(DIL_W, P_QD // DIL_W), row(DIL_W, P_KD // DIL_W),
                  row(4 * LANE),
                  _full((1, hw)), _full((1, hw)), _full((1, LANE)), _full((1, DIL_W)), _full((1, DIL_W)),
                  _full((hw, LANE)), _full((LANE, hw)), _full((1, LANE)), _full((hw, LANE)), _full((LANE, hw)), _full((1, LANE)),
                  _full((DIL_W, LANE)), _full((LANE, DIL_W)), _full((1, LANE)), _full((hw, LANE)), _full((DIL_W, LANE))],
        out_specs=[row(hw), row(hw + DIL_W), row(LANE), row(DIL_W), row(DIL_W), row(DIL_W), _full((8, LANE))],
        out_shape=[jax.ShapeDtypeStruct((s, hw), BF16), jax.ShapeDtypeStruct((s, hw + DIL_W), BF16),
                   jax.ShapeDtypeStruct((s, LANE), BF16)] + [jax.ShapeDtypeStruct((s, DIL_W), BF16)] * 3
        + [jax.ShapeDtypeStruct((8, LANE), F32)],
        scratch_shapes=[pltpu.VMEM((8, hw), F32)],
        compiler_params=_params(("arbitrary",), 28 << 20),
    )(*_in_hbm(dqm, dkm, dvm, dqd, dkd, dvd, q_raw, kv_raw, proj, proj, proj, tab),
      gains["q"], gains["k"], gains["kpe"], gains["dq"], gains["dk"],
      c["seg_q"], c["exp_q"], c["inv_q"], c["seg_k"], c["exp_k"], c["inv_k"], c["seg_d"], c["exp_d"], c["inv_d"],
      c["fold_q"], c["fold_d"])


def _latnorm_bwd(dql, dkvl, proj, g_q, g_kv):
    s = proj.shape[0]
    n_steps = s // ROW_TILE

    def body(dql_ref, dkvl_ref, q_ref, kv_ref, gq_ref, gkv_ref, dq_ref, dkv_ref, dg_ref):
        i = pl.program_id(0)

        @pl.when(i == 0)
        def _():
            dg_ref[...] = jnp.zeros_like(dg_ref)

        def one(x, dyg, gain):
            r = _rms(x)
            xn = x * r
            dxn = dyg * gain
            dx = r * (dxn - xn * jnp.mean(dxn * xn, axis=-1, keepdims=True))
            return dx, jnp.sum(dyg * xn, axis=0, keepdims=True)

        dq, gq_l = one(q_ref[...], dql_ref[...], gq_ref[...])
        dkv, gkv_l = one(kv_ref[...], dkvl_ref[...], gkv_ref[...])
        dq_ref[...] = dq.astype(BF16)
        dkv_ref[...] = dkv.astype(BF16)
        dg_ref[0:1, :] += gq_l
        dg_ref[1:2, 0:KV_LORA] += gkv_l

    t = ROW_TILE
    return pl.pallas_call(
        body, name="latnorm_bwd", grid=(n_steps,),
        in_specs=[pl.BlockSpec((t, Q_LORA), lambda i: (i, 0)), pl.BlockSpec((t, KV_LORA), lambda i: (i, 0)),
                  pl.BlockSpec((t, Q_LORA), lambda i: (i, P_QLAT // Q_LORA)),
                  pl.BlockSpec((t, KV_LORA), lambda i: (i, P_KVLAT // KV_LORA)),
                  _full((1, Q_LORA)), _full((1, KV_LORA))],
        out_specs=[pl.BlockSpec((t, Q_LORA), lambda i: (i, 0)), pl.BlockSpec((t, KV_LORA), lambda i: (i, 0)), _full((8, Q_LORA))],
        out_shape=[jax.ShapeDtypeStruct((s, Q_LORA), BF16), jax.ShapeDtypeStruct((s, KV_LORA), BF16),
                   jax.ShapeDtypeStruct((8, Q_LORA), F32)],
        compiler_params=_params(("arbitrary",)),
    )(dql, dkvl, proj, proj, g_q, g_kv)


def _resid_prenorm(x, mix, g1, gain, scale, shift):
    s, d = x.shape

    def body(x_ref, mix_ref, g1_ref, g_ref, sc_ref, sh_ref, x1_ref, h_ref):
        x1 = x_ref[...] + g1_ref[...] * mix_ref[...]
        x1_ref[...] = x1
        h_ref[...] = ((x1 * _rms(x1)) * g_ref[...] * (1.0 + sc_ref[...]) + sh_ref[...]).astype(BF16)

    row = pl.BlockSpec((ROW_TILE, d), lambda i: (i, 0))
    vec = _full((1, d))
    return pl.pallas_call(
        body, name="resid_prenorm", grid=(s // ROW_TILE,),
        in_specs=[row, row, vec, vec, vec, vec], out_specs=[row, row],
        out_shape=[jax.ShapeDtypeStruct((s, d), F32), jax.ShapeDtypeStruct((s, d), BF16)],
        compiler_params=_params(("parallel",)),
    )(x, mix, g1, gain, scale, shift)


CONV_TILE = 1408
HALO = 8


def _shift_down(x, halo, k):
    t = x.shape[0]
    row = lax.broadcasted_iota(I32, (t, 1), 0)
    out = pltpu.roll(x, k, 0)
    for r in range(k):
        out = jnp.where(row == r, halo[HALO - k + r:HALO - k + r + 1, :], out)
    return out


def _shift_up(x, halo, k):
    t = x.shape[0]
    row = lax.broadcasted_iota(I32, (t, 1), 0)
    out = pltpu.roll(x, t - k, 0)
    for r in range(k):
        out = jnp.where(row == t - k + r, halo[r:r + 1, :], out)
    return out


def _conv_fwd(x, halo, w, b):
    p1, p2 = _shift_down(x, halo, 1), _shift_down(x, halo, 2)
    u = b + p2 * w[0:1, :]
    u = u + p1 * w[1:2, :]
    u = u + x * w[2:3, :]
    return u, p1, p2


def _sigmoid(x):
    return 1.0 / (1.0 + jnp.exp(-x))


def _conv_gate(up, w_conv, b_conv):
    s = up.shape[0]
    t = ROW_TILE
    nj = D_FF // CONV_TILE
    hb = t // HALO

    def body(g_ref, v_ref, gh_ref, vh_ref, wg_ref, wv_ref, bg_ref, bv_ref, a_ref):
        live = (pl.program_id(0) > 0).astype(F32)
        ug, _, _ = _conv_fwd(g_ref[...], gh_ref[...] * live, wg_ref[...], bg_ref[...])
        uv, _, _ = _conv_fwd(v_ref[...], vh_ref[...] * live, wv_ref[...], bv_ref[...])
        a_ref[...] = (ug * _sigmoid(ug) * uv).astype(BF16)

    main = lambda off: pl.BlockSpec((t, CONV_TILE), lambda i, j: (i, j + off))
    halo = lambda off: pl.BlockSpec((HALO, CONV_TILE), lambda i, j: (jnp.maximum(i * hb - 1, 0), j + off))
    wsp = lambda off: pl.BlockSpec((3, CONV_TILE), lambda i, j: (0, j + off))
    bsp = lambda off: pl.BlockSpec((1, CONV_TILE), lambda i, j: (0, j + off))
    return pl.pallas_call(
        body, name="conv_gate", grid=(s // t, nj),
        in_specs=[main(0), main(nj), halo(0), halo(nj), wsp(0), wsp(nj), bsp(0), bsp(nj)],
        out_specs=pl.BlockSpec((t, CONV_TILE), lambda i, j: (i, j)),
        out_shape=jax.ShapeDtypeStruct((s, D_FF), BF16),
        compiler_params=_params(("parallel", "parallel"), 12 << 20),
    )(up, up, up, up, w_conv, w_conv, b_conv, b_conv)


def _gate_bwd(up, da, w_conv, b_conv):
    s = up.shape[0]
    t = ROW_TILE
    nj = D_FF // CONV_TILE
    hb = t // HALO
    n_i = s // t

    def body(g_ref, v_ref, gh_ref, vh_ref, gn_ref, vn_ref, da_ref, dan_ref, wg_ref, wv_ref, bg_ref, bv_ref,
             dupg_ref, dupv_ref, dbg_ref, dbv_ref, dwg_ref, dwv_ref):
        i = pl.program_id(1)

        @pl.when(i == 0)
        def _():
            for r in (dbg_ref, dbv_ref, dwg_ref, dwv_ref):
                r[...] = jnp.zeros_like(r)

        def d_gate(ug, uv, da_v):
            sg = _sigmoid(ug)
            return da_v * uv * (sg * (1.0 + ug * (1.0 - sg))), da_v * (ug * sg)

        live = (i > 0).astype(F32)
        xg, xv = g_ref[...], v_ref[...]
        wg, wv = wg_ref[...], wv_ref[...]
        ug, g1, g2 = _conv_fwd(xg, gh_ref[...] * live, wg, bg_ref[...])
        uv, v1, v2 = _conv_fwd(xv, vh_ref[...] * live, wv, bv_ref[...])
        dug, duv = d_gate(ug, uv, da_ref[...])

        more = (i < n_i - 1).astype(F32)
        ug_n, _, _ = _conv_fwd(gn_ref[...], xg[t - HALO:, :], wg, bg_ref[...])
        uv_n, _, _ = _conv_fwd(vn_ref[...], xv[t - HALO:, :], wv, bv_ref[...])
        dug_n, duv_n = d_gate(ug_n, uv_n, dan_ref[...] * more)

        def conv_t(du, du_n, w):
            return du * w[2:3, :] + _shift_up(du, du_n, 1) * w[1:2, :] + _shift_up(du, du_n, 2) * w[0:1, :]

        dupg_ref[...] = conv_t(dug, dug_n, wg).astype(BF16)
        dupv_ref[...] = conv_t(duv, duv_n, wv).astype(BF16)
        csum = lambda z: jnp.sum(z, axis=0, keepdims=True)
        dbg_ref[...] += csum(dug)
        dbv_ref[...] += csum(duv)
        dwg_ref[0:1, :] += csum(dug * g2)
        dwg_ref[1:2, :] += csum(dug * g1)
        dwg_ref[2:3, :] += csum(dug * xg)
        dwv_ref[0:1, :] += csum(duv * v2)
        dwv_ref[1:2, :] += csum(duv * v1)
        dwv_ref[2:3, :] += csum(duv * xv)

    last_halo = s // HALO - 1
    main = lambda off: pl.BlockSpec((t, CONV_TILE), lambda j, i: (i, j + off))
    halo = lambda off: pl.BlockSpec((HALO, CONV_TILE), lambda j, i: (jnp.maximum(i * hb - 1, 0), j + off))
    nxt = lambda off: pl.BlockSpec((HALO, CONV_TILE), lambda j, i: (jnp.minimum((i + 1) * hb, last_halo), j + off))
    wsp = lambda off: pl.BlockSpec((3, CONV_TILE), lambda j, i: (0, j + off))
    bsp = lambda off: pl.BlockSpec((1, CONV_TILE), lambda j, i: (0, j + off))
    outs = pl.pallas_call(
        body, name="gate_bwd", grid=(nj, n_i),
        in_specs=[main(0), main(nj), halo(0), halo(nj), nxt(0), nxt(nj), main(0), nxt(0),
                  wsp(0), wsp(nj), bsp(0), bsp(nj)],
        out_specs=[main(0), main(0),
                   pl.BlockSpec((1, CONV_TILE), lambda j, i: (0, j)), pl.BlockSpec((1, CONV_TILE), lambda j, i: (0, j)),
                   pl.BlockSpec((3, CONV_TILE), lambda j, i: (0, j)), pl.BlockSpec((3, CONV_TILE), lambda j, i: (0, j))],
        out_shape=[jax.ShapeDtypeStruct((s, D_FF), BF16), jax.ShapeDtypeStruct((s, D_FF), BF16),
                   jax.ShapeDtypeStruct((1, D_FF), F32), jax.ShapeDtypeStruct((1, D_FF), F32),
                   jax.ShapeDtypeStruct((3, D_FF), F32), jax.ShapeDtypeStruct((3, D_FF), F32)],
        compiler_params=_params(("parallel", "arbitrary"), 24 << 20),
    )(up, up, up, up, up, up, da, da, w_conv, w_conv, b_conv, b_conv)
    return outs


def _final(x1, ffn, tgt, g2):
    s, d = x1.shape
    n_steps = s // ROW_TILE

    def body(x1_ref, f_ref, t_ref, g2_ref, dy_ref, df_ref, dg2_ref, loss_ref, lacc_ref):
        i = pl.program_id(0)

        @pl.when(i == 0)
        def _():
            dg2_ref[...] = jnp.zeros_like(dg2_ref)
            lacc_ref[...] = jnp.zeros_like(lacc_ref)

        f = f_ref[...]
        e = x1_ref[...] + g2_ref[...] * f - t_ref[...]
        dy = e * (1.0 / d)
        dy_ref[...] = dy
        df_ref[...] = (dy * g2_ref[...]).astype(BF16)
        dg2_ref[...] += jnp.sum(dy * f, axis=0, keepdims=True)
        lacc_ref[...] += jnp.sum(e * e, axis=0, keepdims=True)

        @pl.when(i == n_steps - 1)
        def _():
            loss_ref[...] = jnp.sum(lacc_ref[...], axis=1, keepdims=True) * (0.5 / d)

    row = pl.BlockSpec((ROW_TILE, d), lambda i: (i, 0))
    return pl.pallas_call(
        body, name="final", grid=(n_steps,),
        in_specs=[row, row, row, _full((1, d))],
        out_specs=[row, row, _full((1, d)), _full((1, 1))],
        out_shape=[jax.ShapeDtypeStruct((s, d), F32), jax.ShapeDtypeStruct((s, d), BF16),
                   jax.ShapeDtypeStruct((1, d), F32), jax.ShapeDtypeStruct((1, 1), F32)],
        scratch_shapes=[pltpu.VMEM((1, d), F32)],
        compiler_params=_params(("arbitrary",)),
    )(x1, ffn, tgt, g2)


def _ffnnorm_bwd(dh2, x1, dy, mix, gain, scale, g1):
    s, d = x1.shape
    n_steps = s // ROW_TILE

    def body(dh_ref, x_ref, dy_ref, mix_ref, g_ref, sc_ref, g1_ref, dx_ref, dm_ref, acc_ref):
        i = pl.program_id(0)

        @pl.when(i == 0)
        def _():
            acc_ref[...] = jnp.zeros_like(acc_ref)

        dh, x = dh_ref[...], x_ref[...]
        r = _rms(x)
        xn = x * r
        dn = dh * (1.0 + sc_ref[...])
        dxn = dn * g_ref[...]
        dx = dy_ref[...] + r * (dxn - xn * jnp.mean(dxn * xn, axis=-1, keepdims=True))
        dx_ref[...] = dx
        dm_ref[...] = (dx * g1_ref[...]).astype(BF16)
        csum = lambda z: jnp.sum(z, axis=0, keepdims=True)
        acc_ref[0:1, :] += csum(dh)
        acc_ref[1:2, :] += csum(dh * (xn * g_ref[...]))
        acc_ref[2:3, :] += csum(dn * xn)
        acc_ref[3:4, :] += csum(dx * mix_ref[...])

    row = pl.BlockSpec((ROW_TILE, d), lambda i: (i, 0))
    vec = _full((1, d))
    return pl.pallas_call(
        body, name="ffnnorm_bwd", grid=(n_steps,),
        in_specs=[row, row, row, row, vec, vec, vec],
        out_specs=[row, row, _full((8, d))],
        out_shape=[jax.ShapeDtypeStruct((s, d), F32), jax.ShapeDtypeStruct((s, d), BF16), jax.ShapeDtypeStruct((8, d), F32)],
        compiler_params=_params(("arbitrary",)),
    )(dh2, x1, dy, mix, gain, scale, g1)


def _mixnorm_bwd(dh, x, dx1, gain, scale):
    s, d = x.shape
    n_steps = s // ROW_TILE

    def body(dh_ref, x_ref, dx1_ref, g_ref, sc_ref, gx_ref, acc_ref):
        i = pl.program_id(0)

        @pl.when(i == 0)
        def _():
            acc_ref[...] = jnp.zeros_like(acc_ref)

        dh, x = dh_ref[...], x_ref[...]
        r = _rms(x)
        xn = x * r
        dn = dh * (1.0 + sc_ref[...])
        dxn = dn * g_ref[...]
        gx_ref[...] = dx1_ref[...] + r * (dxn - xn * jnp.mean(dxn * xn, axis=-1, keepdims=True))
        csum = lambda z: jnp.sum(z, axis=0, keepdims=True)
        acc_ref[0:1, :] += csum(dh)
        acc_ref[1:2, :] += csum(dh * (xn * g_ref[...]))
        acc_ref[2:3, :] += csum(dn * xn)

    row = pl.BlockSpec((ROW_TILE, d), lambda i: (i, 0))
    vec = _full((1, d))
    return pl.pallas_call(
        body, name="mixnorm_bwd", grid=(n_steps,),
        in_specs=[row, row, row, vec, vec],
        out_specs=[row, _full((8, d))],
        out_shape=[jax.ShapeDtypeStruct((s, d), F32), jax.ShapeDtypeStruct((8, d), F32)],
        compiler_params=_params(("arbitrary",)),
    )(dh, x, dx1, gain, scale)


def _key_count(d, dilated):
    if not dilated:
        return jnp.where(d >= 0, 1.0, 0.0)
    one = lambda cond: jnp.where(cond, 1.0, 0.0)
    cnt = one(d <= 128) + one(((d & 3) == 0) & (d <= 512)) + one((d & 15) == 0)
    return jnp.where(d >= 0, cnt, 0.0)


def _block_kinds(mla):
    return (0, "diag", "none") if mla else (512, "near", "far")


NEAR_OFFSETS = 4


def _scores_t(ka, qa, scale, kind, rel_t, offset, near_tabs=None):
    return _mask_scores(lax.dot_general(ka, qa, NT, preferred_element_type=F32), scale, kind, rel_t, offset, near_tabs)


def _fill_near_tables(bias_ref, cnt_ref, rel_t):
    for idx in range(NEAR_OFFSETS):
        cnt = _key_count(rel_t + (idx - 1) * ATT_TK, True)
        cnt_ref[idx] = cnt
        bias_ref[idx] = jnp.where(cnt > 0.0, 0.0, NEG_INF)


def _mask_scores(products, scale, kind, rel_t, offset, near_tabs=None):
    st = products * (scale * LOG2E)
    cnt = None
    if kind == "diag":
        st = jnp.where(rel_t + offset >= 0, st, NEG_INF)
    elif kind == "far":
        st = jnp.where((rel_t & 15) == 0, st, NEG_INF)
    elif kind == "near":
        bias_ref, cnt_ref = near_tabs
        idx = offset // ATT_TK + 1
        st = st + bias_ref[idx]
        cnt = cnt_ref[idx]
    return st, cnt


def _attn_fwd(q, k, v, mla, scale, name, gather=()):
    s = q.shape[0]
    qw = 2 * LANE if mla else LANE
    tq, tk = ATT_TQ, ATT_TK
    reach, kind_near, kind_far = _block_kinds(mla)
    assert s % tq == 0 and tq % tk == 0 and reach % tk == 0 and (mla or (reach + tq) // tk == NEAR_OFFSETS)
    ng = len(gather)
    last_step = HEADS // 2 - 1

    def body(*refs):
        q_ref, k_ref, v_ref = refs[:3]
        o_ref, lse_ref = refs[3 + ng:5 + ng]
        vt_ref, st_ref = refs[5 + 2 * ng:7 + 2 * ng]
        near_tabs = None if mla else refs[7 + 2 * ng:9 + 2 * ng]
        n_tabs = 0 if mla else 2
        comm = (refs[3:3 + ng], refs[5 + ng:5 + 2 * ng]) + tuple(refs[7 + n_tabs + 2 * ng:])
        if ng:
            @pl.when(pl.program_id(0) == 0)
            def _():
                _Gather(*comm).start()

            @pl.when(pl.program_id(0) == last_step)
            def _():
                _Gather(*comm).forward()

        lane = lax.broadcasted_iota(I32, (1, LANE), 1)
        rel_t = lax.broadcasted_iota(I32, (tk, tq), 1) - lax.broadcasted_iota(I32, (tk, tq), 0)
        if not mla:
            _fill_near_tables(*near_tabs, rel_t)

        def transpose_v(j, carry):
            c0 = pl.multiple_of(j * tk, tk)
            vt_ref[:, pl.ds(c0, tk)] = v_ref[pl.ds(c0, tk), :].astype(F32).T.astype(BF16)
            return carry

        lax.fori_loop(0, s // tk, transpose_v, 0)

        def q_block(qi, carry):
            r0 = pl.multiple_of(qi * tq, tq)
            kcols = [slice(a * LANE, (a + 1) * LANE) if mla else slice(0, LANE) for a in range(2)]
            qas = [q_ref[pl.ds(r0, tq), kcols[a]] for a in range(2)]
            if not mla:
                qas = [jnp.where(lane < DIL_DIM, qas[0], jnp.zeros_like(qas[0])),
                       jnp.where(lane >= DIL_DIM, qas[1], jnp.zeros_like(qas[1]))]

            n_k = (r0 + tq) // tk

            def products(kj):
                c0 = pl.multiple_of(kj * tk, tk)
                return [lax.dot_general(k_ref[pl.ds(c0, tk), kcols[a]], qas[a], NT, preferred_element_type=F32)
                        for a in range(2)]

            for a, pr in enumerate(products(0)):
                st_ref[0, a] = pr

            def k_block(kj, c, kind):
                c0 = pl.multiple_of(kj * tk, tk)
                slot = kj & 1
                ahead = products(jnp.minimum(kj + 1, n_k - 1))
                out = []
                for a in range(2):
                    m, l, acc = c[a]
                    st, cnt = _mask_scores(st_ref[slot, a], scale, kind, rel_t, r0 - c0, near_tabs)
                    st_ref[1 - slot, a] = ahead[a]
                    m_new = jnp.maximum(m, jnp.max(st, axis=0, keepdims=True))
                    alpha = jnp.exp2(m - m_new)
                    p = jnp.exp2(st - m_new)
                    if cnt is not None:
                        p = p * cnt
                    l = alpha * l + jnp.sum(p, axis=0, keepdims=True)
                    vt = vt_ref[a * DIL_DIM:(a + 1) * DIL_DIM, pl.ds(c0, tk)]
                    acc = alpha * acc + jnp.dot(vt, p.astype(BF16), preferred_element_type=F32)
                    out.append((m_new, l, acc))
                return tuple(out)

            one = (jnp.full((1, tq), NEG_INF, F32), jnp.zeros((1, tq), F32), jnp.zeros((DIL_DIM, tq), F32))
            first_near = jnp.maximum((r0 - reach) // tk, 0)
            c = lax.fori_loop(0, first_near, functools.partial(k_block, kind=kind_far), (one, one))
            res = lax.fori_loop(first_near, (r0 + tq) // tk, functools.partial(k_block, kind=kind_near), c)
            o_t = jnp.concatenate([res[a][2] / res[a][1] for a in range(2)], axis=0)
            o_ref[pl.ds(r0, tq), :] = o_t.T.astype(BF16)
            for a in range(2):
                lse_ref[a, :, pl.ds(r0, tq)] = res[a][0] * LN2 + jnp.log(res[a][1])
            return carry

        lax.fori_loop(0, s // tq, q_block, 0)

        if ng:
            @pl.when(pl.program_id(0) == last_step)
            def _():
                _Gather(*comm).finish()

    return pl.pallas_call(
        body, name=name, grid=(HEADS // 2,),
        in_specs=[pl.BlockSpec((s, qw), lambda h: (0, h)), pl.BlockSpec((s, qw), lambda h: (0, h)),
                  pl.BlockSpec((s, LANE), lambda h: (0, h))] + [ANY] * ng,
        out_specs=[pl.BlockSpec((s, LANE), lambda h: (0, h)), pl.BlockSpec((2, 1, s), lambda h: (h, 0, 0))] + [ANY] * ng,
        out_shape=[jax.ShapeDtypeStruct((s, DIL_W), BF16), jax.ShapeDtypeStruct((HEADS, 1, s), F32)] + _Gather.out_shapes(gather),
        scratch_shapes=[pltpu.VMEM((LANE, s), BF16), pltpu.VMEM((2, 2, tk, tq), F32)]
        + ([] if mla else [pltpu.VMEM((NEAR_OFFSETS, tk, tq), F32)] * 2) + (_Gather.scratch(gather) if ng else []),
        compiler_params=_params(("arbitrary",) if ng else ("parallel",), 12 << 20),
    )(*_in_hbm(q, k, v), *gather)


def _attn_bwd(q, k, v, o, do, do_block0, lse, mla, scale, name, scatter=()):
    s = q.shape[0]
    qw = 2 * LANE if mla else LANE
    tq, tk = ATT_TQ, ATT_TK
    nq = s // tq
    reach, kind_near, kind_far = _block_kinds(mla)
    assert s % tq == 0 and tq % tk == 0
    ns = len(scatter)
    last_step = HEADS // 2 - 1

    def body(*refs):
        q_ref, k_ref, v_ref, o_ref, do_ref, lse_ref = refs[:6]
        dq_ref, dk_ref, dv_ref = refs[6 + ns:9 + ns]
        kt_ref, dot_ref, dob_ref, dqt_ref, delta_ref, lse2_ref = refs[9 + 2 * ns:15 + 2 * ns]
        near_tabs = None if mla else refs[15 + 2 * ns:17 + 2 * ns]
        n_tabs = 0 if mla else 2
        comm = (refs[6:6 + ns], refs[9 + ns:9 + 2 * ns]) + tuple(refs[15 + n_tabs + 2 * ns:])
        if ns:
            @pl.when(pl.program_id(0) == 0)
            def _():
                _Scatter(*comm).start()

        lane = lax.broadcasted_iota(I32, (1, LANE), 1)
        row = lax.broadcasted_iota(I32, (LANE, 1), 0)
        rel_t = lax.broadcasted_iota(I32, (tk, tq), 1) - lax.broadcasted_iota(I32, (tk, tq), 0)
        if not mla:
            _fill_near_tables(*near_tabs, rel_t)

        def prepare(j, carry):
            c0 = pl.multiple_of(j * tk, tk)
            do_blk = do_ref[pl.ds(c0, tk), :]
            dob_ref[pl.ds(c0, tk), :] = do_blk.astype(BF16)
            do_t = do_blk.T
            dot_ref[:, pl.ds(c0, tk)] = do_t.astype(BF16)
            prod = do_t * o_ref[pl.ds(c0, tk), :].astype(F32).T
            delta_ref[0, :, pl.ds(c0, tk)] = jnp.sum(prod[0:DIL_DIM], axis=0, keepdims=True)
            delta_ref[1, :, pl.ds(c0, tk)] = jnp.sum(prod[DIL_DIM:LANE], axis=0, keepdims=True)
            for w in range(qw // LANE):
                kt_ref[w * LANE:(w + 1) * LANE, pl.ds(c0, tk)] = (
                    k_ref[pl.ds(c0, tk), w * LANE:(w + 1) * LANE].astype(F32).T.astype(BF16))
            return carry

        lax.fori_loop(0, s // tk, prepare, 0)
        dqt_ref[...] = jnp.zeros_like(dqt_ref)
        lse2_ref[...] = lse_ref[...] * LOG2E

        sels = [lane < DIL_DIM, lane >= DIL_DIM]
        rsels = [row < DIL_DIM, row >= DIL_DIM]
        cols = [slice(a * LANE, (a + 1) * LANE) if mla else slice(0, LANE) for a in range(2)]

        def k_block(kj, carry):
            c0 = pl.multiple_of(kj * tk, tk)
            kas = [k_ref[pl.ds(c0, tk), cols[a]] for a in range(2)]
            kts = [kt_ref[cols[a], pl.ds(c0, tk)] for a in range(2)]
            if not mla:
                kas = [jnp.where(sels[a], kas[a], jnp.zeros_like(kas[a])) for a in range(2)]
                kts = [jnp.where(rsels[a], kts[a], jnp.zeros_like(kts[a])) for a in range(2)]
            vb = v_ref[pl.ds(c0, tk), :]
            vbs = [jnp.where(sels[a], vb, jnp.zeros_like(vb)) for a in range(2)]

            first = c0 // tq

            def q_block(qi, c, kind):
                r0 = pl.multiple_of(qi * tq, tq)
                out, dq_parts = [], []
                for a in range(2):
                    dk_acc, dv_acc = c[a]
                    qa = q_ref[pl.ds(r0, tq), cols[a]]
                    st, cnt = _scores_t(kas[a], qa, scale, kind, rel_t, r0 - c0, near_tabs)
                    p = jnp.exp2(st - lse2_ref[a, :, pl.ds(r0, tq)])
                    if cnt is not None:
                        p = p * cnt
                    dp = jnp.dot(vbs[a], dot_ref[:, pl.ds(r0, tq)], preferred_element_type=F32)
                    ds = (p * (dp - delta_ref[a, :, pl.ds(r0, tq)]) * scale).astype(BF16)
                    dv_acc = dv_acc + jnp.dot(p.astype(BF16), dob_ref[pl.ds(r0, tq), :], preferred_element_type=F32)
                    dk_acc = dk_acc + jnp.dot(ds, qa, preferred_element_type=F32)
                    dq_parts.append(jnp.dot(kts[a], ds, preferred_element_type=F32))
                    out.append((dk_acc, dv_acc))
                if mla:
                    for a in range(2):
                        dqt_ref[cols[a], pl.ds(r0, tq)] += dq_parts[a]
                else:
                    dqt_ref[:, pl.ds(r0, tq)] += dq_parts[0] + dq_parts[1]
                return tuple(out)

            zero = jnp.zeros((tk, LANE), F32)
            last_near = jnp.minimum((c0 + tk - 1 + reach) // tq + 1, nq)
            c = lax.fori_loop(first, last_near, functools.partial(q_block, kind=kind_near), ((zero, zero), (zero, zero)))
            (dk0, dv0), (dk1, dv1) = lax.fori_loop(last_near, nq, functools.partial(q_block, kind=kind_far), c)
            if mla:
                dk_ref[pl.ds(c0, tk), cols[0]] = dk0
                dk_ref[pl.ds(c0, tk), cols[1]] = dk1
            else:
                dk_ref[pl.ds(c0, tk), :] = jnp.where(sels[0], dk0, dk1)
            dv_ref[pl.ds(c0, tk), :] = jnp.where(sels[0], dv0, dv1)
            return carry

        lax.fori_loop(0, s // tk, k_block, 0)

        def write_dq(j, carry):
            c0 = pl.multiple_of(j * tk, tk)
            for w in range(qw // LANE):
                dq_ref[pl.ds(c0, tk), w * LANE:(w + 1) * LANE] = dqt_ref[w * LANE:(w + 1) * LANE, pl.ds(c0, tk)].T
            return carry

        lax.fori_loop(0, s // tk, write_dq, 0)

        if ns:
            @pl.when(pl.program_id(0) == last_step)
            def _():
                _Scatter(*comm).finish()

    b0 = do_block0
    return pl.pallas_call(
        body, name=name, grid=(HEADS // 2,),
        in_specs=[pl.BlockSpec((s, qw), lambda h: (0, h)), pl.BlockSpec((s, qw), lambda h: (0, h)),
                  pl.BlockSpec((s, LANE), lambda h: (0, h)), pl.BlockSpec((s, LANE), lambda h: (0, h)),
                  pl.BlockSpec((s, LANE), lambda h: (0, h + b0)), pl.BlockSpec((2, 1, s), lambda h: (h, 0, 0))] + [ANY] * ns,
        out_specs=[pl.BlockSpec((s, qw), lambda h: (0, h)), pl.BlockSpec((s, qw), lambda h: (0, h)),
                   pl.BlockSpec((s, LANE), lambda h: (0, h))] + [ANY] * ns,
        out_shape=[jax.ShapeDtypeStruct(q.shape, F32), jax.ShapeDtypeStruct(k.shape, F32), jax.ShapeDtypeStruct((s, DIL_W), F32)]
        + _Scatter.out_shapes(scatter),
        scratch_shapes=[pltpu.VMEM((qw, s), BF16), pltpu.VMEM((LANE, s), BF16), pltpu.VMEM((s, LANE), BF16),
                        pltpu.VMEM((qw, s), F32), pltpu.VMEM((2, 1, s), F32), pltpu.VMEM((2, 1, s), F32)]
        + ([] if mla else [pltpu.VMEM((NEAR_OFFSETS, tk, tq), F32)] * 2) + (_Scatter.semaphores(ns) if ns else []),
        compiler_params=_params(("arbitrary",) if ns else ("parallel",), 24 << 20),
    )(*_in_hbm(q, k, v, o, do, lse), *scatter)


def _ada_bwd(c_all, dmod_shard):
    n, d = c_all.shape
    cols = dmod_shard.shape[1]

    def body(c_ref, g_ref, o_ref):
        cv = c_ref[...]
        o_ref[...] = lax.dot_general(cv * _sigmoid(cv), g_ref[...], TN, precision=HIGHEST, preferred_element_type=F32)

    return pl.pallas_call(
        body, name="ada_bwd", out_shape=jax.ShapeDtypeStruct((d, cols), F32),
        compiler_params=_params(None, 16 << 20),
    )(c_all, dmod_shard)


SMALL_WIDTHS = (("g_mix_norm", D_MODEL), ("g_q_lat", Q_LORA), ("g_kv_lat", KV_LORA), ("g_mla_q_nope", NOPE),
                ("g_mla_q_pe", ROPE), ("g_mla_k_nope", NOPE), ("g_mla_k_pe", ROPE), ("g_dil_q", DIL_DIM),
                ("g_dil_k", DIL_DIM), ("g_ffn_norm", D_MODEL), ("b_conv", UP_W))


def _small_layout():
    pieces = (("dmod", 6 * D_MODEL),) + SMALL_WIDTHS + tuple(("w_conv%d" % k, UP_W) for k in range(3)) + (("loss", 1),)
    layout, off = {}, 0
    for name, width in pieces:
        layout[name] = (width, off)
        off += -(-width // LANE) * LANE
    return layout, off


def _pack_small(acc1, acc2, dg2, dglat, dgains, dbg, dbv, dwg, dwv, loss_part):
    layout, total = _small_layout()

    def body(a1, a2, g2, gl, gg, bg, bv, wg, wv, ls, o_ref):
        o_ref[...] = jnp.zeros_like(o_ref)

        def put(name, src, shift=0):
            start = layout[name][1] + shift
            o_ref[:, start:start + src.shape[1]] = src

        for k, src in enumerate((a1[0:1, :], a1[1:2, :], a2[3:4, :], a2[0:1, :], a2[1:2, :], g2[...])):
            put("dmod", src, k * D_MODEL)
        put("g_mix_norm", a1[2:3, :])
        put("g_q_lat", gl[0:1, :])
        put("g_kv_lat", gl[1:2, 0:KV_LORA])
        put("g_mla_q_nope", gg[0:1, 0:NOPE])
        put("g_mla_q_pe", gg[5:6, 0:ROPE])
        put("g_mla_k_nope", gg[1:2, 0:NOPE])
        put("g_mla_k_pe", gg[2:3, 0:ROPE])
        put("g_dil_q", gg[3:4, 0:DIL_DIM])
        put("g_dil_k", gg[4:5, 0:DIL_DIM])
        put("g_ffn_norm", a2[2:3, :])
        put("b_conv", bg[...])
        put("b_conv", bv[...], D_FF)
        for k in range(3):
            put("w_conv%d" % k, wg[k:k + 1, :])
            put("w_conv%d" % k, wv[k:k + 1, :], D_FF)
        put("loss", ls[...])

    ins = (acc1, acc2, dg2, dglat, dgains, dbg, dbv, dwg, dwv, loss_part)
    return pl.pallas_call(
        body, name="pack_small", grid=(1,), in_specs=[_full(a.shape) for a in ins], out_specs=_full((1, total)),
        out_shape=jax.ShapeDtypeStruct((1, total), F32),
        compiler_params=_params(("arbitrary",), 2 << 20),
    )(*_in_hbm(*ins))


def _sum_unpack(g):
    n_dev, _, total = g.shape
    layout, _ = _small_layout()

    def body(g_ref, *refs):
        o_refs, s_ref = refs[:-1], refs[-1]
        acc = g_ref[0]
        for k in range(1, n_dev):
            acc = acc + g_ref[k]
        s_ref[...] = acc
        take = lambda name: s_ref[:, layout[name][1]:layout[name][1] + layout[name][0]]
        o_refs[0][...] = take("dmod")
        for i, (name, _) in enumerate(SMALL_WIDTHS):
            o_refs[1 + i][...] = take(name)
        for k in range(3):
            o_refs[-2][k:k + 1, :] = take("w_conv%d" % k)
        o_refs[-1][...] = take("loss")

    shapes = [(1, 6 * D_MODEL)] + [(1, w) for _, w in SMALL_WIDTHS] + [(3, UP_W), (1, 1)]
    return pl.pallas_call(
        body, name="sum_unpack", out_shape=[jax.ShapeDtypeStruct(sh, F32) for sh in shapes],
        scratch_shapes=[pltpu.VMEM((1, total), F32)],
        compiler_params=_params(None, 4 << 20),
    )(g)


def _adamw_math(w, g, m, v):
    mn = ADAM_B1 * m + (1.0 - ADAM_B1) * g
    vn = ADAM_B2 * v + (1.0 - ADAM_B2) * (g * g)
    m_hat = mn / (1.0 - ADAM_B1 ** ADAM_STEP)
    v_hat = vn / (1.0 - ADAM_B2 ** ADAM_STEP)
    return -ADAM_LR * (m_hat / (jnp.sqrt(v_hat) + ADAM_EPS) + ADAM_WD * w), mn, vn


def _adamw_vectors(ws, gs, ms, vs):
    k = len(ws)

    def body(*refs):
        for i in range(k):
            d, mn, vn = _adamw_math(refs[i][...], refs[k + i][...], refs[2 * k + i][...], refs[3 * k + i][...])
            refs[4 * k + i][...] = d
            refs[5 * k + i][...] = mn
            refs[6 * k + i][...] = vn

    blocks = [_full(w.shape) for w in ws]
    outs = pl.pallas_call(
        body, name="adamw_vectors", grid=(1,), in_specs=blocks * 4, out_specs=blocks * 3,
        out_shape=[jax.ShapeDtypeStruct(w.shape, F32) for w in ws] * 3,
        compiler_params=_params(("arbitrary",), 2 << 20),
    )(*_in_hbm(*ws, *gs, *ms, *vs))
    return outs[:k], outs[k:2 * k], outs[2 * k:]


def _adamw(w, g, m, v, name):
    r, c = w.shape
    tr = r
    for cand in (256, 128, 64, 32, 16):
        if r % cand == 0 and r > cand:
            tr = cand
            break

    def body(w_ref, g_ref, m_ref, v_ref, d_ref, mo_ref, vo_ref):
        d_ref[...], mo_ref[...], vo_ref[...] = _adamw_math(w_ref[...], g_ref[...], m_ref[...], v_ref[...])

    blk = pl.BlockSpec((tr, c), lambda i: (i, 0))
    return pl.pallas_call(
        body, name=name, grid=(r // tr,), in_specs=[blk] * 4, out_specs=[blk] * 3,
        out_shape=[jax.ShapeDtypeStruct((r, c), F32)] * 3,
        compiler_params=_params(("parallel",), 7 * _nbytes((tr, c), F32)),
    )(w, g, m, v)


def _position():
    return lax.axis_index("x"), lax.axis_index("y"), lax.axis_index("c")


def _other_chips(x, y):
    return [(1 - x, y, 2 * (1 - x) + y), (x, 1 - y, 2 * x + (1 - y)), (1 - x, 1 - y, 2 * (1 - x) + (1 - y))]


class _SmallGather:
    def __init__(self, v_ref, out_ref, send_sems, recv_sems, local_sem):
        x, y, c = _position()
        me = 4 * x + 2 * y + c
        self.local = pltpu.make_async_copy(v_ref, out_ref.at[me], local_sem)
        self.sends, self.arrivals = [], []
        for k in range(N_DEV - 1):
            fx, fy, fc = ((k + 1) >> 2) & 1, ((k + 1) >> 1) & 1, (k + 1) & 1
            px, py, pc = (1 - x if fx else x), (1 - y if fy else y), (1 - c if fc else c)

            def copy(dst, k=k, peer=(px, py, pc)):
                return pltpu.make_async_remote_copy(src_ref=v_ref, dst_ref=dst, send_sem=send_sems.at[k],
                                                    recv_sem=recv_sems.at[k], device_id=peer, device_id_type=MESH)

            self.sends.append(copy(out_ref.at[me]))
            self.arrivals.append(copy(out_ref.at[4 * px + 2 * py + pc]))

    @staticmethod
    def semaphores():
        return [pltpu.SemaphoreType.DMA((N_DEV - 1,)), pltpu.SemaphoreType.DMA((N_DEV - 1,)), pltpu.SemaphoreType.DMA]

    def start(self):
        self.local.start()
        for cp in self.sends:
            cp.start()

    def finish(self):
        for cp in self.arrivals:
            cp.wait_recv()
        for cp in self.sends:
            cp.wait_send()
        self.local.wait()


def _prologue(c_taps, w_ada_shard, b_shard, pos_col, rope_consts, shards):
    n = len(shards)
    s = pos_col.shape[0]
    cols = w_ada_shard.shape[1]
    freq, csel, ssel = rope_consts

    def body(*refs):
        ct_ref, w_ref, b_ref, p_ref, f_ref, cs_ref, ss_ref = refs[:7]
        sh_refs = refs[7:7 + n]
        ct_all_ref, mod_all_ref, tab_ref = refs[7 + n:10 + n]
        g_refs = refs[10 + n:10 + 2 * n]
        mod_blk_ref = refs[10 + 2 * n]
        sems = refs[11 + 2 * n:]
        weights = _Gather(sh_refs, g_refs, *sems[6:])
        weights.start()
        first = _SmallGather(ct_ref, ct_all_ref, *sems[0:3])
        first.start()
        first.finish()
        cv = ct_all_ref[:, 0, 0:D_MODEL]
        sc = (cv * _sigmoid(cv)).astype(BF16)
        mod_blk_ref[...] = jnp.dot(sc, w_ref[...].astype(BF16), preferred_element_type=F32) + b_ref[...]
        second = _SmallGather(mod_blk_ref, mod_all_ref, *sems[3:6])
        second.start()

        def table_rows(i, carry):
            r0 = pl.multiple_of(i * ROW_TILE, ROW_TILE)
            ang = p_ref[pl.ds(r0, ROW_TILE), :].astype(F32) * f_ref[...]
            tab_ref[pl.ds(r0, ROW_TILE), :] = cs_ref[...] * jnp.cos(ang) + ss_ref[...] * jnp.sin(ang)
            return carry

        lax.fori_loop(0, s // ROW_TILE, table_rows, 0)
        second.finish()
        weights.forward()
        weights.finish()

    return pl.pallas_call(
        body, name="prologue",
        out_shape=[jax.ShapeDtypeStruct((N_DEV,) + c_taps.shape, F32), jax.ShapeDtypeStruct((N_DEV, N_DEV, cols), F32),
                   jax.ShapeDtypeStruct((s, 4 * LANE), F32)] + _Gather.out_shapes(shards),
        in_specs=[IN_VMEM] * 7 + [ANY] * n, out_specs=[IN_VMEM] * 3 + [ANY] * n,
        scratch_shapes=[pltpu.VMEM((N_DEV, cols), F32)] + _SmallGather.semaphores() * 2 + _Gather.scratch(shards),
        compiler_params=_params(None, 14 << 20),
    )(c_taps, w_ada_shard, b_shard, pos_col, freq, csel, ssel, *shards)


IN_VMEM = pl.BlockSpec(memory_space=pltpu.VMEM)
ANY = pl.BlockSpec(memory_space=pl.ANY)


class _Gather:
    def __init__(self, w_refs, out_refs, send_sems, recv_sems, own_sems, *bounce_refs):
        x, y, c = _position()
        q0 = 2 * x + y
        sibling = (x, y, 1 - c)
        self.ici, self.ici_in, self.fwd, self.fwd_in, self.own_in, self.own_out = [], [], [], [], [], []
        for k, (w_ref, out_ref) in enumerate(zip(w_refs, out_refs)):
            half = w_ref.shape[0] // 2
            self.own_in.append(pltpu.make_async_copy(w_ref, bounce_refs[k], own_sems.at[2 * k]))
            self.own_out.append(pltpu.make_async_copy(bounce_refs[k], out_ref.at[q0], own_sems.at[2 * k + 1]))

            def blk(q, e, out_ref=out_ref, half=half):
                return out_ref.at[q, pl.ds(pl.multiple_of(e * half, 16), half), :]

            def copy(src, dst, i, to):
                return pltpu.make_async_remote_copy(src_ref=src, dst_ref=dst, send_sem=send_sems.at[i], recv_sem=recv_sems.at[i],
                                                    device_id=to, device_id_type=MESH)

            src = w_ref.at[pl.ds(pl.multiple_of(c * half, 16), half), :]
            for j, (cx, cy, qj) in enumerate(_other_chips(x, y)):
                self.ici.append(copy(src, blk(q0, c), 6 * k + j, (cx, cy, c)))
                self.ici_in.append(copy(blk(qj, c), blk(qj, c), 6 * k + j, (cx, cy, c)))
                self.fwd.append(copy(blk(qj, c), blk(qj, c), 6 * k + 3 + j, sibling))
                self.fwd_in.append(copy(blk(qj, 1 - c), blk(qj, 1 - c), 6 * k + 3 + j, sibling))

    @staticmethod
    def out_shapes(shards):
        return [jax.ShapeDtypeStruct((N_CHIP,) + s.shape, s.dtype) for s in shards]

    @staticmethod
    def scratch(shards):
        n = len(shards)
        return ([pltpu.SemaphoreType.DMA((6 * n,)), pltpu.SemaphoreType.DMA((6 * n,)), pltpu.SemaphoreType.DMA((2 * n,))]
                + [pltpu.VMEM(s.shape, s.dtype) for s in shards])

    def start(self):
        for cp in self.ici + self.own_in:
            cp.start()

    def forward(self):
        for fetched, placed in zip(self.own_in, self.own_out):
            fetched.wait()
            placed.start()
        for arrived, onward in zip(self.ici_in, self.fwd):
            arrived.wait_recv()
            onward.start()

    def finish(self):
        for cp in self.fwd_in:
            cp.wait_recv()
        for cp in self.ici + self.fwd:
            cp.wait_send()
        for cp in self.own_out:
            cp.wait()


def _swap_halves_d2d(grads, name):
    n = len(grads)

    def body(*refs):
        swap = _PairSwap(refs[:n], refs[n:2 * n], *refs[2 * n:])
        swap.start()
        swap.finish()

    return pl.pallas_call(
        body, name=name,
        out_shape=_PairSwap.out_shapes(grads), in_specs=[ANY] * n, out_specs=[ANY] * n,
        scratch_shapes=_PairSwap.semaphores(n),
    )(*grads)


class _PairSwap:
    def __init__(self, g_refs, out_refs, send_sems, recv_sems):
        x, y, c = _position()
        self.copies = [
            pltpu.make_async_remote_copy(src_ref=g_ref.at[:, 1 - c], dst_ref=out_ref, send_sem=send_sems.at[k],
                                         recv_sem=recv_sems.at[k], device_id=(x, y, 1 - c), device_id_type=MESH)
            for k, (g_ref, out_ref) in enumerate(zip(g_refs, out_refs))]

    @staticmethod
    def out_shapes(grads):
        return [jax.ShapeDtypeStruct((N_CHIP,) + g.shape[2:], g.dtype) for g in grads]

    @staticmethod
    def semaphores(n):
        return [pltpu.SemaphoreType.DMA((n,)), pltpu.SemaphoreType.DMA((n,))]

    def start(self):
        for cp in self.copies:
            cp.start()

    def finish(self):
        for cp in self.copies:
            cp.wait_recv()
        for cp in self.copies:
            cp.wait_send()


def _pair_sum(g, a, c_idx, name):
    _, _, rh, cols = g.shape
    tr = rh
    for cand in (256, 128, 64, 32, 16):
        if rh % cand == 0 and rh > cand:
            tr = cand
            break

    def body(c_ref, g_ref, a_ref, o_ref):
        o_ref[...] = (g_ref[...] + a_ref[...]).astype(BF16)

    return pl.pallas_call(
        body, name=name,
        grid_spec=pltpu.PrefetchScalarGridSpec(
            num_scalar_prefetch=1, grid=(N_CHIP, rh // tr),
            in_specs=[pl.BlockSpec((None, None, tr, cols), lambda q, i, c_ref: (q, c_ref[0], i, 0)),
                      pl.BlockSpec((None, tr, cols), lambda q, i, c_ref: (q, i, 0))],
            out_specs=pl.BlockSpec((None, tr, cols), lambda q, i, c_ref: (q, i, 0))),
        out_shape=jax.ShapeDtypeStruct((N_CHIP, rh, cols), BF16),
        compiler_params=_params(("parallel", "parallel"), 10 * _nbytes((tr, cols), F32)),
    )(c_idx, g, a)


def _scatter_and_gather(parts, small, name):
    n = len(parts)

    def body(*refs):
        scatter = _Scatter(refs[:n], refs[n + 1:2 * n + 1], *refs[2 * n + 2:2 * n + 4])
        gather = _SmallGather(refs[n], refs[2 * n + 1], *refs[2 * n + 4:])
        scatter.start()
        gather.start()
        gather.finish()
        scatter.finish()

    return pl.pallas_call(
        body, name=name,
        out_shape=_Scatter.out_shapes(parts) + [jax.ShapeDtypeStruct((N_DEV,) + small.shape, F32)],
        in_specs=[ANY] * n + [IN_VMEM], out_specs=[ANY] * n + [IN_VMEM],
        scratch_shapes=_Scatter.semaphores(n) + _SmallGather.semaphores(),
        compiler_params=_params(None, 10 * _nbytes(small.shape, F32)),
    )(*parts, small)


class _Scatter:
    def __init__(self, p_refs, out_refs, send_sems, recv_sems):
        x, y, c = _position()
        self.copies = []
        for k, (p_ref, out_ref) in enumerate(zip(p_refs, out_refs)):
            for j, (cx, cy, qj) in enumerate(_other_chips(x, y)):
                self.copies.append(pltpu.make_async_remote_copy(
                    src_ref=p_ref.at[qj], dst_ref=out_ref.at[j], send_sem=send_sems.at[3 * k + j],
                    recv_sem=recv_sems.at[3 * k + j], device_id=(cx, cy, c), device_id_type=MESH))

    @staticmethod
    def out_shapes(parts):
        return [jax.ShapeDtypeStruct((3,) + p.shape[1:], p.dtype) for p in parts]

    @staticmethod
    def semaphores(n):
        return [pltpu.SemaphoreType.DMA((3 * n,)), pltpu.SemaphoreType.DMA((3 * n,))]

    def start(self):
        for cp in self.copies:
            cp.start()

    def finish(self):
        for cp in self.copies:
            cp.wait_recv()
        for cp in self.copies:
            cp.wait_send()


def _shard_sum(p, b, qc_idx, name):
    _, rh, cols = p.shape
    tr = rh
    for cand in (256, 128, 64, 32, 16):
        if rh % cand == 0 and rh > cand:
            tr = cand
            break

    def body(qc_ref, p_ref, b_ref, o_ref):
        acc = p_ref[...].astype(F32)
        for j in range(3):
            acc = acc + b_ref[j].astype(F32)
        o_ref[...] = acc

    return pl.pallas_call(
        body, name=name,
        grid_spec=pltpu.PrefetchScalarGridSpec(
            num_scalar_prefetch=1, grid=(rh // tr,),
            in_specs=[pl.BlockSpec((None, tr, cols), lambda i, qc_ref: (qc_ref[0], i, 0)),
                      pl.BlockSpec((3, tr, cols), lambda i, qc_ref: (0, i, 0))],
            out_specs=pl.BlockSpec((None, tr, cols), lambda i, qc_ref: (qc_ref[1], i, 0))),
        out_shape=jax.ShapeDtypeStruct((2, rh, cols), F32),
        compiler_params=_params(("parallel",), 8 * _nbytes((tr, cols), F32)),
    )(qc_idx, p, b)


def _join_halves(shards):
    n = len(shards)

    def body(*refs):
        out_refs = refs[n:2 * n]
        send_sems, recv_sems = refs[2 * n:]
        x, y, c = _position()
        cps = [pltpu.make_async_remote_copy(src_ref=out_refs[k].at[c], dst_ref=out_refs[k].at[c], send_sem=send_sems.at[k],
                                            recv_sem=recv_sems.at[k], device_id=(x, y, 1 - c), device_id_type=MESH)
               for k in range(n)]
        for cp in cps:
            cp.start()
        for k in range(n):
            arriving = out_refs[k].at[1 - c]
            pltpu.make_async_remote_copy(src_ref=arriving, dst_ref=arriving, send_sem=send_sems.at[k], recv_sem=recv_sems.at[k],
                                         device_id=(x, y, 1 - c), device_id_type=MESH).wait_recv()
        for cp in cps:
            cp.wait_send()

    return pl.pallas_call(
        body, name="rs_join",
        out_shape=[jax.ShapeDtypeStruct(a.shape, a.dtype) for a in shards],
        in_specs=[ANY] * n, out_specs=[ANY] * n, input_output_aliases={k: k for k in range(n)},
        scratch_shapes=[pltpu.SemaphoreType.DMA((n,)), pltpu.SemaphoreType.DMA((n,))],
    )(*shards)


def _cols_from_shards(g):
    q, r, cs = g.shape
    return jnp.transpose(g, (1, 0, 2)).reshape(r, q * cs)


def _cols_to_shards(w):
    r, cfull = w.shape
    return jnp.transpose(w.reshape(r, N_CHIP, cfull // N_CHIP), (1, 0, 2))


def _pad_w_in(w):
    z = lambda n: jnp.zeros((w.shape[0], n), w.dtype)
    q_lat, kv_lat, kpe = w[:, 0:512], w[:, 512:768], w[:, 768:800]
    qd, kd, vd = w[:, 800:1312], w[:, 1312:1824], w[:, 1824:2336]
    return jnp.concatenate([q_lat, qd, kd, vd, kv_lat, z(KPE_OFF), kpe, z(LANE - KPE_OFF - ROPE)], axis=1)


def _unpad_w_in(g):
    return jnp.concatenate([g[:, P_QLAT:P_QLAT + Q_LORA], g[:, P_KVLAT:P_KVLAT + KV_LORA],
                            g[:, P_KPE + KPE_OFF:P_KPE + KPE_OFF + ROPE], g[:, P_QD:P_QD + 3 * DIL_W]], axis=1)


def _pad_w_qb(w):
    w3 = w.reshape(Q_LORA, HEADS, NOPE + ROPE)
    return jnp.pad(w3, ((0, 0), (0, 0), (0, LANE - NOPE - ROPE))).reshape(Q_LORA, HEADS * LANE)


def _unpad_w_qb(g):
    return g.reshape(Q_LORA, HEADS, LANE)[:, :, :NOPE + ROPE].reshape(Q_LORA, HEADS * (NOPE + ROPE))


def _pad_w_kvb(w):
    w3 = w.reshape(KV_LORA, HEADS, 2 * NOPE)
    kp = jnp.pad(w3[:, :, :NOPE], ((0, 0), (0, 0), (0, LANE - NOPE))).reshape(KV_LORA, HEADS * LANE)
    return jnp.concatenate([kp, w3[:, :, NOPE:].reshape(KV_LORA, DIL_W)], axis=1)


def _unpad_w_kvb(g):
    gk = g[:, :HEADS * LANE].reshape(KV_LORA, HEADS, LANE)[:, :, :NOPE]
    gv = g[:, HEADS * LANE:].reshape(KV_LORA, HEADS, NOPE)
    return jnp.concatenate([gk, gv], axis=2).reshape(KV_LORA, HEADS * 2 * NOPE)


def _head_gains(g_q_nope, g_q_pe, g_k_nope, g_k_pe, g_dq, g_dk):
    z = lambda n: jnp.zeros((1, n), F32)
    q1 = jnp.concatenate([g_q_nope, g_q_pe, z(LANE - NOPE - ROPE)], axis=1)
    k1 = jnp.concatenate([g_k_nope, z(LANE - NOPE)], axis=1)
    kpe = jnp.concatenate([z(KPE_OFF), g_k_pe, z(LANE - KPE_OFF - ROPE)], axis=1)
    return dict(q=jnp.tile(q1, (1, HEADS)), k=jnp.tile(k1, (1, HEADS)), kpe=kpe,
                dq=jnp.tile(g_dq, (1, HEADS)), dk=jnp.tile(g_dk, (1, HEADS)))


def kernel(x, c, positions, w_ada, b_ada, g_mix_norm, w_in, g_q_lat, w_q_b, g_kv_lat, w_kv_b, g_mla_q_nope, g_mla_q_pe, g_mla_k_nope, g_mla_k_pe, g_dil_q, g_dil_k, w_o, g_ffn_norm, w_up, w_conv, b_conv, w_down, loss_target, m_w_ada, m_b_ada, m_g_mix_norm, m_w_in, m_g_q_lat, m_w_q_b, m_g_kv_lat, m_w_kv_b, m_g_mla_q_nope, m_g_mla_q_pe, m_g_mla_k_nope, m_g_mla_k_pe, m_g_dil_q, m_g_dil_k, m_w_o, m_g_ffn_norm, m_w_up, m_w_conv, m_b_conv, m_w_down, v_w_ada, v_b_ada, v_g_mix_norm, v_w_in, v_g_q_lat, v_w_q_b, v_g_kv_lat, v_w_kv_b, v_g_mla_q_nope, v_g_mla_q_pe, v_g_mla_k_nope, v_g_mla_k_pe, v_g_dil_q, v_g_dil_k, v_w_o, v_g_ffn_norm, v_w_up, v_w_conv, v_b_conv, v_w_down):
    args = dict(locals())
    weights = {n: args[n][0] for n in ("w_ada", "w_in", "w_q_b", "w_kv_b", "w_o", "w_up", "w_conv", "w_down")}
    small_w = {n: args[n] for n in ("b_ada",) + tuple(n for n, _ in SMALL_WIDTHS)}
    mom_m = {n[2:]: (args[n][0] if args[n].ndim == 3 else args[n]) for n in args if n.startswith("m_")}
    mom_v = {n[2:]: (args[n][0] if args[n].ndim == 3 else args[n]) for n in args if n.startswith("v_")}

    xi, yi, ci = _position()
    q0 = 2 * xi + yi
    me = 4 * xi + 2 * yi + ci
    xs, tgt = x[0], loss_target[0]
    s = xs.shape[0]
    consts = _seg_consts()
    c_idx, qc_idx = jnp.reshape(ci, (1,)).astype(I32), jnp.stack([q0, ci]).astype(I32)

    def halves(g4):
        q, r, cc = g4.shape
        return g4.reshape(q, 2, r // 2, cc)

    own_first = [weights[n].astype(BF16) for n in ("w_in", "w_q_b", "w_kv_b")]
    own_later = [weights[n].astype(BF16) for n in ("w_o", "w_up", "w_down")]
    conv_cols = UP_W // N_CHIP
    ada_cols = w_ada.shape[2]
    b_shard = lax.dynamic_slice_in_dim(b_ada, q0 * ada_cols, ada_cols, axis=1)
    c_taps = jnp.concatenate([c, weights["w_conv"].reshape(1, 3 * conv_cols)], axis=1)
    c_taps_all, mod_all, tab, *gathered = _prologue(c_taps, weights["w_ada"], b_shard, positions.reshape(s, 1),
                                                    _rope_consts(), own_first)
    c_all = c_taps_all[:, 0, :D_MODEL]
    w_conv_f = c_taps_all[:, 0, D_MODEL:].reshape(N_CHIP, 2, 3, conv_cols)[:, 0]
    w_conv_f = jnp.transpose(w_conv_f, (1, 0, 2)).reshape(3, UP_W)
    mod_all = mod_all.reshape(N_CHIP, 2, N_DEV, ada_cols)
    mod = lax.dynamic_index_in_dim(lax.dynamic_index_in_dim(mod_all, ci, 1, False), me, 1, False)
    mod = mod.reshape(1, N_CHIP * ada_cols)
    sh1, sc1, g1, sh2, sc2, g2 = [mod[:, k * D_MODEL:(k + 1) * D_MODEL] for k in range(6)]
    w_in_p = _pad_w_in(_cols_from_shards(gathered[0]))
    w_qb_p = _pad_w_qb(_cols_from_shards(gathered[1]))
    w_kvb_p = _pad_w_kvb(_cols_from_shards(gathered[2]))
    gains = _head_gains(g_mla_q_nope, g_mla_q_pe, g_mla_k_nope, g_mla_k_pe, g_dil_q, g_dil_k)

    h = _prenorm(xs, g_mix_norm, sc1, sh1, "prenorm")
    proj = _mm(h, w_in_p, "nn", F32, 512, P_COLS, "mm_in")
    ql, kvl = _latnorm(proj, g_q_lat, g_kv_lat)
    q_raw = _mm(ql, w_qb_p, "nn", F32, 512, HEADS * LANE, "mm_qb")
    kv_raw = _mm(kvl, w_kvb_p, "nn", F32, 512, HEADS * LANE + DIL_W, "mm_kvb")
    qm, km, vm, qd, kd, vd = _attn_prep(q_raw, kv_raw, proj, tab, gains, consts)
    scale_m, scale_d = (NOPE + ROPE) ** -0.5, DIL_DIM ** -0.5
    o_m, lse_m, got_up = _attn_fwd(qm, km, vm, True, scale_m, "attn_mla", gather=own_later[1:2])
    o_d, lse_d, got_o, got_down = _attn_fwd(qd, kd, vd, False, scale_d, "attn_dil", gather=[own_later[0], own_later[2]])
    gathered = [got_o, got_up, got_down]
    w_o_f = gathered[0].reshape(D_MODEL, D_MODEL)
    w_up_f = _cols_from_shards(gathered[1])
    w_down_f = gathered[2].reshape(D_FF, D_MODEL)
    mix_in = jnp.concatenate([o_m, o_d], axis=1)
    mix = _mm(mix_in, w_o_f, "nn", F32, 512, D_MODEL, "mm_o")
    x1, h2 = _resid_prenorm(xs, mix, g1, g_ffn_norm, sc2, sh2)
    up = _mm(h2, w_up_f, "nn", F32, 512, CONV_TILE, "mm_up")
    act = _conv_gate(up, w_conv_f, b_conv)
    ffn = _mm(act, w_down_f, "nn", F32, 256, D_MODEL, "mm_down")
    dy, dffn, dg2, loss_part = _final(x1, ffn, tgt, g2)

    da = _mm(dffn, w_down_f, "nt", F32, 512, CONV_TILE, "mm_down_dx")
    gw_down = _mm(act, dffn, "tn", F32, 256, D_MODEL, "mm_down_dw")
    dup_g, dup_v, dbg, dbv, dwg, dwv = _gate_bwd(up, da, w_conv_f, b_conv)
    dup = jnp.concatenate([dup_g, dup_v], axis=1)
    early_names = ("w_up", "w_down", "w_o")
    gw_up = _mm(h2, dup, "tn", F32, 512, CONV_TILE, "mm_up_dw", col_shards=True)
    early = [halves(gw_up), halves(gw_down.reshape(N_CHIP, D_FF // N_CHIP, D_MODEL))]
    dh2, *early_sib = _mm(dup, w_up_f, "nt", F32, 256, 512, "mm_up_dx", swap=early, b_outer=True)
    dx1, dmix, acc2 = _ffnnorm_bwd(dh2, x1, dy, mix, g_ffn_norm, sc2, g1)
    gw_o = _mm(mix_in, dmix, "tn", F32, 512, D_MODEL, "mm_o_dw")
    early.append(halves(gw_o.reshape(N_CHIP, D_MODEL // N_CHIP, D_MODEL)))
    dmix_in, sib_o = _mm(dmix, w_o_f, "nt", F32, 512, D_MODEL, "mm_o_dx", swap=early[2:])
    early_sib.append(sib_o)
    early_sums = [_pair_sum(g, a, c_idx, "pair_sum_" + n) for g, a, n in zip(early, early_sib, early_names)]
    dqm, dkm, dvm, *early_recv = _attn_bwd(qm, km, vm, o_m, dmix_in, 0, lse_m, True, scale_m, "attn_mla_bwd",
                                           scatter=early_sums[:1])
    dqd, dkd, dvd, *early_recv_d = _attn_bwd(qd, kd, vd, o_d, dmix_in, DIL_W // LANE, lse_d, False, scale_d,
                                             "attn_dil_bwd", scatter=early_sums[1:])
    early_recv = early_recv + early_recv_d
    dq_raw, dkv_raw, dkpe_b, dqd_b, dkd_b, dvd_b, dgains = _attn_prep_bwd(
        dqm, dkm, dvm, dqd, dkd, dvd, q_raw, kv_raw, proj, tab, gains, consts)
    dql = _mm(dq_raw, w_qb_p, "nt", F32, 512, Q_LORA, "mm_qb_dx")
    gw_qb = _unpad_w_qb(_mm(ql, dq_raw, "tn", F32, Q_LORA, HEADS * LANE, "mm_qb_dw"))
    dkvl = _mm(dkv_raw, w_kvb_p, "nt", F32, 512, KV_LORA, "mm_kvb_dx")
    gw_kvb = _unpad_w_kvb(_mm(kvl, dkv_raw, "tn", F32, KV_LORA, HEADS * LANE + DIL_W, "mm_kvb_dw"))
    dqlat_b, dkvlat_b, dglat = _latnorm_bwd(dql, dkvl, proj, g_q_lat, g_kv_lat)
    dproj = jnp.concatenate([dqlat_b, dqd_b, dkd_b, dvd_b, dkvlat_b, dkpe_b], axis=1)
    dh = _mm(dproj, w_in_p, "nt", F32, 512, D_MODEL, "mm_in_dx")
    gw_in = _unpad_w_in(_mm(h, dproj, "tn", F32, 512, P_COLS, "mm_in_dw"))
    grad_x, acc1 = _mixnorm_bwd(dh, xs, dx1, g_mix_norm, sc1)

    packed = _pack_small(acc1, acc2, dg2, dglat, dgains, dbg, dbv, dwg, dwv, loss_part)
    late_names = ("w_in", "w_q_b", "w_kv_b")
    late = [halves(_cols_to_shards(gw_in)), halves(_cols_to_shards(gw_qb)), halves(_cols_to_shards(gw_kvb))]
    late_sib = _swap_halves_d2d(late, "rs_pair_swap_late")
    late_sums = [_pair_sum(g, a, c_idx, "pair_sum_" + n) for g, a, n in zip(late, late_sib, late_names)]
    *late_recv, gathered_small = _scatter_and_gather(late_sums, packed, "rs_scatter_late")

    grad_b_ada, *small_grads, gconv_full, loss_sum = _sum_unpack(gathered_small)
    grads = {"b_ada": grad_b_ada}
    grads.update({n: g for (n, _), g in zip(SMALL_WIDTHS, small_grads)})
    shard_cols = UP_W // N_CHIP
    grads["w_conv"] = lax.dynamic_slice_in_dim(gconv_full, q0 * shard_cols, shard_cols, axis=1)
    dmod_all = gathered_small[:, 0, :6 * D_MODEL]
    grads["w_ada"] = _ada_bwd(c_all, lax.dynamic_slice_in_dim(dmod_all, q0 * ada_cols, ada_cols, axis=1))

    big_names = late_names + early_names
    half_sums = [_shard_sum(p, b, qc_idx, "shard_sum_" + n)
                 for p, b, n in zip(late_sums + early_sums, list(late_recv) + list(early_recv), big_names)]
    for n, full in zip(big_names, _join_halves(half_sums)):
        grads[n] = full.reshape(2 * full.shape[1], full.shape[2])

    delta, new_m, new_v = {}, {}, {}
    for n in ("w_ada", "w_in", "w_q_b", "w_kv_b", "w_o", "w_up", "w_conv", "w_down"):
        operands = (weights[n], grads[n], mom_m[n], mom_v[n])
        flipped = n in ("w_in", "w_q_b")
        if flipped:
            operands = [jnp.swapaxes(a, 0, 1) for a in operands]
            grads[n] = jnp.swapaxes(operands[1], 0, 1)
        if n == "w_ada":
            operands = _in_hbm(*operands)
        delta[n], new_m[n], new_v[n] = _adamw(*operands, "adamw_" + n)
        if flipped:
            delta[n], new_m[n], new_v[n] = (jnp.swapaxes(a, 0, 1) for a in (delta[n], new_m[n], new_v[n]))
    vec_names = ("b_ada",) + tuple(n for n, _ in SMALL_WIDTHS)
    sd, sm, sv = _adamw_vectors(*[[d_[n] for n in vec_names] for d_ in (small_w, grads, mom_m, mom_v)])
    for k, n in enumerate(vec_names):
        delta[n], new_m[n], new_v[n] = sd[k], sm[k], sv[k]

    loss = loss_sum[0, 0]
    order = ("w_ada", "b_ada", "g_mix_norm", "w_in", "g_q_lat", "w_q_b", "g_kv_lat", "w_kv_b", "g_mla_q_nope", "g_mla_q_pe",
             "g_mla_k_nope", "g_mla_k_pe", "g_dil_q", "g_dil_k", "w_o", "g_ffn_norm", "w_up", "w_conv", "b_conv", "w_down")
    lead = lambda n, z: z[None] if n.startswith("w_") else z
    outs = [loss, grad_x[None]]
    for d_ in (grads, delta, new_m, new_v):
        outs += [lead(n, d_[n]) for n in order]
    return tuple(outs)
```

```python
import functools

import numpy as np
import jax
import jax.numpy as jnp
from jax import lax
from jax.experimental import pallas as pl
from jax.experimental.pallas import tpu as pltpu

F32 = jnp.float32
BF16 = jnp.bfloat16
I32 = jnp.int32

D_MODEL = 1024
HEADS = 8
NOPE = 64
ROPE = 32
Q_LORA = 512
KV_LORA = 256
DIL_DIM = 64
DIL_W = HEADS * DIL_DIM
D_FF = 2816
UP_W = 2 * D_FF
IN_COLS = Q_LORA + KV_LORA + ROPE + 3 * DIL_W
ROPE_THETA = 10000.0
EPS = 1e-6
NEG_INF = -1e30
N_DEV = 8
N_CHIP = 4

ADAM_LR = 0.001
ADAM_B1 = 0.9
ADAM_B2 = 0.999
ADAM_EPS = 1e-08
ADAM_WD = 0.01
ADAM_STEP = 10

LANE = 128
ROW_TILE = 256
ATT_TQ = 512
ATT_TK = 256
LOG2E = 1.4426950408889634
LN2 = 0.6931471805599453
VMEM_CAP = 56 * 1024 * 1024
VMEM_FLOOR = 32 * 1024 * 1024

P_QLAT, P_QD, P_KD, P_VD, P_KVLAT, P_KPE = 0, 512, 1024, 1536, 2048, 2304
P_COLS = 2432
KPE_OFF = 64

NN = (((1,), (0,)), ((), ()))
NT = (((1,), (1,)), ((), ()))
TN = (((0,), (0,)), ((), ()))
HIGHEST = lax.Precision.HIGHEST
MESH = pl.DeviceIdType.MESH


def _params(sem=None, est_bytes=0):
    limit = int(min(max(2 * est_bytes + (4 << 20), VMEM_FLOOR), VMEM_CAP))
    if sem is None:
        return pltpu.CompilerParams(vmem_limit_bytes=limit)
    return pltpu.CompilerParams(dimension_semantics=sem, vmem_limit_bytes=limit)


def _nbytes(shape, dtype):
    return int(np.prod(shape)) * jnp.dtype(dtype).itemsize


def _in_hbm(*xs):
    return [pltpu.with_memory_space_constraint(x, pltpu.HBM) for x in xs]


def _mm(a, b, dims, out_dtype, tm, tn, name, col_shards=False, swap=(), b_outer=False):
    def spec(block, index):
        if b_outer:
            return pl.BlockSpec(block, lambda g0, g1: index(g1, g0))
        return pl.BlockSpec(block, index)

    if dims == "nn":
        (m, k), (k2, n) = a.shape, b.shape
        a_spec = spec((tm, k), lambda i, j: (i, 0))
        b_spec = spec((k, tn), lambda i, j: (0, j))
        dn = NN
    elif dims == "nt":
        (m, k), (n, k2) = a.shape, b.shape
        a_spec = spec((tm, k), lambda i, j: (i, 0))
        b_spec = spec((tn, k), lambda i, j: (j, 0))
        dn = NT
    else:
        (k, m), (k2, n) = a.shape, b.shape
        a_spec = spec((k, tm), lambda i, j: (0, i))
        b_spec = spec((k, tn), lambda i, j: (0, j))
        dn = TN
    assert k == k2 and m % tm == 0 and n % tn == 0, (name, a.shape, b.shape, tm, tn)

    nw = len(swap)
    grid = (n // tn, m // tm) if b_outer else (m // tm, n // tn)

    def body(*refs):
        a_ref, b_ref, o_ref = refs[0], refs[1], refs[2 + nw]
        comm = (refs[2:2 + nw], refs[3 + nw:3 + 2 * nw]) + tuple(refs[3 + 2 * nw:])
        if nw:
            @pl.when((pl.program_id(0) == 0) & (pl.program_id(1) == 0))
            def _():
                _PairSwap(*comm).start()

        o_ref[...] = lax.dot_general(a_ref[...], b_ref[...], dn, preferred_element_type=F32).astype(o_ref.dtype)

        if nw:
            @pl.when((pl.program_id(0) == grid[0] - 1) & (pl.program_id(1) == grid[1] - 1))
            def _():
                _PairSwap(*comm).finish()

    est = _nbytes((tm, k), a.dtype) + _nbytes((tn, k), b.dtype) + _nbytes((tm, tn), F32) + _nbytes((tm, tn), out_dtype)
    if col_shards:
        out_spec = spec((None, tm, tn), lambda i, j: (j, i, 0))
        out_shape = jax.ShapeDtypeStruct((n // tn, m, tn), out_dtype)
    else:
        out_spec = spec((tm, tn), lambda i, j: (i, j))
        out_shape = jax.ShapeDtypeStruct((m, n), out_dtype)
    out = pl.pallas_call(
        body, name=name, grid=grid,
        in_specs=[a_spec, b_spec] + [ANY] * nw,
        out_specs=[out_spec] + [ANY] * nw,
        out_shape=[out_shape] + _PairSwap.out_shapes(swap),
        scratch_shapes=_PairSwap.semaphores(nw) if nw else [],
        compiler_params=_params(("arbitrary", "arbitrary") if nw else ("parallel", "parallel"), est),
    )(a, b, *swap)
    return out if nw else out[0]


def _seg_consts():
    seg_q = np.zeros((HEADS * LANE, LANE), np.float32)
    inv_q = np.zeros((1, LANE), np.float32)
    seg_k = np.zeros((HEADS * LANE, LANE), np.float32)
    inv_k = np.zeros((1, LANE), np.float32)
    seg_d = np.zeros((DIL_W, LANE), np.float32)
    inv_d = np.zeros((1, LANE), np.float32)
    for h in range(HEADS):
        seg_q[h * LANE:h * LANE + NOPE, 2 * h] = 1.0
        seg_q[h * LANE + NOPE:h * LANE + NOPE + ROPE, 2 * h + 1] = 1.0
        inv_q[0, 2 * h], inv_q[0, 2 * h + 1] = 1.0 / NOPE, 1.0 / ROPE
        seg_k[h * LANE:h * LANE + NOPE, h] = 1.0
        inv_k[0, h] = 1.0 / NOPE
        seg_d[h * DIL_DIM:(h + 1) * DIL_DIM, h] = 1.0
        inv_d[0, h] = 1.0 / DIL_DIM
    fold_q = np.tile(np.eye(LANE, dtype=np.float32), (HEADS, 1))
    fold_d = np.zeros((DIL_W, LANE), np.float32)
    fold_d[np.arange(DIL_W), np.arange(DIL_W) % DIL_DIM] = 1.0
    j = lambda v: jnp.asarray(v)
    b = lambda v: jnp.asarray(v, dtype=BF16)
    return dict(seg_q=b(seg_q), exp_q=b(seg_q.T.copy()), inv_q=j(inv_q), seg_k=b(seg_k), exp_k=b(seg_k.T.copy()),
                inv_k=j(inv_k), seg_d=b(seg_d), exp_d=b(seg_d.T.copy()), inv_d=j(inv_d), fold_q=j(fold_q), fold_d=j(fold_d))


def _rope_consts():
    inv_d = jnp.power(ROPE_THETA, -2.0 * jnp.arange(DIL_DIM // 2, dtype=F32) / DIL_DIM)
    inv_q = jnp.power(ROPE_THETA, -2.0 * jnp.arange(ROPE // 2, dtype=F32) / ROPE)
    lanes = np.arange(LANE)
    freq_d = inv_d[lanes % (DIL_DIM // 2)]
    in_pe = (lanes >= KPE_OFF) & (lanes < KPE_OFF + ROPE)
    freq_q = jnp.where(jnp.asarray(in_pe), inv_q[(lanes - KPE_OFF) % (ROPE // 2)], 0.0)
    sign_d = np.where(lanes % DIL_DIM < DIL_DIM // 2, -1.0, 1.0).astype(np.float32)
    sign_q = np.where(in_pe, np.where((lanes - KPE_OFF) < ROPE // 2, -1.0, 1.0), 0.0).astype(np.float32)
    zeros, ones = np.zeros(LANE, np.float32), np.ones(LANE, np.float32)
    freq = jnp.concatenate([freq_d, freq_d, freq_q, freq_q])[None, :]
    csel = jnp.asarray(np.concatenate([ones, zeros, ones, zeros]))[None, :]
    ssel = jnp.asarray(np.concatenate([zeros, sign_d, zeros, sign_q]))[None, :]
    return freq, csel, ssel


def _full(shape):
    return pl.BlockSpec(shape, lambda *_: (0,) * len(shape))


def _tile_lanes(x, n):
    return jnp.concatenate([x] * n, axis=1)


def _rms(x):
    return lax.rsqrt(jnp.mean(x * x, axis=-1, keepdims=True) + EPS)


def _prenorm(x, gain, scale, shift, name):
    s, d = x.shape

    def body(x_ref, g_ref, sc_ref, sh_ref, h_ref):
        xv = x_ref[...]
        h = (xv * _rms(xv)) * g_ref[...] * (1.0 + sc_ref[...]) + sh_ref[...]
        h_ref[...] = h.astype(BF16)

    row = pl.BlockSpec((ROW_TILE, d), lambda i: (i, 0))
    return pl.pallas_call(
        body, name=name, grid=(s // ROW_TILE,),
        in_specs=[row, _full((1, d)), _full((1, d)), _full((1, d))],
        out_specs=row, out_shape=jax.ShapeDtypeStruct((s, d), BF16),
        compiler_params=_params(("parallel",)),
    )(x, gain, scale, shift)


def _latnorm(proj, g_q, g_kv):
    s = proj.shape[0]

    def body(q_ref, kv_ref, gq_ref, gkv_ref, ql_ref, kvl_ref):
        q, kv = q_ref[...], kv_ref[...]
        ql_ref[...] = ((q * _rms(q)) * gq_ref[...]).astype(BF16)
        kvl_ref[...] = ((kv * _rms(kv)) * gkv_ref[...]).astype(BF16)

    return pl.pallas_call(
        body, name="latnorm", grid=(s // ROW_TILE,),
        in_specs=[pl.BlockSpec((ROW_TILE, Q_LORA), lambda i: (i, P_QLAT // Q_LORA)),
                  pl.BlockSpec((ROW_TILE, KV_LORA), lambda i: (i, P_KVLAT // KV_LORA)),
                  _full((1, Q_LORA)), _full((1, KV_LORA))],
        out_specs=[pl.BlockSpec((ROW_TILE, Q_LORA), lambda i: (i, 0)), pl.BlockSpec((ROW_TILE, KV_LORA), lambda i: (i, 0))],
        out_shape=[jax.ShapeDtypeStruct((s, Q_LORA), BF16), jax.ShapeDtypeStruct((s, KV_LORA), BF16)],
        compiler_params=_params(("parallel",)),
    )(proj, proj, g_q, g_kv)


def _dot01(v, mat01):
    hi = v.astype(BF16)
    lo = (v - hi.astype(F32)).astype(BF16)
    return jnp.dot(hi, mat01, preferred_element_type=F32) + jnp.dot(lo, mat01, preferred_element_type=F32)


def _seg_rinv(x, seg, exp, inv):
    r = lax.rsqrt(_dot01(x * x, seg) * inv + EPS)
    return _dot01(r, exp)


def _seg_mean(v, seg, exp, inv):
    return _dot01(_dot01(v, seg) * inv, exp)


def _swap_halves(x, half):
    n = x.shape[1]
    lane = lax.broadcasted_iota(I32, (1, n), 1)
    first = (lane & (2 * half - 1)) < half
    return jnp.where(first, pltpu.roll(x, n - half, 1), pltpu.roll(x, half, 1))


def _rope(x, cos, sin_signed, half):
    return x * cos + _swap_halves(x, half) * sin_signed


def _rope_bwd(dy, cos, sin_signed, half):
    return dy * cos + _swap_halves(dy * sin_signed, half)


def _pe_lane_mask(n):
    lane = lax.broadcasted_iota(I32, (1, n), 1) & (LANE - 1)
    return (lane >= KPE_OFF) & (lane < KPE_OFF + ROPE)


def _attn_prep(q_raw, kv_raw, proj, tab, gains, consts):
    s = q_raw.shape[0]
    hw = HEADS * LANE

    def body(q_ref, kv_ref, kpe_ref, qd_ref, kd_ref, vd_ref, tab_ref,
             gq_ref, gk_ref, gkpe_ref, gdq_ref, gdk_ref,
             segq_ref, expq_ref, invq_ref, segk_ref, expk_ref, invk_ref, segd_ref, expd_ref, invd_ref,
             qm_ref, km_ref, vm_ref, qdo_ref, kdo_ref, vdo_ref):
        tab_v = tab_ref[...]
        cos_d, sin_d = _tile_lanes(tab_v[:, 0:LANE], DIL_W // LANE), _tile_lanes(tab_v[:, LANE:2 * LANE], DIL_W // LANE)
        cos_q1, sin_q1 = tab_v[:, 2 * LANE:3 * LANE], tab_v[:, 3 * LANE:4 * LANE]
        cos_q, sin_q = _tile_lanes(cos_q1, HEADS), _tile_lanes(sin_q1, HEADS)

        q = q_ref[...]
        qn = q * _seg_rinv(q, segq_ref[...], expq_ref[...], invq_ref[...]) * gq_ref[...]
        qm_ref[...] = _rope(qn, cos_q, sin_q, ROPE // 2).astype(BF16)

        kv = kv_ref[...]
        kp = kv[:, :hw]
        kn = kp * _seg_rinv(kp, segk_ref[...], expk_ref[...], invk_ref[...]) * gk_ref[...]
        kpe = kpe_ref[...]
        r_pe = lax.rsqrt(jnp.sum(kpe * kpe, axis=-1, keepdims=True) * (1.0 / ROPE) + EPS)
        kpe_r = _rope(kpe * r_pe * gkpe_ref[...], cos_q1, sin_q1, ROPE // 2)
        km_ref[...] = (kn + _tile_lanes(kpe_r, HEADS)).astype(BF16)
        vm_ref[...] = kv[:, hw:].astype(BF16)

        qd = qd_ref[...]
        qdn = qd * _seg_rinv(qd, segd_ref[...], expd_ref[...], invd_ref[...]) * gdq_ref[...]
        qdo_ref[...] = _rope(qdn, cos_d, sin_d, DIL_DIM // 2).astype(BF16)
        kd = kd_ref[...]
        kdn = kd * _seg_rinv(kd, segd_ref[...], expd_ref[...], invd_ref[...]) * gdk_ref[...]
        kdo_ref[...] = _rope(kdn, cos_d, sin_d, DIL_DIM // 2).astype(BF16)
        vdo_ref[...] = vd_ref[...].astype(BF16)

    t = ROW_TILE
    row = lambda w, cb=0: pl.BlockSpec((t, w), lambda i: (i, cb))
    c = consts
    return pl.pallas_call(
        body, name="attn_prep", grid=(s // t,),
        in_specs=[row(hw), row(hw + DIL_W), row(LANE, P_KPE // LANE), row(DIL_W, P_QD // DIL_W), row(DIL_W, P_KD // DIL_W),
                  row(DIL_W, P_VD // DIL_W), row(4 * LANE),
                  _full((1, hw)), _full((1, hw)), _full((1, LANE)), _full((1, DIL_W)), _full((1, DIL_W)),
                  _full((hw, LANE)), _full((LANE, hw)), _full((1, LANE)), _full((hw, LANE)), _full((LANE, hw)), _full((1, LANE)),
                  _full((DIL_W, LANE)), _full((LANE, DIL_W)), _full((1, LANE))],
        out_specs=[row(hw), row(hw), row(DIL_W), row(DIL_W), row(DIL_W), row(DIL_W)],
        out_shape=[jax.ShapeDtypeStruct((s, hw), BF16), jax.ShapeDtypeStruct((s, hw), BF16)]
        + [jax.ShapeDtypeStruct((s, DIL_W), BF16)] * 4,
        compiler_params=_params(("parallel",), 24 << 20),
    )(*_in_hbm(q_raw, kv_raw, proj, proj, proj, proj, tab), gains["q"], gains["k"], gains["kpe"], gains["dq"], gains["dk"],
      c["seg_q"], c["exp_q"], c["inv_q"], c["seg_k"], c["exp_k"], c["inv_k"], c["seg_d"], c["exp_d"], c["inv_d"])


def _attn_prep_bwd(dqm, dkm, dvm, dqd, dkd, dvd, q_raw, kv_raw, proj, tab, gains, consts):
    s = q_raw.shape[0]
    hw = HEADS * LANE
    n_steps = s // ROW_TILE

    def body(dqm_ref, dkm_ref, dvm_ref, dqd_ref, dkd_ref, dvd_ref, q_ref, kv_ref, kpe_ref, qd_ref, kd_ref, tab_ref,
             gq_ref, gk_ref, gkpe_ref, gdq_ref, gdk_ref,
             segq_ref, expq_ref, invq_ref, segk_ref, expk_ref, invk_ref, segd_ref, expd_ref, invd_ref, foldq_ref, foldd_ref,
             dq_ref, dkv_ref, dkpe_ref, dqdo_ref, dkdo_ref, dvdo_ref, dg_ref, acc_ref):
        i = pl.program_id(0)

        @pl.when(i == 0)
        def _():
            acc_ref[...] = jnp.zeros_like(acc_ref)

        tab_v = tab_ref[...]
        cos_d, sin_d = _tile_lanes(tab_v[:, 0:LANE], DIL_W // LANE), _tile_lanes(tab_v[:, LANE:2 * LANE], DIL_W // LANE)
        cos_q1, sin_q1 = tab_v[:, 2 * LANE:3 * LANE], tab_v[:, 3 * LANE:4 * LANE]
        cos_q, sin_q = _tile_lanes(cos_q1, HEADS), _tile_lanes(sin_q1, HEADS)

        def norm_bwd(x, dyg, gain, seg, exp, inv):
            rinv = _seg_rinv(x, seg, exp, inv)
            xn = x * rinv
            dxn = dyg * gain
            dx = rinv * (dxn - xn * _seg_mean(dxn * xn, seg, exp, inv))
            return dx, jnp.sum(dyg * xn, axis=0, keepdims=True)

        dq, gq_l = norm_bwd(q_ref[...], _rope_bwd(dqm_ref[...], cos_q, sin_q, ROPE // 2), gq_ref[...],
                            segq_ref[...], expq_ref[...], invq_ref[...])
        dq_ref[...] = dq.astype(BF16)

        dkm = dkm_ref[...]
        kv = kv_ref[...]
        dkp, gk_l = norm_bwd(kv[:, :hw], dkm, gk_ref[...], segk_ref[...], expk_ref[...], invk_ref[...])
        dkv_ref[:, :hw] = dkp.astype(BF16)
        dkv_ref[:, hw:] = dvm_ref[...].astype(BF16)

        dkpe_r = dkm[:, 0:LANE]
        for h in range(1, HEADS):
            dkpe_r = dkpe_r + dkm[:, h * LANE:(h + 1) * LANE]
        dkpe_r = jnp.where(_pe_lane_mask(LANE), dkpe_r, 0.0)
        dyg = _rope_bwd(dkpe_r, cos_q1, sin_q1, ROPE // 2)
        kpe = kpe_ref[...]
        r_pe = lax.rsqrt(jnp.sum(kpe * kpe, axis=-1, keepdims=True) * (1.0 / ROPE) + EPS)
        xn = kpe * r_pe
        dxn = dyg * gkpe_ref[...]
        dkpe = r_pe * (dxn - xn * (jnp.sum(dxn * xn, axis=-1, keepdims=True) * (1.0 / ROPE)))
        dkpe_ref[...] = dkpe.astype(BF16)
        gkpe_l = jnp.sum(dyg * xn, axis=0, keepdims=True)

        dqd_v, gdq_l = norm_bwd(qd_ref[...], _rope_bwd(dqd_ref[...], cos_d, sin_d, DIL_DIM // 2), gdq_ref[...],
                                segd_ref[...], expd_ref[...], invd_ref[...])
        dqdo_ref[...] = dqd_v.astype(BF16)
        dkd_v, gdk_l = norm_bwd(kd_ref[...], _rope_bwd(dkd_ref[...], cos_d, sin_d, DIL_DIM // 2), gdk_ref[...],
                                segd_ref[...], expd_ref[...], invd_ref[...])
        dkdo_ref[...] = dkd_v.astype(BF16)
        dvdo_ref[...] = dvd_ref[...].astype(BF16)

        acc_ref[0:1, :] += gq_l
        acc_ref[1:2, :] += gk_l
        acc_ref[2:3, 0:LANE] += gkpe_l
        acc_ref[3:4, 0:DIL_W] += gdq_l
        acc_ref[4:5, 0:DIL_W] += gdk_l

        @pl.when(i == n_steps - 1)
        def _():
            acc = acc_ref[...]
            fq = jnp.dot(acc, foldq_ref[...], precision=HIGHEST, preferred_element_type=F32)
            fd = jnp.dot(acc[:, 0:DIL_W], foldd_ref[...], precision=HIGHEST, preferred_element_type=F32)
            rows = lax.broadcasted_iota(I32, (8, LANE), 0)
            base = jnp.where(rows < 2, fq, jnp.where(rows == 2, acc[:, 0:LANE], fd))
            at0 = pltpu.roll(base, LANE - KPE_OFF, 1)
            dg_ref[...] = jnp.where(rows == 5, pltpu.roll(at0, 5, 0), jnp.where(rows == 2, at0, base))

    t = ROW_TILE
    row = lambda w, cb=0: pl.BlockSpec((t, w), lambda i: (i, cb))
    c = consts
    return pl.pallas_call(
        body, name="attn_prep_bwd", grid=(n_steps,),
        in_specs=[row(hw), row(hw), row(DIL_W), row(DIL_W), row(DIL_W), row(DIL_W),
                  row(hw), row(hw + DIL_W), row(LANE, P_KPE // LANE), row(DIL_W, P_QD // DIL_W), row(DIL_W, P_KD // DIL_W),
                  row(4 * LANE),
                  _full((1, hw)), _full((1, hw)), _full((1, LANE)), _full((1, DIL_W)), _full((1, DIL_W)),
                  _full((hw, LANE)), _full((LANE, hw)), _full((1, LANE)), _full((hw, LANE)), _full((LANE, hw)), _full((1, LANE)),
                  _full((DIL_W, LANE)), _full((LANE, DIL_W)), _full((1, LANE)), _full((hw, LANE)), _full((DIL_W, LANE))],
        out_specs=[row(hw), row(hw + DIL_W), row(LANE), row(DIL_W), row(DIL_W), row(DIL_W), _full((8, LANE))],
        out_shape=[jax.ShapeDtypeStruct((s, hw), BF16), jax.ShapeDtypeStruct((s, hw + DIL_W), BF16),
                   jax.ShapeDtypeStruct((s, LANE), BF16)] + [jax.ShapeDtypeStruct((s, DIL_W), BF16)] * 3
        + [jax.ShapeDtypeStruct((8, LANE), F32)],
        scratch_shapes=[pltpu.VMEM((8, hw), F32)],
        compiler_params=_params(("arbitrary",), 28 << 20),
    )(*_in_hbm(dqm, dkm, dvm, dqd, dkd, dvd, q_raw, kv_raw, proj, proj, proj, tab),
      gains["q"], gains["k"], gains["kpe"], gains["dq"], gains["dk"],
      c["seg_q"], c["exp_q"], c["inv_q"], c["seg_k"], c["exp_k"], c["inv_k"], c["seg_d"], c["exp_d"], c["inv_d"],
      c["fold_q"], c["fold_d"])


def _latnorm_bwd(dql, dkvl, proj, g_q, g_kv):
    s = proj.shape[0]
    n_steps = s // ROW_TILE

    def body(dql_ref, dkvl_ref, q_ref, kv_ref, gq_ref, gkv_ref, dq_ref, dkv_ref, dg_ref):
        i = pl.program_id(0)

        @pl.when(i == 0)
        def _():
            dg_ref[...] = jnp.zeros_like(dg_ref)

        def one(x, dyg, gain):
            r = _rms(x)
            xn = x * r
            dxn = dyg * gain
            dx = r * (dxn - xn * jnp.mean(dxn * xn, axis=-1, keepdims=True))
            return dx, jnp.sum(dyg * xn, axis=0, keepdims=True)

        dq, gq_l = one(q_ref[...], dql_ref[...], gq_ref[...])
        dkv, gkv_l = one(kv_ref[...], dkvl_ref[...], gkv_ref[...])
        dq_ref[...] = dq.astype(BF16)
        dkv_ref[...] = dkv.astype(BF16)
        dg_ref[0:1, :] += gq_l
        dg_ref[1:2, 0:KV_LORA] += gkv_l

    t = ROW_TILE
    return pl.pallas_call(
        body, name="latnorm_bwd", grid=(n_steps,),
        in_specs=[pl.BlockSpec((t, Q_LORA), lambda i: (i, 0)), pl.BlockSpec((t, KV_LORA), lambda i: (i, 0)),
                  pl.BlockSpec((t, Q_LORA), lambda i: (i, P_QLAT // Q_LORA)),
                  pl.BlockSpec((t, KV_LORA), lambda i: (i, P_KVLAT // KV_LORA)),
                  _full((1, Q_LORA)), _full((1, KV_LORA))],
        out_specs=[pl.BlockSpec((t, Q_LORA), lambda i: (i, 0)), pl.BlockSpec((t, KV_LORA), lambda i: (i, 0)), _full((8, Q_LORA))],
        out_shape=[jax.ShapeDtypeStruct((s, Q_LORA), BF16), jax.ShapeDtypeStruct((s, KV_LORA), BF16),
                   jax.ShapeDtypeStruct((8, Q_LORA), F32)],
        compiler_params=_params(("arbitrary",)),
    )(dql, dkvl, proj, proj, g_q, g_kv)


def _resid_prenorm(x, mix, g1, gain, scale, shift):
    s, d = x.shape

    def body(x_ref, mix_ref, g1_ref, g_ref, sc_ref, sh_ref, x1_ref, h_ref):
        x1 = x_ref[...] + g1_ref[...] * mix_ref[...]
        x1_ref[...] = x1
        h_ref[...] = ((x1 * _rms(x1)) * g_ref[...] * (1.0 + sc_ref[...]) + sh_ref[...]).astype(BF16)

    row = pl.BlockSpec((ROW_TILE, d), lambda i: (i, 0))
    vec = _full((1, d))
    return pl.pallas_call(
        body, name="resid_prenorm", grid=(s // ROW_TILE,),
        in_specs=[row, row, vec, vec, vec, vec], out_specs=[row, row],
        out_shape=[jax.ShapeDtypeStruct((s, d), F32), jax.ShapeDtypeStruct((s, d), BF16)],
        compiler_params=_params(("parallel",)),
    )(x, mix, g1, gain, scale, shift)


CONV_TILE = 1408
HALO = 8


def _shift_down(x, halo, k):
    t = x.shape[0]
    row = lax.broadcasted_iota(I32, (t, 1), 0)
    out = pltpu.roll(x, k, 0)
    for r in range(k):
        out = jnp.where(row == r, halo[HALO - k + r:HALO - k + r + 1, :], out)
    return out


def _shift_up(x, halo, k):
    t = x.shape[0]
    row = lax.broadcasted_iota(I32, (t, 1), 0)
    out = pltpu.roll(x, t - k, 0)
    for r in range(k):
        out = jnp.where(row == t - k + r, halo[r:r + 1, :], out)
    return out


def _conv_fwd(x, halo, w, b):
    p1, p2 = _shift_down(x, halo, 1), _shift_down(x, halo, 2)
    u = b + p2 * w[0:1, :]
    u = u + p1 * w[1:2, :]
    u = u + x * w[2:3, :]
    return u, p1, p2


def _sigmoid(x):
    return 1.0 / (1.0 + jnp.exp(-x))


def _conv_gate(up, w_conv, b_conv):
    s = up.shape[0]
    t = ROW_TILE
    nj = D_FF // CONV_TILE
    hb = t // HALO

    def body(g_ref, v_ref, gh_ref, vh_ref, wg_ref, wv_ref, bg_ref, bv_ref, a_ref):
        live = (pl.program_id(0) > 0).astype(F32)
        ug, _, _ = _conv_fwd(g_ref[...], gh_ref[...] * live, wg_ref[...], bg_ref[...])
        uv, _, _ = _conv_fwd(v_ref[...], vh_ref[...] * live, wv_ref[...], bv_ref[...])
        a_ref[...] = (ug * _sigmoid(ug) * uv).astype(BF16)

    main = lambda off: pl.BlockSpec((t, CONV_TILE), lambda i, j: (i, j + off))
    halo = lambda off: pl.BlockSpec((HALO, CONV_TILE), lambda i, j: (jnp.maximum(i * hb - 1, 0), j + off))
    wsp = lambda off: pl.BlockSpec((3, CONV_TILE), lambda i, j: (0, j + off))
    bsp = lambda off: pl.BlockSpec((1, CONV_TILE), lambda i, j: (0, j + off))
    return pl.pallas_call(
        body, name="conv_gate", grid=(s // t, nj),
        in_specs=[main(0), main(nj), halo(0), halo(nj), wsp(0), wsp(nj), bsp(0), bsp(nj)],
        out_specs=pl.BlockSpec((t, CONV_TILE), lambda i, j: (i, j)),
        out_shape=jax.ShapeDtypeStruct((s, D_FF), BF16),
        compiler_params=_params(("parallel", "parallel"), 12 << 20),
    )(up, up, up, up, w_conv, w_conv, b_conv, b_conv)


def _gate_bwd(up, da, w_conv, b_conv):
    s = up.shape[0]
    t = ROW_TILE
    nj = D_FF // CONV_TILE
    hb = t // HALO
    n_i = s // t

    def body(g_ref, v_ref, gh_ref, vh_ref, gn_ref, vn_ref, da_ref, dan_ref, wg_ref, wv_ref, bg_ref, bv_ref,
             dupg_ref, dupv_ref, dbg_ref, dbv_ref, dwg_ref, dwv_ref):
        i = pl.program_id(1)

        @pl.when(i == 0)
        def _():
            for r in (dbg_ref, dbv_ref, dwg_ref, dwv_ref):
                r[...] = jnp.zeros_like(r)

        def d_gate(ug, uv, da_v):
            sg = _sigmoid(ug)
            return da_v * uv * (sg * (1.0 + ug * (1.0 - sg))), da_v * (ug * sg)

        live = (i > 0).astype(F32)
        xg, xv = g_ref[...], v_ref[...]
        wg, wv = wg_ref[...], wv_ref[...]
        ug, g1, g2 = _conv_fwd(xg, gh_ref[...] * live, wg, bg_ref[...])
        uv, v1, v2 = _conv_fwd(xv, vh_ref[...] * live, wv, bv_ref[...])
        dug, duv = d_gate(ug, uv, da_ref[...])

        more = (i < n_i - 1).astype(F32)
        ug_n, _, _ = _conv_fwd(gn_ref[...], xg[t - HALO:, :], wg, bg_ref[...])
        uv_n, _, _ = _conv_fwd(vn_ref[...], xv[t - HALO:, :], wv, bv_ref[...])
        dug_n, duv_n = d_gate(ug_n, uv_n, dan_ref[...] * more)

        def conv_t(du, du_n, w):
            return du * w[2:3, :] + _shift_up(du, du_n, 1) * w[1:2, :] + _shift_up(du, du_n, 2) * w[0:1, :]

        dupg_ref[...] = conv_t(dug, dug_n, wg).astype(BF16)
        dupv_ref[...] = conv_t(duv, duv_n, wv).astype(BF16)
        csum = lambda z: jnp.sum(z, axis=0, keepdims=True)
        dbg_ref[...] += csum(dug)
        dbv_ref[...] += csum(duv)
        dwg_ref[0:1, :] += csum(dug * g2)
        dwg_ref[1:2, :] += csum(dug * g1)
        dwg_ref[2:3, :] += csum(dug * xg)
        dwv_ref[0:1, :] += csum(duv * v2)
        dwv_ref[1:2, :] += csum(duv * v1)
        dwv_ref[2:3, :] += csum(duv * xv)

    last_halo = s // HALO - 1
    main = lambda off: pl.BlockSpec((t, CONV_TILE), lambda j, i: (i, j + off))
    halo = lambda off: pl.BlockSpec((HALO, CONV_TILE), lambda j, i: (jnp.maximum(i * hb - 1, 0), j + off))
    nxt = lambda off: pl.BlockSpec((HALO, CONV_TILE), lambda j, i: (jnp.minimum((i + 1) * hb, last_halo), j + off))
    wsp = lambda off: pl.BlockSpec((3, CONV_TILE), lambda j, i: (0, j + off))
    bsp = lambda off: pl.BlockSpec((1, CONV_TILE), lambda j, i: (0, j + off))
    outs = pl.pallas_call(
        body, name="gate_bwd", grid=(nj, n_i),
        in_specs=[main(0), main(nj), halo(0), halo(nj), nxt(0), nxt(nj), main(0), nxt(0),
                  wsp(0), wsp(nj), bsp(0), bsp(nj)],
        out_specs=[main(0), main(0),
                   pl.BlockSpec((1, CONV_TILE), lambda j, i: (0, j)), pl.BlockSpec((1, CONV_TILE), lambda j, i: (0, j)),
                   pl.BlockSpec((3, CONV_TILE), lambda j, i: (0, j)), pl.BlockSpec((3, CONV_TILE), lambda j, i: (0, j))],
        out_shape=[jax.ShapeDtypeStruct((s, D_FF), BF16), jax.ShapeDtypeStruct((s, D_FF), BF16),
                   jax.ShapeDtypeStruct((1, D_FF), F32), jax.ShapeDtypeStruct((1, D_FF), F32),
                   jax.ShapeDtypeStruct((3, D_FF), F32), jax.ShapeDtypeStruct((3, D_FF), F32)],
        compiler_params=_params(("parallel", "arbitrary"), 24 << 20),
    )(up, up, up, up, up, up, da, da, w_conv, w_conv, b_conv, b_conv)
    return outs


def _final(x1, ffn, tgt, g2):
    s, d = x1.shape
    n_steps = s // ROW_TILE

    def body(x1_ref, f_ref, t_ref, g2_ref, dy_ref, df_ref, dg2_ref, loss_ref, lacc_ref):
        i = pl.program_id(0)

        @pl.when(i == 0)
        def _():
            dg2_ref[...] = jnp.zeros_like(dg2_ref)
            lacc_ref[...] = jnp.zeros_like(lacc_ref)

        f = f_ref[...]
        e = x1_ref[...] + g2_ref[...] * f - t_ref[...]
        dy = e * (1.0 / d)
        dy_ref[...] = dy
        df_ref[...] = (dy * g2_ref[...]).astype(BF16)
        dg2_ref[...] += jnp.sum(dy * f, axis=0, keepdims=True)
        lacc_ref[...] += jnp.sum(e * e, axis=0, keepdims=True)

        @pl.when(i == n_steps - 1)
        def _():
            loss_ref[...] = jnp.sum(lacc_ref[...], axis=1, keepdims=True) * (0.5 / d)

    row = pl.BlockSpec((ROW_TILE, d), lambda i: (i, 0))
    return pl.pallas_call(
        body, name="final", grid=(n_steps,),
        in_specs=[row, row, row, _full((1, d))],
        out_specs=[row, row, _full((1, d)), _full((1, 1))],
        out_shape=[jax.ShapeDtypeStruct((s, d), F32), jax.ShapeDtypeStruct((s, d), BF16),
                   jax.ShapeDtypeStruct((1, d), F32), jax.ShapeDtypeStruct((1, 1), F32)],
        scratch_shapes=[pltpu.VMEM((1, d), F32)],
        compiler_params=_params(("arbitrary",)),
    )(x1, ffn, tgt, g2)


def _ffnnorm_bwd(dh2, x1, dy, mix, gain, scale, g1):
    s, d = x1.shape
    n_steps = s // ROW_TILE

    def body(dh_ref, x_ref, dy_ref, mix_ref, g_ref, sc_ref, g1_ref, dx_ref, dm_ref, acc_ref):
        i = pl.program_id(0)

        @pl.when(i == 0)
        def _():
            acc_ref[...] = jnp.zeros_like(acc_ref)

        dh, x = dh_ref[...], x_ref[...]
        r = _rms(x)
        xn = x * r
        dn = dh * (1.0 + sc_ref[...])
        dxn = dn * g_ref[...]
        dx = dy_ref[...] + r * (dxn - xn * jnp.mean(dxn * xn, axis=-1, keepdims=True))
        dx_ref[...] = dx
        dm_ref[...] = (dx * g1_ref[...]).astype(BF16)
        csum = lambda z: jnp.sum(z, axis=0, keepdims=True)
        acc_ref[0:1, :] += csum(dh)
        acc_ref[1:2, :] += csum(dh * (xn * g_ref[...]))
        acc_ref[2:3, :] += csum(dn * xn)
        acc_ref[3:4, :] += csum(dx * mix_ref[...])

    row = pl.BlockSpec((ROW_TILE, d), lambda i: (i, 0))
    vec = _full((1, d))
    return pl.pallas_call(
        body, name="ffnnorm_bwd", grid=(n_steps,),
        in_specs=[row, row, row, row, vec, vec, vec],
        out_specs=[row, row, _full((8, d))],
        out_shape=[jax.ShapeDtypeStruct((s, d), F32), jax.ShapeDtypeStruct((s, d), BF16), jax.ShapeDtypeStruct((8, d), F32)],
        compiler_params=_params(("arbitrary",)),
    )(dh2, x1, dy, mix, gain, scale, g1)


def _mixnorm_bwd(dh, x, dx1, gain, scale):
    s, d = x.shape
    n_steps = s // ROW_TILE

    def body(dh_ref, x_ref, dx1_ref, g_ref, sc_ref, gx_ref, acc_ref):
        i = pl.program_id(0)

        @pl.when(i == 0)
        def _():
            acc_ref[...] = jnp.zeros_like(acc_ref)

        dh, x = dh_ref[...], x_ref[...]
        r = _rms(x)
        xn = x * r
        dn = dh * (1.0 + sc_ref[...])
        dxn = dn * g_ref[...]
        gx_ref[...] = dx1_ref[...] + r * (dxn - xn * jnp.mean(dxn * xn, axis=-1, keepdims=True))
        csum = lambda z: jnp.sum(z, axis=0, keepdims=True)
        acc_ref[0:1, :] += csum(dh)
        acc_ref[1:2, :] += csum(dh * (xn * g_ref[...]))
        acc_ref[2:3, :] += csum(dn * xn)

    row = pl.BlockSpec((ROW_TILE, d), lambda i: (i, 0))
    vec = _full((1, d))
    return pl.pallas_call(
        body, name="mixnorm_bwd", grid=(n_steps,),
        in_specs=[row, row, row, vec, vec],
        out_specs=[row, _full((8, d))],
        out_shape=[jax.ShapeDtypeStruct((s, d), F32), jax.ShapeDtypeStruct((8, d), F32)],
        compiler_params=_params(("arbitrary",)),
    )(dh, x, dx1, gain, scale)


def _key_count(d, dilated):
    if not dilated:
        return jnp.where(d >= 0, 1.0, 0.0)
    one = lambda cond: jnp.where(cond, 1.0, 0.0)
    cnt = one(d <= 128) + one(((d & 3) == 0) & (d <= 512)) + one((d & 15) == 0)
    return jnp.where(d >= 0, cnt, 0.0)


def _block_kinds(mla):
    return (0, "diag", "none") if mla else (512, "near", "far")


NEAR_OFFSETS = 4


def _scores_t(ka, qa, scale, kind, rel_t, offset, near_tabs=None):
    return _mask_scores(lax.dot_general(ka, qa, NT, preferred_element_type=F32), scale, kind, rel_t, offset, near_tabs)


def _fill_near_tables(bias_ref, cnt_ref, rel_t):
    for idx in range(NEAR_OFFSETS):
        cnt = _key_count(rel_t + (idx - 1) * ATT_TK, True)
        cnt_ref[idx] = cnt
        bias_ref[idx] = jnp.where(cnt > 0.0, 0.0, NEG_INF)


def _mask_scores(products, scale, kind, rel_t, offset, near_tabs=None):
    st = products * (scale * LOG2E)
    cnt = None
    if kind == "diag":
        st = jnp.where(rel_t + offset >= 0, st, NEG_INF)
    elif kind == "far":
        st = jnp.where((rel_t & 15) == 0, st, NEG_INF)
    elif kind == "near":
        bias_ref, cnt_ref = near_tabs
        idx = offset // ATT_TK + 1
        st = st + bias_ref[idx]
        cnt = cnt_ref[idx]
    return st, cnt


def _attn_fwd(q, k, v, mla, scale, name, gather=()):
    s = q.shape[0]
    qw = 2 * LANE if mla else LANE
    tq, tk = ATT_TQ, ATT_TK
    reach, kind_near, kind_far = _block_kinds(mla)
    assert s % tq == 0 and tq % tk == 0 and reach % tk == 0 and (mla or (reach + tq) // tk == NEAR_OFFSETS)
    ng = len(gather)
    last_step = HEADS // 2 - 1

    def body(*refs):
        q_ref, k_ref, v_ref = refs[:3]
        o_ref, lse_ref = refs[3 + ng:5 + ng]
        vt_ref, st_ref = refs[5 + 2 * ng:7 + 2 * ng]
        near_tabs = None if mla else refs[7 + 2 * ng:9 + 2 * ng]
        n_tabs = 0 if mla else 2
        comm = (refs[3:3 + ng], refs[5 + ng:5 + 2 * ng]) + tuple(refs[7 + n_tabs + 2 * ng:])
        if ng:
            @pl.when(pl.program_id(0) == 0)
            def _():
                _Gather(*comm).start()

            @pl.when(pl.program_id(0) == last_step)
            def _():
                _Gather(*comm).forward()

        lane = lax.broadcasted_iota(I32, (1, LANE), 1)
        rel_t = lax.broadcasted_iota(I32, (tk, tq), 1) - lax.broadcasted_iota(I32, (tk, tq), 0)
        if not mla:
            _fill_near_tables(*near_tabs, rel_t)

        def transpose_v(j, carry):
            c0 = pl.multiple_of(j * tk, tk)
            vt_ref[:, pl.ds(c0, tk)] = v_ref[pl.ds(c0, tk), :].astype(F32).T.astype(BF16)
            return carry

        lax.fori_loop(0, s // tk, transpose_v, 0)

        def q_block(qi, carry):
            r0 = pl.multiple_of(qi * tq, tq)
            kcols = [slice(a * LANE, (a + 1) * LANE) if mla else slice(0, LANE) for a in range(2)]
            qas = [q_ref[pl.ds(r0, tq), kcols[a]] for a in range(2)]
            if not mla:
                qas = [jnp.where(lane < DIL_DIM, qas[0], jnp.zeros_like(qas[0])),
                       jnp.where(lane >= DIL_DIM, qas[1], jnp.zeros_like(qas[1]))]

            n_k = (r0 + tq) // tk

            def products(kj):
                c0 = pl.multiple_of(kj * tk, tk)
                return [lax.dot_general(k_ref[pl.ds(c0, tk), kcols[a]], qas[a], NT, preferred_element_type=F32)
                        for a in range(2)]

            for a, pr in enumerate(products(0)):
                st_ref[0, a] = pr

            def k_block(kj, c, kind):
                c0 = pl.multiple_of(kj * tk, tk)
                slot = kj & 1
                ahead = products(jnp.minimum(kj + 1, n_k - 1))
                out = []
                for a in range(2):
                    m, l, acc = c[a]
                    st, cnt = _mask_scores(st_ref[slot, a], scale, kind, rel_t, r0 - c0, near_tabs)
                    st_ref[1 - slot, a] = ahead[a]
                    m_new = jnp.maximum(m, jnp.max(st, axis=0, keepdims=True))
                    alpha = jnp.exp2(m - m_new)
                    p = jnp.exp2(st - m_new)
                    if cnt is not None:
                        p = p * cnt
                    l = alpha * l + jnp.sum(p, axis=0, keepdims=True)
                    vt = vt_ref[a * DIL_DIM:(a + 1) * DIL_DIM, pl.ds(c0, tk)]
                    acc = alpha * acc + jnp.dot(vt, p.astype(BF16), preferred_element_type=F32)
                    out.append((m_new, l, acc))
                return tuple(out)

            one = (jnp.full((1, tq), NEG_INF, F32), jnp.zeros((1, tq), F32), jnp.zeros((DIL_DIM, tq), F32))
            first_near = jnp.maximum((r0 - reach) // tk, 0)
            c = lax.fori_loop(0, first_near, functools.partial(k_block, kind=kind_far), (one, one))
            res = lax.fori_loop(first_near, (r0 + tq) // tk, functools.partial(k_block, kind=kind_near), c)
            o_t = jnp.concatenate([res[a][2] / res[a][1] for a in range(2)], axis=0)
            o_ref[pl.ds(r0, tq), :] = o_t.T.astype(BF16)
            for a in range(2):
                lse_ref[a, :, pl.ds(r0, tq)] = res[a][0] * LN2 + jnp.log(res[a][1])
            return carry

        lax.fori_loop(0, s // tq, q_block, 0)

        if ng:
            @pl.when(pl.program_id(0) == last_step)
            def _():
                _Gather(*comm).finish()

    return pl.pallas_call(
        body, name=name, grid=(HEADS // 2,),
        in_specs=[pl.BlockSpec((s, qw), lambda h: (0, h)), pl.BlockSpec((s, qw), lambda h: (0, h)),
                  pl.BlockSpec((s, LANE), lambda h: (0, h))] + [ANY] * ng,
        out_specs=[pl.BlockSpec((s, LANE), lambda h: (0, h)), pl.BlockSpec((2, 1, s), lambda h: (h, 0, 0))] + [ANY] * ng,
        out_shape=[jax.ShapeDtypeStruct((s, DIL_W), BF16), jax.ShapeDtypeStruct((HEADS, 1, s), F32)] + _Gather.out_shapes(gather),
        scratch_shapes=[pltpu.VMEM((LANE, s), BF16), pltpu.VMEM((2, 2, tk, tq), F32)]
        + ([] if mla else [pltpu.VMEM((NEAR_OFFSETS, tk, tq), F32)] * 2) + (_Gather.scratch(gather) if ng else []),
        compiler_params=_params(("arbitrary",) if ng else ("parallel",), 12 << 20),
    )(*_in_hbm(q, k, v), *gather)


def _attn_bwd(q, k, v, o, do, do_block0, lse, mla, scale, name, scatter=()):
    s = q.shape[0]
    qw = 2 * LANE if mla else LANE
    tq, tk = ATT_TQ, ATT_TK
    nq = s // tq
    reach, kind_near, kind_far = _block_kinds(mla)
    assert s % tq == 0 and tq % tk == 0
    ns = len(scatter)
    last_step = HEADS // 2 - 1

    def body(*refs):
        q_ref, k_ref, v_ref, o_ref, do_ref, lse_ref = refs[:6]
        dq_ref, dk_ref, dv_ref = refs[6 + ns:9 + ns]
        kt_ref, dot_ref, dob_ref, dqt_ref, delta_ref, lse2_ref = refs[9 + 2 * ns:15 + 2 * ns]
        near_tabs = None if mla else refs[15 + 2 * ns:17 + 2 * ns]
        n_tabs = 0 if mla else 2
        comm = (refs[6:6 + ns], refs[9 + ns:9 + 2 * ns]) + tuple(refs[15 + n_tabs + 2 * ns:])
        if ns:
            @pl.when(pl.program_id(0) == 0)
            def _():
                _Scatter(*comm).start()

        lane = lax.broadcasted_iota(I32, (1, LANE), 1)
        row = lax.broadcasted_iota(I32, (LANE, 1), 0)
        rel_t = lax.broadcasted_iota(I32, (tk, tq), 1) - lax.broadcasted_iota(I32, (tk, tq), 0)
        if not mla:
            _fill_near_tables(*near_tabs, rel_t)

        def prepare(j, carry):
            c0 = pl.multiple_of(j * tk, tk)
            do_blk = do_ref[pl.ds(c0, tk), :]
            dob_ref[pl.ds(c0, tk), :] = do_blk.astype(BF16)
            do_t = do_blk.T
            dot_ref[:, pl.ds(c0, tk)] = do_t.astype(BF16)
            prod = do_t * o_ref[pl.ds(c0, tk), :].astype(F32).T
            delta_ref[0, :, pl.ds(c0, tk)] = jnp.sum(prod[0:DIL_DIM], axis=0, keepdims=True)
            delta_ref[1, :, pl.ds(c0, tk)] = jnp.sum(prod[DIL_DIM:LANE], axis=0, keepdims=True)
            for w in range(qw // LANE):
                kt_ref[w * LANE:(w + 1) * LANE, pl.ds(c0, tk)] = (
                    k_ref[pl.ds(c0, tk), w * LANE:(w + 1) * LANE].astype(F32).T.astype(BF16))
            return carry

        lax.fori_loop(0, s // tk, prepare, 0)
        dqt_ref[...] = jnp.zeros_like(dqt_ref)
        lse2_ref[...] = lse_ref[...] * LOG2E

        sels = [lane < DIL_DIM, lane >= DIL_DIM]
        rsels = [row < DIL_DIM, row >= DIL_DIM]
        cols = [slice(a * LANE, (a + 1) * LANE) if mla else slice(0, LANE) for a in range(2)]

        def k_block(kj, carry):
            c0 = pl.multiple_of(kj * tk, tk)
            kas = [k_ref[pl.ds(c0, tk), cols[a]] for a in range(2)]
            kts = [kt_ref[cols[a], pl.ds(c0, tk)] for a in range(2)]
            if not mla:
                kas = [jnp.where(sels[a], kas[a], jnp.zeros_like(kas[a])) for a in range(2)]
                kts = [jnp.where(rsels[a], kts[a], jnp.zeros_like(kts[a])) for a in range(2)]
            vb = v_ref[pl.ds(c0, tk), :]
            vbs = [jnp.where(sels[a], vb, jnp.zeros_like(vb)) for a in range(2)]

            first = c0 // tq

            def q_block(qi, c, kind):
                r0 = pl.multiple_of(qi * tq, tq)
                out, dq_parts = [], []
                for a in range(2):
                    dk_acc, dv_acc = c[a]
                    qa = q_ref[pl.ds(r0, tq), cols[a]]
                    st, cnt = _scores_t(kas[a], qa, scale, kind, rel_t, r0 - c0, near_tabs)
                    p = jnp.exp2(st - lse2_ref[a, :, pl.ds(r0, tq)])
                    if cnt is not None:
                        p = p * cnt
                    dp = jnp.dot(vbs[a], dot_ref[:, pl.ds(r0, tq)], preferred_element_type=F32)
                    ds = (p * (dp - delta_ref[a, :, pl.ds(r0, tq)]) * scale).astype(BF16)
                    dv_acc = dv_acc + jnp.dot(p.astype(BF16), dob_ref[pl.ds(r0, tq), :], preferred_element_type=F32)
                    dk_acc = dk_acc + jnp.dot(ds, qa, preferred_element_type=F32)
                    dq_parts.append(jnp.dot(kts[a], ds, preferred_element_type=F32))
                    out.append((dk_acc, dv_acc))
                if mla:
                    for a in range(2):
                        dqt_ref[cols[a], pl.ds(r0, tq)] += dq_parts[a]
                else:
                    dqt_ref[:, pl.ds(r0, tq)] += dq_parts[0] + dq_parts[1]
                return tuple(out)

            zero = jnp.zeros((tk, LANE), F32)
            last_near = jnp.minimum((c0 + tk - 1 + reach) // tq + 1, nq)
            c = lax.fori_loop(first, last_near, functools.partial(q_block, kind=kind_near), ((zero, zero), (zero, zero)))
            (dk0, dv0), (dk1, dv1) = lax.fori_loop(last_near, nq, functools.partial(q_block, kind=kind_far), c)
            if mla:
                dk_ref[pl.ds(c0, tk), cols[0]] = dk0
                dk_ref[pl.ds(c0, tk), cols[1]] = dk1
            else:
                dk_ref[pl.ds(c0, tk), :] = jnp.where(sels[0], dk0, dk1)
            dv_ref[pl.ds(c0, tk), :] = jnp.where(sels[0], dv0, dv1)
            return carry

        lax.fori_loop(0, s // tk, k_block, 0)

        def write_dq(j, carry):
            c0 = pl.multiple_of(j * tk, tk)
            for w in range(qw // LANE):
                dq_ref[pl.ds(c0, tk), w * LANE:(w + 1) * LANE] = dqt_ref[w * LANE:(w + 1) * LANE, pl.ds(c0, tk)].T
            return carry

        lax.fori_loop(0, s // tk, write_dq, 0)

        if ns:
            @pl.when(pl.program_id(0) == last_step)
            def _():
                _Scatter(*comm).finish()

    b0 = do_block0
    return pl.pallas_call(
        body, name=name, grid=(HEADS // 2,),
        in_specs=[pl.BlockSpec((s, qw), lambda h: (0, h)), pl.BlockSpec((s, qw), lambda h: (0, h)),
                  pl.BlockSpec((s, LANE), lambda h: (0, h)), pl.BlockSpec((s, LANE), lambda h: (0, h)),
                  pl.BlockSpec((s, LANE), lambda h: (0, h + b0)), pl.BlockSpec((2, 1, s), lambda h: (h, 0, 0))] + [ANY] * ns,
        out_specs=[pl.BlockSpec((s, qw), lambda h: (0, h)), pl.BlockSpec((s, qw), lambda h: (0, h)),
                   pl.BlockSpec((s, LANE), lambda h: (0, h))] + [ANY] * ns,
        out_shape=[jax.ShapeDtypeStruct(q.shape, F32), jax.ShapeDtypeStruct(k.shape, F32), jax.ShapeDtypeStruct((s, DIL_W), F32)]
        + _Scatter.out_shapes(scatter),
        scratch_shapes=[pltpu.VMEM((qw, s), BF16), pltpu.VMEM((LANE, s), BF16), pltpu.VMEM((s, LANE), BF16),
                        pltpu.VMEM((qw, s), F32), pltpu.VMEM((2, 1, s), F32), pltpu.VMEM((2, 1, s), F32)]
        + ([] if mla else [pltpu.VMEM((NEAR_OFFSETS, tk, tq), F32)] * 2) + (_Scatter.semaphores(ns) if ns else []),
        compiler_params=_params(("arbitrary",) if ns else ("parallel",), 24 << 20),
    )(*_in_hbm(q, k, v, o, do, lse), *scatter)


def _ada_bwd(c_all, dmod_shard):
    n, d = c_all.shape
    cols = dmod_shard.shape[1]

    def body(c_ref, g_ref, o_ref):
        cv = c_ref[...]
        o_ref[...] = lax.dot_general(cv * _sigmoid(cv), g_ref[...], TN, precision=HIGHEST, preferred_element_type=F32)

    return pl.pallas_call(
        body, name="ada_bwd", out_shape=jax.ShapeDtypeStruct((d, cols), F32),
        compiler_params=_params(None, 16 << 20),
    )(c_all, dmod_shard)


SMALL_WIDTHS = (("g_mix_norm", D_MODEL), ("g_q_lat", Q_LORA), ("g_kv_lat", KV_LORA), ("g_mla_q_nope", NOPE),
                ("g_mla_q_pe", ROPE), ("g_mla_k_nope", NOPE), ("g_mla_k_pe", ROPE), ("g_dil_q", DIL_DIM),
                ("g_dil_k", DIL_DIM), ("g_ffn_norm", D_MODEL), ("b_conv", UP_W))


def _small_layout():
    pieces = (("dmod", 6 * D_MODEL),) + SMALL_WIDTHS + tuple(("w_conv%d" % k, UP_W) for k in range(3)) + (("loss", 1),)
    layout, off = {}, 0
    for name, width in pieces:
        layout[name] = (width, off)
        off += -(-width // LANE) * LANE
    return layout, off


def _pack_small(acc1, acc2, dg2, dglat, dgains, dbg, dbv, dwg, dwv, loss_part):
    layout, total = _small_layout()

    def body(a1, a2, g2, gl, gg, bg, bv, wg, wv, ls, o_ref):
        o_ref[...] = jnp.zeros_like(o_ref)

        def put(name, src, shift=0):
            start = layout[name][1] + shift
            o_ref[:, start:start + src.shape[1]] = src

        for k, src in enumerate((a1[0:1, :], a1[1:2, :], a2[3:4, :], a2[0:1, :], a2[1:2, :], g2[...])):
            put("dmod", src, k * D_MODEL)
        put("g_mix_norm", a1[2:3, :])
        put("g_q_lat", gl[0:1, :])
        put("g_kv_lat", gl[1:2, 0:KV_LORA])
        put("g_mla_q_nope", gg[0:1, 0:NOPE])
        put("g_mla_q_pe", gg[5:6, 0:ROPE])
        put("g_mla_k_nope", gg[1:2, 0:NOPE])
        put("g_mla_k_pe", gg[2:3, 0:ROPE])
        put("g_dil_q", gg[3:4, 0:DIL_DIM])
        put("g_dil_k", gg[4:5, 0:DIL_DIM])
        put("g_ffn_norm", a2[2:3, :])
        put("b_conv", bg[...])
        put("b_conv", bv[...], D_FF)
        for k in range(3):
            put("w_conv%d" % k, wg[k:k + 1, :])
            put("w_conv%d" % k, wv[k:k + 1, :], D_FF)
        put("loss", ls[...])

    ins = (acc1, acc2, dg2, dglat, dgains, dbg, dbv, dwg, dwv, loss_part)
    return pl.pallas_call(
        body, name="pack_small", grid=(1,), in_specs=[_full(a.shape) for a in ins], out_specs=_full((1, total)),
        out_shape=jax.ShapeDtypeStruct((1, total), F32),
        compiler_params=_params(("arbitrary",), 2 << 20),
    )(*_in_hbm(*ins))


def _sum_unpack(g):
    n_dev, _, total = g.shape
    layout, _ = _small_layout()

    def body(g_ref, *refs):
        o_refs, s_ref = refs[:-1], refs[-1]
        acc = g_ref[0]
        for k in range(1, n_dev):
            acc = acc + g_ref[k]
        s_ref[...] = acc
        take = lambda name: s_ref[:, layout[name][1]:layout[name][1] + layout[name][0]]
        o_refs[0][...] = take("dmod")
        for i, (name, _) in enumerate(SMALL_WIDTHS):
            o_refs[1 + i][...] = take(name)
        for k in range(3):
            o_refs[-2][k:k + 1, :] = take("w_conv%d" % k)
        o_refs[-1][...] = take("loss")

    shapes = [(1, 6 * D_MODEL)] + [(1, w) for _, w in SMALL_WIDTHS] + [(3, UP_W), (1, 1)]
    return pl.pallas_call(
        body, name="sum_unpack", out_shape=[jax.ShapeDtypeStruct(sh, F32) for sh in shapes],
        scratch_shapes=[pltpu.VMEM((1, total), F32)],
        compiler_params=_params(None, 4 << 20),
    )(g)


def _adamw_math(w, g, m, v):
    mn = ADAM_B1 * m + (1.0 - ADAM_B1) * g
    vn = ADAM_B2 * v + (1.0 - ADAM_B2) * (g * g)
    m_hat = mn / (1.0 - ADAM_B1 ** ADAM_STEP)
    v_hat = vn / (1.0 - ADAM_B2 ** ADAM_STEP)
    return -ADAM_LR * (m_hat / (jnp.sqrt(v_hat) + ADAM_EPS) + ADAM_WD * w), mn, vn


def _adamw_vectors(ws, gs, ms, vs):
    k = len(ws)

    def body(*refs):
        for i in range(k):
            d, mn, vn = _adamw_math(refs[i][...], refs[k + i][...], refs[2 * k + i][...], refs[3 * k + i][...])
            refs[4 * k + i][...] = d
            refs[5 * k + i][...] = mn
            refs[6 * k + i][...] = vn

    blocks = [_full(w.shape) for w in ws]
    outs = pl.pallas_call(
        body, name="adamw_vectors", grid=(1,), in_specs=blocks * 4, out_specs=blocks * 3,
        out_shape=[jax.ShapeDtypeStruct(w.shape, F32) for w in ws] * 3,
        compiler_params=_params(("arbitrary",), 2 << 20),
    )(*_in_hbm(*ws, *gs, *ms, *vs))
    return outs[:k], outs[k:2 * k], outs[2 * k:]


def _adamw(w, g, m, v, name):
    r, c = w.shape
    tr = r
    for cand in (256, 128, 64, 32, 16):
        if r % cand == 0 and r > cand:
            tr = cand
            break

    def body(w_ref, g_ref, m_ref, v_ref, d_ref, mo_ref, vo_ref):
        d_ref[...], mo_ref[...], vo_ref[...] = _adamw_math(w_ref[...], g_ref[...], m_ref[...], v_ref[...])

    blk = pl.BlockSpec((tr, c), lambda i: (i, 0))
    return pl.pallas_call(
        body, name=name, grid=(r // tr,), in_specs=[blk] * 4, out_specs=[blk] * 3,
        out_shape=[jax.ShapeDtypeStruct((r, c), F32)] * 3,
        compiler_params=_params(("parallel",), 7 * _nbytes((tr, c), F32)),
    )(w, g, m, v)


def _position():
    return lax.axis_index("x"), lax.axis_index("y"), lax.axis_index("c")


def _other_chips(x, y):
    return [(1 - x, y, 2 * (1 - x) + y), (x, 1 - y, 2 * x + (1 - y)), (1 - x, 1 - y, 2 * (1 - x) + (1 - y))]


class _SmallGather:
    def __init__(self, v_ref, out_ref, send_sems, recv_sems, local_sem):
        x, y, c = _position()
        me = 4 * x + 2 * y + c
        self.local = pltpu.make_async_copy(v_ref, out_ref.at[me], local_sem)
        self.sends, self.arrivals = [], []
        for k in range(N_DEV - 1):
            fx, fy, fc = ((k + 1) >> 2) & 1, ((k + 1) >> 1) & 1, (k + 1) & 1
            px, py, pc = (1 - x if fx else x), (1 - y if fy else y), (1 - c if fc else c)

            def copy(dst, k=k, peer=(px, py, pc)):
                return pltpu.make_async_remote_copy(src_ref=v_ref, dst_ref=dst, send_sem=send_sems.at[k],
                                                    recv_sem=recv_sems.at[k], device_id=peer, device_id_type=MESH)

            self.sends.append(copy(out_ref.at[me]))
            self.arrivals.append(copy(out_ref.at[4 * px + 2 * py + pc]))

    @staticmethod
    def semaphores():
        return [pltpu.SemaphoreType.DMA((N_DEV - 1,)), pltpu.SemaphoreType.DMA((N_DEV - 1,)), pltpu.SemaphoreType.DMA]

    def start(self):
        self.local.start()
        for cp in self.sends:
            cp.start()

    def finish(self):
        for cp in self.arrivals:
            cp.wait_recv()
        for cp in self.sends:
            cp.wait_send()
        self.local.wait()


def _prologue(c_taps, w_ada_shard, b_shard, pos_col, rope_consts, shards):
    n = len(shards)
    s = pos_col.shape[0]
    cols = w_ada_shard.shape[1]
    freq, csel, ssel = rope_consts

    def body(*refs):
        ct_ref, w_ref, b_ref, p_ref, f_ref, cs_ref, ss_ref = refs[:7]
        sh_refs = refs[7:7 + n]
        ct_all_ref, mod_all_ref, tab_ref = refs[7 + n:10 + n]
        g_refs = refs[10 + n:10 + 2 * n]
        mod_blk_ref = refs[10 + 2 * n]
        sems = refs[11 + 2 * n:]
        weights = _Gather(sh_refs, g_refs, *sems[6:])
        weights.start()
        first = _SmallGather(ct_ref, ct_all_ref, *sems[0:3])
        first.start()
        first.finish()
        cv = ct_all_ref[:, 0, 0:D_MODEL]
        sc = (cv * _sigmoid(cv)).astype(BF16)
        mod_blk_ref[...] = jnp.dot(sc, w_ref[...].astype(BF16), preferred_element_type=F32) + b_ref[...]
        second = _SmallGather(mod_blk_ref, mod_all_ref, *sems[3:6])
        second.start()

        def table_rows(i, carry):
            r0 = pl.multiple_of(i * ROW_TILE, ROW_TILE)
            ang = p_ref[pl.ds(r0, ROW_TILE), :].astype(F32) * f_ref[...]
            tab_ref[pl.ds(r0, ROW_TILE), :] = cs_ref[...] * jnp.cos(ang) + ss_ref[...] * jnp.sin(ang)
            return carry

        lax.fori_loop(0, s // ROW_TILE, table_rows, 0)
        second.finish()
        weights.forward()
        weights.finish()

    return pl.pallas_call(
        body, name="prologue",
        out_shape=[jax.ShapeDtypeStruct((N_DEV,) + c_taps.shape, F32), jax.ShapeDtypeStruct((N_DEV, N_DEV, cols), F32),
                   jax.ShapeDtypeStruct((s, 4 * LANE), F32)] + _Gather.out_shapes(shards),
        in_specs=[IN_VMEM] * 7 + [ANY] * n, out_specs=[IN_VMEM] * 3 + [ANY] * n,
        scratch_shapes=[pltpu.VMEM((N_DEV, cols), F32)] + _SmallGather.semaphores() * 2 + _Gather.scratch(shards),
        compiler_params=_params(None, 14 << 20),
    )(c_taps, w_ada_shard, b_shard, pos_col, freq, csel, ssel, *shards)


IN_VMEM = pl.BlockSpec(memory_space=pltpu.VMEM)
ANY = pl.BlockSpec(memory_space=pl.ANY)


class _Gather:
    def __init__(self, w_refs, out_refs, send_sems, recv_sems, own_sems, *bounce_refs):
        x, y, c = _position()
        q0 = 2 * x + y
        sibling = (x, y, 1 - c)
        self.ici, self.ici_in, self.fwd, self.fwd_in, self.own_in, self.own_out = [], [], [], [], [], []
        for k, (w_ref, out_ref) in enumerate(zip(w_refs, out_refs)):
            half = w_ref.shape[0] // 2
            self.own_in.append(pltpu.make_async_copy(w_ref, bounce_refs[k], own_sems.at[2 * k]))
            self.own_out.append(pltpu.make_async_copy(bounce_refs[k], out_ref.at[q0], own_sems.at[2 * k + 1]))

            def blk(q, e, out_ref=out_ref, half=half):
                return out_ref.at[q, pl.ds(pl.multiple_of(e * half, 16), half), :]

            def copy(src, dst, i, to):
                return pltpu.make_async_remote_copy(src_ref=src, dst_ref=dst, send_sem=send_sems.at[i], recv_sem=recv_sems.at[i],
                                                    device_id=to, device_id_type=MESH)

            src = w_ref.at[pl.ds(pl.multiple_of(c * half, 16), half), :]
            for j, (cx, cy, qj) in enumerate(_other_chips(x, y)):
                self.ici.append(copy(src, blk(q0, c), 6 * k + j, (cx, cy, c)))
                self.ici_in.append(copy(blk(qj, c), blk(qj, c), 6 * k + j, (cx, cy, c)))
                self.fwd.append(copy(blk(qj, c), blk(qj, c), 6 * k + 3 + j, sibling))
                self.fwd_in.append(copy(blk(qj, 1 - c), blk(qj, 1 - c), 6 * k + 3 + j, sibling))

    @staticmethod
    def out_shapes(shards):
        return [jax.ShapeDtypeStruct((N_CHIP,) + s.shape, s.dtype) for s in shards]

    @staticmethod
    def scratch(shards):
        n = len(shards)
        return ([pltpu.SemaphoreType.DMA((6 * n,)), pltpu.SemaphoreType.DMA((6 * n,)), pltpu.SemaphoreType.DMA((2 * n,))]
                + [pltpu.VMEM(s.shape, s.dtype) for s in shards])

    def start(self):
        for cp in self.ici + self.own_in:
            cp.start()

    def forward(self):
        for fetched, placed in zip(self.own_in, self.own_out):
            fetched.wait()
            placed.start()
        for arrived, onward in zip(self.ici_in, self.fwd):
            arrived.wait_recv()
            onward.start()

    def finish(self):
        for cp in self.fwd_in:
            cp.wait_recv()
        for cp in self.ici + self.fwd:
            cp.wait_send()
        for cp in self.own_out:
            cp.wait()


def _swap_halves_d2d(grads, name):
    n = len(grads)

    def body(*refs):
        swap = _PairSwap(refs[:n], refs[n:2 * n], *refs[2 * n:])
        swap.start()
        swap.finish()

    return pl.pallas_call(
        body, name=name,
        out_shape=_PairSwap.out_shapes(grads), in_specs=[ANY] * n, out_specs=[ANY] * n,
        scratch_shapes=_PairSwap.semaphores(n),
    )(*grads)


class _PairSwap:
    def __init__(self, g_refs, out_refs, send_sems, recv_sems):
        x, y, c = _position()
        self.copies = [
            pltpu.make_async_remote_copy(src_ref=g_ref.at[:, 1 - c], dst_ref=out_ref, send_sem=send_sems.at[k],
                                         recv_sem=recv_sems.at[k], device_id=(x, y, 1 - c), device_id_type=MESH)
            for k, (g_ref, out_ref) in enumerate(zip(g_refs, out_refs))]

    @staticmethod
    def out_shapes(grads):
        return [jax.ShapeDtypeStruct((N_CHIP,) + g.shape[2:], g.dtype) for g in grads]

    @staticmethod
    def semaphores(n):
        return [pltpu.SemaphoreType.DMA((n,)), pltpu.SemaphoreType.DMA((n,))]

    def start(self):
        for cp in self.copies:
            cp.start()

    def finish(self):
        for cp in self.copies:
            cp.wait_recv()
        for cp in self.copies:
            cp.wait_send()


def _pair_sum(g, a, c_idx, name):
    _, _, rh, cols = g.shape
    tr = rh
    for cand in (256, 128, 64, 32, 16):
        if rh % cand == 0 and rh > cand:
            tr = cand
            break

    def body(c_ref, g_ref, a_ref, o_ref):
        o_ref[...] = (g_ref[...] + a_ref[...]).astype(BF16)

    return pl.pallas_call(
        body, name=name,
        grid_spec=pltpu.PrefetchScalarGridSpec(
            num_scalar_prefetch=1, grid=(N_CHIP, rh // tr),
            in_specs=[pl.BlockSpec((None, None, tr, cols), lambda q, i, c_ref: (q, c_ref[0], i, 0)),
                      pl.BlockSpec((None, tr, cols), lambda q, i, c_ref: (q, i, 0))],
            out_specs=pl.BlockSpec((None, tr, cols), lambda q, i, c_ref: (q, i, 0))),
        out_shape=jax.ShapeDtypeStruct((N_CHIP, rh, cols), BF16),
        compiler_params=_params(("parallel", "parallel"), 10 * _nbytes((tr, cols), F32)),
    )(c_idx, g, a)


def _scatter_and_gather(parts, small, name):
    n = len(parts)

    def body(*refs):
        scatter = _Scatter(refs[:n], refs[n + 1:2 * n + 1], *refs[2 * n + 2:2 * n + 4])
        gather = _SmallGather(refs[n], refs[2 * n + 1], *refs[2 * n + 4:])
        scatter.start()
        gather.start()
        gather.finish()
        scatter.finish()

    return pl.pallas_call(
        body, name=name,
        out_shape=_Scatter.out_shapes(parts) + [jax.ShapeDtypeStruct((N_DEV,) + small.shape, F32)],
        in_specs=[ANY] * n + [IN_VMEM], out_specs=[ANY] * n + [IN_VMEM],
        scratch_shapes=_Scatter.semaphores(n) + _SmallGather.semaphores(),
        compiler_params=_params(None, 10 * _nbytes(small.shape, F32)),
    )(*parts, small)


class _Scatter:
    def __init__(self, p_refs, out_refs, send_sems, recv_sems):
        x, y, c = _position()
        self.copies = []
        for k, (p_ref, out_ref) in enumerate(zip(p_refs, out_refs)):
            for j, (cx, cy, qj) in enumerate(_other_chips(x, y)):
                self.copies.append(pltpu.make_async_remote_copy(
                    src_ref=p_ref.at[qj], dst_ref=out_ref.at[j], send_sem=send_sems.at[3 * k + j],
                    recv_sem=recv_sems.at[3 * k + j], device_id=(cx, cy, c), device_id_type=MESH))

    @staticmethod
    def out_shapes(parts):
        return [jax.ShapeDtypeStruct((3,) + p.shape[1:], p.dtype) for p in parts]

    @staticmethod
    def semaphores(n):
        return [pltpu.SemaphoreType.DMA((3 * n,)), pltpu.SemaphoreType.DMA((3 * n,))]

    def start(self):
        for cp in self.copies:
            cp.start()

    def finish(self):
        for cp in self.copies:
            cp.wait_recv()
        for cp in self.copies:
            cp.wait_send()


def _shard_sum(p, b, qc_idx, name):
    _, rh, cols = p.shape
    tr = rh
    for cand in (256, 128, 64, 32, 16):
        if rh % cand == 0 and rh > cand:
            tr = cand
            break

    def body(qc_ref, p_ref, b_ref, o_ref):
        acc = p_ref[...].astype(F32)
        for j in range(3):
            acc = acc + b_ref[j].astype(F32)
        o_ref[...] = acc

    return pl.pallas_call(
        body, name=name,
        grid_spec=pltpu.PrefetchScalarGridSpec(
            num_scalar_prefetch=1, grid=(rh // tr,),
            in_specs=[pl.BlockSpec((None, tr, cols), lambda i, qc_ref: (qc_ref[0], i, 0)),
                      pl.BlockSpec((3, tr, cols), lambda i, qc_ref: (0, i, 0))],
            out_specs=pl.BlockSpec((None, tr, cols), lambda i, qc_ref: (qc_ref[1], i, 0))),
        out_shape=jax.ShapeDtypeStruct((2, rh, cols), F32),
        compiler_params=_params(("parallel",), 8 * _nbytes((tr, cols), F32)),
    )(qc_idx, p, b)


def _join_halves(shards):
    n = len(shards)

    def body(*refs):
        out_refs = refs[n:2 * n]
        send_sems, recv_sems = refs[2 * n:]
        x, y, c = _position()
        cps = [pltpu.make_async_remote_copy(src_ref=out_refs[k].at[c], dst_ref=out_refs[k].at[c], send_sem=send_sems.at[k],
                                            recv_sem=recv_sems.at[k], device_id=(x, y, 1 - c), device_id_type=MESH)
               for k in range(n)]
        for cp in cps:
            cp.start()
        for k in range(n):
            arriving = out_refs[k].at[1 - c]
            pltpu.make_async_remote_copy(src_ref=arriving, dst_ref=arriving, send_sem=send_sems.at[k], recv_sem=recv_sems.at[k],
                                         device_id=(x, y, 1 - c), device_id_type=MESH).wait_recv()
        for cp in cps:
            cp.wait_send()

    return pl.pallas_call(
        body, name="rs_join",
        out_shape=[jax.ShapeDtypeStruct(a.shape, a.dtype) for a in shards],
        in_specs=[ANY] * n, out_specs=[ANY] * n, input_output_aliases={k: k for k in range(n)},
        scratch_shapes=[pltpu.SemaphoreType.DMA((n,)), pltpu.SemaphoreType.DMA((n,))],
    )(*shards)


def _cols_from_shards(g):
    q, r, cs = g.shape
    return jnp.transpose(g, (1, 0, 2)).reshape(r, q * cs)


def _cols_to_shards(w):
    r, cfull = w.shape
    return jnp.transpose(w.reshape(r, N_CHIP, cfull // N_CHIP), (1, 0, 2))


def _pad_w_in(w):
    z = lambda n: jnp.zeros((w.shape[0], n), w.dtype)
    q_lat, kv_lat, kpe = w[:, 0:512], w[:, 512:768], w[:, 768:800]
    qd, kd, vd = w[:, 800:1312], w[:, 1312:1824], w[:, 1824:2336]
    return jnp.concatenate([q_lat, qd, kd, vd, kv_lat, z(KPE_OFF), kpe, z(LANE - KPE_OFF - ROPE)], axis=1)


def _pad_w_qb(w):
    w3 = w.reshape(Q_LORA, HEADS, NOPE + ROPE)
    return jnp.pad(w3, ((0, 0), (0, 0), (0, LANE - NOPE - ROPE))).reshape(Q_LORA, HEADS * LANE)


def _unpad_w_qb(g):
    return g.reshape(Q_LORA, HEADS, LANE)[:, :, :NOPE + ROPE].reshape(Q_LORA, HEADS * (NOPE + ROPE))


def _pad_w_kvb(w):
    w3 = w.reshape(KV_LORA, HEADS, 2 * NOPE)
    kp = jnp.pad(w3[:, :, :NOPE], ((0, 0), (0, 0), (0, LANE - NOPE))).reshape(KV_LORA, HEADS * LANE)
    return jnp.concatenate([kp, w3[:, :, NOPE:].reshape(KV_LORA, DIL_W)], axis=1)


def _unpad_w_kvb(g):
    gk = g[:, :HEADS * LANE].reshape(KV_LORA, HEADS, LANE)[:, :, :NOPE]
    gv = g[:, HEADS * LANE:].reshape(KV_LORA, HEADS, NOPE)
    return jnp.concatenate([gk, gv], axis=2).reshape(KV_LORA, HEADS * 2 * NOPE)


def _head_gains(g_q_nope, g_q_pe, g_k_nope, g_k_pe, g_dq, g_dk):
    z = lambda n: jnp.zeros((1, n), F32)
    q1 = jnp.concatenate([g_q_nope, g_q_pe, z(LANE - NOPE - ROPE)], axis=1)
    k1 = jnp.concatenate([g_k_nope, z(LANE - NOPE)], axis=1)
    kpe = jnp.concatenate([z(KPE_OFF), g_k_pe, z(LANE - KPE_OFF - ROPE)], axis=1)
    return dict(q=jnp.tile(q1, (1, HEADS)), k=jnp.tile(k1, (1, HEADS)), kpe=kpe,
                dq=jnp.tile(g_dq, (1, HEADS)), dk=jnp.tile(g_dk, (1, HEADS)))


def kernel(x, c, positions, w_ada, b_ada, g_mix_norm, w_in, g_q_lat, w_q_b, g_kv_lat, w_kv_b, g_mla_q_nope, g_mla_q_pe, g_mla_k_nope, g_mla_k_pe, g_dil_q, g_dil_k, w_o, g_ffn_norm, w_up, w_conv, b_conv, w_down, loss_target, m_w_ada, m_b_ada, m_g_mix_norm, m_w_in, m_g_q_lat, m_w_q_b, m_g_kv_lat, m_w_kv_b, m_g_mla_q_nope, m_g_mla_q_pe, m_g_mla_k_nope, m_g_mla_k_pe, m_g_dil_q, m_g_dil_k, m_w_o, m_g_ffn_norm, m_w_up, m_w_conv, m_b_conv, m_w_down, v_w_ada, v_b_ada, v_g_mix_norm, v_w_in, v_g_q_lat, v_w_q_b, v_g_kv_lat, v_w_kv_b, v_g_mla_q_nope, v_g_mla_q_pe, v_g_mla_k_nope, v_g_mla_k_pe, v_g_dil_q, v_g_dil_k, v_w_o, v_g_ffn_norm, v_w_up, v_w_conv, v_b_conv, v_w_down):
    args = dict(locals())
    weights = {n: args[n][0] for n in ("w_ada", "w_in", "w_q_b", "w_kv_b", "w_o", "w_up", "w_conv", "w_down")}
    small_w = {n: args[n] for n in ("b_ada",) + tuple(n for n, _ in SMALL_WIDTHS)}
    mom_m = {n[2:]: (args[n][0] if args[n].ndim == 3 else args[n]) for n in args if n.startswith("m_")}
    mom_v = {n[2:]: (args[n][0] if args[n].ndim == 3 else args[n]) for n in args if n.startswith("v_")}

    xi, yi, ci = _position()
    q0 = 2 * xi + yi
    me = 4 * xi + 2 * yi + ci
    xs, tgt = x[0], loss_target[0]
    s = xs.shape[0]
    consts = _seg_consts()
    c_idx, qc_idx = jnp.reshape(ci, (1,)).astype(I32), jnp.stack([q0, ci]).astype(I32)

    def halves(g4):
        q, r, cc = g4.shape
        return g4.reshape(q, 2, r // 2, cc)

    own_first = [weights[n].astype(BF16) for n in ("w_in", "w_q_b", "w_kv_b")]
    own_later = [weights[n].astype(BF16) for n in ("w_o", "w_up", "w_down")]
    conv_cols = UP_W // N_CHIP
    ada_cols = w_ada.shape[2]
    b_shard = lax.dynamic_slice_in_dim(b_ada, q0 * ada_cols, ada_cols, axis=1)
    c_taps = jnp.concatenate([c, weights["w_conv"].reshape(1, 3 * conv_cols)], axis=1)
    c_taps_all, mod_all, tab, *gathered = _prologue(c_taps, weights["w_ada"], b_shard, positions.reshape(s, 1),
                                                    _rope_consts(), own_first)
    c_all = c_taps_all[:, 0, :D_MODEL]
    w_conv_f = c_taps_all[:, 0, D_MODEL:].reshape(N_CHIP, 2, 3, conv_cols)[:, 0]
    w_conv_f = jnp.transpose(w_conv_f, (1, 0, 2)).reshape(3, UP_W)
    mod_all = mod_all.reshape(N_CHIP, 2, N_DEV, ada_cols)
    mod = lax.dynamic_index_in_dim(lax.dynamic_index_in_dim(mod_all, ci, 1, False), me, 1, False)
    mod = mod.reshape(1, N_CHIP * ada_cols)
    sh1, sc1, g1, sh2, sc2, g2 = [mod[:, k * D_MODEL:(k + 1) * D_MODEL] for k in range(6)]
    w_in_f = _cols_from_shards(gathered[0])
    w_in_p = _pad_w_in(w_in_f)
    w_qb_p = _pad_w_qb(_cols_from_shards(gathered[1]))
    w_kvb_p = _pad_w_kvb(_cols_from_shards(gathered[2]))
    gains = _head_gains(g_mla_q_nope, g_mla_q_pe, g_mla_k_nope, g_mla_k_pe, g_dil_q, g_dil_k)

    h = _prenorm(xs, g_mix_norm, sc1, sh1, "prenorm")
    proj = _mm(h, w_in_p, "nn", F32, 512, P_COLS, "mm_in")
    ql, kvl = _latnorm(proj, g_q_lat, g_kv_lat)
    q_raw = _mm(ql, w_qb_p, "nn", F32, 512, HEADS * LANE, "mm_qb")
    kv_raw = _mm(kvl, w_kvb_p, "nn", F32, 512, HEADS * LANE + DIL_W, "mm_kvb")
    qm, km, vm, qd, kd, vd = _attn_prep(q_raw, kv_raw, proj, tab, gains, consts)
    scale_m, scale_d = (NOPE + ROPE) ** -0.5, DIL_DIM ** -0.5
    o_m, lse_m, got_up = _attn_fwd(qm, km, vm, True, scale_m, "attn_mla", gather=own_later[1:2])
    o_d, lse_d, got_o, got_down = _attn_fwd(qd, kd, vd, False, scale_d, "attn_dil", gather=[own_later[0], own_later[2]])
    gathered = [got_o, got_up, got_down]
    w_o_f = gathered[0].reshape(D_MODEL, D_MODEL)
    w_up_f = _cols_from_shards(gathered[1])
    w_down_f = gathered[2].reshape(D_FF, D_MODEL)
    mix_in = jnp.concatenate([o_m, o_d], axis=1)
    mix = _mm(mix_in, w_o_f, "nn", F32, 512, D_MODEL, "mm_o")
    x1, h2 = _resid_prenorm(xs, mix, g1, g_ffn_norm, sc2, sh2)
    up = _mm(h2, w_up_f, "nn", F32, 512, CONV_TILE, "mm_up")
    act = _conv_gate(up, w_conv_f, b_conv)
    ffn = _mm(act, w_down_f, "nn", F32, 256, D_MODEL, "mm_down")
    dy, dffn, dg2, loss_part = _final(x1, ffn, tgt, g2)

    da = _mm(dffn, w_down_f, "nt", F32, 512, CONV_TILE, "mm_down_dx")
    gw_down = _mm(act, dffn, "tn", F32, 256, D_MODEL, "mm_down_dw")
    dup_g, dup_v, dbg, dbv, dwg, dwv = _gate_bwd(up, da, w_conv_f, b_conv)
    dup = jnp.concatenate([dup_g, dup_v], axis=1)
    early_names = ("w_up", "w_down", "w_o")
    gw_up = _mm(h2, dup, "tn", F32, 512, CONV_TILE, "mm_up_dw", col_shards=True)
    early = [halves(gw_up), halves(gw_down.reshape(N_CHIP, D_FF // N_CHIP, D_MODEL))]
    dh2, *early_sib = _mm(dup, w_up_f, "nt", F32, 256, 512, "mm_up_dx", swap=early, b_outer=True)
    dx1, dmix, acc2 = _ffnnorm_bwd(dh2, x1, dy, mix, g_ffn_norm, sc2, g1)
    gw_o = _mm(mix_in, dmix, "tn", F32, 512, D_MODEL, "mm_o_dw")
    early.append(halves(gw_o.reshape(N_CHIP, D_MODEL // N_CHIP, D_MODEL)))
    dmix_in, sib_o = _mm(dmix, w_o_f, "nt", F32, 512, D_MODEL, "mm_o_dx", swap=early[2:])
    early_sib.append(sib_o)
    early_sums = [_pair_sum(g, a, c_idx, "pair_sum_" + n) for g, a, n in zip(early, early_sib, early_names)]
    dqm, dkm, dvm, *early_recv = _attn_bwd(qm, km, vm, o_m, dmix_in, 0, lse_m, True, scale_m, "attn_mla_bwd",
                                           scatter=early_sums[:1])
    dqd, dkd, dvd, *early_recv_d = _attn_bwd(qd, kd, vd, o_d, dmix_in, DIL_W // LANE, lse_d, False, scale_d,
                                             "attn_dil_bwd", scatter=early_sums[1:])
    early_recv = early_recv + early_recv_d
    dq_raw, dkv_raw, dkpe_b, dqd_b, dkd_b, dvd_b, dgains = _attn_prep_bwd(
        dqm, dkm, dvm, dqd, dkd, dvd, q_raw, kv_raw, proj, tab, gains, consts)
    dql = _mm(dq_raw, w_qb_p, "nt", F32, 512, Q_LORA, "mm_qb_dx")
    gw_qb = _unpad_w_qb(_mm(ql, dq_raw, "tn", F32, Q_LORA, HEADS * LANE, "mm_qb_dw"))
    dkvl = _mm(dkv_raw, w_kvb_p, "nt", F32, 512, KV_LORA, "mm_kvb_dx")
    gw_kvb = _unpad_w_kvb(_mm(kvl, dkv_raw, "tn", F32, KV_LORA, HEADS * LANE + DIL_W, "mm_kvb_dw"))
    dqlat_b, dkvlat_b, dglat = _latnorm_bwd(dql, dkvl, proj, g_q_lat, g_kv_lat)
    dproj = jnp.concatenate([dqlat_b, dkvlat_b, dkpe_b[:, KPE_OFF:KPE_OFF + ROPE], dqd_b, dkd_b, dvd_b], axis=1)
    dh = _mm(dproj, w_in_f, "nt", F32, 512, D_MODEL, "mm_in_dx")
    gw_in = _mm(h, dproj, "tn", F32, 512, IN_COLS, "mm_in_dw")
    grad_x, acc1 = _mixnorm_bwd(dh, xs, dx1, g_mix_norm, sc1)

    packed = _pack_small(acc1, acc2, dg2, dglat, dgains, dbg, dbv, dwg, dwv, loss_part)
    late_names = ("w_in", "w_q_b", "w_kv_b")
    late = [halves(_cols_to_shards(gw_in)), halves(_cols_to_shards(gw_qb)), halves(_cols_to_shards(gw_kvb))]
    late_sib = _swap_halves_d2d(late, "rs_pair_swap_late")
    late_sums = [_pair_sum(g, a, c_idx, "pair_sum_" + n) for g, a, n in zip(late, late_sib, late_names)]
    *late_recv, gathered_small = _scatter_and_gather(late_sums, packed, "rs_scatter_late")

    grad_b_ada, *small_grads, gconv_full, loss_sum = _sum_unpack(gathered_small)
    grads = {"b_ada": grad_b_ada}
    grads.update({n: g for (n, _), g in zip(SMALL_WIDTHS, small_grads)})
    shard_cols = UP_W // N_CHIP
    grads["w_conv"] = lax.dynamic_slice_in_dim(gconv_full, q0 * shard_cols, shard_cols, axis=1)
    dmod_all = gathered_small[:, 0, :6 * D_MODEL]
    grads["w_ada"] = _ada_bwd(c_all, lax.dynamic_slice_in_dim(dmod_all, q0 * ada_cols, ada_cols, axis=1))

    big_names = late_names + early_names
    half_sums = [_shard_sum(p, b, qc_idx, "shard_sum_" + n)
                 for p, b, n in zip(late_sums + early_sums, list(late_recv) + list(early_recv), big_names)]
    for n, full in zip(big_names, _join_halves(half_sums)):
        grads[n] = full.reshape(2 * full.shape[1], full.shape[2])

    delta, new_m, new_v = {}, {}, {}
    for n in ("w_ada", "w_in", "w_q_b", "w_kv_b", "w_o", "w_up", "w_conv", "w_down"):
        operands = (weights[n], grads[n], mom_m[n], mom_v[n])
        flipped = n in ("w_in", "w_q_b")
        if flipped:
            operands = [jnp.swapaxes(a, 0, 1) for a in operands]
            grads[n] = jnp.swapaxes(operands[1], 0, 1)
        if n == "w_ada":
            operands = _in_hbm(*operands)
        delta[n], new_m[n], new_v[n] = _adamw(*operands, "adamw_" + n)
        if flipped:
            delta[n], new_m[n], new_v[n] = (jnp.swapaxes(a, 0, 1) for a in (delta[n], new_m[n], new_v[n]))
    vec_names = ("b_ada",) + tuple(n for n, _ in SMALL_WIDTHS)
    sd, sm, sv = _adamw_vectors(*[[d_[n] for n in vec_names] for d_ in (small_w, grads, mom_m, mom_v)])
    for k, n in enumerate(vec_names):
        delta[n], new_m[n], new_v[n] = sd[k], sm[k], sv[k]

    loss = loss_sum[0, 0]
    order = ("w_ada", "b_ada", "g_mix_norm", "w_in", "g_q_lat", "w_q_b", "g_kv_lat", "w_kv_b", "g_mla_q_nope", "g_mla_q_pe",
             "g_mla_k_nope", "g_mla_k_pe", "g_dil_q", "g_dil_k", "w_o", "g_ffn_norm", "w_up", "w_conv", "b_conv", "w_down")
    lead = lambda n, z: z[None] if n.startswith("w_") else z
    outs = [loss, grad_x[None]]
    for d_ in (grads, delta, new_m, new_v):
        outs += [lead(n, d_[n]) for n in order]
    return tuple(outs)
```

```python
import functools

import numpy as np
import jax
import jax.numpy as jnp
from jax import lax
from jax.experimental import pallas as pl
from jax.experimental.pallas import tpu as pltpu

F32 = jnp.float32
BF16 = jnp.bfloat16
I32 = jnp.int32

D_MODEL = 1024
HEADS = 8
NOPE = 64
ROPE = 32
Q_LORA = 512
KV_LORA = 256
DIL_DIM = 64
DIL_W = HEADS * DIL_DIM
D_FF = 2816
UP_W = 2 * D_FF
IN_COLS = Q_LORA + KV_LORA + ROPE + 3 * DIL_W
ROPE_THETA = 10000.0
EPS = 1e-6
NEG_INF = -1e30
N_DEV = 8
N_CHIP = 4

ADAM_LR = 0.001
ADAM_B1 = 0.9
ADAM_B2 = 0.999
ADAM_EPS = 1e-08
ADAM_WD = 0.01
ADAM_STEP = 10

LANE = 128
ROW_TILE = 256
ATT_TQ = 512
ATT_TK = 256
LOG2E = 1.4426950408889634
LN2 = 0.6931471805599453
VMEM_CAP = 56 * 1024 * 1024
VMEM_FLOOR = 32 * 1024 * 1024

P_QLAT, P_QD, P_KD, P_VD, P_KVLAT, P_KPE = 0, 512, 1024, 1536, 2048, 2304
P_COLS = 2432
KPE_OFF = 64

NN = (((1,), (0,)), ((), ()))
NT = (((1,), (1,)), ((), ()))
TN = (((0,), (0,)), ((), ()))
HIGHEST = lax.Precision.HIGHEST
MESH = pl.DeviceIdType.MESH


def _params(sem=None, est_bytes=0):
    limit = int(min(max(2 * est_bytes + (4 << 20), VMEM_FLOOR), VMEM_CAP))
    if sem is None:
        return pltpu.CompilerParams(vmem_limit_bytes=limit)
    return pltpu.CompilerParams(dimension_semantics=sem, vmem_limit_bytes=limit)


def _nbytes(shape, dtype):
    return int(np.prod(shape)) * jnp.dtype(dtype).itemsize


def _in_hbm(*xs):
    return [pltpu.with_memory_space_constraint(x, pltpu.HBM) for x in xs]


def _mm(a, b, dims, out_dtype, tm, tn, name, col_shards=False, swap=(), b_outer=False):
    def spec(block, index):
        if b_outer:
            return pl.BlockSpec(block, lambda g0, g1: index(g1, g0))
        return pl.BlockSpec(block, index)

    if dims == "nn":
        (m, k), (k2, n) = a.shape, b.shape
        a_spec = spec((tm, k), lambda i, j: (i, 0))
        b_spec = spec((k, tn), lambda i, j: (0, j))
        dn = NN
    elif dims == "nt":
        (m, k), (n, k2) = a.shape, b.shape
        a_spec = spec((tm, k), lambda i, j: (i, 0))
        b_spec = spec((tn, k), lambda i, j: (j, 0))
        dn = NT
    else:
        (k, m), (k2, n) = a.shape, b.shape
        a_spec = spec((k, tm), lambda i, j: (0, i))
        b_spec = spec((k, tn), lambda i, j: (0, j))
        dn = TN
    assert k == k2 and m % tm == 0 and n % tn == 0, (name, a.shape, b.shape, tm, tn)

    nw = len(swap)
    grid = (n // tn, m // tm) if b_outer else (m // tm, n // tn)

    def body(*refs):
        a_ref, b_ref, o_ref = refs[0], refs[1], refs[2 + nw]
        comm = (refs[2:2 + nw], refs[3 + nw:3 + 2 * nw]) + tuple(refs[3 + 2 * nw:])
        if nw:
            @pl.when((pl.program_id(0) == 0) & (pl.program_id(1) == 0))
            def _():
                _PairSwap(*comm).start()

        o_ref[...] = lax.dot_general(a_ref[...], b_ref[...], dn, preferred_element_type=F32).astype(o_ref.dtype)

        if nw:
            @pl.when((pl.program_id(0) == grid[0] - 1) & (pl.program_id(1) == grid[1] - 1))
            def _():
                _PairSwap(*comm).finish()

    est = _nbytes((tm, k), a.dtype) + _nbytes((tn, k), b.dtype) + _nbytes((tm, tn), F32) + _nbytes((tm, tn), out_dtype)
    if col_shards:
        out_spec = spec((None, tm, tn), lambda i, j: (j, i, 0))
        out_shape = jax.ShapeDtypeStruct((n // tn, m, tn), out_dtype)
    else:
        out_spec = spec((tm, tn), lambda i, j: (i, j))
        out_shape = jax.ShapeDtypeStruct((m, n), out_dtype)
    out = pl.pallas_call(
        body, name=name, grid=grid,
        in_specs=[a_spec, b_spec] + [ANY] * nw,
        out_specs=[out_spec] + [ANY] * nw,
        out_shape=[out_shape] + _PairSwap.out_shapes(swap),
        scratch_shapes=_PairSwap.semaphores(nw) if nw else [],
        compiler_params=_params(("arbitrary", "arbitrary") if nw else ("parallel", "parallel"), est),
    )(a, b, *swap)
    return out if nw else out[0]


def _seg_consts():
    seg_q = np.zeros((HEADS * LANE, LANE), np.float32)
    inv_q = np.zeros((1, LANE), np.float32)
    seg_k = np.zeros((HEADS * LANE, LANE), np.float32)
    inv_k = np.zeros((1, LANE), np.float32)
    seg_d = np.zeros((DIL_W, LANE), np.float32)
    inv_d = np.zeros((1, LANE), np.float32)
    for h in range(HEADS):
        seg_q[h * LANE:h * LANE + NOPE, 2 * h] = 1.0
        seg_q[h * LANE + NOPE:h * LANE + NOPE + ROPE, 2 * h + 1] = 1.0
        inv_q[0, 2 * h], inv_q[0, 2 * h + 1] = 1.0 / NOPE, 1.0 / ROPE
        seg_k[h * LANE:h * LANE + NOPE, h] = 1.0
        inv_k[0, h] = 1.0 / NOPE
        seg_d[h * DIL_DIM:(h + 1) * DIL_DIM, h] = 1.0
        inv_d[0, h] = 1.0 / DIL_DIM
    fold_q = np.tile(np.eye(LANE, dtype=np.float32), (HEADS, 1))
    fold_d = np.zeros((DIL_W, LANE), np.float32)
    fold_d[np.arange(DIL_W), np.arange(DIL_W) % DIL_DIM] = 1.0
    j = lambda v: jnp.asarray(v)
    b = lambda v: jnp.asarray(v, dtype=BF16)
    return dict(seg_q=b(seg_q), exp_q=b(seg_q.T.copy()), inv_q=j(inv_q), seg_k=b(seg_k), exp_k=b(seg_k.T.copy()),
                inv_k=j(inv_k), seg_d=b(seg_d), exp_d=b(seg_d.T.copy()), inv_d=j(inv_d), fold_q=j(fold_q), fold_d=j(fold_d))


def _rope_consts():
    inv_d = jnp.power(ROPE_THETA, -2.0 * jnp.arange(DIL_DIM // 2, dtype=F32) / DIL_DIM)
    inv_q = jnp.power(ROPE_THETA, -2.0 * jnp.arange(ROPE // 2, dtype=F32) / ROPE)
    lanes = np.arange(LANE)
    freq_d = inv_d[lanes % (DIL_DIM // 2)]
    in_pe = (lanes >= KPE_OFF) & (lanes < KPE_OFF + ROPE)
    freq_q = jnp.where(jnp.asarray(in_pe), inv_q[(lanes - KPE_OFF) % (ROPE // 2)], 0.0)
    sign_d = np.where(lanes % DIL_DIM < DIL_DIM // 2, -1.0, 1.0).astype(np.float32)
    sign_q = np.where(in_pe, np.where((lanes - KPE_OFF) < ROPE // 2, -1.0, 1.0), 0.0).astype(np.float32)
    zeros, ones = np.zeros(LANE, np.float32), np.ones(LANE, np.float32)
    freq = jnp.concatenate([freq_d, freq_d, freq_q, freq_q])[None, :]
    csel = jnp.asarray(np.concatenate([ones, zeros, ones, zeros]))[None, :]
    ssel = jnp.asarray(np.concatenate([zeros, sign_d, zeros, sign_q]))[None, :]
    return freq, csel, ssel


def _full(shape):
    return pl.BlockSpec(shape, lambda *_: (0,) * len(shape))


def _tile_lanes(x, n):
    return jnp.concatenate([x] * n, axis=1)


def _rms(x):
    return lax.rsqrt(jnp.mean(x * x, axis=-1, keepdims=True) + EPS)


def _prenorm(x, gain, scale, shift, name):
    s, d = x.shape

    def body(x_ref, g_ref, sc_ref, sh_ref, h_ref):
        xv = x_ref[...]
        h = (xv * _rms(xv)) * g_ref[...] * (1.0 + sc_ref[...]) + sh_ref[...]
        h_ref[...] = h.astype(BF16)

    row = pl.BlockSpec((ROW_TILE, d), lambda i: (i, 0))
    return pl.pallas_call(
        body, name=name, grid=(s // ROW_TILE,),
        in_specs=[row, _full((1, d)), _full((1, d)), _full((1, d))],
        out_specs=row, out_shape=jax.ShapeDtypeStruct((s, d), BF16),
        compiler_params=_params(("parallel",)),
    )(x, gain, scale, shift)


def _latnorm(proj, g_q, g_kv):
    s = proj.shape[0]

    def body(q_ref, kv_ref, gq_ref, gkv_ref, ql_ref, kvl_ref):
        q, kv = q_ref[...], kv_ref[...]
        ql_ref[...] = ((q * _rms(q)) * gq_ref[...]).astype(BF16)
        kvl_ref[...] = ((kv * _rms(kv)) * gkv_ref[...]).astype(BF16)

    return pl.pallas_call(
        body, name="latnorm", grid=(s // ROW_TILE,),
        in_specs=[pl.BlockSpec((ROW_TILE, Q_LORA), lambda i: (i, P_QLAT // Q_LORA)),
                  pl.BlockSpec((ROW_TILE, KV_LORA), lambda i: (i, P_KVLAT // KV_LORA)),
                  _full((1, Q_LORA)), _full((1, KV_LORA))],
        out_specs=[pl.BlockSpec((ROW_TILE, Q_LORA), lambda i: (i, 0)), pl.BlockSpec((ROW_TILE, KV_LORA), lambda i: (i, 0))],
        out_shape=[jax.ShapeDtypeStruct((s, Q_LORA), BF16), jax.ShapeDtypeStruct((s, KV_LORA), BF16)],
        compiler_params=_params(("parallel",)),
    )(proj, proj, g_q, g_kv)


def _dot01(v, mat01):
    hi = v.astype(BF16)
    lo = (v - hi.astype(F32)).astype(BF16)
    return jnp.dot(hi, mat01, preferred_element_type=F32) + jnp.dot(lo, mat01, preferred_element_type=F32)


def _seg_rinv(x, seg, exp, inv):
    r = lax.rsqrt(_dot01(x * x, seg) * inv + EPS)
    return _dot01(r, exp)


def _seg_mean(v, seg, exp, inv):
    return _dot01(_dot01(v, seg) * inv, exp)


def _swap_halves(x, half):
    n = x.shape[1]
    lane = lax.broadcasted_iota(I32, (1, n), 1)
    first = (lane & (2 * half - 1)) < half
    return jnp.where(first, pltpu.roll(x, n - half, 1), pltpu.roll(x, half, 1))


def _rope(x, cos, sin_signed, half):
    return x * cos + _swap_halves(x, half) * sin_signed


def _rope_bwd(dy, cos, sin_signed, half):
    return dy * cos + _swap_halves(dy * sin_signed, half)


def _pe_lane_mask(n):
    lane = lax.broadcasted_iota(I32, (1, n), 1) & (LANE - 1)
    return (lane >= KPE_OFF) & (lane < KPE_OFF + ROPE)


def _attn_prep(q_raw, kv_raw, proj, tab, gains, consts):
    s = q_raw.shape[0]
    hw = HEADS * LANE

    def body(q_ref, kv_ref, kpe_ref, qd_ref, kd_ref, vd_ref, tab_ref,
             gq_ref, gk_ref, gkpe_ref, gdq_ref, gdk_ref,
             segq_ref, expq_ref, invq_ref, segk_ref, expk_ref, invk_ref, segd_ref, expd_ref, invd_ref,
             qm_ref, km_ref, vm_ref, qdo_ref, kdo_ref, vdo_ref):
        tab_v = tab_ref[...]
        cos_d, sin_d = _tile_lanes(tab_v[:, 0:LANE], DIL_W // LANE), _tile_lanes(tab_v[:, LANE:2 * LANE], DIL_W // LANE)
        cos_q1, sin_q1 = tab_v[:, 2 * LANE:3 * LANE], tab_v[:, 3 * LANE:4 * LANE]
        cos_q, sin_q = _tile_lanes(cos_q1, HEADS), _tile_lanes(sin_q1, HEADS)

        q = q_ref[...]
        qn = q * _seg_rinv(q, segq_ref[...], expq_ref[...], invq_ref[...]) * gq_ref[...]
        qm_ref[...] = _rope(qn, cos_q, sin_q, ROPE // 2).astype(BF16)

        kv = kv_ref[...]
        kp = kv[:, :hw]
        kn = kp * _seg_rinv(kp, segk_ref[...], expk_ref[...], invk_ref[...]) * gk_ref[...]
        kpe = kpe_ref[...]
        r_pe = lax.rsqrt(jnp.sum(kpe * kpe, axis=-1, keepdims=True) * (1.0 / ROPE) + EPS)
        kpe_r = _rope(kpe * r_pe * gkpe_ref[...], cos_q1, sin_q1, ROPE // 2)
        km_ref[...] = (kn + _tile_lanes(kpe_r, HEADS)).astype(BF16)
        vm_ref[...] = kv[:, hw:].astype(BF16)

        qd = qd_ref[...]
        qdn = qd * _seg_rinv(qd, segd_ref[...], expd_ref[...], invd_ref[...]) * gdq_ref[...]
        qdo_ref[...] = _rope(qdn, cos_d, sin_d, DIL_DIM // 2).astype(BF16)
        kd = kd_ref[...]
        kdn = kd * _seg_rinv(kd, segd_ref[...], expd_ref[...], invd_ref[...]) * gdk_ref[...]
        kdo_ref[...] = _rope(kdn, cos_d, sin_d, DIL_DIM // 2).astype(BF16)
        vdo_ref[...] = vd_ref[...].astype(BF16)

    t = ROW_TILE
    row = lambda w, cb=0: pl.BlockSpec((t, w), lambda i: (i, cb))
    c = consts
    return pl.pallas_call(
        body, name="attn_prep", grid=(s // t,),
        in_specs=[row(hw), row(hw + DIL_W), row(LANE, P_KPE // LANE), row(DIL_W, P_QD // DIL_W), row(DIL_W, P_KD // DIL_W),
                  row(DIL_W, P_VD // DIL_W), row(4 * LANE),
                  _full((1, hw)), _full((1, hw)), _full((1, LANE)), _full((1, DIL_W)), _full((1, DIL_W)),
                  _full((hw, LANE)), _full((LANE, hw)), _full((1, LANE)), _full((hw, LANE)), _full((LANE, hw)), _full((1, LANE)),
                  _full((DIL_W, LANE)), _full((LANE, DIL_W)), _full((1, LANE))],
        out_specs=[row(hw), row(hw), row(DIL_W), row(DIL_W), row(DIL_W), row(DIL_W)],
        out_shape=[jax.ShapeDtypeStruct((s, hw), BF16), jax.ShapeDtypeStruct((s, hw), BF16)]
        + [jax.ShapeDtypeStruct((s, DIL_W), BF16)] * 4,
        compiler_params=_params(("parallel",), 24 << 20),
    )(*_in_hbm(q_raw, kv_raw, proj, proj, proj, proj), tab, gains["q"], gains["k"], gains["kpe"], gains["dq"], gains["dk"],
      c["seg_q"], c["exp_q"], c["inv_q"], c["seg_k"], c["exp_k"], c["inv_k"], c["seg_d"], c["exp_d"], c["inv_d"])


def _attn_prep_bwd(dqm, dkm, dvm, dqd, dkd, dvd, q_raw, kv_raw, proj, tab, gains, consts):
    s = q_raw.shape[0]
    hw = HEADS * LANE
    n_steps = s // ROW_TILE

    def body(dqm_ref, dkm_ref, dvm_ref, dqd_ref, dkd_ref, dvd_ref, q_ref, kv_ref, kpe_ref, qd_ref, kd_ref, tab_ref,
             gq_ref, gk_ref, gkpe_ref, gdq_ref, gdk_ref,
             segq_ref, expq_ref, invq_ref, segk_ref, expk_ref, invk_ref, segd_ref, expd_ref, invd_ref, foldq_ref, foldd_ref,
             dq_ref, dkv_ref, dkpe_ref, dqdo_ref, dkdo_ref, dvdo_ref, dg_ref, acc_ref):
        i = pl.program_id(0)

        @pl.when(i == 0)
        def _():
            acc_ref[...] = jnp.zeros_like(acc_ref)

        tab_v = tab_ref[...]
        cos_d, sin_d = _tile_lanes(tab_v[:, 0:LANE], DIL_W // LANE), _tile_lanes(tab_v[:, LANE:2 * LANE], DIL_W // LANE)
        cos_q1, sin_q1 = tab_v[:, 2 * LANE:3 * LANE], tab_v[:, 3 * LANE:4 * LANE]
        cos_q, sin_q = _tile_lanes(cos_q1, HEADS), _tile_lanes(sin_q1, HEADS)

        def norm_bwd(x, dyg, gain, seg, exp, inv):
            rinv = _seg_rinv(x, seg, exp, inv)
            xn = x * rinv
            dxn = dyg * gain
            dx = rinv * (dxn - xn * _seg_mean(dxn * xn, seg, exp, inv))
            return dx, jnp.sum(dyg * xn, axis=0, keepdims=True)

        dq, gq_l = norm_bwd(q_ref[...], _rope_bwd(dqm_ref[...], cos_q, sin_q, ROPE // 2), gq_ref[...],
                            segq_ref[...], expq_ref[...], invq_ref[...])
        dq_ref[...] = dq.astype(BF16)

        dkm = dkm_ref[...]
        kv = kv_ref[...]
        dkp, gk_l = norm_bwd(kv[:, :hw], dkm, gk_ref[...], segk_ref[...], expk_ref[...], invk_ref[...])
        dkv_ref[:, :hw] = dkp.astype(BF16)
        dkv_ref[:, hw:] = dvm_ref[...].astype(BF16)

        dkpe_r = dkm[:, 0:LANE]
        for h in range(1, HEADS):
            dkpe_r = dkpe_r + dkm[:, h * LANE:(h + 1) * LANE]
        dkpe_r = jnp.where(_pe_lane_mask(LANE), dkpe_r, 0.0)
        dyg = _rope_bwd(dkpe_r, cos_q1, sin_q1, ROPE // 2)
        kpe = kpe_ref[...]
        r_pe = lax.rsqrt(jnp.sum(kpe * kpe, axis=-1, keepdims=True) * (1.0 / ROPE) + EPS)
        xn = kpe * r_pe
        dxn = dyg * gkpe_ref[...]
        dkpe = r_pe * (dxn - xn * (jnp.sum(dxn * xn, axis=-1, keepdims=True) * (1.0 / ROPE)))
        dkpe_ref[...] = dkpe.astype(BF16)
        gkpe_l = jnp.sum(dyg * xn, axis=0, keepdims=True)

        dqd_v, gdq_l = norm_bwd(qd_ref[...], _rope_bwd(dqd_ref[...], cos_d, sin_d, DIL_DIM // 2), gdq_ref[...],
                                segd_ref[...], expd_ref[...], invd_ref[...])
        dqdo_ref[...] = dqd_v.astype(BF16)
        dkd_v, gdk_l = norm_bwd(kd_ref[...], _rope_bwd(dkd_ref[...], cos_d, sin_d, DIL_DIM // 2), gdk_ref[...],
                                segd_ref[...], expd_ref[...], invd_ref[...])
        dkdo_ref[...] = dkd_v.astype(BF16)
        dvdo_ref[...] = dvd_ref[...].astype(BF16)

        acc_ref[0:1, :] += gq_l
        acc_ref[1:2, :] += gk_l
        acc_ref[2:3, 0:LANE] += gkpe_l
        acc_ref[3:4, 0:DIL_W] += gdq_l
        acc_ref[4:5, 0:DIL_W] += gdk_l

        @pl.when(i == n_steps - 1)
        def _():
            acc = acc_ref[...]
            fq = jnp.dot(acc, foldq_ref[...], precision=HIGHEST, preferred_element_type=F32)
            fd = jnp.dot(acc[:, 0:DIL_W], foldd_ref[...], precision=HIGHEST, preferred_element_type=F32)
            rows = lax.broadcasted_iota(I32, (8, LANE), 0)
            base = jnp.where(rows < 2, fq, jnp.where(rows == 2, acc[:, 0:LANE], fd))
            at0 = pltpu.roll(base, LANE - KPE_OFF, 1)
            dg_ref[...] = jnp.where(rows == 5, pltpu.roll(at0, 5, 0), jnp.where(rows == 2, at0, base))

    t = ROW_TILE
    row = lambda w, cb=0: pl.BlockSpec((t, w), lambda i: (i, cb))
    c = consts
    return pl.pallas_call(
        body, name="attn_prep_bwd", grid=(n_steps,),
        in_specs=[row(hw), row(hw), row(DIL_W), row(DIL_W), row(DIL_W), row(DIL_W),
                  row(hw), row(hw + DIL_W), row(LANE, P_KPE // LANE), row(DIL_W, P_QD // DIL_W), row(DIL_W, P_KD // DIL_W),
                  row(4 * LANE),
                  _full((1, hw)), _full((1, hw)), _full((1, LANE)), _full((1, DIL_W)), _full((1, DIL_W)),
                  _full((hw, LANE)), _full((LANE, hw)), _full((1, LANE)), _full((hw, LANE)), _full((LANE, hw)), _full((1, LANE)),
                  _full((DIL_W, LANE)), _full((LANE, DIL_W)), _full((1, LANE)), _full((hw, LANE)), _full((DIL_W, LANE))],
        out_specs=[row(hw), row(hw + DIL_W), row(LANE), row(DIL_W), row(DIL_W), row(DIL_W), _full((8, LANE))],
        out_shape=[jax.ShapeDtypeStruct((s, hw), BF16), jax.ShapeDtypeStruct((s, hw + DIL_W), BF16),
                   jax.ShapeDtypeStruct((s, LANE), BF16)] + [jax.ShapeDtypeStruct((s, DIL_W), BF16)] * 3
        + [jax.ShapeDtypeStruct((8, LANE), F32)],
        scratch_shapes=[pltpu.VMEM((8, hw), F32)],
        compiler_params=_params(("arbitrary",), 28 << 20),
    )(*_in_hbm(dqm, dkm, dvm, dqd, dkd, dvd, q_raw, kv_raw, proj, proj, proj), tab,
      gains["q"], gains["k"], gains["kpe"], gains["dq"], gains["dk"],
      c["seg_q"], c["exp_q"], c["inv_q"], c["seg_k"], c["exp_k"], c["inv_k"], c["seg_d"], c["exp_d"], c["inv_d"],
      c["fold_q"], c["fold_d"])


def _latnorm_bwd(dql, dkvl, proj, g_q, g_kv):
    s = proj.shape[0]
    n_steps = s // ROW_TILE

    def body(dql_ref, dkvl_ref, q_ref, kv_ref, gq_ref, gkv_ref, dq_ref, dkv_ref, dg_ref):
        i = pl.program_id(0)

        @pl.when(i == 0)
        def _():
            dg_ref[...] = jnp.zeros_like(dg_ref)

        def one(x, dyg, gain):
            r = _rms(x)
            xn = x * r
            dxn = dyg * gain
            dx = r * (dxn - xn * jnp.mean(dxn * xn, axis=-1, keepdims=True))
            return dx, jnp.sum(dyg * xn, axis=0, keepdims=True)

        dq, gq_l = one(q_ref[...], dql_ref[...], gq_ref[...])
        dkv, gkv_l = one(kv_ref[...], dkvl_ref[...], gkv_ref[...])
        dq_ref[...] = dq.astype(BF16)
        dkv_ref[...] = dkv.astype(BF16)
        dg_ref[0:1, :] += gq_l
        dg_ref[1:2, 0:KV_LORA] += gkv_l

    t = ROW_TILE
    return pl.pallas_call(
        body, name="latnorm_bwd", grid=(n_steps,),
        in_specs=[pl.BlockSpec((t, Q_LORA), lambda i: (i, 0)), pl.BlockSpec((t, KV_LORA), lambda i: (i, 0)),
                  pl.BlockSpec((t, Q_LORA), lambda i: (i, P_QLAT // Q_LORA)),
                  pl.BlockSpec((t, KV_LORA), lambda i: (i, P_KVLAT // KV_LORA)),
                  _full((1, Q_LORA)), _full((1, KV_LORA))],
        out_specs=[pl.BlockSpec((t, Q_LORA), lambda i: (i, 0)), pl.BlockSpec((t, KV_LORA), lambda i: (i, 0)), _full((8, Q_LORA))],
        out_shape=[jax.ShapeDtypeStruct((s, Q_LORA), BF16), jax.ShapeDtypeStruct((s, KV_LORA), BF16),
                   jax.ShapeDtypeStruct((8, Q_LORA), F32)],
        compiler_params=_params(("arbitrary",)),
    )(dql, dkvl, proj, proj, g_q, g_kv)


def _resid_prenorm(x, mix, g1, gain, scale, shift):
    s, d = x.shape

    def body(x_ref, mix_ref, g1_ref, g_ref, sc_ref, sh_ref, x1_ref, h_ref):
        x1 = x_ref[...] + g1_ref[...] * mix_ref[...]
        x1_ref[...] = x1
        h_ref[...] = ((x1 * _rms(x1)) * g_ref[...] * (1.0 + sc_ref[...]) + sh_ref[...]).astype(BF16)

    row = pl.BlockSpec((ROW_TILE, d), lambda i: (i, 0))
    vec = _full((1, d))
    return pl.pallas_call(
        body, name="resid_prenorm", grid=(s // ROW_TILE,),
        in_specs=[row, row, vec, vec, vec, vec], out_specs=[row, row],
        out_shape=[jax.ShapeDtypeStruct((s, d), F32), jax.ShapeDtypeStruct((s, d), BF16)],
        compiler_params=_params(("parallel",)),
    )(x, mix, g1, gain, scale, shift)


CONV_TILE = 1408
HALO = 8


def _shift_down(x, halo, k):
    t = x.shape[0]
    row = lax.broadcasted_iota(I32, (t, 1), 0)
    out = pltpu.roll(x, k, 0)
    for r in range(k):
        out = jnp.where(row == r, halo[HALO - k + r:HALO - k + r + 1, :], out)
    return out


def _shift_up(x, halo, k):
    t = x.shape[0]
    row = lax.broadcasted_iota(I32, (t, 1), 0)
    out = pltpu.roll(x, t - k, 0)
    for r in range(k):
        out = jnp.where(row == t - k + r, halo[r:r + 1, :], out)
    return out


def _conv_fwd(x, halo, w, b):
    p1, p2 = _shift_down(x, halo, 1), _shift_down(x, halo, 2)
    u = b + p2 * w[0:1, :]
    u = u + p1 * w[1:2, :]
    u = u + x * w[2:3, :]
    return u, p1, p2


def _sigmoid(x):
    return 1.0 / (1.0 + jnp.exp(-x))


def _conv_gate(up, w_conv, b_conv):
    s = up.shape[0]
    t = ROW_TILE
    nj = D_FF // CONV_TILE
    hb = t // HALO

    def body(g_ref, v_ref, gh_ref, vh_ref, wg_ref, wv_ref, bg_ref, bv_ref, a_ref):
        live = (pl.program_id(0) > 0).astype(F32)
        ug, _, _ = _conv_fwd(g_ref[...], gh_ref[...] * live, wg_ref[...], bg_ref[...])
        uv, _, _ = _conv_fwd(v_ref[...], vh_ref[...] * live, wv_ref[...], bv_ref[...])
        a_ref[...] = (ug * _sigmoid(ug) * uv).astype(BF16)

    main = lambda off: pl.BlockSpec((t, CONV_TILE), lambda i, j: (i, j + off))
    halo = lambda off: pl.BlockSpec((HALO, CONV_TILE), lambda i, j: (jnp.maximum(i * hb - 1, 0), j + off))
    wsp = lambda off: pl.BlockSpec((3, CONV_TILE), lambda i, j: (0, j + off))
    bsp = lambda off: pl.BlockSpec((1, CONV_TILE), lambda i, j: (0, j + off))
    return pl.pallas_call(
        body, name="conv_gate", grid=(s // t, nj),
        in_specs=[main(0), main(nj), halo(0), halo(nj), wsp(0), wsp(nj), bsp(0), bsp(nj)],
        out_specs=pl.BlockSpec((t, CONV_TILE), lambda i, j: (i, j)),
        out_shape=jax.ShapeDtypeStruct((s, D_FF), BF16),
        compiler_params=_params(("parallel", "parallel"), 12 << 20),
    )(up, up, up, up, w_conv, w_conv, b_conv, b_conv)


def _gate_bwd(up, da, w_conv, b_conv):
    s = up.shape[0]
    t = ROW_TILE
    nj = D_FF // CONV_TILE
    hb = t // HALO
    n_i = s // t

    def body(g_ref, v_ref, gh_ref, vh_ref, gn_ref, vn_ref, da_ref, dan_ref, wg_ref, wv_ref, bg_ref, bv_ref,
             dupg_ref, dupv_ref, dbg_ref, dbv_ref, dwg_ref, dwv_ref):
        i = pl.program_id(1)

        @pl.when(i == 0)
        def _():
            for r in (dbg_ref, dbv_ref, dwg_ref, dwv_ref):
                r[...] = jnp.zeros_like(r)

        def d_gate(ug, uv, da_v):
            sg = _sigmoid(ug)
            return da_v * uv * (sg * (1.0 + ug * (1.0 - sg))), da_v * (ug * sg)

        live = (i > 0).astype(F32)
        xg, xv = g_ref[...], v_ref[...]
        wg, wv = wg_ref[...], wv_ref[...]
        ug, g1, g2 = _conv_fwd(xg, gh_ref[...] * live, wg, bg_ref[...])
        uv, v1, v2 = _conv_fwd(xv, vh_ref[...] * live, wv, bv_ref[...])
        dug, duv = d_gate(ug, uv, da_ref[...])

        more = (i < n_i - 1).astype(F32)
        ug_n, _, _ = _conv_fwd(gn_ref[...], xg[t - HALO:, :], wg, bg_ref[...])
        uv_n, _, _ = _conv_fwd(vn_ref[...], xv[t - HALO:, :], wv, bv_ref[...])
        dug_n, duv_n = d_gate(ug_n, uv_n, dan_ref[...] * more)

        def conv_t(du, du_n, w):
            return du * w[2:3, :] + _shift_up(du, du_n, 1) * w[1:2, :] + _shift_up(du, du_n, 2) * w[0:1, :]

        dupg_ref[...] = conv_t(dug, dug_n, wg).astype(BF16)
        dupv_ref[...] = conv_t(duv, duv_n, wv).astype(BF16)
        csum = lambda z: jnp.sum(z, axis=0, keepdims=True)
        dbg_ref[...] += csum(dug)
        dbv_ref[...] += csum(duv)
        dwg_ref[0:1, :] += csum(dug * g2)
        dwg_ref[1:2, :] += csum(dug * g1)
        dwg_ref[2:3, :] += csum(dug * xg)
        dwv_ref[0:1, :] += csum(duv * v2)
        dwv_ref[1:2, :] += csum(duv * v1)
        dwv_ref[2:3, :] += csum(duv * xv)

    last_halo = s // HALO - 1
    main = lambda off: pl.BlockSpec((t, CONV_TILE), lambda j, i: (i, j + off))
    halo = lambda off: pl.BlockSpec((HALO, CONV_TILE), lambda j, i: (jnp.maximum(i * hb - 1, 0), j + off))
    nxt = lambda off: pl.BlockSpec((HALO, CONV_TILE), lambda j, i: (jnp.minimum((i + 1) * hb, last_halo), j + off))
    wsp = lambda off: pl.BlockSpec((3, CONV_TILE), lambda j, i: (0, j + off))
    bsp = lambda off: pl.BlockSpec((1, CONV_TILE), lambda j, i: (0, j + off))
    outs = pl.pallas_call(
        body, name="gate_bwd", grid=(nj, n_i),
        in_specs=[main(0), main(nj), halo(0), halo(nj), nxt(0), nxt(nj), main(0), nxt(0),
                  wsp(0), wsp(nj), bsp(0), bsp(nj)],
        out_specs=[main(0), main(0),
                   pl.BlockSpec((1, CONV_TILE), lambda j, i: (0, j)), pl.BlockSpec((1, CONV_TILE), lambda j, i: (0, j)),
                   pl.BlockSpec((3, CONV_TILE), lambda j, i: (0, j)), pl.BlockSpec((3, CONV_TILE), lambda j, i: (0, j))],
        out_shape=[jax.ShapeDtypeStruct((s, D_FF), BF16), jax.ShapeDtypeStruct((s, D_FF), BF16),
                   jax.ShapeDtypeStruct((1, D_FF), F32), jax.ShapeDtypeStruct((1, D_FF), F32),
                   jax.ShapeDtypeStruct((3, D_FF), F32), jax.ShapeDtypeStruct((3, D_FF), F32)],
        compiler_params=_params(("parallel", "arbitrary"), 24 << 20),
    )(up, up, up, up, up, up, da, da, w_conv, w_conv, b_conv, b_conv)
    return outs


def _final(x1, ffn, tgt, g2):
    s, d = x1.shape
    n_steps = s // ROW_TILE

    def body(x1_ref, f_ref, t_ref, g2_ref, dy_ref, df_ref, dg2_ref, loss_ref, lacc_ref):
        i = pl.program_id(0)

        @pl.when(i == 0)
        def _():
            dg2_ref[...] = jnp.zeros_like(dg2_ref)
            lacc_ref[...] = jnp.zeros_like(lacc_ref)

        f = f_ref[...]
        e = x1_ref[...] + g2_ref[...] * f - t_ref[...]
        dy = e * (1.0 / d)
        dy_ref[...] = dy
        df_ref[...] = (dy * g2_ref[...]).astype(BF16)
        dg2_ref[...] += jnp.sum(dy * f, axis=0, keepdims=True)
        lacc_ref[...] += jnp.sum(e * e, axis=0, keepdims=True)

        @pl.when(i == n_steps - 1)
        def _():
            loss_ref[...] = jnp.sum(lacc_ref[...], axis=1, keepdims=True) * (0.5 / d)

    row = pl.BlockSpec((ROW_TILE, d), lambda i: (i, 0))
    return pl.pallas_call(
        body, name="final", grid=(n_steps,),
        in_specs=[row, row, row, _full((1, d))],
        out_specs=[row, row, _full((1, d)), _full((1, 1))],
        out_shape=[jax.ShapeDtypeStruct((s, d), F32), jax.ShapeDtypeStruct((s, d), BF16),
                   jax.ShapeDtypeStruct((1, d), F32), jax.ShapeDtypeStruct((1, 1), F32)],
        scratch_shapes=[pltpu.VMEM((1, d), F32)],
        compiler_params=_params(("arbitrary",)),
    )(x1, ffn, tgt, g2)


def _ffnnorm_bwd(dh2, x1, dy, mix, gain, scale, g1):
    s, d = x1.shape
    n_steps = s // ROW_TILE

    def body(dh_ref, x_ref, dy_ref, mix_ref, g_ref, sc_ref, g1_ref, dx_ref, dm_ref, acc_ref):
        i = pl.program_id(0)

        @pl.when(i == 0)
        def _():
            acc_ref[...] = jnp.zeros_like(acc_ref)

        dh, x = dh_ref[...], x_ref[...]
        r = _rms(x)
        xn = x * r
        dn = dh * (1.0 + sc_ref[...])
        dxn = dn * g_ref[...]
        dx = dy_ref[...] + r * (dxn - xn * jnp.mean(dxn * xn, axis=-1, keepdims=True))
        dx_ref[...] = dx
        dm_ref[...] = (dx * g1_ref[...]).astype(BF16)
        csum = lambda z: jnp.sum(z, axis=0, keepdims=True)
        acc_ref[0:1, :] += csum(dh)
        acc_ref[1:2, :] += csum(dh * (xn * g_ref[...]))
        acc_ref[2:3, :] += csum(dn * xn)
        acc_ref[3:4, :] += csum(dx * mix_ref[...])

    row = pl.BlockSpec((ROW_TILE, d), lambda i: (i, 0))
    vec = _full((1, d))
    return pl.pallas_call(
        body, name="ffnnorm_bwd", grid=(n_steps,),
        in_specs=[row, row, row, row, vec, vec, vec],
        out_specs=[row, row, _full((8, d))],
        out_shape=[jax.ShapeDtypeStruct((s, d), F32), jax.ShapeDtypeStruct((s, d), BF16), jax.ShapeDtypeStruct((8, d), F32)],
        compiler_params=_params(("arbitrary",)),
    )(dh2, x1, dy, mix, gain, scale, g1)


def _mixnorm_bwd(dh, x, dx1, gain, scale):
    s, d = x.shape
    n_steps = s // ROW_TILE

    def body(dh_ref, x_ref, dx1_ref, g_ref, sc_ref, gx_ref, acc_ref):
        i = pl.program_id(0)

        @pl.when(i == 0)
        def _():
            acc_ref[...] = jnp.zeros_like(acc_ref)

        dh, x = dh_ref[...], x_ref[...]
        r = _rms(x)
        xn = x * r
        dn = dh * (1.0 + sc_ref[...])
        dxn = dn * g_ref[...]
        gx_ref[...] = dx1_ref[...] + r * (dxn - xn * jnp.mean(dxn * xn, axis=-1, keepdims=True))
        csum = lambda z: jnp.sum(z, axis=0, keepdims=True)
        acc_ref[0:1, :] += csum(dh)
        acc_ref[1:2, :] += csum(dh * (xn * g_ref[...]))
        acc_ref[2:3, :] += csum(dn * xn)

    row = pl.BlockSpec((ROW_TILE, d), lambda i: (i, 0))
    vec = _full((1, d))
    return pl.pallas_call(
        body, name="mixnorm_bwd", grid=(n_steps,),
        in_specs=[row, row, row, vec, vec],
        out_specs=[row, _full((8, d))],
        out_shape=[jax.ShapeDtypeStruct((s, d), F32), jax.ShapeDtypeStruct((8, d), F32)],
        compiler_params=_params(("arbitrary",)),
    )(dh, x, dx1, gain, scale)


def _key_count(d, dilated):
    if not dilated:
        return jnp.where(d >= 0, 1.0, 0.0)
    one = lambda cond: jnp.where(cond, 1.0, 0.0)
    cnt = one(d <= 128) + one(((d & 3) == 0) & (d <= 512)) + one((d & 15) == 0)
    return jnp.where(d >= 0, cnt, 0.0)


def _block_kinds(mla):
    return (0, "diag", "none") if mla else (512, "near", "far")


NEAR_OFFSETS = 4


def _scores_t(ka, qa, scale, kind, rel_t, offset, near_tabs=None):
    return _mask_scores(lax.dot_general(ka, qa, NT, preferred_element_type=F32), scale, kind, rel_t, offset, near_tabs)


def _fill_near_tables(bias_ref, cnt_ref, rel_t):
    for idx in range(NEAR_OFFSETS):
        cnt = _key_count(rel_t + (idx - 1) * ATT_TK, True)
        cnt_ref[idx] = cnt
        bias_ref[idx] = jnp.where(cnt > 0.0, 0.0, NEG_INF)


def _mask_scores(products, scale, kind, rel_t, offset, near_tabs=None):
    st = products * (scale * LOG2E)
    cnt = None
    if kind == "diag":
        st = jnp.where(rel_t + offset >= 0, st, NEG_INF)
    elif kind == "far":
        st = jnp.where((rel_t & 15) == 0, st, NEG_INF)
    elif kind == "near":
        bias_ref, cnt_ref = near_tabs
        idx = offset // ATT_TK + 1
        st = st + bias_ref[idx]
        cnt = cnt_ref[idx]
    return st, cnt


def _attn_fwd(q, k, v, mla, scale, name, gather=()):
    s = q.shape[0]
    qw = 2 * LANE if mla else LANE
    tq, tk = ATT_TQ, ATT_TK
    reach, kind_near, kind_far = _block_kinds(mla)
    assert s % tq == 0 and tq % tk == 0 and reach % tk == 0 and (mla or (reach + tq) // tk == NEAR_OFFSETS)
    ng = len(gather)
    last_step = HEADS // 2 - 1

    def body(*refs):
        q_ref, k_ref, v_ref = refs[:3]
        o_ref, lse_ref = refs[3 + ng:5 + ng]
        vt_ref, st_ref = refs[5 + 2 * ng:7 + 2 * ng]
        near_tabs = None if mla else refs[7 + 2 * ng:9 + 2 * ng]
        n_tabs = 0 if mla else 2
        comm = (refs[3:3 + ng], refs[5 + ng:5 + 2 * ng]) + tuple(refs[7 + n_tabs + 2 * ng:])
        if ng:
            @pl.when(pl.program_id(0) == 0)
            def _():
                _Gather(*comm).start()

            @pl.when(pl.program_id(0) == last_step)
            def _():
                _Gather(*comm).forward()

        lane = lax.broadcasted_iota(I32, (1, LANE), 1)
        rel_t = lax.broadcasted_iota(I32, (tk, tq), 1) - lax.broadcasted_iota(I32, (tk, tq), 0)
        if not mla:
            _fill_near_tables(*near_tabs, rel_t)

        def transpose_v(j, carry):
            c0 = pl.multiple_of(j * tk, tk)
            vt_ref[:, pl.ds(c0, tk)] = v_ref[pl.ds(c0, tk), :].astype(F32).T.astype(BF16)
            return carry

        lax.fori_loop(0, s // tk, transpose_v, 0)

        def q_block(qi, carry):
            r0 = pl.multiple_of(qi * tq, tq)
            kcols = [slice(a * LANE, (a + 1) * LANE) if mla else slice(0, LANE) for a in range(2)]
            qas = [q_ref[pl.ds(r0, tq), kcols[a]] for a in range(2)]
            if not mla:
                qas = [jnp.where(lane < DIL_DIM, qas[0], jnp.zeros_like(qas[0])),
                       jnp.where(lane >= DIL_DIM, qas[1], jnp.zeros_like(qas[1]))]

            n_k = (r0 + tq) // tk

            def products(kj):
                c0 = pl.multiple_of(kj * tk, tk)
                return [lax.dot_general(k_ref[pl.ds(c0, tk), kcols[a]], qas[a], NT, preferred_element_type=F32)
                        for a in range(2)]

            for a, pr in enumerate(products(0)):
                st_ref[0, a] = pr

            def k_block(kj, c, kind):
                c0 = pl.multiple_of(kj * tk, tk)
                slot = kj & 1
                ahead = products(jnp.minimum(kj + 1, n_k - 1))
                out = []
                for a in range(2):
                    m, l, acc = c[a]
                    st, cnt = _mask_scores(st_ref[slot, a], scale, kind, rel_t, r0 - c0, near_tabs)
                    st_ref[1 - slot, a] = ahead[a]
                    m_new = jnp.maximum(m, jnp.max(st, axis=0, keepdims=True))
                    alpha = jnp.exp2(m - m_new)
                    p = jnp.exp2(st - m_new)
                    if cnt is not None:
                        p = p * cnt
                    l = alpha * l + jnp.sum(p, axis=0, keepdims=True)
                    vt = vt_ref[a * DIL_DIM:(a + 1) * DIL_DIM, pl.ds(c0, tk)]
                    acc = alpha * acc + jnp.dot(vt, p.astype(BF16), preferred_element_type=F32)
                    out.append((m_new, l, acc))
                return tuple(out)

            one = (jnp.full((1, tq), NEG_INF, F32), jnp.zeros((1, tq), F32), jnp.zeros((DIL_DIM, tq), F32))
            first_near = jnp.maximum((r0 - reach) // tk, 0)
            c = lax.fori_loop(0, first_near, functools.partial(k_block, kind=kind_far), (one, one))
            res = lax.fori_loop(first_near, (r0 + tq) // tk, functools.partial(k_block, kind=kind_near), c)
            o_t = jnp.concatenate([res[a][2] / res[a][1] for a in range(2)], axis=0)
            o_ref[pl.ds(r0, tq), :] = o_t.T.astype(BF16)
            for a in range(2):
                lse_ref[a, :, pl.ds(r0, tq)] = res[a][0] * LN2 + jnp.log(res[a][1])
            return carry

        lax.fori_loop(0, s // tq, q_block, 0)

        if ng:
            @pl.when(pl.program_id(0) == last_step)
            def _():
                _Gather(*comm).finish()

    return pl.pallas_call(
        body, name=name, grid=(HEADS // 2,),
        in_specs=[pl.BlockSpec((s, qw), lambda h: (0, h)), pl.BlockSpec((s, qw), lambda h: (0, h)),
                  pl.BlockSpec((s, LANE), lambda h: (0, h))] + [ANY] * ng,
        out_specs=[pl.BlockSpec((s, LANE), lambda h: (0, h)), pl.BlockSpec((2, 1, s), lambda h: (h, 0, 0))] + [ANY] * ng,
        out_shape=[jax.ShapeDtypeStruct((s, DIL_W), BF16), jax.ShapeDtypeStruct((HEADS, 1, s), F32)] + _Gather.out_shapes(gather),
        scratch_shapes=[pltpu.VMEM((LANE, s), BF16), pltpu.VMEM((2, 2, tk, tq), F32)]
        + ([] if mla else [pltpu.VMEM((NEAR_OFFSETS, tk, tq), F32)] * 2) + (_Gather.scratch(gather) if ng else []),
        compiler_params=_params(("arbitrary",) if ng else ("parallel",), 12 << 20),
    )(*_in_hbm(q, k, v), *gather)


def _attn_bwd(q, k, v, o, do, do_block0, lse, mla, scale, name, scatter=()):
    s = q.shape[0]
    qw = 2 * LANE if mla else LANE
    tq, tk = ATT_TQ, ATT_TK
    nq = s // tq
    reach, kind_near, kind_far = _block_kinds(mla)
    assert s % tq == 0 and tq % tk == 0
    ns = len(scatter)
    last_step = HEADS // 2 - 1

    def body(*refs):
        q_ref, k_ref, v_ref, o_ref, do_ref, lse_ref = refs[:6]
        dq_ref, dk_ref, dv_ref = refs[6 + ns:9 + ns]
        kt_ref, dot_ref, dob_ref, dqt_ref, delta_ref, lse2_ref = refs[9 + 2 * ns:15 + 2 * ns]
        near_tabs = None if mla else refs[15 + 2 * ns:17 + 2 * ns]
        n_tabs = 0 if mla else 2
        comm = (refs[6:6 + ns], refs[9 + ns:9 + 2 * ns]) + tuple(refs[15 + n_tabs + 2 * ns:])
        if ns:
            @pl.when(pl.program_id(0) == 0)
            def _():
                _Scatter(*comm).start()

        lane = lax.broadcasted_iota(I32, (1, LANE), 1)
        row = lax.broadcasted_iota(I32, (LANE, 1), 0)
        rel_t = lax.broadcasted_iota(I32, (tk, tq), 1) - lax.broadcasted_iota(I32, (tk, tq), 0)
        if not mla:
            _fill_near_tables(*near_tabs, rel_t)

        def prepare(j, carry):
            c0 = pl.multiple_of(j * tk, tk)
            do_blk = do_ref[pl.ds(c0, tk), :]
            dob_ref[pl.ds(c0, tk), :] = do_blk.astype(BF16)
            do_t = do_blk.T
            dot_ref[:, pl.ds(c0, tk)] = do_t.astype(BF16)
            prod = do_t * o_ref[pl.ds(c0, tk), :].astype(F32).T
            delta_ref[0, :, pl.ds(c0, tk)] = jnp.sum(prod[0:DIL_DIM], axis=0, keepdims=True)
            delta_ref[1, :, pl.ds(c0, tk)] = jnp.sum(prod[DIL_DIM:LANE], axis=0, keepdims=True)
            for w in range(qw // LANE):
                kt_ref[w * LANE:(w + 1) * LANE, pl.ds(c0, tk)] = (
                    k_ref[pl.ds(c0, tk), w * LANE:(w + 1) * LANE].astype(F32).T.astype(BF16))
            return carry

        lax.fori_loop(0, s // tk, prepare, 0)
        dqt_ref[...] = jnp.zeros_like(dqt_ref)
        lse2_ref[...] = lse_ref[...] * LOG2E

        sels = [lane < DIL_DIM, lane >= DIL_DIM]
        rsels = [row < DIL_DIM, row >= DIL_DIM]
        cols = [slice(a * LANE, (a + 1) * LANE) if mla else slice(0, LANE) for a in range(2)]

        def k_block(kj, carry):
            c0 = pl.multiple_of(kj * tk, tk)
            kas = [k_ref[pl.ds(c0, tk), cols[a]] for a in range(2)]
            kts = [kt_ref[cols[a], pl.ds(c0, tk)] for a in range(2)]
            if not mla:
                kas = [jnp.where(sels[a], kas[a], jnp.zeros_like(kas[a])) for a in range(2)]
                kts = [jnp.where(rsels[a], kts[a], jnp.zeros_like(kts[a])) for a in range(2)]
            vb = v_ref[pl.ds(c0, tk), :]
            vbs = [jnp.where(sels[a], vb, jnp.zeros_like(vb)) for a in range(2)]

            first = c0 // tq

            def q_block(qi, c, kind):
                r0 = pl.multiple_of(qi * tq, tq)
                out, dq_parts = [], []
                for a in range(2):
                    dk_acc, dv_acc = c[a]
                    qa = q_ref[pl.ds(r0, tq), cols[a]]
                    st, cnt = _scores_t(kas[a], qa, scale, kind, rel_t, r0 - c0, near_tabs)
                    p = jnp.exp2(st - lse2_ref[a, :, pl.ds(r0, tq)])
                    if cnt is not None:
                        p = p * cnt
                    dp = jnp.dot(vbs[a], dot_ref[:, pl.ds(r0, tq)], preferred_element_type=F32)
                    ds = (p * (dp - delta_ref[a, :, pl.ds(r0, tq)]) * scale).astype(BF16)
                    dv_acc = dv_acc + jnp.dot(p.astype(BF16), dob_ref[pl.ds(r0, tq), :], preferred_element_type=F32)
                    dk_acc = dk_acc + jnp.dot(ds, qa, preferred_element_type=F32)
                    dq_parts.append(jnp.dot(kts[a], ds, preferred_element_type=F32))
                    out.append((dk_acc, dv_acc))
                if mla:
                    for a in range(2):
                        dqt_ref[cols[a], pl.ds(r0, tq)] += dq_parts[a]
                else:
                    dqt_ref[:, pl.ds(r0, tq)] += dq_parts[0] + dq_parts[1]
                return tuple(out)

            zero = jnp.zeros((tk, LANE), F32)
            last_near = jnp.minimum((c0 + tk - 1 + reach) // tq + 1, nq)
            c = lax.fori_loop(first, last_near, functools.partial(q_block, kind=kind_near), ((zero, zero), (zero, zero)))
            (dk0, dv0), (dk1, dv1) = lax.fori_loop(last_near, nq, functools.partial(q_block, kind=kind_far), c)
            if mla:
                dk_ref[pl.ds(c0, tk), cols[0]] = dk0
                dk_ref[pl.ds(c0, tk), cols[1]] = dk1
            else:
                dk_ref[pl.ds(c0, tk), :] = jnp.where(sels[0], dk0, dk1)
            dv_ref[pl.ds(c0, tk), :] = jnp.where(sels[0], dv0, dv1)
            return carry

        lax.fori_loop(0, s // tk, k_block, 0)

        def write_dq(j, carry):
            c0 = pl.multiple_of(j * tk, tk)
            for w in range(qw // LANE):
                dq_ref[pl.ds(c0, tk), w * LANE:(w + 1) * LANE] = dqt_ref[w * LANE:(w + 1) * LANE, pl.ds(c0, tk)].T
            return carry

        lax.fori_loop(0, s // tk, write_dq, 0)

        if ns:
            @pl.when(pl.program_id(0) == last_step)
            def _():
                _Scatter(*comm).finish()

    b0 = do_block0
    return pl.pallas_call(
        body, name=name, grid=(HEADS // 2,),
        in_specs=[pl.BlockSpec((s, qw), lambda h: (0, h)), pl.BlockSpec((s, qw), lambda h: (0, h)),
                  pl.BlockSpec((s, LANE), lambda h: (0, h)), pl.BlockSpec((s, LANE), lambda h: (0, h)),
                  pl.BlockSpec((s, LANE), lambda h: (0, h + b0)), pl.BlockSpec((2, 1, s), lambda h: (h, 0, 0))] + [ANY] * ns,
        out_specs=[pl.BlockSpec((s, qw), lambda h: (0, h)), pl.BlockSpec((s, qw), lambda h: (0, h)),
                   pl.BlockSpec((s, LANE), lambda h: (0, h))] + [ANY] * ns,
        out_shape=[jax.ShapeDtypeStruct(q.shape, F32), jax.ShapeDtypeStruct(k.shape, F32), jax.ShapeDtypeStruct((s, DIL_W), F32)]
        + _Scatter.out_shapes(scatter),
        scratch_shapes=[pltpu.VMEM((qw, s), BF16), pltpu.VMEM((LANE, s), BF16), pltpu.VMEM((s, LANE), BF16),
                        pltpu.VMEM((qw, s), F32), pltpu.VMEM((2, 1, s), F32), pltpu.VMEM((2, 1, s), F32)]
        + ([] if mla else [pltpu.VMEM((NEAR_OFFSETS, tk, tq), F32)] * 2) + (_Scatter.semaphores(ns) if ns else []),
        compiler_params=_params(("arbitrary",) if ns else ("parallel",), 24 << 20),
    )(*_in_hbm(q, k, v, o, do, lse), *scatter)


def _ada_bwd(c_all, dmod_shard):
    n, d = c_all.shape
    cols = dmod_shard.shape[1]

    def body(c_ref, g_ref, o_ref):
        cv = c_ref[...]
        o_ref[...] = lax.dot_general(cv * _sigmoid(cv), g_ref[...], TN, precision=HIGHEST, preferred_element_type=F32)

    return pl.pallas_call(
        body, name="ada_bwd", out_shape=jax.ShapeDtypeStruct((d, cols), F32),
        compiler_params=_params(None, 16 << 20),
    )(c_all, dmod_shard)


SMALL_WIDTHS = (("g_mix_norm", D_MODEL), ("g_q_lat", Q_LORA), ("g_kv_lat", KV_LORA), ("g_mla_q_nope", NOPE),
                ("g_mla_q_pe", ROPE), ("g_mla_k_nope", NOPE), ("g_mla_k_pe", ROPE), ("g_dil_q", DIL_DIM),
                ("g_dil_k", DIL_DIM), ("g_ffn_norm", D_MODEL), ("b_conv", UP_W))


def _small_layout():
    pieces = (("dmod", 6 * D_MODEL),) + SMALL_WIDTHS + tuple(("w_conv%d" % k, UP_W) for k in range(3)) + (("loss", 1),)
    layout, off = {}, 0
    for name, width in pieces:
        layout[name] = (width, off)
        off += -(-width // LANE) * LANE
    return layout, off


def _pack_small(acc1, acc2, dg2, dglat, dgains, dbg, dbv, dwg, dwv, loss_part):
    layout, total = _small_layout()

    def body(a1, a2, g2, gl, gg, bg, bv, wg, wv, ls, o_ref):
        o_ref[...] = jnp.zeros_like(o_ref)

        def put(name, src, shift=0):
            start = layout[name][1] + shift
            o_ref[:, start:start + src.shape[1]] = src

        for k, src in enumerate((a1[0:1, :], a1[1:2, :], a2[3:4, :], a2[0:1, :], a2[1:2, :], g2[...])):
            put("dmod", src, k * D_MODEL)
        put("g_mix_norm", a1[2:3, :])
        put("g_q_lat", gl[0:1, :])
        put("g_kv_lat", gl[1:2, 0:KV_LORA])
        put("g_mla_q_nope", gg[0:1, 0:NOPE])
        put("g_mla_q_pe", gg[5:6, 0:ROPE])
        put("g_mla_k_nope", gg[1:2, 0:NOPE])
        put("g_mla_k_pe", gg[2:3, 0:ROPE])
        put("g_dil_q", gg[3:4, 0:DIL_DIM])
        put("g_dil_k", gg[4:5, 0:DIL_DIM])
        put("g_ffn_norm", a2[2:3, :])
        put("b_conv", bg[...])
        put("b_conv", bv[...], D_FF)
        for k in range(3):
            put("w_conv%d" % k, wg[k:k + 1, :])
            put("w_conv%d" % k, wv[k:k + 1, :], D_FF)
        put("loss", ls[...])

    ins = (acc1, acc2, dg2, dglat, dgains, dbg, dbv, dwg, dwv, loss_part)
    return pl.pallas_call(
        body, name="pack_small", grid=(1,), in_specs=[_full(a.shape) for a in ins], out_specs=_full((1, total)),
        out_shape=jax.ShapeDtypeStruct((1, total), F32),
        compiler_params=_params(("arbitrary",), 2 << 20),
    )(*_in_hbm(*ins))


def _sum_unpack(g):
    n_dev, _, total = g.shape
    layout, _ = _small_layout()

    def body(g_ref, *refs):
        o_refs, s_ref = refs[:-1], refs[-1]
        acc = g_ref[0]
        for k in range(1, n_dev):
            acc = acc + g_ref[k]
        s_ref[...] = acc
        take = lambda name: s_ref[:, layout[name][1]:layout[name][1] + layout[name][0]]
        o_refs[0][...] = take("dmod")
        for i, (name, _) in enumerate(SMALL_WIDTHS):
            o_refs[1 + i][...] = take(name)
        for k in range(3):
            o_refs[-2][k:k + 1, :] = take("w_conv%d" % k)
        o_refs[-1][...] = take("loss")

    shapes = [(1, 6 * D_MODEL)] + [(1, w) for _, w in SMALL_WIDTHS] + [(3, UP_W), (1, 1)]
    return pl.pallas_call(
        body, name="sum_unpack", out_shape=[jax.ShapeDtypeStruct(sh, F32) for sh in shapes],
        scratch_shapes=[pltpu.VMEM((1, total), F32)],
        compiler_params=_params(None, 4 << 20),
    )(g)


def _adamw_math(w, g, m, v):
    mn = ADAM_B1 * m + (1.0 - ADAM_B1) * g
    vn = ADAM_B2 * v + (1.0 - ADAM_B2) * (g * g)
    m_hat = mn / (1.0 - ADAM_B1 ** ADAM_STEP)
    v_hat = vn / (1.0 - ADAM_B2 ** ADAM_STEP)
    return -ADAM_LR * (m_hat / (jnp.sqrt(v_hat) + ADAM_EPS) + ADAM_WD * w), mn, vn


def _adamw_vectors(ws, gs, ms, vs):
    k = len(ws)

    def body(*refs):
        for i in range(k):
            d, mn, vn = _adamw_math(refs[i][...], refs[k + i][...], refs[2 * k + i][...], refs[3 * k + i][...])
            refs[4 * k + i][...] = d
            refs[5 * k + i][...] = mn
            refs[6 * k + i][...] = vn

    blocks = [_full(w.shape) for w in ws]
    outs = pl.pallas_call(
        body, name="adamw_vectors", grid=(1,), in_specs=blocks * 4, out_specs=blocks * 3,
        out_shape=[jax.ShapeDtypeStruct(w.shape, F32) for w in ws] * 3,
        compiler_params=_params(("arbitrary",), 2 << 20),
    )(*_in_hbm(*ws, *gs, *ms, *vs))
    return outs[:k], outs[k:2 * k], outs[2 * k:]


def _adamw(w, g, m, v, name):
    r, c = w.shape
    tr = r
    for cand in (256, 128, 64, 32, 16):
        if r % cand == 0 and r > cand:
            tr = cand
            break

    def body(w_ref, g_ref, m_ref, v_ref, d_ref, mo_ref, vo_ref):
        d_ref[...], mo_ref[...], vo_ref[...] = _adamw_math(w_ref[...], g_ref[...], m_ref[...], v_ref[...])

    blk = pl.BlockSpec((tr, c), lambda i: (i, 0))
    return pl.pallas_call(
        body, name=name, grid=(r // tr,), in_specs=[blk] * 4, out_specs=[blk] * 3,
        out_shape=[jax.ShapeDtypeStruct((r, c), F32)] * 3,
        compiler_params=_params(("parallel",), 7 * _nbytes((tr, c), F32)),
    )(w, g, m, v)


def _position():
    return lax.axis_index("x"), lax.axis_index("y"), lax.axis_index("c")


def _other_chips(x, y):
    return [(1 - x, y, 2 * (1 - x) + y), (x, 1 - y, 2 * x + (1 - y)), (1 - x, 1 - y, 2 * (1 - x) + (1 - y))]


class _SmallGather:
    def __init__(self, v_ref, out_ref, send_sems, recv_sems, local_sem):
        x, y, c = _position()
        me = 4 * x + 2 * y + c
        self.local = pltpu.make_async_copy(v_ref, out_ref.at[me], local_sem)
        self.sends, self.arrivals = [], []
        for k in range(N_DEV - 1):
            fx, fy, fc = ((k + 1) >> 2) & 1, ((k + 1) >> 1) & 1, (k + 1) & 1
            px, py, pc = (1 - x if fx else x), (1 - y if fy else y), (1 - c if fc else c)

            def copy(dst, k=k, peer=(px, py, pc)):
                return pltpu.make_async_remote_copy(src_ref=v_ref, dst_ref=dst, send_sem=send_sems.at[k],
                                                    recv_sem=recv_sems.at[k], device_id=peer, device_id_type=MESH)

            self.sends.append(copy(out_ref.at[me]))
            self.arrivals.append(copy(out_ref.at[4 * px + 2 * py + pc]))

    @staticmethod
    def semaphores():
        return [pltpu.SemaphoreType.DMA((N_DEV - 1,)), pltpu.SemaphoreType.DMA((N_DEV - 1,)), pltpu.SemaphoreType.DMA]

    def start(self):
        self.local.start()
        for cp in self.sends:
            cp.start()

    def finish(self):
        for cp in self.arrivals:
            cp.wait_recv()
        for cp in self.sends:
            cp.wait_send()
        self.local.wait()


def _prologue(c_taps, w_ada_shard, b_shard, pos_col, rope_consts, shards):
    n = len(shards)
    s = pos_col.shape[0]
    cols = w_ada_shard.shape[1]
    freq, csel, ssel = rope_consts

    def body(*refs):
        ct_ref, w_ref, b_ref, p_ref, f_ref, cs_ref, ss_ref = refs[:7]
        sh_refs = refs[7:7 + n]
        ct_all_ref, mod_all_ref, tab_ref = refs[7 + n:10 + n]
        g_refs = refs[10 + n:10 + 2 * n]
        mod_blk_ref = refs[10 + 2 * n]
        sems = refs[11 + 2 * n:]
        weights = _Gather(sh_refs, g_refs, *sems[6:])
        weights.start()
        first = _SmallGather(ct_ref, ct_all_ref, *sems[0:3])
        first.start()
        first.finish()
        cv = ct_all_ref[:, 0, 0:D_MODEL]
        sc = (cv * _sigmoid(cv)).astype(BF16)
        mod_blk_ref[...] = jnp.dot(sc, w_ref[...].astype(BF16), preferred_element_type=F32) + b_ref[...]
        second = _SmallGather(mod_blk_ref, mod_all_ref, *sems[3:6])
        second.start()

        def table_rows(i, carry):
            r0 = pl.multiple_of(i * ROW_TILE, ROW_TILE)
            ang = p_ref[pl.ds(r0, ROW_TILE), :].astype(F32) * f_ref[...]
            tab_ref[pl.ds(r0, ROW_TILE), :] = cs_ref[...] * jnp.cos(ang) + ss_ref[...] * jnp.sin(ang)
            return carry

        lax.fori_loop(0, s // ROW_TILE, table_rows, 0)
        second.finish()
        weights.forward()
        weights.finish()

    return pl.pallas_call(
        body, name="prologue",
        out_shape=[jax.ShapeDtypeStruct((N_DEV,) + c_taps.shape, F32), jax.ShapeDtypeStruct((N_DEV, N_DEV, cols), F32),
                   jax.ShapeDtypeStruct((s, 4 * LANE), F32)] + _Gather.out_shapes(shards),
        in_specs=[IN_VMEM] * 7 + [ANY] * n, out_specs=[IN_VMEM] * 3 + [ANY] * n,
        scratch_shapes=[pltpu.VMEM((N_DEV, cols), F32)] + _SmallGather.semaphores() * 2 + _Gather.scratch(shards),
        compiler_params=_params(None, 14 << 20),
    )(c_taps, w_ada_shard, b_shard, pos_col, freq, csel, ssel, *shards)


IN_VMEM = pl.BlockSpec(memory_space=pltpu.VMEM)
ANY = pl.BlockSpec(memory_space=pl.ANY)


class _Gather:
    def __init__(self, w_refs, out_refs, send_sems, recv_sems, own_sems, *bounce_refs):
        x, y, c = _position()
        q0 = 2 * x + y
        sibling = (x, y, 1 - c)
        self.ici, self.ici_in, self.fwd, self.fwd_in, self.own_in, self.own_out = [], [], [], [], [], []
        for k, (w_ref, out_ref) in enumerate(zip(w_refs, out_refs)):
            half = w_ref.shape[0] // 2
            self.own_in.append(pltpu.make_async_copy(w_ref, bounce_refs[k], own_sems.at[2 * k]))
            self.own_out.append(pltpu.make_async_copy(bounce_refs[k], out_ref.at[q0], own_sems.at[2 * k + 1]))

            def blk(q, e, out_ref=out_ref, half=half):
                return out_ref.at[q, pl.ds(pl.multiple_of(e * half, 16), half), :]

            def copy(src, dst, i, to):
                return pltpu.make_async_remote_copy(src_ref=src, dst_ref=dst, send_sem=send_sems.at[i], recv_sem=recv_sems.at[i],
                                                    device_id=to, device_id_type=MESH)

            src = w_ref.at[pl.ds(pl.multiple_of(c * half, 16), half), :]
            for j, (cx, cy, qj) in enumerate(_other_chips(x, y)):
                self.ici.append(copy(src, blk(q0, c), 6 * k + j, (cx, cy, c)))
                self.ici_in.append(copy(blk(qj, c), blk(qj, c), 6 * k + j, (cx, cy, c)))
                self.fwd.append(copy(blk(qj, c), blk(qj, c), 6 * k + 3 + j, sibling))
                self.fwd_in.append(copy(blk(qj, 1 - c), blk(qj, 1 - c), 6 * k + 3 + j, sibling))

    @staticmethod
    def out_shapes(shards):
        return [jax.ShapeDtypeStruct((N_CHIP,) + s.shape, s.dtype) for s in shards]

    @staticmethod
    def scratch(shards):
        n = len(shards)
        return ([pltpu.SemaphoreType.DMA((6 * n,)), pltpu.SemaphoreType.DMA((6 * n,)), pltpu.SemaphoreType.DMA((2 * n,))]
                + [pltpu.VMEM(s.shape, s.dtype) for s in shards])

    def start(self):
        for cp in self.ici + self.own_in:
            cp.start()

    def forward(self):
        for fetched, placed in zip(self.own_in, self.own_out):
            fetched.wait()
            placed.start()
        for arrived, onward in zip(self.ici_in, self.fwd):
            arrived.wait_recv()
            onward.start()

    def finish(self):
        for cp in self.fwd_in:
            cp.wait_recv()
        for cp in self.ici + self.fwd:
            cp.wait_send()
        for cp in self.own_out:
            cp.wait()


class _PairSwap:
    def __init__(self, g_refs, out_refs, send_sems, recv_sems):
        x, y, c = _position()
        self.copies = [
            pltpu.make_async_remote_copy(src_ref=g_ref.at[:, 1 - c], dst_ref=out_ref, send_sem=send_sems.at[k],
                                         recv_sem=recv_sems.at[k], device_id=(x, y, 1 - c), device_id_type=MESH)
            for k, (g_ref, out_ref) in enumerate(zip(g_refs, out_refs))]

    @staticmethod
    def out_shapes(grads):
        return [jax.ShapeDtypeStruct((N_CHIP,) + g.shape[2:], g.dtype) for g in grads]

    @staticmethod
    def semaphores(n):
        return [pltpu.SemaphoreType.DMA((n,)), pltpu.SemaphoreType.DMA((n,))]

    def start(self):
        for cp in self.copies:
            cp.start()

    def finish(self):
        for cp in self.copies:
            cp.wait_recv()
        for cp in self.copies:
            cp.wait_send()


def _pair_sum(g, a, c_idx, name):
    _, _, rh, cols = g.shape
    tr = rh
    for cand in (256, 128, 64, 32, 16):
        if rh % cand == 0 and rh > cand:
            tr = cand
            break

    def body(c_ref, g_ref, a_ref, o_ref):
        o_ref[...] = (g_ref[...] + a_ref[...]).astype(BF16)

    return pl.pallas_call(
        body, name=name,
        grid_spec=pltpu.PrefetchScalarGridSpec(
            num_scalar_prefetch=1, grid=(N_CHIP, rh // tr),
            in_specs=[pl.BlockSpec((None, None, tr, cols), lambda q, i, c_ref: (q, c_ref[0], i, 0)),
                      pl.BlockSpec((None, tr, cols), lambda q, i, c_ref: (q, i, 0))],
            out_specs=pl.BlockSpec((None, tr, cols), lambda q, i, c_ref: (q, i, 0))),
        out_shape=jax.ShapeDtypeStruct((N_CHIP, rh, cols), BF16),
        compiler_params=_params(("parallel", "parallel"), 10 * _nbytes((tr, cols), F32)),
    )(c_idx, g, a)


def _scatter_and_gather(parts, small, name):
    n = len(parts)

    def body(*refs):
        scatter = _Scatter(refs[:n], refs[n + 1:2 * n + 1], *refs[2 * n + 2:2 * n + 4])
        gather = _SmallGather(refs[n], refs[2 * n + 1], *refs[2 * n + 4:])
        scatter.start()
        gather.start()
        gather.finish()
        scatter.finish()

    return pl.pallas_call(
        body, name=name,
        out_shape=_Scatter.out_shapes(parts) + [jax.ShapeDtypeStruct((N_DEV,) + small.shape, F32)],
        in_specs=[ANY] * n + [IN_VMEM], out_specs=[ANY] * n + [IN_VMEM],
        scratch_shapes=_Scatter.semaphores(n) + _SmallGather.semaphores(),
        compiler_params=_params(None, 10 * _nbytes(small.shape, F32)),
    )(*parts, small)


class _Scatter:
    def __init__(self, p_refs, out_refs, send_sems, recv_sems):
        x, y, c = _position()
        self.copies = []
        for k, (p_ref, out_ref) in enumerate(zip(p_refs, out_refs)):
            for j, (cx, cy, qj) in enumerate(_other_chips(x, y)):
                self.copies.append(pltpu.make_async_remote_copy(
                    src_ref=p_ref.at[qj], dst_ref=out_ref.at[j], send_sem=send_sems.at[3 * k + j],
                    recv_sem=recv_sems.at[3 * k + j], device_id=(cx, cy, c), device_id_type=MESH))

    @staticmethod
    def out_shapes(parts):
        return [jax.ShapeDtypeStruct((3,) + p.shape[1:], p.dtype) for p in parts]

    @staticmethod
    def semaphores(n):
        return [pltpu.SemaphoreType.DMA((3 * n,)), pltpu.SemaphoreType.DMA((3 * n,))]

    def start(self):
        for cp in self.copies:
            cp.start()

    def finish(self):
        for cp in self.copies:
            cp.wait_recv()
        for cp in self.copies:
            cp.wait_send()


def _shard_sum(p, b, qc_idx, name):
    _, rh, cols = p.shape
    tr = rh
    for cand in (256, 128, 64, 32, 16):
        if rh % cand == 0 and rh > cand:
            tr = cand
            break

    def body(qc_ref, p_ref, b_ref, o_ref):
        acc = p_ref[...].astype(F32)
        for j in range(3):
            acc = acc + b_ref[j].astype(F32)
        o_ref[...] = acc

    return pl.pallas_call(
        body, name=name,
        grid_spec=pltpu.PrefetchScalarGridSpec(
            num_scalar_prefetch=1, grid=(rh // tr,),
            in_specs=[pl.BlockSpec((None, tr, cols), lambda i, qc_ref: (qc_ref[0], i, 0)),
                      pl.BlockSpec((3, tr, cols), lambda i, qc_ref: (0, i, 0))],
            out_specs=pl.BlockSpec((None, tr, cols), lambda i, qc_ref: (qc_ref[1], i, 0))),
        out_shape=jax.ShapeDtypeStruct((2, rh, cols), F32),
        compiler_params=_params(("parallel",), 8 * _nbytes((tr, cols), F32)),
    )(qc_idx, p, b)


def _join_halves(shards):
    n = len(shards)

    def body(*refs):
        out_refs = refs[n:2 * n]
        send_sems, recv_sems = refs[2 * n:]
        x, y, c = _position()
        cps = [pltpu.make_async_remote_copy(src_ref=out_refs[k].at[c], dst_ref=out_refs[k].at[c], send_sem=send_sems.at[k],
                                            recv_sem=recv_sems.at[k], device_id=(x, y, 1 - c), device_id_type=MESH)
               for k in range(n)]
        for cp in cps:
            cp.start()
        for k in range(n):
            arriving = out_refs[k].at[1 - c]
            pltpu.make_async_remote_copy(src_ref=arriving, dst_ref=arriving, send_sem=send_sems.at[k], recv_sem=recv_sems.at[k],
                                         device_id=(x, y, 1 - c), device_id_type=MESH).wait_recv()
        for cp in cps:
            cp.wait_send()

    return pl.pallas_call(
        body, name="rs_join",
        out_shape=[jax.ShapeDtypeStruct(a.shape, a.dtype) for a in shards],
        in_specs=[ANY] * n, out_specs=[ANY] * n, input_output_aliases={k: k for k in range(n)},
        scratch_shapes=[pltpu.SemaphoreType.DMA((n,)), pltpu.SemaphoreType.DMA((n,))],
    )(*shards)


def _cols_from_shards(g):
    q, r, cs = g.shape
    return jnp.transpose(g, (1, 0, 2)).reshape(r, q * cs)


def _cols_to_shards(w):
    r, cfull = w.shape
    return jnp.transpose(w.reshape(r, N_CHIP, cfull // N_CHIP), (1, 0, 2))


def _pad_w_in(w):
    z = lambda n: jnp.zeros((w.shape[0], n), w.dtype)
    q_lat, kv_lat, kpe = w[:, 0:512], w[:, 512:768], w[:, 768:800]
    qd, kd, vd = w[:, 800:1312], w[:, 1312:1824], w[:, 1824:2336]
    return jnp.concatenate([q_lat, qd, kd, vd, kv_lat, z(KPE_OFF), kpe, z(LANE - KPE_OFF - ROPE)], axis=1)


def _pad_w_qb(w):
    w3 = w.reshape(Q_LORA, HEADS, NOPE + ROPE)
    return jnp.pad(w3, ((0, 0), (0, 0), (0, LANE - NOPE - ROPE))).reshape(Q_LORA, HEADS * LANE)


def _unpad_w_qb(g):
    return g.reshape(Q_LORA, HEADS, LANE)[:, :, :NOPE + ROPE].reshape(Q_LORA, HEADS * (NOPE + ROPE))


def _pad_w_kvb(w):
    w3 = w.reshape(KV_LORA, HEADS, 2 * NOPE)
    kp = jnp.pad(w3[:, :, :NOPE], ((0, 0), (0, 0), (0, LANE - NOPE))).reshape(KV_LORA, HEADS * LANE)
    return jnp.concatenate([kp, w3[:, :, NOPE:].reshape(KV_LORA, DIL_W)], axis=1)


def _unpad_w_kvb(g):
    gk = g[:, :HEADS * LANE].reshape(KV_LORA, HEADS, LANE)[:, :, :NOPE]
    gv = g[:, HEADS * LANE:].reshape(KV_LORA, HEADS, NOPE)
    return jnp.concatenate([gk, gv], axis=2).reshape(KV_LORA, HEADS * 2 * NOPE)


def _head_gains(g_q_nope, g_q_pe, g_k_nope, g_k_pe, g_dq, g_dk):
    z = lambda n: jnp.zeros((1, n), F32)
    q1 = jnp.concatenate([g_q_nope, g_q_pe, z(LANE - NOPE - ROPE)], axis=1)
    k1 = jnp.concatenate([g_k_nope, z(LANE - NOPE)], axis=1)
    kpe = jnp.concatenate([z(KPE_OFF), g_k_pe, z(LANE - KPE_OFF - ROPE)], axis=1)
    return dict(q=jnp.tile(q1, (1, HEADS)), k=jnp.tile(k1, (1, HEADS)), kpe=kpe,
                dq=jnp.tile(g_dq, (1, HEADS)), dk=jnp.tile(g_dk, (1, HEADS)))


def kernel(x, c, positions, w_ada, b_ada, g_mix_norm, w_in, g_q_lat, w_q_b, g_kv_lat, w_kv_b, g_mla_q_nope, g_mla_q_pe, g_mla_k_nope, g_mla_k_pe, g_dil_q, g_dil_k, w_o, g_ffn_norm, w_up, w_conv, b_conv, w_down, loss_target, m_w_ada, m_b_ada, m_g_mix_norm, m_w_in, m_g_q_lat, m_w_q_b, m_g_kv_lat, m_w_kv_b, m_g_mla_q_nope, m_g_mla_q_pe, m_g_mla_k_nope, m_g_mla_k_pe, m_g_dil_q, m_g_dil_k, m_w_o, m_g_ffn_norm, m_w_up, m_w_conv, m_b_conv, m_w_down, v_w_ada, v_b_ada, v_g_mix_norm, v_w_in, v_g_q_lat, v_w_q_b, v_g_kv_lat, v_w_kv_b, v_g_mla_q_nope, v_g_mla_q_pe, v_g_mla_k_nope, v_g_mla_k_pe, v_g_dil_q, v_g_dil_k, v_w_o, v_g_ffn_norm, v_w_up, v_w_conv, v_b_conv, v_w_down):
    args = dict(locals())
    weights = {n: args[n][0] for n in ("w_ada", "w_in", "w_q_b", "w_kv_b", "w_o", "w_up", "w_conv", "w_down")}
    small_w = {n: args[n] for n in ("b_ada",) + tuple(n for n, _ in SMALL_WIDTHS)}
    mom_m = {n[2:]: (args[n][0] if args[n].ndim == 3 else args[n]) for n in args if n.startswith("m_")}
    mom_v = {n[2:]: (args[n][0] if args[n].ndim == 3 else args[n]) for n in args if n.startswith("v_")}

    xi, yi, ci = _position()
    q0 = 2 * xi + yi
    me = 4 * xi + 2 * yi + ci
    xs, tgt = x[0], loss_target[0]
    s = xs.shape[0]
    consts = _seg_consts()
    c_idx, qc_idx = jnp.reshape(ci, (1,)).astype(I32), jnp.stack([q0, ci]).astype(I32)

    def halves(g4):
        q, r, cc = g4.shape
        return g4.reshape(q, 2, r // 2, cc)

    own_first = [weights[n].astype(BF16) for n in ("w_in", "w_q_b", "w_kv_b")]
    own_later = [weights[n].astype(BF16) for n in ("w_o", "w_up", "w_down")]
    conv_cols = UP_W // N_CHIP
    ada_cols = w_ada.shape[2]
    b_shard = lax.dynamic_slice_in_dim(b_ada, q0 * ada_cols, ada_cols, axis=1)
    c_taps = jnp.concatenate([c, weights["w_conv"].reshape(1, 3 * conv_cols)], axis=1)
    c_taps_all, mod_all, tab, *gathered = _prologue(c_taps, weights["w_ada"], b_shard, positions.reshape(s, 1),
                                                    _rope_consts(), own_first)
    c_all = c_taps_all[:, 0, :D_MODEL]
    w_conv_f = c_taps_all[:, 0, D_MODEL:].reshape(N_CHIP, 2, 3, conv_cols)[:, 0]
    w_conv_f = jnp.transpose(w_conv_f, (1, 0, 2)).reshape(3, UP_W)
    mod_all = mod_all.reshape(N_CHIP, 2, N_DEV, ada_cols)
    mod = lax.dynamic_index_in_dim(lax.dynamic_index_in_dim(mod_all, ci, 1, False), me, 1, False)
    mod = mod.reshape(1, N_CHIP * ada_cols)
    sh1, sc1, g1, sh2, sc2, g2 = [mod[:, k * D_MODEL:(k + 1) * D_MODEL] for k in range(6)]
    w_in_f = _cols_from_shards(gathered[0])
    w_in_p = _pad_w_in(w_in_f)
    w_qb_p = _pad_w_qb(_cols_from_shards(gathered[1]))
    w_kvb_p = _pad_w_kvb(_cols_from_shards(gathered[2]))
    gains = _head_gains(g_mla_q_nope, g_mla_q_pe, g_mla_k_nope, g_mla_k_pe, g_dil_q, g_dil_k)

    h = _prenorm(xs, g_mix_norm, sc1, sh1, "prenorm")
    proj = _mm(h, w_in_p, "nn", F32, 512, P_COLS, "mm_in")
    ql, kvl = _latnorm(proj, g_q_lat, g_kv_lat)
    q_raw = _mm(ql, w_qb_p, "nn", F32, 512, HEADS * LANE, "mm_qb")
    kv_raw = _mm(kvl, w_kvb_p, "nn", F32, 512, HEADS * LANE + DIL_W, "mm_kvb")
    qm, km, vm, qd, kd, vd = _attn_prep(q_raw, kv_raw, proj, tab, gains, consts)
    scale_m, scale_d = (NOPE + ROPE) ** -0.5, DIL_DIM ** -0.5
    o_m, lse_m, got_up = _attn_fwd(qm, km, vm, True, scale_m, "attn_mla", gather=own_later[1:2])
    o_d, lse_d, got_o, got_down = _attn_fwd(qd, kd, vd, False, scale_d, "attn_dil", gather=[own_later[0], own_later[2]])
    gathered = [got_o, got_up, got_down]
    w_o_f = gathered[0].reshape(D_MODEL, D_MODEL)
    w_up_f = _cols_from_shards(gathered[1])
    w_down_f = gathered[2].reshape(D_FF, D_MODEL)
    mix_in = jnp.concatenate([o_m, o_d], axis=1)
    mix = _mm(mix_in, w_o_f, "nn", F32, 512, D_MODEL, "mm_o")
    x1, h2 = _resid_prenorm(xs, mix, g1, g_ffn_norm, sc2, sh2)
    up = _mm(h2, w_up_f, "nn", F32, 512, CONV_TILE, "mm_up")
    act = _conv_gate(up, w_conv_f, b_conv)
    ffn = _mm(act, w_down_f, "nn", F32, 256, D_MODEL, "mm_down")
    dy, dffn, dg2, loss_part = _final(x1, ffn, tgt, g2)

    da = _mm(dffn, w_down_f, "nt", F32, 512, CONV_TILE, "mm_down_dx")
    gw_down = _mm(act, dffn, "tn", F32, 256, D_MODEL, "mm_down_dw")
    dup_g, dup_v, dbg, dbv, dwg, dwv = _gate_bwd(up, da, w_conv_f, b_conv)
    dup = jnp.concatenate([dup_g, dup_v], axis=1)
    early_names = ("w_up", "w_down", "w_o")
    gw_up = _mm(h2, dup, "tn", F32, 512, CONV_TILE, "mm_up_dw", col_shards=True)
    early = [halves(gw_up), halves(gw_down.reshape(N_CHIP, D_FF // N_CHIP, D_MODEL))]
    dh2, *early_sib = _mm(dup, w_up_f, "nt", F32, 256, 512, "mm_up_dx", swap=early, b_outer=True)
    dx1, dmix, acc2 = _ffnnorm_bwd(dh2, x1, dy, mix, g_ffn_norm, sc2, g1)
    gw_o = _mm(mix_in, dmix, "tn", F32, 512, D_MODEL, "mm_o_dw")
    early.append(halves(gw_o.reshape(N_CHIP, D_MODEL // N_CHIP, D_MODEL)))
    dmix_in, sib_o = _mm(dmix, w_o_f, "nt", F32, 512, D_MODEL, "mm_o_dx", swap=early[2:])
    early_sib.append(sib_o)
    early_sums = [_pair_sum(g, a, c_idx, "pair_sum_" + n) for g, a, n in zip(early, early_sib, early_names)]
    dqm, dkm, dvm, *early_recv = _attn_bwd(qm, km, vm, o_m, dmix_in, 0, lse_m, True, scale_m, "attn_mla_bwd",
                                           scatter=early_sums[:1])
    dqd, dkd, dvd, *early_recv_d = _attn_bwd(qd, kd, vd, o_d, dmix_in, DIL_W // LANE, lse_d, False, scale_d,
                                             "attn_dil_bwd", scatter=early_sums[1:])
    early_recv = early_recv + early_recv_d
    dq_raw, dkv_raw, dkpe_b, dqd_b, dkd_b, dvd_b, dgains = _attn_prep_bwd(
        dqm, dkm, dvm, dqd, dkd, dvd, q_raw, kv_raw, proj, tab, gains, consts)
    dql = _mm(dq_raw, w_qb_p, "nt", F32, 512, Q_LORA, "mm_qb_dx")
    gw_qb = _unpad_w_qb(_mm(ql, dq_raw, "tn", F32, Q_LORA, HEADS * LANE, "mm_qb_dw"))
    dkvl = _mm(dkv_raw, w_kvb_p, "nt", F32, 512, KV_LORA, "mm_kvb_dx")
    gw_kvb = _unpad_w_kvb(_mm(kvl, dkv_raw, "tn", F32, KV_LORA, HEADS * LANE + DIL_W, "mm_kvb_dw"))
    dqlat_b, dkvlat_b, dglat = _latnorm_bwd(dql, dkvl, proj, g_q_lat, g_kv_lat)
    dproj = jnp.concatenate([dqlat_b, dkvlat_b, dkpe_b[:, KPE_OFF:KPE_OFF + ROPE], dqd_b, dkd_b, dvd_b], axis=1)
    gw_in = _mm(h, dproj, "tn", F32, 512, IN_COLS, "mm_in_dw")
    late_names = ("w_in", "w_q_b", "w_kv_b")
    late = [halves(_cols_to_shards(gw_in)), halves(_cols_to_shards(gw_qb)), halves(_cols_to_shards(gw_kvb))]
    dh, *late_sib = _mm(dproj, w_in_f, "nt", F32, 512, D_MODEL, "mm_in_dx", swap=late)
    grad_x, acc1 = _mixnorm_bwd(dh, xs, dx1, g_mix_norm, sc1)

    packed = _pack_small(acc1, acc2, dg2, dglat, dgains, dbg, dbv, dwg, dwv, loss_part)
    late_sums = [_pair_sum(g, a, c_idx, "pair_sum_" + n) for g, a, n in zip(late, late_sib, late_names)]
    *late_recv, gathered_small = _scatter_and_gather(late_sums, packed, "rs_scatter_late")

    grad_b_ada, *small_grads, gconv_full, loss_sum = _sum_unpack(gathered_small)
    grads = {"b_ada": grad_b_ada}
    grads.update({n: g for (n, _), g in zip(SMALL_WIDTHS, small_grads)})
    shard_cols = UP_W // N_CHIP
    grads["w_conv"] = lax.dynamic_slice_in_dim(gconv_full, q0 * shard_cols, shard_cols, axis=1)
    dmod_all = gathered_small[:, 0, :6 * D_MODEL]
    grads["w_ada"] = _ada_bwd(c_all, lax.dynamic_slice_in_dim(dmod_all, q0 * ada_cols, ada_cols, axis=1))

    big_names = late_names + early_names
    half_sums = [_shard_sum(p, b, qc_idx, "shard_sum_" + n)
                 for p, b, n in zip(late_sums + early_sums, list(late_recv) + list(early_recv), big_names)]
    for n, full in zip(big_names, _join_halves(half_sums)):
        grads[n] = full.reshape(2 * full.shape[1], full.shape[2])

    delta, new_m, new_v = {}, {}, {}
    for n in ("w_ada", "w_in", "w_q_b", "w_kv_b", "w_o", "w_up", "w_conv", "w_down"):
        operands = (weights[n], grads[n], mom_m[n], mom_v[n])
        flipped = n in ("w_in", "w_q_b")
        if flipped:
            operands = [jnp.swapaxes(a, 0, 1) for a in operands]
            grads[n] = jnp.swapaxes(operands[1], 0, 1)
        if n == "w_ada":
            operands = _in_hbm(*operands)
        delta[n], new_m[n], new_v[n] = _adamw(*operands, "adamw_" + n)
        if flipped:
            delta[n], new_m[n], new_v[n] = (jnp.swapaxes(a, 0, 1) for a in (delta[n], new_m[n], new_v[n]))
    vec_names = ("b_ada",) + tuple(n for n, _ in SMALL_WIDTHS)
    sd, sm, sv = _adamw_vectors(*[[d_[n] for n in vec_names] for d_ in (small_w, grads, mom_m, mom_v)])
    for k, n in enumerate(vec_names):
        delta[n], new_m[n], new_v[n] = sd[k], sm[k], sv[k]

    loss = loss_sum[0, 0]
    order = ("w_ada", "b_ada", "g_mix_norm", "w_in", "g_q_lat", "w_q_b", "g_kv_lat", "w_kv_b", "g_mla_q_nope", "g_mla_q_pe",
             "g_mla_k_nope", "g_mla_k_pe", "g_dil_q", "g_dil_k", "w_o", "g_ffn_norm", "w_up", "w_conv", "b_conv", "w_down")
    lead = lambda n, z: z[None] if n.startswith("w_") else z
    outs = [loss, grad_x[None]]
    for d_ in (grads, delta, new_m, new_v):
        outs += [lead(n, d_[n]) for n in order]
    return tuple(outs)
```

```python
import functools

import numpy as np
import jax
import jax.numpy as jnp
from jax import lax
from jax.experimental import pallas as pl
from jax.experimental.pallas import tpu as pltpu

F32 = jnp.float32
BF16 = jnp.bfloat16
I32 = jnp.int32

D_MODEL = 1024
HEADS = 8
NOPE = 64
ROPE = 32
Q_LORA = 512
KV_LORA = 256
DIL_DIM = 64
DIL_W = HEADS * DIL_DIM
D_FF = 2816
UP_W = 2 * D_FF
IN_COLS = Q_LORA + KV_LORA + ROPE + 3 * DIL_W
ROPE_THETA = 10000.0
EPS = 1e-6
NEG_INF = -1e30
N_DEV = 8
N_CHIP = 4

ADAM_LR = 0.001
ADAM_B1 = 0.9
ADAM_B2 = 0.999
ADAM_EPS = 1e-08
ADAM_WD = 0.01
ADAM_STEP = 10

LANE = 128
ROW_TILE = 256
ATT_TQ = 512
ATT_TK = 256
ATT_TK_BWD = 512
LOG2E = 1.4426950408889634
LN2 = 0.6931471805599453
VMEM_CAP = 56 * 1024 * 1024
VMEM_FLOOR = 32 * 1024 * 1024

P_QLAT, P_QD, P_KD, P_VD, P_KVLAT, P_KPE = 0, 512, 1024, 1536, 2048, 2304
P_COLS = 2432
KPE_OFF = 64

NN = (((1,), (0,)), ((), ()))
NT = (((1,), (1,)), ((), ()))
TN = (((0,), (0,)), ((), ()))
HIGHEST = lax.Precision.HIGHEST
MESH = pl.DeviceIdType.MESH


def _params(sem=None, est_bytes=0):
    limit = int(min(max(2 * est_bytes + (4 << 20), VMEM_FLOOR), VMEM_CAP))
    if sem is None:
        return pltpu.CompilerParams(vmem_limit_bytes=limit)
    return pltpu.CompilerParams(dimension_semantics=sem, vmem_limit_bytes=limit)


def _nbytes(shape, dtype):
    return int(np.prod(shape)) * jnp.dtype(dtype).itemsize


def _in_hbm(*xs):
    return [pltpu.with_memory_space_constraint(x, pltpu.HBM) for x in xs]


def _mm(a, b, dims, out_dtype, tm, tn, name, col_shards=False, swap=(), b_outer=False):
    def spec(block, index):
        if b_outer:
            return pl.BlockSpec(block, lambda g0, g1: index(g1, g0))
        return pl.BlockSpec(block, index)

    if dims == "nn":
        (m, k), (k2, n) = a.shape, b.shape
        a_spec = spec((tm, k), lambda i, j: (i, 0))
        b_spec = spec((k, tn), lambda i, j: (0, j))
        dn = NN
    elif dims == "nt":
        (m, k), (n, k2) = a.shape, b.shape
        a_spec = spec((tm, k), lambda i, j: (i, 0))
        b_spec = spec((tn, k), lambda i, j: (j, 0))
        dn = NT
    else:
        (k, m), (k2, n) = a.shape, b.shape
        a_spec = spec((k, tm), lambda i, j: (0, i))
        b_spec = spec((k, tn), lambda i, j: (0, j))
        dn = TN
    assert k == k2 and m % tm == 0 and n % tn == 0, (name, a.shape, b.shape, tm, tn)

    nw = len(swap)
    grid = (n // tn, m // tm) if b_outer else (m // tm, n // tn)

    def body(*refs):
        a_ref, b_ref, o_ref = refs[0], refs[1], refs[2 + nw]
        comm = (refs[2:2 + nw], refs[3 + nw:3 + 2 * nw]) + tuple(refs[3 + 2 * nw:])
        if nw:
            @pl.when((pl.program_id(0) == 0) & (pl.program_id(1) == 0))
            def _():
                _PairSwap(*comm).start()

        o_ref[...] = lax.dot_general(a_ref[...], b_ref[...], dn, preferred_element_type=F32).astype(o_ref.dtype)

        if nw:
            @pl.when((pl.program_id(0) == grid[0] - 1) & (pl.program_id(1) == grid[1] - 1))
            def _():
                _PairSwap(*comm).finish()

    est = _nbytes((tm, k), a.dtype) + _nbytes((tn, k), b.dtype) + _nbytes((tm, tn), F32) + _nbytes((tm, tn), out_dtype)
    if col_shards:
        out_spec = spec((None, tm, tn), lambda i, j: (j, i, 0))
        out_shape = jax.ShapeDtypeStruct((n // tn, m, tn), out_dtype)
    else:
        out_spec = spec((tm, tn), lambda i, j: (i, j))
        out_shape = jax.ShapeDtypeStruct((m, n), out_dtype)
    out = pl.pallas_call(
        body, name=name, grid=grid,
        in_specs=[a_spec, b_spec] + [ANY] * nw,
        out_specs=[out_spec] + [ANY] * nw,
        out_shape=[out_shape] + _PairSwap.out_shapes(swap),
        scratch_shapes=_PairSwap.semaphores(nw) if nw else [],
        compiler_params=_params(("arbitrary", "arbitrary") if nw else ("parallel", "parallel"), est),
    )(a, b, *swap)
    return out if nw else out[0]


def _seg_consts():
    seg_q = np.zeros((HEADS * LANE, LANE), np.float32)
    inv_q = np.zeros((1, LANE), np.float32)
    seg_k = np.zeros((HEADS * LANE, LANE), np.float32)
    inv_k = np.zeros((1, LANE), np.float32)
    seg_d = np.zeros((DIL_W, LANE), np.float32)
    inv_d = np.zeros((1, LANE), np.float32)
    for h in range(HEADS):
        seg_q[h * LANE:h * LANE + NOPE, 2 * h] = 1.0
        seg_q[h * LANE + NOPE:h * LANE + NOPE + ROPE, 2 * h + 1] = 1.0
        inv_q[0, 2 * h], inv_q[0, 2 * h + 1] = 1.0 / NOPE, 1.0 / ROPE
        seg_k[h * LANE:h * LANE + NOPE, h] = 1.0
        inv_k[0, h] = 1.0 / NOPE
        seg_d[h * DIL_DIM:(h + 1) * DIL_DIM, h] = 1.0
        inv_d[0, h] = 1.0 / DIL_DIM
    fold_q = np.tile(np.eye(LANE, dtype=np.float32), (HEADS, 1))
    fold_d = np.zeros((DIL_W, LANE), np.float32)
    fold_d[np.arange(DIL_W), np.arange(DIL_W) % DIL_DIM] = 1.0
    j = lambda v: jnp.asarray(v)
    b = lambda v: jnp.asarray(v, dtype=BF16)
    return dict(seg_q=b(seg_q), exp_q=b(seg_q.T.copy()), inv_q=j(inv_q), seg_k=b(seg_k), exp_k=b(seg_k.T.copy()),
                inv_k=j(inv_k), seg_d=b(seg_d), exp_d=b(seg_d.T.copy()), inv_d=j(inv_d), fold_q=j(fold_q), fold_d=j(fold_d))


def _rope_consts():
    inv_d = jnp.power(ROPE_THETA, -2.0 * jnp.arange(DIL_DIM // 2, dtype=F32) / DIL_DIM)
    inv_q = jnp.power(ROPE_THETA, -2.0 * jnp.arange(ROPE // 2, dtype=F32) / ROPE)
    lanes = np.arange(LANE)
    freq_d = inv_d[lanes % (DIL_DIM // 2)]
    in_pe = (lanes >= KPE_OFF) & (lanes < KPE_OFF + ROPE)
    freq_q = jnp.where(jnp.asarray(in_pe), inv_q[(lanes - KPE_OFF) % (ROPE // 2)], 0.0)
    sign_d = np.where(lanes % DIL_DIM < DIL_DIM // 2, -1.0, 1.0).astype(np.float32)
    sign_q = np.where(in_pe, np.where((lanes - KPE_OFF) < ROPE // 2, -1.0, 1.0), 0.0).astype(np.float32)
    zeros, ones = np.zeros(LANE, np.float32), np.ones(LANE, np.float32)
    freq = jnp.concatenate([freq_d, freq_d, freq_q, freq_q])[None, :]
    csel = jnp.asarray(np.concatenate([ones, zeros, ones, zeros]))[None, :]
    ssel = jnp.asarray(np.concatenate([zeros, sign_d, zeros, sign_q]))[None, :]
    return freq, csel, ssel


def _full(shape):
    return pl.BlockSpec(shape, lambda *_: (0,) * len(shape))


def _tile_lanes(x, n):
    return jnp.concatenate([x] * n, axis=1)


def _rms(x):
    return lax.rsqrt(jnp.mean(x * x, axis=-1, keepdims=True) + EPS)


def _prenorm(x, gain, scale, shift, name):
    s, d = x.shape

    def body(x_ref, g_ref, sc_ref, sh_ref, h_ref):
        xv = x_ref[...]
        h = (xv * _rms(xv)) * g_ref[...] * (1.0 + sc_ref[...]) + sh_ref[...]
        h_ref[...] = h.astype(BF16)

    row = pl.BlockSpec((ROW_TILE, d), lambda i: (i, 0))
    return pl.pallas_call(
        body, name=name, grid=(s // ROW_TILE,),
        in_specs=[row, _full((1, d)), _full((1, d)), _full((1, d))],
        out_specs=row, out_shape=jax.ShapeDtypeStruct((s, d), BF16),
        compiler_params=_params(("parallel",)),
    )(x, gain, scale, shift)


def _latnorm(proj, g_q, g_kv):
    s = proj.shape[0]

    def body(q_ref, kv_ref, gq_ref, gkv_ref, ql_ref, kvl_ref):
        q, kv = q_ref[...], kv_ref[...]
        ql_ref[...] = ((q * _rms(q)) * gq_ref[...]).astype(BF16)
        kvl_ref[...] = ((kv * _rms(kv)) * gkv_ref[...]).astype(BF16)

    return pl.pallas_call(
        body, name="latnorm", grid=(s // ROW_TILE,),
        in_specs=[pl.BlockSpec((ROW_TILE, Q_LORA), lambda i: (i, P_QLAT // Q_LORA)),
                  pl.BlockSpec((ROW_TILE, KV_LORA), lambda i: (i, P_KVLAT // KV_LORA)),
                  _full((1, Q_LORA)), _full((1, KV_LORA))],
        out_specs=[pl.BlockSpec((ROW_TILE, Q_LORA), lambda i: (i, 0)), pl.BlockSpec((ROW_TILE, KV_LORA), lambda i: (i, 0))],
        out_shape=[jax.ShapeDtypeStruct((s, Q_LORA), BF16), jax.ShapeDtypeStruct((s, KV_LORA), BF16)],
        compiler_params=_params(("parallel",)),
    )(proj, proj, g_q, g_kv)


def _dot01(v, mat01):
    hi = v.astype(BF16)
    lo = (v - hi.astype(F32)).astype(BF16)
    return jnp.dot(hi, mat01, preferred_element_type=F32) + jnp.dot(lo, mat01, preferred_element_type=F32)


def _seg_rinv(x, seg, exp, inv):
    r = lax.rsqrt(_dot01(x * x, seg) * inv + EPS)
    return _dot01(r, exp)


def _seg_mean(v, seg, exp, inv):
    return _dot01(_dot01(v, seg) * inv, exp)


def _swap_halves(x, half):
    n = x.shape[1]
    lane = lax.broadcasted_iota(I32, (1, n), 1)
    first = (lane & (2 * half - 1)) < half
    return jnp.where(first, pltpu.roll(x, n - half, 1), pltpu.roll(x, half, 1))


def _rope(x, cos, sin_signed, half):
    return x * cos + _swap_halves(x, half) * sin_signed


def _rope_bwd(dy, cos, sin_signed, half):
    return dy * cos + _swap_halves(dy * sin_signed, half)


def _pe_lane_mask(n):
    lane = lax.broadcasted_iota(I32, (1, n), 1) & (LANE - 1)
    return (lane >= KPE_OFF) & (lane < KPE_OFF + ROPE)


def _attn_prep(q_raw, kv_raw, proj, tab, gains, consts):
    s = q_raw.shape[0]
    hw = HEADS * LANE

    def body(q_ref, kv_ref, kpe_ref, qd_ref, kd_ref, vd_ref, tab_ref,
             gq_ref, gk_ref, gkpe_ref, gdq_ref, gdk_ref,
             segq_ref, expq_ref, invq_ref, segk_ref, expk_ref, invk_ref, segd_ref, expd_ref, invd_ref,
             qm_ref, km_ref, vm_ref, qdo_ref, kdo_ref, vdo_ref):
        tab_v = tab_ref[...]
        cos_d, sin_d = _tile_lanes(tab_v[:, 0:LANE], DIL_W // LANE), _tile_lanes(tab_v[:, LANE:2 * LANE], DIL_W // LANE)
        cos_q1, sin_q1 = tab_v[:, 2 * LANE:3 * LANE], tab_v[:, 3 * LANE:4 * LANE]
        cos_q, sin_q = _tile_lanes(cos_q1, HEADS), _tile_lanes(sin_q1, HEADS)

        q = q_ref[...]
        qn = q * _seg_rinv(q, segq_ref[...], expq_ref[...], invq_ref[...]) * gq_ref[...]
        qm_ref[...] = _rope(qn, cos_q, sin_q, ROPE // 2).astype(BF16)

        kv = kv_ref[...]
        kp = kv[:, :hw]
        kn = kp * _seg_rinv(kp, segk_ref[...], expk_ref[...], invk_ref[...]) * gk_ref[...]
        kpe = kpe_ref[...]
        r_pe = lax.rsqrt(jnp.sum(kpe * kpe, axis=-1, keepdims=True) * (1.0 / ROPE) + EPS)
        kpe_r = _rope(kpe * r_pe * gkpe_ref[...], cos_q1, sin_q1, ROPE // 2)
        km_ref[...] = (kn + _tile_lanes(kpe_r, HEADS)).astype(BF16)
        vm_ref[...] = kv[:, hw:].astype(BF16)

        qd = qd_ref[...]
        qdn = qd * _seg_rinv(qd, segd_ref[...], expd_ref[...], invd_ref[...]) * gdq_ref[...]
        qdo_ref[...] = _rope(qdn, cos_d, sin_d, DIL_DIM // 2).astype(BF16)
        kd = kd_ref[...]
        kdn = kd * _seg_rinv(kd, segd_ref[...], expd_ref[...], invd_ref[...]) * gdk_ref[...]
        kdo_ref[...] = _rope(kdn, cos_d, sin_d, DIL_DIM // 2).astype(BF16)
        vdo_ref[...] = vd_ref[...].astype(BF16)

    t = ROW_TILE
    row = lambda w, cb=0: pl.BlockSpec((t, w), lambda i: (i, cb))
    c = consts
    return pl.pallas_call(
        body, name="attn_prep", grid=(s // t,),
        in_specs=[row(hw), row(hw + DIL_W), row(LANE, P_KPE // LANE), row(DIL_W, P_QD // DIL_W), row(DIL_W, P_KD // DIL_W),
                  row(DIL_W, P_VD // DIL_W), row(4 * LANE),
                  _full((1, hw)), _full((1, hw)), _full((1, LANE)), _full((1, DIL_W)), _full((1, DIL_W)),
                  _full((hw, LANE)), _full((LANE, hw)), _full((1, LANE)), _full((hw, LANE)), _full((LANE, hw)), _full((1, LANE)),
                  _full((DIL_W, LANE)), _full((LANE, DIL_W)), _full((1, LANE))],
        out_specs=[row(hw), row(hw), row(DIL_W), row(DIL_W), row(DIL_W), row(DIL_W)],
        out_shape=[jax.ShapeDtypeStruct((s, hw), BF16), jax.ShapeDtypeStruct((s, hw), BF16)]
        + [jax.ShapeDtypeStruct((s, DIL_W), BF16)] * 4,
        compiler_params=_params(("parallel",), 24 << 20),
    )(*_in_hbm(q_raw, kv_raw, proj, proj, proj, proj), tab, gains["q"], gains["k"], gains["kpe"], gains["dq"], gains["dk"],
      c["seg_q"], c["exp_q"], c["inv_q"], c["seg_k"], c["exp_k"], c["inv_k"], c["seg_d"], c["exp_d"], c["inv_d"])


def _attn_prep_bwd(dqm, dkm, dvm, dqd, dkd, dvd, q_raw, kv_raw, proj, tab, gains, consts):
    s = q_raw.shape[0]
    hw = HEADS * LANE
    n_steps = s // ROW_TILE

    def body(dqm_ref, dkm_ref, dvm_ref, dqd_ref, dkd_ref, dvd_ref, q_ref, kv_ref, kpe_ref, qd_ref, kd_ref, tab_ref,
             gq_ref, gk_ref, gkpe_ref, gdq_ref, gdk_ref,
             segq_ref, expq_ref, invq_ref, segk_ref, expk_ref, invk_ref, segd_ref, expd_ref, invd_ref, foldq_ref, foldd_ref,
             dq_ref, dkv_ref, dkpe_ref, dqdo_ref, dkdo_ref, dvdo_ref, dg_ref, acc_ref):
        i = pl.program_id(0)

        @pl.when(i == 0)
        def _():
            acc_ref[...] = jnp.zeros_like(acc_ref)

        tab_v = tab_ref[...]
        cos_d, sin_d = _tile_lanes(tab_v[:, 0:LANE], DIL_W // LANE), _tile_lanes(tab_v[:, LANE:2 * LANE], DIL_W // LANE)
        cos_q1, sin_q1 = tab_v[:, 2 * LANE:3 * LANE], tab_v[:, 3 * LANE:4 * LANE]
        cos_q, sin_q = _tile_lanes(cos_q1, HEADS), _tile_lanes(sin_q1, HEADS)

        def norm_bwd(x, dyg, gain, seg, exp, inv):
            rinv = _seg_rinv(x, seg, exp, inv)
            xn = x * rinv
            dxn = dyg * gain
            dx = rinv * (dxn - xn * _seg_mean(dxn * xn, seg, exp, inv))
            return dx, jnp.sum(dyg * xn, axis=0, keepdims=True)

        dq, gq_l = norm_bwd(q_ref[...], _rope_bwd(dqm_ref[...], cos_q, sin_q, ROPE // 2), gq_ref[...],
                            segq_ref[...], expq_ref[...], invq_ref[...])
        dq_ref[...] = dq.astype(BF16)

        dkm = dkm_ref[...]
        kv = kv_ref[...]
        dkp, gk_l = norm_bwd(kv[:, :hw], dkm, gk_ref[...], segk_ref[...], expk_ref[...], invk_ref[...])
        dkv_ref[:, :hw] = dkp.astype(BF16)
        dkv_ref[:, hw:] = dvm_ref[...].astype(BF16)

        dkpe_r = dkm[:, 0:LANE]
        for h in range(1, HEADS):
            dkpe_r = dkpe_r + dkm[:, h * LANE:(h + 1) * LANE]
        dkpe_r = jnp.where(_pe_lane_mask(LANE), dkpe_r, 0.0)
        dyg = _rope_bwd(dkpe_r, cos_q1, sin_q1, ROPE // 2)
        kpe = kpe_ref[...]
        r_pe = lax.rsqrt(jnp.sum(kpe * kpe, axis=-1, keepdims=True) * (1.0 / ROPE) + EPS)
        xn = kpe * r_pe
        dxn = dyg * gkpe_ref[...]
        dkpe = r_pe * (dxn - xn * (jnp.sum(dxn * xn, axis=-1, keepdims=True) * (1.0 / ROPE)))
        dkpe_ref[...] = dkpe.astype(BF16)
        gkpe_l = jnp.sum(dyg * xn, axis=0, keepdims=True)

        dqd_v, gdq_l = norm_bwd(qd_ref[...], _rope_bwd(dqd_ref[...], cos_d, sin_d, DIL_DIM // 2), gdq_ref[...],
                                segd_ref[...], expd_ref[...], invd_ref[...])
        dqdo_ref[...] = dqd_v.astype(BF16)
        dkd_v, gdk_l = norm_bwd(kd_ref[...], _rope_bwd(dkd_ref[...], cos_d, sin_d, DIL_DIM // 2), gdk_ref[...],
                                segd_ref[...], expd_ref[...], invd_ref[...])
        dkdo_ref[...] = dkd_v.astype(BF16)
        dvdo_ref[...] = dvd_ref[...].astype(BF16)

        acc_ref[0:1, :] += gq_l
        acc_ref[1:2, :] += gk_l
        acc_ref[2:3, 0:LANE] += gkpe_l
        acc_ref[3:4, 0:DIL_W] += gdq_l
        acc_ref[4:5, 0:DIL_W] += gdk_l

        @pl.when(i == n_steps - 1)
        def _():
            acc = acc_ref[...]
            fq = jnp.dot(acc, foldq_ref[...], precision=HIGHEST, preferred_element_type=F32)
            fd = jnp.dot(acc[:, 0:DIL_W], foldd_ref[...], precision=HIGHEST, preferred_element_type=F32)
            rows = lax.broadcasted_iota(I32, (8, LANE), 0)
            base = jnp.where(rows < 2, fq, jnp.where(rows == 2, acc[:, 0:LANE], fd))
            at0 = pltpu.roll(base, LANE - KPE_OFF, 1)
            dg_ref[...] = jnp.where(rows == 5, pltpu.roll(at0, 5, 0), jnp.where(rows == 2, at0, base))

    t = ROW_TILE
    row = lambda w, cb=0: pl.BlockSpec((t, w), lambda i: (i, cb))
    c = consts
    return pl.pallas_call(
        body, name="attn_prep_bwd", grid=(n_steps,),
        in_specs=[row(hw), row(hw), row(DIL_W), row(DIL_W), row(DIL_W), row(DIL_W),
                  row(hw), row(hw + DIL_W), row(LANE, P_KPE // LANE), row(DIL_W, P_QD // DIL_W), row(DIL_W, P_KD // DIL_W),
                  row(4 * LANE),
                  _full((1, hw)), _full((1, hw)), _full((1, LANE)), _full((1, DIL_W)), _full((1, DIL_W)),
                  _full((hw, LANE)), _full((LANE, hw)), _full((1, LANE)), _full((hw, LANE)), _full((LANE, hw)), _full((1, LANE)),
                  _full((DIL_W, LANE)), _full((LANE, DIL_W)), _full((1, LANE)), _full((hw, LANE)), _full((DIL_W, LANE))],
        out_specs=[row(hw), row(hw + DIL_W), row(LANE), row(DIL_W), row(DIL_W), row(DIL_W), _full((8, LANE))],
        out_shape=[jax.ShapeDtypeStruct((s, hw), BF16), jax.ShapeDtypeStruct((s, hw + DIL_W), BF16),
                   jax.ShapeDtypeStruct((s, LANE), BF16)] + [jax.ShapeDtypeStruct((s, DIL_W), BF16)] * 3
        + [jax.ShapeDtypeStruct((8, LANE), F32)],
        scratch_shapes=[pltpu.VMEM((8, hw), F32)],
        compiler_params=_params(("arbitrary",), 28 << 20),
    )(*_in_hbm(dqm, dkm, dvm, dqd, dkd, dvd, q_raw, kv_raw, proj, proj, proj), tab,
      gains["q"], gains["k"], gains["kpe"], gains["dq"], gains["dk"],
      c["seg_q"], c["exp_q"], c["inv_q"], c["seg_k"], c["exp_k"], c["inv_k"], c["seg_d"], c["exp_d"], c["inv_d"],
      c["fold_q"], c["fold_d"])


def _latnorm_bwd(dql, dkvl, proj, g_q, g_kv):
    s = proj.shape[0]
    n_steps = s // ROW_TILE

    def body(dql_ref, dkvl_ref, q_ref, kv_ref, gq_ref, gkv_ref, dq_ref, dkv_ref, dg_ref):
        i = pl.program_id(0)

        @pl.when(i == 0)
        def _():
            dg_ref[...] = jnp.zeros_like(dg_ref)

        def one(x, dyg, gain):
            r = _rms(x)
            xn = x * r
            dxn = dyg * gain
            dx = r * (dxn - xn * jnp.mean(dxn * xn, axis=-1, keepdims=True))
            return dx, jnp.sum(dyg * xn, axis=0, keepdims=True)

        dq, gq_l = one(q_ref[...], dql_ref[...], gq_ref[...])
        dkv, gkv_l = one(kv_ref[...], dkvl_ref[...], gkv_ref[...])
        dq_ref[...] = dq.astype(BF16)
        dkv_ref[...] = dkv.astype(BF16)
        dg_ref[0:1, :] += gq_l
        dg_ref[1:2, 0:KV_LORA] += gkv_l

    t = ROW_TILE
    return pl.pallas_call(
        body, name="latnorm_bwd", grid=(n_steps,),
        in_specs=[pl.BlockSpec((t, Q_LORA), lambda i: (i, 0)), pl.BlockSpec((t, KV_LORA), lambda i: (i, 0)),
                  pl.BlockSpec((t, Q_LORA), lambda i: (i, P_QLAT // Q_LORA)),
                  pl.BlockSpec((t, KV_LORA), lambda i: (i, P_KVLAT // KV_LORA)),
                  _full((1, Q_LORA)), _full((1, KV_LORA))],
        out_specs=[pl.BlockSpec((t, Q_LORA), lambda i: (i, 0)), pl.BlockSpec((t, KV_LORA), lambda i: (i, 0)), _full((8, Q_LORA))],
        out_shape=[jax.ShapeDtypeStruct((s, Q_LORA), BF16), jax.ShapeDtypeStruct((s, KV_LORA), BF16),
                   jax.ShapeDtypeStruct((8, Q_LORA), F32)],
        compiler_params=_params(("arbitrary",)),
    )(dql, dkvl, proj, proj, g_q, g_kv)


def _resid_prenorm(x, mix, g1, gain, scale, shift):
    s, d = x.shape

    def body(x_ref, mix_ref, g1_ref, g_ref, sc_ref, sh_ref, x1_ref, h_ref):
        x1 = x_ref[...] + g1_ref[...] * mix_ref[...]
        x1_ref[...] = x1
        h_ref[...] = ((x1 * _rms(x1)) * g_ref[...] * (1.0 + sc_ref[...]) + sh_ref[...]).astype(BF16)

    row = pl.BlockSpec((ROW_TILE, d), lambda i: (i, 0))
    vec = _full((1, d))
    return pl.pallas_call(
        body, name="resid_prenorm", grid=(s // ROW_TILE,),
        in_specs=[row, row, vec, vec, vec, vec], out_specs=[row, row],
        out_shape=[jax.ShapeDtypeStruct((s, d), F32), jax.ShapeDtypeStruct((s, d), BF16)],
        compiler_params=_params(("parallel",)),
    )(x, mix, g1, gain, scale, shift)


CONV_TILE = 1408
HALO = 8


def _shift_down(x, halo, k):
    t = x.shape[0]
    row = lax.broadcasted_iota(I32, (t, 1), 0)
    out = pltpu.roll(x, k, 0)
    for r in range(k):
        out = jnp.where(row == r, halo[HALO - k + r:HALO - k + r + 1, :], out)
    return out


def _shift_up(x, halo, k):
    t = x.shape[0]
    row = lax.broadcasted_iota(I32, (t, 1), 0)
    out = pltpu.roll(x, t - k, 0)
    for r in range(k):
        out = jnp.where(row == t - k + r, halo[r:r + 1, :], out)
    return out


def _conv_fwd(x, halo, w, b):
    p1, p2 = _shift_down(x, halo, 1), _shift_down(x, halo, 2)
    u = b + p2 * w[0:1, :]
    u = u + p1 * w[1:2, :]
    u = u + x * w[2:3, :]
    return u, p1, p2


def _sigmoid(x):
    return 1.0 / (1.0 + jnp.exp(-x))


def _conv_gate(up, w_conv, b_conv):
    s = up.shape[0]
    t = ROW_TILE
    nj = D_FF // CONV_TILE
    hb = t // HALO

    def body(g_ref, v_ref, gh_ref, vh_ref, wg_ref, wv_ref, bg_ref, bv_ref, a_ref):
        live = (pl.program_id(0) > 0).astype(F32)
        ug, _, _ = _conv_fwd(g_ref[...], gh_ref[...] * live, wg_ref[...], bg_ref[...])
        uv, _, _ = _conv_fwd(v_ref[...], vh_ref[...] * live, wv_ref[...], bv_ref[...])
        a_ref[...] = (ug * _sigmoid(ug) * uv).astype(BF16)

    main = lambda off: pl.BlockSpec((t, CONV_TILE), lambda i, j: (i, j + off))
    halo = lambda off: pl.BlockSpec((HALO, CONV_TILE), lambda i, j: (jnp.maximum(i * hb - 1, 0), j + off))
    wsp = lambda off: pl.BlockSpec((3, CONV_TILE), lambda i, j: (0, j + off))
    bsp = lambda off: pl.BlockSpec((1, CONV_TILE), lambda i, j: (0, j + off))
    return pl.pallas_call(
        body, name="conv_gate", grid=(s // t, nj),
        in_specs=[main(0), main(nj), halo(0), halo(nj), wsp(0), wsp(nj), bsp(0), bsp(nj)],
        out_specs=pl.BlockSpec((t, CONV_TILE), lambda i, j: (i, j)),
        out_shape=jax.ShapeDtypeStruct((s, D_FF), BF16),
        compiler_params=_params(("parallel", "parallel"), 12 << 20),
    )(up, up, up, up, w_conv, w_conv, b_conv, b_conv)


def _gate_bwd(up, da, w_conv, b_conv):
    s = up.shape[0]
    t = ROW_TILE
    nj = D_FF // CONV_TILE
    hb = t // HALO
    n_i = s // t

    def body(g_ref, v_ref, gh_ref, vh_ref, gn_ref, vn_ref, da_ref, dan_ref, wg_ref, wv_ref, bg_ref, bv_ref,
             dupg_ref, dupv_ref, dbg_ref, dbv_ref, dwg_ref, dwv_ref):
        i = pl.program_id(1)

        @pl.when(i == 0)
        def _():
            for r in (dbg_ref, dbv_ref, dwg_ref, dwv_ref):
                r[...] = jnp.zeros_like(r)

        def d_gate(ug, uv, da_v):
            sg = _sigmoid(ug)
            return da_v * uv * (sg * (1.0 + ug * (1.0 - sg))), da_v * (ug * sg)

        live = (i > 0).astype(F32)
        xg, xv = g_ref[...], v_ref[...]
        wg, wv = wg_ref[...], wv_ref[...]
        ug, g1, g2 = _conv_fwd(xg, gh_ref[...] * live, wg, bg_ref[...])
        uv, v1, v2 = _conv_fwd(xv, vh_ref[...] * live, wv, bv_ref[...])
        dug, duv = d_gate(ug, uv, da_ref[...])

        more = (i < n_i - 1).astype(F32)
        ug_n, _, _ = _conv_fwd(gn_ref[...], xg[t - HALO:, :], wg, bg_ref[...])
        uv_n, _, _ = _conv_fwd(vn_ref[...], xv[t - HALO:, :], wv, bv_ref[...])
        dug_n, duv_n = d_gate(ug_n, uv_n, dan_ref[...] * more)

        def conv_t(du, du_n, w):
            return du * w[2:3, :] + _shift_up(du, du_n, 1) * w[1:2, :] + _shift_up(du, du_n, 2) * w[0:1, :]

        dupg_ref[...] = conv_t(dug, dug_n, wg).astype(BF16)
        dupv_ref[...] = conv_t(duv, duv_n, wv).astype(BF16)
        csum = lambda z: jnp.sum(z, axis=0, keepdims=True)
        dbg_ref[...] += csum(dug)
        dbv_ref[...] += csum(duv)
        dwg_ref[0:1, :] += csum(dug * g2)
        dwg_ref[1:2, :] += csum(dug * g1)
        dwg_ref[2:3, :] += csum(dug * xg)
        dwv_ref[0:1, :] += csum(duv * v2)
        dwv_ref[1:2, :] += csum(duv * v1)
        dwv_ref[2:3, :] += csum(duv * xv)

    last_halo = s // HALO - 1
    main = lambda off: pl.BlockSpec((t, CONV_TILE), lambda j, i: (i, j + off))
    halo = lambda off: pl.BlockSpec((HALO, CONV_TILE), lambda j, i: (jnp.maximum(i * hb - 1, 0), j + off))
    nxt = lambda off: pl.BlockSpec((HALO, CONV_TILE), lambda j, i: (jnp.minimum((i + 1) * hb, last_halo), j + off))
    wsp = lambda off: pl.BlockSpec((3, CONV_TILE), lambda j, i: (0, j + off))
    bsp = lambda off: pl.BlockSpec((1, CONV_TILE), lambda j, i: (0, j + off))
    outs = pl.pallas_call(
        body, name="gate_bwd", grid=(nj, n_i),
        in_specs=[main(0), main(nj), halo(0), halo(nj), nxt(0), nxt(nj), main(0), nxt(0),
                  wsp(0), wsp(nj), bsp(0), bsp(nj)],
        out_specs=[main(0), main(0),
                   pl.BlockSpec((1, CONV_TILE), lambda j, i: (0, j)), pl.BlockSpec((1, CONV_TILE), lambda j, i: (0, j)),
                   pl.BlockSpec((3, CONV_TILE), lambda j, i: (0, j)), pl.BlockSpec((3, CONV_TILE), lambda j, i: (0, j))],
        out_shape=[jax.ShapeDtypeStruct((s, D_FF), BF16), jax.ShapeDtypeStruct((s, D_FF), BF16),
                   jax.ShapeDtypeStruct((1, D_FF), F32), jax.ShapeDtypeStruct((1, D_FF), F32),
                   jax.ShapeDtypeStruct((3, D_FF), F32), jax.ShapeDtypeStruct((3, D_FF), F32)],
        compiler_params=_params(("parallel", "arbitrary"), 24 << 20),
    )(up, up, up, up, up, up, da, da, w_conv, w_conv, b_conv, b_conv)
    return outs


def _final(x1, ffn, tgt, g2):
    s, d = x1.shape
    n_steps = s // ROW_TILE

    def body(x1_ref, f_ref, t_ref, g2_ref, dy_ref, df_ref, dg2_ref, loss_ref, lacc_ref):
        i = pl.program_id(0)

        @pl.when(i == 0)
        def _():
            dg2_ref[...] = jnp.zeros_like(dg2_ref)
            lacc_ref[...] = jnp.zeros_like(lacc_ref)

        f = f_ref[...]
        e = x1_ref[...] + g2_ref[...] * f - t_ref[...]
        dy = e * (1.0 / d)
        dy_ref[...] = dy
        df_ref[...] = (dy * g2_ref[...]).astype(BF16)
        dg2_ref[...] += jnp.sum(dy * f, axis=0, keepdims=True)
        lacc_ref[...] += jnp.sum(e * e, axis=0, keepdims=True)

        @pl.when(i == n_steps - 1)
        def _():
            loss_ref[...] = jnp.sum(lacc_ref[...], axis=1, keepdims=True) * (0.5 / d)

    row = pl.BlockSpec((ROW_TILE, d), lambda i: (i, 0))
    return pl.pallas_call(
        body, name="final", grid=(n_steps,),
        in_specs=[row, row, row, _full((1, d))],
        out_specs=[row, row, _full((1, d)), _full((1, 1))],
        out_shape=[jax.ShapeDtypeStruct((s, d), F32), jax.ShapeDtypeStruct((s, d), BF16),
                   jax.ShapeDtypeStruct((1, d), F32), jax.ShapeDtypeStruct((1, 1), F32)],
        scratch_shapes=[pltpu.VMEM((1, d), F32)],
        compiler_params=_params(("arbitrary",)),
    )(x1, ffn, tgt, g2)


def _ffnnorm_bwd(dh2, x1, dy, mix, gain, scale, g1):
    s, d = x1.shape
    n_steps = s // ROW_TILE

    def body(dh_ref, x_ref, dy_ref, mix_ref, g_ref, sc_ref, g1_ref, dx_ref, dm_ref, acc_ref):
        i = pl.program_id(0)

        @pl.when(i == 0)
        def _():
            acc_ref[...] = jnp.zeros_like(acc_ref)

        dh, x = dh_ref[...], x_ref[...]
        r = _rms(x)
        xn = x * r
        dn = dh * (1.0 + sc_ref[...])
        dxn = dn * g_ref[...]
        dx = dy_ref[...] + r * (dxn - xn * jnp.mean(dxn * xn, axis=-1, keepdims=True))
        dx_ref[...] = dx
        dm_ref[...] = (dx * g1_ref[...]).astype(BF16)
        csum = lambda z: jnp.sum(z, axis=0, keepdims=True)
        acc_ref[0:1, :] += csum(dh)
        acc_ref[1:2, :] += csum(dh * (xn * g_ref[...]))
        acc_ref[2:3, :] += csum(dn * xn)
        acc_ref[3:4, :] += csum(dx * mix_ref[...])

    row = pl.BlockSpec((ROW_TILE, d), lambda i: (i, 0))
    vec = _full((1, d))
    return pl.pallas_call(
        body, name="ffnnorm_bwd", grid=(n_steps,),
        in_specs=[row, row, row, row, vec, vec, vec],
        out_specs=[row, row, _full((8, d))],
        out_shape=[jax.ShapeDtypeStruct((s, d), F32), jax.ShapeDtypeStruct((s, d), BF16), jax.ShapeDtypeStruct((8, d), F32)],
        compiler_params=_params(("arbitrary",)),
    )(dh2, x1, dy, mix, gain, scale, g1)


def _mixnorm_bwd(dh, x, dx1, gain, scale):
    s, d = x.shape
    n_steps = s // ROW_TILE

    def body(dh_ref, x_ref, dx1_ref, g_ref, sc_ref, gx_ref, acc_ref):
        i = pl.program_id(0)

        @pl.when(i == 0)
        def _():
            acc_ref[...] = jnp.zeros_like(acc_ref)

        dh, x = dh_ref[...], x_ref[...]
        r = _rms(x)
        xn = x * r
        dn = dh * (1.0 + sc_ref[...])
        dxn = dn * g_ref[...]
        gx_ref[...] = dx1_ref[...] + r * (dxn - xn * jnp.mean(dxn * xn, axis=-1, keepdims=True))
        csum = lambda z: jnp.sum(z, axis=0, keepdims=True)
        acc_ref[0:1, :] += csum(dh)
        acc_ref[1:2, :] += csum(dh * (xn * g_ref[...]))
        acc_ref[2:3, :] += csum(dn * xn)

    row = pl.BlockSpec((ROW_TILE, d), lambda i: (i, 0))
    vec = _full((1, d))
    return pl.pallas_call(
        body, name="mixnorm_bwd", grid=(n_steps,),
        in_specs=[row, row, row, vec, vec],
        out_specs=[row, _full((8, d))],
        out_shape=[jax.ShapeDtypeStruct((s, d), F32), jax.ShapeDtypeStruct((8, d), F32)],
        compiler_params=_params(("arbitrary",)),
    )(dh, x, dx1, gain, scale)


def _key_count(d, dilated):
    if not dilated:
        return jnp.where(d >= 0, 1.0, 0.0)
    one = lambda cond: jnp.where(cond, 1.0, 0.0)
    cnt = one(d <= 128) + one(((d & 3) == 0) & (d <= 512)) + one((d & 15) == 0)
    return jnp.where(d >= 0, cnt, 0.0)


def _block_kinds(mla):
    return (0, "diag", "none") if mla else (NEAR_REACH, "near", "far")


NEAR_REACH = 512


def _near_offsets(tk, tq):
    return (NEAR_REACH - (tk - tq)) // tk + 1


def _scores_t(ka, qa, scale, kind, rel_t, offset, near_tabs=None):
    return _mask_scores(lax.dot_general(ka, qa, NT, preferred_element_type=F32), scale, kind, rel_t, offset, near_tabs)


def _fill_near_tables(bias_ref, cnt_ref, rel_t):
    tk, tq = rel_t.shape
    for idx in range(_near_offsets(tk, tq)):
        cnt = _key_count(rel_t + (tk - tq) + idx * tk, True)
        cnt_ref[idx] = cnt
        bias_ref[idx] = jnp.where(cnt > 0.0, 0.0, NEG_INF)


def _mask_scores(products, scale, kind, rel_t, offset, near_tabs=None):
    st = products * (scale * LOG2E)
    cnt = None
    if kind == "diag":
        st = jnp.where(rel_t + offset >= 0, st, NEG_INF)
    elif kind == "far":
        st = jnp.where((rel_t & 15) == 0, st, NEG_INF)
    elif kind == "near":
        bias_ref, cnt_ref = near_tabs
        tk, tq = rel_t.shape
        idx = (offset - (tk - tq)) // tk
        st = st + bias_ref[idx]
        cnt = cnt_ref[idx]
    return st, cnt


def _attn_fwd(q, k, v, mla, scale, name, gather=()):
    s = q.shape[0]
    qw = 2 * LANE if mla else LANE
    tq, tk = ATT_TQ, ATT_TK
    reach, kind_near, kind_far = _block_kinds(mla)
    assert s % tq == 0 and tq % tk == 0 and reach % tk == 0 and reach in (0, NEAR_REACH)
    ng = len(gather)
    last_step = HEADS // 2 - 1

    def body(*refs):
        q_ref, k_ref, v_ref = refs[:3]
        o_ref, lse_ref = refs[3 + ng:5 + ng]
        vt_ref, st_ref = refs[5 + 2 * ng:7 + 2 * ng]
        near_tabs = None if mla else refs[7 + 2 * ng:9 + 2 * ng]
        n_tabs = 0 if mla else 2
        comm = (refs[3:3 + ng], refs[5 + ng:5 + 2 * ng]) + tuple(refs[7 + n_tabs + 2 * ng:])
        if ng:
            @pl.when(pl.program_id(0) == 0)
            def _():
                _Gather(*comm).start()

            @pl.when(pl.program_id(0) == last_step)
            def _():
                _Gather(*comm).forward()

        lane = lax.broadcasted_iota(I32, (1, LANE), 1)
        rel_t = lax.broadcasted_iota(I32, (tk, tq), 1) - lax.broadcasted_iota(I32, (tk, tq), 0)
        if not mla:
            _fill_near_tables(*near_tabs, rel_t)

        def transpose_v(j, carry):
            c0 = pl.multiple_of(j * tk, tk)
            vt_ref[:, pl.ds(c0, tk)] = v_ref[pl.ds(c0, tk), :].astype(F32).T.astype(BF16)
            return carry

        lax.fori_loop(0, s // tk, transpose_v, 0)

        def q_block(qi, carry):
            r0 = pl.multiple_of(qi * tq, tq)
            kcols = [slice(a * LANE, (a + 1) * LANE) if mla else slice(0, LANE) for a in range(2)]
            qas = [q_ref[pl.ds(r0, tq), kcols[a]] for a in range(2)]
            if not mla:
                qas = [jnp.where(lane < DIL_DIM, qas[0], jnp.zeros_like(qas[0])),
                       jnp.where(lane >= DIL_DIM, qas[1], jnp.zeros_like(qas[1]))]

            n_k = (r0 + tq) // tk

            def products(kj):
                c0 = pl.multiple_of(kj * tk, tk)
                return [lax.dot_general(k_ref[pl.ds(c0, tk), kcols[a]], qas[a], NT, preferred_element_type=F32)
                        for a in range(2)]

            for a, pr in enumerate(products(0)):
                st_ref[0, a] = pr

            def k_block(kj, c, kind):
                c0 = pl.multiple_of(kj * tk, tk)
                slot = kj & 1
                ahead = products(jnp.minimum(kj + 1, n_k - 1))
                out = []
                for a in range(2):
                    m, l, acc = c[a]
                    st, cnt = _mask_scores(st_ref[slot, a], scale, kind, rel_t, r0 - c0, near_tabs)
                    st_ref[1 - slot, a] = ahead[a]
                    m_new = jnp.maximum(m, jnp.max(st, axis=0, keepdims=True))
                    alpha = jnp.exp2(m - m_new)
                    p = jnp.exp2(st - m_new)
                    if cnt is not None:
                        p = p * cnt
                    l = alpha * l + jnp.sum(p, axis=0, keepdims=True)
                    vt = vt_ref[a * DIL_DIM:(a + 1) * DIL_DIM, pl.ds(c0, tk)]
                    acc = alpha * acc + jnp.dot(vt, p.astype(BF16), preferred_element_type=F32)
                    out.append((m_new, l, acc))
                return tuple(out)

            one = (jnp.full((1, tq), NEG_INF, F32), jnp.zeros((1, tq), F32), jnp.zeros((DIL_DIM, tq), F32))
            first_near = jnp.maximum((r0 - reach) // tk, 0)
            c = lax.fori_loop(0, first_near, functools.partial(k_block, kind=kind_far), (one, one))
            res = lax.fori_loop(first_near, (r0 + tq) // tk, functools.partial(k_block, kind=kind_near), c)
            o_t = jnp.concatenate([res[a][2] / res[a][1] for a in range(2)], axis=0)
            o_ref[pl.ds(r0, tq), :] = o_t.T.astype(BF16)
            for a in range(2):
                lse_ref[a, :, pl.ds(r0, tq)] = res[a][0] * LN2 + jnp.log(res[a][1])
            return carry

        lax.fori_loop(0, s // tq, q_block, 0)

        if ng:
            @pl.when(pl.program_id(0) == last_step)
            def _():
                _Gather(*comm).finish()

    return pl.pallas_call(
        body, name=name, grid=(HEADS // 2,),
        in_specs=[pl.BlockSpec((s, qw), lambda h: (0, h)), pl.BlockSpec((s, qw), lambda h: (0, h)),
                  pl.BlockSpec((s, LANE), lambda h: (0, h))] + [ANY] * ng,
        out_specs=[pl.BlockSpec((s, LANE), lambda h: (0, h)), pl.BlockSpec((2, 1, s), lambda h: (h, 0, 0))] + [ANY] * ng,
        out_shape=[jax.ShapeDtypeStruct((s, DIL_W), BF16), jax.ShapeDtypeStruct((HEADS, 1, s), F32)] + _Gather.out_shapes(gather),
        scratch_shapes=[pltpu.VMEM((LANE, s), BF16), pltpu.VMEM((2, 2, tk, tq), F32)]
        + ([] if mla else [pltpu.VMEM((_near_offsets(tk, tq), tk, tq), F32)] * 2) + (_Gather.scratch(gather) if ng else []),
        compiler_params=_params(("arbitrary",) if ng else ("parallel",), 12 << 20),
    )(*_in_hbm(q, k, v), *gather)


def _attn_bwd(q, k, v, o, do, do_block0, lse, mla, scale, name, scatter=()):
    s = q.shape[0]
    qw = 2 * LANE if mla else LANE
    tq, tk = ATT_TQ, ATT_TK_BWD
    nq = s // tq
    reach, kind_near, kind_far = _block_kinds(mla)
    assert s % tq == 0 and s % tk == 0
    ns = len(scatter)
    last_step = HEADS // 2 - 1

    def body(*refs):
        q_ref, k_ref, v_ref, o_ref, do_ref, lse_ref = refs[:6]
        dq_ref, dk_ref, dv_ref = refs[6 + ns:9 + ns]
        kt_ref, dot_ref, dob_ref, dqt_ref, delta_ref, lse2_ref = refs[9 + 2 * ns:15 + 2 * ns]
        near_tabs = None if mla else refs[15 + 2 * ns:17 + 2 * ns]
        n_tabs = 0 if mla else 2
        comm = (refs[6:6 + ns], refs[9 + ns:9 + 2 * ns]) + tuple(refs[15 + n_tabs + 2 * ns:])
        if ns:
            @pl.when(pl.program_id(0) == 0)
            def _():
                _Scatter(*comm).start()

        lane = lax.broadcasted_iota(I32, (1, LANE), 1)
        row = lax.broadcasted_iota(I32, (LANE, 1), 0)
        rel_t = lax.broadcasted_iota(I32, (tk, tq), 1) - lax.broadcasted_iota(I32, (tk, tq), 0)
        if not mla:
            _fill_near_tables(*near_tabs, rel_t)

        def prepare(j, carry):
            c0 = pl.multiple_of(j * tk, tk)
            do_blk = do_ref[pl.ds(c0, tk), :]
            dob_ref[pl.ds(c0, tk), :] = do_blk.astype(BF16)
            do_t = do_blk.T
            dot_ref[:, pl.ds(c0, tk)] = do_t.astype(BF16)
            prod = do_t * o_ref[pl.ds(c0, tk), :].astype(F32).T
            delta_ref[0, :, pl.ds(c0, tk)] = jnp.sum(prod[0:DIL_DIM], axis=0, keepdims=True)
            delta_ref[1, :, pl.ds(c0, tk)] = jnp.sum(prod[DIL_DIM:LANE], axis=0, keepdims=True)
            for w in range(qw // LANE):
                kt_ref[w * LANE:(w + 1) * LANE, pl.ds(c0, tk)] = (
                    k_ref[pl.ds(c0, tk), w * LANE:(w + 1) * LANE].astype(F32).T.astype(BF16))
            return carry

        lax.fori_loop(0, s // tk, prepare, 0)
        dqt_ref[...] = jnp.zeros_like(dqt_ref)
        lse2_ref[...] = lse_ref[...] * LOG2E

        sels = [lane < DIL_DIM, lane >= DIL_DIM]
        rsels = [row < DIL_DIM, row >= DIL_DIM]
        cols = [slice(a * LANE, (a + 1) * LANE) if mla else slice(0, LANE) for a in range(2)]

        def k_block(kj, carry):
            c0 = pl.multiple_of(kj * tk, tk)
            kas = [k_ref[pl.ds(c0, tk), cols[a]] for a in range(2)]
            kts = [kt_ref[cols[a], pl.ds(c0, tk)] for a in range(2)]
            if not mla:
                kas = [jnp.where(sels[a], kas[a], jnp.zeros_like(kas[a])) for a in range(2)]
                kts = [jnp.where(rsels[a], kts[a], jnp.zeros_like(kts[a])) for a in range(2)]
            vb = v_ref[pl.ds(c0, tk), :]
            vbs = [jnp.where(sels[a], vb, jnp.zeros_like(vb)) for a in range(2)]

            first = c0 // tq

            def q_block(qi, c, kind):
                r0 = pl.multiple_of(qi * tq, tq)
                out, dq_parts = [], []
                for a in range(2):
                    dk_acc, dv_acc = c[a]
                    qa = q_ref[pl.ds(r0, tq), cols[a]]
                    st, cnt = _scores_t(kas[a], qa, scale, kind, rel_t, r0 - c0, near_tabs)
                    p = jnp.exp2(st - lse2_ref[a, :, pl.ds(r0, tq)])
                    if cnt is not None:
                        p = p * cnt
                    dp = jnp.dot(vbs[a], dot_ref[:, pl.ds(r0, tq)], preferred_element_type=F32)
                    ds = (p * (dp - delta_ref[a, :, pl.ds(r0, tq)]) * scale).astype(BF16)
                    dv_acc = dv_acc + jnp.dot(p.astype(BF16), dob_ref[pl.ds(r0, tq), :], preferred_element_type=F32)
                    dk_acc = dk_acc + jnp.dot(ds, qa, preferred_element_type=F32)
                    dq_parts.append(jnp.dot(kts[a], ds, preferred_element_type=F32))
                    out.append((dk_acc, dv_acc))
                if mla:
                    for a in range(2):
                        dqt_ref[cols[a], pl.ds(r0, tq)] += dq_parts[a]
                else:
                    dqt_ref[:, pl.ds(r0, tq)] += dq_parts[0] + dq_parts[1]
                return tuple(out)

            zero = jnp.zeros((tk, LANE), F32)
            last_near = jnp.minimum((c0 + tk - 1 + reach) // tq + 1, nq)
            c = lax.fori_loop(first, last_near, functools.partial(q_block, kind=kind_near), ((zero, zero), (zero, zero)))
            (dk0, dv0), (dk1, dv1) = lax.fori_loop(last_near, nq, functools.partial(q_block, kind=kind_far), c)
            if mla:
                dk_ref[pl.ds(c0, tk), cols[0]] = dk0
                dk_ref[pl.ds(c0, tk), cols[1]] = dk1
            else:
                dk_ref[pl.ds(c0, tk), :] = jnp.where(sels[0], dk0, dk1)
            dv_ref[pl.ds(c0, tk), :] = jnp.where(sels[0], dv0, dv1)
            return carry

        lax.fori_loop(0, s // tk, k_block, 0)

        def write_dq(j, carry):
            c0 = pl.multiple_of(j * tk, tk)
            for w in range(qw // LANE):
                dq_ref[pl.ds(c0, tk), w * LANE:(w + 1) * LANE] = dqt_ref[w * LANE:(w + 1) * LANE, pl.ds(c0, tk)].T
            return carry

        lax.fori_loop(0, s // tk, write_dq, 0)

        if ns:
            @pl.when(pl.program_id(0) == last_step)
            def _():
                _Scatter(*comm).finish()

    b0 = do_block0
    return pl.pallas_call(
        body, name=name, grid=(HEADS // 2,),
        in_specs=[pl.BlockSpec((s, qw), lambda h: (0, h)), pl.BlockSpec((s, qw), lambda h: (0, h)),
                  pl.BlockSpec((s, LANE), lambda h: (0, h)), pl.BlockSpec((s, LANE), lambda h: (0, h)),
                  pl.BlockSpec((s, LANE), lambda h: (0, h + b0)), pl.BlockSpec((2, 1, s), lambda h: (h, 0, 0))] + [ANY] * ns,
        out_specs=[pl.BlockSpec((s, qw), lambda h: (0, h)), pl.BlockSpec((s, qw), lambda h: (0, h)),
                   pl.BlockSpec((s, LANE), lambda h: (0, h))] + [ANY] * ns,
        out_shape=[jax.ShapeDtypeStruct(q.shape, F32), jax.ShapeDtypeStruct(k.shape, F32), jax.ShapeDtypeStruct((s, DIL_W), F32)]
        + _Scatter.out_shapes(scatter),
        scratch_shapes=[pltpu.VMEM((qw, s), BF16), pltpu.VMEM((LANE, s), BF16), pltpu.VMEM((s, LANE), BF16),
                        pltpu.VMEM((qw, s), F32), pltpu.VMEM((2, 1, s), F32), pltpu.VMEM((2, 1, s), F32)]
        + ([] if mla else [pltpu.VMEM((_near_offsets(tk, tq), tk, tq), F32)] * 2) + (_Scatter.semaphores(ns) if ns else []),
        compiler_params=_params(("arbitrary",) if ns else ("parallel",), 24 << 20),
    )(*_in_hbm(q, k, v, o, do, lse), *scatter)


def _ada_bwd(c_all, dmod_shard):
    n, d = c_all.shape
    cols = dmod_shard.shape[1]

    def body(c_ref, g_ref, o_ref):
        cv = c_ref[...]
        o_ref[...] = lax.dot_general(cv * _sigmoid(cv), g_ref[...], TN, precision=HIGHEST, preferred_element_type=F32)

    return pl.pallas_call(
        body, name="ada_bwd", out_shape=jax.ShapeDtypeStruct((d, cols), F32),
        compiler_params=_params(None, 16 << 20),
    )(c_all, dmod_shard)


SMALL_WIDTHS = (("g_mix_norm", D_MODEL), ("g_q_lat", Q_LORA), ("g_kv_lat", KV_LORA), ("g_mla_q_nope", NOPE),
                ("g_mla_q_pe", ROPE), ("g_mla_k_nope", NOPE), ("g_mla_k_pe", ROPE), ("g_dil_q", DIL_DIM),
                ("g_dil_k", DIL_DIM), ("g_ffn_norm", D_MODEL), ("b_conv", UP_W))


def _small_layout():
    pieces = (("dmod", 6 * D_MODEL),) + SMALL_WIDTHS + tuple(("w_conv%d" % k, UP_W) for k in range(3)) + (("loss", 1),)
    layout, off = {}, 0
    for name, width in pieces:
        layout[name] = (width, off)
        off += -(-width // LANE) * LANE
    return layout, off


def _pack_small(acc1, acc2, dg2, dglat, dgains, dbg, dbv, dwg, dwv, loss_part):
    layout, total = _small_layout()

    def body(a1, a2, g2, gl, gg, bg, bv, wg, wv, ls, o_ref):
        o_ref[...] = jnp.zeros_like(o_ref)

        def put(name, src, shift=0):
            start = layout[name][1] + shift
            o_ref[:, start:start + src.shape[1]] = src

        for k, src in enumerate((a1[0:1, :], a1[1:2, :], a2[3:4, :], a2[0:1, :], a2[1:2, :], g2[...])):
            put("dmod", src, k * D_MODEL)
        put("g_mix_norm", a1[2:3, :])
        put("g_q_lat", gl[0:1, :])
        put("g_kv_lat", gl[1:2, 0:KV_LORA])
        put("g_mla_q_nope", gg[0:1, 0:NOPE])
        put("g_mla_q_pe", gg[5:6, 0:ROPE])
        put("g_mla_k_nope", gg[1:2, 0:NOPE])
        put("g_mla_k_pe", gg[2:3, 0:ROPE])
        put("g_dil_q", gg[3:4, 0:DIL_DIM])
        put("g_dil_k", gg[4:5, 0:DIL_DIM])
        put("g_ffn_norm", a2[2:3, :])
        put("b_conv", bg[...])
        put("b_conv", bv[...], D_FF)
        for k in range(3):
            put("w_conv%d" % k, wg[k:k + 1, :])
            put("w_conv%d" % k, wv[k:k + 1, :], D_FF)
        put("loss", ls[...])

    ins = (acc1, acc2, dg2, dglat, dgains, dbg, dbv, dwg, dwv, loss_part)
    return pl.pallas_call(
        body, name="pack_small", grid=(1,), in_specs=[_full(a.shape) for a in ins], out_specs=_full((1, total)),
        out_shape=jax.ShapeDtypeStruct((1, total), F32),
        compiler_params=_params(("arbitrary",), 2 << 20),
    )(*_in_hbm(*ins))


def _sum_unpack(g):
    n_dev, _, total = g.shape
    layout, _ = _small_layout()

    def body(g_ref, *refs):
        o_refs, s_ref = refs[:-1], refs[-1]
        acc = g_ref[0]
        for k in range(1, n_dev):
            acc = acc + g_ref[k]
        s_ref[...] = acc
        take = lambda name: s_ref[:, layout[name][1]:layout[name][1] + layout[name][0]]
        o_refs[0][...] = take("dmod")
        for i, (name, _) in enumerate(SMALL_WIDTHS):
            o_refs[1 + i][...] = take(name)
        for k in range(3):
            o_refs[-2][k:k + 1, :] = take("w_conv%d" % k)
        o_refs[-1][...] = take("loss")

    shapes = [(1, 6 * D_MODEL)] + [(1, w) for _, w in SMALL_WIDTHS] + [(3, UP_W), (1, 1)]
    return pl.pallas_call(
        body, name="sum_unpack", out_shape=[jax.ShapeDtypeStruct(sh, F32) for sh in shapes],
        scratch_shapes=[pltpu.VMEM((1, total), F32)],
        compiler_params=_params(None, 4 << 20),
    )(g)


def _adamw_math(w, g, m, v):
    mn = ADAM_B1 * m + (1.0 - ADAM_B1) * g
    vn = ADAM_B2 * v + (1.0 - ADAM_B2) * (g * g)
    m_hat = mn / (1.0 - ADAM_B1 ** ADAM_STEP)
    v_hat = vn / (1.0 - ADAM_B2 ** ADAM_STEP)
    return -ADAM_LR * (m_hat / (jnp.sqrt(v_hat) + ADAM_EPS) + ADAM_WD * w), mn, vn


def _adamw_vectors(ws, gs, ms, vs):
    k = len(ws)

    def body(*refs):
        for i in range(k):
            d, mn, vn = _adamw_math(refs[i][...], refs[k + i][...], refs[2 * k + i][...], refs[3 * k + i][...])
            refs[4 * k + i][...] = d
            refs[5 * k + i][...] = mn
            refs[6 * k + i][...] = vn

    blocks = [_full(w.shape) for w in ws]
    outs = pl.pallas_call(
        body, name="adamw_vectors", grid=(1,), in_specs=blocks * 4, out_specs=blocks * 3,
        out_shape=[jax.ShapeDtypeStruct(w.shape, F32) for w in ws] * 3,
        compiler_params=_params(("arbitrary",), 2 << 20),
    )(*_in_hbm(*ws, *gs, *ms, *vs))
    return outs[:k], outs[k:2 * k], outs[2 * k:]


def _adamw(w, g, m, v, name):
    r, c = w.shape
    tr = r
    for cand in (256, 128, 64, 32, 16):
        if r % cand == 0 and r > cand:
            tr = cand
            break

    def body(w_ref, g_ref, m_ref, v_ref, d_ref, mo_ref, vo_ref):
        d_ref[...], mo_ref[...], vo_ref[...] = _adamw_math(w_ref[...], g_ref[...], m_ref[...], v_ref[...])

    blk = pl.BlockSpec((tr, c), lambda i: (i, 0))
    return pl.pallas_call(
        body, name=name, grid=(r // tr,), in_specs=[blk] * 4, out_specs=[blk] * 3,
        out_shape=[jax.ShapeDtypeStruct((r, c), F32)] * 3,
        compiler_params=_params(("parallel",), 7 * _nbytes((tr, c), F32)),
    )(w, g, m, v)


def _position():
    return lax.axis_index("x"), lax.axis_index("y"), lax.axis_index("c")


def _other_chips(x, y):
    return [(1 - x, y, 2 * (1 - x) + y), (x, 1 - y, 2 * x + (1 - y)), (1 - x, 1 - y, 2 * (1 - x) + (1 - y))]


class _SmallGather:
    def __init__(self, v_ref, out_ref, send_sems, recv_sems, local_sem):
        x, y, c = _position()
        me = 4 * x + 2 * y + c
        self.local = pltpu.make_async_copy(v_ref, out_ref.at[me], local_sem)
        self.sends, self.arrivals = [], []
        for k in range(N_DEV - 1):
            fx, fy, fc = ((k + 1) >> 2) & 1, ((k + 1) >> 1) & 1, (k + 1) & 1
            px, py, pc = (1 - x if fx else x), (1 - y if fy else y), (1 - c if fc else c)

            def copy(dst, k=k, peer=(px, py, pc)):
                return pltpu.make_async_remote_copy(src_ref=v_ref, dst_ref=dst, send_sem=send_sems.at[k],
                                                    recv_sem=recv_sems.at[k], device_id=peer, device_id_type=MESH)

            self.sends.append(copy(out_ref.at[me]))
            self.arrivals.append(copy(out_ref.at[4 * px + 2 * py + pc]))

    @staticmethod
    def semaphores():
        return [pltpu.SemaphoreType.DMA((N_DEV - 1,)), pltpu.SemaphoreType.DMA((N_DEV - 1,)), pltpu.SemaphoreType.DMA]

    def start(self):
        self.local.start()
        for cp in self.sends:
            cp.start()

    def finish(self):
        for cp in self.arrivals:
            cp.wait_recv()
        for cp in self.sends:
            cp.wait_send()
        self.local.wait()


def _prologue(c_taps, w_ada_shard, b_shard, pos_col, rope_consts, shards):
    n = len(shards)
    s = pos_col.shape[0]
    cols = w_ada_shard.shape[1]
    freq, csel, ssel = rope_consts

    def body(*refs):
        ct_ref, w_ref, b_ref, p_ref, f_ref, cs_ref, ss_ref = refs[:7]
        sh_refs = refs[7:7 + n]
        ct_all_ref, mod_all_ref, tab_ref = refs[7 + n:10 + n]
        g_refs = refs[10 + n:10 + 2 * n]
        mod_blk_ref = refs[10 + 2 * n]
        sems = refs[11 + 2 * n:]
        weights = _Gather(sh_refs, g_refs, *sems[6:])
        weights.start()
        first = _SmallGather(ct_ref, ct_all_ref, *sems[0:3])
        first.start()
        first.finish()
        cv = ct_all_ref[:, 0, 0:D_MODEL]
        sc = (cv * _sigmoid(cv)).astype(BF16)
        mod_blk_ref[...] = jnp.dot(sc, w_ref[...].astype(BF16), preferred_element_type=F32) + b_ref[...]
        second = _SmallGather(mod_blk_ref, mod_all_ref, *sems[3:6])
        second.start()

        def table_rows(i, carry):
            r0 = pl.multiple_of(i * ROW_TILE, ROW_TILE)
            ang = p_ref[pl.ds(r0, ROW_TILE), :].astype(F32) * f_ref[...]
            tab_ref[pl.ds(r0, ROW_TILE), :] = cs_ref[...] * jnp.cos(ang) + ss_ref[...] * jnp.sin(ang)
            return carry

        lax.fori_loop(0, s // ROW_TILE, table_rows, 0)
        second.finish()
        weights.forward()
        weights.finish()

    return pl.pallas_call(
        body, name="prologue",
        out_shape=[jax.ShapeDtypeStruct((N_DEV,) + c_taps.shape, F32), jax.ShapeDtypeStruct((N_DEV, N_DEV, cols), F32),
                   jax.ShapeDtypeStruct((s, 4 * LANE), F32)] + _Gather.out_shapes(shards),
        in_specs=[IN_VMEM] * 7 + [ANY] * n, out_specs=[IN_VMEM] * 3 + [ANY] * n,
        scratch_shapes=[pltpu.VMEM((N_DEV, cols), F32)] + _SmallGather.semaphores() * 2 + _Gather.scratch(shards),
        compiler_params=_params(None, 14 << 20),
    )(c_taps, w_ada_shard, b_shard, pos_col, freq, csel, ssel, *shards)


IN_VMEM = pl.BlockSpec(memory_space=pltpu.VMEM)
ANY = pl.BlockSpec(memory_space=pl.ANY)


class _Gather:
    def __init__(self, w_refs, out_refs, send_sems, recv_sems, own_sems, *bounce_refs):
        x, y, c = _position()
        q0 = 2 * x + y
        sibling = (x, y, 1 - c)
        self.ici, self.ici_in, self.fwd, self.fwd_in, self.own_in, self.own_out = [], [], [], [], [], []
        for k, (w_ref, out_ref) in enumerate(zip(w_refs, out_refs)):
            half = w_ref.shape[0] // 2
            self.own_in.append(pltpu.make_async_copy(w_ref, bounce_refs[k], own_sems.at[2 * k]))
            self.own_out.append(pltpu.make_async_copy(bounce_refs[k], out_ref.at[q0], own_sems.at[2 * k + 1]))

            def blk(q, e, out_ref=out_ref, half=half):
                return out_ref.at[q, pl.ds(pl.multiple_of(e * half, 16), half), :]

            def copy(src, dst, i, to):
                return pltpu.make_async_remote_copy(src_ref=src, dst_ref=dst, send_sem=send_sems.at[i], recv_sem=recv_sems.at[i],
                                                    device_id=to, device_id_type=MESH)

            src = w_ref.at[pl.ds(pl.multiple_of(c * half, 16), half), :]
            for j, (cx, cy, qj) in enumerate(_other_chips(x, y)):
                self.ici.append(copy(src, blk(q0, c), 6 * k + j, (cx, cy, c)))
                self.ici_in.append(copy(blk(qj, c), blk(qj, c), 6 * k + j, (cx, cy, c)))
                self.fwd.append(copy(blk(qj, c), blk(qj, c), 6 * k + 3 + j, sibling))
                self.fwd_in.append(copy(blk(qj, 1 - c), blk(qj, 1 - c), 6 * k + 3 + j, sibling))

    @staticmethod
    def out_shapes(shards):
        return [jax.ShapeDtypeStruct((N_CHIP,) + s.shape, s.dtype) for s in shards]

    @staticmethod
    def scratch(shards):
        n = len(shards)
        return ([pltpu.SemaphoreType.DMA((6 * n,)), pltpu.SemaphoreType.DMA((6 * n,)), pltpu.SemaphoreType.DMA((2 * n,))]
                + [pltpu.VMEM(s.shape, s.dtype) for s in shards])

    def start(self):
        for cp in self.ici + self.own_in:
            cp.start()

    def forward(self):
        for fetched, placed in zip(self.own_in, self.own_out):
            fetched.wait()
            placed.start()
        for arrived, onward in zip(self.ici_in, self.fwd):
            arrived.wait_recv()
            onward.start()

    def finish(self):
        for cp in self.fwd_in:
            cp.wait_recv()
        for cp in self.ici + self.fwd:
            cp.wait_send()
        for cp in self.own_out:
            cp.wait()


class _PairSwap:
    def __init__(self, g_refs, out_refs, send_sems, recv_sems):
        x, y, c = _position()
        self.copies = [
            pltpu.make_async_remote_copy(src_ref=g_ref.at[:, 1 - c], dst_ref=out_ref, send_sem=send_sems.at[k],
                                         recv_sem=recv_sems.at[k], device_id=(x, y, 1 - c), device_id_type=MESH)
            for k, (g_ref, out_ref) in enumerate(zip(g_refs, out_refs))]

    @staticmethod
    def out_shapes(grads):
        return [jax.ShapeDtypeStruct((N_CHIP,) + g.shape[2:], g.dtype) for g in grads]

    @staticmethod
    def semaphores(n):
        return [pltpu.SemaphoreType.DMA((n,)), pltpu.SemaphoreType.DMA((n,))]

    def start(self):
        for cp in self.copies:
            cp.start()

    def finish(self):
        for cp in self.copies:
            cp.wait_recv()
        for cp in self.copies:
            cp.wait_send()


def _pair_sum(g, a, c_idx, name):
    _, _, rh, cols = g.shape
    tr = rh
    for cand in (256, 128, 64, 32, 16):
        if rh % cand == 0 and rh > cand:
            tr = cand
            break

    def body(c_ref, g_ref, a_ref, o_ref):
        o_ref[...] = (g_ref[...] + a_ref[...]).astype(BF16)

    return pl.pallas_call(
        body, name=name,
        grid_spec=pltpu.PrefetchScalarGridSpec(
            num_scalar_prefetch=1, grid=(N_CHIP, rh // tr),
            in_specs=[pl.BlockSpec((None, None, tr, cols), lambda q, i, c_ref: (q, c_ref[0], i, 0)),
                      pl.BlockSpec((None, tr, cols), lambda q, i, c_ref: (q, i, 0))],
            out_specs=pl.BlockSpec((None, tr, cols), lambda q, i, c_ref: (q, i, 0))),
        out_shape=jax.ShapeDtypeStruct((N_CHIP, rh, cols), BF16),
        compiler_params=_params(("parallel", "parallel"), 10 * _nbytes((tr, cols), F32)),
    )(c_idx, g, a)


def _scatter_and_gather(parts, small, name):
    n = len(parts)

    def body(*refs):
        scatter = _Scatter(refs[:n], refs[n + 1:2 * n + 1], *refs[2 * n + 2:2 * n + 4])
        gather = _SmallGather(refs[n], refs[2 * n + 1], *refs[2 * n + 4:])
        scatter.start()
        gather.start()
        gather.finish()
        scatter.finish()

    return pl.pallas_call(
        body, name=name,
        out_shape=_Scatter.out_shapes(parts) + [jax.ShapeDtypeStruct((N_DEV,) + small.shape, F32)],
        in_specs=[ANY] * n + [IN_VMEM], out_specs=[ANY] * n + [IN_VMEM],
        scratch_shapes=_Scatter.semaphores(n) + _SmallGather.semaphores(),
        compiler_params=_params(None, 10 * _nbytes(small.shape, F32)),
    )(*parts, small)


class _Scatter:
    def __init__(self, p_refs, out_refs, send_sems, recv_sems):
        x, y, c = _position()
        self.copies = []
        for k, (p_ref, out_ref) in enumerate(zip(p_refs, out_refs)):
            for j, (cx, cy, qj) in enumerate(_other_chips(x, y)):
                self.copies.append(pltpu.make_async_remote_copy(
                    src_ref=p_ref.at[qj], dst_ref=out_ref.at[j], send_sem=send_sems.at[3 * k + j],
                    recv_sem=recv_sems.at[3 * k + j], device_id=(cx, cy, c), device_id_type=MESH))

    @staticmethod
    def out_shapes(parts):
        return [jax.ShapeDtypeStruct((3,) + p.shape[1:], p.dtype) for p in parts]

    @staticmethod
    def semaphores(n):
        return [pltpu.SemaphoreType.DMA((3 * n,)), pltpu.SemaphoreType.DMA((3 * n,))]

    def start(self):
        for cp in self.copies:
            cp.start()

    def finish(self):
        for cp in self.copies:
            cp.wait_recv()
        for cp in self.copies:
            cp.wait_send()


def _shard_sum(p, b, qc_idx, name):
    _, rh, cols = p.shape
    tr = rh
    for cand in (256, 128, 64, 32, 16):
        if rh % cand == 0 and rh > cand:
            tr = cand
            break

    def body(qc_ref, p_ref, b_ref, o_ref):
        acc = p_ref[...].astype(F32)
        for j in range(3):
            acc = acc + b_ref[j].astype(F32)
        o_ref[...] = acc

    return pl.pallas_call(
        body, name=name,
        grid_spec=pltpu.PrefetchScalarGridSpec(
            num_scalar_prefetch=1, grid=(rh // tr,),
            in_specs=[pl.BlockSpec((None, tr, cols), lambda i, qc_ref: (qc_ref[0], i, 0)),
                      pl.BlockSpec((3, tr, cols), lambda i, qc_ref: (0, i, 0))],
            out_specs=pl.BlockSpec((None, tr, cols), lambda i, qc_ref: (qc_ref[1], i, 0))),
        out_shape=jax.ShapeDtypeStruct((2, rh, cols), F32),
        compiler_params=_params(("parallel",), 8 * _nbytes((tr, cols), F32)),
    )(qc_idx, p, b)


def _join_halves(shards):
    n = len(shards)

    def body(*refs):
        out_refs = refs[n:2 * n]
        send_sems, recv_sems = refs[2 * n:]
        x, y, c = _position()
        cps = [pltpu.make_async_remote_copy(src_ref=out_refs[k].at[c], dst_ref=out_refs[k].at[c], send_sem=send_sems.at[k],
                                            recv_sem=recv_sems.at[k], device_id=(x, y, 1 - c), device_id_type=MESH)
               for k in range(n)]
        for cp in cps:
            cp.start()
        for k in range(n):
            arriving = out_refs[k].at[1 - c]
            pltpu.make_async_remote_copy(src_ref=arriving, dst_ref=arriving, send_sem=send_sems.at[k], recv_sem=recv_sems.at[k],
                                         device_id=(x, y, 1 - c), device_id_type=MESH).wait_recv()
        for cp in cps:
            cp.wait_send()

    return pl.pallas_call(
        body, name="rs_join",
        out_shape=[jax.ShapeDtypeStruct(a.shape, a.dtype) for a in shards],
        in_specs=[ANY] * n, out_specs=[ANY] * n, input_output_aliases={k: k for k in range(n)},
        scratch_shapes=[pltpu.SemaphoreType.DMA((n,)), pltpu.SemaphoreType.DMA((n,))],
    )(*shards)


def _cols_from_shards(g):
    q, r, cs = g.shape
    return jnp.transpose(g, (1, 0, 2)).reshape(r, q * cs)


def _cols_to_shards(w):
    r, cfull = w.shape
    return jnp.transpose(w.reshape(r, N_CHIP, cfull // N_CHIP), (1, 0, 2))


def _pad_w_in(w):
    z = lambda n: jnp.zeros((w.shape[0], n), w.dtype)
    q_lat, kv_lat, kpe = w[:, 0:512], w[:, 512:768], w[:, 768:800]
    qd, kd, vd = w[:, 800:1312], w[:, 1312:1824], w[:, 1824:2336]
    return jnp.concatenate([q_lat, qd, kd, vd, kv_lat, z(KPE_OFF), kpe, z(LANE - KPE_OFF - ROPE)], axis=1)


def _pad_w_qb(w):
    w3 = w.reshape(Q_LORA, HEADS, NOPE + ROPE)
    return jnp.pad(w3, ((0, 0), (0, 0), (0, LANE - NOPE - ROPE))).reshape(Q_LORA, HEADS * LANE)


def _unpad_w_qb(g):
    return g.reshape(Q_LORA, HEADS, LANE)[:, :, :NOPE + ROPE].reshape(Q_LORA, HEADS * (NOPE + ROPE))


def _pad_w_kvb(w):
    w3 = w.reshape(KV_LORA, HEADS, 2 * NOPE)
    kp = jnp.pad(w3[:, :, :NOPE], ((0, 0), (0, 0), (0, LANE - NOPE))).reshape(KV_LORA, HEADS * LANE)
    return jnp.concatenate([kp, w3[:, :, NOPE:].reshape(KV_LORA, DIL_W)], axis=1)


def _unpad_w_kvb(g):
    gk = g[:, :HEADS * LANE].reshape(KV_LORA, HEADS, LANE)[:, :, :NOPE]
    gv = g[:, HEADS * LANE:].reshape(KV_LORA, HEADS, NOPE)
    return jnp.concatenate([gk, gv], axis=2).reshape(KV_LORA, HEADS * 2 * NOPE)


def _head_gains(g_q_nope, g_q_pe, g_k_nope, g_k_pe, g_dq, g_dk):
    z = lambda n: jnp.zeros((1, n), F32)
    q1 = jnp.concatenate([g_q_nope, g_q_pe, z(LANE - NOPE - ROPE)], axis=1)
    k1 = jnp.concatenate([g_k_nope, z(LANE - NOPE)], axis=1)
    kpe = jnp.concatenate([z(KPE_OFF), g_k_pe, z(LANE - KPE_OFF - ROPE)], axis=1)
    return dict(q=jnp.tile(q1, (1, HEADS)), k=jnp.tile(k1, (1, HEADS)), kpe=kpe,
                dq=jnp.tile(g_dq, (1, HEADS)), dk=jnp.tile(g_dk, (1, HEADS)))


def kernel(x, c, positions, w_ada, b_ada, g_mix_norm, w_in, g_q_lat, w_q_b, g_kv_lat, w_kv_b, g_mla_q_nope, g_mla_q_pe, g_mla_k_nope, g_mla_k_pe, g_dil_q, g_dil_k, w_o, g_ffn_norm, w_up, w_conv, b_conv, w_down, loss_target, m_w_ada, m_b_ada, m_g_mix_norm, m_w_in, m_g_q_lat, m_w_q_b, m_g_kv_lat, m_w_kv_b, m_g_mla_q_nope, m_g_mla_q_pe, m_g_mla_k_nope, m_g_mla_k_pe, m_g_dil_q, m_g_dil_k, m_w_o, m_g_ffn_norm, m_w_up, m_w_conv, m_b_conv, m_w_down, v_w_ada, v_b_ada, v_g_mix_norm, v_w_in, v_g_q_lat, v_w_q_b, v_g_kv_lat, v_w_kv_b, v_g_mla_q_nope, v_g_mla_q_pe, v_g_mla_k_nope, v_g_mla_k_pe, v_g_dil_q, v_g_dil_k, v_w_o, v_g_ffn_norm, v_w_up, v_w_conv, v_b_conv, v_w_down):
    args = dict(locals())
    weights = {n: args[n][0] for n in ("w_ada", "w_in", "w_q_b", "w_kv_b", "w_o", "w_up", "w_conv", "w_down")}
    small_w = {n: args[n] for n in ("b_ada",) + tuple(n for n, _ in SMALL_WIDTHS)}
    mom_m = {n[2:]: (args[n][0] if args[n].ndim == 3 else args[n]) for n in args if n.startswith("m_")}
    mom_v = {n[2:]: (args[n][0] if args[n].ndim == 3 else args[n]) for n in args if n.startswith("v_")}

    xi, yi, ci = _position()
    q0 = 2 * xi + yi
    me = 4 * xi + 2 * yi + ci
    xs, tgt = x[0], loss_target[0]
    s = xs.shape[0]
    consts = _seg_consts()
    c_idx, qc_idx = jnp.reshape(ci, (1,)).astype(I32), jnp.stack([q0, ci]).astype(I32)

    def halves(g4):
        q, r, cc = g4.shape
        return g4.reshape(q, 2, r // 2, cc)

    own_first = [weights[n].astype(BF16) for n in ("w_in", "w_q_b", "w_kv_b")]
    own_later = [weights[n].astype(BF16) for n in ("w_o", "w_up", "w_down")]
    conv_cols = UP_W // N_CHIP
    ada_cols = w_ada.shape[2]
    b_shard = lax.dynamic_slice_in_dim(b_ada, q0 * ada_cols, ada_cols, axis=1)
    c_taps = jnp.concatenate([c, weights["w_conv"].reshape(1, 3 * conv_cols)], axis=1)
    c_taps_all, mod_all, tab, *gathered = _prologue(c_taps, weights["w_ada"], b_shard, positions.reshape(s, 1),
                                                    _rope_consts(), own_first)
    c_all = c_taps_all[:, 0, :D_MODEL]
    w_conv_f = c_taps_all[:, 0, D_MODEL:].reshape(N_CHIP, 2, 3, conv_cols)[:, 0]
    w_conv_f = jnp.transpose(w_conv_f, (1, 0, 2)).reshape(3, UP_W)
    mod_all = mod_all.reshape(N_CHIP, 2, N_DEV, ada_cols)
    mod = lax.dynamic_index_in_dim(lax.dynamic_index_in_dim(mod_all, ci, 1, False), me, 1, False)
    mod = mod.reshape(1, N_CHIP * ada_cols)
    sh1, sc1, g1, sh2, sc2, g2 = [mod[:, k * D_MODEL:(k + 1) * D_MODEL] for k in range(6)]
    w_in_f = _cols_from_shards(gathered[0])
    w_in_p = _pad_w_in(w_in_f)
    w_qb_p = _pad_w_qb(_cols_from_shards(gathered[1]))
    w_kvb_p = _pad_w_kvb(_cols_from_shards(gathered[2]))
    gains = _head_gains(g_mla_q_nope, g_mla_q_pe, g_mla_k_nope, g_mla_k_pe, g_dil_q, g_dil_k)

    h = _prenorm(xs, g_mix_norm, sc1, sh1, "prenorm")
    proj = _mm(h, w_in_p, "nn", F32, 512, P_COLS, "mm_in")
    ql, kvl = _latnorm(proj, g_q_lat, g_kv_lat)
    q_raw = _mm(ql, w_qb_p, "nn", F32, 512, HEADS * LANE, "mm_qb")
    kv_raw = _mm(kvl, w_kvb_p, "nn", F32, 512, HEADS * LANE + DIL_W, "mm_kvb")
    qm, km, vm, qd, kd, vd = _attn_prep(q_raw, kv_raw, proj, tab, gains, consts)
    scale_m, scale_d = (NOPE + ROPE) ** -0.5, DIL_DIM ** -0.5
    o_m, lse_m, got_up = _attn_fwd(qm, km, vm, True, scale_m, "attn_mla", gather=own_later[1:2])
    o_d, lse_d, got_o, got_down = _attn_fwd(qd, kd, vd, False, scale_d, "attn_dil", gather=[own_later[0], own_later[2]])
    gathered = [got_o, got_up, got_down]
    w_o_f = gathered[0].reshape(D_MODEL, D_MODEL)
    w_up_f = _cols_from_shards(gathered[1])
    w_down_f = gathered[2].reshape(D_FF, D_MODEL)
    mix_in = jnp.concatenate([o_m, o_d], axis=1)
    mix = _mm(mix_in, w_o_f, "nn", F32, 512, D_MODEL, "mm_o")
    x1, h2 = _resid_prenorm(xs, mix, g1, g_ffn_norm, sc2, sh2)
    up = _mm(h2, w_up_f, "nn", F32, 512, CONV_TILE, "mm_up")
    act = _conv_gate(up, w_conv_f, b_conv)
    ffn = _mm(act, w_down_f, "nn", F32, 256, D_MODEL, "mm_down")
    dy, dffn, dg2, loss_part = _final(x1, ffn, tgt, g2)

    da = _mm(dffn, w_down_f, "nt", F32, 512, CONV_TILE, "mm_down_dx")
    gw_down = _mm(act, dffn, "tn", F32, 256, D_MODEL, "mm_down_dw")
    dup_g, dup_v, dbg, dbv, dwg, dwv = _gate_bwd(up, da, w_conv_f, b_conv)
    dup = jnp.concatenate([dup_g, dup_v], axis=1)
    early_names = ("w_up", "w_down", "w_o")
    gw_up = _mm(h2, dup, "tn", F32, 512, CONV_TILE, "mm_up_dw", col_shards=True)
    early = [halves(gw_up), halves(gw_down.reshape(N_CHIP, D_FF // N_CHIP, D_MODEL))]
    dh2, *early_sib = _mm(dup, w_up_f, "nt", F32, 256, 512, "mm_up_dx", swap=early, b_outer=True)
    dx1, dmix, acc2 = _ffnnorm_bwd(dh2, x1, dy, mix, g_ffn_norm, sc2, g1)
    gw_o = _mm(mix_in, dmix, "tn", F32, 512, D_MODEL, "mm_o_dw")
    early.append(halves(gw_o.reshape(N_CHIP, D_MODEL // N_CHIP, D_MODEL)))
    dmix_in, sib_o = _mm(dmix, w_o_f, "nt", F32, 512, D_MODEL, "mm_o_dx", swap=early[2:])
    early_sib.append(sib_o)
    early_sums = [_pair_sum(g, a, c_idx, "pair_sum_" + n) for g, a, n in zip(early, early_sib, early_names)]
    dqm, dkm, dvm, *early_recv = _attn_bwd(qm, km, vm, o_m, dmix_in, 0, lse_m, True, scale_m, "attn_mla_bwd",
                                           scatter=early_sums[:1])
    dqd, dkd, dvd, *early_recv_d = _attn_bwd(qd, kd, vd, o_d, dmix_in, DIL_W // LANE, lse_d, False, scale_d,
                                             "attn_dil_bwd", scatter=early_sums[1:])
    early_recv = early_recv + early_recv_d
    dq_raw, dkv_raw, dkpe_b, dqd_b, dkd_b, dvd_b, dgains = _attn_prep_bwd(
        dqm, dkm, dvm, dqd, dkd, dvd, q_raw, kv_raw, proj, tab, gains, consts)
    dql = _mm(dq_raw, w_qb_p, "nt", F32, 512, Q_LORA, "mm_qb_dx")
    gw_qb = _unpad_w_qb(_mm(ql, dq_raw, "tn", F32, Q_LORA, HEADS * LANE, "mm_qb_dw"))
    dkvl = _mm(dkv_raw, w_kvb_p, "nt", F32, 512, KV_LORA, "mm_kvb_dx")
    gw_kvb = _unpad_w_kvb(_mm(kvl, dkv_raw, "tn", F32, KV_LORA, HEADS * LANE + DIL_W, "mm_kvb_dw"))
    dqlat_b, dkvlat_b, dglat = _latnorm_bwd(dql, dkvl, proj, g_q_lat, g_kv_lat)
    dproj = jnp.concatenate([dqlat_b, dkvlat_b, dkpe_b[:, KPE_OFF:KPE_OFF + ROPE], dqd_b, dkd_b, dvd_b], axis=1)
    gw_in = _mm(h, dproj, "tn", F32, 512, IN_COLS, "mm_in_dw")
    late_names = ("w_in", "w_q_b", "w_kv_b")
    late = [halves(_cols_to_shards(gw_in)), halves(_cols_to_shards(gw_qb)), halves(_cols_to_shards(gw_kvb))]
    dh, *late_sib = _mm(dproj, w_in_f, "nt", F32, 512, D_MODEL, "mm_in_dx", swap=late)
    grad_x, acc1 = _mixnorm_bwd(dh, xs, dx1, g_mix_norm, sc1)

    packed = _pack_small(acc1, acc2, dg2, dglat, dgains, dbg, dbv, dwg, dwv, loss_part)
    late_sums = [_pair_sum(g, a, c_idx, "pair_sum_" + n) for g, a, n in zip(late, late_sib, late_names)]
    *late_recv, gathered_small = _scatter_and_gather(late_sums, packed, "rs_scatter_late")

    grad_b_ada, *small_grads, gconv_full, loss_sum = _sum_unpack(gathered_small)
    grads = {"b_ada": grad_b_ada}
    grads.update({n: g for (n, _), g in zip(SMALL_WIDTHS, small_grads)})
    shard_cols = UP_W // N_CHIP
    grads["w_conv"] = lax.dynamic_slice_in_dim(gconv_full, q0 * shard_cols, shard_cols, axis=1)
    dmod_all = gathered_small[:, 0, :6 * D_MODEL]
    grads["w_ada"] = _ada_bwd(c_all, lax.dynamic_slice_in_dim(dmod_all, q0 * ada_cols, ada_cols, axis=1))

    big_names = late_names + early_names
    half_sums = [_shard_sum(p, b, qc_idx, "shard_sum_" + n)
                 for p, b, n in zip(late_sums + early_sums, list(late_recv) + list(early_recv), big_names)]
    for n, full in zip(big_names, _join_halves(half_sums)):
        grads[n] = full.reshape(2 * full.shape[1], full.shape[2])

    delta, new_m, new_v = {}, {}, {}
    for n in ("w_ada", "w_in", "w_q_b", "w_kv_b", "w_o", "w_up", "w_conv", "w_down"):
        operands = (weights[n], grads[n], mom_m[n], mom_v[n])
        flipped = n in ("w_in", "w_q_b")
        if flipped:
            operands = [jnp.swapaxes(a, 0, 1) for a in operands]
            grads[n] = jnp.swapaxes(operands[1], 0, 1)
        if n == "w_ada":
            operands = _in_hbm(*operands)
        delta[n], new_m[n], new_v[n] = _adamw(*operands, "adamw_" + n)
        if flipped:
            delta[n], new_m[n], new_v[n] = (jnp.swapaxes(a, 0, 1) for a in (delta[n], new_m[n], new_v[n]))
    vec_names = ("b_ada",) + tuple(n for n, _ in SMALL_WIDTHS)
    sd, sm, sv = _adamw_vectors(*[[d_[n] for n in vec_names] for d_ in (small_w, grads, mom_m, mom_v)])
    for k, n in enumerate(vec_names):
        delta[n], new_m[n], new_v[n] = sd[k], sm[k], sv[k]

    loss = loss_sum[0, 0]
    order = ("w_ada", "b_ada", "g_mix_norm", "w_in", "g_q_lat", "w_q_b", "g_kv_lat", "w_kv_b", "g_mla_q_nope", "g_mla_q_pe",
             "g_mla_k_nope", "g_mla_k_pe", "g_dil_q", "g_dil_k", "w_o", "g_ffn_norm", "w_up", "w_conv", "b_conv", "w_down")
    lead = lambda n, z: z[None] if n.startswith("w_") else z
    outs = [loss, grad_x[None]]
    for d_ in (grads, delta, new_m, new_v):
        outs += [lead(n, d_[n]) for n in order]
    return tuple(outs)
```

```python
import functools

import numpy as np
import jax
import jax.numpy as jnp
from jax import lax
from jax.experimental import pallas as pl
from jax.experimental.pallas import tpu as pltpu

F32 = jnp.float32
BF16 = jnp.bfloat16
I32 = jnp.int32

D_MODEL = 1024
HEADS = 8
NOPE = 64
ROPE = 32
Q_LORA = 512
KV_LORA = 256
DIL_DIM = 64
DIL_W = HEADS * DIL_DIM
D_FF = 2816
UP_W = 2 * D_FF
IN_COLS = Q_LORA + KV_LORA + ROPE + 3 * DIL_W
ROPE_THETA = 10000.0
EPS = 1e-6
NEG_INF = -1e30
N_DEV = 8
N_CHIP = 4

ADAM_LR = 0.001
ADAM_B1 = 0.9
ADAM_B2 = 0.999
ADAM_EPS = 1e-08
ADAM_WD = 0.01
ADAM_STEP = 10

LANE = 128
ROW_TILE = 256
NORM_TILE = 512
ATT_TQ = 512
ATT_TK = 256
ATT_TK_BWD = 512
LOG2E = 1.4426950408889634
LN2 = 0.6931471805599453
VMEM_CAP = 56 * 1024 * 1024
VMEM_FLOOR = 32 * 1024 * 1024

P_QLAT, P_QD, P_KD, P_VD, P_KVLAT, P_KPE = 0, 512, 1024, 1536, 2048, 2304
P_COLS = 2432
KPE_OFF = 64

NN = (((1,), (0,)), ((), ()))
NT = (((1,), (1,)), ((), ()))
TN = (((0,), (0,)), ((), ()))
HIGHEST = lax.Precision.HIGHEST
MESH = pl.DeviceIdType.MESH


def _params(sem=None, est_bytes=0):
    limit = int(min(max(2 * est_bytes + (4 << 20), VMEM_FLOOR), VMEM_CAP))
    if sem is None:
        return pltpu.CompilerParams(vmem_limit_bytes=limit)
    return pltpu.CompilerParams(dimension_semantics=sem, vmem_limit_bytes=limit)


def _nbytes(shape, dtype):
    return int(np.prod(shape)) * jnp.dtype(dtype).itemsize


def _in_hbm(*xs):
    return [pltpu.with_memory_space_constraint(x, pltpu.HBM) for x in xs]


def _mm(a, b, dims, out_dtype, tm, tn, name, col_shards=False, swap=(), b_outer=False):
    def spec(block, index):
        if b_outer:
            return pl.BlockSpec(block, lambda g0, g1: index(g1, g0))
        return pl.BlockSpec(block, index)

    if dims == "nn":
        (m, k), (k2, n) = a.shape, b.shape
        a_spec = spec((tm, k), lambda i, j: (i, 0))
        b_spec = spec((k, tn), lambda i, j: (0, j))
        dn = NN
    elif dims == "nt":
        (m, k), (n, k2) = a.shape, b.shape
        a_spec = spec((tm, k), lambda i, j: (i, 0))
        b_spec = spec((tn, k), lambda i, j: (j, 0))
        dn = NT
    else:
        (k, m), (k2, n) = a.shape, b.shape
        a_spec = spec((k, tm), lambda i, j: (0, i))
        b_spec = spec((k, tn), lambda i, j: (0, j))
        dn = TN
    assert k == k2 and m % tm == 0 and n % tn == 0, (name, a.shape, b.shape, tm, tn)

    nw = len(swap)
    grid = (n // tn, m // tm) if b_outer else (m // tm, n // tn)

    def body(*refs):
        a_ref, b_ref, o_ref = refs[0], refs[1], refs[2 + nw]
        comm = (refs[2:2 + nw], refs[3 + nw:3 + 2 * nw]) + tuple(refs[3 + 2 * nw:])
        if nw:
            @pl.when((pl.program_id(0) == 0) & (pl.program_id(1) == 0))
            def _():
                _PairSwap(*comm).start()

        o_ref[...] = lax.dot_general(a_ref[...], b_ref[...], dn, preferred_element_type=F32).astype(o_ref.dtype)

        if nw:
            @pl.when((pl.program_id(0) == grid[0] - 1) & (pl.program_id(1) == grid[1] - 1))
            def _():
                _PairSwap(*comm).finish()

    est = _nbytes((tm, k), a.dtype) + _nbytes((tn, k), b.dtype) + _nbytes((tm, tn), F32) + _nbytes((tm, tn), out_dtype)
    if col_shards:
        out_spec = spec((None, tm, tn), lambda i, j: (j, i, 0))
        out_shape = jax.ShapeDtypeStruct((n // tn, m, tn), out_dtype)
    else:
        out_spec = spec((tm, tn), lambda i, j: (i, j))
        out_shape = jax.ShapeDtypeStruct((m, n), out_dtype)
    out = pl.pallas_call(
        body, name=name, grid=grid,
        in_specs=[a_spec, b_spec] + [ANY] * nw,
        out_specs=[out_spec] + [ANY] * nw,
        out_shape=[out_shape] + _PairSwap.out_shapes(swap),
        scratch_shapes=_PairSwap.semaphores(nw) if nw else [],
        compiler_params=_params(("arbitrary", "arbitrary") if nw else ("parallel", "parallel"), est),
    )(a, b, *swap)
    return out if nw else out[0]


def _seg_consts():
    seg_q = np.zeros((HEADS * LANE, LANE), np.float32)
    inv_q = np.zeros((1, LANE), np.float32)
    seg_k = np.zeros((HEADS * LANE, LANE), np.float32)
    inv_k = np.zeros((1, LANE), np.float32)
    seg_d = np.zeros((DIL_W, LANE), np.float32)
    inv_d = np.zeros((1, LANE), np.float32)
    for h in range(HEADS):
        seg_q[h * LANE:h * LANE + NOPE, 2 * h] = 1.0
        seg_q[h * LANE + NOPE:h * LANE + NOPE + ROPE, 2 * h + 1] = 1.0
        inv_q[0, 2 * h], inv_q[0, 2 * h + 1] = 1.0 / NOPE, 1.0 / ROPE
        seg_k[h * LANE:h * LANE + NOPE, h] = 1.0
        inv_k[0, h] = 1.0 / NOPE
        seg_d[h * DIL_DIM:(h + 1) * DIL_DIM, h] = 1.0
        inv_d[0, h] = 1.0 / DIL_DIM
    fold_q = np.tile(np.eye(LANE, dtype=np.float32), (HEADS, 1))
    fold_d = np.zeros((DIL_W, LANE), np.float32)
    fold_d[np.arange(DIL_W), np.arange(DIL_W) % DIL_DIM] = 1.0
    j = lambda v: jnp.asarray(v)
    b = lambda v: jnp.asarray(v, dtype=BF16)
    return dict(seg_q=b(seg_q), exp_q=b(seg_q.T.copy()), inv_q=j(inv_q), seg_k=b(seg_k), exp_k=b(seg_k.T.copy()),
                inv_k=j(inv_k), seg_d=b(seg_d), exp_d=b(seg_d.T.copy()), inv_d=j(inv_d), fold_q=j(fold_q), fold_d=j(fold_d))


def _rope_consts():
    inv_d = jnp.power(ROPE_THETA, -2.0 * jnp.arange(DIL_DIM // 2, dtype=F32) / DIL_DIM)
    inv_q = jnp.power(ROPE_THETA, -2.0 * jnp.arange(ROPE // 2, dtype=F32) / ROPE)
    lanes = np.arange(LANE)
    freq_d = inv_d[lanes % (DIL_DIM // 2)]
    in_pe = (lanes >= KPE_OFF) & (lanes < KPE_OFF + ROPE)
    freq_q = jnp.where(jnp.asarray(in_pe), inv_q[(lanes - KPE_OFF) % (ROPE // 2)], 0.0)
    sign_d = np.where(lanes % DIL_DIM < DIL_DIM // 2, -1.0, 1.0).astype(np.float32)
    sign_q = np.where(in_pe, np.where((lanes - KPE_OFF) < ROPE // 2, -1.0, 1.0), 0.0).astype(np.float32)
    zeros, ones = np.zeros(LANE, np.float32), np.ones(LANE, np.float32)
    freq = jnp.concatenate([freq_d, freq_d, freq_q, freq_q])[None, :]
    csel = jnp.asarray(np.concatenate([ones, zeros, ones, zeros]))[None, :]
    ssel = jnp.asarray(np.concatenate([zeros, sign_d, zeros, sign_q]))[None, :]
    return freq, csel, ssel


def _full(shape):
    return pl.BlockSpec(shape, lambda *_: (0,) * len(shape))


def _tile_lanes(x, n):
    return jnp.concatenate([x] * n, axis=1)


def _rms(x):
    return lax.rsqrt(jnp.mean(x * x, axis=-1, keepdims=True) + EPS)


def _prenorm(x, gain, scale, shift, name):
    s, d = x.shape

    def body(x_ref, g_ref, sc_ref, sh_ref, h_ref):
        xv = x_ref[...]
        h = (xv * _rms(xv)) * g_ref[...] * (1.0 + sc_ref[...]) + sh_ref[...]
        h_ref[...] = h.astype(BF16)

    row = pl.BlockSpec((NORM_TILE, d), lambda i: (i, 0))
    return pl.pallas_call(
        body, name=name, grid=(s // NORM_TILE,),
        in_specs=[row, _full((1, d)), _full((1, d)), _full((1, d))],
        out_specs=row, out_shape=jax.ShapeDtypeStruct((s, d), BF16),
        compiler_params=_params(("parallel",)),
    )(x, gain, scale, shift)


def _latnorm(proj, g_q, g_kv):
    s = proj.shape[0]

    def body(q_ref, kv_ref, gq_ref, gkv_ref, ql_ref, kvl_ref):
        q, kv = q_ref[...], kv_ref[...]
        ql_ref[...] = ((q * _rms(q)) * gq_ref[...]).astype(BF16)
        kvl_ref[...] = ((kv * _rms(kv)) * gkv_ref[...]).astype(BF16)

    return pl.pallas_call(
        body, name="latnorm", grid=(s // NORM_TILE,),
        in_specs=[pl.BlockSpec((NORM_TILE, Q_LORA), lambda i: (i, P_QLAT // Q_LORA)),
                  pl.BlockSpec((NORM_TILE, KV_LORA), lambda i: (i, P_KVLAT // KV_LORA)),
                  _full((1, Q_LORA)), _full((1, KV_LORA))],
        out_specs=[pl.BlockSpec((NORM_TILE, Q_LORA), lambda i: (i, 0)), pl.BlockSpec((NORM_TILE, KV_LORA), lambda i: (i, 0))],
        out_shape=[jax.ShapeDtypeStruct((s, Q_LORA), BF16), jax.ShapeDtypeStruct((s, KV_LORA), BF16)],
        compiler_params=_params(("parallel",)),
    )(proj, proj, g_q, g_kv)


def _dot01(v, mat01):
    hi = v.astype(BF16)
    lo = (v - hi.astype(F32)).astype(BF16)
    return jnp.dot(hi, mat01, preferred_element_type=F32) + jnp.dot(lo, mat01, preferred_element_type=F32)


def _seg_rinv(x, seg, exp, inv):
    r = lax.rsqrt(_dot01(x * x, seg) * inv + EPS)
    return _dot01(r, exp)


def _seg_mean(v, seg, exp, inv):
    return _dot01(_dot01(v, seg) * inv, exp)


def _swap_halves(x, half):
    n = x.shape[1]
    lane = lax.broadcasted_iota(I32, (1, n), 1)
    first = (lane & (2 * half - 1)) < half
    return jnp.where(first, pltpu.roll(x, n - half, 1), pltpu.roll(x, half, 1))


def _rope(x, cos, sin_signed, half):
    return x * cos + _swap_halves(x, half) * sin_signed


def _rope_bwd(dy, cos, sin_signed, half):
    return dy * cos + _swap_halves(dy * sin_signed, half)


def _pe_lane_mask(n):
    lane = lax.broadcasted_iota(I32, (1, n), 1) & (LANE - 1)
    return (lane >= KPE_OFF) & (lane < KPE_OFF + ROPE)


def _attn_prep(q_raw, kv_raw, proj, tab, gains, consts):
    s = q_raw.shape[0]
    hw = HEADS * LANE

    def body(q_ref, kv_ref, kpe_ref, qd_ref, kd_ref, vd_ref, tab_ref,
             gq_ref, gk_ref, gkpe_ref, gdq_ref, gdk_ref,
             segq_ref, expq_ref, invq_ref, segk_ref, expk_ref, invk_ref, segd_ref, expd_ref, invd_ref,
             qm_ref, km_ref, vm_ref, qdo_ref, kdo_ref, vdo_ref):
        tab_v = tab_ref[...]
        cos_d, sin_d = _tile_lanes(tab_v[:, 0:LANE], DIL_W // LANE), _tile_lanes(tab_v[:, LANE:2 * LANE], DIL_W // LANE)
        cos_q1, sin_q1 = tab_v[:, 2 * LANE:3 * LANE], tab_v[:, 3 * LANE:4 * LANE]
        cos_q, sin_q = _tile_lanes(cos_q1, HEADS), _tile_lanes(sin_q1, HEADS)

        q = q_ref[...]
        qn = q * _seg_rinv(q, segq_ref[...], expq_ref[...], invq_ref[...]) * gq_ref[...]
        qm_ref[...] = _rope(qn, cos_q, sin_q, ROPE // 2).astype(BF16)

        kv = kv_ref[...]
        kp = kv[:, :hw]
        kn = kp * _seg_rinv(kp, segk_ref[...], expk_ref[...], invk_ref[...]) * gk_ref[...]
        kpe = kpe_ref[...]
        r_pe = lax.rsqrt(jnp.sum(kpe * kpe, axis=-1, keepdims=True) * (1.0 / ROPE) + EPS)
        kpe_r = _rope(kpe * r_pe * gkpe_ref[...], cos_q1, sin_q1, ROPE // 2)
        km_ref[...] = (kn + _tile_lanes(kpe_r, HEADS)).astype(BF16)
        vm_ref[...] = kv[:, hw:].astype(BF16)

        qd = qd_ref[...]
        qdn = qd * _seg_rinv(qd, segd_ref[...], expd_ref[...], invd_ref[...]) * gdq_ref[...]
        qdo_ref[...] = _rope(qdn, cos_d, sin_d, DIL_DIM // 2).astype(BF16)
        kd = kd_ref[...]
        kdn = kd * _seg_rinv(kd, segd_ref[...], expd_ref[...], invd_ref[...]) * gdk_ref[...]
        kdo_ref[...] = _rope(kdn, cos_d, sin_d, DIL_DIM // 2).astype(BF16)
        vdo_ref[...] = vd_ref[...].astype(BF16)

    t = ROW_TILE
    row = lambda w, cb=0: pl.BlockSpec((t, w), lambda i: (i, cb))
    c = consts
    return pl.pallas_call(
        body, name="attn_prep", grid=(s // t,),
        in_specs=[row(hw), row(hw + DIL_W), row(LANE, P_KPE // LANE), row(DIL_W, P_QD // DIL_W), row(DIL_W, P_KD // DIL_W),
                  row(DIL_W, P_VD // DIL_W), row(4 * LANE),
                  _full((1, hw)), _full((1, hw)), _full((1, LANE)), _full((1, DIL_W)), _full((1, DIL_W)),
                  _full((hw, LANE)), _full((LANE, hw)), _full((1, LANE)), _full((hw, LANE)), _full((LANE, hw)), _full((1, LANE)),
                  _full((DIL_W, LANE)), _full((LANE, DIL_W)), _full((1, LANE))],
        out_specs=[row(hw), row(hw), row(DIL_W), row(DIL_W), row(DIL_W), row(DIL_W)],
        out_shape=[jax.ShapeDtypeStruct((s, hw), BF16), jax.ShapeDtypeStruct((s, hw), BF16)]
        + [jax.ShapeDtypeStruct((s, DIL_W), BF16)] * 4,
        compiler_params=_params(("parallel",), 24 << 20),
    )(*_in_hbm(q_raw, kv_raw, proj, proj, proj, proj), tab, gains["q"], gains["k"], gains["kpe"], gains["dq"], gains["dk"],
      c["seg_q"], c["exp_q"], c["inv_q"], c["seg_k"], c["exp_k"], c["inv_k"], c["seg_d"], c["exp_d"], c["inv_d"])


def _attn_prep_bwd(dqm, dkm, dvm, dqd, dkd, dvd, q_raw, kv_raw, proj, tab, gains, consts):
    s = q_raw.shape[0]
    hw = HEADS * LANE
    n_steps = s // ROW_TILE

    def body(dqm_ref, dkm_ref, dvm_ref, dqd_ref, dkd_ref, dvd_ref, q_ref, kv_ref, kpe_ref, qd_ref, kd_ref, tab_ref,
             gq_ref, gk_ref, gkpe_ref, gdq_ref, gdk_ref,
             segq_ref, expq_ref, invq_ref, segk_ref, expk_ref, invk_ref, segd_ref, expd_ref, invd_ref, foldq_ref, foldd_ref,
             dq_ref, dkv_ref, dkpe_ref, dqdo_ref, dkdo_ref, dvdo_ref, dg_ref, acc_ref):
        i = pl.program_id(0)

        @pl.when(i == 0)
        def _():
            acc_ref[...] = jnp.zeros_like(acc_ref)

        tab_v = tab_ref[...]
        cos_d, sin_d = _tile_lanes(tab_v[:, 0:LANE], DIL_W // LANE), _tile_lanes(tab_v[:, LANE:2 * LANE], DIL_W // LANE)
        cos_q1, sin_q1 = tab_v[:, 2 * LANE:3 * LANE], tab_v[:, 3 * LANE:4 * LANE]
        cos_q, sin_q = _tile_lanes(cos_q1, HEADS), _tile_lanes(sin_q1, HEADS)

        def norm_bwd(x, dyg, gain, seg, exp, inv):
            rinv = _seg_rinv(x, seg, exp, inv)
            xn = x * rinv
            dxn = dyg * gain
            dx = rinv * (dxn - xn * _seg_mean(dxn * xn, seg, exp, inv))
            return dx, jnp.sum(dyg * xn, axis=0, keepdims=True)

        dq, gq_l = norm_bwd(q_ref[...], _rope_bwd(dqm_ref[...], cos_q, sin_q, ROPE // 2), gq_ref[...],
                            segq_ref[...], expq_ref[...], invq_ref[...])
        dq_ref[...] = dq.astype(BF16)

        dkm = dkm_ref[...]
        kv = kv_ref[...]
        dkp, gk_l = norm_bwd(kv[:, :hw], dkm, gk_ref[...], segk_ref[...], expk_ref[...], invk_ref[...])
        dkv_ref[:, :hw] = dkp.astype(BF16)
        dkv_ref[:, hw:] = dvm_ref[...].astype(BF16)

        dkpe_r = dkm[:, 0:LANE]
        for h in range(1, HEADS):
            dkpe_r = dkpe_r + dkm[:, h * LANE:(h + 1) * LANE]
        dkpe_r = jnp.where(_pe_lane_mask(LANE), dkpe_r, 0.0)
        dyg = _rope_bwd(dkpe_r, cos_q1, sin_q1, ROPE // 2)
        kpe = kpe_ref[...]
        r_pe = lax.rsqrt(jnp.sum(kpe * kpe, axis=-1, keepdims=True) * (1.0 / ROPE) + EPS)
        xn = kpe * r_pe
        dxn = dyg * gkpe_ref[...]
        dkpe = r_pe * (dxn - xn * (jnp.sum(dxn * xn, axis=-1, keepdims=True) * (1.0 / ROPE)))
        dkpe_ref[...] = dkpe.astype(BF16)
        gkpe_l = jnp.sum(dyg * xn, axis=0, keepdims=True)

        dqd_v, gdq_l = norm_bwd(qd_ref[...], _rope_bwd(dqd_ref[...], cos_d, sin_d, DIL_DIM // 2), gdq_ref[...],
                                segd_ref[...], expd_ref[...], invd_ref[...])
        dqdo_ref[...] = dqd_v.astype(BF16)
        dkd_v, gdk_l = norm_bwd(kd_ref[...], _rope_bwd(dkd_ref[...], cos_d, sin_d, DIL_DIM // 2), gdk_ref[...],
                                segd_ref[...], expd_ref[...], invd_ref[...])
        dkdo_ref[...] = dkd_v.astype(BF16)
        dvdo_ref[...] = dvd_ref[...].astype(BF16)

        acc_ref[0:1, :] += gq_l
        acc_ref[1:2, :] += gk_l
        acc_ref[2:3, 0:LANE] += gkpe_l
        acc_ref[3:4, 0:DIL_W] += gdq_l
        acc_ref[4:5, 0:DIL_W] += gdk_l

        @pl.when(i == n_steps - 1)
        def _():
            acc = acc_ref[...]
            fq = jnp.dot(acc, foldq_ref[...], precision=HIGHEST, preferred_element_type=F32)
            fd = jnp.dot(acc[:, 0:DIL_W], foldd_ref[...], precision=HIGHEST, preferred_element_type=F32)
            rows = lax.broadcasted_iota(I32, (8, LANE), 0)
            base = jnp.where(rows < 2, fq, jnp.where(rows == 2, acc[:, 0:LANE], fd))
            at0 = pltpu.roll(base, LANE - KPE_OFF, 1)
            dg_ref[...] = jnp.where(rows == 5, pltpu.roll(at0, 5, 0), jnp.where(rows == 2, at0, base))

    t = ROW_TILE
    row = lambda w, cb=0: pl.BlockSpec((t, w), lambda i: (i, cb))
    c = consts
    return pl.pallas_call(
        body, name="attn_prep_bwd", grid=(n_steps,),
        in_specs=[row(hw), row(hw), row(DIL_W), row(DIL_W), row(DIL_W), row(DIL_W),
                  row(hw), row(hw + DIL_W), row(LANE, P_KPE // LANE), row(DIL_W, P_QD // DIL_W), row(DIL_W, P_KD // DIL_W),
                  row(4 * LANE),
                  _full((1, hw)), _full((1, hw)), _full((1, LANE)), _full((1, DIL_W)), _full((1, DIL_W)),
                  _full((hw, LANE)), _full((LANE, hw)), _full((1, LANE)), _full((hw, LANE)), _full((LANE, hw)), _full((1, LANE)),
                  _full((DIL_W, LANE)), _full((LANE, DIL_W)), _full((1, LANE)), _full((hw, LANE)), _full((DIL_W, LANE))],
        out_specs=[row(hw), row(hw + DIL_W), row(LANE), row(DIL_W), row(DIL_W), row(DIL_W), _full((8, LANE))],
        out_shape=[jax.ShapeDtypeStruct((s, hw), BF16), jax.ShapeDtypeStruct((s, hw + DIL_W), BF16),
                   jax.ShapeDtypeStruct((s, LANE), BF16)] + [jax.ShapeDtypeStruct((s, DIL_W), BF16)] * 3
        + [jax.ShapeDtypeStruct((8, LANE), F32)],
        scratch_shapes=[pltpu.VMEM((8, hw), F32)],
        compiler_params=_params(("arbitrary",), 28 << 20),
    )(*_in_hbm(dqm, dkm, dvm, dqd, dkd, dvd, q_raw, kv_raw, proj, proj, proj), tab,
      gains["q"], gains["k"], gains["kpe"], gains["dq"], gains["dk"],
      c["seg_q"], c["exp_q"], c["inv_q"], c["seg_k"], c["exp_k"], c["inv_k"], c["seg_d"], c["exp_d"], c["inv_d"],
      c["fold_q"], c["fold_d"])


def _latnorm_bwd(dql, dkvl, proj, g_q, g_kv):
    s = proj.shape[0]
    n_steps = s // NORM_TILE

    def body(dql_ref, dkvl_ref, q_ref, kv_ref, gq_ref, gkv_ref, dq_ref, dkv_ref, dg_ref):
        i = pl.program_id(0)

        @pl.when(i == 0)
        def _():
            dg_ref[...] = jnp.zeros_like(dg_ref)

        def one(x, dyg, gain):
            r = _rms(x)
            xn = x * r
            dxn = dyg * gain
            dx = r * (dxn - xn * jnp.mean(dxn * xn, axis=-1, keepdims=True))
            return dx, jnp.sum(dyg * xn, axis=0, keepdims=True)

        dq, gq_l = one(q_ref[...], dql_ref[...], gq_ref[...])
        dkv, gkv_l = one(kv_ref[...], dkvl_ref[...], gkv_ref[...])
        dq_ref[...] = dq.astype(BF16)
        dkv_ref[...] = dkv.astype(BF16)
        dg_ref[0:1, :] += gq_l
        dg_ref[1:2, 0:KV_LORA] += gkv_l

    t = NORM_TILE
    return pl.pallas_call(
        body, name="latnorm_bwd", grid=(n_steps,),
        in_specs=[pl.BlockSpec((t, Q_LORA), lambda i: (i, 0)), pl.BlockSpec((t, KV_LORA), lambda i: (i, 0)),
                  pl.BlockSpec((t, Q_LORA), lambda i: (i, P_QLAT // Q_LORA)),
                  pl.BlockSpec((t, KV_LORA), lambda i: (i, P_KVLAT // KV_LORA)),
                  _full((1, Q_LORA)), _full((1, KV_LORA))],
        out_specs=[pl.BlockSpec((t, Q_LORA), lambda i: (i, 0)), pl.BlockSpec((t, KV_LORA), lambda i: (i, 0)), _full((8, Q_LORA))],
        out_shape=[jax.ShapeDtypeStruct((s, Q_LORA), BF16), jax.ShapeDtypeStruct((s, KV_LORA), BF16),
                   jax.ShapeDtypeStruct((8, Q_LORA), F32)],
        compiler_params=_params(("arbitrary",)),
    )(dql, dkvl, proj, proj, g_q, g_kv)


def _resid_prenorm(x, mix, g1, gain, scale, shift):
    s, d = x.shape

    def body(x_ref, mix_ref, g1_ref, g_ref, sc_ref, sh_ref, x1_ref, h_ref):
        x1 = x_ref[...] + g1_ref[...] * mix_ref[...]
        x1_ref[...] = x1
        h_ref[...] = ((x1 * _rms(x1)) * g_ref[...] * (1.0 + sc_ref[...]) + sh_ref[...]).astype(BF16)

    row = pl.BlockSpec((NORM_TILE, d), lambda i: (i, 0))
    vec = _full((1, d))
    return pl.pallas_call(
        body, name="resid_prenorm", grid=(s // NORM_TILE,),
        in_specs=[row, row, vec, vec, vec, vec], out_specs=[row, row],
        out_shape=[jax.ShapeDtypeStruct((s, d), F32), jax.ShapeDtypeStruct((s, d), BF16)],
        compiler_params=_params(("parallel",)),
    )(x, mix, g1, gain, scale, shift)


CONV_TILE = 1408
HALO = 8


def _shift_down(x, halo, k):
    t = x.shape[0]
    row = lax.broadcasted_iota(I32, (t, 1), 0)
    out = pltpu.roll(x, k, 0)
    for r in range(k):
        out = jnp.where(row == r, halo[HALO - k + r:HALO - k + r + 1, :], out)
    return out


def _shift_up(x, halo, k):
    t = x.shape[0]
    row = lax.broadcasted_iota(I32, (t, 1), 0)
    out = pltpu.roll(x, t - k, 0)
    for r in range(k):
        out = jnp.where(row == t - k + r, halo[r:r + 1, :], out)
    return out


def _conv_fwd(x, halo, w, b):
    p1, p2 = _shift_down(x, halo, 1), _shift_down(x, halo, 2)
    u = b + p2 * w[0:1, :]
    u = u + p1 * w[1:2, :]
    u = u + x * w[2:3, :]
    return u, p1, p2


def _sigmoid(x):
    return 0.5 * jnp.tanh(0.5 * x) + 0.5


def _conv_gate(up, w_conv, b_conv):
    s = up.shape[0]
    t = ROW_TILE
    nj = D_FF // CONV_TILE
    hb = t // HALO

    def body(g_ref, v_ref, gh_ref, vh_ref, wg_ref, wv_ref, bg_ref, bv_ref, a_ref):
        live = (pl.program_id(0) > 0).astype(F32)
        ug, _, _ = _conv_fwd(g_ref[...], gh_ref[...] * live, wg_ref[...], bg_ref[...])
        uv, _, _ = _conv_fwd(v_ref[...], vh_ref[...] * live, wv_ref[...], bv_ref[...])
        a_ref[...] = (ug * _sigmoid(ug) * uv).astype(BF16)

    main = lambda off: pl.BlockSpec((t, CONV_TILE), lambda i, j: (i, j + off))
    halo = lambda off: pl.BlockSpec((HALO, CONV_TILE), lambda i, j: (jnp.maximum(i * hb - 1, 0), j + off))
    wsp = lambda off: pl.BlockSpec((3, CONV_TILE), lambda i, j: (0, j + off))
    bsp = lambda off: pl.BlockSpec((1, CONV_TILE), lambda i, j: (0, j + off))
    return pl.pallas_call(
        body, name="conv_gate", grid=(s // t, nj),
        in_specs=[main(0), main(nj), halo(0), halo(nj), wsp(0), wsp(nj), bsp(0), bsp(nj)],
        out_specs=pl.BlockSpec((t, CONV_TILE), lambda i, j: (i, j)),
        out_shape=jax.ShapeDtypeStruct((s, D_FF), BF16),
        compiler_params=_params(("parallel", "parallel"), 12 << 20),
    )(up, up, up, up, w_conv, w_conv, b_conv, b_conv)


def _gate_bwd(up, da, w_conv, b_conv):
    s = up.shape[0]
    t = ROW_TILE
    nj = D_FF // CONV_TILE
    hb = t // HALO
    n_i = s // t

    def body(g_ref, v_ref, gh_ref, vh_ref, gn_ref, vn_ref, da_ref, dan_ref, wg_ref, wv_ref, bg_ref, bv_ref,
             dupg_ref, dupv_ref, dbg_ref, dbv_ref, dwg_ref, dwv_ref):
        i = pl.program_id(1)

        @pl.when(i == 0)
        def _():
            for r in (dbg_ref, dbv_ref, dwg_ref, dwv_ref):
                r[...] = jnp.zeros_like(r)

        def d_gate(ug, uv, da_v):
            sg = _sigmoid(ug)
            return da_v * uv * (sg * (1.0 + ug * (1.0 - sg))), da_v * (ug * sg)

        live = (i > 0).astype(F32)
        xg, xv = g_ref[...], v_ref[...]
        wg, wv = wg_ref[...], wv_ref[...]
        ug, g1, g2 = _conv_fwd(xg, gh_ref[...] * live, wg, bg_ref[...])
        uv, v1, v2 = _conv_fwd(xv, vh_ref[...] * live, wv, bv_ref[...])
        dug, duv = d_gate(ug, uv, da_ref[...])

        more = (i < n_i - 1).astype(F32)
        ug_n, _, _ = _conv_fwd(gn_ref[...], xg[t - HALO:, :], wg, bg_ref[...])
        uv_n, _, _ = _conv_fwd(vn_ref[...], xv[t - HALO:, :], wv, bv_ref[...])
        dug_n, duv_n = d_gate(ug_n, uv_n, dan_ref[...] * more)

        def conv_t(du, du_n, w):
            return du * w[2:3, :] + _shift_up(du, du_n, 1) * w[1:2, :] + _shift_up(du, du_n, 2) * w[0:1, :]

        dupg_ref[...] = conv_t(dug, dug_n, wg).astype(BF16)
        dupv_ref[...] = conv_t(duv, duv_n, wv).astype(BF16)
        csum = lambda z: jnp.sum(z, axis=0, keepdims=True)
        dbg_ref[...] += csum(dug)
        dbv_ref[...] += csum(duv)
        dwg_ref[0:1, :] += csum(dug * g2)
        dwg_ref[1:2, :] += csum(dug * g1)
        dwg_ref[2:3, :] += csum(dug * xg)
        dwv_ref[0:1, :] += csum(duv * v2)
        dwv_ref[1:2, :] += csum(duv * v1)
        dwv_ref[2:3, :] += csum(duv * xv)

    last_halo = s // HALO - 1
    main = lambda off: pl.BlockSpec((t, CONV_TILE), lambda j, i: (i, j + off))
    halo = lambda off: pl.BlockSpec((HALO, CONV_TILE), lambda j, i: (jnp.maximum(i * hb - 1, 0), j + off))
    nxt = lambda off: pl.BlockSpec((HALO, CONV_TILE), lambda j, i: (jnp.minimum((i + 1) * hb, last_halo), j + off))
    wsp = lambda off: pl.BlockSpec((3, CONV_TILE), lambda j, i: (0, j + off))
    bsp = lambda off: pl.BlockSpec((1, CONV_TILE), lambda j, i: (0, j + off))
    outs = pl.pallas_call(
        body, name="gate_bwd", grid=(nj, n_i),
        in_specs=[main(0), main(nj), halo(0), halo(nj), nxt(0), nxt(nj), main(0), nxt(0),
                  wsp(0), wsp(nj), bsp(0), bsp(nj)],
        out_specs=[main(0), main(0),
                   pl.BlockSpec((1, CONV_TILE), lambda j, i: (0, j)), pl.BlockSpec((1, CONV_TILE), lambda j, i: (0, j)),
                   pl.BlockSpec((3, CONV_TILE), lambda j, i: (0, j)), pl.BlockSpec((3, CONV_TILE), lambda j, i: (0, j))],
        out_shape=[jax.ShapeDtypeStruct((s, D_FF), BF16), jax.ShapeDtypeStruct((s, D_FF), BF16),
                   jax.ShapeDtypeStruct((1, D_FF), F32), jax.ShapeDtypeStruct((1, D_FF), F32),
                   jax.ShapeDtypeStruct((3, D_FF), F32), jax.ShapeDtypeStruct((3, D_FF), F32)],
        compiler_params=_params(("parallel", "arbitrary"), 24 << 20),
    )(up, up, up, up, up, up, da, da, w_conv, w_conv, b_conv, b_conv)
    return outs


def _final(x1, ffn, tgt, g2):
    s, d = x1.shape
    n_steps = s // NORM_TILE

    def body(x1_ref, f_ref, t_ref, g2_ref, dy_ref, df_ref, dg2_ref, loss_ref, lacc_ref):
        i = pl.program_id(0)

        @pl.when(i == 0)
        def _():
            dg2_ref[...] = jnp.zeros_like(dg2_ref)
            lacc_ref[...] = jnp.zeros_like(lacc_ref)

        f = f_ref[...]
        e = x1_ref[...] + g2_ref[...] * f - t_ref[...]
        dy = e * (1.0 / d)
        dy_ref[...] = dy
        df_ref[...] = (dy * g2_ref[...]).astype(BF16)
        dg2_ref[...] += jnp.sum(dy * f, axis=0, keepdims=True)
        lacc_ref[...] += jnp.sum(e * e, axis=0, keepdims=True)

        @pl.when(i == n_steps - 1)
        def _():
            loss_ref[...] = jnp.sum(lacc_ref[...], axis=1, keepdims=True) * (0.5 / d)

    row = pl.BlockSpec((NORM_TILE, d), lambda i: (i, 0))
    return pl.pallas_call(
        body, name="final", grid=(n_steps,),
        in_specs=[row, row, row, _full((1, d))],
        out_specs=[row, row, _full((1, d)), _full((1, 1))],
        out_shape=[jax.ShapeDtypeStruct((s, d), F32), jax.ShapeDtypeStruct((s, d), BF16),
                   jax.ShapeDtypeStruct((1, d), F32), jax.ShapeDtypeStruct((1, 1), F32)],
        scratch_shapes=[pltpu.VMEM((1, d), F32)],
        compiler_params=_params(("arbitrary",)),
    )(x1, ffn, tgt, g2)


def _ffnnorm_bwd(dh2, x1, dy, mix, gain, scale, g1):
    s, d = x1.shape
    n_steps = s // NORM_TILE

    def body(dh_ref, x_ref, dy_ref, mix_ref, g_ref, sc_ref, g1_ref, dx_ref, dm_ref, acc_ref):
        i = pl.program_id(0)

        @pl.when(i == 0)
        def _():
            acc_ref[...] = jnp.zeros_like(acc_ref)

        dh, x = dh_ref[...], x_ref[...]
        r = _rms(x)
        xn = x * r
        dn = dh * (1.0 + sc_ref[...])
        dxn = dn * g_ref[...]
        dx = dy_ref[...] + r * (dxn - xn * jnp.mean(dxn * xn, axis=-1, keepdims=True))
        dx_ref[...] = dx
        dm_ref[...] = (dx * g1_ref[...]).astype(BF16)
        csum = lambda z: jnp.sum(z, axis=0, keepdims=True)
        acc_ref[0:1, :] += csum(dh)
        acc_ref[1:2, :] += csum(dh * (xn * g_ref[...]))
        acc_ref[2:3, :] += csum(dn * xn)
        acc_ref[3:4, :] += csum(dx * mix_ref[...])

    row = pl.BlockSpec((NORM_TILE, d), lambda i: (i, 0))
    vec = _full((1, d))
    return pl.pallas_call(
        body, name="ffnnorm_bwd", grid=(n_steps,),
        in_specs=[row, row, row, row, vec, vec, vec],
        out_specs=[row, row, _full((8, d))],
        out_shape=[jax.ShapeDtypeStruct((s, d), F32), jax.ShapeDtypeStruct((s, d), BF16), jax.ShapeDtypeStruct((8, d), F32)],
        compiler_params=_params(("arbitrary",)),
    )(dh2, x1, dy, mix, gain, scale, g1)


def _mixnorm_bwd(dh, x, dx1, gain, scale):
    s, d = x.shape
    n_steps = s // NORM_TILE

    def body(dh_ref, x_ref, dx1_ref, g_ref, sc_ref, gx_ref, acc_ref):
        i = pl.program_id(0)

        @pl.when(i == 0)
        def _():
            acc_ref[...] = jnp.zeros_like(acc_ref)

        dh, x = dh_ref[...], x_ref[...]
        r = _rms(x)
        xn = x * r
        dn = dh * (1.0 + sc_ref[...])
        dxn = dn * g_ref[...]
        gx_ref[...] = dx1_ref[...] + r * (dxn - xn * jnp.mean(dxn * xn, axis=-1, keepdims=True))
        csum = lambda z: jnp.sum(z, axis=0, keepdims=True)
        acc_ref[0:1, :] += csum(dh)
        acc_ref[1:2, :] += csum(dh * (xn * g_ref[...]))
        acc_ref[2:3, :] += csum(dn * xn)

    row = pl.BlockSpec((NORM_TILE, d), lambda i: (i, 0))
    vec = _full((1, d))
    return pl.pallas_call(
        body, name="mixnorm_bwd", grid=(n_steps,),
        in_specs=[row, row, row, vec, vec],
        out_specs=[row, _full((8, d))],
        out_shape=[jax.ShapeDtypeStruct((s, d), F32), jax.ShapeDtypeStruct((8, d), F32)],
        compiler_params=_params(("arbitrary",)),
    )(dh, x, dx1, gain, scale)


def _key_count(d, dilated):
    if not dilated:
        return jnp.where(d >= 0, 1.0, 0.0)
    one = lambda cond: jnp.where(cond, 1.0, 0.0)
    cnt = one(d <= 128) + one(((d & 3) == 0) & (d <= 512)) + one((d & 15) == 0)
    return jnp.where(d >= 0, cnt, 0.0)


def _block_kinds(mla):
    return (0, "diag", "none") if mla else (NEAR_REACH, "near", "far")


NEAR_REACH = 512


def _near_offsets(tk, tq):
    return (NEAR_REACH - (tk - tq)) // tk + 1


def _scores_t(ka, qa, scale, kind, rel_t, offset, near_tabs=None):
    return _mask_scores(lax.dot_general(ka, qa, NT, preferred_element_type=F32), scale, kind, rel_t, offset, near_tabs)


def _fill_near_tables(bias_ref, cnt_ref, rel_t):
    tk, tq = rel_t.shape
    for idx in range(_near_offsets(tk, tq)):
        cnt = _key_count(rel_t + (tk - tq) + idx * tk, True)
        cnt_ref[idx] = cnt
        bias_ref[idx] = jnp.where(cnt > 0.0, 0.0, NEG_INF)


def _mask_scores(products, scale, kind, rel_t, offset, near_tabs=None):
    st = products * (scale * LOG2E)
    cnt = None
    if kind == "diag":
        st = jnp.where(rel_t + offset >= 0, st, NEG_INF)
    elif kind == "far":
        st = jnp.where((rel_t & 15) == 0, st, NEG_INF)
    elif kind == "near":
        bias_ref, cnt_ref = near_tabs
        tk, tq = rel_t.shape
        idx = (offset - (tk - tq)) // tk
        st = st + bias_ref[idx]
        cnt = cnt_ref[idx]
    return st, cnt


def _attn_fwd(q, k, v, mla, scale, name, gather=()):
    s = q.shape[0]
    qw = 2 * LANE if mla else LANE
    tq, tk = ATT_TQ, ATT_TK
    reach, kind_near, kind_far = _block_kinds(mla)
    assert s % tq == 0 and tq % tk == 0 and reach % tk == 0 and reach in (0, NEAR_REACH)
    ng = len(gather)
    last_step = HEADS // 2 - 1

    def body(*refs):
        q_ref, k_ref, v_ref = refs[:3]
        o_ref, lse_ref = refs[3 + ng:5 + ng]
        vt_ref, st_ref = refs[5 + 2 * ng:7 + 2 * ng]
        near_tabs = None if mla else refs[7 + 2 * ng:9 + 2 * ng]
        n_tabs = 0 if mla else 2
        comm = (refs[3:3 + ng], refs[5 + ng:5 + 2 * ng]) + tuple(refs[7 + n_tabs + 2 * ng:])
        if ng:
            @pl.when(pl.program_id(0) == 0)
            def _():
                _Gather(*comm).start()

            @pl.when(pl.program_id(0) == last_step)
            def _():
                _Gather(*comm).forward()

        lane = lax.broadcasted_iota(I32, (1, LANE), 1)
        rel_t = lax.broadcasted_iota(I32, (tk, tq), 1) - lax.broadcasted_iota(I32, (tk, tq), 0)
        if not mla:
            _fill_near_tables(*near_tabs, rel_t)

        def transpose_v(j, carry):
            c0 = pl.multiple_of(j * tk, tk)
            vt_ref[:, pl.ds(c0, tk)] = v_ref[pl.ds(c0, tk), :].astype(F32).T.astype(BF16)
            return carry

        lax.fori_loop(0, s // tk, transpose_v, 0)

        def q_block(qi, carry):
            r0 = pl.multiple_of(qi * tq, tq)
            kcols = [slice(a * LANE, (a + 1) * LANE) if mla else slice(0, LANE) for a in range(2)]
            qas = [q_ref[pl.ds(r0, tq), kcols[a]] for a in range(2)]
            if not mla:
                qas = [jnp.where(lane < DIL_DIM, qas[0], jnp.zeros_like(qas[0])),
                       jnp.where(lane >= DIL_DIM, qas[1], jnp.zeros_like(qas[1]))]

            n_k = (r0 + tq) // tk

            def products(kj):
                c0 = pl.multiple_of(kj * tk, tk)
                return [lax.dot_general(k_ref[pl.ds(c0, tk), kcols[a]], qas[a], NT, preferred_element_type=F32)
                        for a in range(2)]

            for a, pr in enumerate(products(0)):
                st_ref[0, a] = pr

            def k_block(kj, c, kind):
                c0 = pl.multiple_of(kj * tk, tk)
                slot = kj & 1
                ahead = products(jnp.minimum(kj + 1, n_k - 1))
                out = []
                for a in range(2):
                    m, l, acc = c[a]
                    st, cnt = _mask_scores(st_ref[slot, a], scale, kind, rel_t, r0 - c0, near_tabs)
                    st_ref[1 - slot, a] = ahead[a]
                    m_new = jnp.maximum(m, jnp.max(st, axis=0, keepdims=True))
                    alpha = jnp.exp2(m - m_new)
                    p = jnp.exp2(st - m_new)
                    if cnt is not None:
                        p = p * cnt
                    l = alpha * l + jnp.sum(p, axis=0, keepdims=True)
                    vt = vt_ref[a * DIL_DIM:(a + 1) * DIL_DIM, pl.ds(c0, tk)]
                    acc = alpha * acc + jnp.dot(vt, p.astype(BF16), preferred_element_type=F32)
                    out.append((m_new, l, acc))
                return tuple(out)

            one = (jnp.full((1, tq), NEG_INF, F32), jnp.zeros((1, tq), F32), jnp.zeros((DIL_DIM, tq), F32))
            first_near = jnp.maximum((r0 - reach) // tk, 0)
            c = lax.fori_loop(0, first_near, functools.partial(k_block, kind=kind_far), (one, one))
            res = lax.fori_loop(first_near, (r0 + tq) // tk, functools.partial(k_block, kind=kind_near), c)
            o_t = jnp.concatenate([res[a][2] / res[a][1] for a in range(2)], axis=0)
            o_ref[pl.ds(r0, tq), :] = o_t.T.astype(BF16)
            for a in range(2):
                lse_ref[a, :, pl.ds(r0, tq)] = res[a][0] * LN2 + jnp.log(res[a][1])
            return carry

        lax.fori_loop(0, s // tq, q_block, 0)

        if ng:
            @pl.when(pl.program_id(0) == last_step)
            def _():
                _Gather(*comm).finish()

    return pl.pallas_call(
        body, name=name, grid=(HEADS // 2,),
        in_specs=[pl.BlockSpec((s, qw), lambda h: (0, h)), pl.BlockSpec((s, qw), lambda h: (0, h)),
                  pl.BlockSpec((s, LANE), lambda h: (0, h))] + [ANY] * ng,
        out_specs=[pl.BlockSpec((s, LANE), lambda h: (0, h)), pl.BlockSpec((2, 1, s), lambda h: (h, 0, 0))] + [ANY] * ng,
        out_shape=[jax.ShapeDtypeStruct((s, DIL_W), BF16), jax.ShapeDtypeStruct((HEADS, 1, s), F32)] + _Gather.out_shapes(gather),
        scratch_shapes=[pltpu.VMEM((LANE, s), BF16), pltpu.VMEM((2, 2, tk, tq), F32)]
        + ([] if mla else [pltpu.VMEM((_near_offsets(tk, tq), tk, tq), F32)] * 2) + (_Gather.scratch(gather) if ng else []),
        compiler_params=_params(("arbitrary",) if ng else ("parallel",), 12 << 20),
    )(*_in_hbm(q, k, v), *gather)


def _attn_bwd(q, k, v, o, do, do_block0, lse, mla, scale, name, scatter=()):
    s = q.shape[0]
    qw = 2 * LANE if mla else LANE
    tq, tk = ATT_TQ, ATT_TK_BWD
    nq = s // tq
    reach, kind_near, kind_far = _block_kinds(mla)
    assert s % tq == 0 and s % tk == 0
    ns = len(scatter)
    last_step = HEADS // 2 - 1

    def body(*refs):
        q_ref, k_ref, v_ref, o_ref, do_ref, lse_ref = refs[:6]
        dq_ref, dk_ref, dv_ref = refs[6 + ns:9 + ns]
        kt_ref, dot_ref, dob_ref, dqt_ref, delta_ref, lse2_ref = refs[9 + 2 * ns:15 + 2 * ns]
        near_tabs = None if mla else refs[15 + 2 * ns:17 + 2 * ns]
        n_tabs = 0 if mla else 2
        comm = (refs[6:6 + ns], refs[9 + ns:9 + 2 * ns]) + tuple(refs[15 + n_tabs + 2 * ns:])
        if ns:
            @pl.when(pl.program_id(0) == 0)
            def _():
                _Scatter(*comm).start()

        lane = lax.broadcasted_iota(I32, (1, LANE), 1)
        row = lax.broadcasted_iota(I32, (LANE, 1), 0)
        rel_t = lax.broadcasted_iota(I32, (tk, tq), 1) - lax.broadcasted_iota(I32, (tk, tq), 0)
        if not mla:
            _fill_near_tables(*near_tabs, rel_t)

        def prepare(j, carry):
            c0 = pl.multiple_of(j * tk, tk)
            do_blk = do_ref[pl.ds(c0, tk), :]
            dob_ref[pl.ds(c0, tk), :] = do_blk.astype(BF16)
            do_t = do_blk.T
            dot_ref[:, pl.ds(c0, tk)] = do_t.astype(BF16)
            prod = do_t * o_ref[pl.ds(c0, tk), :].astype(F32).T
            delta_ref[0, :, pl.ds(c0, tk)] = jnp.sum(prod[0:DIL_DIM], axis=0, keepdims=True)
            delta_ref[1, :, pl.ds(c0, tk)] = jnp.sum(prod[DIL_DIM:LANE], axis=0, keepdims=True)
            for w in range(qw // LANE):
                kt_ref[w * LANE:(w + 1) * LANE, pl.ds(c0, tk)] = (
                    k_ref[pl.ds(c0, tk), w * LANE:(w + 1) * LANE].astype(F32).T.astype(BF16))
            return carry

        lax.fori_loop(0, s // tk, prepare, 0)
        dqt_ref[...] = jnp.zeros_like(dqt_ref)
        lse2_ref[...] = lse_ref[...] * LOG2E

        sels = [lane < DIL_DIM, lane >= DIL_DIM]
        rsels = [row < DIL_DIM, row >= DIL_DIM]
        cols = [slice(a * LANE, (a + 1) * LANE) if mla else slice(0, LANE) for a in range(2)]

        def k_block(kj, carry):
            c0 = pl.multiple_of(kj * tk, tk)
            kas = [k_ref[pl.ds(c0, tk), cols[a]] for a in range(2)]
            kts = [kt_ref[cols[a], pl.ds(c0, tk)] for a in range(2)]
            if not mla:
                kas = [jnp.where(sels[a], kas[a], jnp.zeros_like(kas[a])) for a in range(2)]
                kts = [jnp.where(rsels[a], kts[a], jnp.zeros_like(kts[a])) for a in range(2)]
            vb = v_ref[pl.ds(c0, tk), :]
            vbs = [jnp.where(sels[a], vb, jnp.zeros_like(vb)) for a in range(2)]

            first = c0 // tq

            def q_block(qi, c, kind):
                r0 = pl.multiple_of(qi * tq, tq)
                out, dq_parts = [], []
                for a in range(2):
                    dk_acc, dv_acc = c[a]
                    qa = q_ref[pl.ds(r0, tq), cols[a]]
                    st, cnt = _scores_t(kas[a], qa, scale, kind, rel_t, r0 - c0, near_tabs)
                    p = jnp.exp2(st - lse2_ref[a, :, pl.ds(r0, tq)])
                    if cnt is not None:
                        p = p * cnt
                    dp = jnp.dot(vbs[a], dot_ref[:, pl.ds(r0, tq)], preferred_element_type=F32)
                    ds = (p * (dp - delta_ref[a, :, pl.ds(r0, tq)]) * scale).astype(BF16)
                    dv_acc = dv_acc + jnp.dot(p.astype(BF16), dob_ref[pl.ds(r0, tq), :], preferred_element_type=F32)
                    dk_acc = dk_acc + jnp.dot(ds, qa, preferred_element_type=F32)
                    dq_parts.append(jnp.dot(kts[a], ds, preferred_element_type=F32))
                    out.append((dk_acc, dv_acc))
                if mla:
                    for a in range(2):
                        dqt_ref[cols[a], pl.ds(r0, tq)] += dq_parts[a]
                else:
                    dqt_ref[:, pl.ds(r0, tq)] += dq_parts[0] + dq_parts[1]
                return tuple(out)

            zero = jnp.zeros((tk, LANE), F32)
            last_near = jnp.minimum((c0 + tk - 1 + reach) // tq + 1, nq)
            c = lax.fori_loop(first, last_near, functools.partial(q_block, kind=kind_near), ((zero, zero), (zero, zero)))
            (dk0, dv0), (dk1, dv1) = lax.fori_loop(last_near, nq, functools.partial(q_block, kind=kind_far), c)
            if mla:
                dk_ref[pl.ds(c0, tk), cols[0]] = dk0
                dk_ref[pl.ds(c0, tk), cols[1]] = dk1
            else:
                dk_ref[pl.ds(c0, tk), :] = jnp.where(sels[0], dk0, dk1)
            dv_ref[pl.ds(c0, tk), :] = jnp.where(sels[0], dv0, dv1)
            return carry

        lax.fori_loop(0, s // tk, k_block, 0)

        def write_dq(j, carry):
            c0 = pl.multiple_of(j * tk, tk)
            for w in range(qw // LANE):
                dq_ref[pl.ds(c0, tk), w * LANE:(w + 1) * LANE] = dqt_ref[w * LANE:(w + 1) * LANE, pl.ds(c0, tk)].T
            return carry

        lax.fori_loop(0, s // tk, write_dq, 0)

        if ns:
            @pl.when(pl.program_id(0) == last_step)
            def _():
                _Scatter(*comm).finish()

    b0 = do_block0
    return pl.pallas_call(
        body, name=name, grid=(HEADS // 2,),
        in_specs=[pl.BlockSpec((s, qw), lambda h: (0, h)), pl.BlockSpec((s, qw), lambda h: (0, h)),
                  pl.BlockSpec((s, LANE), lambda h: (0, h)), pl.BlockSpec((s, LANE), lambda h: (0, h)),
                  pl.BlockSpec((s, LANE), lambda h: (0, h + b0)), pl.BlockSpec((2, 1, s), lambda h: (h, 0, 0))] + [ANY] * ns,
        out_specs=[pl.BlockSpec((s, qw), lambda h: (0, h)), pl.BlockSpec((s, qw), lambda h: (0, h)),
                   pl.BlockSpec((s, LANE), lambda h: (0, h))] + [ANY] * ns,
        out_shape=[jax.ShapeDtypeStruct(q.shape, F32), jax.ShapeDtypeStruct(k.shape, F32), jax.ShapeDtypeStruct((s, DIL_W), F32)]
        + _Scatter.out_shapes(scatter),
        scratch_shapes=[pltpu.VMEM((qw, s), BF16), pltpu.VMEM((LANE, s), BF16), pltpu.VMEM((s, LANE), BF16),
                        pltpu.VMEM((qw, s), F32), pltpu.VMEM((2, 1, s), F32), pltpu.VMEM((2, 1, s), F32)]
        + ([] if mla else [pltpu.VMEM((_near_offsets(tk, tq), tk, tq), F32)] * 2) + (_Scatter.semaphores(ns) if ns else []),
        compiler_params=_params(("arbitrary",) if ns else ("parallel",), 24 << 20),
    )(*_in_hbm(q, k, v, o, do, lse), *scatter)


def _ada_bwd(c_all, dmod_shard):
    n, d = c_all.shape
    cols = dmod_shard.shape[1]

    def body(c_ref, g_ref, o_ref):
        cv = c_ref[...]
        o_ref[...] = lax.dot_general(cv * _sigmoid(cv), g_ref[...], TN, precision=HIGHEST, preferred_element_type=F32)

    return pl.pallas_call(
        body, name="ada_bwd", out_shape=jax.ShapeDtypeStruct((d, cols), F32),
        compiler_params=_params(None, 16 << 20),
    )(c_all, dmod_shard)


SMALL_WIDTHS = (("g_mix_norm", D_MODEL), ("g_q_lat", Q_LORA), ("g_kv_lat", KV_LORA), ("g_mla_q_nope", NOPE),
                ("g_mla_q_pe", ROPE), ("g_mla_k_nope", NOPE), ("g_mla_k_pe", ROPE), ("g_dil_q", DIL_DIM),
                ("g_dil_k", DIL_DIM), ("g_ffn_norm", D_MODEL), ("b_conv", UP_W))


def _small_layout():
    pieces = (("dmod", 6 * D_MODEL),) + SMALL_WIDTHS + tuple(("w_conv%d" % k, UP_W) for k in range(3)) + (("loss", 1),)
    layout, off = {}, 0
    for name, width in pieces:
        layout[name] = (width, off)
        off += -(-width // LANE) * LANE
    return layout, off


def _pack_small(acc1, acc2, dg2, dglat, dgains, dbg, dbv, dwg, dwv, loss_part):
    layout, total = _small_layout()

    def body(a1, a2, g2, gl, gg, bg, bv, wg, wv, ls, o_ref):
        o_ref[...] = jnp.zeros_like(o_ref)

        def put(name, src, shift=0):
            start = layout[name][1] + shift
            o_ref[:, start:start + src.shape[1]] = src

        for k, src in enumerate((a1[0:1, :], a1[1:2, :], a2[3:4, :], a2[0:1, :], a2[1:2, :], g2[...])):
            put("dmod", src, k * D_MODEL)
        put("g_mix_norm", a1[2:3, :])
        put("g_q_lat", gl[0:1, :])
        put("g_kv_lat", gl[1:2, 0:KV_LORA])
        put("g_mla_q_nope", gg[0:1, 0:NOPE])
        put("g_mla_q_pe", gg[5:6, 0:ROPE])
        put("g_mla_k_nope", gg[1:2, 0:NOPE])
        put("g_mla_k_pe", gg[2:3, 0:ROPE])
        put("g_dil_q", gg[3:4, 0:DIL_DIM])
        put("g_dil_k", gg[4:5, 0:DIL_DIM])
        put("g_ffn_norm", a2[2:3, :])
        put("b_conv", bg[...])
        put("b_conv", bv[...], D_FF)
        for k in range(3):
            put("w_conv%d" % k, wg[k:k + 1, :])
            put("w_conv%d" % k, wv[k:k + 1, :], D_FF)
        put("loss", ls[...])

    ins = (acc1, acc2, dg2, dglat, dgains, dbg, dbv, dwg, dwv, loss_part)
    return pl.pallas_call(
        body, name="pack_small", grid=(1,), in_specs=[_full(a.shape) for a in ins], out_specs=_full((1, total)),
        out_shape=jax.ShapeDtypeStruct((1, total), F32),
        compiler_params=_params(("arbitrary",), 2 << 20),
    )(*_in_hbm(*ins))


def _sum_unpack(g):
    n_dev, _, total = g.shape
    layout, _ = _small_layout()

    def body(g_ref, *refs):
        o_refs, s_ref = refs[:-1], refs[-1]
        acc = g_ref[0]
        for k in range(1, n_dev):
            acc = acc + g_ref[k]
        s_ref[...] = acc
        take = lambda name: s_ref[:, layout[name][1]:layout[name][1] + layout[name][0]]
        o_refs[0][...] = take("dmod")
        for i, (name, _) in enumerate(SMALL_WIDTHS):
            o_refs[1 + i][...] = take(name)
        for k in range(3):
            o_refs[-2][k:k + 1, :] = take("w_conv%d" % k)
        o_refs[-1][...] = take("loss")

    shapes = [(1, 6 * D_MODEL)] + [(1, w) for _, w in SMALL_WIDTHS] + [(3, UP_W), (1, 1)]
    return pl.pallas_call(
        body, name="sum_unpack", out_shape=[jax.ShapeDtypeStruct(sh, F32) for sh in shapes],
        scratch_shapes=[pltpu.VMEM((1, total), F32)],
        compiler_params=_params(None, 4 << 20),
    )(g)


def _adamw_math(w, g, m, v):
    mn = ADAM_B1 * m + (1.0 - ADAM_B1) * g
    vn = ADAM_B2 * v + (1.0 - ADAM_B2) * (g * g)
    m_hat = mn / (1.0 - ADAM_B1 ** ADAM_STEP)
    v_hat = vn / (1.0 - ADAM_B2 ** ADAM_STEP)
    return -ADAM_LR * (m_hat / (jnp.sqrt(v_hat) + ADAM_EPS) + ADAM_WD * w), mn, vn


def _adamw_vectors(ws, gs, ms, vs):
    k = len(ws)

    def body(*refs):
        for i in range(k):
            d, mn, vn = _adamw_math(refs[i][...], refs[k + i][...], refs[2 * k + i][...], refs[3 * k + i][...])
            refs[4 * k + i][...] = d
            refs[5 * k + i][...] = mn
            refs[6 * k + i][...] = vn

    blocks = [_full(w.shape) for w in ws]
    outs = pl.pallas_call(
        body, name="adamw_vectors", grid=(1,), in_specs=blocks * 4, out_specs=blocks * 3,
        out_shape=[jax.ShapeDtypeStruct(w.shape, F32) for w in ws] * 3,
        compiler_params=_params(("arbitrary",), 2 << 20),
    )(*_in_hbm(*ws, *gs, *ms, *vs))
    return outs[:k], outs[k:2 * k], outs[2 * k:]


def _adamw(w, g, m, v, name):
    r, c = w.shape
    tr = r
    for cand in (256, 128, 64, 32, 16):
        if r % cand == 0 and r > cand:
            tr = cand
            break

    def body(w_ref, g_ref, m_ref, v_ref, d_ref, mo_ref, vo_ref):
        d_ref[...], mo_ref[...], vo_ref[...] = _adamw_math(w_ref[...], g_ref[...], m_ref[...], v_ref[...])

    blk = pl.BlockSpec((tr, c), lambda i: (i, 0))
    return pl.pallas_call(
        body, name=name, grid=(r // tr,), in_specs=[blk] * 4, out_specs=[blk] * 3,
        out_shape=[jax.ShapeDtypeStruct((r, c), F32)] * 3,
        compiler_params=_params(("parallel",), 7 * _nbytes((tr, c), F32)),
    )(w, g, m, v)


def _position():
    return lax.axis_index("x"), lax.axis_index("y"), lax.axis_index("c")


def _other_chips(x, y):
    return [(1 - x, y, 2 * (1 - x) + y), (x, 1 - y, 2 * x + (1 - y)), (1 - x, 1 - y, 2 * (1 - x) + (1 - y))]


class _SmallGather:
    def __init__(self, v_ref, out_ref, send_sems, recv_sems, local_sem):
        x, y, c = _position()
        me = 4 * x + 2 * y + c
        self.local = pltpu.make_async_copy(v_ref, out_ref.at[me], local_sem)
        self.sends, self.arrivals = [], []
        for k in range(N_DEV - 1):
            fx, fy, fc = ((k + 1) >> 2) & 1, ((k + 1) >> 1) & 1, (k + 1) & 1
            px, py, pc = (1 - x if fx else x), (1 - y if fy else y), (1 - c if fc else c)

            def copy(dst, k=k, peer=(px, py, pc)):
                return pltpu.make_async_remote_copy(src_ref=v_ref, dst_ref=dst, send_sem=send_sems.at[k],
                                                    recv_sem=recv_sems.at[k], device_id=peer, device_id_type=MESH)

            self.sends.append(copy(out_ref.at[me]))
            self.arrivals.append(copy(out_ref.at[4 * px + 2 * py + pc]))

    @staticmethod
    def semaphores():
        return [pltpu.SemaphoreType.DMA((N_DEV - 1,)), pltpu.SemaphoreType.DMA((N_DEV - 1,)), pltpu.SemaphoreType.DMA]

    def start(self):
        self.local.start()
        for cp in self.sends:
            cp.start()

    def finish(self):
        for cp in self.arrivals:
            cp.wait_recv()
        for cp in self.sends:
            cp.wait_send()
        self.local.wait()


def _prologue(c_taps, w_ada_shard, b_shard, pos_col, rope_consts, shards):
    n = len(shards)
    s = pos_col.shape[0]
    cols = w_ada_shard.shape[1]
    freq, csel, ssel = rope_consts

    def body(*refs):
        ct_ref, w_ref, b_ref, p_ref, f_ref, cs_ref, ss_ref = refs[:7]
        sh_refs = refs[7:7 + n]
        ct_all_ref, mod_all_ref, tab_ref = refs[7 + n:10 + n]
        g_refs = refs[10 + n:10 + 2 * n]
        mod_blk_ref = refs[10 + 2 * n]
        sems = refs[11 + 2 * n:]
        weights = _Gather(sh_refs, g_refs, *sems[6:])
        weights.start()
        first = _SmallGather(ct_ref, ct_all_ref, *sems[0:3])
        first.start()
        first.finish()
        cv = ct_all_ref[:, 0, 0:D_MODEL]
        sc = (cv * _sigmoid(cv)).astype(BF16)
        mod_blk_ref[...] = jnp.dot(sc, w_ref[...].astype(BF16), preferred_element_type=F32) + b_ref[...]
        second = _SmallGather(mod_blk_ref, mod_all_ref, *sems[3:6])
        second.start()

        def table_rows(i, carry):
            r0 = pl.multiple_of(i * ROW_TILE, ROW_TILE)
            ang = p_ref[pl.ds(r0, ROW_TILE), :].astype(F32) * f_ref[...]
            tab_ref[pl.ds(r0, ROW_TILE), :] = cs_ref[...] * jnp.cos(ang) + ss_ref[...] * jnp.sin(ang)
            return carry

        lax.fori_loop(0, s // ROW_TILE, table_rows, 0)
        second.finish()
        weights.forward()
        weights.finish()

    return pl.pallas_call(
        body, name="prologue",
        out_shape=[jax.ShapeDtypeStruct((N_DEV,) + c_taps.shape, F32), jax.ShapeDtypeStruct((N_DEV, N_DEV, cols), F32),
                   jax.ShapeDtypeStruct((s, 4 * LANE), F32)] + _Gather.out_shapes(shards),
        in_specs=[IN_VMEM] * 7 + [ANY] * n, out_specs=[IN_VMEM] * 3 + [ANY] * n,
        scratch_shapes=[pltpu.VMEM((N_DEV, cols), F32)] + _SmallGather.semaphores() * 2 + _Gather.scratch(shards),
        compiler_params=_params(None, 14 << 20),
    )(c_taps, w_ada_shard, b_shard, pos_col, freq, csel, ssel, *shards)


IN_VMEM = pl.BlockSpec(memory_space=pltpu.VMEM)
ANY = pl.BlockSpec(memory_space=pl.ANY)


class _Gather:
    def __init__(self, w_refs, out_refs, send_sems, recv_sems, own_sems, *bounce_refs):
        x, y, c = _position()
        q0 = 2 * x + y
        sibling = (x, y, 1 - c)
        self.ici, self.ici_in, self.fwd, self.fwd_in, self.own_in, self.own_out = [], [], [], [], [], []
        for k, (w_ref, out_ref) in enumerate(zip(w_refs, out_refs)):
            half = w_ref.shape[0] // 2
            self.own_in.append(pltpu.make_async_copy(w_ref, bounce_refs[k], own_sems.at[2 * k]))
            self.own_out.append(pltpu.make_async_copy(bounce_refs[k], out_ref.at[q0], own_sems.at[2 * k + 1]))

            def blk(q, e, out_ref=out_ref, half=half):
                return out_ref.at[q, pl.ds(pl.multiple_of(e * half, 16), half), :]

            def copy(src, dst, i, to):
                return pltpu.make_async_remote_copy(src_ref=src, dst_ref=dst, send_sem=send_sems.at[i], recv_sem=recv_sems.at[i],
                                                    device_id=to, device_id_type=MESH)

            src = w_ref.at[pl.ds(pl.multiple_of(c * half, 16), half), :]
            for j, (cx, cy, qj) in enumerate(_other_chips(x, y)):
                self.ici.append(copy(src, blk(q0, c), 6 * k + j, (cx, cy, c)))
                self.ici_in.append(copy(blk(qj, c), blk(qj, c), 6 * k + j, (cx, cy, c)))
                self.fwd.append(copy(blk(qj, c), blk(qj, c), 6 * k + 3 + j, sibling))
                self.fwd_in.append(copy(blk(qj, 1 - c), blk(qj, 1 - c), 6 * k + 3 + j, sibling))

    @staticmethod
    def out_shapes(shards):
        return [jax.ShapeDtypeStruct((N_CHIP,) + s.shape, s.dtype) for s in shards]

    @staticmethod
    def scratch(shards):
        n = len(shards)
        return ([pltpu.SemaphoreType.DMA((6 * n,)), pltpu.SemaphoreType.DMA((6 * n,)), pltpu.SemaphoreType.DMA((2 * n,))]
                + [pltpu.VMEM(s.shape, s.dtype) for s in shards])

    def start(self):
        for cp in self.ici + self.own_in:
            cp.start()

    def forward(self):
        for fetched, placed in zip(self.own_in, self.own_out):
            fetched.wait()
            placed.start()
        for arrived, onward in zip(self.ici_in, self.fwd):
            arrived.wait_recv()
            onward.start()

    def finish(self):
        for cp in self.fwd_in:
            cp.wait_recv()
        for cp in self.ici + self.fwd:
            cp.wait_send()
        for cp in self.own_out:
            cp.wait()


class _PairSwap:
    def __init__(self, g_refs, out_refs, send_sems, recv_sems):
        x, y, c = _position()
        self.copies = [
            pltpu.make_async_remote_copy(src_ref=g_ref.at[:, 1 - c], dst_ref=out_ref, send_sem=send_sems.at[k],
                                         recv_sem=recv_sems.at[k], device_id=(x, y, 1 - c), device_id_type=MESH)
            for k, (g_ref, out_ref) in enumerate(zip(g_refs, out_refs))]

    @staticmethod
    def out_shapes(grads):
        return [jax.ShapeDtypeStruct((N_CHIP,) + g.shape[2:], g.dtype) for g in grads]

    @staticmethod
    def semaphores(n):
        return [pltpu.SemaphoreType.DMA((n,)), pltpu.SemaphoreType.DMA((n,))]

    def start(self):
        for cp in self.copies:
            cp.start()

    def finish(self):
        for cp in self.copies:
            cp.wait_recv()
        for cp in self.copies:
            cp.wait_send()


def _pair_sum(g, a, c_idx, name):
    _, _, rh, cols = g.shape
    tr = rh
    for cand in (256, 128, 64, 32, 16):
        if rh % cand == 0 and rh > cand:
            tr = cand
            break

    def body(c_ref, g_ref, a_ref, o_ref):
        o_ref[...] = (g_ref[...] + a_ref[...]).astype(BF16)

    return pl.pallas_call(
        body, name=name,
        grid_spec=pltpu.PrefetchScalarGridSpec(
            num_scalar_prefetch=1, grid=(N_CHIP, rh // tr),
            in_specs=[pl.BlockSpec((None, None, tr, cols), lambda q, i, c_ref: (q, c_ref[0], i, 0)),
                      pl.BlockSpec((None, tr, cols), lambda q, i, c_ref: (q, i, 0))],
            out_specs=pl.BlockSpec((None, tr, cols), lambda q, i, c_ref: (q, i, 0))),
        out_shape=jax.ShapeDtypeStruct((N_CHIP, rh, cols), BF16),
        compiler_params=_params(("parallel", "parallel"), 10 * _nbytes((tr, cols), F32)),
    )(c_idx, g, a)


def _scatter_and_gather(parts, small, name):
    n = len(parts)

    def body(*refs):
        scatter = _Scatter(refs[:n], refs[n + 1:2 * n + 1], *refs[2 * n + 2:2 * n + 4])
        gather = _SmallGather(refs[n], refs[2 * n + 1], *refs[2 * n + 4:])
        scatter.start()
        gather.start()
        gather.finish()
        scatter.finish()

    return pl.pallas_call(
        body, name=name,
        out_shape=_Scatter.out_shapes(parts) + [jax.ShapeDtypeStruct((N_DEV,) + small.shape, F32)],
        in_specs=[ANY] * n + [IN_VMEM], out_specs=[ANY] * n + [IN_VMEM],
        scratch_shapes=_Scatter.semaphores(n) + _SmallGather.semaphores(),
        compiler_params=_params(None, 10 * _nbytes(small.shape, F32)),
    )(*parts, small)


class _Scatter:
    def __init__(self, p_refs, out_refs, send_sems, recv_sems):
        x, y, c = _position()
        self.copies = []
        for k, (p_ref, out_ref) in enumerate(zip(p_refs, out_refs)):
            for j, (cx, cy, qj) in enumerate(_other_chips(x, y)):
                self.copies.append(pltpu.make_async_remote_copy(
                    src_ref=p_ref.at[qj], dst_ref=out_ref.at[j], send_sem=send_sems.at[3 * k + j],
                    recv_sem=recv_sems.at[3 * k + j], device_id=(cx, cy, c), device_id_type=MESH))

    @staticmethod
    def out_shapes(parts):
        return [jax.ShapeDtypeStruct((3,) + p.shape[1:], p.dtype) for p in parts]

    @staticmethod
    def semaphores(n):
        return [pltpu.SemaphoreType.DMA((3 * n,)), pltpu.SemaphoreType.DMA((3 * n,))]

    def start(self):
        for cp in self.copies:
            cp.start()

    def finish(self):
        for cp in self.copies:
            cp.wait_recv()
        for cp in self.copies:
            cp.wait_send()


def _shard_sum(p, b, qc_idx, name):
    _, rh, cols = p.shape
    tr = rh
    for cand in (256, 128, 64, 32, 16):
        if rh % cand == 0 and rh > cand:
            tr = cand
            break

    def body(qc_ref, p_ref, b_ref, o_ref):
        acc = p_ref[...].astype(F32)
        for j in range(3):
            acc = acc + b_ref[j].astype(F32)
        o_ref[...] = acc

    return pl.pallas_call(
        body, name=name,
        grid_spec=pltpu.PrefetchScalarGridSpec(
            num_scalar_prefetch=1, grid=(rh // tr,),
            in_specs=[pl.BlockSpec((None, tr, cols), lambda i, qc_ref: (qc_ref[0], i, 0)),
                      pl.BlockSpec((3, tr, cols), lambda i, qc_ref: (0, i, 0))],
            out_specs=pl.BlockSpec((None, tr, cols), lambda i, qc_ref: (qc_ref[1], i, 0))),
        out_shape=jax.ShapeDtypeStruct((2, rh, cols), F32),
        compiler_params=_params(("parallel",), 8 * _nbytes((tr, cols), F32)),
    )(qc_idx, p, b)


def _join_halves(shards):
    n = len(shards)

    def body(*refs):
        out_refs = refs[n:2 * n]
        send_sems, recv_sems = refs[2 * n:]
        x, y, c = _position()
        cps = [pltpu.make_async_remote_copy(src_ref=out_refs[k].at[c], dst_ref=out_refs[k].at[c], send_sem=send_sems.at[k],
                                            recv_sem=recv_sems.at[k], device_id=(x, y, 1 - c), device_id_type=MESH)
               for k in range(n)]
        for cp in cps:
            cp.start()
        for k in range(n):
            arriving = out_refs[k].at[1 - c]
            pltpu.make_async_remote_copy(src_ref=arriving, dst_ref=arriving, send_sem=send_sems.at[k], recv_sem=recv_sems.at[k],
                                         device_id=(x, y, 1 - c), device_id_type=MESH).wait_recv()
        for cp in cps:
            cp.wait_send()

    return pl.pallas_call(
        body, name="rs_join",
        out_shape=[jax.ShapeDtypeStruct(a.shape, a.dtype) for a in shards],
        in_specs=[ANY] * n, out_specs=[ANY] * n, input_output_aliases={k: k for k in range(n)},
        scratch_shapes=[pltpu.SemaphoreType.DMA((n,)), pltpu.SemaphoreType.DMA((n,))],
    )(*shards)


def _cols_from_shards(g):
    q, r, cs = g.shape
    return jnp.transpose(g, (1, 0, 2)).reshape(r, q * cs)


def _cols_to_shards(w):
    r, cfull = w.shape
    return jnp.transpose(w.reshape(r, N_CHIP, cfull // N_CHIP), (1, 0, 2))


def _pad_w_in(w):
    z = lambda n: jnp.zeros((w.shape[0], n), w.dtype)
    q_lat, kv_lat, kpe = w[:, 0:512], w[:, 512:768], w[:, 768:800]
    qd, kd, vd = w[:, 800:1312], w[:, 1312:1824], w[:, 1824:2336]
    return jnp.concatenate([q_lat, qd, kd, vd, kv_lat, z(KPE_OFF), kpe, z(LANE - KPE_OFF - ROPE)], axis=1)


def _pad_w_qb(w):
    w3 = w.reshape(Q_LORA, HEADS, NOPE + ROPE)
    return jnp.pad(w3, ((0, 0), (0, 0), (0, LANE - NOPE - ROPE))).reshape(Q_LORA, HEADS * LANE)


def _unpad_w_qb(g):
    return g.reshape(Q_LORA, HEADS, LANE)[:, :, :NOPE + ROPE].reshape(Q_LORA, HEADS * (NOPE + ROPE))


def _pad_w_kvb(w):
    w3 = w.reshape(KV_LORA, HEADS, 2 * NOPE)
    kp = jnp.pad(w3[:, :, :NOPE], ((0, 0), (0, 0), (0, LANE - NOPE))).reshape(KV_LORA, HEADS * LANE)
    return jnp.concatenate([kp, w3[:, :, NOPE:].reshape(KV_LORA, DIL_W)], axis=1)


def _unpad_w_kvb(g):
    gk = g[:, :HEADS * LANE].reshape(KV_LORA, HEADS, LANE)[:, :, :NOPE]
    gv = g[:, HEADS * LANE:].reshape(KV_LORA, HEADS, NOPE)
    return jnp.concatenate([gk, gv], axis=2).reshape(KV_LORA, HEADS * 2 * NOPE)


def _head_gains(g_q_nope, g_q_pe, g_k_nope, g_k_pe, g_dq, g_dk):
    z = lambda n: jnp.zeros((1, n), F32)
    q1 = jnp.concatenate([g_q_nope, g_q_pe, z(LANE - NOPE - ROPE)], axis=1)
    k1 = jnp.concatenate([g_k_nope, z(LANE - NOPE)], axis=1)
    kpe = jnp.concatenate([z(KPE_OFF), g_k_pe, z(LANE - KPE_OFF - ROPE)], axis=1)
    return dict(q=jnp.tile(q1, (1, HEADS)), k=jnp.tile(k1, (1, HEADS)), kpe=kpe,
                dq=jnp.tile(g_dq, (1, HEADS)), dk=jnp.tile(g_dk, (1, HEADS)))


def kernel(x, c, positions, w_ada, b_ada, g_mix_norm, w_in, g_q_lat, w_q_b, g_kv_lat, w_kv_b, g_mla_q_nope, g_mla_q_pe, g_mla_k_nope, g_mla_k_pe, g_dil_q, g_dil_k, w_o, g_ffn_norm, w_up, w_conv, b_conv, w_down, loss_target, m_w_ada, m_b_ada, m_g_mix_norm, m_w_in, m_g_q_lat, m_w_q_b, m_g_kv_lat, m_w_kv_b, m_g_mla_q_nope, m_g_mla_q_pe, m_g_mla_k_nope, m_g_mla_k_pe, m_g_dil_q, m_g_dil_k, m_w_o, m_g_ffn_norm, m_w_up, m_w_conv, m_b_conv, m_w_down, v_w_ada, v_b_ada, v_g_mix_norm, v_w_in, v_g_q_lat, v_w_q_b, v_g_kv_lat, v_w_kv_b, v_g_mla_q_nope, v_g_mla_q_pe, v_g_mla_k_nope, v_g_mla_k_pe, v_g_dil_q, v_g_dil_k, v_w_o, v_g_ffn_norm, v_w_up, v_w_conv, v_b_conv, v_w_down):
    args = dict(locals())
    weights = {n: args[n][0] for n in ("w_ada", "w_in", "w_q_b", "w_kv_b", "w_o", "w_up", "w_conv", "w_down")}
    small_w = {n: args[n] for n in ("b_ada",) + tuple(n for n, _ in SMALL_WIDTHS)}
    mom_m = {n[2:]: (args[n][0] if args[n].ndim == 3 else args[n]) for n in args if n.startswith("m_")}
    mom_v = {n[2:]: (args[n][0] if args[n].ndim == 3 else args[n]) for n in args if n.startswith("v_")}

    xi, yi, ci = _position()
    q0 = 2 * xi + yi
    me = 4 * xi + 2 * yi + ci
    xs, tgt = x[0], loss_target[0]
    s = xs.shape[0]
    consts = _seg_consts()
    c_idx, qc_idx = jnp.reshape(ci, (1,)).astype(I32), jnp.stack([q0, ci]).astype(I32)

    def halves(g4):
        q, r, cc = g4.shape
        return g4.reshape(q, 2, r // 2, cc)

    own_first = [weights[n].astype(BF16) for n in ("w_in", "w_q_b", "w_kv_b")]
    own_later = [weights[n].astype(BF16) for n in ("w_o", "w_up", "w_down")]
    conv_cols = UP_W // N_CHIP
    ada_cols = w_ada.shape[2]
    b_shard = lax.dynamic_slice_in_dim(b_ada, q0 * ada_cols, ada_cols, axis=1)
    c_taps = jnp.concatenate([c, weights["w_conv"].reshape(1, 3 * conv_cols)], axis=1)
    c_taps_all, mod_all, tab, *gathered = _prologue(c_taps, weights["w_ada"], b_shard, positions.reshape(s, 1),
                                                    _rope_consts(), own_first)
    c_all = c_taps_all[:, 0, :D_MODEL]
    w_conv_f = c_taps_all[:, 0, D_MODEL:].reshape(N_CHIP, 2, 3, conv_cols)[:, 0]
    w_conv_f = jnp.transpose(w_conv_f, (1, 0, 2)).reshape(3, UP_W)
    mod_all = mod_all.reshape(N_CHIP, 2, N_DEV, ada_cols)
    mod = lax.dynamic_index_in_dim(lax.dynamic_index_in_dim(mod_all, ci, 1, False), me, 1, False)
    mod = mod.reshape(1, N_CHIP * ada_cols)
    sh1, sc1, g1, sh2, sc2, g2 = [mod[:, k * D_MODEL:(k + 1) * D_MODEL] for k in range(6)]
    w_in_f = _cols_from_shards(gathered[0])
    w_in_p = _pad_w_in(w_in_f)
    w_qb_p = _pad_w_qb(_cols_from_shards(gathered[1]))
    w_kvb_p = _pad_w_kvb(_cols_from_shards(gathered[2]))
    gains = _head_gains(g_mla_q_nope, g_mla_q_pe, g_mla_k_nope, g_mla_k_pe, g_dil_q, g_dil_k)

    h = _prenorm(xs, g_mix_norm, sc1, sh1, "prenorm")
    proj = _mm(h, w_in_p, "nn", F32, 512, P_COLS, "mm_in")
    ql, kvl = _latnorm(proj, g_q_lat, g_kv_lat)
    q_raw = _mm(ql, w_qb_p, "nn", F32, 512, HEADS * LANE, "mm_qb")
    kv_raw = _mm(kvl, w_kvb_p, "nn", F32, 512, HEADS * LANE + DIL_W, "mm_kvb")
    qm, km, vm, qd, kd, vd = _attn_prep(q_raw, kv_raw, proj, tab, gains, consts)
    scale_m, scale_d = (NOPE + ROPE) ** -0.5, DIL_DIM ** -0.5
    o_m, lse_m, got_up = _attn_fwd(qm, km, vm, True, scale_m, "attn_mla", gather=own_later[1:2])
    o_d, lse_d, got_o, got_down = _attn_fwd(qd, kd, vd, False, scale_d, "attn_dil", gather=[own_later[0], own_later[2]])
    gathered = [got_o, got_up, got_down]
    w_o_f = gathered[0].reshape(D_MODEL, D_MODEL)
    w_up_f = _cols_from_shards(gathered[1])
    w_down_f = gathered[2].reshape(D_FF, D_MODEL)
    mix_in = jnp.concatenate([o_m, o_d], axis=1)
    mix = _mm(mix_in, w_o_f, "nn", F32, 512, D_MODEL, "mm_o")
    x1, h2 = _resid_prenorm(xs, mix, g1, g_ffn_norm, sc2, sh2)
    up = _mm(h2, w_up_f, "nn", F32, 512, CONV_TILE, "mm_up")
    act = _conv_gate(up, w_conv_f, b_conv)
    ffn = _mm(act, w_down_f, "nn", F32, 256, D_MODEL, "mm_down")
    dy, dffn, dg2, loss_part = _final(x1, ffn, tgt, g2)

    da = _mm(dffn, w_down_f, "nt", F32, 512, CONV_TILE, "mm_down_dx")
    gw_down = _mm(act, dffn, "tn", F32, 256, D_MODEL, "mm_down_dw")
    dup_g, dup_v, dbg, dbv, dwg, dwv = _gate_bwd(up, da, w_conv_f, b_conv)
    dup = jnp.concatenate([dup_g, dup_v], axis=1)
    early_names = ("w_up", "w_down", "w_o")
    gw_up = _mm(h2, dup, "tn", F32, 512, CONV_TILE, "mm_up_dw", col_shards=True)
    early = [halves(gw_up), halves(gw_down.reshape(N_CHIP, D_FF // N_CHIP, D_MODEL))]
    dh2, *early_sib = _mm(dup, w_up_f, "nt", F32, 256, 512, "mm_up_dx", swap=early, b_outer=True)
    dx1, dmix, acc2 = _ffnnorm_bwd(dh2, x1, dy, mix, g_ffn_norm, sc2, g1)
    gw_o = _mm(mix_in, dmix, "tn", F32, 512, D_MODEL, "mm_o_dw")
    early.append(halves(gw_o.reshape(N_CHIP, D_MODEL // N_CHIP, D_MODEL)))
    dmix_in, sib_o = _mm(dmix, w_o_f, "nt", F32, 512, D_MODEL, "mm_o_dx", swap=early[2:])
    early_sib.append(sib_o)
    early_sums = [_pair_sum(g, a, c_idx, "pair_sum_" + n) for g, a, n in zip(early, early_sib, early_names)]
    dqm, dkm, dvm, *early_recv = _attn_bwd(qm, km, vm, o_m, dmix_in, 0, lse_m, True, scale_m, "attn_mla_bwd",
                                           scatter=early_sums[:1])
    dqd, dkd, dvd, *early_recv_d = _attn_bwd(qd, kd, vd, o_d, dmix_in, DIL_W // LANE, lse_d, False, scale_d,
                                             "attn_dil_bwd", scatter=early_sums[1:])
    early_recv = early_recv + early_recv_d
    dq_raw, dkv_raw, dkpe_b, dqd_b, dkd_b, dvd_b, dgains = _attn_prep_bwd(
        dqm, dkm, dvm, dqd, dkd, dvd, q_raw, kv_raw, proj, tab, gains, consts)
    dql = _mm(dq_raw, w_qb_p, "nt", F32, 512, Q_LORA, "mm_qb_dx")
    gw_qb = _unpad_w_qb(_mm(ql, dq_raw, "tn", F32, Q_LORA, HEADS * LANE, "mm_qb_dw"))
    dkvl = _mm(dkv_raw, w_kvb_p, "nt", F32, 512, KV_LORA, "mm_kvb_dx")
    gw_kvb = _unpad_w_kvb(_mm(kvl, dkv_raw, "tn", F32, KV_LORA, HEADS * LANE + DIL_W, "mm_kvb_dw"))
    dqlat_b, dkvlat_b, dglat = _latnorm_bwd(dql, dkvl, proj, g_q_lat, g_kv_lat)
    dproj = jnp.concatenate([dqlat_b, dkvlat_b, dkpe_b[:, KPE_OFF:KPE_OFF + ROPE], dqd_b, dkd_b, dvd_b], axis=1)
    gw_in = _mm(h, dproj, "tn", F32, 512, IN_COLS, "mm_in_dw")
    late_names = ("w_in", "w_q_b", "w_kv_b")
    late = [halves(_cols_to_shards(gw_in)), halves(_cols_to_shards(gw_qb)), halves(_cols_to_shards(gw_kvb))]
    dh, *late_sib = _mm(dproj, w_in_f, "nt", F32, 512, D_MODEL, "mm_in_dx", swap=late)
    grad_x, acc1 = _mixnorm_bwd(dh, xs, dx1, g_mix_norm, sc1)

    packed = _pack_small(acc1, acc2, dg2, dglat, dgains, dbg, dbv, dwg, dwv, loss_part)
    late_sums = [_pair_sum(g, a, c_idx, "pair_sum_" + n) for g, a, n in zip(late, late_sib, late_names)]
    *late_recv, gathered_small = _scatter_and_gather(late_sums, packed, "rs_scatter_late")

    grad_b_ada, *small_grads, gconv_full, loss_sum = _sum_unpack(gathered_small)
    grads = {"b_ada": grad_b_ada}
    grads.update({n: g for (n, _), g in zip(SMALL_WIDTHS, small_grads)})
    shard_cols = UP_W // N_CHIP
    grads["w_conv"] = lax.dynamic_slice_in_dim(gconv_full, q0 * shard_cols, shard_cols, axis=1)
    dmod_all = gathered_small[:, 0, :6 * D_MODEL]
    grads["w_ada"] = _ada_bwd(c_all, lax.dynamic_slice_in_dim(dmod_all, q0 * ada_cols, ada_cols, axis=1))

    big_names = late_names + early_names
    half_sums = [_shard_sum(p, b, qc_idx, "shard_sum_" + n)
                 for p, b, n in zip(late_sums + early_sums, list(late_recv) + list(early_recv), big_names)]
    for n, full in zip(big_names, _join_halves(half_sums)):
        grads[n] = full.reshape(2 * full.shape[1], full.shape[2])

    delta, new_m, new_v = {}, {}, {}
    for n in ("w_ada", "w_in", "w_q_b", "w_kv_b", "w_o", "w_up", "w_conv", "w_down"):
        operands = (weights[n], grads[n], mom_m[n], mom_v[n])
        flipped = n in ("w_in", "w_q_b")
        if flipped:
            operands = [jnp.swapaxes(a, 0, 1) for a in operands]
            grads[n] = jnp.swapaxes(operands[1], 0, 1)
        if n == "w_ada":
            operands = _in_hbm(*operands)
        delta[n], new_m[n], new_v[n] = _adamw(*operands, "adamw_" + n)
        if flipped:
            delta[n], new_m[n], new_v[n] = (jnp.swapaxes(a, 0, 1) for a in (delta[n], new_m[n], new_v[n]))
    vec_names = ("b_ada",) + tuple(n for n, _ in SMALL_WIDTHS)
    sd, sm, sv = _adamw_vectors(*[[d_[n] for n in vec_names] for d_ in (small_w, grads, mom_m, mom_v)])
    for k, n in enumerate(vec_names):
        delta[n], new_m[n], new_v[n] = sd[k], sm[k], sv[k]

    loss = loss_sum[0, 0]
    order = ("w_ada", "b_ada", "g_mix_norm", "w_in", "g_q_lat", "w_q_b", "g_kv_lat", "w_kv_b", "g_mla_q_nope", "g_mla_q_pe",
             "g_mla_k_nope", "g_mla_k_pe", "g_dil_q", "g_dil_k", "w_o", "g_ffn_norm", "w_up", "w_conv", "b_conv", "w_down")
    lead = lambda n, z: z[None] if n.startswith("w_") else z
    outs = [loss, grad_x[None]]
    for d_ in (grads, delta, new_m, new_v):
        outs += [lead(n, d_[n]) for n in order]
    return tuple(outs)
```

```python
import functools

import numpy as np
import jax
import jax.numpy as jnp
from jax import lax
from jax.experimental import pallas as pl
from jax.experimental.pallas import tpu as pltpu

F32 = jnp.float32
BF16 = jnp.bfloat16
I32 = jnp.int32

D_MODEL = 1024
HEADS = 8
NOPE = 64
ROPE = 32
Q_LORA = 512
KV_LORA = 256
DIL_DIM = 64
DIL_W = HEADS * DIL_DIM
D_FF = 2816
UP_W = 2 * D_FF
IN_COLS = Q_LORA + KV_LORA + ROPE + 3 * DIL_W
ROPE_THETA = 10000.0
EPS = 1e-6
NEG_INF = -1e30
N_DEV = 8
N_CHIP = 4

ADAM_LR = 0.001
ADAM_B1 = 0.9
ADAM_B2 = 0.999
ADAM_EPS = 1e-08
ADAM_WD = 0.01
ADAM_STEP = 10

LANE = 128
ROW_TILE = 256
NORM_TILE = 512
ATT_TQ = 512
ATT_TK = 256
ATT_TK_BWD = 512
LOG2E = 1.4426950408889634
LN2 = 0.6931471805599453
VMEM_CAP = 56 * 1024 * 1024
VMEM_FLOOR = 32 * 1024 * 1024

P_QLAT, P_QD, P_KD, P_VD, P_KVLAT, P_KPE = 0, 512, 1024, 1536, 2048, 2304
P_COLS = 2432
KPE_OFF = 64

NN = (((1,), (0,)), ((), ()))
NT = (((1,), (1,)), ((), ()))
TN = (((0,), (0,)), ((), ()))
HIGHEST = lax.Precision.HIGHEST
MESH = pl.DeviceIdType.MESH


def _params(sem=None, est_bytes=0):
    limit = int(min(max(2 * est_bytes + (4 << 20), VMEM_FLOOR), VMEM_CAP))
    if sem is None:
        return pltpu.CompilerParams(vmem_limit_bytes=limit)
    return pltpu.CompilerParams(dimension_semantics=sem, vmem_limit_bytes=limit)


def _nbytes(shape, dtype):
    return int(np.prod(shape)) * jnp.dtype(dtype).itemsize


def _in_hbm(*xs):
    return [pltpu.with_memory_space_constraint(x, pltpu.HBM) for x in xs]


def _mm(a, b, dims, out_dtype, tm, tn, name, col_shards=False, swap=(), b_outer=False):
    def spec(block, index):
        if b_outer:
            return pl.BlockSpec(block, lambda g0, g1: index(g1, g0))
        return pl.BlockSpec(block, index)

    if dims == "nn":
        (m, k), (k2, n) = a.shape, b.shape
        a_spec = spec((tm, k), lambda i, j: (i, 0))
        b_spec = spec((k, tn), lambda i, j: (0, j))
        dn = NN
    elif dims == "nt":
        (m, k), (n, k2) = a.shape, b.shape
        a_spec = spec((tm, k), lambda i, j: (i, 0))
        b_spec = spec((tn, k), lambda i, j: (j, 0))
        dn = NT
    else:
        (k, m), (k2, n) = a.shape, b.shape
        a_spec = spec((k, tm), lambda i, j: (0, i))
        b_spec = spec((k, tn), lambda i, j: (0, j))
        dn = TN
    assert k == k2 and m % tm == 0 and n % tn == 0, (name, a.shape, b.shape, tm, tn)

    nw = len(swap)
    grid = (n // tn, m // tm) if b_outer else (m // tm, n // tn)

    def body(*refs):
        a_ref, b_ref, o_ref = refs[0], refs[1], refs[2 + nw]
        comm = (refs[2:2 + nw], refs[3 + nw:3 + 2 * nw]) + tuple(refs[3 + 2 * nw:])
        if nw:
            @pl.when((pl.program_id(0) == 0) & (pl.program_id(1) == 0))
            def _():
                _PairSwap(*comm).start()

        o_ref[...] = lax.dot_general(a_ref[...], b_ref[...], dn, preferred_element_type=F32).astype(o_ref.dtype)

        if nw:
            @pl.when((pl.program_id(0) == grid[0] - 1) & (pl.program_id(1) == grid[1] - 1))
            def _():
                _PairSwap(*comm).finish()

    est = _nbytes((tm, k), a.dtype) + _nbytes((tn, k), b.dtype) + _nbytes((tm, tn), F32) + _nbytes((tm, tn), out_dtype)
    if col_shards:
        out_spec = spec((None, tm, tn), lambda i, j: (j, i, 0))
        out_shape = jax.ShapeDtypeStruct((n // tn, m, tn), out_dtype)
    else:
        out_spec = spec((tm, tn), lambda i, j: (i, j))
        out_shape = jax.ShapeDtypeStruct((m, n), out_dtype)
    out = pl.pallas_call(
        body, name=name, grid=grid,
        in_specs=[a_spec, b_spec] + [ANY] * nw,
        out_specs=[out_spec] + [ANY] * nw,
        out_shape=[out_shape] + _PairSwap.out_shapes(swap),
        scratch_shapes=_PairSwap.semaphores(nw) if nw else [],
        compiler_params=_params(("arbitrary", "arbitrary") if nw else ("parallel", "parallel"), est),
    )(a, b, *swap)
    return out if nw else out[0]


def _seg_consts():
    seg_q = np.zeros((HEADS * LANE, LANE), np.float32)
    inv_q = np.zeros((1, LANE), np.float32)
    seg_k = np.zeros((HEADS * LANE, LANE), np.float32)
    inv_k = np.zeros((1, LANE), np.float32)
    seg_d = np.zeros((DIL_W, LANE), np.float32)
    inv_d = np.zeros((1, LANE), np.float32)
    for h in range(HEADS):
        seg_q[h * LANE:h * LANE + NOPE, 2 * h] = 1.0
        seg_q[h * LANE + NOPE:h * LANE + NOPE + ROPE, 2 * h + 1] = 1.0
        inv_q[0, 2 * h], inv_q[0, 2 * h + 1] = 1.0 / NOPE, 1.0 / ROPE
        seg_k[h * LANE:h * LANE + NOPE, h] = 1.0
        inv_k[0, h] = 1.0 / NOPE
        seg_d[h * DIL_DIM:(h + 1) * DIL_DIM, h] = 1.0
        inv_d[0, h] = 1.0 / DIL_DIM
    fold_q = np.tile(np.eye(LANE, dtype=np.float32), (HEADS, 1))
    fold_d = np.zeros((DIL_W, LANE), np.float32)
    fold_d[np.arange(DIL_W), np.arange(DIL_W) % DIL_DIM] = 1.0
    j = lambda v: jnp.asarray(v)
    b = lambda v: jnp.asarray(v, dtype=BF16)
    return dict(seg_q=b(seg_q), exp_q=b(seg_q.T.copy()), inv_q=j(inv_q), seg_k=b(seg_k), exp_k=b(seg_k.T.copy()),
                inv_k=j(inv_k), seg_d=b(seg_d), exp_d=b(seg_d.T.copy()), inv_d=j(inv_d), fold_q=j(fold_q), fold_d=j(fold_d))


def _rope_consts():
    inv_d = jnp.power(ROPE_THETA, -2.0 * jnp.arange(DIL_DIM // 2, dtype=F32) / DIL_DIM)
    inv_q = jnp.power(ROPE_THETA, -2.0 * jnp.arange(ROPE // 2, dtype=F32) / ROPE)
    lanes = np.arange(LANE)
    freq_d = inv_d[lanes % (DIL_DIM // 2)]
    in_pe = (lanes >= KPE_OFF) & (lanes < KPE_OFF + ROPE)
    freq_q = jnp.where(jnp.asarray(in_pe), inv_q[(lanes - KPE_OFF) % (ROPE // 2)], 0.0)
    sign_d = np.where(lanes % DIL_DIM < DIL_DIM // 2, -1.0, 1.0).astype(np.float32)
    sign_q = np.where(in_pe, np.where((lanes - KPE_OFF) < ROPE // 2, -1.0, 1.0), 0.0).astype(np.float32)
    zeros, ones = np.zeros(LANE, np.float32), np.ones(LANE, np.float32)
    freq = jnp.concatenate([freq_d, freq_d, freq_q, freq_q])[None, :]
    csel = jnp.asarray(np.concatenate([ones, zeros, ones, zeros]))[None, :]
    ssel = jnp.asarray(np.concatenate([zeros, sign_d, zeros, sign_q]))[None, :]
    return freq, csel, ssel


def _full(shape):
    return pl.BlockSpec(shape, lambda *_: (0,) * len(shape))


def _tile_lanes(x, n):
    return jnp.concatenate([x] * n, axis=1)


def _rms(x):
    return lax.rsqrt(jnp.mean(x * x, axis=-1, keepdims=True) + EPS)


def _prenorm(x, gain, scale, shift, name):
    s, d = x.shape

    def body(x_ref, g_ref, sc_ref, sh_ref, h_ref):
        xv = x_ref[...]
        h = (xv * _rms(xv)) * g_ref[...] * (1.0 + sc_ref[...]) + sh_ref[...]
        h_ref[...] = h.astype(BF16)

    row = pl.BlockSpec((NORM_TILE, d), lambda i: (i, 0))
    return pl.pallas_call(
        body, name=name, grid=(s // NORM_TILE,),
        in_specs=[row, _full((1, d)), _full((1, d)), _full((1, d))],
        out_specs=row, out_shape=jax.ShapeDtypeStruct((s, d), BF16),
        compiler_params=_params(("parallel",)),
    )(x, *_in_hbm(gain, scale, shift))


def _latnorm(proj, g_q, g_kv):
    s = proj.shape[0]

    def body(q_ref, kv_ref, gq_ref, gkv_ref, ql_ref, kvl_ref):
        q, kv = q_ref[...], kv_ref[...]
        ql_ref[...] = ((q * _rms(q)) * gq_ref[...]).astype(BF16)
        kvl_ref[...] = ((kv * _rms(kv)) * gkv_ref[...]).astype(BF16)

    return pl.pallas_call(
        body, name="latnorm", grid=(s // NORM_TILE,),
        in_specs=[pl.BlockSpec((NORM_TILE, Q_LORA), lambda i: (i, P_QLAT // Q_LORA)),
                  pl.BlockSpec((NORM_TILE, KV_LORA), lambda i: (i, P_KVLAT // KV_LORA)),
                  _full((1, Q_LORA)), _full((1, KV_LORA))],
        out_specs=[pl.BlockSpec((NORM_TILE, Q_LORA), lambda i: (i, 0)), pl.BlockSpec((NORM_TILE, KV_LORA), lambda i: (i, 0))],
        out_shape=[jax.ShapeDtypeStruct((s, Q_LORA), BF16), jax.ShapeDtypeStruct((s, KV_LORA), BF16)],
        compiler_params=_params(("parallel",)),
    )(proj, proj, *_in_hbm(g_q, g_kv))


def _dot01(v, mat01):
    hi = v.astype(BF16)
    lo = (v - hi.astype(F32)).astype(BF16)
    return jnp.dot(hi, mat01, preferred_element_type=F32) + jnp.dot(lo, mat01, preferred_element_type=F32)


def _seg_rinv(x, seg, exp, inv):
    r = lax.rsqrt(_dot01(x * x, seg) * inv + EPS)
    return _dot01(r, exp)


def _seg_mean(v, seg, exp, inv):
    return _dot01(_dot01(v, seg) * inv, exp)


def _swap_halves(x, half):
    n = x.shape[1]
    lane = lax.broadcasted_iota(I32, (1, n), 1)
    first = (lane & (2 * half - 1)) < half
    return jnp.where(first, pltpu.roll(x, n - half, 1), pltpu.roll(x, half, 1))


def _rope(x, cos, sin_signed, half):
    return x * cos + _swap_halves(x, half) * sin_signed


def _rope_bwd(dy, cos, sin_signed, half):
    return dy * cos + _swap_halves(dy * sin_signed, half)


def _pe_lane_mask(n):
    lane = lax.broadcasted_iota(I32, (1, n), 1) & (LANE - 1)
    return (lane >= KPE_OFF) & (lane < KPE_OFF + ROPE)


def _attn_prep(q_raw, kv_raw, proj, tab, gains, consts):
    s = q_raw.shape[0]
    hw = HEADS * LANE

    def body(q_ref, kv_ref, kpe_ref, qd_ref, kd_ref, vd_ref, tab_ref,
             gq_ref, gk_ref, gkpe_ref, gdq_ref, gdk_ref,
             segq_ref, expq_ref, invq_ref, segk_ref, expk_ref, invk_ref, segd_ref, expd_ref, invd_ref,
             qm_ref, km_ref, vm_ref, qdo_ref, kdo_ref, vdo_ref):
        tab_v = tab_ref[...]
        cos_d, sin_d = _tile_lanes(tab_v[:, 0:LANE], DIL_W // LANE), _tile_lanes(tab_v[:, LANE:2 * LANE], DIL_W // LANE)
        cos_q1, sin_q1 = tab_v[:, 2 * LANE:3 * LANE], tab_v[:, 3 * LANE:4 * LANE]
        cos_q, sin_q = _tile_lanes(cos_q1, HEADS), _tile_lanes(sin_q1, HEADS)

        q = q_ref[...]
        qn = q * _seg_rinv(q, segq_ref[...], expq_ref[...], invq_ref[...]) * gq_ref[...]
        qm_ref[...] = _rope(qn, cos_q, sin_q, ROPE // 2).astype(BF16)

        kv = kv_ref[...]
        kp = kv[:, :hw]
        kn = kp * _seg_rinv(kp, segk_ref[...], expk_ref[...], invk_ref[...]) * gk_ref[...]
        kpe = kpe_ref[...]
        r_pe = lax.rsqrt(jnp.sum(kpe * kpe, axis=-1, keepdims=True) * (1.0 / ROPE) + EPS)
        kpe_r = _rope(kpe * r_pe * gkpe_ref[...], cos_q1, sin_q1, ROPE // 2)
        km_ref[...] = (kn + _tile_lanes(kpe_r, HEADS)).astype(BF16)
        vm_ref[...] = kv[:, hw:].astype(BF16)

        qd = qd_ref[...]
        qdn = qd * _seg_rinv(qd, segd_ref[...], expd_ref[...], invd_ref[...]) * gdq_ref[...]
        qdo_ref[...] = _rope(qdn, cos_d, sin_d, DIL_DIM // 2).astype(BF16)
        kd = kd_ref[...]
        kdn = kd * _seg_rinv(kd, segd_ref[...], expd_ref[...], invd_ref[...]) * gdk_ref[...]
        kdo_ref[...] = _rope(kdn, cos_d, sin_d, DIL_DIM // 2).astype(BF16)
        vdo_ref[...] = vd_ref[...].astype(BF16)

    t = ROW_TILE
    row = lambda w, cb=0: pl.BlockSpec((t, w), lambda i: (i, cb))
    c = consts
    return pl.pallas_call(
        body, name="attn_prep", grid=(s // t,),
        in_specs=[row(hw), row(hw + DIL_W), row(LANE, P_KPE // LANE), row(DIL_W, P_QD // DIL_W), row(DIL_W, P_KD // DIL_W),
                  row(DIL_W, P_VD // DIL_W), row(4 * LANE),
                  _full((1, hw)), _full((1, hw)), _full((1, LANE)), _full((1, DIL_W)), _full((1, DIL_W)),
                  _full((hw, LANE)), _full((LANE, hw)), _full((1, LANE)), _full((hw, LANE)), _full((LANE, hw)), _full((1, LANE)),
                  _full((DIL_W, LANE)), _full((LANE, DIL_W)), _full((1, LANE))],
        out_specs=[row(hw), row(hw), row(DIL_W), row(DIL_W), row(DIL_W), row(DIL_W)],
        out_shape=[jax.ShapeDtypeStruct((s, hw), BF16), jax.ShapeDtypeStruct((s, hw), BF16)]
        + [jax.ShapeDtypeStruct((s, DIL_W), BF16)] * 4,
        compiler_params=_params(("parallel",), 24 << 20),
    )(*_in_hbm(q_raw, kv_raw, proj, proj, proj, proj), tab,
      *_in_hbm(gains["q"], gains["k"], gains["kpe"], gains["dq"], gains["dk"],
               c["seg_q"], c["exp_q"], c["inv_q"], c["seg_k"], c["exp_k"], c["inv_k"], c["seg_d"], c["exp_d"], c["inv_d"]))


def _attn_prep_bwd(dqm, dkm, dvm, dqd, dkd, dvd, q_raw, kv_raw, proj, tab, gains, consts):
    s = q_raw.shape[0]
    hw = HEADS * LANE
    n_steps = s // ROW_TILE

    def body(dqm_ref, dkm_ref, dvm_ref, dqd_ref, dkd_ref, dvd_ref, q_ref, kv_ref, kpe_ref, qd_ref, kd_ref, tab_ref,
             gq_ref, gk_ref, gkpe_ref, gdq_ref, gdk_ref,
             segq_ref, expq_ref, invq_ref, segk_ref, expk_ref, invk_ref, segd_ref, expd_ref, invd_ref, foldq_ref, foldd_ref,
             dq_ref, dkv_ref, dkpe_ref, dqdo_ref, dkdo_ref, dvdo_ref, dg_ref, acc_ref):
        i = pl.program_id(0)

        @pl.when(i == 0)
        def _():
            acc_ref[...] = jnp.zeros_like(acc_ref)

        tab_v = tab_ref[...]
        cos_d, sin_d = _tile_lanes(tab_v[:, 0:LANE], DIL_W // LANE), _tile_lanes(tab_v[:, LANE:2 * LANE], DIL_W // LANE)
        cos_q1, sin_q1 = tab_v[:, 2 * LANE:3 * LANE], tab_v[:, 3 * LANE:4 * LANE]
        cos_q, sin_q = _tile_lanes(cos_q1, HEADS), _tile_lanes(sin_q1, HEADS)

        def norm_bwd(x, dyg, gain, seg, exp, inv):
            rinv = _seg_rinv(x, seg, exp, inv)
            xn = x * rinv
            dxn = dyg * gain
            dx = rinv * (dxn - xn * _seg_mean(dxn * xn, seg, exp, inv))
            return dx, jnp.sum(dyg * xn, axis=0, keepdims=True)

        dq, gq_l = norm_bwd(q_ref[...], _rope_bwd(dqm_ref[...], cos_q, sin_q, ROPE // 2), gq_ref[...],
                            segq_ref[...], expq_ref[...], invq_ref[...])
        dq_ref[...] = dq.astype(BF16)

        dkm = dkm_ref[...]
        kv = kv_ref[...]
        dkp, gk_l = norm_bwd(kv[:, :hw], dkm, gk_ref[...], segk_ref[...], expk_ref[...], invk_ref[...])
        dkv_ref[:, :hw] = dkp.astype(BF16)
        dkv_ref[:, hw:] = dvm_ref[...].astype(BF16)

        dkpe_r = dkm[:, 0:LANE]
        for h in range(1, HEADS):
            dkpe_r = dkpe_r + dkm[:, h * LANE:(h + 1) * LANE]
        dkpe_r = jnp.where(_pe_lane_mask(LANE), dkpe_r, 0.0)
        dyg = _rope_bwd(dkpe_r, cos_q1, sin_q1, ROPE // 2)
        kpe = kpe_ref[...]
        r_pe = lax.rsqrt(jnp.sum(kpe * kpe, axis=-1, keepdims=True) * (1.0 / ROPE) + EPS)
        xn = kpe * r_pe
        dxn = dyg * gkpe_ref[...]
        dkpe = r_pe * (dxn - xn * (jnp.sum(dxn * xn, axis=-1, keepdims=True) * (1.0 / ROPE)))
        dkpe_ref[...] = dkpe.astype(BF16)
        gkpe_l = jnp.sum(dyg * xn, axis=0, keepdims=True)

        dqd_v, gdq_l = norm_bwd(qd_ref[...], _rope_bwd(dqd_ref[...], cos_d, sin_d, DIL_DIM // 2), gdq_ref[...],
                                segd_ref[...], expd_ref[...], invd_ref[...])
        dqdo_ref[...] = dqd_v.astype(BF16)
        dkd_v, gdk_l = norm_bwd(kd_ref[...], _rope_bwd(dkd_ref[...], cos_d, sin_d, DIL_DIM // 2), gdk_ref[...],
                                segd_ref[...], expd_ref[...], invd_ref[...])
        dkdo_ref[...] = dkd_v.astype(BF16)
        dvdo_ref[...] = dvd_ref[...].astype(BF16)

        acc_ref[0:1, :] += gq_l
        acc_ref[1:2, :] += gk_l
        acc_ref[2:3, 0:LANE] += gkpe_l
        acc_ref[3:4, 0:DIL_W] += gdq_l
        acc_ref[4:5, 0:DIL_W] += gdk_l

        @pl.when(i == n_steps - 1)
        def _():
            acc = acc_ref[...]
            fq = jnp.dot(acc, foldq_ref[...], precision=HIGHEST, preferred_element_type=F32)
            fd = jnp.dot(acc[:, 0:DIL_W], foldd_ref[...], precision=HIGHEST, preferred_element_type=F32)
            rows = lax.broadcasted_iota(I32, (8, LANE), 0)
            base = jnp.where(rows < 2, fq, jnp.where(rows == 2, acc[:, 0:LANE], fd))
            at0 = pltpu.roll(base, LANE - KPE_OFF, 1)
            dg_ref[...] = jnp.where(rows == 5, pltpu.roll(at0, 5, 0), jnp.where(rows == 2, at0, base))

    t = ROW_TILE
    row = lambda w, cb=0: pl.BlockSpec((t, w), lambda i: (i, cb))
    c = consts
    return pl.pallas_call(
        body, name="attn_prep_bwd", grid=(n_steps,),
        in_specs=[row(hw), row(hw), row(DIL_W), row(DIL_W), row(DIL_W), row(DIL_W),
                  row(hw), row(hw + DIL_W), row(LANE, P_KPE // LANE), row(DIL_W, P_QD // DIL_W), row(DIL_W, P_KD // DIL_W),
                  row(4 * LANE),
                  _full((1, hw)), _full((1, hw)), _full((1, LANE)), _full((1, DIL_W)), _full((1, DIL_W)),
                  _full((hw, LANE)), _full((LANE, hw)), _full((1, LANE)), _full((hw, LANE)), _full((LANE, hw)), _full((1, LANE)),
                  _full((DIL_W, LANE)), _full((LANE, DIL_W)), _full((1, LANE)), _full((hw, LANE)), _full((DIL_W, LANE))],
        out_specs=[row(hw), row(hw + DIL_W), row(LANE), row(DIL_W), row(DIL_W), row(DIL_W), _full((8, LANE))],
        out_shape=[jax.ShapeDtypeStruct((s, hw), BF16), jax.ShapeDtypeStruct((s, hw + DIL_W), BF16),
                   jax.ShapeDtypeStruct((s, LANE), BF16)] + [jax.ShapeDtypeStruct((s, DIL_W), BF16)] * 3
        + [jax.ShapeDtypeStruct((8, LANE), F32)],
        scratch_shapes=[pltpu.VMEM((8, hw), F32)],
        compiler_params=_params(("arbitrary",), 28 << 20),
    )(*_in_hbm(dqm, dkm, dvm, dqd, dkd, dvd, q_raw, kv_raw, proj, proj, proj), tab,
      *_in_hbm(gains["q"], gains["k"], gains["kpe"], gains["dq"], gains["dk"],
               c["seg_q"], c["exp_q"], c["inv_q"], c["seg_k"], c["exp_k"], c["inv_k"], c["seg_d"], c["exp_d"], c["inv_d"],
               c["fold_q"], c["fold_d"]))


def _latnorm_bwd(dql, dkvl, proj, g_q, g_kv):
    s = proj.shape[0]
    n_steps = s // NORM_TILE

    def body(dql_ref, dkvl_ref, q_ref, kv_ref, gq_ref, gkv_ref, dq_ref, dkv_ref, dg_ref):
        i = pl.program_id(0)

        @pl.when(i == 0)
        def _():
            dg_ref[...] = jnp.zeros_like(dg_ref)

        def one(x, dyg, gain):
            r = _rms(x)
            xn = x * r
            dxn = dyg * gain
            dx = r * (dxn - xn * jnp.mean(dxn * xn, axis=-1, keepdims=True))
            return dx, jnp.sum(dyg * xn, axis=0, keepdims=True)

        dq, gq_l = one(q_ref[...], dql_ref[...], gq_ref[...])
        dkv, gkv_l = one(kv_ref[...], dkvl_ref[...], gkv_ref[...])
        dq_ref[...] = dq.astype(BF16)
        dkv_ref[...] = dkv.astype(BF16)
        dg_ref[0:1, :] += gq_l
        dg_ref[1:2, 0:KV_LORA] += gkv_l

    t = NORM_TILE
    return pl.pallas_call(
        body, name="latnorm_bwd", grid=(n_steps,),
        in_specs=[pl.BlockSpec((t, Q_LORA), lambda i: (i, 0)), pl.BlockSpec((t, KV_LORA), lambda i: (i, 0)),
                  pl.BlockSpec((t, Q_LORA), lambda i: (i, P_QLAT // Q_LORA)),
                  pl.BlockSpec((t, KV_LORA), lambda i: (i, P_KVLAT // KV_LORA)),
                  _full((1, Q_LORA)), _full((1, KV_LORA))],
        out_specs=[pl.BlockSpec((t, Q_LORA), lambda i: (i, 0)), pl.BlockSpec((t, KV_LORA), lambda i: (i, 0)), _full((8, Q_LORA))],
        out_shape=[jax.ShapeDtypeStruct((s, Q_LORA), BF16), jax.ShapeDtypeStruct((s, KV_LORA), BF16),
                   jax.ShapeDtypeStruct((8, Q_LORA), F32)],
        compiler_params=_params(("arbitrary",)),
    )(dql, dkvl, proj, proj, *_in_hbm(g_q, g_kv))


def _resid_prenorm(x, mix, g1, gain, scale, shift):
    s, d = x.shape

    def body(x_ref, mix_ref, g1_ref, g_ref, sc_ref, sh_ref, x1_ref, h_ref):
        x1 = x_ref[...] + g1_ref[...] * mix_ref[...]
        x1_ref[...] = x1
        h_ref[...] = ((x1 * _rms(x1)) * g_ref[...] * (1.0 + sc_ref[...]) + sh_ref[...]).astype(BF16)

    row = pl.BlockSpec((NORM_TILE, d), lambda i: (i, 0))
    vec = _full((1, d))
    return pl.pallas_call(
        body, name="resid_prenorm", grid=(s // NORM_TILE,),
        in_specs=[row, row, vec, vec, vec, vec], out_specs=[row, row],
        out_shape=[jax.ShapeDtypeStruct((s, d), F32), jax.ShapeDtypeStruct((s, d), BF16)],
        compiler_params=_params(("parallel",)),
    )(x, mix, *_in_hbm(g1, gain, scale, shift))


CONV_TILE = 1408
HALO = 8


def _shift_down(x, halo, k):
    t = x.shape[0]
    row = lax.broadcasted_iota(I32, (t, 1), 0)
    out = pltpu.roll(x, k, 0)
    for r in range(k):
        out = jnp.where(row == r, halo[HALO - k + r:HALO - k + r + 1, :], out)
    return out


def _shift_up(x, halo, k):
    t = x.shape[0]
    row = lax.broadcasted_iota(I32, (t, 1), 0)
    out = pltpu.roll(x, t - k, 0)
    for r in range(k):
        out = jnp.where(row == t - k + r, halo[r:r + 1, :], out)
    return out


def _conv_fwd(x, halo, w, b):
    p1, p2 = _shift_down(x, halo, 1), _shift_down(x, halo, 2)
    u = b + p2 * w[0:1, :]
    u = u + p1 * w[1:2, :]
    u = u + x * w[2:3, :]
    return u, p1, p2


def _sigmoid(x):
    return 0.5 * jnp.tanh(0.5 * x) + 0.5


def _conv_gate(up, w_conv, b_conv):
    s = up.shape[0]
    t = ROW_TILE
    nj = D_FF // CONV_TILE
    hb = t // HALO

    def body(g_ref, v_ref, gh_ref, vh_ref, wg_ref, wv_ref, bg_ref, bv_ref, a_ref):
        live = (pl.program_id(0) > 0).astype(F32)
        ug, _, _ = _conv_fwd(g_ref[...], gh_ref[...] * live, wg_ref[...], bg_ref[...])
        uv, _, _ = _conv_fwd(v_ref[...], vh_ref[...] * live, wv_ref[...], bv_ref[...])
        a_ref[...] = (ug * _sigmoid(ug) * uv).astype(BF16)

    main = lambda off: pl.BlockSpec((t, CONV_TILE), lambda i, j: (i, j + off))
    halo = lambda off: pl.BlockSpec((HALO, CONV_TILE), lambda i, j: (jnp.maximum(i * hb - 1, 0), j + off))
    wsp = lambda off: pl.BlockSpec((3, CONV_TILE), lambda i, j: (0, j + off))
    bsp = lambda off: pl.BlockSpec((1, CONV_TILE), lambda i, j: (0, j + off))
    return pl.pallas_call(
        body, name="conv_gate", grid=(s // t, nj),
        in_specs=[main(0), main(nj), halo(0), halo(nj), wsp(0), wsp(nj), bsp(0), bsp(nj)],
        out_specs=pl.BlockSpec((t, CONV_TILE), lambda i, j: (i, j)),
        out_shape=jax.ShapeDtypeStruct((s, D_FF), BF16),
        compiler_params=_params(("parallel", "parallel"), 12 << 20),
    )(up, up, up, up, *_in_hbm(w_conv, w_conv, b_conv, b_conv))


def _gate_bwd(up, da, w_conv, b_conv):
    s = up.shape[0]
    t = ROW_TILE
    nj = D_FF // CONV_TILE
    hb = t // HALO
    n_i = s // t

    def body(g_ref, v_ref, gh_ref, vh_ref, gn_ref, vn_ref, da_ref, dan_ref, wg_ref, wv_ref, bg_ref, bv_ref,
             dupg_ref, dupv_ref, dbg_ref, dbv_ref, dwg_ref, dwv_ref):
        i = pl.program_id(1)

        @pl.when(i == 0)
        def _():
            for r in (dbg_ref, dbv_ref, dwg_ref, dwv_ref):
                r[...] = jnp.zeros_like(r)

        def d_gate(ug, uv, da_v):
            sg = _sigmoid(ug)
            return da_v * uv * (sg * (1.0 + ug * (1.0 - sg))), da_v * (ug * sg)

        live = (i > 0).astype(F32)
        xg, xv = g_ref[...], v_ref[...]
        wg, wv = wg_ref[...], wv_ref[...]
        ug, g1, g2 = _conv_fwd(xg, gh_ref[...] * live, wg, bg_ref[...])
        uv, v1, v2 = _conv_fwd(xv, vh_ref[...] * live, wv, bv_ref[...])
        dug, duv = d_gate(ug, uv, da_ref[...])

        more = (i < n_i - 1).astype(F32)
        ug_n, _, _ = _conv_fwd(gn_ref[...], xg[t - HALO:, :], wg, bg_ref[...])
        uv_n, _, _ = _conv_fwd(vn_ref[...], xv[t - HALO:, :], wv, bv_ref[...])
        dug_n, duv_n = d_gate(ug_n, uv_n, dan_ref[...] * more)

        def conv_t(du, du_n, w):
            return du * w[2:3, :] + _shift_up(du, du_n, 1) * w[1:2, :] + _shift_up(du, du_n, 2) * w[0:1, :]

        dupg_ref[...] = conv_t(dug, dug_n, wg).astype(BF16)
        dupv_ref[...] = conv_t(duv, duv_n, wv).astype(BF16)
        csum = lambda z: jnp.sum(z, axis=0, keepdims=True)
        dbg_ref[...] += csum(dug)
        dbv_ref[...] += csum(duv)
        dwg_ref[0:1, :] += csum(dug * g2)
        dwg_ref[1:2, :] += csum(dug * g1)
        dwg_ref[2:3, :] += csum(dug * xg)
        dwv_ref[0:1, :] += csum(duv * v2)
        dwv_ref[1:2, :] += csum(duv * v1)
        dwv_ref[2:3, :] += csum(duv * xv)

    last_halo = s // HALO - 1
    main = lambda off: pl.BlockSpec((t, CONV_TILE), lambda j, i: (i, j + off))
    halo = lambda off: pl.BlockSpec((HALO, CONV_TILE), lambda j, i: (jnp.maximum(i * hb - 1, 0), j + off))
    nxt = lambda off: pl.BlockSpec((HALO, CONV_TILE), lambda j, i: (jnp.minimum((i + 1) * hb, last_halo), j + off))
    wsp = lambda off: pl.BlockSpec((3, CONV_TILE), lambda j, i: (0, j + off))
    bsp = lambda off: pl.BlockSpec((1, CONV_TILE), lambda j, i: (0, j + off))
    outs = pl.pallas_call(
        body, name="gate_bwd", grid=(nj, n_i),
        in_specs=[main(0), main(nj), halo(0), halo(nj), nxt(0), nxt(nj), main(0), nxt(0),
                  wsp(0), wsp(nj), bsp(0), bsp(nj)],
        out_specs=[main(0), main(0),
                   pl.BlockSpec((1, CONV_TILE), lambda j, i: (0, j)), pl.BlockSpec((1, CONV_TILE), lambda j, i: (0, j)),
                   pl.BlockSpec((3, CONV_TILE), lambda j, i: (0, j)), pl.BlockSpec((3, CONV_TILE), lambda j, i: (0, j))],
        out_shape=[jax.ShapeDtypeStruct((s, D_FF), BF16), jax.ShapeDtypeStruct((s, D_FF), BF16),
                   jax.ShapeDtypeStruct((1, D_FF), F32), jax.ShapeDtypeStruct((1, D_FF), F32),
                   jax.ShapeDtypeStruct((3, D_FF), F32), jax.ShapeDtypeStruct((3, D_FF), F32)],
        compiler_params=_params(("parallel", "arbitrary"), 24 << 20),
    )(up, up, up, up, up, up, da, da, *_in_hbm(w_conv, w_conv, b_conv, b_conv))
    return outs


def _final(x1, ffn, tgt, g2):
    s, d = x1.shape
    n_steps = s // NORM_TILE

    def body(x1_ref, f_ref, t_ref, g2_ref, dy_ref, df_ref, dg2_ref, loss_ref, lacc_ref):
        i = pl.program_id(0)

        @pl.when(i == 0)
        def _():
            dg2_ref[...] = jnp.zeros_like(dg2_ref)
            lacc_ref[...] = jnp.zeros_like(lacc_ref)

        f = f_ref[...]
        e = x1_ref[...] + g2_ref[...] * f - t_ref[...]
        dy = e * (1.0 / d)
        dy_ref[...] = dy
        df_ref[...] = (dy * g2_ref[...]).astype(BF16)
        dg2_ref[...] += jnp.sum(dy * f, axis=0, keepdims=True)
        lacc_ref[...] += jnp.sum(e * e, axis=0, keepdims=True)

        @pl.when(i == n_steps - 1)
        def _():
            loss_ref[...] = jnp.sum(lacc_ref[...], axis=1, keepdims=True) * (0.5 / d)

    row = pl.BlockSpec((NORM_TILE, d), lambda i: (i, 0))
    return pl.pallas_call(
        body, name="final", grid=(n_steps,),
        in_specs=[row, row, row, _full((1, d))],
        out_specs=[row, row, _full((1, d)), _full((1, 1))],
        out_shape=[jax.ShapeDtypeStruct((s, d), F32), jax.ShapeDtypeStruct((s, d), BF16),
                   jax.ShapeDtypeStruct((1, d), F32), jax.ShapeDtypeStruct((1, 1), F32)],
        scratch_shapes=[pltpu.VMEM((1, d), F32)],
        compiler_params=_params(("arbitrary",)),
    )(x1, ffn, tgt, *_in_hbm(g2))


def _ffnnorm_bwd(dh2, x1, dy, mix, gain, scale, g1):
    s, d = x1.shape
    n_steps = s // NORM_TILE

    def body(dh_ref, x_ref, dy_ref, mix_ref, g_ref, sc_ref, g1_ref, dx_ref, dm_ref, acc_ref):
        i = pl.program_id(0)

        @pl.when(i == 0)
        def _():
            acc_ref[...] = jnp.zeros_like(acc_ref)

        dh, x = dh_ref[...], x_ref[...]
        r = _rms(x)
        xn = x * r
        dn = dh * (1.0 + sc_ref[...])
        dxn = dn * g_ref[...]
        dx = dy_ref[...] + r * (dxn - xn * jnp.mean(dxn * xn, axis=-1, keepdims=True))
        dx_ref[...] = dx
        dm_ref[...] = (dx * g1_ref[...]).astype(BF16)
        csum = lambda z: jnp.sum(z, axis=0, keepdims=True)
        acc_ref[0:1, :] += csum(dh)
        acc_ref[1:2, :] += csum(dh * (xn * g_ref[...]))
        acc_ref[2:3, :] += csum(dn * xn)
        acc_ref[3:4, :] += csum(dx * mix_ref[...])

    row = pl.BlockSpec((NORM_TILE, d), lambda i: (i, 0))
    vec = _full((1, d))
    return pl.pallas_call(
        body, name="ffnnorm_bwd", grid=(n_steps,),
        in_specs=[row, row, row, row, vec, vec, vec],
        out_specs=[row, row, _full((8, d))],
        out_shape=[jax.ShapeDtypeStruct((s, d), F32), jax.ShapeDtypeStruct((s, d), BF16), jax.ShapeDtypeStruct((8, d), F32)],
        compiler_params=_params(("arbitrary",)),
    )(dh2, x1, dy, mix, *_in_hbm(gain, scale, g1))


def _mixnorm_bwd(dh, x, dx1, gain, scale):
    s, d = x.shape
    n_steps = s // NORM_TILE

    def body(dh_ref, x_ref, dx1_ref, g_ref, sc_ref, gx_ref, acc_ref):
        i = pl.program_id(0)

        @pl.when(i == 0)
        def _():
            acc_ref[...] = jnp.zeros_like(acc_ref)

        dh, x = dh_ref[...], x_ref[...]
        r = _rms(x)
        xn = x * r
        dn = dh * (1.0 + sc_ref[...])
        dxn = dn * g_ref[...]
        gx_ref[...] = dx1_ref[...] + r * (dxn - xn * jnp.mean(dxn * xn, axis=-1, keepdims=True))
        csum = lambda z: jnp.sum(z, axis=0, keepdims=True)
        acc_ref[0:1, :] += csum(dh)
        acc_ref[1:2, :] += csum(dh * (xn * g_ref[...]))
        acc_ref[2:3, :] += csum(dn * xn)

    row = pl.BlockSpec((NORM_TILE, d), lambda i: (i, 0))
    vec = _full((1, d))
    return pl.pallas_call(
        body, name="mixnorm_bwd", grid=(n_steps,),
        in_specs=[row, row, row, vec, vec],
        out_specs=[row, _full((8, d))],
        out_shape=[jax.ShapeDtypeStruct((s, d), F32), jax.ShapeDtypeStruct((8, d), F32)],
        compiler_params=_params(("arbitrary",)),
    )(dh, x, dx1, *_in_hbm(gain, scale))


def _key_count(d, dilated):
    if not dilated:
        return jnp.where(d >= 0, 1.0, 0.0)
    one = lambda cond: jnp.where(cond, 1.0, 0.0)
    cnt = one(d <= 128) + one(((d & 3) == 0) & (d <= 512)) + one((d & 15) == 0)
    return jnp.where(d >= 0, cnt, 0.0)


def _block_kinds(mla):
    return (0, "diag", "none") if mla else (NEAR_REACH, "near", "far")


NEAR_REACH = 512


def _near_offsets(tk, tq):
    return (NEAR_REACH - (tk - tq)) // tk + 1


def _scores_t(ka, qa, scale, kind, rel_t, offset, near_tabs=None):
    return _mask_scores(lax.dot_general(ka, qa, NT, preferred_element_type=F32), scale, kind, rel_t, offset, near_tabs)


def _fill_near_tables(bias_ref, cnt_ref, rel_t):
    tk, tq = rel_t.shape
    for idx in range(_near_offsets(tk, tq)):
        cnt = _key_count(rel_t + (tk - tq) + idx * tk, True)
        cnt_ref[idx] = cnt
        bias_ref[idx] = jnp.where(cnt > 0.0, 0.0, NEG_INF)


def _mask_scores(products, scale, kind, rel_t, offset, near_tabs=None):
    st = products * (scale * LOG2E)
    cnt = None
    if kind == "diag":
        st = jnp.where(rel_t + offset >= 0, st, NEG_INF)
    elif kind == "far":
        st = jnp.where((rel_t & 15) == 0, st, NEG_INF)
    elif kind == "near":
        bias_ref, cnt_ref = near_tabs
        tk, tq = rel_t.shape
        idx = (offset - (tk - tq)) // tk
        st = st + bias_ref[idx]
        cnt = cnt_ref[idx]
    return st, cnt


def _attn_fwd(q, k, v, mla, scale, name, gather=()):
    s = q.shape[0]
    qw = 2 * LANE if mla else LANE
    tq, tk = ATT_TQ, ATT_TK
    reach, kind_near, kind_far = _block_kinds(mla)
    assert s % tq == 0 and tq % tk == 0 and reach % tk == 0 and reach in (0, NEAR_REACH)
    ng = len(gather)
    last_step = HEADS // 2 - 1

    def body(*refs):
        q_ref, k_ref, v_ref = refs[:3]
        o_ref, lse_ref = refs[3 + ng:5 + ng]
        vt_ref, st_ref = refs[5 + 2 * ng:7 + 2 * ng]
        near_tabs = None if mla else refs[7 + 2 * ng:9 + 2 * ng]
        n_tabs = 0 if mla else 2
        comm = (refs[3:3 + ng], refs[5 + ng:5 + 2 * ng]) + tuple(refs[7 + n_tabs + 2 * ng:])
        if ng:
            @pl.when(pl.program_id(0) == 0)
            def _():
                _Gather(*comm).start()

            @pl.when(pl.program_id(0) == last_step)
            def _():
                _Gather(*comm).forward()

        lane = lax.broadcasted_iota(I32, (1, LANE), 1)
        rel_t = lax.broadcasted_iota(I32, (tk, tq), 1) - lax.broadcasted_iota(I32, (tk, tq), 0)
        if not mla:
            _fill_near_tables(*near_tabs, rel_t)

        def transpose_v(j, carry):
            c0 = pl.multiple_of(j * tk, tk)
            vt_ref[:, pl.ds(c0, tk)] = v_ref[pl.ds(c0, tk), :].astype(F32).T.astype(BF16)
            return carry

        lax.fori_loop(0, s // tk, transpose_v, 0)

        def q_block(qi, carry):
            r0 = pl.multiple_of(qi * tq, tq)
            kcols = [slice(a * LANE, (a + 1) * LANE) if mla else slice(0, LANE) for a in range(2)]
            qas = [q_ref[pl.ds(r0, tq), kcols[a]] for a in range(2)]
            if not mla:
                qas = [jnp.where(lane < DIL_DIM, qas[0], jnp.zeros_like(qas[0])),
                       jnp.where(lane >= DIL_DIM, qas[1], jnp.zeros_like(qas[1]))]

            n_k = (r0 + tq) // tk

            def products(kj):
                c0 = pl.multiple_of(kj * tk, tk)
                return [lax.dot_general(k_ref[pl.ds(c0, tk), kcols[a]], qas[a], NT, preferred_element_type=F32)
                        for a in range(2)]

            for a, pr in enumerate(products(0)):
                st_ref[0, a] = pr

            def k_block(kj, c, kind):
                c0 = pl.multiple_of(kj * tk, tk)
                slot = kj & 1
                ahead = products(jnp.minimum(kj + 1, n_k - 1))
                out = []
                for a in range(2):
                    m, l, acc = c[a]
                    st, cnt = _mask_scores(st_ref[slot, a], scale, kind, rel_t, r0 - c0, near_tabs)
                    st_ref[1 - slot, a] = ahead[a]
                    m_new = jnp.maximum(m, jnp.max(st, axis=0, keepdims=True))
                    alpha = jnp.exp2(m - m_new)
                    p = jnp.exp2(st - m_new)
                    if cnt is not None:
                        p = p * cnt
                    l = alpha * l + jnp.sum(p, axis=0, keepdims=True)
                    vt = vt_ref[a * DIL_DIM:(a + 1) * DIL_DIM, pl.ds(c0, tk)]
                    acc = alpha * acc + jnp.dot(vt, p.astype(BF16), preferred_element_type=F32)
                    out.append((m_new, l, acc))
                return tuple(out)

            one = (jnp.full((1, tq), NEG_INF, F32), jnp.zeros((1, tq), F32), jnp.zeros((DIL_DIM, tq), F32))
            first_near = jnp.maximum((r0 - reach) // tk, 0)
            c = lax.fori_loop(0, first_near, functools.partial(k_block, kind=kind_far), (one, one))
            res = lax.fori_loop(first_near, (r0 + tq) // tk, functools.partial(k_block, kind=kind_near), c)
            o_t = jnp.concatenate([res[a][2] / res[a][1] for a in range(2)], axis=0)
            o_ref[pl.ds(r0, tq), :] = o_t.T.astype(BF16)
            for a in range(2):
                lse_ref[a, :, pl.ds(r0, tq)] = res[a][0] * LN2 + jnp.log(res[a][1])
            return carry

        lax.fori_loop(0, s // tq, q_block, 0)

        if ng:
            @pl.when(pl.program_id(0) == last_step)
            def _():
                _Gather(*comm).finish()

    return pl.pallas_call(
        body, name=name, grid=(HEADS // 2,),
        in_specs=[pl.BlockSpec((s, qw), lambda h: (0, h)), pl.BlockSpec((s, qw), lambda h: (0, h)),
                  pl.BlockSpec((s, LANE), lambda h: (0, h))] + [ANY] * ng,
        out_specs=[pl.BlockSpec((s, LANE), lambda h: (0, h)), pl.BlockSpec((2, 1, s), lambda h: (h, 0, 0))] + [ANY] * ng,
        out_shape=[jax.ShapeDtypeStruct((s, DIL_W), BF16), jax.ShapeDtypeStruct((HEADS, 1, s), F32)] + _Gather.out_shapes(gather),
        scratch_shapes=[pltpu.VMEM((LANE, s), BF16), pltpu.VMEM((2, 2, tk, tq), F32)]
        + ([] if mla else [pltpu.VMEM((_near_offsets(tk, tq), tk, tq), F32)] * 2) + (_Gather.scratch(gather) if ng else []),
        compiler_params=_params(("arbitrary",) if ng else ("parallel",), 12 << 20),
    )(*_in_hbm(q, k, v), *gather)


def _attn_bwd(q, k, v, o, do, do_block0, lse, mla, scale, name, scatter=()):
    s = q.shape[0]
    qw = 2 * LANE if mla else LANE
    tq, tk = ATT_TQ, ATT_TK_BWD
    nq = s // tq
    reach, kind_near, kind_far = _block_kinds(mla)
    assert s % tq == 0 and s % tk == 0
    ns = len(scatter)
    last_step = HEADS // 2 - 1

    def body(*refs):
        q_ref, k_ref, v_ref, o_ref, do_ref, lse_ref = refs[:6]
        dq_ref, dk_ref, dv_ref = refs[6 + ns:9 + ns]
        kt_ref, dot_ref, dob_ref, dqt_ref, delta_ref, lse2_ref = refs[9 + 2 * ns:15 + 2 * ns]
        near_tabs = None if mla else refs[15 + 2 * ns:17 + 2 * ns]
        n_tabs = 0 if mla else 2
        comm = (refs[6:6 + ns], refs[9 + ns:9 + 2 * ns]) + tuple(refs[15 + n_tabs + 2 * ns:])
        if ns:
            @pl.when(pl.program_id(0) == 0)
            def _():
                _Scatter(*comm).start()

        lane = lax.broadcasted_iota(I32, (1, LANE), 1)
        row = lax.broadcasted_iota(I32, (LANE, 1), 0)
        rel_t = lax.broadcasted_iota(I32, (tk, tq), 1) - lax.broadcasted_iota(I32, (tk, tq), 0)
        if not mla:
            _fill_near_tables(*near_tabs, rel_t)

        def prepare(j, carry):
            c0 = pl.multiple_of(j * tk, tk)
            do_blk = do_ref[pl.ds(c0, tk), :]
            dob_ref[pl.ds(c0, tk), :] = do_blk.astype(BF16)
            do_t = do_blk.T
            dot_ref[:, pl.ds(c0, tk)] = do_t.astype(BF16)
            prod = do_t * o_ref[pl.ds(c0, tk), :].astype(F32).T
            delta_ref[0, :, pl.ds(c0, tk)] = jnp.sum(prod[0:DIL_DIM], axis=0, keepdims=True)
            delta_ref[1, :, pl.ds(c0, tk)] = jnp.sum(prod[DIL_DIM:LANE], axis=0, keepdims=True)
            for w in range(qw // LANE):
                kt_ref[w * LANE:(w + 1) * LANE, pl.ds(c0, tk)] = (
                    k_ref[pl.ds(c0, tk), w * LANE:(w + 1) * LANE].astype(F32).T.astype(BF16))
            return carry

        lax.fori_loop(0, s // tk, prepare, 0)
        dqt_ref[...] = jnp.zeros_like(dqt_ref)
        lse2_ref[...] = lse_ref[...] * LOG2E

        sels = [lane < DIL_DIM, lane >= DIL_DIM]
        rsels = [row < DIL_DIM, row >= DIL_DIM]
        cols = [slice(a * LANE, (a + 1) * LANE) if mla else slice(0, LANE) for a in range(2)]

        def k_block(kj, carry):
            c0 = pl.multiple_of(kj * tk, tk)
            kas = [k_ref[pl.ds(c0, tk), cols[a]] for a in range(2)]
            kts = [kt_ref[cols[a], pl.ds(c0, tk)] for a in range(2)]
            if not mla:
                kas = [jnp.where(sels[a], kas[a], jnp.zeros_like(kas[a])) for a in range(2)]
                kts = [jnp.where(rsels[a], kts[a], jnp.zeros_like(kts[a])) for a in range(2)]
            vb = v_ref[pl.ds(c0, tk), :]
            vbs = [jnp.where(sels[a], vb, jnp.zeros_like(vb)) for a in range(2)]

            first = c0 // tq

            def q_block(qi, c, kind):
                r0 = pl.multiple_of(qi * tq, tq)
                out, dq_parts = [], []
                for a in range(2):
                    dk_acc, dv_acc = c[a]
                    qa = q_ref[pl.ds(r0, tq), cols[a]]
                    st, cnt = _scores_t(kas[a], qa, scale, kind, rel_t, r0 - c0, near_tabs)
                    p = jnp.exp2(st - lse2_ref[a, :, pl.ds(r0, tq)])
                    if cnt is not None:
                        p = p * cnt
                    dp = jnp.dot(vbs[a], dot_ref[:, pl.ds(r0, tq)], preferred_element_type=F32)
                    ds = (p * (dp - delta_ref[a, :, pl.ds(r0, tq)]) * scale).astype(BF16)
                    dv_acc = dv_acc + jnp.dot(p.astype(BF16), dob_ref[pl.ds(r0, tq), :], preferred_element_type=F32)
                    dk_acc = dk_acc + jnp.dot(ds, qa, preferred_element_type=F32)
                    dq_parts.append(jnp.dot(kts[a], ds, preferred_element_type=F32))
                    out.append((dk_acc, dv_acc))
                if mla:
                    for a in range(2):
                        dqt_ref[cols[a], pl.ds(r0, tq)] += dq_parts[a]
                else:
                    dqt_ref[:, pl.ds(r0, tq)] += dq_parts[0] + dq_parts[1]
                return tuple(out)

            zero = jnp.zeros((tk, LANE), F32)
            last_near = jnp.minimum((c0 + tk - 1 + reach) // tq + 1, nq)
            c = lax.fori_loop(first, last_near, functools.partial(q_block, kind=kind_near), ((zero, zero), (zero, zero)))
            (dk0, dv0), (dk1, dv1) = lax.fori_loop(last_near, nq, functools.partial(q_block, kind=kind_far), c)
            if mla:
                dk_ref[pl.ds(c0, tk), cols[0]] = dk0
                dk_ref[pl.ds(c0, tk), cols[1]] = dk1
            else:
                dk_ref[pl.ds(c0, tk), :] = jnp.where(sels[0], dk0, dk1)
            dv_ref[pl.ds(c0, tk), :] = jnp.where(sels[0], dv0, dv1)
            return carry

        lax.fori_loop(0, s // tk, k_block, 0)

        def write_dq(j, carry):
            c0 = pl.multiple_of(j * tk, tk)
            for w in range(qw // LANE):
                dq_ref[pl.ds(c0, tk), w * LANE:(w + 1) * LANE] = dqt_ref[w * LANE:(w + 1) * LANE, pl.ds(c0, tk)].T
            return carry

        lax.fori_loop(0, s // tk, write_dq, 0)

        if ns:
            @pl.when(pl.program_id(0) == last_step)
            def _():
                _Scatter(*comm).finish()

    b0 = do_block0
    return pl.pallas_call(
        body, name=name, grid=(HEADS // 2,),
        in_specs=[pl.BlockSpec((s, qw), lambda h: (0, h)), pl.BlockSpec((s, qw), lambda h: (0, h)),
                  pl.BlockSpec((s, LANE), lambda h: (0, h)), pl.BlockSpec((s, LANE), lambda h: (0, h)),
                  pl.BlockSpec((s, LANE), lambda h: (0, h + b0)), pl.BlockSpec((2, 1, s), lambda h: (h, 0, 0))] + [ANY] * ns,
        out_specs=[pl.BlockSpec((s, qw), lambda h: (0, h)), pl.BlockSpec((s, qw), lambda h: (0, h)),
                   pl.BlockSpec((s, LANE), lambda h: (0, h))] + [ANY] * ns,
        out_shape=[jax.ShapeDtypeStruct(q.shape, F32), jax.ShapeDtypeStruct(k.shape, F32), jax.ShapeDtypeStruct((s, DIL_W), F32)]
        + _Scatter.out_shapes(scatter),
        scratch_shapes=[pltpu.VMEM((qw, s), BF16), pltpu.VMEM((LANE, s), BF16), pltpu.VMEM((s, LANE), BF16),
                        pltpu.VMEM((qw, s), F32), pltpu.VMEM((2, 1, s), F32), pltpu.VMEM((2, 1, s), F32)]
        + ([] if mla else [pltpu.VMEM((_near_offsets(tk, tq), tk, tq), F32)] * 2) + (_Scatter.semaphores(ns) if ns else []),
        compiler_params=_params(("arbitrary",) if ns else ("parallel",), 24 << 20),
    )(*_in_hbm(q, k, v, o, do, lse), *scatter)


def _ada_bwd(c_all, dmod_shard):
    n, d = c_all.shape
    cols = dmod_shard.shape[1]

    def body(c_ref, g_ref, o_ref):
        cv = c_ref[...]
        o_ref[...] = lax.dot_general(cv * _sigmoid(cv), g_ref[...], TN, precision=HIGHEST, preferred_element_type=F32)

    return pl.pallas_call(
        body, name="ada_bwd", out_shape=jax.ShapeDtypeStruct((d, cols), F32),
        compiler_params=_params(None, 16 << 20),
    )(c_all, dmod_shard)


SMALL_WIDTHS = (("g_mix_norm", D_MODEL), ("g_q_lat", Q_LORA), ("g_kv_lat", KV_LORA), ("g_mla_q_nope", NOPE),
                ("g_mla_q_pe", ROPE), ("g_mla_k_nope", NOPE), ("g_mla_k_pe", ROPE), ("g_dil_q", DIL_DIM),
                ("g_dil_k", DIL_DIM), ("g_ffn_norm", D_MODEL), ("b_conv", UP_W))


def _small_layout():
    pieces = (("dmod", 6 * D_MODEL),) + SMALL_WIDTHS + tuple(("w_conv%d" % k, UP_W) for k in range(3)) + (("loss", 1),)
    layout, off = {}, 0
    for name, width in pieces:
        layout[name] = (width, off)
        off += -(-width // LANE) * LANE
    return layout, off


def _pack_small(acc1, acc2, dg2, dglat, dgains, dbg, dbv, dwg, dwv, loss_part):
    layout, total = _small_layout()

    def body(a1, a2, g2, gl, gg, bg, bv, wg, wv, ls, o_ref):
        o_ref[...] = jnp.zeros_like(o_ref)

        def put(name, src, shift=0):
            start = layout[name][1] + shift
            o_ref[:, start:start + src.shape[1]] = src

        for k, src in enumerate((a1[0:1, :], a1[1:2, :], a2[3:4, :], a2[0:1, :], a2[1:2, :], g2[...])):
            put("dmod", src, k * D_MODEL)
        put("g_mix_norm", a1[2:3, :])
        put("g_q_lat", gl[0:1, :])
        put("g_kv_lat", gl[1:2, 0:KV_LORA])
        put("g_mla_q_nope", gg[0:1, 0:NOPE])
        put("g_mla_q_pe", gg[5:6, 0:ROPE])
        put("g_mla_k_nope", gg[1:2, 0:NOPE])
        put("g_mla_k_pe", gg[2:3, 0:ROPE])
        put("g_dil_q", gg[3:4, 0:DIL_DIM])
        put("g_dil_k", gg[4:5, 0:DIL_DIM])
        put("g_ffn_norm", a2[2:3, :])
        put("b_conv", bg[...])
        put("b_conv", bv[...], D_FF)
        for k in range(3):
            put("w_conv%d" % k, wg[k:k + 1, :])
            put("w_conv%d" % k, wv[k:k + 1, :], D_FF)
        put("loss", ls[...])

    ins = (acc1, acc2, dg2, dglat, dgains, dbg, dbv, dwg, dwv, loss_part)
    return pl.pallas_call(
        body, name="pack_small", grid=(1,), in_specs=[_full(a.shape) for a in ins], out_specs=_full((1, total)),
        out_shape=jax.ShapeDtypeStruct((1, total), F32),
        compiler_params=_params(("arbitrary",), 2 << 20),
    )(*_in_hbm(*ins))


def _sum_unpack(g):
    n_dev, _, total = g.shape
    layout, _ = _small_layout()

    def body(g_ref, *refs):
        o_refs, s_ref = refs[:-1], refs[-1]
        acc = g_ref[0]
        for k in range(1, n_dev):
            acc = acc + g_ref[k]
        s_ref[...] = acc
        take = lambda name: s_ref[:, layout[name][1]:layout[name][1] + layout[name][0]]
        o_refs[0][...] = take("dmod")
        for i, (name, _) in enumerate(SMALL_WIDTHS):
            o_refs[1 + i][...] = take(name)
        for k in range(3):
            o_refs[-2][k:k + 1, :] = take("w_conv%d" % k)
        o_refs[-1][...] = take("loss")

    shapes = [(1, 6 * D_MODEL)] + [(1, w) for _, w in SMALL_WIDTHS] + [(3, UP_W), (1, 1)]
    return pl.pallas_call(
        body, name="sum_unpack", out_shape=[jax.ShapeDtypeStruct(sh, F32) for sh in shapes],
        scratch_shapes=[pltpu.VMEM((1, total), F32)],
        compiler_params=_params(None, 4 << 20),
    )(g)


def _adamw_math(w, g, m, v):
    mn = ADAM_B1 * m + (1.0 - ADAM_B1) * g
    vn = ADAM_B2 * v + (1.0 - ADAM_B2) * (g * g)
    m_hat = mn / (1.0 - ADAM_B1 ** ADAM_STEP)
    v_hat = vn / (1.0 - ADAM_B2 ** ADAM_STEP)
    return -ADAM_LR * (m_hat / (jnp.sqrt(v_hat) + ADAM_EPS) + ADAM_WD * w), mn, vn


def _adamw_vectors(ws, gs, ms, vs):
    k = len(ws)

    def body(*refs):
        for i in range(k):
            d, mn, vn = _adamw_math(refs[i][...], refs[k + i][...], refs[2 * k + i][...], refs[3 * k + i][...])
            refs[4 * k + i][...] = d
            refs[5 * k + i][...] = mn
            refs[6 * k + i][...] = vn

    blocks = [_full(w.shape) for w in ws]
    outs = pl.pallas_call(
        body, name="adamw_vectors", grid=(1,), in_specs=blocks * 4, out_specs=blocks * 3,
        out_shape=[jax.ShapeDtypeStruct(w.shape, F32) for w in ws] * 3,
        compiler_params=_params(("arbitrary",), 2 << 20),
    )(*_in_hbm(*ws, *gs, *ms, *vs))
    return outs[:k], outs[k:2 * k], outs[2 * k:]


def _adamw(w, g, m, v, name):
    r, c = w.shape
    tr = r
    for cand in (256, 128, 64, 32, 16):
        if r % cand == 0 and r > cand:
            tr = cand
            break

    def body(w_ref, g_ref, m_ref, v_ref, d_ref, mo_ref, vo_ref):
        d_ref[...], mo_ref[...], vo_ref[...] = _adamw_math(w_ref[...], g_ref[...], m_ref[...], v_ref[...])

    blk = pl.BlockSpec((tr, c), lambda i: (i, 0))
    return pl.pallas_call(
        body, name=name, grid=(r // tr,), in_specs=[blk] * 4, out_specs=[blk] * 3,
        out_shape=[jax.ShapeDtypeStruct((r, c), F32)] * 3,
        compiler_params=_params(("parallel",), 7 * _nbytes((tr, c), F32)),
    )(w, g, m, v)


def _position():
    return lax.axis_index("x"), lax.axis_index("y"), lax.axis_index("c")


def _other_chips(x, y):
    return [(1 - x, y, 2 * (1 - x) + y), (x, 1 - y, 2 * x + (1 - y)), (1 - x, 1 - y, 2 * (1 - x) + (1 - y))]


class _SmallGather:
    def __init__(self, v_ref, out_ref, send_sems, recv_sems, local_sem):
        x, y, c = _position()
        me = 4 * x + 2 * y + c
        self.local = pltpu.make_async_copy(v_ref, out_ref.at[me], local_sem)
        self.sends, self.arrivals = [], []
        for k in range(N_DEV - 1):
            fx, fy, fc = ((k + 1) >> 2) & 1, ((k + 1) >> 1) & 1, (k + 1) & 1
            px, py, pc = (1 - x if fx else x), (1 - y if fy else y), (1 - c if fc else c)

            def copy(dst, k=k, peer=(px, py, pc)):
                return pltpu.make_async_remote_copy(src_ref=v_ref, dst_ref=dst, send_sem=send_sems.at[k],
                                                    recv_sem=recv_sems.at[k], device_id=peer, device_id_type=MESH)

            self.sends.append(copy(out_ref.at[me]))
            self.arrivals.append(copy(out_ref.at[4 * px + 2 * py + pc]))

    @staticmethod
    def semaphores():
        return [pltpu.SemaphoreType.DMA((N_DEV - 1,)), pltpu.SemaphoreType.DMA((N_DEV - 1,)), pltpu.SemaphoreType.DMA]

    def start(self):
        self.local.start()
        for cp in self.sends:
            cp.start()

    def finish(self):
        for cp in self.arrivals:
            cp.wait_recv()
        for cp in self.sends:
            cp.wait_send()
        self.local.wait()


def _prologue(c_taps, w_ada_shard, b_shard, pos_col, rope_consts, shards):
    n = len(shards)
    s = pos_col.shape[0]
    cols = w_ada_shard.shape[1]
    freq, csel, ssel = rope_consts

    def body(*refs):
        ct_ref, w_ref, b_ref, p_ref, f_ref, cs_ref, ss_ref = refs[:7]
        sh_refs = refs[7:7 + n]
        ct_all_ref, mod_all_ref, tab_ref = refs[7 + n:10 + n]
        g_refs = refs[10 + n:10 + 2 * n]
        mod_blk_ref = refs[10 + 2 * n]
        sems = refs[11 + 2 * n:]
        weights = _Gather(sh_refs, g_refs, *sems[6:])
        weights.start()
        first = _SmallGather(ct_ref, ct_all_ref, *sems[0:3])
        first.start()
        first.finish()
        cv = ct_all_ref[:, 0, 0:D_MODEL]
        sc = (cv * _sigmoid(cv)).astype(BF16)
        mod_blk_ref[...] = jnp.dot(sc, w_ref[...].astype(BF16), preferred_element_type=F32) + b_ref[...]
        second = _SmallGather(mod_blk_ref, mod_all_ref, *sems[3:6])
        second.start()

        def table_rows(i, carry):
            r0 = pl.multiple_of(i * ROW_TILE, ROW_TILE)
            ang = p_ref[pl.ds(r0, ROW_TILE), :].astype(F32) * f_ref[...]
            tab_ref[pl.ds(r0, ROW_TILE), :] = cs_ref[...] * jnp.cos(ang) + ss_ref[...] * jnp.sin(ang)
            return carry

        lax.fori_loop(0, s // ROW_TILE, table_rows, 0)
        second.finish()
        weights.forward()
        weights.finish()

    return pl.pallas_call(
        body, name="prologue",
        out_shape=[jax.ShapeDtypeStruct((N_DEV,) + c_taps.shape, F32), jax.ShapeDtypeStruct((N_DEV, N_DEV, cols), F32),
                   jax.ShapeDtypeStruct((s, 4 * LANE), F32)] + _Gather.out_shapes(shards),
        in_specs=[IN_VMEM] * 7 + [ANY] * n, out_specs=[IN_VMEM] * 3 + [ANY] * n,
        scratch_shapes=[pltpu.VMEM((N_DEV, cols), F32)] + _SmallGather.semaphores() * 2 + _Gather.scratch(shards),
        compiler_params=_params(None, 14 << 20),
    )(c_taps, w_ada_shard, b_shard, pos_col, freq, csel, ssel, *shards)


IN_VMEM = pl.BlockSpec(memory_space=pltpu.VMEM)
ANY = pl.BlockSpec(memory_space=pl.ANY)


class _Gather:
    def __init__(self, w_refs, out_refs, send_sems, recv_sems, own_sems, *bounce_refs):
        x, y, c = _position()
        q0 = 2 * x + y
        sibling = (x, y, 1 - c)
        self.ici, self.ici_in, self.fwd, self.fwd_in, self.own_in, self.own_out = [], [], [], [], [], []
        for k, (w_ref, out_ref) in enumerate(zip(w_refs, out_refs)):
            half = w_ref.shape[0] // 2
            self.own_in.append(pltpu.make_async_copy(w_ref, bounce_refs[k], own_sems.at[2 * k]))
            self.own_out.append(pltpu.make_async_copy(bounce_refs[k], out_ref.at[q0], own_sems.at[2 * k + 1]))

            def blk(q, e, out_ref=out_ref, half=half):
                return out_ref.at[q, pl.ds(pl.multiple_of(e * half, 16), half), :]

            def copy(src, dst, i, to):
                return pltpu.make_async_remote_copy(src_ref=src, dst_ref=dst, send_sem=send_sems.at[i], recv_sem=recv_sems.at[i],
                                                    device_id=to, device_id_type=MESH)

            src = w_ref.at[pl.ds(pl.multiple_of(c * half, 16), half), :]
            for j, (cx, cy, qj) in enumerate(_other_chips(x, y)):
                self.ici.append(copy(src, blk(q0, c), 6 * k + j, (cx, cy, c)))
                self.ici_in.append(copy(blk(qj, c), blk(qj, c), 6 * k + j, (cx, cy, c)))
                self.fwd.append(copy(blk(qj, c), blk(qj, c), 6 * k + 3 + j, sibling))
                self.fwd_in.append(copy(blk(qj, 1 - c), blk(qj, 1 - c), 6 * k + 3 + j, sibling))

    @staticmethod
    def out_shapes(shards):
        return [jax.ShapeDtypeStruct((N_CHIP,) + s.shape, s.dtype) for s in shards]

    @staticmethod
    def scratch(shards):
        n = len(shards)
        return ([pltpu.SemaphoreType.DMA((6 * n,)), pltpu.SemaphoreType.DMA((6 * n,)), pltpu.SemaphoreType.DMA((2 * n,))]
                + [pltpu.VMEM(s.shape, s.dtype) for s in shards])

    def start(self):
        for cp in self.ici + self.own_in:
            cp.start()

    def forward(self):
        for fetched, placed in zip(self.own_in, self.own_out):
            fetched.wait()
            placed.start()
        for arrived, onward in zip(self.ici_in, self.fwd):
            arrived.wait_recv()
            onward.start()

    def finish(self):
        for cp in self.fwd_in:
            cp.wait_recv()
        for cp in self.ici + self.fwd:
            cp.wait_send()
        for cp in self.own_out:
            cp.wait()


class _PairSwap:
    def __init__(self, g_refs, out_refs, send_sems, recv_sems):
        x, y, c = _position()
        self.copies = [
            pltpu.make_async_remote_copy(src_ref=g_ref.at[:, 1 - c], dst_ref=out_ref, send_sem=send_sems.at[k],
                                         recv_sem=recv_sems.at[k], device_id=(x, y, 1 - c), device_id_type=MESH)
            for k, (g_ref, out_ref) in enumerate(zip(g_refs, out_refs))]

    @staticmethod
    def out_shapes(grads):
        return [jax.ShapeDtypeStruct((N_CHIP,) + g.shape[2:], g.dtype) for g in grads]

    @staticmethod
    def semaphores(n):
        return [pltpu.SemaphoreType.DMA((n,)), pltpu.SemaphoreType.DMA((n,))]

    def start(self):
        for cp in self.copies:
            cp.start()

    def finish(self):
        for cp in self.copies:
            cp.wait_recv()
        for cp in self.copies:
            cp.wait_send()


def _pair_sum(g, a, c_idx, name):
    _, _, rh, cols = g.shape
    tr = rh
    for cand in (256, 128, 64, 32, 16):
        if rh % cand == 0 and rh > cand:
            tr = cand
            break

    def body(c_ref, g_ref, a_ref, o_ref):
        o_ref[...] = (g_ref[...] + a_ref[...]).astype(BF16)

    return pl.pallas_call(
        body, name=name,
        grid_spec=pltpu.PrefetchScalarGridSpec(
            num_scalar_prefetch=1, grid=(N_CHIP, rh // tr),
            in_specs=[pl.BlockSpec((None, None, tr, cols), lambda q, i, c_ref: (q, c_ref[0], i, 0)),
                      pl.BlockSpec((None, tr, cols), lambda q, i, c_ref: (q, i, 0))],
            out_specs=pl.BlockSpec((None, tr, cols), lambda q, i, c_ref: (q, i, 0))),
        out_shape=jax.ShapeDtypeStruct((N_CHIP, rh, cols), BF16),
        compiler_params=_params(("parallel", "parallel"), 10 * _nbytes((tr, cols), F32)),
    )(c_idx, g, a)


def _scatter_and_gather(parts, small, name):
    n = len(parts)

    def body(*refs):
        scatter = _Scatter(refs[:n], refs[n + 1:2 * n + 1], *refs[2 * n + 2:2 * n + 4])
        gather = _SmallGather(refs[n], refs[2 * n + 1], *refs[2 * n + 4:])
        scatter.start()
        gather.start()
        gather.finish()
        scatter.finish()

    return pl.pallas_call(
        body, name=name,
        out_shape=_Scatter.out_shapes(parts) + [jax.ShapeDtypeStruct((N_DEV,) + small.shape, F32)],
        in_specs=[ANY] * n + [IN_VMEM], out_specs=[ANY] * n + [IN_VMEM],
        scratch_shapes=_Scatter.semaphores(n) + _SmallGather.semaphores(),
        compiler_params=_params(None, 10 * _nbytes(small.shape, F32)),
    )(*parts, small)


class _Scatter:
    def __init__(self, p_refs, out_refs, send_sems, recv_sems):
        x, y, c = _position()
        self.copies = []
        for k, (p_ref, out_ref) in enumerate(zip(p_refs, out_refs)):
            for j, (cx, cy, qj) in enumerate(_other_chips(x, y)):
                self.copies.append(pltpu.make_async_remote_copy(
                    src_ref=p_ref.at[qj], dst_ref=out_ref.at[j], send_sem=send_sems.at[3 * k + j],
                    recv_sem=recv_sems.at[3 * k + j], device_id=(cx, cy, c), device_id_type=MESH))

    @staticmethod
    def out_shapes(parts):
        return [jax.ShapeDtypeStruct((3,) + p.shape[1:], p.dtype) for p in parts]

    @staticmethod
    def semaphores(n):
        return [pltpu.SemaphoreType.DMA((3 * n,)), pltpu.SemaphoreType.DMA((3 * n,))]

    def start(self):
        for cp in self.copies:
            cp.start()

    def finish(self):
        for cp in self.copies:
            cp.wait_recv()
        for cp in self.copies:
            cp.wait_send()


def _shard_sum(p, b, qc_idx, name):
    _, rh, cols = p.shape
    tr = rh
    for cand in (256, 128, 64, 32, 16):
        if rh % cand == 0 and rh > cand:
            tr = cand
            break

    def body(qc_ref, p_ref, b_ref, o_ref):
        acc = p_ref[...].astype(F32)
        for j in range(3):
            acc = acc + b_ref[j].astype(F32)
        o_ref[...] = acc

    return pl.pallas_call(
        body, name=name,
        grid_spec=pltpu.PrefetchScalarGridSpec(
            num_scalar_prefetch=1, grid=(rh // tr,),
            in_specs=[pl.BlockSpec((None, tr, cols), lambda i, qc_ref: (qc_ref[0], i, 0)),
                      pl.BlockSpec((3, tr, cols), lambda i, qc_ref: (0, i, 0))],
            out_specs=pl.BlockSpec((None, tr, cols), lambda i, qc_ref: (qc_ref[1], i, 0))),
        out_shape=jax.ShapeDtypeStruct((2, rh, cols), F32),
        compiler_params=_params(("parallel",), 8 * _nbytes((tr, cols), F32)),
    )(qc_idx, p, b)


def _join_halves(shards):
    n = len(shards)

    def body(*refs):
        out_refs = refs[n:2 * n]
        send_sems, recv_sems = refs[2 * n:]
        x, y, c = _position()
        cps = [pltpu.make_async_remote_copy(src_ref=out_refs[k].at[c], dst_ref=out_refs[k].at[c], send_sem=send_sems.at[k],
                                            recv_sem=recv_sems.at[k], device_id=(x, y, 1 - c), device_id_type=MESH)
               for k in range(n)]
        for cp in cps:
            cp.start()
        for k in range(n):
            arriving = out_refs[k].at[1 - c]
            pltpu.make_async_remote_copy(src_ref=arriving, dst_ref=arriving, send_sem=send_sems.at[k], recv_sem=recv_sems.at[k],
                                         device_id=(x, y, 1 - c), device_id_type=MESH).wait_recv()
        for cp in cps:
            cp.wait_send()

    return pl.pallas_call(
        body, name="rs_join",
        out_shape=[jax.ShapeDtypeStruct(a.shape, a.dtype) for a in shards],
        in_specs=[ANY] * n, out_specs=[ANY] * n, input_output_aliases={k: k for k in range(n)},
        scratch_shapes=[pltpu.SemaphoreType.DMA((n,)), pltpu.SemaphoreType.DMA((n,))],
    )(*shards)


def _cols_from_shards(g):
    q, r, cs = g.shape
    return jnp.transpose(g, (1, 0, 2)).reshape(r, q * cs)


def _cols_to_shards(w):
    r, cfull = w.shape
    return jnp.transpose(w.reshape(r, N_CHIP, cfull // N_CHIP), (1, 0, 2))


def _pad_w_in(w):
    z = lambda n: jnp.zeros((w.shape[0], n), w.dtype)
    q_lat, kv_lat, kpe = w[:, 0:512], w[:, 512:768], w[:, 768:800]
    qd, kd, vd = w[:, 800:1312], w[:, 1312:1824], w[:, 1824:2336]
    return jnp.concatenate([q_lat, qd, kd, vd, kv_lat, z(KPE_OFF), kpe, z(LANE - KPE_OFF - ROPE)], axis=1)


def _pad_w_qb(w):
    w3 = w.reshape(Q_LORA, HEADS, NOPE + ROPE)
    return jnp.pad(w3, ((0, 0), (0, 0), (0, LANE - NOPE - ROPE))).reshape(Q_LORA, HEADS * LANE)


def _unpad_w_qb(g):
    return g.reshape(Q_LORA, HEADS, LANE)[:, :, :NOPE + ROPE].reshape(Q_LORA, HEADS * (NOPE + ROPE))


def _pad_w_kvb(w):
    w3 = w.reshape(KV_LORA, HEADS, 2 * NOPE)
    kp = jnp.pad(w3[:, :, :NOPE], ((0, 0), (0, 0), (0, LANE - NOPE))).reshape(KV_LORA, HEADS * LANE)
    return jnp.concatenate([kp, w3[:, :, NOPE:].reshape(KV_LORA, DIL_W)], axis=1)


def _unpad_w_kvb(g):
    gk = g[:, :HEADS * LANE].reshape(KV_LORA, HEADS, LANE)[:, :, :NOPE]
    gv = g[:, HEADS * LANE:].reshape(KV_LORA, HEADS, NOPE)
    return jnp.concatenate([gk, gv], axis=2).reshape(KV_LORA, HEADS * 2 * NOPE)


def _head_gains(g_q_nope, g_q_pe, g_k_nope, g_k_pe, g_dq, g_dk):
    z = lambda n: jnp.zeros((1, n), F32)
    q1 = jnp.concatenate([g_q_nope, g_q_pe, z(LANE - NOPE - ROPE)], axis=1)
    k1 = jnp.concatenate([g_k_nope, z(LANE - NOPE)], axis=1)
    kpe = jnp.concatenate([z(KPE_OFF), g_k_pe, z(LANE - KPE_OFF - ROPE)], axis=1)
    return dict(q=jnp.tile(q1, (1, HEADS)), k=jnp.tile(k1, (1, HEADS)), kpe=kpe,
                dq=jnp.tile(g_dq, (1, HEADS)), dk=jnp.tile(g_dk, (1, HEADS)))


def kernel(x, c, positions, w_ada, b_ada, g_mix_norm, w_in, g_q_lat, w_q_b, g_kv_lat, w_kv_b, g_mla_q_nope, g_mla_q_pe, g_mla_k_nope, g_mla_k_pe, g_dil_q, g_dil_k, w_o, g_ffn_norm, w_up, w_conv, b_conv, w_down, loss_target, m_w_ada, m_b_ada, m_g_mix_norm, m_w_in, m_g_q_lat, m_w_q_b, m_g_kv_lat, m_w_kv_b, m_g_mla_q_nope, m_g_mla_q_pe, m_g_mla_k_nope, m_g_mla_k_pe, m_g_dil_q, m_g_dil_k, m_w_o, m_g_ffn_norm, m_w_up, m_w_conv, m_b_conv, m_w_down, v_w_ada, v_b_ada, v_g_mix_norm, v_w_in, v_g_q_lat, v_w_q_b, v_g_kv_lat, v_w_kv_b, v_g_mla_q_nope, v_g_mla_q_pe, v_g_mla_k_nope, v_g_mla_k_pe, v_g_dil_q, v_g_dil_k, v_w_o, v_g_ffn_norm, v_w_up, v_w_conv, v_b_conv, v_w_down):
    args = dict(locals())
    weights = {n: args[n][0] for n in ("w_ada", "w_in", "w_q_b", "w_kv_b", "w_o", "w_up", "w_conv", "w_down")}
    small_w = {n: args[n] for n in ("b_ada",) + tuple(n for n, _ in SMALL_WIDTHS)}
    mom_m = {n[2:]: (args[n][0] if args[n].ndim == 3 else args[n]) for n in args if n.startswith("m_")}
    mom_v = {n[2:]: (args[n][0] if args[n].ndim == 3 else args[n]) for n in args if n.startswith("v_")}

    xi, yi, ci = _position()
    q0 = 2 * xi + yi
    me = 4 * xi + 2 * yi + ci
    xs, tgt = x[0], loss_target[0]
    s = xs.shape[0]
    consts = _seg_consts()
    c_idx, qc_idx = jnp.reshape(ci, (1,)).astype(I32), jnp.stack([q0, ci]).astype(I32)

    def halves(g4):
        q, r, cc = g4.shape
        return g4.reshape(q, 2, r // 2, cc)

    own_first = [weights[n].astype(BF16) for n in ("w_in", "w_q_b", "w_kv_b")]
    own_later = [weights[n].astype(BF16) for n in ("w_o", "w_up", "w_down")]
    conv_cols = UP_W // N_CHIP
    ada_cols = w_ada.shape[2]
    b_shard = lax.dynamic_slice_in_dim(b_ada, q0 * ada_cols, ada_cols, axis=1)
    c_taps = jnp.concatenate([c, weights["w_conv"].reshape(1, 3 * conv_cols)], axis=1)
    c_taps_all, mod_all, tab, *gathered = _prologue(c_taps, weights["w_ada"], b_shard, positions.reshape(s, 1),
                                                    _rope_consts(), own_first)
    c_all = c_taps_all[:, 0, :D_MODEL]
    w_conv_f = c_taps_all[:, 0, D_MODEL:].reshape(N_CHIP, 2, 3, conv_cols)[:, 0]
    w_conv_f = jnp.transpose(w_conv_f, (1, 0, 2)).reshape(3, UP_W)
    mod_all = mod_all.reshape(N_CHIP, 2, N_DEV, ada_cols)
    mod = lax.dynamic_index_in_dim(lax.dynamic_index_in_dim(mod_all, ci, 1, False), me, 1, False)
    mod = mod.reshape(1, N_CHIP * ada_cols)
    sh1, sc1, g1, sh2, sc2, g2 = [mod[:, k * D_MODEL:(k + 1) * D_MODEL] for k in range(6)]
    w_in_f = _cols_from_shards(gathered[0])
    w_in_p = _pad_w_in(w_in_f)
    w_qb_p = _pad_w_qb(_cols_from_shards(gathered[1]))
    w_kvb_p = _pad_w_kvb(_cols_from_shards(gathered[2]))
    gains = _head_gains(g_mla_q_nope, g_mla_q_pe, g_mla_k_nope, g_mla_k_pe, g_dil_q, g_dil_k)

    h = _prenorm(xs, g_mix_norm, sc1, sh1, "prenorm")
    proj = _mm(h, w_in_p, "nn", F32, 512, P_COLS, "mm_in")
    ql, kvl = _latnorm(proj, g_q_lat, g_kv_lat)
    q_raw = _mm(ql, w_qb_p, "nn", F32, 512, HEADS * LANE, "mm_qb")
    kv_raw = _mm(kvl, w_kvb_p, "nn", F32, 512, HEADS * LANE + DIL_W, "mm_kvb")
    qm, km, vm, qd, kd, vd = _attn_prep(q_raw, kv_raw, proj, tab, gains, consts)
    scale_m, scale_d = (NOPE + ROPE) ** -0.5, DIL_DIM ** -0.5
    o_m, lse_m, got_up = _attn_fwd(qm, km, vm, True, scale_m, "attn_mla", gather=own_later[1:2])
    o_d, lse_d, got_o, got_down = _attn_fwd(qd, kd, vd, False, scale_d, "attn_dil", gather=[own_later[0], own_later[2]])
    gathered = [got_o, got_up, got_down]
    w_o_f = gathered[0].reshape(D_MODEL, D_MODEL)
    w_up_f = _cols_from_shards(gathered[1])
    w_down_f = gathered[2].reshape(D_FF, D_MODEL)
    mix_in = jnp.concatenate([o_m, o_d], axis=1)
    mix = _mm(mix_in, w_o_f, "nn", F32, 512, D_MODEL, "mm_o")
    x1, h2 = _resid_prenorm(xs, mix, g1, g_ffn_norm, sc2, sh2)
    up = _mm(h2, w_up_f, "nn", F32, 512, CONV_TILE, "mm_up")
    act = _conv_gate(up, w_conv_f, b_conv)
    ffn = _mm(act, w_down_f, "nn", F32, 256, D_MODEL, "mm_down")
    dy, dffn, dg2, loss_part = _final(x1, ffn, tgt, g2)

    da = _mm(dffn, w_down_f, "nt", F32, 512, CONV_TILE, "mm_down_dx")
    gw_down = _mm(act, dffn, "tn", F32, 256, D_MODEL, "mm_down_dw")
    dup_g, dup_v, dbg, dbv, dwg, dwv = _gate_bwd(up, da, w_conv_f, b_conv)
    dup = jnp.concatenate([dup_g, dup_v], axis=1)
    early_names = ("w_up", "w_down", "w_o")
    gw_up = _mm(h2, dup, "tn", F32, 512, CONV_TILE, "mm_up_dw", col_shards=True)
    early = [halves(gw_up), halves(gw_down.reshape(N_CHIP, D_FF // N_CHIP, D_MODEL))]
    dh2, *early_sib = _mm(dup, w_up_f, "nt", F32, 256, 512, "mm_up_dx", swap=early, b_outer=True)
    dx1, dmix, acc2 = _ffnnorm_bwd(dh2, x1, dy, mix, g_ffn_norm, sc2, g1)
    gw_o = _mm(mix_in, dmix, "tn", F32, 512, D_MODEL, "mm_o_dw")
    early.append(halves(gw_o.reshape(N_CHIP, D_MODEL // N_CHIP, D_MODEL)))
    dmix_in, sib_o = _mm(dmix, w_o_f, "nt", F32, 512, D_MODEL, "mm_o_dx", swap=early[2:])
    early_sib.append(sib_o)
    early_sums = [_pair_sum(g, a, c_idx, "pair_sum_" + n) for g, a, n in zip(early, early_sib, early_names)]
    dqm, dkm, dvm, *early_recv = _attn_bwd(qm, km, vm, o_m, dmix_in, 0, lse_m, True, scale_m, "attn_mla_bwd",
                                           scatter=early_sums[:1])
    dqd, dkd, dvd, *early_recv_d = _attn_bwd(qd, kd, vd, o_d, dmix_in, DIL_W // LANE, lse_d, False, scale_d,
                                             "attn_dil_bwd", scatter=early_sums[1:])
    early_recv = early_recv + early_recv_d
    dq_raw, dkv_raw, dkpe_b, dqd_b, dkd_b, dvd_b, dgains = _attn_prep_bwd(
        dqm, dkm, dvm, dqd, dkd, dvd, q_raw, kv_raw, proj, tab, gains, consts)
    dql = _mm(dq_raw, w_qb_p, "nt", F32, 512, Q_LORA, "mm_qb_dx")
    gw_qb = _unpad_w_qb(_mm(ql, dq_raw, "tn", F32, Q_LORA, HEADS * LANE, "mm_qb_dw"))
    dkvl = _mm(dkv_raw, w_kvb_p, "nt", F32, 512, KV_LORA, "mm_kvb_dx")
    gw_kvb = _unpad_w_kvb(_mm(kvl, dkv_raw, "tn", F32, KV_LORA, HEADS * LANE + DIL_W, "mm_kvb_dw"))
    dqlat_b, dkvlat_b, dglat = _latnorm_bwd(dql, dkvl, proj, g_q_lat, g_kv_lat)
    dproj = jnp.concatenate([dqlat_b, dkvlat_b, dkpe_b[:, KPE_OFF:KPE_OFF + ROPE], dqd_b, dkd_b, dvd_b], axis=1)
    gw_in = _mm(h, dproj, "tn", F32, 512, IN_COLS, "mm_in_dw")
    late_names = ("w_in", "w_q_b", "w_kv_b")
    late = [halves(_cols_to_shards(gw_in)), halves(_cols_to_shards(gw_qb)), halves(_cols_to_shards(gw_kvb))]
    dh, *late_sib = _mm(dproj, w_in_f, "nt", F32, 512, D_MODEL, "mm_in_dx", swap=late)
    grad_x, acc1 = _mixnorm_bwd(dh, xs, dx1, g_mix_norm, sc1)

    packed = _pack_small(acc1, acc2, dg2, dglat, dgains, dbg, dbv, dwg, dwv, loss_part)
    late_sums = [_pair_sum(g, a, c_idx, "pair_sum_" + n) for g, a, n in zip(late, late_sib, late_names)]
    *late_recv, gathered_small = _scatter_and_gather(late_sums, packed, "rs_scatter_late")

    grad_b_ada, *small_grads, gconv_full, loss_sum = _sum_unpack(gathered_small)
    grads = {"b_ada": grad_b_ada}
    grads.update({n: g for (n, _), g in zip(SMALL_WIDTHS, small_grads)})
    shard_cols = UP_W // N_CHIP
    grads["w_conv"] = lax.dynamic_slice_in_dim(gconv_full, q0 * shard_cols, shard_cols, axis=1)
    dmod_all = gathered_small[:, 0, :6 * D_MODEL]
    grads["w_ada"] = _ada_bwd(c_all, lax.dynamic_slice_in_dim(dmod_all, q0 * ada_cols, ada_cols, axis=1))

    big_names = late_names + early_names
    half_sums = [_shard_sum(p, b, qc_idx, "shard_sum_" + n)
                 for p, b, n in zip(late_sums + early_sums, list(late_recv) + list(early_recv), big_names)]
    for n, full in zip(big_names, _join_halves(half_sums)):
        grads[n] = full.reshape(2 * full.shape[1], full.shape[2])

    delta, new_m, new_v = {}, {}, {}
    for n in ("w_ada", "w_in", "w_q_b", "w_kv_b", "w_o", "w_up", "w_conv", "w_down"):
        operands = (weights[n], grads[n], mom_m[n], mom_v[n])
        flipped = n in ("w_in", "w_q_b")
        if flipped:
            operands = [jnp.swapaxes(a, 0, 1) for a in operands]
            grads[n] = jnp.swapaxes(operands[1], 0, 1)
        if n == "w_ada":
            operands = _in_hbm(*operands)
        delta[n], new_m[n], new_v[n] = _adamw(*operands, "adamw_" + n)
        if flipped:
            delta[n], new_m[n], new_v[n] = (jnp.swapaxes(a, 0, 1) for a in (delta[n], new_m[n], new_v[n]))
    vec_names = ("b_ada",) + tuple(n for n, _ in SMALL_WIDTHS)
    sd, sm, sv = _adamw_vectors(*[[d_[n] for n in vec_names] for d_ in (small_w, grads, mom_m, mom_v)])
    for k, n in enumerate(vec_names):
        delta[n], new_m[n], new_v[n] = sd[k], sm[k], sv[k]

    loss = loss_sum[0, 0]
    order = ("w_ada", "b_ada", "g_mix_norm", "w_in", "g_q_lat", "w_q_b", "g_kv_lat", "w_kv_b", "g_mla_q_nope", "g_mla_q_pe",
             "g_mla_k_nope", "g_mla_k_pe", "g_dil_q", "g_dil_k", "w_o", "g_ffn_norm", "w_up", "w_conv", "b_conv", "w_down")
    lead = lambda n, z: z[None] if n.startswith("w_") else z
    outs = [loss, grad_x[None]]
    for d_ in (grads, delta, new_m, new_v):
        outs += [lead(n, d_[n]) for n in order]
    return tuple(outs)
```

```python
import functools

import numpy as np
import jax
import jax.numpy as jnp
from jax import lax
from jax.experimental import pallas as pl
from jax.experimental.pallas import tpu as pltpu

F32 = jnp.float32
BF16 = jnp.bfloat16
I32 = jnp.int32

D_MODEL = 1024
HEADS = 8
NOPE = 64
ROPE = 32
Q_LORA = 512
KV_LORA = 256
DIL_DIM = 64
DIL_W = HEADS * DIL_DIM
D_FF = 2816
UP_W = 2 * D_FF
IN_COLS = Q_LORA + KV_LORA + ROPE + 3 * DIL_W
ROPE_THETA = 10000.0
EPS = 1e-6
NEG_INF = -1e30
N_DEV = 8
N_CHIP = 4

ADAM_LR = 0.001
ADAM_B1 = 0.9
ADAM_B2 = 0.999
ADAM_EPS = 1e-08
ADAM_WD = 0.01
ADAM_STEP = 10

LANE = 128
ROW_TILE = 256
NORM_TILE = 512
ATT_TQ = 512
ATT_TK = 256
ATT_TK_BWD = 512
LOG2E = 1.4426950408889634
LN2 = 0.6931471805599453
VMEM_CAP = 56 * 1024 * 1024
VMEM_FLOOR = 32 * 1024 * 1024

P_QLAT, P_QD, P_KD, P_VD, P_KVLAT, P_KPE = 0, 512, 1024, 1536, 2048, 2304
P_COLS = 2432
KPE_OFF = 64

NN = (((1,), (0,)), ((), ()))
NT = (((1,), (1,)), ((), ()))
TN = (((0,), (0,)), ((), ()))
HIGHEST = lax.Precision.HIGHEST
MESH = pl.DeviceIdType.MESH


def _params(sem=None, est_bytes=0):
    limit = int(min(max(2 * est_bytes + (4 << 20), VMEM_FLOOR), VMEM_CAP))
    if sem is None:
        return pltpu.CompilerParams(vmem_limit_bytes=limit)
    return pltpu.CompilerParams(dimension_semantics=sem, vmem_limit_bytes=limit)


def _nbytes(shape, dtype):
    return int(np.prod(shape)) * jnp.dtype(dtype).itemsize


def _in_hbm(*xs):
    return [pltpu.with_memory_space_constraint(x, pltpu.HBM) for x in xs]


def _mm(a, b, dims, out_dtype, tm, tn, name, col_shards=False, swap=(), b_outer=False):
    def spec(block, index):
        if b_outer:
            return pl.BlockSpec(block, lambda g0, g1: index(g1, g0))
        return pl.BlockSpec(block, index)

    if dims == "nn":
        (m, k), (k2, n) = a.shape, b.shape
        a_spec = spec((tm, k), lambda i, j: (i, 0))
        b_spec = spec((k, tn), lambda i, j: (0, j))
        dn = NN
    elif dims == "nt":
        (m, k), (n, k2) = a.shape, b.shape
        a_spec = spec((tm, k), lambda i, j: (i, 0))
        b_spec = spec((tn, k), lambda i, j: (j, 0))
        dn = NT
    else:
        (k, m), (k2, n) = a.shape, b.shape
        a_spec = spec((k, tm), lambda i, j: (0, i))
        b_spec = spec((k, tn), lambda i, j: (0, j))
        dn = TN
    assert k == k2 and m % tm == 0 and n % tn == 0, (name, a.shape, b.shape, tm, tn)

    nw = len(swap)
    grid = (n // tn, m // tm) if b_outer else (m // tm, n // tn)

    def body(*refs):
        a_ref, b_ref, o_ref = refs[0], refs[1], refs[2 + nw]
        comm = (refs[2:2 + nw], refs[3 + nw:3 + 2 * nw]) + tuple(refs[3 + 2 * nw:])
        if nw:
            @pl.when((pl.program_id(0) == 0) & (pl.program_id(1) == 0))
            def _():
                _PairSwap(*comm).start()

        o_ref[...] = lax.dot_general(a_ref[...], b_ref[...], dn, preferred_element_type=F32).astype(o_ref.dtype)

        if nw:
            @pl.when((pl.program_id(0) == grid[0] - 1) & (pl.program_id(1) == grid[1] - 1))
            def _():
                _PairSwap(*comm).finish()

    est = _nbytes((tm, k), a.dtype) + _nbytes((tn, k), b.dtype) + _nbytes((tm, tn), F32) + _nbytes((tm, tn), out_dtype)
    if col_shards:
        out_spec = spec((None, tm, tn), lambda i, j: (j, i, 0))
        out_shape = jax.ShapeDtypeStruct((n // tn, m, tn), out_dtype)
    else:
        out_spec = spec((tm, tn), lambda i, j: (i, j))
        out_shape = jax.ShapeDtypeStruct((m, n), out_dtype)
    out = pl.pallas_call(
        body, name=name, grid=grid,
        in_specs=[a_spec, b_spec] + [ANY] * nw,
        out_specs=[out_spec] + [ANY] * nw,
        out_shape=[out_shape] + _PairSwap.out_shapes(swap),
        scratch_shapes=_PairSwap.semaphores(nw) if nw else [],
        compiler_params=_params(("arbitrary", "arbitrary") if nw else ("parallel", "parallel"), est),
    )(a, b, *swap)
    return out if nw else out[0]


def _seg_consts():
    seg_q = np.zeros((HEADS * LANE, LANE), np.float32)
    inv_q = np.zeros((1, LANE), np.float32)
    seg_k = np.zeros((HEADS * LANE, LANE), np.float32)
    inv_k = np.zeros((1, LANE), np.float32)
    seg_d = np.zeros((DIL_W, LANE), np.float32)
    inv_d = np.zeros((1, LANE), np.float32)
    for h in range(HEADS):
        seg_q[h * LANE:h * LANE + NOPE, 2 * h] = 1.0
        seg_q[h * LANE + NOPE:h * LANE + NOPE + ROPE, 2 * h + 1] = 1.0
        inv_q[0, 2 * h], inv_q[0, 2 * h + 1] = 1.0 / NOPE, 1.0 / ROPE
        seg_k[h * LANE:h * LANE + NOPE, h] = 1.0
        inv_k[0, h] = 1.0 / NOPE
        seg_d[h * DIL_DIM:(h + 1) * DIL_DIM, h] = 1.0
        inv_d[0, h] = 1.0 / DIL_DIM
    fold_q = np.tile(np.eye(LANE, dtype=np.float32), (HEADS, 1))
    fold_d = np.zeros((DIL_W, LANE), np.float32)
    fold_d[np.arange(DIL_W), np.arange(DIL_W) % DIL_DIM] = 1.0
    j = lambda v: jnp.asarray(v)
    b = lambda v: jnp.asarray(v, dtype=BF16)
    return dict(seg_q=b(seg_q), exp_q=b(seg_q.T.copy()), inv_q=j(inv_q), seg_k=b(seg_k), exp_k=b(seg_k.T.copy()),
                inv_k=j(inv_k), seg_d=b(seg_d), exp_d=b(seg_d.T.copy()), inv_d=j(inv_d), fold_q=j(fold_q), fold_d=j(fold_d))


def _rope_consts():
    inv_d = jnp.power(ROPE_THETA, -2.0 * jnp.arange(DIL_DIM // 2, dtype=F32) / DIL_DIM)
    inv_q = jnp.power(ROPE_THETA, -2.0 * jnp.arange(ROPE // 2, dtype=F32) / ROPE)
    lanes = np.arange(LANE)
    freq_d = inv_d[lanes % (DIL_DIM // 2)]
    in_pe = (lanes >= KPE_OFF) & (lanes < KPE_OFF + ROPE)
    freq_q = jnp.where(jnp.asarray(in_pe), inv_q[(lanes - KPE_OFF) % (ROPE // 2)], 0.0)
    sign_d = np.where(lanes % DIL_DIM < DIL_DIM // 2, -1.0, 1.0).astype(np.float32)
    sign_q = np.where(in_pe, np.where((lanes - KPE_OFF) < ROPE // 2, -1.0, 1.0), 0.0).astype(np.float32)
    zeros, ones = np.zeros(LANE, np.float32), np.ones(LANE, np.float32)
    freq = jnp.concatenate([freq_d, freq_d, freq_q, freq_q])[None, :]
    csel = jnp.asarray(np.concatenate([ones, zeros, ones, zeros]))[None, :]
    ssel = jnp.asarray(np.concatenate([zeros, sign_d, zeros, sign_q]))[None, :]
    return freq, csel, ssel


def _full(shape):
    return pl.BlockSpec(shape, lambda *_: (0,) * len(shape))


def _tile_lanes(x, n):
    return jnp.concatenate([x] * n, axis=1)


def _rms(x):
    return lax.rsqrt(jnp.mean(x * x, axis=-1, keepdims=True) + EPS)


def _prenorm(x, gain, scale, shift, name):
    s, d = x.shape

    def body(x_ref, g_ref, sc_ref, sh_ref, h_ref):
        xv = x_ref[...]
        h = (xv * _rms(xv)) * g_ref[...] * (1.0 + sc_ref[...]) + sh_ref[...]
        h_ref[...] = h.astype(BF16)

    row = pl.BlockSpec((NORM_TILE, d), lambda i: (i, 0))
    return pl.pallas_call(
        body, name=name, grid=(s // NORM_TILE,),
        in_specs=[row, _full((1, d)), _full((1, d)), _full((1, d))],
        out_specs=row, out_shape=jax.ShapeDtypeStruct((s, d), BF16),
        compiler_params=_params(("parallel",)),
    )(x, gain, scale, shift)


def _latnorm(proj, g_q, g_kv):
    s = proj.shape[0]

    def body(q_ref, kv_ref, gq_ref, gkv_ref, ql_ref, kvl_ref):
        q, kv = q_ref[...], kv_ref[...]
        ql_ref[...] = ((q * _rms(q)) * gq_ref[...]).astype(BF16)
        kvl_ref[...] = ((kv * _rms(kv)) * gkv_ref[...]).astype(BF16)

    return pl.pallas_call(
        body, name="latnorm", grid=(s // NORM_TILE,),
        in_specs=[pl.BlockSpec((NORM_TILE, Q_LORA), lambda i: (i, P_QLAT // Q_LORA)),
                  pl.BlockSpec((NORM_TILE, KV_LORA), lambda i: (i, P_KVLAT // KV_LORA)),
                  _full((1, Q_LORA)), _full((1, KV_LORA))],
        out_specs=[pl.BlockSpec((NORM_TILE, Q_LORA), lambda i: (i, 0)), pl.BlockSpec((NORM_TILE, KV_LORA), lambda i: (i, 0))],
        out_shape=[jax.ShapeDtypeStruct((s, Q_LORA), BF16), jax.ShapeDtypeStruct((s, KV_LORA), BF16)],
        compiler_params=_params(("parallel",)),
    )(proj, proj, g_q, g_kv)


def _dot01(v, mat01):
    hi = v.astype(BF16)
    lo = (v - hi.astype(F32)).astype(BF16)
    return jnp.dot(hi, mat01, preferred_element_type=F32) + jnp.dot(lo, mat01, preferred_element_type=F32)


def _seg_rinv(x, seg, exp, inv):
    r = lax.rsqrt(_dot01(x * x, seg) * inv + EPS)
    return _dot01(r, exp)


def _seg_mean(v, seg, exp, inv):
    return _dot01(_dot01(v, seg) * inv, exp)


def _swap_halves(x, half):
    n = x.shape[1]
    lane = lax.broadcasted_iota(I32, (1, n), 1)
    first = (lane & (2 * half - 1)) < half
    return jnp.where(first, pltpu.roll(x, n - half, 1), pltpu.roll(x, half, 1))


def _rope(x, cos, sin_signed, half):
    return x * cos + _swap_halves(x, half) * sin_signed


def _rope_bwd(dy, cos, sin_signed, half):
    return dy * cos + _swap_halves(dy * sin_signed, half)


def _pe_lane_mask(n):
    lane = lax.broadcasted_iota(I32, (1, n), 1) & (LANE - 1)
    return (lane >= KPE_OFF) & (lane < KPE_OFF + ROPE)


def _attn_prep(q_raw, kv_raw, proj, tab, gains, consts):
    s = q_raw.shape[0]
    hw = HEADS * LANE

    def body(q_ref, kv_ref, kpe_ref, qd_ref, kd_ref, vd_ref, tab_ref,
             gq_ref, gk_ref, gkpe_ref, gdq_ref, gdk_ref,
             segq_ref, expq_ref, invq_ref, segk_ref, expk_ref, invk_ref, segd_ref, expd_ref, invd_ref,
             qm_ref, km_ref, vm_ref, qdo_ref, kdo_ref, vdo_ref):
        tab_v = tab_ref[...]
        cos_d, sin_d = _tile_lanes(tab_v[:, 0:LANE], DIL_W // LANE), _tile_lanes(tab_v[:, LANE:2 * LANE], DIL_W // LANE)
        cos_q1, sin_q1 = tab_v[:, 2 * LANE:3 * LANE], tab_v[:, 3 * LANE:4 * LANE]
        cos_q, sin_q = _tile_lanes(cos_q1, HEADS), _tile_lanes(sin_q1, HEADS)

        q = q_ref[...]
        qn = q * _seg_rinv(q, segq_ref[...], expq_ref[...], invq_ref[...]) * gq_ref[...]
        qm_ref[...] = _rope(qn, cos_q, sin_q, ROPE // 2).astype(BF16)

        kv = kv_ref[...]
        kp = kv[:, :hw]
        kn = kp * _seg_rinv(kp, segk_ref[...], expk_ref[...], invk_ref[...]) * gk_ref[...]
        kpe = kpe_ref[...]
        r_pe = lax.rsqrt(jnp.sum(kpe * kpe, axis=-1, keepdims=True) * (1.0 / ROPE) + EPS)
        kpe_r = _rope(kpe * r_pe * gkpe_ref[...], cos_q1, sin_q1, ROPE // 2)
        km_ref[...] = (kn + _tile_lanes(kpe_r, HEADS)).astype(BF16)
        vm_ref[...] = kv[:, hw:].astype(BF16)

        qd = qd_ref[...]
        qdn = qd * _seg_rinv(qd, segd_ref[...], expd_ref[...], invd_ref[...]) * gdq_ref[...]
        qdo_ref[...] = _rope(qdn, cos_d, sin_d, DIL_DIM // 2).astype(BF16)
        kd = kd_ref[...]
        kdn = kd * _seg_rinv(kd, segd_ref[...], expd_ref[...], invd_ref[...]) * gdk_ref[...]
        kdo_ref[...] = _rope(kdn, cos_d, sin_d, DIL_DIM // 2).astype(BF16)
        vdo_ref[...] = vd_ref[...].astype(BF16)

    t = ROW_TILE
    row = lambda w, cb=0: pl.BlockSpec((t, w), lambda i: (i, cb))
    c = consts
    return pl.pallas_call(
        body, name="attn_prep", grid=(s // t,),
        in_specs=[row(hw), row(hw + DIL_W), row(LANE, P_KPE // LANE), row(DIL_W, P_QD // DIL_W), row(DIL_W, P_KD // DIL_W),
                  row(DIL_W, P_VD // DIL_W), row(4 * LANE),
                  _full((1, hw)), _full((1, hw)), _full((1, LANE)), _full((1, DIL_W)), _full((1, DIL_W)),
                  _full((hw, LANE)), _full((LANE, hw)), _full((1, LANE)), _full((hw, LANE)), _full((LANE, hw)), _full((1, LANE)),
                  _full((DIL_W, LANE)), _full((LANE, DIL_W)), _full((1, LANE))],
        out_specs=[row(hw), row(hw), row(DIL_W), row(DIL_W), row(DIL_W), row(DIL_W)],
        out_shape=[jax.ShapeDtypeStruct((s, hw), BF16), jax.ShapeDtypeStruct((s, hw), BF16)]
        + [jax.ShapeDtypeStruct((s, DIL_W), BF16)] * 4,
        compiler_params=_params(("parallel",), 24 << 20),
    )(*_in_hbm(q_raw, kv_raw, proj, proj, proj, proj), tab, gains["q"], gains["k"], gains["kpe"], gains["dq"], gains["dk"],
      c["seg_q"], c["exp_q"], c["inv_q"], c["seg_k"], c["exp_k"], c["inv_k"], c["seg_d"], c["exp_d"], c["inv_d"])


def _attn_prep_bwd(dqm, dkm, dvm, dqd, dkd, dvd, q_raw, kv_raw, proj, tab, gains, consts):
    s = q_raw.shape[0]
    hw = HEADS * LANE
    n_steps = s // ROW_TILE

    def body(dqm_ref, dkm_ref, dvm_ref, dqd_ref, dkd_ref, dvd_ref, q_ref, kv_ref, kpe_ref, qd_ref, kd_ref, tab_ref,
             gq_ref, gk_ref, gkpe_ref, gdq_ref, gdk_ref,
             segq_ref, expq_ref, invq_ref, segk_ref, expk_ref, invk_ref, segd_ref, expd_ref, invd_ref, foldq_ref, foldd_ref,
             dq_ref, dkv_ref, dkpe_ref, dqdo_ref, dkdo_ref, dvdo_ref, dg_ref, acc_ref):
        i = pl.program_id(0)

        @pl.when(i == 0)
        def _():
            acc_ref[...] = jnp.zeros_like(acc_ref)

        tab_v = tab_ref[...]
        cos_d, sin_d = _tile_lanes(tab_v[:, 0:LANE], DIL_W // LANE), _tile_lanes(tab_v[:, LANE:2 * LANE], DIL_W // LANE)
        cos_q1, sin_q1 = tab_v[:, 2 * LANE:3 * LANE], tab_v[:, 3 * LANE:4 * LANE]
        cos_q, sin_q = _tile_lanes(cos_q1, HEADS), _tile_lanes(sin_q1, HEADS)

        def norm_bwd(x, dyg, gain, seg, exp, inv):
            rinv = _seg_rinv(x, seg, exp, inv)
            xn = x * rinv
            dxn = dyg * gain
            dx = rinv * (dxn - xn * _seg_mean(dxn * xn, seg, exp, inv))
            return dx, jnp.sum(dyg * xn, axis=0, keepdims=True)

        dq, gq_l = norm_bwd(q_ref[...], _rope_bwd(dqm_ref[...], cos_q, sin_q, ROPE // 2), gq_ref[...],
                            segq_ref[...], expq_ref[...], invq_ref[...])
        dq_ref[...] = dq.astype(BF16)

        dkm = dkm_ref[...]
        kv = kv_ref[...]
        dkp, gk_l = norm_bwd(kv[:, :hw], dkm, gk_ref[...], segk_ref[...], expk_ref[...], invk_ref[...])
        dkv_ref[:, :hw] = dkp.astype(BF16)
        dkv_ref[:, hw:] = dvm_ref[...].astype(BF16)

        dkpe_r = dkm[:, 0:LANE]
        for h in range(1, HEADS):
            dkpe_r = dkpe_r + dkm[:, h * LANE:(h + 1) * LANE]
        dkpe_r = jnp.where(_pe_lane_mask(LANE), dkpe_r, 0.0)
        dyg = _rope_bwd(dkpe_r, cos_q1, sin_q1, ROPE // 2)
        kpe = kpe_ref[...]
        r_pe = lax.rsqrt(jnp.sum(kpe * kpe, axis=-1, keepdims=True) * (1.0 / ROPE) + EPS)
        xn = kpe * r_pe
        dxn = dyg * gkpe_ref[...]
        dkpe = r_pe * (dxn - xn * (jnp.sum(dxn * xn, axis=-1, keepdims=True) * (1.0 / ROPE)))
        dkpe_ref[...] = dkpe.astype(BF16)
        gkpe_l = jnp.sum(dyg * xn, axis=0, keepdims=True)

        dqd_v, gdq_l = norm_bwd(qd_ref[...], _rope_bwd(dqd_ref[...], cos_d, sin_d, DIL_DIM // 2), gdq_ref[...],
                                segd_ref[...], expd_ref[...], invd_ref[...])
        dqdo_ref[...] = dqd_v.astype(BF16)
        dkd_v, gdk_l = norm_bwd(kd_ref[...], _rope_bwd(dkd_ref[...], cos_d, sin_d, DIL_DIM // 2), gdk_ref[...],
                                segd_ref[...], expd_ref[...], invd_ref[...])
        dkdo_ref[...] = dkd_v.astype(BF16)
        dvdo_ref[...] = dvd_ref[...].astype(BF16)

        acc_ref[0:1, :] += gq_l
        acc_ref[1:2, :] += gk_l
        acc_ref[2:3, 0:LANE] += gkpe_l
        acc_ref[3:4, 0:DIL_W] += gdq_l
        acc_ref[4:5, 0:DIL_W] += gdk_l

        @pl.when(i == n_steps - 1)
        def _():
            acc = acc_ref[...]
            fq = jnp.dot(acc, foldq_ref[...], precision=HIGHEST, preferred_element_type=F32)
            fd = jnp.dot(acc[:, 0:DIL_W], foldd_ref[...], precision=HIGHEST, preferred_element_type=F32)
            rows = lax.broadcasted_iota(I32, (8, LANE), 0)
            base = jnp.where(rows < 2, fq, jnp.where(rows == 2, acc[:, 0:LANE], fd))
            at0 = pltpu.roll(base, LANE - KPE_OFF, 1)
            dg_ref[...] = jnp.where(rows == 5, pltpu.roll(at0, 5, 0), jnp.where(rows == 2, at0, base))

    t = ROW_TILE
    row = lambda w, cb=0: pl.BlockSpec((t, w), lambda i: (i, cb))
    c = consts
    return pl.pallas_call(
        body, name="attn_prep_bwd", grid=(n_steps,),
        in_specs=[row(hw), row(hw), row(DIL_W), row(DIL_W), row(DIL_W), row(DIL_W),
                  row(hw), row(hw + DIL_W), row(LANE, P_KPE // LANE), row(DIL_W, P_QD // DIL_W), row(DIL_W, P_KD // DIL_W),
                  row(4 * LANE),
                  _full((1, hw)), _full((1, hw)), _full((1, LANE)), _full((1, DIL_W)), _full((1, DIL_W)),
                  _full((hw, LANE)), _full((LANE, hw)), _full((1, LANE)), _full((hw, LANE)), _full((LANE, hw)), _full((1, LANE)),
                  _full((DIL_W, LANE)), _full((LANE, DIL_W)), _full((1, LANE)), _full((hw, LANE)), _full((DIL_W, LANE))],
        out_specs=[row(hw), row(hw + DIL_W), row(LANE), row(DIL_W), row(DIL_W), row(DIL_W), _full((8, LANE))],
        out_shape=[jax.ShapeDtypeStruct((s, hw), BF16), jax.ShapeDtypeStruct((s, hw + DIL_W), BF16),
                   jax.ShapeDtypeStruct((s, LANE), BF16)] + [jax.ShapeDtypeStruct((s, DIL_W), BF16)] * 3
        + [jax.ShapeDtypeStruct((8, LANE), F32)],
        scratch_shapes=[pltpu.VMEM((8, hw), F32)],
        compiler_params=_params(("arbitrary",), 28 << 20),
    )(*_in_hbm(dqm, dkm, dvm, dqd, dkd, dvd, q_raw, kv_raw, proj, proj, proj), tab,
      gains["q"], gains["k"], gains["kpe"], gains["dq"], gains["dk"],
      c["seg_q"], c["exp_q"], c["inv_q"], c["seg_k"], c["exp_k"], c["inv_k"], c["seg_d"], c["exp_d"], c["inv_d"],
      c["fold_q"], c["fold_d"])


def _latnorm_bwd(dql, dkvl, proj, g_q, g_kv):
    s = proj.shape[0]
    n_steps = s // NORM_TILE

    def body(dql_ref, dkvl_ref, q_ref, kv_ref, gq_ref, gkv_ref, dq_ref, dkv_ref, dg_ref):
        i = pl.program_id(0)

        @pl.when(i == 0)
        def _():
            dg_ref[...] = jnp.zeros_like(dg_ref)

        def one(x, dyg, gain):
            r = _rms(x)
            xn = x * r
            dxn = dyg * gain
            dx = r * (dxn - xn * jnp.mean(dxn * xn, axis=-1, keepdims=True))
            return dx, jnp.sum(dyg * xn, axis=0, keepdims=True)

        dq, gq_l = one(q_ref[...], dql_ref[...], gq_ref[...])
        dkv, gkv_l = one(kv_ref[...], dkvl_ref[...], gkv_ref[...])
        dq_ref[...] = dq.astype(BF16)
        dkv_ref[...] = dkv.astype(BF16)
        dg_ref[0:1, :] += gq_l
        dg_ref[1:2, 0:KV_LORA] += gkv_l

    t = NORM_TILE
    return pl.pallas_call(
        body, name="latnorm_bwd", grid=(n_steps,),
        in_specs=[pl.BlockSpec((t, Q_LORA), lambda i: (i, 0)), pl.BlockSpec((t, KV_LORA), lambda i: (i, 0)),
                  pl.BlockSpec((t, Q_LORA), lambda i: (i, P_QLAT // Q_LORA)),
                  pl.BlockSpec((t, KV_LORA), lambda i: (i, P_KVLAT // KV_LORA)),
                  _full((1, Q_LORA)), _full((1, KV_LORA))],
        out_specs=[pl.BlockSpec((t, Q_LORA), lambda i: (i, 0)), pl.BlockSpec((t, KV_LORA), lambda i: (i, 0)), _full((8, Q_LORA))],
        out_shape=[jax.ShapeDtypeStruct((s, Q_LORA), BF16), jax.ShapeDtypeStruct((s, KV_LORA), BF16),
                   jax.ShapeDtypeStruct((8, Q_LORA), F32)],
        compiler_params=_params(("arbitrary",)),
    )(dql, dkvl, proj, proj, g_q, g_kv)


def _resid_prenorm(x, mix, g1, gain, scale, shift):
    s, d = x.shape

    def body(x_ref, mix_ref, g1_ref, g_ref, sc_ref, sh_ref, x1_ref, h_ref):
        x1 = x_ref[...] + g1_ref[...] * mix_ref[...]
        x1_ref[...] = x1
        h_ref[...] = ((x1 * _rms(x1)) * g_ref[...] * (1.0 + sc_ref[...]) + sh_ref[...]).astype(BF16)

    row = pl.BlockSpec((NORM_TILE, d), lambda i: (i, 0))
    vec = _full((1, d))
    return pl.pallas_call(
        body, name="resid_prenorm", grid=(s // NORM_TILE,),
        in_specs=[row, row, vec, vec, vec, vec], out_specs=[row, row],
        out_shape=[jax.ShapeDtypeStruct((s, d), F32), jax.ShapeDtypeStruct((s, d), BF16)],
        compiler_params=_params(("parallel",)),
    )(x, mix, g1, gain, scale, shift)


CONV_TILE = 1408
HALO = 8


def _shift_down(x, halo, k):
    t = x.shape[0]
    row = lax.broadcasted_iota(I32, (t, 1), 0)
    out = pltpu.roll(x, k, 0)
    for r in range(k):
        out = jnp.where(row == r, halo[HALO - k + r:HALO - k + r + 1, :], out)
    return out


def _shift_up(x, halo, k):
    t = x.shape[0]
    row = lax.broadcasted_iota(I32, (t, 1), 0)
    out = pltpu.roll(x, t - k, 0)
    for r in range(k):
        out = jnp.where(row == t - k + r, halo[r:r + 1, :], out)
    return out


def _conv_fwd(x, halo, w, b):
    p1, p2 = _shift_down(x, halo, 1), _shift_down(x, halo, 2)
    u = b + p2 * w[0:1, :]
    u = u + p1 * w[1:2, :]
    u = u + x * w[2:3, :]
    return u, p1, p2


def _sigmoid(x):
    return 0.5 * jnp.tanh(0.5 * x) + 0.5


def _conv_gate(up, w_conv, b_conv):
    s = up.shape[0]
    t = ROW_TILE
    nj = D_FF // CONV_TILE
    hb = t // HALO

    def body(g_ref, v_ref, gh_ref, vh_ref, wg_ref, wv_ref, bg_ref, bv_ref, a_ref):
        live = (pl.program_id(0) > 0).astype(F32)
        ug, _, _ = _conv_fwd(g_ref[...], gh_ref[...] * live, wg_ref[...], bg_ref[...])
        uv, _, _ = _conv_fwd(v_ref[...], vh_ref[...] * live, wv_ref[...], bv_ref[...])
        a_ref[...] = (ug * _sigmoid(ug) * uv).astype(BF16)

    main = lambda off: pl.BlockSpec((t, CONV_TILE), lambda i, j: (i, j + off))
    halo = lambda off: pl.BlockSpec((HALO, CONV_TILE), lambda i, j: (jnp.maximum(i * hb - 1, 0), j + off))
    wsp = lambda off: pl.BlockSpec((3, CONV_TILE), lambda i, j: (0, j + off))
    bsp = lambda off: pl.BlockSpec((1, CONV_TILE), lambda i, j: (0, j + off))
    return pl.pallas_call(
        body, name="conv_gate", grid=(s // t, nj),
        in_specs=[main(0), main(nj), halo(0), halo(nj), wsp(0), wsp(nj), bsp(0), bsp(nj)],
        out_specs=pl.BlockSpec((t, CONV_TILE), lambda i, j: (i, j)),
        out_shape=jax.ShapeDtypeStruct((s, D_FF), BF16),
        compiler_params=_params(("parallel", "parallel"), 12 << 20),
    )(up, up, up, up, w_conv, w_conv, b_conv, b_conv)


def _gate_bwd(up, da, w_conv, b_conv):
    s = up.shape[0]
    t = ROW_TILE
    nj = D_FF // CONV_TILE
    hb = t // HALO
    n_i = s // t

    def body(g_ref, v_ref, gh_ref, vh_ref, gn_ref, vn_ref, da_ref, dan_ref, wg_ref, wv_ref, bg_ref, bv_ref,
             dupg_ref, dupv_ref, dbg_ref, dbv_ref, dwg_ref, dwv_ref):
        i = pl.program_id(1)

        @pl.when(i == 0)
        def _():
            for r in (dbg_ref, dbv_ref, dwg_ref, dwv_ref):
                r[...] = jnp.zeros_like(r)

        def d_gate(ug, uv, da_v):
            sg = _sigmoid(ug)
            return da_v * uv * (sg * (1.0 + ug * (1.0 - sg))), da_v * (ug * sg)

        live = (i > 0).astype(F32)
        xg, xv = g_ref[...], v_ref[...]
        wg, wv = wg_ref[...], wv_ref[...]
        ug, g1, g2 = _conv_fwd(xg, gh_ref[...] * live, wg, bg_ref[...])
        uv, v1, v2 = _conv_fwd(xv, vh_ref[...] * live, wv, bv_ref[...])
        dug, duv = d_gate(ug, uv, da_ref[...])

        more = (i < n_i - 1).astype(F32)
        ug_n, _, _ = _conv_fwd(gn_ref[...], xg[t - HALO:, :], wg, bg_ref[...])
        uv_n, _, _ = _conv_fwd(vn_ref[...], xv[t - HALO:, :], wv, bv_ref[...])
        dug_n, duv_n = d_gate(ug_n, uv_n, dan_ref[...] * more)

        def conv_t(du, du_n, w):
            return du * w[2:3, :] + _shift_up(du, du_n, 1) * w[1:2, :] + _shift_up(du, du_n, 2) * w[0:1, :]

        dupg_ref[...] = conv_t(dug, dug_n, wg).astype(BF16)
        dupv_ref[...] = conv_t(duv, duv_n, wv).astype(BF16)
        csum = lambda z: jnp.sum(z, axis=0, keepdims=True)
        dbg_ref[...] += csum(dug)
        dbv_ref[...] += csum(duv)
        dwg_ref[0:1, :] += csum(dug * g2)
        dwg_ref[1:2, :] += csum(dug * g1)
        dwg_ref[2:3, :] += csum(dug * xg)
        dwv_ref[0:1, :] += csum(duv * v2)
        dwv_ref[1:2, :] += csum(duv * v1)
        dwv_ref[2:3, :] += csum(duv * xv)

    last_halo = s // HALO - 1
    main = lambda off: pl.BlockSpec((t, CONV_TILE), lambda j, i: (i, j + off))
    halo = lambda off: pl.BlockSpec((HALO, CONV_TILE), lambda j, i: (jnp.maximum(i * hb - 1, 0), j + off))
    nxt = lambda off: pl.BlockSpec((HALO, CONV_TILE), lambda j, i: (jnp.minimum((i + 1) * hb, last_halo), j + off))
    wsp = lambda off: pl.BlockSpec((3, CONV_TILE), lambda j, i: (0, j + off))
    bsp = lambda off: pl.BlockSpec((1, CONV_TILE), lambda j, i: (0, j + off))
    outs = pl.pallas_call(
        body, name="gate_bwd", grid=(nj, n_i),
        in_specs=[main(0), main(nj), halo(0), halo(nj), nxt(0), nxt(nj), main(0), nxt(0),
                  wsp(0), wsp(nj), bsp(0), bsp(nj)],
        out_specs=[main(0), main(0),
                   pl.BlockSpec((1, CONV_TILE), lambda j, i: (0, j)), pl.BlockSpec((1, CONV_TILE), lambda j, i: (0, j)),
                   pl.BlockSpec((3, CONV_TILE), lambda j, i: (0, j)), pl.BlockSpec((3, CONV_TILE), lambda j, i: (0, j))],
        out_shape=[jax.ShapeDtypeStruct((s, D_FF), BF16), jax.ShapeDtypeStruct((s, D_FF), BF16),
                   jax.ShapeDtypeStruct((1, D_FF), F32), jax.ShapeDtypeStruct((1, D_FF), F32),
                   jax.ShapeDtypeStruct((3, D_FF), F32), jax.ShapeDtypeStruct((3, D_FF), F32)],
        compiler_params=_params(("parallel", "arbitrary"), 24 << 20),
    )(up, up, up, up, up, up, da, da, w_conv, w_conv, b_conv, b_conv)
    return outs


def _final(x1, ffn, tgt, g2):
    s, d = x1.shape
    n_steps = s // NORM_TILE

    def body(x1_ref, f_ref, t_ref, g2_ref, dy_ref, df_ref, dg2_ref, loss_ref, lacc_ref):
        i = pl.program_id(0)

        @pl.when(i == 0)
        def _():
            dg2_ref[...] = jnp.zeros_like(dg2_ref)
            lacc_ref[...] = jnp.zeros_like(lacc_ref)

        f = f_ref[...]
        e = x1_ref[...] + g2_ref[...] * f - t_ref[...]
        dy = e * (1.0 / d)
        dy_ref[...] = dy
        df_ref[...] = (dy * g2_ref[...]).astype(BF16)
        dg2_ref[...] += jnp.sum(dy * f, axis=0, keepdims=True)
        lacc_ref[...] += jnp.sum(e * e, axis=0, keepdims=True)

        @pl.when(i == n_steps - 1)
        def _():
            loss_ref[...] = jnp.sum(lacc_ref[...], axis=1, keepdims=True) * (0.5 / d)

    row = pl.BlockSpec((NORM_TILE, d), lambda i: (i, 0))
    return pl.pallas_call(
        body, name="final", grid=(n_steps,),
        in_specs=[row, row, row, _full((1, d))],
        out_specs=[row, row, _full((1, d)), _full((1, 1))],
        out_shape=[jax.ShapeDtypeStruct((s, d), F32), jax.ShapeDtypeStruct((s, d), BF16),
                   jax.ShapeDtypeStruct((1, d), F32), jax.ShapeDtypeStruct((1, 1), F32)],
        scratch_shapes=[pltpu.VMEM((1, d), F32)],
        compiler_params=_params(("arbitrary",)),
    )(x1, ffn, tgt, g2)


def _ffnnorm_bwd(dh2, x1, dy, mix, gain, scale, g1):
    s, d = x1.shape
    n_steps = s // NORM_TILE

    def body(dh_ref, x_ref, dy_ref, mix_ref, g_ref, sc_ref, g1_ref, dx_ref, dm_ref, acc_ref):
        i = pl.program_id(0)

        @pl.when(i == 0)
        def _():
            acc_ref[...] = jnp.zeros_like(acc_ref)

        dh, x = dh_ref[...], x_ref[...]
        r = _rms(x)
        xn = x * r
        dn = dh * (1.0 + sc_ref[...])
        dxn = dn * g_ref[...]
        dx = dy_ref[...] + r * (dxn - xn * jnp.mean(dxn * xn, axis=-1, keepdims=True))
        dx_ref[...] = dx
        dm_ref[...] = (dx * g1_ref[...]).astype(BF16)
        csum = lambda z: jnp.sum(z, axis=0, keepdims=True)
        acc_ref[0:1, :] += csum(dh)
        acc_ref[1:2, :] += csum(dh * (xn * g_ref[...]))
        acc_ref[2:3, :] += csum(dn * xn)
        acc_ref[3:4, :] += csum(dx * mix_ref[...])

    row = pl.BlockSpec((NORM_TILE, d), lambda i: (i, 0))
    vec = _full((1, d))
    return pl.pallas_call(
        body, name="ffnnorm_bwd", grid=(n_steps,),
        in_specs=[row, row, row, row, vec, vec, vec],
        out_specs=[row, row, _full((8, d))],
        out_shape=[jax.ShapeDtypeStruct((s, d), F32), jax.ShapeDtypeStruct((s, d), BF16), jax.ShapeDtypeStruct((8, d), F32)],
        compiler_params=_params(("arbitrary",)),
    )(dh2, x1, dy, mix, gain, scale, g1)


def _mixnorm_bwd(dh, x, dx1, gain, scale):
    s, d = x.shape
    n_steps = s // NORM_TILE

    def body(dh_ref, x_ref, dx1_ref, g_ref, sc_ref, gx_ref, acc_ref):
        i = pl.program_id(0)

        @pl.when(i == 0)
        def _():
            acc_ref[...] = jnp.zeros_like(acc_ref)

        dh, x = dh_ref[...], x_ref[...]
        r = _rms(x)
        xn = x * r
        dn = dh * (1.0 + sc_ref[...])
        dxn = dn * g_ref[...]
        gx_ref[...] = dx1_ref[...] + r * (dxn - xn * jnp.mean(dxn * xn, axis=-1, keepdims=True))
        csum = lambda z: jnp.sum(z, axis=0, keepdims=True)
        acc_ref[0:1, :] += csum(dh)
        acc_ref[1:2, :] += csum(dh * (xn * g_ref[...]))
        acc_ref[2:3, :] += csum(dn * xn)

    row = pl.BlockSpec((NORM_TILE, d), lambda i: (i, 0))
    vec = _full((1, d))
    return pl.pallas_call(
        body, name="mixnorm_bwd", grid=(n_steps,),
        in_specs=[row, row, row, vec, vec],
        out_specs=[row, _full((8, d))],
        out_shape=[jax.ShapeDtypeStruct((s, d), F32), jax.ShapeDtypeStruct((8, d), F32)],
        compiler_params=_params(("arbitrary",)),
    )(dh, x, dx1, gain, scale)


def _key_count(d, dilated):
    if not dilated:
        return jnp.where(d >= 0, 1.0, 0.0)
    one = lambda cond: jnp.where(cond, 1.0, 0.0)
    cnt = one(d <= 128) + one(((d & 3) == 0) & (d <= 512)) + one((d & 15) == 0)
    return jnp.where(d >= 0, cnt, 0.0)


def _block_kinds(mla):
    return (0, "diag", "none") if mla else (NEAR_REACH, "near", "far")


NEAR_REACH = 512


def _near_offsets(tk, tq):
    return (NEAR_REACH - (tk - tq)) // tk + 1


def _scores_t(ka, qa, scale, kind, rel_t, offset, near_tabs=None):
    return _mask_scores(lax.dot_general(ka, qa, NT, preferred_element_type=F32), scale, kind, rel_t, offset, near_tabs)


def _fill_near_tables(bias_ref, cnt_ref, rel_t):
    tk, tq = rel_t.shape
    for idx in range(_near_offsets(tk, tq)):
        cnt = _key_count(rel_t + (tk - tq) + idx * tk, True)
        cnt_ref[idx] = cnt
        bias_ref[idx] = jnp.where(cnt > 0.0, 0.0, NEG_INF)


def _mask_scores(products, scale, kind, rel_t, offset, near_tabs=None):
    st = products * (scale * LOG2E)
    cnt = None
    if kind == "diag":
        st = jnp.where(rel_t + offset >= 0, st, NEG_INF)
    elif kind == "far":
        st = jnp.where((rel_t & 15) == 0, st, NEG_INF)
    elif kind == "near":
        bias_ref, cnt_ref = near_tabs
        tk, tq = rel_t.shape
        idx = (offset - (tk - tq)) // tk
        st = st + bias_ref[idx]
        cnt = cnt_ref[idx]
    return st, cnt


def _attn_fwd(q, k, v, mla, scale, name, gather=()):
    s = q.shape[0]
    qw = 2 * LANE if mla else LANE
    tq, tk = ATT_TQ, ATT_TK
    reach, kind_near, kind_far = _block_kinds(mla)
    assert s % tq == 0 and tq % tk == 0 and reach % tk == 0 and reach in (0, NEAR_REACH)
    ng = len(gather)
    last_step = HEADS // 2 - 1

    def body(*refs):
        q_ref, k_ref, v_ref = refs[:3]
        o_ref, lse_ref = refs[3 + ng:5 + ng]
        vt_ref, st_ref = refs[5 + 2 * ng:7 + 2 * ng]
        near_tabs = None if mla else refs[7 + 2 * ng:9 + 2 * ng]
        n_tabs = 0 if mla else 2
        comm = (refs[3:3 + ng], refs[5 + ng:5 + 2 * ng]) + tuple(refs[7 + n_tabs + 2 * ng:])
        if ng:
            @pl.when(pl.program_id(0) == 0)
            def _():
                _Gather(*comm).start()

            @pl.when(pl.program_id(0) == last_step)
            def _():
                _Gather(*comm).forward()

        lane = lax.broadcasted_iota(I32, (1, LANE), 1)
        rel_t = lax.broadcasted_iota(I32, (tk, tq), 1) - lax.broadcasted_iota(I32, (tk, tq), 0)
        if not mla:
            _fill_near_tables(*near_tabs, rel_t)

        def transpose_v(j, carry):
            c0 = pl.multiple_of(j * tk, tk)
            vt_ref[:, pl.ds(c0, tk)] = v_ref[pl.ds(c0, tk), :].astype(F32).T.astype(BF16)
            return carry

        lax.fori_loop(0, s // tk, transpose_v, 0)

        def q_block(qi, carry):
            r0 = pl.multiple_of(qi * tq, tq)
            kcols = [slice(a * LANE, (a + 1) * LANE) if mla else slice(0, LANE) for a in range(2)]
            qas = [q_ref[pl.ds(r0, tq), kcols[a]] for a in range(2)]
            if not mla:
                qas = [jnp.where(lane < DIL_DIM, qas[0], jnp.zeros_like(qas[0])),
                       jnp.where(lane >= DIL_DIM, qas[1], jnp.zeros_like(qas[1]))]

            n_k = (r0 + tq) // tk

            def products(kj):
                c0 = pl.multiple_of(kj * tk, tk)
                return [lax.dot_general(k_ref[pl.ds(c0, tk), kcols[a]], qas[a], NT, preferred_element_type=F32)
                        for a in range(2)]

            for a, pr in enumerate(products(0)):
                st_ref[0, a] = pr

            def k_block(kj, c, kind):
                c0 = pl.multiple_of(kj * tk, tk)
                slot = kj & 1
                ahead = products(jnp.minimum(kj + 1, n_k - 1))
                out = []
                for a in range(2):
                    m, l, acc = c[a]
                    st, cnt = _mask_scores(st_ref[slot, a], scale, kind, rel_t, r0 - c0, near_tabs)
                    st_ref[1 - slot, a] = ahead[a]
                    m_new = jnp.maximum(m, jnp.max(st, axis=0, keepdims=True))
                    alpha = jnp.exp2(m - m_new)
                    p = jnp.exp2(st - m_new)
                    if cnt is not None:
                        p = p * cnt
                    l = alpha * l + jnp.sum(p, axis=0, keepdims=True)
                    vt = vt_ref[a * DIL_DIM:(a + 1) * DIL_DIM, pl.ds(c0, tk)]
                    acc = alpha * acc + jnp.dot(vt, p.astype(BF16), preferred_element_type=F32)
                    out.append((m_new, l, acc))
                return tuple(out)

            one = (jnp.full((1, tq), NEG_INF, F32), jnp.zeros((1, tq), F32), jnp.zeros((DIL_DIM, tq), F32))
            first_near = jnp.maximum((r0 - reach) // tk, 0)
            c = lax.fori_loop(0, first_near, functools.partial(k_block, kind=kind_far), (one, one))
            res = lax.fori_loop(first_near, (r0 + tq) // tk, functools.partial(k_block, kind=kind_near), c)
            o_t = jnp.concatenate([res[a][2] / res[a][1] for a in range(2)], axis=0)
            o_ref[pl.ds(r0, tq), :] = o_t.T.astype(BF16)
            for a in range(2):
                lse_ref[a, :, pl.ds(r0, tq)] = res[a][0] * LN2 + jnp.log(res[a][1])
            return carry

        lax.fori_loop(0, s // tq, q_block, 0)

        if ng:
            @pl.when(pl.program_id(0) == last_step)
            def _():
                _Gather(*comm).finish()

    return pl.pallas_call(
        body, name=name, grid=(HEADS // 2,),
        in_specs=[pl.BlockSpec((s, qw), lambda h: (0, h)), pl.BlockSpec((s, qw), lambda h: (0, h)),
                  pl.BlockSpec((s, LANE), lambda h: (0, h))] + [ANY] * ng,
        out_specs=[pl.BlockSpec((s, LANE), lambda h: (0, h)), pl.BlockSpec((2, 1, s), lambda h: (h, 0, 0))] + [ANY] * ng,
        out_shape=[jax.ShapeDtypeStruct((s, DIL_W), BF16), jax.ShapeDtypeStruct((HEADS, 1, s), F32)] + _Gather.out_shapes(gather),
        scratch_shapes=[pltpu.VMEM((LANE, s), BF16), pltpu.VMEM((2, 2, tk, tq), F32)]
        + ([] if mla else [pltpu.VMEM((_near_offsets(tk, tq), tk, tq), F32)] * 2) + (_Gather.scratch(gather) if ng else []),
        compiler_params=_params(("arbitrary",) if ng else ("parallel",), 12 << 20),
    )(*_in_hbm(q, k, v), *gather)


def _attn_bwd(q, k, v, o, do, do_block0, lse, mla, scale, name, scatter=()):
    s = q.shape[0]
    qw = 2 * LANE if mla else LANE
    tq, tk = ATT_TQ, ATT_TK_BWD
    nq = s // tq
    reach, kind_near, kind_far = _block_kinds(mla)
    assert s % tq == 0 and s % tk == 0
    ns = len(scatter)
    last_step = HEADS // 2 - 1

    def body(*refs):
        q_ref, k_ref, v_ref, o_ref, do_ref, lse_ref = refs[:6]
        dq_ref, dk_ref, dv_ref = refs[6 + ns:9 + ns]
        kt_ref, dot_ref, dob_ref, dqt_ref, delta_ref, lse2_ref = refs[9 + 2 * ns:15 + 2 * ns]
        near_tabs = None if mla else refs[15 + 2 * ns:17 + 2 * ns]
        n_tabs = 0 if mla else 2
        comm = (refs[6:6 + ns], refs[9 + ns:9 + 2 * ns]) + tuple(refs[15 + n_tabs + 2 * ns:])
        if ns:
            @pl.when(pl.program_id(0) == 0)
            def _():
                _Scatter(*comm).start()

        lane = lax.broadcasted_iota(I32, (1, LANE), 1)
        row = lax.broadcasted_iota(I32, (LANE, 1), 0)
        rel_t = lax.broadcasted_iota(I32, (tk, tq), 1) - lax.broadcasted_iota(I32, (tk, tq), 0)
        if not mla:
            _fill_near_tables(*near_tabs, rel_t)

        def prepare(j, carry):
            c0 = pl.multiple_of(j * tk, tk)
            do_blk = do_ref[pl.ds(c0, tk), :]
            dob_ref[pl.ds(c0, tk), :] = do_blk.astype(BF16)
            do_t = do_blk.T
            dot_ref[:, pl.ds(c0, tk)] = do_t.astype(BF16)
            prod = do_t * o_ref[pl.ds(c0, tk), :].astype(F32).T
            delta_ref[0, :, pl.ds(c0, tk)] = jnp.sum(prod[0:DIL_DIM], axis=0, keepdims=True)
            delta_ref[1, :, pl.ds(c0, tk)] = jnp.sum(prod[DIL_DIM:LANE], axis=0, keepdims=True)
            for w in range(qw // LANE):
                kt_ref[w * LANE:(w + 1) * LANE, pl.ds(c0, tk)] = (
                    k_ref[pl.ds(c0, tk), w * LANE:(w + 1) * LANE].astype(F32).T.astype(BF16))
            return carry

        lax.fori_loop(0, s // tk, prepare, 0)
        dqt_ref[...] = jnp.zeros_like(dqt_ref)
        lse2_ref[...] = lse_ref[...] * LOG2E

        sels = [lane < DIL_DIM, lane >= DIL_DIM]
        rsels = [row < DIL_DIM, row >= DIL_DIM]
        cols = [slice(a * LANE, (a + 1) * LANE) if mla else slice(0, LANE) for a in range(2)]

        def k_block(kj, carry):
            c0 = pl.multiple_of(kj * tk, tk)
            kas = [k_ref[pl.ds(c0, tk), cols[a]] for a in range(2)]
            kts = [kt_ref[cols[a], pl.ds(c0, tk)] for a in range(2)]
            if not mla:
                kas = [jnp.where(sels[a], kas[a], jnp.zeros_like(kas[a])) for a in range(2)]
                kts = [jnp.where(rsels[a], kts[a], jnp.zeros_like(kts[a])) for a in range(2)]
            vb = v_ref[pl.ds(c0, tk), :]
            vbs = [jnp.where(sels[a], vb, jnp.zeros_like(vb)) for a in range(2)]

            first = c0 // tq

            def q_block(qi, c, kind):
                r0 = pl.multiple_of(qi * tq, tq)
                out, dq_parts = [], []
                for a in range(2):
                    dk_acc, dv_acc = c[a]
                    qa = q_ref[pl.ds(r0, tq), cols[a]]
                    st, cnt = _scores_t(kas[a], qa, scale, kind, rel_t, r0 - c0, near_tabs)
                    p = jnp.exp2(st - lse2_ref[a, :, pl.ds(r0, tq)])
                    if cnt is not None:
                        p = p * cnt
                    dp = jnp.dot(vbs[a], dot_ref[:, pl.ds(r0, tq)], preferred_element_type=F32)
                    ds = (p * (dp - delta_ref[a, :, pl.ds(r0, tq)]) * scale).astype(BF16)
                    dv_acc = dv_acc + jnp.dot(p.astype(BF16), dob_ref[pl.ds(r0, tq), :], preferred_element_type=F32)
                    dk_acc = dk_acc + jnp.dot(ds, qa, preferred_element_type=F32)
                    dq_parts.append(jnp.dot(kts[a], ds, preferred_element_type=F32))
                    out.append((dk_acc, dv_acc))
                if mla:
                    for a in range(2):
                        dqt_ref[cols[a], pl.ds(r0, tq)] += dq_parts[a]
                else:
                    dqt_ref[:, pl.ds(r0, tq)] += dq_parts[0] + dq_parts[1]
                return tuple(out)

            zero = jnp.zeros((tk, LANE), F32)
            last_near = jnp.minimum((c0 + tk - 1 + reach) // tq + 1, nq)
            c = lax.fori_loop(first, last_near, functools.partial(q_block, kind=kind_near), ((zero, zero), (zero, zero)))
            (dk0, dv0), (dk1, dv1) = lax.fori_loop(last_near, nq, functools.partial(q_block, kind=kind_far), c)
            if mla:
                dk_ref[pl.ds(c0, tk), cols[0]] = dk0
                dk_ref[pl.ds(c0, tk), cols[1]] = dk1
            else:
                dk_ref[pl.ds(c0, tk), :] = jnp.where(sels[0], dk0, dk1)
            dv_ref[pl.ds(c0, tk), :] = jnp.where(sels[0], dv0, dv1)
            return carry

        lax.fori_loop(0, s // tk, k_block, 0)

        def write_dq(j, carry):
            c0 = pl.multiple_of(j * tk, tk)
            for w in range(qw // LANE):
                dq_ref[pl.ds(c0, tk), w * LANE:(w + 1) * LANE] = dqt_ref[w * LANE:(w + 1) * LANE, pl.ds(c0, tk)].T
            return carry

        lax.fori_loop(0, s // tk, write_dq, 0)

        if ns:
            @pl.when(pl.program_id(0) == last_step)
            def _():
                _Scatter(*comm).finish()

    b0 = do_block0
    return pl.pallas_call(
        body, name=name, grid=(HEADS // 2,),
        in_specs=[pl.BlockSpec((s, qw), lambda h: (0, h)), pl.BlockSpec((s, qw), lambda h: (0, h)),
                  pl.BlockSpec((s, LANE), lambda h: (0, h)), pl.BlockSpec((s, LANE), lambda h: (0, h)),
                  pl.BlockSpec((s, LANE), lambda h: (0, h + b0)), pl.BlockSpec((2, 1, s), lambda h: (h, 0, 0))] + [ANY] * ns,
        out_specs=[pl.BlockSpec((s, qw), lambda h: (0, h)), pl.BlockSpec((s, qw), lambda h: (0, h)),
                   pl.BlockSpec((s, LANE), lambda h: (0, h))] + [ANY] * ns,
        out_shape=[jax.ShapeDtypeStruct(q.shape, F32), jax.ShapeDtypeStruct(k.shape, F32), jax.ShapeDtypeStruct((s, DIL_W), F32)]
        + _Scatter.out_shapes(scatter),
        scratch_shapes=[pltpu.VMEM((qw, s), BF16), pltpu.VMEM((LANE, s), BF16), pltpu.VMEM((s, LANE), BF16),
                        pltpu.VMEM((qw, s), F32), pltpu.VMEM((2, 1, s), F32), pltpu.VMEM((2, 1, s), F32)]
        + ([] if mla else [pltpu.VMEM((_near_offsets(tk, tq), tk, tq), F32)] * 2) + (_Scatter.semaphores(ns) if ns else []),
        compiler_params=_params(("arbitrary",) if ns else ("parallel",), 24 << 20),
    )(*_in_hbm(q, k, v, o, do, lse), *scatter)


def _ada_bwd(c_all, dmod_shard):
    n, d = c_all.shape
    cols = dmod_shard.shape[1]

    def body(c_ref, g_ref, o_ref):
        cv = c_ref[...]
        o_ref[...] = lax.dot_general(cv * _sigmoid(cv), g_ref[...], TN, precision=HIGHEST, preferred_element_type=F32)

    return pl.pallas_call(
        body, name="ada_bwd", out_shape=jax.ShapeDtypeStruct((d, cols), F32),
        compiler_params=_params(None, 16 << 20),
    )(c_all, dmod_shard)


SMALL_WIDTHS = (("g_mix_norm", D_MODEL), ("g_q_lat", Q_LORA), ("g_kv_lat", KV_LORA), ("g_mla_q_nope", NOPE),
                ("g_mla_q_pe", ROPE), ("g_mla_k_nope", NOPE), ("g_mla_k_pe", ROPE), ("g_dil_q", DIL_DIM),
                ("g_dil_k", DIL_DIM), ("g_ffn_norm", D_MODEL), ("b_conv", UP_W))


def _small_layout():
    pieces = (("dmod", 6 * D_MODEL),) + SMALL_WIDTHS + tuple(("w_conv%d" % k, UP_W) for k in range(3)) + (("loss", 1),)
    layout, off = {}, 0
    for name, width in pieces:
        layout[name] = (width, off)
        off += -(-width // LANE) * LANE
    return layout, off


def _pack_small(acc1, acc2, dg2, dglat, dgains, dbg, dbv, dwg, dwv, loss_part):
    layout, total = _small_layout()

    def body(a1, a2, g2, gl, gg, bg, bv, wg, wv, ls, o_ref):
        o_ref[...] = jnp.zeros_like(o_ref)

        def put(name, src, shift=0):
            start = layout[name][1] + shift
            o_ref[:, start:start + src.shape[1]] = src

        for k, src in enumerate((a1[0:1, :], a1[1:2, :], a2[3:4, :], a2[0:1, :], a2[1:2, :], g2[...])):
            put("dmod", src, k * D_MODEL)
        put("g_mix_norm", a1[2:3, :])
        put("g_q_lat", gl[0:1, :])
        put("g_kv_lat", gl[1:2, 0:KV_LORA])
        put("g_mla_q_nope", gg[0:1, 0:NOPE])
        put("g_mla_q_pe", gg[5:6, 0:ROPE])
        put("g_mla_k_nope", gg[1:2, 0:NOPE])
        put("g_mla_k_pe", gg[2:3, 0:ROPE])
        put("g_dil_q", gg[3:4, 0:DIL_DIM])
        put("g_dil_k", gg[4:5, 0:DIL_DIM])
        put("g_ffn_norm", a2[2:3, :])
        put("b_conv", bg[...])
        put("b_conv", bv[...], D_FF)
        for k in range(3):
            put("w_conv%d" % k, wg[k:k + 1, :])
            put("w_conv%d" % k, wv[k:k + 1, :], D_FF)
        put("loss", ls[...])

    ins = (acc1, acc2, dg2, dglat, dgains, dbg, dbv, dwg, dwv, loss_part)
    return pl.pallas_call(
        body, name="pack_small", grid=(1,), in_specs=[_full(a.shape) for a in ins], out_specs=_full((1, total)),
        out_shape=jax.ShapeDtypeStruct((1, total), F32),
        compiler_params=_params(("arbitrary",), 2 << 20),
    )(*_in_hbm(*ins))


def _sum_unpack(g):
    n_dev, _, total = g.shape
    layout, _ = _small_layout()

    def body(g_ref, *refs):
        o_refs, s_ref = refs[:-1], refs[-1]
        acc = g_ref[0]
        for k in range(1, n_dev):
            acc = acc + g_ref[k]
        s_ref[...] = acc
        take = lambda name: s_ref[:, layout[name][1]:layout[name][1] + layout[name][0]]
        o_refs[0][...] = take("dmod")
        for i, (name, _) in enumerate(SMALL_WIDTHS):
            o_refs[1 + i][...] = take(name)
        for k in range(3):
            o_refs[-2][k:k + 1, :] = take("w_conv%d" % k)
        o_refs[-1][...] = take("loss")

    shapes = [(1, 6 * D_MODEL)] + [(1, w) for _, w in SMALL_WIDTHS] + [(3, UP_W), (1, 1)]
    return pl.pallas_call(
        body, name="sum_unpack", out_shape=[jax.ShapeDtypeStruct(sh, F32) for sh in shapes],
        scratch_shapes=[pltpu.VMEM((1, total), F32)],
        compiler_params=_params(None, 4 << 20),
    )(g)


def _adamw_math(w, g, m, v):
    mn = ADAM_B1 * m + (1.0 - ADAM_B1) * g
    vn = ADAM_B2 * v + (1.0 - ADAM_B2) * (g * g)
    m_hat = mn / (1.0 - ADAM_B1 ** ADAM_STEP)
    v_hat = vn / (1.0 - ADAM_B2 ** ADAM_STEP)
    return -ADAM_LR * (m_hat / (jnp.sqrt(v_hat) + ADAM_EPS) + ADAM_WD * w), mn, vn


def _adamw_vectors(ws, gs, ms, vs):
    k = len(ws)

    def body(*refs):
        for i in range(k):
            d, mn, vn = _adamw_math(refs[i][...], refs[k + i][...], refs[2 * k + i][...], refs[3 * k + i][...])
            refs[4 * k + i][...] = d
            refs[5 * k + i][...] = mn
            refs[6 * k + i][...] = vn

    blocks = [_full(w.shape) for w in ws]
    outs = pl.pallas_call(
        body, name="adamw_vectors", grid=(1,), in_specs=blocks * 4, out_specs=blocks * 3,
        out_shape=[jax.ShapeDtypeStruct(w.shape, F32) for w in ws] * 3,
        compiler_params=_params(("arbitrary",), 2 << 20),
    )(*_in_hbm(*ws, *gs, *ms, *vs))
    return outs[:k], outs[k:2 * k], outs[2 * k:]


def _adamw(w, g, m, v, name):
    r, c = w.shape
    tr = r
    for cand in (256, 128, 64, 32, 16):
        if r % cand == 0 and r > cand:
            tr = cand
            break

    def body(w_ref, g_ref, m_ref, v_ref, d_ref, mo_ref, vo_ref):
        d_ref[...], mo_ref[...], vo_ref[...] = _adamw_math(w_ref[...], g_ref[...], m_ref[...], v_ref[...])

    blk = pl.BlockSpec((tr, c), lambda i: (i, 0))
    return pl.pallas_call(
        body, name=name, grid=(r // tr,), in_specs=[blk] * 4, out_specs=[blk] * 3,
        out_shape=[jax.ShapeDtypeStruct((r, c), F32)] * 3,
        compiler_params=_params(("parallel",), 7 * _nbytes((tr, c), F32)),
    )(w, g, m, v)


def _position():
    return lax.axis_index("x"), lax.axis_index("y"), lax.axis_index("c")


def _other_chips(x, y):
    return [(1 - x, y, 2 * (1 - x) + y), (x, 1 - y, 2 * x + (1 - y)), (1 - x, 1 - y, 2 * (1 - x) + (1 - y))]


class _SmallGather:
    def __init__(self, v_ref, out_ref, send_sems, recv_sems, local_sem):
        x, y, c = _position()
        me = 4 * x + 2 * y + c
        self.local = pltpu.make_async_copy(v_ref, out_ref.at[me], local_sem)
        self.sends, self.arrivals = [], []
        for k in range(N_DEV - 1):
            fx, fy, fc = ((k + 1) >> 2) & 1, ((k + 1) >> 1) & 1, (k + 1) & 1
            px, py, pc = (1 - x if fx else x), (1 - y if fy else y), (1 - c if fc else c)

            def copy(dst, k=k, peer=(px, py, pc)):
                return pltpu.make_async_remote_copy(src_ref=v_ref, dst_ref=dst, send_sem=send_sems.at[k],
                                                    recv_sem=recv_sems.at[k], device_id=peer, device_id_type=MESH)

            self.sends.append(copy(out_ref.at[me]))
            self.arrivals.append(copy(out_ref.at[4 * px + 2 * py + pc]))

    @staticmethod
    def semaphores():
        return [pltpu.SemaphoreType.DMA((N_DEV - 1,)), pltpu.SemaphoreType.DMA((N_DEV - 1,)), pltpu.SemaphoreType.DMA]

    def start(self):
        self.local.start()
        for cp in self.sends:
            cp.start()

    def finish(self):
        for cp in self.arrivals:
            cp.wait_recv()
        for cp in self.sends:
            cp.wait_send()
        self.local.wait()


def _prologue(c_taps, w_ada_shard, b_shard, pos_col, rope_consts, shards):
    n = len(shards)
    s = pos_col.shape[0]
    cols = w_ada_shard.shape[1]
    freq, csel, ssel = rope_consts

    def body(*refs):
        ct_ref, w_ref, b_ref, p_ref, f_ref, cs_ref, ss_ref = refs[:7]
        sh_refs = refs[7:7 + n]
        ct_all_ref, mod_all_ref, tab_ref = refs[7 + n:10 + n]
        g_refs = refs[10 + n:10 + 2 * n]
        mod_blk_ref = refs[10 + 2 * n]
        sems = refs[11 + 2 * n:]
        weights = _Gather(sh_refs, g_refs, *sems[6:])
        weights.start()
        first = _SmallGather(ct_ref, ct_all_ref, *sems[0:3])
        first.start()
        first.finish()
        cv = ct_all_ref[:, 0, 0:D_MODEL]
        sc = (cv * _sigmoid(cv)).astype(BF16)
        mod_blk_ref[...] = jnp.dot(sc, w_ref[...].astype(BF16), preferred_element_type=F32) + b_ref[...]
        second = _SmallGather(mod_blk_ref, mod_all_ref, *sems[3:6])
        second.start()

        def table_rows(i, carry):
            r0 = pl.multiple_of(i * ROW_TILE, ROW_TILE)
            ang = p_ref[pl.ds(r0, ROW_TILE), :].astype(F32) * f_ref[...]
            tab_ref[pl.ds(r0, ROW_TILE), :] = cs_ref[...] * jnp.cos(ang) + ss_ref[...] * jnp.sin(ang)
            return carry

        lax.fori_loop(0, s // ROW_TILE, table_rows, 0)
        second.finish()
        weights.forward()
        weights.finish()

    return pl.pallas_call(
        body, name="prologue",
        out_shape=[jax.ShapeDtypeStruct((N_DEV,) + c_taps.shape, F32), jax.ShapeDtypeStruct((N_DEV, N_DEV, cols), F32),
                   jax.ShapeDtypeStruct((s, 4 * LANE), F32)] + _Gather.out_shapes(shards),
        in_specs=[IN_VMEM] * 7 + [ANY] * n, out_specs=[IN_VMEM] * 3 + [ANY] * n,
        scratch_shapes=[pltpu.VMEM((N_DEV, cols), F32)] + _SmallGather.semaphores() * 2 + _Gather.scratch(shards),
        compiler_params=_params(None, 14 << 20),
    )(c_taps, w_ada_shard, b_shard, pos_col, freq, csel, ssel, *shards)


IN_VMEM = pl.BlockSpec(memory_space=pltpu.VMEM)
ANY = pl.BlockSpec(memory_space=pl.ANY)


class _Gather:
    def __init__(self, w_refs, out_refs, send_sems, recv_sems, own_sems, *bounce_refs):
        x, y, c = _position()
        q0 = 2 * x + y
        sibling = (x, y, 1 - c)
        self.ici, self.ici_in, self.fwd, self.fwd_in, self.own_in, self.own_out = [], [], [], [], [], []
        for k, (w_ref, out_ref) in enumerate(zip(w_refs, out_refs)):
            half = w_ref.shape[0] // 2
            self.own_in.append(pltpu.make_async_copy(w_ref, bounce_refs[k], own_sems.at[2 * k]))
            self.own_out.append(pltpu.make_async_copy(bounce_refs[k], out_ref.at[q0], own_sems.at[2 * k + 1]))

            def blk(q, e, out_ref=out_ref, half=half):
                return out_ref.at[q, pl.ds(pl.multiple_of(e * half, 16), half), :]

            def copy(src, dst, i, to):
                return pltpu.make_async_remote_copy(src_ref=src, dst_ref=dst, send_sem=send_sems.at[i], recv_sem=recv_sems.at[i],
                                                    device_id=to, device_id_type=MESH)

            src = w_ref.at[pl.ds(pl.multiple_of(c * half, 16), half), :]
            for j, (cx, cy, qj) in enumerate(_other_chips(x, y)):
                self.ici.append(copy(src, blk(q0, c), 6 * k + j, (cx, cy, c)))
                self.ici_in.append(copy(blk(qj, c), blk(qj, c), 6 * k + j, (cx, cy, c)))
                self.fwd.append(copy(blk(qj, c), blk(qj, c), 6 * k + 3 + j, sibling))
                self.fwd_in.append(copy(blk(qj, 1 - c), blk(qj, 1 - c), 6 * k + 3 + j, sibling))

    @staticmethod
    def out_shapes(shards):
        return [jax.ShapeDtypeStruct((N_CHIP,) + s.shape, s.dtype) for s in shards]

    @staticmethod
    def scratch(shards):
        n = len(shards)
        return ([pltpu.SemaphoreType.DMA((6 * n,)), pltpu.SemaphoreType.DMA((6 * n,)), pltpu.SemaphoreType.DMA((2 * n,))]
                + [pltpu.VMEM(s.shape, s.dtype) for s in shards])

    def start(self):
        for cp in self.ici + self.own_in:
            cp.start()

    def forward(self):
        for fetched, placed in zip(self.own_in, self.own_out):
            fetched.wait()
            placed.start()
        for arrived, onward in zip(self.ici_in, self.fwd):
            arrived.wait_recv()
            onward.start()

    def finish(self):
        for cp in self.fwd_in:
            cp.wait_recv()
        for cp in self.ici + self.fwd:
            cp.wait_send()
        for cp in self.own_out:
            cp.wait()


class _PairSwap:
    def __init__(self, g_refs, out_refs, send_sems, recv_sems):
        x, y, c = _position()
        self.copies = [
            pltpu.make_async_remote_copy(src_ref=g_ref.at[:, 1 - c], dst_ref=out_ref, send_sem=send_sems.at[k],
                                         recv_sem=recv_sems.at[k], device_id=(x, y, 1 - c), device_id_type=MESH)
            for k, (g_ref, out_ref) in enumerate(zip(g_refs, out_refs))]

    @staticmethod
    def out_shapes(grads):
        return [jax.ShapeDtypeStruct((N_CHIP,) + g.shape[2:], g.dtype) for g in grads]

    @staticmethod
    def semaphores(n):
        return [pltpu.SemaphoreType.DMA((n,)), pltpu.SemaphoreType.DMA((n,))]

    def start(self):
        for cp in self.copies:
            cp.start()

    def finish(self):
        for cp in self.copies:
            cp.wait_recv()
        for cp in self.copies:
            cp.wait_send()


def _pair_sum(g, a, c_idx, name):
    _, _, rh, cols = g.shape
    tr = rh
    for cand in (256, 128, 64, 32, 16):
        if rh % cand == 0 and rh > cand:
            tr = cand
            break

    def body(c_ref, g_ref, a_ref, o_ref):
        o_ref[...] = (g_ref[...] + a_ref[...]).astype(BF16)

    return pl.pallas_call(
        body, name=name,
        grid_spec=pltpu.PrefetchScalarGridSpec(
            num_scalar_prefetch=1, grid=(N_CHIP, rh // tr),
            in_specs=[pl.BlockSpec((None, None, tr, cols), lambda q, i, c_ref: (q, c_ref[0], i, 0)),
                      pl.BlockSpec((None, tr, cols), lambda q, i, c_ref: (q, i, 0))],
            out_specs=pl.BlockSpec((None, tr, cols), lambda q, i, c_ref: (q, i, 0))),
        out_shape=jax.ShapeDtypeStruct((N_CHIP, rh, cols), BF16),
        compiler_params=_params(("parallel", "parallel"), 10 * _nbytes((tr, cols), F32)),
    )(c_idx, g, a)


def _scatter_and_gather(parts, small, name):
    n = len(parts)

    def body(*refs):
        scatter = _Scatter(refs[:n], refs[n + 1:2 * n + 1], *refs[2 * n + 2:2 * n + 4])
        gather = _SmallGather(refs[n], refs[2 * n + 1], *refs[2 * n + 4:])
        scatter.start()
        gather.start()
        gather.finish()
        scatter.finish()

    return pl.pallas_call(
        body, name=name,
        out_shape=_Scatter.out_shapes(parts) + [jax.ShapeDtypeStruct((N_DEV,) + small.shape, F32)],
        in_specs=[ANY] * n + [IN_VMEM], out_specs=[ANY] * n + [IN_VMEM],
        scratch_shapes=_Scatter.semaphores(n) + _SmallGather.semaphores(),
        compiler_params=_params(None, 10 * _nbytes(small.shape, F32)),
    )(*parts, small)


class _Scatter:
    def __init__(self, p_refs, out_refs, send_sems, recv_sems):
        x, y, c = _position()
        self.copies = []
        for k, (p_ref, out_ref) in enumerate(zip(p_refs, out_refs)):
            for j, (cx, cy, qj) in enumerate(_other_chips(x, y)):
                self.copies.append(pltpu.make_async_remote_copy(
                    src_ref=p_ref.at[qj], dst_ref=out_ref.at[j], send_sem=send_sems.at[3 * k + j],
                    recv_sem=recv_sems.at[3 * k + j], device_id=(cx, cy, c), device_id_type=MESH))

    @staticmethod
    def out_shapes(parts):
        return [jax.ShapeDtypeStruct((3,) + p.shape[1:], p.dtype) for p in parts]

    @staticmethod
    def semaphores(n):
        return [pltpu.SemaphoreType.DMA((3 * n,)), pltpu.SemaphoreType.DMA((3 * n,))]

    def start(self):
        for cp in self.copies:
            cp.start()

    def finish(self):
        for cp in self.copies:
            cp.wait_recv()
        for cp in self.copies:
            cp.wait_send()


def _shard_sum(p, b, qc_idx, name):
    _, rh, cols = p.shape
    tr = rh
    for cand in (256, 128, 64, 32, 16):
        if rh % cand == 0 and rh > cand:
            tr = cand
            break

    def body(qc_ref, p_ref, b_ref, o_ref):
        acc = p_ref[...].astype(F32)
        for j in range(3):
            acc = acc + b_ref[j].astype(F32)
        o_ref[...] = acc

    return pl.pallas_call(
        body, name=name,
        grid_spec=pltpu.PrefetchScalarGridSpec(
            num_scalar_prefetch=1, grid=(rh // tr,),
            in_specs=[pl.BlockSpec((None, tr, cols), lambda i, qc_ref: (qc_ref[0], i, 0)),
                      pl.BlockSpec((3, tr, cols), lambda i, qc_ref: (0, i, 0))],
            out_specs=pl.BlockSpec((None, tr, cols), lambda i, qc_ref: (qc_ref[1], i, 0))),
        out_shape=jax.ShapeDtypeStruct((2, rh, cols), F32),
        compiler_params=_params(("parallel",), 8 * _nbytes((tr, cols), F32)),
    )(qc_idx, p, b)


def _join_halves(shards):
    n = len(shards)

    def body(*refs):
        out_refs = refs[n:2 * n]
        send_sems, recv_sems = refs[2 * n:]
        x, y, c = _position()
        cps = [pltpu.make_async_remote_copy(src_ref=out_refs[k].at[c], dst_ref=out_refs[k].at[c], send_sem=send_sems.at[k],
                                            recv_sem=recv_sems.at[k], device_id=(x, y, 1 - c), device_id_type=MESH)
               for k in range(n)]
        for cp in cps:
            cp.start()
        for k in range(n):
            arriving = out_refs[k].at[1 - c]
            pltpu.make_async_remote_copy(src_ref=arriving, dst_ref=arriving, send_sem=send_sems.at[k], recv_sem=recv_sems.at[k],
                                         device_id=(x, y, 1 - c), device_id_type=MESH).wait_recv()
        for cp in cps:
            cp.wait_send()

    return pl.pallas_call(
        body, name="rs_join",
        out_shape=[jax.ShapeDtypeStruct(a.shape, a.dtype) for a in shards],
        in_specs=[ANY] * n, out_specs=[ANY] * n, input_output_aliases={k: k for k in range(n)},
        scratch_shapes=[pltpu.SemaphoreType.DMA((n,)), pltpu.SemaphoreType.DMA((n,))],
    )(*shards)


def _cols_from_shards(g):
    q, r, cs = g.shape
    return jnp.transpose(g, (1, 0, 2)).reshape(r, q * cs)


def _cols_to_shards(w):
    r, cfull = w.shape
    return jnp.transpose(w.reshape(r, N_CHIP, cfull // N_CHIP), (1, 0, 2))


def _pad_w_in(w):
    z = lambda n: jnp.zeros((w.shape[0], n), w.dtype)
    q_lat, kv_lat, kpe = w[:, 0:512], w[:, 512:768], w[:, 768:800]
    qd, kd, vd = w[:, 800:1312], w[:, 1312:1824], w[:, 1824:2336]
    return jnp.concatenate([q_lat, qd, kd, vd, kv_lat, z(KPE_OFF), kpe, z(LANE - KPE_OFF - ROPE)], axis=1)


def _pad_w_qb(w):
    w3 = w.reshape(Q_LORA, HEADS, NOPE + ROPE)
    return jnp.pad(w3, ((0, 0), (0, 0), (0, LANE - NOPE - ROPE))).reshape(Q_LORA, HEADS * LANE)


def _unpad_w_qb(g):
    return g.reshape(Q_LORA, HEADS, LANE)[:, :, :NOPE + ROPE].reshape(Q_LORA, HEADS * (NOPE + ROPE))


def _pad_w_kvb(w):
    w3 = w.reshape(KV_LORA, HEADS, 2 * NOPE)
    kp = jnp.pad(w3[:, :, :NOPE], ((0, 0), (0, 0), (0, LANE - NOPE))).reshape(KV_LORA, HEADS * LANE)
    return jnp.concatenate([kp, w3[:, :, NOPE:].reshape(KV_LORA, DIL_W)], axis=1)


def _unpad_w_kvb(g):
    gk = g[:, :HEADS * LANE].reshape(KV_LORA, HEADS, LANE)[:, :, :NOPE]
    gv = g[:, HEADS * LANE:].reshape(KV_LORA, HEADS, NOPE)
    return jnp.concatenate([gk, gv], axis=2).reshape(KV_LORA, HEADS * 2 * NOPE)


def _head_gains(g_q_nope, g_q_pe, g_k_nope, g_k_pe, g_dq, g_dk):
    z = lambda n: jnp.zeros((1, n), F32)
    q1 = jnp.concatenate([g_q_nope, g_q_pe, z(LANE - NOPE - ROPE)], axis=1)
    k1 = jnp.concatenate([g_k_nope, z(LANE - NOPE)], axis=1)
    kpe = jnp.concatenate([z(KPE_OFF), g_k_pe, z(LANE - KPE_OFF - ROPE)], axis=1)
    return dict(q=jnp.tile(q1, (1, HEADS)), k=jnp.tile(k1, (1, HEADS)), kpe=kpe,
                dq=jnp.tile(g_dq, (1, HEADS)), dk=jnp.tile(g_dk, (1, HEADS)))


def kernel(x, c, positions, w_ada, b_ada, g_mix_norm, w_in, g_q_lat, w_q_b, g_kv_lat, w_kv_b, g_mla_q_nope, g_mla_q_pe, g_mla_k_nope, g_mla_k_pe, g_dil_q, g_dil_k, w_o, g_ffn_norm, w_up, w_conv, b_conv, w_down, loss_target, m_w_ada, m_b_ada, m_g_mix_norm, m_w_in, m_g_q_lat, m_w_q_b, m_g_kv_lat, m_w_kv_b, m_g_mla_q_nope, m_g_mla_q_pe, m_g_mla_k_nope, m_g_mla_k_pe, m_g_dil_q, m_g_dil_k, m_w_o, m_g_ffn_norm, m_w_up, m_w_conv, m_b_conv, m_w_down, v_w_ada, v_b_ada, v_g_mix_norm, v_w_in, v_g_q_lat, v_w_q_b, v_g_kv_lat, v_w_kv_b, v_g_mla_q_nope, v_g_mla_q_pe, v_g_mla_k_nope, v_g_mla_k_pe, v_g_dil_q, v_g_dil_k, v_w_o, v_g_ffn_norm, v_w_up, v_w_conv, v_b_conv, v_w_down):
    args = dict(locals())
    weights = {n: args[n][0] for n in ("w_ada", "w_in", "w_q_b", "w_kv_b", "w_o", "w_up", "w_conv", "w_down")}
    small_w = {n: args[n] for n in ("b_ada",) + tuple(n for n, _ in SMALL_WIDTHS)}
    mom_m = {n[2:]: (args[n][0] if args[n].ndim == 3 else args[n]) for n in args if n.startswith("m_")}
    mom_v = {n[2:]: (args[n][0] if args[n].ndim == 3 else args[n]) for n in args if n.startswith("v_")}

    xi, yi, ci = _position()
    q0 = 2 * xi + yi
    me = 4 * xi + 2 * yi + ci
    xs, tgt = x[0], loss_target[0]
    s = xs.shape[0]
    consts = _seg_consts()
    c_idx, qc_idx = jnp.reshape(ci, (1,)).astype(I32), jnp.stack([q0, ci]).astype(I32)

    def halves(g4):
        q, r, cc = g4.shape
        return g4.reshape(q, 2, r // 2, cc)

    own_first = [weights[n].astype(BF16) for n in ("w_in", "w_q_b", "w_kv_b")]
    own_later = [weights[n].astype(BF16) for n in ("w_o", "w_up", "w_down")]
    conv_cols = UP_W // N_CHIP
    ada_cols = w_ada.shape[2]
    b_shard = lax.dynamic_slice_in_dim(b_ada, q0 * ada_cols, ada_cols, axis=1)
    c_taps = jnp.concatenate([c, weights["w_conv"].reshape(1, 3 * conv_cols)], axis=1)
    c_taps_all, mod_all, tab, *gathered = _prologue(c_taps, weights["w_ada"], b_shard, positions.reshape(s, 1),
                                                    _rope_consts(), own_first)
    c_all = c_taps_all[:, 0, :D_MODEL]
    w_conv_f = c_taps_all[:, 0, D_MODEL:].reshape(N_CHIP, 2, 3, conv_cols)[:, 0]
    w_conv_f = jnp.transpose(w_conv_f, (1, 0, 2)).reshape(3, UP_W)
    mod_all = mod_all.reshape(N_CHIP, 2, N_DEV, ada_cols)
    mod = lax.dynamic_index_in_dim(lax.dynamic_index_in_dim(mod_all, ci, 1, False), me, 1, False)
    mod = mod.reshape(1, N_CHIP * ada_cols)
    sh1, sc1, g1, sh2, sc2, g2 = [mod[:, k * D_MODEL:(k + 1) * D_MODEL] for k in range(6)]
    w_in_f = _cols_from_shards(gathered[0])
    w_in_p = _pad_w_in(w_in_f)
    w_qb_p = _pad_w_qb(_cols_from_shards(gathered[1]))
    w_kvb_p = _pad_w_kvb(_cols_from_shards(gathered[2]))
    gains = _head_gains(g_mla_q_nope, g_mla_q_pe, g_mla_k_nope, g_mla_k_pe, g_dil_q, g_dil_k)

    h = _prenorm(xs, g_mix_norm, sc1, sh1, "prenorm")
    proj = _mm(h, w_in_p, "nn", F32, 512, P_COLS, "mm_in")
    ql, kvl = _latnorm(proj, g_q_lat, g_kv_lat)
    q_raw = _mm(ql, w_qb_p, "nn", F32, 512, HEADS * LANE, "mm_qb")
    kv_raw = _mm(kvl, w_kvb_p, "nn", F32, 512, HEADS * LANE + DIL_W, "mm_kvb")
    qm, km, vm, qd, kd, vd = _attn_prep(q_raw, kv_raw, proj, tab, gains, consts)
    scale_m, scale_d = (NOPE + ROPE) ** -0.5, DIL_DIM ** -0.5
    o_m, lse_m, got_up = _attn_fwd(qm, km, vm, True, scale_m, "attn_mla", gather=own_later[1:2])
    o_d, lse_d, got_o, got_down = _attn_fwd(qd, kd, vd, False, scale_d, "attn_dil", gather=[own_later[0], own_later[2]])
    gathered = [got_o, got_up, got_down]
    w_o_f = gathered[0].reshape(D_MODEL, D_MODEL)
    w_up_f = _cols_from_shards(gathered[1])
    w_down_f = gathered[2].reshape(D_FF, D_MODEL)
    mix_in = jnp.concatenate([o_m, o_d], axis=1)
    mix = _mm(mix_in, w_o_f, "nn", F32, 512, D_MODEL, "mm_o")
    x1, h2 = _resid_prenorm(xs, mix, g1, g_ffn_norm, sc2, sh2)
    up = _mm(h2, w_up_f, "nn", F32, 1024, CONV_TILE, "mm_up")
    act = _conv_gate(up, w_conv_f, b_conv)
    ffn = _mm(act, w_down_f, "nn", F32, 512, D_MODEL, "mm_down")
    dy, dffn, dg2, loss_part = _final(x1, ffn, tgt, g2)

    da = _mm(dffn, w_down_f, "nt", F32, 512, CONV_TILE, "mm_down_dx")
    gw_down = _mm(act, dffn, "tn", F32, 256, D_MODEL, "mm_down_dw")
    dup_g, dup_v, dbg, dbv, dwg, dwv = _gate_bwd(up, da, w_conv_f, b_conv)
    dup = jnp.concatenate([dup_g, dup_v], axis=1)
    early_names = ("w_up", "w_down", "w_o")
    gw_up = _mm(h2, dup, "tn", F32, 512, CONV_TILE, "mm_up_dw", col_shards=True)
    early = [halves(gw_up), halves(gw_down.reshape(N_CHIP, D_FF // N_CHIP, D_MODEL))]
    dh2, *early_sib = _mm(dup, w_up_f, "nt", F32, 256, 512, "mm_up_dx", swap=early, b_outer=True)
    dx1, dmix, acc2 = _ffnnorm_bwd(dh2, x1, dy, mix, g_ffn_norm, sc2, g1)
    gw_o = _mm(mix_in, dmix, "tn", F32, 512, D_MODEL, "mm_o_dw")
    early.append(halves(gw_o.reshape(N_CHIP, D_MODEL // N_CHIP, D_MODEL)))
    dmix_in, sib_o = _mm(dmix, w_o_f, "nt", F32, 512, D_MODEL, "mm_o_dx", swap=early[2:])
    early_sib.append(sib_o)
    early_sums = [_pair_sum(g, a, c_idx, "pair_sum_" + n) for g, a, n in zip(early, early_sib, early_names)]
    dqm, dkm, dvm, *early_recv = _attn_bwd(qm, km, vm, o_m, dmix_in, 0, lse_m, True, scale_m, "attn_mla_bwd",
                                           scatter=early_sums[:1])
    dqd, dkd, dvd, *early_recv_d = _attn_bwd(qd, kd, vd, o_d, dmix_in, DIL_W // LANE, lse_d, False, scale_d,
                                             "attn_dil_bwd", scatter=early_sums[1:])
    early_recv = early_recv + early_recv_d
    dq_raw, dkv_raw, dkpe_b, dqd_b, dkd_b, dvd_b, dgains = _attn_prep_bwd(
        dqm, dkm, dvm, dqd, dkd, dvd, q_raw, kv_raw, proj, tab, gains, consts)
    dql = _mm(dq_raw, w_qb_p, "nt", F32, 512, Q_LORA, "mm_qb_dx")
    gw_qb = _unpad_w_qb(_mm(ql, dq_raw, "tn", F32, Q_LORA, HEADS * LANE, "mm_qb_dw"))
    dkvl = _mm(dkv_raw, w_kvb_p, "nt", F32, 512, KV_LORA, "mm_kvb_dx")
    gw_kvb = _unpad_w_kvb(_mm(kvl, dkv_raw, "tn", F32, KV_LORA, HEADS * LANE + DIL_W, "mm_kvb_dw"))
    dqlat_b, dkvlat_b, dglat = _latnorm_bwd(dql, dkvl, proj, g_q_lat, g_kv_lat)
    dproj = jnp.concatenate([dqlat_b, dkvlat_b, dkpe_b[:, KPE_OFF:KPE_OFF + ROPE], dqd_b, dkd_b, dvd_b], axis=1)
    gw_in = _mm(h, dproj, "tn", F32, 512, IN_COLS, "mm_in_dw")
    late_names = ("w_in", "w_q_b", "w_kv_b")
    late = [halves(_cols_to_shards(gw_in)), halves(_cols_to_shards(gw_qb)), halves(_cols_to_shards(gw_kvb))]
    dh, *late_sib = _mm(dproj, w_in_f, "nt", F32, 512, D_MODEL, "mm_in_dx", swap=late)
    grad_x, acc1 = _mixnorm_bwd(dh, xs, dx1, g_mix_norm, sc1)

    packed = _pack_small(acc1, acc2, dg2, dglat, dgains, dbg, dbv, dwg, dwv, loss_part)
    late_sums = [_pair_sum(g, a, c_idx, "pair_sum_" + n) for g, a, n in zip(late, late_sib, late_names)]
    *late_recv, gathered_small = _scatter_and_gather(late_sums, packed, "rs_scatter_late")

    grad_b_ada, *small_grads, gconv_full, loss_sum = _sum_unpack(gathered_small)
    grads = {"b_ada": grad_b_ada}
    grads.update({n: g for (n, _), g in zip(SMALL_WIDTHS, small_grads)})
    shard_cols = UP_W // N_CHIP
    grads["w_conv"] = lax.dynamic_slice_in_dim(gconv_full, q0 * shard_cols, shard_cols, axis=1)
    dmod_all = gathered_small[:, 0, :6 * D_MODEL]
    grads["w_ada"] = _ada_bwd(c_all, lax.dynamic_slice_in_dim(dmod_all, q0 * ada_cols, ada_cols, axis=1))

    big_names = late_names + early_names
    half_sums = [_shard_sum(p, b, qc_idx, "shard_sum_" + n)
                 for p, b, n in zip(late_sums + early_sums, list(late_recv) + list(early_recv), big_names)]
    for n, full in zip(big_names, _join_halves(half_sums)):
        grads[n] = full.reshape(2 * full.shape[1], full.shape[2])

    delta, new_m, new_v = {}, {}, {}
    for n in ("w_ada", "w_in", "w_q_b", "w_kv_b", "w_o", "w_up", "w_conv", "w_down"):
        operands = (weights[n], grads[n], mom_m[n], mom_v[n])
        flipped = n in ("w_in", "w_q_b")
        if flipped:
            operands = [jnp.swapaxes(a, 0, 1) for a in operands]
            grads[n] = jnp.swapaxes(operands[1], 0, 1)
        if n == "w_ada":
            operands = _in_hbm(*operands)
        delta[n], new_m[n], new_v[n] = _adamw(*operands, "adamw_" + n)
        if flipped:
            delta[n], new_m[n], new_v[n] = (jnp.swapaxes(a, 0, 1) for a in (delta[n], new_m[n], new_v[n]))
    vec_names = ("b_ada",) + tuple(n for n, _ in SMALL_WIDTHS)
    sd, sm, sv = _adamw_vectors(*[[d_[n] for n in vec_names] for d_ in (small_w, grads, mom_m, mom_v)])
    for k, n in enumerate(vec_names):
        delta[n], new_m[n], new_v[n] = sd[k], sm[k], sv[k]

    loss = loss_sum[0, 0]
    order = ("w_ada", "b_ada", "g_mix_norm", "w_in", "g_q_lat", "w_q_b", "g_kv_lat", "w_kv_b", "g_mla_q_nope", "g_mla_q_pe",
             "g_mla_k_nope", "g_mla_k_pe", "g_dil_q", "g_dil_k", "w_o", "g_ffn_norm", "w_up", "w_conv", "b_conv", "w_down")
    lead = lambda n, z: z[None] if n.startswith("w_") else z
    outs = [loss, grad_x[None]]
    for d_ in (grads, delta, new_m, new_v):
        outs += [lead(n, d_[n]) for n in order]
    return tuple(outs)
```

```python
import functools

import numpy as np
import jax
import jax.numpy as jnp
from jax import lax
from jax.experimental import pallas as pl
from jax.experimental.pallas import tpu as pltpu

F32 = jnp.float32
BF16 = jnp.bfloat16
I32 = jnp.int32

D_MODEL = 1024
HEADS = 8
NOPE = 64
ROPE = 32
Q_LORA = 512
KV_LORA = 256
DIL_DIM = 64
DIL_W = HEADS * DIL_DIM
D_FF = 2816
UP_W = 2 * D_FF
IN_COLS = Q_LORA + KV_LORA + ROPE + 3 * DIL_W
ROPE_THETA = 10000.0
EPS = 1e-6
NEG_INF = -1e30
N_DEV = 8
N_CHIP = 4

ADAM_LR = 0.001
ADAM_B1 = 0.9
ADAM_B2 = 0.999
ADAM_EPS = 1e-08
ADAM_WD = 0.01
ADAM_STEP = 10

LANE = 128
ROW_TILE = 256
NORM_TILE = 512
ATT_TQ = 512
ATT_TK = 256
ATT_TK_BWD = 512
LOG2E = 1.4426950408889634
LN2 = 0.6931471805599453
VMEM_CAP = 56 * 1024 * 1024
VMEM_FLOOR = 32 * 1024 * 1024

P_QLAT, P_QD, P_KD, P_VD, P_KVLAT, P_KPE = 0, 512, 1024, 1536, 2048, 2304
P_COLS = 2432
KPE_OFF = 64

NN = (((1,), (0,)), ((), ()))
NT = (((1,), (1,)), ((), ()))
TN = (((0,), (0,)), ((), ()))
HIGHEST = lax.Precision.HIGHEST
MESH = pl.DeviceIdType.MESH


def _params(sem=None, est_bytes=0):
    limit = int(min(max(2 * est_bytes + (4 << 20), VMEM_FLOOR), VMEM_CAP))
    if sem is None:
        return pltpu.CompilerParams(vmem_limit_bytes=limit)
    return pltpu.CompilerParams(dimension_semantics=sem, vmem_limit_bytes=limit)


def _nbytes(shape, dtype):
    return int(np.prod(shape)) * jnp.dtype(dtype).itemsize


def _in_hbm(*xs):
    return [pltpu.with_memory_space_constraint(x, pltpu.HBM) for x in xs]


def _mm(a, b, dims, out_dtype, tm, tn, name, col_shards=False, swap=(), b_outer=False):
    def spec(block, index):
        if b_outer:
            return pl.BlockSpec(block, lambda g0, g1: index(g1, g0))
        return pl.BlockSpec(block, index)

    if dims == "nn":
        (m, k), (k2, n) = a.shape, b.shape
        a_spec = spec((tm, k), lambda i, j: (i, 0))
        b_spec = spec((k, tn), lambda i, j: (0, j))
        dn = NN
    elif dims == "nt":
        (m, k), (n, k2) = a.shape, b.shape
        a_spec = spec((tm, k), lambda i, j: (i, 0))
        b_spec = spec((tn, k), lambda i, j: (j, 0))
        dn = NT
    else:
        (k, m), (k2, n) = a.shape, b.shape
        a_spec = spec((k, tm), lambda i, j: (0, i))
        b_spec = spec((k, tn), lambda i, j: (0, j))
        dn = TN
    assert k == k2 and m % tm == 0 and n % tn == 0, (name, a.shape, b.shape, tm, tn)

    nw = len(swap)
    grid = (n // tn, m // tm) if b_outer else (m // tm, n // tn)

    def body(*refs):
        a_ref, b_ref, o_ref = refs[0], refs[1], refs[2 + nw]
        comm = (refs[2:2 + nw], refs[3 + nw:3 + 2 * nw]) + tuple(refs[3 + 2 * nw:])
        if nw:
            @pl.when((pl.program_id(0) == 0) & (pl.program_id(1) == 0))
            def _():
                _PairSwap(*comm).start()

        o_ref[...] = lax.dot_general(a_ref[...], b_ref[...], dn, preferred_element_type=F32).astype(o_ref.dtype)

        if nw:
            @pl.when((pl.program_id(0) == grid[0] - 1) & (pl.program_id(1) == grid[1] - 1))
            def _():
                _PairSwap(*comm).finish()

    est = _nbytes((tm, k), a.dtype) + _nbytes((tn, k), b.dtype) + _nbytes((tm, tn), F32) + _nbytes((tm, tn), out_dtype)
    if col_shards:
        out_spec = spec((None, tm, tn), lambda i, j: (j, i, 0))
        out_shape = jax.ShapeDtypeStruct((n // tn, m, tn), out_dtype)
    else:
        out_spec = spec((tm, tn), lambda i, j: (i, j))
        out_shape = jax.ShapeDtypeStruct((m, n), out_dtype)
    out = pl.pallas_call(
        body, name=name, grid=grid,
        in_specs=[a_spec, b_spec] + [ANY] * nw,
        out_specs=[out_spec] + [ANY] * nw,
        out_shape=[out_shape] + _PairSwap.out_shapes(swap),
        scratch_shapes=_PairSwap.semaphores(nw) if nw else [],
        compiler_params=_params(("arbitrary", "arbitrary") if nw else ("parallel", "parallel"), est),
    )(a, b, *swap)
    return out if nw else out[0]


def _seg_consts():
    seg_q = np.zeros((HEADS * LANE, LANE), np.float32)
    inv_q = np.zeros((1, LANE), np.float32)
    seg_k = np.zeros((HEADS * LANE, LANE), np.float32)
    inv_k = np.zeros((1, LANE), np.float32)
    seg_d = np.zeros((DIL_W, LANE), np.float32)
    inv_d = np.zeros((1, LANE), np.float32)
    for h in range(HEADS):
        seg_q[h * LANE:h * LANE + NOPE, 2 * h] = 1.0
        seg_q[h * LANE + NOPE:h * LANE + NOPE + ROPE, 2 * h + 1] = 1.0
        inv_q[0, 2 * h], inv_q[0, 2 * h + 1] = 1.0 / NOPE, 1.0 / ROPE
        seg_k[h * LANE:h * LANE + NOPE, h] = 1.0
        inv_k[0, h] = 1.0 / NOPE
        seg_d[h * DIL_DIM:(h + 1) * DIL_DIM, h] = 1.0
        inv_d[0, h] = 1.0 / DIL_DIM
    fold_q = np.tile(np.eye(LANE, dtype=np.float32), (HEADS, 1))
    fold_d = np.zeros((DIL_W, LANE), np.float32)
    fold_d[np.arange(DIL_W), np.arange(DIL_W) % DIL_DIM] = 1.0
    j = lambda v: jnp.asarray(v)
    b = lambda v: jnp.asarray(v, dtype=BF16)
    return dict(seg_q=b(seg_q), exp_q=b(seg_q.T.copy()), inv_q=j(inv_q), seg_k=b(seg_k), exp_k=b(seg_k.T.copy()),
                inv_k=j(inv_k), seg_d=b(seg_d), exp_d=b(seg_d.T.copy()), inv_d=j(inv_d), fold_q=j(fold_q), fold_d=j(fold_d))


def _rope_consts():
    inv_d = jnp.power(ROPE_THETA, -2.0 * jnp.arange(DIL_DIM // 2, dtype=F32) / DIL_DIM)
    inv_q = jnp.power(ROPE_THETA, -2.0 * jnp.arange(ROPE // 2, dtype=F32) / ROPE)
    lanes = np.arange(LANE)
    freq_d = inv_d[lanes % (DIL_DIM // 2)]
    in_pe = (lanes >= KPE_OFF) & (lanes < KPE_OFF + ROPE)
    freq_q = jnp.where(jnp.asarray(in_pe), inv_q[(lanes - KPE_OFF) % (ROPE // 2)], 0.0)
    sign_d = np.where(lanes % DIL_DIM < DIL_DIM // 2, -1.0, 1.0).astype(np.float32)
    sign_q = np.where(in_pe, np.where((lanes - KPE_OFF) < ROPE // 2, -1.0, 1.0), 0.0).astype(np.float32)
    zeros, ones = np.zeros(LANE, np.float32), np.ones(LANE, np.float32)
    freq = jnp.concatenate([freq_d, freq_d, freq_q, freq_q])[None, :]
    csel = jnp.asarray(np.concatenate([ones, zeros, ones, zeros]))[None, :]
    ssel = jnp.asarray(np.concatenate([zeros, sign_d, zeros, sign_q]))[None, :]
    return freq, csel, ssel


def _full(shape):
    return pl.BlockSpec(shape, lambda *_: (0,) * len(shape))


def _tile_lanes(x, n):
    return jnp.concatenate([x] * n, axis=1)


def _rms(x):
    return lax.rsqrt(jnp.mean(x * x, axis=-1, keepdims=True) + EPS)


def _prenorm(x, gain, scale, shift, name):
    s, d = x.shape

    def body(x_ref, g_ref, sc_ref, sh_ref, h_ref):
        xv = x_ref[...]
        h = (xv * _rms(xv)) * g_ref[...] * (1.0 + sc_ref[...]) + sh_ref[...]
        h_ref[...] = h.astype(BF16)

    row = pl.BlockSpec((NORM_TILE, d), lambda i: (i, 0))
    return pl.pallas_call(
        body, name=name, grid=(s // NORM_TILE,),
        in_specs=[row, _full((1, d)), _full((1, d)), _full((1, d))],
        out_specs=row, out_shape=jax.ShapeDtypeStruct((s, d), BF16),
        compiler_params=_params(("parallel",)),
    )(x, gain, scale, shift)


def _latnorm(proj, g_q, g_kv):
    s = proj.shape[0]

    def body(q_ref, kv_ref, gq_ref, gkv_ref, ql_ref, kvl_ref):
        q, kv = q_ref[...], kv_ref[...]
        ql_ref[...] = ((q * _rms(q)) * gq_ref[...]).astype(BF16)
        kvl_ref[...] = ((kv * _rms(kv)) * gkv_ref[...]).astype(BF16)

    return pl.pallas_call(
        body, name="latnorm", grid=(s // NORM_TILE,),
        in_specs=[pl.BlockSpec((NORM_TILE, Q_LORA), lambda i: (i, P_QLAT // Q_LORA)),
                  pl.BlockSpec((NORM_TILE, KV_LORA), lambda i: (i, P_KVLAT // KV_LORA)),
                  _full((1, Q_LORA)), _full((1, KV_LORA))],
        out_specs=[pl.BlockSpec((NORM_TILE, Q_LORA), lambda i: (i, 0)), pl.BlockSpec((NORM_TILE, KV_LORA), lambda i: (i, 0))],
        out_shape=[jax.ShapeDtypeStruct((s, Q_LORA), BF16), jax.ShapeDtypeStruct((s, KV_LORA), BF16)],
        compiler_params=_params(("parallel",)),
    )(proj, proj, g_q, g_kv)


def _dot01(v, mat01):
    hi = v.astype(BF16)
    lo = (v - hi.astype(F32)).astype(BF16)
    return jnp.dot(hi, mat01, preferred_element_type=F32) + jnp.dot(lo, mat01, preferred_element_type=F32)


def _seg_rinv(x, seg, exp, inv):
    r = lax.rsqrt(_dot01(x * x, seg) * inv + EPS)
    return _dot01(r, exp)


def _seg_mean(v, seg, exp, inv):
    return _dot01(_dot01(v, seg) * inv, exp)


def _swap_halves(x, half):
    n = x.shape[1]
    lane = lax.broadcasted_iota(I32, (1, n), 1)
    first = (lane & (2 * half - 1)) < half
    return jnp.where(first, pltpu.roll(x, n - half, 1), pltpu.roll(x, half, 1))


def _rope(x, cos, sin_signed, half):
    return x * cos + _swap_halves(x, half) * sin_signed


def _rope_bwd(dy, cos, sin_signed, half):
    return dy * cos + _swap_halves(dy * sin_signed, half)


def _pe_lane_mask(n):
    lane = lax.broadcasted_iota(I32, (1, n), 1) & (LANE - 1)
    return (lane >= KPE_OFF) & (lane < KPE_OFF + ROPE)


def _attn_prep(q_raw, kv_raw, proj, tab, gains, consts):
    s = q_raw.shape[0]
    hw = HEADS * LANE

    def body(q_ref, kv_ref, kpe_ref, qd_ref, kd_ref, vd_ref, tab_ref,
             gq_ref, gk_ref, gkpe_ref, gdq_ref, gdk_ref,
             segq_ref, expq_ref, invq_ref, segk_ref, expk_ref, invk_ref, segd_ref, expd_ref, invd_ref,
             qm_ref, km_ref, vm_ref, qdo_ref, kdo_ref, vdo_ref):
        tab_v = tab_ref[...]
        cos_d, sin_d = _tile_lanes(tab_v[:, 0:LANE], DIL_W // LANE), _tile_lanes(tab_v[:, LANE:2 * LANE], DIL_W // LANE)
        cos_q1, sin_q1 = tab_v[:, 2 * LANE:3 * LANE], tab_v[:, 3 * LANE:4 * LANE]
        cos_q, sin_q = _tile_lanes(cos_q1, HEADS), _tile_lanes(sin_q1, HEADS)

        q = q_ref[...]
        qn = q * _seg_rinv(q, segq_ref[...], expq_ref[...], invq_ref[...]) * gq_ref[...]
        qm_ref[...] = _rope(qn, cos_q, sin_q, ROPE // 2).astype(BF16)

        kv = kv_ref[...]
        kp = kv[:, :hw]
        kn = kp * _seg_rinv(kp, segk_ref[...], expk_ref[...], invk_ref[...]) * gk_ref[...]
        kpe = kpe_ref[...]
        r_pe = lax.rsqrt(jnp.sum(kpe * kpe, axis=-1, keepdims=True) * (1.0 / ROPE) + EPS)
        kpe_r = _rope(kpe * r_pe * gkpe_ref[...], cos_q1, sin_q1, ROPE // 2)
        km_ref[...] = (kn + _tile_lanes(kpe_r, HEADS)).astype(BF16)
        vm_ref[...] = kv[:, hw:].astype(BF16)

        qd = qd_ref[...]
        qdn = qd * _seg_rinv(qd, segd_ref[...], expd_ref[...], invd_ref[...]) * gdq_ref[...]
        qdo_ref[...] = _rope(qdn, cos_d, sin_d, DIL_DIM // 2).astype(BF16)
        kd = kd_ref[...]
        kdn = kd * _seg_rinv(kd, segd_ref[...], expd_ref[...], invd_ref[...]) * gdk_ref[...]
        kdo_ref[...] = _rope(kdn, cos_d, sin_d, DIL_DIM // 2).astype(BF16)
        vdo_ref[...] = vd_ref[...].astype(BF16)

    t = ROW_TILE
    row = lambda w, cb=0: pl.BlockSpec((t, w), lambda i: (i, cb))
    c = consts
    return pl.pallas_call(
        body, name="attn_prep", grid=(s // t,),
        in_specs=[row(hw), row(hw + DIL_W), row(LANE, P_KPE // LANE), row(DIL_W, P_QD // DIL_W), row(DIL_W, P_KD // DIL_W),
                  row(DIL_W, P_VD // DIL_W), row(4 * LANE),
                  _full((1, hw)), _full((1, hw)), _full((1, LANE)), _full((1, DIL_W)), _full((1, DIL_W)),
                  _full((hw, LANE)), _full((LANE, hw)), _full((1, LANE)), _full((hw, LANE)), _full((LANE, hw)), _full((1, LANE)),
                  _full((DIL_W, LANE)), _full((LANE, DIL_W)), _full((1, LANE))],
        out_specs=[row(hw), row(hw), row(DIL_W), row(DIL_W), row(DIL_W), row(DIL_W)],
        out_shape=[jax.ShapeDtypeStruct((s, hw), BF16), jax.ShapeDtypeStruct((s, hw), BF16)]
        + [jax.ShapeDtypeStruct((s, DIL_W), BF16)] * 4,
        compiler_params=_params(("parallel",), 24 << 20),
    )(*_in_hbm(q_raw, kv_raw, proj, proj, proj, proj), tab, gains["q"], gains["k"], gains["kpe"], gains["dq"], gains["dk"],
      c["seg_q"], c["exp_q"], c["inv_q"], c["seg_k"], c["exp_k"], c["inv_k"], c["seg_d"], c["exp_d"], c["inv_d"])


def _attn_prep_bwd(dqm, dkm, dvm, dqd, dkd, dvd, q_raw, kv_raw, proj, tab, gains, consts):
    s = q_raw.shape[0]
    hw = HEADS * LANE
    n_steps = s // ROW_TILE

    def body(dqm_ref, dkm_ref, dvm_ref, dqd_ref, dkd_ref, dvd_ref, q_ref, kv_ref, kpe_ref, qd_ref, kd_ref, tab_ref,
             gq_ref, gk_ref, gkpe_ref, gdq_ref, gdk_ref,
             segq_ref, expq_ref, invq_ref, segk_ref, expk_ref, invk_ref, segd_ref, expd_ref, invd_ref, foldq_ref, foldd_ref,
             dq_ref, dkv_ref, dkpe_ref, dqdo_ref, dkdo_ref, dvdo_ref, dg_ref, acc_ref):
        i = pl.program_id(0)

        @pl.when(i == 0)
        def _():
            acc_ref[...] = jnp.zeros_like(acc_ref)

        tab_v = tab_ref[...]
        cos_d, sin_d = _tile_lanes(tab_v[:, 0:LANE], DIL_W // LANE), _tile_lanes(tab_v[:, LANE:2 * LANE], DIL_W // LANE)
        cos_q1, sin_q1 = tab_v[:, 2 * LANE:3 * LANE], tab_v[:, 3 * LANE:4 * LANE]
        cos_q, sin_q = _tile_lanes(cos_q1, HEADS), _tile_lanes(sin_q1, HEADS)

        def norm_bwd(x, dyg, gain, seg, exp, inv):
            rinv = _seg_rinv(x, seg, exp, inv)
            xn = x * rinv
            dxn = dyg * gain
            dx = rinv * (dxn - xn * _seg_mean(dxn * xn, seg, exp, inv))
            return dx, jnp.sum(dyg * xn, axis=0, keepdims=True)

        dq, gq_l = norm_bwd(q_ref[...], _rope_bwd(dqm_ref[...], cos_q, sin_q, ROPE // 2), gq_ref[...],
                            segq_ref[...], expq_ref[...], invq_ref[...])
        dq_ref[...] = dq.astype(BF16)

        dkm = dkm_ref[...]
        kv = kv_ref[...]
        dkp, gk_l = norm_bwd(kv[:, :hw], dkm, gk_ref[...], segk_ref[...], expk_ref[...], invk_ref[...])
        dkv_ref[:, :hw] = dkp.astype(BF16)
        dkv_ref[:, hw:] = dvm_ref[...].astype(BF16)

        dkpe_r = dkm[:, 0:LANE]
        for h in range(1, HEADS):
            dkpe_r = dkpe_r + dkm[:, h * LANE:(h + 1) * LANE]
        dkpe_r = jnp.where(_pe_lane_mask(LANE), dkpe_r, 0.0)
        dyg = _rope_bwd(dkpe_r, cos_q1, sin_q1, ROPE // 2)
        kpe = kpe_ref[...]
        r_pe = lax.rsqrt(jnp.sum(kpe * kpe, axis=-1, keepdims=True) * (1.0 / ROPE) + EPS)
        xn = kpe * r_pe
        dxn = dyg * gkpe_ref[...]
        dkpe = r_pe * (dxn - xn * (jnp.sum(dxn * xn, axis=-1, keepdims=True) * (1.0 / ROPE)))
        dkpe_ref[...] = dkpe.astype(BF16)
        gkpe_l = jnp.sum(dyg * xn, axis=0, keepdims=True)

        dqd_v, gdq_l = norm_bwd(qd_ref[...], _rope_bwd(dqd_ref[...], cos_d, sin_d, DIL_DIM // 2), gdq_ref[...],
                                segd_ref[...], expd_ref[...], invd_ref[...])
        dqdo_ref[...] = dqd_v.astype(BF16)
        dkd_v, gdk_l = norm_bwd(kd_ref[...], _rope_bwd(dkd_ref[...], cos_d, sin_d, DIL_DIM // 2), gdk_ref[...],
                                segd_ref[...], expd_ref[...], invd_ref[...])
        dkdo_ref[...] = dkd_v.astype(BF16)
        dvdo_ref[...] = dvd_ref[...].astype(BF16)

        acc_ref[0:1, :] += gq_l
        acc_ref[1:2, :] += gk_l
        acc_ref[2:3, 0:LANE] += gkpe_l
        acc_ref[3:4, 0:DIL_W] += gdq_l
        acc_ref[4:5, 0:DIL_W] += gdk_l

        @pl.when(i == n_steps - 1)
        def _():
            acc = acc_ref[...]
            fq = jnp.dot(acc, foldq_ref[...], precision=HIGHEST, preferred_element_type=F32)
            fd = jnp.dot(acc[:, 0:DIL_W], foldd_ref[...], precision=HIGHEST, preferred_element_type=F32)
            rows = lax.broadcasted_iota(I32, (8, LANE), 0)
            base = jnp.where(rows < 2, fq, jnp.where(rows == 2, acc[:, 0:LANE], fd))
            at0 = pltpu.roll(base, LANE - KPE_OFF, 1)
            dg_ref[...] = jnp.where(rows == 5, pltpu.roll(at0, 5, 0), jnp.where(rows == 2, at0, base))

    t = ROW_TILE
    row = lambda w, cb=0: pl.BlockSpec((t, w), lambda i: (i, cb))
    c = consts
    return pl.pallas_call(
        body, name="attn_prep_bwd", grid=(n_steps,),
        in_specs=[row(hw), row(hw), row(DIL_W), row(DIL_W), row(DIL_W), row(DIL_W),
                  row(hw), row(hw + DIL_W), row(LANE, P_KPE // LANE), row(DIL_W, P_QD // DIL_W), row(DIL_W, P_KD // DIL_W),
                  row(4 * LANE),
                  _full((1, hw)), _full((1, hw)), _full((1, LANE)), _full((1, DIL_W)), _full((1, DIL_W)),
                  _full((hw, LANE)), _full((LANE, hw)), _full((1, LANE)), _full((hw, LANE)), _full((LANE, hw)), _full((1, LANE)),
                  _full((DIL_W, LANE)), _full((LANE, DIL_W)), _full((1, LANE)), _full((hw, LANE)), _full((DIL_W, LANE))],
        out_specs=[row(hw), row(hw + DIL_W), row(LANE), row(DIL_W), row(DIL_W), row(DIL_W), _full((8, LANE))],
        out_shape=[jax.ShapeDtypeStruct((s, hw), BF16), jax.ShapeDtypeStruct((s, hw + DIL_W), BF16),
                   jax.ShapeDtypeStruct((s, LANE), BF16)] + [jax.ShapeDtypeStruct((s, DIL_W), BF16)] * 3
        + [jax.ShapeDtypeStruct((8, LANE), F32)],
        scratch_shapes=[pltpu.VMEM((8, hw), F32)],
        compiler_params=_params(("arbitrary",), 28 << 20),
    )(*_in_hbm(dqm, dkm, dvm, dqd, dkd, dvd, q_raw, kv_raw, proj, proj, proj), tab,
      gains["q"], gains["k"], gains["kpe"], gains["dq"], gains["dk"],
      c["seg_q"], c["exp_q"], c["inv_q"], c["seg_k"], c["exp_k"], c["inv_k"], c["seg_d"], c["exp_d"], c["inv_d"],
      c["fold_q"], c["fold_d"])


def _latnorm_bwd(dql, dkvl, proj, g_q, g_kv):
    s = proj.shape[0]
    n_steps = s // NORM_TILE

    def body(dql_ref, dkvl_ref, q_ref, kv_ref, gq_ref, gkv_ref, dq_ref, dkv_ref, dg_ref):
        i = pl.program_id(0)

        @pl.when(i == 0)
        def _():
            dg_ref[...] = jnp.zeros_like(dg_ref)

        def one(x, dyg, gain):
            r = _rms(x)
            xn = x * r
            dxn = dyg * gain
            dx = r * (dxn - xn * jnp.mean(dxn * xn, axis=-1, keepdims=True))
            return dx, jnp.sum(dyg * xn, axis=0, keepdims=True)

        dq, gq_l = one(q_ref[...], dql_ref[...], gq_ref[...])
        dkv, gkv_l = one(kv_ref[...], dkvl_ref[...], gkv_ref[...])
        dq_ref[...] = dq.astype(BF16)
        dkv_ref[...] = dkv.astype(BF16)
        dg_ref[0:1, :] += gq_l
        dg_ref[1:2, 0:KV_LORA] += gkv_l

    t = NORM_TILE
    return pl.pallas_call(
        body, name="latnorm_bwd", grid=(n_steps,),
        in_specs=[pl.BlockSpec((t, Q_LORA), lambda i: (i, 0)), pl.BlockSpec((t, KV_LORA), lambda i: (i, 0)),
                  pl.BlockSpec((t, Q_LORA), lambda i: (i, P_QLAT // Q_LORA)),
                  pl.BlockSpec((t, KV_LORA), lambda i: (i, P_KVLAT // KV_LORA)),
                  _full((1, Q_LORA)), _full((1, KV_LORA))],
        out_specs=[pl.BlockSpec((t, Q_LORA), lambda i: (i, 0)), pl.BlockSpec((t, KV_LORA), lambda i: (i, 0)), _full((8, Q_LORA))],
        out_shape=[jax.ShapeDtypeStruct((s, Q_LORA), BF16), jax.ShapeDtypeStruct((s, KV_LORA), BF16),
                   jax.ShapeDtypeStruct((8, Q_LORA), F32)],
        compiler_params=_params(("arbitrary",)),
    )(dql, dkvl, proj, proj, g_q, g_kv)


def _resid_prenorm(x, mix, g1, gain, scale, shift):
    s, d = x.shape

    def body(x_ref, mix_ref, g1_ref, g_ref, sc_ref, sh_ref, x1_ref, h_ref):
        x1 = x_ref[...] + g1_ref[...] * mix_ref[...]
        x1_ref[...] = x1
        h_ref[...] = ((x1 * _rms(x1)) * g_ref[...] * (1.0 + sc_ref[...]) + sh_ref[...]).astype(BF16)

    row = pl.BlockSpec((NORM_TILE, d), lambda i: (i, 0))
    vec = _full((1, d))
    return pl.pallas_call(
        body, name="resid_prenorm", grid=(s // NORM_TILE,),
        in_specs=[row, row, vec, vec, vec, vec], out_specs=[row, row],
        out_shape=[jax.ShapeDtypeStruct((s, d), F32), jax.ShapeDtypeStruct((s, d), BF16)],
        compiler_params=_params(("parallel",)),
    )(x, mix, g1, gain, scale, shift)


CONV_TILE = 1408
HALO = 8


def _shift_down(x, halo, k):
    t = x.shape[0]
    row = lax.broadcasted_iota(I32, (t, 1), 0)
    out = pltpu.roll(x, k, 0)
    for r in range(k):
        out = jnp.where(row == r, halo[HALO - k + r:HALO - k + r + 1, :], out)
    return out


def _shift_up(x, halo, k):
    t = x.shape[0]
    row = lax.broadcasted_iota(I32, (t, 1), 0)
    out = pltpu.roll(x, t - k, 0)
    for r in range(k):
        out = jnp.where(row == t - k + r, halo[r:r + 1, :], out)
    return out


def _conv_fwd(x, halo, w, b):
    p1, p2 = _shift_down(x, halo, 1), _shift_down(x, halo, 2)
    u = b + p2 * w[0:1, :]
    u = u + p1 * w[1:2, :]
    u = u + x * w[2:3, :]
    return u, p1, p2


def _sigmoid(x):
    return 0.5 * jnp.tanh(0.5 * x) + 0.5


def _conv_gate(up, w_conv, b_conv):
    s = up.shape[0]
    t = ROW_TILE
    nj = D_FF // CONV_TILE
    hb = t // HALO

    def body(g_ref, v_ref, gh_ref, vh_ref, wg_ref, wv_ref, bg_ref, bv_ref, a_ref):
        live = (pl.program_id(0) > 0).astype(F32)
        ug, _, _ = _conv_fwd(g_ref[...], gh_ref[...] * live, wg_ref[...], bg_ref[...])
        uv, _, _ = _conv_fwd(v_ref[...], vh_ref[...] * live, wv_ref[...], bv_ref[...])
        a_ref[...] = (ug * _sigmoid(ug) * uv).astype(BF16)

    main = lambda off: pl.BlockSpec((t, CONV_TILE), lambda i, j: (i, j + off))
    halo = lambda off: pl.BlockSpec((HALO, CONV_TILE), lambda i, j: (jnp.maximum(i * hb - 1, 0), j + off))
    wsp = lambda off: pl.BlockSpec((3, CONV_TILE), lambda i, j: (0, j + off))
    bsp = lambda off: pl.BlockSpec((1, CONV_TILE), lambda i, j: (0, j + off))
    return pl.pallas_call(
        body, name="conv_gate", grid=(s // t, nj),
        in_specs=[main(0), main(nj), halo(0), halo(nj), wsp(0), wsp(nj), bsp(0), bsp(nj)],
        out_specs=pl.BlockSpec((t, CONV_TILE), lambda i, j: (i, j)),
        out_shape=jax.ShapeDtypeStruct((s, D_FF), BF16),
        compiler_params=_params(("parallel", "parallel"), 12 << 20),
    )(up, up, up, up, w_conv, w_conv, b_conv, b_conv)


def _gate_bwd(up, da, w_conv, b_conv):
    s = up.shape[0]
    t = ROW_TILE
    nj = D_FF // CONV_TILE
    hb = t // HALO
    n_i = s // t

    def body(g_ref, v_ref, gh_ref, vh_ref, gn_ref, vn_ref, da_ref, dan_ref, wg_ref, wv_ref, bg_ref, bv_ref,
             dupg_ref, dupv_ref, dbg_ref, dbv_ref, dwg_ref, dwv_ref):
        i = pl.program_id(1)

        @pl.when(i == 0)
        def _():
            for r in (dbg_ref, dbv_ref, dwg_ref, dwv_ref):
                r[...] = jnp.zeros_like(r)

        def d_gate(ug, uv, da_v):
            sg = _sigmoid(ug)
            return da_v * uv * (sg * (1.0 + ug * (1.0 - sg))), da_v * (ug * sg)

        live = (i > 0).astype(F32)
        xg, xv = g_ref[...], v_ref[...]
        wg, wv = wg_ref[...], wv_ref[...]
        ug, g1, g2 = _conv_fwd(xg, gh_ref[...] * live, wg, bg_ref[...])
        uv, v1, v2 = _conv_fwd(xv, vh_ref[...] * live, wv, bv_ref[...])
        dug, duv = d_gate(ug, uv, da_ref[...])

        more = (i < n_i - 1).astype(F32)
        ug_n, _, _ = _conv_fwd(gn_ref[...], xg[t - HALO:, :], wg, bg_ref[...])
        uv_n, _, _ = _conv_fwd(vn_ref[...], xv[t - HALO:, :], wv, bv_ref[...])
        dug_n, duv_n = d_gate(ug_n, uv_n, dan_ref[...] * more)

        def conv_t(du, du_n, w):
            return du * w[2:3, :] + _shift_up(du, du_n, 1) * w[1:2, :] + _shift_up(du, du_n, 2) * w[0:1, :]

        dupg_ref[...] = conv_t(dug, dug_n, wg).astype(BF16)
        dupv_ref[...] = conv_t(duv, duv_n, wv).astype(BF16)
        csum = lambda z: jnp.sum(z, axis=0, keepdims=True)
        dbg_ref[...] += csum(dug)
        dbv_ref[...] += csum(duv)
        dwg_ref[0:1, :] += csum(dug * g2)
        dwg_ref[1:2, :] += csum(dug * g1)
        dwg_ref[2:3, :] += csum(dug * xg)
        dwv_ref[0:1, :] += csum(duv * v2)
        dwv_ref[1:2, :] += csum(duv * v1)
        dwv_ref[2:3, :] += csum(duv * xv)

    last_halo = s // HALO - 1
    main = lambda off: pl.BlockSpec((t, CONV_TILE), lambda j, i: (i, j + off))
    halo = lambda off: pl.BlockSpec((HALO, CONV_TILE), lambda j, i: (jnp.maximum(i * hb - 1, 0), j + off))
    nxt = lambda off: pl.BlockSpec((HALO, CONV_TILE), lambda j, i: (jnp.minimum((i + 1) * hb, last_halo), j + off))
    wsp = lambda off: pl.BlockSpec((3, CONV_TILE), lambda j, i: (0, j + off))
    bsp = lambda off: pl.BlockSpec((1, CONV_TILE), lambda j, i: (0, j + off))
    outs = pl.pallas_call(
        body, name="gate_bwd", grid=(nj, n_i),
        in_specs=[main(0), main(nj), halo(0), halo(nj), nxt(0), nxt(nj), main(0), nxt(0),
                  wsp(0), wsp(nj), bsp(0), bsp(nj)],
        out_specs=[main(0), main(0),
                   pl.BlockSpec((1, CONV_TILE), lambda j, i: (0, j)), pl.BlockSpec((1, CONV_TILE), lambda j, i: (0, j)),
                   pl.BlockSpec((3, CONV_TILE), lambda j, i: (0, j)), pl.BlockSpec((3, CONV_TILE), lambda j, i: (0, j))],
        out_shape=[jax.ShapeDtypeStruct((s, D_FF), BF16), jax.ShapeDtypeStruct((s, D_FF), BF16),
                   jax.ShapeDtypeStruct((1, D_FF), F32), jax.ShapeDtypeStruct((1, D_FF), F32),
                   jax.ShapeDtypeStruct((3, D_FF), F32), jax.ShapeDtypeStruct((3, D_FF), F32)],
        compiler_params=_params(("parallel", "arbitrary"), 24 << 20),
    )(up, up, up, up, up, up, da, da, w_conv, w_conv, b_conv, b_conv)
    return outs


def _final(x1, ffn, tgt, g2):
    s, d = x1.shape
    n_steps = s // NORM_TILE

    def body(x1_ref, f_ref, t_ref, g2_ref, dy_ref, df_ref, dg2_ref, loss_ref, lacc_ref):
        i = pl.program_id(0)

        @pl.when(i == 0)
        def _():
            dg2_ref[...] = jnp.zeros_like(dg2_ref)
            lacc_ref[...] = jnp.zeros_like(lacc_ref)

        f = f_ref[...]
        e = x1_ref[...] + g2_ref[...] * f - t_ref[...]
        dy = e * (1.0 / d)
        dy_ref[...] = dy
        df_ref[...] = (dy * g2_ref[...]).astype(BF16)
        dg2_ref[...] += jnp.sum(dy * f, axis=0, keepdims=True)
        lacc_ref[...] += jnp.sum(e * e, axis=0, keepdims=True)

        @pl.when(i == n_steps - 1)
        def _():
            loss_ref[...] = jnp.sum(lacc_ref[...], axis=1, keepdims=True) * (0.5 / d)

    row = pl.BlockSpec((NORM_TILE, d), lambda i: (i, 0))
    return pl.pallas_call(
        body, name="final", grid=(n_steps,),
        in_specs=[row, row, row, _full((1, d))],
        out_specs=[row, row, _full((1, d)), _full((1, 1))],
        out_shape=[jax.ShapeDtypeStruct((s, d), F32), jax.ShapeDtypeStruct((s, d), BF16),
                   jax.ShapeDtypeStruct((1, d), F32), jax.ShapeDtypeStruct((1, 1), F32)],
        scratch_shapes=[pltpu.VMEM((1, d), F32)],
        compiler_params=_params(("arbitrary",)),
    )(x1, ffn, tgt, g2)


def _ffnnorm_bwd(dh2, x1, dy, mix, gain, scale, g1):
    s, d = x1.shape
    n_steps = s // NORM_TILE

    def body(dh_ref, x_ref, dy_ref, mix_ref, g_ref, sc_ref, g1_ref, dx_ref, dm_ref, acc_ref):
        i = pl.program_id(0)

        @pl.when(i == 0)
        def _():
            acc_ref[...] = jnp.zeros_like(acc_ref)

        dh, x = dh_ref[...], x_ref[...]
        r = _rms(x)
        xn = x * r
        dn = dh * (1.0 + sc_ref[...])
        dxn = dn * g_ref[...]
        dx = dy_ref[...] + r * (dxn - xn * jnp.mean(dxn * xn, axis=-1, keepdims=True))
        dx_ref[...] = dx
        dm_ref[...] = (dx * g1_ref[...]).astype(BF16)
        csum = lambda z: jnp.sum(z, axis=0, keepdims=True)
        acc_ref[0:1, :] += csum(dh)
        acc_ref[1:2, :] += csum(dh * (xn * g_ref[...]))
        acc_ref[2:3, :] += csum(dn * xn)
        acc_ref[3:4, :] += csum(dx * mix_ref[...])

    row = pl.BlockSpec((NORM_TILE, d), lambda i: (i, 0))
    vec = _full((1, d))
    return pl.pallas_call(
        body, name="ffnnorm_bwd", grid=(n_steps,),
        in_specs=[row, row, row, row, vec, vec, vec],
        out_specs=[row, row, _full((8, d))],
        out_shape=[jax.ShapeDtypeStruct((s, d), F32), jax.ShapeDtypeStruct((s, d), BF16), jax.ShapeDtypeStruct((8, d), F32)],
        compiler_params=_params(("arbitrary",)),
    )(dh2, x1, dy, mix, gain, scale, g1)


def _mixnorm_bwd(dh, x, dx1, gain, scale):
    s, d = x.shape
    n_steps = s // NORM_TILE

    def body(dh_ref, x_ref, dx1_ref, g_ref, sc_ref, gx_ref, acc_ref):
        i = pl.program_id(0)

        @pl.when(i == 0)
        def _():
            acc_ref[...] = jnp.zeros_like(acc_ref)

        dh, x = dh_ref[...], x_ref[...]
        r = _rms(x)
        xn = x * r
        dn = dh * (1.0 + sc_ref[...])
        dxn = dn * g_ref[...]
        gx_ref[...] = dx1_ref[...] + r * (dxn - xn * jnp.mean(dxn * xn, axis=-1, keepdims=True))
        csum = lambda z: jnp.sum(z, axis=0, keepdims=True)
        acc_ref[0:1, :] += csum(dh)
        acc_ref[1:2, :] += csum(dh * (xn * g_ref[...]))
        acc_ref[2:3, :] += csum(dn * xn)

    row = pl.BlockSpec((NORM_TILE, d), lambda i: (i, 0))
    vec = _full((1, d))
    return pl.pallas_call(
        body, name="mixnorm_bwd", grid=(n_steps,),
        in_specs=[row, row, row, vec, vec],
        out_specs=[row, _full((8, d))],
        out_shape=[jax.ShapeDtypeStruct((s, d), F32), jax.ShapeDtypeStruct((8, d), F32)],
        compiler_params=_params(("arbitrary",)),
    )(dh, x, dx1, gain, scale)


def _key_count(d, dilated):
    if not dilated:
        return jnp.where(d >= 0, 1.0, 0.0)
    one = lambda cond: jnp.where(cond, 1.0, 0.0)
    cnt = one(d <= 128) + one(((d & 3) == 0) & (d <= 512)) + one((d & 15) == 0)
    return jnp.where(d >= 0, cnt, 0.0)


def _block_kinds(mla):
    return (0, "diag", "none") if mla else (NEAR_REACH, "near", "far")


NEAR_REACH = 512


def _near_offsets(tk, tq):
    return (NEAR_REACH - (tk - tq)) // tk + 1


def _scores_t(ka, qa, scale, kind, rel_t, offset, near_tabs=None):
    return _mask_scores(lax.dot_general(ka, qa, NT, preferred_element_type=F32), scale, kind, rel_t, offset, near_tabs)


def _fill_near_tables(bias_ref, cnt_ref, rel_t):
    tk, tq = rel_t.shape
    for idx in range(_near_offsets(tk, tq)):
        cnt = _key_count(rel_t + (tk - tq) + idx * tk, True)
        cnt_ref[idx] = cnt
        bias_ref[idx] = jnp.where(cnt > 0.0, 0.0, NEG_INF)


def _mask_scores(products, scale, kind, rel_t, offset, near_tabs=None):
    st = products * (scale * LOG2E)
    cnt = None
    if kind == "diag":
        st = jnp.where(rel_t + offset >= 0, st, NEG_INF)
    elif kind == "far":
        st = jnp.where((rel_t & 15) == 0, st, NEG_INF)
    elif kind == "near":
        bias_ref, cnt_ref = near_tabs
        tk, tq = rel_t.shape
        idx = (offset - (tk - tq)) // tk
        st = st + bias_ref[idx]
        cnt = cnt_ref[idx]
    return st, cnt


def _attn_fwd(q, k, v, mla, scale, name, gather=()):
    s = q.shape[0]
    qw = 2 * LANE if mla else LANE
    tq, tk = ATT_TQ, ATT_TK
    reach, kind_near, kind_far = _block_kinds(mla)
    assert s % tq == 0 and tq % tk == 0 and reach % tk == 0 and reach in (0, NEAR_REACH)
    ng = len(gather)
    last_step = HEADS // 2 - 1

    def body(*refs):
        q_ref, k_ref, v_ref = refs[:3]
        o_ref, lse_ref = refs[3 + ng:5 + ng]
        vt_ref, st_ref = refs[5 + 2 * ng:7 + 2 * ng]
        near_tabs = None if mla else refs[7 + 2 * ng:9 + 2 * ng]
        n_tabs = 0 if mla else 2
        comm = (refs[3:3 + ng], refs[5 + ng:5 + 2 * ng]) + tuple(refs[7 + n_tabs + 2 * ng:])
        if ng:
            @pl.when(pl.program_id(0) == 0)
            def _():
                _Gather(*comm).start()

            @pl.when(pl.program_id(0) == last_step)
            def _():
                _Gather(*comm).forward()

        lane = lax.broadcasted_iota(I32, (1, LANE), 1)
        rel_t = lax.broadcasted_iota(I32, (tk, tq), 1) - lax.broadcasted_iota(I32, (tk, tq), 0)
        if not mla:
            _fill_near_tables(*near_tabs, rel_t)

        def transpose_v(j, carry):
            c0 = pl.multiple_of(j * tk, tk)
            vt_ref[:, pl.ds(c0, tk)] = v_ref[pl.ds(c0, tk), :].astype(F32).T.astype(BF16)
            return carry

        lax.fori_loop(0, s // tk, transpose_v, 0)

        def q_block(qi, carry):
            r0 = pl.multiple_of(qi * tq, tq)
            kcols = [slice(a * LANE, (a + 1) * LANE) if mla else slice(0, LANE) for a in range(2)]
            qas = [q_ref[pl.ds(r0, tq), kcols[a]] for a in range(2)]
            if not mla:
                qas = [jnp.where(lane < DIL_DIM, qas[0], jnp.zeros_like(qas[0])),
                       jnp.where(lane >= DIL_DIM, qas[1], jnp.zeros_like(qas[1]))]

            n_k = (r0 + tq) // tk

            def products(kj):
                c0 = pl.multiple_of(kj * tk, tk)
                return [lax.dot_general(k_ref[pl.ds(c0, tk), kcols[a]], qas[a], NT, preferred_element_type=F32)
                        for a in range(2)]

            for a, pr in enumerate(products(0)):
                st_ref[0, a] = pr

            def k_block(kj, c, kind):
                c0 = pl.multiple_of(kj * tk, tk)
                slot = kj & 1
                ahead = products(jnp.minimum(kj + 1, n_k - 1))
                out = []
                for a in range(2):
                    m, l, acc = c[a]
                    st, cnt = _mask_scores(st_ref[slot, a], scale, kind, rel_t, r0 - c0, near_tabs)
                    st_ref[1 - slot, a] = ahead[a]
                    m_new = jnp.maximum(m, jnp.max(st, axis=0, keepdims=True))
                    alpha = jnp.exp2(m - m_new)
                    p = jnp.exp2(st - m_new)
                    if cnt is not None:
                        p = p * cnt
                    l = alpha * l + jnp.sum(p, axis=0, keepdims=True)
                    vt = vt_ref[a * DIL_DIM:(a + 1) * DIL_DIM, pl.ds(c0, tk)]
                    acc = alpha * acc + jnp.dot(vt, p.astype(BF16), preferred_element_type=F32)
                    out.append((m_new, l, acc))
                return tuple(out)

            one = (jnp.full((1, tq), NEG_INF, F32), jnp.zeros((1, tq), F32), jnp.zeros((DIL_DIM, tq), F32))
            first_near = jnp.maximum((r0 - reach) // tk, 0)
            c = lax.fori_loop(0, first_near, functools.partial(k_block, kind=kind_far), (one, one))
            res = lax.fori_loop(first_near, (r0 + tq) // tk, functools.partial(k_block, kind=kind_near), c)
            o_t = jnp.concatenate([res[a][2] / res[a][1] for a in range(2)], axis=0)
            o_ref[pl.ds(r0, tq), :] = o_t.T.astype(BF16)
            for a in range(2):
                lse_ref[a, :, pl.ds(r0, tq)] = res[a][0] * LN2 + jnp.log(res[a][1])
            return carry

        lax.fori_loop(0, s // tq, q_block, 0)

        if ng:
            @pl.when(pl.program_id(0) == last_step)
            def _():
                _Gather(*comm).finish()

    return pl.pallas_call(
        body, name=name, grid=(HEADS // 2,),
        in_specs=[pl.BlockSpec((s, qw), lambda h: (0, h)), pl.BlockSpec((s, qw), lambda h: (0, h)),
                  pl.BlockSpec((s, LANE), lambda h: (0, h))] + [ANY] * ng,
        out_specs=[pl.BlockSpec((s, LANE), lambda h: (0, h)), pl.BlockSpec((2, 1, s), lambda h: (h, 0, 0))] + [ANY] * ng,
        out_shape=[jax.ShapeDtypeStruct((s, DIL_W), BF16), jax.ShapeDtypeStruct((HEADS, 1, s), F32)] + _Gather.out_shapes(gather),
        scratch_shapes=[pltpu.VMEM((LANE, s), BF16), pltpu.VMEM((2, 2, tk, tq), F32)]
        + ([] if mla else [pltpu.VMEM((_near_offsets(tk, tq), tk, tq), F32)] * 2) + (_Gather.scratch(gather) if ng else []),
        compiler_params=_params(("arbitrary",) if ng else ("parallel",), 12 << 20),
    )(*_in_hbm(q, k, v), *gather)


def _attn_bwd(q, k, v, o, do, do_block0, lse, mla, scale, name, scatter=()):
    s = q.shape[0]
    qw = 2 * LANE if mla else LANE
    tq, tk = ATT_TQ, ATT_TK_BWD
    nq = s // tq
    reach, kind_near, kind_far = _block_kinds(mla)
    assert s % tq == 0 and s % tk == 0
    ns = len(scatter)
    last_step = HEADS // 2 - 1

    def body(*refs):
        q_ref, k_ref, v_ref, o_ref, do_ref, lse_ref = refs[:6]
        dq_ref, dk_ref, dv_ref = refs[6 + ns:9 + ns]
        kt_ref, dot_ref, dob_ref, dqt_ref, delta_ref, lse2_ref = refs[9 + 2 * ns:15 + 2 * ns]
        near_tabs = None if mla else refs[15 + 2 * ns:17 + 2 * ns]
        n_tabs = 0 if mla else 2
        comm = (refs[6:6 + ns], refs[9 + ns:9 + 2 * ns]) + tuple(refs[15 + n_tabs + 2 * ns:])
        if ns:
            @pl.when(pl.program_id(0) == 0)
            def _():
                _Scatter(*comm).start()

        lane = lax.broadcasted_iota(I32, (1, LANE), 1)
        row = lax.broadcasted_iota(I32, (LANE, 1), 0)
        rel_t = lax.broadcasted_iota(I32, (tk, tq), 1) - lax.broadcasted_iota(I32, (tk, tq), 0)
        if not mla:
            _fill_near_tables(*near_tabs, rel_t)

        def prepare(j, carry):
            c0 = pl.multiple_of(j * tk, tk)
            do_blk = do_ref[pl.ds(c0, tk), :]
            dob_ref[pl.ds(c0, tk), :] = do_blk.astype(BF16)
            do_t = do_blk.T
            dot_ref[:, pl.ds(c0, tk)] = do_t.astype(BF16)
            prod = do_t * o_ref[pl.ds(c0, tk), :].astype(F32).T
            delta_ref[0, :, pl.ds(c0, tk)] = jnp.sum(prod[0:DIL_DIM], axis=0, keepdims=True)
            delta_ref[1, :, pl.ds(c0, tk)] = jnp.sum(prod[DIL_DIM:LANE], axis=0, keepdims=True)
            for w in range(qw // LANE):
                kt_ref[w * LANE:(w + 1) * LANE, pl.ds(c0, tk)] = (
                    k_ref[pl.ds(c0, tk), w * LANE:(w + 1) * LANE].astype(F32).T.astype(BF16))
            return carry

        lax.fori_loop(0, s // tk, prepare, 0)
        dqt_ref[...] = jnp.zeros_like(dqt_ref)
        lse2_ref[...] = lse_ref[...] * LOG2E

        sels = [lane < DIL_DIM, lane >= DIL_DIM]
        rsels = [row < DIL_DIM, row >= DIL_DIM]
        cols = [slice(a * LANE, (a + 1) * LANE) if mla else slice(0, LANE) for a in range(2)]

        def k_block(kj, carry):
            c0 = pl.multiple_of(kj * tk, tk)
            kas = [k_ref[pl.ds(c0, tk), cols[a]] for a in range(2)]
            kts = [kt_ref[cols[a], pl.ds(c0, tk)] for a in range(2)]
            if not mla:
                kas = [jnp.where(sels[a], kas[a], jnp.zeros_like(kas[a])) for a in range(2)]
                kts = [jnp.where(rsels[a], kts[a], jnp.zeros_like(kts[a])) for a in range(2)]
            vb = v_ref[pl.ds(c0, tk), :]
            vbs = [jnp.where(sels[a], vb, jnp.zeros_like(vb)) for a in range(2)]

            first = c0 // tq

            def q_block(qi, c, kind):
                r0 = pl.multiple_of(qi * tq, tq)
                out, dq_parts = [], []
                for a in range(2):
                    dk_acc, dv_acc = c[a]
                    qa = q_ref[pl.ds(r0, tq), cols[a]]
                    st, cnt = _scores_t(kas[a], qa, scale, kind, rel_t, r0 - c0, near_tabs)
                    p = jnp.exp2(st - lse2_ref[a, :, pl.ds(r0, tq)])
                    if cnt is not None:
                        p = p * cnt
                    dp = jnp.dot(vbs[a], dot_ref[:, pl.ds(r0, tq)], preferred_element_type=F32)
                    ds = (p * (dp - delta_ref[a, :, pl.ds(r0, tq)]) * scale).astype(BF16)
                    dv_acc = dv_acc + jnp.dot(p.astype(BF16), dob_ref[pl.ds(r0, tq), :], preferred_element_type=F32)
                    dk_acc = dk_acc + jnp.dot(ds, qa, preferred_element_type=F32)
                    dq_parts.append(jnp.dot(kts[a], ds, preferred_element_type=F32))
                    out.append((dk_acc, dv_acc))
                if mla:
                    for a in range(2):
                        dqt_ref[cols[a], pl.ds(r0, tq)] += dq_parts[a]
                else:
                    dqt_ref[:, pl.ds(r0, tq)] += dq_parts[0] + dq_parts[1]
                return tuple(out)

            zero = jnp.zeros((tk, LANE), F32)
            last_near = jnp.minimum((c0 + tk - 1 + reach) // tq + 1, nq)
            c = lax.fori_loop(first, last_near, functools.partial(q_block, kind=kind_near), ((zero, zero), (zero, zero)))
            (dk0, dv0), (dk1, dv1) = lax.fori_loop(last_near, nq, functools.partial(q_block, kind=kind_far), c)
            if mla:
                dk_ref[pl.ds(c0, tk), cols[0]] = dk0
                dk_ref[pl.ds(c0, tk), cols[1]] = dk1
            else:
                dk_ref[pl.ds(c0, tk), :] = jnp.where(sels[0], dk0, dk1)
            dv_ref[pl.ds(c0, tk), :] = jnp.where(sels[0], dv0, dv1)
            return carry

        lax.fori_loop(0, s // tk, k_block, 0)

        def write_dq(j, carry):
            c0 = pl.multiple_of(j * tk, tk)
            for w in range(qw // LANE):
                dq_ref[pl.ds(c0, tk), w * LANE:(w + 1) * LANE] = dqt_ref[w * LANE:(w + 1) * LANE, pl.ds(c0, tk)].T
            return carry

        lax.fori_loop(0, s // tk, write_dq, 0)

        if ns:
            @pl.when(pl.program_id(0) == last_step)
            def _():
                _Scatter(*comm).finish()

    b0 = do_block0
    return pl.pallas_call(
        body, name=name, grid=(HEADS // 2,),
        in_specs=[pl.BlockSpec((s, qw), lambda h: (0, h)), pl.BlockSpec((s, qw), lambda h: (0, h)),
                  pl.BlockSpec((s, LANE), lambda h: (0, h)), pl.BlockSpec((s, LANE), lambda h: (0, h)),
                  pl.BlockSpec((s, LANE), lambda h: (0, h + b0)), pl.BlockSpec((2, 1, s), lambda h: (h, 0, 0))] + [ANY] * ns,
        out_specs=[pl.BlockSpec((s, qw), lambda h: (0, h)), pl.BlockSpec((s, qw), lambda h: (0, h)),
                   pl.BlockSpec((s, LANE), lambda h: (0, h))] + [ANY] * ns,
        out_shape=[jax.ShapeDtypeStruct(q.shape, F32), jax.ShapeDtypeStruct(k.shape, F32), jax.ShapeDtypeStruct((s, DIL_W), F32)]
        + _Scatter.out_shapes(scatter),
        scratch_shapes=[pltpu.VMEM((qw, s), BF16), pltpu.VMEM((LANE, s), BF16), pltpu.VMEM((s, LANE), BF16),
                        pltpu.VMEM((qw, s), F32), pltpu.VMEM((2, 1, s), F32), pltpu.VMEM((2, 1, s), F32)]
        + ([] if mla else [pltpu.VMEM((_near_offsets(tk, tq), tk, tq), F32)] * 2) + (_Scatter.semaphores(ns) if ns else []),
        compiler_params=_params(("arbitrary",) if ns else ("parallel",), 24 << 20),
    )(*_in_hbm(q, k, v, o, do, lse), *scatter)


def _ada_bwd(c_all, dmod_shard):
    n, d = c_all.shape
    cols = dmod_shard.shape[1]

    def body(c_ref, g_ref, o_ref):
        cv = c_ref[...]
        o_ref[...] = lax.dot_general(cv * _sigmoid(cv), g_ref[...], TN, precision=HIGHEST, preferred_element_type=F32)

    return pl.pallas_call(
        body, name="ada_bwd", out_shape=jax.ShapeDtypeStruct((d, cols), F32),
        compiler_params=_params(None, 16 << 20),
    )(c_all, dmod_shard)


SMALL_WIDTHS = (("g_mix_norm", D_MODEL), ("g_q_lat", Q_LORA), ("g_kv_lat", KV_LORA), ("g_mla_q_nope", NOPE),
                ("g_mla_q_pe", ROPE), ("g_mla_k_nope", NOPE), ("g_mla_k_pe", ROPE), ("g_dil_q", DIL_DIM),
                ("g_dil_k", DIL_DIM), ("g_ffn_norm", D_MODEL), ("b_conv", UP_W))


def _small_layout():
    pieces = (("dmod", 6 * D_MODEL),) + SMALL_WIDTHS + tuple(("w_conv%d" % k, UP_W) for k in range(3)) + (("loss", 1),)
    layout, off = {}, 0
    for name, width in pieces:
        layout[name] = (width, off)
        off += -(-width // LANE) * LANE
    return layout, off


def _pack_small(acc1, acc2, dg2, dglat, dgains, dbg, dbv, dwg, dwv, loss_part):
    layout, total = _small_layout()

    def body(a1, a2, g2, gl, gg, bg, bv, wg, wv, ls, o_ref):
        o_ref[...] = jnp.zeros_like(o_ref)

        def put(name, src, shift=0):
            start = layout[name][1] + shift
            o_ref[:, start:start + src.shape[1]] = src

        for k, src in enumerate((a1[0:1, :], a1[1:2, :], a2[3:4, :], a2[0:1, :], a2[1:2, :], g2[...])):
            put("dmod", src, k * D_MODEL)
        put("g_mix_norm", a1[2:3, :])
        put("g_q_lat", gl[0:1, :])
        put("g_kv_lat", gl[1:2, 0:KV_LORA])
        put("g_mla_q_nope", gg[0:1, 0:NOPE])
        put("g_mla_q_pe", gg[5:6, 0:ROPE])
        put("g_mla_k_nope", gg[1:2, 0:NOPE])
        put("g_mla_k_pe", gg[2:3, 0:ROPE])
        put("g_dil_q", gg[3:4, 0:DIL_DIM])
        put("g_dil_k", gg[4:5, 0:DIL_DIM])
        put("g_ffn_norm", a2[2:3, :])
        put("b_conv", bg[...])
        put("b_conv", bv[...], D_FF)
        for k in range(3):
            put("w_conv%d" % k, wg[k:k + 1, :])
            put("w_conv%d" % k, wv[k:k + 1, :], D_FF)
        put("loss", ls[...])

    ins = (acc1, acc2, dg2, dglat, dgains, dbg, dbv, dwg, dwv, loss_part)
    return pl.pallas_call(
        body, name="pack_small", grid=(1,), in_specs=[_full(a.shape) for a in ins], out_specs=_full((1, total)),
        out_shape=jax.ShapeDtypeStruct((1, total), F32),
        compiler_params=_params(("arbitrary",), 2 << 20),
    )(*_in_hbm(*ins))


def _sum_unpack(g):
    n_dev, _, total = g.shape
    layout, _ = _small_layout()

    def body(g_ref, *refs):
        o_refs, s_ref = refs[:-1], refs[-1]
        acc = g_ref[0]
        for k in range(1, n_dev):
            acc = acc + g_ref[k]
        s_ref[...] = acc
        take = lambda name: s_ref[:, layout[name][1]:layout[name][1] + layout[name][0]]
        o_refs[0][...] = take("dmod")
        for i, (name, _) in enumerate(SMALL_WIDTHS):
            o_refs[1 + i][...] = take(name)
        for k in range(3):
            o_refs[-2][k:k + 1, :] = take("w_conv%d" % k)
        o_refs[-1][...] = take("loss")

    shapes = [(1, 6 * D_MODEL)] + [(1, w) for _, w in SMALL_WIDTHS] + [(3, UP_W), (1, 1)]
    return pl.pallas_call(
        body, name="sum_unpack", out_shape=[jax.ShapeDtypeStruct(sh, F32) for sh in shapes],
        scratch_shapes=[pltpu.VMEM((1, total), F32)],
        compiler_params=_params(None, 4 << 20),
    )(g)


def _adamw_math(w, g, m, v):
    mn = ADAM_B1 * m + (1.0 - ADAM_B1) * g
    vn = ADAM_B2 * v + (1.0 - ADAM_B2) * (g * g)
    m_hat = mn / (1.0 - ADAM_B1 ** ADAM_STEP)
    v_hat = vn / (1.0 - ADAM_B2 ** ADAM_STEP)
    return -ADAM_LR * (m_hat / (jnp.sqrt(v_hat) + ADAM_EPS) + ADAM_WD * w), mn, vn


def _adamw_vectors(ws, gs, ms, vs):
    k = len(ws)

    def body(*refs):
        for i in range(k):
            d, mn, vn = _adamw_math(refs[i][...], refs[k + i][...], refs[2 * k + i][...], refs[3 * k + i][...])
            refs[4 * k + i][...] = d
            refs[5 * k + i][...] = mn
            refs[6 * k + i][...] = vn

    blocks = [_full(w.shape) for w in ws]
    outs = pl.pallas_call(
        body, name="adamw_vectors", grid=(1,), in_specs=blocks * 4, out_specs=blocks * 3,
        out_shape=[jax.ShapeDtypeStruct(w.shape, F32) for w in ws] * 3,
        compiler_params=_params(("arbitrary",), 2 << 20),
    )(*_in_hbm(*ws, *gs, *ms, *vs))
    return outs[:k], outs[k:2 * k], outs[2 * k:]


def _adamw(w, g, m, v, name):
    r, c = w.shape
    tr = r
    for cand in (256, 128, 64, 32, 16):
        if r % cand == 0 and r > cand:
            tr = cand
            break

    def body(w_ref, g_ref, m_ref, v_ref, d_ref, mo_ref, vo_ref):
        d_ref[...], mo_ref[...], vo_ref[...] = _adamw_math(w_ref[...], g_ref[...], m_ref[...], v_ref[...])

    blk = pl.BlockSpec((tr, c), lambda i: (i, 0))
    return pl.pallas_call(
        body, name=name, grid=(r // tr,), in_specs=[blk] * 4, out_specs=[blk] * 3,
        out_shape=[jax.ShapeDtypeStruct((r, c), F32)] * 3,
        compiler_params=_params(("parallel",), 7 * _nbytes((tr, c), F32)),
    )(w, g, m, v)


def _position():
    return lax.axis_index("x"), lax.axis_index("y"), lax.axis_index("c")


def _other_chips(x, y):
    return [(1 - x, y, 2 * (1 - x) + y), (x, 1 - y, 2 * x + (1 - y)), (1 - x, 1 - y, 2 * (1 - x) + (1 - y))]


class _SmallGather:
    def __init__(self, v_ref, out_ref, send_sems, recv_sems, local_sem):
        x, y, c = _position()
        me = 4 * x + 2 * y + c
        self.local = pltpu.make_async_copy(v_ref, out_ref.at[me], local_sem)
        self.sends, self.arrivals = [], []
        for k in range(N_DEV - 1):
            fx, fy, fc = ((k + 1) >> 2) & 1, ((k + 1) >> 1) & 1, (k + 1) & 1
            px, py, pc = (1 - x if fx else x), (1 - y if fy else y), (1 - c if fc else c)

            def copy(dst, k=k, peer=(px, py, pc)):
                return pltpu.make_async_remote_copy(src_ref=v_ref, dst_ref=dst, send_sem=send_sems.at[k],
                                                    recv_sem=recv_sems.at[k], device_id=peer, device_id_type=MESH)

            self.sends.append(copy(out_ref.at[me]))
            self.arrivals.append(copy(out_ref.at[4 * px + 2 * py + pc]))

    @staticmethod
    def semaphores():
        return [pltpu.SemaphoreType.DMA((N_DEV - 1,)), pltpu.SemaphoreType.DMA((N_DEV - 1,)), pltpu.SemaphoreType.DMA]

    def start(self):
        self.local.start()
        for cp in self.sends:
            cp.start()

    def finish(self):
        for cp in self.arrivals:
            cp.wait_recv()
        for cp in self.sends:
            cp.wait_send()
        self.local.wait()


def _prologue(c_taps, w_ada_shard, b_shard, pos_col, rope_consts, shards):
    n = len(shards)
    s = pos_col.shape[0]
    cols = w_ada_shard.shape[1]
    freq, csel, ssel = rope_consts

    def body(*refs):
        ct_ref, w_ref, b_ref, p_ref, f_ref, cs_ref, ss_ref = refs[:7]
        sh_refs = refs[7:7 + n]
        ct_all_ref, mod_all_ref, tab_ref = refs[7 + n:10 + n]
        g_refs = refs[10 + n:10 + 2 * n]
        mod_blk_ref = refs[10 + 2 * n]
        sems = refs[11 + 2 * n:]
        weights = _Gather(sh_refs, g_refs, *sems[6:])
        weights.start()
        first = _SmallGather(ct_ref, ct_all_ref, *sems[0:3])
        first.start()
        first.finish()
        cv = ct_all_ref[:, 0, 0:D_MODEL]
        sc = (cv * _sigmoid(cv)).astype(BF16)
        mod_blk_ref[...] = jnp.dot(sc, w_ref[...].astype(BF16), preferred_element_type=F32) + b_ref[...]
        second = _SmallGather(mod_blk_ref, mod_all_ref, *sems[3:6])
        second.start()

        def table_rows(i, carry):
            r0 = pl.multiple_of(i * ROW_TILE, ROW_TILE)
            ang = p_ref[pl.ds(r0, ROW_TILE), :].astype(F32) * f_ref[...]
            tab_ref[pl.ds(r0, ROW_TILE), :] = cs_ref[...] * jnp.cos(ang) + ss_ref[...] * jnp.sin(ang)
            return carry

        lax.fori_loop(0, s // ROW_TILE, table_rows, 0)
        second.finish()
        weights.forward()
        weights.finish()

    return pl.pallas_call(
        body, name="prologue",
        out_shape=[jax.ShapeDtypeStruct((N_DEV,) + c_taps.shape, F32), jax.ShapeDtypeStruct((N_DEV, N_DEV, cols), F32),
                   jax.ShapeDtypeStruct((s, 4 * LANE), F32)] + _Gather.out_shapes(shards),
        in_specs=[IN_VMEM] * 7 + [ANY] * n, out_specs=[IN_VMEM] * 3 + [ANY] * n,
        scratch_shapes=[pltpu.VMEM((N_DEV, cols), F32)] + _SmallGather.semaphores() * 2 + _Gather.scratch(shards),
        compiler_params=_params(None, 14 << 20),
    )(c_taps, w_ada_shard, b_shard, pos_col, freq, csel, ssel, *shards)


IN_VMEM = pl.BlockSpec(memory_space=pltpu.VMEM)
ANY = pl.BlockSpec(memory_space=pl.ANY)


class _Gather:
    def __init__(self, w_refs, out_refs, send_sems, recv_sems, own_sems, *bounce_refs):
        x, y, c = _position()
        q0 = 2 * x + y
        sibling = (x, y, 1 - c)
        self.ici, self.ici_in, self.fwd, self.fwd_in, self.own_in, self.own_out = [], [], [], [], [], []
        for k, (w_ref, out_ref) in enumerate(zip(w_refs, out_refs)):
            half = w_ref.shape[0] // 2
            self.own_in.append(pltpu.make_async_copy(w_ref, bounce_refs[k], own_sems.at[2 * k]))
            self.own_out.append(pltpu.make_async_copy(bounce_refs[k], out_ref.at[q0], own_sems.at[2 * k + 1]))

            def blk(q, e, out_ref=out_ref, half=half):
                return out_ref.at[q, pl.ds(pl.multiple_of(e * half, 16), half), :]

            def copy(src, dst, i, to):
                return pltpu.make_async_remote_copy(src_ref=src, dst_ref=dst, send_sem=send_sems.at[i], recv_sem=recv_sems.at[i],
                                                    device_id=to, device_id_type=MESH)

            src = w_ref.at[pl.ds(pl.multiple_of(c * half, 16), half), :]
            for j, (cx, cy, qj) in enumerate(_other_chips(x, y)):
                self.ici.append(copy(src, blk(q0, c), 6 * k + j, (cx, cy, c)))
                self.ici_in.append(copy(blk(qj, c), blk(qj, c), 6 * k + j, (cx, cy, c)))
                self.fwd.append(copy(blk(qj, c), blk(qj, c), 6 * k + 3 + j, sibling))
                self.fwd_in.append(copy(blk(qj, 1 - c), blk(qj, 1 - c), 6 * k + 3 + j, sibling))

    @staticmethod
    def out_shapes(shards):
        return [jax.ShapeDtypeStruct((N_CHIP,) + s.shape, s.dtype) for s in shards]

    @staticmethod
    def scratch(shards):
        n = len(shards)
        return ([pltpu.SemaphoreType.DMA((6 * n,)), pltpu.SemaphoreType.DMA((6 * n,)), pltpu.SemaphoreType.DMA((2 * n,))]
                + [pltpu.VMEM(s.shape, s.dtype) for s in shards])

    def start(self):
        for cp in self.ici + self.own_in:
            cp.start()

    def forward(self):
        for fetched, placed in zip(self.own_in, self.own_out):
            fetched.wait()
            placed.start()
        for arrived, onward in zip(self.ici_in, self.fwd):
            arrived.wait_recv()
            onward.start()

    def finish(self):
        for cp in self.fwd_in:
            cp.wait_recv()
        for cp in self.ici + self.fwd:
            cp.wait_send()
        for cp in self.own_out:
            cp.wait()


class _PairSwap:
    def __init__(self, g_refs, out_refs, send_sems, recv_sems):
        x, y, c = _position()
        self.copies = [
            pltpu.make_async_remote_copy(src_ref=g_ref.at[:, 1 - c], dst_ref=out_ref, send_sem=send_sems.at[k],
                                         recv_sem=recv_sems.at[k], device_id=(x, y, 1 - c), device_id_type=MESH)
            for k, (g_ref, out_ref) in enumerate(zip(g_refs, out_refs))]

    @staticmethod
    def out_shapes(grads):
        return [jax.ShapeDtypeStruct((N_CHIP,) + g.shape[2:], g.dtype) for g in grads]

    @staticmethod
    def semaphores(n):
        return [pltpu.SemaphoreType.DMA((n,)), pltpu.SemaphoreType.DMA((n,))]

    def start(self):
        for cp in self.copies:
            cp.start()

    def finish(self):
        for cp in self.copies:
            cp.wait_recv()
        for cp in self.copies:
            cp.wait_send()


def _pair_sum(g, a, c_idx, name):
    _, _, rh, cols = g.shape
    tr = rh
    for cand in (256, 128, 64, 32, 16):
        if rh % cand == 0 and rh > cand:
            tr = cand
            break

    def body(c_ref, g_ref, a_ref, o_ref):
        o_ref[...] = (g_ref[...] + a_ref[...]).astype(BF16)

    return pl.pallas_call(
        body, name=name,
        grid_spec=pltpu.PrefetchScalarGridSpec(
            num_scalar_prefetch=1, grid=(N_CHIP, rh // tr),
            in_specs=[pl.BlockSpec((None, None, tr, cols), lambda q, i, c_ref: (q, c_ref[0], i, 0)),
                      pl.BlockSpec((None, tr, cols), lambda q, i, c_ref: (q, i, 0))],
            out_specs=pl.BlockSpec((None, tr, cols), lambda q, i, c_ref: (q, i, 0))),
        out_shape=jax.ShapeDtypeStruct((N_CHIP, rh, cols), BF16),
        compiler_params=_params(("parallel", "parallel"), 10 * _nbytes((tr, cols), F32)),
    )(c_idx, g, a)


def _scatter_and_gather(parts, small, name):
    n = len(parts)

    def body(*refs):
        scatter = _Scatter(refs[:n], refs[n + 1:2 * n + 1], *refs[2 * n + 2:2 * n + 4])
        gather = _SmallGather(refs[n], refs[2 * n + 1], *refs[2 * n + 4:])
        scatter.start()
        gather.start()
        gather.finish()
        scatter.finish()

    return pl.pallas_call(
        body, name=name,
        out_shape=_Scatter.out_shapes(parts) + [jax.ShapeDtypeStruct((N_DEV,) + small.shape, F32)],
        in_specs=[ANY] * n + [IN_VMEM], out_specs=[ANY] * n + [IN_VMEM],
        scratch_shapes=_Scatter.semaphores(n) + _SmallGather.semaphores(),
        compiler_params=_params(None, 10 * _nbytes(small.shape, F32)),
    )(*parts, small)


class _Scatter:
    def __init__(self, p_refs, out_refs, send_sems, recv_sems):
        x, y, c = _position()
        self.copies = []
        for k, (p_ref, out_ref) in enumerate(zip(p_refs, out_refs)):
            for j, (cx, cy, qj) in enumerate(_other_chips(x, y)):
                self.copies.append(pltpu.make_async_remote_copy(
                    src_ref=p_ref.at[qj], dst_ref=out_ref.at[j], send_sem=send_sems.at[3 * k + j],
                    recv_sem=recv_sems.at[3 * k + j], device_id=(cx, cy, c), device_id_type=MESH))

    @staticmethod
    def out_shapes(parts):
        return [jax.ShapeDtypeStruct((3,) + p.shape[1:], p.dtype) for p in parts]

    @staticmethod
    def semaphores(n):
        return [pltpu.SemaphoreType.DMA((3 * n,)), pltpu.SemaphoreType.DMA((3 * n,))]

    def start(self):
        for cp in self.copies:
            cp.start()

    def finish(self):
        for cp in self.copies:
            cp.wait_recv()
        for cp in self.copies:
            cp.wait_send()


def _shard_sum(p, b, qc_idx, name):
    _, rh, cols = p.shape
    tr = rh
    for cand in (256, 128, 64, 32, 16):
        if rh % cand == 0 and rh > cand:
            tr = cand
            break

    def body(qc_ref, p_ref, b_ref, o_ref):
        acc = p_ref[...].astype(F32)
        for j in range(3):
            acc = acc + b_ref[j].astype(F32)
        o_ref[...] = acc

    return pl.pallas_call(
        body, name=name,
        grid_spec=pltpu.PrefetchScalarGridSpec(
            num_scalar_prefetch=1, grid=(rh // tr,),
            in_specs=[pl.BlockSpec((None, tr, cols), lambda i, qc_ref: (qc_ref[0], i, 0)),
                      pl.BlockSpec((3, tr, cols), lambda i, qc_ref: (0, i, 0))],
            out_specs=pl.BlockSpec((None, tr, cols), lambda i, qc_ref: (qc_ref[1], i, 0))),
        out_shape=jax.ShapeDtypeStruct((2, rh, cols), F32),
        compiler_params=_params(("parallel",), 8 * _nbytes((tr, cols), F32)),
    )(qc_idx, p, b)


def _join_halves(shards):
    n = len(shards)

    def body(*refs):
        out_refs = refs[n:2 * n]
        send_sems, recv_sems = refs[2 * n:]
        x, y, c = _position()
        cps = [pltpu.make_async_remote_copy(src_ref=out_refs[k].at[c], dst_ref=out_refs[k].at[c], send_sem=send_sems.at[k],
                                            recv_sem=recv_sems.at[k], device_id=(x, y, 1 - c), device_id_type=MESH)
               for k in range(n)]
        for cp in cps:
            cp.start()
        for k in range(n):
            arriving = out_refs[k].at[1 - c]
            pltpu.make_async_remote_copy(src_ref=arriving, dst_ref=arriving, send_sem=send_sems.at[k], recv_sem=recv_sems.at[k],
                                         device_id=(x, y, 1 - c), device_id_type=MESH).wait_recv()
        for cp in cps:
            cp.wait_send()

    return pl.pallas_call(
        body, name="rs_join",
        out_shape=[jax.ShapeDtypeStruct(a.shape, a.dtype) for a in shards],
        in_specs=[ANY] * n, out_specs=[ANY] * n, input_output_aliases={k: k for k in range(n)},
        scratch_shapes=[pltpu.SemaphoreType.DMA((n,)), pltpu.SemaphoreType.DMA((n,))],
    )(*shards)


def _cols_from_shards(g):
    q, r, cs = g.shape
    return jnp.transpose(g, (1, 0, 2)).reshape(r, q * cs)


def _cols_to_shards(w):
    r, cfull = w.shape
    return jnp.transpose(w.reshape(r, N_CHIP, cfull // N_CHIP), (1, 0, 2))


def _pad_w_in(w):
    z = lambda n: jnp.zeros((w.shape[0], n), w.dtype)
    q_lat, kv_lat, kpe = w[:, 0:512], w[:, 512:768], w[:, 768:800]
    qd, kd, vd = w[:, 800:1312], w[:, 1312:1824], w[:, 1824:2336]
    return jnp.concatenate([q_lat, qd, kd, vd, kv_lat, z(KPE_OFF), kpe, z(LANE - KPE_OFF - ROPE)], axis=1)


def _pad_w_qb(w):
    w3 = w.reshape(Q_LORA, HEADS, NOPE + ROPE)
    return jnp.pad(w3, ((0, 0), (0, 0), (0, LANE - NOPE - ROPE))).reshape(Q_LORA, HEADS * LANE)


def _unpad_w_qb(g):
    return g.reshape(Q_LORA, HEADS, LANE)[:, :, :NOPE + ROPE].reshape(Q_LORA, HEADS * (NOPE + ROPE))


def _pad_w_kvb(w):
    w3 = w.reshape(KV_LORA, HEADS, 2 * NOPE)
    kp = jnp.pad(w3[:, :, :NOPE], ((0, 0), (0, 0), (0, LANE - NOPE))).reshape(KV_LORA, HEADS * LANE)
    return jnp.concatenate([kp, w3[:, :, NOPE:].reshape(KV_LORA, DIL_W)], axis=1)


def _unpad_w_kvb(g):
    gk = g[:, :HEADS * LANE].reshape(KV_LORA, HEADS, LANE)[:, :, :NOPE]
    gv = g[:, HEADS * LANE:].reshape(KV_LORA, HEADS, NOPE)
    return jnp.concatenate([gk, gv], axis=2).reshape(KV_LORA, HEADS * 2 * NOPE)


def _head_gains(g_q_nope, g_q_pe, g_k_nope, g_k_pe, g_dq, g_dk):
    z = lambda n: jnp.zeros((1, n), F32)
    q1 = jnp.concatenate([g_q_nope, g_q_pe, z(LANE - NOPE - ROPE)], axis=1)
    k1 = jnp.concatenate([g_k_nope, z(LANE - NOPE)], axis=1)
    kpe = jnp.concatenate([z(KPE_OFF), g_k_pe, z(LANE - KPE_OFF - ROPE)], axis=1)
    return dict(q=jnp.tile(q1, (1, HEADS)), k=jnp.tile(k1, (1, HEADS)), kpe=kpe,
                dq=jnp.tile(g_dq, (1, HEADS)), dk=jnp.tile(g_dk, (1, HEADS)))


def kernel(x, c, positions, w_ada, b_ada, g_mix_norm, w_in, g_q_lat, w_q_b, g_kv_lat, w_kv_b, g_mla_q_nope, g_mla_q_pe, g_mla_k_nope, g_mla_k_pe, g_dil_q, g_dil_k, w_o, g_ffn_norm, w_up, w_conv, b_conv, w_down, loss_target, m_w_ada, m_b_ada, m_g_mix_norm, m_w_in, m_g_q_lat, m_w_q_b, m_g_kv_lat, m_w_kv_b, m_g_mla_q_nope, m_g_mla_q_pe, m_g_mla_k_nope, m_g_mla_k_pe, m_g_dil_q, m_g_dil_k, m_w_o, m_g_ffn_norm, m_w_up, m_w_conv, m_b_conv, m_w_down, v_w_ada, v_b_ada, v_g_mix_norm, v_w_in, v_g_q_lat, v_w_q_b, v_g_kv_lat, v_w_kv_b, v_g_mla_q_nope, v_g_mla_q_pe, v_g_mla_k_nope, v_g_mla_k_pe, v_g_dil_q, v_g_dil_k, v_w_o, v_g_ffn_norm, v_w_up, v_w_conv, v_b_conv, v_w_down):
    args = dict(locals())
    weights = {n: args[n][0] for n in ("w_ada", "w_in", "w_q_b", "w_kv_b", "w_o", "w_up", "w_conv", "w_down")}
    small_w = {n: args[n] for n in ("b_ada",) + tuple(n for n, _ in SMALL_WIDTHS)}
    mom_m = {n[2:]: (args[n][0] if args[n].ndim == 3 else args[n]) for n in args if n.startswith("m_")}
    mom_v = {n[2:]: (args[n][0] if args[n].ndim == 3 else args[n]) for n in args if n.startswith("v_")}

    xi, yi, ci = _position()
    q0 = 2 * xi + yi
    me = 4 * xi + 2 * yi + ci
    xs, tgt = x[0], loss_target[0]
    s = xs.shape[0]
    consts = _seg_consts()
    c_idx, qc_idx = jnp.reshape(ci, (1,)).astype(I32), jnp.stack([q0, ci]).astype(I32)

    def halves(g4):
        q, r, cc = g4.shape
        return g4.reshape(q, 2, r // 2, cc)

    own_first = [weights[n].astype(BF16) for n in ("w_in", "w_q_b", "w_kv_b")]
    own_later = [weights[n].astype(BF16) for n in ("w_o", "w_up", "w_down")]
    conv_cols = UP_W // N_CHIP
    ada_cols = w_ada.shape[2]
    b_shard = lax.dynamic_slice_in_dim(b_ada, q0 * ada_cols, ada_cols, axis=1)
    c_taps = jnp.concatenate([c, weights["w_conv"].reshape(1, 3 * conv_cols)], axis=1)
    c_taps_all, mod_all, tab, *gathered = _prologue(c_taps, weights["w_ada"], b_shard, positions.reshape(s, 1),
                                                    _rope_consts(), own_first)
    c_all = c_taps_all[:, 0, :D_MODEL]
    w_conv_f = c_taps_all[:, 0, D_MODEL:].reshape(N_CHIP, 2, 3, conv_cols)[:, 0]
    w_conv_f = jnp.transpose(w_conv_f, (1, 0, 2)).reshape(3, UP_W)
    mod_all = mod_all.reshape(N_CHIP, 2, N_DEV, ada_cols)
    mod = lax.dynamic_index_in_dim(lax.dynamic_index_in_dim(mod_all, ci, 1, False), me, 1, False)
    mod = mod.reshape(1, N_CHIP * ada_cols)
    sh1, sc1, g1, sh2, sc2, g2 = [mod[:, k * D_MODEL:(k + 1) * D_MODEL] for k in range(6)]
    w_in_f = _cols_from_shards(gathered[0])
    w_in_p = _pad_w_in(w_in_f)
    w_qb_p = _pad_w_qb(_cols_from_shards(gathered[1]))
    w_kvb_p = _pad_w_kvb(_cols_from_shards(gathered[2]))
    gains = _head_gains(g_mla_q_nope, g_mla_q_pe, g_mla_k_nope, g_mla_k_pe, g_dil_q, g_dil_k)

    h = _prenorm(xs, g_mix_norm, sc1, sh1, "prenorm")
    proj = _mm(h, w_in_p, "nn", F32, 512, P_COLS, "mm_in")
    ql, kvl = _latnorm(proj, g_q_lat, g_kv_lat)
    q_raw = _mm(ql, w_qb_p, "nn", F32, 512, HEADS * LANE, "mm_qb")
    kv_raw = _mm(kvl, w_kvb_p, "nn", F32, 512, HEADS * LANE + DIL_W, "mm_kvb")
    qm, km, vm, qd, kd, vd = _attn_prep(q_raw, kv_raw, proj, tab, gains, consts)
    scale_m, scale_d = (NOPE + ROPE) ** -0.5, DIL_DIM ** -0.5
    o_m, lse_m, got_up = _attn_fwd(qm, km, vm, True, scale_m, "attn_mla", gather=own_later[1:2])
    o_d, lse_d, got_o, got_down = _attn_fwd(qd, kd, vd, False, scale_d, "attn_dil", gather=[own_later[0], own_later[2]])
    gathered = [got_o, got_up, got_down]
    w_o_f = gathered[0].reshape(D_MODEL, D_MODEL)
    w_up_f = _cols_from_shards(gathered[1])
    w_down_f = gathered[2].reshape(D_FF, D_MODEL)
    mix_in = jnp.concatenate([o_m, o_d], axis=1)
    mix = _mm(mix_in, w_o_f, "nn", F32, 512, D_MODEL, "mm_o")
    x1, h2 = _resid_prenorm(xs, mix, g1, g_ffn_norm, sc2, sh2)
    up = _mm(h2, w_up_f, "nn", F32, 1024, CONV_TILE, "mm_up")
    act = _conv_gate(up, w_conv_f, b_conv)
    ffn = _mm(act, w_down_f, "nn", F32, 512, D_MODEL, "mm_down")
    dy, dffn, dg2, loss_part = _final(x1, ffn, tgt, g2)

    da = _mm(dffn, w_down_f, "nt", F32, 1024, CONV_TILE, "mm_down_dx")
    gw_down = _mm(act, dffn, "tn", F32, 256, D_MODEL, "mm_down_dw")
    dup_g, dup_v, dbg, dbv, dwg, dwv = _gate_bwd(up, da, w_conv_f, b_conv)
    dup = jnp.concatenate([dup_g, dup_v], axis=1)
    early_names = ("w_up", "w_down", "w_o")
    gw_up = _mm(h2, dup, "tn", F32, 1024, CONV_TILE, "mm_up_dw", col_shards=True)
    early = [halves(gw_up), halves(gw_down.reshape(N_CHIP, D_FF // N_CHIP, D_MODEL))]
    dh2, *early_sib = _mm(dup, w_up_f, "nt", F32, 256, 512, "mm_up_dx", swap=early, b_outer=True)
    dx1, dmix, acc2 = _ffnnorm_bwd(dh2, x1, dy, mix, g_ffn_norm, sc2, g1)
    gw_o = _mm(mix_in, dmix, "tn", F32, 512, D_MODEL, "mm_o_dw")
    early.append(halves(gw_o.reshape(N_CHIP, D_MODEL // N_CHIP, D_MODEL)))
    dmix_in, sib_o = _mm(dmix, w_o_f, "nt", F32, 512, D_MODEL, "mm_o_dx", swap=early[2:])
    early_sib.append(sib_o)
    early_sums = [_pair_sum(g, a, c_idx, "pair_sum_" + n) for g, a, n in zip(early, early_sib, early_names)]
    dqm, dkm, dvm, *early_recv = _attn_bwd(qm, km, vm, o_m, dmix_in, 0, lse_m, True, scale_m, "attn_mla_bwd",
                                           scatter=early_sums[:1])
    dqd, dkd, dvd, *early_recv_d = _attn_bwd(qd, kd, vd, o_d, dmix_in, DIL_W // LANE, lse_d, False, scale_d,
                                             "attn_dil_bwd", scatter=early_sums[1:])
    early_recv = early_recv + early_recv_d
    dq_raw, dkv_raw, dkpe_b, dqd_b, dkd_b, dvd_b, dgains = _attn_prep_bwd(
        dqm, dkm, dvm, dqd, dkd, dvd, q_raw, kv_raw, proj, tab, gains, consts)
    dql = _mm(dq_raw, w_qb_p, "nt", F32, 512, Q_LORA, "mm_qb_dx")
    gw_qb = _unpad_w_qb(_mm(ql, dq_raw, "tn", F32, Q_LORA, HEADS * LANE, "mm_qb_dw"))
    dkvl = _mm(dkv_raw, w_kvb_p, "nt", F32, 512, KV_LORA, "mm_kvb_dx")
    gw_kvb = _unpad_w_kvb(_mm(kvl, dkv_raw, "tn", F32, KV_LORA, HEADS * LANE + DIL_W, "mm_kvb_dw"))
    dqlat_b, dkvlat_b, dglat = _latnorm_bwd(dql, dkvl, proj, g_q_lat, g_kv_lat)
    dproj = jnp.concatenate([dqlat_b, dkvlat_b, dkpe_b[:, KPE_OFF:KPE_OFF + ROPE], dqd_b, dkd_b, dvd_b], axis=1)
    gw_in = _mm(h, dproj, "tn", F32, 512, IN_COLS, "mm_in_dw")
    late_names = ("w_in", "w_q_b", "w_kv_b")
    late = [halves(_cols_to_shards(gw_in)), halves(_cols_to_shards(gw_qb)), halves(_cols_to_shards(gw_kvb))]
    dh, *late_sib = _mm(dproj, w_in_f, "nt", F32, 512, D_MODEL, "mm_in_dx", swap=late)
    grad_x, acc1 = _mixnorm_bwd(dh, xs, dx1, g_mix_norm, sc1)

    packed = _pack_small(acc1, acc2, dg2, dglat, dgains, dbg, dbv, dwg, dwv, loss_part)
    late_sums = [_pair_sum(g, a, c_idx, "pair_sum_" + n) for g, a, n in zip(late, late_sib, late_names)]
    *late_recv, gathered_small = _scatter_and_gather(late_sums, packed, "rs_scatter_late")

    grad_b_ada, *small_grads, gconv_full, loss_sum = _sum_unpack(gathered_small)
    grads = {"b_ada": grad_b_ada}
    grads.update({n: g for (n, _), g in zip(SMALL_WIDTHS, small_grads)})
    shard_cols = UP_W // N_CHIP
    grads["w_conv"] = lax.dynamic_slice_in_dim(gconv_full, q0 * shard_cols, shard_cols, axis=1)
    dmod_all = gathered_small[:, 0, :6 * D_MODEL]
    grads["w_ada"] = _ada_bwd(c_all, lax.dynamic_slice_in_dim(dmod_all, q0 * ada_cols, ada_cols, axis=1))

    big_names = late_names + early_names
    half_sums = [_shard_sum(p, b, qc_idx, "shard_sum_" + n)
                 for p, b, n in zip(late_sums + early_sums, list(late_recv) + list(early_recv), big_names)]
    for n, full in zip(big_names, _join_halves(half_sums)):
        grads[n] = full.reshape(2 * full.shape[1], full.shape[2])

    delta, new_m, new_v = {}, {}, {}
    for n in ("w_ada", "w_in", "w_q_b", "w_kv_b", "w_o", "w_up", "w_conv", "w_down"):
        operands = (weights[n], grads[n], mom_m[n], mom_v[n])
        flipped = n in ("w_in", "w_q_b")
        if flipped:
            operands = [jnp.swapaxes(a, 0, 1) for a in operands]
            grads[n] = jnp.swapaxes(operands[1], 0, 1)
        if n == "w_ada":
            operands = _in_hbm(*operands)
        delta[n], new_m[n], new_v[n] = _adamw(*operands, "adamw_" + n)
        if flipped:
            delta[n], new_m[n], new_v[n] = (jnp.swapaxes(a, 0, 1) for a in (delta[n], new_m[n], new_v[n]))
    vec_names = ("b_ada",) + tuple(n for n, _ in SMALL_WIDTHS)
    sd, sm, sv = _adamw_vectors(*[[d_[n] for n in vec_names] for d_ in (small_w, grads, mom_m, mom_v)])
    for k, n in enumerate(vec_names):
        delta[n], new_m[n], new_v[n] = sd[k], sm[k], sv[k]

    loss = loss_sum[0, 0]
    order = ("w_ada", "b_ada", "g_mix_norm", "w_in", "g_q_lat", "w_q_b", "g_kv_lat", "w_kv_b", "g_mla_q_nope", "g_mla_q_pe",
             "g_mla_k_nope", "g_mla_k_pe", "g_dil_q", "g_dil_k", "w_o", "g_ffn_norm", "w_up", "w_conv", "b_conv", "w_down")
    lead = lambda n, z: z[None] if n.startswith("w_") else z
    outs = [loss, grad_x[None]]
    for d_ in (grads, delta, new_m, new_v):
        outs += [lead(n, d_[n]) for n in order]
    return tuple(outs)
```

```python
import functools

import numpy as np
import jax
import jax.numpy as jnp
from jax import lax
from jax.experimental import pallas as pl
from jax.experimental.pallas import tpu as pltpu

F32 = jnp.float32
BF16 = jnp.bfloat16
I32 = jnp.int32

D_MODEL = 1024
HEADS = 8
NOPE = 64
ROPE = 32
Q_LORA = 512
KV_LORA = 256
DIL_DIM = 64
DIL_W = HEADS * DIL_DIM
D_FF = 2816
UP_W = 2 * D_FF
IN_COLS = Q_LORA + KV_LORA + ROPE + 3 * DIL_W
ROPE_THETA = 10000.0
EPS = 1e-6
NEG_INF = -1e30
N_DEV = 8
N_CHIP = 4

ADAM_LR = 0.001
ADAM_B1 = 0.9
ADAM_B2 = 0.999
ADAM_EPS = 1e-08
ADAM_WD = 0.01
ADAM_STEP = 10

LANE = 128
ROW_TILE = 256
NORM_TILE = 512
ATT_TQ = 512
ATT_TK = 256
ATT_TK_BWD = 512
LOG2E = 1.4426950408889634
LN2 = 0.6931471805599453
VMEM_CAP = 56 * 1024 * 1024
VMEM_FLOOR = 32 * 1024 * 1024

P_QLAT, P_QD, P_KD, P_VD, P_KVLAT, P_KPE = 0, 512, 1024, 1536, 2048, 2304
P_COLS = 2432
KPE_OFF = 64

NN = (((1,), (0,)), ((), ()))
NT = (((1,), (1,)), ((), ()))
TN = (((0,), (0,)), ((), ()))
HIGHEST = lax.Precision.HIGHEST
MESH = pl.DeviceIdType.MESH


def _params(sem=None, est_bytes=0):
    limit = int(min(max(2 * est_bytes + (4 << 20), VMEM_FLOOR), VMEM_CAP))
    if sem is None:
        return pltpu.CompilerParams(vmem_limit_bytes=limit)
    return pltpu.CompilerParams(dimension_semantics=sem, vmem_limit_bytes=limit)


def _nbytes(shape, dtype):
    return int(np.prod(shape)) * jnp.dtype(dtype).itemsize


def _in_hbm(*xs):
    return [pltpu.with_memory_space_constraint(x, pltpu.HBM) for x in xs]


def _mm(a, b, dims, out_dtype, tm, tn, name, col_shards=False, swap=(), b_outer=False):
    def spec(block, index):
        if b_outer:
            return pl.BlockSpec(block, lambda g0, g1: index(g1, g0))
        return pl.BlockSpec(block, index)

    if dims == "nn":
        (m, k), (k2, n) = a.shape, b.shape
        a_spec = spec((tm, k), lambda i, j: (i, 0))
        b_spec = spec((k, tn), lambda i, j: (0, j))
        dn = NN
    elif dims == "nt":
        (m, k), (n, k2) = a.shape, b.shape
        a_spec = spec((tm, k), lambda i, j: (i, 0))
        b_spec = spec((tn, k), lambda i, j: (j, 0))
        dn = NT
    else:
        (k, m), (k2, n) = a.shape, b.shape
        a_spec = spec((k, tm), lambda i, j: (0, i))
        b_spec = spec((k, tn), lambda i, j: (0, j))
        dn = TN
    assert k == k2 and m % tm == 0 and n % tn == 0, (name, a.shape, b.shape, tm, tn)

    nw = len(swap)
    grid = (n // tn, m // tm) if b_outer else (m // tm, n // tn)

    def body(*refs):
        a_ref, b_ref, o_ref = refs[0], refs[1], refs[2 + nw]
        comm = (refs[2:2 + nw], refs[3 + nw:3 + 2 * nw]) + tuple(refs[3 + 2 * nw:])
        if nw:
            @pl.when((pl.program_id(0) == 0) & (pl.program_id(1) == 0))
            def _():
                _PairSwap(*comm).start()

        o_ref[...] = lax.dot_general(a_ref[...], b_ref[...], dn, preferred_element_type=F32).astype(o_ref.dtype)

        if nw:
            @pl.when((pl.program_id(0) == grid[0] - 1) & (pl.program_id(1) == grid[1] - 1))
            def _():
                _PairSwap(*comm).finish()

    est = _nbytes((tm, k), a.dtype) + _nbytes((tn, k), b.dtype) + _nbytes((tm, tn), F32) + _nbytes((tm, tn), out_dtype)
    if col_shards:
        out_spec = spec((None, tm, tn), lambda i, j: (j, i, 0))
        out_shape = jax.ShapeDtypeStruct((n // tn, m, tn), out_dtype)
    else:
        out_spec = spec((tm, tn), lambda i, j: (i, j))
        out_shape = jax.ShapeDtypeStruct((m, n), out_dtype)
    out = pl.pallas_call(
        body, name=name, grid=grid,
        in_specs=[a_spec, b_spec] + [ANY] * nw,
        out_specs=[out_spec] + [ANY] * nw,
        out_shape=[out_shape] + _PairSwap.out_shapes(swap),
        scratch_shapes=_PairSwap.semaphores(nw) if nw else [],
        compiler_params=_params(("arbitrary", "arbitrary") if nw else ("parallel", "parallel"), est),
    )(a, b, *swap)
    return out if nw else out[0]


def _seg_consts():
    seg_q = np.zeros((HEADS * LANE, LANE), np.float32)
    inv_q = np.zeros((1, LANE), np.float32)
    seg_k = np.zeros((HEADS * LANE, LANE), np.float32)
    inv_k = np.zeros((1, LANE), np.float32)
    seg_d = np.zeros((DIL_W, LANE), np.float32)
    inv_d = np.zeros((1, LANE), np.float32)
    for h in range(HEADS):
        seg_q[h * LANE:h * LANE + NOPE, 2 * h] = 1.0
        seg_q[h * LANE + NOPE:h * LANE + NOPE + ROPE, 2 * h + 1] = 1.0
        inv_q[0, 2 * h], inv_q[0, 2 * h + 1] = 1.0 / NOPE, 1.0 / ROPE
        seg_k[h * LANE:h * LANE + NOPE, h] = 1.0
        inv_k[0, h] = 1.0 / NOPE
        seg_d[h * DIL_DIM:(h + 1) * DIL_DIM, h] = 1.0
        inv_d[0, h] = 1.0 / DIL_DIM
    fold_q = np.tile(np.eye(LANE, dtype=np.float32), (HEADS, 1))
    fold_d = np.zeros((DIL_W, LANE), np.float32)
    fold_d[np.arange(DIL_W), np.arange(DIL_W) % DIL_DIM] = 1.0
    j = lambda v: jnp.asarray(v)
    b = lambda v: jnp.asarray(v, dtype=BF16)
    return dict(seg_q=b(seg_q), exp_q=b(seg_q.T.copy()), inv_q=j(inv_q), seg_k=b(seg_k), exp_k=b(seg_k.T.copy()),
                inv_k=j(inv_k), seg_d=b(seg_d), exp_d=b(seg_d.T.copy()), inv_d=j(inv_d), fold_q=j(fold_q), fold_d=j(fold_d))


def _rope_consts():
    inv_d = jnp.power(ROPE_THETA, -2.0 * jnp.arange(DIL_DIM // 2, dtype=F32) / DIL_DIM)
    inv_q = jnp.power(ROPE_THETA, -2.0 * jnp.arange(ROPE // 2, dtype=F32) / ROPE)
    lanes = np.arange(LANE)
    freq_d = inv_d[lanes % (DIL_DIM // 2)]
    in_pe = (lanes >= KPE_OFF) & (lanes < KPE_OFF + ROPE)
    freq_q = jnp.where(jnp.asarray(in_pe), inv_q[(lanes - KPE_OFF) % (ROPE // 2)], 0.0)
    sign_d = np.where(lanes % DIL_DIM < DIL_DIM // 2, -1.0, 1.0).astype(np.float32)
    sign_q = np.where(in_pe, np.where((lanes - KPE_OFF) < ROPE // 2, -1.0, 1.0), 0.0).astype(np.float32)
    zeros, ones = np.zeros(LANE, np.float32), np.ones(LANE, np.float32)
    freq = jnp.concatenate([freq_d, freq_d, freq_q, freq_q])[None, :]
    csel = jnp.asarray(np.concatenate([ones, zeros, ones, zeros]))[None, :]
    ssel = jnp.asarray(np.concatenate([zeros, sign_d, zeros, sign_q]))[None, :]
    return freq, csel, ssel


def _full(shape):
    return pl.BlockSpec(shape, lambda *_: (0,) * len(shape))


def _tile_lanes(x, n):
    return jnp.concatenate([x] * n, axis=1)


def _rms(x):
    return lax.rsqrt(jnp.mean(x * x, axis=-1, keepdims=True) + EPS)


def _prenorm(x, gain, scale, shift, name):
    s, d = x.shape

    def body(x_ref, g_ref, sc_ref, sh_ref, h_ref):
        xv = x_ref[...]
        h = (xv * _rms(xv)) * g_ref[...] * (1.0 + sc_ref[...]) + sh_ref[...]
        h_ref[...] = h.astype(BF16)

    row = pl.BlockSpec((NORM_TILE, d), lambda i: (i, 0))
    return pl.pallas_call(
        body, name=name, grid=(s // NORM_TILE,),
        in_specs=[row, _full((1, d)), _full((1, d)), _full((1, d))],
        out_specs=row, out_shape=jax.ShapeDtypeStruct((s, d), BF16),
        compiler_params=_params(("parallel",)),
    )(x, gain, scale, shift)


def _latnorm(proj, g_q, g_kv):
    s = proj.shape[0]

    def body(q_ref, kv_ref, gq_ref, gkv_ref, ql_ref, kvl_ref):
        q, kv = q_ref[...], kv_ref[...]
        ql_ref[...] = ((q * _rms(q)) * gq_ref[...]).astype(BF16)
        kvl_ref[...] = ((kv * _rms(kv)) * gkv_ref[...]).astype(BF16)

    return pl.pallas_call(
        body, name="latnorm", grid=(s // NORM_TILE,),
        in_specs=[pl.BlockSpec((NORM_TILE, Q_LORA), lambda i: (i, P_QLAT // Q_LORA)),
                  pl.BlockSpec((NORM_TILE, KV_LORA), lambda i: (i, P_KVLAT // KV_LORA)),
                  _full((1, Q_LORA)), _full((1, KV_LORA))],
        out_specs=[pl.BlockSpec((NORM_TILE, Q_LORA), lambda i: (i, 0)), pl.BlockSpec((NORM_TILE, KV_LORA), lambda i: (i, 0))],
        out_shape=[jax.ShapeDtypeStruct((s, Q_LORA), BF16), jax.ShapeDtypeStruct((s, KV_LORA), BF16)],
        compiler_params=_params(("parallel",)),
    )(proj, proj, g_q, g_kv)


def _dot01(v, mat01):
    hi = v.astype(BF16)
    lo = (v - hi.astype(F32)).astype(BF16)
    return jnp.dot(hi, mat01, preferred_element_type=F32) + jnp.dot(lo, mat01, preferred_element_type=F32)


def _seg_rinv(x, seg, exp, inv):
    r = lax.rsqrt(_dot01(x * x, seg) * inv + EPS)
    return _dot01(r, exp)


def _seg_mean(v, seg, exp, inv):
    return _dot01(_dot01(v, seg) * inv, exp)


def _swap_halves(x, half):
    n = x.shape[1]
    lane = lax.broadcasted_iota(I32, (1, n), 1)
    first = (lane & (2 * half - 1)) < half
    return jnp.where(first, pltpu.roll(x, n - half, 1), pltpu.roll(x, half, 1))


def _rope(x, cos, sin_signed, half):
    return x * cos + _swap_halves(x, half) * sin_signed


def _rope_bwd(dy, cos, sin_signed, half):
    return dy * cos + _swap_halves(dy * sin_signed, half)


def _pe_lane_mask(n):
    lane = lax.broadcasted_iota(I32, (1, n), 1) & (LANE - 1)
    return (lane >= KPE_OFF) & (lane < KPE_OFF + ROPE)


def _attn_prep(q_raw, kv_raw, proj, tab, gains, consts):
    s = q_raw.shape[0]
    hw = HEADS * LANE

    def body(q_ref, kv_ref, kpe_ref, qd_ref, kd_ref, vd_ref, tab_ref,
             gq_ref, gk_ref, gkpe_ref, gdq_ref, gdk_ref,
             segq_ref, expq_ref, invq_ref, segk_ref, expk_ref, invk_ref, segd_ref, expd_ref, invd_ref,
             qm_ref, km_ref, vm_ref, qdo_ref, kdo_ref, vdo_ref):
        tab_v = tab_ref[...]
        cos_d, sin_d = _tile_lanes(tab_v[:, 0:LANE], DIL_W // LANE), _tile_lanes(tab_v[:, LANE:2 * LANE], DIL_W // LANE)
        cos_q1, sin_q1 = tab_v[:, 2 * LANE:3 * LANE], tab_v[:, 3 * LANE:4 * LANE]
        cos_q, sin_q = _tile_lanes(cos_q1, HEADS), _tile_lanes(sin_q1, HEADS)

        q = q_ref[...]
        qn = q * _seg_rinv(q, segq_ref[...], expq_ref[...], invq_ref[...]) * gq_ref[...]
        qm_ref[...] = _rope(qn, cos_q, sin_q, ROPE // 2).astype(BF16)

        kv = kv_ref[...]
        kp = kv[:, :hw]
        kn = kp * _seg_rinv(kp, segk_ref[...], expk_ref[...], invk_ref[...]) * gk_ref[...]
        kpe = kpe_ref[...]
        r_pe = lax.rsqrt(jnp.sum(kpe * kpe, axis=-1, keepdims=True) * (1.0 / ROPE) + EPS)
        kpe_r = _rope(kpe * r_pe * gkpe_ref[...], cos_q1, sin_q1, ROPE // 2)
        km_ref[...] = (kn + _tile_lanes(kpe_r, HEADS)).astype(BF16)
        vm_ref[...] = kv[:, hw:].astype(BF16)

        qd = qd_ref[...]
        qdn = qd * _seg_rinv(qd, segd_ref[...], expd_ref[...], invd_ref[...]) * gdq_ref[...]
        qdo_ref[...] = _rope(qdn, cos_d, sin_d, DIL_DIM // 2).astype(BF16)
        kd = kd_ref[...]
        kdn = kd * _seg_rinv(kd, segd_ref[...], expd_ref[...], invd_ref[...]) * gdk_ref[...]
        kdo_ref[...] = _rope(kdn, cos_d, sin_d, DIL_DIM // 2).astype(BF16)
        vdo_ref[...] = vd_ref[...].astype(BF16)

    t = ROW_TILE
    row = lambda w, cb=0: pl.BlockSpec((t, w), lambda i: (i, cb))
    c = consts
    return pl.pallas_call(
        body, name="attn_prep", grid=(s // t,),
        in_specs=[row(hw), row(hw + DIL_W), row(LANE, P_KPE // LANE), row(DIL_W, P_QD // DIL_W), row(DIL_W, P_KD // DIL_W),
                  row(DIL_W, P_VD // DIL_W), row(4 * LANE),
                  _full((1, hw)), _full((1, hw)), _full((1, LANE)), _full((1, DIL_W)), _full((1, DIL_W)),
                  _full((hw, LANE)), _full((LANE, hw)), _full((1, LANE)), _full((hw, LANE)), _full((LANE, hw)), _full((1, LANE)),
                  _full((DIL_W, LANE)), _full((LANE, DIL_W)), _full((1, LANE))],
        out_specs=[row(hw), row(hw), row(DIL_W), row(DIL_W), row(DIL_W), row(DIL_W)],
        out_shape=[jax.ShapeDtypeStruct((s, hw), BF16), jax.ShapeDtypeStruct((s, hw), BF16)]
        + [jax.ShapeDtypeStruct((s, DIL_W), BF16)] * 4,
        compiler_params=_params(("parallel",), 24 << 20),
    )(*_in_hbm(q_raw, kv_raw, proj, proj, proj, proj), tab, gains["q"], gains["k"], gains["kpe"], gains["dq"], gains["dk"],
      c["seg_q"], c["exp_q"], c["inv_q"], c["seg_k"], c["exp_k"], c["inv_k"], c["seg_d"], c["exp_d"], c["inv_d"])


def _attn_prep_bwd(dqm, dkm, dvm, dqd, dkd, dvd, q_raw, kv_raw, proj, tab, gains, consts):
    s = q_raw.shape[0]
    hw = HEADS * LANE
    n_steps = s // ROW_TILE

    def body(dqm_ref, dkm_ref, dvm_ref, dqd_ref, dkd_ref, dvd_ref, q_ref, kv_ref, kpe_ref, qd_ref, kd_ref, tab_ref,
             gq_ref, gk_ref, gkpe_ref, gdq_ref, gdk_ref,
             segq_ref, expq_ref, invq_ref, segk_ref, expk_ref, invk_ref, segd_ref, expd_ref, invd_ref, foldq_ref, foldd_ref,
             dq_ref, dkv_ref, dkpe_ref, dqdo_ref, dkdo_ref, dvdo_ref, dg_ref, acc_ref):
        i = pl.program_id(0)

        @pl.when(i == 0)
        def _():
            acc_ref[...] = jnp.zeros_like(acc_ref)

        tab_v = tab_ref[...]
        cos_d, sin_d = _tile_lanes(tab_v[:, 0:LANE], DIL_W // LANE), _tile_lanes(tab_v[:, LANE:2 * LANE], DIL_W // LANE)
        cos_q1, sin_q1 = tab_v[:, 2 * LANE:3 * LANE], tab_v[:, 3 * LANE:4 * LANE]
        cos_q, sin_q = _tile_lanes(cos_q1, HEADS), _tile_lanes(sin_q1, HEADS)

        def norm_bwd(x, dyg, gain, seg, exp, inv):
            rinv = _seg_rinv(x, seg, exp, inv)
            xn = x * rinv
            dxn = dyg * gain
            dx = rinv * (dxn - xn * _seg_mean(dxn * xn, seg, exp, inv))
            return dx, jnp.sum(dyg * xn, axis=0, keepdims=True)

        dq, gq_l = norm_bwd(q_ref[...], _rope_bwd(dqm_ref[...], cos_q, sin_q, ROPE // 2), gq_ref[...],
                            segq_ref[...], expq_ref[...], invq_ref[...])
        dq_ref[...] = dq.astype(BF16)

        dkm = dkm_ref[...]
        kv = kv_ref[...]
        dkp, gk_l = norm_bwd(kv[:, :hw], dkm, gk_ref[...], segk_ref[...], expk_ref[...], invk_ref[...])
        dkv_ref[:, :hw] = dkp.astype(BF16)
        dkv_ref[:, hw:] = dvm_ref[...].astype(BF16)

        dkpe_r = dkm[:, 0:LANE]
        for h in range(1, HEADS):
            dkpe_r = dkpe_r + dkm[:, h * LANE:(h + 1) * LANE]
        dkpe_r = jnp.where(_pe_lane_mask(LANE), dkpe_r, 0.0)
        dyg = _rope_bwd(dkpe_r, cos_q1, sin_q1, ROPE // 2)
        kpe = kpe_ref[...]
        r_pe = lax.rsqrt(jnp.sum(kpe * kpe, axis=-1, keepdims=True) * (1.0 / ROPE) + EPS)
        xn = kpe * r_pe
        dxn = dyg * gkpe_ref[...]
        dkpe = r_pe * (dxn - xn * (jnp.sum(dxn * xn, axis=-1, keepdims=True) * (1.0 / ROPE)))
        dkpe_ref[...] = dkpe.astype(BF16)
        gkpe_l = jnp.sum(dyg * xn, axis=0, keepdims=True)

        dqd_v, gdq_l = norm_bwd(qd_ref[...], _rope_bwd(dqd_ref[...], cos_d, sin_d, DIL_DIM // 2), gdq_ref[...],
                                segd_ref[...], expd_ref[...], invd_ref[...])
        dqdo_ref[...] = dqd_v.astype(BF16)
        dkd_v, gdk_l = norm_bwd(kd_ref[...], _rope_bwd(dkd_ref[...], cos_d, sin_d, DIL_DIM // 2), gdk_ref[...],
                                segd_ref[...], expd_ref[...], invd_ref[...])
        dkdo_ref[...] = dkd_v.astype(BF16)
        dvdo_ref[...] = dvd_ref[...].astype(BF16)

        acc_ref[0:1, :] += gq_l
        acc_ref[1:2, :] += gk_l
        acc_ref[2:3, 0:LANE] += gkpe_l
        acc_ref[3:4, 0:DIL_W] += gdq_l
        acc_ref[4:5, 0:DIL_W] += gdk_l

        @pl.when(i == n_steps - 1)
        def _():
            acc = acc_ref[...]
            fq = jnp.dot(acc, foldq_ref[...], precision=HIGHEST, preferred_element_type=F32)
            fd = jnp.dot(acc[:, 0:DIL_W], foldd_ref[...], precision=HIGHEST, preferred_element_type=F32)
            rows = lax.broadcasted_iota(I32, (8, LANE), 0)
            base = jnp.where(rows < 2, fq, jnp.where(rows == 2, acc[:, 0:LANE], fd))
            at0 = pltpu.roll(base, LANE - KPE_OFF, 1)
            dg_ref[...] = jnp.where(rows == 5, pltpu.roll(at0, 5, 0), jnp.where(rows == 2, at0, base))

    t = ROW_TILE
    row = lambda w, cb=0: pl.BlockSpec((t, w), lambda i: (i, cb))
    c = consts
    return pl.pallas_call(
        body, name="attn_prep_bwd", grid=(n_steps,),
        in_specs=[row(hw), row(hw), row(DIL_W), row(DIL_W), row(DIL_W), row(DIL_W),
                  row(hw), row(hw + DIL_W), row(LANE, P_KPE // LANE), row(DIL_W, P_QD // DIL_W), row(DIL_W, P_KD // DIL_W),
                  row(4 * LANE),
                  _full((1, hw)), _full((1, hw)), _full((1, LANE)), _full((1, DIL_W)), _full((1, DIL_W)),
                  _full((hw, LANE)), _full((LANE, hw)), _full((1, LANE)), _full((hw, LANE)), _full((LANE, hw)), _full((1, LANE)),
                  _full((DIL_W, LANE)), _full((LANE, DIL_W)), _full((1, LANE)), _full((hw, LANE)), _full((DIL_W, LANE))],
        out_specs=[row(hw), row(hw + DIL_W), row(LANE), row(DIL_W), row(DIL_W), row(DIL_W), _full((8, LANE))],
        out_shape=[jax.ShapeDtypeStruct((s, hw), BF16), jax.ShapeDtypeStruct((s, hw + DIL_W), BF16),
                   jax.ShapeDtypeStruct((s, LANE), BF16)] + [jax.ShapeDtypeStruct((s, DIL_W), BF16)] * 3
        + [jax.ShapeDtypeStruct((8, LANE), F32)],
        scratch_shapes=[pltpu.VMEM((8, hw), F32)],
        compiler_params=_params(("arbitrary",), 28 << 20),
    )(*_in_hbm(dqm, dkm, dvm, dqd, dkd, dvd, q_raw, kv_raw, proj, proj, proj), tab,
      gains["q"], gains["k"], gains["kpe"], gains["dq"], gains["dk"],
      c["seg_q"], c["exp_q"], c["inv_q"], c["seg_k"], c["exp_k"], c["inv_k"], c["seg_d"], c["exp_d"], c["inv_d"],
      c["fold_q"], c["fold_d"])


def _latnorm_bwd(dql, dkvl, proj, g_q, g_kv):
    s = proj.shape[0]
    n_steps = s // NORM_TILE

    def body(dql_ref, dkvl_ref, q_ref, kv_ref, gq_ref, gkv_ref, dq_ref, dkv_ref, dg_ref):
        i = pl.program_id(0)

        @pl.when(i == 0)
        def _():
            dg_ref[...] = jnp.zeros_like(dg_ref)

        def one(x, dyg, gain):
            r = _rms(x)
            xn = x * r
            dxn = dyg * gain
            dx = r * (dxn - xn * jnp.mean(dxn * xn, axis=-1, keepdims=True))
            return dx, jnp.sum(dyg * xn, axis=0, keepdims=True)

        dq, gq_l = one(q_ref[...], dql_ref[...], gq_ref[...])
        dkv, gkv_l = one(kv_ref[...], dkvl_ref[...], gkv_ref[...])
        dq_ref[...] = dq.astype(BF16)
        dkv_ref[...] = dkv.astype(BF16)
        dg_ref[0:1, :] += gq_l
        dg_ref[1:2, 0:KV_LORA] += gkv_l

    t = NORM_TILE
    return pl.pallas_call(
        body, name="latnorm_bwd", grid=(n_steps,),
        in_specs=[pl.BlockSpec((t, Q_LORA), lambda i: (i, 0)), pl.BlockSpec((t, KV_LORA), lambda i: (i, 0)),
                  pl.BlockSpec((t, Q_LORA), lambda i: (i, P_QLAT // Q_LORA)),
                  pl.BlockSpec((t, KV_LORA), lambda i: (i, P_KVLAT // KV_LORA)),
                  _full((1, Q_LORA)), _full((1, KV_LORA))],
        out_specs=[pl.BlockSpec((t, Q_LORA), lambda i: (i, 0)), pl.BlockSpec((t, KV_LORA), lambda i: (i, 0)), _full((8, Q_LORA))],
        out_shape=[jax.ShapeDtypeStruct((s, Q_LORA), BF16), jax.ShapeDtypeStruct((s, KV_LORA), BF16),
                   jax.ShapeDtypeStruct((8, Q_LORA), F32)],
        compiler_params=_params(("arbitrary",)),
    )(dql, dkvl, proj, proj, g_q, g_kv)


def _resid_prenorm(x, mix, g1, gain, scale, shift):
    s, d = x.shape

    def body(x_ref, mix_ref, g1_ref, g_ref, sc_ref, sh_ref, x1_ref, h_ref):
        x1 = x_ref[...] + g1_ref[...] * mix_ref[...]
        x1_ref[...] = x1
        h_ref[...] = ((x1 * _rms(x1)) * g_ref[...] * (1.0 + sc_ref[...]) + sh_ref[...]).astype(BF16)

    row = pl.BlockSpec((NORM_TILE, d), lambda i: (i, 0))
    vec = _full((1, d))
    return pl.pallas_call(
        body, name="resid_prenorm", grid=(s // NORM_TILE,),
        in_specs=[row, row, vec, vec, vec, vec], out_specs=[row, row],
        out_shape=[jax.ShapeDtypeStruct((s, d), F32), jax.ShapeDtypeStruct((s, d), BF16)],
        compiler_params=_params(("parallel",)),
    )(x, mix, g1, gain, scale, shift)


CONV_TILE = 1408
HALO = 8


def _shift_down(x, halo, k):
    t = x.shape[0]
    row = lax.broadcasted_iota(I32, (t, 1), 0)
    out = pltpu.roll(x, k, 0)
    for r in range(k):
        out = jnp.where(row == r, halo[HALO - k + r:HALO - k + r + 1, :], out)
    return out


def _shift_up(x, halo, k):
    t = x.shape[0]
    row = lax.broadcasted_iota(I32, (t, 1), 0)
    out = pltpu.roll(x, t - k, 0)
    for r in range(k):
        out = jnp.where(row == t - k + r, halo[r:r + 1, :], out)
    return out


def _conv_fwd(x, halo, w, b):
    p1, p2 = _shift_down(x, halo, 1), _shift_down(x, halo, 2)
    u = b + p2 * w[0:1, :]
    u = u + p1 * w[1:2, :]
    u = u + x * w[2:3, :]
    return u, p1, p2


def _sigmoid(x):
    return 0.5 * jnp.tanh(0.5 * x) + 0.5


def _conv_gate(up, w_conv, b_conv):
    s = up.shape[0]
    t = ROW_TILE
    nj = D_FF // CONV_TILE
    hb = t // HALO

    def body(g_ref, v_ref, gh_ref, vh_ref, wg_ref, wv_ref, bg_ref, bv_ref, a_ref):
        live = (pl.program_id(0) > 0).astype(F32)
        ug, _, _ = _conv_fwd(g_ref[...], gh_ref[...] * live, wg_ref[...], bg_ref[...])
        uv, _, _ = _conv_fwd(v_ref[...], vh_ref[...] * live, wv_ref[...], bv_ref[...])
        a_ref[...] = (ug * _sigmoid(ug) * uv).astype(BF16)

    main = lambda off: pl.BlockSpec((t, CONV_TILE), lambda i, j: (i, j + off))
    halo = lambda off: pl.BlockSpec((HALO, CONV_TILE), lambda i, j: (jnp.maximum(i * hb - 1, 0), j + off))
    wsp = lambda off: pl.BlockSpec((3, CONV_TILE), lambda i, j: (0, j + off))
    bsp = lambda off: pl.BlockSpec((1, CONV_TILE), lambda i, j: (0, j + off))
    return pl.pallas_call(
        body, name="conv_gate", grid=(s // t, nj),
        in_specs=[main(0), main(nj), halo(0), halo(nj), wsp(0), wsp(nj), bsp(0), bsp(nj)],
        out_specs=pl.BlockSpec((t, CONV_TILE), lambda i, j: (i, j)),
        out_shape=jax.ShapeDtypeStruct((s, D_FF), BF16),
        compiler_params=_params(("parallel", "parallel"), 12 << 20),
    )(up, up, up, up, w_conv, w_conv, b_conv, b_conv)


def _gate_bwd(up, da, w_conv, b_conv):
    s = up.shape[0]
    t = ROW_TILE
    nj = D_FF // CONV_TILE
    hb = t // HALO
    n_i = s // t

    def body(g_ref, v_ref, gh_ref, vh_ref, gn_ref, vn_ref, da_ref, dan_ref, wg_ref, wv_ref, bg_ref, bv_ref,
             dupg_ref, dupv_ref, dbg_ref, dbv_ref, dwg_ref, dwv_ref):
        i = pl.program_id(1)

        @pl.when(i == 0)
        def _():
            for r in (dbg_ref, dbv_ref, dwg_ref, dwv_ref):
                r[...] = jnp.zeros_like(r)

        def d_gate(ug, uv, da_v):
            sg = _sigmoid(ug)
            return da_v * uv * (sg * (1.0 + ug * (1.0 - sg))), da_v * (ug * sg)

        live = (i > 0).astype(F32)
        xg, xv = g_ref[...], v_ref[...]
        wg, wv = wg_ref[...], wv_ref[...]
        ug, g1, g2 = _conv_fwd(xg, gh_ref[...] * live, wg, bg_ref[...])
        uv, v1, v2 = _conv_fwd(xv, vh_ref[...] * live, wv, bv_ref[...])
        dug, duv = d_gate(ug, uv, da_ref[...])

        more = (i < n_i - 1).astype(F32)
        ug_n, _, _ = _conv_fwd(gn_ref[...], xg[t - HALO:, :], wg, bg_ref[...])
        uv_n, _, _ = _conv_fwd(vn_ref[...], xv[t - HALO:, :], wv, bv_ref[...])
        dug_n, duv_n = d_gate(ug_n, uv_n, dan_ref[...] * more)

        def conv_t(du, du_n, w):
            return du * w[2:3, :] + _shift_up(du, du_n, 1) * w[1:2, :] + _shift_up(du, du_n, 2) * w[0:1, :]

        dupg_ref[...] = conv_t(dug, dug_n, wg).astype(BF16)
        dupv_ref[...] = conv_t(duv, duv_n, wv).astype(BF16)
        csum = lambda z: jnp.sum(z, axis=0, keepdims=True)
        dbg_ref[...] += csum(dug)
        dbv_ref[...] += csum(duv)
        dwg_ref[0:1, :] += csum(dug * g2)
        dwg_ref[1:2, :] += csum(dug * g1)
        dwg_ref[2:3, :] += csum(dug * xg)
        dwv_ref[0:1, :] += csum(duv * v2)
        dwv_ref[1:2, :] += csum(duv * v1)
        dwv_ref[2:3, :] += csum(duv * xv)

    last_halo = s // HALO - 1
    main = lambda off: pl.BlockSpec((t, CONV_TILE), lambda j, i: (i, j + off))
    halo = lambda off: pl.BlockSpec((HALO, CONV_TILE), lambda j, i: (jnp.maximum(i * hb - 1, 0), j + off))
    nxt = lambda off: pl.BlockSpec((HALO, CONV_TILE), lambda j, i: (jnp.minimum((i + 1) * hb, last_halo), j + off))
    wsp = lambda off: pl.BlockSpec((3, CONV_TILE), lambda j, i: (0, j + off))
    bsp = lambda off: pl.BlockSpec((1, CONV_TILE), lambda j, i: (0, j + off))
    outs = pl.pallas_call(
        body, name="gate_bwd", grid=(nj, n_i),
        in_specs=[main(0), main(nj), halo(0), halo(nj), nxt(0), nxt(nj), main(0), nxt(0),
                  wsp(0), wsp(nj), bsp(0), bsp(nj)],
        out_specs=[main(0), main(0),
                   pl.BlockSpec((1, CONV_TILE), lambda j, i: (0, j)), pl.BlockSpec((1, CONV_TILE), lambda j, i: (0, j)),
                   pl.BlockSpec((3, CONV_TILE), lambda j, i: (0, j)), pl.BlockSpec((3, CONV_TILE), lambda j, i: (0, j))],
        out_shape=[jax.ShapeDtypeStruct((s, D_FF), BF16), jax.ShapeDtypeStruct((s, D_FF), BF16),
                   jax.ShapeDtypeStruct((1, D_FF), F32), jax.ShapeDtypeStruct((1, D_FF), F32),
                   jax.ShapeDtypeStruct((3, D_FF), F32), jax.ShapeDtypeStruct((3, D_FF), F32)],
        compiler_params=_params(("parallel", "arbitrary"), 24 << 20),
    )(up, up, up, up, up, up, da, da, w_conv, w_conv, b_conv, b_conv)
    return outs


def _final(x1, ffn, tgt, g2):
    s, d = x1.shape
    n_steps = s // NORM_TILE

    def body(x1_ref, f_ref, t_ref, g2_ref, dy_ref, df_ref, dg2_ref, loss_ref, lacc_ref):
        i = pl.program_id(0)

        @pl.when(i == 0)
        def _():
            dg2_ref[...] = jnp.zeros_like(dg2_ref)
            lacc_ref[...] = jnp.zeros_like(lacc_ref)

        f = f_ref[...]
        e = x1_ref[...] + g2_ref[...] * f - t_ref[...]
        dy = e * (1.0 / d)
        dy_ref[...] = dy
        df_ref[...] = (dy * g2_ref[...]).astype(BF16)
        dg2_ref[...] += jnp.sum(dy * f, axis=0, keepdims=True)
        lacc_ref[...] += jnp.sum(e * e, axis=0, keepdims=True)

        @pl.when(i == n_steps - 1)
        def _():
            loss_ref[...] = jnp.sum(lacc_ref[...], axis=1, keepdims=True) * (0.5 / d)

    row = pl.BlockSpec((NORM_TILE, d), lambda i: (i, 0))
    return pl.pallas_call(
        body, name="final", grid=(n_steps,),
        in_specs=[row, row, row, _full((1, d))],
        out_specs=[row, row, _full((1, d)), _full((1, 1))],
        out_shape=[jax.ShapeDtypeStruct((s, d), F32), jax.ShapeDtypeStruct((s, d), BF16),
                   jax.ShapeDtypeStruct((1, d), F32), jax.ShapeDtypeStruct((1, 1), F32)],
        scratch_shapes=[pltpu.VMEM((1, d), F32)],
        compiler_params=_params(("arbitrary",)),
    )(x1, ffn, tgt, g2)


def _ffnnorm_bwd(dh2, x1, dy, mix, gain, scale, g1):
    s, d = x1.shape
    n_steps = s // NORM_TILE

    def body(dh_ref, x_ref, dy_ref, mix_ref, g_ref, sc_ref, g1_ref, dx_ref, dm_ref, acc_ref):
        i = pl.program_id(0)

        @pl.when(i == 0)
        def _():
            acc_ref[...] = jnp.zeros_like(acc_ref)

        dh, x = dh_ref[...], x_ref[...]
        r = _rms(x)
        xn = x * r
        dn = dh * (1.0 + sc_ref[...])
        dxn = dn * g_ref[...]
        dx = dy_ref[...] + r * (dxn - xn * jnp.mean(dxn * xn, axis=-1, keepdims=True))
        dx_ref[...] = dx
        dm_ref[...] = (dx * g1_ref[...]).astype(BF16)
        csum = lambda z: jnp.sum(z, axis=0, keepdims=True)
        acc_ref[0:1, :] += csum(dh)
        acc_ref[1:2, :] += csum(dh * (xn * g_ref[...]))
        acc_ref[2:3, :] += csum(dn * xn)
        acc_ref[3:4, :] += csum(dx * mix_ref[...])

    row = pl.BlockSpec((NORM_TILE, d), lambda i: (i, 0))
    vec = _full((1, d))
    return pl.pallas_call(
        body, name="ffnnorm_bwd", grid=(n_steps,),
        in_specs=[row, row, row, row, vec, vec, vec],
        out_specs=[row, row, _full((8, d))],
        out_shape=[jax.ShapeDtypeStruct((s, d), F32), jax.ShapeDtypeStruct((s, d), BF16), jax.ShapeDtypeStruct((8, d), F32)],
        compiler_params=_params(("arbitrary",)),
    )(dh2, x1, dy, mix, gain, scale, g1)


def _mixnorm_bwd(dh, x, dx1, gain, scale):
    s, d = x.shape
    n_steps = s // NORM_TILE

    def body(dh_ref, x_ref, dx1_ref, g_ref, sc_ref, gx_ref, acc_ref):
        i = pl.program_id(0)

        @pl.when(i == 0)
        def _():
            acc_ref[...] = jnp.zeros_like(acc_ref)

        dh, x = dh_ref[...], x_ref[...]
        r = _rms(x)
        xn = x * r
        dn = dh * (1.0 + sc_ref[...])
        dxn = dn * g_ref[...]
        gx_ref[...] = dx1_ref[...] + r * (dxn - xn * jnp.mean(dxn * xn, axis=-1, keepdims=True))
        csum = lambda z: jnp.sum(z, axis=0, keepdims=True)
        acc_ref[0:1, :] += csum(dh)
        acc_ref[1:2, :] += csum(dh * (xn * g_ref[...]))
        acc_ref[2:3, :] += csum(dn * xn)

    row = pl.BlockSpec((NORM_TILE, d), lambda i: (i, 0))
    vec = _full((1, d))
    return pl.pallas_call(
        body, name="mixnorm_bwd", grid=(n_steps,),
        in_specs=[row, row, row, vec, vec],
        out_specs=[row, _full((8, d))],
        out_shape=[jax.ShapeDtypeStruct((s, d), F32), jax.ShapeDtypeStruct((8, d), F32)],
        compiler_params=_params(("arbitrary",)),
    )(dh, x, dx1, gain, scale)


def _key_count(d, dilated):
    if not dilated:
        return jnp.where(d >= 0, 1.0, 0.0)
    one = lambda cond: jnp.where(cond, 1.0, 0.0)
    cnt = one(d <= 128) + one(((d & 3) == 0) & (d <= 512)) + one((d & 15) == 0)
    return jnp.where(d >= 0, cnt, 0.0)


def _block_kinds(mla):
    return (0, "diag", "none") if mla else (NEAR_REACH, "near", "far")


NEAR_REACH = 512


def _near_offsets(tk, tq):
    return (NEAR_REACH - (tk - tq)) // tk + 1


def _scores_t(ka, qa, scale, kind, rel_t, offset, near_tabs=None):
    return _mask_scores(lax.dot_general(ka, qa, NT, preferred_element_type=F32), scale, kind, rel_t, offset, near_tabs)


def _fill_near_tables(bias_ref, cnt_ref, rel_t):
    tk, tq = rel_t.shape
    for idx in range(_near_offsets(tk, tq)):
        cnt = _key_count(rel_t + (tk - tq) + idx * tk, True)
        cnt_ref[idx] = cnt
        bias_ref[idx] = jnp.where(cnt > 0.0, 0.0, NEG_INF)


def _mask_scores(products, scale, kind, rel_t, offset, near_tabs=None):
    st = products * (scale * LOG2E)
    cnt = None
    if kind == "diag":
        st = jnp.where(rel_t + offset >= 0, st, NEG_INF)
    elif kind == "far":
        st = jnp.where((rel_t & 15) == 0, st, NEG_INF)
    elif kind == "near":
        bias_ref, cnt_ref = near_tabs
        tk, tq = rel_t.shape
        idx = (offset - (tk - tq)) // tk
        st = st + bias_ref[idx]
        cnt = cnt_ref[idx]
    return st, cnt


def _attn_fwd(q, k, v, mla, scale, name, gather=()):
    s = q.shape[0]
    qw = 2 * LANE if mla else LANE
    tq, tk = ATT_TQ, ATT_TK
    reach, kind_near, kind_far = _block_kinds(mla)
    assert s % tq == 0 and tq % tk == 0 and reach % tk == 0 and reach in (0, NEAR_REACH)
    ng = len(gather)
    last_step = HEADS // 2 - 1

    def body(*refs):
        q_ref, k_ref, v_ref = refs[:3]
        o_ref, lse_ref = refs[3 + ng:5 + ng]
        vt_ref, st_ref = refs[5 + 2 * ng:7 + 2 * ng]
        near_tabs = None if mla else refs[7 + 2 * ng:9 + 2 * ng]
        n_tabs = 0 if mla else 2
        comm = (refs[3:3 + ng], refs[5 + ng:5 + 2 * ng]) + tuple(refs[7 + n_tabs + 2 * ng:])
        if ng:
            @pl.when(pl.program_id(0) == 0)
            def _():
                _Gather(*comm).start()

            @pl.when(pl.program_id(0) == last_step)
            def _():
                _Gather(*comm).forward()

        lane = lax.broadcasted_iota(I32, (1, LANE), 1)
        rel_t = lax.broadcasted_iota(I32, (tk, tq), 1) - lax.broadcasted_iota(I32, (tk, tq), 0)
        if not mla:
            _fill_near_tables(*near_tabs, rel_t)

        def transpose_v(j, carry):
            c0 = pl.multiple_of(j * tk, tk)
            vt_ref[:, pl.ds(c0, tk)] = v_ref[pl.ds(c0, tk), :].astype(F32).T.astype(BF16)
            return carry

        lax.fori_loop(0, s // tk, transpose_v, 0)

        def q_block(qi, carry):
            r0 = pl.multiple_of(qi * tq, tq)
            kcols = [slice(a * LANE, (a + 1) * LANE) if mla else slice(0, LANE) for a in range(2)]
            qas = [q_ref[pl.ds(r0, tq), kcols[a]] for a in range(2)]
            if not mla:
                qas = [jnp.where(lane < DIL_DIM, qas[0], jnp.zeros_like(qas[0])),
                       jnp.where(lane >= DIL_DIM, qas[1], jnp.zeros_like(qas[1]))]

            n_k = (r0 + tq) // tk

            def products(kj):
                c0 = pl.multiple_of(kj * tk, tk)
                return [lax.dot_general(k_ref[pl.ds(c0, tk), kcols[a]], qas[a], NT, preferred_element_type=F32)
                        for a in range(2)]

            for a, pr in enumerate(products(0)):
                st_ref[0, a] = pr

            def k_block(kj, c, kind):
                c0 = pl.multiple_of(kj * tk, tk)
                slot = kj & 1
                ahead = products(jnp.minimum(kj + 1, n_k - 1))
                out = []
                for a in range(2):
                    m, l, acc = c[a]
                    st, cnt = _mask_scores(st_ref[slot, a], scale, kind, rel_t, r0 - c0, near_tabs)
                    st_ref[1 - slot, a] = ahead[a]
                    m_new = jnp.maximum(m, jnp.max(st, axis=0, keepdims=True))
                    alpha = jnp.exp2(m - m_new)
                    p = jnp.exp2(st - m_new)
                    if cnt is not None:
                        p = p * cnt
                    l = alpha * l + jnp.sum(p, axis=0, keepdims=True)
                    vt = vt_ref[a * DIL_DIM:(a + 1) * DIL_DIM, pl.ds(c0, tk)]
                    acc = alpha * acc + jnp.dot(vt, p.astype(BF16), preferred_element_type=F32)
                    out.append((m_new, l, acc))
                return tuple(out)

            one = (jnp.full((1, tq), NEG_INF, F32), jnp.zeros((1, tq), F32), jnp.zeros((DIL_DIM, tq), F32))
            first_near = jnp.maximum((r0 - reach) // tk, 0)
            c = lax.fori_loop(0, first_near, functools.partial(k_block, kind=kind_far), (one, one))
            res = lax.fori_loop(first_near, (r0 + tq) // tk, functools.partial(k_block, kind=kind_near), c)
            o_t = jnp.concatenate([res[a][2] / res[a][1] for a in range(2)], axis=0)
            o_ref[pl.ds(r0, tq), :] = o_t.T.astype(BF16)
            for a in range(2):
                lse_ref[a, :, pl.ds(r0, tq)] = res[a][0] * LN2 + jnp.log(res[a][1])
            return carry

        lax.fori_loop(0, s // tq, q_block, 0)

        if ng:
            @pl.when(pl.program_id(0) == last_step)
            def _():
                _Gather(*comm).finish()

    return pl.pallas_call(
        body, name=name, grid=(HEADS // 2,),
        in_specs=[pl.BlockSpec((s, qw), lambda h: (0, h)), pl.BlockSpec((s, qw), lambda h: (0, h)),
                  pl.BlockSpec((s, LANE), lambda h: (0, h))] + [ANY] * ng,
        out_specs=[pl.BlockSpec((s, LANE), lambda h: (0, h)), pl.BlockSpec((2, 1, s), lambda h: (h, 0, 0))] + [ANY] * ng,
        out_shape=[jax.ShapeDtypeStruct((s, DIL_W), BF16), jax.ShapeDtypeStruct((HEADS, 1, s), F32)] + _Gather.out_shapes(gather),
        scratch_shapes=[pltpu.VMEM((LANE, s), BF16), pltpu.VMEM((2, 2, tk, tq), F32)]
        + ([] if mla else [pltpu.VMEM((_near_offsets(tk, tq), tk, tq), F32)] * 2) + (_Gather.scratch(gather) if ng else []),
        compiler_params=_params(("arbitrary",) if ng else ("parallel",), 12 << 20),
    )(*_in_hbm(q, k, v), *gather)


def _attn_bwd(q, k, v, o, do, do_block0, lse, mla, scale, name, scatter=()):
    s = q.shape[0]
    qw = 2 * LANE if mla else LANE
    tq, tk = ATT_TQ, ATT_TK_BWD
    nq = s // tq
    reach, kind_near, kind_far = _block_kinds(mla)
    assert s % tq == 0 and s % tk == 0
    ns = len(scatter)
    last_step = HEADS // 2 - 1

    def body(*refs):
        q_ref, k_ref, v_ref, o_ref, do_ref, lse_ref = refs[:6]
        dq_ref, dk_ref, dv_ref = refs[6 + ns:9 + ns]
        kt_ref, dot_ref, dob_ref, dqt_ref, delta_ref, lse2_ref = refs[9 + 2 * ns:15 + 2 * ns]
        near_tabs = None if mla else refs[15 + 2 * ns:17 + 2 * ns]
        n_tabs = 0 if mla else 2
        comm = (refs[6:6 + ns], refs[9 + ns:9 + 2 * ns]) + tuple(refs[15 + n_tabs + 2 * ns:])
        if ns:
            @pl.when(pl.program_id(0) == 0)
            def _():
                _Scatter(*comm).start()

        lane = lax.broadcasted_iota(I32, (1, LANE), 1)
        row = lax.broadcasted_iota(I32, (LANE, 1), 0)
        rel_t = lax.broadcasted_iota(I32, (tk, tq), 1) - lax.broadcasted_iota(I32, (tk, tq), 0)
        if not mla:
            _fill_near_tables(*near_tabs, rel_t)

        def prepare(j, carry):
            c0 = pl.multiple_of(j * tk, tk)
            do_blk = do_ref[pl.ds(c0, tk), :]
            dob_ref[pl.ds(c0, tk), :] = do_blk.astype(BF16)
            do_t = do_blk.T
            dot_ref[:, pl.ds(c0, tk)] = do_t.astype(BF16)
            prod = do_t * o_ref[pl.ds(c0, tk), :].astype(F32).T
            delta_ref[0, :, pl.ds(c0, tk)] = jnp.sum(prod[0:DIL_DIM], axis=0, keepdims=True)
            delta_ref[1, :, pl.ds(c0, tk)] = jnp.sum(prod[DIL_DIM:LANE], axis=0, keepdims=True)
            for w in range(qw // LANE):
                kt_ref[w * LANE:(w + 1) * LANE, pl.ds(c0, tk)] = (
                    k_ref[pl.ds(c0, tk), w * LANE:(w + 1) * LANE].astype(F32).T.astype(BF16))
            return carry

        lax.fori_loop(0, s // tk, prepare, 0)
        dqt_ref[...] = jnp.zeros_like(dqt_ref)
        lse2_ref[...] = lse_ref[...] * LOG2E

        sels = [lane < DIL_DIM, lane >= DIL_DIM]
        rsels = [row < DIL_DIM, row >= DIL_DIM]
        cols = [slice(a * LANE, (a + 1) * LANE) if mla else slice(0, LANE) for a in range(2)]

        def k_block(kj, carry):
            c0 = pl.multiple_of(kj * tk, tk)
            kas = [k_ref[pl.ds(c0, tk), cols[a]] for a in range(2)]
            kts = [kt_ref[cols[a], pl.ds(c0, tk)] for a in range(2)]
            if not mla:
                kas = [jnp.where(sels[a], kas[a], jnp.zeros_like(kas[a])) for a in range(2)]
                kts = [jnp.where(rsels[a], kts[a], jnp.zeros_like(kts[a])) for a in range(2)]
            vb = v_ref[pl.ds(c0, tk), :]
            vbs = [jnp.where(sels[a], vb, jnp.zeros_like(vb)) for a in range(2)]

            first = c0 // tq

            def q_block(qi, c, kind):
                r0 = pl.multiple_of(qi * tq, tq)
                out, dq_parts = [], []
                for a in range(2):
                    dk_acc, dv_acc = c[a]
                    qa = q_ref[pl.ds(r0, tq), cols[a]]
                    st, cnt = _scores_t(kas[a], qa, scale, kind, rel_t, r0 - c0, near_tabs)
                    p = jnp.exp2(st - lse2_ref[a, :, pl.ds(r0, tq)])
                    if cnt is not None:
                        p = p * cnt
                    dp = jnp.dot(vbs[a], dot_ref[:, pl.ds(r0, tq)], preferred_element_type=F32)
                    ds = (p * (dp - delta_ref[a, :, pl.ds(r0, tq)]) * scale).astype(BF16)
                    dv_acc = dv_acc + jnp.dot(p.astype(BF16), dob_ref[pl.ds(r0, tq), :], preferred_element_type=F32)
                    dk_acc = dk_acc + jnp.dot(ds, qa, preferred_element_type=F32)
                    dq_parts.append(jnp.dot(kts[a], ds, preferred_element_type=F32))
                    out.append((dk_acc, dv_acc))
                if mla:
                    for a in range(2):
                        dqt_ref[cols[a], pl.ds(r0, tq)] += dq_parts[a]
                else:
                    dqt_ref[:, pl.ds(r0, tq)] += dq_parts[0] + dq_parts[1]
                return tuple(out)

            zero = jnp.zeros((tk, LANE), F32)
            last_near = jnp.minimum((c0 + tk - 1 + reach) // tq + 1, nq)
            c = lax.fori_loop(first, last_near, functools.partial(q_block, kind=kind_near), ((zero, zero), (zero, zero)))
            (dk0, dv0), (dk1, dv1) = lax.fori_loop(last_near, nq, functools.partial(q_block, kind=kind_far), c)
            if mla:
                dk_ref[pl.ds(c0, tk), cols[0]] = dk0
                dk_ref[pl.ds(c0, tk), cols[1]] = dk1
            else:
                dk_ref[pl.ds(c0, tk), :] = jnp.where(sels[0], dk0, dk1)
            dv_ref[pl.ds(c0, tk), :] = jnp.where(sels[0], dv0, dv1)
            return carry

        lax.fori_loop(0, s // tk, k_block, 0)

        def write_dq(j, carry):
            c0 = pl.multiple_of(j * tk, tk)
            for w in range(qw // LANE):
                dq_ref[pl.ds(c0, tk), w * LANE:(w + 1) * LANE] = dqt_ref[w * LANE:(w + 1) * LANE, pl.ds(c0, tk)].T
            return carry

        lax.fori_loop(0, s // tk, write_dq, 0)

        if ns:
            @pl.when(pl.program_id(0) == last_step)
            def _():
                _Scatter(*comm).finish()

    b0 = do_block0
    return pl.pallas_call(
        body, name=name, grid=(HEADS // 2,),
        in_specs=[pl.BlockSpec((s, qw), lambda h: (0, h)), pl.BlockSpec((s, qw), lambda h: (0, h)),
                  pl.BlockSpec((s, LANE), lambda h: (0, h)), pl.BlockSpec((s, LANE), lambda h: (0, h)),
                  pl.BlockSpec((s, LANE), lambda h: (0, h + b0)), pl.BlockSpec((2, 1, s), lambda h: (h, 0, 0))] + [ANY] * ns,
        out_specs=[pl.BlockSpec((s, qw), lambda h: (0, h)), pl.BlockSpec((s, qw), lambda h: (0, h)),
                   pl.BlockSpec((s, LANE), lambda h: (0, h))] + [ANY] * ns,
        out_shape=[jax.ShapeDtypeStruct(q.shape, F32), jax.ShapeDtypeStruct(k.shape, F32), jax.ShapeDtypeStruct((s, DIL_W), F32)]
        + _Scatter.out_shapes(scatter),
        scratch_shapes=[pltpu.VMEM((qw, s), BF16), pltpu.VMEM((LANE, s), BF16), pltpu.VMEM((s, LANE), BF16),
                        pltpu.VMEM((qw, s), F32), pltpu.VMEM((2, 1, s), F32), pltpu.VMEM((2, 1, s), F32)]
        + ([] if mla else [pltpu.VMEM((_near_offsets(tk, tq), tk, tq), F32)] * 2) + (_Scatter.semaphores(ns) if ns else []),
        compiler_params=_params(("arbitrary",) if ns else ("parallel",), 24 << 20),
    )(*_in_hbm(q, k, v, o, do, lse), *scatter)


def _ada_bwd(c_all, dmod_shard):
    n, d = c_all.shape
    cols = dmod_shard.shape[1]

    def body(c_ref, g_ref, o_ref):
        cv = c_ref[...]
        o_ref[...] = lax.dot_general(cv * _sigmoid(cv), g_ref[...], TN, precision=HIGHEST, preferred_element_type=F32)

    return pl.pallas_call(
        body, name="ada_bwd", out_shape=jax.ShapeDtypeStruct((d, cols), F32),
        compiler_params=_params(None, 16 << 20),
    )(c_all, dmod_shard)


SMALL_WIDTHS = (("g_mix_norm", D_MODEL), ("g_q_lat", Q_LORA), ("g_kv_lat", KV_LORA), ("g_mla_q_nope", NOPE),
                ("g_mla_q_pe", ROPE), ("g_mla_k_nope", NOPE), ("g_mla_k_pe", ROPE), ("g_dil_q", DIL_DIM),
                ("g_dil_k", DIL_DIM), ("g_ffn_norm", D_MODEL), ("b_conv", UP_W))


def _small_layout():
    pieces = (("dmod", 6 * D_MODEL),) + SMALL_WIDTHS + tuple(("w_conv%d" % k, UP_W) for k in range(3)) + (("loss", 1),)
    layout, off = {}, 0
    for name, width in pieces:
        layout[name] = (width, off)
        off += -(-width // LANE) * LANE
    return layout, off


def _pack_small(acc1, acc2, dg2, dglat, dgains, dbg, dbv, dwg, dwv, loss_part):
    layout, total = _small_layout()

    def body(a1, a2, g2, gl, gg, bg, bv, wg, wv, ls, o_ref):
        o_ref[...] = jnp.zeros_like(o_ref)

        def put(name, src, shift=0):
            start = layout[name][1] + shift
            o_ref[:, start:start + src.shape[1]] = src

        for k, src in enumerate((a1[0:1, :], a1[1:2, :], a2[3:4, :], a2[0:1, :], a2[1:2, :], g2[...])):
            put("dmod", src, k * D_MODEL)
        put("g_mix_norm", a1[2:3, :])
        put("g_q_lat", gl[0:1, :])
        put("g_kv_lat", gl[1:2, 0:KV_LORA])
        put("g_mla_q_nope", gg[0:1, 0:NOPE])
        put("g_mla_q_pe", gg[5:6, 0:ROPE])
        put("g_mla_k_nope", gg[1:2, 0:NOPE])
        put("g_mla_k_pe", gg[2:3, 0:ROPE])
        put("g_dil_q", gg[3:4, 0:DIL_DIM])
        put("g_dil_k", gg[4:5, 0:DIL_DIM])
        put("g_ffn_norm", a2[2:3, :])
        put("b_conv", bg[...])
        put("b_conv", bv[...], D_FF)
        for k in range(3):
            put("w_conv%d" % k, wg[k:k + 1, :])
            put("w_conv%d" % k, wv[k:k + 1, :], D_FF)
        put("loss", ls[...])

    ins = (acc1, acc2, dg2, dglat, dgains, dbg, dbv, dwg, dwv, loss_part)
    return pl.pallas_call(
        body, name="pack_small", grid=(1,), in_specs=[_full(a.shape) for a in ins], out_specs=_full((1, total)),
        out_shape=jax.ShapeDtypeStruct((1, total), F32),
        compiler_params=_params(("arbitrary",), 2 << 20),
    )(*_in_hbm(*ins))


def _sum_unpack(g):
    n_dev, _, total = g.shape
    layout, _ = _small_layout()

    def body(g_ref, *refs):
        o_refs, s_ref = refs[:-1], refs[-1]
        acc = g_ref[0]
        for k in range(1, n_dev):
            acc = acc + g_ref[k]
        s_ref[...] = acc
        take = lambda name: s_ref[:, layout[name][1]:layout[name][1] + layout[name][0]]
        o_refs[0][...] = take("dmod")
        for i, (name, _) in enumerate(SMALL_WIDTHS):
            o_refs[1 + i][...] = take(name)
        for k in range(3):
            o_refs[-2][k:k + 1, :] = take("w_conv%d" % k)
        o_refs[-1][...] = take("loss")

    shapes = [(1, 6 * D_MODEL)] + [(1, w) for _, w in SMALL_WIDTHS] + [(3, UP_W), (1, 1)]
    return pl.pallas_call(
        body, name="sum_unpack", out_shape=[jax.ShapeDtypeStruct(sh, F32) for sh in shapes],
        scratch_shapes=[pltpu.VMEM((1, total), F32)],
        compiler_params=_params(None, 4 << 20),
    )(g)


def _adamw_math(w, g, m, v):
    mn = ADAM_B1 * m + (1.0 - ADAM_B1) * g
    vn = ADAM_B2 * v + (1.0 - ADAM_B2) * (g * g)
    m_hat = mn / (1.0 - ADAM_B1 ** ADAM_STEP)
    v_hat = vn / (1.0 - ADAM_B2 ** ADAM_STEP)
    return -ADAM_LR * (m_hat / (jnp.sqrt(v_hat) + ADAM_EPS) + ADAM_WD * w), mn, vn


def _adamw_vectors(ws, gs, ms, vs):
    k = len(ws)

    def body(*refs):
        for i in range(k):
            d, mn, vn = _adamw_math(refs[i][...], refs[k + i][...], refs[2 * k + i][...], refs[3 * k + i][...])
            refs[4 * k + i][...] = d
            refs[5 * k + i][...] = mn
            refs[6 * k + i][...] = vn

    blocks = [_full(w.shape) for w in ws]
    outs = pl.pallas_call(
        body, name="adamw_vectors", grid=(1,), in_specs=blocks * 4, out_specs=blocks * 3,
        out_shape=[jax.ShapeDtypeStruct(w.shape, F32) for w in ws] * 3,
        compiler_params=_params(("arbitrary",), 2 << 20),
    )(*_in_hbm(*ws, *gs, *ms, *vs))
    return outs[:k], outs[k:2 * k], outs[2 * k:]


def _adamw(w, g, m, v, name):
    r, c = w.shape
    tr = r
    for cand in (256, 128, 64, 32, 16):
        if r % cand == 0 and r > cand:
            tr = cand
            break

    def body(w_ref, g_ref, m_ref, v_ref, d_ref, mo_ref, vo_ref):
        d_ref[...], mo_ref[...], vo_ref[...] = _adamw_math(w_ref[...], g_ref[...], m_ref[...], v_ref[...])

    blk = pl.BlockSpec((tr, c), lambda i: (i, 0))
    return pl.pallas_call(
        body, name=name, grid=(r // tr,), in_specs=[blk] * 4, out_specs=[blk] * 3,
        out_shape=[jax.ShapeDtypeStruct((r, c), F32)] * 3,
        compiler_params=_params(("parallel",), 7 * _nbytes((tr, c), F32)),
    )(w, g, m, v)


def _position():
    return lax.axis_index("x"), lax.axis_index("y"), lax.axis_index("c")


def _other_chips(x, y):
    return [(1 - x, y, 2 * (1 - x) + y), (x, 1 - y, 2 * x + (1 - y)), (1 - x, 1 - y, 2 * (1 - x) + (1 - y))]


class _SmallGather:
    def __init__(self, v_ref, out_ref, send_sems, recv_sems, local_sem):
        x, y, c = _position()
        me = 4 * x + 2 * y + c
        self.local = pltpu.make_async_copy(v_ref, out_ref.at[me], local_sem)
        self.sends, self.arrivals = [], []
        for k in range(N_DEV - 1):
            fx, fy, fc = ((k + 1) >> 2) & 1, ((k + 1) >> 1) & 1, (k + 1) & 1
            px, py, pc = (1 - x if fx else x), (1 - y if fy else y), (1 - c if fc else c)

            def copy(dst, k=k, peer=(px, py, pc)):
                return pltpu.make_async_remote_copy(src_ref=v_ref, dst_ref=dst, send_sem=send_sems.at[k],
                                                    recv_sem=recv_sems.at[k], device_id=peer, device_id_type=MESH)

            self.sends.append(copy(out_ref.at[me]))
            self.arrivals.append(copy(out_ref.at[4 * px + 2 * py + pc]))

    @staticmethod
    def semaphores():
        return [pltpu.SemaphoreType.DMA((N_DEV - 1,)), pltpu.SemaphoreType.DMA((N_DEV - 1,)), pltpu.SemaphoreType.DMA]

    def start(self):
        self.local.start()
        for cp in self.sends:
            cp.start()

    def finish(self):
        for cp in self.arrivals:
            cp.wait_recv()
        for cp in self.sends:
            cp.wait_send()
        self.local.wait()


def _prologue(c_taps, w_ada_shard, b_shard, pos_col, rope_consts, shards):
    n = len(shards)
    s = pos_col.shape[0]
    cols = w_ada_shard.shape[1]
    freq, csel, ssel = rope_consts

    def body(*refs):
        ct_ref, w_ref, b_ref, p_ref, f_ref, cs_ref, ss_ref = refs[:7]
        sh_refs = refs[7:7 + n]
        ct_all_ref, mod_all_ref, tab_ref = refs[7 + n:10 + n]
        g_refs = refs[10 + n:10 + 2 * n]
        mod_blk_ref = refs[10 + 2 * n]
        sems = refs[11 + 2 * n:]
        weights = _Gather(sh_refs, g_refs, *sems[6:])
        weights.start()
        first = _SmallGather(ct_ref, ct_all_ref, *sems[0:3])
        first.start()
        first.finish()
        cv = ct_all_ref[:, 0, 0:D_MODEL]
        sc = (cv * _sigmoid(cv)).astype(BF16)
        mod_blk_ref[...] = jnp.dot(sc, w_ref[...].astype(BF16), preferred_element_type=F32) + b_ref[...]
        second = _SmallGather(mod_blk_ref, mod_all_ref, *sems[3:6])
        second.start()

        def table_rows(i, carry):
            r0 = pl.multiple_of(i * ROW_TILE, ROW_TILE)
            ang = p_ref[pl.ds(r0, ROW_TILE), :].astype(F32) * f_ref[...]
            tab_ref[pl.ds(r0, ROW_TILE), :] = cs_ref[...] * jnp.cos(ang) + ss_ref[...] * jnp.sin(ang)
            return carry

        lax.fori_loop(0, s // ROW_TILE, table_rows, 0)
        second.finish()
        weights.forward()
        weights.finish()

    return pl.pallas_call(
        body, name="prologue",
        out_shape=[jax.ShapeDtypeStruct((N_DEV,) + c_taps.shape, F32), jax.ShapeDtypeStruct((N_DEV, N_DEV, cols), F32),
                   jax.ShapeDtypeStruct((s, 4 * LANE), F32)] + _Gather.out_shapes(shards),
        in_specs=[IN_VMEM] * 7 + [ANY] * n, out_specs=[IN_VMEM] * 3 + [ANY] * n,
        scratch_shapes=[pltpu.VMEM((N_DEV, cols), F32)] + _SmallGather.semaphores() * 2 + _Gather.scratch(shards),
        compiler_params=_params(None, 14 << 20),
    )(c_taps, w_ada_shard, b_shard, pos_col, freq, csel, ssel, *shards)


IN_VMEM = pl.BlockSpec(memory_space=pltpu.VMEM)
ANY = pl.BlockSpec(memory_space=pl.ANY)


class _Gather:
    def __init__(self, w_refs, out_refs, send_sems, recv_sems, own_sems, *bounce_refs):
        x, y, c = _position()
        q0 = 2 * x + y
        sibling = (x, y, 1 - c)
        self.ici, self.ici_in, self.fwd, self.fwd_in, self.own_in, self.own_out = [], [], [], [], [], []
        for k, (w_ref, out_ref) in enumerate(zip(w_refs, out_refs)):
            half = w_ref.shape[0] // 2
            self.own_in.append(pltpu.make_async_copy(w_ref, bounce_refs[k], own_sems.at[2 * k]))
            self.own_out.append(pltpu.make_async_copy(bounce_refs[k], out_ref.at[q0], own_sems.at[2 * k + 1]))

            def blk(q, e, out_ref=out_ref, half=half):
                return out_ref.at[q, pl.ds(pl.multiple_of(e * half, 16), half), :]

            def copy(src, dst, i, to):
                return pltpu.make_async_remote_copy(src_ref=src, dst_ref=dst, send_sem=send_sems.at[i], recv_sem=recv_sems.at[i],
                                                    device_id=to, device_id_type=MESH)

            src = w_ref.at[pl.ds(pl.multiple_of(c * half, 16), half), :]
            for j, (cx, cy, qj) in enumerate(_other_chips(x, y)):
                self.ici.append(copy(src, blk(q0, c), 6 * k + j, (cx, cy, c)))
                self.ici_in.append(copy(blk(qj, c), blk(qj, c), 6 * k + j, (cx, cy, c)))
                self.fwd.append(copy(blk(qj, c), blk(qj, c), 6 * k + 3 + j, sibling))
                self.fwd_in.append(copy(blk(qj, 1 - c), blk(qj, 1 - c), 6 * k + 3 + j, sibling))

    @staticmethod
    def out_shapes(shards):
        return [jax.ShapeDtypeStruct((N_CHIP,) + s.shape, s.dtype) for s in shards]

    @staticmethod
    def scratch(shards):
        n = len(shards)
        return ([pltpu.SemaphoreType.DMA((6 * n,)), pltpu.SemaphoreType.DMA((6 * n,)), pltpu.SemaphoreType.DMA((2 * n,))]
                + [pltpu.VMEM(s.shape, s.dtype) for s in shards])

    def start(self):
        for cp in self.ici + self.own_in:
            cp.start()

    def forward(self):
        for fetched, placed in zip(self.own_in, self.own_out):
            fetched.wait()
            placed.start()
        for arrived, onward in zip(self.ici_in, self.fwd):
            arrived.wait_recv()
            onward.start()

    def finish(self):
        for cp in self.fwd_in:
            cp.wait_recv()
        for cp in self.ici + self.fwd:
            cp.wait_send()
        for cp in self.own_out:
            cp.wait()


class _PairSwap:
    def __init__(self, g_refs, out_refs, send_sems, recv_sems):
        x, y, c = _position()
        self.copies = [
            pltpu.make_async_remote_copy(src_ref=g_ref.at[:, 1 - c], dst_ref=out_ref, send_sem=send_sems.at[k],
                                         recv_sem=recv_sems.at[k], device_id=(x, y, 1 - c), device_id_type=MESH)
            for k, (g_ref, out_ref) in enumerate(zip(g_refs, out_refs))]

    @staticmethod
    def out_shapes(grads):
        return [jax.ShapeDtypeStruct((N_CHIP,) + g.shape[2:], g.dtype) for g in grads]

    @staticmethod
    def semaphores(n):
        return [pltpu.SemaphoreType.DMA((n,)), pltpu.SemaphoreType.DMA((n,))]

    def start(self):
        for cp in self.copies:
            cp.start()

    def finish(self):
        for cp in self.copies:
            cp.wait_recv()
        for cp in self.copies:
            cp.wait_send()


def _pair_sum(g, a, c_idx, name):
    _, _, rh, cols = g.shape
    tr = rh
    for cand in (256, 128, 64, 32, 16):
        if rh % cand == 0 and rh > cand:
            tr = cand
            break

    def body(c_ref, g_ref, a_ref, o_ref):
        o_ref[...] = (g_ref[...] + a_ref[...]).astype(BF16)

    return pl.pallas_call(
        body, name=name,
        grid_spec=pltpu.PrefetchScalarGridSpec(
            num_scalar_prefetch=1, grid=(N_CHIP, rh // tr),
            in_specs=[pl.BlockSpec((None, None, tr, cols), lambda q, i, c_ref: (q, c_ref[0], i, 0)),
                      pl.BlockSpec((None, tr, cols), lambda q, i, c_ref: (q, i, 0))],
            out_specs=pl.BlockSpec((None, tr, cols), lambda q, i, c_ref: (q, i, 0))),
        out_shape=jax.ShapeDtypeStruct((N_CHIP, rh, cols), BF16),
        compiler_params=_params(("parallel", "parallel"), 10 * _nbytes((tr, cols), F32)),
    )(c_idx, g, a)


def _scatter_and_gather(parts, small, name):
    n = len(parts)

    def body(*refs):
        scatter = _Scatter(refs[:n], refs[n + 1:2 * n + 1], *refs[2 * n + 2:2 * n + 4])
        gather = _SmallGather(refs[n], refs[2 * n + 1], *refs[2 * n + 4:])
        scatter.start()
        gather.start()
        gather.finish()
        scatter.finish()

    return pl.pallas_call(
        body, name=name,
        out_shape=_Scatter.out_shapes(parts) + [jax.ShapeDtypeStruct((N_DEV,) + small.shape, F32)],
        in_specs=[ANY] * n + [IN_VMEM], out_specs=[ANY] * n + [IN_VMEM],
        scratch_shapes=_Scatter.semaphores(n) + _SmallGather.semaphores(),
        compiler_params=_params(None, 10 * _nbytes(small.shape, F32)),
    )(*parts, small)


class _Scatter:
    def __init__(self, p_refs, out_refs, send_sems, recv_sems):
        x, y, c = _position()
        self.copies = []
        for k, (p_ref, out_ref) in enumerate(zip(p_refs, out_refs)):
            for j, (cx, cy, qj) in enumerate(_other_chips(x, y)):
                self.copies.append(pltpu.make_async_remote_copy(
                    src_ref=p_ref.at[qj], dst_ref=out_ref.at[j], send_sem=send_sems.at[3 * k + j],
                    recv_sem=recv_sems.at[3 * k + j], device_id=(cx, cy, c), device_id_type=MESH))

    @staticmethod
    def out_shapes(parts):
        return [jax.ShapeDtypeStruct((3,) + p.shape[1:], p.dtype) for p in parts]

    @staticmethod
    def semaphores(n):
        return [pltpu.SemaphoreType.DMA((3 * n,)), pltpu.SemaphoreType.DMA((3 * n,))]

    def start(self):
        for cp in self.copies:
            cp.start()

    def finish(self):
        for cp in self.copies:
            cp.wait_recv()
        for cp in self.copies:
            cp.wait_send()


def _shard_sum(p, b, qc_idx, name):
    _, rh, cols = p.shape
    tr = rh
    for cand in (256, 128, 64, 32, 16):
        if rh % cand == 0 and rh > cand:
            tr = cand
            break

    def body(qc_ref, p_ref, b_ref, o_ref):
        acc = p_ref[...].astype(F32)
        for j in range(3):
            acc = acc + b_ref[j].astype(F32)
        o_ref[...] = acc

    return pl.pallas_call(
        body, name=name,
        grid_spec=pltpu.PrefetchScalarGridSpec(
            num_scalar_prefetch=1, grid=(rh // tr,),
            in_specs=[pl.BlockSpec((None, tr, cols), lambda i, qc_ref: (qc_ref[0], i, 0)),
                      pl.BlockSpec((3, tr, cols), lambda i, qc_ref: (0, i, 0))],
            out_specs=pl.BlockSpec((None, tr, cols), lambda i, qc_ref: (qc_ref[1], i, 0))),
        out_shape=jax.ShapeDtypeStruct((2, rh, cols), F32),
        compiler_params=_params(("parallel",), 8 * _nbytes((tr, cols), F32)),
    )(qc_idx, p, b)


def _join_halves(shards):
    n = len(shards)

    def body(*refs):
        out_refs = refs[n:2 * n]
        send_sems, recv_sems = refs[2 * n:]
        x, y, c = _position()
        cps = [pltpu.make_async_remote_copy(src_ref=out_refs[k].at[c], dst_ref=out_refs[k].at[c], send_sem=send_sems.at[k],
                                            recv_sem=recv_sems.at[k], device_id=(x, y, 1 - c), device_id_type=MESH)
               for k in range(n)]
        for cp in cps:
            cp.start()
        for k in range(n):
            arriving = out_refs[k].at[1 - c]
            pltpu.make_async_remote_copy(src_ref=arriving, dst_ref=arriving, send_sem=send_sems.at[k], recv_sem=recv_sems.at[k],
                                         device_id=(x, y, 1 - c), device_id_type=MESH).wait_recv()
        for cp in cps:
            cp.wait_send()

    return pl.pallas_call(
        body, name="rs_join",
        out_shape=[jax.ShapeDtypeStruct(a.shape, a.dtype) for a in shards],
        in_specs=[ANY] * n, out_specs=[ANY] * n, input_output_aliases={k: k for k in range(n)},
        scratch_shapes=[pltpu.SemaphoreType.DMA((n,)), pltpu.SemaphoreType.DMA((n,))],
    )(*shards)


def _cols_from_shards(g):
    q, r, cs = g.shape
    return jnp.transpose(g, (1, 0, 2)).reshape(r, q * cs)


def _cols_to_shards(w):
    r, cfull = w.shape
    return jnp.transpose(w.reshape(r, N_CHIP, cfull // N_CHIP), (1, 0, 2))


def _pad_w_in(w):
    z = lambda n: jnp.zeros((w.shape[0], n), w.dtype)
    q_lat, kv_lat, kpe = w[:, 0:512], w[:, 512:768], w[:, 768:800]
    qd, kd, vd = w[:, 800:1312], w[:, 1312:1824], w[:, 1824:2336]
    return jnp.concatenate([q_lat, qd, kd, vd, kv_lat, z(KPE_OFF), kpe, z(LANE - KPE_OFF - ROPE)], axis=1)


def _pad_w_qb(w):
    w3 = w.reshape(Q_LORA, HEADS, NOPE + ROPE)
    return jnp.pad(w3, ((0, 0), (0, 0), (0, LANE - NOPE - ROPE))).reshape(Q_LORA, HEADS * LANE)


def _unpad_w_qb(g):
    return g.reshape(Q_LORA, HEADS, LANE)[:, :, :NOPE + ROPE].reshape(Q_LORA, HEADS * (NOPE + ROPE))


def _pad_w_kvb(w):
    w3 = w.reshape(KV_LORA, HEADS, 2 * NOPE)
    kp = jnp.pad(w3[:, :, :NOPE], ((0, 0), (0, 0), (0, LANE - NOPE))).reshape(KV_LORA, HEADS * LANE)
    return jnp.concatenate([kp, w3[:, :, NOPE:].reshape(KV_LORA, DIL_W)], axis=1)


def _unpad_w_kvb(g):
    gk = g[:, :HEADS * LANE].reshape(KV_LORA, HEADS, LANE)[:, :, :NOPE]
    gv = g[:, HEADS * LANE:].reshape(KV_LORA, HEADS, NOPE)
    return jnp.concatenate([gk, gv], axis=2).reshape(KV_LORA, HEADS * 2 * NOPE)


def _head_gains(g_q_nope, g_q_pe, g_k_nope, g_k_pe, g_dq, g_dk):
    z = lambda n: jnp.zeros((1, n), F32)
    q1 = jnp.concatenate([g_q_nope, g_q_pe, z(LANE - NOPE - ROPE)], axis=1)
    k1 = jnp.concatenate([g_k_nope, z(LANE - NOPE)], axis=1)
    kpe = jnp.concatenate([z(KPE_OFF), g_k_pe, z(LANE - KPE_OFF - ROPE)], axis=1)
    return dict(q=jnp.tile(q1, (1, HEADS)), k=jnp.tile(k1, (1, HEADS)), kpe=kpe,
                dq=jnp.tile(g_dq, (1, HEADS)), dk=jnp.tile(g_dk, (1, HEADS)))


def kernel(x, c, positions, w_ada, b_ada, g_mix_norm, w_in, g_q_lat, w_q_b, g_kv_lat, w_kv_b, g_mla_q_nope, g_mla_q_pe, g_mla_k_nope, g_mla_k_pe, g_dil_q, g_dil_k, w_o, g_ffn_norm, w_up, w_conv, b_conv, w_down, loss_target, m_w_ada, m_b_ada, m_g_mix_norm, m_w_in, m_g_q_lat, m_w_q_b, m_g_kv_lat, m_w_kv_b, m_g_mla_q_nope, m_g_mla_q_pe, m_g_mla_k_nope, m_g_mla_k_pe, m_g_dil_q, m_g_dil_k, m_w_o, m_g_ffn_norm, m_w_up, m_w_conv, m_b_conv, m_w_down, v_w_ada, v_b_ada, v_g_mix_norm, v_w_in, v_g_q_lat, v_w_q_b, v_g_kv_lat, v_w_kv_b, v_g_mla_q_nope, v_g_mla_q_pe, v_g_mla_k_nope, v_g_mla_k_pe, v_g_dil_q, v_g_dil_k, v_w_o, v_g_ffn_norm, v_w_up, v_w_conv, v_b_conv, v_w_down):
    args = dict(locals())
    weights = {n: args[n][0] for n in ("w_ada", "w_in", "w_q_b", "w_kv_b", "w_o", "w_up", "w_conv", "w_down")}
    small_w = {n: args[n] for n in ("b_ada",) + tuple(n for n, _ in SMALL_WIDTHS)}
    mom_m = {n[2:]: (args[n][0] if args[n].ndim == 3 else args[n]) for n in args if n.startswith("m_")}
    mom_v = {n[2:]: (args[n][0] if args[n].ndim == 3 else args[n]) for n in args if n.startswith("v_")}

    xi, yi, ci = _position()
    q0 = 2 * xi + yi
    me = 4 * xi + 2 * yi + ci
    xs, tgt = x[0], loss_target[0]
    s = xs.shape[0]
    consts = _seg_consts()
    c_idx, qc_idx = jnp.reshape(ci, (1,)).astype(I32), jnp.stack([q0, ci]).astype(I32)

    def halves(g4):
        q, r, cc = g4.shape
        return g4.reshape(q, 2, r // 2, cc)

    own_first = [weights[n].astype(BF16) for n in ("w_in", "w_q_b", "w_kv_b")]
    own_later = [weights[n].astype(BF16) for n in ("w_o", "w_up", "w_down")]
    conv_cols = UP_W // N_CHIP
    ada_cols = w_ada.shape[2]
    b_shard = lax.dynamic_slice_in_dim(b_ada, q0 * ada_cols, ada_cols, axis=1)
    c_taps = jnp.concatenate([c, weights["w_conv"].reshape(1, 3 * conv_cols)], axis=1)
    c_taps_all, mod_all, tab, *gathered = _prologue(c_taps, weights["w_ada"], b_shard, positions.reshape(s, 1),
                                                    _rope_consts(), own_first)
    c_all = c_taps_all[:, 0, :D_MODEL]
    w_conv_f = c_taps_all[:, 0, D_MODEL:].reshape(N_CHIP, 2, 3, conv_cols)[:, 0]
    w_conv_f = jnp.transpose(w_conv_f, (1, 0, 2)).reshape(3, UP_W)
    mod_all = mod_all.reshape(N_CHIP, 2, N_DEV, ada_cols)
    mod = lax.dynamic_index_in_dim(lax.dynamic_index_in_dim(mod_all, ci, 1, False), me, 1, False)
    mod = mod.reshape(1, N_CHIP * ada_cols)
    sh1, sc1, g1, sh2, sc2, g2 = [mod[:, k * D_MODEL:(k + 1) * D_MODEL] for k in range(6)]
    w_in_f = _cols_from_shards(gathered[0])
    w_in_p = _pad_w_in(w_in_f)
    w_qb_p = _pad_w_qb(_cols_from_shards(gathered[1]))
    w_kvb_p = _pad_w_kvb(_cols_from_shards(gathered[2]))
    gains = _head_gains(g_mla_q_nope, g_mla_q_pe, g_mla_k_nope, g_mla_k_pe, g_dil_q, g_dil_k)

    h = _prenorm(xs, g_mix_norm, sc1, sh1, "prenorm")
    proj = _mm(h, w_in_p, "nn", F32, 512, P_COLS, "mm_in")
    ql, kvl = _latnorm(proj, g_q_lat, g_kv_lat)
    q_raw = _mm(ql, w_qb_p, "nn", F32, 1024, HEADS * LANE, "mm_qb")
    kv_raw = _mm(kvl, w_kvb_p, "nn", F32, 1024, HEADS * LANE + DIL_W, "mm_kvb")
    qm, km, vm, qd, kd, vd = _attn_prep(q_raw, kv_raw, proj, tab, gains, consts)
    scale_m, scale_d = (NOPE + ROPE) ** -0.5, DIL_DIM ** -0.5
    o_m, lse_m, got_up = _attn_fwd(qm, km, vm, True, scale_m, "attn_mla", gather=own_later[1:2])
    o_d, lse_d, got_o, got_down = _attn_fwd(qd, kd, vd, False, scale_d, "attn_dil", gather=[own_later[0], own_later[2]])
    gathered = [got_o, got_up, got_down]
    w_o_f = gathered[0].reshape(D_MODEL, D_MODEL)
    w_up_f = _cols_from_shards(gathered[1])
    w_down_f = gathered[2].reshape(D_FF, D_MODEL)
    mix_in = jnp.concatenate([o_m, o_d], axis=1)
    mix = _mm(mix_in, w_o_f, "nn", F32, 1024, D_MODEL, "mm_o")
    x1, h2 = _resid_prenorm(xs, mix, g1, g_ffn_norm, sc2, sh2)
    up = _mm(h2, w_up_f, "nn", F32, 1024, CONV_TILE, "mm_up")
    act = _conv_gate(up, w_conv_f, b_conv)
    ffn = _mm(act, w_down_f, "nn", F32, 512, D_MODEL, "mm_down")
    dy, dffn, dg2, loss_part = _final(x1, ffn, tgt, g2)

    da = _mm(dffn, w_down_f, "nt", F32, 1024, CONV_TILE, "mm_down_dx")
    gw_down = _mm(act, dffn, "tn", F32, 256, D_MODEL, "mm_down_dw")
    dup_g, dup_v, dbg, dbv, dwg, dwv = _gate_bwd(up, da, w_conv_f, b_conv)
    dup = jnp.concatenate([dup_g, dup_v], axis=1)
    early_names = ("w_up", "w_down", "w_o")
    gw_up = _mm(h2, dup, "tn", F32, 1024, CONV_TILE, "mm_up_dw", col_shards=True)
    early = [halves(gw_up), halves(gw_down.reshape(N_CHIP, D_FF // N_CHIP, D_MODEL))]
    dh2, *early_sib = _mm(dup, w_up_f, "nt", F32, 256, 512, "mm_up_dx", swap=early, b_outer=True)
    dx1, dmix, acc2 = _ffnnorm_bwd(dh2, x1, dy, mix, g_ffn_norm, sc2, g1)
    gw_o = _mm(mix_in, dmix, "tn", F32, 1024, D_MODEL, "mm_o_dw")
    early.append(halves(gw_o.reshape(N_CHIP, D_MODEL // N_CHIP, D_MODEL)))
    dmix_in, sib_o = _mm(dmix, w_o_f, "nt", F32, 512, D_MODEL, "mm_o_dx", swap=early[2:])
    early_sib.append(sib_o)
    early_sums = [_pair_sum(g, a, c_idx, "pair_sum_" + n) for g, a, n in zip(early, early_sib, early_names)]
    dqm, dkm, dvm, *early_recv = _attn_bwd(qm, km, vm, o_m, dmix_in, 0, lse_m, True, scale_m, "attn_mla_bwd",
                                           scatter=early_sums[:1])
    dqd, dkd, dvd, *early_recv_d = _attn_bwd(qd, kd, vd, o_d, dmix_in, DIL_W // LANE, lse_d, False, scale_d,
                                             "attn_dil_bwd", scatter=early_sums[1:])
    early_recv = early_recv + early_recv_d
    dq_raw, dkv_raw, dkpe_b, dqd_b, dkd_b, dvd_b, dgains = _attn_prep_bwd(
        dqm, dkm, dvm, dqd, dkd, dvd, q_raw, kv_raw, proj, tab, gains, consts)
    dql = _mm(dq_raw, w_qb_p, "nt", F32, 1024, Q_LORA, "mm_qb_dx")
    gw_qb = _unpad_w_qb(_mm(ql, dq_raw, "tn", F32, Q_LORA, HEADS * LANE, "mm_qb_dw"))
    dkvl = _mm(dkv_raw, w_kvb_p, "nt", F32, 1024, KV_LORA, "mm_kvb_dx")
    gw_kvb = _unpad_w_kvb(_mm(kvl, dkv_raw, "tn", F32, KV_LORA, HEADS * LANE + DIL_W, "mm_kvb_dw"))
    dqlat_b, dkvlat_b, dglat = _latnorm_bwd(dql, dkvl, proj, g_q_lat, g_kv_lat)
    dproj = jnp.concatenate([dqlat_b, dkvlat_b, dkpe_b[:, KPE_OFF:KPE_OFF + ROPE], dqd_b, dkd_b, dvd_b], axis=1)
    gw_in = _mm(h, dproj, "tn", F32, 512, IN_COLS, "mm_in_dw")
    late_names = ("w_in", "w_q_b", "w_kv_b")
    late = [halves(_cols_to_shards(gw_in)), halves(_cols_to_shards(gw_qb)), halves(_cols_to_shards(gw_kvb))]
    dh, *late_sib = _mm(dproj, w_in_f, "nt", F32, 512, D_MODEL, "mm_in_dx", swap=late)
    grad_x, acc1 = _mixnorm_bwd(dh, xs, dx1, g_mix_norm, sc1)

    packed = _pack_small(acc1, acc2, dg2, dglat, dgains, dbg, dbv, dwg, dwv, loss_part)
    late_sums = [_pair_sum(g, a, c_idx, "pair_sum_" + n) for g, a, n in zip(late, late_sib, late_names)]
    *late_recv, gathered_small = _scatter_and_gather(late_sums, packed, "rs_scatter_late")

    grad_b_ada, *small_grads, gconv_full, loss_sum = _sum_unpack(gathered_small)
    grads = {"b_ada": grad_b_ada}
    grads.update({n: g for (n, _), g in zip(SMALL_WIDTHS, small_grads)})
    shard_cols = UP_W // N_CHIP
    grads["w_conv"] = lax.dynamic_slice_in_dim(gconv_full, q0 * shard_cols, shard_cols, axis=1)
    dmod_all = gathered_small[:, 0, :6 * D_MODEL]
    grads["w_ada"] = _ada_bwd(c_all, lax.dynamic_slice_in_dim(dmod_all, q0 * ada_cols, ada_cols, axis=1))

    big_names = late_names + early_names
    half_sums = [_shard_sum(p, b, qc_idx, "shard_sum_" + n)
                 for p, b, n in zip(late_sums + early_sums, list(late_recv) + list(early_recv), big_names)]
    for n, full in zip(big_names, _join_halves(half_sums)):
        grads[n] = full.reshape(2 * full.shape[1], full.shape[2])

    delta, new_m, new_v = {}, {}, {}
    for n in ("w_ada", "w_in", "w_q_b", "w_kv_b", "w_o", "w_up", "w_conv", "w_down"):
        operands = (weights[n], grads[n], mom_m[n], mom_v[n])
        flipped = n in ("w_in", "w_q_b")
        if flipped:
            operands = [jnp.swapaxes(a, 0, 1) for a in operands]
            grads[n] = jnp.swapaxes(operands[1], 0, 1)
        if n == "w_ada":
            operands = _in_hbm(*operands)
        delta[n], new_m[n], new_v[n] = _adamw(*operands, "adamw_" + n)
        if flipped:
            delta[n], new_m[n], new_v[n] = (jnp.swapaxes(a, 0, 1) for a in (delta[n], new_m[n], new_v[n]))
    vec_names = ("b_ada",) + tuple(n for n, _ in SMALL_WIDTHS)
    sd, sm, sv = _adamw_vectors(*[[d_[n] for n in vec_names] for d_ in (small_w, grads, mom_m, mom_v)])
    for k, n in enumerate(vec_names):
        delta[n], new_m[n], new_v[n] = sd[k], sm[k], sv[k]

    loss = loss_sum[0, 0]
    order = ("w_ada", "b_ada", "g_mix_norm", "w_in", "g_q_lat", "w_q_b", "g_kv_lat", "w_kv_b", "g_mla_q_nope", "g_mla_q_pe",
             "g_mla_k_nope", "g_mla_k_pe", "g_dil_q", "g_dil_k", "w_o", "g_ffn_norm", "w_up", "w_conv", "b_conv", "w_down")
    lead = lambda n, z: z[None] if n.startswith("w_") else z
    outs = [loss, grad_x[None]]
    for d_ in (grads, delta, new_m, new_v):
        outs += [lead(n, d_[n]) for n in order]
    return tuple(outs)
```

```python
import functools

import numpy as np
import jax
import jax.numpy as jnp
from jax import lax
from jax.experimental import pallas as pl
from jax.experimental.pallas import tpu as pltpu

F32 = jnp.float32
BF16 = jnp.bfloat16
I32 = jnp.int32

D_MODEL = 1024
HEADS = 8
NOPE = 64
ROPE = 32
Q_LORA = 512
KV_LORA = 256
DIL_DIM = 64
DIL_W = HEADS * DIL_DIM
D_FF = 2816
UP_W = 2 * D_FF
IN_COLS = Q_LORA + KV_LORA + ROPE + 3 * DIL_W
ROPE_THETA = 10000.0
EPS = 1e-6
NEG_INF = -1e30
N_DEV = 8
N_CHIP = 4

ADAM_LR = 0.001
ADAM_B1 = 0.9
ADAM_B2 = 0.999
ADAM_EPS = 1e-08
ADAM_WD = 0.01
ADAM_STEP = 10

LANE = 128
ROW_TILE = 256
NORM_TILE = 512
ATT_TQ = 512
ATT_TK = 256
ATT_TK_BWD = 512
LOG2E = 1.4426950408889634
LN2 = 0.6931471805599453
VMEM_CAP = 56 * 1024 * 1024
VMEM_FLOOR = 32 * 1024 * 1024

P_QLAT, P_QD, P_KD, P_VD, P_KVLAT, P_KPE = 0, 512, 1024, 1536, 2048, 2304
P_COLS = 2432
KPE_OFF = 64

NN = (((1,), (0,)), ((), ()))
NT = (((1,), (1,)), ((), ()))
TN = (((0,), (0,)), ((), ()))
HIGHEST = lax.Precision.HIGHEST
MESH = pl.DeviceIdType.MESH


def _params(sem=None, est_bytes=0):
    limit = int(min(max(2 * est_bytes + (4 << 20), VMEM_FLOOR), VMEM_CAP))
    if sem is None:
        return pltpu.CompilerParams(vmem_limit_bytes=limit)
    return pltpu.CompilerParams(dimension_semantics=sem, vmem_limit_bytes=limit)


def _nbytes(shape, dtype):
    return int(np.prod(shape)) * jnp.dtype(dtype).itemsize


def _in_hbm(*xs):
    return [pltpu.with_memory_space_constraint(x, pltpu.HBM) for x in xs]


def _mm(a, b, dims, out_dtype, tm, tn, name, col_shards=False, swap=(), b_outer=False):
    def spec(block, index):
        if b_outer:
            return pl.BlockSpec(block, lambda g0, g1: index(g1, g0))
        return pl.BlockSpec(block, index)

    if dims == "nn":
        (m, k), (k2, n) = a.shape, b.shape
        a_spec = spec((tm, k), lambda i, j: (i, 0))
        b_spec = spec((k, tn), lambda i, j: (0, j))
        dn = NN
    elif dims == "nt":
        (m, k), (n, k2) = a.shape, b.shape
        a_spec = spec((tm, k), lambda i, j: (i, 0))
        b_spec = spec((tn, k), lambda i, j: (j, 0))
        dn = NT
    else:
        (k, m), (k2, n) = a.shape, b.shape
        a_spec = spec((k, tm), lambda i, j: (0, i))
        b_spec = spec((k, tn), lambda i, j: (0, j))
        dn = TN
    assert k == k2 and m % tm == 0 and n % tn == 0, (name, a.shape, b.shape, tm, tn)

    nw = len(swap)
    grid = (n // tn, m // tm) if b_outer else (m // tm, n // tn)

    def body(*refs):
        a_ref, b_ref, o_ref = refs[0], refs[1], refs[2 + nw]
        comm = (refs[2:2 + nw], refs[3 + nw:3 + 2 * nw]) + tuple(refs[3 + 2 * nw:])
        if nw:
            @pl.when((pl.program_id(0) == 0) & (pl.program_id(1) == 0))
            def _():
                _PairSwap(*comm).start()

        o_ref[...] = lax.dot_general(a_ref[...], b_ref[...], dn, preferred_element_type=F32).astype(o_ref.dtype)

        if nw:
            @pl.when((pl.program_id(0) == grid[0] - 1) & (pl.program_id(1) == grid[1] - 1))
            def _():
                _PairSwap(*comm).finish()

    est = _nbytes((tm, k), a.dtype) + _nbytes((tn, k), b.dtype) + _nbytes((tm, tn), F32) + _nbytes((tm, tn), out_dtype)
    if col_shards:
        out_spec = spec((None, tm, tn), lambda i, j: (j, i, 0))
        out_shape = jax.ShapeDtypeStruct((n // tn, m, tn), out_dtype)
    else:
        out_spec = spec((tm, tn), lambda i, j: (i, j))
        out_shape = jax.ShapeDtypeStruct((m, n), out_dtype)
    out = pl.pallas_call(
        body, name=name, grid=grid,
        in_specs=[a_spec, b_spec] + [ANY] * nw,
        out_specs=[out_spec] + [ANY] * nw,
        out_shape=[out_shape] + _PairSwap.out_shapes(swap),
        scratch_shapes=_PairSwap.semaphores(nw) if nw else [],
        compiler_params=_params(("arbitrary", "arbitrary") if nw else ("parallel", "parallel"), est),
    )(a, b, *swap)
    return out if nw else out[0]


def _seg_consts():
    seg_q = np.zeros((HEADS * LANE, LANE), np.float32)
    inv_q = np.zeros((1, LANE), np.float32)
    seg_k = np.zeros((HEADS * LANE, LANE), np.float32)
    inv_k = np.zeros((1, LANE), np.float32)
    seg_d = np.zeros((DIL_W, LANE), np.float32)
    inv_d = np.zeros((1, LANE), np.float32)
    for h in range(HEADS):
        seg_q[h * LANE:h * LANE + NOPE, 2 * h] = 1.0
        seg_q[h * LANE + NOPE:h * LANE + NOPE + ROPE, 2 * h + 1] = 1.0
        inv_q[0, 2 * h], inv_q[0, 2 * h + 1] = 1.0 / NOPE, 1.0 / ROPE
        seg_k[h * LANE:h * LANE + NOPE, h] = 1.0
        inv_k[0, h] = 1.0 / NOPE
        seg_d[h * DIL_DIM:(h + 1) * DIL_DIM, h] = 1.0
        inv_d[0, h] = 1.0 / DIL_DIM
    fold_q = np.tile(np.eye(LANE, dtype=np.float32), (HEADS, 1))
    fold_d = np.zeros((DIL_W, LANE), np.float32)
    fold_d[np.arange(DIL_W), np.arange(DIL_W) % DIL_DIM] = 1.0
    j = lambda v: jnp.asarray(v)
    b = lambda v: jnp.asarray(v, dtype=BF16)
    return dict(seg_q=b(seg_q), exp_q=b(seg_q.T.copy()), inv_q=j(inv_q), seg_k=b(seg_k), exp_k=b(seg_k.T.copy()),
                inv_k=j(inv_k), seg_d=b(seg_d), exp_d=b(seg_d.T.copy()), inv_d=j(inv_d), fold_q=j(fold_q), fold_d=j(fold_d))


def _rope_consts():
    inv_d = jnp.power(ROPE_THETA, -2.0 * jnp.arange(DIL_DIM // 2, dtype=F32) / DIL_DIM)
    inv_q = jnp.power(ROPE_THETA, -2.0 * jnp.arange(ROPE // 2, dtype=F32) / ROPE)
    lanes = np.arange(LANE)
    freq_d = inv_d[lanes % (DIL_DIM // 2)]
    in_pe = (lanes >= KPE_OFF) & (lanes < KPE_OFF + ROPE)
    freq_q = jnp.where(jnp.asarray(in_pe), inv_q[(lanes - KPE_OFF) % (ROPE // 2)], 0.0)
    sign_d = np.where(lanes % DIL_DIM < DIL_DIM // 2, -1.0, 1.0).astype(np.float32)
    sign_q = np.where(in_pe, np.where((lanes - KPE_OFF) < ROPE // 2, -1.0, 1.0), 0.0).astype(np.float32)
    zeros, ones = np.zeros(LANE, np.float32), np.ones(LANE, np.float32)
    freq = jnp.concatenate([freq_d, freq_d, freq_q, freq_q])[None, :]
    csel = jnp.asarray(np.concatenate([ones, zeros, ones, zeros]))[None, :]
    ssel = jnp.asarray(np.concatenate([zeros, sign_d, zeros, sign_q]))[None, :]
    return freq, csel, ssel


def _full(shape):
    return pl.BlockSpec(shape, lambda *_: (0,) * len(shape))


def _tile_lanes(x, n):
    return jnp.concatenate([x] * n, axis=1)


def _rms(x):
    return lax.rsqrt(jnp.mean(x * x, axis=-1, keepdims=True) + EPS)


def _prenorm(x, gain, scale, shift, name):
    s, d = x.shape

    def body(x_ref, g_ref, sc_ref, sh_ref, h_ref):
        xv = x_ref[...]
        h = (xv * _rms(xv)) * g_ref[...] * (1.0 + sc_ref[...]) + sh_ref[...]
        h_ref[...] = h.astype(BF16)

    row = pl.BlockSpec((NORM_TILE, d), lambda i: (i, 0))
    return pl.pallas_call(
        body, name=name, grid=(s // NORM_TILE,),
        in_specs=[row, _full((1, d)), _full((1, d)), _full((1, d))],
        out_specs=row, out_shape=jax.ShapeDtypeStruct((s, d), BF16),
        compiler_params=_params(("parallel",)),
    )(x, gain, scale, shift)


def _latnorm(proj, g_q, g_kv):
    s = proj.shape[0]

    def body(q_ref, kv_ref, gq_ref, gkv_ref, ql_ref, kvl_ref):
        q, kv = q_ref[...], kv_ref[...]
        ql_ref[...] = ((q * _rms(q)) * gq_ref[...]).astype(BF16)
        kvl_ref[...] = ((kv * _rms(kv)) * gkv_ref[...]).astype(BF16)

    return pl.pallas_call(
        body, name="latnorm", grid=(s // NORM_TILE,),
        in_specs=[pl.BlockSpec((NORM_TILE, Q_LORA), lambda i: (i, P_QLAT // Q_LORA)),
                  pl.BlockSpec((NORM_TILE, KV_LORA), lambda i: (i, P_KVLAT // KV_LORA)),
                  _full((1, Q_LORA)), _full((1, KV_LORA))],
        out_specs=[pl.BlockSpec((NORM_TILE, Q_LORA), lambda i: (i, 0)), pl.BlockSpec((NORM_TILE, KV_LORA), lambda i: (i, 0))],
        out_shape=[jax.ShapeDtypeStruct((s, Q_LORA), BF16), jax.ShapeDtypeStruct((s, KV_LORA), BF16)],
        compiler_params=_params(("parallel",)),
    )(proj, proj, g_q, g_kv)


def _dot01(v, mat01):
    hi = v.astype(BF16)
    lo = (v - hi.astype(F32)).astype(BF16)
    return jnp.dot(hi, mat01, preferred_element_type=F32) + jnp.dot(lo, mat01, preferred_element_type=F32)


def _seg_rinv(x, seg, exp, inv):
    r = lax.rsqrt(_dot01(x * x, seg) * inv + EPS)
    return _dot01(r, exp)


def _seg_mean(v, seg, exp, inv):
    return _dot01(_dot01(v, seg) * inv, exp)


def _swap_halves(x, half):
    n = x.shape[1]
    lane = lax.broadcasted_iota(I32, (1, n), 1)
    first = (lane & (2 * half - 1)) < half
    return jnp.where(first, pltpu.roll(x, n - half, 1), pltpu.roll(x, half, 1))


def _rope(x, cos, sin_signed, half):
    return x * cos + _swap_halves(x, half) * sin_signed


def _rope_bwd(dy, cos, sin_signed, half):
    return dy * cos + _swap_halves(dy * sin_signed, half)


def _pe_lane_mask(n):
    lane = lax.broadcasted_iota(I32, (1, n), 1) & (LANE - 1)
    return (lane >= KPE_OFF) & (lane < KPE_OFF + ROPE)


def _attn_prep(q_raw, kv_raw, proj, tab, gains, consts):
    s = q_raw.shape[0]
    hw = HEADS * LANE

    def body(q_ref, kv_ref, kpe_ref, qd_ref, kd_ref, vd_ref, tab_ref,
             gq_ref, gk_ref, gkpe_ref, gdq_ref, gdk_ref,
             segq_ref, expq_ref, invq_ref, segk_ref, expk_ref, invk_ref, segd_ref, expd_ref, invd_ref,
             qm_ref, km_ref, vm_ref, qdo_ref, kdo_ref, vdo_ref):
        tab_v = tab_ref[...]
        cos_d, sin_d = _tile_lanes(tab_v[:, 0:LANE], DIL_W // LANE), _tile_lanes(tab_v[:, LANE:2 * LANE], DIL_W // LANE)
        cos_q1, sin_q1 = tab_v[:, 2 * LANE:3 * LANE], tab_v[:, 3 * LANE:4 * LANE]
        cos_q, sin_q = _tile_lanes(cos_q1, HEADS), _tile_lanes(sin_q1, HEADS)

        q = q_ref[...]
        qn = q * _seg_rinv(q, segq_ref[...], expq_ref[...], invq_ref[...]) * gq_ref[...]
        qm_ref[...] = _rope(qn, cos_q, sin_q, ROPE // 2).astype(BF16)

        kv = kv_ref[...]
        kp = kv[:, :hw]
        kn = kp * _seg_rinv(kp, segk_ref[...], expk_ref[...], invk_ref[...]) * gk_ref[...]
        kpe = kpe_ref[...]
        r_pe = lax.rsqrt(jnp.sum(kpe * kpe, axis=-1, keepdims=True) * (1.0 / ROPE) + EPS)
        kpe_r = _rope(kpe * r_pe * gkpe_ref[...], cos_q1, sin_q1, ROPE // 2)
        km_ref[...] = (kn + _tile_lanes(kpe_r, HEADS)).astype(BF16)
        vm_ref[...] = kv[:, hw:].astype(BF16)

        qd = qd_ref[...]
        qdn = qd * _seg_rinv(qd, segd_ref[...], expd_ref[...], invd_ref[...]) * gdq_ref[...]
        qdo_ref[...] = _rope(qdn, cos_d, sin_d, DIL_DIM // 2).astype(BF16)
        kd = kd_ref[...]
        kdn = kd * _seg_rinv(kd, segd_ref[...], expd_ref[...], invd_ref[...]) * gdk_ref[...]
        kdo_ref[...] = _rope(kdn, cos_d, sin_d, DIL_DIM // 2).astype(BF16)
        vdo_ref[...] = vd_ref[...].astype(BF16)

    t = ROW_TILE
    row = lambda w, cb=0: pl.BlockSpec((t, w), lambda i: (i, cb))
    c = consts
    return pl.pallas_call(
        body, name="attn_prep", grid=(s // t,),
        in_specs=[row(hw), row(hw + DIL_W), row(LANE, P_KPE // LANE), row(DIL_W, P_QD // DIL_W), row(DIL_W, P_KD // DIL_W),
                  row(DIL_W, P_VD // DIL_W), row(4 * LANE),
                  _full((1, hw)), _full((1, hw)), _full((1, LANE)), _full((1, DIL_W)), _full((1, DIL_W)),
                  _full((hw, LANE)), _full((LANE, hw)), _full((1, LANE)), _full((hw, LANE)), _full((LANE, hw)), _full((1, LANE)),
                  _full((DIL_W, LANE)), _full((LANE, DIL_W)), _full((1, LANE))],
        out_specs=[row(hw), row(hw), row(DIL_W), row(DIL_W), row(DIL_W), row(DIL_W)],
        out_shape=[jax.ShapeDtypeStruct((s, hw), BF16), jax.ShapeDtypeStruct((s, hw), BF16)]
        + [jax.ShapeDtypeStruct((s, DIL_W), BF16)] * 4,
        compiler_params=_params(("parallel",), 24 << 20),
    )(*_in_hbm(q_raw, kv_raw, proj, proj, proj, proj), tab, gains["q"], gains["k"], gains["kpe"], gains["dq"], gains["dk"],
      c["seg_q"], c["exp_q"], c["inv_q"], c["seg_k"], c["exp_k"], c["inv_k"], c["seg_d"], c["exp_d"], c["inv_d"])


def _attn_prep_bwd(dqm, dkm, dvm, dqd, dkd, dvd, q_raw, kv_raw, proj, tab, gains, consts):
    s = q_raw.shape[0]
    hw = HEADS * LANE
    n_steps = s // ROW_TILE

    def body(dqm_ref, dkm_ref, dvm_ref, dqd_ref, dkd_ref, dvd_ref, q_ref, kv_ref, kpe_ref, qd_ref, kd_ref, tab_ref,
             gq_ref, gk_ref, gkpe_ref, gdq_ref, gdk_ref,
             segq_ref, expq_ref, invq_ref, segk_ref, expk_ref, invk_ref, segd_ref, expd_ref, invd_ref, foldq_ref, foldd_ref,
             dq_ref, dkv_ref, dkpe_ref, dqdo_ref, dkdo_ref, dvdo_ref, dg_ref, acc_ref):
        i = pl.program_id(0)

        @pl.when(i == 0)
        def _():
            acc_ref[...] = jnp.zeros_like(acc_ref)

        tab_v = tab_ref[...]
        cos_d, sin_d = _tile_lanes(tab_v[:, 0:LANE], DIL_W // LANE), _tile_lanes(tab_v[:, LANE:2 * LANE], DIL_W // LANE)
        cos_q1, sin_q1 = tab_v[:, 2 * LANE:3 * LANE], tab_v[:, 3 * LANE:4 * LANE]
        cos_q, sin_q = _tile_lanes(cos_q1, HEADS), _tile_lanes(sin_q1, HEADS)

        def norm_bwd(x, dyg, gain, seg, exp, inv):
            rinv = _seg_rinv(x, seg, exp, inv)
            xn = x * rinv
            dxn = dyg * gain
            dx = rinv * (dxn - xn * _seg_mean(dxn * xn, seg, exp, inv))
            return dx, jnp.sum(dyg * xn, axis=0, keepdims=True)

        dq, gq_l = norm_bwd(q_ref[...], _rope_bwd(dqm_ref[...], cos_q, sin_q, ROPE // 2), gq_ref[...],
                            segq_ref[...], expq_ref[...], invq_ref[...])
        dq_ref[...] = dq.astype(BF16)

        dkm = dkm_ref[...]
        kv = kv_ref[...]
        dkp, gk_l = norm_bwd(kv[:, :hw], dkm, gk_ref[...], segk_ref[...], expk_ref[...], invk_ref[...])
        dkv_ref[:, :hw] = dkp.astype(BF16)
        dkv_ref[:, hw:] = dvm_ref[...].astype(BF16)

        dkpe_r = dkm[:, 0:LANE]
        for h in range(1, HEADS):
            dkpe_r = dkpe_r + dkm[:, h * LANE:(h + 1) * LANE]
        dkpe_r = jnp.where(_pe_lane_mask(LANE), dkpe_r, 0.0)
        dyg = _rope_bwd(dkpe_r, cos_q1, sin_q1, ROPE // 2)
        kpe = kpe_ref[...]
        r_pe = lax.rsqrt(jnp.sum(kpe * kpe, axis=-1, keepdims=True) * (1.0 / ROPE) + EPS)
        xn = kpe * r_pe
        dxn = dyg * gkpe_ref[...]
        dkpe = r_pe * (dxn - xn * (jnp.sum(dxn * xn, axis=-1, keepdims=True) * (1.0 / ROPE)))
        dkpe_ref[...] = dkpe.astype(BF16)
        gkpe_l = jnp.sum(dyg * xn, axis=0, keepdims=True)

        dqd_v, gdq_l = norm_bwd(qd_ref[...], _rope_bwd(dqd_ref[...], cos_d, sin_d, DIL_DIM // 2), gdq_ref[...],
                                segd_ref[...], expd_ref[...], invd_ref[...])
        dqdo_ref[...] = dqd_v.astype(BF16)
        dkd_v, gdk_l = norm_bwd(kd_ref[...], _rope_bwd(dkd_ref[...], cos_d, sin_d, DIL_DIM // 2), gdk_ref[...],
                                segd_ref[...], expd_ref[...], invd_ref[...])
        dkdo_ref[...] = dkd_v.astype(BF16)
        dvdo_ref[...] = dvd_ref[...].astype(BF16)

        acc_ref[0:1, :] += gq_l
        acc_ref[1:2, :] += gk_l
        acc_ref[2:3, 0:LANE] += gkpe_l
        acc_ref[3:4, 0:DIL_W] += gdq_l
        acc_ref[4:5, 0:DIL_W] += gdk_l

        @pl.when(i == n_steps - 1)
        def _():
            acc = acc_ref[...]
            fq = jnp.dot(acc, foldq_ref[...], precision=HIGHEST, preferred_element_type=F32)
            fd = jnp.dot(acc[:, 0:DIL_W], foldd_ref[...], precision=HIGHEST, preferred_element_type=F32)
            rows = lax.broadcasted_iota(I32, (8, LANE), 0)
            base = jnp.where(rows < 2, fq, jnp.where(rows == 2, acc[:, 0:LANE], fd))
            at0 = pltpu.roll(base, LANE - KPE_OFF, 1)
            dg_ref[...] = jnp.where(rows == 5, pltpu.roll(at0, 5, 0), jnp.where(rows == 2, at0, base))

    t = ROW_TILE
    row = lambda w, cb=0: pl.BlockSpec((t, w), lambda i: (i, cb))
    c = consts
    return pl.pallas_call(
        body, name="attn_prep_bwd", grid=(n_steps,),
        in_specs=[row(hw), row(hw), row(DIL_W), row(DIL_W), row(DIL_W), row(DIL_W),
                  row(hw), row(hw + DIL_W), row(LANE, P_KPE // LANE), row(DIL_W, P_QD // DIL_W), row(DIL_W, P_KD // DIL_W),
                  row(4 * LANE),
                  _full((1, hw)), _full((1, hw)), _full((1, LANE)), _full((1, DIL_W)), _full((1, DIL_W)),
                  _full((hw, LANE)), _full((LANE, hw)), _full((1, LANE)), _full((hw, LANE)), _full((LANE, hw)), _full((1, LANE)),
                  _full((DIL_W, LANE)), _full((LANE, DIL_W)), _full((1, LANE)), _full((hw, LANE)), _full((DIL_W, LANE))],
        out_specs=[row(hw), row(hw + DIL_W), row(LANE), row(DIL_W), row(DIL_W), row(DIL_W), _full((8, LANE))],
        out_shape=[jax.ShapeDtypeStruct((s, hw), BF16), jax.ShapeDtypeStruct((s, hw + DIL_W), BF16),
                   jax.ShapeDtypeStruct((s, LANE), BF16)] + [jax.ShapeDtypeStruct((s, DIL_W), BF16)] * 3
        + [jax.ShapeDtypeStruct((8, LANE), F32)],
        scratch_shapes=[pltpu.VMEM((8, hw), F32)],
        compiler_params=_params(("arbitrary",), 28 << 20),
    )(*_in_hbm(dqm, dkm, dvm, dqd, dkd, dvd, q_raw, kv_raw, proj, proj, proj), tab,
      gains["q"], gains["k"], gains["kpe"], gains["dq"], gains["dk"],
      c["seg_q"], c["exp_q"], c["inv_q"], c["seg_k"], c["exp_k"], c["inv_k"], c["seg_d"], c["exp_d"], c["inv_d"],
      c["fold_q"], c["fold_d"])


def _latnorm_bwd(dql, dkvl, proj, g_q, g_kv):
    s = proj.shape[0]
    n_steps = s // NORM_TILE

    def body(dql_ref, dkvl_ref, q_ref, kv_ref, gq_ref, gkv_ref, dq_ref, dkv_ref, dg_ref):
        i = pl.program_id(0)

        @pl.when(i == 0)
        def _():
            dg_ref[...] = jnp.zeros_like(dg_ref)

        def one(x, dyg, gain):
            r = _rms(x)
            xn = x * r
            dxn = dyg * gain
            dx = r * (dxn - xn * jnp.mean(dxn * xn, axis=-1, keepdims=True))
            return dx, jnp.sum(dyg * xn, axis=0, keepdims=True)

        dq, gq_l = one(q_ref[...], dql_ref[...], gq_ref[...])
        dkv, gkv_l = one(kv_ref[...], dkvl_ref[...], gkv_ref[...])
        dq_ref[...] = dq.astype(BF16)
        dkv_ref[...] = dkv.astype(BF16)
        dg_ref[0:1, :] += gq_l
        dg_ref[1:2, 0:KV_LORA] += gkv_l

    t = NORM_TILE
    return pl.pallas_call(
        body, name="latnorm_bwd", grid=(n_steps,),
        in_specs=[pl.BlockSpec((t, Q_LORA), lambda i: (i, 0)), pl.BlockSpec((t, KV_LORA), lambda i: (i, 0)),
                  pl.BlockSpec((t, Q_LORA), lambda i: (i, P_QLAT // Q_LORA)),
                  pl.BlockSpec((t, KV_LORA), lambda i: (i, P_KVLAT // KV_LORA)),
                  _full((1, Q_LORA)), _full((1, KV_LORA))],
        out_specs=[pl.BlockSpec((t, Q_LORA), lambda i: (i, 0)), pl.BlockSpec((t, KV_LORA), lambda i: (i, 0)), _full((8, Q_LORA))],
        out_shape=[jax.ShapeDtypeStruct((s, Q_LORA), BF16), jax.ShapeDtypeStruct((s, KV_LORA), BF16),
                   jax.ShapeDtypeStruct((8, Q_LORA), F32)],
        compiler_params=_params(("arbitrary",)),
    )(dql, dkvl, proj, proj, g_q, g_kv)


def _resid_prenorm(x, mix, g1, gain, scale, shift):
    s, d = x.shape

    def body(x_ref, mix_ref, g1_ref, g_ref, sc_ref, sh_ref, x1_ref, h_ref):
        x1 = x_ref[...] + g1_ref[...] * mix_ref[...]
        x1_ref[...] = x1
        h_ref[...] = ((x1 * _rms(x1)) * g_ref[...] * (1.0 + sc_ref[...]) + sh_ref[...]).astype(BF16)

    row = pl.BlockSpec((NORM_TILE, d), lambda i: (i, 0))
    vec = _full((1, d))
    return pl.pallas_call(
        body, name="resid_prenorm", grid=(s // NORM_TILE,),
        in_specs=[row, row, vec, vec, vec, vec], out_specs=[row, row],
        out_shape=[jax.ShapeDtypeStruct((s, d), F32), jax.ShapeDtypeStruct((s, d), BF16)],
        compiler_params=_params(("parallel",)),
    )(x, mix, g1, gain, scale, shift)


CONV_TILE = 1408
HALO = 8


def _shift_down(x, halo, k):
    t = x.shape[0]
    row = lax.broadcasted_iota(I32, (t, 1), 0)
    out = pltpu.roll(x, k, 0)
    for r in range(k):
        out = jnp.where(row == r, halo[HALO - k + r:HALO - k + r + 1, :], out)
    return out


def _shift_up(x, halo, k):
    t = x.shape[0]
    row = lax.broadcasted_iota(I32, (t, 1), 0)
    out = pltpu.roll(x, t - k, 0)
    for r in range(k):
        out = jnp.where(row == t - k + r, halo[r:r + 1, :], out)
    return out


def _conv_fwd(x, halo, w, b):
    p1, p2 = _shift_down(x, halo, 1), _shift_down(x, halo, 2)
    u = b + p2 * w[0:1, :]
    u = u + p1 * w[1:2, :]
    u = u + x * w[2:3, :]
    return u, p1, p2


def _sigmoid(x):
    return 0.5 * jnp.tanh(0.5 * x) + 0.5


def _conv_gate(up, w_conv, b_conv):
    s = up.shape[0]
    t = ROW_TILE
    nj = D_FF // CONV_TILE
    hb = t // HALO

    def body(g_ref, v_ref, gh_ref, vh_ref, wg_ref, wv_ref, bg_ref, bv_ref, a_ref):
        live = (pl.program_id(0) > 0).astype(F32)
        ug, _, _ = _conv_fwd(g_ref[...], gh_ref[...] * live, wg_ref[...], bg_ref[...])
        uv, _, _ = _conv_fwd(v_ref[...], vh_ref[...] * live, wv_ref[...], bv_ref[...])
        a_ref[...] = (ug * _sigmoid(ug) * uv).astype(BF16)

    main = lambda off: pl.BlockSpec((t, CONV_TILE), lambda i, j: (i, j + off))
    halo = lambda off: pl.BlockSpec((HALO, CONV_TILE), lambda i, j: (jnp.maximum(i * hb - 1, 0), j + off))
    wsp = lambda off: pl.BlockSpec((3, CONV_TILE), lambda i, j: (0, j + off))
    bsp = lambda off: pl.BlockSpec((1, CONV_TILE), lambda i, j: (0, j + off))
    return pl.pallas_call(
        body, name="conv_gate", grid=(s // t, nj),
        in_specs=[main(0), main(nj), halo(0), halo(nj), wsp(0), wsp(nj), bsp(0), bsp(nj)],
        out_specs=pl.BlockSpec((t, CONV_TILE), lambda i, j: (i, j)),
        out_shape=jax.ShapeDtypeStruct((s, D_FF), BF16),
        compiler_params=_params(("parallel", "parallel"), 12 << 20),
    )(up, up, up, up, w_conv, w_conv, b_conv, b_conv)


def _gate_bwd(up, da, w_conv, b_conv):
    s = up.shape[0]
    t = ROW_TILE
    nj = D_FF // CONV_TILE
    hb = t // HALO
    n_i = s // t

    def body(g_ref, v_ref, gh_ref, vh_ref, gn_ref, vn_ref, da_ref, dan_ref, wg_ref, wv_ref, bg_ref, bv_ref,
             dupg_ref, dupv_ref, dbg_ref, dbv_ref, dwg_ref, dwv_ref):
        i = pl.program_id(1)

        @pl.when(i == 0)
        def _():
            for r in (dbg_ref, dbv_ref, dwg_ref, dwv_ref):
                r[...] = jnp.zeros_like(r)

        def d_gate(ug, uv, da_v):
            sg = _sigmoid(ug)
            return da_v * uv * (sg * (1.0 + ug * (1.0 - sg))), da_v * (ug * sg)

        live = (i > 0).astype(F32)
        xg, xv = g_ref[...], v_ref[...]
        wg, wv = wg_ref[...], wv_ref[...]
        ug, g1, g2 = _conv_fwd(xg, gh_ref[...] * live, wg, bg_ref[...])
        uv, v1, v2 = _conv_fwd(xv, vh_ref[...] * live, wv, bv_ref[...])
        dug, duv = d_gate(ug, uv, da_ref[...])

        more = (i < n_i - 1).astype(F32)
        ug_n, _, _ = _conv_fwd(gn_ref[...], xg[t - HALO:, :], wg, bg_ref[...])
        uv_n, _, _ = _conv_fwd(vn_ref[...], xv[t - HALO:, :], wv, bv_ref[...])
        dug_n, duv_n = d_gate(ug_n, uv_n, dan_ref[...] * more)

        def conv_t(du, du_n, w):
            return du * w[2:3, :] + _shift_up(du, du_n, 1) * w[1:2, :] + _shift_up(du, du_n, 2) * w[0:1, :]

        dupg_ref[...] = conv_t(dug, dug_n, wg).astype(BF16)
        dupv_ref[...] = conv_t(duv, duv_n, wv).astype(BF16)
        csum = lambda z: jnp.sum(z, axis=0, keepdims=True)
        dbg_ref[...] += csum(dug)
        dbv_ref[...] += csum(duv)
        dwg_ref[0:1, :] += csum(dug * g2)
        dwg_ref[1:2, :] += csum(dug * g1)
        dwg_ref[2:3, :] += csum(dug * xg)
        dwv_ref[0:1, :] += csum(duv * v2)
        dwv_ref[1:2, :] += csum(duv * v1)
        dwv_ref[2:3, :] += csum(duv * xv)

    last_halo = s // HALO - 1
    main = lambda off: pl.BlockSpec((t, CONV_TILE), lambda j, i: (i, j + off))
    halo = lambda off: pl.BlockSpec((HALO, CONV_TILE), lambda j, i: (jnp.maximum(i * hb - 1, 0), j + off))
    nxt = lambda off: pl.BlockSpec((HALO, CONV_TILE), lambda j, i: (jnp.minimum((i + 1) * hb, last_halo), j + off))
    wsp = lambda off: pl.BlockSpec((3, CONV_TILE), lambda j, i: (0, j + off))
    bsp = lambda off: pl.BlockSpec((1, CONV_TILE), lambda j, i: (0, j + off))
    outs = pl.pallas_call(
        body, name="gate_bwd", grid=(nj, n_i),
        in_specs=[main(0), main(nj), halo(0), halo(nj), nxt(0), nxt(nj), main(0), nxt(0),
                  wsp(0), wsp(nj), bsp(0), bsp(nj)],
        out_specs=[main(0), main(0),
                   pl.BlockSpec((1, CONV_TILE), lambda j, i: (0, j)), pl.BlockSpec((1, CONV_TILE), lambda j, i: (0, j)),
                   pl.BlockSpec((3, CONV_TILE), lambda j, i: (0, j)), pl.BlockSpec((3, CONV_TILE), lambda j, i: (0, j))],
        out_shape=[jax.ShapeDtypeStruct((s, D_FF), BF16), jax.ShapeDtypeStruct((s, D_FF), BF16),
                   jax.ShapeDtypeStruct((1, D_FF), F32), jax.ShapeDtypeStruct((1, D_FF), F32),
                   jax.ShapeDtypeStruct((3, D_FF), F32), jax.ShapeDtypeStruct((3, D_FF), F32)],
        compiler_params=_params(("parallel", "arbitrary"), 24 << 20),
    )(up, up, up, up, up, up, da, da, w_conv, w_conv, b_conv, b_conv)
    return outs


def _final(x1, ffn, tgt, g2):
    s, d = x1.shape
    n_steps = s // NORM_TILE

    def body(x1_ref, f_ref, t_ref, g2_ref, dy_ref, df_ref, dg2_ref, loss_ref, lacc_ref):
        i = pl.program_id(0)

        @pl.when(i == 0)
        def _():
            dg2_ref[...] = jnp.zeros_like(dg2_ref)
            lacc_ref[...] = jnp.zeros_like(lacc_ref)

        f = f_ref[...]
        e = x1_ref[...] + g2_ref[...] * f - t_ref[...]
        dy = e * (1.0 / d)
        dy_ref[...] = dy
        df_ref[...] = (dy * g2_ref[...]).astype(BF16)
        dg2_ref[...] += jnp.sum(dy * f, axis=0, keepdims=True)
        lacc_ref[...] += jnp.sum(e * e, axis=0, keepdims=True)

        @pl.when(i == n_steps - 1)
        def _():
            loss_ref[...] = jnp.sum(lacc_ref[...], axis=1, keepdims=True) * (0.5 / d)

    row = pl.BlockSpec((NORM_TILE, d), lambda i: (i, 0))
    return pl.pallas_call(
        body, name="final", grid=(n_steps,),
        in_specs=[row, row, row, _full((1, d))],
        out_specs=[row, row, _full((1, d)), _full((1, 1))],
        out_shape=[jax.ShapeDtypeStruct((s, d), F32), jax.ShapeDtypeStruct((s, d), BF16),
                   jax.ShapeDtypeStruct((1, d), F32), jax.ShapeDtypeStruct((1, 1), F32)],
        scratch_shapes=[pltpu.VMEM((1, d), F32)],
        compiler_params=_params(("arbitrary",)),
    )(x1, ffn, tgt, g2)


def _ffnnorm_bwd(dh2, x1, dy, mix, gain, scale, g1):
    s, d = x1.shape
    n_steps = s // NORM_TILE

    def body(dh_ref, x_ref, dy_ref, mix_ref, g_ref, sc_ref, g1_ref, dx_ref, dm_ref, acc_ref):
        i = pl.program_id(0)

        @pl.when(i == 0)
        def _():
            acc_ref[...] = jnp.zeros_like(acc_ref)

        dh, x = dh_ref[...], x_ref[...]
        r = _rms(x)
        xn = x * r
        dn = dh * (1.0 + sc_ref[...])
        dxn = dn * g_ref[...]
        dx = dy_ref[...] + r * (dxn - xn * jnp.mean(dxn * xn, axis=-1, keepdims=True))
        dx_ref[...] = dx
        dm_ref[...] = (dx * g1_ref[...]).astype(BF16)
        csum = lambda z: jnp.sum(z, axis=0, keepdims=True)
        acc_ref[0:1, :] += csum(dh)
        acc_ref[1:2, :] += csum(dh * (xn * g_ref[...]))
        acc_ref[2:3, :] += csum(dn * xn)
        acc_ref[3:4, :] += csum(dx * mix_ref[...])

    row = pl.BlockSpec((NORM_TILE, d), lambda i: (i, 0))
    vec = _full((1, d))
    return pl.pallas_call(
        body, name="ffnnorm_bwd", grid=(n_steps,),
        in_specs=[row, row, row, row, vec, vec, vec],
        out_specs=[row, row, _full((8, d))],
        out_shape=[jax.ShapeDtypeStruct((s, d), F32), jax.ShapeDtypeStruct((s, d), BF16), jax.ShapeDtypeStruct((8, d), F32)],
        compiler_params=_params(("arbitrary",)),
    )(dh2, x1, dy, mix, gain, scale, g1)


def _mixnorm_bwd(dh, x, dx1, gain, scale):
    s, d = x.shape
    n_steps = s // NORM_TILE

    def body(dh_ref, x_ref, dx1_ref, g_ref, sc_ref, gx_ref, acc_ref):
        i = pl.program_id(0)

        @pl.when(i == 0)
        def _():
            acc_ref[...] = jnp.zeros_like(acc_ref)

        dh, x = dh_ref[...], x_ref[...]
        r = _rms(x)
        xn = x * r
        dn = dh * (1.0 + sc_ref[...])
        dxn = dn * g_ref[...]
        gx_ref[...] = dx1_ref[...] + r * (dxn - xn * jnp.mean(dxn * xn, axis=-1, keepdims=True))
        csum = lambda z: jnp.sum(z, axis=0, keepdims=True)
        acc_ref[0:1, :] += csum(dh)
        acc_ref[1:2, :] += csum(dh * (xn * g_ref[...]))
        acc_ref[2:3, :] += csum(dn * xn)

    row = pl.BlockSpec((NORM_TILE, d), lambda i: (i, 0))
    vec = _full((1, d))
    return pl.pallas_call(
        body, name="mixnorm_bwd", grid=(n_steps,),
        in_specs=[row, row, row, vec, vec],
        out_specs=[row, _full((8, d))],
        out_shape=[jax.ShapeDtypeStruct((s, d), F32), jax.ShapeDtypeStruct((8, d), F32)],
        compiler_params=_params(("arbitrary",)),
    )(dh, x, dx1, gain, scale)


def _key_count(d, dilated):
    if not dilated:
        return jnp.where(d >= 0, 1.0, 0.0)
    one = lambda cond: jnp.where(cond, 1.0, 0.0)
    cnt = one(d <= 128) + one(((d & 3) == 0) & (d <= 512)) + one((d & 15) == 0)
    return jnp.where(d >= 0, cnt, 0.0)


def _block_kinds(mla):
    return (0, "diag", "none") if mla else (NEAR_REACH, "near", "far")


NEAR_REACH = 512


def _near_offsets(tk, tq):
    return (NEAR_REACH - (tk - tq)) // tk + 1


def _scores_t(ka, qa, scale, kind, rel_t, offset, near_tabs=None):
    return _mask_scores(lax.dot_general(ka, qa, NT, preferred_element_type=F32), scale, kind, rel_t, offset, near_tabs)


def _fill_near_tables(bias_ref, cnt_ref, rel_t):
    tk, tq = rel_t.shape
    for idx in range(_near_offsets(tk, tq)):
        cnt = _key_count(rel_t + (tk - tq) + idx * tk, True)
        cnt_ref[idx] = cnt
        bias_ref[idx] = jnp.where(cnt > 0.0, 0.0, NEG_INF)


def _mask_scores(products, scale, kind, rel_t, offset, near_tabs=None):
    st = products * (scale * LOG2E)
    cnt = None
    if kind == "diag":
        st = jnp.where(rel_t + offset >= 0, st, NEG_INF)
    elif kind == "far":
        st = jnp.where((rel_t & 15) == 0, st, NEG_INF)
    elif kind == "near":
        bias_ref, cnt_ref = near_tabs
        tk, tq = rel_t.shape
        idx = (offset - (tk - tq)) // tk
        st = st + bias_ref[idx]
        cnt = cnt_ref[idx]
    return st, cnt


def _attn_fwd(q, k, v, mla, scale, name, gather=()):
    s = q.shape[0]
    qw = 2 * LANE if mla else LANE
    tq, tk = ATT_TQ, ATT_TK
    reach, kind_near, kind_far = _block_kinds(mla)
    assert s % tq == 0 and tq % tk == 0 and reach % tk == 0 and reach in (0, NEAR_REACH)
    ng = len(gather)
    last_step = HEADS // 2 - 1

    def body(*refs):
        q_ref, k_ref, v_ref = refs[:3]
        o_ref, lse_ref = refs[3 + ng:5 + ng]
        vt_ref, st_ref = refs[5 + 2 * ng:7 + 2 * ng]
        near_tabs = None if mla else refs[7 + 2 * ng:9 + 2 * ng]
        n_tabs = 0 if mla else 2
        comm = (refs[3:3 + ng], refs[5 + ng:5 + 2 * ng]) + tuple(refs[7 + n_tabs + 2 * ng:])
        if ng:
            @pl.when(pl.program_id(0) == 0)
            def _():
                _Gather(*comm).start()

            @pl.when(pl.program_id(0) == last_step)
            def _():
                _Gather(*comm).forward()

        lane = lax.broadcasted_iota(I32, (1, LANE), 1)
        rel_t = lax.broadcasted_iota(I32, (tk, tq), 1) - lax.broadcasted_iota(I32, (tk, tq), 0)
        if not mla:
            _fill_near_tables(*near_tabs, rel_t)

        def transpose_v(j, carry):
            c0 = pl.multiple_of(j * tk, tk)
            vt_ref[:, pl.ds(c0, tk)] = v_ref[pl.ds(c0, tk), :].astype(F32).T.astype(BF16)
            return carry

        lax.fori_loop(0, s // tk, transpose_v, 0)

        def q_block(qi, carry):
            r0 = pl.multiple_of(qi * tq, tq)
            kcols = [slice(a * LANE, (a + 1) * LANE) if mla else slice(0, LANE) for a in range(2)]
            qas = [q_ref[pl.ds(r0, tq), kcols[a]] for a in range(2)]
            if not mla:
                qas = [jnp.where(lane < DIL_DIM, qas[0], jnp.zeros_like(qas[0])),
                       jnp.where(lane >= DIL_DIM, qas[1], jnp.zeros_like(qas[1]))]

            n_k = (r0 + tq) // tk

            def products(kj):
                c0 = pl.multiple_of(kj * tk, tk)
                return [lax.dot_general(k_ref[pl.ds(c0, tk), kcols[a]], qas[a], NT, preferred_element_type=F32)
                        for a in range(2)]

            for a, pr in enumerate(products(0)):
                st_ref[0, a] = pr

            def k_block(kj, c, kind):
                c0 = pl.multiple_of(kj * tk, tk)
                slot = kj & 1
                ahead = products(jnp.minimum(kj + 1, n_k - 1))
                out = []
                for a in range(2):
                    m, l, acc = c[a]
                    st, cnt = _mask_scores(st_ref[slot, a], scale, kind, rel_t, r0 - c0, near_tabs)
                    st_ref[1 - slot, a] = ahead[a]
                    m_new = jnp.maximum(m, jnp.max(st, axis=0, keepdims=True))
                    alpha = jnp.exp2(m - m_new)
                    p = jnp.exp2(st - m_new)
                    if cnt is not None:
                        p = p * cnt
                    l = alpha * l + jnp.sum(p, axis=0, keepdims=True)
                    vt = vt_ref[a * DIL_DIM:(a + 1) * DIL_DIM, pl.ds(c0, tk)]
                    acc = alpha * acc + jnp.dot(vt, p.astype(BF16), preferred_element_type=F32)
                    out.append((m_new, l, acc))
                return tuple(out)

            one = (jnp.full((1, tq), NEG_INF, F32), jnp.zeros((1, tq), F32), jnp.zeros((DIL_DIM, tq), F32))
            first_near = jnp.maximum((r0 - reach) // tk, 0)
            c = lax.fori_loop(0, first_near, functools.partial(k_block, kind=kind_far), (one, one))
            res = lax.fori_loop(first_near, (r0 + tq) // tk, functools.partial(k_block, kind=kind_near), c)
            o_t = jnp.concatenate([res[a][2] / res[a][1] for a in range(2)], axis=0)
            o_ref[pl.ds(r0, tq), :] = o_t.T.astype(BF16)
            for a in range(2):
                lse_ref[a, :, pl.ds(r0, tq)] = res[a][0] * LN2 + jnp.log(res[a][1])
            return carry

        lax.fori_loop(0, s // tq, q_block, 0)

        if ng:
            @pl.when(pl.program_id(0) == last_step)
            def _():
                _Gather(*comm).finish()

    return pl.pallas_call(
        body, name=name, grid=(HEADS // 2,),
        in_specs=[pl.BlockSpec((s, qw), lambda h: (0, h)), pl.BlockSpec((s, qw), lambda h: (0, h)),
                  pl.BlockSpec((s, LANE), lambda h: (0, h))] + [ANY] * ng,
        out_specs=[pl.BlockSpec((s, LANE), lambda h: (0, h)), pl.BlockSpec((2, 1, s), lambda h: (h, 0, 0))] + [ANY] * ng,
        out_shape=[jax.ShapeDtypeStruct((s, DIL_W), BF16), jax.ShapeDtypeStruct((HEADS, 1, s), F32)] + _Gather.out_shapes(gather),
        scratch_shapes=[pltpu.VMEM((LANE, s), BF16), pltpu.VMEM((2, 2, tk, tq), F32)]
        + ([] if mla else [pltpu.VMEM((_near_offsets(tk, tq), tk, tq), F32)] * 2) + (_Gather.scratch(gather) if ng else []),
        compiler_params=_params(("arbitrary",) if ng else ("parallel",), 12 << 20),
    )(*_in_hbm(q, k, v), *gather)


def _attn_bwd(q, k, v, o, do, do_block0, lse, mla, scale, name, scatter=()):
    s = q.shape[0]
    qw = 2 * LANE if mla else LANE
    tq, tk = ATT_TQ, ATT_TK_BWD
    nq = s // tq
    reach, kind_near, kind_far = _block_kinds(mla)
    assert s % tq == 0 and s % tk == 0
    ns = len(scatter)
    last_step = HEADS // 2 - 1

    def body(*refs):
        q_ref, k_ref, v_ref, o_ref, do_ref, lse_ref = refs[:6]
        dq_ref, dk_ref, dv_ref = refs[6 + ns:9 + ns]
        kt_ref, dot_ref, dob_ref, dqt_ref, delta_ref, lse2_ref = refs[9 + 2 * ns:15 + 2 * ns]
        near_tabs = None if mla else refs[15 + 2 * ns:17 + 2 * ns]
        n_tabs = 0 if mla else 2
        comm = (refs[6:6 + ns], refs[9 + ns:9 + 2 * ns]) + tuple(refs[15 + n_tabs + 2 * ns:])
        if ns:
            @pl.when(pl.program_id(0) == 0)
            def _():
                _Scatter(*comm).start()

        lane = lax.broadcasted_iota(I32, (1, LANE), 1)
        row = lax.broadcasted_iota(I32, (LANE, 1), 0)
        rel_t = lax.broadcasted_iota(I32, (tk, tq), 1) - lax.broadcasted_iota(I32, (tk, tq), 0)
        if not mla:
            _fill_near_tables(*near_tabs, rel_t)

        def prepare(j, carry):
            c0 = pl.multiple_of(j * tk, tk)
            do_blk = do_ref[pl.ds(c0, tk), :]
            dob_ref[pl.ds(c0, tk), :] = do_blk.astype(BF16)
            do_t = do_blk.T
            dot_ref[:, pl.ds(c0, tk)] = do_t.astype(BF16)
            prod = do_t * o_ref[pl.ds(c0, tk), :].astype(F32).T
            delta_ref[0, :, pl.ds(c0, tk)] = jnp.sum(prod[0:DIL_DIM], axis=0, keepdims=True)
            delta_ref[1, :, pl.ds(c0, tk)] = jnp.sum(prod[DIL_DIM:LANE], axis=0, keepdims=True)
            for w in range(qw // LANE):
                kt_ref[w * LANE:(w + 1) * LANE, pl.ds(c0, tk)] = (
                    k_ref[pl.ds(c0, tk), w * LANE:(w + 1) * LANE].astype(F32).T.astype(BF16))
            return carry

        lax.fori_loop(0, s // tk, prepare, 0)
        dqt_ref[...] = jnp.zeros_like(dqt_ref)
        lse2_ref[...] = lse_ref[...] * LOG2E

        sels = [lane < DIL_DIM, lane >= DIL_DIM]
        rsels = [row < DIL_DIM, row >= DIL_DIM]
        cols = [slice(a * LANE, (a + 1) * LANE) if mla else slice(0, LANE) for a in range(2)]

        def k_block(kj, carry):
            c0 = pl.multiple_of(kj * tk, tk)
            kas = [k_ref[pl.ds(c0, tk), cols[a]] for a in range(2)]
            kts = [kt_ref[cols[a], pl.ds(c0, tk)] for a in range(2)]
            if not mla:
                kas = [jnp.where(sels[a], kas[a], jnp.zeros_like(kas[a])) for a in range(2)]
                kts = [jnp.where(rsels[a], kts[a], jnp.zeros_like(kts[a])) for a in range(2)]
            vb = v_ref[pl.ds(c0, tk), :]
            vbs = [jnp.where(sels[a], vb, jnp.zeros_like(vb)) for a in range(2)]

            first = c0 // tq

            def q_block(qi, c, kind):
                r0 = pl.multiple_of(qi * tq, tq)
                out, dq_parts = [], []
                for a in range(2):
                    dk_acc, dv_acc = c[a]
                    qa = q_ref[pl.ds(r0, tq), cols[a]]
                    st, cnt = _scores_t(kas[a], qa, scale, kind, rel_t, r0 - c0, near_tabs)
                    p = jnp.exp2(st - lse2_ref[a, :, pl.ds(r0, tq)])
                    if cnt is not None:
                        p = p * cnt
                    dp = jnp.dot(vbs[a], dot_ref[:, pl.ds(r0, tq)], preferred_element_type=F32)
                    ds = (p * (dp - delta_ref[a, :, pl.ds(r0, tq)]) * scale).astype(BF16)
                    dv_acc = dv_acc + jnp.dot(p.astype(BF16), dob_ref[pl.ds(r0, tq), :], preferred_element_type=F32)
                    dk_acc = dk_acc + jnp.dot(ds, qa, preferred_element_type=F32)
                    dq_parts.append(jnp.dot(kts[a], ds, preferred_element_type=F32))
                    out.append((dk_acc, dv_acc))
                if mla:
                    for a in range(2):
                        dqt_ref[cols[a], pl.ds(r0, tq)] += dq_parts[a]
                else:
                    dqt_ref[:, pl.ds(r0, tq)] += dq_parts[0] + dq_parts[1]
                return tuple(out)

            zero = jnp.zeros((tk, LANE), F32)
            last_near = jnp.minimum((c0 + tk - 1 + reach) // tq + 1, nq)
            c = lax.fori_loop(first, last_near, functools.partial(q_block, kind=kind_near), ((zero, zero), (zero, zero)))
            (dk0, dv0), (dk1, dv1) = lax.fori_loop(last_near, nq, functools.partial(q_block, kind=kind_far), c)
            if mla:
                dk_ref[pl.ds(c0, tk), cols[0]] = dk0
                dk_ref[pl.ds(c0, tk), cols[1]] = dk1
            else:
                dk_ref[pl.ds(c0, tk), :] = jnp.where(sels[0], dk0, dk1)
            dv_ref[pl.ds(c0, tk), :] = jnp.where(sels[0], dv0, dv1)
            return carry

        lax.fori_loop(0, s // tk, k_block, 0)

        def write_dq(j, carry):
            c0 = pl.multiple_of(j * tk, tk)
            for w in range(qw // LANE):
                dq_ref[pl.ds(c0, tk), w * LANE:(w + 1) * LANE] = dqt_ref[w * LANE:(w + 1) * LANE, pl.ds(c0, tk)].T
            return carry

        lax.fori_loop(0, s // tk, write_dq, 0)

        if ns:
            @pl.when(pl.program_id(0) == last_step)
            def _():
                _Scatter(*comm).finish()

    b0 = do_block0
    return pl.pallas_call(
        body, name=name, grid=(HEADS // 2,),
        in_specs=[pl.BlockSpec((s, qw), lambda h: (0, h)), pl.BlockSpec((s, qw), lambda h: (0, h)),
                  pl.BlockSpec((s, LANE), lambda h: (0, h)), pl.BlockSpec((s, LANE), lambda h: (0, h)),
                  pl.BlockSpec((s, LANE), lambda h: (0, h + b0)), pl.BlockSpec((2, 1, s), lambda h: (h, 0, 0))] + [ANY] * ns,
        out_specs=[pl.BlockSpec((s, qw), lambda h: (0, h)), pl.BlockSpec((s, qw), lambda h: (0, h)),
                   pl.BlockSpec((s, LANE), lambda h: (0, h))] + [ANY] * ns,
        out_shape=[jax.ShapeDtypeStruct(q.shape, F32), jax.ShapeDtypeStruct(k.shape, F32), jax.ShapeDtypeStruct((s, DIL_W), F32)]
        + _Scatter.out_shapes(scatter),
        scratch_shapes=[pltpu.VMEM((qw, s), BF16), pltpu.VMEM((LANE, s), BF16), pltpu.VMEM((s, LANE), BF16),
                        pltpu.VMEM((qw, s), F32), pltpu.VMEM((2, 1, s), F32), pltpu.VMEM((2, 1, s), F32)]
        + ([] if mla else [pltpu.VMEM((_near_offsets(tk, tq), tk, tq), F32)] * 2) + (_Scatter.semaphores(ns) if ns else []),
        compiler_params=_params(("arbitrary",) if ns else ("parallel",), 24 << 20),
    )(*_in_hbm(q, k, v, o, do, lse), *scatter)


def _ada_bwd(c_all, dmod_shard):
    n, d = c_all.shape
    cols = dmod_shard.shape[1]

    def body(c_ref, g_ref, o_ref):
        cv = c_ref[...]
        o_ref[...] = lax.dot_general(cv * _sigmoid(cv), g_ref[...], TN, precision=HIGHEST, preferred_element_type=F32)

    return pl.pallas_call(
        body, name="ada_bwd", out_shape=jax.ShapeDtypeStruct((d, cols), F32),
        compiler_params=_params(None, 16 << 20),
    )(c_all, dmod_shard)


SMALL_WIDTHS = (("g_mix_norm", D_MODEL), ("g_q_lat", Q_LORA), ("g_kv_lat", KV_LORA), ("g_mla_q_nope", NOPE),
                ("g_mla_q_pe", ROPE), ("g_mla_k_nope", NOPE), ("g_mla_k_pe", ROPE), ("g_dil_q", DIL_DIM),
                ("g_dil_k", DIL_DIM), ("g_ffn_norm", D_MODEL), ("b_conv", UP_W))


def _small_layout():
    pieces = (("dmod", 6 * D_MODEL),) + SMALL_WIDTHS + tuple(("w_conv%d" % k, UP_W) for k in range(3)) + (("loss", 1),)
    layout, off = {}, 0
    for name, width in pieces:
        layout[name] = (width, off)
        off += -(-width // LANE) * LANE
    return layout, off


def _pack_small(acc1, acc2, dg2, dglat, dgains, dbg, dbv, dwg, dwv, loss_part):
    layout, total = _small_layout()

    def body(a1, a2, g2, gl, gg, bg, bv, wg, wv, ls, o_ref):
        o_ref[...] = jnp.zeros_like(o_ref)

        def put(name, src, shift=0):
            start = layout[name][1] + shift
            o_ref[:, start:start + src.shape[1]] = src

        for k, src in enumerate((a1[0:1, :], a1[1:2, :], a2[3:4, :], a2[0:1, :], a2[1:2, :], g2[...])):
            put("dmod", src, k * D_MODEL)
        put("g_mix_norm", a1[2:3, :])
        put("g_q_lat", gl[0:1, :])
        put("g_kv_lat", gl[1:2, 0:KV_LORA])
        put("g_mla_q_nope", gg[0:1, 0:NOPE])
        put("g_mla_q_pe", gg[5:6, 0:ROPE])
        put("g_mla_k_nope", gg[1:2, 0:NOPE])
        put("g_mla_k_pe", gg[2:3, 0:ROPE])
        put("g_dil_q", gg[3:4, 0:DIL_DIM])
        put("g_dil_k", gg[4:5, 0:DIL_DIM])
        put("g_ffn_norm", a2[2:3, :])
        put("b_conv", bg[...])
        put("b_conv", bv[...], D_FF)
        for k in range(3):
            put("w_conv%d" % k, wg[k:k + 1, :])
            put("w_conv%d" % k, wv[k:k + 1, :], D_FF)
        put("loss", ls[...])

    ins = (acc1, acc2, dg2, dglat, dgains, dbg, dbv, dwg, dwv, loss_part)
    return pl.pallas_call(
        body, name="pack_small", grid=(1,), in_specs=[_full(a.shape) for a in ins], out_specs=_full((1, total)),
        out_shape=jax.ShapeDtypeStruct((1, total), F32),
        compiler_params=_params(("arbitrary",), 2 << 20),
    )(*_in_hbm(*ins))


def _sum_unpack(g):
    n_dev, _, total = g.shape
    layout, _ = _small_layout()

    def body(g_ref, *refs):
        o_refs, s_ref = refs[:-1], refs[-1]
        acc = g_ref[0]
        for k in range(1, n_dev):
            acc = acc + g_ref[k]
        s_ref[...] = acc
        take = lambda name: s_ref[:, layout[name][1]:layout[name][1] + layout[name][0]]
        o_refs[0][...] = take("dmod")
        for i, (name, _) in enumerate(SMALL_WIDTHS):
            o_refs[1 + i][...] = take(name)
        for k in range(3):
            o_refs[-2][k:k + 1, :] = take("w_conv%d" % k)
        o_refs[-1][...] = take("loss")

    shapes = [(1, 6 * D_MODEL)] + [(1, w) for _, w in SMALL_WIDTHS] + [(3, UP_W), (1, 1)]
    return pl.pallas_call(
        body, name="sum_unpack", out_shape=[jax.ShapeDtypeStruct(sh, F32) for sh in shapes],
        scratch_shapes=[pltpu.VMEM((1, total), F32)],
        compiler_params=_params(None, 4 << 20),
    )(g)


def _adamw_math(w, g, m, v):
    mn = ADAM_B1 * m + (1.0 - ADAM_B1) * g
    vn = ADAM_B2 * v + (1.0 - ADAM_B2) * (g * g)
    m_hat = mn / (1.0 - ADAM_B1 ** ADAM_STEP)
    v_hat = vn / (1.0 - ADAM_B2 ** ADAM_STEP)
    return -ADAM_LR * (m_hat / (jnp.sqrt(v_hat) + ADAM_EPS) + ADAM_WD * w), mn, vn


def _adamw_vectors(ws, gs, ms, vs):
    k = len(ws)

    def body(*refs):
        for i in range(k):
            d, mn, vn = _adamw_math(refs[i][...], refs[k + i][...], refs[2 * k + i][...], refs[3 * k + i][...])
            refs[4 * k + i][...] = d
            refs[5 * k + i][...] = mn
            refs[6 * k + i][...] = vn

    blocks = [_full(w.shape) for w in ws]
    outs = pl.pallas_call(
        body, name="adamw_vectors", grid=(1,), in_specs=blocks * 4, out_specs=blocks * 3,
        out_shape=[jax.ShapeDtypeStruct(w.shape, F32) for w in ws] * 3,
        compiler_params=_params(("arbitrary",), 2 << 20),
    )(*_in_hbm(*ws, *gs, *ms, *vs))
    return outs[:k], outs[k:2 * k], outs[2 * k:]


def _adamw(w, g, m, v, name):
    r, c = w.shape
    tr = r
    for cand in (256, 128, 64, 32, 16):
        if r % cand == 0 and r > cand:
            tr = cand
            break

    def body(w_ref, g_ref, m_ref, v_ref, d_ref, mo_ref, vo_ref):
        d_ref[...], mo_ref[...], vo_ref[...] = _adamw_math(w_ref[...], g_ref[...], m_ref[...], v_ref[...])

    blk = pl.BlockSpec((tr, c), lambda i: (i, 0))
    return pl.pallas_call(
        body, name=name, grid=(r // tr,), in_specs=[blk] * 4, out_specs=[blk] * 3,
        out_shape=[jax.ShapeDtypeStruct((r, c), F32)] * 3,
        compiler_params=_params(("parallel",), 7 * _nbytes((tr, c), F32)),
    )(w, g, m, v)


def _position():
    return lax.axis_index("x"), lax.axis_index("y"), lax.axis_index("c")


def _other_chips(x, y):
    return [(1 - x, y, 2 * (1 - x) + y), (x, 1 - y, 2 * x + (1 - y)), (1 - x, 1 - y, 2 * (1 - x) + (1 - y))]


class _SmallGather:
    def __init__(self, v_ref, out_ref, send_sems, recv_sems, local_sem):
        x, y, c = _position()
        me = 4 * x + 2 * y + c
        self.local = pltpu.make_async_copy(v_ref, out_ref.at[me], local_sem)
        self.sends, self.arrivals = [], []
        for k in range(N_DEV - 1):
            fx, fy, fc = ((k + 1) >> 2) & 1, ((k + 1) >> 1) & 1, (k + 1) & 1
            px, py, pc = (1 - x if fx else x), (1 - y if fy else y), (1 - c if fc else c)

            def copy(dst, k=k, peer=(px, py, pc)):
                return pltpu.make_async_remote_copy(src_ref=v_ref, dst_ref=dst, send_sem=send_sems.at[k],
                                                    recv_sem=recv_sems.at[k], device_id=peer, device_id_type=MESH)

            self.sends.append(copy(out_ref.at[me]))
            self.arrivals.append(copy(out_ref.at[4 * px + 2 * py + pc]))

    @staticmethod
    def semaphores():
        return [pltpu.SemaphoreType.DMA((N_DEV - 1,)), pltpu.SemaphoreType.DMA((N_DEV - 1,)), pltpu.SemaphoreType.DMA]

    def start(self):
        self.local.start()
        for cp in self.sends:
            cp.start()

    def finish(self):
        for cp in self.arrivals:
            cp.wait_recv()
        for cp in self.sends:
            cp.wait_send()
        self.local.wait()


def _prologue(c_taps, w_ada_shard, b_shard, pos_col, rope_consts, shards):
    n = len(shards)
    s = pos_col.shape[0]
    cols = w_ada_shard.shape[1]
    freq, csel, ssel = rope_consts

    def body(*refs):
        ct_ref, w_ref, b_ref, p_ref, f_ref, cs_ref, ss_ref = refs[:7]
        sh_refs = refs[7:7 + n]
        ct_all_ref, mod_all_ref, tab_ref = refs[7 + n:10 + n]
        g_refs = refs[10 + n:10 + 2 * n]
        mod_blk_ref = refs[10 + 2 * n]
        sems = refs[11 + 2 * n:]
        first = _SmallGather(ct_ref, ct_all_ref, *sems[0:3])
        first.start()
        first.finish()
        cv = ct_all_ref[:, 0, 0:D_MODEL]
        sc = (cv * _sigmoid(cv)).astype(BF16)
        mod_blk_ref[...] = jnp.dot(sc, w_ref[...].astype(BF16), preferred_element_type=F32) + b_ref[...]
        second = _SmallGather(mod_blk_ref, mod_all_ref, *sems[3:6])
        second.start()
        weights = _Gather(sh_refs, g_refs, *sems[6:])
        weights.start()

        def table_rows(i, carry):
            r0 = pl.multiple_of(i * ROW_TILE, ROW_TILE)
            ang = p_ref[pl.ds(r0, ROW_TILE), :].astype(F32) * f_ref[...]
            tab_ref[pl.ds(r0, ROW_TILE), :] = cs_ref[...] * jnp.cos(ang) + ss_ref[...] * jnp.sin(ang)
            return carry

        lax.fori_loop(0, s // ROW_TILE, table_rows, 0)
        second.finish()
        weights.forward()
        weights.finish()

    return pl.pallas_call(
        body, name="prologue",
        out_shape=[jax.ShapeDtypeStruct((N_DEV,) + c_taps.shape, F32), jax.ShapeDtypeStruct((N_DEV, N_DEV, cols), F32),
                   jax.ShapeDtypeStruct((s, 4 * LANE), F32)] + _Gather.out_shapes(shards),
        in_specs=[IN_VMEM] * 7 + [ANY] * n, out_specs=[IN_VMEM] * 3 + [ANY] * n,
        scratch_shapes=[pltpu.VMEM((N_DEV, cols), F32)] + _SmallGather.semaphores() * 2 + _Gather.scratch(shards),
        compiler_params=_params(None, 14 << 20),
    )(c_taps, w_ada_shard, b_shard, pos_col, freq, csel, ssel, *shards)


IN_VMEM = pl.BlockSpec(memory_space=pltpu.VMEM)
ANY = pl.BlockSpec(memory_space=pl.ANY)


class _Gather:
    def __init__(self, w_refs, out_refs, send_sems, recv_sems, own_sems, *bounce_refs):
        x, y, c = _position()
        q0 = 2 * x + y
        sibling = (x, y, 1 - c)
        self.ici, self.ici_in, self.fwd, self.fwd_in, self.own_in, self.own_out = [], [], [], [], [], []
        for k, (w_ref, out_ref) in enumerate(zip(w_refs, out_refs)):
            half = w_ref.shape[0] // 2
            self.own_in.append(pltpu.make_async_copy(w_ref, bounce_refs[k], own_sems.at[2 * k]))
            self.own_out.append(pltpu.make_async_copy(bounce_refs[k], out_ref.at[q0], own_sems.at[2 * k + 1]))

            def blk(q, e, out_ref=out_ref, half=half):
                return out_ref.at[q, pl.ds(pl.multiple_of(e * half, 16), half), :]

            def copy(src, dst, i, to):
                return pltpu.make_async_remote_copy(src_ref=src, dst_ref=dst, send_sem=send_sems.at[i], recv_sem=recv_sems.at[i],
                                                    device_id=to, device_id_type=MESH)

            src = w_ref.at[pl.ds(pl.multiple_of(c * half, 16), half), :]
            for j, (cx, cy, qj) in enumerate(_other_chips(x, y)):
                self.ici.append(copy(src, blk(q0, c), 6 * k + j, (cx, cy, c)))
                self.ici_in.append(copy(blk(qj, c), blk(qj, c), 6 * k + j, (cx, cy, c)))
                self.fwd.append(copy(blk(qj, c), blk(qj, c), 6 * k + 3 + j, sibling))
                self.fwd_in.append(copy(blk(qj, 1 - c), blk(qj, 1 - c), 6 * k + 3 + j, sibling))

    @staticmethod
    def out_shapes(shards):
        return [jax.ShapeDtypeStruct((N_CHIP,) + s.shape, s.dtype) for s in shards]

    @staticmethod
    def scratch(shards):
        n = len(shards)
        return ([pltpu.SemaphoreType.DMA((6 * n,)), pltpu.SemaphoreType.DMA((6 * n,)), pltpu.SemaphoreType.DMA((2 * n,))]
                + [pltpu.VMEM(s.shape, s.dtype) for s in shards])

    def start(self):
        for cp in self.ici + self.own_in:
            cp.start()

    def forward(self):
        for fetched, placed in zip(self.own_in, self.own_out):
            fetched.wait()
            placed.start()
        for arrived, onward in zip(self.ici_in, self.fwd):
            arrived.wait_recv()
            onward.start()

    def finish(self):
        for cp in self.fwd_in:
            cp.wait_recv()
        for cp in self.ici + self.fwd:
            cp.wait_send()
        for cp in self.own_out:
            cp.wait()


class _PairSwap:
    def __init__(self, g_refs, out_refs, send_sems, recv_sems):
        x, y, c = _position()
        self.copies = [
            pltpu.make_async_remote_copy(src_ref=g_ref.at[:, 1 - c], dst_ref=out_ref, send_sem=send_sems.at[k],
                                         recv_sem=recv_sems.at[k], device_id=(x, y, 1 - c), device_id_type=MESH)
            for k, (g_ref, out_ref) in enumerate(zip(g_refs, out_refs))]

    @staticmethod
    def out_shapes(grads):
        return [jax.ShapeDtypeStruct((N_CHIP,) + g.shape[2:], g.dtype) for g in grads]

    @staticmethod
    def semaphores(n):
        return [pltpu.SemaphoreType.DMA((n,)), pltpu.SemaphoreType.DMA((n,))]

    def start(self):
        for cp in self.copies:
            cp.start()

    def finish(self):
        for cp in self.copies:
            cp.wait_recv()
        for cp in self.copies:
            cp.wait_send()


def _pair_sum(g, a, c_idx, name):
    _, _, rh, cols = g.shape
    tr = rh
    for cand in (256, 128, 64, 32, 16):
        if rh % cand == 0 and rh > cand:
            tr = cand
            break

    def body(c_ref, g_ref, a_ref, o_ref):
        o_ref[...] = (g_ref[...] + a_ref[...]).astype(BF16)

    return pl.pallas_call(
        body, name=name,
        grid_spec=pltpu.PrefetchScalarGridSpec(
            num_scalar_prefetch=1, grid=(N_CHIP, rh // tr),
            in_specs=[pl.BlockSpec((None, None, tr, cols), lambda q, i, c_ref: (q, c_ref[0], i, 0)),
                      pl.BlockSpec((None, tr, cols), lambda q, i, c_ref: (q, i, 0))],
            out_specs=pl.BlockSpec((None, tr, cols), lambda q, i, c_ref: (q, i, 0))),
        out_shape=jax.ShapeDtypeStruct((N_CHIP, rh, cols), BF16),
        compiler_params=_params(("parallel", "parallel"), 10 * _nbytes((tr, cols), F32)),
    )(c_idx, g, a)


def _scatter_and_gather(parts, small, name):
    n = len(parts)

    def body(*refs):
        scatter = _Scatter(refs[:n], refs[n + 1:2 * n + 1], *refs[2 * n + 2:2 * n + 4])
        gather = _SmallGather(refs[n], refs[2 * n + 1], *refs[2 * n + 4:])
        scatter.start()
        gather.start()
        gather.finish()
        scatter.finish()

    return pl.pallas_call(
        body, name=name,
        out_shape=_Scatter.out_shapes(parts) + [jax.ShapeDtypeStruct((N_DEV,) + small.shape, F32)],
        in_specs=[ANY] * n + [IN_VMEM], out_specs=[ANY] * n + [IN_VMEM],
        scratch_shapes=_Scatter.semaphores(n) + _SmallGather.semaphores(),
        compiler_params=_params(None, 10 * _nbytes(small.shape, F32)),
    )(*parts, small)


class _Scatter:
    def __init__(self, p_refs, out_refs, send_sems, recv_sems):
        x, y, c = _position()
        self.copies = []
        for k, (p_ref, out_ref) in enumerate(zip(p_refs, out_refs)):
            for j, (cx, cy, qj) in enumerate(_other_chips(x, y)):
                self.copies.append(pltpu.make_async_remote_copy(
                    src_ref=p_ref.at[qj], dst_ref=out_ref.at[j], send_sem=send_sems.at[3 * k + j],
                    recv_sem=recv_sems.at[3 * k + j], device_id=(cx, cy, c), device_id_type=MESH))

    @staticmethod
    def out_shapes(parts):
        return [jax.ShapeDtypeStruct((3,) + p.shape[1:], p.dtype) for p in parts]

    @staticmethod
    def semaphores(n):
        return [pltpu.SemaphoreType.DMA((3 * n,)), pltpu.SemaphoreType.DMA((3 * n,))]

    def start(self):
        for cp in self.copies:
            cp.start()

    def finish(self):
        for cp in self.copies:
            cp.wait_recv()
        for cp in self.copies:
            cp.wait_send()


def _shard_sum(p, b, qc_idx, name):
    _, rh, cols = p.shape
    tr = rh
    for cand in (256, 128, 64, 32, 16):
        if rh % cand == 0 and rh > cand:
            tr = cand
            break

    def body(qc_ref, p_ref, b_ref, o_ref):
        acc = p_ref[...].astype(F32)
        for j in range(3):
            acc = acc + b_ref[j].astype(F32)
        o_ref[...] = acc

    return pl.pallas_call(
        body, name=name,
        grid_spec=pltpu.PrefetchScalarGridSpec(
            num_scalar_prefetch=1, grid=(rh // tr,),
            in_specs=[pl.BlockSpec((None, tr, cols), lambda i, qc_ref: (qc_ref[0], i, 0)),
                      pl.BlockSpec((3, tr, cols), lambda i, qc_ref: (0, i, 0))],
            out_specs=pl.BlockSpec((None, tr, cols), lambda i, qc_ref: (qc_ref[1], i, 0))),
        out_shape=jax.ShapeDtypeStruct((2, rh, cols), F32),
        compiler_params=_params(("parallel",), 8 * _nbytes((tr, cols), F32)),
    )(qc_idx, p, b)


def _join_halves(shards):
    n = len(shards)

    def body(*refs):
        out_refs = refs[n:2 * n]
        send_sems, recv_sems = refs[2 * n:]
        x, y, c = _position()
        cps = [pltpu.make_async_remote_copy(src_ref=out_refs[k].at[c], dst_ref=out_refs[k].at[c], send_sem=send_sems.at[k],
                                            recv_sem=recv_sems.at[k], device_id=(x, y, 1 - c), device_id_type=MESH)
               for k in range(n)]
        for cp in cps:
            cp.start()
        for k in range(n):
            arriving = out_refs[k].at[1 - c]
            pltpu.make_async_remote_copy(src_ref=arriving, dst_ref=arriving, send_sem=send_sems.at[k], recv_sem=recv_sems.at[k],
                                         device_id=(x, y, 1 - c), device_id_type=MESH).wait_recv()
        for cp in cps:
            cp.wait_send()

    return pl.pallas_call(
        body, name="rs_join",
        out_shape=[jax.ShapeDtypeStruct(a.shape, a.dtype) for a in shards],
        in_specs=[ANY] * n, out_specs=[ANY] * n, input_output_aliases={k: k for k in range(n)},
        scratch_shapes=[pltpu.SemaphoreType.DMA((n,)), pltpu.SemaphoreType.DMA((n,))],
    )(*shards)


def _cols_from_shards(g):
    q, r, cs = g.shape
    return jnp.transpose(g, (1, 0, 2)).reshape(r, q * cs)


def _cols_to_shards(w):
    r, cfull = w.shape
    return jnp.transpose(w.reshape(r, N_CHIP, cfull // N_CHIP), (1, 0, 2))


def _pad_w_in(w):
    z = lambda n: jnp.zeros((w.shape[0], n), w.dtype)
    q_lat, kv_lat, kpe = w[:, 0:512], w[:, 512:768], w[:, 768:800]
    qd, kd, vd = w[:, 800:1312], w[:, 1312:1824], w[:, 1824:2336]
    return jnp.concatenate([q_lat, qd, kd, vd, kv_lat, z(KPE_OFF), kpe, z(LANE - KPE_OFF - ROPE)], axis=1)


def _pad_w_qb(w):
    w3 = w.reshape(Q_LORA, HEADS, NOPE + ROPE)
    return jnp.pad(w3, ((0, 0), (0, 0), (0, LANE - NOPE - ROPE))).reshape(Q_LORA, HEADS * LANE)


def _unpad_w_qb(g):
    return g.reshape(Q_LORA, HEADS, LANE)[:, :, :NOPE + ROPE].reshape(Q_LORA, HEADS * (NOPE + ROPE))


def _pad_w_kvb(w):
    w3 = w.reshape(KV_LORA, HEADS, 2 * NOPE)
    kp = jnp.pad(w3[:, :, :NOPE], ((0, 0), (0, 0), (0, LANE - NOPE))).reshape(KV_LORA, HEADS * LANE)
    return jnp.concatenate([kp, w3[:, :, NOPE:].reshape(KV_LORA, DIL_W)], axis=1)


def _unpad_w_kvb(g):
    gk = g[:, :HEADS * LANE].reshape(KV_LORA, HEADS, LANE)[:, :, :NOPE]
    gv = g[:, HEADS * LANE:].reshape(KV_LORA, HEADS, NOPE)
    return jnp.concatenate([gk, gv], axis=2).reshape(KV_LORA, HEADS * 2 * NOPE)


def _head_gains(g_q_nope, g_q_pe, g_k_nope, g_k_pe, g_dq, g_dk):
    z = lambda n: jnp.zeros((1, n), F32)
    q1 = jnp.concatenate([g_q_nope, g_q_pe, z(LANE - NOPE - ROPE)], axis=1)
    k1 = jnp.concatenate([g_k_nope, z(LANE - NOPE)], axis=1)
    kpe = jnp.concatenate([z(KPE_OFF), g_k_pe, z(LANE - KPE_OFF - ROPE)], axis=1)
    return dict(q=jnp.tile(q1, (1, HEADS)), k=jnp.tile(k1, (1, HEADS)), kpe=kpe,
                dq=jnp.tile(g_dq, (1, HEADS)), dk=jnp.tile(g_dk, (1, HEADS)))


def kernel(x, c, positions, w_ada, b_ada, g_mix_norm, w_in, g_q_lat, w_q_b, g_kv_lat, w_kv_b, g_mla_q_nope, g_mla_q_pe, g_mla_k_nope, g_mla_k_pe, g_dil_q, g_dil_k, w_o, g_ffn_norm, w_up, w_conv, b_conv, w_down, loss_target, m_w_ada, m_b_ada, m_g_mix_norm, m_w_in, m_g_q_lat, m_w_q_b, m_g_kv_lat, m_w_kv_b, m_g_mla_q_nope, m_g_mla_q_pe, m_g_mla_k_nope, m_g_mla_k_pe, m_g_dil_q, m_g_dil_k, m_w_o, m_g_ffn_norm, m_w_up, m_w_conv, m_b_conv, m_w_down, v_w_ada, v_b_ada, v_g_mix_norm, v_w_in, v_g_q_lat, v_w_q_b, v_g_kv_lat, v_w_kv_b, v_g_mla_q_nope, v_g_mla_q_pe, v_g_mla_k_nope, v_g_mla_k_pe, v_g_dil_q, v_g_dil_k, v_w_o, v_g_ffn_norm, v_w_up, v_w_conv, v_b_conv, v_w_down):
    args = dict(locals())
    weights = {n: args[n][0] for n in ("w_ada", "w_in", "w_q_b", "w_kv_b", "w_o", "w_up", "w_conv", "w_down")}
    small_w = {n: args[n] for n in ("b_ada",) + tuple(n for n, _ in SMALL_WIDTHS)}
    mom_m = {n[2:]: (args[n][0] if args[n].ndim == 3 else args[n]) for n in args if n.startswith("m_")}
    mom_v = {n[2:]: (args[n][0] if args[n].ndim == 3 else args[n]) for n in args if n.startswith("v_")}

    xi, yi, ci = _position()
    q0 = 2 * xi + yi
    me = 4 * xi + 2 * yi + ci
    xs, tgt = x[0], loss_target[0]
    s = xs.shape[0]
    consts = _seg_consts()
    c_idx, qc_idx = jnp.reshape(ci, (1,)).astype(I32), jnp.stack([q0, ci]).astype(I32)

    def halves(g4):
        q, r, cc = g4.shape
        return g4.reshape(q, 2, r // 2, cc)

    own_first = [weights[n].astype(BF16) for n in ("w_in", "w_q_b", "w_kv_b")]
    own_later = [weights[n].astype(BF16) for n in ("w_o", "w_up", "w_down")]
    conv_cols = UP_W // N_CHIP
    ada_cols = w_ada.shape[2]
    b_shard = lax.dynamic_slice_in_dim(b_ada, q0 * ada_cols, ada_cols, axis=1)
    c_taps = jnp.concatenate([c, weights["w_conv"].reshape(1, 3 * conv_cols)], axis=1)
    c_taps_all, mod_all, tab, *gathered = _prologue(c_taps, weights["w_ada"], b_shard, positions.reshape(s, 1),
                                                    _rope_consts(), own_first)
    c_all = c_taps_all[:, 0, :D_MODEL]
    w_conv_f = c_taps_all[:, 0, D_MODEL:].reshape(N_CHIP, 2, 3, conv_cols)[:, 0]
    w_conv_f = jnp.transpose(w_conv_f, (1, 0, 2)).reshape(3, UP_W)
    mod_all = mod_all.reshape(N_CHIP, 2, N_DEV, ada_cols)
    mod = lax.dynamic_index_in_dim(lax.dynamic_index_in_dim(mod_all, ci, 1, False), me, 1, False)
    mod = mod.reshape(1, N_CHIP * ada_cols)
    sh1, sc1, g1, sh2, sc2, g2 = [mod[:, k * D_MODEL:(k + 1) * D_MODEL] for k in range(6)]
    w_in_f = _cols_from_shards(gathered[0])
    w_in_p = _pad_w_in(w_in_f)
    w_qb_p = _pad_w_qb(_cols_from_shards(gathered[1]))
    w_kvb_p = _pad_w_kvb(_cols_from_shards(gathered[2]))
    gains = _head_gains(g_mla_q_nope, g_mla_q_pe, g_mla_k_nope, g_mla_k_pe, g_dil_q, g_dil_k)

    h = _prenorm(xs, g_mix_norm, sc1, sh1, "prenorm")
    proj = _mm(h, w_in_p, "nn", F32, 512, P_COLS, "mm_in")
    ql, kvl = _latnorm(proj, g_q_lat, g_kv_lat)
    q_raw = _mm(ql, w_qb_p, "nn", F32, 1024, HEADS * LANE, "mm_qb")
    kv_raw = _mm(kvl, w_kvb_p, "nn", F32, 1024, HEADS * LANE + DIL_W, "mm_kvb")
    qm, km, vm, qd, kd, vd = _attn_prep(q_raw, kv_raw, proj, tab, gains, consts)
    scale_m, scale_d = (NOPE + ROPE) ** -0.5, DIL_DIM ** -0.5
    o_m, lse_m, got_up = _attn_fwd(qm, km, vm, True, scale_m, "attn_mla", gather=own_later[1:2])
    o_d, lse_d, got_o, got_down = _attn_fwd(qd, kd, vd, False, scale_d, "attn_dil", gather=[own_later[0], own_later[2]])
    gathered = [got_o, got_up, got_down]
    w_o_f = gathered[0].reshape(D_MODEL, D_MODEL)
    w_up_f = _cols_from_shards(gathered[1])
    w_down_f = gathered[2].reshape(D_FF, D_MODEL)
    mix_in = jnp.concatenate([o_m, o_d], axis=1)
    mix = _mm(mix_in, w_o_f, "nn", F32, 1024, D_MODEL, "mm_o")
    x1, h2 = _resid_prenorm(xs, mix, g1, g_ffn_norm, sc2, sh2)
    up = _mm(h2, w_up_f, "nn", F32, 1024, CONV_TILE, "mm_up")
    act = _conv_gate(up, w_conv_f, b_conv)
    ffn = _mm(act, w_down_f, "nn", F32, 512, D_MODEL, "mm_down")
    dy, dffn, dg2, loss_part = _final(x1, ffn, tgt, g2)

    da = _mm(dffn, w_down_f, "nt", F32, 1024, CONV_TILE, "mm_down_dx")
    gw_down = _mm(act, dffn, "tn", F32, 256, D_MODEL, "mm_down_dw")
    dup_g, dup_v, dbg, dbv, dwg, dwv = _gate_bwd(up, da, w_conv_f, b_conv)
    dup = jnp.concatenate([dup_g, dup_v], axis=1)
    early_names = ("w_up", "w_down", "w_o")
    gw_up = _mm(h2, dup, "tn", F32, 1024, CONV_TILE, "mm_up_dw", col_shards=True)
    early = [halves(gw_up), halves(gw_down.reshape(N_CHIP, D_FF // N_CHIP, D_MODEL))]
    dh2, *early_sib = _mm(dup, w_up_f, "nt", F32, 256, 512, "mm_up_dx", swap=early, b_outer=True)
    dx1, dmix, acc2 = _ffnnorm_bwd(dh2, x1, dy, mix, g_ffn_norm, sc2, g1)
    gw_o = _mm(mix_in, dmix, "tn", F32, 1024, D_MODEL, "mm_o_dw")
    early.append(halves(gw_o.reshape(N_CHIP, D_MODEL // N_CHIP, D_MODEL)))
    dmix_in, sib_o = _mm(dmix, w_o_f, "nt", F32, 512, D_MODEL, "mm_o_dx", swap=early[2:])
    early_sib.append(sib_o)
    early_sums = [_pair_sum(g, a, c_idx, "pair_sum_" + n) for g, a, n in zip(early, early_sib, early_names)]
    dqm, dkm, dvm, *early_recv = _attn_bwd(qm, km, vm, o_m, dmix_in, 0, lse_m, True, scale_m, "attn_mla_bwd",
                                           scatter=early_sums[:1])
    dqd, dkd, dvd, *early_recv_d = _attn_bwd(qd, kd, vd, o_d, dmix_in, DIL_W // LANE, lse_d, False, scale_d,
                                             "attn_dil_bwd", scatter=early_sums[1:])
    early_recv = early_recv + early_recv_d
    dq_raw, dkv_raw, dkpe_b, dqd_b, dkd_b, dvd_b, dgains = _attn_prep_bwd(
        dqm, dkm, dvm, dqd, dkd, dvd, q_raw, kv_raw, proj, tab, gains, consts)
    dql = _mm(dq_raw, w_qb_p, "nt", F32, 1024, Q_LORA, "mm_qb_dx")
    gw_qb = _unpad_w_qb(_mm(ql, dq_raw, "tn", F32, Q_LORA, HEADS * LANE, "mm_qb_dw"))
    dkvl = _mm(dkv_raw, w_kvb_p, "nt", F32, 1024, KV_LORA, "mm_kvb_dx")
    gw_kvb = _unpad_w_kvb(_mm(kvl, dkv_raw, "tn", F32, KV_LORA, HEADS * LANE + DIL_W, "mm_kvb_dw"))
    dqlat_b, dkvlat_b, dglat = _latnorm_bwd(dql, dkvl, proj, g_q_lat, g_kv_lat)
    dproj = jnp.concatenate([dqlat_b, dkvlat_b, dkpe_b[:, KPE_OFF:KPE_OFF + ROPE], dqd_b, dkd_b, dvd_b], axis=1)
    gw_in = _mm(h, dproj, "tn", F32, 512, IN_COLS, "mm_in_dw")
    late_names = ("w_in", "w_q_b", "w_kv_b")
    late = [halves(_cols_to_shards(gw_in)), halves(_cols_to_shards(gw_qb)), halves(_cols_to_shards(gw_kvb))]
    dh, *late_sib = _mm(dproj, w_in_f, "nt", F32, 512, D_MODEL, "mm_in_dx", swap=late)
    grad_x, acc1 = _mixnorm_bwd(dh, xs, dx1, g_mix_norm, sc1)

    packed = _pack_small(acc1, acc2, dg2, dglat, dgains, dbg, dbv, dwg, dwv, loss_part)
    late_sums = [_pair_sum(g, a, c_idx, "pair_sum_" + n) for g, a, n in zip(late, late_sib, late_names)]
    *late_recv, gathered_small = _scatter_and_gather(late_sums, packed, "rs_scatter_late")

    grad_b_ada, *small_grads, gconv_full, loss_sum = _sum_unpack(gathered_small)
    grads = {"b_ada": grad_b_ada}
    grads.update({n: g for (n, _), g in zip(SMALL_WIDTHS, small_grads)})
    shard_cols = UP_W // N_CHIP
    grads["w_conv"] = lax.dynamic_slice_in_dim(gconv_full, q0 * shard_cols, shard_cols, axis=1)
    dmod_all = gathered_small[:, 0, :6 * D_MODEL]
    grads["w_ada"] = _ada_bwd(c_all, lax.dynamic_slice_in_dim(dmod_all, q0 * ada_cols, ada_cols, axis=1))

    big_names = late_names + early_names
    half_sums = [_shard_sum(p, b, qc_idx, "shard_sum_" + n)
                 for p, b, n in zip(late_sums + early_sums, list(late_recv) + list(early_recv), big_names)]
    for n, full in zip(big_names, _join_halves(half_sums)):
        grads[n] = full.reshape(2 * full.shape[1], full.shape[2])

    delta, new_m, new_v = {}, {}, {}
    for n in ("w_ada", "w_in", "w_q_b", "w_kv_b", "w_o", "w_up", "w_conv", "w_down"):
        operands = (weights[n], grads[n], mom_m[n], mom_v[n])
        flipped = n in ("w_in", "w_q_b")
        if flipped:
            operands = [jnp.swapaxes(a, 0, 1) for a in operands]
            grads[n] = jnp.swapaxes(operands[1], 0, 1)
        if n == "w_ada":
            operands = _in_hbm(*operands)
        delta[n], new_m[n], new_v[n] = _adamw(*operands, "adamw_" + n)
        if flipped:
            delta[n], new_m[n], new_v[n] = (jnp.swapaxes(a, 0, 1) for a in (delta[n], new_m[n], new_v[n]))
    vec_names = ("b_ada",) + tuple(n for n, _ in SMALL_WIDTHS)
    sd, sm, sv = _adamw_vectors(*[[d_[n] for n in vec_names] for d_ in (small_w, grads, mom_m, mom_v)])
    for k, n in enumerate(vec_names):
        delta[n], new_m[n], new_v[n] = sd[k], sm[k], sv[k]

    loss = loss_sum[0, 0]
    order = ("w_ada", "b_ada", "g_mix_norm", "w_in", "g_q_lat", "w_q_b", "g_kv_lat", "w_kv_b", "g_mla_q_nope", "g_mla_q_pe",
             "g_mla_k_nope", "g_mla_k_pe", "g_dil_q", "g_dil_k", "w_o", "g_ffn_norm", "w_up", "w_conv", "b_conv", "w_down")
    lead = lambda n, z: z[None] if n.startswith("w_") else z
    outs = [loss, grad_x[None]]
    for d_ in (grads, delta, new_m, new_v):
        outs += [lead(n, d_[n]) for n in order]
    return tuple(outs)
```

```python
import functools

import numpy as np
import jax
import jax.numpy as jnp
from jax import lax
from jax.experimental import pallas as pl
from jax.experimental.pallas import tpu as pltpu

F32 = jnp.float32
BF16 = jnp.bfloat16
I32 = jnp.int32

D_MODEL = 1024
HEADS = 8
NOPE = 64
ROPE = 32
Q_LORA = 512
KV_LORA = 256
DIL_DIM = 64
DIL_W = HEADS * DIL_DIM
D_FF = 2816
UP_W = 2 * D_FF
IN_COLS = Q_LORA + KV_LORA + ROPE + 3 * DIL_W
ROPE_THETA = 10000.0
EPS = 1e-6
NEG_INF = -1e30
N_DEV = 8
N_CHIP = 4

ADAM_LR = 0.001
ADAM_B1 = 0.9
ADAM_B2 = 0.999
ADAM_EPS = 1e-08
ADAM_WD = 0.01
ADAM_STEP = 10

LANE = 128
ROW_TILE = 256
NORM_TILE = 512
ATT_TQ = 512
ATT_TK = 256
ATT_TK_BWD = 512
LOG2E = 1.4426950408889634
LN2 = 0.6931471805599453
VMEM_CAP = 56 * 1024 * 1024
VMEM_FLOOR = 32 * 1024 * 1024

P_QLAT, P_QD, P_KD, P_VD, P_KVLAT, P_KPE = 0, 512, 1024, 1536, 2048, 2304
P_COLS = 2432
KPE_OFF = 64

NN = (((1,), (0,)), ((), ()))
NT = (((1,), (1,)), ((), ()))
TN = (((0,), (0,)), ((), ()))
HIGHEST = lax.Precision.HIGHEST
MESH = pl.DeviceIdType.MESH


def _params(sem=None, est_bytes=0):
    limit = int(min(max(2 * est_bytes + (4 << 20), VMEM_FLOOR), VMEM_CAP))
    if sem is None:
        return pltpu.CompilerParams(vmem_limit_bytes=limit)
    return pltpu.CompilerParams(dimension_semantics=sem, vmem_limit_bytes=limit)


def _nbytes(shape, dtype):
    return int(np.prod(shape)) * jnp.dtype(dtype).itemsize


def _in_hbm(*xs):
    return [pltpu.with_memory_space_constraint(x, pltpu.HBM) for x in xs]


def _mm(a, b, dims, out_dtype, tm, tn, name, col_shards=False, swap=(), b_outer=False):
    def spec(block, index):
        if b_outer:
            return pl.BlockSpec(block, lambda g0, g1: index(g1, g0))
        return pl.BlockSpec(block, index)

    if dims == "nn":
        (m, k), (k2, n) = a.shape, b.shape
        a_spec = spec((tm, k), lambda i, j: (i, 0))
        b_spec = spec((k, tn), lambda i, j: (0, j))
        dn = NN
    elif dims == "nt":
        (m, k), (n, k2) = a.shape, b.shape
        a_spec = spec((tm, k), lambda i, j: (i, 0))
        b_spec = spec((tn, k), lambda i, j: (j, 0))
        dn = NT
    else:
        (k, m), (k2, n) = a.shape, b.shape
        a_spec = spec((k, tm), lambda i, j: (0, i))
        b_spec = spec((k, tn), lambda i, j: (0, j))
        dn = TN
    assert k == k2 and m % tm == 0 and n % tn == 0, (name, a.shape, b.shape, tm, tn)

    nw = len(swap)
    grid = (n // tn, m // tm) if b_outer else (m // tm, n // tn)

    def body(*refs):
        a_ref, b_ref, o_ref = refs[0], refs[1], refs[2 + nw]
        comm = (refs[2:2 + nw], refs[3 + nw:3 + 2 * nw]) + tuple(refs[3 + 2 * nw:])
        if nw:
            @pl.when((pl.program_id(0) == 0) & (pl.program_id(1) == 0))
            def _():
                _PairSwap(*comm).start()

        o_ref[...] = lax.dot_general(a_ref[...], b_ref[...], dn, preferred_element_type=F32).astype(o_ref.dtype)

        if nw:
            @pl.when((pl.program_id(0) == grid[0] - 1) & (pl.program_id(1) == grid[1] - 1))
            def _():
                _PairSwap(*comm).finish()

    est = _nbytes((tm, k), a.dtype) + _nbytes((tn, k), b.dtype) + _nbytes((tm, tn), F32) + _nbytes((tm, tn), out_dtype)
    if col_shards:
        out_spec = spec((None, tm, tn), lambda i, j: (j, i, 0))
        out_shape = jax.ShapeDtypeStruct((n // tn, m, tn), out_dtype)
    else:
        out_spec = spec((tm, tn), lambda i, j: (i, j))
        out_shape = jax.ShapeDtypeStruct((m, n), out_dtype)
    out = pl.pallas_call(
        body, name=name, grid=grid,
        in_specs=[a_spec, b_spec] + [ANY] * nw,
        out_specs=[out_spec] + [ANY] * nw,
        out_shape=[out_shape] + _PairSwap.out_shapes(swap),
        scratch_shapes=_PairSwap.semaphores(nw) if nw else [],
        compiler_params=_params(("arbitrary", "arbitrary") if nw else ("parallel", "parallel"), est),
    )(a, b, *swap)
    return out if nw else out[0]


def _seg_consts():
    seg_q = np.zeros((HEADS * LANE, LANE), np.float32)
    inv_q = np.zeros((1, LANE), np.float32)
    seg_k = np.zeros((HEADS * LANE, LANE), np.float32)
    inv_k = np.zeros((1, LANE), np.float32)
    seg_d = np.zeros((DIL_W, LANE), np.float32)
    inv_d = np.zeros((1, LANE), np.float32)
    for h in range(HEADS):
        seg_q[h * LANE:h * LANE + NOPE, 2 * h] = 1.0
        seg_q[h * LANE + NOPE:h * LANE + NOPE + ROPE, 2 * h + 1] = 1.0
        inv_q[0, 2 * h], inv_q[0, 2 * h + 1] = 1.0 / NOPE, 1.0 / ROPE
        seg_k[h * LANE:h * LANE + NOPE, h] = 1.0
        inv_k[0, h] = 1.0 / NOPE
        seg_d[h * DIL_DIM:(h + 1) * DIL_DIM, h] = 1.0
        inv_d[0, h] = 1.0 / DIL_DIM
    fold_q = np.tile(np.eye(LANE, dtype=np.float32), (HEADS, 1))
    fold_d = np.zeros((DIL_W, LANE), np.float32)
    fold_d[np.arange(DIL_W), np.arange(DIL_W) % DIL_DIM] = 1.0
    j = lambda v: jnp.asarray(v)
    b = lambda v: jnp.asarray(v, dtype=BF16)
    return dict(seg_q=b(seg_q), exp_q=b(seg_q.T.copy()), inv_q=j(inv_q), seg_k=b(seg_k), exp_k=b(seg_k.T.copy()),
                inv_k=j(inv_k), seg_d=b(seg_d), exp_d=b(seg_d.T.copy()), inv_d=j(inv_d), fold_q=j(fold_q), fold_d=j(fold_d))


def _rope_consts():
    inv_d = jnp.power(ROPE_THETA, -2.0 * jnp.arange(DIL_DIM // 2, dtype=F32) / DIL_DIM)
    inv_q = jnp.power(ROPE_THETA, -2.0 * jnp.arange(ROPE // 2, dtype=F32) / ROPE)
    lanes = np.arange(LANE)
    freq_d = inv_d[lanes % (DIL_DIM // 2)]
    in_pe = (lanes >= KPE_OFF) & (lanes < KPE_OFF + ROPE)
    freq_q = jnp.where(jnp.asarray(in_pe), inv_q[(lanes - KPE_OFF) % (ROPE // 2)], 0.0)
    sign_d = np.where(lanes % DIL_DIM < DIL_DIM // 2, -1.0, 1.0).astype(np.float32)
    sign_q = np.where(in_pe, np.where((lanes - KPE_OFF) < ROPE // 2, -1.0, 1.0), 0.0).astype(np.float32)
    zeros, ones = np.zeros(LANE, np.float32), np.ones(LANE, np.float32)
    freq = jnp.concatenate([freq_d, freq_d, freq_q, freq_q])[None, :]
    csel = jnp.asarray(np.concatenate([ones, zeros, ones, zeros]))[None, :]
    ssel = jnp.asarray(np.concatenate([zeros, sign_d, zeros, sign_q]))[None, :]
    return freq, csel, ssel


def _full(shape):
    return pl.BlockSpec(shape, lambda *_: (0,) * len(shape))


def _tile_lanes(x, n):
    return jnp.concatenate([x] * n, axis=1)


def _rms(x):
    return lax.rsqrt(jnp.mean(x * x, axis=-1, keepdims=True) + EPS)


def _prenorm(x, gain, scale, shift, name):
    s, d = x.shape

    def body(x_ref, g_ref, sc_ref, sh_ref, h_ref):
        xv = x_ref[...]
        h = (xv * _rms(xv)) * g_ref[...] * (1.0 + sc_ref[...]) + sh_ref[...]
        h_ref[...] = h.astype(BF16)

    row = pl.BlockSpec((NORM_TILE, d), lambda i: (i, 0))
    return pl.pallas_call(
        body, name=name, grid=(s // NORM_TILE,),
        in_specs=[row, _full((1, d)), _full((1, d)), _full((1, d))],
        out_specs=row, out_shape=jax.ShapeDtypeStruct((s, d), BF16),
        compiler_params=_params(("parallel",)),
    )(x, gain, scale, shift)


def _latnorm(proj, g_q, g_kv):
    s = proj.shape[0]

    def body(q_ref, kv_ref, gq_ref, gkv_ref, ql_ref, kvl_ref):
        q, kv = q_ref[...], kv_ref[...]
        ql_ref[...] = ((q * _rms(q)) * gq_ref[...]).astype(BF16)
        kvl_ref[...] = ((kv * _rms(kv)) * gkv_ref[...]).astype(BF16)

    return pl.pallas_call(
        body, name="latnorm", grid=(s // NORM_TILE,),
        in_specs=[pl.BlockSpec((NORM_TILE, Q_LORA), lambda i: (i, P_QLAT // Q_LORA)),
                  pl.BlockSpec((NORM_TILE, KV_LORA), lambda i: (i, P_KVLAT // KV_LORA)),
                  _full((1, Q_LORA)), _full((1, KV_LORA))],
        out_specs=[pl.BlockSpec((NORM_TILE, Q_LORA), lambda i: (i, 0)), pl.BlockSpec((NORM_TILE, KV_LORA), lambda i: (i, 0))],
        out_shape=[jax.ShapeDtypeStruct((s, Q_LORA), BF16), jax.ShapeDtypeStruct((s, KV_LORA), BF16)],
        compiler_params=_params(("parallel",)),
    )(proj, proj, g_q, g_kv)


def _dot01(v, mat01):
    hi = v.astype(BF16)
    lo = (v - hi.astype(F32)).astype(BF16)
    return jnp.dot(hi, mat01, preferred_element_type=F32) + jnp.dot(lo, mat01, preferred_element_type=F32)


def _seg_rinv(x, seg, exp, inv):
    r = lax.rsqrt(_dot01(x * x, seg) * inv + EPS)
    return _dot01(r, exp)


def _seg_mean(v, seg, exp, inv):
    return _dot01(_dot01(v, seg) * inv, exp)


def _swap_halves(x, half):
    n = x.shape[1]
    lane = lax.broadcasted_iota(I32, (1, n), 1)
    first = (lane & (2 * half - 1)) < half
    return jnp.where(first, pltpu.roll(x, n - half, 1), pltpu.roll(x, half, 1))


def _rope(x, cos, sin_signed, half):
    return x * cos + _swap_halves(x, half) * sin_signed


def _rope_bwd(dy, cos, sin_signed, half):
    return dy * cos + _swap_halves(dy * sin_signed, half)


def _pe_lane_mask(n):
    lane = lax.broadcasted_iota(I32, (1, n), 1) & (LANE - 1)
    return (lane >= KPE_OFF) & (lane < KPE_OFF + ROPE)


def _attn_prep(q_raw, kv_raw, proj, tab, gains, consts):
    s = q_raw.shape[0]
    hw = HEADS * LANE

    def body(q_ref, kv_ref, kpe_ref, qd_ref, kd_ref, vd_ref, tab_ref,
             gq_ref, gk_ref, gkpe_ref, gdq_ref, gdk_ref,
             segq_ref, expq_ref, invq_ref, segk_ref, expk_ref, invk_ref, segd_ref, expd_ref, invd_ref,
             qm_ref, km_ref, vm_ref, qdo_ref, kdo_ref, vdo_ref):
        tab_v = tab_ref[...]
        cos_d, sin_d = _tile_lanes(tab_v[:, 0:LANE], DIL_W // LANE), _tile_lanes(tab_v[:, LANE:2 * LANE], DIL_W // LANE)
        cos_q1, sin_q1 = tab_v[:, 2 * LANE:3 * LANE], tab_v[:, 3 * LANE:4 * LANE]
        cos_q, sin_q = _tile_lanes(cos_q1, HEADS), _tile_lanes(sin_q1, HEADS)

        q = q_ref[...]
        qn = q * _seg_rinv(q, segq_ref[...], expq_ref[...], invq_ref[...]) * gq_ref[...]
        qm_ref[...] = _rope(qn, cos_q, sin_q, ROPE // 2).astype(BF16)

        kv = kv_ref[...]
        kp = kv[:, :hw]
        kn = kp * _seg_rinv(kp, segk_ref[...], expk_ref[...], invk_ref[...]) * gk_ref[...]
        kpe = kpe_ref[...]
        r_pe = lax.rsqrt(jnp.sum(kpe * kpe, axis=-1, keepdims=True) * (1.0 / ROPE) + EPS)
        kpe_r = _rope(kpe * r_pe * gkpe_ref[...], cos_q1, sin_q1, ROPE // 2)
        km_ref[...] = (kn + _tile_lanes(kpe_r, HEADS)).astype(BF16)
        vm_ref[...] = kv[:, hw:].astype(BF16)

        qd = qd_ref[...]
        qdn = qd * _seg_rinv(qd, segd_ref[...], expd_ref[...], invd_ref[...]) * gdq_ref[...]
        qdo_ref[...] = _rope(qdn, cos_d, sin_d, DIL_DIM // 2).astype(BF16)
        kd = kd_ref[...]
        kdn = kd * _seg_rinv(kd, segd_ref[...], expd_ref[...], invd_ref[...]) * gdk_ref[...]
        kdo_ref[...] = _rope(kdn, cos_d, sin_d, DIL_DIM // 2).astype(BF16)
        vdo_ref[...] = vd_ref[...].astype(BF16)

    t = ROW_TILE
    row = lambda w, cb=0: pl.BlockSpec((t, w), lambda i: (i, cb))
    c = consts
    return pl.pallas_call(
        body, name="attn_prep", grid=(s // t,),
        in_specs=[row(hw), row(hw + DIL_W), row(LANE, P_KPE // LANE), row(DIL_W, P_QD // DIL_W), row(DIL_W, P_KD // DIL_W),
                  row(DIL_W, P_VD // DIL_W), row(4 * LANE),
                  _full((1, hw)), _full((1, hw)), _full((1, LANE)), _full((1, DIL_W)), _full((1, DIL_W)),
                  _full((hw, LANE)), _full((LANE, hw)), _full((1, LANE)), _full((hw, LANE)), _full((LANE, hw)), _full((1, LANE)),
                  _full((DIL_W, LANE)), _full((LANE, DIL_W)), _full((1, LANE))],
        out_specs=[row(hw), row(hw), row(DIL_W), row(DIL_W), row(DIL_W), row(DIL_W)],
        out_shape=[jax.ShapeDtypeStruct((s, hw), BF16), jax.ShapeDtypeStruct((s, hw), BF16)]
        + [jax.ShapeDtypeStruct((s, DIL_W), BF16)] * 4,
        compiler_params=_params(("parallel",), 24 << 20),
    )(*_in_hbm(q_raw, kv_raw, proj, proj, proj, proj), tab, gains["q"], gains["k"], gains["kpe"], gains["dq"], gains["dk"],
      c["seg_q"], c["exp_q"], c["inv_q"], c["seg_k"], c["exp_k"], c["inv_k"], c["seg_d"], c["exp_d"], c["inv_d"])


def _attn_prep_bwd(dqm, dkm, dvm, dqd, dkd, dvd, q_raw, kv_raw, proj, tab, gains, consts):
    s = q_raw.shape[0]
    hw = HEADS * LANE
    n_steps = s // ROW_TILE

    def body(dqm_ref, dkm_ref, dvm_ref, dqd_ref, dkd_ref, dvd_ref, q_ref, kv_ref, kpe_ref, qd_ref, kd_ref, tab_ref,
             gq_ref, gk_ref, gkpe_ref, gdq_ref, gdk_ref,
             segq_ref, expq_ref, invq_ref, segk_ref, expk_ref, invk_ref, segd_ref, expd_ref, invd_ref, foldq_ref, foldd_ref,
             dq_ref, dkv_ref, dkpe_ref, dqdo_ref, dkdo_ref, dvdo_ref, dg_ref, acc_ref):
        i = pl.program_id(0)

        @pl.when(i == 0)
        def _():
            acc_ref[...] = jnp.zeros_like(acc_ref)

        tab_v = tab_ref[...]
        cos_d, sin_d = _tile_lanes(tab_v[:, 0:LANE], DIL_W // LANE), _tile_lanes(tab_v[:, LANE:2 * LANE], DIL_W // LANE)
        cos_q1, sin_q1 = tab_v[:, 2 * LANE:3 * LANE], tab_v[:, 3 * LANE:4 * LANE]
        cos_q, sin_q = _tile_lanes(cos_q1, HEADS), _tile_lanes(sin_q1, HEADS)

        def norm_bwd(x, dyg, gain, seg, exp, inv):
            rinv = _seg_rinv(x, seg, exp, inv)
            xn = x * rinv
            dxn = dyg * gain
            dx = rinv * (dxn - xn * _seg_mean(dxn * xn, seg, exp, inv))
            return dx, jnp.sum(dyg * xn, axis=0, keepdims=True)

        dq, gq_l = norm_bwd(q_ref[...], _rope_bwd(dqm_ref[...], cos_q, sin_q, ROPE // 2), gq_ref[...],
                            segq_ref[...], expq_ref[...], invq_ref[...])
        dq_ref[...] = dq.astype(BF16)

        dkm = dkm_ref[...]
        kv = kv_ref[...]
        dkp, gk_l = norm_bwd(kv[:, :hw], dkm, gk_ref[...], segk_ref[...], expk_ref[...], invk_ref[...])
        dkv_ref[:, :hw] = dkp.astype(BF16)
        dkv_ref[:, hw:] = dvm_ref[...].astype(BF16)

        dkpe_r = dkm[:, 0:LANE]
        for h in range(1, HEADS):
            dkpe_r = dkpe_r + dkm[:, h * LANE:(h + 1) * LANE]
        dkpe_r = jnp.where(_pe_lane_mask(LANE), dkpe_r, 0.0)
        dyg = _rope_bwd(dkpe_r, cos_q1, sin_q1, ROPE // 2)
        kpe = kpe_ref[...]
        r_pe = lax.rsqrt(jnp.sum(kpe * kpe, axis=-1, keepdims=True) * (1.0 / ROPE) + EPS)
        xn = kpe * r_pe
        dxn = dyg * gkpe_ref[...]
        dkpe = r_pe * (dxn - xn * (jnp.sum(dxn * xn, axis=-1, keepdims=True) * (1.0 / ROPE)))
        dkpe_ref[...] = dkpe.astype(BF16)
        gkpe_l = jnp.sum(dyg * xn, axis=0, keepdims=True)

        dqd_v, gdq_l = norm_bwd(qd_ref[...], _rope_bwd(dqd_ref[...], cos_d, sin_d, DIL_DIM // 2), gdq_ref[...],
                                segd_ref[...], expd_ref[...], invd_ref[...])
        dqdo_ref[...] = dqd_v.astype(BF16)
        dkd_v, gdk_l = norm_bwd(kd_ref[...], _rope_bwd(dkd_ref[...], cos_d, sin_d, DIL_DIM // 2), gdk_ref[...],
                                segd_ref[...], expd_ref[...], invd_ref[...])
        dkdo_ref[...] = dkd_v.astype(BF16)
        dvdo_ref[...] = dvd_ref[...].astype(BF16)

        acc_ref[0:1, :] += gq_l
        acc_ref[1:2, :] += gk_l
        acc_ref[2:3, 0:LANE] += gkpe_l
        acc_ref[3:4, 0:DIL_W] += gdq_l
        acc_ref[4:5, 0:DIL_W] += gdk_l

        @pl.when(i == n_steps - 1)
        def _():
            acc = acc_ref[...]
            fq = jnp.dot(acc, foldq_ref[...], precision=HIGHEST, preferred_element_type=F32)
            fd = jnp.dot(acc[:, 0:DIL_W], foldd_ref[...], precision=HIGHEST, preferred_element_type=F32)
            rows = lax.broadcasted_iota(I32, (8, LANE), 0)
            base = jnp.where(rows < 2, fq, jnp.where(rows == 2, acc[:, 0:LANE], fd))
            at0 = pltpu.roll(base, LANE - KPE_OFF, 1)
            dg_ref[...] = jnp.where(rows == 5, pltpu.roll(at0, 5, 0), jnp.where(rows == 2, at0, base))

    t = ROW_TILE
    row = lambda w, cb=0: pl.BlockSpec((t, w), lambda i: (i, cb))
    c = consts
    return pl.pallas_call(
        body, name="attn_prep_bwd", grid=(n_steps,),
        in_specs=[row(hw), row(hw), row(DIL_W), row(DIL_W), row(DIL_W), row(DIL_W),
                  row(hw), row(hw + DIL_W), row(LANE, P_KPE // LANE), row(DIL_W, P_QD // DIL_W), row(DIL_W, P_KD // DIL_W),
                  row(4 * LANE),
                  _full((1, hw)), _full((1, hw)), _full((1, LANE)), _full((1, DIL_W)), _full((1, DIL_W)),
                  _full((hw, LANE)), _full((LANE, hw)), _full((1, LANE)), _full((hw, LANE)), _full((LANE, hw)), _full((1, LANE)),
                  _full((DIL_W, LANE)), _full((LANE, DIL_W)), _full((1, LANE)), _full((hw, LANE)), _full((DIL_W, LANE))],
        out_specs=[row(hw), row(hw + DIL_W), row(LANE), row(DIL_W), row(DIL_W), row(DIL_W), _full((8, LANE))],
        out_shape=[jax.ShapeDtypeStruct((s, hw), BF16), jax.ShapeDtypeStruct((s, hw + DIL_W), BF16),
                   jax.ShapeDtypeStruct((s, LANE), BF16)] + [jax.ShapeDtypeStruct((s, DIL_W), BF16)] * 3
        + [jax.ShapeDtypeStruct((8, LANE), F32)],
        scratch_shapes=[pltpu.VMEM((8, hw), F32)],
        compiler_params=_params(("arbitrary",), 28 << 20),
    )(*_in_hbm(dqm, dkm, dvm, dqd, dkd, dvd, q_raw, kv_raw, proj, proj, proj), tab,
      gains["q"], gains["k"], gains["kpe"], gains["dq"], gains["dk"],
      c["seg_q"], c["exp_q"], c["inv_q"], c["seg_k"], c["exp_k"], c["inv_k"], c["seg_d"], c["exp_d"], c["inv_d"],
      c["fold_q"], c["fold_d"])


def _latnorm_bwd(dql, dkvl, proj, g_q, g_kv):
    s = proj.shape[0]
    n_steps = s // NORM_TILE

    def body(dql_ref, dkvl_ref, q_ref, kv_ref, gq_ref, gkv_ref, dq_ref, dkv_ref, dg_ref):
        i = pl.program_id(0)

        @pl.when(i == 0)
        def _():
            dg_ref[...] = jnp.zeros_like(dg_ref)

        def one(x, dyg, gain):
            r = _rms(x)
            xn = x * r
            dxn = dyg * gain
            dx = r * (dxn - xn * jnp.mean(dxn * xn, axis=-1, keepdims=True))
            return dx, jnp.sum(dyg * xn, axis=0, keepdims=True)

        dq, gq_l = one(q_ref[...], dql_ref[...], gq_ref[...])
        dkv, gkv_l = one(kv_ref[...], dkvl_ref[...], gkv_ref[...])
        dq_ref[...] = dq.astype(BF16)
        dkv_ref[...] = dkv.astype(BF16)
        dg_ref[0:1, :] += gq_l
        dg_ref[1:2, 0:KV_LORA] += gkv_l

    t = NORM_TILE
    return pl.pallas_call(
        body, name="latnorm_bwd", grid=(n_steps,),
        in_specs=[pl.BlockSpec((t, Q_LORA), lambda i: (i, 0)), pl.BlockSpec((t, KV_LORA), lambda i: (i, 0)),
                  pl.BlockSpec((t, Q_LORA), lambda i: (i, P_QLAT // Q_LORA)),
                  pl.BlockSpec((t, KV_LORA), lambda i: (i, P_KVLAT // KV_LORA)),
                  _full((1, Q_LORA)), _full((1, KV_LORA))],
        out_specs=[pl.BlockSpec((t, Q_LORA), lambda i: (i, 0)), pl.BlockSpec((t, KV_LORA), lambda i: (i, 0)), _full((8, Q_LORA))],
        out_shape=[jax.ShapeDtypeStruct((s, Q_LORA), BF16), jax.ShapeDtypeStruct((s, KV_LORA), BF16),
                   jax.ShapeDtypeStruct((8, Q_LORA), F32)],
        compiler_params=_params(("arbitrary",)),
    )(dql, dkvl, proj, proj, g_q, g_kv)


def _resid_prenorm(x, mix, g1, gain, scale, shift):
    s, d = x.shape

    def body(x_ref, mix_ref, g1_ref, g_ref, sc_ref, sh_ref, x1_ref, h_ref):
        x1 = x_ref[...] + g1_ref[...] * mix_ref[...]
        x1_ref[...] = x1
        h_ref[...] = ((x1 * _rms(x1)) * g_ref[...] * (1.0 + sc_ref[...]) + sh_ref[...]).astype(BF16)

    row = pl.BlockSpec((NORM_TILE, d), lambda i: (i, 0))
    vec = _full((1, d))
    return pl.pallas_call(
        body, name="resid_prenorm", grid=(s // NORM_TILE,),
        in_specs=[row, row, vec, vec, vec, vec], out_specs=[row, row],
        out_shape=[jax.ShapeDtypeStruct((s, d), F32), jax.ShapeDtypeStruct((s, d), BF16)],
        compiler_params=_params(("parallel",)),
    )(x, mix, g1, gain, scale, shift)


CONV_TILE = 1408
HALO = 8


def _shift_down(x, halo, k):
    t = x.shape[0]
    row = lax.broadcasted_iota(I32, (t, 1), 0)
    out = pltpu.roll(x, k, 0)
    for r in range(k):
        out = jnp.where(row == r, halo[HALO - k + r:HALO - k + r + 1, :], out)
    return out


def _shift_up(x, halo, k):
    t = x.shape[0]
    row = lax.broadcasted_iota(I32, (t, 1), 0)
    out = pltpu.roll(x, t - k, 0)
    for r in range(k):
        out = jnp.where(row == t - k + r, halo[r:r + 1, :], out)
    return out


def _conv_fwd(x, halo, w, b):
    p1, p2 = _shift_down(x, halo, 1), _shift_down(x, halo, 2)
    u = b + p2 * w[0:1, :]
    u = u + p1 * w[1:2, :]
    u = u + x * w[2:3, :]
    return u, p1, p2


def _sigmoid(x):
    return 0.5 * jnp.tanh(0.5 * x) + 0.5


def _conv_gate(up, w_conv, b_conv):
    s = up.shape[0]
    t = ROW_TILE
    nj = D_FF // CONV_TILE
    hb = t // HALO

    def body(g_ref, v_ref, gh_ref, vh_ref, wg_ref, wv_ref, bg_ref, bv_ref, a_ref):
        live = (pl.program_id(0) > 0).astype(F32)
        ug, _, _ = _conv_fwd(g_ref[...], gh_ref[...] * live, wg_ref[...], bg_ref[...])
        uv, _, _ = _conv_fwd(v_ref[...], vh_ref[...] * live, wv_ref[...], bv_ref[...])
        a_ref[...] = (ug * _sigmoid(ug) * uv).astype(BF16)

    main = lambda off: pl.BlockSpec((t, CONV_TILE), lambda i, j: (i, j + off))
    halo = lambda off: pl.BlockSpec((HALO, CONV_TILE), lambda i, j: (jnp.maximum(i * hb - 1, 0), j + off))
    wsp = lambda off: pl.BlockSpec((3, CONV_TILE), lambda i, j: (0, j + off))
    bsp = lambda off: pl.BlockSpec((1, CONV_TILE), lambda i, j: (0, j + off))
    return pl.pallas_call(
        body, name="conv_gate", grid=(s // t, nj),
        in_specs=[main(0), main(nj), halo(0), halo(nj), wsp(0), wsp(nj), bsp(0), bsp(nj)],
        out_specs=pl.BlockSpec((t, CONV_TILE), lambda i, j: (i, j)),
        out_shape=jax.ShapeDtypeStruct((s, D_FF), BF16),
        compiler_params=_params(("parallel", "parallel"), 12 << 20),
    )(up, up, up, up, w_conv, w_conv, b_conv, b_conv)


def _gate_bwd(up, da, w_conv, b_conv):
    s = up.shape[0]
    t = ROW_TILE
    nj = D_FF // CONV_TILE
    hb = t // HALO
    n_i = s // t

    def body(g_ref, v_ref, gh_ref, vh_ref, gn_ref, vn_ref, da_ref, dan_ref, wg_ref, wv_ref, bg_ref, bv_ref,
             dupg_ref, dupv_ref, dbg_ref, dbv_ref, dwg_ref, dwv_ref):
        i = pl.program_id(1)

        @pl.when(i == 0)
        def _():
            for r in (dbg_ref, dbv_ref, dwg_ref, dwv_ref):
                r[...] = jnp.zeros_like(r)

        def d_gate(ug, uv, da_v):
            sg = _sigmoid(ug)
            return da_v * uv * (sg * (1.0 + ug * (1.0 - sg))), da_v * (ug * sg)

        live = (i > 0).astype(F32)
        xg, xv = g_ref[...], v_ref[...]
        wg, wv = wg_ref[...], wv_ref[...]
        ug, g1, g2 = _conv_fwd(xg, gh_ref[...] * live, wg, bg_ref[...])
        uv, v1, v2 = _conv_fwd(xv, vh_ref[...] * live, wv, bv_ref[...])
        dug, duv = d_gate(ug, uv, da_ref[...])

        more = (i < n_i - 1).astype(F32)
        ug_n, _, _ = _conv_fwd(gn_ref[...], xg[t - HALO:, :], wg, bg_ref[...])
        uv_n, _, _ = _conv_fwd(vn_ref[...], xv[t - HALO:, :], wv, bv_ref[...])
        dug_n, duv_n = d_gate(ug_n, uv_n, dan_ref[...] * more)

        def conv_t(du, du_n, w):
            return du * w[2:3, :] + _shift_up(du, du_n, 1) * w[1:2, :] + _shift_up(du, du_n, 2) * w[0:1, :]

        dupg_ref[...] = conv_t(dug, dug_n, wg).astype(BF16)
        dupv_ref[...] = conv_t(duv, duv_n, wv).astype(BF16)
        csum = lambda z: jnp.sum(z, axis=0, keepdims=True)
        dbg_ref[...] += csum(dug)
        dbv_ref[...] += csum(duv)
        dwg_ref[0:1, :] += csum(dug * g2)
        dwg_ref[1:2, :] += csum(dug * g1)
        dwg_ref[2:3, :] += csum(dug * xg)
        dwv_ref[0:1, :] += csum(duv * v2)
        dwv_ref[1:2, :] += csum(duv * v1)
        dwv_ref[2:3, :] += csum(duv * xv)

    last_halo = s // HALO - 1
    main = lambda off: pl.BlockSpec((t, CONV_TILE), lambda j, i: (i, j + off))
    halo = lambda off: pl.BlockSpec((HALO, CONV_TILE), lambda j, i: (jnp.maximum(i * hb - 1, 0), j + off))
    nxt = lambda off: pl.BlockSpec((HALO, CONV_TILE), lambda j, i: (jnp.minimum((i + 1) * hb, last_halo), j + off))
    wsp = lambda off: pl.BlockSpec((3, CONV_TILE), lambda j, i: (0, j + off))
    bsp = lambda off: pl.BlockSpec((1, CONV_TILE), lambda j, i: (0, j + off))
    outs = pl.pallas_call(
        body, name="gate_bwd", grid=(nj, n_i),
        in_specs=[main(0), main(nj), halo(0), halo(nj), nxt(0), nxt(nj), main(0), nxt(0),
                  wsp(0), wsp(nj), bsp(0), bsp(nj)],
        out_specs=[main(0), main(0),
                   pl.BlockSpec((1, CONV_TILE), lambda j, i: (0, j)), pl.BlockSpec((1, CONV_TILE), lambda j, i: (0, j)),
                   pl.BlockSpec((3, CONV_TILE), lambda j, i: (0, j)), pl.BlockSpec((3, CONV_TILE), lambda j, i: (0, j))],
        out_shape=[jax.ShapeDtypeStruct((s, D_FF), BF16), jax.ShapeDtypeStruct((s, D_FF), BF16),
                   jax.ShapeDtypeStruct((1, D_FF), F32), jax.ShapeDtypeStruct((1, D_FF), F32),
                   jax.ShapeDtypeStruct((3, D_FF), F32), jax.ShapeDtypeStruct((3, D_FF), F32)],
        compiler_params=_params(("parallel", "arbitrary"), 24 << 20),
    )(up, up, up, up, up, up, da, da, w_conv, w_conv, b_conv, b_conv)
    return outs


def _down_final(act, w_down, x1, tgt, g2):
    s, d = x1.shape
    k = act.shape[1]
    n_steps = s // NORM_TILE

    def body(a_ref, w_ref, x1_ref, t_ref, g2_ref, dy_ref, df_ref, dg2_ref, loss_ref, lacc_ref):
        i = pl.program_id(0)

        @pl.when(i == 0)
        def _():
            dg2_ref[...] = jnp.zeros_like(dg2_ref)
            lacc_ref[...] = jnp.zeros_like(lacc_ref)

        f = jnp.dot(a_ref[...], w_ref[...], preferred_element_type=F32)
        e = x1_ref[...] + g2_ref[...] * f - t_ref[...]
        dy = e * (1.0 / d)
        dy_ref[...] = dy
        df_ref[...] = (dy * g2_ref[...]).astype(BF16)
        dg2_ref[...] += jnp.sum(dy * f, axis=0, keepdims=True)
        lacc_ref[...] += jnp.sum(e * e, axis=0, keepdims=True)

        @pl.when(i == n_steps - 1)
        def _():
            loss_ref[...] = jnp.sum(lacc_ref[...], axis=1, keepdims=True) * (0.5 / d)

    row = pl.BlockSpec((NORM_TILE, d), lambda i: (i, 0))
    est = (_nbytes((NORM_TILE, k), BF16) + _nbytes((k, d), BF16) + 4 * _nbytes((NORM_TILE, d), F32))
    return pl.pallas_call(
        body, name="mm_down_final", grid=(n_steps,),
        in_specs=[pl.BlockSpec((NORM_TILE, k), lambda i: (i, 0)), _full((k, d)), row, row, _full((1, d))],
        out_specs=[row, row, _full((1, d)), _full((1, 1))],
        out_shape=[jax.ShapeDtypeStruct((s, d), F32), jax.ShapeDtypeStruct((s, d), BF16),
                   jax.ShapeDtypeStruct((1, d), F32), jax.ShapeDtypeStruct((1, 1), F32)],
        scratch_shapes=[pltpu.VMEM((1, d), F32)],
        compiler_params=_params(("arbitrary",), est),
    )(act, w_down, x1, tgt, g2)


def _ffnnorm_bwd(dh2, x1, dy, mix, gain, scale, g1):
    s, d = x1.shape
    n_steps = s // NORM_TILE

    def body(dh_ref, x_ref, dy_ref, mix_ref, g_ref, sc_ref, g1_ref, dx_ref, dm_ref, acc_ref):
        i = pl.program_id(0)

        @pl.when(i == 0)
        def _():
            acc_ref[...] = jnp.zeros_like(acc_ref)

        dh, x = dh_ref[...], x_ref[...]
        r = _rms(x)
        xn = x * r
        dn = dh * (1.0 + sc_ref[...])
        dxn = dn * g_ref[...]
        dx = dy_ref[...] + r * (dxn - xn * jnp.mean(dxn * xn, axis=-1, keepdims=True))
        dx_ref[...] = dx
        dm_ref[...] = (dx * g1_ref[...]).astype(BF16)
        csum = lambda z: jnp.sum(z, axis=0, keepdims=True)
        acc_ref[0:1, :] += csum(dh)
        acc_ref[1:2, :] += csum(dh * (xn * g_ref[...]))
        acc_ref[2:3, :] += csum(dn * xn)
        acc_ref[3:4, :] += csum(dx * mix_ref[...])

    row = pl.BlockSpec((NORM_TILE, d), lambda i: (i, 0))
    vec = _full((1, d))
    return pl.pallas_call(
        body, name="ffnnorm_bwd", grid=(n_steps,),
        in_specs=[row, row, row, row, vec, vec, vec],
        out_specs=[row, row, _full((8, d))],
        out_shape=[jax.ShapeDtypeStruct((s, d), F32), jax.ShapeDtypeStruct((s, d), BF16), jax.ShapeDtypeStruct((8, d), F32)],
        compiler_params=_params(("arbitrary",)),
    )(dh2, x1, dy, mix, gain, scale, g1)


def _mixnorm_bwd(dh, x, dx1, gain, scale):
    s, d = x.shape
    n_steps = s // NORM_TILE

    def body(dh_ref, x_ref, dx1_ref, g_ref, sc_ref, gx_ref, acc_ref):
        i = pl.program_id(0)

        @pl.when(i == 0)
        def _():
            acc_ref[...] = jnp.zeros_like(acc_ref)

        dh, x = dh_ref[...], x_ref[...]
        r = _rms(x)
        xn = x * r
        dn = dh * (1.0 + sc_ref[...])
        dxn = dn * g_ref[...]
        gx_ref[...] = dx1_ref[...] + r * (dxn - xn * jnp.mean(dxn * xn, axis=-1, keepdims=True))
        csum = lambda z: jnp.sum(z, axis=0, keepdims=True)
        acc_ref[0:1, :] += csum(dh)
        acc_ref[1:2, :] += csum(dh * (xn * g_ref[...]))
        acc_ref[2:3, :] += csum(dn * xn)

    row = pl.BlockSpec((NORM_TILE, d), lambda i: (i, 0))
    vec = _full((1, d))
    return pl.pallas_call(
        body, name="mixnorm_bwd", grid=(n_steps,),
        in_specs=[row, row, row, vec, vec],
        out_specs=[row, _full((8, d))],
        out_shape=[jax.ShapeDtypeStruct((s, d), F32), jax.ShapeDtypeStruct((8, d), F32)],
        compiler_params=_params(("arbitrary",)),
    )(dh, x, dx1, gain, scale)


def _key_count(d, dilated):
    if not dilated:
        return jnp.where(d >= 0, 1.0, 0.0)
    one = lambda cond: jnp.where(cond, 1.0, 0.0)
    cnt = one(d <= 128) + one(((d & 3) == 0) & (d <= 512)) + one((d & 15) == 0)
    return jnp.where(d >= 0, cnt, 0.0)


def _block_kinds(mla):
    return (0, "diag", "none") if mla else (NEAR_REACH, "near", "far")


NEAR_REACH = 512


def _near_offsets(tk, tq):
    return (NEAR_REACH - (tk - tq)) // tk + 1


def _scores_t(ka, qa, scale, kind, rel_t, offset, near_tabs=None):
    return _mask_scores(lax.dot_general(ka, qa, NT, preferred_element_type=F32), scale, kind, rel_t, offset, near_tabs)


def _fill_near_tables(bias_ref, cnt_ref, rel_t):
    tk, tq = rel_t.shape
    for idx in range(_near_offsets(tk, tq)):
        cnt = _key_count(rel_t + (tk - tq) + idx * tk, True)
        cnt_ref[idx] = cnt
        bias_ref[idx] = jnp.where(cnt > 0.0, 0.0, NEG_INF)


def _mask_scores(products, scale, kind, rel_t, offset, near_tabs=None):
    st = products * (scale * LOG2E)
    cnt = None
    if kind == "diag":
        st = jnp.where(rel_t + offset >= 0, st, NEG_INF)
    elif kind == "far":
        st = jnp.where((rel_t & 15) == 0, st, NEG_INF)
    elif kind == "near":
        bias_ref, cnt_ref = near_tabs
        tk, tq = rel_t.shape
        idx = (offset - (tk - tq)) // tk
        st = st + bias_ref[idx]
        cnt = cnt_ref[idx]
    return st, cnt


def _attn_fwd(q, k, v, mla, scale, name, gather=()):
    s = q.shape[0]
    qw = 2 * LANE if mla else LANE
    tq, tk = ATT_TQ, ATT_TK
    reach, kind_near, kind_far = _block_kinds(mla)
    assert s % tq == 0 and tq % tk == 0 and reach % tk == 0 and reach in (0, NEAR_REACH)
    ng = len(gather)
    last_step = HEADS // 2 - 1

    def body(*refs):
        q_ref, k_ref, v_ref = refs[:3]
        o_ref, lse_ref = refs[3 + ng:5 + ng]
        vt_ref, st_ref = refs[5 + 2 * ng:7 + 2 * ng]
        near_tabs = None if mla else refs[7 + 2 * ng:9 + 2 * ng]
        n_tabs = 0 if mla else 2
        comm = (refs[3:3 + ng], refs[5 + ng:5 + 2 * ng]) + tuple(refs[7 + n_tabs + 2 * ng:])
        if ng:
            @pl.when(pl.program_id(0) == 0)
            def _():
                _Gather(*comm).start()

            @pl.when(pl.program_id(0) == last_step)
            def _():
                _Gather(*comm).forward()

        lane = lax.broadcasted_iota(I32, (1, LANE), 1)
        rel_t = lax.broadcasted_iota(I32, (tk, tq), 1) - lax.broadcasted_iota(I32, (tk, tq), 0)
        if not mla:
            _fill_near_tables(*near_tabs, rel_t)

        def transpose_v(j, carry):
            c0 = pl.multiple_of(j * tk, tk)
            vt_ref[:, pl.ds(c0, tk)] = v_ref[pl.ds(c0, tk), :].astype(F32).T.astype(BF16)
            return carry

        lax.fori_loop(0, s // tk, transpose_v, 0)

        def q_block(qi, carry):
            r0 = pl.multiple_of(qi * tq, tq)
            kcols = [slice(a * LANE, (a + 1) * LANE) if mla else slice(0, LANE) for a in range(2)]
            qas = [q_ref[pl.ds(r0, tq), kcols[a]] for a in range(2)]
            if not mla:
                qas = [jnp.where(lane < DIL_DIM, qas[0], jnp.zeros_like(qas[0])),
                       jnp.where(lane >= DIL_DIM, qas[1], jnp.zeros_like(qas[1]))]

            n_k = (r0 + tq) // tk

            def products(kj):
                c0 = pl.multiple_of(kj * tk, tk)
                return [lax.dot_general(k_ref[pl.ds(c0, tk), kcols[a]], qas[a], NT, preferred_element_type=F32)
                        for a in range(2)]

            for a, pr in enumerate(products(0)):
                st_ref[0, a] = pr

            def k_block(kj, c, kind):
                c0 = pl.multiple_of(kj * tk, tk)
                slot = kj & 1
                ahead = products(jnp.minimum(kj + 1, n_k - 1))
                out = []
                for a in range(2):
                    m, l, acc = c[a]
                    st, cnt = _mask_scores(st_ref[slot, a], scale, kind, rel_t, r0 - c0, near_tabs)
                    st_ref[1 - slot, a] = ahead[a]
                    m_new = jnp.maximum(m, jnp.max(st, axis=0, keepdims=True))
                    alpha = jnp.exp2(m - m_new)
                    p = jnp.exp2(st - m_new)
                    if cnt is not None:
                        p = p * cnt
                    l = alpha * l + jnp.sum(p, axis=0, keepdims=True)
                    vt = vt_ref[a * DIL_DIM:(a + 1) * DIL_DIM, pl.ds(c0, tk)]
                    acc = alpha * acc + jnp.dot(vt, p.astype(BF16), preferred_element_type=F32)
                    out.append((m_new, l, acc))
                return tuple(out)

            one = (jnp.full((1, tq), NEG_INF, F32), jnp.zeros((1, tq), F32), jnp.zeros((DIL_DIM, tq), F32))
            first_near = jnp.maximum((r0 - reach) // tk, 0)
            c = lax.fori_loop(0, first_near, functools.partial(k_block, kind=kind_far), (one, one))
            res = lax.fori_loop(first_near, (r0 + tq) // tk, functools.partial(k_block, kind=kind_near), c)
            o_t = jnp.concatenate([res[a][2] / res[a][1] for a in range(2)], axis=0)
            o_ref[pl.ds(r0, tq), :] = o_t.T.astype(BF16)
            for a in range(2):
                lse_ref[a, :, pl.ds(r0, tq)] = res[a][0] * LN2 + jnp.log(res[a][1])
            return carry

        lax.fori_loop(0, s // tq, q_block, 0)

        if ng:
            @pl.when(pl.program_id(0) == last_step)
            def _():
                _Gather(*comm).finish()

    return pl.pallas_call(
        body, name=name, grid=(HEADS // 2,),
        in_specs=[pl.BlockSpec((s, qw), lambda h: (0, h)), pl.BlockSpec((s, qw), lambda h: (0, h)),
                  pl.BlockSpec((s, LANE), lambda h: (0, h))] + [ANY] * ng,
        out_specs=[pl.BlockSpec((s, LANE), lambda h: (0, h)), pl.BlockSpec((2, 1, s), lambda h: (h, 0, 0))] + [ANY] * ng,
        out_shape=[jax.ShapeDtypeStruct((s, DIL_W), BF16), jax.ShapeDtypeStruct((HEADS, 1, s), F32)] + _Gather.out_shapes(gather),
        scratch_shapes=[pltpu.VMEM((LANE, s), BF16), pltpu.VMEM((2, 2, tk, tq), F32)]
        + ([] if mla else [pltpu.VMEM((_near_offsets(tk, tq), tk, tq), F32)] * 2) + (_Gather.scratch(gather) if ng else []),
        compiler_params=_params(("arbitrary",) if ng else ("parallel",), 12 << 20),
    )(*_in_hbm(q, k, v), *gather)


def _attn_bwd(q, k, v, o, do, do_block0, lse, mla, scale, name, scatter=()):
    s = q.shape[0]
    qw = 2 * LANE if mla else LANE
    tq, tk = ATT_TQ, ATT_TK_BWD
    nq = s // tq
    reach, kind_near, kind_far = _block_kinds(mla)
    assert s % tq == 0 and s % tk == 0
    ns = len(scatter)
    last_step = HEADS // 2 - 1

    def body(*refs):
        q_ref, k_ref, v_ref, o_ref, do_ref, lse_ref = refs[:6]
        dq_ref, dk_ref, dv_ref = refs[6 + ns:9 + ns]
        kt_ref, dot_ref, dob_ref, dqt_ref, delta_ref, lse2_ref = refs[9 + 2 * ns:15 + 2 * ns]
        near_tabs = None if mla else refs[15 + 2 * ns:17 + 2 * ns]
        n_tabs = 0 if mla else 2
        comm = (refs[6:6 + ns], refs[9 + ns:9 + 2 * ns]) + tuple(refs[15 + n_tabs + 2 * ns:])
        if ns:
            @pl.when(pl.program_id(0) == 0)
            def _():
                _Scatter(*comm).start()

        lane = lax.broadcasted_iota(I32, (1, LANE), 1)
        row = lax.broadcasted_iota(I32, (LANE, 1), 0)
        rel_t = lax.broadcasted_iota(I32, (tk, tq), 1) - lax.broadcasted_iota(I32, (tk, tq), 0)
        if not mla:
            _fill_near_tables(*near_tabs, rel_t)

        def prepare(j, carry):
            c0 = pl.multiple_of(j * tk, tk)
            do_blk = do_ref[pl.ds(c0, tk), :]
            dob_ref[pl.ds(c0, tk), :] = do_blk.astype(BF16)
            do_t = do_blk.T
            dot_ref[:, pl.ds(c0, tk)] = do_t.astype(BF16)
            prod = do_t * o_ref[pl.ds(c0, tk), :].astype(F32).T
            delta_ref[0, :, pl.ds(c0, tk)] = jnp.sum(prod[0:DIL_DIM], axis=0, keepdims=True)
            delta_ref[1, :, pl.ds(c0, tk)] = jnp.sum(prod[DIL_DIM:LANE], axis=0, keepdims=True)
            for w in range(qw // LANE):
                kt_ref[w * LANE:(w + 1) * LANE, pl.ds(c0, tk)] = (
                    k_ref[pl.ds(c0, tk), w * LANE:(w + 1) * LANE].astype(F32).T.astype(BF16))
            return carry

        lax.fori_loop(0, s // tk, prepare, 0)
        dqt_ref[...] = jnp.zeros_like(dqt_ref)
        lse2_ref[...] = lse_ref[...] * LOG2E

        sels = [lane < DIL_DIM, lane >= DIL_DIM]
        rsels = [row < DIL_DIM, row >= DIL_DIM]
        cols = [slice(a * LANE, (a + 1) * LANE) if mla else slice(0, LANE) for a in range(2)]

        def k_block(kj, carry):
            c0 = pl.multiple_of(kj * tk, tk)
            kas = [k_ref[pl.ds(c0, tk), cols[a]] for a in range(2)]
            kts = [kt_ref[cols[a], pl.ds(c0, tk)] for a in range(2)]
            if not mla:
                kas = [jnp.where(sels[a], kas[a], jnp.zeros_like(kas[a])) for a in range(2)]
                kts = [jnp.where(rsels[a], kts[a], jnp.zeros_like(kts[a])) for a in range(2)]
            vb = v_ref[pl.ds(c0, tk), :]
            vbs = [jnp.where(sels[a], vb, jnp.zeros_like(vb)) for a in range(2)]

            first = c0 // tq

            def q_block(qi, c, kind):
                r0 = pl.multiple_of(qi * tq, tq)
                out, dq_parts = [], []
                for a in range(2):
                    dk_acc, dv_acc = c[a]
                    qa = q_ref[pl.ds(r0, tq), cols[a]]
                    st, cnt = _scores_t(kas[a], qa, scale, kind, rel_t, r0 - c0, near_tabs)
                    p = jnp.exp2(st - lse2_ref[a, :, pl.ds(r0, tq)])
                    if cnt is not None:
                        p = p * cnt
                    dp = jnp.dot(vbs[a], dot_ref[:, pl.ds(r0, tq)], preferred_element_type=F32)
                    ds = (p * (dp - delta_ref[a, :, pl.ds(r0, tq)]) * scale).astype(BF16)
                    dv_acc = dv_acc + jnp.dot(p.astype(BF16), dob_ref[pl.ds(r0, tq), :], preferred_element_type=F32)
                    dk_acc = dk_acc + jnp.dot(ds, qa, preferred_element_type=F32)
                    dq_parts.append(jnp.dot(kts[a], ds, preferred_element_type=F32))
                    out.append((dk_acc, dv_acc))
                if mla:
                    for a in range(2):
                        dqt_ref[cols[a], pl.ds(r0, tq)] += dq_parts[a]
                else:
                    dqt_ref[:, pl.ds(r0, tq)] += dq_parts[0] + dq_parts[1]
                return tuple(out)

            zero = jnp.zeros((tk, LANE), F32)
            last_near = jnp.minimum((c0 + tk - 1 + reach) // tq + 1, nq)
            c = lax.fori_loop(first, last_near, functools.partial(q_block, kind=kind_near), ((zero, zero), (zero, zero)))
            (dk0, dv0), (dk1, dv1) = lax.fori_loop(last_near, nq, functools.partial(q_block, kind=kind_far), c)
            if mla:
                dk_ref[pl.ds(c0, tk), cols[0]] = dk0
                dk_ref[pl.ds(c0, tk), cols[1]] = dk1
            else:
                dk_ref[pl.ds(c0, tk), :] = jnp.where(sels[0], dk0, dk1)
            dv_ref[pl.ds(c0, tk), :] = jnp.where(sels[0], dv0, dv1)
            return carry

        lax.fori_loop(0, s // tk, k_block, 0)

        def write_dq(j, carry):
            c0 = pl.multiple_of(j * tk, tk)
            for w in range(qw // LANE):
                dq_ref[pl.ds(c0, tk), w * LANE:(w + 1) * LANE] = dqt_ref[w * LANE:(w + 1) * LANE, pl.ds(c0, tk)].T
            return carry

        lax.fori_loop(0, s // tk, write_dq, 0)

        if ns:
            @pl.when(pl.program_id(0) == last_step)
            def _():
                _Scatter(*comm).finish()

    b0 = do_block0
    return pl.pallas_call(
        body, name=name, grid=(HEADS // 2,),
        in_specs=[pl.BlockSpec((s, qw), lambda h: (0, h)), pl.BlockSpec((s, qw), lambda h: (0, h)),
                  pl.BlockSpec((s, LANE), lambda h: (0, h)), pl.BlockSpec((s, LANE), lambda h: (0, h)),
                  pl.BlockSpec((s, LANE), lambda h: (0, h + b0)), pl.BlockSpec((2, 1, s), lambda h: (h, 0, 0))] + [ANY] * ns,
        out_specs=[pl.BlockSpec((s, qw), lambda h: (0, h)), pl.BlockSpec((s, qw), lambda h: (0, h)),
                   pl.BlockSpec((s, LANE), lambda h: (0, h))] + [ANY] * ns,
        out_shape=[jax.ShapeDtypeStruct(q.shape, F32), jax.ShapeDtypeStruct(k.shape, F32), jax.ShapeDtypeStruct((s, DIL_W), F32)]
        + _Scatter.out_shapes(scatter),
        scratch_shapes=[pltpu.VMEM((qw, s), BF16), pltpu.VMEM((LANE, s), BF16), pltpu.VMEM((s, LANE), BF16),
                        pltpu.VMEM((qw, s), F32), pltpu.VMEM((2, 1, s), F32), pltpu.VMEM((2, 1, s), F32)]
        + ([] if mla else [pltpu.VMEM((_near_offsets(tk, tq), tk, tq), F32)] * 2) + (_Scatter.semaphores(ns) if ns else []),
        compiler_params=_params(("arbitrary",) if ns else ("parallel",), 24 << 20),
    )(*_in_hbm(q, k, v, o, do, lse), *scatter)


def _ada_bwd(c_all, dmod_shard):
    n, d = c_all.shape
    cols = dmod_shard.shape[1]

    def body(c_ref, g_ref, o_ref):
        cv = c_ref[...]
        o_ref[...] = lax.dot_general(cv * _sigmoid(cv), g_ref[...], TN, precision=HIGHEST, preferred_element_type=F32)

    return pl.pallas_call(
        body, name="ada_bwd", out_shape=jax.ShapeDtypeStruct((d, cols), F32),
        compiler_params=_params(None, 16 << 20),
    )(c_all, dmod_shard)


SMALL_WIDTHS = (("g_mix_norm", D_MODEL), ("g_q_lat", Q_LORA), ("g_kv_lat", KV_LORA), ("g_mla_q_nope", NOPE),
                ("g_mla_q_pe", ROPE), ("g_mla_k_nope", NOPE), ("g_mla_k_pe", ROPE), ("g_dil_q", DIL_DIM),
                ("g_dil_k", DIL_DIM), ("g_ffn_norm", D_MODEL), ("b_conv", UP_W))


def _small_layout():
    pieces = (("dmod", 6 * D_MODEL),) + SMALL_WIDTHS + tuple(("w_conv%d" % k, UP_W) for k in range(3)) + (("loss", 1),)
    layout, off = {}, 0
    for name, width in pieces:
        layout[name] = (width, off)
        off += -(-width // LANE) * LANE
    return layout, off


def _pack_small(acc1, acc2, dg2, dglat, dgains, dbg, dbv, dwg, dwv, loss_part):
    layout, total = _small_layout()

    def body(a1, a2, g2, gl, gg, bg, bv, wg, wv, ls, o_ref):
        o_ref[...] = jnp.zeros_like(o_ref)

        def put(name, src, shift=0):
            start = layout[name][1] + shift
            o_ref[:, start:start + src.shape[1]] = src

        for k, src in enumerate((a1[0:1, :], a1[1:2, :], a2[3:4, :], a2[0:1, :], a2[1:2, :], g2[...])):
            put("dmod", src, k * D_MODEL)
        put("g_mix_norm", a1[2:3, :])
        put("g_q_lat", gl[0:1, :])
        put("g_kv_lat", gl[1:2, 0:KV_LORA])
        put("g_mla_q_nope", gg[0:1, 0:NOPE])
        put("g_mla_q_pe", gg[5:6, 0:ROPE])
        put("g_mla_k_nope", gg[1:2, 0:NOPE])
        put("g_mla_k_pe", gg[2:3, 0:ROPE])
        put("g_dil_q", gg[3:4, 0:DIL_DIM])
        put("g_dil_k", gg[4:5, 0:DIL_DIM])
        put("g_ffn_norm", a2[2:3, :])
        put("b_conv", bg[...])
        put("b_conv", bv[...], D_FF)
        for k in range(3):
            put("w_conv%d" % k, wg[k:k + 1, :])
            put("w_conv%d" % k, wv[k:k + 1, :], D_FF)
        put("loss", ls[...])

    ins = (acc1, acc2, dg2, dglat, dgains, dbg, dbv, dwg, dwv, loss_part)
    return pl.pallas_call(
        body, name="pack_small", grid=(1,), in_specs=[_full(a.shape) for a in ins], out_specs=_full((1, total)),
        out_shape=jax.ShapeDtypeStruct((1, total), F32),
        compiler_params=_params(("arbitrary",), 2 << 20),
    )(*_in_hbm(*ins))


def _sum_unpack(g):
    n_dev, _, total = g.shape
    layout, _ = _small_layout()

    def body(g_ref, *refs):
        o_refs, s_ref = refs[:-1], refs[-1]
        acc = g_ref[0]
        for k in range(1, n_dev):
            acc = acc + g_ref[k]
        s_ref[...] = acc
        take = lambda name: s_ref[:, layout[name][1]:layout[name][1] + layout[name][0]]
        o_refs[0][...] = take("dmod")
        for i, (name, _) in enumerate(SMALL_WIDTHS):
            o_refs[1 + i][...] = take(name)
        for k in range(3):
            o_refs[-2][k:k + 1, :] = take("w_conv%d" % k)
        o_refs[-1][...] = take("loss")

    shapes = [(1, 6 * D_MODEL)] + [(1, w) for _, w in SMALL_WIDTHS] + [(3, UP_W), (1, 1)]
    return pl.pallas_call(
        body, name="sum_unpack", out_shape=[jax.ShapeDtypeStruct(sh, F32) for sh in shapes],
        scratch_shapes=[pltpu.VMEM((1, total), F32)],
        compiler_params=_params(None, 4 << 20),
    )(g)


def _adamw_math(w, g, m, v):
    mn = ADAM_B1 * m + (1.0 - ADAM_B1) * g
    vn = ADAM_B2 * v + (1.0 - ADAM_B2) * (g * g)
    m_hat = mn / (1.0 - ADAM_B1 ** ADAM_STEP)
    v_hat = vn / (1.0 - ADAM_B2 ** ADAM_STEP)
    return -ADAM_LR * (m_hat / (jnp.sqrt(v_hat) + ADAM_EPS) + ADAM_WD * w), mn, vn


def _adamw_vectors(ws, gs, ms, vs):
    k = len(ws)

    def body(*refs):
        for i in range(k):
            d, mn, vn = _adamw_math(refs[i][...], refs[k + i][...], refs[2 * k + i][...], refs[3 * k + i][...])
            refs[4 * k + i][...] = d
            refs[5 * k + i][...] = mn
            refs[6 * k + i][...] = vn

    blocks = [_full(w.shape) for w in ws]
    outs = pl.pallas_call(
        body, name="adamw_vectors", grid=(1,), in_specs=blocks * 4, out_specs=blocks * 3,
        out_shape=[jax.ShapeDtypeStruct(w.shape, F32) for w in ws] * 3,
        compiler_params=_params(("arbitrary",), 2 << 20),
    )(*_in_hbm(*ws, *gs, *ms, *vs))
    return outs[:k], outs[k:2 * k], outs[2 * k:]


def _adamw(w, g, m, v, name):
    r, c = w.shape
    tr = r
    for cand in (256, 128, 64, 32, 16):
        if r % cand == 0 and r > cand:
            tr = cand
            break

    def body(w_ref, g_ref, m_ref, v_ref, d_ref, mo_ref, vo_ref):
        d_ref[...], mo_ref[...], vo_ref[...] = _adamw_math(w_ref[...], g_ref[...], m_ref[...], v_ref[...])

    blk = pl.BlockSpec((tr, c), lambda i: (i, 0))
    return pl.pallas_call(
        body, name=name, grid=(r // tr,), in_specs=[blk] * 4, out_specs=[blk] * 3,
        out_shape=[jax.ShapeDtypeStruct((r, c), F32)] * 3,
        compiler_params=_params(("parallel",), 7 * _nbytes((tr, c), F32)),
    )(w, g, m, v)


def _position():
    return lax.axis_index("x"), lax.axis_index("y"), lax.axis_index("c")


def _other_chips(x, y):
    return [(1 - x, y, 2 * (1 - x) + y), (x, 1 - y, 2 * x + (1 - y)), (1 - x, 1 - y, 2 * (1 - x) + (1 - y))]


class _SmallGather:
    def __init__(self, v_ref, out_ref, send_sems, recv_sems, local_sem):
        x, y, c = _position()
        me = 4 * x + 2 * y + c
        self.local = pltpu.make_async_copy(v_ref, out_ref.at[me], local_sem)
        self.sends, self.arrivals = [], []
        for k in range(N_DEV - 1):
            fx, fy, fc = ((k + 1) >> 2) & 1, ((k + 1) >> 1) & 1, (k + 1) & 1
            px, py, pc = (1 - x if fx else x), (1 - y if fy else y), (1 - c if fc else c)

            def copy(dst, k=k, peer=(px, py, pc)):
                return pltpu.make_async_remote_copy(src_ref=v_ref, dst_ref=dst, send_sem=send_sems.at[k],
                                                    recv_sem=recv_sems.at[k], device_id=peer, device_id_type=MESH)

            self.sends.append(copy(out_ref.at[me]))
            self.arrivals.append(copy(out_ref.at[4 * px + 2 * py + pc]))

    @staticmethod
    def semaphores():
        return [pltpu.SemaphoreType.DMA((N_DEV - 1,)), pltpu.SemaphoreType.DMA((N_DEV - 1,)), pltpu.SemaphoreType.DMA]

    def start(self):
        self.local.start()
        for cp in self.sends:
            cp.start()

    def finish(self):
        for cp in self.arrivals:
            cp.wait_recv()
        for cp in self.sends:
            cp.wait_send()
        self.local.wait()


def _prologue(c_taps, w_ada_shard, b_shard, pos_col, rope_consts, shards):
    n = len(shards)
    s = pos_col.shape[0]
    cols = w_ada_shard.shape[1]
    freq, csel, ssel = rope_consts

    def body(*refs):
        ct_ref, w_ref, b_ref, p_ref, f_ref, cs_ref, ss_ref = refs[:7]
        sh_refs = refs[7:7 + n]
        ct_all_ref, mod_all_ref, tab_ref = refs[7 + n:10 + n]
        g_refs = refs[10 + n:10 + 2 * n]
        mod_blk_ref = refs[10 + 2 * n]
        sems = refs[11 + 2 * n:]
        first = _SmallGather(ct_ref, ct_all_ref, *sems[0:3])
        first.start()
        first.finish()
        cv = ct_all_ref[:, 0, 0:D_MODEL]
        sc = (cv * _sigmoid(cv)).astype(BF16)
        mod_blk_ref[...] = jnp.dot(sc, w_ref[...].astype(BF16), preferred_element_type=F32) + b_ref[...]
        second = _SmallGather(mod_blk_ref, mod_all_ref, *sems[3:6])
        second.start()
        weights = _Gather(sh_refs, g_refs, *sems[6:])
        weights.start()

        def table_rows(i, carry):
            r0 = pl.multiple_of(i * ROW_TILE, ROW_TILE)
            ang = p_ref[pl.ds(r0, ROW_TILE), :].astype(F32) * f_ref[...]
            tab_ref[pl.ds(r0, ROW_TILE), :] = cs_ref[...] * jnp.cos(ang) + ss_ref[...] * jnp.sin(ang)
            return carry

        lax.fori_loop(0, s // ROW_TILE, table_rows, 0)
        second.finish()
        weights.forward()
        weights.finish()

    return pl.pallas_call(
        body, name="prologue",
        out_shape=[jax.ShapeDtypeStruct((N_DEV,) + c_taps.shape, F32), jax.ShapeDtypeStruct((N_DEV, N_DEV, cols), F32),
                   jax.ShapeDtypeStruct((s, 4 * LANE), F32)] + _Gather.out_shapes(shards),
        in_specs=[IN_VMEM] * 7 + [ANY] * n, out_specs=[IN_VMEM] * 3 + [ANY] * n,
        scratch_shapes=[pltpu.VMEM((N_DEV, cols), F32)] + _SmallGather.semaphores() * 2 + _Gather.scratch(shards),
        compiler_params=_params(None, 14 << 20),
    )(c_taps, w_ada_shard, b_shard, pos_col, freq, csel, ssel, *shards)


IN_VMEM = pl.BlockSpec(memory_space=pltpu.VMEM)
ANY = pl.BlockSpec(memory_space=pl.ANY)


class _Gather:
    def __init__(self, w_refs, out_refs, send_sems, recv_sems, own_sems, *bounce_refs):
        x, y, c = _position()
        q0 = 2 * x + y
        sibling = (x, y, 1 - c)
        self.ici, self.ici_in, self.fwd, self.fwd_in, self.own_in, self.own_out = [], [], [], [], [], []
        for k, (w_ref, out_ref) in enumerate(zip(w_refs, out_refs)):
            half = w_ref.shape[0] // 2
            self.own_in.append(pltpu.make_async_copy(w_ref, bounce_refs[k], own_sems.at[2 * k]))
            self.own_out.append(pltpu.make_async_copy(bounce_refs[k], out_ref.at[q0], own_sems.at[2 * k + 1]))

            def blk(q, e, out_ref=out_ref, half=half):
                return out_ref.at[q, pl.ds(pl.multiple_of(e * half, 16), half), :]

            def copy(src, dst, i, to):
                return pltpu.make_async_remote_copy(src_ref=src, dst_ref=dst, send_sem=send_sems.at[i], recv_sem=recv_sems.at[i],
                                                    device_id=to, device_id_type=MESH)

            src = w_ref.at[pl.ds(pl.multiple_of(c * half, 16), half), :]
            for j, (cx, cy, qj) in enumerate(_other_chips(x, y)):
                self.ici.append(copy(src, blk(q0, c), 6 * k + j, (cx, cy, c)))
                self.ici_in.append(copy(blk(qj, c), blk(qj, c), 6 * k + j, (cx, cy, c)))
                self.fwd.append(copy(blk(qj, c), blk(qj, c), 6 * k + 3 + j, sibling))
                self.fwd_in.append(copy(blk(qj, 1 - c), blk(qj, 1 - c), 6 * k + 3 + j, sibling))

    @staticmethod
    def out_shapes(shards):
        return [jax.ShapeDtypeStruct((N_CHIP,) + s.shape, s.dtype) for s in shards]

    @staticmethod
    def scratch(shards):
        n = len(shards)
        return ([pltpu.SemaphoreType.DMA((6 * n,)), pltpu.SemaphoreType.DMA((6 * n,)), pltpu.SemaphoreType.DMA((2 * n,))]
                + [pltpu.VMEM(s.shape, s.dtype) for s in shards])

    def start(self):
        for cp in self.ici + self.own_in:
            cp.start()

    def forward(self):
        for fetched, placed in zip(self.own_in, self.own_out):
            fetched.wait()
            placed.start()
        for arrived, onward in zip(self.ici_in, self.fwd):
            arrived.wait_recv()
            onward.start()

    def finish(self):
        for cp in self.fwd_in:
            cp.wait_recv()
        for cp in self.ici + self.fwd:
            cp.wait_send()
        for cp in self.own_out:
            cp.wait()


class _PairSwap:
    def __init__(self, g_refs, out_refs, send_sems, recv_sems):
        x, y, c = _position()
        self.copies = [
            pltpu.make_async_remote_copy(src_ref=g_ref.at[:, 1 - c], dst_ref=out_ref, send_sem=send_sems.at[k],
                                         recv_sem=recv_sems.at[k], device_id=(x, y, 1 - c), device_id_type=MESH)
            for k, (g_ref, out_ref) in enumerate(zip(g_refs, out_refs))]

    @staticmethod
    def out_shapes(grads):
        return [jax.ShapeDtypeStruct((N_CHIP,) + g.shape[2:], g.dtype) for g in grads]

    @staticmethod
    def semaphores(n):
        return [pltpu.SemaphoreType.DMA((n,)), pltpu.SemaphoreType.DMA((n,))]

    def start(self):
        for cp in self.copies:
            cp.start()

    def finish(self):
        for cp in self.copies:
            cp.wait_recv()
        for cp in self.copies:
            cp.wait_send()


def _pair_sum(g, a, c_idx, name):
    _, _, rh, cols = g.shape
    tr = rh
    for cand in (256, 128, 64, 32, 16):
        if rh % cand == 0 and rh > cand:
            tr = cand
            break

    def body(c_ref, g_ref, a_ref, o_ref):
        o_ref[...] = (g_ref[...] + a_ref[...]).astype(BF16)

    return pl.pallas_call(
        body, name=name,
        grid_spec=pltpu.PrefetchScalarGridSpec(
            num_scalar_prefetch=1, grid=(N_CHIP, rh // tr),
            in_specs=[pl.BlockSpec((None, None, tr, cols), lambda q, i, c_ref: (q, c_ref[0], i, 0)),
                      pl.BlockSpec((None, tr, cols), lambda q, i, c_ref: (q, i, 0))],
            out_specs=pl.BlockSpec((None, tr, cols), lambda q, i, c_ref: (q, i, 0))),
        out_shape=jax.ShapeDtypeStruct((N_CHIP, rh, cols), BF16),
        compiler_params=_params(("parallel", "parallel"), 10 * _nbytes((tr, cols), F32)),
    )(c_idx, g, a)


def _scatter_and_gather(parts, small, name):
    n = len(parts)

    def body(*refs):
        scatter = _Scatter(refs[:n], refs[n + 1:2 * n + 1], *refs[2 * n + 2:2 * n + 4])
        gather = _SmallGather(refs[n], refs[2 * n + 1], *refs[2 * n + 4:])
        scatter.start()
        gather.start()
        gather.finish()
        scatter.finish()

    return pl.pallas_call(
        body, name=name,
        out_shape=_Scatter.out_shapes(parts) + [jax.ShapeDtypeStruct((N_DEV,) + small.shape, F32)],
        in_specs=[ANY] * n + [IN_VMEM], out_specs=[ANY] * n + [IN_VMEM],
        scratch_shapes=_Scatter.semaphores(n) + _SmallGather.semaphores(),
        compiler_params=_params(None, 10 * _nbytes(small.shape, F32)),
    )(*parts, small)


class _Scatter:
    def __init__(self, p_refs, out_refs, send_sems, recv_sems):
        x, y, c = _position()
        self.copies = []
        for k, (p_ref, out_ref) in enumerate(zip(p_refs, out_refs)):
            for j, (cx, cy, qj) in enumerate(_other_chips(x, y)):
                self.copies.append(pltpu.make_async_remote_copy(
                    src_ref=p_ref.at[qj], dst_ref=out_ref.at[j], send_sem=send_sems.at[3 * k + j],
                    recv_sem=recv_sems.at[3 * k + j], device_id=(cx, cy, c), device_id_type=MESH))

    @staticmethod
    def out_shapes(parts):
        return [jax.ShapeDtypeStruct((3,) + p.shape[1:], p.dtype) for p in parts]

    @staticmethod
    def semaphores(n):
        return [pltpu.SemaphoreType.DMA((3 * n,)), pltpu.SemaphoreType.DMA((3 * n,))]

    def start(self):
        for cp in self.copies:
            cp.start()

    def finish(self):
        for cp in self.copies:
            cp.wait_recv()
        for cp in self.copies:
            cp.wait_send()


def _shard_sum(p, b, qc_idx, name):
    _, rh, cols = p.shape
    tr = rh
    for cand in (256, 128, 64, 32, 16):
        if rh % cand == 0 and rh > cand:
            tr = cand
            break

    def body(qc_ref, p_ref, b_ref, o_ref):
        acc = p_ref[...].astype(F32)
        for j in range(3):
            acc = acc + b_ref[j].astype(F32)
        o_ref[...] = acc

    return pl.pallas_call(
        body, name=name,
        grid_spec=pltpu.PrefetchScalarGridSpec(
            num_scalar_prefetch=1, grid=(rh // tr,),
            in_specs=[pl.BlockSpec((None, tr, cols), lambda i, qc_ref: (qc_ref[0], i, 0)),
                      pl.BlockSpec((3, tr, cols), lambda i, qc_ref: (0, i, 0))],
            out_specs=pl.BlockSpec((None, tr, cols), lambda i, qc_ref: (qc_ref[1], i, 0))),
        out_shape=jax.ShapeDtypeStruct((2, rh, cols), F32),
        compiler_params=_params(("parallel",), 8 * _nbytes((tr, cols), F32)),
    )(qc_idx, p, b)


def _join_halves(shards):
    n = len(shards)

    def body(*refs):
        out_refs = refs[n:2 * n]
        send_sems, recv_sems = refs[2 * n:]
        x, y, c = _position()
        cps = [pltpu.make_async_remote_copy(src_ref=out_refs[k].at[c], dst_ref=out_refs[k].at[c], send_sem=send_sems.at[k],
                                            recv_sem=recv_sems.at[k], device_id=(x, y, 1 - c), device_id_type=MESH)
               for k in range(n)]
        for cp in cps:
            cp.start()
        for k in range(n):
            arriving = out_refs[k].at[1 - c]
            pltpu.make_async_remote_copy(src_ref=arriving, dst_ref=arriving, send_sem=send_sems.at[k], recv_sem=recv_sems.at[k],
                                         device_id=(x, y, 1 - c), device_id_type=MESH).wait_recv()
        for cp in cps:
            cp.wait_send()

    return pl.pallas_call(
        body, name="rs_join",
        out_shape=[jax.ShapeDtypeStruct(a.shape, a.dtype) for a in shards],
        in_specs=[ANY] * n, out_specs=[ANY] * n, input_output_aliases={k: k for k in range(n)},
        scratch_shapes=[pltpu.SemaphoreType.DMA((n,)), pltpu.SemaphoreType.DMA((n,))],
    )(*shards)


def _cols_from_shards(g):
    q, r, cs = g.shape
    return jnp.transpose(g, (1, 0, 2)).reshape(r, q * cs)


def _cols_to_shards(w):
    r, cfull = w.shape
    return jnp.transpose(w.reshape(r, N_CHIP, cfull // N_CHIP), (1, 0, 2))


def _pad_w_in(w):
    z = lambda n: jnp.zeros((w.shape[0], n), w.dtype)
    q_lat, kv_lat, kpe = w[:, 0:512], w[:, 512:768], w[:, 768:800]
    qd, kd, vd = w[:, 800:1312], w[:, 1312:1824], w[:, 1824:2336]
    return jnp.concatenate([q_lat, qd, kd, vd, kv_lat, z(KPE_OFF), kpe, z(LANE - KPE_OFF - ROPE)], axis=1)


def _pad_w_qb(w):
    w3 = w.reshape(Q_LORA, HEADS, NOPE + ROPE)
    return jnp.pad(w3, ((0, 0), (0, 0), (0, LANE - NOPE - ROPE))).reshape(Q_LORA, HEADS * LANE)


def _unpad_w_qb(g):
    return g.reshape(Q_LORA, HEADS, LANE)[:, :, :NOPE + ROPE].reshape(Q_LORA, HEADS * (NOPE + ROPE))


def _pad_w_kvb(w):
    w3 = w.reshape(KV_LORA, HEADS, 2 * NOPE)
    kp = jnp.pad(w3[:, :, :NOPE], ((0, 0), (0, 0), (0, LANE - NOPE))).reshape(KV_LORA, HEADS * LANE)
    return jnp.concatenate([kp, w3[:, :, NOPE:].reshape(KV_LORA, DIL_W)], axis=1)


def _unpad_w_kvb(g):
    gk = g[:, :HEADS * LANE].reshape(KV_LORA, HEADS, LANE)[:, :, :NOPE]
    gv = g[:, HEADS * LANE:].reshape(KV_LORA, HEADS, NOPE)
    return jnp.concatenate([gk, gv], axis=2).reshape(KV_LORA, HEADS * 2 * NOPE)


def _head_gains(g_q_nope, g_q_pe, g_k_nope, g_k_pe, g_dq, g_dk):
    z = lambda n: jnp.zeros((1, n), F32)
    q1 = jnp.concatenate([g_q_nope, g_q_pe, z(LANE - NOPE - ROPE)], axis=1)
    k1 = jnp.concatenate([g_k_nope, z(LANE - NOPE)], axis=1)
    kpe = jnp.concatenate([z(KPE_OFF), g_k_pe, z(LANE - KPE_OFF - ROPE)], axis=1)
    return dict(q=jnp.tile(q1, (1, HEADS)), k=jnp.tile(k1, (1, HEADS)), kpe=kpe,
                dq=jnp.tile(g_dq, (1, HEADS)), dk=jnp.tile(g_dk, (1, HEADS)))


def kernel(x, c, positions, w_ada, b_ada, g_mix_norm, w_in, g_q_lat, w_q_b, g_kv_lat, w_kv_b, g_mla_q_nope, g_mla_q_pe, g_mla_k_nope, g_mla_k_pe, g_dil_q, g_dil_k, w_o, g_ffn_norm, w_up, w_conv, b_conv, w_down, loss_target, m_w_ada, m_b_ada, m_g_mix_norm, m_w_in, m_g_q_lat, m_w_q_b, m_g_kv_lat, m_w_kv_b, m_g_mla_q_nope, m_g_mla_q_pe, m_g_mla_k_nope, m_g_mla_k_pe, m_g_dil_q, m_g_dil_k, m_w_o, m_g_ffn_norm, m_w_up, m_w_conv, m_b_conv, m_w_down, v_w_ada, v_b_ada, v_g_mix_norm, v_w_in, v_g_q_lat, v_w_q_b, v_g_kv_lat, v_w_kv_b, v_g_mla_q_nope, v_g_mla_q_pe, v_g_mla_k_nope, v_g_mla_k_pe, v_g_dil_q, v_g_dil_k, v_w_o, v_g_ffn_norm, v_w_up, v_w_conv, v_b_conv, v_w_down):
    args = dict(locals())
    weights = {n: args[n][0] for n in ("w_ada", "w_in", "w_q_b", "w_kv_b", "w_o", "w_up", "w_conv", "w_down")}
    small_w = {n: args[n] for n in ("b_ada",) + tuple(n for n, _ in SMALL_WIDTHS)}
    mom_m = {n[2:]: (args[n][0] if args[n].ndim == 3 else args[n]) for n in args if n.startswith("m_")}
    mom_v = {n[2:]: (args[n][0] if args[n].ndim == 3 else args[n]) for n in args if n.startswith("v_")}

    xi, yi, ci = _position()
    q0 = 2 * xi + yi
    me = 4 * xi + 2 * yi + ci
    xs, tgt = x[0], loss_target[0]
    s = xs.shape[0]
    consts = _seg_consts()
    c_idx, qc_idx = jnp.reshape(ci, (1,)).astype(I32), jnp.stack([q0, ci]).astype(I32)

    def halves(g4):
        q, r, cc = g4.shape
        return g4.reshape(q, 2, r // 2, cc)

    own_first = [weights[n].astype(BF16) for n in ("w_in", "w_q_b", "w_kv_b")]
    own_later = [weights[n].astype(BF16) for n in ("w_o", "w_up", "w_down")]
    conv_cols = UP_W // N_CHIP
    ada_cols = w_ada.shape[2]
    b_shard = lax.dynamic_slice_in_dim(b_ada, q0 * ada_cols, ada_cols, axis=1)
    c_taps = jnp.concatenate([c, weights["w_conv"].reshape(1, 3 * conv_cols)], axis=1)
    c_taps_all, mod_all, tab, *gathered = _prologue(c_taps, weights["w_ada"], b_shard, positions.reshape(s, 1),
                                                    _rope_consts(), own_first)
    c_all = c_taps_all[:, 0, :D_MODEL]
    w_conv_f = c_taps_all[:, 0, D_MODEL:].reshape(N_CHIP, 2, 3, conv_cols)[:, 0]
    w_conv_f = jnp.transpose(w_conv_f, (1, 0, 2)).reshape(3, UP_W)
    mod_all = mod_all.reshape(N_CHIP, 2, N_DEV, ada_cols)
    mod = lax.dynamic_index_in_dim(lax.dynamic_index_in_dim(mod_all, ci, 1, False), me, 1, False)
    mod = mod.reshape(1, N_CHIP * ada_cols)
    sh1, sc1, g1, sh2, sc2, g2 = [mod[:, k * D_MODEL:(k + 1) * D_MODEL] for k in range(6)]
    w_in_f = _cols_from_shards(gathered[0])
    w_in_p = _pad_w_in(w_in_f)
    w_qb_p = _pad_w_qb(_cols_from_shards(gathered[1]))
    w_kvb_p = _pad_w_kvb(_cols_from_shards(gathered[2]))
    gains = _head_gains(g_mla_q_nope, g_mla_q_pe, g_mla_k_nope, g_mla_k_pe, g_dil_q, g_dil_k)

    h = _prenorm(xs, g_mix_norm, sc1, sh1, "prenorm")
    proj = _mm(h, w_in_p, "nn", F32, 512, P_COLS, "mm_in")
    ql, kvl = _latnorm(proj, g_q_lat, g_kv_lat)
    q_raw = _mm(ql, w_qb_p, "nn", F32, 1024, HEADS * LANE, "mm_qb")
    kv_raw = _mm(kvl, w_kvb_p, "nn", F32, 1024, HEADS * LANE + DIL_W, "mm_kvb")
    qm, km, vm, qd, kd, vd = _attn_prep(q_raw, kv_raw, proj, tab, gains, consts)
    scale_m, scale_d = (NOPE + ROPE) ** -0.5, DIL_DIM ** -0.5
    o_m, lse_m, got_up = _attn_fwd(qm, km, vm, True, scale_m, "attn_mla", gather=own_later[1:2])
    o_d, lse_d, got_o, got_down = _attn_fwd(qd, kd, vd, False, scale_d, "attn_dil", gather=[own_later[0], own_later[2]])
    gathered = [got_o, got_up, got_down]
    w_o_f = gathered[0].reshape(D_MODEL, D_MODEL)
    w_up_f = _cols_from_shards(gathered[1])
    w_down_f = gathered[2].reshape(D_FF, D_MODEL)
    mix_in = jnp.concatenate([o_m, o_d], axis=1)
    mix = _mm(mix_in, w_o_f, "nn", F32, 1024, D_MODEL, "mm_o")
    x1, h2 = _resid_prenorm(xs, mix, g1, g_ffn_norm, sc2, sh2)
    up = _mm(h2, w_up_f, "nn", F32, 1024, CONV_TILE, "mm_up")
    act = _conv_gate(up, w_conv_f, b_conv)
    dy, dffn, dg2, loss_part = _down_final(act, w_down_f, x1, tgt, g2)

    da = _mm(dffn, w_down_f, "nt", F32, 1024, CONV_TILE, "mm_down_dx")
    gw_down = _mm(act, dffn, "tn", F32, 256, D_MODEL, "mm_down_dw")
    dup_g, dup_v, dbg, dbv, dwg, dwv = _gate_bwd(up, da, w_conv_f, b_conv)
    dup = jnp.concatenate([dup_g, dup_v], axis=1)
    early_names = ("w_up", "w_down", "w_o")
    gw_up = _mm(h2, dup, "tn", F32, 1024, CONV_TILE, "mm_up_dw", col_shards=True)
    early = [halves(gw_up), halves(gw_down.reshape(N_CHIP, D_FF // N_CHIP, D_MODEL))]
    dh2, *early_sib = _mm(dup, w_up_f, "nt", F32, 256, 512, "mm_up_dx", swap=early, b_outer=True)
    dx1, dmix, acc2 = _ffnnorm_bwd(dh2, x1, dy, mix, g_ffn_norm, sc2, g1)
    gw_o = _mm(mix_in, dmix, "tn", F32, 1024, D_MODEL, "mm_o_dw")
    early.append(halves(gw_o.reshape(N_CHIP, D_MODEL // N_CHIP, D_MODEL)))
    dmix_in, sib_o = _mm(dmix, w_o_f, "nt", F32, 512, D_MODEL, "mm_o_dx", swap=early[2:])
    early_sib.append(sib_o)
    early_sums = [_pair_sum(g, a, c_idx, "pair_sum_" + n) for g, a, n in zip(early, early_sib, early_names)]
    dqm, dkm, dvm, *early_recv = _attn_bwd(qm, km, vm, o_m, dmix_in, 0, lse_m, True, scale_m, "attn_mla_bwd",
                                           scatter=early_sums[:1])
    dqd, dkd, dvd, *early_recv_d = _attn_bwd(qd, kd, vd, o_d, dmix_in, DIL_W // LANE, lse_d, False, scale_d,
                                             "attn_dil_bwd", scatter=early_sums[1:])
    early_recv = early_recv + early_recv_d
    dq_raw, dkv_raw, dkpe_b, dqd_b, dkd_b, dvd_b, dgains = _attn_prep_bwd(
        dqm, dkm, dvm, dqd, dkd, dvd, q_raw, kv_raw, proj, tab, gains, consts)
    dql = _mm(dq_raw, w_qb_p, "nt", F32, 1024, Q_LORA, "mm_qb_dx")
    gw_qb = _unpad_w_qb(_mm(ql, dq_raw, "tn", F32, Q_LORA, HEADS * LANE, "mm_qb_dw"))
    dkvl = _mm(dkv_raw, w_kvb_p, "nt", F32, 1024, KV_LORA, "mm_kvb_dx")
    gw_kvb = _unpad_w_kvb(_mm(kvl, dkv_raw, "tn", F32, KV_LORA, HEADS * LANE + DIL_W, "mm_kvb_dw"))
    dqlat_b, dkvlat_b, dglat = _latnorm_bwd(dql, dkvl, proj, g_q_lat, g_kv_lat)
    dproj = jnp.concatenate([dqlat_b, dkvlat_b, dkpe_b[:, KPE_OFF:KPE_OFF + ROPE], dqd_b, dkd_b, dvd_b], axis=1)
    gw_in = _mm(h, dproj, "tn", F32, 512, IN_COLS, "mm_in_dw")
    late_names = ("w_in", "w_q_b", "w_kv_b")
    late = [halves(_cols_to_shards(gw_in)), halves(_cols_to_shards(gw_qb)), halves(_cols_to_shards(gw_kvb))]
    dh, *late_sib = _mm(dproj, w_in_f, "nt", F32, 512, D_MODEL, "mm_in_dx", swap=late)
    grad_x, acc1 = _mixnorm_bwd(dh, xs, dx1, g_mix_norm, sc1)

    packed = _pack_small(acc1, acc2, dg2, dglat, dgains, dbg, dbv, dwg, dwv, loss_part)
    late_sums = [_pair_sum(g, a, c_idx, "pair_sum_" + n) for g, a, n in zip(late, late_sib, late_names)]
    *late_recv, gathered_small = _scatter_and_gather(late_sums, packed, "rs_scatter_late")

    grad_b_ada, *small_grads, gconv_full, loss_sum = _sum_unpack(gathered_small)
    grads = {"b_ada": grad_b_ada}
    grads.update({n: g for (n, _), g in zip(SMALL_WIDTHS, small_grads)})
    shard_cols = UP_W // N_CHIP
    grads["w_conv"] = lax.dynamic_slice_in_dim(gconv_full, q0 * shard_cols, shard_cols, axis=1)
    dmod_all = gathered_small[:, 0, :6 * D_MODEL]
    grads["w_ada"] = _ada_bwd(c_all, lax.dynamic_slice_in_dim(dmod_all, q0 * ada_cols, ada_cols, axis=1))

    big_names = late_names + early_names
    half_sums = [_shard_sum(p, b, qc_idx, "shard_sum_" + n)
                 for p, b, n in zip(late_sums + early_sums, list(late_recv) + list(early_recv), big_names)]
    for n, full in zip(big_names, _join_halves(half_sums)):
        grads[n] = full.reshape(2 * full.shape[1], full.shape[2])

    delta, new_m, new_v = {}, {}, {}
    for n in ("w_ada", "w_in", "w_q_b", "w_kv_b", "w_o", "w_up", "w_conv", "w_down"):
        operands = (weights[n], grads[n], mom_m[n], mom_v[n])
        flipped = n in ("w_in", "w_q_b")
        if flipped:
            operands = [jnp.swapaxes(a, 0, 1) for a in operands]
            grads[n] = jnp.swapaxes(operands[1], 0, 1)
        if n == "w_ada":
            operands = _in_hbm(*operands)
        delta[n], new_m[n], new_v[n] = _adamw(*operands, "adamw_" + n)
        if flipped:
            delta[n], new_m[n], new_v[n] = (jnp.swapaxes(a, 0, 1) for a in (delta[n], new_m[n], new_v[n]))
    vec_names = ("b_ada",) + tuple(n for n, _ in SMALL_WIDTHS)
    sd, sm, sv = _adamw_vectors(*[[d_[n] for n in vec_names] for d_ in (small_w, grads, mom_m, mom_v)])
    for k, n in enumerate(vec_names):
        delta[n], new_m[n], new_v[n] = sd[k], sm[k], sv[k]

    loss = loss_sum[0, 0]
    order = ("w_ada", "b_ada", "g_mix_norm", "w_in", "g_q_lat", "w_q_b", "g_kv_lat", "w_kv_b", "g_mla_q_nope", "g_mla_q_pe",
             "g_mla_k_nope", "g_mla_k_pe", "g_dil_q", "g_dil_k", "w_o", "g_ffn_norm", "w_up", "w_conv", "b_conv", "w_down")
    lead = lambda n, z: z[None] if n.startswith("w_") else z
    outs = [loss, grad_x[None]]
    for d_ in (grads, delta, new_m, new_v):
        outs += [lead(n, d_[n]) for n in order]
    return tuple(outs)
```

```python
import functools

import numpy as np
import jax
import jax.numpy as jnp
from jax import lax
from jax.experimental import pallas as pl
from jax.experimental.pallas import tpu as pltpu

F32 = jnp.float32
BF16 = jnp.bfloat16
I32 = jnp.int32

D_MODEL = 1024
HEADS = 8
NOPE = 64
ROPE = 32
Q_LORA = 512
KV_LORA = 256
DIL_DIM = 64
DIL_W = HEADS * DIL_DIM
D_FF = 2816
UP_W = 2 * D_FF
IN_COLS = Q_LORA + KV_LORA + ROPE + 3 * DIL_W
ROPE_THETA = 10000.0
EPS = 1e-6
NEG_INF = -1e30
N_DEV = 8
N_CHIP = 4

ADAM_LR = 0.001
ADAM_B1 = 0.9
ADAM_B2 = 0.999
ADAM_EPS = 1e-08
ADAM_WD = 0.01
ADAM_STEP = 10

LANE = 128
ROW_TILE = 256
NORM_TILE = 512
ATT_TQ = 512
ATT_TK = 256
ATT_TK_BWD = 512
LOG2E = 1.4426950408889634
LN2 = 0.6931471805599453
VMEM_CAP = 56 * 1024 * 1024
VMEM_FLOOR = 32 * 1024 * 1024

P_QLAT, P_QD, P_KD, P_VD, P_KVLAT, P_KPE = 0, 512, 1024, 1536, 2048, 2304
P_COLS = 2432
KPE_OFF = 64

NN = (((1,), (0,)), ((), ()))
NT = (((1,), (1,)), ((), ()))
TN = (((0,), (0,)), ((), ()))
HIGHEST = lax.Precision.HIGHEST
MESH = pl.DeviceIdType.MESH


def _params(sem=None, est_bytes=0):
    limit = int(min(max(2 * est_bytes + (4 << 20), VMEM_FLOOR), VMEM_CAP))
    if sem is None:
        return pltpu.CompilerParams(vmem_limit_bytes=limit)
    return pltpu.CompilerParams(dimension_semantics=sem, vmem_limit_bytes=limit)


def _nbytes(shape, dtype):
    return int(np.prod(shape)) * jnp.dtype(dtype).itemsize


def _in_hbm(*xs):
    return [pltpu.with_memory_space_constraint(x, pltpu.HBM) for x in xs]


def _mm(a, b, dims, out_dtype, tm, tn, name, col_shards=False, swap=(), b_outer=False):
    def spec(block, index):
        if b_outer:
            return pl.BlockSpec(block, lambda g0, g1: index(g1, g0))
        return pl.BlockSpec(block, index)

    if dims == "nn":
        (m, k), (k2, n) = a.shape, b.shape
        a_spec = spec((tm, k), lambda i, j: (i, 0))
        b_spec = spec((k, tn), lambda i, j: (0, j))
        dn = NN
    elif dims == "nt":
        (m, k), (n, k2) = a.shape, b.shape
        a_spec = spec((tm, k), lambda i, j: (i, 0))
        b_spec = spec((tn, k), lambda i, j: (j, 0))
        dn = NT
    else:
        (k, m), (k2, n) = a.shape, b.shape
        a_spec = spec((k, tm), lambda i, j: (0, i))
        b_spec = spec((k, tn), lambda i, j: (0, j))
        dn = TN
    assert k == k2 and m % tm == 0 and n % tn == 0, (name, a.shape, b.shape, tm, tn)

    nw = len(swap)
    grid = (n // tn, m // tm) if b_outer else (m // tm, n // tn)

    def body(*refs):
        a_ref, b_ref, o_ref = refs[0], refs[1], refs[2 + nw]
        comm = (refs[2:2 + nw], refs[3 + nw:3 + 2 * nw]) + tuple(refs[3 + 2 * nw:])
        if nw:
            @pl.when((pl.program_id(0) == 0) & (pl.program_id(1) == 0))
            def _():
                _PairSwap(*comm).start()

        o_ref[...] = lax.dot_general(a_ref[...], b_ref[...], dn, preferred_element_type=F32).astype(o_ref.dtype)

        if nw:
            @pl.when((pl.program_id(0) == grid[0] - 1) & (pl.program_id(1) == grid[1] - 1))
            def _():
                _PairSwap(*comm).finish()

    est = _nbytes((tm, k), a.dtype) + _nbytes((tn, k), b.dtype) + _nbytes((tm, tn), F32) + _nbytes((tm, tn), out_dtype)
    if col_shards:
        out_spec = spec((None, tm, tn), lambda i, j: (j, i, 0))
        out_shape = jax.ShapeDtypeStruct((n // tn, m, tn), out_dtype)
    else:
        out_spec = spec((tm, tn), lambda i, j: (i, j))
        out_shape = jax.ShapeDtypeStruct((m, n), out_dtype)
    out = pl.pallas_call(
        body, name=name, grid=grid,
        in_specs=[a_spec, b_spec] + [ANY] * nw,
        out_specs=[out_spec] + [ANY] * nw,
        out_shape=[out_shape] + _PairSwap.out_shapes(swap),
        scratch_shapes=_PairSwap.semaphores(nw) if nw else [],
        compiler_params=_params(("arbitrary", "arbitrary") if nw else ("parallel", "parallel"), est),
    )(a, b, *swap)
    return out if nw else out[0]


def _seg_consts():
    seg_q = np.zeros((HEADS * LANE, LANE), np.float32)
    inv_q = np.zeros((1, LANE), np.float32)
    seg_k = np.zeros((HEADS * LANE, LANE), np.float32)
    inv_k = np.zeros((1, LANE), np.float32)
    seg_d = np.zeros((DIL_W, LANE), np.float32)
    inv_d = np.zeros((1, LANE), np.float32)
    for h in range(HEADS):
        seg_q[h * LANE:h * LANE + NOPE, 2 * h] = 1.0
        seg_q[h * LANE + NOPE:h * LANE + NOPE + ROPE, 2 * h + 1] = 1.0
        inv_q[0, 2 * h], inv_q[0, 2 * h + 1] = 1.0 / NOPE, 1.0 / ROPE
        seg_k[h * LANE:h * LANE + NOPE, h] = 1.0
        inv_k[0, h] = 1.0 / NOPE
        seg_d[h * DIL_DIM:(h + 1) * DIL_DIM, h] = 1.0
        inv_d[0, h] = 1.0 / DIL_DIM
    fold_q = np.tile(np.eye(LANE, dtype=np.float32), (HEADS, 1))
    fold_d = np.zeros((DIL_W, LANE), np.float32)
    fold_d[np.arange(DIL_W), np.arange(DIL_W) % DIL_DIM] = 1.0
    j = lambda v: jnp.asarray(v)
    b = lambda v: jnp.asarray(v, dtype=BF16)
    return dict(seg_q=b(seg_q), exp_q=b(seg_q.T.copy()), inv_q=j(inv_q), seg_k=b(seg_k), exp_k=b(seg_k.T.copy()),
                inv_k=j(inv_k), seg_d=b(seg_d), exp_d=b(seg_d.T.copy()), inv_d=j(inv_d), fold_q=j(fold_q), fold_d=j(fold_d))


def _rope_consts():
    inv_d = jnp.power(ROPE_THETA, -2.0 * jnp.arange(DIL_DIM // 2, dtype=F32) / DIL_DIM)
    inv_q = jnp.power(ROPE_THETA, -2.0 * jnp.arange(ROPE // 2, dtype=F32) / ROPE)
    lanes = np.arange(LANE)
    freq_d = inv_d[lanes % (DIL_DIM // 2)]
    in_pe = (lanes >= KPE_OFF) & (lanes < KPE_OFF + ROPE)
    freq_q = jnp.where(jnp.asarray(in_pe), inv_q[(lanes - KPE_OFF) % (ROPE // 2)], 0.0)
    sign_d = np.where(lanes % DIL_DIM < DIL_DIM // 2, -1.0, 1.0).astype(np.float32)
    sign_q = np.where(in_pe, np.where((lanes - KPE_OFF) < ROPE // 2, -1.0, 1.0), 0.0).astype(np.float32)
    zeros, ones = np.zeros(LANE, np.float32), np.ones(LANE, np.float32)
    freq = jnp.concatenate([freq_d, freq_d, freq_q, freq_q])[None, :]
    csel = jnp.asarray(np.concatenate([ones, zeros, ones, zeros]))[None, :]
    ssel = jnp.asarray(np.concatenate([zeros, sign_d, zeros, sign_q]))[None, :]
    return freq, csel, ssel


def _full(shape):
    return pl.BlockSpec(shape, lambda *_: (0,) * len(shape))


def _tile_lanes(x, n):
    return jnp.concatenate([x] * n, axis=1)


def _rms(x):
    return lax.rsqrt(jnp.mean(x * x, axis=-1, keepdims=True) + EPS)


def _prenorm(x, gain, scale, shift, name):
    s, d = x.shape

    def body(x_ref, g_ref, sc_ref, sh_ref, h_ref):
        xv = x_ref[...]
        h = (xv * _rms(xv)) * g_ref[...] * (1.0 + sc_ref[...]) + sh_ref[...]
        h_ref[...] = h.astype(BF16)

    row = pl.BlockSpec((NORM_TILE, d), lambda i: (i, 0))
    return pl.pallas_call(
        body, name=name, grid=(s // NORM_TILE,),
        in_specs=[row, _full((1, d)), _full((1, d)), _full((1, d))],
        out_specs=row, out_shape=jax.ShapeDtypeStruct((s, d), BF16),
        compiler_params=_params(("parallel",)),
    )(x, gain, scale, shift)


def _latnorm(proj, g_q, g_kv):
    s = proj.shape[0]

    def body(q_ref, kv_ref, gq_ref, gkv_ref, ql_ref, kvl_ref):
        q, kv = q_ref[...], kv_ref[...]
        ql_ref[...] = ((q * _rms(q)) * gq_ref[...]).astype(BF16)
        kvl_ref[...] = ((kv * _rms(kv)) * gkv_ref[...]).astype(BF16)

    return pl.pallas_call(
        body, name="latnorm", grid=(s // NORM_TILE,),
        in_specs=[pl.BlockSpec((NORM_TILE, Q_LORA), lambda i: (i, P_QLAT // Q_LORA)),
                  pl.BlockSpec((NORM_TILE, KV_LORA), lambda i: (i, P_KVLAT // KV_LORA)),
                  _full((1, Q_LORA)), _full((1, KV_LORA))],
        out_specs=[pl.BlockSpec((NORM_TILE, Q_LORA), lambda i: (i, 0)), pl.BlockSpec((NORM_TILE, KV_LORA), lambda i: (i, 0))],
        out_shape=[jax.ShapeDtypeStruct((s, Q_LORA), BF16), jax.ShapeDtypeStruct((s, KV_LORA), BF16)],
        compiler_params=_params(("parallel",)),
    )(proj, proj, g_q, g_kv)


def _dot01(v, mat01):
    hi = v.astype(BF16)
    lo = (v - hi.astype(F32)).astype(BF16)
    return jnp.dot(hi, mat01, preferred_element_type=F32) + jnp.dot(lo, mat01, preferred_element_type=F32)


def _seg_rinv(x, seg, exp, inv):
    r = lax.rsqrt(_dot01(x * x, seg) * inv + EPS)
    return _dot01(r, exp)


def _seg_mean(v, seg, exp, inv):
    return _dot01(_dot01(v, seg) * inv, exp)


def _swap_halves(x, half):
    n = x.shape[1]
    lane = lax.broadcasted_iota(I32, (1, n), 1)
    first = (lane & (2 * half - 1)) < half
    return jnp.where(first, pltpu.roll(x, n - half, 1), pltpu.roll(x, half, 1))


def _rope(x, cos, sin_signed, half):
    return x * cos + _swap_halves(x, half) * sin_signed


def _rope_bwd(dy, cos, sin_signed, half):
    return dy * cos + _swap_halves(dy * sin_signed, half)


def _pe_lane_mask(n):
    lane = lax.broadcasted_iota(I32, (1, n), 1) & (LANE - 1)
    return (lane >= KPE_OFF) & (lane < KPE_OFF + ROPE)


def _attn_prep(q_raw, kv_raw, proj, tab, gains, consts):
    s = q_raw.shape[0]
    hw = HEADS * LANE

    def body(q_ref, kv_ref, kpe_ref, qd_ref, kd_ref, vd_ref, tab_ref,
             gq_ref, gk_ref, gkpe_ref, gdq_ref, gdk_ref,
             segq_ref, expq_ref, invq_ref, segk_ref, expk_ref, invk_ref, segd_ref, expd_ref, invd_ref,
             qm_ref, km_ref, vm_ref, qdo_ref, kdo_ref, vdo_ref):
        tab_v = tab_ref[...]
        cos_d, sin_d = _tile_lanes(tab_v[:, 0:LANE], DIL_W // LANE), _tile_lanes(tab_v[:, LANE:2 * LANE], DIL_W // LANE)
        cos_q1, sin_q1 = tab_v[:, 2 * LANE:3 * LANE], tab_v[:, 3 * LANE:4 * LANE]
        cos_q, sin_q = _tile_lanes(cos_q1, HEADS), _tile_lanes(sin_q1, HEADS)

        q = q_ref[...]
        qn = q * _seg_rinv(q, segq_ref[...], expq_ref[...], invq_ref[...]) * gq_ref[...]
        qm_ref[...] = _rope(qn, cos_q, sin_q, ROPE // 2).astype(BF16)

        kv = kv_ref[...]
        kp = kv[:, :hw]
        kn = kp * _seg_rinv(kp, segk_ref[...], expk_ref[...], invk_ref[...]) * gk_ref[...]
        kpe = kpe_ref[...]
        r_pe = lax.rsqrt(jnp.sum(kpe * kpe, axis=-1, keepdims=True) * (1.0 / ROPE) + EPS)
        kpe_r = _rope(kpe * r_pe * gkpe_ref[...], cos_q1, sin_q1, ROPE // 2)
        km_ref[...] = (kn + _tile_lanes(kpe_r, HEADS)).astype(BF16)
        vm_ref[...] = kv[:, hw:].astype(BF16)

        qd = qd_ref[...]
        qdn = qd * _seg_rinv(qd, segd_ref[...], expd_ref[...], invd_ref[...]) * gdq_ref[...]
        qdo_ref[...] = _rope(qdn, cos_d, sin_d, DIL_DIM // 2).astype(BF16)
        kd = kd_ref[...]
        kdn = kd * _seg_rinv(kd, segd_ref[...], expd_ref[...], invd_ref[...]) * gdk_ref[...]
        kdo_ref[...] = _rope(kdn, cos_d, sin_d, DIL_DIM // 2).astype(BF16)
        vdo_ref[...] = vd_ref[...].astype(BF16)

    t = ROW_TILE
    row = lambda w, cb=0: pl.BlockSpec((t, w), lambda i: (i, cb))
    c = consts
    return pl.pallas_call(
        body, name="attn_prep", grid=(s // t,),
        in_specs=[row(hw), row(hw + DIL_W), row(LANE, P_KPE // LANE), row(DIL_W, P_QD // DIL_W), row(DIL_W, P_KD // DIL_W),
                  row(DIL_W, P_VD // DIL_W), row(4 * LANE),
                  _full((1, hw)), _full((1, hw)), _full((1, LANE)), _full((1, DIL_W)), _full((1, DIL_W)),
                  _full((hw, LANE)), _full((LANE, hw)), _full((1, LANE)), _full((hw, LANE)), _full((LANE, hw)), _full((1, LANE)),
                  _full((DIL_W, LANE)), _full((LANE, DIL_W)), _full((1, LANE))],
        out_specs=[row(hw), row(hw), row(DIL_W), row(DIL_W), row(DIL_W), row(DIL_W)],
        out_shape=[jax.ShapeDtypeStruct((s, hw), BF16), jax.ShapeDtypeStruct((s, hw), BF16)]
        + [jax.ShapeDtypeStruct((s, DIL_W), BF16)] * 4,
        compiler_params=_params(("parallel",), 24 << 20),
    )(*_in_hbm(q_raw, kv_raw, proj, proj, proj, proj), tab, gains["q"], gains["k"], gains["kpe"], gains["dq"], gains["dk"],
      c["seg_q"], c["exp_q"], c["inv_q"], c["seg_k"], c["exp_k"], c["inv_k"], c["seg_d"], c["exp_d"], c["inv_d"])


def _attn_prep_bwd(dqm, dkm, dvm, dqd, dkd, dvd, q_raw, kv_raw, proj, tab, gains, consts):
    s = q_raw.shape[0]
    hw = HEADS * LANE
    n_steps = s // ROW_TILE

    def body(dqm_ref, dkm_ref, dvm_ref, dqd_ref, dkd_ref, dvd_ref, q_ref, kv_ref, kpe_ref, qd_ref, kd_ref, tab_ref,
             gq_ref, gk_ref, gkpe_ref, gdq_ref, gdk_ref,
             segq_ref, expq_ref, invq_ref, segk_ref, expk_ref, invk_ref, segd_ref, expd_ref, invd_ref, foldq_ref, foldd_ref,
             dq_ref, dkv_ref, dkpe_ref, dqdo_ref, dkdo_ref, dvdo_ref, dg_ref, acc_ref):
        i = pl.program_id(0)

        @pl.when(i == 0)
        def _():
            acc_ref[...] = jnp.zeros_like(acc_ref)

        tab_v = tab_ref[...]
        cos_d, sin_d = _tile_lanes(tab_v[:, 0:LANE], DIL_W // LANE), _tile_lanes(tab_v[:, LANE:2 * LANE], DIL_W // LANE)
        cos_q1, sin_q1 = tab_v[:, 2 * LANE:3 * LANE], tab_v[:, 3 * LANE:4 * LANE]
        cos_q, sin_q = _tile_lanes(cos_q1, HEADS), _tile_lanes(sin_q1, HEADS)

        def norm_bwd(x, dyg, gain, seg, exp, inv):
            rinv = _seg_rinv(x, seg, exp, inv)
            xn = x * rinv
            dxn = dyg * gain
            dx = rinv * (dxn - xn * _seg_mean(dxn * xn, seg, exp, inv))
            return dx, jnp.sum(dyg * xn, axis=0, keepdims=True)

        dq, gq_l = norm_bwd(q_ref[...], _rope_bwd(dqm_ref[...], cos_q, sin_q, ROPE // 2), gq_ref[...],
                            segq_ref[...], expq_ref[...], invq_ref[...])
        dq_ref[...] = dq.astype(BF16)

        dkm = dkm_ref[...]
        kv = kv_ref[...]
        dkp, gk_l = norm_bwd(kv[:, :hw], dkm, gk_ref[...], segk_ref[...], expk_ref[...], invk_ref[...])
        dkv_ref[:, :hw] = dkp.astype(BF16)
        dkv_ref[:, hw:] = dvm_ref[...].astype(BF16)

        dkpe_r = dkm[:, 0:LANE]
        for h in range(1, HEADS):
            dkpe_r = dkpe_r + dkm[:, h * LANE:(h + 1) * LANE]
        dkpe_r = jnp.where(_pe_lane_mask(LANE), dkpe_r, 0.0)
        dyg = _rope_bwd(dkpe_r, cos_q1, sin_q1, ROPE // 2)
        kpe = kpe_ref[...]
        r_pe = lax.rsqrt(jnp.sum(kpe * kpe, axis=-1, keepdims=True) * (1.0 / ROPE) + EPS)
        xn = kpe * r_pe
        dxn = dyg * gkpe_ref[...]
        dkpe = r_pe * (dxn - xn * (jnp.sum(dxn * xn, axis=-1, keepdims=True) * (1.0 / ROPE)))
        dkpe_ref[...] = dkpe.astype(BF16)
        gkpe_l = jnp.sum(dyg * xn, axis=0, keepdims=True)

        dqd_v, gdq_l = norm_bwd(qd_ref[...], _rope_bwd(dqd_ref[...], cos_d, sin_d, DIL_DIM // 2), gdq_ref[...],
                                segd_ref[...], expd_ref[...], invd_ref[...])
        dqdo_ref[...] = dqd_v.astype(BF16)
        dkd_v, gdk_l = norm_bwd(kd_ref[...], _rope_bwd(dkd_ref[...], cos_d, sin_d, DIL_DIM // 2), gdk_ref[...],
                                segd_ref[...], expd_ref[...], invd_ref[...])
        dkdo_ref[...] = dkd_v.astype(BF16)
        dvdo_ref[...] = dvd_ref[...].astype(BF16)

        acc_ref[0:1, :] += gq_l
        acc_ref[1:2, :] += gk_l
        acc_ref[2:3, 0:LANE] += gkpe_l
        acc_ref[3:4, 0:DIL_W] += gdq_l
        acc_ref[4:5, 0:DIL_W] += gdk_l

        @pl.when(i == n_steps - 1)
        def _():
            acc = acc_ref[...]
            fq = jnp.dot(acc, foldq_ref[...], precision=HIGHEST, preferred_element_type=F32)
            fd = jnp.dot(acc[:, 0:DIL_W], foldd_ref[...], precision=HIGHEST, preferred_element_type=F32)
            rows = lax.broadcasted_iota(I32, (8, LANE), 0)
            base = jnp.where(rows < 2, fq, jnp.where(rows == 2, acc[:, 0:LANE], fd))
            at0 = pltpu.roll(base, LANE - KPE_OFF, 1)
            dg_ref[...] = jnp.where(rows == 5, pltpu.roll(at0, 5, 0), jnp.where(rows == 2, at0, base))

    t = ROW_TILE
    row = lambda w, cb=0: pl.BlockSpec((t, w), lambda i: (i, cb))
    c = consts
    return pl.pallas_call(
        body, name="attn_prep_bwd", grid=(n_steps,),
        in_specs=[row(hw), row(hw), row(DIL_W), row(DIL_W), row(DIL_W), row(DIL_W),
                  row(hw), row(hw + DIL_W), row(LANE, P_KPE // LANE), row(DIL_W, P_QD // DIL_W), row(DIL_W, P_KD // DIL_W),
                  row(4 * LANE),
                  _full((1, hw)), _full((1, hw)), _full((1, LANE)), _full((1, DIL_W)), _full((1, DIL_W)),
                  _full((hw, LANE)), _full((LANE, hw)), _full((1, LANE)), _full((hw, LANE)), _full((LANE, hw)), _full((1, LANE)),
                  _full((DIL_W, LANE)), _full((LANE, DIL_W)), _full((1, LANE)), _full((hw, LANE)), _full((DIL_W, LANE))],
        out_specs=[row(hw), row(hw + DIL_W), row(LANE), row(DIL_W), row(DIL_W), row(DIL_W), _full((8, LANE))],
        out_shape=[jax.ShapeDtypeStruct((s, hw), BF16), jax.ShapeDtypeStruct((s, hw + DIL_W), BF16),
                   jax.ShapeDtypeStruct((s, LANE), BF16)] + [jax.ShapeDtypeStruct((s, DIL_W), BF16)] * 3
        + [jax.ShapeDtypeStruct((8, LANE), F32)],
        scratch_shapes=[pltpu.VMEM((8, hw), F32)],
        compiler_params=_params(("arbitrary",), 28 << 20),
    )(*_in_hbm(dqm, dkm, dvm, dqd, dkd, dvd, q_raw, kv_raw, proj, proj, proj), tab,
      gains["q"], gains["k"], gains["kpe"], gains["dq"], gains["dk"],
      c["seg_q"], c["exp_q"], c["inv_q"], c["seg_k"], c["exp_k"], c["inv_k"], c["seg_d"], c["exp_d"], c["inv_d"],
      c["fold_q"], c["fold_d"])


def _latnorm_bwd(dql, dkvl, proj, g_q, g_kv):
    s = proj.shape[0]
    n_steps = s // NORM_TILE

    def body(dql_ref, dkvl_ref, q_ref, kv_ref, gq_ref, gkv_ref, dq_ref, dkv_ref, dg_ref):
        i = pl.program_id(0)

        @pl.when(i == 0)
        def _():
            dg_ref[...] = jnp.zeros_like(dg_ref)

        def one(x, dyg, gain):
            r = _rms(x)
            xn = x * r
            dxn = dyg * gain
            dx = r * (dxn - xn * jnp.mean(dxn * xn, axis=-1, keepdims=True))
            return dx, jnp.sum(dyg * xn, axis=0, keepdims=True)

        dq, gq_l = one(q_ref[...], dql_ref[...], gq_ref[...])
        dkv, gkv_l = one(kv_ref[...], dkvl_ref[...], gkv_ref[...])
        dq_ref[...] = dq.astype(BF16)
        dkv_ref[...] = dkv.astype(BF16)
        dg_ref[0:1, :] += gq_l
        dg_ref[1:2, 0:KV_LORA] += gkv_l

    t = NORM_TILE
    return pl.pallas_call(
        body, name="latnorm_bwd", grid=(n_steps,),
        in_specs=[pl.BlockSpec((t, Q_LORA), lambda i: (i, 0)), pl.BlockSpec((t, KV_LORA), lambda i: (i, 0)),
                  pl.BlockSpec((t, Q_LORA), lambda i: (i, P_QLAT // Q_LORA)),
                  pl.BlockSpec((t, KV_LORA), lambda i: (i, P_KVLAT // KV_LORA)),
                  _full((1, Q_LORA)), _full((1, KV_LORA))],
        out_specs=[pl.BlockSpec((t, Q_LORA), lambda i: (i, 0)), pl.BlockSpec((t, KV_LORA), lambda i: (i, 0)), _full((8, Q_LORA))],
        out_shape=[jax.ShapeDtypeStruct((s, Q_LORA), BF16), jax.ShapeDtypeStruct((s, KV_LORA), BF16),
                   jax.ShapeDtypeStruct((8, Q_LORA), F32)],
        compiler_params=_params(("arbitrary",)),
    )(dql, dkvl, proj, proj, g_q, g_kv)


def _o_resid_prenorm(mix_in, w_o, x, g1, gain, scale, shift):
    s, d = x.shape
    k = mix_in.shape[1]

    def body(a_ref, w_ref, x_ref, g1_ref, g_ref, sc_ref, sh_ref, mix_ref, x1_ref, h_ref):
        mix = jnp.dot(a_ref[...], w_ref[...], preferred_element_type=F32)
        mix_ref[...] = mix
        x1 = x_ref[...] + g1_ref[...] * mix
        x1_ref[...] = x1
        h_ref[...] = ((x1 * _rms(x1)) * g_ref[...] * (1.0 + sc_ref[...]) + sh_ref[...]).astype(BF16)

    row = pl.BlockSpec((NORM_TILE, d), lambda i: (i, 0))
    vec = _full((1, d))
    est = _nbytes((NORM_TILE, k), BF16) + _nbytes((k, d), BF16) + 4 * _nbytes((NORM_TILE, d), F32)
    return pl.pallas_call(
        body, name="mm_o_resid_prenorm", grid=(s // NORM_TILE,),
        in_specs=[pl.BlockSpec((NORM_TILE, k), lambda i: (i, 0)), _full((k, d)), row, vec, vec, vec, vec],
        out_specs=[row, row, row],
        out_shape=[jax.ShapeDtypeStruct((s, d), F32), jax.ShapeDtypeStruct((s, d), F32),
                   jax.ShapeDtypeStruct((s, d), BF16)],
        compiler_params=_params(("parallel",), est),
    )(mix_in, w_o, x, g1, gain, scale, shift)


CONV_TILE = 1408
HALO = 8


def _shift_down(x, halo, k):
    t = x.shape[0]
    row = lax.broadcasted_iota(I32, (t, 1), 0)
    out = pltpu.roll(x, k, 0)
    for r in range(k):
        out = jnp.where(row == r, halo[HALO - k + r:HALO - k + r + 1, :], out)
    return out


def _shift_up(x, halo, k):
    t = x.shape[0]
    row = lax.broadcasted_iota(I32, (t, 1), 0)
    out = pltpu.roll(x, t - k, 0)
    for r in range(k):
        out = jnp.where(row == t - k + r, halo[r:r + 1, :], out)
    return out


def _conv_fwd(x, halo, w, b):
    p1, p2 = _shift_down(x, halo, 1), _shift_down(x, halo, 2)
    u = b + p2 * w[0:1, :]
    u = u + p1 * w[1:2, :]
    u = u + x * w[2:3, :]
    return u, p1, p2


def _sigmoid(x):
    return 0.5 * jnp.tanh(0.5 * x) + 0.5


def _conv_gate(up, w_conv, b_conv):
    s = up.shape[0]
    t = ROW_TILE
    nj = D_FF // CONV_TILE
    hb = t // HALO

    def body(g_ref, v_ref, gh_ref, vh_ref, wg_ref, wv_ref, bg_ref, bv_ref, a_ref):
        live = (pl.program_id(0) > 0).astype(F32)
        ug, _, _ = _conv_fwd(g_ref[...], gh_ref[...] * live, wg_ref[...], bg_ref[...])
        uv, _, _ = _conv_fwd(v_ref[...], vh_ref[...] * live, wv_ref[...], bv_ref[...])
        a_ref[...] = (ug * _sigmoid(ug) * uv).astype(BF16)

    main = lambda off: pl.BlockSpec((t, CONV_TILE), lambda i, j: (i, j + off))
    halo = lambda off: pl.BlockSpec((HALO, CONV_TILE), lambda i, j: (jnp.maximum(i * hb - 1, 0), j + off))
    wsp = lambda off: pl.BlockSpec((3, CONV_TILE), lambda i, j: (0, j + off))
    bsp = lambda off: pl.BlockSpec((1, CONV_TILE), lambda i, j: (0, j + off))
    return pl.pallas_call(
        body, name="conv_gate", grid=(s // t, nj),
        in_specs=[main(0), main(nj), halo(0), halo(nj), wsp(0), wsp(nj), bsp(0), bsp(nj)],
        out_specs=pl.BlockSpec((t, CONV_TILE), lambda i, j: (i, j)),
        out_shape=jax.ShapeDtypeStruct((s, D_FF), BF16),
        compiler_params=_params(("parallel", "parallel"), 12 << 20),
    )(up, up, up, up, w_conv, w_conv, b_conv, b_conv)


def _gate_bwd(up, da, w_conv, b_conv):
    s = up.shape[0]
    t = ROW_TILE
    nj = D_FF // CONV_TILE
    hb = t // HALO
    n_i = s // t

    def body(g_ref, v_ref, gh_ref, vh_ref, gn_ref, vn_ref, da_ref, dan_ref, wg_ref, wv_ref, bg_ref, bv_ref,
             dupg_ref, dupv_ref, dbg_ref, dbv_ref, dwg_ref, dwv_ref):
        i = pl.program_id(1)

        @pl.when(i == 0)
        def _():
            for r in (dbg_ref, dbv_ref, dwg_ref, dwv_ref):
                r[...] = jnp.zeros_like(r)

        def d_gate(ug, uv, da_v):
            sg = _sigmoid(ug)
            return da_v * uv * (sg * (1.0 + ug * (1.0 - sg))), da_v * (ug * sg)

        live = (i > 0).astype(F32)
        xg, xv = g_ref[...], v_ref[...]
        wg, wv = wg_ref[...], wv_ref[...]
        ug, g1, g2 = _conv_fwd(xg, gh_ref[...] * live, wg, bg_ref[...])
        uv, v1, v2 = _conv_fwd(xv, vh_ref[...] * live, wv, bv_ref[...])
        dug, duv = d_gate(ug, uv, da_ref[...])

        more = (i < n_i - 1).astype(F32)
        ug_n, _, _ = _conv_fwd(gn_ref[...], xg[t - HALO:, :], wg, bg_ref[...])
        uv_n, _, _ = _conv_fwd(vn_ref[...], xv[t - HALO:, :], wv, bv_ref[...])
        dug_n, duv_n = d_gate(ug_n, uv_n, dan_ref[...] * more)

        def conv_t(du, du_n, w):
            return du * w[2:3, :] + _shift_up(du, du_n, 1) * w[1:2, :] + _shift_up(du, du_n, 2) * w[0:1, :]

        dupg_ref[...] = conv_t(dug, dug_n, wg).astype(BF16)
        dupv_ref[...] = conv_t(duv, duv_n, wv).astype(BF16)
        csum = lambda z: jnp.sum(z, axis=0, keepdims=True)
        dbg_ref[...] += csum(dug)
        dbv_ref[...] += csum(duv)
        dwg_ref[0:1, :] += csum(dug * g2)
        dwg_ref[1:2, :] += csum(dug * g1)
        dwg_ref[2:3, :] += csum(dug * xg)
        dwv_ref[0:1, :] += csum(duv * v2)
        dwv_ref[1:2, :] += csum(duv * v1)
        dwv_ref[2:3, :] += csum(duv * xv)

    last_halo = s // HALO - 1
    main = lambda off: pl.BlockSpec((t, CONV_TILE), lambda j, i: (i, j + off))
    halo = lambda off: pl.BlockSpec((HALO, CONV_TILE), lambda j, i: (jnp.maximum(i * hb - 1, 0), j + off))
    nxt = lambda off: pl.BlockSpec((HALO, CONV_TILE), lambda j, i: (jnp.minimum((i + 1) * hb, last_halo), j + off))
    wsp = lambda off: pl.BlockSpec((3, CONV_TILE), lambda j, i: (0, j + off))
    bsp = lambda off: pl.BlockSpec((1, CONV_TILE), lambda j, i: (0, j + off))
    outs = pl.pallas_call(
        body, name="gate_bwd", grid=(nj, n_i),
        in_specs=[main(0), main(nj), halo(0), halo(nj), nxt(0), nxt(nj), main(0), nxt(0),
                  wsp(0), wsp(nj), bsp(0), bsp(nj)],
        out_specs=[main(0), main(0),
                   pl.BlockSpec((1, CONV_TILE), lambda j, i: (0, j)), pl.BlockSpec((1, CONV_TILE), lambda j, i: (0, j)),
                   pl.BlockSpec((3, CONV_TILE), lambda j, i: (0, j)), pl.BlockSpec((3, CONV_TILE), lambda j, i: (0, j))],
        out_shape=[jax.ShapeDtypeStruct((s, D_FF), BF16), jax.ShapeDtypeStruct((s, D_FF), BF16),
                   jax.ShapeDtypeStruct((1, D_FF), F32), jax.ShapeDtypeStruct((1, D_FF), F32),
                   jax.ShapeDtypeStruct((3, D_FF), F32), jax.ShapeDtypeStruct((3, D_FF), F32)],
        compiler_params=_params(("parallel", "arbitrary"), 24 << 20),
    )(up, up, up, up, up, up, da, da, w_conv, w_conv, b_conv, b_conv)
    return outs


def _down_final(act, w_down, x1, tgt, g2):
    s, d = x1.shape
    k = act.shape[1]
    n_steps = s // NORM_TILE

    def body(a_ref, w_ref, x1_ref, t_ref, g2_ref, dy_ref, df_ref, dg2_ref, loss_ref, lacc_ref):
        i = pl.program_id(0)

        @pl.when(i == 0)
        def _():
            dg2_ref[...] = jnp.zeros_like(dg2_ref)
            lacc_ref[...] = jnp.zeros_like(lacc_ref)

        f = jnp.dot(a_ref[...], w_ref[...], preferred_element_type=F32)
        e = x1_ref[...] + g2_ref[...] * f - t_ref[...]
        dy = e * (1.0 / d)
        dy_ref[...] = dy
        df_ref[...] = (dy * g2_ref[...]).astype(BF16)
        dg2_ref[...] += jnp.sum(dy * f, axis=0, keepdims=True)
        lacc_ref[...] += jnp.sum(e * e, axis=0, keepdims=True)

        @pl.when(i == n_steps - 1)
        def _():
            loss_ref[...] = jnp.sum(lacc_ref[...], axis=1, keepdims=True) * (0.5 / d)

    row = pl.BlockSpec((NORM_TILE, d), lambda i: (i, 0))
    est = (_nbytes((NORM_TILE, k), BF16) + _nbytes((k, d), BF16) + 4 * _nbytes((NORM_TILE, d), F32))
    return pl.pallas_call(
        body, name="mm_down_final", grid=(n_steps,),
        in_specs=[pl.BlockSpec((NORM_TILE, k), lambda i: (i, 0)), _full((k, d)), row, row, _full((1, d))],
        out_specs=[row, row, _full((1, d)), _full((1, 1))],
        out_shape=[jax.ShapeDtypeStruct((s, d), F32), jax.ShapeDtypeStruct((s, d), BF16),
                   jax.ShapeDtypeStruct((1, d), F32), jax.ShapeDtypeStruct((1, 1), F32)],
        scratch_shapes=[pltpu.VMEM((1, d), F32)],
        compiler_params=_params(("arbitrary",), est),
    )(act, w_down, x1, tgt, g2)


def _ffnnorm_bwd(dh2, x1, dy, mix, gain, scale, g1):
    s, d = x1.shape
    n_steps = s // NORM_TILE

    def body(dh_ref, x_ref, dy_ref, mix_ref, g_ref, sc_ref, g1_ref, dx_ref, dm_ref, acc_ref):
        i = pl.program_id(0)

        @pl.when(i == 0)
        def _():
            acc_ref[...] = jnp.zeros_like(acc_ref)

        dh, x = dh_ref[...], x_ref[...]
        r = _rms(x)
        xn = x * r
        dn = dh * (1.0 + sc_ref[...])
        dxn = dn * g_ref[...]
        dx = dy_ref[...] + r * (dxn - xn * jnp.mean(dxn * xn, axis=-1, keepdims=True))
        dx_ref[...] = dx
        dm_ref[...] = (dx * g1_ref[...]).astype(BF16)
        csum = lambda z: jnp.sum(z, axis=0, keepdims=True)
        acc_ref[0:1, :] += csum(dh)
        acc_ref[1:2, :] += csum(dh * (xn * g_ref[...]))
        acc_ref[2:3, :] += csum(dn * xn)
        acc_ref[3:4, :] += csum(dx * mix_ref[...])

    row = pl.BlockSpec((NORM_TILE, d), lambda i: (i, 0))
    vec = _full((1, d))
    return pl.pallas_call(
        body, name="ffnnorm_bwd", grid=(n_steps,),
        in_specs=[row, row, row, row, vec, vec, vec],
        out_specs=[row, row, _full((8, d))],
        out_shape=[jax.ShapeDtypeStruct((s, d), F32), jax.ShapeDtypeStruct((s, d), BF16), jax.ShapeDtypeStruct((8, d), F32)],
        compiler_params=_params(("arbitrary",)),
    )(dh2, x1, dy, mix, gain, scale, g1)


def _mixnorm_bwd(dh, x, dx1, gain, scale):
    s, d = x.shape
    n_steps = s // NORM_TILE

    def body(dh_ref, x_ref, dx1_ref, g_ref, sc_ref, gx_ref, acc_ref):
        i = pl.program_id(0)

        @pl.when(i == 0)
        def _():
            acc_ref[...] = jnp.zeros_like(acc_ref)

        dh, x = dh_ref[...], x_ref[...]
        r = _rms(x)
        xn = x * r
        dn = dh * (1.0 + sc_ref[...])
        dxn = dn * g_ref[...]
        gx_ref[...] = dx1_ref[...] + r * (dxn - xn * jnp.mean(dxn * xn, axis=-1, keepdims=True))
        csum = lambda z: jnp.sum(z, axis=0, keepdims=True)
        acc_ref[0:1, :] += csum(dh)
        acc_ref[1:2, :] += csum(dh * (xn * g_ref[...]))
        acc_ref[2:3, :] += csum(dn * xn)

    row = pl.BlockSpec((NORM_TILE, d), lambda i: (i, 0))
    vec = _full((1, d))
    return pl.pallas_call(
        body, name="mixnorm_bwd", grid=(n_steps,),
        in_specs=[row, row, row, vec, vec],
        out_specs=[row, _full((8, d))],
        out_shape=[jax.ShapeDtypeStruct((s, d), F32), jax.ShapeDtypeStruct((8, d), F32)],
        compiler_params=_params(("arbitrary",)),
    )(dh, x, dx1, gain, scale)


def _key_count(d, dilated):
    if not dilated:
        return jnp.where(d >= 0, 1.0, 0.0)
    one = lambda cond: jnp.where(cond, 1.0, 0.0)
    cnt = one(d <= 128) + one(((d & 3) == 0) & (d <= 512)) + one((d & 15) == 0)
    return jnp.where(d >= 0, cnt, 0.0)


def _block_kinds(mla):
    return (0, "diag", "none") if mla else (NEAR_REACH, "near", "far")


NEAR_REACH = 512


def _near_offsets(tk, tq):
    return (NEAR_REACH - (tk - tq)) // tk + 1


def _scores_t(ka, qa, scale, kind, rel_t, offset, near_tabs=None):
    return _mask_scores(lax.dot_general(ka, qa, NT, preferred_element_type=F32), scale, kind, rel_t, offset, near_tabs)


def _fill_near_tables(bias_ref, cnt_ref, rel_t):
    tk, tq = rel_t.shape
    for idx in range(_near_offsets(tk, tq)):
        cnt = _key_count(rel_t + (tk - tq) + idx * tk, True)
        cnt_ref[idx] = cnt
        bias_ref[idx] = jnp.where(cnt > 0.0, 0.0, NEG_INF)


def _mask_scores(products, scale, kind, rel_t, offset, near_tabs=None):
    st = products * (scale * LOG2E)
    cnt = None
    if kind == "diag":
        st = jnp.where(rel_t + offset >= 0, st, NEG_INF)
    elif kind == "far":
        st = jnp.where((rel_t & 15) == 0, st, NEG_INF)
    elif kind == "near":
        bias_ref, cnt_ref = near_tabs
        tk, tq = rel_t.shape
        idx = (offset - (tk - tq)) // tk
        st = st + bias_ref[idx]
        cnt = cnt_ref[idx]
    return st, cnt


def _attn_fwd(q, k, v, mla, scale, name, gather=()):
    s = q.shape[0]
    qw = 2 * LANE if mla else LANE
    tq, tk = ATT_TQ, ATT_TK
    reach, kind_near, kind_far = _block_kinds(mla)
    assert s % tq == 0 and tq % tk == 0 and reach % tk == 0 and reach in (0, NEAR_REACH)
    ng = len(gather)
    last_step = HEADS // 2 - 1

    def body(*refs):
        q_ref, k_ref, v_ref = refs[:3]
        o_ref, lse_ref = refs[3 + ng:5 + ng]
        vt_ref, st_ref = refs[5 + 2 * ng:7 + 2 * ng]
        near_tabs = None if mla else refs[7 + 2 * ng:9 + 2 * ng]
        n_tabs = 0 if mla else 2
        comm = (refs[3:3 + ng], refs[5 + ng:5 + 2 * ng]) + tuple(refs[7 + n_tabs + 2 * ng:])
        if ng:
            @pl.when(pl.program_id(0) == 0)
            def _():
                _Gather(*comm).start()

            @pl.when(pl.program_id(0) == last_step)
            def _():
                _Gather(*comm).forward()

        lane = lax.broadcasted_iota(I32, (1, LANE), 1)
        rel_t = lax.broadcasted_iota(I32, (tk, tq), 1) - lax.broadcasted_iota(I32, (tk, tq), 0)
        if not mla:
            _fill_near_tables(*near_tabs, rel_t)

        def transpose_v(j, carry):
            c0 = pl.multiple_of(j * tk, tk)
            vt_ref[:, pl.ds(c0, tk)] = v_ref[pl.ds(c0, tk), :].astype(F32).T.astype(BF16)
            return carry

        lax.fori_loop(0, s // tk, transpose_v, 0)

        def q_block(qi, carry):
            r0 = pl.multiple_of(qi * tq, tq)
            kcols = [slice(a * LANE, (a + 1) * LANE) if mla else slice(0, LANE) for a in range(2)]
            qas = [q_ref[pl.ds(r0, tq), kcols[a]] for a in range(2)]
            if not mla:
                qas = [jnp.where(lane < DIL_DIM, qas[0], jnp.zeros_like(qas[0])),
                       jnp.where(lane >= DIL_DIM, qas[1], jnp.zeros_like(qas[1]))]

            n_k = (r0 + tq) // tk

            def products(kj):
                c0 = pl.multiple_of(kj * tk, tk)
                return [lax.dot_general(k_ref[pl.ds(c0, tk), kcols[a]], qas[a], NT, preferred_element_type=F32)
                        for a in range(2)]

            for a, pr in enumerate(products(0)):
                st_ref[0, a] = pr

            def k_block(kj, c, kind):
                c0 = pl.multiple_of(kj * tk, tk)
                slot = kj & 1
                ahead = products(jnp.minimum(kj + 1, n_k - 1))
                out = []
                for a in range(2):
                    m, l, acc = c[a]
                    st, cnt = _mask_scores(st_ref[slot, a], scale, kind, rel_t, r0 - c0, near_tabs)
                    st_ref[1 - slot, a] = ahead[a]
                    m_new = jnp.maximum(m, jnp.max(st, axis=0, keepdims=True))
                    alpha = jnp.exp2(m - m_new)
                    p = jnp.exp2(st - m_new)
                    if cnt is not None:
                        p = p * cnt
                    l = alpha * l + jnp.sum(p, axis=0, keepdims=True)
                    vt = vt_ref[a * DIL_DIM:(a + 1) * DIL_DIM, pl.ds(c0, tk)]
                    acc = alpha * acc + jnp.dot(vt, p.astype(BF16), preferred_element_type=F32)
                    out.append((m_new, l, acc))
                return tuple(out)

            one = (jnp.full((1, tq), NEG_INF, F32), jnp.zeros((1, tq), F32), jnp.zeros((DIL_DIM, tq), F32))
            first_near = jnp.maximum((r0 - reach) // tk, 0)
            c = lax.fori_loop(0, first_near, functools.partial(k_block, kind=kind_far), (one, one))
            res = lax.fori_loop(first_near, (r0 + tq) // tk, functools.partial(k_block, kind=kind_near), c)
            o_t = jnp.concatenate([res[a][2] / res[a][1] for a in range(2)], axis=0)
            o_ref[pl.ds(r0, tq), :] = o_t.T.astype(BF16)
            for a in range(2):
                lse_ref[a, :, pl.ds(r0, tq)] = res[a][0] * LN2 + jnp.log(res[a][1])
            return carry

        lax.fori_loop(0, s // tq, q_block, 0)

        if ng:
            @pl.when(pl.program_id(0) == last_step)
            def _():
                _Gather(*comm).finish()

    return pl.pallas_call(
        body, name=name, grid=(HEADS // 2,),
        in_specs=[pl.BlockSpec((s, qw), lambda h: (0, h)), pl.BlockSpec((s, qw), lambda h: (0, h)),
                  pl.BlockSpec((s, LANE), lambda h: (0, h))] + [ANY] * ng,
        out_specs=[pl.BlockSpec((s, LANE), lambda h: (0, h)), pl.BlockSpec((2, 1, s), lambda h: (h, 0, 0))] + [ANY] * ng,
        out_shape=[jax.ShapeDtypeStruct((s, DIL_W), BF16), jax.ShapeDtypeStruct((HEADS, 1, s), F32)] + _Gather.out_shapes(gather),
        scratch_shapes=[pltpu.VMEM((LANE, s), BF16), pltpu.VMEM((2, 2, tk, tq), F32)]
        + ([] if mla else [pltpu.VMEM((_near_offsets(tk, tq), tk, tq), F32)] * 2) + (_Gather.scratch(gather) if ng else []),
        compiler_params=_params(("arbitrary",) if ng else ("parallel",), 12 << 20),
    )(*_in_hbm(q, k, v), *gather)


def _attn_bwd(q, k, v, o, do, do_block0, lse, mla, scale, name, scatter=()):
    s = q.shape[0]
    qw = 2 * LANE if mla else LANE
    tq, tk = ATT_TQ, ATT_TK_BWD
    nq = s // tq
    reach, kind_near, kind_far = _block_kinds(mla)
    assert s % tq == 0 and s % tk == 0
    ns = len(scatter)
    last_step = HEADS // 2 - 1

    def body(*refs):
        q_ref, k_ref, v_ref, o_ref, do_ref, lse_ref = refs[:6]
        dq_ref, dk_ref, dv_ref = refs[6 + ns:9 + ns]
        kt_ref, dot_ref, dob_ref, dqt_ref, delta_ref, lse2_ref = refs[9 + 2 * ns:15 + 2 * ns]
        near_tabs = None if mla else refs[15 + 2 * ns:17 + 2 * ns]
        n_tabs = 0 if mla else 2
        comm = (refs[6:6 + ns], refs[9 + ns:9 + 2 * ns]) + tuple(refs[15 + n_tabs + 2 * ns:])
        if ns:
            @pl.when(pl.program_id(0) == 0)
            def _():
                _Scatter(*comm).start()

        lane = lax.broadcasted_iota(I32, (1, LANE), 1)
        row = lax.broadcasted_iota(I32, (LANE, 1), 0)
        rel_t = lax.broadcasted_iota(I32, (tk, tq), 1) - lax.broadcasted_iota(I32, (tk, tq), 0)
        if not mla:
            _fill_near_tables(*near_tabs, rel_t)

        def prepare(j, carry):
            c0 = pl.multiple_of(j * tk, tk)
            do_blk = do_ref[pl.ds(c0, tk), :]
            dob_ref[pl.ds(c0, tk), :] = do_blk.astype(BF16)
            do_t = do_blk.T
            dot_ref[:, pl.ds(c0, tk)] = do_t.astype(BF16)
            prod = do_t * o_ref[pl.ds(c0, tk), :].astype(F32).T
            delta_ref[0, :, pl.ds(c0, tk)] = jnp.sum(prod[0:DIL_DIM], axis=0, keepdims=True)
            delta_ref[1, :, pl.ds(c0, tk)] = jnp.sum(prod[DIL_DIM:LANE], axis=0, keepdims=True)
            for w in range(qw // LANE):
                kt_ref[w * LANE:(w + 1) * LANE, pl.ds(c0, tk)] = (
                    k_ref[pl.ds(c0, tk), w * LANE:(w + 1) * LANE].astype(F32).T.astype(BF16))
            return carry

        lax.fori_loop(0, s // tk, prepare, 0)
        dqt_ref[...] = jnp.zeros_like(dqt_ref)
        lse2_ref[...] = lse_ref[...] * LOG2E

        sels = [lane < DIL_DIM, lane >= DIL_DIM]
        rsels = [row < DIL_DIM, row >= DIL_DIM]
        cols = [slice(a * LANE, (a + 1) * LANE) if mla else slice(0, LANE) for a in range(2)]

        def k_block(kj, carry):
            c0 = pl.multiple_of(kj * tk, tk)
            kas = [k_ref[pl.ds(c0, tk), cols[a]] for a in range(2)]
            kts = [kt_ref[cols[a], pl.ds(c0, tk)] for a in range(2)]
            if not mla:
                kas = [jnp.where(sels[a], kas[a], jnp.zeros_like(kas[a])) for a in range(2)]
                kts = [jnp.where(rsels[a], kts[a], jnp.zeros_like(kts[a])) for a in range(2)]
            vb = v_ref[pl.ds(c0, tk), :]
            vbs = [jnp.where(sels[a], vb, jnp.zeros_like(vb)) for a in range(2)]

            first = c0 // tq

            def q_block(qi, c, kind):
                r0 = pl.multiple_of(qi * tq, tq)
                out, dq_parts = [], []
                for a in range(2):
                    dk_acc, dv_acc = c[a]
                    qa = q_ref[pl.ds(r0, tq), cols[a]]
                    st, cnt = _scores_t(kas[a], qa, scale, kind, rel_t, r0 - c0, near_tabs)
                    p = jnp.exp2(st - lse2_ref[a, :, pl.ds(r0, tq)])
                    if cnt is not None:
                        p = p * cnt
                    dp = jnp.dot(vbs[a], dot_ref[:, pl.ds(r0, tq)], preferred_element_type=F32)
                    ds = (p * (dp - delta_ref[a, :, pl.ds(r0, tq)]) * scale).astype(BF16)
                    dv_acc = dv_acc + jnp.dot(p.astype(BF16), dob_ref[pl.ds(r0, tq), :], preferred_element_type=F32)
                    dk_acc = dk_acc + jnp.dot(ds, qa, preferred_element_type=F32)
                    dq_parts.append(jnp.dot(kts[a], ds, preferred_element_type=F32))
                    out.append((dk_acc, dv_acc))
                if mla:
                    for a in range(2):
                        dqt_ref[cols[a], pl.ds(r0, tq)] += dq_parts[a]
                else:
                    dqt_ref[:, pl.ds(r0, tq)] += dq_parts[0] + dq_parts[1]
                return tuple(out)

            zero = jnp.zeros((tk, LANE), F32)
            last_near = jnp.minimum((c0 + tk - 1 + reach) // tq + 1, nq)
            c = lax.fori_loop(first, last_near, functools.partial(q_block, kind=kind_near), ((zero, zero), (zero, zero)))
            (dk0, dv0), (dk1, dv1) = lax.fori_loop(last_near, nq, functools.partial(q_block, kind=kind_far), c)
            if mla:
                dk_ref[pl.ds(c0, tk), cols[0]] = dk0
                dk_ref[pl.ds(c0, tk), cols[1]] = dk1
            else:
                dk_ref[pl.ds(c0, tk), :] = jnp.where(sels[0], dk0, dk1)
            dv_ref[pl.ds(c0, tk), :] = jnp.where(sels[0], dv0, dv1)
            return carry

        lax.fori_loop(0, s // tk, k_block, 0)

        def write_dq(j, carry):
            c0 = pl.multiple_of(j * tk, tk)
            for w in range(qw // LANE):
                dq_ref[pl.ds(c0, tk), w * LANE:(w + 1) * LANE] = dqt_ref[w * LANE:(w + 1) * LANE, pl.ds(c0, tk)].T
            return carry

        lax.fori_loop(0, s // tk, write_dq, 0)

        if ns:
            @pl.when(pl.program_id(0) == last_step)
            def _():
                _Scatter(*comm).finish()

    b0 = do_block0
    return pl.pallas_call(
        body, name=name, grid=(HEADS // 2,),
        in_specs=[pl.BlockSpec((s, qw), lambda h: (0, h)), pl.BlockSpec((s, qw), lambda h: (0, h)),
                  pl.BlockSpec((s, LANE), lambda h: (0, h)), pl.BlockSpec((s, LANE), lambda h: (0, h)),
                  pl.BlockSpec((s, LANE), lambda h: (0, h + b0)), pl.BlockSpec((2, 1, s), lambda h: (h, 0, 0))] + [ANY] * ns,
        out_specs=[pl.BlockSpec((s, qw), lambda h: (0, h)), pl.BlockSpec((s, qw), lambda h: (0, h)),
                   pl.BlockSpec((s, LANE), lambda h: (0, h))] + [ANY] * ns,
        out_shape=[jax.ShapeDtypeStruct(q.shape, F32), jax.ShapeDtypeStruct(k.shape, F32), jax.ShapeDtypeStruct((s, DIL_W), F32)]
        + _Scatter.out_shapes(scatter),
        scratch_shapes=[pltpu.VMEM((qw, s), BF16), pltpu.VMEM((LANE, s), BF16), pltpu.VMEM((s, LANE), BF16),
                        pltpu.VMEM((qw, s), F32), pltpu.VMEM((2, 1, s), F32), pltpu.VMEM((2, 1, s), F32)]
        + ([] if mla else [pltpu.VMEM((_near_offsets(tk, tq), tk, tq), F32)] * 2) + (_Scatter.semaphores(ns) if ns else []),
        compiler_params=_params(("arbitrary",) if ns else ("parallel",), 24 << 20),
    )(*_in_hbm(q, k, v, o, do, lse), *scatter)


def _ada_bwd(c_all, dmod_shard):
    n, d = c_all.shape
    cols = dmod_shard.shape[1]

    def body(c_ref, g_ref, o_ref):
        cv = c_ref[...]
        o_ref[...] = lax.dot_general(cv * _sigmoid(cv), g_ref[...], TN, precision=HIGHEST, preferred_element_type=F32)

    return pl.pallas_call(
        body, name="ada_bwd", out_shape=jax.ShapeDtypeStruct((d, cols), F32),
        compiler_params=_params(None, 16 << 20),
    )(c_all, dmod_shard)


SMALL_WIDTHS = (("g_mix_norm", D_MODEL), ("g_q_lat", Q_LORA), ("g_kv_lat", KV_LORA), ("g_mla_q_nope", NOPE),
                ("g_mla_q_pe", ROPE), ("g_mla_k_nope", NOPE), ("g_mla_k_pe", ROPE), ("g_dil_q", DIL_DIM),
                ("g_dil_k", DIL_DIM), ("g_ffn_norm", D_MODEL), ("b_conv", UP_W))


def _small_layout():
    pieces = (("dmod", 6 * D_MODEL),) + SMALL_WIDTHS + tuple(("w_conv%d" % k, UP_W) for k in range(3)) + (("loss", 1),)
    layout, off = {}, 0
    for name, width in pieces:
        layout[name] = (width, off)
        off += -(-width // LANE) * LANE
    return layout, off


def _pack_small(acc1, acc2, dg2, dglat, dgains, dbg, dbv, dwg, dwv, loss_part):
    layout, total = _small_layout()

    def body(a1, a2, g2, gl, gg, bg, bv, wg, wv, ls, o_ref):
        o_ref[...] = jnp.zeros_like(o_ref)

        def put(name, src, shift=0):
            start = layout[name][1] + shift
            o_ref[:, start:start + src.shape[1]] = src

        for k, src in enumerate((a1[0:1, :], a1[1:2, :], a2[3:4, :], a2[0:1, :], a2[1:2, :], g2[...])):
            put("dmod", src, k * D_MODEL)
        put("g_mix_norm", a1[2:3, :])
        put("g_q_lat", gl[0:1, :])
        put("g_kv_lat", gl[1:2, 0:KV_LORA])
        put("g_mla_q_nope", gg[0:1, 0:NOPE])
        put("g_mla_q_pe", gg[5:6, 0:ROPE])
        put("g_mla_k_nope", gg[1:2, 0:NOPE])
        put("g_mla_k_pe", gg[2:3, 0:ROPE])
        put("g_dil_q", gg[3:4, 0:DIL_DIM])
        put("g_dil_k", gg[4:5, 0:DIL_DIM])
        put("g_ffn_norm", a2[2:3, :])
        put("b_conv", bg[...])
        put("b_conv", bv[...], D_FF)
        for k in range(3):
            put("w_conv%d" % k, wg[k:k + 1, :])
            put("w_conv%d" % k, wv[k:k + 1, :], D_FF)
        put("loss", ls[...])

    ins = (acc1, acc2, dg2, dglat, dgains, dbg, dbv, dwg, dwv, loss_part)
    return pl.pallas_call(
        body, name="pack_small", grid=(1,), in_specs=[_full(a.shape) for a in ins], out_specs=_full((1, total)),
        out_shape=jax.ShapeDtypeStruct((1, total), F32),
        compiler_params=_params(("arbitrary",), 2 << 20),
    )(*_in_hbm(*ins))


def _sum_unpack(g):
    n_dev, _, total = g.shape
    layout, _ = _small_layout()

    def body(g_ref, *refs):
        o_refs, s_ref = refs[:-1], refs[-1]
        acc = g_ref[0]
        for k in range(1, n_dev):
            acc = acc + g_ref[k]
        s_ref[...] = acc
        take = lambda name: s_ref[:, layout[name][1]:layout[name][1] + layout[name][0]]
        o_refs[0][...] = take("dmod")
        for i, (name, _) in enumerate(SMALL_WIDTHS):
            o_refs[1 + i][...] = take(name)
        for k in range(3):
            o_refs[-2][k:k + 1, :] = take("w_conv%d" % k)
        o_refs[-1][...] = take("loss")

    shapes = [(1, 6 * D_MODEL)] + [(1, w) for _, w in SMALL_WIDTHS] + [(3, UP_W), (1, 1)]
    return pl.pallas_call(
        body, name="sum_unpack", out_shape=[jax.ShapeDtypeStruct(sh, F32) for sh in shapes],
        scratch_shapes=[pltpu.VMEM((1, total), F32)],
        compiler_params=_params(None, 4 << 20),
    )(g)


def _adamw_math(w, g, m, v):
    mn = ADAM_B1 * m + (1.0 - ADAM_B1) * g
    vn = ADAM_B2 * v + (1.0 - ADAM_B2) * (g * g)
    m_hat = mn / (1.0 - ADAM_B1 ** ADAM_STEP)
    v_hat = vn / (1.0 - ADAM_B2 ** ADAM_STEP)
    return -ADAM_LR * (m_hat / (jnp.sqrt(v_hat) + ADAM_EPS) + ADAM_WD * w), mn, vn


def _adamw_vectors(ws, gs, ms, vs):
    k = len(ws)

    def body(*refs):
        for i in range(k):
            d, mn, vn = _adamw_math(refs[i][...], refs[k + i][...], refs[2 * k + i][...], refs[3 * k + i][...])
            refs[4 * k + i][...] = d
            refs[5 * k + i][...] = mn
            refs[6 * k + i][...] = vn

    blocks = [_full(w.shape) for w in ws]
    outs = pl.pallas_call(
        body, name="adamw_vectors", grid=(1,), in_specs=blocks * 4, out_specs=blocks * 3,
        out_shape=[jax.ShapeDtypeStruct(w.shape, F32) for w in ws] * 3,
        compiler_params=_params(("arbitrary",), 2 << 20),
    )(*_in_hbm(*ws, *gs, *ms, *vs))
    return outs[:k], outs[k:2 * k], outs[2 * k:]


def _adamw(w, g, m, v, name):
    r, c = w.shape
    tr = r
    for cand in (256, 128, 64, 32, 16):
        if r % cand == 0 and r > cand:
            tr = cand
            break

    def body(w_ref, g_ref, m_ref, v_ref, d_ref, mo_ref, vo_ref):
        d_ref[...], mo_ref[...], vo_ref[...] = _adamw_math(w_ref[...], g_ref[...], m_ref[...], v_ref[...])

    blk = pl.BlockSpec((tr, c), lambda i: (i, 0))
    return pl.pallas_call(
        body, name=name, grid=(r // tr,), in_specs=[blk] * 4, out_specs=[blk] * 3,
        out_shape=[jax.ShapeDtypeStruct((r, c), F32)] * 3,
        compiler_params=_params(("parallel",), 7 * _nbytes((tr, c), F32)),
    )(w, g, m, v)


def _position():
    return lax.axis_index("x"), lax.axis_index("y"), lax.axis_index("c")


def _other_chips(x, y):
    return [(1 - x, y, 2 * (1 - x) + y), (x, 1 - y, 2 * x + (1 - y)), (1 - x, 1 - y, 2 * (1 - x) + (1 - y))]


class _SmallGather:
    def __init__(self, v_ref, out_ref, send_sems, recv_sems, local_sem):
        x, y, c = _position()
        me = 4 * x + 2 * y + c
        self.local = pltpu.make_async_copy(v_ref, out_ref.at[me], local_sem)
        self.sends, self.arrivals = [], []
        for k in range(N_DEV - 1):
            fx, fy, fc = ((k + 1) >> 2) & 1, ((k + 1) >> 1) & 1, (k + 1) & 1
            px, py, pc = (1 - x if fx else x), (1 - y if fy else y), (1 - c if fc else c)

            def copy(dst, k=k, peer=(px, py, pc)):
                return pltpu.make_async_remote_copy(src_ref=v_ref, dst_ref=dst, send_sem=send_sems.at[k],
                                                    recv_sem=recv_sems.at[k], device_id=peer, device_id_type=MESH)

            self.sends.append(copy(out_ref.at[me]))
            self.arrivals.append(copy(out_ref.at[4 * px + 2 * py + pc]))

    @staticmethod
    def semaphores():
        return [pltpu.SemaphoreType.DMA((N_DEV - 1,)), pltpu.SemaphoreType.DMA((N_DEV - 1,)), pltpu.SemaphoreType.DMA]

    def start(self):
        self.local.start()
        for cp in self.sends:
            cp.start()

    def finish(self):
        for cp in self.arrivals:
            cp.wait_recv()
        for cp in self.sends:
            cp.wait_send()
        self.local.wait()


def _prologue(c_taps, w_ada_shard, b_shard, pos_col, rope_consts, shards):
    n = len(shards)
    s = pos_col.shape[0]
    cols = w_ada_shard.shape[1]
    freq, csel, ssel = rope_consts

    def body(*refs):
        ct_ref, w_ref, b_ref, p_ref, f_ref, cs_ref, ss_ref = refs[:7]
        sh_refs = refs[7:7 + n]
        ct_all_ref, mod_all_ref, tab_ref = refs[7 + n:10 + n]
        g_refs = refs[10 + n:10 + 2 * n]
        mod_blk_ref = refs[10 + 2 * n]
        sems = refs[11 + 2 * n:]
        first = _SmallGather(ct_ref, ct_all_ref, *sems[0:3])
        first.start()
        first.finish()
        cv = ct_all_ref[:, 0, 0:D_MODEL]
        sc = (cv * _sigmoid(cv)).astype(BF16)
        mod_blk_ref[...] = jnp.dot(sc, w_ref[...].astype(BF16), preferred_element_type=F32) + b_ref[...]
        second = _SmallGather(mod_blk_ref, mod_all_ref, *sems[3:6])
        second.start()
        weights = _Gather(sh_refs, g_refs, *sems[6:])
        weights.start()

        def table_rows(i, carry):
            r0 = pl.multiple_of(i * ROW_TILE, ROW_TILE)
            ang = p_ref[pl.ds(r0, ROW_TILE), :].astype(F32) * f_ref[...]
            tab_ref[pl.ds(r0, ROW_TILE), :] = cs_ref[...] * jnp.cos(ang) + ss_ref[...] * jnp.sin(ang)
            return carry

        lax.fori_loop(0, s // ROW_TILE, table_rows, 0)
        second.finish()
        weights.forward()
        weights.finish()

    return pl.pallas_call(
        body, name="prologue",
        out_shape=[jax.ShapeDtypeStruct((N_DEV,) + c_taps.shape, F32), jax.ShapeDtypeStruct((N_DEV, N_DEV, cols), F32),
                   jax.ShapeDtypeStruct((s, 4 * LANE), F32)] + _Gather.out_shapes(shards),
        in_specs=[IN_VMEM] * 7 + [ANY] * n, out_specs=[IN_VMEM] * 3 + [ANY] * n,
        scratch_shapes=[pltpu.VMEM((N_DEV, cols), F32)] + _SmallGather.semaphores() * 2 + _Gather.scratch(shards),
        compiler_params=_params(None, 14 << 20),
    )(c_taps, w_ada_shard, b_shard, pos_col, freq, csel, ssel, *shards)


IN_VMEM = pl.BlockSpec(memory_space=pltpu.VMEM)
ANY = pl.BlockSpec(memory_space=pl.ANY)


class _Gather:
    def __init__(self, w_refs, out_refs, send_sems, recv_sems, own_sems, *bounce_refs):
        x, y, c = _position()
        q0 = 2 * x + y
        sibling = (x, y, 1 - c)
        self.ici, self.ici_in, self.fwd, self.fwd_in, self.own_in, self.own_out = [], [], [], [], [], []
        for k, (w_ref, out_ref) in enumerate(zip(w_refs, out_refs)):
            half = w_ref.shape[0] // 2
            self.own_in.append(pltpu.make_async_copy(w_ref, bounce_refs[k], own_sems.at[2 * k]))
            self.own_out.append(pltpu.make_async_copy(bounce_refs[k], out_ref.at[q0], own_sems.at[2 * k + 1]))

            def blk(q, e, out_ref=out_ref, half=half):
                return out_ref.at[q, pl.ds(pl.multiple_of(e * half, 16), half), :]

            def copy(src, dst, i, to):
                return pltpu.make_async_remote_copy(src_ref=src, dst_ref=dst, send_sem=send_sems.at[i], recv_sem=recv_sems.at[i],
                                                    device_id=to, device_id_type=MESH)

            src = w_ref.at[pl.ds(pl.multiple_of(c * half, 16), half), :]
            for j, (cx, cy, qj) in enumerate(_other_chips(x, y)):
                self.ici.append(copy(src, blk(q0, c), 6 * k + j, (cx, cy, c)))
                self.ici_in.append(copy(blk(qj, c), blk(qj, c), 6 * k + j, (cx, cy, c)))
                self.fwd.append(copy(blk(qj, c), blk(qj, c), 6 * k + 3 + j, sibling))
                self.fwd_in.append(copy(blk(qj, 1 - c), blk(qj, 1 - c), 6 * k + 3 + j, sibling))

    @staticmethod
    def out_shapes(shards):
        return [jax.ShapeDtypeStruct((N_CHIP,) + s.shape, s.dtype) for s in shards]

    @staticmethod
    def scratch(shards):
        n = len(shards)
        return ([pltpu.SemaphoreType.DMA((6 * n,)), pltpu.SemaphoreType.DMA((6 * n,)), pltpu.SemaphoreType.DMA((2 * n,))]
                + [pltpu.VMEM(s.shape, s.dtype) for s in shards])

    def start(self):
        for cp in self.ici + self.own_in:
            cp.start()

    def forward(self):
        for fetched, placed in zip(self.own_in, self.own_out):
            fetched.wait()
            placed.start()
        for arrived, onward in zip(self.ici_in, self.fwd):
            arrived.wait_recv()
            onward.start()

    def finish(self):
        for cp in self.fwd_in:
            cp.wait_recv()
        for cp in self.ici + self.fwd:
            cp.wait_send()
        for cp in self.own_out:
            cp.wait()


class _PairSwap:
    def __init__(self, g_refs, out_refs, send_sems, recv_sems):
        x, y, c = _position()
        self.copies = [
            pltpu.make_async_remote_copy(src_ref=g_ref.at[:, 1 - c], dst_ref=out_ref, send_sem=send_sems.at[k],
                                         recv_sem=recv_sems.at[k], device_id=(x, y, 1 - c), device_id_type=MESH)
            for k, (g_ref, out_ref) in enumerate(zip(g_refs, out_refs))]

    @staticmethod
    def out_shapes(grads):
        return [jax.ShapeDtypeStruct((N_CHIP,) + g.shape[2:], g.dtype) for g in grads]

    @staticmethod
    def semaphores(n):
        return [pltpu.SemaphoreType.DMA((n,)), pltpu.SemaphoreType.DMA((n,))]

    def start(self):
        for cp in self.copies:
            cp.start()

    def finish(self):
        for cp in self.copies:
            cp.wait_recv()
        for cp in self.copies:
            cp.wait_send()


def _pair_sum(g, a, c_idx, name):
    _, _, rh, cols = g.shape
    tr = rh
    for cand in (256, 128, 64, 32, 16):
        if rh % cand == 0 and rh > cand:
            tr = cand
            break

    def body(c_ref, g_ref, a_ref, o_ref):
        o_ref[...] = (g_ref[...] + a_ref[...]).astype(BF16)

    return pl.pallas_call(
        body, name=name,
        grid_spec=pltpu.PrefetchScalarGridSpec(
            num_scalar_prefetch=1, grid=(N_CHIP, rh // tr),
            in_specs=[pl.BlockSpec((None, None, tr, cols), lambda q, i, c_ref: (q, c_ref[0], i, 0)),
                      pl.BlockSpec((None, tr, cols), lambda q, i, c_ref: (q, i, 0))],
            out_specs=pl.BlockSpec((None, tr, cols), lambda q, i, c_ref: (q, i, 0))),
        out_shape=jax.ShapeDtypeStruct((N_CHIP, rh, cols), BF16),
        compiler_params=_params(("parallel", "parallel"), 10 * _nbytes((tr, cols), F32)),
    )(c_idx, g, a)


def _scatter_and_gather(parts, small, name):
    n = len(parts)

    def body(*refs):
        scatter = _Scatter(refs[:n], refs[n + 1:2 * n + 1], *refs[2 * n + 2:2 * n + 4])
        gather = _SmallGather(refs[n], refs[2 * n + 1], *refs[2 * n + 4:])
        scatter.start()
        gather.start()
        gather.finish()
        scatter.finish()

    return pl.pallas_call(
        body, name=name,
        out_shape=_Scatter.out_shapes(parts) + [jax.ShapeDtypeStruct((N_DEV,) + small.shape, F32)],
        in_specs=[ANY] * n + [IN_VMEM], out_specs=[ANY] * n + [IN_VMEM],
        scratch_shapes=_Scatter.semaphores(n) + _SmallGather.semaphores(),
        compiler_params=_params(None, 10 * _nbytes(small.shape, F32)),
    )(*parts, small)


class _Scatter:
    def __init__(self, p_refs, out_refs, send_sems, recv_sems):
        x, y, c = _position()
        self.copies = []
        for k, (p_ref, out_ref) in enumerate(zip(p_refs, out_refs)):
            for j, (cx, cy, qj) in enumerate(_other_chips(x, y)):
                self.copies.append(pltpu.make_async_remote_copy(
                    src_ref=p_ref.at[qj], dst_ref=out_ref.at[j], send_sem=send_sems.at[3 * k + j],
                    recv_sem=recv_sems.at[3 * k + j], device_id=(cx, cy, c), device_id_type=MESH))

    @staticmethod
    def out_shapes(parts):
        return [jax.ShapeDtypeStruct((3,) + p.shape[1:], p.dtype) for p in parts]

    @staticmethod
    def semaphores(n):
        return [pltpu.SemaphoreType.DMA((3 * n,)), pltpu.SemaphoreType.DMA((3 * n,))]

    def start(self):
        for cp in self.copies:
            cp.start()

    def finish(self):
        for cp in self.copies:
            cp.wait_recv()
        for cp in self.copies:
            cp.wait_send()


def _shard_sum(p, b, qc_idx, name):
    _, rh, cols = p.shape
    tr = rh
    for cand in (256, 128, 64, 32, 16):
        if rh % cand == 0 and rh > cand:
            tr = cand
            break

    def body(qc_ref, p_ref, b_ref, o_ref):
        acc = p_ref[...].astype(F32)
        for j in range(3):
            acc = acc + b_ref[j].astype(F32)
        o_ref[...] = acc

    return pl.pallas_call(
        body, name=name,
        grid_spec=pltpu.PrefetchScalarGridSpec(
            num_scalar_prefetch=1, grid=(rh // tr,),
            in_specs=[pl.BlockSpec((None, tr, cols), lambda i, qc_ref: (qc_ref[0], i, 0)),
                      pl.BlockSpec((3, tr, cols), lambda i, qc_ref: (0, i, 0))],
            out_specs=pl.BlockSpec((None, tr, cols), lambda i, qc_ref: (qc_ref[1], i, 0))),
        out_shape=jax.ShapeDtypeStruct((2, rh, cols), F32),
        compiler_params=_params(("parallel",), 8 * _nbytes((tr, cols), F32)),
    )(qc_idx, p, b)


def _join_halves(shards):
    n = len(shards)

    def body(*refs):
        out_refs = refs[n:2 * n]
        send_sems, recv_sems = refs[2 * n:]
        x, y, c = _position()
        cps = [pltpu.make_async_remote_copy(src_ref=out_refs[k].at[c], dst_ref=out_refs[k].at[c], send_sem=send_sems.at[k],
                                            recv_sem=recv_sems.at[k], device_id=(x, y, 1 - c), device_id_type=MESH)
               for k in range(n)]
        for cp in cps:
            cp.start()
        for k in range(n):
            arriving = out_refs[k].at[1 - c]
            pltpu.make_async_remote_copy(src_ref=arriving, dst_ref=arriving, send_sem=send_sems.at[k], recv_sem=recv_sems.at[k],
                                         device_id=(x, y, 1 - c), device_id_type=MESH).wait_recv()
        for cp in cps:
            cp.wait_send()

    return pl.pallas_call(
        body, name="rs_join",
        out_shape=[jax.ShapeDtypeStruct(a.shape, a.dtype) for a in shards],
        in_specs=[ANY] * n, out_specs=[ANY] * n, input_output_aliases={k: k for k in range(n)},
        scratch_shapes=[pltpu.SemaphoreType.DMA((n,)), pltpu.SemaphoreType.DMA((n,))],
    )(*shards)


def _cols_from_shards(g):
    q, r, cs = g.shape
    return jnp.transpose(g, (1, 0, 2)).reshape(r, q * cs)


def _cols_to_shards(w):
    r, cfull = w.shape
    return jnp.transpose(w.reshape(r, N_CHIP, cfull // N_CHIP), (1, 0, 2))


def _pad_w_in(w):
    z = lambda n: jnp.zeros((w.shape[0], n), w.dtype)
    q_lat, kv_lat, kpe = w[:, 0:512], w[:, 512:768], w[:, 768:800]
    qd, kd, vd = w[:, 800:1312], w[:, 1312:1824], w[:, 1824:2336]
    return jnp.concatenate([q_lat, qd, kd, vd, kv_lat, z(KPE_OFF), kpe, z(LANE - KPE_OFF - ROPE)], axis=1)


def _pad_w_qb(w):
    w3 = w.reshape(Q_LORA, HEADS, NOPE + ROPE)
    return jnp.pad(w3, ((0, 0), (0, 0), (0, LANE - NOPE - ROPE))).reshape(Q_LORA, HEADS * LANE)


def _unpad_w_qb(g):
    return g.reshape(Q_LORA, HEADS, LANE)[:, :, :NOPE + ROPE].reshape(Q_LORA, HEADS * (NOPE + ROPE))


def _pad_w_kvb(w):
    w3 = w.reshape(KV_LORA, HEADS, 2 * NOPE)
    kp = jnp.pad(w3[:, :, :NOPE], ((0, 0), (0, 0), (0, LANE - NOPE))).reshape(KV_LORA, HEADS * LANE)
    return jnp.concatenate([kp, w3[:, :, NOPE:].reshape(KV_LORA, DIL_W)], axis=1)


def _unpad_w_kvb(g):
    gk = g[:, :HEADS * LANE].reshape(KV_LORA, HEADS, LANE)[:, :, :NOPE]
    gv = g[:, HEADS * LANE:].reshape(KV_LORA, HEADS, NOPE)
    return jnp.concatenate([gk, gv], axis=2).reshape(KV_LORA, HEADS * 2 * NOPE)


def _head_gains(g_q_nope, g_q_pe, g_k_nope, g_k_pe, g_dq, g_dk):
    z = lambda n: jnp.zeros((1, n), F32)
    q1 = jnp.concatenate([g_q_nope, g_q_pe, z(LANE - NOPE - ROPE)], axis=1)
    k1 = jnp.concatenate([g_k_nope, z(LANE - NOPE)], axis=1)
    kpe = jnp.concatenate([z(KPE_OFF), g_k_pe, z(LANE - KPE_OFF - ROPE)], axis=1)
    return dict(q=jnp.tile(q1, (1, HEADS)), k=jnp.tile(k1, (1, HEADS)), kpe=kpe,
                dq=jnp.tile(g_dq, (1, HEADS)), dk=jnp.tile(g_dk, (1, HEADS)))


def kernel(x, c, positions, w_ada, b_ada, g_mix_norm, w_in, g_q_lat, w_q_b, g_kv_lat, w_kv_b, g_mla_q_nope, g_mla_q_pe, g_mla_k_nope, g_mla_k_pe, g_dil_q, g_dil_k, w_o, g_ffn_norm, w_up, w_conv, b_conv, w_down, loss_target, m_w_ada, m_b_ada, m_g_mix_norm, m_w_in, m_g_q_lat, m_w_q_b, m_g_kv_lat, m_w_kv_b, m_g_mla_q_nope, m_g_mla_q_pe, m_g_mla_k_nope, m_g_mla_k_pe, m_g_dil_q, m_g_dil_k, m_w_o, m_g_ffn_norm, m_w_up, m_w_conv, m_b_conv, m_w_down, v_w_ada, v_b_ada, v_g_mix_norm, v_w_in, v_g_q_lat, v_w_q_b, v_g_kv_lat, v_w_kv_b, v_g_mla_q_nope, v_g_mla_q_pe, v_g_mla_k_nope, v_g_mla_k_pe, v_g_dil_q, v_g_dil_k, v_w_o, v_g_ffn_norm, v_w_up, v_w_conv, v_b_conv, v_w_down):
    args = dict(locals())
    weights = {n: args[n][0] for n in ("w_ada", "w_in", "w_q_b", "w_kv_b", "w_o", "w_up", "w_conv", "w_down")}
    small_w = {n: args[n] for n in ("b_ada",) + tuple(n for n, _ in SMALL_WIDTHS)}
    mom_m = {n[2:]: (args[n][0] if args[n].ndim == 3 else args[n]) for n in args if n.startswith("m_")}
    mom_v = {n[2:]: (args[n][0] if args[n].ndim == 3 else args[n]) for n in args if n.startswith("v_")}

    xi, yi, ci = _position()
    q0 = 2 * xi + yi
    me = 4 * xi + 2 * yi + ci
    xs, tgt = x[0], loss_target[0]
    s = xs.shape[0]
    consts = _seg_consts()
    c_idx, qc_idx = jnp.reshape(ci, (1,)).astype(I32), jnp.stack([q0, ci]).astype(I32)

    def halves(g4):
        q, r, cc = g4.shape
        return g4.reshape(q, 2, r // 2, cc)

    own_first = [weights[n].astype(BF16) for n in ("w_in", "w_q_b", "w_kv_b")]
    own_later = [weights[n].astype(BF16) for n in ("w_o", "w_up", "w_down")]
    conv_cols = UP_W // N_CHIP
    ada_cols = w_ada.shape[2]
    b_shard = lax.dynamic_slice_in_dim(b_ada, q0 * ada_cols, ada_cols, axis=1)
    c_taps = jnp.concatenate([c, weights["w_conv"].reshape(1, 3 * conv_cols)], axis=1)
    c_taps_all, mod_all, tab, *gathered = _prologue(c_taps, weights["w_ada"], b_shard, positions.reshape(s, 1),
                                                    _rope_consts(), own_first)
    c_all = c_taps_all[:, 0, :D_MODEL]
    w_conv_f = c_taps_all[:, 0, D_MODEL:].reshape(N_CHIP, 2, 3, conv_cols)[:, 0]
    w_conv_f = jnp.transpose(w_conv_f, (1, 0, 2)).reshape(3, UP_W)
    mod_all = mod_all.reshape(N_CHIP, 2, N_DEV, ada_cols)
    mod = lax.dynamic_index_in_dim(lax.dynamic_index_in_dim(mod_all, ci, 1, False), me, 1, False)
    mod = mod.reshape(1, N_CHIP * ada_cols)
    sh1, sc1, g1, sh2, sc2, g2 = [mod[:, k * D_MODEL:(k + 1) * D_MODEL] for k in range(6)]
    w_in_f = _cols_from_shards(gathered[0])
    w_in_p = _pad_w_in(w_in_f)
    w_qb_p = _pad_w_qb(_cols_from_shards(gathered[1]))
    w_kvb_p = _pad_w_kvb(_cols_from_shards(gathered[2]))
    gains = _head_gains(g_mla_q_nope, g_mla_q_pe, g_mla_k_nope, g_mla_k_pe, g_dil_q, g_dil_k)

    h = _prenorm(xs, g_mix_norm, sc1, sh1, "prenorm")
    proj = _mm(h, w_in_p, "nn", F32, 512, P_COLS, "mm_in")
    ql, kvl = _latnorm(proj, g_q_lat, g_kv_lat)
    q_raw = _mm(ql, w_qb_p, "nn", F32, 1024, HEADS * LANE, "mm_qb")
    kv_raw = _mm(kvl, w_kvb_p, "nn", F32, 1024, HEADS * LANE + DIL_W, "mm_kvb")
    qm, km, vm, qd, kd, vd = _attn_prep(q_raw, kv_raw, proj, tab, gains, consts)
    scale_m, scale_d = (NOPE + ROPE) ** -0.5, DIL_DIM ** -0.5
    o_m, lse_m, got_up = _attn_fwd(qm, km, vm, True, scale_m, "attn_mla", gather=own_later[1:2])
    o_d, lse_d, got_o, got_down = _attn_fwd(qd, kd, vd, False, scale_d, "attn_dil", gather=[own_later[0], own_later[2]])
    gathered = [got_o, got_up, got_down]
    w_o_f = gathered[0].reshape(D_MODEL, D_MODEL)
    w_up_f = _cols_from_shards(gathered[1])
    w_down_f = gathered[2].reshape(D_FF, D_MODEL)
    mix_in = jnp.concatenate([o_m, o_d], axis=1)
    mix, x1, h2 = _o_resid_prenorm(mix_in, w_o_f, xs, g1, g_ffn_norm, sc2, sh2)
    up = _mm(h2, w_up_f, "nn", F32, 1024, CONV_TILE, "mm_up")
    act = _conv_gate(up, w_conv_f, b_conv)
    dy, dffn, dg2, loss_part = _down_final(act, w_down_f, x1, tgt, g2)

    da = _mm(dffn, w_down_f, "nt", F32, 1024, CONV_TILE, "mm_down_dx")
    gw_down = _mm(act, dffn, "tn", F32, 256, D_MODEL, "mm_down_dw")
    dup_g, dup_v, dbg, dbv, dwg, dwv = _gate_bwd(up, da, w_conv_f, b_conv)
    dup = jnp.concatenate([dup_g, dup_v], axis=1)
    early_names = ("w_up", "w_down", "w_o")
    gw_up = _mm(h2, dup, "tn", F32, 1024, CONV_TILE, "mm_up_dw", col_shards=True)
    early = [halves(gw_up), halves(gw_down.reshape(N_CHIP, D_FF // N_CHIP, D_MODEL))]
    dh2, *early_sib = _mm(dup, w_up_f, "nt", F32, 256, 512, "mm_up_dx", swap=early, b_outer=True)
    dx1, dmix, acc2 = _ffnnorm_bwd(dh2, x1, dy, mix, g_ffn_norm, sc2, g1)
    gw_o = _mm(mix_in, dmix, "tn", F32, 1024, D_MODEL, "mm_o_dw")
    early.append(halves(gw_o.reshape(N_CHIP, D_MODEL // N_CHIP, D_MODEL)))
    dmix_in, sib_o = _mm(dmix, w_o_f, "nt", F32, 512, D_MODEL, "mm_o_dx", swap=early[2:])
    early_sib.append(sib_o)
    early_sums = [_pair_sum(g, a, c_idx, "pair_sum_" + n) for g, a, n in zip(early, early_sib, early_names)]
    dqm, dkm, dvm, *early_recv = _attn_bwd(qm, km, vm, o_m, dmix_in, 0, lse_m, True, scale_m, "attn_mla_bwd",
                                           scatter=early_sums[:1])
    dqd, dkd, dvd, *early_recv_d = _attn_bwd(qd, kd, vd, o_d, dmix_in, DIL_W // LANE, lse_d, False, scale_d,
                                             "attn_dil_bwd", scatter=early_sums[1:])
    early_recv = early_recv + early_recv_d
    dq_raw, dkv_raw, dkpe_b, dqd_b, dkd_b, dvd_b, dgains = _attn_prep_bwd(
        dqm, dkm, dvm, dqd, dkd, dvd, q_raw, kv_raw, proj, tab, gains, consts)
    dql = _mm(dq_raw, w_qb_p, "nt", F32, 1024, Q_LORA, "mm_qb_dx")
    gw_qb = _unpad_w_qb(_mm(ql, dq_raw, "tn", F32, Q_LORA, HEADS * LANE, "mm_qb_dw"))
    dkvl = _mm(dkv_raw, w_kvb_p, "nt", F32, 1024, KV_LORA, "mm_kvb_dx")
    gw_kvb = _unpad_w_kvb(_mm(kvl, dkv_raw, "tn", F32, KV_LORA, HEADS * LANE + DIL_W, "mm_kvb_dw"))
    dqlat_b, dkvlat_b, dglat = _latnorm_bwd(dql, dkvl, proj, g_q_lat, g_kv_lat)
    dproj = jnp.concatenate([dqlat_b, dkvlat_b, dkpe_b[:, KPE_OFF:KPE_OFF + ROPE], dqd_b, dkd_b, dvd_b], axis=1)
    gw_in = _mm(h, dproj, "tn", F32, 512, IN_COLS, "mm_in_dw")
    late_names = ("w_in", "w_q_b", "w_kv_b")
    late = [halves(_cols_to_shards(gw_in)), halves(_cols_to_shards(gw_qb)), halves(_cols_to_shards(gw_kvb))]
    dh, *late_sib = _mm(dproj, w_in_f, "nt", F32, 512, D_MODEL, "mm_in_dx", swap=late)
    grad_x, acc1 = _mixnorm_bwd(dh, xs, dx1, g_mix_norm, sc1)

    packed = _pack_small(acc1, acc2, dg2, dglat, dgains, dbg, dbv, dwg, dwv, loss_part)
    late_sums = [_pair_sum(g, a, c_idx, "pair_sum_" + n) for g, a, n in zip(late, late_sib, late_names)]
    *late_recv, gathered_small = _scatter_and_gather(late_sums, packed, "rs_scatter_late")

    grad_b_ada, *small_grads, gconv_full, loss_sum = _sum_unpack(gathered_small)
    grads = {"b_ada": grad_b_ada}
    grads.update({n: g for (n, _), g in zip(SMALL_WIDTHS, small_grads)})
    shard_cols = UP_W // N_CHIP
    grads["w_conv"] = lax.dynamic_slice_in_dim(gconv_full, q0 * shard_cols, shard_cols, axis=1)
    dmod_all = gathered_small[:, 0, :6 * D_MODEL]
    grads["w_ada"] = _ada_bwd(c_all, lax.dynamic_slice_in_dim(dmod_all, q0 * ada_cols, ada_cols, axis=1))

    big_names = late_names + early_names
    half_sums = [_shard_sum(p, b, qc_idx, "shard_sum_" + n)
                 for p, b, n in zip(late_sums + early_sums, list(late_recv) + list(early_recv), big_names)]
    for n, full in zip(big_names, _join_halves(half_sums)):
        grads[n] = full.reshape(2 * full.shape[1], full.shape[2])

    delta, new_m, new_v = {}, {}, {}
    for n in ("w_ada", "w_in", "w_q_b", "w_kv_b", "w_o", "w_up", "w_conv", "w_down"):
        operands = (weights[n], grads[n], mom_m[n], mom_v[n])
        flipped = n in ("w_in", "w_q_b")
        if flipped:
            operands = [jnp.swapaxes(a, 0, 1) for a in operands]
            grads[n] = jnp.swapaxes(operands[1], 0, 1)
        if n == "w_ada":
            operands = _in_hbm(*operands)
        delta[n], new_m[n], new_v[n] = _adamw(*operands, "adamw_" + n)
        if flipped:
            delta[n], new_m[n], new_v[n] = (jnp.swapaxes(a, 0, 1) for a in (delta[n], new_m[n], new_v[n]))
    vec_names = ("b_ada",) + tuple(n for n, _ in SMALL_WIDTHS)
    sd, sm, sv = _adamw_vectors(*[[d_[n] for n in vec_names] for d_ in (small_w, grads, mom_m, mom_v)])
    for k, n in enumerate(vec_names):
        delta[n], new_m[n], new_v[n] = sd[k], sm[k], sv[k]

    loss = loss_sum[0, 0]
    order = ("w_ada", "b_ada", "g_mix_norm", "w_in", "g_q_lat", "w_q_b", "g_kv_lat", "w_kv_b", "g_mla_q_nope", "g_mla_q_pe",
             "g_mla_k_nope", "g_mla_k_pe", "g_dil_q", "g_dil_k", "w_o", "g_ffn_norm", "w_up", "w_conv", "b_conv", "w_down")
    lead = lambda n, z: z[None] if n.startswith("w_") else z
    outs = [loss, grad_x[None]]
    for d_ in (grads, delta, new_m, new_v):
        outs += [lead(n, d_[n]) for n in order]
    return tuple(outs)
```

```python
import functools

import numpy as np
import jax
import jax.numpy as jnp
from jax import lax
from jax.experimental import pallas as pl
from jax.experimental.pallas import tpu as pltpu

F32 = jnp.float32
BF16 = jnp.bfloat16
I32 = jnp.int32

D_MODEL = 1024
HEADS = 8
NOPE = 64
ROPE = 32
Q_LORA = 512
KV_LORA = 256
DIL_DIM = 64
DIL_W = HEADS * DIL_DIM
D_FF = 2816
UP_W = 2 * D_FF
IN_COLS = Q_LORA + KV_LORA + ROPE + 3 * DIL_W
ROPE_THETA = 10000.0
EPS = 1e-6
NEG_INF = -1e30
N_DEV = 8
N_CHIP = 4

ADAM_LR = 0.001
ADAM_B1 = 0.9
ADAM_B2 = 0.999
ADAM_EPS = 1e-08
ADAM_WD = 0.01
ADAM_STEP = 10

LANE = 128
ROW_TILE = 256
NORM_TILE = 512
ATT_TQ = 512
ATT_TK = 256
ATT_TK_BWD = 512
LOG2E = 1.4426950408889634
LN2 = 0.6931471805599453
VMEM_CAP = 56 * 1024 * 1024
VMEM_FLOOR = 32 * 1024 * 1024

P_QLAT, P_QD, P_KD, P_VD, P_KVLAT, P_KPE = 0, 512, 1024, 1536, 2048, 2304
P_COLS = 2432
KPE_OFF = 64

NN = (((1,), (0,)), ((), ()))
NT = (((1,), (1,)), ((), ()))
TN = (((0,), (0,)), ((), ()))
HIGHEST = lax.Precision.HIGHEST
MESH = pl.DeviceIdType.MESH


def _params(sem=None, est_bytes=0):
    limit = int(min(max(2 * est_bytes + (4 << 20), VMEM_FLOOR), VMEM_CAP))
    if sem is None:
        return pltpu.CompilerParams(vmem_limit_bytes=limit)
    return pltpu.CompilerParams(dimension_semantics=sem, vmem_limit_bytes=limit)


def _nbytes(shape, dtype):
    return int(np.prod(shape)) * jnp.dtype(dtype).itemsize


def _in_hbm(*xs):
    return [pltpu.with_memory_space_constraint(x, pltpu.HBM) for x in xs]


def _mm(a, b, dims, out_dtype, tm, tn, name, col_shards=False, swap=(), b_outer=False):
    def spec(block, index):
        if b_outer:
            return pl.BlockSpec(block, lambda g0, g1: index(g1, g0))
        return pl.BlockSpec(block, index)

    if dims == "nn":
        (m, k), (k2, n) = a.shape, b.shape
        a_spec = spec((tm, k), lambda i, j: (i, 0))
        b_spec = spec((k, tn), lambda i, j: (0, j))
        dn = NN
    elif dims == "nt":
        (m, k), (n, k2) = a.shape, b.shape
        a_spec = spec((tm, k), lambda i, j: (i, 0))
        b_spec = spec((tn, k), lambda i, j: (j, 0))
        dn = NT
    else:
        (k, m), (k2, n) = a.shape, b.shape
        a_spec = spec((k, tm), lambda i, j: (0, i))
        b_spec = spec((k, tn), lambda i, j: (0, j))
        dn = TN
    assert k == k2 and m % tm == 0 and n % tn == 0, (name, a.shape, b.shape, tm, tn)

    nw = len(swap)
    grid = (n // tn, m // tm) if b_outer else (m // tm, n // tn)

    def body(*refs):
        a_ref, b_ref, o_ref = refs[0], refs[1], refs[2 + nw]
        comm = (refs[2:2 + nw], refs[3 + nw:3 + 2 * nw]) + tuple(refs[3 + 2 * nw:])
        if nw:
            @pl.when((pl.program_id(0) == 0) & (pl.program_id(1) == 0))
            def _():
                _PairSwap(*comm).start()

        o_ref[...] = lax.dot_general(a_ref[...], b_ref[...], dn, preferred_element_type=F32).astype(o_ref.dtype)

        if nw:
            @pl.when((pl.program_id(0) == grid[0] - 1) & (pl.program_id(1) == grid[1] - 1))
            def _():
                _PairSwap(*comm).finish()

    est = _nbytes((tm, k), a.dtype) + _nbytes((tn, k), b.dtype) + _nbytes((tm, tn), F32) + _nbytes((tm, tn), out_dtype)
    if col_shards:
        out_spec = spec((None, tm, tn), lambda i, j: (j, i, 0))
        out_shape = jax.ShapeDtypeStruct((n // tn, m, tn), out_dtype)
    else:
        out_spec = spec((tm, tn), lambda i, j: (i, j))
        out_shape = jax.ShapeDtypeStruct((m, n), out_dtype)
    out = pl.pallas_call(
        body, name=name, grid=grid,
        in_specs=[a_spec, b_spec] + [ANY] * nw,
        out_specs=[out_spec] + [ANY] * nw,
        out_shape=[out_shape] + _PairSwap.out_shapes(swap),
        scratch_shapes=_PairSwap.semaphores(nw) if nw else [],
        compiler_params=_params(("arbitrary", "arbitrary") if nw else ("parallel", "parallel"), est),
    )(a, b, *swap)
    return out if nw else out[0]


def _seg_consts():
    seg_q = np.zeros((HEADS * LANE, LANE), np.float32)
    inv_q = np.zeros((1, LANE), np.float32)
    seg_k = np.zeros((HEADS * LANE, LANE), np.float32)
    inv_k = np.zeros((1, LANE), np.float32)
    seg_d = np.zeros((DIL_W, LANE), np.float32)
    inv_d = np.zeros((1, LANE), np.float32)
    for h in range(HEADS):
        seg_q[h * LANE:h * LANE + NOPE, 2 * h] = 1.0
        seg_q[h * LANE + NOPE:h * LANE + NOPE + ROPE, 2 * h + 1] = 1.0
        inv_q[0, 2 * h], inv_q[0, 2 * h + 1] = 1.0 / NOPE, 1.0 / ROPE
        seg_k[h * LANE:h * LANE + NOPE, h] = 1.0
        inv_k[0, h] = 1.0 / NOPE
        seg_d[h * DIL_DIM:(h + 1) * DIL_DIM, h] = 1.0
        inv_d[0, h] = 1.0 / DIL_DIM
    fold_q = np.tile(np.eye(LANE, dtype=np.float32), (HEADS, 1))
    fold_d = np.zeros((DIL_W, LANE), np.float32)
    fold_d[np.arange(DIL_W), np.arange(DIL_W) % DIL_DIM] = 1.0
    j = lambda v: jnp.asarray(v)
    b = lambda v: jnp.asarray(v, dtype=BF16)
    return dict(seg_q=b(seg_q), exp_q=b(seg_q.T.copy()), inv_q=j(inv_q), seg_k=b(seg_k), exp_k=b(seg_k.T.copy()),
                inv_k=j(inv_k), seg_d=b(seg_d), exp_d=b(seg_d.T.copy()), inv_d=j(inv_d), fold_q=j(fold_q), fold_d=j(fold_d))


def _rope_consts():
    inv_d = jnp.power(ROPE_THETA, -2.0 * jnp.arange(DIL_DIM // 2, dtype=F32) / DIL_DIM)
    inv_q = jnp.power(ROPE_THETA, -2.0 * jnp.arange(ROPE // 2, dtype=F32) / ROPE)
    lanes = np.arange(LANE)
    freq_d = inv_d[lanes % (DIL_DIM // 2)]
    in_pe = (lanes >= KPE_OFF) & (lanes < KPE_OFF + ROPE)
    freq_q = jnp.where(jnp.asarray(in_pe), inv_q[(lanes - KPE_OFF) % (ROPE // 2)], 0.0)
    sign_d = np.where(lanes % DIL_DIM < DIL_DIM // 2, -1.0, 1.0).astype(np.float32)
    sign_q = np.where(in_pe, np.where((lanes - KPE_OFF) < ROPE // 2, -1.0, 1.0), 0.0).astype(np.float32)
    zeros, ones = np.zeros(LANE, np.float32), np.ones(LANE, np.float32)
    freq = jnp.concatenate([freq_d, freq_d, freq_q, freq_q])[None, :]
    csel = jnp.asarray(np.concatenate([ones, zeros, ones, zeros]))[None, :]
    ssel = jnp.asarray(np.concatenate([zeros, sign_d, zeros, sign_q]))[None, :]
    return freq, csel, ssel


def _full(shape):
    return pl.BlockSpec(shape, lambda *_: (0,) * len(shape))


def _tile_lanes(x, n):
    return jnp.concatenate([x] * n, axis=1)


def _rms(x):
    return lax.rsqrt(jnp.mean(x * x, axis=-1, keepdims=True) + EPS)


def _in_proj(x, gain, scale, shift, w_in, g_q, g_kv):
    s, d = x.shape
    cols = w_in.shape[1]

    def body(x_ref, g_ref, sc_ref, sh_ref, w_ref, gq_ref, gkv_ref, h_ref, p_ref, ql_ref, kvl_ref):
        xv = x_ref[...]
        h = ((xv * _rms(xv)) * g_ref[...] * (1.0 + sc_ref[...]) + sh_ref[...]).astype(BF16)
        h_ref[...] = h
        p_ref[...] = jnp.dot(h, w_ref[...], preferred_element_type=F32)
        q = p_ref[:, P_QLAT:P_QLAT + Q_LORA]
        kv = p_ref[:, P_KVLAT:P_KVLAT + KV_LORA]
        ql_ref[...] = ((q * _rms(q)) * gq_ref[...]).astype(BF16)
        kvl_ref[...] = ((kv * _rms(kv)) * gkv_ref[...]).astype(BF16)

    def rows(c):
        return pl.BlockSpec((NORM_TILE, c), lambda i: (i, 0))

    vec = _full((1, d))
    est = (_nbytes((NORM_TILE, d), F32) + _nbytes((d, cols), BF16) + 2 * _nbytes((NORM_TILE, cols), F32)
           + _nbytes((NORM_TILE, d), F32))
    return pl.pallas_call(
        body, name="prenorm_mm_in_latnorm", grid=(s // NORM_TILE,),
        in_specs=[rows(d), vec, vec, vec, _full((d, cols)), _full((1, Q_LORA)), _full((1, KV_LORA))],
        out_specs=[rows(d), rows(cols), rows(Q_LORA), rows(KV_LORA)],
        out_shape=[jax.ShapeDtypeStruct((s, d), BF16), jax.ShapeDtypeStruct((s, cols), F32),
                   jax.ShapeDtypeStruct((s, Q_LORA), BF16), jax.ShapeDtypeStruct((s, KV_LORA), BF16)],
        compiler_params=_params(("parallel",), est),
    )(x, gain, scale, shift, w_in, g_q, g_kv)


def _dot01(v, mat01):
    hi = v.astype(BF16)
    lo = (v - hi.astype(F32)).astype(BF16)
    return jnp.dot(hi, mat01, preferred_element_type=F32) + jnp.dot(lo, mat01, preferred_element_type=F32)


def _seg_rinv(x, seg, exp, inv):
    r = lax.rsqrt(_dot01(x * x, seg) * inv + EPS)
    return _dot01(r, exp)


def _seg_mean(v, seg, exp, inv):
    return _dot01(_dot01(v, seg) * inv, exp)


def _swap_halves(x, half):
    n = x.shape[1]
    lane = lax.broadcasted_iota(I32, (1, n), 1)
    first = (lane & (2 * half - 1)) < half
    return jnp.where(first, pltpu.roll(x, n - half, 1), pltpu.roll(x, half, 1))


def _rope(x, cos, sin_signed, half):
    return x * cos + _swap_halves(x, half) * sin_signed


def _rope_bwd(dy, cos, sin_signed, half):
    return dy * cos + _swap_halves(dy * sin_signed, half)


def _pe_lane_mask(n):
    lane = lax.broadcasted_iota(I32, (1, n), 1) & (LANE - 1)
    return (lane >= KPE_OFF) & (lane < KPE_OFF + ROPE)


def _attn_prep(q_raw, kv_raw, proj, tab, gains, consts):
    s = q_raw.shape[0]
    hw = HEADS * LANE

    def body(q_ref, kv_ref, kpe_ref, qd_ref, kd_ref, vd_ref, tab_ref,
             gq_ref, gk_ref, gkpe_ref, gdq_ref, gdk_ref,
             segq_ref, expq_ref, invq_ref, segk_ref, expk_ref, invk_ref, segd_ref, expd_ref, invd_ref,
             qm_ref, km_ref, vm_ref, qdo_ref, kdo_ref, vdo_ref):
        tab_v = tab_ref[...]
        cos_d, sin_d = _tile_lanes(tab_v[:, 0:LANE], DIL_W // LANE), _tile_lanes(tab_v[:, LANE:2 * LANE], DIL_W // LANE)
        cos_q1, sin_q1 = tab_v[:, 2 * LANE:3 * LANE], tab_v[:, 3 * LANE:4 * LANE]
        cos_q, sin_q = _tile_lanes(cos_q1, HEADS), _tile_lanes(sin_q1, HEADS)

        q = q_ref[...]
        qn = q * _seg_rinv(q, segq_ref[...], expq_ref[...], invq_ref[...]) * gq_ref[...]
        qm_ref[...] = _rope(qn, cos_q, sin_q, ROPE // 2).astype(BF16)

        kv = kv_ref[...]
        kp = kv[:, :hw]
        kn = kp * _seg_rinv(kp, segk_ref[...], expk_ref[...], invk_ref[...]) * gk_ref[...]
        kpe = kpe_ref[...]
        r_pe = lax.rsqrt(jnp.sum(kpe * kpe, axis=-1, keepdims=True) * (1.0 / ROPE) + EPS)
        kpe_r = _rope(kpe * r_pe * gkpe_ref[...], cos_q1, sin_q1, ROPE // 2)
        km_ref[...] = (kn + _tile_lanes(kpe_r, HEADS)).astype(BF16)
        vm_ref[...] = kv[:, hw:].astype(BF16)

        qd = qd_ref[...]
        qdn = qd * _seg_rinv(qd, segd_ref[...], expd_ref[...], invd_ref[...]) * gdq_ref[...]
        qdo_ref[...] = _rope(qdn, cos_d, sin_d, DIL_DIM // 2).astype(BF16)
        kd = kd_ref[...]
        kdn = kd * _seg_rinv(kd, segd_ref[...], expd_ref[...], invd_ref[...]) * gdk_ref[...]
        kdo_ref[...] = _rope(kdn, cos_d, sin_d, DIL_DIM // 2).astype(BF16)
        vdo_ref[...] = vd_ref[...].astype(BF16)

    t = ROW_TILE
    row = lambda w, cb=0: pl.BlockSpec((t, w), lambda i: (i, cb))
    c = consts
    return pl.pallas_call(
        body, name="attn_prep", grid=(s // t,),
        in_specs=[row(hw), row(hw + DIL_W), row(LANE, P_KPE // LANE), row(DIL_W, P_QD // DIL_W), row(DIL_W, P_KD // DIL_W),
                  row(DIL_W, P_VD // DIL_W), row(4 * LANE),
                  _full((1, hw)), _full((1, hw)), _full((1, LANE)), _full((1, DIL_W)), _full((1, DIL_W)),
                  _full((hw, LANE)), _full((LANE, hw)), _full((1, LANE)), _full((hw, LANE)), _full((LANE, hw)), _full((1, LANE)),
                  _full((DIL_W, LANE)), _full((LANE, DIL_W)), _full((1, LANE))],
        out_specs=[row(hw), row(hw), row(DIL_W), row(DIL_W), row(DIL_W), row(DIL_W)],
        out_shape=[jax.ShapeDtypeStruct((s, hw), BF16), jax.ShapeDtypeStruct((s, hw), BF16)]
        + [jax.ShapeDtypeStruct((s, DIL_W), BF16)] * 4,
        compiler_params=_params(("parallel",), 24 << 20),
    )(*_in_hbm(q_raw, kv_raw, proj, proj, proj, proj), tab, gains["q"], gains["k"], gains["kpe"], gains["dq"], gains["dk"],
      c["seg_q"], c["exp_q"], c["inv_q"], c["seg_k"], c["exp_k"], c["inv_k"], c["seg_d"], c["exp_d"], c["inv_d"])


def _attn_prep_bwd(dqm, dkm, dvm, dqd, dkd, dvd, q_raw, kv_raw, proj, tab, gains, consts):
    s = q_raw.shape[0]
    hw = HEADS * LANE
    n_steps = s // ROW_TILE

    def body(dqm_ref, dkm_ref, dvm_ref, dqd_ref, dkd_ref, dvd_ref, q_ref, kv_ref, kpe_ref, qd_ref, kd_ref, tab_ref,
             gq_ref, gk_ref, gkpe_ref, gdq_ref, gdk_ref,
             segq_ref, expq_ref, invq_ref, segk_ref, expk_ref, invk_ref, segd_ref, expd_ref, invd_ref, foldq_ref, foldd_ref,
             dq_ref, dkv_ref, dkpe_ref, dqdo_ref, dkdo_ref, dvdo_ref, dg_ref, acc_ref):
        i = pl.program_id(0)

        @pl.when(i == 0)
        def _():
            acc_ref[...] = jnp.zeros_like(acc_ref)

        tab_v = tab_ref[...]
        cos_d, sin_d = _tile_lanes(tab_v[:, 0:LANE], DIL_W // LANE), _tile_lanes(tab_v[:, LANE:2 * LANE], DIL_W // LANE)
        cos_q1, sin_q1 = tab_v[:, 2 * LANE:3 * LANE], tab_v[:, 3 * LANE:4 * LANE]
        cos_q, sin_q = _tile_lanes(cos_q1, HEADS), _tile_lanes(sin_q1, HEADS)

        def norm_bwd(x, dyg, gain, seg, exp, inv):
            rinv = _seg_rinv(x, seg, exp, inv)
            xn = x * rinv
            dxn = dyg * gain
            dx = rinv * (dxn - xn * _seg_mean(dxn * xn, seg, exp, inv))
            return dx, jnp.sum(dyg * xn, axis=0, keepdims=True)

        dq, gq_l = norm_bwd(q_ref[...], _rope_bwd(dqm_ref[...], cos_q, sin_q, ROPE // 2), gq_ref[...],
                            segq_ref[...], expq_ref[...], invq_ref[...])
        dq_ref[...] = dq.astype(BF16)

        dkm = dkm_ref[...]
        kv = kv_ref[...]
        dkp, gk_l = norm_bwd(kv[:, :hw], dkm, gk_ref[...], segk_ref[...], expk_ref[...], invk_ref[...])
        dkv_ref[:, :hw] = dkp.astype(BF16)
        dkv_ref[:, hw:] = dvm_ref[...].astype(BF16)

        dkpe_r = dkm[:, 0:LANE]
        for h in range(1, HEADS):
            dkpe_r = dkpe_r + dkm[:, h * LANE:(h + 1) * LANE]
        dkpe_r = jnp.where(_pe_lane_mask(LANE), dkpe_r, 0.0)
        dyg = _rope_bwd(dkpe_r, cos_q1, sin_q1, ROPE // 2)
        kpe = kpe_ref[...]
        r_pe = lax.rsqrt(jnp.sum(kpe * kpe, axis=-1, keepdims=True) * (1.0 / ROPE) + EPS)
        xn = kpe * r_pe
        dxn = dyg * gkpe_ref[...]
        dkpe = r_pe * (dxn - xn * (jnp.sum(dxn * xn, axis=-1, keepdims=True) * (1.0 / ROPE)))
        dkpe_ref[...] = dkpe.astype(BF16)
        gkpe_l = jnp.sum(dyg * xn, axis=0, keepdims=True)

        dqd_v, gdq_l = norm_bwd(qd_ref[...], _rope_bwd(dqd_ref[...], cos_d, sin_d, DIL_DIM // 2), gdq_ref[...],
                                segd_ref[...], expd_ref[...], invd_ref[...])
        dqdo_ref[...] = dqd_v.astype(BF16)
        dkd_v, gdk_l = norm_bwd(kd_ref[...], _rope_bwd(dkd_ref[...], cos_d, sin_d, DIL_DIM // 2), gdk_ref[...],
                                segd_ref[...], expd_ref[...], invd_ref[...])
        dkdo_ref[...] = dkd_v.astype(BF16)
        dvdo_ref[...] = dvd_ref[...].astype(BF16)

        acc_ref[0:1, :] += gq_l
        acc_ref[1:2, :] += gk_l
        acc_ref[2:3, 0:LANE] += gkpe_l
        acc_ref[3:4, 0:DIL_W] += gdq_l
        acc_ref[4:5, 0:DIL_W] += gdk_l

        @pl.when(i == n_steps - 1)
        def _():
            acc = acc_ref[...]
            fq = jnp.dot(acc, foldq_ref[...], precision=HIGHEST, preferred_element_type=F32)
            fd = jnp.dot(acc[:, 0:DIL_W], foldd_ref[...], precision=HIGHEST, preferred_element_type=F32)
            rows = lax.broadcasted_iota(I32, (8, LANE), 0)
            base = jnp.where(rows < 2, fq, jnp.where(rows == 2, acc[:, 0:LANE], fd))
            at0 = pltpu.roll(base, LANE - KPE_OFF, 1)
            dg_ref[...] = jnp.where(rows == 5, pltpu.roll(at0, 5, 0), jnp.where(rows == 2, at0, base))

    t = ROW_TILE
    row = lambda w, cb=0: pl.BlockSpec((t, w), lambda i: (i, cb))
    c = consts
    return pl.pallas_call(
        body, name="attn_prep_bwd", grid=(n_steps,),
        in_specs=[row(hw), row(hw), row(DIL_W), row(DIL_W), row(DIL_W), row(DIL_W),
                  row(hw), row(hw + DIL_W), row(LANE, P_KPE // LANE), row(DIL_W, P_QD // DIL_W), row(DIL_W, P_KD // DIL_W),
                  row(4 * LANE),
                  _full((1, hw)), _full((1, hw)), _full((1, LANE)), _full((1, DIL_W)), _full((1, DIL_W)),
                  _full((hw, LANE)), _full((LANE, hw)), _full((1, LANE)), _full((hw, LANE)), _full((LANE, hw)), _full((1, LANE)),
                  _full((DIL_W, LANE)), _full((LANE, DIL_W)), _full((1, LANE)), _full((hw, LANE)), _full((DIL_W, LANE))],
        out_specs=[row(hw), row(hw + DIL_W), row(LANE), row(DIL_W), row(DIL_W), row(DIL_W), _full((8, LANE))],
        out_shape=[jax.ShapeDtypeStruct((s, hw), BF16), jax.ShapeDtypeStruct((s, hw + DIL_W), BF16),
                   jax.ShapeDtypeStruct((s, LANE), BF16)] + [jax.ShapeDtypeStruct((s, DIL_W), BF16)] * 3
        + [jax.ShapeDtypeStruct((8, LANE), F32)],
        scratch_shapes=[pltpu.VMEM((8, hw), F32)],
        compiler_params=_params(("arbitrary",), 28 << 20),
    )(*_in_hbm(dqm, dkm, dvm, dqd, dkd, dvd, q_raw, kv_raw, proj, proj, proj), tab,
      gains["q"], gains["k"], gains["kpe"], gains["dq"], gains["dk"],
      c["seg_q"], c["exp_q"], c["inv_q"], c["seg_k"], c["exp_k"], c["inv_k"], c["seg_d"], c["exp_d"], c["inv_d"],
      c["fold_q"], c["fold_d"])


def _latnorm_bwd(dql, dkvl, proj, g_q, g_kv):
    s = proj.shape[0]
    n_steps = s // NORM_TILE

    def body(dql_ref, dkvl_ref, q_ref, kv_ref, gq_ref, gkv_ref, dq_ref, dkv_ref, dg_ref):
        i = pl.program_id(0)

        @pl.when(i == 0)
        def _():
            dg_ref[...] = jnp.zeros_like(dg_ref)

        def one(x, dyg, gain):
            r = _rms(x)
            xn = x * r
            dxn = dyg * gain
            dx = r * (dxn - xn * jnp.mean(dxn * xn, axis=-1, keepdims=True))
            return dx, jnp.sum(dyg * xn, axis=0, keepdims=True)

        dq, gq_l = one(q_ref[...], dql_ref[...], gq_ref[...])
        dkv, gkv_l = one(kv_ref[...], dkvl_ref[...], gkv_ref[...])
        dq_ref[...] = dq.astype(BF16)
        dkv_ref[...] = dkv.astype(BF16)
        dg_ref[0:1, :] += gq_l
        dg_ref[1:2, 0:KV_LORA] += gkv_l

    t = NORM_TILE
    return pl.pallas_call(
        body, name="latnorm_bwd", grid=(n_steps,),
        in_specs=[pl.BlockSpec((t, Q_LORA), lambda i: (i, 0)), pl.BlockSpec((t, KV_LORA), lambda i: (i, 0)),
                  pl.BlockSpec((t, Q_LORA), lambda i: (i, P_QLAT // Q_LORA)),
                  pl.BlockSpec((t, KV_LORA), lambda i: (i, P_KVLAT // KV_LORA)),
                  _full((1, Q_LORA)), _full((1, KV_LORA))],
        out_specs=[pl.BlockSpec((t, Q_LORA), lambda i: (i, 0)), pl.BlockSpec((t, KV_LORA), lambda i: (i, 0)), _full((8, Q_LORA))],
        out_shape=[jax.ShapeDtypeStruct((s, Q_LORA), BF16), jax.ShapeDtypeStruct((s, KV_LORA), BF16),
                   jax.ShapeDtypeStruct((8, Q_LORA), F32)],
        compiler_params=_params(("arbitrary",)),
    )(dql, dkvl, proj, proj, g_q, g_kv)


def _o_resid_prenorm(mix_in, w_o, x, g1, gain, scale, shift):
    s, d = x.shape
    k = mix_in.shape[1]

    def body(a_ref, w_ref, x_ref, g1_ref, g_ref, sc_ref, sh_ref, mix_ref, x1_ref, h_ref):
        mix = jnp.dot(a_ref[...], w_ref[...], preferred_element_type=F32)
        mix_ref[...] = mix
        x1 = x_ref[...] + g1_ref[...] * mix
        x1_ref[...] = x1
        h_ref[...] = ((x1 * _rms(x1)) * g_ref[...] * (1.0 + sc_ref[...]) + sh_ref[...]).astype(BF16)

    row = pl.BlockSpec((NORM_TILE, d), lambda i: (i, 0))
    vec = _full((1, d))
    est = _nbytes((NORM_TILE, k), BF16) + _nbytes((k, d), BF16) + 4 * _nbytes((NORM_TILE, d), F32)
    return pl.pallas_call(
        body, name="mm_o_resid_prenorm", grid=(s // NORM_TILE,),
        in_specs=[pl.BlockSpec((NORM_TILE, k), lambda i: (i, 0)), _full((k, d)), row, vec, vec, vec, vec],
        out_specs=[row, row, row],
        out_shape=[jax.ShapeDtypeStruct((s, d), F32), jax.ShapeDtypeStruct((s, d), F32),
                   jax.ShapeDtypeStruct((s, d), BF16)],
        compiler_params=_params(("parallel",), est),
    )(mix_in, w_o, x, g1, gain, scale, shift)


CONV_TILE = 1408
HALO = 8


def _shift_down(x, halo, k):
    t = x.shape[0]
    row = lax.broadcasted_iota(I32, (t, 1), 0)
    out = pltpu.roll(x, k, 0)
    for r in range(k):
        out = jnp.where(row == r, halo[HALO - k + r:HALO - k + r + 1, :], out)
    return out


def _shift_up(x, halo, k):
    t = x.shape[0]
    row = lax.broadcasted_iota(I32, (t, 1), 0)
    out = pltpu.roll(x, t - k, 0)
    for r in range(k):
        out = jnp.where(row == t - k + r, halo[r:r + 1, :], out)
    return out


def _conv_fwd(x, halo, w, b):
    p1, p2 = _shift_down(x, halo, 1), _shift_down(x, halo, 2)
    u = b + p2 * w[0:1, :]
    u = u + p1 * w[1:2, :]
    u = u + x * w[2:3, :]
    return u, p1, p2


def _sigmoid(x):
    return 0.5 * jnp.tanh(0.5 * x) + 0.5


def _conv_gate(up, w_conv, b_conv):
    s = up.shape[0]
    t = ROW_TILE
    nj = D_FF // CONV_TILE
    hb = t // HALO

    def body(g_ref, v_ref, gh_ref, vh_ref, wg_ref, wv_ref, bg_ref, bv_ref, a_ref):
        live = (pl.program_id(0) > 0).astype(F32)
        ug, _, _ = _conv_fwd(g_ref[...], gh_ref[...] * live, wg_ref[...], bg_ref[...])
        uv, _, _ = _conv_fwd(v_ref[...], vh_ref[...] * live, wv_ref[...], bv_ref[...])
        a_ref[...] = (ug * _sigmoid(ug) * uv).astype(BF16)

    main = lambda off: pl.BlockSpec((t, CONV_TILE), lambda i, j: (i, j + off))
    halo = lambda off: pl.BlockSpec((HALO, CONV_TILE), lambda i, j: (jnp.maximum(i * hb - 1, 0), j + off))
    wsp = lambda off: pl.BlockSpec((3, CONV_TILE), lambda i, j: (0, j + off))
    bsp = lambda off: pl.BlockSpec((1, CONV_TILE), lambda i, j: (0, j + off))
    return pl.pallas_call(
        body, name="conv_gate", grid=(s // t, nj),
        in_specs=[main(0), main(nj), halo(0), halo(nj), wsp(0), wsp(nj), bsp(0), bsp(nj)],
        out_specs=pl.BlockSpec((t, CONV_TILE), lambda i, j: (i, j)),
        out_shape=jax.ShapeDtypeStruct((s, D_FF), BF16),
        compiler_params=_params(("parallel", "parallel"), 12 << 20),
    )(up, up, up, up, w_conv, w_conv, b_conv, b_conv)


def _gate_bwd(up, da, w_conv, b_conv):
    s = up.shape[0]
    t = ROW_TILE
    nj = D_FF // CONV_TILE
    hb = t // HALO
    n_i = s // t

    def body(g_ref, v_ref, gh_ref, vh_ref, gn_ref, vn_ref, da_ref, dan_ref, wg_ref, wv_ref, bg_ref, bv_ref,
             dupg_ref, dupv_ref, dbg_ref, dbv_ref, dwg_ref, dwv_ref):
        i = pl.program_id(1)

        @pl.when(i == 0)
        def _():
            for r in (dbg_ref, dbv_ref, dwg_ref, dwv_ref):
                r[...] = jnp.zeros_like(r)

        def d_gate(ug, uv, da_v):
            sg = _sigmoid(ug)
            return da_v * uv * (sg * (1.0 + ug * (1.0 - sg))), da_v * (ug * sg)

        live = (i > 0).astype(F32)
        xg, xv = g_ref[...], v_ref[...]
        wg, wv = wg_ref[...], wv_ref[...]
        ug, g1, g2 = _conv_fwd(xg, gh_ref[...] * live, wg, bg_ref[...])
        uv, v1, v2 = _conv_fwd(xv, vh_ref[...] * live, wv, bv_ref[...])
        dug, duv = d_gate(ug, uv, da_ref[...])

        more = (i < n_i - 1).astype(F32)
        ug_n, _, _ = _conv_fwd(gn_ref[...], xg[t - HALO:, :], wg, bg_ref[...])
        uv_n, _, _ = _conv_fwd(vn_ref[...], xv[t - HALO:, :], wv, bv_ref[...])
        dug_n, duv_n = d_gate(ug_n, uv_n, dan_ref[...] * more)

        def conv_t(du, du_n, w):
            return du * w[2:3, :] + _shift_up(du, du_n, 1) * w[1:2, :] + _shift_up(du, du_n, 2) * w[0:1, :]

        dupg_ref[...] = conv_t(dug, dug_n, wg).astype(BF16)
        dupv_ref[...] = conv_t(duv, duv_n, wv).astype(BF16)
        csum = lambda z: jnp.sum(z, axis=0, keepdims=True)
        dbg_ref[...] += csum(dug)
        dbv_ref[...] += csum(duv)
        dwg_ref[0:1, :] += csum(dug * g2)
        dwg_ref[1:2, :] += csum(dug * g1)
        dwg_ref[2:3, :] += csum(dug * xg)
        dwv_ref[0:1, :] += csum(duv * v2)
        dwv_ref[1:2, :] += csum(duv * v1)
        dwv_ref[2:3, :] += csum(duv * xv)

    last_halo = s // HALO - 1
    main = lambda off: pl.BlockSpec((t, CONV_TILE), lambda j, i: (i, j + off))
    halo = lambda off: pl.BlockSpec((HALO, CONV_TILE), lambda j, i: (jnp.maximum(i * hb - 1, 0), j + off))
    nxt = lambda off: pl.BlockSpec((HALO, CONV_TILE), lambda j, i: (jnp.minimum((i + 1) * hb, last_halo), j + off))
    wsp = lambda off: pl.BlockSpec((3, CONV_TILE), lambda j, i: (0, j + off))
    bsp = lambda off: pl.BlockSpec((1, CONV_TILE), lambda j, i: (0, j + off))
    outs = pl.pallas_call(
        body, name="gate_bwd", grid=(nj, n_i),
        in_specs=[main(0), main(nj), halo(0), halo(nj), nxt(0), nxt(nj), main(0), nxt(0),
                  wsp(0), wsp(nj), bsp(0), bsp(nj)],
        out_specs=[main(0), main(0),
                   pl.BlockSpec((1, CONV_TILE), lambda j, i: (0, j)), pl.BlockSpec((1, CONV_TILE), lambda j, i: (0, j)),
                   pl.BlockSpec((3, CONV_TILE), lambda j, i: (0, j)), pl.BlockSpec((3, CONV_TILE), lambda j, i: (0, j))],
        out_shape=[jax.ShapeDtypeStruct((s, D_FF), BF16), jax.ShapeDtypeStruct((s, D_FF), BF16),
                   jax.ShapeDtypeStruct((1, D_FF), F32), jax.ShapeDtypeStruct((1, D_FF), F32),
                   jax.ShapeDtypeStruct((3, D_FF), F32), jax.ShapeDtypeStruct((3, D_FF), F32)],
        compiler_params=_params(("parallel", "arbitrary"), 24 << 20),
    )(up, up, up, up, up, up, da, da, w_conv, w_conv, b_conv, b_conv)
    return outs


def _down_final(act, w_down, x1, tgt, g2):
    s, d = x1.shape
    k = act.shape[1]
    n_steps = s // NORM_TILE

    def body(a_ref, w_ref, x1_ref, t_ref, g2_ref, dy_ref, df_ref, dg2_ref, loss_ref, lacc_ref):
        i = pl.program_id(0)

        @pl.when(i == 0)
        def _():
            dg2_ref[...] = jnp.zeros_like(dg2_ref)
            lacc_ref[...] = jnp.zeros_like(lacc_ref)

        f = jnp.dot(a_ref[...], w_ref[...], preferred_element_type=F32)
        e = x1_ref[...] + g2_ref[...] * f - t_ref[...]
        dy = e * (1.0 / d)
        dy_ref[...] = dy
        df_ref[...] = (dy * g2_ref[...]).astype(BF16)
        dg2_ref[...] += jnp.sum(dy * f, axis=0, keepdims=True)
        lacc_ref[...] += jnp.sum(e * e, axis=0, keepdims=True)

        @pl.when(i == n_steps - 1)
        def _():
            loss_ref[...] = jnp.sum(lacc_ref[...], axis=1, keepdims=True) * (0.5 / d)

    row = pl.BlockSpec((NORM_TILE, d), lambda i: (i, 0))
    est = (_nbytes((NORM_TILE, k), BF16) + _nbytes((k, d), BF16) + 4 * _nbytes((NORM_TILE, d), F32))
    return pl.pallas_call(
        body, name="mm_down_final", grid=(n_steps,),
        in_specs=[pl.BlockSpec((NORM_TILE, k), lambda i: (i, 0)), _full((k, d)), row, row, _full((1, d))],
        out_specs=[row, row, _full((1, d)), _full((1, 1))],
        out_shape=[jax.ShapeDtypeStruct((s, d), F32), jax.ShapeDtypeStruct((s, d), BF16),
                   jax.ShapeDtypeStruct((1, d), F32), jax.ShapeDtypeStruct((1, 1), F32)],
        scratch_shapes=[pltpu.VMEM((1, d), F32)],
        compiler_params=_params(("arbitrary",), est),
    )(act, w_down, x1, tgt, g2)


def _ffnnorm_bwd(dh2, x1, dy, mix, gain, scale, g1):
    s, d = x1.shape
    n_steps = s // NORM_TILE

    def body(dh_ref, x_ref, dy_ref, mix_ref, g_ref, sc_ref, g1_ref, dx_ref, dm_ref, acc_ref):
        i = pl.program_id(0)

        @pl.when(i == 0)
        def _():
            acc_ref[...] = jnp.zeros_like(acc_ref)

        dh, x = dh_ref[...], x_ref[...]
        r = _rms(x)
        xn = x * r
        dn = dh * (1.0 + sc_ref[...])
        dxn = dn * g_ref[...]
        dx = dy_ref[...] + r * (dxn - xn * jnp.mean(dxn * xn, axis=-1, keepdims=True))
        dx_ref[...] = dx
        dm_ref[...] = (dx * g1_ref[...]).astype(BF16)
        csum = lambda z: jnp.sum(z, axis=0, keepdims=True)
        acc_ref[0:1, :] += csum(dh)
        acc_ref[1:2, :] += csum(dh * (xn * g_ref[...]))
        acc_ref[2:3, :] += csum(dn * xn)
        acc_ref[3:4, :] += csum(dx * mix_ref[...])

    row = pl.BlockSpec((NORM_TILE, d), lambda i: (i, 0))
    vec = _full((1, d))
    return pl.pallas_call(
        body, name="ffnnorm_bwd", grid=(n_steps,),
        in_specs=[row, row, row, row, vec, vec, vec],
        out_specs=[row, row, _full((8, d))],
        out_shape=[jax.ShapeDtypeStruct((s, d), F32), jax.ShapeDtypeStruct((s, d), BF16), jax.ShapeDtypeStruct((8, d), F32)],
        compiler_params=_params(("arbitrary",)),
    )(dh2, x1, dy, mix, gain, scale, g1)


def _mixnorm_bwd(dh, x, dx1, gain, scale):
    s, d = x.shape
    n_steps = s // NORM_TILE

    def body(dh_ref, x_ref, dx1_ref, g_ref, sc_ref, gx_ref, acc_ref):
        i = pl.program_id(0)

        @pl.when(i == 0)
        def _():
            acc_ref[...] = jnp.zeros_like(acc_ref)

        dh, x = dh_ref[...], x_ref[...]
        r = _rms(x)
        xn = x * r
        dn = dh * (1.0 + sc_ref[...])
        dxn = dn * g_ref[...]
        gx_ref[...] = dx1_ref[...] + r * (dxn - xn * jnp.mean(dxn * xn, axis=-1, keepdims=True))
        csum = lambda z: jnp.sum(z, axis=0, keepdims=True)
        acc_ref[0:1, :] += csum(dh)
        acc_ref[1:2, :] += csum(dh * (xn * g_ref[...]))
        acc_ref[2:3, :] += csum(dn * xn)

    row = pl.BlockSpec((NORM_TILE, d), lambda i: (i, 0))
    vec = _full((1, d))
    return pl.pallas_call(
        body, name="mixnorm_bwd", grid=(n_steps,),
        in_specs=[row, row, row, vec, vec],
        out_specs=[row, _full((8, d))],
        out_shape=[jax.ShapeDtypeStruct((s, d), F32), jax.ShapeDtypeStruct((8, d), F32)],
        compiler_params=_params(("arbitrary",)),
    )(dh, x, dx1, gain, scale)


def _key_count(d, dilated):
    if not dilated:
        return jnp.where(d >= 0, 1.0, 0.0)
    one = lambda cond: jnp.where(cond, 1.0, 0.0)
    cnt = one(d <= 128) + one(((d & 3) == 0) & (d <= 512)) + one((d & 15) == 0)
    return jnp.where(d >= 0, cnt, 0.0)


def _block_kinds(mla):
    return (0, "diag", "none") if mla else (NEAR_REACH, "near", "far")


NEAR_REACH = 512


def _near_offsets(tk, tq):
    return (NEAR_REACH - (tk - tq)) // tk + 1


def _scores_t(ka, qa, scale, kind, rel_t, offset, near_tabs=None):
    return _mask_scores(lax.dot_general(ka, qa, NT, preferred_element_type=F32), scale, kind, rel_t, offset, near_tabs)


def _fill_near_tables(bias_ref, cnt_ref, rel_t):
    tk, tq = rel_t.shape
    for idx in range(_near_offsets(tk, tq)):
        cnt = _key_count(rel_t + (tk - tq) + idx * tk, True)
        cnt_ref[idx] = cnt
        bias_ref[idx] = jnp.where(cnt > 0.0, 0.0, NEG_INF)


def _mask_scores(products, scale, kind, rel_t, offset, near_tabs=None):
    st = products * (scale * LOG2E)
    cnt = None
    if kind == "diag":
        st = jnp.where(rel_t + offset >= 0, st, NEG_INF)
    elif kind == "far":
        st = jnp.where((rel_t & 15) == 0, st, NEG_INF)
    elif kind == "near":
        bias_ref, cnt_ref = near_tabs
        tk, tq = rel_t.shape
        idx = (offset - (tk - tq)) // tk
        st = st + bias_ref[idx]
        cnt = cnt_ref[idx]
    return st, cnt


def _attn_fwd(q, k, v, mla, scale, name, gather=()):
    s = q.shape[0]
    qw = 2 * LANE if mla else LANE
    tq, tk = ATT_TQ, ATT_TK
    reach, kind_near, kind_far = _block_kinds(mla)
    assert s % tq == 0 and tq % tk == 0 and reach % tk == 0 and reach in (0, NEAR_REACH)
    ng = len(gather)
    last_step = HEADS // 2 - 1

    def body(*refs):
        q_ref, k_ref, v_ref = refs[:3]
        o_ref, lse_ref = refs[3 + ng:5 + ng]
        vt_ref, st_ref = refs[5 + 2 * ng:7 + 2 * ng]
        near_tabs = None if mla else refs[7 + 2 * ng:9 + 2 * ng]
        n_tabs = 0 if mla else 2
        comm = (refs[3:3 + ng], refs[5 + ng:5 + 2 * ng]) + tuple(refs[7 + n_tabs + 2 * ng:])
        if ng:
            @pl.when(pl.program_id(0) == 0)
            def _():
                _Gather(*comm).start()

            @pl.when(pl.program_id(0) == last_step)
            def _():
                _Gather(*comm).forward()

        lane = lax.broadcasted_iota(I32, (1, LANE), 1)
        rel_t = lax.broadcasted_iota(I32, (tk, tq), 1) - lax.broadcasted_iota(I32, (tk, tq), 0)
        if not mla:
            _fill_near_tables(*near_tabs, rel_t)

        def transpose_v(j, carry):
            c0 = pl.multiple_of(j * tk, tk)
            vt_ref[:, pl.ds(c0, tk)] = v_ref[pl.ds(c0, tk), :].astype(F32).T.astype(BF16)
            return carry

        lax.fori_loop(0, s // tk, transpose_v, 0)

        def q_block(qi, carry):
            r0 = pl.multiple_of(qi * tq, tq)
            kcols = [slice(a * LANE, (a + 1) * LANE) if mla else slice(0, LANE) for a in range(2)]
            qas = [q_ref[pl.ds(r0, tq), kcols[a]] for a in range(2)]
            if not mla:
                qas = [jnp.where(lane < DIL_DIM, qas[0], jnp.zeros_like(qas[0])),
                       jnp.where(lane >= DIL_DIM, qas[1], jnp.zeros_like(qas[1]))]

            n_k = (r0 + tq) // tk

            def products(kj):
                c0 = pl.multiple_of(kj * tk, tk)
                return [lax.dot_general(k_ref[pl.ds(c0, tk), kcols[a]], qas[a], NT, preferred_element_type=F32)
                        for a in range(2)]

            for a, pr in enumerate(products(0)):
                st_ref[0, a] = pr

            def k_block(kj, c, kind):
                c0 = pl.multiple_of(kj * tk, tk)
                slot = kj & 1
                ahead = products(jnp.minimum(kj + 1, n_k - 1))
                out = []
                for a in range(2):
                    m, l, acc = c[a]
                    st, cnt = _mask_scores(st_ref[slot, a], scale, kind, rel_t, r0 - c0, near_tabs)
                    st_ref[1 - slot, a] = ahead[a]
                    m_new = jnp.maximum(m, jnp.max(st, axis=0, keepdims=True))
                    alpha = jnp.exp2(m - m_new)
                    p = jnp.exp2(st - m_new)
                    if cnt is not None:
                        p = p * cnt
                    l = alpha * l + jnp.sum(p, axis=0, keepdims=True)
                    vt = vt_ref[a * DIL_DIM:(a + 1) * DIL_DIM, pl.ds(c0, tk)]
                    acc = alpha * acc + jnp.dot(vt, p.astype(BF16), preferred_element_type=F32)
                    out.append((m_new, l, acc))
                return tuple(out)

            one = (jnp.full((1, tq), NEG_INF, F32), jnp.zeros((1, tq), F32), jnp.zeros((DIL_DIM, tq), F32))
            first_near = jnp.maximum((r0 - reach) // tk, 0)
            c = lax.fori_loop(0, first_near, functools.partial(k_block, kind=kind_far), (one, one))
            res = lax.fori_loop(first_near, (r0 + tq) // tk, functools.partial(k_block, kind=kind_near), c)
            o_t = jnp.concatenate([res[a][2] / res[a][1] for a in range(2)], axis=0)
            o_ref[pl.ds(r0, tq), :] = o_t.T.astype(BF16)
            for a in range(2):
                lse_ref[a, :, pl.ds(r0, tq)] = res[a][0] * LN2 + jnp.log(res[a][1])
            return carry

        lax.fori_loop(0, s // tq, q_block, 0)

        if ng:
            @pl.when(pl.program_id(0) == last_step)
            def _():
                _Gather(*comm).finish()

    return pl.pallas_call(
        body, name=name, grid=(HEADS // 2,),
        in_specs=[pl.BlockSpec((s, qw), lambda h: (0, h)), pl.BlockSpec((s, qw), lambda h: (0, h)),
                  pl.BlockSpec((s, LANE), lambda h: (0, h))] + [ANY] * ng,
        out_specs=[pl.BlockSpec((s, LANE), lambda h: (0, h)), pl.BlockSpec((2, 1, s), lambda h: (h, 0, 0))] + [ANY] * ng,
        out_shape=[jax.ShapeDtypeStruct((s, DIL_W), BF16), jax.ShapeDtypeStruct((HEADS, 1, s), F32)] + _Gather.out_shapes(gather),
        scratch_shapes=[pltpu.VMEM((LANE, s), BF16), pltpu.VMEM((2, 2, tk, tq), F32)]
        + ([] if mla else [pltpu.VMEM((_near_offsets(tk, tq), tk, tq), F32)] * 2) + (_Gather.scratch(gather) if ng else []),
        compiler_params=_params(("arbitrary",) if ng else ("parallel",), 12 << 20),
    )(*_in_hbm(q, k, v), *gather)


def _attn_bwd(q, k, v, o, do, do_block0, lse, mla, scale, name, scatter=()):
    s = q.shape[0]
    qw = 2 * LANE if mla else LANE
    tq, tk = ATT_TQ, ATT_TK_BWD
    nq = s // tq
    reach, kind_near, kind_far = _block_kinds(mla)
    assert s % tq == 0 and s % tk == 0
    ns = len(scatter)
    last_step = HEADS // 2 - 1

    def body(*refs):
        q_ref, k_ref, v_ref, o_ref, do_ref, lse_ref = refs[:6]
        dq_ref, dk_ref, dv_ref = refs[6 + ns:9 + ns]
        kt_ref, dot_ref, dob_ref, dqt_ref, delta_ref, lse2_ref = refs[9 + 2 * ns:15 + 2 * ns]
        near_tabs = None if mla else refs[15 + 2 * ns:17 + 2 * ns]
        n_tabs = 0 if mla else 2
        comm = (refs[6:6 + ns], refs[9 + ns:9 + 2 * ns]) + tuple(refs[15 + n_tabs + 2 * ns:])
        if ns:
            @pl.when(pl.program_id(0) == 0)
            def _():
                _Scatter(*comm).start()

        lane = lax.broadcasted_iota(I32, (1, LANE), 1)
        row = lax.broadcasted_iota(I32, (LANE, 1), 0)
        rel_t = lax.broadcasted_iota(I32, (tk, tq), 1) - lax.broadcasted_iota(I32, (tk, tq), 0)
        if not mla:
            _fill_near_tables(*near_tabs, rel_t)

        def prepare(j, carry):
            c0 = pl.multiple_of(j * tk, tk)
            do_blk = do_ref[pl.ds(c0, tk), :]
            dob_ref[pl.ds(c0, tk), :] = do_blk.astype(BF16)
            do_t = do_blk.T
            dot_ref[:, pl.ds(c0, tk)] = do_t.astype(BF16)
            prod = do_t * o_ref[pl.ds(c0, tk), :].astype(F32).T
            delta_ref[0, :, pl.ds(c0, tk)] = jnp.sum(prod[0:DIL_DIM], axis=0, keepdims=True)
            delta_ref[1, :, pl.ds(c0, tk)] = jnp.sum(prod[DIL_DIM:LANE], axis=0, keepdims=True)
            for w in range(qw // LANE):
                kt_ref[w * LANE:(w + 1) * LANE, pl.ds(c0, tk)] = (
                    k_ref[pl.ds(c0, tk), w * LANE:(w + 1) * LANE].astype(F32).T.astype(BF16))
            return carry

        lax.fori_loop(0, s // tk, prepare, 0)
        dqt_ref[...] = jnp.zeros_like(dqt_ref)
        lse2_ref[...] = lse_ref[...] * LOG2E

        sels = [lane < DIL_DIM, lane >= DIL_DIM]
        rsels = [row < DIL_DIM, row >= DIL_DIM]
        cols = [slice(a * LANE, (a + 1) * LANE) if mla else slice(0, LANE) for a in range(2)]

        def k_block(kj, carry):
            c0 = pl.multiple_of(kj * tk, tk)
            kas = [k_ref[pl.ds(c0, tk), cols[a]] for a in range(2)]
            kts = [kt_ref[cols[a], pl.ds(c0, tk)] for a in range(2)]
            if not mla:
                kas = [jnp.where(sels[a], kas[a], jnp.zeros_like(kas[a])) for a in range(2)]
                kts = [jnp.where(rsels[a], kts[a], jnp.zeros_like(kts[a])) for a in range(2)]
            vb = v_ref[pl.ds(c0, tk), :]
            vbs = [jnp.where(sels[a], vb, jnp.zeros_like(vb)) for a in range(2)]

            first = c0 // tq

            def q_block(qi, c, kind):
                r0 = pl.multiple_of(qi * tq, tq)
                out, dq_parts = [], []
                for a in range(2):
                    dk_acc, dv_acc = c[a]
                    qa = q_ref[pl.ds(r0, tq), cols[a]]
                    st, cnt = _scores_t(kas[a], qa, scale, kind, rel_t, r0 - c0, near_tabs)
                    p = jnp.exp2(st - lse2_ref[a, :, pl.ds(r0, tq)])
                    if cnt is not None:
                        p = p * cnt
                    dp = jnp.dot(vbs[a], dot_ref[:, pl.ds(r0, tq)], preferred_element_type=F32)
                    ds = (p * (dp - delta_ref[a, :, pl.ds(r0, tq)]) * scale).astype(BF16)
                    dv_acc = dv_acc + jnp.dot(p.astype(BF16), dob_ref[pl.ds(r0, tq), :], preferred_element_type=F32)
                    dk_acc = dk_acc + jnp.dot(ds, qa, preferred_element_type=F32)
                    dq_parts.append(jnp.dot(kts[a], ds, preferred_element_type=F32))
                    out.append((dk_acc, dv_acc))
                if mla:
                    for a in range(2):
                        dqt_ref[cols[a], pl.ds(r0, tq)] += dq_parts[a]
                else:
                    dqt_ref[:, pl.ds(r0, tq)] += dq_parts[0] + dq_parts[1]
                return tuple(out)

            zero = jnp.zeros((tk, LANE), F32)
            last_near = jnp.minimum((c0 + tk - 1 + reach) // tq + 1, nq)
            c = lax.fori_loop(first, last_near, functools.partial(q_block, kind=kind_near), ((zero, zero), (zero, zero)))
            (dk0, dv0), (dk1, dv1) = lax.fori_loop(last_near, nq, functools.partial(q_block, kind=kind_far), c)
            if mla:
                dk_ref[pl.ds(c0, tk), cols[0]] = dk0
                dk_ref[pl.ds(c0, tk), cols[1]] = dk1
            else:
                dk_ref[pl.ds(c0, tk), :] = jnp.where(sels[0], dk0, dk1)
            dv_ref[pl.ds(c0, tk), :] = jnp.where(sels[0], dv0, dv1)
            return carry

        lax.fori_loop(0, s // tk, k_block, 0)

        def write_dq(j, carry):
            c0 = pl.multiple_of(j * tk, tk)
            for w in range(qw // LANE):
                dq_ref[pl.ds(c0, tk), w * LANE:(w + 1) * LANE] = dqt_ref[w * LANE:(w + 1) * LANE, pl.ds(c0, tk)].T
            return carry

        lax.fori_loop(0, s // tk, write_dq, 0)

        if ns:
            @pl.when(pl.program_id(0) == last_step)
            def _():
                _Scatter(*comm).finish()

    b0 = do_block0
    return pl.pallas_call(
        body, name=name, grid=(HEADS // 2,),
        in_specs=[pl.BlockSpec((s, qw), lambda h: (0, h)), pl.BlockSpec((s, qw), lambda h: (0, h)),
                  pl.BlockSpec((s, LANE), lambda h: (0, h)), pl.BlockSpec((s, LANE), lambda h: (0, h)),
                  pl.BlockSpec((s, LANE), lambda h: (0, h + b0)), pl.BlockSpec((2, 1, s), lambda h: (h, 0, 0))] + [ANY] * ns,
        out_specs=[pl.BlockSpec((s, qw), lambda h: (0, h)), pl.BlockSpec((s, qw), lambda h: (0, h)),
                   pl.BlockSpec((s, LANE), lambda h: (0, h))] + [ANY] * ns,
        out_shape=[jax.ShapeDtypeStruct(q.shape, F32), jax.ShapeDtypeStruct(k.shape, F32), jax.ShapeDtypeStruct((s, DIL_W), F32)]
        + _Scatter.out_shapes(scatter),
        scratch_shapes=[pltpu.VMEM((qw, s), BF16), pltpu.VMEM((LANE, s), BF16), pltpu.VMEM((s, LANE), BF16),
                        pltpu.VMEM((qw, s), F32), pltpu.VMEM((2, 1, s), F32), pltpu.VMEM((2, 1, s), F32)]
        + ([] if mla else [pltpu.VMEM((_near_offsets(tk, tq), tk, tq), F32)] * 2) + (_Scatter.semaphores(ns) if ns else []),
        compiler_params=_params(("arbitrary",) if ns else ("parallel",), 24 << 20),
    )(*_in_hbm(q, k, v, o, do, lse), *scatter)


def _ada_bwd(c_all, dmod_shard):
    n, d = c_all.shape
    cols = dmod_shard.shape[1]

    def body(c_ref, g_ref, o_ref):
        cv = c_ref[...]
        o_ref[...] = lax.dot_general(cv * _sigmoid(cv), g_ref[...], TN, precision=HIGHEST, preferred_element_type=F32)

    return pl.pallas_call(
        body, name="ada_bwd", out_shape=jax.ShapeDtypeStruct((d, cols), F32),
        compiler_params=_params(None, 16 << 20),
    )(c_all, dmod_shard)


SMALL_WIDTHS = (("g_mix_norm", D_MODEL), ("g_q_lat", Q_LORA), ("g_kv_lat", KV_LORA), ("g_mla_q_nope", NOPE),
                ("g_mla_q_pe", ROPE), ("g_mla_k_nope", NOPE), ("g_mla_k_pe", ROPE), ("g_dil_q", DIL_DIM),
                ("g_dil_k", DIL_DIM), ("g_ffn_norm", D_MODEL), ("b_conv", UP_W))


def _small_layout():
    pieces = (("dmod", 6 * D_MODEL),) + SMALL_WIDTHS + tuple(("w_conv%d" % k, UP_W) for k in range(3)) + (("loss", 1),)
    layout, off = {}, 0
    for name, width in pieces:
        layout[name] = (width, off)
        off += -(-width // LANE) * LANE
    return layout, off


def _pack_small(acc1, acc2, dg2, dglat, dgains, dbg, dbv, dwg, dwv, loss_part):
    layout, total = _small_layout()

    def body(a1, a2, g2, gl, gg, bg, bv, wg, wv, ls, o_ref):
        o_ref[...] = jnp.zeros_like(o_ref)

        def put(name, src, shift=0):
            start = layout[name][1] + shift
            o_ref[:, start:start + src.shape[1]] = src

        for k, src in enumerate((a1[0:1, :], a1[1:2, :], a2[3:4, :], a2[0:1, :], a2[1:2, :], g2[...])):
            put("dmod", src, k * D_MODEL)
        put("g_mix_norm", a1[2:3, :])
        put("g_q_lat", gl[0:1, :])
        put("g_kv_lat", gl[1:2, 0:KV_LORA])
        put("g_mla_q_nope", gg[0:1, 0:NOPE])
        put("g_mla_q_pe", gg[5:6, 0:ROPE])
        put("g_mla_k_nope", gg[1:2, 0:NOPE])
        put("g_mla_k_pe", gg[2:3, 0:ROPE])
        put("g_dil_q", gg[3:4, 0:DIL_DIM])
        put("g_dil_k", gg[4:5, 0:DIL_DIM])
        put("g_ffn_norm", a2[2:3, :])
        put("b_conv", bg[...])
        put("b_conv", bv[...], D_FF)
        for k in range(3):
            put("w_conv%d" % k, wg[k:k + 1, :])
            put("w_conv%d" % k, wv[k:k + 1, :], D_FF)
        put("loss", ls[...])

    ins = (acc1, acc2, dg2, dglat, dgains, dbg, dbv, dwg, dwv, loss_part)
    return pl.pallas_call(
        body, name="pack_small", grid=(1,), in_specs=[_full(a.shape) for a in ins], out_specs=_full((1, total)),
        out_shape=jax.ShapeDtypeStruct((1, total), F32),
        compiler_params=_params(("arbitrary",), 2 << 20),
    )(*_in_hbm(*ins))


def _sum_unpack(g):
    n_dev, _, total = g.shape
    layout, _ = _small_layout()

    def body(g_ref, *refs):
        o_refs, s_ref = refs[:-1], refs[-1]
        acc = g_ref[0]
        for k in range(1, n_dev):
            acc = acc + g_ref[k]
        s_ref[...] = acc
        take = lambda name: s_ref[:, layout[name][1]:layout[name][1] + layout[name][0]]
        o_refs[0][...] = take("dmod")
        for i, (name, _) in enumerate(SMALL_WIDTHS):
            o_refs[1 + i][...] = take(name)
        for k in range(3):
            o_refs[-2][k:k + 1, :] = take("w_conv%d" % k)
        o_refs[-1][...] = take("loss")

    shapes = [(1, 6 * D_MODEL)] + [(1, w) for _, w in SMALL_WIDTHS] + [(3, UP_W), (1, 1)]
    return pl.pallas_call(
        body, name="sum_unpack", out_shape=[jax.ShapeDtypeStruct(sh, F32) for sh in shapes],
        scratch_shapes=[pltpu.VMEM((1, total), F32)],
        compiler_params=_params(None, 4 << 20),
    )(g)


def _adamw_math(w, g, m, v):
    mn = ADAM_B1 * m + (1.0 - ADAM_B1) * g
    vn = ADAM_B2 * v + (1.0 - ADAM_B2) * (g * g)
    m_hat = mn / (1.0 - ADAM_B1 ** ADAM_STEP)
    v_hat = vn / (1.0 - ADAM_B2 ** ADAM_STEP)
    return -ADAM_LR * (m_hat / (jnp.sqrt(v_hat) + ADAM_EPS) + ADAM_WD * w), mn, vn


def _adamw_vectors(ws, gs, ms, vs):
    k = len(ws)

    def body(*refs):
        for i in range(k):
            d, mn, vn = _adamw_math(refs[i][...], refs[k + i][...], refs[2 * k + i][...], refs[3 * k + i][...])
            refs[4 * k + i][...] = d
            refs[5 * k + i][...] = mn
            refs[6 * k + i][...] = vn

    blocks = [_full(w.shape) for w in ws]
    outs = pl.pallas_call(
        body, name="adamw_vectors", grid=(1,), in_specs=blocks * 4, out_specs=blocks * 3,
        out_shape=[jax.ShapeDtypeStruct(w.shape, F32) for w in ws] * 3,
        compiler_params=_params(("arbitrary",), 2 << 20),
    )(*_in_hbm(*ws, *gs, *ms, *vs))
    return outs[:k], outs[k:2 * k], outs[2 * k:]


def _adamw(w, g, m, v, name):
    r, c = w.shape
    tr = r
    for cand in (256, 128, 64, 32, 16):
        if r % cand == 0 and r > cand:
            tr = cand
            break

    def body(w_ref, g_ref, m_ref, v_ref, d_ref, mo_ref, vo_ref):
        d_ref[...], mo_ref[...], vo_ref[...] = _adamw_math(w_ref[...], g_ref[...], m_ref[...], v_ref[...])

    blk = pl.BlockSpec((tr, c), lambda i: (i, 0))
    return pl.pallas_call(
        body, name=name, grid=(r // tr,), in_specs=[blk] * 4, out_specs=[blk] * 3,
        out_shape=[jax.ShapeDtypeStruct((r, c), F32)] * 3,
        compiler_params=_params(("parallel",), 7 * _nbytes((tr, c), F32)),
    )(w, g, m, v)


def _position():
    return lax.axis_index("x"), lax.axis_index("y"), lax.axis_index("c")


def _other_chips(x, y):
    return [(1 - x, y, 2 * (1 - x) + y), (x, 1 - y, 2 * x + (1 - y)), (1 - x, 1 - y, 2 * (1 - x) + (1 - y))]


class _SmallGather:
    def __init__(self, v_ref, out_ref, send_sems, recv_sems, local_sem):
        x, y, c = _position()
        me = 4 * x + 2 * y + c
        self.local = pltpu.make_async_copy(v_ref, out_ref.at[me], local_sem)
        self.sends, self.arrivals = [], []
        for k in range(N_DEV - 1):
            fx, fy, fc = ((k + 1) >> 2) & 1, ((k + 1) >> 1) & 1, (k + 1) & 1
            px, py, pc = (1 - x if fx else x), (1 - y if fy else y), (1 - c if fc else c)

            def copy(dst, k=k, peer=(px, py, pc)):
                return pltpu.make_async_remote_copy(src_ref=v_ref, dst_ref=dst, send_sem=send_sems.at[k],
                                                    recv_sem=recv_sems.at[k], device_id=peer, device_id_type=MESH)

            self.sends.append(copy(out_ref.at[me]))
            self.arrivals.append(copy(out_ref.at[4 * px + 2 * py + pc]))

    @staticmethod
    def semaphores():
        return [pltpu.SemaphoreType.DMA((N_DEV - 1,)), pltpu.SemaphoreType.DMA((N_DEV - 1,)), pltpu.SemaphoreType.DMA]

    def start(self):
        self.local.start()
        for cp in self.sends:
            cp.start()

    def finish(self):
        for cp in self.arrivals:
            cp.wait_recv()
        for cp in self.sends:
            cp.wait_send()
        self.local.wait()


def _prologue(c_taps, w_ada_shard, b_shard, pos_col, rope_consts, shards):
    n = len(shards)
    s = pos_col.shape[0]
    cols = w_ada_shard.shape[1]
    freq, csel, ssel = rope_consts

    def body(*refs):
        ct_ref, w_ref, b_ref, p_ref, f_ref, cs_ref, ss_ref = refs[:7]
        sh_refs = refs[7:7 + n]
        ct_all_ref, mod_all_ref, tab_ref = refs[7 + n:10 + n]
        g_refs = refs[10 + n:10 + 2 * n]
        mod_blk_ref = refs[10 + 2 * n]
        sems = refs[11 + 2 * n:]
        first = _SmallGather(ct_ref, ct_all_ref, *sems[0:3])
        first.start()
        first.finish()
        cv = ct_all_ref[:, 0, 0:D_MODEL]
        sc = (cv * _sigmoid(cv)).astype(BF16)
        mod_blk_ref[...] = jnp.dot(sc, w_ref[...].astype(BF16), preferred_element_type=F32) + b_ref[...]
        second = _SmallGather(mod_blk_ref, mod_all_ref, *sems[3:6])
        second.start()
        weights = _Gather(sh_refs, g_refs, *sems[6:])
        weights.start()

        def table_rows(i, carry):
            r0 = pl.multiple_of(i * ROW_TILE, ROW_TILE)
            ang = p_ref[pl.ds(r0, ROW_TILE), :].astype(F32) * f_ref[...]
            tab_ref[pl.ds(r0, ROW_TILE), :] = cs_ref[...] * jnp.cos(ang) + ss_ref[...] * jnp.sin(ang)
            return carry

        lax.fori_loop(0, s // ROW_TILE, table_rows, 0)
        second.finish()
        weights.forward()
        weights.finish()

    return pl.pallas_call(
        body, name="prologue",
        out_shape=[jax.ShapeDtypeStruct((N_DEV,) + c_taps.shape, F32), jax.ShapeDtypeStruct((N_DEV, N_DEV, cols), F32),
                   jax.ShapeDtypeStruct((s, 4 * LANE), F32)] + _Gather.out_shapes(shards),
        in_specs=[IN_VMEM] * 7 + [ANY] * n, out_specs=[IN_VMEM] * 3 + [ANY] * n,
        scratch_shapes=[pltpu.VMEM((N_DEV, cols), F32)] + _SmallGather.semaphores() * 2 + _Gather.scratch(shards),
        compiler_params=_params(None, 14 << 20),
    )(c_taps, w_ada_shard, b_shard, pos_col, freq, csel, ssel, *shards)


IN_VMEM = pl.BlockSpec(memory_space=pltpu.VMEM)
ANY = pl.BlockSpec(memory_space=pl.ANY)


class _Gather:
    def __init__(self, w_refs, out_refs, send_sems, recv_sems, own_sems, *bounce_refs):
        x, y, c = _position()
        q0 = 2 * x + y
        sibling = (x, y, 1 - c)
        self.ici, self.ici_in, self.fwd, self.fwd_in, self.own_in, self.own_out = [], [], [], [], [], []
        for k, (w_ref, out_ref) in enumerate(zip(w_refs, out_refs)):
            half = w_ref.shape[0] // 2
            self.own_in.append(pltpu.make_async_copy(w_ref, bounce_refs[k], own_sems.at[2 * k]))
            self.own_out.append(pltpu.make_async_copy(bounce_refs[k], out_ref.at[q0], own_sems.at[2 * k + 1]))

            def blk(q, e, out_ref=out_ref, half=half):
                return out_ref.at[q, pl.ds(pl.multiple_of(e * half, 16), half), :]

            def copy(src, dst, i, to):
                return pltpu.make_async_remote_copy(src_ref=src, dst_ref=dst, send_sem=send_sems.at[i], recv_sem=recv_sems.at[i],
                                                    device_id=to, device_id_type=MESH)

            src = w_ref.at[pl.ds(pl.multiple_of(c * half, 16), half), :]
            for j, (cx, cy, qj) in enumerate(_other_chips(x, y)):
                self.ici.append(copy(src, blk(q0, c), 6 * k + j, (cx, cy, c)))
                self.ici_in.append(copy(blk(qj, c), blk(qj, c), 6 * k + j, (cx, cy, c)))
                self.fwd.append(copy(blk(qj, c), blk(qj, c), 6 * k + 3 + j, sibling))
                self.fwd_in.append(copy(blk(qj, 1 - c), blk(qj, 1 - c), 6 * k + 3 + j, sibling))

    @staticmethod
    def out_shapes(shards):
        return [jax.ShapeDtypeStruct((N_CHIP,) + s.shape, s.dtype) for s in shards]

    @staticmethod
    def scratch(shards):
        n = len(shards)
        return ([pltpu.SemaphoreType.DMA((6 * n,)), pltpu.SemaphoreType.DMA((6 * n,)), pltpu.SemaphoreType.DMA((2 * n,))]
                + [pltpu.VMEM(s.shape, s.dtype) for s in shards])

    def start(self):
        for cp in self.ici + self.own_in:
            cp.start()

    def forward(self):
        for fetched, placed in zip(self.own_in, self.own_out):
            fetched.wait()
            placed.start()
        for arrived, onward in zip(self.ici_in, self.fwd):
            arrived.wait_recv()
            onward.start()

    def finish(self):
        for cp in self.fwd_in:
            cp.wait_recv()
        for cp in self.ici + self.fwd:
            cp.wait_send()
        for cp in self.own_out:
            cp.wait()


class _PairSwap:
    def __init__(self, g_refs, out_refs, send_sems, recv_sems):
        x, y, c = _position()
        self.copies = [
            pltpu.make_async_remote_copy(src_ref=g_ref.at[:, 1 - c], dst_ref=out_ref, send_sem=send_sems.at[k],
                                         recv_sem=recv_sems.at[k], device_id=(x, y, 1 - c), device_id_type=MESH)
            for k, (g_ref, out_ref) in enumerate(zip(g_refs, out_refs))]

    @staticmethod
    def out_shapes(grads):
        return [jax.ShapeDtypeStruct((N_CHIP,) + g.shape[2:], g.dtype) for g in grads]

    @staticmethod
    def semaphores(n):
        return [pltpu.SemaphoreType.DMA((n,)), pltpu.SemaphoreType.DMA((n,))]

    def start(self):
        for cp in self.copies:
            cp.start()

    def finish(self):
        for cp in self.copies:
            cp.wait_recv()
        for cp in self.copies:
            cp.wait_send()


def _pair_sum(g, a, c_idx, name):
    _, _, rh, cols = g.shape
    tr = rh
    for cand in (256, 128, 64, 32, 16):
        if rh % cand == 0 and rh > cand:
            tr = cand
            break

    def body(c_ref, g_ref, a_ref, o_ref):
        o_ref[...] = (g_ref[...] + a_ref[...]).astype(BF16)

    return pl.pallas_call(
        body, name=name,
        grid_spec=pltpu.PrefetchScalarGridSpec(
            num_scalar_prefetch=1, grid=(N_CHIP, rh // tr),
            in_specs=[pl.BlockSpec((None, None, tr, cols), lambda q, i, c_ref: (q, c_ref[0], i, 0)),
                      pl.BlockSpec((None, tr, cols), lambda q, i, c_ref: (q, i, 0))],
            out_specs=pl.BlockSpec((None, tr, cols), lambda q, i, c_ref: (q, i, 0))),
        out_shape=jax.ShapeDtypeStruct((N_CHIP, rh, cols), BF16),
        compiler_params=_params(("parallel", "parallel"), 10 * _nbytes((tr, cols), F32)),
    )(c_idx, g, a)


def _scatter_and_gather(parts, small, name):
    n = len(parts)

    def body(*refs):
        scatter = _Scatter(refs[:n], refs[n + 1:2 * n + 1], *refs[2 * n + 2:2 * n + 4])
        gather = _SmallGather(refs[n], refs[2 * n + 1], *refs[2 * n + 4:])
        scatter.start()
        gather.start()
        gather.finish()
        scatter.finish()

    return pl.pallas_call(
        body, name=name,
        out_shape=_Scatter.out_shapes(parts) + [jax.ShapeDtypeStruct((N_DEV,) + small.shape, F32)],
        in_specs=[ANY] * n + [IN_VMEM], out_specs=[ANY] * n + [IN_VMEM],
        scratch_shapes=_Scatter.semaphores(n) + _SmallGather.semaphores(),
        compiler_params=_params(None, 10 * _nbytes(small.shape, F32)),
    )(*parts, small)


class _Scatter:
    def __init__(self, p_refs, out_refs, send_sems, recv_sems):
        x, y, c = _position()
        self.copies = []
        for k, (p_ref, out_ref) in enumerate(zip(p_refs, out_refs)):
            for j, (cx, cy, qj) in enumerate(_other_chips(x, y)):
                self.copies.append(pltpu.make_async_remote_copy(
                    src_ref=p_ref.at[qj], dst_ref=out_ref.at[j], send_sem=send_sems.at[3 * k + j],
                    recv_sem=recv_sems.at[3 * k + j], device_id=(cx, cy, c), device_id_type=MESH))

    @staticmethod
    def out_shapes(parts):
        return [jax.ShapeDtypeStruct((3,) + p.shape[1:], p.dtype) for p in parts]

    @staticmethod
    def semaphores(n):
        return [pltpu.SemaphoreType.DMA((3 * n,)), pltpu.SemaphoreType.DMA((3 * n,))]

    def start(self):
        for cp in self.copies:
            cp.start()

    def finish(self):
        for cp in self.copies:
            cp.wait_recv()
        for cp in self.copies:
            cp.wait_send()


def _shard_sum(p, b, qc_idx, name):
    _, rh, cols = p.shape
    tr = rh
    for cand in (256, 128, 64, 32, 16):
        if rh % cand == 0 and rh > cand:
            tr = cand
            break

    def body(qc_ref, p_ref, b_ref, o_ref):
        acc = p_ref[...].astype(F32)
        for j in range(3):
            acc = acc + b_ref[j].astype(F32)
        o_ref[...] = acc

    return pl.pallas_call(
        body, name=name,
        grid_spec=pltpu.PrefetchScalarGridSpec(
            num_scalar_prefetch=1, grid=(rh // tr,),
            in_specs=[pl.BlockSpec((None, tr, cols), lambda i, qc_ref: (qc_ref[0], i, 0)),
                      pl.BlockSpec((3, tr, cols), lambda i, qc_ref: (0, i, 0))],
            out_specs=pl.BlockSpec((None, tr, cols), lambda i, qc_ref: (qc_ref[1], i, 0))),
        out_shape=jax.ShapeDtypeStruct((2, rh, cols), F32),
        compiler_params=_params(("parallel",), 8 * _nbytes((tr, cols), F32)),
    )(qc_idx, p, b)


def _join_halves(shards):
    n = len(shards)

    def body(*refs):
        out_refs = refs[n:2 * n]
        send_sems, recv_sems = refs[2 * n:]
        x, y, c = _position()
        cps = [pltpu.make_async_remote_copy(src_ref=out_refs[k].at[c], dst_ref=out_refs[k].at[c], send_sem=send_sems.at[k],
                                            recv_sem=recv_sems.at[k], device_id=(x, y, 1 - c), device_id_type=MESH)
               for k in range(n)]
        for cp in cps:
            cp.start()
        for k in range(n):
            arriving = out_refs[k].at[1 - c]
            pltpu.make_async_remote_copy(src_ref=arriving, dst_ref=arriving, send_sem=send_sems.at[k], recv_sem=recv_sems.at[k],
                                         device_id=(x, y, 1 - c), device_id_type=MESH).wait_recv()
        for cp in cps:
            cp.wait_send()

    return pl.pallas_call(
        body, name="rs_join",
        out_shape=[jax.ShapeDtypeStruct(a.shape, a.dtype) for a in shards],
        in_specs=[ANY] * n, out_specs=[ANY] * n, input_output_aliases={k: k for k in range(n)},
        scratch_shapes=[pltpu.SemaphoreType.DMA((n,)), pltpu.SemaphoreType.DMA((n,))],
    )(*shards)


def _cols_from_shards(g):
    q, r, cs = g.shape
    return jnp.transpose(g, (1, 0, 2)).reshape(r, q * cs)


def _cols_to_shards(w):
    r, cfull = w.shape
    return jnp.transpose(w.reshape(r, N_CHIP, cfull // N_CHIP), (1, 0, 2))


def _pad_w_in(w):
    z = lambda n: jnp.zeros((w.shape[0], n), w.dtype)
    q_lat, kv_lat, kpe = w[:, 0:512], w[:, 512:768], w[:, 768:800]
    qd, kd, vd = w[:, 800:1312], w[:, 1312:1824], w[:, 1824:2336]
    return jnp.concatenate([q_lat, qd, kd, vd, kv_lat, z(KPE_OFF), kpe, z(LANE - KPE_OFF - ROPE)], axis=1)


def _pad_w_qb(w):
    w3 = w.reshape(Q_LORA, HEADS, NOPE + ROPE)
    return jnp.pad(w3, ((0, 0), (0, 0), (0, LANE - NOPE - ROPE))).reshape(Q_LORA, HEADS * LANE)


def _unpad_w_qb(g):
    return g.reshape(Q_LORA, HEADS, LANE)[:, :, :NOPE + ROPE].reshape(Q_LORA, HEADS * (NOPE + ROPE))


def _pad_w_kvb(w):
    w3 = w.reshape(KV_LORA, HEADS, 2 * NOPE)
    kp = jnp.pad(w3[:, :, :NOPE], ((0, 0), (0, 0), (0, LANE - NOPE))).reshape(KV_LORA, HEADS * LANE)
    return jnp.concatenate([kp, w3[:, :, NOPE:].reshape(KV_LORA, DIL_W)], axis=1)


def _unpad_w_kvb(g):
    gk = g[:, :HEADS * LANE].reshape(KV_LORA, HEADS, LANE)[:, :, :NOPE]
    gv = g[:, HEADS * LANE:].reshape(KV_LORA, HEADS, NOPE)
    return jnp.concatenate([gk, gv], axis=2).reshape(KV_LORA, HEADS * 2 * NOPE)


def _head_gains(g_q_nope, g_q_pe, g_k_nope, g_k_pe, g_dq, g_dk):
    z = lambda n: jnp.zeros((1, n), F32)
    q1 = jnp.concatenate([g_q_nope, g_q_pe, z(LANE - NOPE - ROPE)], axis=1)
    k1 = jnp.concatenate([g_k_nope, z(LANE - NOPE)], axis=1)
    kpe = jnp.concatenate([z(KPE_OFF), g_k_pe, z(LANE - KPE_OFF - ROPE)], axis=1)
    return dict(q=jnp.tile(q1, (1, HEADS)), k=jnp.tile(k1, (1, HEADS)), kpe=kpe,
                dq=jnp.tile(g_dq, (1, HEADS)), dk=jnp.tile(g_dk, (1, HEADS)))


def kernel(x, c, positions, w_ada, b_ada, g_mix_norm, w_in, g_q_lat, w_q_b, g_kv_lat, w_kv_b, g_mla_q_nope, g_mla_q_pe, g_mla_k_nope, g_mla_k_pe, g_dil_q, g_dil_k, w_o, g_ffn_norm, w_up, w_conv, b_conv, w_down, loss_target, m_w_ada, m_b_ada, m_g_mix_norm, m_w_in, m_g_q_lat, m_w_q_b, m_g_kv_lat, m_w_kv_b, m_g_mla_q_nope, m_g_mla_q_pe, m_g_mla_k_nope, m_g_mla_k_pe, m_g_dil_q, m_g_dil_k, m_w_o, m_g_ffn_norm, m_w_up, m_w_conv, m_b_conv, m_w_down, v_w_ada, v_b_ada, v_g_mix_norm, v_w_in, v_g_q_lat, v_w_q_b, v_g_kv_lat, v_w_kv_b, v_g_mla_q_nope, v_g_mla_q_pe, v_g_mla_k_nope, v_g_mla_k_pe, v_g_dil_q, v_g_dil_k, v_w_o, v_g_ffn_norm, v_w_up, v_w_conv, v_b_conv, v_w_down):
    args = dict(locals())
    weights = {n: args[n][0] for n in ("w_ada", "w_in", "w_q_b", "w_kv_b", "w_o", "w_up", "w_conv", "w_down")}
    small_w = {n: args[n] for n in ("b_ada",) + tuple(n for n, _ in SMALL_WIDTHS)}
    mom_m = {n[2:]: (args[n][0] if args[n].ndim == 3 else args[n]) for n in args if n.startswith("m_")}
    mom_v = {n[2:]: (args[n][0] if args[n].ndim == 3 else args[n]) for n in args if n.startswith("v_")}

    xi, yi, ci = _position()
    q0 = 2 * xi + yi
    me = 4 * xi + 2 * yi + ci
    xs, tgt = x[0], loss_target[0]
    s = xs.shape[0]
    consts = _seg_consts()
    c_idx, qc_idx = jnp.reshape(ci, (1,)).astype(I32), jnp.stack([q0, ci]).astype(I32)

    def halves(g4):
        q, r, cc = g4.shape
        return g4.reshape(q, 2, r // 2, cc)

    own_first = [weights[n].astype(BF16) for n in ("w_in", "w_q_b", "w_kv_b")]
    own_later = [weights[n].astype(BF16) for n in ("w_o", "w_up", "w_down")]
    conv_cols = UP_W // N_CHIP
    ada_cols = w_ada.shape[2]
    b_shard = lax.dynamic_slice_in_dim(b_ada, q0 * ada_cols, ada_cols, axis=1)
    c_taps = jnp.concatenate([c, weights["w_conv"].reshape(1, 3 * conv_cols)], axis=1)
    c_taps_all, mod_all, tab, *gathered = _prologue(c_taps, weights["w_ada"], b_shard, positions.reshape(s, 1),
                                                    _rope_consts(), own_first)
    c_all = c_taps_all[:, 0, :D_MODEL]
    w_conv_f = c_taps_all[:, 0, D_MODEL:].reshape(N_CHIP, 2, 3, conv_cols)[:, 0]
    w_conv_f = jnp.transpose(w_conv_f, (1, 0, 2)).reshape(3, UP_W)
    mod_all = mod_all.reshape(N_CHIP, 2, N_DEV, ada_cols)
    mod = lax.dynamic_index_in_dim(lax.dynamic_index_in_dim(mod_all, ci, 1, False), me, 1, False)
    mod = mod.reshape(1, N_CHIP * ada_cols)
    sh1, sc1, g1, sh2, sc2, g2 = [mod[:, k * D_MODEL:(k + 1) * D_MODEL] for k in range(6)]
    w_in_f = _cols_from_shards(gathered[0])
    w_in_p = _pad_w_in(w_in_f)
    w_qb_p = _pad_w_qb(_cols_from_shards(gathered[1]))
    w_kvb_p = _pad_w_kvb(_cols_from_shards(gathered[2]))
    gains = _head_gains(g_mla_q_nope, g_mla_q_pe, g_mla_k_nope, g_mla_k_pe, g_dil_q, g_dil_k)

    h, proj, ql, kvl = _in_proj(xs, g_mix_norm, sc1, sh1, w_in_p, g_q_lat, g_kv_lat)
    q_raw = _mm(ql, w_qb_p, "nn", F32, 1024, HEADS * LANE, "mm_qb")
    kv_raw = _mm(kvl, w_kvb_p, "nn", F32, 1024, HEADS * LANE + DIL_W, "mm_kvb")
    qm, km, vm, qd, kd, vd = _attn_prep(q_raw, kv_raw, proj, tab, gains, consts)
    scale_m, scale_d = (NOPE + ROPE) ** -0.5, DIL_DIM ** -0.5
    o_m, lse_m, got_up = _attn_fwd(qm, km, vm, True, scale_m, "attn_mla", gather=own_later[1:2])
    o_d, lse_d, got_o, got_down = _attn_fwd(qd, kd, vd, False, scale_d, "attn_dil", gather=[own_later[0], own_later[2]])
    gathered = [got_o, got_up, got_down]
    w_o_f = gathered[0].reshape(D_MODEL, D_MODEL)
    w_up_f = _cols_from_shards(gathered[1])
    w_down_f = gathered[2].reshape(D_FF, D_MODEL)
    mix_in = jnp.concatenate([o_m, o_d], axis=1)
    mix, x1, h2 = _o_resid_prenorm(mix_in, w_o_f, xs, g1, g_ffn_norm, sc2, sh2)
    up = _mm(h2, w_up_f, "nn", F32, 1024, CONV_TILE, "mm_up")
    act = _conv_gate(up, w_conv_f, b_conv)
    dy, dffn, dg2, loss_part = _down_final(act, w_down_f, x1, tgt, g2)

    da = _mm(dffn, w_down_f, "nt", F32, 1024, CONV_TILE, "mm_down_dx")
    gw_down = _mm(act, dffn, "tn", F32, 256, D_MODEL, "mm_down_dw")
    dup_g, dup_v, dbg, dbv, dwg, dwv = _gate_bwd(up, da, w_conv_f, b_conv)
    dup = jnp.concatenate([dup_g, dup_v], axis=1)
    early_names = ("w_up", "w_down", "w_o")
    gw_up = _mm(h2, dup, "tn", F32, 1024, CONV_TILE, "mm_up_dw", col_shards=True)
    early = [halves(gw_up), halves(gw_down.reshape(N_CHIP, D_FF // N_CHIP, D_MODEL))]
    dh2, *early_sib = _mm(dup, w_up_f, "nt", F32, 256, 512, "mm_up_dx", swap=early, b_outer=True)
    dx1, dmix, acc2 = _ffnnorm_bwd(dh2, x1, dy, mix, g_ffn_norm, sc2, g1)
    gw_o = _mm(mix_in, dmix, "tn", F32, 1024, D_MODEL, "mm_o_dw")
    early.append(halves(gw_o.reshape(N_CHIP, D_MODEL // N_CHIP, D_MODEL)))
    dmix_in, sib_o = _mm(dmix, w_o_f, "nt", F32, 512, D_MODEL, "mm_o_dx", swap=early[2:])
    early_sib.append(sib_o)
    early_sums = [_pair_sum(g, a, c_idx, "pair_sum_" + n) for g, a, n in zip(early, early_sib, early_names)]
    dqm, dkm, dvm, *early_recv = _attn_bwd(qm, km, vm, o_m, dmix_in, 0, lse_m, True, scale_m, "attn_mla_bwd",
                                           scatter=early_sums[:1])
    dqd, dkd, dvd, *early_recv_d = _attn_bwd(qd, kd, vd, o_d, dmix_in, DIL_W // LANE, lse_d, False, scale_d,
                                             "attn_dil_bwd", scatter=early_sums[1:])
    early_recv = early_recv + early_recv_d
    dq_raw, dkv_raw, dkpe_b, dqd_b, dkd_b, dvd_b, dgains = _attn_prep_bwd(
        dqm, dkm, dvm, dqd, dkd, dvd, q_raw, kv_raw, proj, tab, gains, consts)
    dql = _mm(dq_raw, w_qb_p, "nt", F32, 1024, Q_LORA, "mm_qb_dx")
    gw_qb = _unpad_w_qb(_mm(ql, dq_raw, "tn", F32, Q_LORA, HEADS * LANE, "mm_qb_dw"))
    dkvl = _mm(dkv_raw, w_kvb_p, "nt", F32, 1024, KV_LORA, "mm_kvb_dx")
    gw_kvb = _unpad_w_kvb(_mm(kvl, dkv_raw, "tn", F32, KV_LORA, HEADS * LANE + DIL_W, "mm_kvb_dw"))
    dqlat_b, dkvlat_b, dglat = _latnorm_bwd(dql, dkvl, proj, g_q_lat, g_kv_lat)
    dproj = jnp.concatenate([dqlat_b, dkvlat_b, dkpe_b[:, KPE_OFF:KPE_OFF + ROPE], dqd_b, dkd_b, dvd_b], axis=1)
    gw_in = _mm(h, dproj, "tn", F32, 512, IN_COLS, "mm_in_dw")
    late_names = ("w_in", "w_q_b", "w_kv_b")
    late = [halves(_cols_to_shards(gw_in)), halves(_cols_to_shards(gw_qb)), halves(_cols_to_shards(gw_kvb))]
    dh, *late_sib = _mm(dproj, w_in_f, "nt", F32, 512, D_MODEL, "mm_in_dx", swap=late)
    grad_x, acc1 = _mixnorm_bwd(dh, xs, dx1, g_mix_norm, sc1)

    packed = _pack_small(acc1, acc2, dg2, dglat, dgains, dbg, dbv, dwg, dwv, loss_part)
    late_sums = [_pair_sum(g, a, c_idx, "pair_sum_" + n) for g, a, n in zip(late, late_sib, late_names)]
    *late_recv, gathered_small = _scatter_and_gather(late_sums, packed, "rs_scatter_late")

    grad_b_ada, *small_grads, gconv_full, loss_sum = _sum_unpack(gathered_small)
    grads = {"b_ada": grad_b_ada}
    grads.update({n: g for (n, _), g in zip(SMALL_WIDTHS, small_grads)})
    shard_cols = UP_W // N_CHIP
    grads["w_conv"] = lax.dynamic_slice_in_dim(gconv_full, q0 * shard_cols, shard_cols, axis=1)
    dmod_all = gathered_small[:, 0, :6 * D_MODEL]
    grads["w_ada"] = _ada_bwd(c_all, lax.dynamic_slice_in_dim(dmod_all, q0 * ada_cols, ada_cols, axis=1))

    big_names = late_names + early_names
    half_sums = [_shard_sum(p, b, qc_idx, "shard_sum_" + n)
                 for p, b, n in zip(late_sums + early_sums, list(late_recv) + list(early_recv), big_names)]
    for n, full in zip(big_names, _join_halves(half_sums)):
        grads[n] = full.reshape(2 * full.shape[1], full.shape[2])

    delta, new_m, new_v = {}, {}, {}
    for n in ("w_ada", "w_in", "w_q_b", "w_kv_b", "w_o", "w_up", "w_conv", "w_down"):
        operands = (weights[n], grads[n], mom_m[n], mom_v[n])
        flipped = n in ("w_in", "w_q_b")
        if flipped:
            operands = [jnp.swapaxes(a, 0, 1) for a in operands]
            grads[n] = jnp.swapaxes(operands[1], 0, 1)
        if n == "w_ada":
            operands = _in_hbm(*operands)
        delta[n], new_m[n], new_v[n] = _adamw(*operands, "adamw_" + n)
        if flipped:
            delta[n], new_m[n], new_v[n] = (jnp.swapaxes(a, 0, 1) for a in (delta[n], new_m[n], new_v[n]))
    vec_names = ("b_ada",) + tuple(n for n, _ in SMALL_WIDTHS)
    sd, sm, sv = _adamw_vectors(*[[d_[n] for n in vec_names] for d_ in (small_w, grads, mom_m, mom_v)])
    for k, n in enumerate(vec_names):
        delta[n], new_m[n], new_v[n] = sd[k], sm[k], sv[k]

    loss = loss_sum[0, 0]
    order = ("w_ada", "b_ada", "g_mix_norm", "w_in", "g_q_lat", "w_q_b", "g_kv_lat", "w_kv_b", "g_mla_q_nope", "g_mla_q_pe",
             "g_mla_k_nope", "g_mla_k_pe", "g_dil_q", "g_dil_k", "w_o", "g_ffn_norm", "w_up", "w_conv", "b_conv", "w_down")
    lead = lambda n, z: z[None] if n.startswith("w_") else z
    outs = [loss, grad_x[None]]
    for d_ in (grads, delta, new_m, new_v):
        outs += [lead(n, d_[n]) for n in order]
    return tuple(outs)
```

```python
import functools

import numpy as np
import jax
import jax.numpy as jnp
from jax import lax
from jax.experimental import pallas as pl
from jax.experimental.pallas import tpu as pltpu

F32 = jnp.float32
BF16 = jnp.bfloat16
I32 = jnp.int32

D_MODEL = 1024
HEADS = 8
NOPE = 64
ROPE = 32
Q_LORA = 512
KV_LORA = 256
DIL_DIM = 64
DIL_W = HEADS * DIL_DIM
D_FF = 2816
UP_W = 2 * D_FF
IN_COLS = Q_LORA + KV_LORA + ROPE + 3 * DIL_W
ROPE_THETA = 10000.0
EPS = 1e-6
NEG_INF = -1e30
N_DEV = 8
N_CHIP = 4

ADAM_LR = 0.001
ADAM_B1 = 0.9
ADAM_B2 = 0.999
ADAM_EPS = 1e-08
ADAM_WD = 0.01
ADAM_STEP = 10

LANE = 128
ROW_TILE = 256
NORM_TILE = 512
ATT_TQ = 512
ATT_TK = 256
ATT_TK_BWD = 512
LOG2E = 1.4426950408889634
LN2 = 0.6931471805599453
VMEM_CAP = 56 * 1024 * 1024
VMEM_FLOOR = 32 * 1024 * 1024

P_QLAT, P_QD, P_KD, P_VD, P_KVLAT, P_KPE = 0, 512, 1024, 1536, 2048, 2304
P_COLS = 2432
KPE_OFF = 64

NN = (((1,), (0,)), ((), ()))
NT = (((1,), (1,)), ((), ()))
TN = (((0,), (0,)), ((), ()))
HIGHEST = lax.Precision.HIGHEST
MESH = pl.DeviceIdType.MESH


def _params(sem=None, est_bytes=0):
    limit = int(min(max(2 * est_bytes + (4 << 20), VMEM_FLOOR), VMEM_CAP))
    if sem is None:
        return pltpu.CompilerParams(vmem_limit_bytes=limit)
    return pltpu.CompilerParams(dimension_semantics=sem, vmem_limit_bytes=limit)


def _nbytes(shape, dtype):
    return int(np.prod(shape)) * jnp.dtype(dtype).itemsize


def _in_hbm(*xs):
    return [pltpu.with_memory_space_constraint(x, pltpu.HBM) for x in xs]


def _mm(a, b, dims, out_dtype, tm, tn, name, col_shards=False, swap=(), b_outer=False):
    def spec(block, index):
        if b_outer:
            return pl.BlockSpec(block, lambda g0, g1: index(g1, g0))
        return pl.BlockSpec(block, index)

    if dims == "nn":
        (m, k), (k2, n) = a.shape, b.shape
        a_spec = spec((tm, k), lambda i, j: (i, 0))
        b_spec = spec((k, tn), lambda i, j: (0, j))
        dn = NN
    elif dims == "nt":
        (m, k), (n, k2) = a.shape, b.shape
        a_spec = spec((tm, k), lambda i, j: (i, 0))
        b_spec = spec((tn, k), lambda i, j: (j, 0))
        dn = NT
    else:
        (k, m), (k2, n) = a.shape, b.shape
        a_spec = spec((k, tm), lambda i, j: (0, i))
        b_spec = spec((k, tn), lambda i, j: (0, j))
        dn = TN
    assert k == k2 and m % tm == 0 and n % tn == 0, (name, a.shape, b.shape, tm, tn)

    nw = len(swap)
    grid = (n // tn, m // tm) if b_outer else (m // tm, n // tn)

    def body(*refs):
        a_ref, b_ref, o_ref = refs[0], refs[1], refs[2 + nw]
        comm = (refs[2:2 + nw], refs[3 + nw:3 + 2 * nw]) + tuple(refs[3 + 2 * nw:])
        if nw:
            @pl.when((pl.program_id(0) == 0) & (pl.program_id(1) == 0))
            def _():
                _PairSwap(*comm).start()

        o_ref[...] = lax.dot_general(a_ref[...], b_ref[...], dn, preferred_element_type=F32).astype(o_ref.dtype)

        if nw:
            @pl.when((pl.program_id(0) == grid[0] - 1) & (pl.program_id(1) == grid[1] - 1))
            def _():
                _PairSwap(*comm).finish()

    est = _nbytes((tm, k), a.dtype) + _nbytes((tn, k), b.dtype) + _nbytes((tm, tn), F32) + _nbytes((tm, tn), out_dtype)
    if col_shards:
        out_spec = spec((None, tm, tn), lambda i, j: (j, i, 0))
        out_shape = jax.ShapeDtypeStruct((n // tn, m, tn), out_dtype)
    else:
        out_spec = spec((tm, tn), lambda i, j: (i, j))
        out_shape = jax.ShapeDtypeStruct((m, n), out_dtype)
    out = pl.pallas_call(
        body, name=name, grid=grid,
        in_specs=[a_spec, b_spec] + [ANY] * nw,
        out_specs=[out_spec] + [ANY] * nw,
        out_shape=[out_shape] + _PairSwap.out_shapes(swap),
        scratch_shapes=_PairSwap.semaphores(nw) if nw else [],
        compiler_params=_params(("arbitrary", "arbitrary") if nw else ("parallel", "parallel"), est),
    )(a, b, *swap)
    return out if nw else out[0]


def _seg_consts():
    seg_q = np.zeros((HEADS * LANE, LANE), np.float32)
    inv_q = np.zeros((1, LANE), np.float32)
    seg_k = np.zeros((HEADS * LANE, LANE), np.float32)
    inv_k = np.zeros((1, LANE), np.float32)
    seg_d = np.zeros((DIL_W, LANE), np.float32)
    inv_d = np.zeros((1, LANE), np.float32)
    for h in range(HEADS):
        seg_q[h * LANE:h * LANE + NOPE, 2 * h] = 1.0
        seg_q[h * LANE + NOPE:h * LANE + NOPE + ROPE, 2 * h + 1] = 1.0
        inv_q[0, 2 * h], inv_q[0, 2 * h + 1] = 1.0 / NOPE, 1.0 / ROPE
        seg_k[h * LANE:h * LANE + NOPE, h] = 1.0
        inv_k[0, h] = 1.0 / NOPE
        seg_d[h * DIL_DIM:(h + 1) * DIL_DIM, h] = 1.0
        inv_d[0, h] = 1.0 / DIL_DIM
    fold_q = np.tile(np.eye(LANE, dtype=np.float32), (HEADS, 1))
    fold_d = np.zeros((DIL_W, LANE), np.float32)
    fold_d[np.arange(DIL_W), np.arange(DIL_W) % DIL_DIM] = 1.0
    j = lambda v: jnp.asarray(v)
    b = lambda v: jnp.asarray(v, dtype=BF16)
    return dict(seg_q=b(seg_q), exp_q=b(seg_q.T.copy()), inv_q=j(inv_q), seg_k=b(seg_k), exp_k=b(seg_k.T.copy()),
                inv_k=j(inv_k), seg_d=b(seg_d), exp_d=b(seg_d.T.copy()), inv_d=j(inv_d), fold_q=j(fold_q), fold_d=j(fold_d))


def _rope_consts():
    inv_d = jnp.power(ROPE_THETA, -2.0 * jnp.arange(DIL_DIM // 2, dtype=F32) / DIL_DIM)
    inv_q = jnp.power(ROPE_THETA, -2.0 * jnp.arange(ROPE // 2, dtype=F32) / ROPE)
    lanes = np.arange(LANE)
    freq_d = inv_d[lanes % (DIL_DIM // 2)]
    in_pe = (lanes >= KPE_OFF) & (lanes < KPE_OFF + ROPE)
    freq_q = jnp.where(jnp.asarray(in_pe), inv_q[(lanes - KPE_OFF) % (ROPE // 2)], 0.0)
    sign_d = np.where(lanes % DIL_DIM < DIL_DIM // 2, -1.0, 1.0).astype(np.float32)
    sign_q = np.where(in_pe, np.where((lanes - KPE_OFF) < ROPE // 2, -1.0, 1.0), 0.0).astype(np.float32)
    zeros, ones = np.zeros(LANE, np.float32), np.ones(LANE, np.float32)
    freq = jnp.concatenate([freq_d, freq_d, freq_q, freq_q])[None, :]
    csel = jnp.asarray(np.concatenate([ones, zeros, ones, zeros]))[None, :]
    ssel = jnp.asarray(np.concatenate([zeros, sign_d, zeros, sign_q]))[None, :]
    return freq, csel, ssel


def _full(shape):
    return pl.BlockSpec(shape, lambda *_: (0,) * len(shape))


def _tile_lanes(x, n):
    return jnp.concatenate([x] * n, axis=1)


def _rms(x):
    return lax.rsqrt(jnp.mean(x * x, axis=-1, keepdims=True) + EPS)


def _in_proj(x, gain, scale, shift, w_in, g_q, g_kv):
    s, d = x.shape
    cols = w_in.shape[1]

    def body(x_ref, g_ref, sc_ref, sh_ref, w_ref, gq_ref, gkv_ref, h_ref, p_ref, ql_ref, kvl_ref):
        xv = x_ref[...]
        h = ((xv * _rms(xv)) * g_ref[...] * (1.0 + sc_ref[...]) + sh_ref[...]).astype(BF16)
        h_ref[...] = h
        p_ref[...] = jnp.dot(h, w_ref[...], preferred_element_type=F32)
        q = p_ref[:, P_QLAT:P_QLAT + Q_LORA]
        kv = p_ref[:, P_KVLAT:P_KVLAT + KV_LORA]
        ql_ref[...] = ((q * _rms(q)) * gq_ref[...]).astype(BF16)
        kvl_ref[...] = ((kv * _rms(kv)) * gkv_ref[...]).astype(BF16)

    def rows(c):
        return pl.BlockSpec((NORM_TILE, c), lambda i: (i, 0))

    vec = _full((1, d))
    est = (_nbytes((NORM_TILE, d), F32) + _nbytes((d, cols), BF16) + 2 * _nbytes((NORM_TILE, cols), F32)
           + _nbytes((NORM_TILE, d), F32))
    return pl.pallas_call(
        body, name="prenorm_mm_in_latnorm", grid=(s // NORM_TILE,),
        in_specs=[rows(d), vec, vec, vec, _full((d, cols)), _full((1, Q_LORA)), _full((1, KV_LORA))],
        out_specs=[rows(d), rows(cols), rows(Q_LORA), rows(KV_LORA)],
        out_shape=[jax.ShapeDtypeStruct((s, d), BF16), jax.ShapeDtypeStruct((s, cols), F32),
                   jax.ShapeDtypeStruct((s, Q_LORA), BF16), jax.ShapeDtypeStruct((s, KV_LORA), BF16)],
        compiler_params=_params(("parallel",), est),
    )(x, gain, scale, shift, w_in, g_q, g_kv)


def _dot01(v, mat01):
    hi = v.astype(BF16)
    lo = (v - hi.astype(F32)).astype(BF16)
    return jnp.dot(hi, mat01, preferred_element_type=F32) + jnp.dot(lo, mat01, preferred_element_type=F32)


def _seg_rinv(x, seg, exp, inv):
    r = lax.rsqrt(_dot01(x * x, seg) * inv + EPS)
    return _dot01(r, exp)


def _seg_mean(v, seg, exp, inv):
    return _dot01(_dot01(v, seg) * inv, exp)


def _swap_halves(x, half):
    n = x.shape[1]
    lane = lax.broadcasted_iota(I32, (1, n), 1)
    first = (lane & (2 * half - 1)) < half
    return jnp.where(first, pltpu.roll(x, n - half, 1), pltpu.roll(x, half, 1))


def _rope(x, cos, sin_signed, half):
    return x * cos + _swap_halves(x, half) * sin_signed


def _rope_bwd(dy, cos, sin_signed, half):
    return dy * cos + _swap_halves(dy * sin_signed, half)


def _pe_lane_mask(n):
    lane = lax.broadcasted_iota(I32, (1, n), 1) & (LANE - 1)
    return (lane >= KPE_OFF) & (lane < KPE_OFF + ROPE)


def _attn_prep(q_raw, kv_raw, proj, tab, gains, consts):
    s = q_raw.shape[0]
    hw = HEADS * LANE

    def body(q_ref, kv_ref, kpe_ref, qd_ref, kd_ref, vd_ref, tab_ref,
             gq_ref, gk_ref, gkpe_ref, gdq_ref, gdk_ref,
             segq_ref, expq_ref, invq_ref, segk_ref, expk_ref, invk_ref, segd_ref, expd_ref, invd_ref,
             qm_ref, km_ref, vm_ref, qdo_ref, kdo_ref, vdo_ref):
        tab_v = tab_ref[...]
        cos_d, sin_d = _tile_lanes(tab_v[:, 0:LANE], DIL_W // LANE), _tile_lanes(tab_v[:, LANE:2 * LANE], DIL_W // LANE)
        cos_q1, sin_q1 = tab_v[:, 2 * LANE:3 * LANE], tab_v[:, 3 * LANE:4 * LANE]
        cos_q, sin_q = _tile_lanes(cos_q1, HEADS), _tile_lanes(sin_q1, HEADS)

        q = q_ref[...]
        qn = q * _seg_rinv(q, segq_ref[...], expq_ref[...], invq_ref[...]) * gq_ref[...]
        qm_ref[...] = _rope(qn, cos_q, sin_q, ROPE // 2).astype(BF16)

        kv = kv_ref[...]
        kp = kv[:, :hw]
        kn = kp * _seg_rinv(kp, segk_ref[...], expk_ref[...], invk_ref[...]) * gk_ref[...]
        kpe = kpe_ref[...]
        r_pe = lax.rsqrt(jnp.sum(kpe * kpe, axis=-1, keepdims=True) * (1.0 / ROPE) + EPS)
        kpe_r = _rope(kpe * r_pe * gkpe_ref[...], cos_q1, sin_q1, ROPE // 2)
        km_ref[...] = (kn + _tile_lanes(kpe_r, HEADS)).astype(BF16)
        vm_ref[...] = kv[:, hw:].astype(BF16)

        qd = qd_ref[...]
        qdn = qd * _seg_rinv(qd, segd_ref[...], expd_ref[...], invd_ref[...]) * gdq_ref[...]
        qdo_ref[...] = _rope(qdn, cos_d, sin_d, DIL_DIM // 2).astype(BF16)
        kd = kd_ref[...]
        kdn = kd * _seg_rinv(kd, segd_ref[...], expd_ref[...], invd_ref[...]) * gdk_ref[...]
        kdo_ref[...] = _rope(kdn, cos_d, sin_d, DIL_DIM // 2).astype(BF16)
        vdo_ref[...] = vd_ref[...].astype(BF16)

    t = ROW_TILE
    row = lambda w, cb=0: pl.BlockSpec((t, w), lambda i: (i, cb))
    c = consts
    return pl.pallas_call(
        body, name="attn_prep", grid=(s // t,),
        in_specs=[row(hw), row(hw + DIL_W), row(LANE, P_KPE // LANE), row(DIL_W, P_QD // DIL_W), row(DIL_W, P_KD // DIL_W),
                  row(DIL_W, P_VD // DIL_W), row(4 * LANE),
                  _full((1, hw)), _full((1, hw)), _full((1, LANE)), _full((1, DIL_W)), _full((1, DIL_W)),
                  _full((hw, LANE)), _full((LANE, hw)), _full((1, LANE)), _full((hw, LANE)), _full((LANE, hw)), _full((1, LANE)),
                  _full((DIL_W, LANE)), _full((LANE, DIL_W)), _full((1, LANE))],
        out_specs=[row(hw), row(hw), row(DIL_W), row(DIL_W), row(DIL_W), row(DIL_W)],
        out_shape=[jax.ShapeDtypeStruct((s, hw), BF16), jax.ShapeDtypeStruct((s, hw), BF16)]
        + [jax.ShapeDtypeStruct((s, DIL_W), BF16)] * 4,
        compiler_params=_params(("parallel",), 24 << 20),
    )(*_in_hbm(q_raw, kv_raw, proj, proj, proj, proj), tab, gains["q"], gains["k"], gains["kpe"], gains["dq"], gains["dk"],
      c["seg_q"], c["exp_q"], c["inv_q"], c["seg_k"], c["exp_k"], c["inv_k"], c["seg_d"], c["exp_d"], c["inv_d"])


def _attn_prep_bwd(dqm, dkm, dvm, dqd, dkd, dvd, q_raw, kv_raw, proj, tab, gains, consts):
    s = q_raw.shape[0]
    hw = HEADS * LANE
    n_steps = s // ROW_TILE

    def body(dqm_ref, dkm_ref, dvm_ref, dqd_ref, dkd_ref, dvd_ref, q_ref, kv_ref, kpe_ref, qd_ref, kd_ref, tab_ref,
             gq_ref, gk_ref, gkpe_ref, gdq_ref, gdk_ref,
             segq_ref, expq_ref, invq_ref, segk_ref, expk_ref, invk_ref, segd_ref, expd_ref, invd_ref, foldq_ref, foldd_ref,
             dq_ref, dkv_ref, dkpe_ref, dqdo_ref, dkdo_ref, dvdo_ref, dg_ref, acc_ref):
        i = pl.program_id(0)

        @pl.when(i == 0)
        def _():
            acc_ref[...] = jnp.zeros_like(acc_ref)

        tab_v = tab_ref[...]
        cos_d, sin_d = _tile_lanes(tab_v[:, 0:LANE], DIL_W // LANE), _tile_lanes(tab_v[:, LANE:2 * LANE], DIL_W // LANE)
        cos_q1, sin_q1 = tab_v[:, 2 * LANE:3 * LANE], tab_v[:, 3 * LANE:4 * LANE]
        cos_q, sin_q = _tile_lanes(cos_q1, HEADS), _tile_lanes(sin_q1, HEADS)

        def norm_bwd(x, dyg, gain, seg, exp, inv):
            rinv = _seg_rinv(x, seg, exp, inv)
            xn = x * rinv
            dxn = dyg * gain
            dx = rinv * (dxn - xn * _seg_mean(dxn * xn, seg, exp, inv))
            return dx, jnp.sum(dyg * xn, axis=0, keepdims=True)

        dq, gq_l = norm_bwd(q_ref[...], _rope_bwd(dqm_ref[...], cos_q, sin_q, ROPE // 2), gq_ref[...],
                            segq_ref[...], expq_ref[...], invq_ref[...])
        dq_ref[...] = dq.astype(BF16)

        dkm = dkm_ref[...]
        kv = kv_ref[...]
        dkp, gk_l = norm_bwd(kv[:, :hw], dkm, gk_ref[...], segk_ref[...], expk_ref[...], invk_ref[...])
        dkv_ref[:, :hw] = dkp.astype(BF16)
        dkv_ref[:, hw:] = dvm_ref[...].astype(BF16)

        dkpe_r = dkm[:, 0:LANE]
        for h in range(1, HEADS):
            dkpe_r = dkpe_r + dkm[:, h * LANE:(h + 1) * LANE]
        dkpe_r = jnp.where(_pe_lane_mask(LANE), dkpe_r, 0.0)
        dyg = _rope_bwd(dkpe_r, cos_q1, sin_q1, ROPE // 2)
        kpe = kpe_ref[...]
        r_pe = lax.rsqrt(jnp.sum(kpe * kpe, axis=-1, keepdims=True) * (1.0 / ROPE) + EPS)
        xn = kpe * r_pe
        dxn = dyg * gkpe_ref[...]
        dkpe = r_pe * (dxn - xn * (jnp.sum(dxn * xn, axis=-1, keepdims=True) * (1.0 / ROPE)))
        dkpe_ref[...] = dkpe.astype(BF16)
        gkpe_l = jnp.sum(dyg * xn, axis=0, keepdims=True)

        dqd_v, gdq_l = norm_bwd(qd_ref[...], _rope_bwd(dqd_ref[...], cos_d, sin_d, DIL_DIM // 2), gdq_ref[...],
                                segd_ref[...], expd_ref[...], invd_ref[...])
        dqdo_ref[...] = dqd_v.astype(BF16)
        dkd_v, gdk_l = norm_bwd(kd_ref[...], _rope_bwd(dkd_ref[...], cos_d, sin_d, DIL_DIM // 2), gdk_ref[...],
                                segd_ref[...], expd_ref[...], invd_ref[...])
        dkdo_ref[...] = dkd_v.astype(BF16)
        dvdo_ref[...] = dvd_ref[...].astype(BF16)

        acc_ref[0:1, :] += gq_l
        acc_ref[1:2, :] += gk_l
        acc_ref[2:3, 0:LANE] += gkpe_l
        acc_ref[3:4, 0:DIL_W] += gdq_l
        acc_ref[4:5, 0:DIL_W] += gdk_l

        @pl.when(i == n_steps - 1)
        def _():
            acc = acc_ref[...]
            fq = jnp.dot(acc, foldq_ref[...], precision=HIGHEST, preferred_element_type=F32)
            fd = jnp.dot(acc[:, 0:DIL_W], foldd_ref[...], precision=HIGHEST, preferred_element_type=F32)
            rows = lax.broadcasted_iota(I32, (8, LANE), 0)
            base = jnp.where(rows < 2, fq, jnp.where(rows == 2, acc[:, 0:LANE], fd))
            at0 = pltpu.roll(base, LANE - KPE_OFF, 1)
            dg_ref[...] = jnp.where(rows == 5, pltpu.roll(at0, 5, 0), jnp.where(rows == 2, at0, base))

    t = ROW_TILE
    row = lambda w, cb=0: pl.BlockSpec((t, w), lambda i: (i, cb))
    c = consts
    return pl.pallas_call(
        body, name="attn_prep_bwd", grid=(n_steps,),
        in_specs=[row(hw), row(hw), row(DIL_W), row(DIL_W), row(DIL_W), row(DIL_W),
                  row(hw), row(hw + DIL_W), row(LANE, P_KPE // LANE), row(DIL_W, P_QD // DIL_W), row(DIL_W, P_KD // DIL_W),
                  row(4 * LANE),
                  _full((1, hw)), _full((1, hw)), _full((1, LANE)), _full((1, DIL_W)), _full((1, DIL_W)),
                  _full((hw, LANE)), _full((LANE, hw)), _full((1, LANE)), _full((hw, LANE)), _full((LANE, hw)), _full((1, LANE)),
                  _full((DIL_W, LANE)), _full((LANE, DIL_W)), _full((1, LANE)), _full((hw, LANE)), _full((DIL_W, LANE))],
        out_specs=[row(hw), row(hw + DIL_W), row(LANE), row(DIL_W), row(DIL_W), row(DIL_W), _full((8, LANE))],
        out_shape=[jax.ShapeDtypeStruct((s, hw), BF16), jax.ShapeDtypeStruct((s, hw + DIL_W), BF16),
                   jax.ShapeDtypeStruct((s, LANE), BF16)] + [jax.ShapeDtypeStruct((s, DIL_W), BF16)] * 3
        + [jax.ShapeDtypeStruct((8, LANE), F32)],
        scratch_shapes=[pltpu.VMEM((8, hw), F32)],
        compiler_params=_params(("arbitrary",), 28 << 20),
    )(*_in_hbm(dqm, dkm, dvm, dqd, dkd, dvd, q_raw, kv_raw, proj, proj, proj), tab,
      gains["q"], gains["k"], gains["kpe"], gains["dq"], gains["dk"],
      c["seg_q"], c["exp_q"], c["inv_q"], c["seg_k"], c["exp_k"], c["inv_k"], c["seg_d"], c["exp_d"], c["inv_d"],
      c["fold_q"], c["fold_d"])


def _latnorm_bwd(dq_raw, dkv_raw, w_qb, w_kvb, proj, g_q, g_kv):
    s = proj.shape[0]
    n_steps = s // NORM_TILE
    kq, kkv = dq_raw.shape[1], dkv_raw.shape[1]

    def body(dqr_ref, dkvr_ref, wq_ref, wkv_ref, q_ref, kv_ref, gq_ref, gkv_ref, dq_ref, dkv_ref, dg_ref):
        i = pl.program_id(0)
        dql = lax.dot_general(dqr_ref[...], wq_ref[...], NT, preferred_element_type=F32)
        dkvl = lax.dot_general(dkvr_ref[...], wkv_ref[...], NT, preferred_element_type=F32)

        @pl.when(i == 0)
        def _():
            dg_ref[...] = jnp.zeros_like(dg_ref)

        def one(x, dyg, gain):
            r = _rms(x)
            xn = x * r
            dxn = dyg * gain
            dx = r * (dxn - xn * jnp.mean(dxn * xn, axis=-1, keepdims=True))
            return dx, jnp.sum(dyg * xn, axis=0, keepdims=True)

        dq, gq_l = one(q_ref[...], dql, gq_ref[...])
        dkv, gkv_l = one(kv_ref[...], dkvl, gkv_ref[...])
        dq_ref[...] = dq.astype(BF16)
        dkv_ref[...] = dkv.astype(BF16)
        dg_ref[0:1, :] += gq_l
        dg_ref[1:2, 0:KV_LORA] += gkv_l

    t = NORM_TILE
    return pl.pallas_call(
        body, name="mm_qb_kvb_dx_latnorm_bwd", grid=(n_steps,),
        in_specs=[pl.BlockSpec((t, kq), lambda i: (i, 0)), pl.BlockSpec((t, kkv), lambda i: (i, 0)),
                  _full((Q_LORA, kq)), _full((KV_LORA, kkv)),
                  pl.BlockSpec((t, Q_LORA), lambda i: (i, P_QLAT // Q_LORA)),
                  pl.BlockSpec((t, KV_LORA), lambda i: (i, P_KVLAT // KV_LORA)),
                  _full((1, Q_LORA)), _full((1, KV_LORA))],
        out_specs=[pl.BlockSpec((t, Q_LORA), lambda i: (i, 0)), pl.BlockSpec((t, KV_LORA), lambda i: (i, 0)), _full((8, Q_LORA))],
        out_shape=[jax.ShapeDtypeStruct((s, Q_LORA), BF16), jax.ShapeDtypeStruct((s, KV_LORA), BF16),
                   jax.ShapeDtypeStruct((8, Q_LORA), F32)],
        compiler_params=_params(("arbitrary",)),
    )(dq_raw, dkv_raw, w_qb, w_kvb, proj, proj, g_q, g_kv)


def _o_resid_prenorm(mix_in, w_o, x, g1, gain, scale, shift):
    s, d = x.shape
    k = mix_in.shape[1]

    def body(a_ref, w_ref, x_ref, g1_ref, g_ref, sc_ref, sh_ref, mix_ref, x1_ref, h_ref):
        mix = jnp.dot(a_ref[...], w_ref[...], preferred_element_type=F32)
        mix_ref[...] = mix
        x1 = x_ref[...] + g1_ref[...] * mix
        x1_ref[...] = x1
        h_ref[...] = ((x1 * _rms(x1)) * g_ref[...] * (1.0 + sc_ref[...]) + sh_ref[...]).astype(BF16)

    row = pl.BlockSpec((NORM_TILE, d), lambda i: (i, 0))
    vec = _full((1, d))
    est = _nbytes((NORM_TILE, k), BF16) + _nbytes((k, d), BF16) + 4 * _nbytes((NORM_TILE, d), F32)
    return pl.pallas_call(
        body, name="mm_o_resid_prenorm", grid=(s // NORM_TILE,),
        in_specs=[pl.BlockSpec((NORM_TILE, k), lambda i: (i, 0)), _full((k, d)), row, vec, vec, vec, vec],
        out_specs=[row, row, row],
        out_shape=[jax.ShapeDtypeStruct((s, d), F32), jax.ShapeDtypeStruct((s, d), F32),
                   jax.ShapeDtypeStruct((s, d), BF16)],
        compiler_params=_params(("parallel",), est),
    )(mix_in, w_o, x, g1, gain, scale, shift)


CONV_TILE = 1408
HALO = 8


def _shift_down(x, halo, k):
    t = x.shape[0]
    row = lax.broadcasted_iota(I32, (t, 1), 0)
    out = pltpu.roll(x, k, 0)
    for r in range(k):
        out = jnp.where(row == r, halo[HALO - k + r:HALO - k + r + 1, :], out)
    return out


def _shift_up(x, halo, k):
    t = x.shape[0]
    row = lax.broadcasted_iota(I32, (t, 1), 0)
    out = pltpu.roll(x, t - k, 0)
    for r in range(k):
        out = jnp.where(row == t - k + r, halo[r:r + 1, :], out)
    return out


def _conv_fwd(x, halo, w, b):
    p1, p2 = _shift_down(x, halo, 1), _shift_down(x, halo, 2)
    u = b + p2 * w[0:1, :]
    u = u + p1 * w[1:2, :]
    u = u + x * w[2:3, :]
    return u, p1, p2


def _sigmoid(x):
    return 0.5 * jnp.tanh(0.5 * x) + 0.5


def _conv_gate(up, w_conv, b_conv):
    s = up.shape[0]
    t = ROW_TILE
    nj = D_FF // CONV_TILE
    hb = t // HALO

    def body(g_ref, v_ref, gh_ref, vh_ref, wg_ref, wv_ref, bg_ref, bv_ref, a_ref):
        live = (pl.program_id(0) > 0).astype(F32)
        ug, _, _ = _conv_fwd(g_ref[...], gh_ref[...] * live, wg_ref[...], bg_ref[...])
        uv, _, _ = _conv_fwd(v_ref[...], vh_ref[...] * live, wv_ref[...], bv_ref[...])
        a_ref[...] = (ug * _sigmoid(ug) * uv).astype(BF16)

    main = lambda off: pl.BlockSpec((t, CONV_TILE), lambda i, j: (i, j + off))
    halo = lambda off: pl.BlockSpec((HALO, CONV_TILE), lambda i, j: (jnp.maximum(i * hb - 1, 0), j + off))
    wsp = lambda off: pl.BlockSpec((3, CONV_TILE), lambda i, j: (0, j + off))
    bsp = lambda off: pl.BlockSpec((1, CONV_TILE), lambda i, j: (0, j + off))
    return pl.pallas_call(
        body, name="conv_gate", grid=(s // t, nj),
        in_specs=[main(0), main(nj), halo(0), halo(nj), wsp(0), wsp(nj), bsp(0), bsp(nj)],
        out_specs=pl.BlockSpec((t, CONV_TILE), lambda i, j: (i, j)),
        out_shape=jax.ShapeDtypeStruct((s, D_FF), BF16),
        compiler_params=_params(("parallel", "parallel"), 12 << 20),
    )(up, up, up, up, w_conv, w_conv, b_conv, b_conv)


def _gate_bwd(up, da, w_conv, b_conv):
    s = up.shape[0]
    t = ROW_TILE
    nj = D_FF // CONV_TILE
    hb = t // HALO
    n_i = s // t

    def body(g_ref, v_ref, gh_ref, vh_ref, gn_ref, vn_ref, da_ref, dan_ref, wg_ref, wv_ref, bg_ref, bv_ref,
             dupg_ref, dupv_ref, dbg_ref, dbv_ref, dwg_ref, dwv_ref):
        i = pl.program_id(1)

        @pl.when(i == 0)
        def _():
            for r in (dbg_ref, dbv_ref, dwg_ref, dwv_ref):
                r[...] = jnp.zeros_like(r)

        def d_gate(ug, uv, da_v):
            sg = _sigmoid(ug)
            return da_v * uv * (sg * (1.0 + ug * (1.0 - sg))), da_v * (ug * sg)

        live = (i > 0).astype(F32)
        xg, xv = g_ref[...], v_ref[...]
        wg, wv = wg_ref[...], wv_ref[...]
        ug, g1, g2 = _conv_fwd(xg, gh_ref[...] * live, wg, bg_ref[...])
        uv, v1, v2 = _conv_fwd(xv, vh_ref[...] * live, wv, bv_ref[...])
        dug, duv = d_gate(ug, uv, da_ref[...])

        more = (i < n_i - 1).astype(F32)
        ug_n, _, _ = _conv_fwd(gn_ref[...], xg[t - HALO:, :], wg, bg_ref[...])
        uv_n, _, _ = _conv_fwd(vn_ref[...], xv[t - HALO:, :], wv, bv_ref[...])
        dug_n, duv_n = d_gate(ug_n, uv_n, dan_ref[...] * more)

        def conv_t(du, du_n, w):
            return du * w[2:3, :] + _shift_up(du, du_n, 1) * w[1:2, :] + _shift_up(du, du_n, 2) * w[0:1, :]

        dupg_ref[...] = conv_t(dug, dug_n, wg).astype(BF16)
        dupv_ref[...] = conv_t(duv, duv_n, wv).astype(BF16)
        csum = lambda z: jnp.sum(z, axis=0, keepdims=True)
        dbg_ref[...] += csum(dug)
        dbv_ref[...] += csum(duv)
        dwg_ref[0:1, :] += csum(dug * g2)
        dwg_ref[1:2, :] += csum(dug * g1)
        dwg_ref[2:3, :] += csum(dug * xg)
        dwv_ref[0:1, :] += csum(duv * v2)
        dwv_ref[1:2, :] += csum(duv * v1)
        dwv_ref[2:3, :] += csum(duv * xv)

    last_halo = s // HALO - 1
    main = lambda off: pl.BlockSpec((t, CONV_TILE), lambda j, i: (i, j + off))
    halo = lambda off: pl.BlockSpec((HALO, CONV_TILE), lambda j, i: (jnp.maximum(i * hb - 1, 0), j + off))
    nxt = lambda off: pl.BlockSpec((HALO, CONV_TILE), lambda j, i: (jnp.minimum((i + 1) * hb, last_halo), j + off))
    wsp = lambda off: pl.BlockSpec((3, CONV_TILE), lambda j, i: (0, j + off))
    bsp = lambda off: pl.BlockSpec((1, CONV_TILE), lambda j, i: (0, j + off))
    outs = pl.pallas_call(
        body, name="gate_bwd", grid=(nj, n_i),
        in_specs=[main(0), main(nj), halo(0), halo(nj), nxt(0), nxt(nj), main(0), nxt(0),
                  wsp(0), wsp(nj), bsp(0), bsp(nj)],
        out_specs=[main(0), main(0),
                   pl.BlockSpec((1, CONV_TILE), lambda j, i: (0, j)), pl.BlockSpec((1, CONV_TILE), lambda j, i: (0, j)),
                   pl.BlockSpec((3, CONV_TILE), lambda j, i: (0, j)), pl.BlockSpec((3, CONV_TILE), lambda j, i: (0, j))],
        out_shape=[jax.ShapeDtypeStruct((s, D_FF), BF16), jax.ShapeDtypeStruct((s, D_FF), BF16),
                   jax.ShapeDtypeStruct((1, D_FF), F32), jax.ShapeDtypeStruct((1, D_FF), F32),
                   jax.ShapeDtypeStruct((3, D_FF), F32), jax.ShapeDtypeStruct((3, D_FF), F32)],
        compiler_params=_params(("parallel", "arbitrary"), 24 << 20),
    )(up, up, up, up, up, up, da, da, w_conv, w_conv, b_conv, b_conv)
    return outs


def _down_final(act, w_down, x1, tgt, g2):
    s, d = x1.shape
    k = act.shape[1]
    n_steps = s // NORM_TILE

    def body(a_ref, w_ref, x1_ref, t_ref, g2_ref, dy_ref, df_ref, dg2_ref, loss_ref, lacc_ref):
        i = pl.program_id(0)

        @pl.when(i == 0)
        def _():
            dg2_ref[...] = jnp.zeros_like(dg2_ref)
            lacc_ref[...] = jnp.zeros_like(lacc_ref)

        f = jnp.dot(a_ref[...], w_ref[...], preferred_element_type=F32)
        e = x1_ref[...] + g2_ref[...] * f - t_ref[...]
        dy = e * (1.0 / d)
        dy_ref[...] = dy
        df_ref[...] = (dy * g2_ref[...]).astype(BF16)
        dg2_ref[...] += jnp.sum(dy * f, axis=0, keepdims=True)
        lacc_ref[...] += jnp.sum(e * e, axis=0, keepdims=True)

        @pl.when(i == n_steps - 1)
        def _():
            loss_ref[...] = jnp.sum(lacc_ref[...], axis=1, keepdims=True) * (0.5 / d)

    row = pl.BlockSpec((NORM_TILE, d), lambda i: (i, 0))
    est = (_nbytes((NORM_TILE, k), BF16) + _nbytes((k, d), BF16) + 4 * _nbytes((NORM_TILE, d), F32))
    return pl.pallas_call(
        body, name="mm_down_final", grid=(n_steps,),
        in_specs=[pl.BlockSpec((NORM_TILE, k), lambda i: (i, 0)), _full((k, d)), row, row, _full((1, d))],
        out_specs=[row, row, _full((1, d)), _full((1, 1))],
        out_shape=[jax.ShapeDtypeStruct((s, d), F32), jax.ShapeDtypeStruct((s, d), BF16),
                   jax.ShapeDtypeStruct((1, d), F32), jax.ShapeDtypeStruct((1, 1), F32)],
        scratch_shapes=[pltpu.VMEM((1, d), F32)],
        compiler_params=_params(("arbitrary",), est),
    )(act, w_down, x1, tgt, g2)


def _ffnnorm_bwd(dh2, x1, dy, mix, gain, scale, g1):
    s, d = x1.shape
    n_steps = s // NORM_TILE

    def body(dh_ref, x_ref, dy_ref, mix_ref, g_ref, sc_ref, g1_ref, dx_ref, dm_ref, acc_ref):
        i = pl.program_id(0)

        @pl.when(i == 0)
        def _():
            acc_ref[...] = jnp.zeros_like(acc_ref)

        dh, x = dh_ref[...], x_ref[...]
        r = _rms(x)
        xn = x * r
        dn = dh * (1.0 + sc_ref[...])
        dxn = dn * g_ref[...]
        dx = dy_ref[...] + r * (dxn - xn * jnp.mean(dxn * xn, axis=-1, keepdims=True))
        dx_ref[...] = dx
        dm_ref[...] = (dx * g1_ref[...]).astype(BF16)
        csum = lambda z: jnp.sum(z, axis=0, keepdims=True)
        acc_ref[0:1, :] += csum(dh)
        acc_ref[1:2, :] += csum(dh * (xn * g_ref[...]))
        acc_ref[2:3, :] += csum(dn * xn)
        acc_ref[3:4, :] += csum(dx * mix_ref[...])

    row = pl.BlockSpec((NORM_TILE, d), lambda i: (i, 0))
    vec = _full((1, d))
    return pl.pallas_call(
        body, name="ffnnorm_bwd", grid=(n_steps,),
        in_specs=[row, row, row, row, vec, vec, vec],
        out_specs=[row, row, _full((8, d))],
        out_shape=[jax.ShapeDtypeStruct((s, d), F32), jax.ShapeDtypeStruct((s, d), BF16), jax.ShapeDtypeStruct((8, d), F32)],
        compiler_params=_params(("arbitrary",)),
    )(dh2, x1, dy, mix, gain, scale, g1)


def _mixnorm_bwd(dh, x, dx1, gain, scale):
    s, d = x.shape
    n_steps = s // NORM_TILE

    def body(dh_ref, x_ref, dx1_ref, g_ref, sc_ref, gx_ref, acc_ref):
        i = pl.program_id(0)

        @pl.when(i == 0)
        def _():
            acc_ref[...] = jnp.zeros_like(acc_ref)

        dh, x = dh_ref[...], x_ref[...]
        r = _rms(x)
        xn = x * r
        dn = dh * (1.0 + sc_ref[...])
        dxn = dn * g_ref[...]
        gx_ref[...] = dx1_ref[...] + r * (dxn - xn * jnp.mean(dxn * xn, axis=-1, keepdims=True))
        csum = lambda z: jnp.sum(z, axis=0, keepdims=True)
        acc_ref[0:1, :] += csum(dh)
        acc_ref[1:2, :] += csum(dh * (xn * g_ref[...]))
        acc_ref[2:3, :] += csum(dn * xn)

    row = pl.BlockSpec((NORM_TILE, d), lambda i: (i, 0))
    vec = _full((1, d))
    return pl.pallas_call(
        body, name="mixnorm_bwd", grid=(n_steps,),
        in_specs=[row, row, row, vec, vec],
        out_specs=[row, _full((8, d))],
        out_shape=[jax.ShapeDtypeStruct((s, d), F32), jax.ShapeDtypeStruct((8, d), F32)],
        compiler_params=_params(("arbitrary",)),
    )(dh, x, dx1, gain, scale)


def _key_count(d, dilated):
    if not dilated:
        return jnp.where(d >= 0, 1.0, 0.0)
    one = lambda cond: jnp.where(cond, 1.0, 0.0)
    cnt = one(d <= 128) + one(((d & 3) == 0) & (d <= 512)) + one((d & 15) == 0)
    return jnp.where(d >= 0, cnt, 0.0)


def _block_kinds(mla):
    return (0, "diag", "none") if mla else (NEAR_REACH, "near", "far")


NEAR_REACH = 512


def _near_offsets(tk, tq):
    return (NEAR_REACH - (tk - tq)) // tk + 1


def _scores_t(ka, qa, scale, kind, rel_t, offset, near_tabs=None):
    return _mask_scores(lax.dot_general(ka, qa, NT, preferred_element_type=F32), scale, kind, rel_t, offset, near_tabs)


def _fill_near_tables(bias_ref, cnt_ref, rel_t):
    tk, tq = rel_t.shape
    for idx in range(_near_offsets(tk, tq)):
        cnt = _key_count(rel_t + (tk - tq) + idx * tk, True)
        cnt_ref[idx] = cnt
        bias_ref[idx] = jnp.where(cnt > 0.0, 0.0, NEG_INF)


def _mask_scores(products, scale, kind, rel_t, offset, near_tabs=None):
    st = products * (scale * LOG2E)
    cnt = None
    if kind == "diag":
        st = jnp.where(rel_t + offset >= 0, st, NEG_INF)
    elif kind == "far":
        st = jnp.where((rel_t & 15) == 0, st, NEG_INF)
    elif kind == "near":
        bias_ref, cnt_ref = near_tabs
        tk, tq = rel_t.shape
        idx = (offset - (tk - tq)) // tk
        st = st + bias_ref[idx]
        cnt = cnt_ref[idx]
    return st, cnt


def _attn_fwd(q, k, v, mla, scale, name, gather=()):
    s = q.shape[0]
    qw = 2 * LANE if mla else LANE
    tq, tk = ATT_TQ, ATT_TK
    reach, kind_near, kind_far = _block_kinds(mla)
    assert s % tq == 0 and tq % tk == 0 and reach % tk == 0 and reach in (0, NEAR_REACH)
    ng = len(gather)
    last_step = HEADS // 2 - 1

    def body(*refs):
        q_ref, k_ref, v_ref = refs[:3]
        o_ref, lse_ref = refs[3 + ng:5 + ng]
        vt_ref, st_ref = refs[5 + 2 * ng:7 + 2 * ng]
        near_tabs = None if mla else refs[7 + 2 * ng:9 + 2 * ng]
        n_tabs = 0 if mla else 2
        comm = (refs[3:3 + ng], refs[5 + ng:5 + 2 * ng]) + tuple(refs[7 + n_tabs + 2 * ng:])
        if ng:
            @pl.when(pl.program_id(0) == 0)
            def _():
                _Gather(*comm).start()

            @pl.when(pl.program_id(0) == last_step)
            def _():
                _Gather(*comm).forward()

        lane = lax.broadcasted_iota(I32, (1, LANE), 1)
        rel_t = lax.broadcasted_iota(I32, (tk, tq), 1) - lax.broadcasted_iota(I32, (tk, tq), 0)
        if not mla:
            _fill_near_tables(*near_tabs, rel_t)

        def transpose_v(j, carry):
            c0 = pl.multiple_of(j * tk, tk)
            vt_ref[:, pl.ds(c0, tk)] = v_ref[pl.ds(c0, tk), :].astype(F32).T.astype(BF16)
            return carry

        lax.fori_loop(0, s // tk, transpose_v, 0)

        def q_block(qi, carry):
            r0 = pl.multiple_of(qi * tq, tq)
            kcols = [slice(a * LANE, (a + 1) * LANE) if mla else slice(0, LANE) for a in range(2)]
            qas = [q_ref[pl.ds(r0, tq), kcols[a]] for a in range(2)]
            if not mla:
                qas = [jnp.where(lane < DIL_DIM, qas[0], jnp.zeros_like(qas[0])),
                       jnp.where(lane >= DIL_DIM, qas[1], jnp.zeros_like(qas[1]))]

            n_k = (r0 + tq) // tk

            def products(kj):
                c0 = pl.multiple_of(kj * tk, tk)
                return [lax.dot_general(k_ref[pl.ds(c0, tk), kcols[a]], qas[a], NT, preferred_element_type=F32)
                        for a in range(2)]

            for a, pr in enumerate(products(0)):
                st_ref[0, a] = pr

            def k_block(kj, c, kind):
                c0 = pl.multiple_of(kj * tk, tk)
                slot = kj & 1
                ahead = products(jnp.minimum(kj + 1, n_k - 1))
                out = []
                for a in range(2):
                    m, l, acc = c[a]
                    st, cnt = _mask_scores(st_ref[slot, a], scale, kind, rel_t, r0 - c0, near_tabs)
                    st_ref[1 - slot, a] = ahead[a]
                    m_new = jnp.maximum(m, jnp.max(st, axis=0, keepdims=True))
                    alpha = jnp.exp2(m - m_new)
                    p = jnp.exp2(st - m_new)
                    if cnt is not None:
                        p = p * cnt
                    l = alpha * l + jnp.sum(p, axis=0, keepdims=True)
                    vt = vt_ref[a * DIL_DIM:(a + 1) * DIL_DIM, pl.ds(c0, tk)]
                    acc = alpha * acc + jnp.dot(vt, p.astype(BF16), preferred_element_type=F32)
                    out.append((m_new, l, acc))
                return tuple(out)

            one = (jnp.full((1, tq), NEG_INF, F32), jnp.zeros((1, tq), F32), jnp.zeros((DIL_DIM, tq), F32))
            first_near = jnp.maximum((r0 - reach) // tk, 0)
            c = lax.fori_loop(0, first_near, functools.partial(k_block, kind=kind_far), (one, one))
            res = lax.fori_loop(first_near, (r0 + tq) // tk, functools.partial(k_block, kind=kind_near), c)
            o_t = jnp.concatenate([res[a][2] / res[a][1] for a in range(2)], axis=0)
            o_ref[pl.ds(r0, tq), :] = o_t.T.astype(BF16)
            for a in range(2):
                lse_ref[a, :, pl.ds(r0, tq)] = res[a][0] * LN2 + jnp.log(res[a][1])
            return carry

        lax.fori_loop(0, s // tq, q_block, 0)

        if ng:
            @pl.when(pl.program_id(0) == last_step)
            def _():
                _Gather(*comm).finish()

    return pl.pallas_call(
        body, name=name, grid=(HEADS // 2,),
        in_specs=[pl.BlockSpec((s, qw), lambda h: (0, h)), pl.BlockSpec((s, qw), lambda h: (0, h)),
                  pl.BlockSpec((s, LANE), lambda h: (0, h))] + [ANY] * ng,
        out_specs=[pl.BlockSpec((s, LANE), lambda h: (0, h)), pl.BlockSpec((2, 1, s), lambda h: (h, 0, 0))] + [ANY] * ng,
        out_shape=[jax.ShapeDtypeStruct((s, DIL_W), BF16), jax.ShapeDtypeStruct((HEADS, 1, s), F32)] + _Gather.out_shapes(gather),
        scratch_shapes=[pltpu.VMEM((LANE, s), BF16), pltpu.VMEM((2, 2, tk, tq), F32)]
        + ([] if mla else [pltpu.VMEM((_near_offsets(tk, tq), tk, tq), F32)] * 2) + (_Gather.scratch(gather) if ng else []),
        compiler_params=_params(("arbitrary",) if ng else ("parallel",), 12 << 20),
    )(*_in_hbm(q, k, v), *gather)


def _attn_bwd(q, k, v, o, do, do_block0, lse, mla, scale, name, scatter=()):
    s = q.shape[0]
    qw = 2 * LANE if mla else LANE
    tq, tk = ATT_TQ, ATT_TK_BWD
    nq = s // tq
    reach, kind_near, kind_far = _block_kinds(mla)
    assert s % tq == 0 and s % tk == 0
    ns = len(scatter)
    last_step = HEADS // 2 - 1

    def body(*refs):
        q_ref, k_ref, v_ref, o_ref, do_ref, lse_ref = refs[:6]
        dq_ref, dk_ref, dv_ref = refs[6 + ns:9 + ns]
        kt_ref, dot_ref, dob_ref, dqt_ref, delta_ref, lse2_ref = refs[9 + 2 * ns:15 + 2 * ns]
        near_tabs = None if mla else refs[15 + 2 * ns:17 + 2 * ns]
        n_tabs = 0 if mla else 2
        comm = (refs[6:6 + ns], refs[9 + ns:9 + 2 * ns]) + tuple(refs[15 + n_tabs + 2 * ns:])
        if ns:
            @pl.when(pl.program_id(0) == 0)
            def _():
                _Scatter(*comm).start()

        lane = lax.broadcasted_iota(I32, (1, LANE), 1)
        row = lax.broadcasted_iota(I32, (LANE, 1), 0)
        rel_t = lax.broadcasted_iota(I32, (tk, tq), 1) - lax.broadcasted_iota(I32, (tk, tq), 0)
        if not mla:
            _fill_near_tables(*near_tabs, rel_t)

        def prepare(j, carry):
            c0 = pl.multiple_of(j * tk, tk)
            do_blk = do_ref[pl.ds(c0, tk), :]
            dob_ref[pl.ds(c0, tk), :] = do_blk.astype(BF16)
            do_t = do_blk.T
            dot_ref[:, pl.ds(c0, tk)] = do_t.astype(BF16)
            prod = do_t * o_ref[pl.ds(c0, tk), :].astype(F32).T
            delta_ref[0, :, pl.ds(c0, tk)] = jnp.sum(prod[0:DIL_DIM], axis=0, keepdims=True)
            delta_ref[1, :, pl.ds(c0, tk)] = jnp.sum(prod[DIL_DIM:LANE], axis=0, keepdims=True)
            for w in range(qw // LANE):
                kt_ref[w * LANE:(w + 1) * LANE, pl.ds(c0, tk)] = (
                    k_ref[pl.ds(c0, tk), w * LANE:(w + 1) * LANE].astype(F32).T.astype(BF16))
            return carry

        lax.fori_loop(0, s // tk, prepare, 0)
        dqt_ref[...] = jnp.zeros_like(dqt_ref)
        lse2_ref[...] = lse_ref[...] * LOG2E

        sels = [lane < DIL_DIM, lane >= DIL_DIM]
        rsels = [row < DIL_DIM, row >= DIL_DIM]
        cols = [slice(a * LANE, (a + 1) * LANE) if mla else slice(0, LANE) for a in range(2)]

        def k_block(kj, carry):
            c0 = pl.multiple_of(kj * tk, tk)
            kas = [k_ref[pl.ds(c0, tk), cols[a]] for a in range(2)]
            kts = [kt_ref[cols[a], pl.ds(c0, tk)] for a in range(2)]
            if not mla:
                kas = [jnp.where(sels[a], kas[a], jnp.zeros_like(kas[a])) for a in range(2)]
                kts = [jnp.where(rsels[a], kts[a], jnp.zeros_like(kts[a])) for a in range(2)]
            vb = v_ref[pl.ds(c0, tk), :]
            vbs = [jnp.where(sels[a], vb, jnp.zeros_like(vb)) for a in range(2)]

            first = c0 // tq

            def q_block(qi, c, kind):
                r0 = pl.multiple_of(qi * tq, tq)
                out, dq_parts = [], []
                for a in range(2):
                    dk_acc, dv_acc = c[a]
                    qa = q_ref[pl.ds(r0, tq), cols[a]]
                    st, cnt = _scores_t(kas[a], qa, scale, kind, rel_t, r0 - c0, near_tabs)
                    p = jnp.exp2(st - lse2_ref[a, :, pl.ds(r0, tq)])
                    if cnt is not None:
                        p = p * cnt
                    dp = jnp.dot(vbs[a], dot_ref[:, pl.ds(r0, tq)], preferred_element_type=F32)
                    ds = (p * (dp - delta_ref[a, :, pl.ds(r0, tq)]) * scale).astype(BF16)
                    dv_acc = dv_acc + jnp.dot(p.astype(BF16), dob_ref[pl.ds(r0, tq), :], preferred_element_type=F32)
                    dk_acc = dk_acc + jnp.dot(ds, qa, preferred_element_type=F32)
                    dq_parts.append(jnp.dot(kts[a], ds, preferred_element_type=F32))
                    out.append((dk_acc, dv_acc))
                if mla:
                    for a in range(2):
                        dqt_ref[cols[a], pl.ds(r0, tq)] += dq_parts[a]
                else:
                    dqt_ref[:, pl.ds(r0, tq)] += dq_parts[0] + dq_parts[1]
                return tuple(out)

            zero = jnp.zeros((tk, LANE), F32)
            last_near = jnp.minimum((c0 + tk - 1 + reach) // tq + 1, nq)
            c = lax.fori_loop(first, last_near, functools.partial(q_block, kind=kind_near), ((zero, zero), (zero, zero)))
            (dk0, dv0), (dk1, dv1) = lax.fori_loop(last_near, nq, functools.partial(q_block, kind=kind_far), c)
            if mla:
                dk_ref[pl.ds(c0, tk), cols[0]] = dk0
                dk_ref[pl.ds(c0, tk), cols[1]] = dk1
            else:
                dk_ref[pl.ds(c0, tk), :] = jnp.where(sels[0], dk0, dk1)
            dv_ref[pl.ds(c0, tk), :] = jnp.where(sels[0], dv0, dv1)
            return carry

        lax.fori_loop(0, s // tk, k_block, 0)

        def write_dq(j, carry):
            c0 = pl.multiple_of(j * tk, tk)
            for w in range(qw // LANE):
                dq_ref[pl.ds(c0, tk), w * LANE:(w + 1) * LANE] = dqt_ref[w * LANE:(w + 1) * LANE, pl.ds(c0, tk)].T
            return carry

        lax.fori_loop(0, s // tk, write_dq, 0)

        if ns:
            @pl.when(pl.program_id(0) == last_step)
            def _():
                _Scatter(*comm).finish()

    b0 = do_block0
    return pl.pallas_call(
        body, name=name, grid=(HEADS // 2,),
        in_specs=[pl.BlockSpec((s, qw), lambda h: (0, h)), pl.BlockSpec((s, qw), lambda h: (0, h)),
                  pl.BlockSpec((s, LANE), lambda h: (0, h)), pl.BlockSpec((s, LANE), lambda h: (0, h)),
                  pl.BlockSpec((s, LANE), lambda h: (0, h + b0)), pl.BlockSpec((2, 1, s), lambda h: (h, 0, 0))] + [ANY] * ns,
        out_specs=[pl.BlockSpec((s, qw), lambda h: (0, h)), pl.BlockSpec((s, qw), lambda h: (0, h)),
                   pl.BlockSpec((s, LANE), lambda h: (0, h))] + [ANY] * ns,
        out_shape=[jax.ShapeDtypeStruct(q.shape, F32), jax.ShapeDtypeStruct(k.shape, F32), jax.ShapeDtypeStruct((s, DIL_W), F32)]
        + _Scatter.out_shapes(scatter),
        scratch_shapes=[pltpu.VMEM((qw, s), BF16), pltpu.VMEM((LANE, s), BF16), pltpu.VMEM((s, LANE), BF16),
                        pltpu.VMEM((qw, s), F32), pltpu.VMEM((2, 1, s), F32), pltpu.VMEM((2, 1, s), F32)]
        + ([] if mla else [pltpu.VMEM((_near_offsets(tk, tq), tk, tq), F32)] * 2) + (_Scatter.semaphores(ns) if ns else []),
        compiler_params=_params(("arbitrary",) if ns else ("parallel",), 24 << 20),
    )(*_in_hbm(q, k, v, o, do, lse), *scatter)


def _ada_bwd(c_all, dmod_shard):
    n, d = c_all.shape
    cols = dmod_shard.shape[1]

    def body(c_ref, g_ref, o_ref):
        cv = c_ref[...]
        o_ref[...] = lax.dot_general(cv * _sigmoid(cv), g_ref[...], TN, precision=HIGHEST, preferred_element_type=F32)

    return pl.pallas_call(
        body, name="ada_bwd", out_shape=jax.ShapeDtypeStruct((d, cols), F32),
        compiler_params=_params(None, 16 << 20),
    )(c_all, dmod_shard)


SMALL_WIDTHS = (("g_mix_norm", D_MODEL), ("g_q_lat", Q_LORA), ("g_kv_lat", KV_LORA), ("g_mla_q_nope", NOPE),
                ("g_mla_q_pe", ROPE), ("g_mla_k_nope", NOPE), ("g_mla_k_pe", ROPE), ("g_dil_q", DIL_DIM),
                ("g_dil_k", DIL_DIM), ("g_ffn_norm", D_MODEL), ("b_conv", UP_W))


def _small_layout():
    pieces = (("dmod", 6 * D_MODEL),) + SMALL_WIDTHS + tuple(("w_conv%d" % k, UP_W) for k in range(3)) + (("loss", 1),)
    layout, off = {}, 0
    for name, width in pieces:
        layout[name] = (width, off)
        off += -(-width // LANE) * LANE
    return layout, off


def _pack_small(acc1, acc2, dg2, dglat, dgains, dbg, dbv, dwg, dwv, loss_part):
    layout, total = _small_layout()

    def body(a1, a2, g2, gl, gg, bg, bv, wg, wv, ls, o_ref):
        o_ref[...] = jnp.zeros_like(o_ref)

        def put(name, src, shift=0):
            start = layout[name][1] + shift
            o_ref[:, start:start + src.shape[1]] = src

        for k, src in enumerate((a1[0:1, :], a1[1:2, :], a2[3:4, :], a2[0:1, :], a2[1:2, :], g2[...])):
            put("dmod", src, k * D_MODEL)
        put("g_mix_norm", a1[2:3, :])
        put("g_q_lat", gl[0:1, :])
        put("g_kv_lat", gl[1:2, 0:KV_LORA])
        put("g_mla_q_nope", gg[0:1, 0:NOPE])
        put("g_mla_q_pe", gg[5:6, 0:ROPE])
        put("g_mla_k_nope", gg[1:2, 0:NOPE])
        put("g_mla_k_pe", gg[2:3, 0:ROPE])
        put("g_dil_q", gg[3:4, 0:DIL_DIM])
        put("g_dil_k", gg[4:5, 0:DIL_DIM])
        put("g_ffn_norm", a2[2:3, :])
        put("b_conv", bg[...])
        put("b_conv", bv[...], D_FF)
        for k in range(3):
            put("w_conv%d" % k, wg[k:k + 1, :])
            put("w_conv%d" % k, wv[k:k + 1, :], D_FF)
        put("loss", ls[...])

    ins = (acc1, acc2, dg2, dglat, dgains, dbg, dbv, dwg, dwv, loss_part)
    return pl.pallas_call(
        body, name="pack_small", grid=(1,), in_specs=[_full(a.shape) for a in ins], out_specs=_full((1, total)),
        out_shape=jax.ShapeDtypeStruct((1, total), F32),
        compiler_params=_params(("arbitrary",), 2 << 20),
    )(*_in_hbm(*ins))


def _sum_unpack(g):
    n_dev, _, total = g.shape
    layout, _ = _small_layout()

    def body(g_ref, *refs):
        o_refs, s_ref = refs[:-1], refs[-1]
        acc = g_ref[0]
        for k in range(1, n_dev):
            acc = acc + g_ref[k]
        s_ref[...] = acc
        take = lambda name: s_ref[:, layout[name][1]:layout[name][1] + layout[name][0]]
        o_refs[0][...] = take("dmod")
        for i, (name, _) in enumerate(SMALL_WIDTHS):
            o_refs[1 + i][...] = take(name)
        for k in range(3):
            o_refs[-2][k:k + 1, :] = take("w_conv%d" % k)
        o_refs[-1][...] = take("loss")

    shapes = [(1, 6 * D_MODEL)] + [(1, w) for _, w in SMALL_WIDTHS] + [(3, UP_W), (1, 1)]
    return pl.pallas_call(
        body, name="sum_unpack", out_shape=[jax.ShapeDtypeStruct(sh, F32) for sh in shapes],
        scratch_shapes=[pltpu.VMEM((1, total), F32)],
        compiler_params=_params(None, 4 << 20),
    )(g)


def _adamw_math(w, g, m, v):
    mn = ADAM_B1 * m + (1.0 - ADAM_B1) * g
    vn = ADAM_B2 * v + (1.0 - ADAM_B2) * (g * g)
    m_hat = mn / (1.0 - ADAM_B1 ** ADAM_STEP)
    v_hat = vn / (1.0 - ADAM_B2 ** ADAM_STEP)
    return -ADAM_LR * (m_hat / (jnp.sqrt(v_hat) + ADAM_EPS) + ADAM_WD * w), mn, vn


def _adamw_vectors(ws, gs, ms, vs):
    k = len(ws)

    def body(*refs):
        for i in range(k):
            d, mn, vn = _adamw_math(refs[i][...], refs[k + i][...], refs[2 * k + i][...], refs[3 * k + i][...])
            refs[4 * k + i][...] = d
            refs[5 * k + i][...] = mn
            refs[6 * k + i][...] = vn

    blocks = [_full(w.shape) for w in ws]
    outs = pl.pallas_call(
        body, name="adamw_vectors", grid=(1,), in_specs=blocks * 4, out_specs=blocks * 3,
        out_shape=[jax.ShapeDtypeStruct(w.shape, F32) for w in ws] * 3,
        compiler_params=_params(("arbitrary",), 2 << 20),
    )(*_in_hbm(*ws, *gs, *ms, *vs))
    return outs[:k], outs[k:2 * k], outs[2 * k:]


def _adamw(w, g, m, v, name):
    r, c = w.shape
    tr = r
    for cand in (256, 128, 64, 32, 16):
        if r % cand == 0 and r > cand:
            tr = cand
            break

    def body(w_ref, g_ref, m_ref, v_ref, d_ref, mo_ref, vo_ref):
        d_ref[...], mo_ref[...], vo_ref[...] = _adamw_math(w_ref[...], g_ref[...], m_ref[...], v_ref[...])

    blk = pl.BlockSpec((tr, c), lambda i: (i, 0))
    return pl.pallas_call(
        body, name=name, grid=(r // tr,), in_specs=[blk] * 4, out_specs=[blk] * 3,
        out_shape=[jax.ShapeDtypeStruct((r, c), F32)] * 3,
        compiler_params=_params(("parallel",), 7 * _nbytes((tr, c), F32)),
    )(w, g, m, v)


def _position():
    return lax.axis_index("x"), lax.axis_index("y"), lax.axis_index("c")


def _other_chips(x, y):
    return [(1 - x, y, 2 * (1 - x) + y), (x, 1 - y, 2 * x + (1 - y)), (1 - x, 1 - y, 2 * (1 - x) + (1 - y))]


class _SmallGather:
    def __init__(self, v_ref, out_ref, send_sems, recv_sems, local_sem):
        x, y, c = _position()
        me = 4 * x + 2 * y + c
        self.local = pltpu.make_async_copy(v_ref, out_ref.at[me], local_sem)
        self.sends, self.arrivals = [], []
        for k in range(N_DEV - 1):
            fx, fy, fc = ((k + 1) >> 2) & 1, ((k + 1) >> 1) & 1, (k + 1) & 1
            px, py, pc = (1 - x if fx else x), (1 - y if fy else y), (1 - c if fc else c)

            def copy(dst, k=k, peer=(px, py, pc)):
                return pltpu.make_async_remote_copy(src_ref=v_ref, dst_ref=dst, send_sem=send_sems.at[k],
                                                    recv_sem=recv_sems.at[k], device_id=peer, device_id_type=MESH)

            self.sends.append(copy(out_ref.at[me]))
            self.arrivals.append(copy(out_ref.at[4 * px + 2 * py + pc]))

    @staticmethod
    def semaphores():
        return [pltpu.SemaphoreType.DMA((N_DEV - 1,)), pltpu.SemaphoreType.DMA((N_DEV - 1,)), pltpu.SemaphoreType.DMA]

    def start(self):
        self.local.start()
        for cp in self.sends:
            cp.start()

    def finish(self):
        for cp in self.arrivals:
            cp.wait_recv()
        for cp in self.sends:
            cp.wait_send()
        self.local.wait()


def _prologue(c_taps, w_ada_shard, b_shard, pos_col, rope_consts, shards):
    n = len(shards)
    s = pos_col.shape[0]
    cols = w_ada_shard.shape[1]
    freq, csel, ssel = rope_consts

    def body(*refs):
        ct_ref, w_ref, b_ref, p_ref, f_ref, cs_ref, ss_ref = refs[:7]
        sh_refs = refs[7:7 + n]
        ct_all_ref, mod_all_ref, tab_ref = refs[7 + n:10 + n]
        g_refs = refs[10 + n:10 + 2 * n]
        mod_blk_ref = refs[10 + 2 * n]
        sems = refs[11 + 2 * n:]
        first = _SmallGather(ct_ref, ct_all_ref, *sems[0:3])
        first.start()
        first.finish()
        cv = ct_all_ref[:, 0, 0:D_MODEL]
        sc = (cv * _sigmoid(cv)).astype(BF16)
        mod_blk_ref[...] = jnp.dot(sc, w_ref[...].astype(BF16), preferred_element_type=F32) + b_ref[...]
        second = _SmallGather(mod_blk_ref, mod_all_ref, *sems[3:6])
        second.start()
        weights = _Gather(sh_refs, g_refs, *sems[6:])
        weights.start()

        def table_rows(i, carry):
            r0 = pl.multiple_of(i * ROW_TILE, ROW_TILE)
            ang = p_ref[pl.ds(r0, ROW_TILE), :].astype(F32) * f_ref[...]
            tab_ref[pl.ds(r0, ROW_TILE), :] = cs_ref[...] * jnp.cos(ang) + ss_ref[...] * jnp.sin(ang)
            return carry

        lax.fori_loop(0, s // ROW_TILE, table_rows, 0)
        second.finish()
        weights.forward()
        weights.finish()

    return pl.pallas_call(
        body, name="prologue",
        out_shape=[jax.ShapeDtypeStruct((N_DEV,) + c_taps.shape, F32), jax.ShapeDtypeStruct((N_DEV, N_DEV, cols), F32),
                   jax.ShapeDtypeStruct((s, 4 * LANE), F32)] + _Gather.out_shapes(shards),
        in_specs=[IN_VMEM] * 7 + [ANY] * n, out_specs=[IN_VMEM] * 3 + [ANY] * n,
        scratch_shapes=[pltpu.VMEM((N_DEV, cols), F32)] + _SmallGather.semaphores() * 2 + _Gather.scratch(shards),
        compiler_params=_params(None, 14 << 20),
    )(c_taps, w_ada_shard, b_shard, pos_col, freq, csel, ssel, *shards)


IN_VMEM = pl.BlockSpec(memory_space=pltpu.VMEM)
ANY = pl.BlockSpec(memory_space=pl.ANY)


class _Gather:
    def __init__(self, w_refs, out_refs, send_sems, recv_sems, own_sems, *bounce_refs):
        x, y, c = _position()
        q0 = 2 * x + y
        sibling = (x, y, 1 - c)
        self.ici, self.ici_in, self.fwd, self.fwd_in, self.own_in, self.own_out = [], [], [], [], [], []
        for k, (w_ref, out_ref) in enumerate(zip(w_refs, out_refs)):
            half = w_ref.shape[0] // 2
            self.own_in.append(pltpu.make_async_copy(w_ref, bounce_refs[k], own_sems.at[2 * k]))
            self.own_out.append(pltpu.make_async_copy(bounce_refs[k], out_ref.at[q0], own_sems.at[2 * k + 1]))

            def blk(q, e, out_ref=out_ref, half=half):
                return out_ref.at[q, pl.ds(pl.multiple_of(e * half, 16), half), :]

            def copy(src, dst, i, to):
                return pltpu.make_async_remote_copy(src_ref=src, dst_ref=dst, send_sem=send_sems.at[i], recv_sem=recv_sems.at[i],
                                                    device_id=to, device_id_type=MESH)

            src = w_ref.at[pl.ds(pl.multiple_of(c * half, 16), half), :]
            for j, (cx, cy, qj) in enumerate(_other_chips(x, y)):
                self.ici.append(copy(src, blk(q0, c), 6 * k + j, (cx, cy, c)))
                self.ici_in.append(copy(blk(qj, c), blk(qj, c), 6 * k + j, (cx, cy, c)))
                self.fwd.append(copy(blk(qj, c), blk(qj, c), 6 * k + 3 + j, sibling))
                self.fwd_in.append(copy(blk(qj, 1 - c), blk(qj, 1 - c), 6 * k + 3 + j, sibling))

    @staticmethod
    def out_shapes(shards):
        return [jax.ShapeDtypeStruct((N_CHIP,) + s.shape, s.dtype) for s in shards]

    @staticmethod
    def scratch(shards):
        n = len(shards)
        return ([pltpu.SemaphoreType.DMA((6 * n,)), pltpu.SemaphoreType.DMA((6 * n,)), pltpu.SemaphoreType.DMA((2 * n,))]
                + [pltpu.VMEM(s.shape, s.dtype) for s in shards])

    def start(self):
        for cp in self.ici + self.own_in:
            cp.start()

    def forward(self):
        for fetched, placed in zip(self.own_in, self.own_out):
            fetched.wait()
            placed.start()
        for arrived, onward in zip(self.ici_in, self.fwd):
            arrived.wait_recv()
            onward.start()

    def finish(self):
        for cp in self.fwd_in:
            cp.wait_recv()
        for cp in self.ici + self.fwd:
            cp.wait_send()
        for cp in self.own_out:
            cp.wait()


class _PairSwap:
    def __init__(self, g_refs, out_refs, send_sems, recv_sems):
        x, y, c = _position()
        self.copies = [
            pltpu.make_async_remote_copy(src_ref=g_ref.at[:, 1 - c], dst_ref=out_ref, send_sem=send_sems.at[k],
                                         recv_sem=recv_sems.at[k], device_id=(x, y, 1 - c), device_id_type=MESH)
            for k, (g_ref, out_ref) in enumerate(zip(g_refs, out_refs))]

    @staticmethod
    def out_shapes(grads):
        return [jax.ShapeDtypeStruct((N_CHIP,) + g.shape[2:], g.dtype) for g in grads]

    @staticmethod
    def semaphores(n):
        return [pltpu.SemaphoreType.DMA((n,)), pltpu.SemaphoreType.DMA((n,))]

    def start(self):
        for cp in self.copies:
            cp.start()

    def finish(self):
        for cp in self.copies:
            cp.wait_recv()
        for cp in self.copies:
            cp.wait_send()


def _pair_sum(g, a, c_idx, name):
    _, _, rh, cols = g.shape
    tr = rh
    for cand in (256, 128, 64, 32, 16):
        if rh % cand == 0 and rh > cand:
            tr = cand
            break

    def body(c_ref, g_ref, a_ref, o_ref):
        o_ref[...] = (g_ref[...] + a_ref[...]).astype(BF16)

    return pl.pallas_call(
        body, name=name,
        grid_spec=pltpu.PrefetchScalarGridSpec(
            num_scalar_prefetch=1, grid=(N_CHIP, rh // tr),
            in_specs=[pl.BlockSpec((None, None, tr, cols), lambda q, i, c_ref: (q, c_ref[0], i, 0)),
                      pl.BlockSpec((None, tr, cols), lambda q, i, c_ref: (q, i, 0))],
            out_specs=pl.BlockSpec((None, tr, cols), lambda q, i, c_ref: (q, i, 0))),
        out_shape=jax.ShapeDtypeStruct((N_CHIP, rh, cols), BF16),
        compiler_params=_params(("parallel", "parallel"), 10 * _nbytes((tr, cols), F32)),
    )(c_idx, g, a)


def _scatter_and_gather(parts, small, name):
    n = len(parts)

    def body(*refs):
        scatter = _Scatter(refs[:n], refs[n + 1:2 * n + 1], *refs[2 * n + 2:2 * n + 4])
        gather = _SmallGather(refs[n], refs[2 * n + 1], *refs[2 * n + 4:])
        scatter.start()
        gather.start()
        gather.finish()
        scatter.finish()

    return pl.pallas_call(
        body, name=name,
        out_shape=_Scatter.out_shapes(parts) + [jax.ShapeDtypeStruct((N_DEV,) + small.shape, F32)],
        in_specs=[ANY] * n + [IN_VMEM], out_specs=[ANY] * n + [IN_VMEM],
        scratch_shapes=_Scatter.semaphores(n) + _SmallGather.semaphores(),
        compiler_params=_params(None, 10 * _nbytes(small.shape, F32)),
    )(*parts, small)


class _Scatter:
    def __init__(self, p_refs, out_refs, send_sems, recv_sems):
        x, y, c = _position()
        self.copies = []
        for k, (p_ref, out_ref) in enumerate(zip(p_refs, out_refs)):
            for j, (cx, cy, qj) in enumerate(_other_chips(x, y)):
                self.copies.append(pltpu.make_async_remote_copy(
                    src_ref=p_ref.at[qj], dst_ref=out_ref.at[j], send_sem=send_sems.at[3 * k + j],
                    recv_sem=recv_sems.at[3 * k + j], device_id=(cx, cy, c), device_id_type=MESH))

    @staticmethod
    def out_shapes(parts):
        return [jax.ShapeDtypeStruct((3,) + p.shape[1:], p.dtype) for p in parts]

    @staticmethod
    def semaphores(n):
        return [pltpu.SemaphoreType.DMA((3 * n,)), pltpu.SemaphoreType.DMA((3 * n,))]

    def start(self):
        for cp in self.copies:
            cp.start()

    def finish(self):
        for cp in self.copies:
            cp.wait_recv()
        for cp in self.copies:
            cp.wait_send()


def _shard_sum(p, b, qc_idx, name):
    _, rh, cols = p.shape
    tr = rh
    for cand in (256, 128, 64, 32, 16):
        if rh % cand == 0 and rh > cand:
            tr = cand
            break

    def body(qc_ref, p_ref, b_ref, o_ref):
        acc = p_ref[...].astype(F32)
        for j in range(3):
            acc = acc + b_ref[j].astype(F32)
        o_ref[...] = acc

    return pl.pallas_call(
        body, name=name,
        grid_spec=pltpu.PrefetchScalarGridSpec(
            num_scalar_prefetch=1, grid=(rh // tr,),
            in_specs=[pl.BlockSpec((None, tr, cols), lambda i, qc_ref: (qc_ref[0], i, 0)),
                      pl.BlockSpec((3, tr, cols), lambda i, qc_ref: (0, i, 0))],
            out_specs=pl.BlockSpec((None, tr, cols), lambda i, qc_ref: (qc_ref[1], i, 0))),
        out_shape=jax.ShapeDtypeStruct((2, rh, cols), F32),
        compiler_params=_params(("parallel",), 8 * _nbytes((tr, cols), F32)),
    )(qc_idx, p, b)


def _join_halves(shards):
    n = len(shards)

    def body(*refs):
        out_refs = refs[n:2 * n]
        send_sems, recv_sems = refs[2 * n:]
        x, y, c = _position()
        cps = [pltpu.make_async_remote_copy(src_ref=out_refs[k].at[c], dst_ref=out_refs[k].at[c], send_sem=send_sems.at[k],
                                            recv_sem=recv_sems.at[k], device_id=(x, y, 1 - c), device_id_type=MESH)
               for k in range(n)]
        for cp in cps:
            cp.start()
        for k in range(n):
            arriving = out_refs[k].at[1 - c]
            pltpu.make_async_remote_copy(src_ref=arriving, dst_ref=arriving, send_sem=send_sems.at[k], recv_sem=recv_sems.at[k],
                                         device_id=(x, y, 1 - c), device_id_type=MESH).wait_recv()
        for cp in cps:
            cp.wait_send()

    return pl.pallas_call(
        body, name="rs_join",
        out_shape=[jax.ShapeDtypeStruct(a.shape, a.dtype) for a in shards],
        in_specs=[ANY] * n, out_specs=[ANY] * n, input_output_aliases={k: k for k in range(n)},
        scratch_shapes=[pltpu.SemaphoreType.DMA((n,)), pltpu.SemaphoreType.DMA((n,))],
    )(*shards)


def _cols_from_shards(g):
    q, r, cs = g.shape
    return jnp.transpose(g, (1, 0, 2)).reshape(r, q * cs)


def _cols_to_shards(w):
    r, cfull = w.shape
    return jnp.transpose(w.reshape(r, N_CHIP, cfull // N_CHIP), (1, 0, 2))


def _pad_w_in(w):
    z = lambda n: jnp.zeros((w.shape[0], n), w.dtype)
    q_lat, kv_lat, kpe = w[:, 0:512], w[:, 512:768], w[:, 768:800]
    qd, kd, vd = w[:, 800:1312], w[:, 1312:1824], w[:, 1824:2336]
    return jnp.concatenate([q_lat, qd, kd, vd, kv_lat, z(KPE_OFF), kpe, z(LANE - KPE_OFF - ROPE)], axis=1)


def _pad_w_qb(w):
    w3 = w.reshape(Q_LORA, HEADS, NOPE + ROPE)
    return jnp.pad(w3, ((0, 0), (0, 0), (0, LANE - NOPE - ROPE))).reshape(Q_LORA, HEADS * LANE)


def _unpad_w_qb(g):
    return g.reshape(Q_LORA, HEADS, LANE)[:, :, :NOPE + ROPE].reshape(Q_LORA, HEADS * (NOPE + ROPE))


def _pad_w_kvb(w):
    w3 = w.reshape(KV_LORA, HEADS, 2 * NOPE)
    kp = jnp.pad(w3[:, :, :NOPE], ((0, 0), (0, 0), (0, LANE - NOPE))).reshape(KV_LORA, HEADS * LANE)
    return jnp.concatenate([kp, w3[:, :, NOPE:].reshape(KV_LORA, DIL_W)], axis=1)


def _unpad_w_kvb(g):
    gk = g[:, :HEADS * LANE].reshape(KV_LORA, HEADS, LANE)[:, :, :NOPE]
    gv = g[:, HEADS * LANE:].reshape(KV_LORA, HEADS, NOPE)
    return jnp.concatenate([gk, gv], axis=2).reshape(KV_LORA, HEADS * 2 * NOPE)


def _head_gains(g_q_nope, g_q_pe, g_k_nope, g_k_pe, g_dq, g_dk):
    z = lambda n: jnp.zeros((1, n), F32)
    q1 = jnp.concatenate([g_q_nope, g_q_pe, z(LANE - NOPE - ROPE)], axis=1)
    k1 = jnp.concatenate([g_k_nope, z(LANE - NOPE)], axis=1)
    kpe = jnp.concatenate([z(KPE_OFF), g_k_pe, z(LANE - KPE_OFF - ROPE)], axis=1)
    return dict(q=jnp.tile(q1, (1, HEADS)), k=jnp.tile(k1, (1, HEADS)), kpe=kpe,
                dq=jnp.tile(g_dq, (1, HEADS)), dk=jnp.tile(g_dk, (1, HEADS)))


def kernel(x, c, positions, w_ada, b_ada, g_mix_norm, w_in, g_q_lat, w_q_b, g_kv_lat, w_kv_b, g_mla_q_nope, g_mla_q_pe, g_mla_k_nope, g_mla_k_pe, g_dil_q, g_dil_k, w_o, g_ffn_norm, w_up, w_conv, b_conv, w_down, loss_target, m_w_ada, m_b_ada, m_g_mix_norm, m_w_in, m_g_q_lat, m_w_q_b, m_g_kv_lat, m_w_kv_b, m_g_mla_q_nope, m_g_mla_q_pe, m_g_mla_k_nope, m_g_mla_k_pe, m_g_dil_q, m_g_dil_k, m_w_o, m_g_ffn_norm, m_w_up, m_w_conv, m_b_conv, m_w_down, v_w_ada, v_b_ada, v_g_mix_norm, v_w_in, v_g_q_lat, v_w_q_b, v_g_kv_lat, v_w_kv_b, v_g_mla_q_nope, v_g_mla_q_pe, v_g_mla_k_nope, v_g_mla_k_pe, v_g_dil_q, v_g_dil_k, v_w_o, v_g_ffn_norm, v_w_up, v_w_conv, v_b_conv, v_w_down):
    args = dict(locals())
    weights = {n: args[n][0] for n in ("w_ada", "w_in", "w_q_b", "w_kv_b", "w_o", "w_up", "w_conv", "w_down")}
    small_w = {n: args[n] for n in ("b_ada",) + tuple(n for n, _ in SMALL_WIDTHS)}
    mom_m = {n[2:]: (args[n][0] if args[n].ndim == 3 else args[n]) for n in args if n.startswith("m_")}
    mom_v = {n[2:]: (args[n][0] if args[n].ndim == 3 else args[n]) for n in args if n.startswith("v_")}

    xi, yi, ci = _position()
    q0 = 2 * xi + yi
    me = 4 * xi + 2 * yi + ci
    xs, tgt = x[0], loss_target[0]
    s = xs.shape[0]
    consts = _seg_consts()
    c_idx, qc_idx = jnp.reshape(ci, (1,)).astype(I32), jnp.stack([q0, ci]).astype(I32)

    def halves(g4):
        q, r, cc = g4.shape
        return g4.reshape(q, 2, r // 2, cc)

    own_first = [weights[n].astype(BF16) for n in ("w_in", "w_q_b", "w_kv_b")]
    own_later = [weights[n].astype(BF16) for n in ("w_o", "w_up", "w_down")]
    conv_cols = UP_W // N_CHIP
    ada_cols = w_ada.shape[2]
    b_shard = lax.dynamic_slice_in_dim(b_ada, q0 * ada_cols, ada_cols, axis=1)
    c_taps = jnp.concatenate([c, weights["w_conv"].reshape(1, 3 * conv_cols)], axis=1)
    c_taps_all, mod_all, tab, *gathered = _prologue(c_taps, weights["w_ada"], b_shard, positions.reshape(s, 1),
                                                    _rope_consts(), own_first)
    c_all = c_taps_all[:, 0, :D_MODEL]
    w_conv_f = c_taps_all[:, 0, D_MODEL:].reshape(N_CHIP, 2, 3, conv_cols)[:, 0]
    w_conv_f = jnp.transpose(w_conv_f, (1, 0, 2)).reshape(3, UP_W)
    mod_all = mod_all.reshape(N_CHIP, 2, N_DEV, ada_cols)
    mod = lax.dynamic_index_in_dim(lax.dynamic_index_in_dim(mod_all, ci, 1, False), me, 1, False)
    mod = mod.reshape(1, N_CHIP * ada_cols)
    sh1, sc1, g1, sh2, sc2, g2 = [mod[:, k * D_MODEL:(k + 1) * D_MODEL] for k in range(6)]
    w_in_f = _cols_from_shards(gathered[0])
    w_in_p = _pad_w_in(w_in_f)
    w_qb_p = _pad_w_qb(_cols_from_shards(gathered[1]))
    w_kvb_p = _pad_w_kvb(_cols_from_shards(gathered[2]))
    gains = _head_gains(g_mla_q_nope, g_mla_q_pe, g_mla_k_nope, g_mla_k_pe, g_dil_q, g_dil_k)

    h, proj, ql, kvl = _in_proj(xs, g_mix_norm, sc1, sh1, w_in_p, g_q_lat, g_kv_lat)
    q_raw = _mm(ql, w_qb_p, "nn", F32, 1024, HEADS * LANE, "mm_qb")
    kv_raw = _mm(kvl, w_kvb_p, "nn", F32, 1024, HEADS * LANE + DIL_W, "mm_kvb")
    qm, km, vm, qd, kd, vd = _attn_prep(q_raw, kv_raw, proj, tab, gains, consts)
    scale_m, scale_d = (NOPE + ROPE) ** -0.5, DIL_DIM ** -0.5
    o_m, lse_m, got_up = _attn_fwd(qm, km, vm, True, scale_m, "attn_mla", gather=own_later[1:2])
    o_d, lse_d, got_o, got_down = _attn_fwd(qd, kd, vd, False, scale_d, "attn_dil", gather=[own_later[0], own_later[2]])
    gathered = [got_o, got_up, got_down]
    w_o_f = gathered[0].reshape(D_MODEL, D_MODEL)
    w_up_f = _cols_from_shards(gathered[1])
    w_down_f = gathered[2].reshape(D_FF, D_MODEL)
    mix_in = jnp.concatenate([o_m, o_d], axis=1)
    mix, x1, h2 = _o_resid_prenorm(mix_in, w_o_f, xs, g1, g_ffn_norm, sc2, sh2)
    up = _mm(h2, w_up_f, "nn", F32, 1024, CONV_TILE, "mm_up")
    act = _conv_gate(up, w_conv_f, b_conv)
    dy, dffn, dg2, loss_part = _down_final(act, w_down_f, x1, tgt, g2)

    da = _mm(dffn, w_down_f, "nt", F32, 1024, CONV_TILE, "mm_down_dx")
    gw_down = _mm(act, dffn, "tn", F32, 256, D_MODEL, "mm_down_dw")
    dup_g, dup_v, dbg, dbv, dwg, dwv = _gate_bwd(up, da, w_conv_f, b_conv)
    dup = jnp.concatenate([dup_g, dup_v], axis=1)
    early_names = ("w_up", "w_down", "w_o")
    gw_up = _mm(h2, dup, "tn", F32, 1024, CONV_TILE, "mm_up_dw", col_shards=True)
    early = [halves(gw_up), halves(gw_down.reshape(N_CHIP, D_FF // N_CHIP, D_MODEL))]
    dh2, *early_sib = _mm(dup, w_up_f, "nt", F32, 256, 512, "mm_up_dx", swap=early, b_outer=True)
    dx1, dmix, acc2 = _ffnnorm_bwd(dh2, x1, dy, mix, g_ffn_norm, sc2, g1)
    gw_o = _mm(mix_in, dmix, "tn", F32, 1024, D_MODEL, "mm_o_dw")
    early.append(halves(gw_o.reshape(N_CHIP, D_MODEL // N_CHIP, D_MODEL)))
    dmix_in, sib_o = _mm(dmix, w_o_f, "nt", F32, 512, D_MODEL, "mm_o_dx", swap=early[2:])
    early_sib.append(sib_o)
    early_sums = [_pair_sum(g, a, c_idx, "pair_sum_" + n) for g, a, n in zip(early, early_sib, early_names)]
    dqm, dkm, dvm, *early_recv = _attn_bwd(qm, km, vm, o_m, dmix_in, 0, lse_m, True, scale_m, "attn_mla_bwd",
                                           scatter=early_sums[:1])
    dqd, dkd, dvd, *early_recv_d = _attn_bwd(qd, kd, vd, o_d, dmix_in, DIL_W // LANE, lse_d, False, scale_d,
                                             "attn_dil_bwd", scatter=early_sums[1:])
    early_recv = early_recv + early_recv_d
    dq_raw, dkv_raw, dkpe_b, dqd_b, dkd_b, dvd_b, dgains = _attn_prep_bwd(
        dqm, dkm, dvm, dqd, dkd, dvd, q_raw, kv_raw, proj, tab, gains, consts)
    gw_qb = _unpad_w_qb(_mm(ql, dq_raw, "tn", F32, Q_LORA, HEADS * LANE, "mm_qb_dw"))
    gw_kvb = _unpad_w_kvb(_mm(kvl, dkv_raw, "tn", F32, KV_LORA, HEADS * LANE + DIL_W, "mm_kvb_dw"))
    dqlat_b, dkvlat_b, dglat = _latnorm_bwd(dq_raw, dkv_raw, w_qb_p, w_kvb_p, proj, g_q_lat, g_kv_lat)
    dproj = jnp.concatenate([dqlat_b, dkvlat_b, dkpe_b[:, KPE_OFF:KPE_OFF + ROPE], dqd_b, dkd_b, dvd_b], axis=1)
    gw_in = _mm(h, dproj, "tn", F32, 512, IN_COLS, "mm_in_dw")
    late_names = ("w_in", "w_q_b", "w_kv_b")
    late = [halves(_cols_to_shards(gw_in)), halves(_cols_to_shards(gw_qb)), halves(_cols_to_shards(gw_kvb))]
    dh, *late_sib = _mm(dproj, w_in_f, "nt", F32, 512, D_MODEL, "mm_in_dx", swap=late)
    grad_x, acc1 = _mixnorm_bwd(dh, xs, dx1, g_mix_norm, sc1)

    packed = _pack_small(acc1, acc2, dg2, dglat, dgains, dbg, dbv, dwg, dwv, loss_part)
    late_sums = [_pair_sum(g, a, c_idx, "pair_sum_" + n) for g, a, n in zip(late, late_sib, late_names)]
    *late_recv, gathered_small = _scatter_and_gather(late_sums, packed, "rs_scatter_late")

    grad_b_ada, *small_grads, gconv_full, loss_sum = _sum_unpack(gathered_small)
    grads = {"b_ada": grad_b_ada}
    grads.update({n: g for (n, _), g in zip(SMALL_WIDTHS, small_grads)})
    shard_cols = UP_W // N_CHIP
    grads["w_conv"] = lax.dynamic_slice_in_dim(gconv_full, q0 * shard_cols, shard_cols, axis=1)
    dmod_all = gathered_small[:, 0, :6 * D_MODEL]
    grads["w_ada"] = _ada_bwd(c_all, lax.dynamic_slice_in_dim(dmod_all, q0 * ada_cols, ada_cols, axis=1))

    big_names = late_names + early_names
    half_sums = [_shard_sum(p, b, qc_idx, "shard_sum_" + n)
                 for p, b, n in zip(late_sums + early_sums, list(late_recv) + list(early_recv), big_names)]
    for n, full in zip(big_names, _join_halves(half_sums)):
        grads[n] = full.reshape(2 * full.shape[1], full.shape[2])

    delta, new_m, new_v = {}, {}, {}
    for n in ("w_ada", "w_in", "w_q_b", "w_kv_b", "w_o", "w_up", "w_conv", "w_down"):
        operands = (weights[n], grads[n], mom_m[n], mom_v[n])
        flipped = n in ("w_in", "w_q_b")
        if flipped:
            operands = [jnp.swapaxes(a, 0, 1) for a in operands]
            grads[n] = jnp.swapaxes(operands[1], 0, 1)
        if n == "w_ada":
            operands = _in_hbm(*operands)
        delta[n], new_m[n], new_v[n] = _adamw(*operands, "adamw_" + n)
        if flipped:
            delta[n], new_m[n], new_v[n] = (jnp.swapaxes(a, 0, 1) for a in (delta[n], new_m[n], new_v[n]))
    vec_names = ("b_ada",) + tuple(n for n, _ in SMALL_WIDTHS)
    sd, sm, sv = _adamw_vectors(*[[d_[n] for n in vec_names] for d_ in (small_w, grads, mom_m, mom_v)])
    for k, n in enumerate(vec_names):
        delta[n], new_m[n], new_v[n] = sd[k], sm[k], sv[k]

    loss = loss_sum[0, 0]
    order = ("w_ada", "b_ada", "g_mix_norm", "w_in", "g_q_lat", "w_q_b", "g_kv_lat", "w_kv_b", "g_mla_q_nope", "g_mla_q_pe",
             "g_mla_k_nope", "g_mla_k_pe", "g_dil_q", "g_dil_k", "w_o", "g_ffn_norm", "w_up", "w_conv", "b_conv", "w_down")
    lead = lambda n, z: z[None] if n.startswith("w_") else z
    outs = [loss, grad_x[None]]
    for d_ in (grads, delta, new_m, new_v):
        outs += [lead(n, d_[n]) for n in order]
    return tuple(outs)
```
